```python
import math
import jax, jax.numpy as jnp
from jax import lax
import numpy as np

D_MODEL = 1024
BATCH = 8
SEQ = 2048
DEPTH = 1

HEAD_DIM = 64
DIL_CONFIGS = ((128, 1), (512, 4), (2048, 16))
N_DIL_GROUPS = len(DIL_CONFIGS)
DIL_HEADS_PER_GROUP = 4
N_DIL_HEADS = N_DIL_GROUPS * DIL_HEADS_PER_GROUP
N_FOX_HEADS = 8
BLOCK = 128
ROPE_THETA = 500000.0
ROPE_DIM = HEAD_DIM // 4
D_FF = -(-(8 * D_MODEL) // (3 * 256)) * 256
EPS = 1e-6
NEG_INF = -1e30

DIL_WIDTH = N_DIL_HEADS * HEAD_DIM
DIL_OUT_WIDTH = DIL_HEADS_PER_GROUP * HEAD_DIM
FOX_WIDTH = N_FOX_HEADS * HEAD_DIM
IN_SPLITS = (DIL_WIDTH, DIL_WIDTH, DIL_WIDTH, FOX_WIDTH, FOX_WIDTH, FOX_WIDTH,
             N_FOX_HEADS, D_MODEL, D_MODEL)
IN_COLS = sum(IN_SPLITS)

kernel_name = "hybrid_dilated_fox_gated_block"


def rmsnorm(x, gain):
    xf = x.astype(jnp.float32)
    y = xf * lax.rsqrt(jnp.mean(xf * xf, axis=-1, keepdims=True) + EPS)
    return (y * gain.astype(jnp.float32)).astype(x.dtype)


def partial_rope(x, positions):
    half = ROPE_DIM // 2
    inv_freq = jnp.power(ROPE_THETA, -jnp.arange(0, ROPE_DIM, 2, dtype=jnp.float32) / ROPE_DIM)
    ang = positions[:, None] * inv_freq[None, :]
    cos = jnp.cos(ang)[None, :, None, :].astype(x.dtype)
    sin = jnp.sin(ang)[None, :, None, :].astype(x.dtype)
    x1, x2, rest = x[..., :half], x[..., half:ROPE_DIM], x[..., ROPE_DIM:]
    return jnp.concatenate([x1 * cos - x2 * sin, x2 * cos + x1 * sin, rest], axis=-1)


def dilated_window_attention(q, k, v, window, dilation):
    B, S, H, Dh = q.shape
    steps = window // dilation
    span = dilation * BLOCK
    s_pad = -(-S // span) * span
    L = s_pad // dilation
    nb = L // BLOCK

    def to_blocks(t):
        t = jnp.pad(t, ((0, 0), (0, s_pad - S), (0, 0), (0, 0)))
        t = t.reshape(B, L, dilation, H, Dh).transpose(0, 2, 1, 3, 4)
        return t.reshape(B * dilation, nb, BLOCK, H, Dh)

    qb, kb, vb = to_blocks(q), to_blocks(k), to_blocks(v)

    def with_prev(t):
        prev = jnp.pad(t[:, :-1], ((0, 0), (1, 0), (0, 0), (0, 0), (0, 0)))
        return jnp.concatenate([prev, t], axis=2)

    kc, vc = with_prev(kb), with_prev(vb)
    scale = 1.0 / math.sqrt(Dh)
    scores = jnp.einsum('znqhd,znkhd->znhqk', qb, kc).astype(jnp.float32) * scale
    qi = jnp.arange(BLOCK)[:, None]
    kj = jnp.arange(2 * BLOCK)[None, :]
    rel = qi + BLOCK - kj
    blk = jnp.arange(nb)[:, None, None]
    valid = (rel >= 0) & (rel <= steps) & (blk * BLOCK + kj - BLOCK >= 0)
    scores = jnp.where(valid[None, :, None, :, :], scores, NEG_INF)
    m = jnp.max(scores, axis=-1, keepdims=True)
    p = jnp.exp(scores - m)
    den = jnp.sum(p, axis=-1, keepdims=True)
    out = jnp.einsum('znhqk,znkhd->znqhd', (p / den).astype(v.dtype), vc)
    lse = (m + jnp.log(den))[..., 0].transpose(0, 1, 3, 2)

    def from_blocks(t):
        t = t.reshape((B, dilation, L) + t.shape[3:])
        t = jnp.moveaxis(t, 1, 2).reshape((B, s_pad) + t.shape[3:])
        return t[:, :S]

    return from_blocks(out), from_blocks(lse)


def forgetting_attention(q, k, v, log_f):
    B, S, H, Dh = q.shape
    F = jnp.cumsum(log_f.astype(jnp.float32), axis=1).transpose(0, 2, 1)
    scale = 1.0 / math.sqrt(Dh)
    outs = []
    for n in range(S // BLOCK):
        lo, hi = n * BLOCK, (n + 1) * BLOCK
        s = jnp.einsum('bqhd,bkhd->bhqk', q[:, lo:hi], k[:, :hi]).astype(jnp.float32) * scale
        s = s + (F[:, :, lo:hi, None] - F[:, :, None, :hi])
        causal = jnp.arange(lo, hi)[:, None] >= jnp.arange(hi)[None, :]
        s = jnp.where(causal[None, None], s, NEG_INF)
        p = jax.nn.softmax(s, axis=-1)
        outs.append(jnp.einsum('bhqk,bkhd->bqhd', p.astype(v.dtype), v[:, :hi]))
    return jnp.concatenate(outs, axis=1)


def token_mixer(h, w_in, w_proj_a, w_proj_b, w_out, b_forget):
    B, S, _ = h.shape
    proj = h @ w_in
    idx = list(np.cumsum(IN_SPLITS)[:-1])
    qa, ka, va, qb, kb, vb, f_logit, g_a, g_b = jnp.split(proj, idx, axis=-1)
    positions = jnp.arange(S, dtype=jnp.float32)

    qa = partial_rope(qa.reshape(B, S, N_DIL_HEADS, HEAD_DIM), positions)
    ka = partial_rope(ka.reshape(B, S, N_DIL_HEADS, HEAD_DIM), positions)
    va = va.reshape(B, S, N_DIL_HEADS, HEAD_DIM)
    group_out, group_lse = [], []
    for g, (window, dilation) in enumerate(DIL_CONFIGS):
        sl = slice(g * DIL_HEADS_PER_GROUP, (g + 1) * DIL_HEADS_PER_GROUP)
        o, lse = dilated_window_attention(qa[:, :, sl], ka[:, :, sl], va[:, :, sl], window, dilation)
        group_out.append(o)
        group_lse.append(lse)
    w_groups = jax.nn.softmax(jnp.stack(group_lse, axis=0), axis=0)
    out_a = jnp.sum(w_groups[..., None].astype(h.dtype) * jnp.stack(group_out, axis=0), axis=0)
    out_a = out_a.reshape(B, S, DIL_OUT_WIDTH)

    log_f = jax.nn.log_sigmoid((f_logit + b_forget).astype(jnp.float32))
    out_b = forgetting_attention(qb.reshape(B, S, N_FOX_HEADS, HEAD_DIM),
                                 kb.reshape(B, S, N_FOX_HEADS, HEAD_DIM),
                                 vb.reshape(B, S, N_FOX_HEADS, HEAD_DIM), log_f)
    out_b = out_b.reshape(B, S, FOX_WIDTH)

    merged = jax.nn.sigmoid(g_a) * (out_a @ w_proj_a) + jax.nn.sigmoid(g_b) * (out_b @ w_proj_b)
    return merged @ w_out


def swiglu(h, w_gate, w_up, w_down):
    return (jax.nn.silu(h @ w_gate) * (h @ w_up)) @ w_down


def _fwd_setup_inputs(seed: int = 0) -> dict:
    key = jax.random.key(seed)
    ks = jax.random.split(key, 14)
    f32 = jnp.float32

    def dense(k, fan_in, fan_out):
        return jax.random.normal(k, (DEPTH, fan_in, fan_out), f32) * fan_in ** -0.5

    def gain(k):
        return 1.0 + 0.05 * jax.random.normal(k, (DEPTH, D_MODEL), f32)

    return {
        "x": jax.random.normal(ks[0], (BATCH, SEQ, D_MODEL), f32),
        "w_in": dense(ks[1], D_MODEL, IN_COLS),
        "w_proj_a": dense(ks[2], DIL_OUT_WIDTH, D_MODEL),
        "w_proj_b": dense(ks[3], FOX_WIDTH, D_MODEL),
        "w_out": dense(ks[4], D_MODEL, D_MODEL),
        "b_forget": jax.random.uniform(ks[5], (DEPTH, N_FOX_HEADS), f32, minval=1.0, maxval=5.0),
        "w_ffn_gate": dense(ks[6], D_MODEL, D_FF),
        "w_ffn_up": dense(ks[7], D_MODEL, D_FF),
        "w_ffn_down": dense(ks[8], D_FF, D_MODEL),
        "norm_mix_pre": gain(ks[9]),
        "norm_mix_post": gain(ks[10]),
        "norm_ffn_pre": gain(ks[11]),
        "norm_ffn_post": gain(ks[12]),
    }


def _fwd_reference(x, w_in, w_proj_a, w_proj_b, w_out, b_forget, w_ffn_gate, w_ffn_up, w_ffn_down,
              norm_mix_pre, norm_mix_post, norm_ffn_pre, norm_ffn_post):
    for layer in range(DEPTH):
        h = rmsnorm(x, norm_mix_pre[layer])
        mix = token_mixer(h, w_in[layer], w_proj_a[layer], w_proj_b[layer], w_out[layer], b_forget[layer])
        x = x + rmsnorm(mix, norm_mix_post[layer])
        h = rmsnorm(x, norm_ffn_pre[layer])
        ff = swiglu(h, w_ffn_gate[layer], w_ffn_up[layer], w_ffn_down[layer])
        x = x + rmsnorm(ff, norm_ffn_post[layer])
    return x


import jax as _jax
import jax.numpy as _jnp

TWIN_FORMAT = 'train_step'
FWD_PARAMS = ['x', 'w_in', 'w_proj_a', 'w_proj_b', 'w_out', 'b_forget', 'w_ffn_gate', 'w_ffn_up', 'w_ffn_down', 'norm_mix_pre', 'norm_mix_post', 'norm_ffn_pre', 'norm_ffn_post']
TWIN_WEIGHTS = ['w_in', 'w_proj_a', 'w_proj_b', 'w_out', 'b_forget', 'w_ffn_gate', 'w_ffn_up', 'w_ffn_down', 'norm_mix_pre', 'norm_mix_post', 'norm_ffn_pre', 'norm_ffn_post']
TWIN_DIFF_INPUT = 'x'
TWIN_INPUTS = ['x', 'w_in', 'w_proj_a', 'w_proj_b', 'w_out', 'b_forget', 'w_ffn_gate', 'w_ffn_up', 'w_ffn_down', 'norm_mix_pre', 'norm_mix_post', 'norm_ffn_pre', 'norm_ffn_post', 'loss_target', 'm_w_in', 'm_w_proj_a', 'm_w_proj_b', 'm_w_out', 'm_b_forget', 'm_w_ffn_gate', 'm_w_ffn_up', 'm_w_ffn_down', 'm_norm_mix_pre', 'm_norm_mix_post', 'm_norm_ffn_pre', 'm_norm_ffn_post', 'v_w_in', 'v_w_proj_a', 'v_w_proj_b', 'v_w_out', 'v_b_forget', 'v_w_ffn_gate', 'v_w_ffn_up', 'v_w_ffn_down', 'v_norm_mix_pre', 'v_norm_mix_post', 'v_norm_ffn_pre', 'v_norm_ffn_post']
TWIN_OUTPUTS = ['loss', 'grad_x', 'grad_w_in', 'grad_w_proj_a', 'grad_w_proj_b', 'grad_w_out', 'grad_b_forget', 'grad_w_ffn_gate', 'grad_w_ffn_up', 'grad_w_ffn_down', 'grad_norm_mix_pre', 'grad_norm_mix_post', 'grad_norm_ffn_pre', 'grad_norm_ffn_post', 'delta_w_in', 'delta_w_proj_a', 'delta_w_proj_b', 'delta_w_out', 'delta_b_forget', 'delta_w_ffn_gate', 'delta_w_ffn_up', 'delta_w_ffn_down', 'delta_norm_mix_pre', 'delta_norm_mix_post', 'delta_norm_ffn_pre', 'delta_norm_ffn_post', 'new_m_w_in', 'new_m_w_proj_a', 'new_m_w_proj_b', 'new_m_w_out', 'new_m_b_forget', 'new_m_w_ffn_gate', 'new_m_w_ffn_up', 'new_m_w_ffn_down', 'new_m_norm_mix_pre', 'new_m_norm_mix_post', 'new_m_norm_ffn_pre', 'new_m_norm_ffn_post', 'new_v_w_in', 'new_v_w_proj_a', 'new_v_w_proj_b', 'new_v_w_out', 'new_v_b_forget', 'new_v_w_ffn_gate', 'new_v_w_ffn_up', 'new_v_w_ffn_down', 'new_v_norm_mix_pre', 'new_v_norm_mix_post', 'new_v_norm_ffn_pre', 'new_v_norm_ffn_post']
TWIN_LEAF_KINDS = {'loss': 'loss', 'grad_x': 'grad_x', 'grad_w_in': 'grad_w', 'grad_w_proj_a': 'grad_w', 'grad_w_proj_b': 'grad_w', 'grad_w_out': 'grad_w', 'grad_b_forget': 'grad_w', 'grad_w_ffn_gate': 'grad_w', 'grad_w_ffn_up': 'grad_w', 'grad_w_ffn_down': 'grad_w', 'grad_norm_mix_pre': 'grad_w', 'grad_norm_mix_post': 'grad_w', 'grad_norm_ffn_pre': 'grad_w', 'grad_norm_ffn_post': 'grad_w', 'delta_w_in': 'delta_w', 'delta_w_proj_a': 'delta_w', 'delta_w_proj_b': 'delta_w', 'delta_w_out': 'delta_w', 'delta_b_forget': 'delta_w', 'delta_w_ffn_gate': 'delta_w', 'delta_w_ffn_up': 'delta_w', 'delta_w_ffn_down': 'delta_w', 'delta_norm_mix_pre': 'delta_w', 'delta_norm_mix_post': 'delta_w', 'delta_norm_ffn_pre': 'delta_w', 'delta_norm_ffn_post': 'delta_w', 'new_m_w_in': 'new_m', 'new_m_w_proj_a': 'new_m', 'new_m_w_proj_b': 'new_m', 'new_m_w_out': 'new_m', 'new_m_b_forget': 'new_m', 'new_m_w_ffn_gate': 'new_m', 'new_m_w_ffn_up': 'new_m', 'new_m_w_ffn_down': 'new_m', 'new_m_norm_mix_pre': 'new_m', 'new_m_norm_mix_post': 'new_m', 'new_m_norm_ffn_pre': 'new_m', 'new_m_norm_ffn_post': 'new_m', 'new_v_w_in': 'new_v', 'new_v_w_proj_a': 'new_v', 'new_v_w_proj_b': 'new_v', 'new_v_w_out': 'new_v', 'new_v_b_forget': 'new_v', 'new_v_w_ffn_gate': 'new_v', 'new_v_w_ffn_up': 'new_v', 'new_v_w_ffn_down': 'new_v', 'new_v_norm_mix_pre': 'new_v', 'new_v_norm_mix_post': 'new_v', 'new_v_norm_ffn_pre': 'new_v', 'new_v_norm_ffn_post': 'new_v'}


def _forward(args):
    return _fwd_reference(*[args[k] for k in FWD_PARAMS])


def _output_shape():
    out = _jax.eval_shape(lambda: _forward(_fwd_setup_inputs(0)))
    return out.shape, out.dtype

N_MICROBATCH = 1
ADAM_LR = 0.001
ADAM_B1 = 0.9
ADAM_B2 = 0.999
ADAM_EPS = 1e-08
ADAM_WD = 0.01
ADAM_STEP = 10
PER_EXAMPLE_BATCH_AXIS = {'x': 0, 'loss_target': 0}
SHARED_INPUTS = []
_WEIGHT_DTYPES = {'w_in': _jnp.float32, 'w_proj_a': _jnp.float32, 'w_proj_b': _jnp.float32, 'w_out': _jnp.float32, 'b_forget': _jnp.float32, 'w_ffn_gate': _jnp.float32, 'w_ffn_up': _jnp.float32, 'w_ffn_down': _jnp.float32, 'norm_mix_pre': _jnp.float32, 'norm_mix_post': _jnp.float32, 'norm_ffn_pre': _jnp.float32, 'norm_ffn_post': _jnp.float32}
MOMENT_SCALE = {'w_in': 2.239852e-01, 'w_proj_a': 1.711098e-01, 'w_proj_b': 3.918288e-01, 'w_out': 4.193627e-01, 'b_forget': 2.085337e+00, 'w_ffn_gate': 1.403608e-01, 'w_ffn_up': 2.002730e-01, 'w_ffn_down': 3.325671e-01, 'norm_mix_pre': 5.695830e-01, 'norm_mix_post': 1.600195e+01, 'norm_ffn_pre': 3.917467e-01, 'norm_ffn_post': 1.606835e+01}


def _to_microbatches(a, axis):
    t = _jnp.moveaxis(a, axis, 0)
    t = t.reshape((N_MICROBATCH, t.shape[0] // N_MICROBATCH) + t.shape[1:])
    return _jnp.moveaxis(t, 1, axis + 1)


def setup_inputs(seed: int = 0) -> dict:
    inp = _fwd_setup_inputs(seed)
    key = _jax.random.fold_in(_jax.random.key(seed), 7919)
    shape, _ = _output_shape()
    out = dict(inp)
    out["loss_target"] = _jax.random.normal(_jax.random.fold_in(key, 0), shape, _jnp.float32)
    for i, name in enumerate(TWIN_WEIGHTS):
        w = inp[name].astype(_jnp.float32)
        if MOMENT_SCALE is None:
            s = _jnp.sqrt(_jnp.mean(_jnp.square(w)) + 1e-30)
        else:
            s = MOMENT_SCALE[name]
        km, kv = _jax.random.split(_jax.random.fold_in(key, i + 1))
        out[name] = w
        out["m_" + name] = s * _jax.random.normal(km, w.shape, _jnp.float32)
        out["v_" + name] = (s * s) * _jax.random.uniform(kv, w.shape, _jnp.float32, 0.5, 1.5)
    if N_MICROBATCH > 1:
        for name, axis in PER_EXAMPLE_BATCH_AXIS.items():
            out[name] = _to_microbatches(out[name], axis)
    return {'x': out['x'], 'w_in': out['w_in'], 'w_proj_a': out['w_proj_a'], 'w_proj_b': out['w_proj_b'], 'w_out': out['w_out'], 'b_forget': out['b_forget'], 'w_ffn_gate': out['w_ffn_gate'], 'w_ffn_up': out['w_ffn_up'], 'w_ffn_down': out['w_ffn_down'], 'norm_mix_pre': out['norm_mix_pre'], 'norm_mix_post': out['norm_mix_post'], 'norm_ffn_pre': out['norm_ffn_pre'], 'norm_ffn_post': out['norm_ffn_post'], 'loss_target': out['loss_target'], 'm_w_in': out['m_w_in'], 'm_w_proj_a': out['m_w_proj_a'], 'm_w_proj_b': out['m_w_proj_b'], 'm_w_out': out['m_w_out'], 'm_b_forget': out['m_b_forget'], 'm_w_ffn_gate': out['m_w_ffn_gate'], 'm_w_ffn_up': out['m_w_ffn_up'], 'm_w_ffn_down': out['m_w_ffn_down'], 'm_norm_mix_pre': out['m_norm_mix_pre'], 'm_norm_mix_post': out['m_norm_mix_post'], 'm_norm_ffn_pre': out['m_norm_ffn_pre'], 'm_norm_ffn_post': out['m_norm_ffn_post'], 'v_w_in': out['v_w_in'], 'v_w_proj_a': out['v_w_proj_a'], 'v_w_proj_b': out['v_w_proj_b'], 'v_w_out': out['v_w_out'], 'v_b_forget': out['v_b_forget'], 'v_w_ffn_gate': out['v_w_ffn_gate'], 'v_w_ffn_up': out['v_w_ffn_up'], 'v_w_ffn_down': out['v_w_ffn_down'], 'v_norm_mix_pre': out['v_norm_mix_pre'], 'v_norm_mix_post': out['v_norm_mix_post'], 'v_norm_ffn_pre': out['v_norm_ffn_pre'], 'v_norm_ffn_post': out['v_norm_ffn_post']}


def _loss(weights, diff, rest, loss_target):
    with _jax.named_scope("forward"):
        args = {**rest, TWIN_DIFF_INPUT: diff, **{k: w.astype(_WEIGHT_DTYPES[k]) for k, w in weights.items()}}
        y = _forward(args)
    with _jax.named_scope("loss_head"):
        err = _jnp.square(y.astype(_jnp.float32) - loss_target)
        return 0.5 * _jnp.sum(_jnp.mean(err, axis=-1)) if err.ndim else 0.5 * err


def _adamw(w, g, m, v):
    m = ADAM_B1 * m + (1.0 - ADAM_B1) * g
    v = ADAM_B2 * v + (1.0 - ADAM_B2) * _jnp.square(g)
    m_hat = m / (1.0 - ADAM_B1 ** ADAM_STEP)
    v_hat = v / (1.0 - ADAM_B2 ** ADAM_STEP)
    delta = -ADAM_LR * (m_hat / (_jnp.sqrt(v_hat) + ADAM_EPS) + ADAM_WD * w)
    return delta, m, v


def reference(x, w_in, w_proj_a, w_proj_b, w_out, b_forget, w_ffn_gate, w_ffn_up, w_ffn_down, norm_mix_pre, norm_mix_post, norm_ffn_pre, norm_ffn_post, loss_target, m_w_in, m_w_proj_a, m_w_proj_b, m_w_out, m_b_forget, m_w_ffn_gate, m_w_ffn_up, m_w_ffn_down, m_norm_mix_pre, m_norm_mix_post, m_norm_ffn_pre, m_norm_ffn_post, v_w_in, v_w_proj_a, v_w_proj_b, v_w_out, v_b_forget, v_w_ffn_gate, v_w_ffn_up, v_w_ffn_down, v_norm_mix_pre, v_norm_mix_post, v_norm_ffn_pre, v_norm_ffn_post):
    given = dict(x=x, w_in=w_in, w_proj_a=w_proj_a, w_proj_b=w_proj_b, w_out=w_out, b_forget=b_forget, w_ffn_gate=w_ffn_gate, w_ffn_up=w_ffn_up, w_ffn_down=w_ffn_down, norm_mix_pre=norm_mix_pre, norm_mix_post=norm_mix_post, norm_ffn_pre=norm_ffn_pre, norm_ffn_post=norm_ffn_post, loss_target=loss_target, m_w_in=m_w_in, m_w_proj_a=m_w_proj_a, m_w_proj_b=m_w_proj_b, m_w_out=m_w_out, m_b_forget=m_b_forget, m_w_ffn_gate=m_w_ffn_gate, m_w_ffn_up=m_w_ffn_up, m_w_ffn_down=m_w_ffn_down, m_norm_mix_pre=m_norm_mix_pre, m_norm_mix_post=m_norm_mix_post, m_norm_ffn_pre=m_norm_ffn_pre, m_norm_ffn_post=m_norm_ffn_post, v_w_in=v_w_in, v_w_proj_a=v_w_proj_a, v_w_proj_b=v_w_proj_b, v_w_out=v_w_out, v_b_forget=v_b_forget, v_w_ffn_gate=v_w_ffn_gate, v_w_ffn_up=v_w_ffn_up, v_w_ffn_down=v_w_ffn_down, v_norm_mix_pre=v_norm_mix_pre, v_norm_mix_post=v_norm_mix_post, v_norm_ffn_pre=v_norm_ffn_pre, v_norm_ffn_post=v_norm_ffn_post)
    weights = {n: given[n] for n in TWIN_WEIGHTS}
    shared = {n: given[n] for n in SHARED_INPUTS}
    per_example = {n: given[n] for n in ['x']}
    grad_fn = _jax.value_and_grad(_loss, argnums=(0, 1))

    def one_microbatch(ex, loss_target):
        ex = dict(ex)
        diff = ex.pop(TWIN_DIFF_INPUT)
        return grad_fn(weights, diff, {**shared, **ex}, loss_target)

    if N_MICROBATCH == 1:
        loss, (grad_w, grad_x) = one_microbatch(per_example, given["loss_target"])
    else:
        def body(carry, xs):
            loss_sum, grad_sum = carry
            l_k, (gw_k, gx_k) = one_microbatch(xs[0], xs[1])
            with _jax.named_scope("update"):
                return (loss_sum + l_k, _jax.tree.map(_jnp.add, grad_sum, gw_k)), gx_k

        init = (_jnp.zeros((), _jnp.float32), _jax.tree.map(_jnp.zeros_like, weights))
        (loss, grad_w), grad_x = _jax.lax.scan(body, init, (per_example, given["loss_target"]))
    with _jax.named_scope("update"):
        delta_w, new_m, new_v = {}, {}, {}
        for n in TWIN_WEIGHTS:
            delta_w[n], new_m[n], new_v[n] = _adamw(weights[n], grad_w[n], given["m_" + n], given["v_" + n])
    return (loss, grad_x, *[grad_w[n] for n in TWIN_WEIGHTS], *[delta_w[n] for n in TWIN_WEIGHTS],
            *[new_m[n] for n in TWIN_WEIGHTS], *[new_v[n] for n in TWIN_WEIGHTS])
```

```python
import functools
import math

import jax
import jax.numpy as jnp
from jax import lax
from jax.experimental import pallas as pl
from jax.experimental.pallas import tpu as pltpu

F32 = jnp.float32
BF16 = jnp.bfloat16
MESH = pl.DeviceIdType.MESH

D_MODEL = 1024
SEQ = 2048
HEAD_DIM = 64
BLOCK = 128
N_BLOCKS = SEQ // BLOCK
DILATIONS = (1, 4, 16)
DIL_GROUP_HEADS = 4
N_DIL_HEADS = 12
N_FOX_HEADS = 8
DIL_WIDTH = N_DIL_HEADS * HEAD_DIM
DIL_OUT_WIDTH = DIL_GROUP_HEADS * HEAD_DIM
FOX_WIDTH = N_FOX_HEADS * HEAD_DIM
D_FF = 2816
ROPE_THETA = 500000.0
ROPE_DIM = HEAD_DIM // 4
ROPE_HALF = ROPE_DIM // 2
EPS = 1e-6
NEG_INF = -1e30
QK_SCALE = 1.0 / math.sqrt(HEAD_DIM)
IN_COLS = 5896
N_DEV = 8
IN_SHARD = IN_COLS // N_DEV
FF_SHARD = D_FF // N_DEV

ADAM_LR = 0.001
ADAM_B1 = 0.9
ADAM_B2 = 0.999
ADAM_EPS = 1e-08
ADAM_WD = 0.01
ADAM_STEP = 10

V7X_VMEM_BYTES = 64 * 2**20
LANES = 128

PROJ_COLS = 6272
COL_GA, COL_GB = 0, 1024
COL_QA, COL_KA, COL_VA = 2304, 3072, 3840
COL_QB, COL_KB, COL_VB = 4608, 5120, 5632
COL_F = 6144

PACK_COLS = 1024
PACK_ROWS = 2048


def _vmem_limit(block_bytes):
    want = 2 * block_bytes + 16 * 2**20
    return int(min(max(want, 32 * 2**20), V7X_VMEM_BYTES - 8 * 2**20))


def _nbytes(shape, dtype):
    return math.prod(shape) * jnp.dtype(dtype).itemsize


def _dot(a, b, dims):
    return lax.dot_general(a, b, (dims, ((), ())), preferred_element_type=F32)


def _dot_nn(a, b):
    return _dot(a, b, ((1,), (0,)))


def _dot_nt(a, b):
    return _dot(a, b, ((1,), (1,)))


def _sigmoid(z):
    return 1.0 / (1.0 + jnp.exp(-z))


def _rowwise(fn, name, n_rows, tm, row_ins, bcast_ins, row_outs, acc_outs=()):
    n_in = len(row_ins) + len(bcast_ins)
    n_ro = len(row_outs)

    def body(*refs):
        res = fn(*[r[...] for r in refs[:n_in]])
        if not isinstance(res, (tuple, list)):
            res = (res,)
        outs = refs[n_in:]
        for r, o in zip(res[:n_ro], outs[:n_ro]):
            o[...] = r.astype(o.dtype)
        first = pl.program_id(0) == 0
        for r, o in zip(res[n_ro:], outs[n_ro:]):
            _accumulate(o, r, first)

    in_specs = [pl.BlockSpec((tm, w), lambda i, cb=cb: (i, cb)) for _, w, cb in row_ins]
    in_specs += [pl.BlockSpec(a.shape, lambda i: (0, 0)) for a in bcast_ins]
    out_specs = [pl.BlockSpec((tm, w), lambda i: (i, 0)) for w, _ in row_outs]
    out_specs += [pl.BlockSpec((1, w), lambda i: (0, 0)) for w in acc_outs]
    out_shape = [jax.ShapeDtypeStruct((n_rows, w), dt) for w, dt in row_outs]
    out_shape += [jax.ShapeDtypeStruct((1, w), F32) for w in acc_outs]
    blk = sum(_nbytes((tm, w), a.dtype) for a, w, _ in row_ins) + sum(_nbytes((tm, w), dt) for w, dt in row_outs)
    return pl.pallas_call(
        body, name=name, grid=(n_rows // tm,), in_specs=in_specs, out_specs=out_specs, out_shape=out_shape,
        compiler_params=pltpu.CompilerParams(
            dimension_semantics=("arbitrary" if acc_outs else "parallel",), vmem_limit_bytes=_vmem_limit(3 * blk)),
    )(*[a for a, _, _ in row_ins], *bcast_ins)


def _accumulate(o_ref, part, first):
    @pl.when(first)
    def _():
        o_ref[...] = part

    @pl.when(jnp.logical_not(first))
    def _():
        o_ref[...] += part


_MM_DIMS = {"nn": ((1,), (0,)), "nt": ((1,), (1,)), "tn": ((0,), (0,))}


def _matmul(pairs, mode, out_dtype, name, tm, tn, tk):
    a0, b0 = pairs[0]
    if mode == "tn":
        kk, m = a0.shape
    else:
        m, kk = a0.shape
    n = b0.shape[0] if mode == "nt" else b0.shape[1]
    assert m % tm == 0 and n % tn == 0 and kk % tk == 0, (name, m, n, kk)
    nk = kk // tk
    n_pairs = len(pairs)
    dims = _MM_DIMS[mode]

    def body(*refs):
        o_ref = refs[2 * n_pairs]
        part = None
        for p in range(n_pairs):
            d = _dot(refs[2 * p][...], refs[2 * p + 1][...], dims)
            part = d if part is None else part + d
        if nk == 1:
            o_ref[...] = part.astype(o_ref.dtype)
            return
        acc = refs[2 * n_pairs + 1]
        k = pl.program_id(2)

        @pl.when(k == 0)
        def _():
            acc[...] = part

        @pl.when(k > 0)
        def _():
            acc[...] += part

        @pl.when(k == nk - 1)
        def _():
            o_ref[...] = acc[...].astype(o_ref.dtype)

    if mode == "tn":
        a_spec = pl.BlockSpec((tk, tm), lambda i, j, k: (k, i))
    else:
        a_spec = pl.BlockSpec((tm, tk), lambda i, j, k: (i, k))
    if mode == "nt":
        b_spec = pl.BlockSpec((tn, tk), lambda i, j, k: (j, k))
    else:
        b_spec = pl.BlockSpec((tk, tn), lambda i, j, k: (k, j))
    blk = n_pairs * (_nbytes((tm, tk), BF16) + _nbytes((tk, tn), BF16)) + 2 * _nbytes((tm, tn), F32)
    flat = [a for pair in pairs for a in pair]
    return pl.pallas_call(
        body, name=name, grid=(m // tm, n // tn, nk),
        in_specs=[a_spec, b_spec] * n_pairs,
        out_specs=pl.BlockSpec((tm, tn), lambda i, j, k: (i, j)),
        out_shape=jax.ShapeDtypeStruct((m, n), out_dtype),
        scratch_shapes=[] if nk == 1 else [pltpu.VMEM((tm, tn), F32)],
        compiler_params=pltpu.CompilerParams(
            dimension_semantics=("parallel", "parallel", "arbitrary"), vmem_limit_bytes=_vmem_limit(blk)),
    )(*flat)


def _rms_scale(x):
    return lax.rsqrt(jnp.mean(x * x, axis=-1, keepdims=True) + EPS)


def _rms_bwd(xin, dyn, g):
    r = _rms_scale(xin)
    u = dyn * g
    dx = r * u - xin * (r * r * r) * jnp.mean(u * xin, axis=-1, keepdims=True)
    dg = jnp.sum(dyn * xin * r, axis=0, keepdims=True)
    return dx, dg


def _mesh_pos():
    return lax.axis_index("x"), lax.axis_index("y"), lax.axis_index("c")


def _all_gather(x, name, n_chunks):
    m, n = x.shape
    cm = m // n_chunks
    assert cm * n_chunks == m

    def body(x_ref, out_ref, send_sems, recv_sems, local_sem):
        mx, my, mc = _mesh_pos()
        me, sib = (mx, my, mc), (mx, my, 1 - mc)
        chips = [(1 - mx, my), (mx, 1 - my), (1 - mx, 1 - my)]

        def rows(dev, ch):
            px, py, pc = dev
            return out_ref.at[4 * px + 2 * py + pc, pl.ds(ch * cm, cm), :]

        def copy(k, ch, block, to, src=None):
            return pltpu.make_async_remote_copy(
                src_ref=rows(block, ch) if src is None else src, dst_ref=rows(block, ch),
                send_sem=send_sems.at[ch * 7 + k], recv_sem=recv_sems.at[ch * 7 + k],
                device_id=to, device_id_type=MESH)

        mine = pltpu.make_async_copy(x_ref, out_ref.at[4 * mx + 2 * my + mc], local_sem)
        mine.start()
        first = []
        for ch in range(n_chunks):
            src = x_ref.at[pl.ds(ch * cm, cm), :]
            first.append(copy(0, ch, me, sib, src))
            first += [copy(1 + j, ch, me, (*chip, mc), src) for j, chip in enumerate(chips)]
        for cp in first:
            cp.start()
        passed = []
        for ch in range(n_chunks):
            for j, chip in enumerate(chips):
                copy(1 + j, ch, (*chip, mc), me).wait_recv()
                fwd = copy(4 + j, ch, (*chip, mc), sib)
                fwd.start()
                passed.append(fwd)
        for ch in range(n_chunks):
            copy(0, ch, sib, me).wait_recv()
            for j, chip in enumerate(chips):
                copy(4 + j, ch, (*chip, 1 - mc), me).wait_recv()
        for cp in first + passed:
            cp.wait_send()
        mine.wait()

    return pl.pallas_call(
        body, name=name,
        out_shape=jax.ShapeDtypeStruct((N_DEV, m, n), x.dtype),
        in_specs=[pl.BlockSpec(memory_space=pl.ANY)],
        out_specs=pl.BlockSpec(memory_space=pl.ANY),
        scratch_shapes=[pltpu.SemaphoreType.DMA((7 * n_chunks,)), pltpu.SemaphoreType.DMA((7 * n_chunks,)),
                        pltpu.SemaphoreType.DMA],
    )(x)


def _sibling_exchange(g4, name):
    _, _, r, c = g4.shape

    def body(g_ref, out_ref, send_sems, recv_sems):
        mx, my, mc = _mesh_pos()
        cps = [pltpu.make_async_remote_copy(
            src_ref=g_ref.at[k, 1 - mc], dst_ref=out_ref.at[k], send_sem=send_sems.at[k], recv_sem=recv_sems.at[k],
            device_id=(mx, my, 1 - mc), device_id_type=MESH) for k in range(4)]
        for cp in cps:
            cp.start()
        for cp in cps:
            cp.wait()

    return pl.pallas_call(
        body, name=name, out_shape=jax.ShapeDtypeStruct((4, r, c), g4.dtype),
        in_specs=[pl.BlockSpec(memory_space=pl.ANY)], out_specs=pl.BlockSpec(memory_space=pl.ANY),
        scratch_shapes=[pltpu.SemaphoreType.DMA((4,)), pltpu.SemaphoreType.DMA((4,))],
    )(g4)


def _chip_exchange(p, name):
    _, r, c = p.shape

    def body(p_ref, out_ref, send_sems, recv_sems):
        mx, my, mc = _mesh_pos()
        k0 = 2 * mx + my
        cps = []
        for flip in (1, 2, 3):
            peer = (1 - mx if flip & 2 else mx, 1 - my if flip & 1 else my, mc)
            cps.append(pltpu.make_async_remote_copy(
                src_ref=p_ref.at[jnp.bitwise_xor(k0, flip)], dst_ref=out_ref.at[flip - 1],
                send_sem=send_sems.at[flip - 1], recv_sem=recv_sems.at[flip - 1],
                device_id=peer, device_id_type=MESH))
        for cp in cps:
            cp.start()
        for cp in cps:
            cp.wait()

    return pl.pallas_call(
        body, name=name, out_shape=jax.ShapeDtypeStruct((3, r, c), p.dtype),
        in_specs=[pl.BlockSpec(memory_space=pl.ANY)], out_specs=pl.BlockSpec(memory_space=pl.ANY),
        scratch_shapes=[pltpu.SemaphoreType.DMA((3,)), pltpu.SemaphoreType.DMA((3,))],
    )(p)


def _add_sibling(g4, recv, core, name, tr):
    _, _, r, c = g4.shape

    def body(core_ref, g_ref, r_ref, o_ref):
        o_ref[0] = g_ref[0, 0] + r_ref[0]

    return pl.pallas_call(
        body, name=name, out_shape=jax.ShapeDtypeStruct((4, r, c), F32),
        grid_spec=pltpu.PrefetchScalarGridSpec(
            num_scalar_prefetch=1, grid=(4, r // tr),
            in_specs=[pl.BlockSpec((1, 1, tr, c), lambda k, i, core_ref: (k, core_ref[0], i, 0)),
                      pl.BlockSpec((1, tr, c), lambda k, i, core_ref: (k, i, 0))],
            out_specs=pl.BlockSpec((1, tr, c), lambda k, i, core_ref: (k, i, 0))),
        compiler_params=pltpu.CompilerParams(dimension_semantics=("parallel", "parallel")),
    )(core, g4, recv)


def _add_chips(p, recv, chip, name, tr):
    _, r, c = p.shape

    def body(chip_ref, p_ref, r_ref, o_ref):
        o_ref[...] = ((p_ref[0] + r_ref[0]) + r_ref[1]) + r_ref[2]

    return pl.pallas_call(
        body, name=name, out_shape=jax.ShapeDtypeStruct((r, c), F32),
        grid_spec=pltpu.PrefetchScalarGridSpec(
            num_scalar_prefetch=1, grid=(r // tr,),
            in_specs=[pl.BlockSpec((1, tr, c), lambda i, chip_ref: (chip_ref[0], i, 0)),
                      pl.BlockSpec((3, tr, c), lambda i, chip_ref: (0, i, 0))],
            out_specs=pl.BlockSpec((tr, c), lambda i, chip_ref: (i, 0))),
        compiler_params=pltpu.CompilerParams(dimension_semantics=("parallel",)),
    )(chip, p, recv)


def _reduce_scatter(g, core, chip):
    _, r, c = g.shape
    g4 = g.reshape(4, 2, r, c)
    from_sibling = _sibling_exchange(g4, "rs_sibling_exchange")
    partial = _add_sibling(g4, from_sibling, core, "rs_add_sibling", 256)
    from_chips = _chip_exchange(partial, "rs_chip_exchange")
    return _add_chips(partial, from_chips, chip, "rs_add_chips", 256)


def _rope_tables():
    positions = jnp.arange(SEQ, dtype=F32)
    inv_freq = jnp.power(ROPE_THETA, -jnp.arange(0, ROPE_DIM, 2, dtype=F32) / ROPE_DIM)
    ang = positions[:, None] * inv_freq[None, :]
    cos, sin = jnp.cos(ang), jnp.sin(ang)
    ones = jnp.ones((SEQ, HEAD_DIM - ROPE_DIM), F32)
    zeros8 = jnp.zeros((SEQ, ROPE_HALF), F32)
    zeros = jnp.zeros((SEQ, HEAD_DIM - ROPE_DIM), F32)
    c_head = jnp.concatenate([cos, cos, ones], axis=1)
    s1_head = jnp.concatenate([-sin, zeros8, zeros], axis=1)
    s2_head = jnp.concatenate([zeros8, sin, zeros], axis=1)
    return tuple(jnp.concatenate([t, t], axis=1) for t in (c_head, s1_head, s2_head))


def _tile_lanes(t, width):
    return jnp.concatenate([t] * (width // LANES), axis=1)


def _rope_apply(x, c, s1, s2):
    w = x.shape[1]
    return x * c + pltpu.roll(x, w - ROPE_HALF, 1) * s1 + pltpu.roll(x, ROPE_HALF, 1) * s2


def _rope_apply_t(dy, c, s1, s2):
    w = dy.shape[1]
    return dy * c + pltpu.roll(dy * s1, ROPE_HALF, 1) + pltpu.roll(dy * s2, w - ROPE_HALF, 1)


def _dil_qkv(proj, tables):
    def fn(q, k, v, c, s1, s2):
        c, s1, s2 = (_tile_lanes(t, DIL_WIDTH) for t in (c, s1, s2))
        return _rope_apply(q, c, s1, s2) * QK_SCALE, _rope_apply(k, c, s1, s2), v

    w = DIL_WIDTH
    return _rowwise(fn, "dil_rope", SEQ, 256,
                    [(proj, w, COL_QA // w), (proj, w, COL_KA // w), (proj, w, COL_VA // w)]
                    + [(t, LANES, 0) for t in tables], [],
                    [(w, BF16)] * 3)


def _dil_qk_bwd(dq, dk, tables):
    def fn(dq, dk, c, s1, s2):
        c, s1, s2 = (_tile_lanes(t, DIL_WIDTH) for t in (c, s1, s2))
        return _rope_apply_t(dq, c, s1, s2), _rope_apply_t(dk, c, s1, s2)

    w = DIL_WIDTH
    return _rowwise(fn, "dil_rope_bwd", SEQ, 256, [(dq, w, 0), (dk, w, 0)] + [(t, LANES, 0) for t in tables], [],
                    [(w, BF16)] * 2)


def _to_heads(a, n_heads):
    return a.reshape(SEQ, n_heads, HEAD_DIM).transpose(1, 0, 2)


def _from_heads(a):
    return a.transpose(1, 0, 2).reshape(SEQ, -1)


def _residue_major(a, d):
    h, _, x = a.shape
    return a.reshape(h, SEQ // d, d, x).transpose(0, 2, 1, 3).reshape(h, SEQ, x)


def _token_major(a, d):
    h, _, x = a.shape
    return a.reshape(h, d, SEQ // d, x).transpose(0, 2, 1, 3).reshape(h, SEQ, x)


def _dil_permute(a, fn):
    g = DIL_GROUP_HEADS
    return jnp.concatenate([fn(a[i * g:(i + 1) * g], d) for i, d in enumerate(DILATIONS)], axis=0)


def _dil_prev_limit(h, n):
    g = h // DIL_GROUP_HEADS
    mask = jnp.where(g == 0, 15, jnp.where(g == 1, 3, 0))
    return jnp.where(jnp.bitwise_and(n, mask) != 0, 0, BLOCK)


def _dil_fwd(q, k, v):
    def body(q_ref, kp_ref, kc_ref, vp_ref, vc_ref, o_ref, lse_ref):
        h, n = pl.program_id(0), pl.program_id(1)
        qi = lax.broadcasted_iota(jnp.int32, (BLOCK, BLOCK), 0)
        kj = lax.broadcasted_iota(jnp.int32, (BLOCK, BLOCK), 1)
        q = q_ref[0]
        sp = jnp.where(kj - qi >= _dil_prev_limit(h, n), _dot_nt(q, kp_ref[0]), NEG_INF)
        sc = jnp.where(kj <= qi, _dot_nt(q, kc_ref[0]), NEG_INF)
        m = jnp.maximum(jnp.max(sp, axis=-1, keepdims=True), jnp.max(sc, axis=-1, keepdims=True))
        pp, pc = jnp.exp(sp - m), jnp.exp(sc - m)
        den = jnp.sum(pp, axis=-1, keepdims=True) + jnp.sum(pc, axis=-1, keepdims=True)
        o_ref[0] = _dot_nn((pp / den).astype(BF16), vp_ref[0]) + _dot_nn((pc / den).astype(BF16), vc_ref[0])
        lse_ref[0] = m + jnp.log(den)

    cur = pl.BlockSpec((1, BLOCK, HEAD_DIM), lambda h, n: (h, n, 0))
    prev = pl.BlockSpec((1, BLOCK, HEAD_DIM), lambda h, n: (h, jnp.maximum(n - 1, 0), 0))
    return pl.pallas_call(
        body, name="dil_attn_fwd", grid=(N_DIL_HEADS, N_BLOCKS),
        in_specs=[cur, prev, cur, prev, cur],
        out_specs=[cur, pl.BlockSpec((1, BLOCK, 1), lambda h, n: (h, n, 0))],
        out_shape=[jax.ShapeDtypeStruct((N_DIL_HEADS, SEQ, HEAD_DIM), F32),
                   jax.ShapeDtypeStruct((N_DIL_HEADS, SEQ, 1), F32)],
        compiler_params=pltpu.CompilerParams(dimension_semantics=("parallel", "parallel")),
    )(q, k, k, v, v)


def _dil_bwd(q, k, v, do, lse_c, lse_r, c_c, c_r):
    def body(q_ref, qn_ref, do_ref, don_ref, kp_ref, kc_ref, vp_ref, vc_ref,
             lsec_ref, cc_ref, lser_ref, cr_ref, lsern_ref, crn_ref, dq_ref, dk_ref, dv_ref):
        h, n = pl.program_id(0), pl.program_id(1)
        row = lax.broadcasted_iota(jnp.int32, (BLOCK, BLOCK), 0)
        col = lax.broadcasted_iota(jnp.int32, (BLOCK, BLOCK), 1)
        q, do = q_ref[0], do_ref[0]
        kc, vc = kc_ref[0], vc_ref[0]

        def ds_of(kb, vb, valid):
            p = jnp.where(valid, jnp.exp(_dot_nt(q, kb) - lsec_ref[0]), 0.0)
            return (p * (_dot_nt(do, vb) + cc_ref[0])).astype(BF16)

        dq = _dot_nn(ds_of(kp_ref[0], vp_ref[0], col - row >= _dil_prev_limit(h, n)), kp_ref[0])
        dq += _dot_nn(ds_of(kc, vc, col <= row), kc)
        dq_ref[0] = dq * QK_SCALE

        def kv_of(qb, dob, lse_row, c_row, valid):
            p = jnp.where(valid, jnp.exp(_dot_nt(kc, qb) - lse_row), 0.0)
            ds = p * (_dot_nt(vc, dob) + c_row)
            return _dot_nn(ds.astype(BF16), qb), _dot_nn(p.astype(BF16), dob)

        dk, dv = kv_of(q, do, lser_ref[0], cr_ref[0], row <= col)
        next_limit = jnp.where(n + 1 < N_BLOCKS, _dil_prev_limit(h, jnp.minimum(n + 1, N_BLOCKS - 1)), BLOCK)
        dk2, dv2 = kv_of(qn_ref[0], don_ref[0], lsern_ref[0], crn_ref[0], row - col >= next_limit)
        dk_ref[0] = dk + dk2
        dv_ref[0] = dv + dv2

    cur = pl.BlockSpec((1, BLOCK, HEAD_DIM), lambda h, n: (h, n, 0))
    prev = pl.BlockSpec((1, BLOCK, HEAD_DIM), lambda h, n: (h, jnp.maximum(n - 1, 0), 0))
    nxt = pl.BlockSpec((1, BLOCK, HEAD_DIM), lambda h, n: (h, jnp.minimum(n + 1, N_BLOCKS - 1), 0))
    col_cur = pl.BlockSpec((1, BLOCK, 1), lambda h, n: (h, n, 0))
    row_cur = pl.BlockSpec((1, 1, BLOCK), lambda h, n: (h, 0, n))
    row_nxt = pl.BlockSpec((1, 1, BLOCK), lambda h, n: (h, 0, jnp.minimum(n + 1, N_BLOCKS - 1)))
    shape = jax.ShapeDtypeStruct((N_DIL_HEADS, SEQ, HEAD_DIM), F32)
    return pl.pallas_call(
        body, name="dil_attn_bwd", grid=(N_DIL_HEADS, N_BLOCKS),
        in_specs=[cur, nxt, cur, nxt, prev, cur, prev, cur, col_cur, col_cur, row_cur, row_cur, row_nxt, row_nxt],
        out_specs=[cur, cur, cur], out_shape=[shape, shape, shape],
        compiler_params=pltpu.CompilerParams(dimension_semantics=("parallel", "parallel")),
    )(q, q, do, do, k, k, v, v, lse_c, c_c, lse_r, c_r, lse_r, c_r)


def _group_weights(l0, l1, l2):
    m = jnp.maximum(jnp.maximum(l0, l1), l2)
    e0, e1, e2 = jnp.exp(l0 - m), jnp.exp(l1 - m), jnp.exp(l2 - m)
    tot = e0 + e1 + e2
    return e0 / tot, e1 / tot, e2 / tot


def _dil_combine(outs, lses):
    def fn(o0, o1, o2, l0, l1, l2):
        w0, w1, w2 = _group_weights(l0, l1, l2)
        return w0 * o0 + w1 * o1 + w2 * o2

    rows = DIL_GROUP_HEADS * SEQ
    return _rowwise(fn, "dil_combine", rows, 1024, [(o, HEAD_DIM, 0) for o in outs] + [(l, 1, 0) for l in lses], [],
                    [(HEAD_DIM, F32)])[0]


def _dil_combine_bwd(d_out, outs, lses):
    def fn(d, o0, o1, o2, l0, l1, l2):
        ws = _group_weights(l0, l1, l2)
        dws = [jnp.sum(d * o, axis=-1, keepdims=True) for o in (o0, o1, o2)]
        mean = ws[0] * dws[0] + ws[1] * dws[1] + ws[2] * dws[2]
        return tuple(w * d for w in ws) + tuple(-w * mean for w in ws)

    rows = DIL_GROUP_HEADS * SEQ
    res = _rowwise(fn, "dil_combine_bwd", rows, 1024,
                   [(d_out, HEAD_DIM, 0)] + [(o, HEAD_DIM, 0) for o in outs] + [(l, 1, 0) for l in lses], [],
                   [(HEAD_DIM, BF16)] * 3 + [(1, F32)] * 3)
    return res[:3], res[3:]


def _fox_qkv(proj):
    def fn(q, k, v):
        return q * QK_SCALE, k, v

    w = FOX_WIDTH
    return _rowwise(fn, "fox_cast", SEQ, 256,
                    [(proj, w, COL_QB // w), (proj, w, COL_KB // w), (proj, w, COL_VB // w)], [], [(w, BF16)] * 3)


def _split3(x):
    hi = x.astype(BF16)
    r1 = x - hi.astype(F32)
    mid = r1.astype(BF16)
    lo = (r1 - mid.astype(F32)).astype(BF16)
    return hi, mid, lo


def _log1p(e):
    u = 1.0 + e
    return jnp.where(u == 1.0, e, jnp.log(u) * (e / (u - 1.0)))


def _fox_gate(proj, b_pad):
    def body(f_ref, b_ref, o_ref):
        row = lax.broadcasted_iota(jnp.int32, (BLOCK, BLOCK), 0)
        col = lax.broadcasted_iota(jnp.int32, (BLOCK, BLOCK), 1)
        tri = jnp.where(col <= row, 1.0, 0.0).astype(BF16)
        carry = jnp.zeros((1, LANES), F32)
        for blk in range(N_BLOCKS):
            z = f_ref[blk * BLOCK:(blk + 1) * BLOCK, :] + b_ref[...]
            logf = jnp.minimum(z, 0.0) - _log1p(jnp.exp(-jnp.abs(z)))
            hi, mid, lo = _split3(logf)
            run = (_dot_nn(tri, hi) + _dot_nn(tri, mid)) + _dot_nn(tri, lo) + carry
            o_ref[blk * BLOCK:(blk + 1) * BLOCK, :] = run
            carry = run[BLOCK - 1:BLOCK, :]

    return pl.pallas_call(
        body, name="fox_gate", grid=(1,),
        in_specs=[pl.BlockSpec((SEQ, LANES), lambda i: (0, COL_F // LANES)), pl.BlockSpec((1, LANES), lambda i: (0, 0))],
        out_specs=pl.BlockSpec((SEQ, LANES), lambda i: (0, 0)),
        out_shape=jax.ShapeDtypeStruct((SEQ, LANES), F32),
    )(proj, b_pad)


def _fox_gate_bwd(d_cum, proj, b_pad):
    def body(d_ref, f_ref, b_ref, dz_ref, db_ref):
        row = lax.broadcasted_iota(jnp.int32, (BLOCK, BLOCK), 0)
        col = lax.broadcasted_iota(jnp.int32, (BLOCK, BLOCK), 1)
        tri = jnp.where(col >= row, 1.0, 0.0).astype(BF16)
        carry = jnp.zeros((1, LANES), F32)
        db = jnp.zeros((1, LANES), F32)
        for blk in reversed(range(N_BLOCKS)):
            hi, mid, lo = _split3(d_ref[blk * BLOCK:(blk + 1) * BLOCK, :])
            run = (_dot_nn(tri, hi) + _dot_nn(tri, mid)) + _dot_nn(tri, lo) + carry
            carry = run[0:1, :]
            z = f_ref[blk * BLOCK:(blk + 1) * BLOCK, :] + b_ref[...]
            dz = run * _sigmoid(-z)
            dz_ref[blk * BLOCK:(blk + 1) * BLOCK, :] = dz.astype(BF16)
            db = db + jnp.sum(dz, axis=0, keepdims=True)
        db_ref[...] = db

    return pl.pallas_call(
        body, name="fox_gate_bwd", grid=(1,),
        in_specs=[pl.BlockSpec((SEQ, LANES), lambda i: (0, 0)),
                  pl.BlockSpec((SEQ, LANES), lambda i: (0, COL_F // LANES)), pl.BlockSpec((1, LANES), lambda i: (0, 0))],
        out_specs=[pl.BlockSpec((SEQ, LANES), lambda i: (0, 0)), pl.BlockSpec((1, LANES), lambda i: (0, 0))],
        out_shape=[jax.ShapeDtypeStruct((SEQ, LANES), BF16), jax.ShapeDtypeStruct((1, LANES), F32)],
    )(d_cum, proj, b_pad)


FOX_TILE = 256


def _fox_fwd(q, k, v, f_col, f_row):
    t = FOX_TILE

    def body(q_ref, k_ref, v_ref, fc_ref, fr_ref, o_ref, lse_ref):
        i = pl.program_id(1)
        row = lax.broadcasted_iota(jnp.int32, (t, SEQ), 0) + i * t
        col = lax.broadcasted_iota(jnp.int32, (t, SEQ), 1)
        s = _dot_nt(q_ref[0], k_ref[0]) + (fc_ref[0] - fr_ref[0])
        s = jnp.where(col <= row, s, NEG_INF)
        m = jnp.max(s, axis=-1, keepdims=True)
        p = jnp.exp(s - m)
        den = jnp.sum(p, axis=-1, keepdims=True)
        o_ref[0] = _dot_nn((p / den).astype(BF16), v_ref[0])
        lse_ref[0] = m + jnp.log(den)

    tile = pl.BlockSpec((1, t, HEAD_DIM), lambda h, i: (h, i, 0))
    full = pl.BlockSpec((1, SEQ, HEAD_DIM), lambda h, i: (h, 0, 0))
    col_t = pl.BlockSpec((1, t, 1), lambda h, i: (h, i, 0))
    row_f = pl.BlockSpec((1, 1, SEQ), lambda h, i: (h, 0, 0))
    return pl.pallas_call(
        body, name="fox_attn_fwd", grid=(N_FOX_HEADS, SEQ // t),
        in_specs=[tile, full, full, col_t, row_f], out_specs=[tile, col_t],
        out_shape=[jax.ShapeDtypeStruct((N_FOX_HEADS, SEQ, HEAD_DIM), F32),
                   jax.ShapeDtypeStruct((N_FOX_HEADS, SEQ, 1), F32)],
        compiler_params=pltpu.CompilerParams(
            dimension_semantics=("parallel", "parallel"), vmem_limit_bytes=_vmem_limit(8 * t * SEQ * 4)),
    )(q, k, v, f_col, f_row)


def _fox_bwd_q(q, k, v, do, lse_c, f_col, f_row):
    t = FOX_TILE

    def body(q_ref, k_ref, v_ref, do_ref, lse_ref, fc_ref, fr_ref, dq_ref, delta_ref):
        i = pl.program_id(1)
        row = lax.broadcasted_iota(jnp.int32, (t, SEQ), 0) + i * t
        col = lax.broadcasted_iota(jnp.int32, (t, SEQ), 1)
        s = _dot_nt(q_ref[0], k_ref[0]) + (fc_ref[0] - fr_ref[0])
        p = jnp.where(col <= row, jnp.exp(s - lse_ref[0]), 0.0)
        dp = _dot_nt(do_ref[0], v_ref[0])
        delta = jnp.sum(p * dp, axis=-1, keepdims=True)
        ds = p * (dp - delta)
        dq_ref[0] = _dot_nn(ds.astype(BF16), k_ref[0]) * QK_SCALE
        delta_ref[0] = delta

    tile = pl.BlockSpec((1, t, HEAD_DIM), lambda h, i: (h, i, 0))
    full = pl.BlockSpec((1, SEQ, HEAD_DIM), lambda h, i: (h, 0, 0))
    col_t = pl.BlockSpec((1, t, 1), lambda h, i: (h, i, 0))
    row_f = pl.BlockSpec((1, 1, SEQ), lambda h, i: (h, 0, 0))
    return pl.pallas_call(
        body, name="fox_attn_bwd_q", grid=(N_FOX_HEADS, SEQ // t),
        in_specs=[tile, full, full, tile, col_t, col_t, row_f], out_specs=[tile, col_t],
        out_shape=[jax.ShapeDtypeStruct((N_FOX_HEADS, SEQ, HEAD_DIM), F32),
                   jax.ShapeDtypeStruct((N_FOX_HEADS, SEQ, 1), F32)],
        compiler_params=pltpu.CompilerParams(
            dimension_semantics=("parallel", "parallel"), vmem_limit_bytes=_vmem_limit(8 * t * SEQ * 4)),
    )(q, k, v, do, lse_c, f_col, f_row)


def _fox_bwd_kv(q, k, v, do, lse_r, delta_r, f_col, f_row):
    t = FOX_TILE

    def body(q_ref, k_ref, v_ref, do_ref, lse_ref, delta_ref, fc_ref, fr_ref, dk_ref, dv_ref, df_ref):
        i = pl.program_id(1)
        key = lax.broadcasted_iota(jnp.int32, (t, SEQ), 0) + i * t
        qry = lax.broadcasted_iota(jnp.int32, (t, SEQ), 1)
        s = _dot_nt(k_ref[0], q_ref[0]) + (fr_ref[0] - fc_ref[0])
        p = jnp.where(key <= qry, jnp.exp(s - lse_ref[0]), 0.0)
        ds = p * (_dot_nt(v_ref[0], do_ref[0]) - delta_ref[0])
        dv_ref[0] = _dot_nn(p.astype(BF16), do_ref[0])
        dk_ref[0] = _dot_nn(ds.astype(BF16), q_ref[0])
        df_ref[0] = -jnp.sum(ds, axis=-1, keepdims=True)

    tile = pl.BlockSpec((1, t, HEAD_DIM), lambda h, i: (h, i, 0))
    full = pl.BlockSpec((1, SEQ, HEAD_DIM), lambda h, i: (h, 0, 0))
    col_t = pl.BlockSpec((1, t, 1), lambda h, i: (h, i, 0))
    row_f = pl.BlockSpec((1, 1, SEQ), lambda h, i: (h, 0, 0))
    shape = jax.ShapeDtypeStruct((N_FOX_HEADS, SEQ, HEAD_DIM), F32)
    return pl.pallas_call(
        body, name="fox_attn_bwd_kv", grid=(N_FOX_HEADS, SEQ // t),
        in_specs=[full, tile, tile, full, row_f, row_f, col_t, row_f], out_specs=[tile, tile, col_t],
        out_shape=[shape, shape, jax.ShapeDtypeStruct((N_FOX_HEADS, SEQ, 1), F32)],
        compiler_params=pltpu.CompilerParams(
            dimension_semantics=("parallel", "parallel"), vmem_limit_bytes=_vmem_limit(8 * t * SEQ * 4)),
    )(q, k, v, do, lse_r, delta_r, f_col, f_row)


MIX_TILE = 256


def _mix_out(out_a, out_b, proj, x, w_pa, w_pb, w_out, g_post, g_ffn_pre):
    tm = MIX_TILE

    def body(a_ref, b_ref, ga_ref, gb_ref, x_ref, wpa_ref, wpb_ref, wo_ref, g2_ref, g3_ref,
             merged_ref, mix_ref, x1_ref, h2_ref):
        ya = _dot_nn(a_ref[...], wpa_ref[...])
        yb = _dot_nn(b_ref[...], wpb_ref[...])
        merged = (_sigmoid(ga_ref[...]) * ya + _sigmoid(gb_ref[...]) * yb).astype(BF16)
        merged_ref[...] = merged
        mix = _dot_nn(merged, wo_ref[...])
        mix_ref[...] = mix
        x1 = x_ref[...] + mix * _rms_scale(mix) * g2_ref[...]
        x1_ref[...] = x1
        h2_ref[...] = (x1 * _rms_scale(x1) * g3_ref[...]).astype(BF16)

    def rows(w, cb=0):
        return pl.BlockSpec((tm, w), lambda i, cb=cb: (i, cb))

    def whole(a):
        return pl.BlockSpec(a.shape, lambda i: (0, 0))

    d = D_MODEL
    blk = _nbytes((tm, d), F32) * 6 + sum(_nbytes(a.shape, BF16) for a in (w_pa, w_pb, w_out))
    return pl.pallas_call(
        body, name="mix_out", grid=(SEQ // tm,),
        in_specs=[rows(DIL_OUT_WIDTH), rows(FOX_WIDTH), rows(d, COL_GA // d), rows(d, COL_GB // d), rows(d),
                  whole(w_pa), whole(w_pb), whole(w_out), whole(g_post), whole(g_ffn_pre)],
        out_specs=[rows(d)] * 4,
        out_shape=[jax.ShapeDtypeStruct((SEQ, d), dt) for dt in (BF16, F32, F32, BF16)],
        compiler_params=pltpu.CompilerParams(dimension_semantics=("parallel",), vmem_limit_bytes=_vmem_limit(blk)),
    )(out_a, out_b, proj, proj, x, w_pa, w_pb, w_out, g_post, g_ffn_pre)


def _mix_out_bwd(dmix, out_a, out_b, proj, w_pa, w_pb, w_out):
    tm = MIX_TILE

    def body(dm_ref, a_ref, b_ref, ga_ref, gb_ref, wpa_ref, wpb_ref, wo_ref,
             dga_ref, dgb_ref, dya_ref, dyb_ref, da_ref, db_ref):
        dmerged = _dot_nt(dm_ref[...], wo_ref[...])
        ya = _dot_nn(a_ref[...], wpa_ref[...])
        yb = _dot_nn(b_ref[...], wpb_ref[...])
        sa, sb = _sigmoid(ga_ref[...]), _sigmoid(gb_ref[...])
        dga_ref[...] = (dmerged * ya * (sa * (1.0 - sa))).astype(BF16)
        dgb_ref[...] = (dmerged * yb * (sb * (1.0 - sb))).astype(BF16)
        dya = (dmerged * sa).astype(BF16)
        dyb = (dmerged * sb).astype(BF16)
        dya_ref[...] = dya
        dyb_ref[...] = dyb
        da_ref[...] = _dot_nt(dya, wpa_ref[...])
        db_ref[...] = _dot_nt(dyb, wpb_ref[...]).astype(BF16)

    def rows(w, cb=0):
        return pl.BlockSpec((tm, w), lambda i, cb=cb: (i, cb))

    def whole(a):
        return pl.BlockSpec(a.shape, lambda i: (0, 0))

    d = D_MODEL
    blk = _nbytes((tm, d), F32) * 8 + sum(_nbytes(a.shape, BF16) for a in (w_pa, w_pb, w_out))
    return pl.pallas_call(
        body, name="mix_out_bwd", grid=(SEQ // tm,),
        in_specs=[rows(d), rows(DIL_OUT_WIDTH), rows(FOX_WIDTH), rows(d, COL_GA // d), rows(d, COL_GB // d),
                  whole(w_pa), whole(w_pb), whole(w_out)],
        out_specs=[rows(d)] * 4 + [rows(DIL_OUT_WIDTH), rows(FOX_WIDTH)],
        out_shape=[jax.ShapeDtypeStruct((SEQ, d), BF16)] * 4
        + [jax.ShapeDtypeStruct((SEQ, DIL_OUT_WIDTH), F32), jax.ShapeDtypeStruct((SEQ, FOX_WIDTH), BF16)],
        compiler_params=pltpu.CompilerParams(dimension_semantics=("parallel",), vmem_limit_bytes=_vmem_limit(blk)),
    )(dmix, out_a, out_b, proj, proj, w_pa, w_pb, w_out)


FFN_TM, FFN_TN = 1024, 256


def _ffn_up(h2, w_gate, w_up):
    tm, tn = FFN_TM, FFN_TN

    def body(h_ref, wg_ref, wu_ref, gate_ref, up_ref, act_ref):
        gate = _dot_nn(h_ref[...], wg_ref[...])
        up = _dot_nn(h_ref[...], wu_ref[...])
        gate_ref[...] = gate
        up_ref[...] = up
        act_ref[...] = (gate * _sigmoid(gate) * up).astype(BF16)

    tile = pl.BlockSpec((tm, tn), lambda i, j: (i, j))
    w_spec = pl.BlockSpec((D_MODEL, tn), lambda i, j: (0, j))
    return pl.pallas_call(
        body, name="ffn_up", grid=(SEQ // tm, D_FF // tn),
        in_specs=[pl.BlockSpec((tm, D_MODEL), lambda i, j: (i, 0)), w_spec, w_spec],
        out_specs=[tile, tile, tile],
        out_shape=[jax.ShapeDtypeStruct((SEQ, D_FF), dt) for dt in (F32, F32, BF16)],
        compiler_params=pltpu.CompilerParams(
            dimension_semantics=("parallel", "parallel"), vmem_limit_bytes=_vmem_limit(8 * 2**20)),
    )(h2, w_gate, w_up)


def _ffn_act_bwd(dff, w_down, gate, up):
    tm, tn = FFN_TM, FFN_TN

    def body(d_ref, wd_ref, gate_ref, up_ref, dgate_ref, dup_ref):
        dact = _dot_nt(d_ref[...], wd_ref[...])
        gate = gate_ref[...]
        sg = _sigmoid(gate)
        dgate_ref[...] = (dact * up_ref[...] * (sg * (1.0 + gate * (1.0 - sg)))).astype(BF16)
        dup_ref[...] = (dact * (gate * sg)).astype(BF16)

    tile = pl.BlockSpec((tm, tn), lambda i, j: (i, j))
    return pl.pallas_call(
        body, name="ffn_act_bwd", grid=(SEQ // tm, D_FF // tn),
        in_specs=[pl.BlockSpec((tm, D_MODEL), lambda i, j: (i, 0)), pl.BlockSpec((tn, D_MODEL), lambda i, j: (j, 0)),
                  tile, tile],
        out_specs=[tile, tile],
        out_shape=[jax.ShapeDtypeStruct((SEQ, D_FF), BF16)] * 2,
        compiler_params=pltpu.CompilerParams(
            dimension_semantics=("parallel", "parallel"), vmem_limit_bytes=_vmem_limit(8 * 2**20)),
    )(dff, w_down, gate, up)


def _loss_head(ff, x1, target, g_post):
    def fn(ff, x1, tgt, g):
        r = _rms_scale(ff)
        nrm = ff * r
        err = (x1 + nrm * g) - tgt
        loss = 0.5 * jnp.sum(jnp.mean(err * err, axis=-1, keepdims=True), axis=0, keepdims=True)
        dy = err * (1.0 / D_MODEL)
        u = dy * g
        dff = r * u - ff * (r * r * r) * jnp.mean(u * ff, axis=-1, keepdims=True)
        return dy, dff, jnp.broadcast_to(loss, (1, LANES)), jnp.sum(dy * nrm, axis=0, keepdims=True)

    d = D_MODEL
    return _rowwise(fn, "loss_head", SEQ, 256, [(ff, d, 0), (x1, d, 0), (target, d, 0)], [g_post],
                    [(d, F32), (d, BF16)], [LANES, d])


def _post_ffn_bwd(dh2, x1, dy, mix, g_ffn_pre, g_mix_post):
    def fn(dh2, x1, dy, mix, g3, g2):
        dx, dg3 = _rms_bwd(x1, dh2, g3)
        dx1 = dy + dx
        dmix, dg2 = _rms_bwd(mix, dx1, g2)
        return dx1, dmix, dg3, dg2

    d = D_MODEL
    return _rowwise(fn, "post_ffn_bwd", SEQ, 256, [(dh2, d, 0), (x1, d, 0), (dy, d, 0), (mix, d, 0)],
                    [g_ffn_pre, g_mix_post], [(d, F32), (d, BF16)], [d, d])


def _input_bwd(dh, x, dx1, g_pre):
    def fn(dh, x, dx1, g):
        dx, dg = _rms_bwd(x, dh, g)
        return dx1 + dx, dg

    d = D_MODEL
    return _rowwise(fn, "input_bwd", SEQ, 256, [(dh, d, 0), (x, d, 0), (dx1, d, 0)], [g_pre], [(d, F32)], [d])


def _adam_math(w, g, m, v):
    m = ADAM_B1 * m + (1.0 - ADAM_B1) * g
    v = ADAM_B2 * v + (1.0 - ADAM_B2) * (g * g)
    m_hat = m / (1.0 - ADAM_B1 ** ADAM_STEP)
    v_hat = v / (1.0 - ADAM_B2 ** ADAM_STEP)
    delta = -ADAM_LR * (m_hat / (jnp.sqrt(v_hat) + ADAM_EPS) + ADAM_WD * w)
    return delta, m, v


def _adam(w, g, m, v, name):
    rows, cols = w.shape
    tm = next(t for t in (256, 128, 64, 32, 16, 8) if rows % t == 0)
    return _rowwise(_adam_math, name, rows, tm, [(a, cols, 0) for a in (w, g, m, v)], [], [(cols, F32)] * 3)


def _adam_small(gathered, w, m, v):
    def body(ga_ref, w_ref, m_ref, v_ref, g_ref, d_ref, nm_ref, nv_ref):
        g = ga_ref[0]
        for dev in range(1, N_DEV):
            g = g + ga_ref[dev]
        g_ref[...] = g
        d_ref[...], nm_ref[...], nv_ref[...] = _adam_math(w_ref[...], g, m_ref[...], v_ref[...])

    shape = jax.ShapeDtypeStruct(w.shape, F32)
    return pl.pallas_call(body, name="adam_small", out_shape=[shape] * 4)(gathered, w, m, v)


_PACK = (("w_in", IN_SHARD), ("w_proj_a", 32), ("w_proj_b", 64), ("w_out", 128),
         ("w_ffn_gate", FF_SHARD), ("w_ffn_up", FF_SHARD), ("w_ffn_down", FF_SHARD))


def _pack_offsets(align):
    offs, o = {}, 0
    for name, rows in _PACK:
        offs[name] = o
        o += -(-rows // align) * align
    assert o <= PACK_ROWS
    return offs


def _pack(parts, align, dtype):
    offs = _pack_offsets(align)
    lead = next(iter(parts.values())).shape[:-2]
    pieces, at = [], 0
    for name, rows in _PACK:
        if offs[name] > at:
            pieces.append(jnp.zeros(lead + (offs[name] - at, PACK_COLS), dtype))
        pieces.append(parts[name].astype(dtype))
        at = offs[name] + rows
    pieces.append(jnp.zeros(lead + (PACK_ROWS - at, PACK_COLS), dtype))
    return jnp.concatenate(pieces, axis=-2)


def _unpack(packed, align):
    offs = _pack_offsets(align)
    return {name: packed[..., offs[name]:offs[name] + rows, :] for name, rows in _PACK}


def _cols_to_slots(full):
    r, c = full.shape
    n = c // N_DEV
    return full.reshape(r, N_DEV, n).transpose(1, 0, 2).reshape(N_DEV, r * n // PACK_COLS, PACK_COLS)


def _slots_to_cols(slots, r):
    n = slots.shape[1] * PACK_COLS // r
    return slots.reshape(N_DEV, r, n).transpose(1, 0, 2).reshape(r, N_DEV * n)


def _rows_to_slots(full):
    r, c = full.shape
    return full.reshape(N_DEV, r // N_DEV, c)


def _proj_weight(w_in_full):
    w = w_in_full
    z = lambda n: jnp.zeros((D_MODEL, n), w.dtype)
    return jnp.concatenate([w[:, 3848:5896], z(256), w[:, 0:3840], w[:, 3840:3848], z(PROJ_COLS - COL_F - 8)], axis=1)


def _proj_weight_grad(dw_r):
    return jnp.concatenate([dw_r[:, COL_QA:COL_F], dw_r[:, COL_F:COL_F + 8], dw_r[:, 0:2048]], axis=1)


def kernel(x, w_in, w_proj_a, w_proj_b, w_out, b_forget, w_ffn_gate, w_ffn_up, w_ffn_down, norm_mix_pre, norm_mix_post, norm_ffn_pre, norm_ffn_post, loss_target, m_w_in, m_w_proj_a, m_w_proj_b, m_w_out, m_b_forget, m_w_ffn_gate, m_w_ffn_up, m_w_ffn_down, m_norm_mix_pre, m_norm_mix_post, m_norm_ffn_pre, m_norm_ffn_post, v_w_in, v_w_proj_a, v_w_proj_b, v_w_out, v_b_forget, v_w_ffn_gate, v_w_ffn_up, v_w_ffn_down, v_norm_mix_pre, v_norm_mix_post, v_norm_ffn_pre, v_norm_ffn_post):
    d = D_MODEL
    shards = {"w_in": w_in[0], "w_proj_a": w_proj_a[0], "w_proj_b": w_proj_b[0], "w_out": w_out[0],
              "w_ffn_gate": w_ffn_gate[0], "w_ffn_up": w_ffn_up[0], "w_ffn_down": w_ffn_down[0]}
    moments_m = {"w_in": m_w_in[0], "w_proj_a": m_w_proj_a[0], "w_proj_b": m_w_proj_b[0], "w_out": m_w_out[0],
                 "w_ffn_gate": m_w_ffn_gate[0], "w_ffn_up": m_w_ffn_up[0], "w_ffn_down": m_w_ffn_down[0]}
    moments_v = {"w_in": v_w_in[0], "w_proj_a": v_w_proj_a[0], "w_proj_b": v_w_proj_b[0], "w_out": v_w_out[0],
                 "w_ffn_gate": v_w_ffn_gate[0], "w_ffn_up": v_w_ffn_up[0], "w_ffn_down": v_w_ffn_down[0]}
    core = lax.axis_index("c").astype(jnp.int32).reshape(1)
    chip = (2 * lax.axis_index("x") + lax.axis_index("y")).astype(jnp.int32).reshape(1)
    x2, target = x[0], loss_target[0]

    packed = _pack({k: a.reshape(-1, PACK_COLS) for k, a in shards.items()}, 16, BF16)
    gathered = _unpack(_all_gather(packed, "weights_all_gather", 4), 16)
    w_r = _proj_weight(_slots_to_cols(gathered["w_in"], d))
    w_pa = _slots_to_cols(gathered["w_proj_a"], DIL_OUT_WIDTH)
    w_pb = _slots_to_cols(gathered["w_proj_b"], FOX_WIDTH)
    w_o = gathered["w_out"].reshape(d, d)
    w_g = _slots_to_cols(gathered["w_ffn_gate"], d)
    w_u = _slots_to_cols(gathered["w_ffn_up"], d)
    w_d = gathered["w_ffn_down"].reshape(D_FF, d)

    h = _rowwise(lambda xb, g: xb * _rms_scale(xb) * g, "norm_mix_pre", SEQ, 256, [(x2, d, 0)], [norm_mix_pre],
                 [(d, BF16)])[0]
    proj = _matmul([(h, w_r)], "nn", F32, "in_proj", 1024, 896, 1024)
    tables = _rope_tables()
    qa, ka, va = (_dil_permute(_to_heads(a, N_DIL_HEADS), _residue_major) for a in _dil_qkv(proj, tables))
    o_dil, lse_dil = _dil_fwd(qa, ka, va)
    o_tok = _dil_permute(o_dil, _token_major)
    lse_tok = _dil_permute(lse_dil, _token_major)
    g4 = DIL_GROUP_HEADS
    o_groups = [o_tok[i * g4:(i + 1) * g4].reshape(g4 * SEQ, HEAD_DIM) for i in range(3)]
    lse_groups = [lse_tok[i * g4:(i + 1) * g4].reshape(g4 * SEQ, 1) for i in range(3)]
    out_a = _from_heads(_dil_combine(o_groups, lse_groups).reshape(g4, SEQ, HEAD_DIM)).astype(BF16)

    b_pad = jnp.pad(b_forget, ((0, 0), (0, LANES - N_FOX_HEADS)))
    f_cum = _fox_gate(proj, b_pad)
    f_col = f_cum[:, :N_FOX_HEADS].T.reshape(N_FOX_HEADS, SEQ, 1)
    f_row = f_col.reshape(N_FOX_HEADS, 1, SEQ)
    qb, kb, vb = (_to_heads(a, N_FOX_HEADS) for a in _fox_qkv(proj))
    o_fox, lse_fox = _fox_fwd(qb, kb, vb, f_col, f_row)
    out_b = _from_heads(o_fox).astype(BF16)

    merged, mix, x1, h2 = _mix_out(out_a, out_b, proj, x2, w_pa, w_pb, w_o, norm_mix_post, norm_ffn_pre)

    gate, up, act = _ffn_up(h2, w_g, w_u)
    ff = _matmul([(act, w_d)], "nn", F32, "ffn_down", 1024, 1024, 1408)
    dy, dff, loss_part, dg_ffn_post = _loss_head(ff, x1, target, norm_ffn_post)
    loss = lax.psum(loss_part[0, 0], ("x", "y", "c"))

    dgate, dup = _ffn_act_bwd(dff, w_d, gate, up)
    dw_down = _matmul([(act, dff)], "tn", F32, "grad_w_ffn_down", 1408, 1024, 1024)
    dw_gate = _matmul([(h2, dgate)], "tn", F32, "grad_w_ffn_gate", 1024, 1408, 1024)
    dw_up = _matmul([(h2, dup)], "tn", F32, "grad_w_ffn_up", 1024, 1408, 1024)
    dh2 = _matmul([(dgate, w_g), (dup, w_u)], "nt", F32, "ffn_up_bwd", 1024, 1024, 1408)
    dx1, dmix, dg_ffn_pre, dg_mix_post = _post_ffn_bwd(dh2, x1, dy, mix, norm_ffn_pre, norm_mix_post)

    dga, dgb, dya, dyb, d_out_a, d_out_b = _mix_out_bwd(dmix, out_a, out_b, proj, w_pa, w_pb, w_o)
    dw_out = _matmul([(merged, dmix)], "tn", F32, "grad_w_out", 1024, 1024, 1024)
    dw_pa = _matmul([(out_a, dya)], "tn", F32, "grad_w_proj_a", DIL_OUT_WIDTH, 1024, SEQ)
    dw_pb = _matmul([(out_b, dyb)], "tn", F32, "grad_w_proj_b", FOX_WIDTH, 1024, SEQ)

    do_fox = _to_heads(d_out_b, N_FOX_HEADS)
    lse_fox_row = lse_fox.reshape(N_FOX_HEADS, 1, SEQ)
    dq_fox, delta_fox = _fox_bwd_q(qb, kb, vb, do_fox, lse_fox, f_col, f_row)
    dk_fox, dv_fox, d_cum = _fox_bwd_kv(qb, kb, vb, do_fox, lse_fox_row, delta_fox.reshape(N_FOX_HEADS, 1, SEQ),
                                        f_col, f_row)
    d_cum_pad = jnp.pad(d_cum.reshape(N_FOX_HEADS, SEQ).T, ((0, 0), (0, LANES - N_FOX_HEADS)))
    dz, db_part = _fox_gate_bwd(d_cum_pad, proj, b_pad)
    dqb, dkb, dvb = (_from_heads(a).astype(BF16) for a in (dq_fox, dk_fox, dv_fox))

    d_out_a_heads = _to_heads(d_out_a, g4).reshape(g4 * SEQ, HEAD_DIM)
    do_groups, c_groups = _dil_combine_bwd(d_out_a_heads, o_groups, lse_groups)
    do_dil = _dil_permute(jnp.concatenate([a.reshape(g4, SEQ, HEAD_DIM) for a in do_groups], axis=0), _residue_major)
    c_dil = _dil_permute(jnp.concatenate([a.reshape(g4, SEQ, 1) for a in c_groups], axis=0), _residue_major)
    dq_dil, dk_dil, dv_dil = _dil_bwd(
        qa, ka, va, do_dil, lse_dil, lse_dil.reshape(N_DIL_HEADS, 1, SEQ), c_dil, c_dil.reshape(N_DIL_HEADS, 1, SEQ))
    dq_tok, dk_tok, dv_tok = (_from_heads(_dil_permute(a, _token_major)) for a in (dq_dil, dk_dil, dv_dil))
    dqa, dka = _dil_qk_bwd(dq_tok, dk_tok, tables)
    dva = dv_tok.astype(BF16)

    dproj = jnp.concatenate(
        [dga, dgb, jnp.zeros((SEQ, COL_QA - 2 * d), BF16), dqa, dka, dva, dqb, dkb, dvb, dz], axis=1)
    dw_r = _matmul([(h, dproj)], "tn", F32, "grad_w_in", 1024, 896, 1024)
    dh = _matmul([(dproj, w_r)], "nt", F32, "in_proj_bwd", 1024, 1024, 896)
    grad_x, dg_mix_pre = _input_bwd(dh, x2, dx1, norm_mix_pre)

    slots = {"w_in": _cols_to_slots(_proj_weight_grad(dw_r)), "w_proj_a": _cols_to_slots(dw_pa),
             "w_proj_b": _cols_to_slots(dw_pb), "w_out": _rows_to_slots(dw_out),
             "w_ffn_gate": _cols_to_slots(dw_gate), "w_ffn_up": _cols_to_slots(dw_up),
             "w_ffn_down": _rows_to_slots(dw_down)}
    reduced = _unpack(_reduce_scatter(_pack(slots, 8, F32), core, chip), 8)
    grads = {k: reduced[k].reshape(shards[k].shape) for k in shards}

    small_part = jnp.concatenate(
        [dg_mix_pre, dg_mix_post, dg_ffn_pre, dg_ffn_post, jnp.pad(db_part, ((0, 0), (0, d - LANES))),
         jnp.zeros((3, d), F32)], axis=0)
    small_all = _all_gather(small_part, "small_grads_all_gather", 1)

    def small_pack(gains, b):
        return jnp.concatenate(gains + [jnp.pad(b, ((0, 0), (0, d - N_FOX_HEADS))), jnp.zeros((3, d), F32)], axis=0)

    small_w = small_pack([norm_mix_pre, norm_mix_post, norm_ffn_pre, norm_ffn_post], b_forget)
    small_m = small_pack([m_norm_mix_pre, m_norm_mix_post, m_norm_ffn_pre, m_norm_ffn_post], m_b_forget)
    small_v = small_pack([v_norm_mix_pre, v_norm_mix_post, v_norm_ffn_pre, v_norm_ffn_post], v_b_forget)
    small = _adam_small(small_all, small_w, small_m, small_v)

    big = {k: _adam(shards[k], grads[k], moments_m[k], moments_v[k], "adam_" + k) for k in shards}

    def leaves(i):
        from_big = (grads if i == 0 else {k: big[k][i - 1] for k in shards})
        s = small[i]
        gains = [s[r:r + 1] for r in range(4)]
        return [from_big["w_in"][None], from_big["w_proj_a"][None], from_big["w_proj_b"][None], from_big["w_out"][None],
                s[4:5, :N_FOX_HEADS], from_big["w_ffn_gate"][None], from_big["w_ffn_up"][None],
                from_big["w_ffn_down"][None], *gains]

    return (loss, grad_x[None], *leaves(0), *leaves(1), *leaves(2), *leaves(3))
```

```python
import functools
import math

import jax
import jax.numpy as jnp
from jax import lax
from jax.experimental import pallas as pl
from jax.experimental.pallas import tpu as pltpu

F32 = jnp.float32
BF16 = jnp.bfloat16
MESH = pl.DeviceIdType.MESH

D_MODEL = 1024
SEQ = 2048
HEAD_DIM = 64
BLOCK = 128
N_BLOCKS = SEQ // BLOCK
DILATIONS = (1, 4, 16)
N_FOX_HEADS = 8
DIL_WIDTH = 768
DIL_OUT_WIDTH = 256
FOX_WIDTH = 512
D_FF = 2816
ROPE_THETA = 500000.0
ROPE_DIM = HEAD_DIM // 4
ROPE_HALF = ROPE_DIM // 2
EPS = 1e-6
NEG_INF = -1e30
QK_SCALE = 1.0 / math.sqrt(HEAD_DIM)
IN_COLS = 5896
N_DEV = 8
IN_SHARD = IN_COLS // N_DEV

ADAM_LR = 0.001
ADAM_B1 = 0.9
ADAM_B2 = 0.999
ADAM_EPS = 1e-08
ADAM_WD = 0.01
ADAM_STEP = 10

V7X_VMEM_BYTES = 64 * 2**20
LANES = 128
SUBLANES = 8

PROJ_COLS = 6272
COL_GA, COL_GB = 0, 1024
COL_QA, COL_KA, COL_VA = 2304, 3072, 3840
COL_QB, COL_KB, COL_VB = 4608, 5120, 5632
COL_F = 6144
F_ROWS = 16


def _vmem_limit(block_bytes):
    want = 2 * block_bytes + 16 * 2**20
    return int(min(max(want, 32 * 2**20), V7X_VMEM_BYTES - 8 * 2**20))


def _nbytes(shape, dtype):
    return math.prod(shape) * jnp.dtype(dtype).itemsize


def _dot(a, b, dims):
    return lax.dot_general(a, b, (dims, ((), ())), preferred_element_type=F32)


def _dot_nn(a, b):
    return _dot(a, b, ((1,), (0,)))


def _dot_nt(a, b):
    return _dot(a, b, ((1,), (1,)))


def _dot_tn(a, b):
    return _dot(a, b, ((0,), (0,)))


def _sigmoid(z):
    return 1.0 / (1.0 + jnp.exp(-z))


def _split3(x):
    hi = x.astype(BF16)
    r1 = x - hi.astype(F32)
    mid = r1.astype(BF16)
    lo = (r1 - mid.astype(F32)).astype(BF16)
    return hi, mid, lo


def _dot3_nn(x, ones_matrix):
    hi, mid, lo = _split3(x)
    return (_dot_nn(hi, ones_matrix) + _dot_nn(mid, ones_matrix)) + _dot_nn(lo, ones_matrix)


def _rowwise(fn, name, n_rows, tm, row_ins, bcast_ins, row_outs, acc_outs=()):
    n_in = len(row_ins) + len(bcast_ins)
    n_ro = len(row_outs)

    def body(*refs):
        res = fn(*[r[...] for r in refs[:n_in]])
        if not isinstance(res, (tuple, list)):
            res = (res,)
        outs = refs[n_in:]
        for r, o in zip(res[:n_ro], outs[:n_ro]):
            o[...] = r.astype(o.dtype)
        first = pl.program_id(0) == 0
        for r, o in zip(res[n_ro:], outs[n_ro:]):
            _accumulate(o, r, first)

    in_specs = [pl.BlockSpec((tm, w), lambda i, cb=cb: (i, cb)) for _, w, cb in row_ins]
    in_specs += [pl.BlockSpec(a.shape, lambda i: (0, 0)) for a in bcast_ins]
    out_specs = [pl.BlockSpec((tm, w), lambda i: (i, 0)) for w, _ in row_outs]
    out_specs += [pl.BlockSpec((1, w), lambda i: (0, 0)) for w in acc_outs]
    out_shape = [jax.ShapeDtypeStruct((n_rows, w), dt) for w, dt in row_outs]
    out_shape += [jax.ShapeDtypeStruct((1, w), F32) for w in acc_outs]
    blk = sum(_nbytes((tm, w), a.dtype) for a, w, _ in row_ins) + sum(_nbytes((tm, w), dt) for w, dt in row_outs)
    return pl.pallas_call(
        body, name=name, grid=(n_rows // tm,), in_specs=in_specs, out_specs=out_specs, out_shape=out_shape,
        compiler_params=pltpu.CompilerParams(
            dimension_semantics=("arbitrary" if acc_outs else "parallel",), vmem_limit_bytes=_vmem_limit(3 * blk)),
    )(*[a for a, _, _ in row_ins], *bcast_ins)


def _accumulate(o_ref, part, first):
    @pl.when(first)
    def _():
        o_ref[...] = part

    @pl.when(jnp.logical_not(first))
    def _():
        o_ref[...] += part


_MM_DIMS = {"nn": ((1,), (0,)), "nt": ((1,), (1,)), "tn": ((0,), (0,))}


def _matmul(pairs, mode, out_dtype, name, tm, tn, tk):
    a0, b0 = pairs[0]
    if mode == "tn":
        kk, m = a0.shape
    else:
        m, kk = a0.shape
    n = b0.shape[0] if mode == "nt" else b0.shape[1]
    assert m % tm == 0 and n % tn == 0 and kk % tk == 0, (name, m, n, kk)
    nk = kk // tk
    n_pairs = len(pairs)
    dims = _MM_DIMS[mode]

    def body(*refs):
        o_ref = refs[2 * n_pairs]
        part = None
        for p in range(n_pairs):
            d = _dot(refs[2 * p][...].astype(BF16), refs[2 * p + 1][...].astype(BF16), dims)
            part = d if part is None else part + d
        if nk == 1:
            o_ref[...] = part.astype(o_ref.dtype)
            return
        acc = refs[2 * n_pairs + 1]
        k = pl.program_id(2)

        @pl.when(k == 0)
        def _():
            acc[...] = part

        @pl.when(k > 0)
        def _():
            acc[...] += part

        @pl.when(k == nk - 1)
        def _():
            o_ref[...] = acc[...].astype(o_ref.dtype)

    if mode == "tn":
        a_spec = pl.BlockSpec((tk, tm), lambda i, j, k: (k, i))
    else:
        a_spec = pl.BlockSpec((tm, tk), lambda i, j, k: (i, k))
    if mode == "nt":
        b_spec = pl.BlockSpec((tn, tk), lambda i, j, k: (j, k))
    else:
        b_spec = pl.BlockSpec((tk, tn), lambda i, j, k: (k, j))
    blk = sum(_nbytes((tm, tk), a.dtype) + _nbytes((tk, tn), b.dtype) for a, b in pairs) + 2 * _nbytes((tm, tn), F32)
    flat = [a for pair in pairs for a in pair]
    return pl.pallas_call(
        body, name=name, grid=(m // tm, n // tn, nk),
        in_specs=[a_spec, b_spec] * n_pairs,
        out_specs=pl.BlockSpec((tm, tn), lambda i, j, k: (i, j)),
        out_shape=jax.ShapeDtypeStruct((m, n), out_dtype),
        scratch_shapes=[] if nk == 1 else [pltpu.VMEM((tm, tn), F32)],
        compiler_params=pltpu.CompilerParams(
            dimension_semantics=("parallel", "parallel", "arbitrary"), vmem_limit_bytes=_vmem_limit(blk)),
    )(*flat)


def _rms_scale(x):
    return lax.rsqrt(jnp.mean(x * x, axis=-1, keepdims=True) + EPS)


def _rms_bwd(xin, dyn, g):
    r = _rms_scale(xin)
    u = dyn * g
    dx = r * u - xin * (r * r * r) * jnp.mean(u * xin, axis=-1, keepdims=True)
    dg = jnp.sum(dyn * xin * r, axis=0, keepdims=True)
    return dx, dg


def _mesh_pos():
    return lax.axis_index("x"), lax.axis_index("y"), lax.axis_index("c")


def _all_gather(xs, name):
    n = len(xs)

    def body(*refs):
        x_refs, out_refs = refs[:n], refs[n:2 * n]
        send_sems, recv_sems, local_sems = refs[2 * n:]
        mx, my, mc = _mesh_pos()
        me, sib = (mx, my, mc), (mx, my, 1 - mc)
        chips = [(1 - mx, my), (mx, 1 - my), (1 - mx, 1 - my)]

        def slot(a, dev):
            px, py, pc = dev
            return out_refs[a].at[4 * px + 2 * py + pc]

        def copy(k, a, block, to, src=None):
            return pltpu.make_async_remote_copy(
                src_ref=slot(a, block) if src is None else src, dst_ref=slot(a, block),
                send_sem=send_sems.at[a * 7 + k], recv_sem=recv_sems.at[a * 7 + k],
                device_id=to, device_id_type=MESH)

        mine = [pltpu.make_async_copy(x_refs[a], slot(a, me), local_sems.at[a]) for a in range(n)]
        for cp in mine:
            cp.start()
        first = []
        for a in range(n):
            first.append(copy(0, a, me, sib, x_refs[a]))
            first += [copy(1 + j, a, me, (*chip, mc), x_refs[a]) for j, chip in enumerate(chips)]
        for cp in first:
            cp.start()
        passed = []
        for a in range(n):
            for j, chip in enumerate(chips):
                copy(1 + j, a, (*chip, mc), me).wait_recv()
                fwd = copy(4 + j, a, (*chip, mc), sib)
                fwd.start()
                passed.append(fwd)
        for a in range(n):
            copy(0, a, sib, me).wait_recv()
            for j, chip in enumerate(chips):
                copy(4 + j, a, (*chip, 1 - mc), me).wait_recv()
        for cp in first + passed:
            cp.wait_send()
        for cp in mine:
            cp.wait()

    hbm = pl.BlockSpec(memory_space=pl.ANY)
    return pl.pallas_call(
        body, name=name,
        out_shape=[jax.ShapeDtypeStruct((N_DEV,) + x.shape, x.dtype) for x in xs],
        in_specs=[hbm] * n, out_specs=[hbm] * n,
        scratch_shapes=[pltpu.SemaphoreType.DMA((7 * n,)), pltpu.SemaphoreType.DMA((7 * n,)),
                        pltpu.SemaphoreType.DMA((n,))],
    )(*xs)


def _sibling_exchange(gs, name):
    n = len(gs)

    def body(*refs):
        g_refs, out_refs = refs[:n], refs[n:2 * n]
        send_sems, recv_sems = refs[2 * n:]
        mx, my, mc = _mesh_pos()
        cps = [pltpu.make_async_remote_copy(
            src_ref=g_refs[a].at[k, 1 - mc], dst_ref=out_refs[a].at[k],
            send_sem=send_sems.at[4 * a + k], recv_sem=recv_sems.at[4 * a + k],
            device_id=(mx, my, 1 - mc), device_id_type=MESH) for a in range(n) for k in range(4)]
        for cp in cps:
            cp.start()
        for cp in cps:
            cp.wait()

    hbm = pl.BlockSpec(memory_space=pl.ANY)
    return pl.pallas_call(
        body, name=name, out_shape=[jax.ShapeDtypeStruct((4,) + g.shape[2:], g.dtype) for g in gs],
        in_specs=[hbm] * n, out_specs=[hbm] * n,
        scratch_shapes=[pltpu.SemaphoreType.DMA((4 * n,)), pltpu.SemaphoreType.DMA((4 * n,))],
    )(*gs)


def _chip_exchange(ps, name):
    n = len(ps)

    def body(*refs):
        p_refs, out_refs = refs[:n], refs[n:2 * n]
        send_sems, recv_sems = refs[2 * n:]
        mx, my, mc = _mesh_pos()
        k0 = 2 * mx + my
        cps = []
        for a in range(n):
            for flip in (1, 2, 3):
                peer = (1 - mx if flip & 2 else mx, 1 - my if flip & 1 else my, mc)
                cps.append(pltpu.make_async_remote_copy(
                    src_ref=p_refs[a].at[jnp.bitwise_xor(k0, flip)], dst_ref=out_refs[a].at[flip - 1],
                    send_sem=send_sems.at[3 * a + flip - 1], recv_sem=recv_sems.at[3 * a + flip - 1],
                    device_id=peer, device_id_type=MESH))
        for cp in cps:
            cp.start()
        for cp in cps:
            cp.wait()

    hbm = pl.BlockSpec(memory_space=pl.ANY)
    return pl.pallas_call(
        body, name=name, out_shape=[jax.ShapeDtypeStruct((3,) + p.shape[1:], p.dtype) for p in ps],
        in_specs=[hbm] * n, out_specs=[hbm] * n,
        scratch_shapes=[pltpu.SemaphoreType.DMA((3 * n,)), pltpu.SemaphoreType.DMA((3 * n,))],
    )(*ps)


def _col_tile(r, c):
    return next(t for t in (1024, 512, 256, 128) if c % t == 0 and (r * t * 4 <= 2**20 or t == 128))


def _add_sibling(g4, recv, core, name):
    _, _, r, c = g4.shape
    tc = _col_tile(r, c)

    def body(core_ref, g_ref, r_ref, o16_ref, o32_ref):
        s = g_ref[0, 0] + r_ref[0]
        o16_ref[0] = s.astype(BF16)
        o32_ref[0] = s

    out = pl.BlockSpec((1, r, tc), lambda k, j, core_ref: (k, 0, j))
    return pl.pallas_call(
        body, name=name,
        out_shape=[jax.ShapeDtypeStruct((4, r, c), BF16), jax.ShapeDtypeStruct((4, r, c), F32)],
        grid_spec=pltpu.PrefetchScalarGridSpec(
            num_scalar_prefetch=1, grid=(4, c // tc),
            in_specs=[pl.BlockSpec((1, 1, r, tc), lambda k, j, core_ref: (k, core_ref[0], 0, j)), out],
            out_specs=[out, out]),
        compiler_params=pltpu.CompilerParams(dimension_semantics=("parallel", "parallel")),
    )(core, g4, recv)


def _add_chips(p32, recv, chip, name):
    _, r, c = p32.shape
    tc = _col_tile(r, c)

    def body(chip_ref, p_ref, r_ref, o_ref):
        o_ref[...] = ((p_ref[0] + r_ref[0].astype(F32)) + r_ref[1].astype(F32)) + r_ref[2].astype(F32)

    return pl.pallas_call(
        body, name=name, out_shape=jax.ShapeDtypeStruct((r, c), F32),
        grid_spec=pltpu.PrefetchScalarGridSpec(
            num_scalar_prefetch=1, grid=(c // tc,),
            in_specs=[pl.BlockSpec((1, r, tc), lambda j, chip_ref: (chip_ref[0], 0, j)),
                      pl.BlockSpec((3, r, tc), lambda j, chip_ref: (0, 0, j))],
            out_specs=pl.BlockSpec((r, tc), lambda j, chip_ref: (0, j))),
        compiler_params=pltpu.CompilerParams(dimension_semantics=("parallel",)),
    )(chip, p32, recv)


def _reduce_scatter(slots, core, chip):
    names = list(slots)
    g4s = [slots[k].reshape((4, 2) + slots[k].shape[1:]) for k in names]
    from_sibling = _sibling_exchange(g4s, "rs_sibling_exchange")
    partials = [_add_sibling(g4, rv, core, "rs_add_sibling_" + k) for k, g4, rv in zip(names, g4s, from_sibling)]
    from_chips = _chip_exchange([p16 for p16, _ in partials], "rs_chip_exchange")
    return {k: _add_chips(p32, rv, chip, "rs_add_chips_" + k)
            for k, (_, p32), rv in zip(names, partials, from_chips)}


def _rope_tables():
    positions = jnp.arange(SEQ, dtype=F32)
    inv_freq = jnp.power(ROPE_THETA, -jnp.arange(0, ROPE_DIM, 2, dtype=F32) / ROPE_DIM)
    ang = positions[:, None] * inv_freq[None, :]
    cos, sin = jnp.cos(ang), jnp.sin(ang)
    ones = jnp.ones((SEQ, HEAD_DIM - ROPE_DIM), F32)
    zeros8 = jnp.zeros((SEQ, ROPE_HALF), F32)
    zeros = jnp.zeros((SEQ, HEAD_DIM - ROPE_DIM), F32)
    c_head = jnp.concatenate([cos, cos, ones], axis=1)
    s1_head = jnp.concatenate([-sin, zeros8, zeros], axis=1)
    s2_head = jnp.concatenate([zeros8, sin, zeros], axis=1)
    return tuple(jnp.concatenate([t, t], axis=1) for t in (c_head, s1_head, s2_head))


def _tile_lanes(t, width):
    return jnp.concatenate([t] * (width // LANES), axis=1)


def _rope_apply(x, c, s1, s2):
    w = x.shape[1]
    return x * c + pltpu.roll(x, w - ROPE_HALF, 1) * s1 + pltpu.roll(x, ROPE_HALF, 1) * s2


def _rope_apply_t(dy, c, s1, s2):
    w = dy.shape[1]
    return dy * c + pltpu.roll(dy * s1, ROPE_HALF, 1) + pltpu.roll(dy * s2, w - ROPE_HALF, 1)


def _dil_qkv(proj, tables):
    def fn(q, k, v, c, s1, s2):
        c, s1, s2 = (_tile_lanes(t, DIL_WIDTH) for t in (c, s1, s2))
        return _rope_apply(q, c, s1, s2) * QK_SCALE, _rope_apply(k, c, s1, s2), v

    w = DIL_WIDTH
    return _rowwise(fn, "dil_rope", SEQ, 256,
                    [(proj, w, COL_QA // w), (proj, w, COL_KA // w), (proj, w, COL_VA // w)]
                    + [(t, LANES, 0) for t in tables], [],
                    [(w, BF16)] * 3)


def _dil_qk_bwd(dq, dk, tables):
    def fn(dq, dk, c, s1, s2):
        c, s1, s2 = (_tile_lanes(t, DIL_WIDTH) for t in (c, s1, s2))
        return _rope_apply_t(dq, c, s1, s2), _rope_apply_t(dk, c, s1, s2)

    w = DIL_WIDTH
    return _rowwise(fn, "dil_rope_bwd", SEQ, 256, [(dq, w, 0), (dk, w, 0)] + [(t, LANES, 0) for t in tables], [],
                    [(w, BF16)] * 2)


def _residue_major(a, front_pad=0):
    cols = []
    for g, d in enumerate(DILATIONS):
        part = a[:, g * 256:(g + 1) * 256]
        cols.append(part.reshape(SEQ // d, d, 256).transpose(1, 0, 2).reshape(SEQ, 256))
    out = jnp.concatenate(cols, axis=1)
    return jnp.pad(out, ((front_pad, 0), (0, 0))) if front_pad else out


def _token_major(a):
    cols = []
    for g, d in enumerate(DILATIONS):
        part = a[:, g * 256:(g + 1) * 256]
        cols.append(part.reshape(d, SEQ // d, 256).transpose(1, 0, 2).reshape(SEQ, 256))
    return jnp.concatenate(cols, axis=1)


def _dil_prev_limit(pair, n):
    g = pair // 2
    mask = jnp.where(g == 0, 15, jnp.where(g == 1, 3, 0))
    return jnp.where(jnp.bitwise_and(n, mask) != 0, 0, BLOCK)


def _dil_valid(limit):
    row = lax.broadcasted_iota(jnp.int32, (BLOCK, 2 * BLOCK), 0)
    col = lax.broadcasted_iota(jnp.int32, (BLOCK, 2 * BLOCK), 1)
    dist = col - row
    return jnp.logical_and(dist >= jnp.where(col < BLOCK, limit, -BLOCK), dist <= BLOCK)


def _upper_half():
    return lax.broadcasted_iota(jnp.int32, (1, LANES), 1) >= HEAD_DIM


def _dil_fwd(q, k, v):
    def body(q_ref, k_ref, v_ref, o_ref, lse_ref):
        pair = pl.program_id(0)
        upper = _upper_half()

        def block(n, carry):
            r0 = pl.multiple_of(n * BLOCK, BLOCK)
            qb = q_ref[pl.ds(r0, BLOCK), :]
            kw = k_ref[pl.ds(r0, 2 * BLOCK), :]
            vw = v_ref[pl.ds(r0, 2 * BLOCK), :]
            valid = _dil_valid(_dil_prev_limit(pair, n))
            outs, lses = [], []
            for head_mask in (jnp.logical_not(upper), upper):
                s = jnp.where(valid, _dot_nt(qb, jnp.where(head_mask, kw, 0)), NEG_INF)
                m = jnp.max(s, axis=-1, keepdims=True)
                p = jnp.exp(s - m)
                den = jnp.sum(p, axis=-1, keepdims=True)
                outs.append(_dot_nn((p / den).astype(BF16), jnp.where(head_mask, vw, 0)))
                lses.append(m + jnp.log(den))
            o_ref[pl.ds(r0, BLOCK), :] = outs[0] + outs[1]
            lse_ref[pl.ds(r0, BLOCK), :] = jnp.where(upper, lses[1], lses[0])
            return carry

        lax.fori_loop(0, N_BLOCKS, block, 0)

    return pl.pallas_call(
        body, name="dil_attn_fwd", grid=(DIL_WIDTH // LANES,),
        in_specs=[pl.BlockSpec((SEQ, LANES), lambda p: (0, p)), pl.BlockSpec((SEQ + BLOCK, LANES), lambda p: (0, p)),
                  pl.BlockSpec((SEQ + BLOCK, LANES), lambda p: (0, p))],
        out_specs=[pl.BlockSpec((SEQ, LANES), lambda p: (0, p))] * 2,
        out_shape=[jax.ShapeDtypeStruct((SEQ, DIL_WIDTH), F32)] * 2,
        compiler_params=pltpu.CompilerParams(dimension_semantics=("parallel",)),
    )(q, k, v)


def _dil_bwd(q, k, v, do, lse, c):
    def body(q_ref, k_ref, v_ref, do_ref, lse_ref, c_ref, dq_ref, dk_ref, dv_ref, dk_acc, dv_acc):
        pair = pl.program_id(0)
        upper = _upper_half()
        dk_acc[...] = jnp.zeros_like(dk_acc)
        dv_acc[...] = jnp.zeros_like(dv_acc)

        def block(n, carry):
            r0 = pl.multiple_of(n * BLOCK, BLOCK)
            qb, dob = q_ref[pl.ds(r0, BLOCK), :], do_ref[pl.ds(r0, BLOCK), :]
            kw = k_ref[pl.ds(r0, 2 * BLOCK), :]
            vw = v_ref[pl.ds(r0, 2 * BLOCK), :]
            lse_t, c_t = lse_ref[pl.ds(r0, BLOCK), :], c_ref[pl.ds(r0, BLOCK), :]
            valid = _dil_valid(_dil_prev_limit(pair, n))
            dq = None
            for e, head_mask in enumerate((jnp.logical_not(upper), upper)):
                km, vm = jnp.where(head_mask, kw, 0), jnp.where(head_mask, vw, 0)
                lse_col = lse_t[:, e * HEAD_DIM:e * HEAD_DIM + 1]
                c_col = c_t[:, e * HEAD_DIM:e * HEAD_DIM + 1]
                p = jnp.where(valid, jnp.exp(_dot_nt(qb, km) - lse_col), 0.0)
                ds = (p * (_dot_nt(dob, vm) + c_col)).astype(BF16)
                part = _dot_nn(ds, km)
                dq = part if dq is None else dq + part
                dk_acc[pl.ds(r0, 2 * BLOCK), :] += _dot_tn(ds, jnp.where(head_mask, qb, 0))
                dv_acc[pl.ds(r0, 2 * BLOCK), :] += _dot_tn(p.astype(BF16), jnp.where(head_mask, dob, 0))
            dq_ref[pl.ds(r0, BLOCK), :] = dq * QK_SCALE
            return carry

        lax.fori_loop(0, N_BLOCKS, block, 0)
        dk_ref[...] = dk_acc[BLOCK:, :]
        dv_ref[...] = dv_acc[BLOCK:, :]

    tok = pl.BlockSpec((SEQ, LANES), lambda p: (0, p))
    padded = pl.BlockSpec((SEQ + BLOCK, LANES), lambda p: (0, p))
    shape = jax.ShapeDtypeStruct((SEQ, DIL_WIDTH), F32)
    return pl.pallas_call(
        body, name="dil_attn_bwd", grid=(DIL_WIDTH // LANES,),
        in_specs=[tok, padded, padded, tok, tok, tok], out_specs=[tok, tok, tok], out_shape=[shape] * 3,
        scratch_shapes=[pltpu.VMEM((SEQ + BLOCK, LANES), F32)] * 2,
        compiler_params=pltpu.CompilerParams(dimension_semantics=("parallel",)),
    )(q, k, v, do, lse, c)


def _group_weights(l0, l1, l2):
    m = jnp.maximum(jnp.maximum(l0, l1), l2)
    e0, e1, e2 = jnp.exp(l0 - m), jnp.exp(l1 - m), jnp.exp(l2 - m)
    tot = e0 + e1 + e2
    return e0 / tot, e1 / tot, e2 / tot


def _dil_combine(o, lse):
    def fn(o0, o1, o2, l0, l1, l2):
        w0, w1, w2 = _group_weights(l0, l1, l2)
        return w0 * o0 + w1 * o1 + w2 * o2

    w = DIL_OUT_WIDTH
    return _rowwise(fn, "dil_combine", SEQ, 512, [(o, w, g) for g in range(3)] + [(lse, w, g) for g in range(3)], [],
                    [(w, F32)])[0]


def _dil_combine_bwd(d_out, o, lse):
    w = DIL_OUT_WIDTH

    def fn(d, o0, o1, o2, l0, l1, l2):
        row = lax.broadcasted_iota(jnp.int32, (w, w), 0) // HEAD_DIM
        col = lax.broadcasted_iota(jnp.int32, (w, w), 1) // HEAD_DIM
        same_head = jnp.where(row == col, 1.0, 0.0).astype(BF16)
        ws = _group_weights(l0, l1, l2)
        dws = [_dot3_nn(d * og, same_head) for og in (o0, o1, o2)]
        mean = ws[0] * dws[0] + ws[1] * dws[1] + ws[2] * dws[2]
        do = jnp.concatenate([wg * d for wg in ws], axis=1)
        c = jnp.concatenate([-wg * mean for wg in ws], axis=1)
        return do, c

    return _rowwise(fn, "dil_combine_bwd", SEQ, 256,
                    [(d_out, w, 0)] + [(o, w, g) for g in range(3)] + [(lse, w, g) for g in range(3)], [],
                    [(DIL_WIDTH, BF16), (DIL_WIDTH, F32)])


def _fox_qkv(proj):
    def fn(q, k, v):
        return q * QK_SCALE, k, v

    w = FOX_WIDTH
    return _rowwise(fn, "fox_cast", SEQ, 256,
                    [(proj, w, COL_QB // w), (proj, w, COL_KB // w), (proj, w, COL_VB // w)], [], [(w, BF16)] * 3)


def _log1p(e):
    u = 1.0 + e
    return jnp.where(u == 1.0, e, jnp.log(u) * (e / (u - 1.0)))


def _fox_gate(proj, b_pad):
    def body(f_ref, b_ref, o_ref):
        z = f_ref[...] + b_ref[...]
        logf = (jnp.minimum(z, 0.0) - _log1p(jnp.exp(-jnp.abs(z)))).T[:F_ROWS]
        row = lax.broadcasted_iota(jnp.int32, (BLOCK, BLOCK), 0)
        col = lax.broadcasted_iota(jnp.int32, (BLOCK, BLOCK), 1)
        before = jnp.where(row <= col, 1.0, 0.0).astype(BF16)
        carry = jnp.zeros((F_ROWS, 1), F32)
        for blk in range(N_BLOCKS):
            run = _dot3_nn(logf[:, blk * BLOCK:(blk + 1) * BLOCK], before) + carry
            o_ref[:, blk * BLOCK:(blk + 1) * BLOCK] = run
            carry = run[:, BLOCK - 1:BLOCK]

    return pl.pallas_call(
        body, name="fox_gate", grid=(1,),
        in_specs=[pl.BlockSpec((SEQ, LANES), lambda i: (0, COL_F // LANES)), pl.BlockSpec((1, LANES), lambda i: (0, 0))],
        out_specs=pl.BlockSpec((F_ROWS, SEQ), lambda i: (0, 0)),
        out_shape=jax.ShapeDtypeStruct((F_ROWS, SEQ), F32),
    )(proj, b_pad)


def _fox_gate_bwd(d_cum, proj, b_pad):
    def body(d_ref, f_ref, b_ref, dz_ref, db_ref):
        row = lax.broadcasted_iota(jnp.int32, (BLOCK, BLOCK), 0)
        col = lax.broadcasted_iota(jnp.int32, (BLOCK, BLOCK), 1)
        after = jnp.where(row >= col, 1.0, 0.0).astype(BF16)
        carry = jnp.zeros((F_ROWS, 1), F32)
        parts = [None] * N_BLOCKS
        for blk in reversed(range(N_BLOCKS)):
            run = _dot3_nn(d_ref[:, blk * BLOCK:(blk + 1) * BLOCK], after) + carry
            parts[blk] = run
            carry = run[:, 0:1]
        dlogf = jnp.concatenate(parts, axis=1)
        dlogf = jnp.concatenate([dlogf, jnp.zeros((LANES - F_ROWS, SEQ), F32)], axis=0).T
        dz = dlogf * _sigmoid(-(f_ref[...] + b_ref[...]))
        dz_ref[...] = dz.astype(BF16)
        db_ref[...] = jnp.sum(dz, axis=0, keepdims=True)

    return pl.pallas_call(
        body, name="fox_gate_bwd", grid=(1,),
        in_specs=[pl.BlockSpec((F_ROWS, SEQ), lambda i: (0, 0)),
                  pl.BlockSpec((SEQ, LANES), lambda i: (0, COL_F // LANES)), pl.BlockSpec((1, LANES), lambda i: (0, 0))],
        out_specs=[pl.BlockSpec((SEQ, LANES), lambda i: (0, 0)), pl.BlockSpec((1, LANES), lambda i: (0, 0))],
        out_shape=[jax.ShapeDtypeStruct((SEQ, LANES), BF16), jax.ShapeDtypeStruct((1, LANES), F32)],
    )(d_cum, proj, b_pad)


FOX_TILE = 256
FOX_TILES = SEQ // FOX_TILE


def _row_to_col(row):
    n = row.shape[1]
    eye = lax.broadcasted_iota(jnp.int32, (n, n), 0) == lax.broadcasted_iota(jnp.int32, (n, n), 1)
    return jnp.sum(jnp.where(eye, row, 0.0), axis=1, keepdims=True)


def _fox_scores(q_tile, km, f_row, i):
    t = FOX_TILE
    ext = (i + 1) * t
    f_q = _row_to_col(f_row[:, i * t:(i + 1) * t])
    s = _dot_nt(q_tile, km[:ext]) + (f_q - f_row[:, :ext])
    row = lax.broadcasted_iota(jnp.int32, (t, ext), 0) + i * t
    col = lax.broadcasted_iota(jnp.int32, (t, ext), 1)
    return s, col <= row


def _fox_fwd(q, k, v, f_rows):
    t = FOX_TILE

    def body(q_ref, k_ref, v_ref, f_ref, o_ref, lse_ref):
        pair = pl.program_id(0)
        upper = _upper_half()
        masks = (jnp.logical_not(upper), upper)
        kms = [jnp.where(hm, k_ref[...], 0) for hm in masks]
        vms = [jnp.where(hm, v_ref[...], 0) for hm in masks]
        f_row = [f_ref[pl.ds(2 * pair + e, 1), :] for e in range(2)]
        for i in range(FOX_TILES):
            q_tile = q_ref[i * t:(i + 1) * t, :]
            outs, lses = [], []
            for e in range(2):
                s, causal = _fox_scores(q_tile, kms[e], f_row[e], i)
                s = jnp.where(causal, s, NEG_INF)
                m = jnp.max(s, axis=-1, keepdims=True)
                p = jnp.exp(s - m)
                den = jnp.sum(p, axis=-1, keepdims=True)
                outs.append(_dot_nn((p / den).astype(BF16), vms[e][:(i + 1) * t]))
                lses.append(m + jnp.log(den))
            o_ref[i * t:(i + 1) * t, :] = outs[0] + outs[1]
            lse_ref[i * t:(i + 1) * t, :] = jnp.where(upper, lses[1], lses[0])

    tok = pl.BlockSpec((SEQ, LANES), lambda p: (0, p))
    return pl.pallas_call(
        body, name="fox_attn_fwd", grid=(FOX_WIDTH // LANES,),
        in_specs=[tok, tok, tok, pl.BlockSpec((F_ROWS, SEQ), lambda p: (0, 0))], out_specs=[tok, tok],
        out_shape=[jax.ShapeDtypeStruct((SEQ, FOX_WIDTH), F32)] * 2,
        compiler_params=pltpu.CompilerParams(
            dimension_semantics=("parallel",), vmem_limit_bytes=_vmem_limit(8 * t * SEQ * 4)),
    )(q, k, v, f_rows)


def _fox_bwd(q, k, v, do, lse, f_rows):
    t = FOX_TILE

    def body(q_ref, k_ref, v_ref, do_ref, lse_ref, f_ref, dq_ref, dk_ref, dv_ref, df_ref):
        pair = pl.program_id(0)
        upper = _upper_half()
        masks = (jnp.logical_not(upper), upper)
        kms = [jnp.where(hm, k_ref[...], 0) for hm in masks]
        vms = [jnp.where(hm, v_ref[...], 0) for hm in masks]
        f_row = [f_ref[pl.ds(2 * pair + e, 1), :] for e in range(2)]
        dk_ref[...] = jnp.zeros_like(dk_ref)
        dv_ref[...] = jnp.zeros_like(dv_ref)
        df_ref[...] = jnp.zeros_like(df_ref)
        for i in range(FOX_TILES):
            ext = (i + 1) * t
            q_tile, do_tile = q_ref[i * t:(i + 1) * t, :], do_ref[i * t:(i + 1) * t, :]
            lse_t = lse_ref[i * t:(i + 1) * t, :]
            dq = None
            for e in range(2):
                s, causal = _fox_scores(q_tile, kms[e], f_row[e], i)
                p = jnp.where(causal, jnp.exp(s - lse_t[:, e * HEAD_DIM:e * HEAD_DIM + 1]), 0.0)
                dp = _dot_nt(do_tile, vms[e][:ext])
                ds = p * (dp - jnp.sum(p * dp, axis=-1, keepdims=True))
                df_ref[0, e:e + 1, :ext] -= jnp.sum(ds, axis=0, keepdims=True)
                ds = ds.astype(BF16)
                part = _dot_nn(ds, kms[e][:ext])
                dq = part if dq is None else dq + part
                dk_ref[:ext, :] += _dot_tn(ds, jnp.where(masks[e], q_tile, 0))
                dv_ref[:ext, :] += _dot_tn(p.astype(BF16), jnp.where(masks[e], do_tile, 0))
            dq_ref[i * t:(i + 1) * t, :] = dq * QK_SCALE

    tok = pl.BlockSpec((SEQ, LANES), lambda p: (0, p))
    shape = jax.ShapeDtypeStruct((SEQ, FOX_WIDTH), F32)
    return pl.pallas_call(
        body, name="fox_attn_bwd", grid=(FOX_WIDTH // LANES,),
        in_specs=[tok] * 5 + [pl.BlockSpec((F_ROWS, SEQ), lambda p: (0, 0))],
        out_specs=[tok, tok, tok, pl.BlockSpec((1, SUBLANES, SEQ), lambda p: (p, 0, 0))],
        out_shape=[shape, shape, shape, jax.ShapeDtypeStruct((FOX_WIDTH // LANES, SUBLANES, SEQ), F32)],
        compiler_params=pltpu.CompilerParams(
            dimension_semantics=("parallel",), vmem_limit_bytes=_vmem_limit(10 * t * SEQ * 4)),
    )(q, k, v, do, lse, f_rows)


MIX_TILE = 256


def _mix_out(out_a, out_b, proj, x, wt_pa, wt_pb, w_out, g_post, g_ffn_pre):
    tm = MIX_TILE

    def body(a_ref, b_ref, ga_ref, gb_ref, x_ref, wpa_ref, wpb_ref, wo_ref, g2_ref, g3_ref,
             merged_ref, mix_ref, x1_ref, h2_ref):
        ya = _dot_nt(a_ref[...].astype(BF16), wpa_ref[...])
        yb = _dot_nt(b_ref[...].astype(BF16), wpb_ref[...])
        merged = (_sigmoid(ga_ref[...]) * ya + _sigmoid(gb_ref[...]) * yb).astype(BF16)
        merged_ref[...] = merged
        mix = _dot_nn(merged, wo_ref[...])
        mix_ref[...] = mix
        x1 = x_ref[...] + mix * _rms_scale(mix) * g2_ref[...]
        x1_ref[...] = x1
        h2_ref[...] = (x1 * _rms_scale(x1) * g3_ref[...]).astype(BF16)

    def rows(w, cb=0):
        return pl.BlockSpec((tm, w), lambda i, cb=cb: (i, cb))

    def whole(a):
        return pl.BlockSpec(a.shape, lambda i: (0, 0))

    d = D_MODEL
    blk = _nbytes((tm, d), F32) * 6 + sum(_nbytes(a.shape, BF16) for a in (wt_pa, wt_pb, w_out))
    return pl.pallas_call(
        body, name="mix_out", grid=(SEQ // tm,),
        in_specs=[rows(DIL_OUT_WIDTH), rows(FOX_WIDTH), rows(d, COL_GA // d), rows(d, COL_GB // d), rows(d),
                  whole(wt_pa), whole(wt_pb), whole(w_out), whole(g_post), whole(g_ffn_pre)],
        out_specs=[rows(d)] * 4,
        out_shape=[jax.ShapeDtypeStruct((SEQ, d), dt) for dt in (BF16, F32, F32, BF16)],
        compiler_params=pltpu.CompilerParams(dimension_semantics=("parallel",), vmem_limit_bytes=_vmem_limit(blk)),
    )(out_a, out_b, proj, proj, x, wt_pa, wt_pb, w_out, g_post, g_ffn_pre)


def _mix_out_bwd(dmix, out_a, out_b, proj, wt_pa, wt_pb, w_out):
    tm = MIX_TILE

    def body(dm_ref, a_ref, b_ref, ga_ref, gb_ref, wpa_ref, wpb_ref, wo_ref,
             dga_ref, dgb_ref, dya_ref, dyb_ref, da_ref, db_ref):
        dmerged = _dot_nt(dm_ref[...], wo_ref[...])
        ya = _dot_nt(a_ref[...].astype(BF16), wpa_ref[...])
        yb = _dot_nt(b_ref[...].astype(BF16), wpb_ref[...])
        sa, sb = _sigmoid(ga_ref[...]), _sigmoid(gb_ref[...])
        dga_ref[...] = (dmerged * ya * (sa * (1.0 - sa))).astype(BF16)
        dgb_ref[...] = (dmerged * yb * (sb * (1.0 - sb))).astype(BF16)
        dya = (dmerged * sa).astype(BF16)
        dyb = (dmerged * sb).astype(BF16)
        dya_ref[...] = dya
        dyb_ref[...] = dyb
        da_ref[...] = _dot_nn(dya, wpa_ref[...])
        db_ref[...] = _dot_nn(dyb, wpb_ref[...]).astype(BF16)

    def rows(w, cb=0):
        return pl.BlockSpec((tm, w), lambda i, cb=cb: (i, cb))

    def whole(a):
        return pl.BlockSpec(a.shape, lambda i: (0, 0))

    d = D_MODEL
    blk = _nbytes((tm, d), F32) * 8 + sum(_nbytes(a.shape, BF16) for a in (wt_pa, wt_pb, w_out))
    return pl.pallas_call(
        body, name="mix_out_bwd", grid=(SEQ // tm,),
        in_specs=[rows(d), rows(DIL_OUT_WIDTH), rows(FOX_WIDTH), rows(d, COL_GA // d), rows(d, COL_GB // d),
                  whole(wt_pa), whole(wt_pb), whole(w_out)],
        out_specs=[rows(d)] * 4 + [rows(DIL_OUT_WIDTH), rows(FOX_WIDTH)],
        out_shape=[jax.ShapeDtypeStruct((SEQ, d), BF16)] * 4
        + [jax.ShapeDtypeStruct((SEQ, DIL_OUT_WIDTH), F32), jax.ShapeDtypeStruct((SEQ, FOX_WIDTH), BF16)],
        compiler_params=pltpu.CompilerParams(dimension_semantics=("parallel",), vmem_limit_bytes=_vmem_limit(blk)),
    )(dmix, out_a, out_b, proj, proj, wt_pa, wt_pb, w_out)


FFN_TM, FFN_TN = 1024, 256


def _ffn_up(h2, wt_gate, wt_up):
    tm, tn = FFN_TM, FFN_TN

    def body(h_ref, wg_ref, wu_ref, gate_ref, up_ref, act_ref):
        gate = _dot_nt(h_ref[...], wg_ref[...])
        up = _dot_nt(h_ref[...], wu_ref[...])
        gate_ref[...] = gate
        up_ref[...] = up
        act_ref[...] = (gate * _sigmoid(gate) * up).astype(BF16)

    tile = pl.BlockSpec((tm, tn), lambda i, j: (i, j))
    w_spec = pl.BlockSpec((tn, D_MODEL), lambda i, j: (j, 0))
    return pl.pallas_call(
        body, name="ffn_up", grid=(SEQ // tm, D_FF // tn),
        in_specs=[pl.BlockSpec((tm, D_MODEL), lambda i, j: (i, 0)), w_spec, w_spec],
        out_specs=[tile, tile, tile],
        out_shape=[jax.ShapeDtypeStruct((SEQ, D_FF), dt) for dt in (F32, F32, BF16)],
        compiler_params=pltpu.CompilerParams(
            dimension_semantics=("parallel", "parallel"), vmem_limit_bytes=_vmem_limit(8 * 2**20)),
    )(h2, wt_gate, wt_up)


def _ffn_act_bwd(dff, w_down, gate, up):
    tm, tn = FFN_TM, FFN_TN

    def body(d_ref, wd_ref, gate_ref, up_ref, dgate_ref, dup_ref):
        dact = _dot_nt(d_ref[...], wd_ref[...])
        gate = gate_ref[...]
        sg = _sigmoid(gate)
        dgate_ref[...] = (dact * up_ref[...] * (sg * (1.0 + gate * (1.0 - sg)))).astype(BF16)
        dup_ref[...] = (dact * (gate * sg)).astype(BF16)

    tile = pl.BlockSpec((tm, tn), lambda i, j: (i, j))
    return pl.pallas_call(
        body, name="ffn_act_bwd", grid=(SEQ // tm, D_FF // tn),
        in_specs=[pl.BlockSpec((tm, D_MODEL), lambda i, j: (i, 0)), pl.BlockSpec((tn, D_MODEL), lambda i, j: (j, 0)),
                  tile, tile],
        out_specs=[tile, tile],
        out_shape=[jax.ShapeDtypeStruct((SEQ, D_FF), BF16)] * 2,
        compiler_params=pltpu.CompilerParams(
            dimension_semantics=("parallel", "parallel"), vmem_limit_bytes=_vmem_limit(8 * 2**20)),
    )(dff, w_down, gate, up)


def _loss_head(ff, x1, target, g_post):
    def fn(ff, x1, tgt, g):
        r = _rms_scale(ff)
        nrm = ff * r
        err = (x1 + nrm * g) - tgt
        loss = 0.5 * jnp.sum(jnp.mean(err * err, axis=-1, keepdims=True), axis=0, keepdims=True)
        dy = err * (1.0 / D_MODEL)
        u = dy * g
        dff = r * u - ff * (r * r * r) * jnp.mean(u * ff, axis=-1, keepdims=True)
        return dy, dff, jnp.broadcast_to(loss, (1, LANES)), jnp.sum(dy * nrm, axis=0, keepdims=True)

    d = D_MODEL
    return _rowwise(fn, "loss_head", SEQ, 256, [(ff, d, 0), (x1, d, 0), (target, d, 0)], [g_post],
                    [(d, F32), (d, BF16)], [LANES, d])


def _post_ffn_bwd(dh2, x1, dy, mix, g_ffn_pre, g_mix_post):
    def fn(dh2, x1, dy, mix, g3, g2):
        dx, dg3 = _rms_bwd(x1, dh2, g3)
        dx1 = dy + dx
        dmix, dg2 = _rms_bwd(mix, dx1, g2)
        return dx1, dmix, dg3, dg2

    d = D_MODEL
    return _rowwise(fn, "post_ffn_bwd", SEQ, 256, [(dh2, d, 0), (x1, d, 0), (dy, d, 0), (mix, d, 0)],
                    [g_ffn_pre, g_mix_post], [(d, F32), (d, BF16)], [d, d])


def _input_bwd(dh, x, dx1, g_pre):
    def fn(dh, x, dx1, g):
        dx, dg = _rms_bwd(x, dh, g)
        return dx1 + dx, dg

    d = D_MODEL
    return _rowwise(fn, "input_bwd", SEQ, 256, [(dh, d, 0), (x, d, 0), (dx1, d, 0)], [g_pre], [(d, F32)], [d])


def _adam_math(w, g, m, v):
    m = ADAM_B1 * m + (1.0 - ADAM_B1) * g
    v = ADAM_B2 * v + (1.0 - ADAM_B2) * (g * g)
    m_hat = m / (1.0 - ADAM_B1 ** ADAM_STEP)
    v_hat = v / (1.0 - ADAM_B2 ** ADAM_STEP)
    delta = -ADAM_LR * (m_hat / (jnp.sqrt(v_hat) + ADAM_EPS) + ADAM_WD * w)
    return delta, m, v


def _adam(w, g, m, v, name):
    r, c = w.shape
    tc = _col_tile(r, c)

    def body(w_ref, g_ref, m_ref, v_ref, d_ref, nm_ref, nv_ref):
        d_ref[...], nm_ref[...], nv_ref[...] = _adam_math(w_ref[...], g_ref[...], m_ref[...], v_ref[...])

    spec = pl.BlockSpec((r, tc), lambda j: (0, j))
    return pl.pallas_call(
        body, name=name, grid=(c // tc,), in_specs=[spec] * 4, out_specs=[spec] * 3,
        out_shape=[jax.ShapeDtypeStruct((r, c), F32)] * 3,
        compiler_params=pltpu.CompilerParams(dimension_semantics=("parallel",)),
    )(w, g, m, v)


SMALL_ROWS = 5 * SUBLANES


def _adam_small(gathered, w, m, v):
    def body(ga_ref, w_ref, m_ref, v_ref, g_ref, d_ref, nm_ref, nv_ref):
        g = ga_ref[0]
        for dev in range(1, N_DEV):
            g = g + ga_ref[dev]
        g_ref[...] = g
        d_ref[...], nm_ref[...], nv_ref[...] = _adam_math(w_ref[...], g, m_ref[...], v_ref[...])

    shape = jax.ShapeDtypeStruct(w.shape, F32)
    return pl.pallas_call(body, name="adam_small", out_shape=[shape] * 4)(gathered, w, m, v)


def _small_pack(rows):
    d = D_MODEL
    return jnp.concatenate(
        [jnp.broadcast_to(jnp.pad(r, ((0, 0), (0, d - r.shape[1]))), (SUBLANES, d)) for r in rows], axis=0)


def _proj_weight_t(wt_in_full):
    w = wt_in_full
    z = lambda n: jnp.zeros((n, D_MODEL), w.dtype)
    return jnp.concatenate([w[3848:5896], z(256), w[0:3840], w[3840:3848], z(PROJ_COLS - COL_F - 8)], axis=0)


def _proj_weight_grad_t(dwt_r):
    return jnp.concatenate([dwt_r[COL_QA:COL_F], dwt_r[COL_F:COL_F + 8], dwt_r[0:2048]], axis=0)


def kernel(x, w_in, w_proj_a, w_proj_b, w_out, b_forget, w_ffn_gate, w_ffn_up, w_ffn_down, norm_mix_pre, norm_mix_post, norm_ffn_pre, norm_ffn_post, loss_target, m_w_in, m_w_proj_a, m_w_proj_b, m_w_out, m_b_forget, m_w_ffn_gate, m_w_ffn_up, m_w_ffn_down, m_norm_mix_pre, m_norm_mix_post, m_norm_ffn_pre, m_norm_ffn_post, v_w_in, v_w_proj_a, v_w_proj_b, v_w_out, v_b_forget, v_w_ffn_gate, v_w_ffn_up, v_w_ffn_down, v_norm_mix_pre, v_norm_mix_post, v_norm_ffn_pre, v_norm_ffn_post):
    d = D_MODEL
    names = ("w_in", "w_proj_a", "w_proj_b", "w_out", "w_ffn_gate", "w_ffn_up", "w_ffn_down")
    col_sharded = ("w_in", "w_proj_a", "w_proj_b", "w_ffn_gate", "w_ffn_up")

    def row_shards(arrs):
        return {k: (a[0].T if k in col_sharded else a[0]) for k, a in zip(names, arrs)}

    shards = row_shards((w_in, w_proj_a, w_proj_b, w_out, w_ffn_gate, w_ffn_up, w_ffn_down))
    moments_m = row_shards((m_w_in, m_w_proj_a, m_w_proj_b, m_w_out, m_w_ffn_gate, m_w_ffn_up, m_w_ffn_down))
    moments_v = row_shards((v_w_in, v_w_proj_a, v_w_proj_b, v_w_out, v_w_ffn_gate, v_w_ffn_up, v_w_ffn_down))
    core = lax.axis_index("c").astype(jnp.int32).reshape(1)
    chip = (2 * lax.axis_index("x") + lax.axis_index("y")).astype(jnp.int32).reshape(1)
    x2, target = x[0], loss_target[0]

    gathered = dict(zip(names, _all_gather([shards[k].astype(BF16) for k in names], "weights_all_gather")))
    wt_r = _proj_weight_t(gathered["w_in"].reshape(IN_COLS, d))
    wt_pa = gathered["w_proj_a"].reshape(d, DIL_OUT_WIDTH)
    wt_pb = gathered["w_proj_b"].reshape(d, FOX_WIDTH)
    w_o = gathered["w_out"].reshape(d, d)
    wt_g = gathered["w_ffn_gate"].reshape(D_FF, d)
    wt_u = gathered["w_ffn_up"].reshape(D_FF, d)
    w_d = gathered["w_ffn_down"].reshape(D_FF, d)

    h = _rowwise(lambda xb, g: xb * _rms_scale(xb) * g, "norm_mix_pre", SEQ, 256, [(x2, d, 0)], [norm_mix_pre],
                 [(d, BF16)])[0]
    proj = _matmul([(h, wt_r)], "nt", F32, "in_proj", 1024, 896, 1024)
    tables = _rope_tables()
    qa_tok, ka_tok, va_tok = _dil_qkv(proj, tables)
    qa = _residue_major(qa_tok)
    ka, va = _residue_major(ka_tok, BLOCK), _residue_major(va_tok, BLOCK)
    o_dil, lse_dil = _dil_fwd(qa, ka, va)
    o_tok, lse_tok = _token_major(o_dil), _token_major(lse_dil)
    out_a = _dil_combine(o_tok, lse_tok)

    b_pad = jnp.pad(b_forget, ((0, 0), (0, LANES - N_FOX_HEADS)))
    f_rows = _fox_gate(proj, b_pad)
    qb, kb, vb = _fox_qkv(proj)
    out_b, lse_fox = _fox_fwd(qb, kb, vb, f_rows)

    merged, mix, x1, h2 = _mix_out(out_a, out_b, proj, x2, wt_pa, wt_pb, w_o, norm_mix_post, norm_ffn_pre)

    gate, up, act = _ffn_up(h2, wt_g, wt_u)
    ff = _matmul([(act, w_d)], "nn", F32, "ffn_down", 1024, 1024, 1408)
    dy, dff, loss_part, dg_ffn_post = _loss_head(ff, x1, target, norm_ffn_post)
    loss = lax.psum(loss_part[0, 0], ("x", "y", "c"))

    dgate, dup = _ffn_act_bwd(dff, w_d, gate, up)
    grads_t = {}
    grads_t["w_ffn_down"] = _matmul([(act, dff)], "tn", F32, "grad_w_ffn_down", 1408, 1024, 1024)
    grads_t["w_ffn_gate"] = _matmul([(dgate, h2)], "tn", F32, "grad_w_ffn_gate", 1408, 1024, 1024)
    grads_t["w_ffn_up"] = _matmul([(dup, h2)], "tn", F32, "grad_w_ffn_up", 1408, 1024, 1024)
    dh2 = _matmul([(dgate, wt_g), (dup, wt_u)], "nn", F32, "ffn_up_bwd", 1024, 1024, 1408)
    dx1, dmix, dg_ffn_pre, dg_mix_post = _post_ffn_bwd(dh2, x1, dy, mix, norm_ffn_pre, norm_mix_post)

    dga, dgb, dya, dyb, d_out_a, d_out_b = _mix_out_bwd(dmix, out_a, out_b, proj, wt_pa, wt_pb, w_o)
    grads_t["w_out"] = _matmul([(merged, dmix)], "tn", F32, "grad_w_out", 1024, 1024, 1024)
    grads_t["w_proj_a"] = _matmul([(dya, out_a)], "tn", F32, "grad_w_proj_a", 1024, DIL_OUT_WIDTH, SEQ)
    grads_t["w_proj_b"] = _matmul([(dyb, out_b)], "tn", F32, "grad_w_proj_b", 1024, FOX_WIDTH, SEQ)

    dq_fox, dk_fox, dv_fox, d_cum = _fox_bwd(qb, kb, vb, d_out_b, lse_fox, f_rows)
    d_cum_rows = jnp.pad(d_cum[:, :2].reshape(N_FOX_HEADS, SEQ), ((0, F_ROWS - N_FOX_HEADS), (0, 0)))
    dz, db_part = _fox_gate_bwd(d_cum_rows, proj, b_pad)

    do_tok, c_tok = _dil_combine_bwd(d_out_a, o_tok, lse_tok)
    dq_dil, dk_dil, dv_dil = _dil_bwd(qa, ka, va, _residue_major(do_tok), lse_dil, _residue_major(c_tok))
    dqa, dka = _dil_qk_bwd(_token_major(dq_dil), _token_major(dk_dil), tables)

    dproj = jnp.concatenate(
        [dga, dgb, jnp.zeros((SEQ, COL_QA - 2 * d), BF16), dqa, dka, _token_major(dv_dil).astype(BF16),
         dq_fox.astype(BF16), dk_fox.astype(BF16), dv_fox.astype(BF16), dz], axis=1)
    dwt_r = _matmul([(dproj, h)], "tn", F32, "grad_w_in", 896, 1024, 1024)
    dh = _matmul([(dproj, wt_r)], "nn", F32, "in_proj_bwd", 1024, 1024, 896)
    grad_x, dg_mix_pre = _input_bwd(dh, x2, dx1, norm_mix_pre)
    grads_t["w_in"] = _proj_weight_grad_t(dwt_r)

    slots = {k: grads_t[k].reshape((N_DEV,) + shards[k].shape) for k in names}
    grads = _reduce_scatter(slots, core, chip)

    small_part = _small_pack([dg_mix_pre, dg_mix_post, dg_ffn_pre, dg_ffn_post, db_part])
    small_all = _all_gather([small_part], "small_grads_all_gather")[0]
    small_w = _small_pack([norm_mix_pre, norm_mix_post, norm_ffn_pre, norm_ffn_post, b_forget])
    small_m = _small_pack([m_norm_mix_pre, m_norm_mix_post, m_norm_ffn_pre, m_norm_ffn_post, m_b_forget])
    small_v = _small_pack([v_norm_mix_pre, v_norm_mix_post, v_norm_ffn_pre, v_norm_ffn_post, v_b_forget])
    small = _adam_small(small_all, small_w, small_m, small_v)

    big = {k: _adam(shards[k], grads[k], moments_m[k], moments_v[k], "adam_" + k) for k in names}

    def leaves(i):
        def nat(k):
            a = grads[k] if i == 0 else big[k][i - 1]
            return (a.T if k in col_sharded else a)[None]

        s = small[i]
        gains = [s[r * SUBLANES:r * SUBLANES + 1] for r in range(4)]
        return [nat("w_in"), nat("w_proj_a"), nat("w_proj_b"), nat("w_out"),
                s[4 * SUBLANES:4 * SUBLANES + 1, :N_FOX_HEADS],
                nat("w_ffn_gate"), nat("w_ffn_up"), nat("w_ffn_down"), *gains]

    return (loss, grad_x[None], *leaves(0), *leaves(1), *leaves(2), *leaves(3))
```

```python
import functools
import math

import jax
import jax.numpy as jnp
from jax import lax
from jax.experimental import pallas as pl
from jax.experimental.pallas import tpu as pltpu

F32 = jnp.float32
BF16 = jnp.bfloat16
MESH = pl.DeviceIdType.MESH

D_MODEL = 1024
SEQ = 2048
HEAD_DIM = 64
BLOCK = 128
N_BLOCKS = SEQ // BLOCK
DILATIONS = (1, 4, 16)
N_FOX_HEADS = 8
DIL_WIDTH = 768
DIL_OUT_WIDTH = 256
FOX_WIDTH = 512
D_FF = 2816
ROPE_THETA = 500000.0
ROPE_DIM = HEAD_DIM // 4
ROPE_HALF = ROPE_DIM // 2
EPS = 1e-6
NEG_INF = -1e30
QK_SCALE = 1.0 / math.sqrt(HEAD_DIM)
IN_COLS = 5896
N_DEV = 8
IN_SHARD = IN_COLS // N_DEV

ADAM_LR = 0.001
ADAM_B1 = 0.9
ADAM_B2 = 0.999
ADAM_EPS = 1e-08
ADAM_WD = 0.01
ADAM_STEP = 10

V7X_VMEM_BYTES = 64 * 2**20
LANES = 128
SUBLANES = 8

PROJ_COLS = 6272
COL_GA, COL_GB = 0, 1024
COL_QA, COL_KA, COL_VA = 2304, 3072, 3840
COL_QB, COL_KB, COL_VB = 4608, 5120, 5632
COL_F = 6144
F_ROWS = 16


def _vmem_limit(block_bytes):
    want = 2 * block_bytes + 16 * 2**20
    return int(min(max(want, 32 * 2**20), V7X_VMEM_BYTES - 8 * 2**20))


def _nbytes(shape, dtype):
    return math.prod(shape) * jnp.dtype(dtype).itemsize


def _dot(a, b, dims):
    return lax.dot_general(a, b, (dims, ((), ())), preferred_element_type=F32)


def _dot_nn(a, b):
    return _dot(a, b, ((1,), (0,)))


def _dot_nt(a, b):
    return _dot(a, b, ((1,), (1,)))


def _dot_tn(a, b):
    return _dot(a, b, ((0,), (0,)))


def _sigmoid(z):
    return 1.0 / (1.0 + jnp.exp(-z))


def _split3(x):
    hi = x.astype(BF16)
    r1 = x - hi.astype(F32)
    mid = r1.astype(BF16)
    lo = (r1 - mid.astype(F32)).astype(BF16)
    return hi, mid, lo


def _dot3_nn(x, ones_matrix):
    hi, mid, lo = _split3(x)
    return (_dot_nn(hi, ones_matrix) + _dot_nn(mid, ones_matrix)) + _dot_nn(lo, ones_matrix)


def _rowwise(fn, name, n_rows, tm, row_ins, bcast_ins, row_outs, acc_outs=(), deps=()):
    n_in = len(row_ins) + len(bcast_ins)
    n_ro = len(row_outs)

    def body(*refs):
        res = fn(*[r[...] for r in refs[:n_in]])
        if not isinstance(res, (tuple, list)):
            res = (res,)
        outs = refs[n_in + len(deps):]
        for r, o in zip(res[:n_ro], outs[:n_ro]):
            o[...] = r.astype(o.dtype)
        first = pl.program_id(0) == 0
        for r, o in zip(res[n_ro:], outs[n_ro:]):
            _accumulate(o, r, first)

    in_specs = [pl.BlockSpec((tm, w), lambda i, cb=cb: (i, cb)) for _, w, cb in row_ins]
    in_specs += [pl.BlockSpec(a.shape, lambda i: (0, 0)) for a in bcast_ins]
    in_specs += [pl.BlockSpec(memory_space=pl.ANY)] * len(deps)
    out_specs = [pl.BlockSpec((tm, w), lambda i: (i, 0)) for w, _ in row_outs]
    out_specs += [pl.BlockSpec((1, w), lambda i: (0, 0)) for w in acc_outs]
    out_shape = [jax.ShapeDtypeStruct((n_rows, w), dt) for w, dt in row_outs]
    out_shape += [jax.ShapeDtypeStruct((1, w), F32) for w in acc_outs]
    blk = sum(_nbytes((tm, w), a.dtype) for a, w, _ in row_ins) + sum(_nbytes((tm, w), dt) for w, dt in row_outs)
    return pl.pallas_call(
        body, name=name, grid=(n_rows // tm,), in_specs=in_specs, out_specs=out_specs, out_shape=out_shape,
        compiler_params=pltpu.CompilerParams(
            dimension_semantics=("arbitrary" if acc_outs else "parallel",), vmem_limit_bytes=_vmem_limit(3 * blk)),
    )(*[a for a, _, _ in row_ins], *bcast_ins, *deps)


def _accumulate(o_ref, part, first):
    @pl.when(first)
    def _():
        o_ref[...] = part

    @pl.when(jnp.logical_not(first))
    def _():
        o_ref[...] += part


_MM_DIMS = {"nn": ((1,), (0,)), "nt": ((1,), (1,)), "tn": ((0,), (0,))}


def _matmul(pairs, mode, out_dtype, name, tm, tn, tk, deps=()):
    a0, b0 = pairs[0]
    if mode == "tn":
        kk, m = a0.shape
    else:
        m, kk = a0.shape
    n = b0.shape[0] if mode == "nt" else b0.shape[1]
    assert m % tm == 0 and n % tn == 0 and kk % tk == 0, (name, m, n, kk)
    nk = kk // tk
    n_pairs = len(pairs)
    dims = _MM_DIMS[mode]
    n_in = 2 * n_pairs + len(deps)

    def body(*refs):
        o_ref = refs[n_in]
        part = None
        for p in range(n_pairs):
            d = _dot(refs[2 * p][...].astype(BF16), refs[2 * p + 1][...].astype(BF16), dims)
            part = d if part is None else part + d
        if nk == 1:
            o_ref[...] = part.astype(o_ref.dtype)
            return
        acc = refs[n_in + 1]
        k = pl.program_id(2)

        @pl.when(k == 0)
        def _():
            acc[...] = part

        @pl.when(k > 0)
        def _():
            acc[...] += part

        @pl.when(k == nk - 1)
        def _():
            o_ref[...] = acc[...].astype(o_ref.dtype)

    if mode == "tn":
        a_spec = pl.BlockSpec((tk, tm), lambda i, j, k: (k, i))
    else:
        a_spec = pl.BlockSpec((tm, tk), lambda i, j, k: (i, k))
    if mode == "nt":
        b_spec = pl.BlockSpec((tn, tk), lambda i, j, k: (j, k))
    else:
        b_spec = pl.BlockSpec((tk, tn), lambda i, j, k: (k, j))
    blk = sum(_nbytes((tm, tk), a.dtype) + _nbytes((tk, tn), b.dtype) for a, b in pairs) + 2 * _nbytes((tm, tn), F32)
    flat = [a for pair in pairs for a in pair]
    return pl.pallas_call(
        body, name=name, grid=(m // tm, n // tn, nk),
        in_specs=[a_spec, b_spec] * n_pairs + [pl.BlockSpec(memory_space=pl.ANY)] * len(deps),
        out_specs=pl.BlockSpec((tm, tn), lambda i, j, k: (i, j)),
        out_shape=jax.ShapeDtypeStruct((m, n), out_dtype),
        scratch_shapes=[] if nk == 1 else [pltpu.VMEM((tm, tn), F32)],
        compiler_params=pltpu.CompilerParams(
            dimension_semantics=("parallel", "parallel", "arbitrary"), vmem_limit_bytes=_vmem_limit(blk)),
    )(*flat, *deps)


def _rms_scale(x):
    return lax.rsqrt(jnp.mean(x * x, axis=-1, keepdims=True) + EPS)


def _rms_bwd(xin, dyn, g):
    r = _rms_scale(xin)
    u = dyn * g
    dx = r * u - xin * (r * r * r) * jnp.mean(u * xin, axis=-1, keepdims=True)
    dg = jnp.sum(dyn * xin * r, axis=0, keepdims=True)
    return dx, dg


def _mesh_pos():
    return lax.axis_index("x"), lax.axis_index("y"), lax.axis_index("c")


def _all_gather(xs, name):
    n = len(xs)

    def body(*refs):
        x_refs, out_refs = refs[:n], refs[n:2 * n]
        send_sems, recv_sems, local_sems = refs[2 * n:]
        mx, my, mc = _mesh_pos()
        me, sib = (mx, my, mc), (mx, my, 1 - mc)
        chips = [(1 - mx, my), (mx, 1 - my), (1 - mx, 1 - my)]

        def slot(a, dev):
            px, py, pc = dev
            return out_refs[a].at[4 * px + 2 * py + pc]

        def copy(k, a, block, to, src=None):
            return pltpu.make_async_remote_copy(
                src_ref=slot(a, block) if src is None else src, dst_ref=slot(a, block),
                send_sem=send_sems.at[a * 7 + k], recv_sem=recv_sems.at[a * 7 + k],
                device_id=to, device_id_type=MESH)

        mine = [pltpu.make_async_copy(x_refs[a], slot(a, me), local_sems.at[a]) for a in range(n)]
        for cp in mine:
            cp.start()
        first = []
        for a in range(n):
            first.append(copy(0, a, me, sib, x_refs[a]))
            first += [copy(1 + j, a, me, (*chip, mc), x_refs[a]) for j, chip in enumerate(chips)]
        for cp in first:
            cp.start()
        passed = []
        for a in range(n):
            for j, chip in enumerate(chips):
                copy(1 + j, a, (*chip, mc), me).wait_recv()
                fwd = copy(4 + j, a, (*chip, mc), sib)
                fwd.start()
                passed.append(fwd)
        for a in range(n):
            copy(0, a, sib, me).wait_recv()
            for j, chip in enumerate(chips):
                copy(4 + j, a, (*chip, 1 - mc), me).wait_recv()
        for cp in first + passed:
            cp.wait_send()
        for cp in mine:
            cp.wait()

    hbm = pl.BlockSpec(memory_space=pl.ANY)
    return pl.pallas_call(
        body, name=name,
        out_shape=[jax.ShapeDtypeStruct((N_DEV,) + x.shape, x.dtype) for x in xs],
        in_specs=[hbm] * n, out_specs=[hbm] * n,
        scratch_shapes=[pltpu.SemaphoreType.DMA((7 * n,)), pltpu.SemaphoreType.DMA((7 * n,)),
                        pltpu.SemaphoreType.DMA((n,))],
    )(*xs)


_HBM = pl.BlockSpec(memory_space=pltpu.HBM)
_SEM = pl.BlockSpec(memory_space=pltpu.SEMAPHORE)
_ANY = pl.BlockSpec(memory_space=pl.ANY)
_DATAFLOW = pltpu.SideEffectType.DATAFLOW_SIDE_EFFECTING


def _flip_peer(flip):
    mx, my, mc = _mesh_pos()
    return (1 - mx if flip & 2 else mx, 1 - my if flip & 1 else my, mc)


def _remote(src, dst, send_sems, recv_sems, k, peer):
    return pltpu.make_async_remote_copy(src_ref=src, dst_ref=dst, send_sem=send_sems.at[k], recv_sem=recv_sems.at[k],
                                        device_id=peer, device_id_type=MESH)


def _gather_chips_copies(srcs, lands, send_sems, recv_sems):
    mx, my, mc = _mesh_pos()
    me = 4 * mx + 2 * my + mc
    return [_remote(srcs[a], lands[a].at[me], send_sems, recv_sems, 3 * a + flip - 1, _flip_peer(flip))
            for a in range(len(srcs)) for flip in (1, 2, 3)]


def _gather_sibling_copies(srcs, lands, send_sems, recv_sems):
    mx, my, mc = _mesh_pos()
    return [_remote(lands[a].at[2 * k + mc], lands[a].at[2 * k + mc], send_sems, recv_sems, 4 * a + k, (mx, my, 1 - mc))
            for a in range(len(lands)) for k in range(4)]


def _scatter_sibling_copies(srcs, lands, send_sems, recv_sems):
    mx, my, mc = _mesh_pos()
    return [_remote(srcs[a].at[k, 1 - mc], lands[a].at[k], send_sems, recv_sems, 4 * a + k, (mx, my, 1 - mc))
            for a in range(len(srcs)) for k in range(4)]


def _scatter_chips_copies(srcs, lands, send_sems, recv_sems):
    mx, my, _ = _mesh_pos()
    k0 = 2 * mx + my
    return [_remote(srcs[a].at[jnp.bitwise_xor(k0, flip)], lands[a].at[flip - 1], send_sems, recv_sems,
                    3 * a + flip - 1, _flip_peer(flip))
            for a in range(len(srcs)) for flip in (1, 2, 3)]


class _Exchange:
    def __init__(self, copies, n_src, send_sems, recv_sems, thru, token):
        self.copies, self.n_src, self.send_sems, self.recv_sems, self.thru, self.token = (
            copies, n_src, send_sems, recv_sems, thru, token)


def _exchange_start(name, copies, srcs, lands, n_copies, after=()):
    bufs = list(srcs) + list(lands)
    nb, ns = len(bufs), len(srcs)

    def body(*refs):
        send_sems, recv_sems = refs[nb + len(after)], refs[nb + len(after) + 1]
        for cp in copies(refs[:ns], refs[ns:nb], send_sems, recv_sems):
            cp.start()
        refs[-1][...] = jnp.zeros_like(refs[-1])

    out = pl.pallas_call(
        body, name=name,
        out_shape=(pltpu.SemaphoreType.DMA((n_copies,)), pltpu.SemaphoreType.DMA((n_copies,)),
                   *[pltpu.HBM(b.shape, b.dtype) for b in bufs], jax.ShapeDtypeStruct((SUBLANES, LANES), F32)),
        in_specs=[_HBM] * nb + [_ANY] * len(after),
        out_specs=(_SEM, _SEM, *[_HBM] * nb, pl.BlockSpec(memory_space=pltpu.VMEM)),
        input_output_aliases={i: 2 + i for i in range(nb)},
        compiler_params=pltpu.CompilerParams(has_side_effects=_DATAFLOW),
    )(*[pltpu.with_memory_space_constraint(b, pltpu.HBM) for b in bufs], *after)
    return _Exchange(copies, ns, out[0], out[1], list(out[2:2 + nb]), out[-1])


def _exchange_wait(name, ex, after):
    nb, ns = len(ex.thru), ex.n_src

    def body(*refs):
        for cp in ex.copies(refs[:ns], refs[ns:nb], refs[nb], refs[nb + 1]):
            cp.wait_send()
            cp.wait_recv()

    out = pl.pallas_call(
        body, name=name, out_shape=tuple(pltpu.HBM(b.shape, b.dtype) for b in ex.thru),
        in_specs=[_HBM] * nb + [_SEM, _SEM] + [_ANY] * len(after), out_specs=tuple([_HBM] * nb),
        input_output_aliases={i: i for i in range(nb)},
        compiler_params=pltpu.CompilerParams(has_side_effects=_DATAFLOW),
    )(*ex.thru, ex.send_sems, ex.recv_sems, *after)
    return list(out[ns:])


def _col_tile(r, c):
    return next(t for t in (1024, 512, 256, 128) if c % t == 0 and (r * t * 4 <= 2**20 or t == 128))


def _add_sibling(g4, recv, core, name):
    _, _, r, c = g4.shape
    tc = _col_tile(r, c)

    def body(core_ref, g_ref, r_ref, o16_ref, o32_ref):
        s = g_ref[0, 0] + r_ref[0]
        o16_ref[0] = s.astype(BF16)
        o32_ref[0] = s

    out = pl.BlockSpec((1, r, tc), lambda k, j, core_ref: (k, 0, j))
    return pl.pallas_call(
        body, name=name,
        out_shape=[jax.ShapeDtypeStruct((4, r, c), BF16), jax.ShapeDtypeStruct((4, r, c), F32)],
        grid_spec=pltpu.PrefetchScalarGridSpec(
            num_scalar_prefetch=1, grid=(4, c // tc),
            in_specs=[pl.BlockSpec((1, 1, r, tc), lambda k, j, core_ref: (k, core_ref[0], 0, j)), out],
            out_specs=[out, out]),
        compiler_params=pltpu.CompilerParams(dimension_semantics=("parallel", "parallel")),
    )(core, g4, recv)


def _add_chips(p32, recv, chip, name):
    _, r, c = p32.shape
    tc = _col_tile(r, c)

    def body(chip_ref, p_ref, r_ref, o_ref):
        o_ref[...] = ((p_ref[0] + r_ref[0].astype(F32)) + r_ref[1].astype(F32)) + r_ref[2].astype(F32)

    return pl.pallas_call(
        body, name=name, out_shape=jax.ShapeDtypeStruct((r, c), F32),
        grid_spec=pltpu.PrefetchScalarGridSpec(
            num_scalar_prefetch=1, grid=(c // tc,),
            in_specs=[pl.BlockSpec((1, r, tc), lambda j, chip_ref: (chip_ref[0], 0, j)),
                      pl.BlockSpec((3, r, tc), lambda j, chip_ref: (0, 0, j))],
            out_specs=pl.BlockSpec((r, tc), lambda j, chip_ref: (0, j))),
        compiler_params=pltpu.CompilerParams(dimension_semantics=("parallel",)),
    )(chip, p32, recv)


class _ReduceScatter:
    def __init__(self, tag, grads_t, core, chip):
        self.tag, self.core, self.chip, self.names = tag, core, chip, list(grads_t)
        self.g4s = [g.reshape(4, 2, g.shape[0] // N_DEV, g.shape[1]) for g in grads_t.values()]
        lands = [lax.empty((4,) + g.shape[2:], F32) for g in self.g4s]
        self.ex = _exchange_start(f"rs_{tag}_sibling_start", _scatter_sibling_copies, self.g4s, lands,
                                  4 * len(self.g4s))
        self.token = self.ex.token

    def start_chips(self, after):
        from_sibling = _exchange_wait(f"rs_{self.tag}_sibling_wait", self.ex, after)
        parts = [_add_sibling(g4, rv, self.core, f"rs_add_sibling_{k}")
                 for k, g4, rv in zip(self.names, self.g4s, from_sibling)]
        self.p32s = [p32 for _, p32 in parts]
        p16s = [p16 for p16, _ in parts]
        lands = [lax.empty((3,) + p.shape[1:], BF16) for p in p16s]
        self.ex = _exchange_start(f"rs_{self.tag}_chips_start", _scatter_chips_copies, p16s, lands, 3 * len(p16s))
        self.token = self.ex.token

    def finish(self, after):
        from_chips = _exchange_wait(f"rs_{self.tag}_chips_wait", self.ex, after)
        return {k: _add_chips(p32, rv, self.chip, f"rs_add_chips_{k}")
                for k, p32, rv in zip(self.names, self.p32s, from_chips)}


def _rope_tables():
    positions = jnp.arange(SEQ, dtype=F32)
    inv_freq = jnp.power(ROPE_THETA, -jnp.arange(0, ROPE_DIM, 2, dtype=F32) / ROPE_DIM)
    ang = positions[:, None] * inv_freq[None, :]
    cos, sin = jnp.cos(ang), jnp.sin(ang)
    ones = jnp.ones((SEQ, HEAD_DIM - ROPE_DIM), F32)
    zeros8 = jnp.zeros((SEQ, ROPE_HALF), F32)
    zeros = jnp.zeros((SEQ, HEAD_DIM - ROPE_DIM), F32)
    c_head = jnp.concatenate([cos, cos, ones], axis=1)
    s1_head = jnp.concatenate([-sin, zeros8, zeros], axis=1)
    s2_head = jnp.concatenate([zeros8, sin, zeros], axis=1)
    return tuple(jnp.concatenate([t, t], axis=1) for t in (c_head, s1_head, s2_head))


def _tile_lanes(t, width):
    return jnp.concatenate([t] * (width // LANES), axis=1)


def _rope_apply(x, c, s1, s2):
    w = x.shape[1]
    return x * c + pltpu.roll(x, w - ROPE_HALF, 1) * s1 + pltpu.roll(x, ROPE_HALF, 1) * s2


def _rope_apply_t(dy, c, s1, s2):
    w = dy.shape[1]
    return dy * c + pltpu.roll(dy * s1, ROPE_HALF, 1) + pltpu.roll(dy * s2, w - ROPE_HALF, 1)


def _dil_qkv(proj, tables):
    def fn(q, k, v, c, s1, s2):
        c, s1, s2 = (_tile_lanes(t, DIL_WIDTH) for t in (c, s1, s2))
        return _rope_apply(q, c, s1, s2) * QK_SCALE, _rope_apply(k, c, s1, s2), v

    w = DIL_WIDTH
    return _rowwise(fn, "dil_rope", SEQ, 256,
                    [(proj, w, COL_QA // w), (proj, w, COL_KA // w), (proj, w, COL_VA // w)]
                    + [(t, LANES, 0) for t in tables], [],
                    [(w, BF16)] * 3)


def _dil_qk_bwd(dq, dk, tables):
    def fn(dq, dk, c, s1, s2):
        c, s1, s2 = (_tile_lanes(t, DIL_WIDTH) for t in (c, s1, s2))
        return _rope_apply_t(dq, c, s1, s2), _rope_apply_t(dk, c, s1, s2)

    w = DIL_WIDTH
    return _rowwise(fn, "dil_rope_bwd", SEQ, 256, [(dq, w, 0), (dk, w, 0)] + [(t, LANES, 0) for t in tables], [],
                    [(w, BF16)] * 2)


def _residue_major(a, front_pad=0):
    cols = []
    for g, d in enumerate(DILATIONS):
        part = a[:, g * 256:(g + 1) * 256]
        cols.append(part.reshape(SEQ // d, d, 256).transpose(1, 0, 2).reshape(SEQ, 256))
    out = jnp.concatenate(cols, axis=1)
    return jnp.pad(out, ((front_pad, 0), (0, 0))) if front_pad else out


def _token_major(a):
    cols = []
    for g, d in enumerate(DILATIONS):
        part = a[:, g * 256:(g + 1) * 256]
        cols.append(part.reshape(d, SEQ // d, 256).transpose(1, 0, 2).reshape(SEQ, 256))
    return jnp.concatenate(cols, axis=1)


def _dil_prev_limit(pair, n):
    g = pair // 2
    mask = jnp.where(g == 0, 15, jnp.where(g == 1, 3, 0))
    return jnp.where(jnp.bitwise_and(n, mask) != 0, 0, BLOCK)


def _dil_valid(limit):
    row = lax.broadcasted_iota(jnp.int32, (BLOCK, 2 * BLOCK), 0)
    col = lax.broadcasted_iota(jnp.int32, (BLOCK, 2 * BLOCK), 1)
    dist = col - row
    return jnp.logical_and(dist >= jnp.where(col < BLOCK, limit, -BLOCK), dist <= BLOCK)


def _upper_half():
    return lax.broadcasted_iota(jnp.int32, (1, LANES), 1) >= HEAD_DIM


def _dil_fwd(q, k, v):
    def body(q_ref, k_ref, v_ref, o_ref, lse_ref):
        pair = pl.program_id(0)
        upper = _upper_half()

        def block(n, carry):
            r0 = pl.multiple_of(n * BLOCK, BLOCK)
            qb = q_ref[pl.ds(r0, BLOCK), :]
            kw = k_ref[pl.ds(r0, 2 * BLOCK), :]
            vw = v_ref[pl.ds(r0, 2 * BLOCK), :]
            valid = _dil_valid(_dil_prev_limit(pair, n))
            outs, lses = [], []
            for head_mask in (jnp.logical_not(upper), upper):
                s = jnp.where(valid, _dot_nt(qb, jnp.where(head_mask, kw, 0)), NEG_INF)
                m = jnp.max(s, axis=-1, keepdims=True)
                p = jnp.exp(s - m)
                den = jnp.sum(p, axis=-1, keepdims=True)
                outs.append(_dot_nn((p / den).astype(BF16), jnp.where(head_mask, vw, 0)))
                lses.append(m + jnp.log(den))
            o_ref[pl.ds(r0, BLOCK), :] = outs[0] + outs[1]
            lse_ref[pl.ds(r0, BLOCK), :] = jnp.where(upper, lses[1], lses[0])
            return carry

        lax.fori_loop(0, N_BLOCKS, block, 0)

    return pl.pallas_call(
        body, name="dil_attn_fwd", grid=(DIL_WIDTH // LANES,),
        in_specs=[pl.BlockSpec((SEQ, LANES), lambda p: (0, p)), pl.BlockSpec((SEQ + BLOCK, LANES), lambda p: (0, p)),
                  pl.BlockSpec((SEQ + BLOCK, LANES), lambda p: (0, p))],
        out_specs=[pl.BlockSpec((SEQ, LANES), lambda p: (0, p))] * 2,
        out_shape=[jax.ShapeDtypeStruct((SEQ, DIL_WIDTH), F32)] * 2,
        compiler_params=pltpu.CompilerParams(dimension_semantics=("parallel",)),
    )(q, k, v)


def _dil_bwd(q, k, v, do, lse, c):
    def body(q_ref, k_ref, v_ref, do_ref, lse_ref, c_ref, dq_ref, dk_ref, dv_ref, dk_acc, dv_acc):
        pair = pl.program_id(0)
        upper = _upper_half()
        dk_acc[...] = jnp.zeros_like(dk_acc)
        dv_acc[...] = jnp.zeros_like(dv_acc)

        def block(n, carry):
            r0 = pl.multiple_of(n * BLOCK, BLOCK)
            qb, dob = q_ref[pl.ds(r0, BLOCK), :], do_ref[pl.ds(r0, BLOCK), :]
            kw = k_ref[pl.ds(r0, 2 * BLOCK), :]
            vw = v_ref[pl.ds(r0, 2 * BLOCK), :]
            lse_t, c_t = lse_ref[pl.ds(r0, BLOCK), :], c_ref[pl.ds(r0, BLOCK), :]
            valid = _dil_valid(_dil_prev_limit(pair, n))
            dq = None
            for e, head_mask in enumerate((jnp.logical_not(upper), upper)):
                km, vm = jnp.where(head_mask, kw, 0), jnp.where(head_mask, vw, 0)
                lse_col = lse_t[:, e * HEAD_DIM:e * HEAD_DIM + 1]
                c_col = c_t[:, e * HEAD_DIM:e * HEAD_DIM + 1]
                p = jnp.where(valid, jnp.exp(_dot_nt(qb, km) - lse_col), 0.0)
                ds = (p * (_dot_nt(dob, vm) + c_col)).astype(BF16)
                part = _dot_nn(ds, km)
                dq = part if dq is None else dq + part
                dk_acc[pl.ds(r0, 2 * BLOCK), :] += _dot_tn(ds, jnp.where(head_mask, qb, 0))
                dv_acc[pl.ds(r0, 2 * BLOCK), :] += _dot_tn(p.astype(BF16), jnp.where(head_mask, dob, 0))
            dq_ref[pl.ds(r0, BLOCK), :] = dq * QK_SCALE
            return carry

        lax.fori_loop(0, N_BLOCKS, block, 0)
        dk_ref[...] = dk_acc[BLOCK:, :]
        dv_ref[...] = dv_acc[BLOCK:, :]

    tok = pl.BlockSpec((SEQ, LANES), lambda p: (0, p))
    padded = pl.BlockSpec((SEQ + BLOCK, LANES), lambda p: (0, p))
    shape = jax.ShapeDtypeStruct((SEQ, DIL_WIDTH), F32)
    return pl.pallas_call(
        body, name="dil_attn_bwd", grid=(DIL_WIDTH // LANES,),
        in_specs=[tok, padded, padded, tok, tok, tok], out_specs=[tok, tok, tok], out_shape=[shape] * 3,
        scratch_shapes=[pltpu.VMEM((SEQ + BLOCK, LANES), F32)] * 2,
        compiler_params=pltpu.CompilerParams(dimension_semantics=("parallel",)),
    )(q, k, v, do, lse, c)


def _group_weights(l0, l1, l2):
    m = jnp.maximum(jnp.maximum(l0, l1), l2)
    e0, e1, e2 = jnp.exp(l0 - m), jnp.exp(l1 - m), jnp.exp(l2 - m)
    tot = e0 + e1 + e2
    return e0 / tot, e1 / tot, e2 / tot


def _dil_combine(o, lse, deps=()):
    def fn(o0, o1, o2, l0, l1, l2):
        w0, w1, w2 = _group_weights(l0, l1, l2)
        return w0 * o0 + w1 * o1 + w2 * o2

    w = DIL_OUT_WIDTH
    return _rowwise(fn, "dil_combine", SEQ, 512, [(o, w, g) for g in range(3)] + [(lse, w, g) for g in range(3)], [],
                    [(w, F32)], deps=deps)[0]


def _dil_combine_bwd(d_out, o, lse):
    w = DIL_OUT_WIDTH

    def fn(d, o0, o1, o2, l0, l1, l2):
        row = lax.broadcasted_iota(jnp.int32, (w, w), 0) // HEAD_DIM
        col = lax.broadcasted_iota(jnp.int32, (w, w), 1) // HEAD_DIM
        same_head = jnp.where(row == col, 1.0, 0.0).astype(BF16)
        ws = _group_weights(l0, l1, l2)
        dws = [_dot3_nn(d * og, same_head) for og in (o0, o1, o2)]
        mean = ws[0] * dws[0] + ws[1] * dws[1] + ws[2] * dws[2]
        do = jnp.concatenate([wg * d for wg in ws], axis=1)
        c = jnp.concatenate([-wg * mean for wg in ws], axis=1)
        return do, c

    return _rowwise(fn, "dil_combine_bwd", SEQ, 256,
                    [(d_out, w, 0)] + [(o, w, g) for g in range(3)] + [(lse, w, g) for g in range(3)], [],
                    [(DIL_WIDTH, BF16), (DIL_WIDTH, F32)])


def _fox_qkv(proj):
    def fn(q, k, v):
        return q * QK_SCALE, k, v

    w = FOX_WIDTH
    return _rowwise(fn, "fox_cast", SEQ, 256,
                    [(proj, w, COL_QB // w), (proj, w, COL_KB // w), (proj, w, COL_VB // w)], [], [(w, BF16)] * 3)


def _log1p(e):
    u = 1.0 + e
    return jnp.where(u == 1.0, e, jnp.log(u) * (e / (u - 1.0)))


def _fox_gate(proj, b_pad):
    def body(f_ref, b_ref, o_ref):
        z = f_ref[...] + b_ref[...]
        logf = (jnp.minimum(z, 0.0) - _log1p(jnp.exp(-jnp.abs(z)))).T[:F_ROWS]
        row = lax.broadcasted_iota(jnp.int32, (BLOCK, BLOCK), 0)
        col = lax.broadcasted_iota(jnp.int32, (BLOCK, BLOCK), 1)
        before = jnp.where(row <= col, 1.0, 0.0).astype(BF16)
        carry = jnp.zeros((F_ROWS, 1), F32)
        for blk in range(N_BLOCKS):
            run = _dot3_nn(logf[:, blk * BLOCK:(blk + 1) * BLOCK], before) + carry
            o_ref[:, blk * BLOCK:(blk + 1) * BLOCK] = run
            carry = run[:, BLOCK - 1:BLOCK]

    return pl.pallas_call(
        body, name="fox_gate", grid=(1,),
        in_specs=[pl.BlockSpec((SEQ, LANES), lambda i: (0, COL_F // LANES)), pl.BlockSpec((1, LANES), lambda i: (0, 0))],
        out_specs=pl.BlockSpec((F_ROWS, SEQ), lambda i: (0, 0)),
        out_shape=jax.ShapeDtypeStruct((F_ROWS, SEQ), F32),
    )(proj, b_pad)


def _fox_gate_bwd(d_cum, proj, b_pad):
    def body(d_ref, f_ref, b_ref, dz_ref, db_ref):
        row = lax.broadcasted_iota(jnp.int32, (BLOCK, BLOCK), 0)
        col = lax.broadcasted_iota(jnp.int32, (BLOCK, BLOCK), 1)
        after = jnp.where(row >= col, 1.0, 0.0).astype(BF16)
        carry = jnp.zeros((F_ROWS, 1), F32)
        parts = [None] * N_BLOCKS
        for blk in reversed(range(N_BLOCKS)):
            run = _dot3_nn(d_ref[:, blk * BLOCK:(blk + 1) * BLOCK], after) + carry
            parts[blk] = run
            carry = run[:, 0:1]
        dlogf = jnp.concatenate(parts, axis=1)
        dlogf = jnp.concatenate([dlogf, jnp.zeros((LANES - F_ROWS, SEQ), F32)], axis=0).T
        dz = dlogf * _sigmoid(-(f_ref[...] + b_ref[...]))
        dz_ref[...] = dz.astype(BF16)
        db_ref[...] = jnp.sum(dz, axis=0, keepdims=True)

    return pl.pallas_call(
        body, name="fox_gate_bwd", grid=(1,),
        in_specs=[pl.BlockSpec((F_ROWS, SEQ), lambda i: (0, 0)),
                  pl.BlockSpec((SEQ, LANES), lambda i: (0, COL_F // LANES)), pl.BlockSpec((1, LANES), lambda i: (0, 0))],
        out_specs=[pl.BlockSpec((SEQ, LANES), lambda i: (0, 0)), pl.BlockSpec((1, LANES), lambda i: (0, 0))],
        out_shape=[jax.ShapeDtypeStruct((SEQ, LANES), BF16), jax.ShapeDtypeStruct((1, LANES), F32)],
    )(d_cum, proj, b_pad)


FOX_TILE = 256
FOX_TILES = SEQ // FOX_TILE


def _row_to_col(row):
    n = row.shape[1]
    eye = lax.broadcasted_iota(jnp.int32, (n, n), 0) == lax.broadcasted_iota(jnp.int32, (n, n), 1)
    return jnp.sum(jnp.where(eye, row, 0.0), axis=1, keepdims=True)


def _fox_scores(q_tile, km, f_row, i):
    t = FOX_TILE
    ext = (i + 1) * t
    f_q = _row_to_col(f_row[:, i * t:(i + 1) * t])
    s = _dot_nt(q_tile, km[:ext]) + (f_q - f_row[:, :ext])
    row = lax.broadcasted_iota(jnp.int32, (t, ext), 0) + i * t
    col = lax.broadcasted_iota(jnp.int32, (t, ext), 1)
    return s, col <= row


def _fox_fwd(q, k, v, f_rows):
    t = FOX_TILE

    def body(q_ref, k_ref, v_ref, f_ref, o_ref, lse_ref):
        pair = pl.program_id(0)
        upper = _upper_half()
        masks = (jnp.logical_not(upper), upper)
        kms = [jnp.where(hm, k_ref[...], 0) for hm in masks]
        vms = [jnp.where(hm, v_ref[...], 0) for hm in masks]
        f_row = [f_ref[pl.ds(2 * pair + e, 1), :] for e in range(2)]
        for i in range(FOX_TILES):
            q_tile = q_ref[i * t:(i + 1) * t, :]
            outs, lses = [], []
            for e in range(2):
                s, causal = _fox_scores(q_tile, kms[e], f_row[e], i)
                s = jnp.where(causal, s, NEG_INF)
                m = jnp.max(s, axis=-1, keepdims=True)
                p = jnp.exp(s - m)
                den = jnp.sum(p, axis=-1, keepdims=True)
                outs.append(_dot_nn((p / den).astype(BF16), vms[e][:(i + 1) * t]))
                lses.append(m + jnp.log(den))
            o_ref[i * t:(i + 1) * t, :] = outs[0] + outs[1]
            lse_ref[i * t:(i + 1) * t, :] = jnp.where(upper, lses[1], lses[0])

    tok = pl.BlockSpec((SEQ, LANES), lambda p: (0, p))
    return pl.pallas_call(
        body, name="fox_attn_fwd", grid=(FOX_WIDTH // LANES,),
        in_specs=[tok, tok, tok, pl.BlockSpec((F_ROWS, SEQ), lambda p: (0, 0))], out_specs=[tok, tok],
        out_shape=[jax.ShapeDtypeStruct((SEQ, FOX_WIDTH), F32)] * 2,
        compiler_params=pltpu.CompilerParams(
            dimension_semantics=("parallel",), vmem_limit_bytes=_vmem_limit(8 * t * SEQ * 4)),
    )(q, k, v, f_rows)


def _fox_bwd(q, k, v, do, lse, f_rows):
    t = FOX_TILE

    def body(q_ref, k_ref, v_ref, do_ref, lse_ref, f_ref, dq_ref, dk_ref, dv_ref, df_ref):
        pair = pl.program_id(0)
        upper = _upper_half()
        masks = (jnp.logical_not(upper), upper)
        kms = [jnp.where(hm, k_ref[...], 0) for hm in masks]
        vms = [jnp.where(hm, v_ref[...], 0) for hm in masks]
        f_row = [f_ref[pl.ds(2 * pair + e, 1), :] for e in range(2)]
        dk_ref[...] = jnp.zeros_like(dk_ref)
        dv_ref[...] = jnp.zeros_like(dv_ref)
        df_ref[...] = jnp.zeros_like(df_ref)
        for i in range(FOX_TILES):
            ext = (i + 1) * t
            q_tile, do_tile = q_ref[i * t:(i + 1) * t, :], do_ref[i * t:(i + 1) * t, :]
            lse_t = lse_ref[i * t:(i + 1) * t, :]
            dq = None
            for e in range(2):
                s, causal = _fox_scores(q_tile, kms[e], f_row[e], i)
                p = jnp.where(causal, jnp.exp(s - lse_t[:, e * HEAD_DIM:e * HEAD_DIM + 1]), 0.0)
                dp = _dot_nt(do_tile, vms[e][:ext])
                ds = p * (dp - jnp.sum(p * dp, axis=-1, keepdims=True))
                df_ref[0, e:e + 1, :ext] -= jnp.sum(ds, axis=0, keepdims=True)
                ds = ds.astype(BF16)
                part = _dot_nn(ds, kms[e][:ext])
                dq = part if dq is None else dq + part
                dk_ref[:ext, :] += _dot_tn(ds, jnp.where(masks[e], q_tile, 0))
                dv_ref[:ext, :] += _dot_tn(p.astype(BF16), jnp.where(masks[e], do_tile, 0))
            dq_ref[i * t:(i + 1) * t, :] = dq * QK_SCALE

    tok = pl.BlockSpec((SEQ, LANES), lambda p: (0, p))
    shape = jax.ShapeDtypeStruct((SEQ, FOX_WIDTH), F32)
    return pl.pallas_call(
        body, name="fox_attn_bwd", grid=(FOX_WIDTH // LANES,),
        in_specs=[tok] * 5 + [pl.BlockSpec((F_ROWS, SEQ), lambda p: (0, 0))],
        out_specs=[tok, tok, tok, pl.BlockSpec((1, SUBLANES, SEQ), lambda p: (p, 0, 0))],
        out_shape=[shape, shape, shape, jax.ShapeDtypeStruct((FOX_WIDTH // LANES, SUBLANES, SEQ), F32)],
        compiler_params=pltpu.CompilerParams(
            dimension_semantics=("parallel",), vmem_limit_bytes=_vmem_limit(10 * t * SEQ * 4)),
    )(q, k, v, do, lse, f_rows)


MIX_TILE = 256


def _mix_out(out_a, out_b, proj, x, wt_pa, wt_pb, w_out, g_post, g_ffn_pre):
    tm = MIX_TILE

    def body(a_ref, b_ref, ga_ref, gb_ref, x_ref, wpa_ref, wpb_ref, wo_ref, g2_ref, g3_ref,
             merged_ref, mix_ref, x1_ref, h2_ref):
        ya = _dot_nt(a_ref[...].astype(BF16), wpa_ref[...])
        yb = _dot_nt(b_ref[...].astype(BF16), wpb_ref[...])
        merged = (_sigmoid(ga_ref[...]) * ya + _sigmoid(gb_ref[...]) * yb).astype(BF16)
        merged_ref[...] = merged
        mix = _dot_nn(merged, wo_ref[...])
        mix_ref[...] = mix
        x1 = x_ref[...] + mix * _rms_scale(mix) * g2_ref[...]
        x1_ref[...] = x1
        h2_ref[...] = (x1 * _rms_scale(x1) * g3_ref[...]).astype(BF16)

    def rows(w, cb=0):
        return pl.BlockSpec((tm, w), lambda i, cb=cb: (i, cb))

    def whole(a):
        return pl.BlockSpec(a.shape, lambda i: (0, 0))

    d = D_MODEL
    blk = _nbytes((tm, d), F32) * 6 + sum(_nbytes(a.shape, BF16) for a in (wt_pa, wt_pb, w_out))
    return pl.pallas_call(
        body, name="mix_out", grid=(SEQ // tm,),
        in_specs=[rows(DIL_OUT_WIDTH), rows(FOX_WIDTH), rows(d, COL_GA // d), rows(d, COL_GB // d), rows(d),
                  whole(wt_pa), whole(wt_pb), whole(w_out), whole(g_post), whole(g_ffn_pre)],
        out_specs=[rows(d)] * 4,
        out_shape=[jax.ShapeDtypeStruct((SEQ, d), dt) for dt in (BF16, F32, F32, BF16)],
        compiler_params=pltpu.CompilerParams(dimension_semantics=("parallel",), vmem_limit_bytes=_vmem_limit(blk)),
    )(out_a, out_b, proj, proj, x, wt_pa, wt_pb, w_out, g_post, g_ffn_pre)


def _mix_out_bwd(dmix, out_a, out_b, proj, wt_pa, wt_pb, w_out, deps=()):
    tm = MIX_TILE

    def body(dm_ref, a_ref, b_ref, ga_ref, gb_ref, wpa_ref, wpb_ref, wo_ref, *rest):
        dga_ref, dgb_ref, dya_ref, dyb_ref, da_ref, db_ref = rest[len(deps):]
        dmerged = _dot_nt(dm_ref[...], wo_ref[...])
        ya = _dot_nt(a_ref[...].astype(BF16), wpa_ref[...])
        yb = _dot_nt(b_ref[...].astype(BF16), wpb_ref[...])
        sa, sb = _sigmoid(ga_ref[...]), _sigmoid(gb_ref[...])
        dga_ref[...] = (dmerged * ya * (sa * (1.0 - sa))).astype(BF16)
        dgb_ref[...] = (dmerged * yb * (sb * (1.0 - sb))).astype(BF16)
        dya = (dmerged * sa).astype(BF16)
        dyb = (dmerged * sb).astype(BF16)
        dya_ref[...] = dya
        dyb_ref[...] = dyb
        da_ref[...] = _dot_nn(dya, wpa_ref[...])
        db_ref[...] = _dot_nn(dyb, wpb_ref[...]).astype(BF16)

    def rows(w, cb=0):
        return pl.BlockSpec((tm, w), lambda i, cb=cb: (i, cb))

    def whole(a):
        return pl.BlockSpec(a.shape, lambda i: (0, 0))

    d = D_MODEL
    blk = _nbytes((tm, d), F32) * 8 + sum(_nbytes(a.shape, BF16) for a in (wt_pa, wt_pb, w_out))
    return pl.pallas_call(
        body, name="mix_out_bwd", grid=(SEQ // tm,),
        in_specs=[rows(d), rows(DIL_OUT_WIDTH), rows(FOX_WIDTH), rows(d, COL_GA // d), rows(d, COL_GB // d),
                  whole(wt_pa), whole(wt_pb), whole(w_out)] + [_ANY] * len(deps),
        out_specs=[rows(d)] * 4 + [rows(DIL_OUT_WIDTH), rows(FOX_WIDTH)],
        out_shape=[jax.ShapeDtypeStruct((SEQ, d), BF16)] * 4
        + [jax.ShapeDtypeStruct((SEQ, DIL_OUT_WIDTH), F32), jax.ShapeDtypeStruct((SEQ, FOX_WIDTH), BF16)],
        compiler_params=pltpu.CompilerParams(dimension_semantics=("parallel",), vmem_limit_bytes=_vmem_limit(blk)),
    )(dmix, out_a, out_b, proj, proj, wt_pa, wt_pb, w_out, *deps)


FFN_TM, FFN_TN = 1024, 256


def _ffn_up(h2, wt_gate, wt_up):
    tm, tn = FFN_TM, FFN_TN

    def body(h_ref, wg_ref, wu_ref, gate_ref, up_ref, act_ref):
        gate = _dot_nt(h_ref[...], wg_ref[...])
        up = _dot_nt(h_ref[...], wu_ref[...])
        gate_ref[...] = gate
        up_ref[...] = up
        act_ref[...] = (gate * _sigmoid(gate) * up).astype(BF16)

    tile = pl.BlockSpec((tm, tn), lambda i, j: (i, j))
    w_spec = pl.BlockSpec((tn, D_MODEL), lambda i, j: (j, 0))
    return pl.pallas_call(
        body, name="ffn_up", grid=(SEQ // tm, D_FF // tn),
        in_specs=[pl.BlockSpec((tm, D_MODEL), lambda i, j: (i, 0)), w_spec, w_spec],
        out_specs=[tile, tile, tile],
        out_shape=[jax.ShapeDtypeStruct((SEQ, D_FF), dt) for dt in (F32, F32, BF16)],
        compiler_params=pltpu.CompilerParams(
            dimension_semantics=("parallel", "parallel"), vmem_limit_bytes=_vmem_limit(8 * 2**20)),
    )(h2, wt_gate, wt_up)


def _ffn_act_bwd(dff, w_down, gate, up):
    tm, tn = FFN_TM, FFN_TN

    def body(d_ref, wd_ref, gate_ref, up_ref, dgate_ref, dup_ref):
        dact = _dot_nt(d_ref[...], wd_ref[...])
        gate = gate_ref[...]
        sg = _sigmoid(gate)
        dgate_ref[...] = (dact * up_ref[...] * (sg * (1.0 + gate * (1.0 - sg)))).astype(BF16)
        dup_ref[...] = (dact * (gate * sg)).astype(BF16)

    tile = pl.BlockSpec((tm, tn), lambda i, j: (i, j))
    return pl.pallas_call(
        body, name="ffn_act_bwd", grid=(SEQ // tm, D_FF // tn),
        in_specs=[pl.BlockSpec((tm, D_MODEL), lambda i, j: (i, 0)), pl.BlockSpec((tn, D_MODEL), lambda i, j: (j, 0)),
                  tile, tile],
        out_specs=[tile, tile],
        out_shape=[jax.ShapeDtypeStruct((SEQ, D_FF), BF16)] * 2,
        compiler_params=pltpu.CompilerParams(
            dimension_semantics=("parallel", "parallel"), vmem_limit_bytes=_vmem_limit(8 * 2**20)),
    )(dff, w_down, gate, up)


def _loss_head(ff, x1, target, g_post):
    def fn(ff, x1, tgt, g):
        r = _rms_scale(ff)
        nrm = ff * r
        err = (x1 + nrm * g) - tgt
        loss = 0.5 * jnp.sum(jnp.mean(err * err, axis=-1, keepdims=True), axis=0, keepdims=True)
        dy = err * (1.0 / D_MODEL)
        u = dy * g
        dff = r * u - ff * (r * r * r) * jnp.mean(u * ff, axis=-1, keepdims=True)
        return dy, dff, jnp.broadcast_to(loss, (1, LANES)), jnp.sum(dy * nrm, axis=0, keepdims=True)

    d = D_MODEL
    return _rowwise(fn, "loss_head", SEQ, 256, [(ff, d, 0), (x1, d, 0), (target, d, 0)], [g_post],
                    [(d, F32), (d, BF16)], [LANES, d])


def _post_ffn_bwd(dh2, x1, dy, mix, g_ffn_pre, g_mix_post):
    def fn(dh2, x1, dy, mix, g3, g2):
        dx, dg3 = _rms_bwd(x1, dh2, g3)
        dx1 = dy + dx
        dmix, dg2 = _rms_bwd(mix, dx1, g2)
        return dx1, dmix, dg3, dg2

    d = D_MODEL
    return _rowwise(fn, "post_ffn_bwd", SEQ, 256, [(dh2, d, 0), (x1, d, 0), (dy, d, 0), (mix, d, 0)],
                    [g_ffn_pre, g_mix_post], [(d, F32), (d, BF16)], [d, d])


def _input_bwd(dh, x, dx1, g_pre, deps=()):
    def fn(dh, x, dx1, g):
        dx, dg = _rms_bwd(x, dh, g)
        return dx1 + dx, dg

    d = D_MODEL
    return _rowwise(fn, "input_bwd", SEQ, 256, [(dh, d, 0), (x, d, 0), (dx1, d, 0)], [g_pre], [(d, F32)], [d],
                    deps=deps)


def _adam_math(w, g, m, v):
    m = ADAM_B1 * m + (1.0 - ADAM_B1) * g
    v = ADAM_B2 * v + (1.0 - ADAM_B2) * (g * g)
    m_hat = m / (1.0 - ADAM_B1 ** ADAM_STEP)
    v_hat = v / (1.0 - ADAM_B2 ** ADAM_STEP)
    delta = -ADAM_LR * (m_hat / (jnp.sqrt(v_hat) + ADAM_EPS) + ADAM_WD * w)
    return delta, m, v


def _adam(w, g, m, v, name):
    r, c = w.shape
    tc = _col_tile(r, c)

    def body(w_ref, g_ref, m_ref, v_ref, d_ref, nm_ref, nv_ref):
        d_ref[...], nm_ref[...], nv_ref[...] = _adam_math(w_ref[...], g_ref[...], m_ref[...], v_ref[...])

    spec = pl.BlockSpec((r, tc), lambda j: (0, j))
    return pl.pallas_call(
        body, name=name, grid=(c // tc,), in_specs=[spec] * 4, out_specs=[spec] * 3,
        out_shape=[jax.ShapeDtypeStruct((r, c), F32)] * 3,
        compiler_params=pltpu.CompilerParams(dimension_semantics=("parallel",)),
    )(w, g, m, v)


SMALL_ROWS = 5 * SUBLANES


def _adam_small(gathered, w, m, v):
    def body(ga_ref, w_ref, m_ref, v_ref, g_ref, d_ref, nm_ref, nv_ref):
        g = ga_ref[0]
        for dev in range(1, N_DEV):
            g = g + ga_ref[dev]
        g_ref[...] = g
        d_ref[...], nm_ref[...], nv_ref[...] = _adam_math(w_ref[...], g, m_ref[...], v_ref[...])

    shape = jax.ShapeDtypeStruct(w.shape, F32)
    return pl.pallas_call(body, name="adam_small", out_shape=[shape] * 4)(gathered, w, m, v)


def _small_pack(rows):
    d = D_MODEL
    return jnp.concatenate(
        [jnp.broadcast_to(jnp.pad(r, ((0, 0), (0, d - r.shape[1]))), (SUBLANES, d)) for r in rows], axis=0)


def _proj_weight_t(wt_in_full):
    w = wt_in_full
    z = lambda n: jnp.zeros((n, D_MODEL), w.dtype)
    return jnp.concatenate([w[3848:5896], z(256), w[0:3840], w[3840:3848], z(PROJ_COLS - COL_F - 8)], axis=0)


def _proj_weight_grad_t(dwt_r):
    return jnp.concatenate([dwt_r[COL_QA:COL_F], dwt_r[COL_F:COL_F + 8], dwt_r[0:2048]], axis=0)


def kernel(x, w_in, w_proj_a, w_proj_b, w_out, b_forget, w_ffn_gate, w_ffn_up, w_ffn_down, norm_mix_pre, norm_mix_post, norm_ffn_pre, norm_ffn_post, loss_target, m_w_in, m_w_proj_a, m_w_proj_b, m_w_out, m_b_forget, m_w_ffn_gate, m_w_ffn_up, m_w_ffn_down, m_norm_mix_pre, m_norm_mix_post, m_norm_ffn_pre, m_norm_ffn_post, v_w_in, v_w_proj_a, v_w_proj_b, v_w_out, v_b_forget, v_w_ffn_gate, v_w_ffn_up, v_w_ffn_down, v_norm_mix_pre, v_norm_mix_post, v_norm_ffn_pre, v_norm_ffn_post):
    d = D_MODEL
    names = ("w_in", "w_proj_a", "w_proj_b", "w_out", "w_ffn_gate", "w_ffn_up", "w_ffn_down")
    col_sharded = ("w_in", "w_proj_a", "w_proj_b", "w_ffn_gate", "w_ffn_up")

    def row_shards(arrs):
        return {k: (a[0].T if k in col_sharded else a[0]) for k, a in zip(names, arrs)}

    shards = row_shards((w_in, w_proj_a, w_proj_b, w_out, w_ffn_gate, w_ffn_up, w_ffn_down))
    moments_m = row_shards((m_w_in, m_w_proj_a, m_w_proj_b, m_w_out, m_w_ffn_gate, m_w_ffn_up, m_w_ffn_down))
    moments_v = row_shards((v_w_in, v_w_proj_a, v_w_proj_b, v_w_out, v_w_ffn_gate, v_w_ffn_up, v_w_ffn_down))
    core = lax.axis_index("c").astype(jnp.int32).reshape(1)
    chip = (2 * lax.axis_index("x") + lax.axis_index("y")).astype(jnp.int32).reshape(1)
    x2, target = x[0], loss_target[0]

    me = 4 * lax.axis_index("x") + 2 * lax.axis_index("y") + lax.axis_index("c")
    mix_names, ffn_names = names[:4], names[4:]
    shards16 = {k: shards[k].astype(BF16) for k in names}

    def landing(k):
        return lax.dynamic_update_slice(lax.empty((N_DEV,) + shards[k].shape, BF16), shards16[k][None], (me, 0, 0))

    ag_mix = _exchange_start("ag_mix_chips_start", _gather_chips_copies, [shards16[k] for k in mix_names],
                             [landing(k) for k in mix_names], 3 * len(mix_names))
    h = _rowwise(lambda xb, g: xb * _rms_scale(xb) * g, "norm_mix_pre", SEQ, 256, [(x2, d, 0)], [norm_mix_pre],
                 [(d, BF16)], deps=[ag_mix.token])[0]
    lands = _exchange_wait("ag_mix_chips_wait", ag_mix, [h])
    ag_mix = _exchange_start("ag_mix_sibling_start", _gather_sibling_copies, [], lands, 4 * len(mix_names))
    ag_ffn = _exchange_start("ag_ffn_chips_start", _gather_chips_copies, [shards16[k] for k in ffn_names],
                             [landing(k) for k in ffn_names], 3 * len(ffn_names), after=[ag_mix.token])
    gathered = dict(zip(mix_names, _exchange_wait("ag_mix_sibling_wait", ag_mix, [ag_ffn.token])))
    wt_r = _proj_weight_t(gathered["w_in"].reshape(IN_COLS, d))
    wt_pa = gathered["w_proj_a"].reshape(d, DIL_OUT_WIDTH)
    wt_pb = gathered["w_proj_b"].reshape(d, FOX_WIDTH)
    w_o = gathered["w_out"].reshape(d, d)

    proj = _matmul([(h, wt_r)], "nt", F32, "in_proj", 1024, 896, 1024)
    tables = _rope_tables()
    qa_tok, ka_tok, va_tok = _dil_qkv(proj, tables)
    qa = _residue_major(qa_tok)
    ka, va = _residue_major(ka_tok, BLOCK), _residue_major(va_tok, BLOCK)
    o_dil, lse_dil = _dil_fwd(qa, ka, va)
    lands = _exchange_wait("ag_ffn_chips_wait", ag_ffn, [o_dil])
    ag_ffn = _exchange_start("ag_ffn_sibling_start", _gather_sibling_copies, [], lands, 4 * len(ffn_names))
    o_tok, lse_tok = _token_major(o_dil), _token_major(lse_dil)
    out_a = _dil_combine(o_tok, lse_tok, deps=[ag_ffn.token])

    b_pad = jnp.pad(b_forget, ((0, 0), (0, LANES - N_FOX_HEADS)))
    f_rows = _fox_gate(proj, b_pad)
    qb, kb, vb = _fox_qkv(proj)
    out_b, lse_fox = _fox_fwd(qb, kb, vb, f_rows)

    merged, mix, x1, h2 = _mix_out(out_a, out_b, proj, x2, wt_pa, wt_pb, w_o, norm_mix_post, norm_ffn_pre)

    gathered = dict(zip(ffn_names, _exchange_wait("ag_ffn_sibling_wait", ag_ffn, [merged])))
    wt_g = gathered["w_ffn_gate"].reshape(D_FF, d)
    wt_u = gathered["w_ffn_up"].reshape(D_FF, d)
    w_d = gathered["w_ffn_down"].reshape(D_FF, d)
    gate, up, act = _ffn_up(h2, wt_g, wt_u)
    ff = _matmul([(act, w_d)], "nn", F32, "ffn_down", 1024, 1024, 1408)
    dy, dff, loss_part, dg_ffn_post = _loss_head(ff, x1, target, norm_ffn_post)
    loss = lax.psum(loss_part[0, 0], ("x", "y", "c"))

    dgate, dup = _ffn_act_bwd(dff, w_d, gate, up)
    grads_t = {}
    grads_t["w_ffn_down"] = _matmul([(act, dff)], "tn", F32, "grad_w_ffn_down", 1408, 1024, 1024)
    grads_t["w_ffn_gate"] = _matmul([(dgate, h2)], "tn", F32, "grad_w_ffn_gate", 1408, 1024, 1024)
    grads_t["w_ffn_up"] = _matmul([(dup, h2)], "tn", F32, "grad_w_ffn_up", 1408, 1024, 1024)
    rs_ffn = _ReduceScatter("ffn", {k: grads_t[k] for k in ffn_names}, core, chip)
    dh2 = _matmul([(dgate, wt_g), (dup, wt_u)], "nn", F32, "ffn_up_bwd", 1024, 1024, 1408, deps=[rs_ffn.token])
    dx1, dmix, dg_ffn_pre, dg_mix_post = _post_ffn_bwd(dh2, x1, dy, mix, norm_ffn_pre, norm_mix_post)
    rs_ffn.start_chips([dmix])

    dga, dgb, dya, dyb, d_out_a, d_out_b = _mix_out_bwd(dmix, out_a, out_b, proj, wt_pa, wt_pb, w_o,
                                                        deps=[rs_ffn.token])
    grads_t["w_out"] = _matmul([(merged, dmix)], "tn", F32, "grad_w_out", 1024, 1024, 1024)
    grads_t["w_proj_a"] = _matmul([(dya, out_a)], "tn", F32, "grad_w_proj_a", 1024, DIL_OUT_WIDTH, SEQ)
    grads_t["w_proj_b"] = _matmul([(dyb, out_b)], "tn", F32, "grad_w_proj_b", 1024, FOX_WIDTH, SEQ)

    dq_fox, dk_fox, dv_fox, d_cum = _fox_bwd(qb, kb, vb, d_out_b, lse_fox, f_rows)
    d_cum_rows = jnp.pad(d_cum[:, :2].reshape(N_FOX_HEADS, SEQ), ((0, F_ROWS - N_FOX_HEADS), (0, 0)))
    dz, db_part = _fox_gate_bwd(d_cum_rows, proj, b_pad)

    do_tok, c_tok = _dil_combine_bwd(d_out_a, o_tok, lse_tok)
    dq_dil, dk_dil, dv_dil = _dil_bwd(qa, ka, va, _residue_major(do_tok), lse_dil, _residue_major(c_tok))
    dqa, dka = _dil_qk_bwd(_token_major(dq_dil), _token_major(dk_dil), tables)

    dproj = jnp.concatenate(
        [dga, dgb, jnp.zeros((SEQ, COL_QA - 2 * d), BF16), dqa, dka, _token_major(dv_dil).astype(BF16),
         dq_fox.astype(BF16), dk_fox.astype(BF16), dv_fox.astype(BF16), dz], axis=1)
    grads = rs_ffn.finish([dproj])
    big = {k: _adam(shards[k], grads[k], moments_m[k], moments_v[k], "adam_" + k) for k in ffn_names}

    dwt_r = _matmul([(dproj, h)], "tn", F32, "grad_w_in", 896, 1024, 1024)
    grads_t["w_in"] = _proj_weight_grad_t(dwt_r)
    rs_mix = _ReduceScatter("mix", {k: grads_t[k] for k in mix_names}, core, chip)
    dh = _matmul([(dproj, wt_r)], "nn", F32, "in_proj_bwd", 1024, 1024, 896, deps=[rs_mix.token])
    rs_mix.start_chips([dh])
    grad_x, dg_mix_pre = _input_bwd(dh, x2, dx1, norm_mix_pre, deps=[rs_mix.token])

    small_part = _small_pack([dg_mix_pre, dg_mix_post, dg_ffn_pre, dg_ffn_post, db_part])
    small_all = _all_gather([small_part], "small_grads_all_gather")[0]
    small_w = _small_pack([norm_mix_pre, norm_mix_post, norm_ffn_pre, norm_ffn_post, b_forget])
    small_m = _small_pack([m_norm_mix_pre, m_norm_mix_post, m_norm_ffn_pre, m_norm_ffn_post, m_b_forget])
    small_v = _small_pack([v_norm_mix_pre, v_norm_mix_post, v_norm_ffn_pre, v_norm_ffn_post, v_b_forget])
    small = _adam_small(small_all, small_w, small_m, small_v)

    grads.update(rs_mix.finish([small[0]] + [big[k][0] for k in ffn_names]))
    big.update({k: _adam(shards[k], grads[k], moments_m[k], moments_v[k], "adam_" + k) for k in mix_names})

    def leaves(i):
        def nat(k):
            a = grads[k] if i == 0 else big[k][i - 1]
            return (a.T if k in col_sharded else a)[None]

        s = small[i]
        gains = [s[r * SUBLANES:r * SUBLANES + 1] for r in range(4)]
        return [nat("w_in"), nat("w_proj_a"), nat("w_proj_b"), nat("w_out"),
                s[4 * SUBLANES:4 * SUBLANES + 1, :N_FOX_HEADS],
                nat("w_ffn_gate"), nat("w_ffn_up"), nat("w_ffn_down"), *gains]

    return (loss, grad_x[None], *leaves(0), *leaves(1), *leaves(2), *leaves(3))
```

```python
import functools
import math

import jax
import jax.numpy as jnp
from jax import lax
from jax.experimental import pallas as pl
from jax.experimental.pallas import tpu as pltpu

F32 = jnp.float32
BF16 = jnp.bfloat16
MESH = pl.DeviceIdType.MESH

D_MODEL = 1024
SEQ = 2048
HEAD_DIM = 64
BLOCK = 128
N_BLOCKS = SEQ // BLOCK
DILATIONS = (1, 4, 16)
N_FOX_HEADS = 8
DIL_WIDTH = 768
DIL_OUT_WIDTH = 256
FOX_WIDTH = 512
D_FF = 2816
ROPE_THETA = 500000.0
ROPE_DIM = HEAD_DIM // 4
ROPE_HALF = ROPE_DIM // 2
EPS = 1e-6
NEG_INF = -1e30
QK_SCALE = 1.0 / math.sqrt(HEAD_DIM)
IN_COLS = 5896
N_DEV = 8
IN_SHARD = IN_COLS // N_DEV

ADAM_LR = 0.001
ADAM_B1 = 0.9
ADAM_B2 = 0.999
ADAM_EPS = 1e-08
ADAM_WD = 0.01
ADAM_STEP = 10

V7X_VMEM_BYTES = 64 * 2**20
LANES = 128
SUBLANES = 8

PROJ_COLS = 6272
COL_GA, COL_GB = 0, 1024
COL_QA, COL_KA, COL_VA = 2304, 3072, 3840
COL_QB, COL_KB, COL_VB = 4608, 5120, 5632
COL_F = 6144
F_ROWS = 16


def _vmem_limit(block_bytes):
    want = 2 * block_bytes + 16 * 2**20
    return int(min(max(want, 32 * 2**20), V7X_VMEM_BYTES - 8 * 2**20))


def _nbytes(shape, dtype):
    return math.prod(shape) * jnp.dtype(dtype).itemsize


def _dot(a, b, dims):
    return lax.dot_general(a, b, (dims, ((), ())), preferred_element_type=F32)


def _dot_nn(a, b):
    return _dot(a, b, ((1,), (0,)))


def _dot_nt(a, b):
    return _dot(a, b, ((1,), (1,)))


def _dot_tn(a, b):
    return _dot(a, b, ((0,), (0,)))


def _sigmoid(z):
    return 1.0 / (1.0 + jnp.exp(-z))


def _split3(x):
    hi = x.astype(BF16)
    r1 = x - hi.astype(F32)
    mid = r1.astype(BF16)
    lo = (r1 - mid.astype(F32)).astype(BF16)
    return hi, mid, lo


def _dot3_nn(x, ones_matrix):
    hi, mid, lo = _split3(x)
    return (_dot_nn(hi, ones_matrix) + _dot_nn(mid, ones_matrix)) + _dot_nn(lo, ones_matrix)


def _rowwise(fn, name, n_rows, tm, row_ins, bcast_ins, row_outs, acc_outs=(), deps=()):
    n_in = len(row_ins) + len(bcast_ins)
    n_ro = len(row_outs)

    def body(*refs):
        res = fn(*[r[...] for r in refs[:n_in]])
        if not isinstance(res, (tuple, list)):
            res = (res,)
        outs = refs[n_in + len(deps):]
        for r, o in zip(res[:n_ro], outs[:n_ro]):
            o[...] = r.astype(o.dtype)
        first = pl.program_id(0) == 0
        for r, o in zip(res[n_ro:], outs[n_ro:]):
            _accumulate(o, r, first)

    in_specs = [pl.BlockSpec((tm, w), lambda i, cb=cb: (i, cb)) for _, w, cb in row_ins]
    in_specs += [pl.BlockSpec(a.shape, lambda i: (0, 0)) for a in bcast_ins]
    in_specs += [pl.BlockSpec(memory_space=pl.ANY)] * len(deps)
    out_specs = [pl.BlockSpec((tm, w), lambda i: (i, 0)) for w, _ in row_outs]
    out_specs += [pl.BlockSpec((1, w), lambda i: (0, 0)) for w in acc_outs]
    out_shape = [jax.ShapeDtypeStruct((n_rows, w), dt) for w, dt in row_outs]
    out_shape += [jax.ShapeDtypeStruct((1, w), F32) for w in acc_outs]
    blk = sum(_nbytes((tm, w), a.dtype) for a, w, _ in row_ins) + sum(_nbytes((tm, w), dt) for w, dt in row_outs)
    return pl.pallas_call(
        body, name=name, grid=(n_rows // tm,), in_specs=in_specs, out_specs=out_specs, out_shape=out_shape,
        compiler_params=pltpu.CompilerParams(
            dimension_semantics=("arbitrary" if acc_outs else "parallel",), vmem_limit_bytes=_vmem_limit(3 * blk)),
    )(*[a for a, _, _ in row_ins], *bcast_ins, *deps)


def _accumulate(o_ref, part, first):
    @pl.when(first)
    def _():
        o_ref[...] = part

    @pl.when(jnp.logical_not(first))
    def _():
        o_ref[...] += part


_MM_DIMS = {"nn": ((1,), (0,)), "nt": ((1,), (1,)), "tn": ((0,), (0,))}


def _matmul(pairs, mode, out_dtype, name, tm, tn, tk, deps=()):
    a0, b0 = pairs[0]
    if mode == "tn":
        kk, m = a0.shape
    else:
        m, kk = a0.shape
    n = b0.shape[0] if mode == "nt" else b0.shape[1]
    assert m % tm == 0 and n % tn == 0 and kk % tk == 0, (name, m, n, kk)
    nk = kk // tk
    n_pairs = len(pairs)
    dims = _MM_DIMS[mode]
    n_in = 2 * n_pairs + len(deps)

    def body(*refs):
        o_ref = refs[n_in]
        part = None
        for p in range(n_pairs):
            d = _dot(refs[2 * p][...].astype(BF16), refs[2 * p + 1][...].astype(BF16), dims)
            part = d if part is None else part + d
        if nk == 1:
            o_ref[...] = part.astype(o_ref.dtype)
            return
        acc = refs[n_in + 1]
        k = pl.program_id(2)

        @pl.when(k == 0)
        def _():
            acc[...] = part

        @pl.when(k > 0)
        def _():
            acc[...] += part

        @pl.when(k == nk - 1)
        def _():
            o_ref[...] = acc[...].astype(o_ref.dtype)

    if mode == "tn":
        a_spec = pl.BlockSpec((tk, tm), lambda i, j, k: (k, i))
    else:
        a_spec = pl.BlockSpec((tm, tk), lambda i, j, k: (i, k))
    if mode == "nt":
        b_spec = pl.BlockSpec((tn, tk), lambda i, j, k: (j, k))
    else:
        b_spec = pl.BlockSpec((tk, tn), lambda i, j, k: (k, j))
    blk = sum(_nbytes((tm, tk), a.dtype) + _nbytes((tk, tn), b.dtype) for a, b in pairs) + 2 * _nbytes((tm, tn), F32)
    flat = [a for pair in pairs for a in pair]
    return pl.pallas_call(
        body, name=name, grid=(m // tm, n // tn, nk),
        in_specs=[a_spec, b_spec] * n_pairs + [pl.BlockSpec(memory_space=pl.ANY)] * len(deps),
        out_specs=pl.BlockSpec((tm, tn), lambda i, j, k: (i, j)),
        out_shape=jax.ShapeDtypeStruct((m, n), out_dtype),
        scratch_shapes=[] if nk == 1 else [pltpu.VMEM((tm, tn), F32)],
        compiler_params=pltpu.CompilerParams(
            dimension_semantics=("parallel", "parallel", "arbitrary"), vmem_limit_bytes=_vmem_limit(blk)),
    )(*flat, *deps)


def _rms_scale(x):
    return lax.rsqrt(jnp.mean(x * x, axis=-1, keepdims=True) + EPS)


def _rms_bwd(xin, dyn, g):
    r = _rms_scale(xin)
    u = dyn * g
    dx = r * u - xin * (r * r * r) * jnp.mean(u * xin, axis=-1, keepdims=True)
    dg = jnp.sum(dyn * xin * r, axis=0, keepdims=True)
    return dx, dg


def _mesh_pos():
    return lax.axis_index("x"), lax.axis_index("y"), lax.axis_index("c")


def _all_gather(xs, name):
    n = len(xs)

    def body(*refs):
        x_refs, out_refs = refs[:n], refs[n:2 * n]
        send_sems, recv_sems, local_sems = refs[2 * n:]
        mx, my, mc = _mesh_pos()
        me, sib = (mx, my, mc), (mx, my, 1 - mc)
        chips = [(1 - mx, my), (mx, 1 - my), (1 - mx, 1 - my)]

        def slot(a, dev):
            px, py, pc = dev
            return out_refs[a].at[4 * px + 2 * py + pc]

        def copy(k, a, block, to, src=None):
            return pltpu.make_async_remote_copy(
                src_ref=slot(a, block) if src is None else src, dst_ref=slot(a, block),
                send_sem=send_sems.at[a * 7 + k], recv_sem=recv_sems.at[a * 7 + k],
                device_id=to, device_id_type=MESH)

        mine = [pltpu.make_async_copy(x_refs[a], slot(a, me), local_sems.at[a]) for a in range(n)]
        for cp in mine:
            cp.start()
        first = []
        for a in range(n):
            first.append(copy(0, a, me, sib, x_refs[a]))
            first += [copy(1 + j, a, me, (*chip, mc), x_refs[a]) for j, chip in enumerate(chips)]
        for cp in first:
            cp.start()
        passed = []
        for a in range(n):
            for j, chip in enumerate(chips):
                copy(1 + j, a, (*chip, mc), me).wait_recv()
                fwd = copy(4 + j, a, (*chip, mc), sib)
                fwd.start()
                passed.append(fwd)
        for a in range(n):
            copy(0, a, sib, me).wait_recv()
            for j, chip in enumerate(chips):
                copy(4 + j, a, (*chip, 1 - mc), me).wait_recv()
        for cp in first + passed:
            cp.wait_send()
        for cp in mine:
            cp.wait()

    hbm = pl.BlockSpec(memory_space=pl.ANY)
    return pl.pallas_call(
        body, name=name,
        out_shape=[jax.ShapeDtypeStruct((N_DEV,) + x.shape, x.dtype) for x in xs],
        in_specs=[hbm] * n, out_specs=[hbm] * n,
        scratch_shapes=[pltpu.SemaphoreType.DMA((7 * n,)), pltpu.SemaphoreType.DMA((7 * n,)),
                        pltpu.SemaphoreType.DMA((n,))],
    )(*xs)


_HBM = pl.BlockSpec(memory_space=pltpu.HBM)
_SEM = pl.BlockSpec(memory_space=pltpu.SEMAPHORE)
_ANY = pl.BlockSpec(memory_space=pl.ANY)
_DATAFLOW = pltpu.SideEffectType.DATAFLOW_SIDE_EFFECTING


def _flip_peer(flip):
    mx, my, mc = _mesh_pos()
    return (1 - mx if flip & 2 else mx, 1 - my if flip & 1 else my, mc)


def _remote(src, dst, send_sems, recv_sems, k, peer):
    return pltpu.make_async_remote_copy(src_ref=src, dst_ref=dst, send_sem=send_sems.at[k], recv_sem=recv_sems.at[k],
                                        device_id=peer, device_id_type=MESH)


def _gather_chips_copies(srcs, lands, send_sems, recv_sems):
    mx, my, mc = _mesh_pos()
    me = 4 * mx + 2 * my + mc
    return [_remote(srcs[a], lands[a].at[me], send_sems, recv_sems, 3 * a + flip - 1, _flip_peer(flip))
            for a in range(len(srcs)) for flip in (1, 2, 3)]


def _gather_sibling_copies(srcs, lands, send_sems, recv_sems):
    mx, my, mc = _mesh_pos()
    return [_remote(lands[a].at[2 * k + mc], lands[a].at[2 * k + mc], send_sems, recv_sems, 4 * a + k, (mx, my, 1 - mc))
            for a in range(len(lands)) for k in range(4)]


def _scatter_sibling_copies(srcs, lands, send_sems, recv_sems):
    mx, my, mc = _mesh_pos()
    return [_remote(srcs[a].at[k, 1 - mc], lands[a].at[k], send_sems, recv_sems, 4 * a + k, (mx, my, 1 - mc))
            for a in range(len(srcs)) for k in range(4)]


def _scatter_chips_copies(srcs, lands, send_sems, recv_sems):
    mx, my, _ = _mesh_pos()
    k0 = 2 * mx + my
    return [_remote(srcs[a].at[jnp.bitwise_xor(k0, flip)], lands[a].at[flip - 1], send_sems, recv_sems,
                    3 * a + flip - 1, _flip_peer(flip))
            for a in range(len(srcs)) for flip in (1, 2, 3)]


class _Exchange:
    def __init__(self, copies, n_src, send_sems, recv_sems, thru, token):
        self.copies, self.n_src, self.send_sems, self.recv_sems, self.thru, self.token = (
            copies, n_src, send_sems, recv_sems, thru, token)


def _exchange_start(name, copies, srcs, lands, n_copies, after=()):
    bufs = list(srcs) + list(lands)
    nb, ns = len(bufs), len(srcs)

    def body(*refs):
        send_sems, recv_sems = refs[nb + len(after)], refs[nb + len(after) + 1]
        for cp in copies(refs[:ns], refs[ns:nb], send_sems, recv_sems):
            cp.start()
        refs[-1][...] = jnp.zeros_like(refs[-1])

    out = pl.pallas_call(
        body, name=name,
        out_shape=(pltpu.SemaphoreType.DMA((n_copies,)), pltpu.SemaphoreType.DMA((n_copies,)),
                   *[pltpu.HBM(b.shape, b.dtype) for b in bufs], jax.ShapeDtypeStruct((SUBLANES, LANES), F32)),
        in_specs=[_HBM] * nb + [_ANY] * len(after),
        out_specs=(_SEM, _SEM, *[_HBM] * nb, pl.BlockSpec(memory_space=pltpu.VMEM)),
        input_output_aliases={i: 2 + i for i in range(nb)},
        compiler_params=pltpu.CompilerParams(has_side_effects=_DATAFLOW),
    )(*[pltpu.with_memory_space_constraint(b, pltpu.HBM) for b in bufs], *after)
    return _Exchange(copies, ns, out[0], out[1], list(out[2:2 + nb]), out[-1])


def _exchange_wait(name, ex, after):
    nb, ns = len(ex.thru), ex.n_src

    def body(*refs):
        for cp in ex.copies(refs[:ns], refs[ns:nb], refs[nb], refs[nb + 1]):
            cp.wait_send()
            cp.wait_recv()

    out = pl.pallas_call(
        body, name=name, out_shape=tuple(pltpu.HBM(b.shape, b.dtype) for b in ex.thru),
        in_specs=[_HBM] * nb + [_SEM, _SEM] + [_ANY] * len(after), out_specs=tuple([_HBM] * nb),
        input_output_aliases={i: i for i in range(nb)},
        compiler_params=pltpu.CompilerParams(has_side_effects=_DATAFLOW),
    )(*ex.thru, ex.send_sems, ex.recv_sems, *after)
    return list(out[:ns]), list(out[ns:])


def _col_tile(r, c):
    return next(t for t in (1024, 512, 256, 128) if c % t == 0 and (r * t * 4 <= 2**20 or t == 128))


def _add_sibling(g4, recv, core, name):
    _, _, r, c = g4.shape
    tc = _col_tile(r, c)

    def body(core_ref, g_ref, r_ref, o16_ref, o32_ref):
        s = g_ref[0, 0] + r_ref[0]
        o16_ref[0] = s.astype(BF16)
        o32_ref[0] = s

    out = pl.BlockSpec((1, r, tc), lambda k, j, core_ref: (k, 0, j))
    return pl.pallas_call(
        body, name=name,
        out_shape=[jax.ShapeDtypeStruct((4, r, c), BF16), jax.ShapeDtypeStruct((4, r, c), F32)],
        grid_spec=pltpu.PrefetchScalarGridSpec(
            num_scalar_prefetch=1, grid=(4, c // tc),
            in_specs=[pl.BlockSpec((1, 1, r, tc), lambda k, j, core_ref: (k, core_ref[0], 0, j)), out],
            out_specs=[out, out]),
        compiler_params=pltpu.CompilerParams(dimension_semantics=("parallel", "parallel")),
    )(core, g4, recv)


def _add_chips(p32, recv, chip, name):
    _, r, c = p32.shape
    tc = _col_tile(r, c)

    def body(chip_ref, p_ref, r_ref, o_ref):
        o_ref[...] = ((p_ref[0] + r_ref[0].astype(F32)) + r_ref[1].astype(F32)) + r_ref[2].astype(F32)

    return pl.pallas_call(
        body, name=name, out_shape=jax.ShapeDtypeStruct((r, c), F32),
        grid_spec=pltpu.PrefetchScalarGridSpec(
            num_scalar_prefetch=1, grid=(c // tc,),
            in_specs=[pl.BlockSpec((1, r, tc), lambda j, chip_ref: (chip_ref[0], 0, j)),
                      pl.BlockSpec((3, r, tc), lambda j, chip_ref: (0, 0, j))],
            out_specs=pl.BlockSpec((r, tc), lambda j, chip_ref: (0, j))),
        compiler_params=pltpu.CompilerParams(dimension_semantics=("parallel",)),
    )(chip, p32, recv)


class _ReduceScatter:
    def __init__(self, tag, grads_t, core, chip):
        self.tag, self.core, self.chip, self.names = tag, core, chip, list(grads_t)
        g4s = [g.reshape(4, 2, g.shape[0] // N_DEV, g.shape[1]) for g in grads_t.values()]
        lands = [lax.empty((4,) + g.shape[2:], F32) for g in g4s]
        self.ex = _exchange_start(f"rs_{tag}_sibling_start", _scatter_sibling_copies, g4s, lands, 4 * len(g4s))
        self.token = self.ex.token

    def start_chips(self, after):
        g4s, from_sibling = _exchange_wait(f"rs_{self.tag}_sibling_wait", self.ex, after)
        parts = [_add_sibling(g4, rv, self.core, f"rs_add_sibling_{k}")
                 for k, g4, rv in zip(self.names, g4s, from_sibling)]
        self.p32s = [p32 for _, p32 in parts]
        p16s = [p16 for p16, _ in parts]
        lands = [lax.empty((3,) + p.shape[1:], BF16) for p in p16s]
        self.ex = _exchange_start(f"rs_{self.tag}_chips_start", _scatter_chips_copies, p16s, lands, 3 * len(p16s))
        self.token = self.ex.token

    def finish(self, after):
        _, from_chips = _exchange_wait(f"rs_{self.tag}_chips_wait", self.ex, after)
        return {k: _add_chips(p32, rv, self.chip, f"rs_add_chips_{k}")
                for k, p32, rv in zip(self.names, self.p32s, from_chips)}


def _rope_tables():
    positions = jnp.arange(SEQ, dtype=F32)
    inv_freq = jnp.power(ROPE_THETA, -jnp.arange(0, ROPE_DIM, 2, dtype=F32) / ROPE_DIM)
    ang = positions[:, None] * inv_freq[None, :]
    cos, sin = jnp.cos(ang), jnp.sin(ang)
    ones = jnp.ones((SEQ, HEAD_DIM - ROPE_DIM), F32)
    zeros8 = jnp.zeros((SEQ, ROPE_HALF), F32)
    zeros = jnp.zeros((SEQ, HEAD_DIM - ROPE_DIM), F32)
    c_head = jnp.concatenate([cos, cos, ones], axis=1)
    s1_head = jnp.concatenate([-sin, zeros8, zeros], axis=1)
    s2_head = jnp.concatenate([zeros8, sin, zeros], axis=1)
    return tuple(jnp.concatenate([t, t], axis=1) for t in (c_head, s1_head, s2_head))


def _tile_lanes(t, width):
    return jnp.concatenate([t] * (width // LANES), axis=1)


def _rope_apply(x, c, s1, s2):
    w = x.shape[1]
    return x * c + pltpu.roll(x, w - ROPE_HALF, 1) * s1 + pltpu.roll(x, ROPE_HALF, 1) * s2


def _rope_apply_t(dy, c, s1, s2):
    w = dy.shape[1]
    return dy * c + pltpu.roll(dy * s1, ROPE_HALF, 1) + pltpu.roll(dy * s2, w - ROPE_HALF, 1)


def _dil_qkv(proj, tables):
    def fn(q, k, v, c, s1, s2):
        c, s1, s2 = (_tile_lanes(t, DIL_WIDTH) for t in (c, s1, s2))
        return _rope_apply(q, c, s1, s2) * QK_SCALE, _rope_apply(k, c, s1, s2), v

    w = DIL_WIDTH
    return _rowwise(fn, "dil_rope", SEQ, 256,
                    [(proj, w, COL_QA // w), (proj, w, COL_KA // w), (proj, w, COL_VA // w)]
                    + [(t, LANES, 0) for t in tables], [],
                    [(w, BF16)] * 3)


def _dil_qk_bwd(dq, dk, tables):
    def fn(dq, dk, c, s1, s2):
        c, s1, s2 = (_tile_lanes(t, DIL_WIDTH) for t in (c, s1, s2))
        return _rope_apply_t(dq, c, s1, s2), _rope_apply_t(dk, c, s1, s2)

    w = DIL_WIDTH
    return _rowwise(fn, "dil_rope_bwd", SEQ, 256, [(dq, w, 0), (dk, w, 0)] + [(t, LANES, 0) for t in tables], [],
                    [(w, BF16)] * 2)


def _residue_major(a, front_pad=0):
    cols = []
    for g, d in enumerate(DILATIONS):
        part = a[:, g * 256:(g + 1) * 256]
        cols.append(part.reshape(SEQ // d, d, 256).transpose(1, 0, 2).reshape(SEQ, 256))
    out = jnp.concatenate(cols, axis=1)
    return jnp.pad(out, ((front_pad, 0), (0, 0))) if front_pad else out


def _token_major(a):
    cols = []
    for g, d in enumerate(DILATIONS):
        part = a[:, g * 256:(g + 1) * 256]
        cols.append(part.reshape(d, SEQ // d, 256).transpose(1, 0, 2).reshape(SEQ, 256))
    return jnp.concatenate(cols, axis=1)


def _dil_prev_limit(pair, n):
    g = pair // 2
    mask = jnp.where(g == 0, 15, jnp.where(g == 1, 3, 0))
    return jnp.where(jnp.bitwise_and(n, mask) != 0, 0, BLOCK)


def _dil_valid(limit):
    row = lax.broadcasted_iota(jnp.int32, (BLOCK, 2 * BLOCK), 0)
    col = lax.broadcasted_iota(jnp.int32, (BLOCK, 2 * BLOCK), 1)
    dist = col - row
    return jnp.logical_and(dist >= jnp.where(col < BLOCK, limit, -BLOCK), dist <= BLOCK)


def _upper_half():
    return lax.broadcasted_iota(jnp.int32, (1, LANES), 1) >= HEAD_DIM


def _dil_fwd(q, k, v):
    def body(q_ref, k_ref, v_ref, o_ref, lse_ref):
        pair = pl.program_id(0)
        upper = _upper_half()

        def block(n, carry):
            r0 = pl.multiple_of(n * BLOCK, BLOCK)
            qb = q_ref[pl.ds(r0, BLOCK), :]
            kw = k_ref[pl.ds(r0, 2 * BLOCK), :]
            vw = v_ref[pl.ds(r0, 2 * BLOCK), :]
            valid = _dil_valid(_dil_prev_limit(pair, n))
            outs, lses = [], []
            for head_mask in (jnp.logical_not(upper), upper):
                s = jnp.where(valid, _dot_nt(qb, jnp.where(head_mask, kw, 0)), NEG_INF)
                m = jnp.max(s, axis=-1, keepdims=True)
                p = jnp.exp(s - m)
                den = jnp.sum(p, axis=-1, keepdims=True)
                outs.append(_dot_nn((p / den).astype(BF16), jnp.where(head_mask, vw, 0)))
                lses.append(m + jnp.log(den))
            o_ref[pl.ds(r0, BLOCK), :] = outs[0] + outs[1]
            lse_ref[pl.ds(r0, BLOCK), :] = jnp.where(upper, lses[1], lses[0])
            return carry

        lax.fori_loop(0, N_BLOCKS, block, 0)

    return pl.pallas_call(
        body, name="dil_attn_fwd", grid=(DIL_WIDTH // LANES,),
        in_specs=[pl.BlockSpec((SEQ, LANES), lambda p: (0, p)), pl.BlockSpec((SEQ + BLOCK, LANES), lambda p: (0, p)),
                  pl.BlockSpec((SEQ + BLOCK, LANES), lambda p: (0, p))],
        out_specs=[pl.BlockSpec((SEQ, LANES), lambda p: (0, p))] * 2,
        out_shape=[jax.ShapeDtypeStruct((SEQ, DIL_WIDTH), F32)] * 2,
        compiler_params=pltpu.CompilerParams(dimension_semantics=("parallel",)),
    )(q, k, v)


def _dil_bwd(q, k, v, do, lse, c):
    def body(q_ref, k_ref, v_ref, do_ref, lse_ref, c_ref, dq_ref, dk_ref, dv_ref, dk_acc, dv_acc):
        pair = pl.program_id(0)
        upper = _upper_half()
        dk_acc[...] = jnp.zeros_like(dk_acc)
        dv_acc[...] = jnp.zeros_like(dv_acc)

        def block(n, carry):
            r0 = pl.multiple_of(n * BLOCK, BLOCK)
            qb, dob = q_ref[pl.ds(r0, BLOCK), :], do_ref[pl.ds(r0, BLOCK), :]
            kw = k_ref[pl.ds(r0, 2 * BLOCK), :]
            vw = v_ref[pl.ds(r0, 2 * BLOCK), :]
            lse_t, c_t = lse_ref[pl.ds(r0, BLOCK), :], c_ref[pl.ds(r0, BLOCK), :]
            valid = _dil_valid(_dil_prev_limit(pair, n))
            dq = None
            for e, head_mask in enumerate((jnp.logical_not(upper), upper)):
                km, vm = jnp.where(head_mask, kw, 0), jnp.where(head_mask, vw, 0)
                lse_col = lse_t[:, e * HEAD_DIM:e * HEAD_DIM + 1]
                c_col = c_t[:, e * HEAD_DIM:e * HEAD_DIM + 1]
                p = jnp.where(valid, jnp.exp(_dot_nt(qb, km) - lse_col), 0.0)
                ds = (p * (_dot_nt(dob, vm) + c_col)).astype(BF16)
                part = _dot_nn(ds, km)
                dq = part if dq is None else dq + part
                dk_acc[pl.ds(r0, 2 * BLOCK), :] += _dot_tn(ds, jnp.where(head_mask, qb, 0))
                dv_acc[pl.ds(r0, 2 * BLOCK), :] += _dot_tn(p.astype(BF16), jnp.where(head_mask, dob, 0))
            dq_ref[pl.ds(r0, BLOCK), :] = dq * QK_SCALE
            return carry

        lax.fori_loop(0, N_BLOCKS, block, 0)
        dk_ref[...] = dk_acc[BLOCK:, :]
        dv_ref[...] = dv_acc[BLOCK:, :]

    tok = pl.BlockSpec((SEQ, LANES), lambda p: (0, p))
    padded = pl.BlockSpec((SEQ + BLOCK, LANES), lambda p: (0, p))
    shape = jax.ShapeDtypeStruct((SEQ, DIL_WIDTH), F32)
    return pl.pallas_call(
        body, name="dil_attn_bwd", grid=(DIL_WIDTH // LANES,),
        in_specs=[tok, padded, padded, tok, tok, tok], out_specs=[tok, tok, tok], out_shape=[shape] * 3,
        scratch_shapes=[pltpu.VMEM((SEQ + BLOCK, LANES), F32)] * 2,
        compiler_params=pltpu.CompilerParams(dimension_semantics=("parallel",)),
    )(q, k, v, do, lse, c)


def _group_weights(l0, l1, l2):
    m = jnp.maximum(jnp.maximum(l0, l1), l2)
    e0, e1, e2 = jnp.exp(l0 - m), jnp.exp(l1 - m), jnp.exp(l2 - m)
    tot = e0 + e1 + e2
    return e0 / tot, e1 / tot, e2 / tot


def _dil_combine(o, lse, deps=()):
    def fn(o0, o1, o2, l0, l1, l2):
        w0, w1, w2 = _group_weights(l0, l1, l2)
        return w0 * o0 + w1 * o1 + w2 * o2

    w = DIL_OUT_WIDTH
    return _rowwise(fn, "dil_combine", SEQ, 512, [(o, w, g) for g in range(3)] + [(lse, w, g) for g in range(3)], [],
                    [(w, F32)], deps=deps)[0]


def _dil_combine_bwd(d_out, o, lse):
    w = DIL_OUT_WIDTH

    def fn(d, o0, o1, o2, l0, l1, l2):
        row = lax.broadcasted_iota(jnp.int32, (w, w), 0) // HEAD_DIM
        col = lax.broadcasted_iota(jnp.int32, (w, w), 1) // HEAD_DIM
        same_head = jnp.where(row == col, 1.0, 0.0).astype(BF16)
        ws = _group_weights(l0, l1, l2)
        dws = [_dot3_nn(d * og, same_head) for og in (o0, o1, o2)]
        mean = ws[0] * dws[0] + ws[1] * dws[1] + ws[2] * dws[2]
        do = jnp.concatenate([wg * d for wg in ws], axis=1)
        c = jnp.concatenate([-wg * mean for wg in ws], axis=1)
        return do, c

    return _rowwise(fn, "dil_combine_bwd", SEQ, 256,
                    [(d_out, w, 0)] + [(o, w, g) for g in range(3)] + [(lse, w, g) for g in range(3)], [],
                    [(DIL_WIDTH, BF16), (DIL_WIDTH, F32)])


def _fox_qkv(proj, deps=()):
    def fn(q, k, v):
        return q * QK_SCALE, k, v

    w = FOX_WIDTH
    return _rowwise(fn, "fox_cast", SEQ, 256,
                    [(proj, w, COL_QB // w), (proj, w, COL_KB // w), (proj, w, COL_VB // w)], [], [(w, BF16)] * 3,
                    deps=deps)


def _log1p(e):
    u = 1.0 + e
    return jnp.where(u == 1.0, e, jnp.log(u) * (e / (u - 1.0)))


def _fox_gate(proj, b_pad):
    def body(f_ref, b_ref, o_ref):
        z = f_ref[...] + b_ref[...]
        logf = (jnp.minimum(z, 0.0) - _log1p(jnp.exp(-jnp.abs(z)))).T[:F_ROWS]
        row = lax.broadcasted_iota(jnp.int32, (BLOCK, BLOCK), 0)
        col = lax.broadcasted_iota(jnp.int32, (BLOCK, BLOCK), 1)
        before = jnp.where(row <= col, 1.0, 0.0).astype(BF16)
        carry = jnp.zeros((F_ROWS, 1), F32)
        for blk in range(N_BLOCKS):
            run = _dot3_nn(logf[:, blk * BLOCK:(blk + 1) * BLOCK], before) + carry
            o_ref[:, blk * BLOCK:(blk + 1) * BLOCK] = run
            carry = run[:, BLOCK - 1:BLOCK]

    return pl.pallas_call(
        body, name="fox_gate", grid=(1,),
        in_specs=[pl.BlockSpec((SEQ, LANES), lambda i: (0, COL_F // LANES)), pl.BlockSpec((1, LANES), lambda i: (0, 0))],
        out_specs=pl.BlockSpec((F_ROWS, SEQ), lambda i: (0, 0)),
        out_shape=jax.ShapeDtypeStruct((F_ROWS, SEQ), F32),
    )(proj, b_pad)


def _fox_gate_bwd(d_cum, proj, b_pad):
    def body(d_ref, f_ref, b_ref, dz_ref, db_ref):
        row = lax.broadcasted_iota(jnp.int32, (BLOCK, BLOCK), 0)
        col = lax.broadcasted_iota(jnp.int32, (BLOCK, BLOCK), 1)
        after = jnp.where(row >= col, 1.0, 0.0).astype(BF16)
        carry = jnp.zeros((F_ROWS, 1), F32)
        parts = [None] * N_BLOCKS
        for blk in reversed(range(N_BLOCKS)):
            run = _dot3_nn(d_ref[:, blk * BLOCK:(blk + 1) * BLOCK], after) + carry
            parts[blk] = run
            carry = run[:, 0:1]
        dlogf = jnp.concatenate(parts, axis=1)
        dlogf = jnp.concatenate([dlogf, jnp.zeros((LANES - F_ROWS, SEQ), F32)], axis=0).T
        dz = dlogf * _sigmoid(-(f_ref[...] + b_ref[...]))
        dz_ref[...] = dz.astype(BF16)
        db_ref[...] = jnp.sum(dz, axis=0, keepdims=True)

    return pl.pallas_call(
        body, name="fox_gate_bwd", grid=(1,),
        in_specs=[pl.BlockSpec((F_ROWS, SEQ), lambda i: (0, 0)),
                  pl.BlockSpec((SEQ, LANES), lambda i: (0, COL_F // LANES)), pl.BlockSpec((1, LANES), lambda i: (0, 0))],
        out_specs=[pl.BlockSpec((SEQ, LANES), lambda i: (0, 0)), pl.BlockSpec((1, LANES), lambda i: (0, 0))],
        out_shape=[jax.ShapeDtypeStruct((SEQ, LANES), BF16), jax.ShapeDtypeStruct((1, LANES), F32)],
    )(d_cum, proj, b_pad)


FOX_TILE = 256
FOX_TILES = SEQ // FOX_TILE


def _row_to_col(row):
    n = row.shape[1]
    eye = lax.broadcasted_iota(jnp.int32, (n, n), 0) == lax.broadcasted_iota(jnp.int32, (n, n), 1)
    return jnp.sum(jnp.where(eye, row, 0.0), axis=1, keepdims=True)


def _fox_scores(q_tile, km, f_row, i):
    t = FOX_TILE
    ext = (i + 1) * t
    f_q = _row_to_col(f_row[:, i * t:(i + 1) * t])
    s = _dot_nt(q_tile, km[:ext]) + (f_q - f_row[:, :ext])
    row = lax.broadcasted_iota(jnp.int32, (t, ext), 0) + i * t
    col = lax.broadcasted_iota(jnp.int32, (t, ext), 1)
    return s, col <= row


def _fox_fwd(q, k, v, f_rows):
    t = FOX_TILE

    def body(q_ref, k_ref, v_ref, f_ref, o_ref, lse_ref):
        pair = pl.program_id(0)
        upper = _upper_half()
        masks = (jnp.logical_not(upper), upper)
        kms = [jnp.where(hm, k_ref[...], 0) for hm in masks]
        vms = [jnp.where(hm, v_ref[...], 0) for hm in masks]
        f_row = [f_ref[pl.ds(2 * pair + e, 1), :] for e in range(2)]
        for i in range(FOX_TILES):
            q_tile = q_ref[i * t:(i + 1) * t, :]
            outs, lses = [], []
            for e in range(2):
                s, causal = _fox_scores(q_tile, kms[e], f_row[e], i)
                s = jnp.where(causal, s, NEG_INF)
                m = jnp.max(s, axis=-1, keepdims=True)
                p = jnp.exp(s - m)
                den = jnp.sum(p, axis=-1, keepdims=True)
                outs.append(_dot_nn((p / den).astype(BF16), vms[e][:(i + 1) * t]))
                lses.append(m + jnp.log(den))
            o_ref[i * t:(i + 1) * t, :] = outs[0] + outs[1]
            lse_ref[i * t:(i + 1) * t, :] = jnp.where(upper, lses[1], lses[0])

    tok = pl.BlockSpec((SEQ, LANES), lambda p: (0, p))
    return pl.pallas_call(
        body, name="fox_attn_fwd", grid=(FOX_WIDTH // LANES,),
        in_specs=[tok, tok, tok, pl.BlockSpec((F_ROWS, SEQ), lambda p: (0, 0))], out_specs=[tok, tok],
        out_shape=[jax.ShapeDtypeStruct((SEQ, FOX_WIDTH), F32)] * 2,
        compiler_params=pltpu.CompilerParams(
            dimension_semantics=("parallel",), vmem_limit_bytes=_vmem_limit(8 * t * SEQ * 4)),
    )(q, k, v, f_rows)


def _fox_bwd(q, k, v, do, lse, f_rows):
    t = FOX_TILE

    def body(q_ref, k_ref, v_ref, do_ref, lse_ref, f_ref, dq_ref, dk_ref, dv_ref, df_ref):
        pair = pl.program_id(0)
        upper = _upper_half()
        masks = (jnp.logical_not(upper), upper)
        kms = [jnp.where(hm, k_ref[...], 0) for hm in masks]
        vms = [jnp.where(hm, v_ref[...], 0) for hm in masks]
        f_row = [f_ref[pl.ds(2 * pair + e, 1), :] for e in range(2)]
        dk_ref[...] = jnp.zeros_like(dk_ref)
        dv_ref[...] = jnp.zeros_like(dv_ref)
        df_ref[...] = jnp.zeros_like(df_ref)
        for i in range(FOX_TILES):
            ext = (i + 1) * t
            q_tile, do_tile = q_ref[i * t:(i + 1) * t, :], do_ref[i * t:(i + 1) * t, :]
            lse_t = lse_ref[i * t:(i + 1) * t, :]
            dq = None
            for e in range(2):
                s, causal = _fox_scores(q_tile, kms[e], f_row[e], i)
                p = jnp.where(causal, jnp.exp(s - lse_t[:, e * HEAD_DIM:e * HEAD_DIM + 1]), 0.0)
                dp = _dot_nt(do_tile, vms[e][:ext])
                ds = p * (dp - jnp.sum(p * dp, axis=-1, keepdims=True))
                df_ref[0, e:e + 1, :ext] -= jnp.sum(ds, axis=0, keepdims=True)
                ds = ds.astype(BF16)
                part = _dot_nn(ds, kms[e][:ext])
                dq = part if dq is None else dq + part
                dk_ref[:ext, :] += _dot_tn(ds, jnp.where(masks[e], q_tile, 0))
                dv_ref[:ext, :] += _dot_tn(p.astype(BF16), jnp.where(masks[e], do_tile, 0))
            dq_ref[i * t:(i + 1) * t, :] = dq * QK_SCALE

    tok = pl.BlockSpec((SEQ, LANES), lambda p: (0, p))
    shape = jax.ShapeDtypeStruct((SEQ, FOX_WIDTH), F32)
    return pl.pallas_call(
        body, name="fox_attn_bwd", grid=(FOX_WIDTH // LANES,),
        in_specs=[tok] * 5 + [pl.BlockSpec((F_ROWS, SEQ), lambda p: (0, 0))],
        out_specs=[tok, tok, tok, pl.BlockSpec((1, SUBLANES, SEQ), lambda p: (p, 0, 0))],
        out_shape=[shape, shape, shape, jax.ShapeDtypeStruct((FOX_WIDTH // LANES, SUBLANES, SEQ), F32)],
        compiler_params=pltpu.CompilerParams(
            dimension_semantics=("parallel",), vmem_limit_bytes=_vmem_limit(10 * t * SEQ * 4)),
    )(q, k, v, do, lse, f_rows)


MIX_TILE = 256


def _mix_out(out_a, out_b, proj, x, wt_pa, wt_pb, w_out, g_post, g_ffn_pre):
    tm = MIX_TILE

    def body(a_ref, b_ref, ga_ref, gb_ref, x_ref, wpa_ref, wpb_ref, wo_ref, g2_ref, g3_ref,
             merged_ref, mix_ref, x1_ref, h2_ref):
        ya = _dot_nt(a_ref[...].astype(BF16), wpa_ref[...])
        yb = _dot_nt(b_ref[...].astype(BF16), wpb_ref[...])
        merged = (_sigmoid(ga_ref[...]) * ya + _sigmoid(gb_ref[...]) * yb).astype(BF16)
        merged_ref[...] = merged
        mix = _dot_nn(merged, wo_ref[...])
        mix_ref[...] = mix
        x1 = x_ref[...] + mix * _rms_scale(mix) * g2_ref[...]
        x1_ref[...] = x1
        h2_ref[...] = (x1 * _rms_scale(x1) * g3_ref[...]).astype(BF16)

    def rows(w, cb=0):
        return pl.BlockSpec((tm, w), lambda i, cb=cb: (i, cb))

    def whole(a):
        return pl.BlockSpec(a.shape, lambda i: (0, 0))

    d = D_MODEL
    blk = _nbytes((tm, d), F32) * 6 + sum(_nbytes(a.shape, BF16) for a in (wt_pa, wt_pb, w_out))
    return pl.pallas_call(
        body, name="mix_out", grid=(SEQ // tm,),
        in_specs=[rows(DIL_OUT_WIDTH), rows(FOX_WIDTH), rows(d, COL_GA // d), rows(d, COL_GB // d), rows(d),
                  whole(wt_pa), whole(wt_pb), whole(w_out), whole(g_post), whole(g_ffn_pre)],
        out_specs=[rows(d)] * 4,
        out_shape=[jax.ShapeDtypeStruct((SEQ, d), dt) for dt in (BF16, F32, F32, BF16)],
        compiler_params=pltpu.CompilerParams(dimension_semantics=("parallel",), vmem_limit_bytes=_vmem_limit(blk)),
    )(out_a, out_b, proj, proj, x, wt_pa, wt_pb, w_out, g_post, g_ffn_pre)


def _mix_out_bwd(dmix, out_a, out_b, proj, wt_pa, wt_pb, w_out, deps=()):
    tm = MIX_TILE

    def body(dm_ref, a_ref, b_ref, ga_ref, gb_ref, wpa_ref, wpb_ref, wo_ref, *rest):
        dga_ref, dgb_ref, dya_ref, dyb_ref, da_ref, db_ref = rest[len(deps):]
        dmerged = _dot_nt(dm_ref[...], wo_ref[...])
        ya = _dot_nt(a_ref[...].astype(BF16), wpa_ref[...])
        yb = _dot_nt(b_ref[...].astype(BF16), wpb_ref[...])
        sa, sb = _sigmoid(ga_ref[...]), _sigmoid(gb_ref[...])
        dga_ref[...] = (dmerged * ya * (sa * (1.0 - sa))).astype(BF16)
        dgb_ref[...] = (dmerged * yb * (sb * (1.0 - sb))).astype(BF16)
        dya = (dmerged * sa).astype(BF16)
        dyb = (dmerged * sb).astype(BF16)
        dya_ref[...] = dya
        dyb_ref[...] = dyb
        da_ref[...] = _dot_nn(dya, wpa_ref[...])
        db_ref[...] = _dot_nn(dyb, wpb_ref[...]).astype(BF16)

    def rows(w, cb=0):
        return pl.BlockSpec((tm, w), lambda i, cb=cb: (i, cb))

    def whole(a):
        return pl.BlockSpec(a.shape, lambda i: (0, 0))

    d = D_MODEL
    blk = _nbytes((tm, d), F32) * 8 + sum(_nbytes(a.shape, BF16) for a in (wt_pa, wt_pb, w_out))
    return pl.pallas_call(
        body, name="mix_out_bwd", grid=(SEQ // tm,),
        in_specs=[rows(d), rows(DIL_OUT_WIDTH), rows(FOX_WIDTH), rows(d, COL_GA // d), rows(d, COL_GB // d),
                  whole(wt_pa), whole(wt_pb), whole(w_out)] + [_ANY] * len(deps),
        out_specs=[rows(d)] * 4 + [rows(DIL_OUT_WIDTH), rows(FOX_WIDTH)],
        out_shape=[jax.ShapeDtypeStruct((SEQ, d), BF16)] * 4
        + [jax.ShapeDtypeStruct((SEQ, DIL_OUT_WIDTH), F32), jax.ShapeDtypeStruct((SEQ, FOX_WIDTH), BF16)],
        compiler_params=pltpu.CompilerParams(dimension_semantics=("parallel",), vmem_limit_bytes=_vmem_limit(blk)),
    )(dmix, out_a, out_b, proj, proj, wt_pa, wt_pb, w_out, *deps)


FFN_TM, FFN_TN = 1024, 256


def _ffn_up(h2, wt_gate, wt_up):
    tm, tn = FFN_TM, FFN_TN

    def body(h_ref, wg_ref, wu_ref, gate_ref, up_ref, act_ref):
        gate = _dot_nt(h_ref[...], wg_ref[...])
        up = _dot_nt(h_ref[...], wu_ref[...])
        gate_ref[...] = gate
        up_ref[...] = up
        act_ref[...] = (gate * _sigmoid(gate) * up).astype(BF16)

    tile = pl.BlockSpec((tm, tn), lambda i, j: (i, j))
    w_spec = pl.BlockSpec((tn, D_MODEL), lambda i, j: (j, 0))
    return pl.pallas_call(
        body, name="ffn_up", grid=(SEQ // tm, D_FF // tn),
        in_specs=[pl.BlockSpec((tm, D_MODEL), lambda i, j: (i, 0)), w_spec, w_spec],
        out_specs=[tile, tile, tile],
        out_shape=[jax.ShapeDtypeStruct((SEQ, D_FF), dt) for dt in (F32, F32, BF16)],
        compiler_params=pltpu.CompilerParams(
            dimension_semantics=("parallel", "parallel"), vmem_limit_bytes=_vmem_limit(8 * 2**20)),
    )(h2, wt_gate, wt_up)


def _ffn_act_bwd(dff, w_down, gate, up):
    tm, tn = FFN_TM, FFN_TN

    def body(d_ref, wd_ref, gate_ref, up_ref, dgate_ref, dup_ref):
        dact = _dot_nt(d_ref[...], wd_ref[...])
        gate = gate_ref[...]
        sg = _sigmoid(gate)
        dgate_ref[...] = (dact * up_ref[...] * (sg * (1.0 + gate * (1.0 - sg)))).astype(BF16)
        dup_ref[...] = (dact * (gate * sg)).astype(BF16)

    tile = pl.BlockSpec((tm, tn), lambda i, j: (i, j))
    return pl.pallas_call(
        body, name="ffn_act_bwd", grid=(SEQ // tm, D_FF // tn),
        in_specs=[pl.BlockSpec((tm, D_MODEL), lambda i, j: (i, 0)), pl.BlockSpec((tn, D_MODEL), lambda i, j: (j, 0)),
                  tile, tile],
        out_specs=[tile, tile],
        out_shape=[jax.ShapeDtypeStruct((SEQ, D_FF), BF16)] * 2,
        compiler_params=pltpu.CompilerParams(
            dimension_semantics=("parallel", "parallel"), vmem_limit_bytes=_vmem_limit(8 * 2**20)),
    )(dff, w_down, gate, up)


def _loss_head(ff, x1, target, g_post):
    def fn(ff, x1, tgt, g):
        r = _rms_scale(ff)
        nrm = ff * r
        err = (x1 + nrm * g) - tgt
        loss = 0.5 * jnp.sum(jnp.mean(err * err, axis=-1, keepdims=True), axis=0, keepdims=True)
        dy = err * (1.0 / D_MODEL)
        u = dy * g
        dff = r * u - ff * (r * r * r) * jnp.mean(u * ff, axis=-1, keepdims=True)
        return dy, dff, jnp.broadcast_to(loss, (1, LANES)), jnp.sum(dy * nrm, axis=0, keepdims=True)

    d = D_MODEL
    return _rowwise(fn, "loss_head", SEQ, 256, [(ff, d, 0), (x1, d, 0), (target, d, 0)], [g_post],
                    [(d, F32), (d, BF16)], [LANES, d])


def _post_ffn_bwd(dh2, x1, dy, mix, g_ffn_pre, g_mix_post):
    def fn(dh2, x1, dy, mix, g3, g2):
        dx, dg3 = _rms_bwd(x1, dh2, g3)
        dx1 = dy + dx
        dmix, dg2 = _rms_bwd(mix, dx1, g2)
        return dx1, dmix, dg3, dg2

    d = D_MODEL
    return _rowwise(fn, "post_ffn_bwd", SEQ, 256, [(dh2, d, 0), (x1, d, 0), (dy, d, 0), (mix, d, 0)],
                    [g_ffn_pre, g_mix_post], [(d, F32), (d, BF16)], [d, d])


def _input_bwd(dh, x, dx1, g_pre, deps=()):
    def fn(dh, x, dx1, g):
        dx, dg = _rms_bwd(x, dh, g)
        return dx1 + dx, dg

    d = D_MODEL
    return _rowwise(fn, "input_bwd", SEQ, 256, [(dh, d, 0), (x, d, 0), (dx1, d, 0)], [g_pre], [(d, F32)], [d],
                    deps=deps)


def _adam_math(w, g, m, v):
    m = ADAM_B1 * m + (1.0 - ADAM_B1) * g
    v = ADAM_B2 * v + (1.0 - ADAM_B2) * (g * g)
    m_hat = m / (1.0 - ADAM_B1 ** ADAM_STEP)
    v_hat = v / (1.0 - ADAM_B2 ** ADAM_STEP)
    delta = -ADAM_LR * (m_hat / (jnp.sqrt(v_hat) + ADAM_EPS) + ADAM_WD * w)
    return delta, m, v


def _adam(w, g, m, v, name):
    r, c = w.shape
    tc = _col_tile(r, c)

    def body(w_ref, g_ref, m_ref, v_ref, d_ref, nm_ref, nv_ref):
        d_ref[...], nm_ref[...], nv_ref[...] = _adam_math(w_ref[...], g_ref[...], m_ref[...], v_ref[...])

    spec = pl.BlockSpec((r, tc), lambda j: (0, j))
    return pl.pallas_call(
        body, name=name, grid=(c // tc,), in_specs=[spec] * 4, out_specs=[spec] * 3,
        out_shape=[jax.ShapeDtypeStruct((r, c), F32)] * 3,
        compiler_params=pltpu.CompilerParams(dimension_semantics=("parallel",)),
    )(w, g, m, v)


def _adam_small(gathered, ws, ms, vs):
    n = len(ws)

    def body(*refs):
        outs = refs[4 * n:]
        for i in range(n):
            ga_ref, w_ref, m_ref, v_ref = (refs[j * n + i] for j in range(4))
            g = ga_ref[0]
            for dev in range(1, N_DEV):
                g = g + ga_ref[dev]
            g = g[:, :w_ref.shape[1]]
            outs[4 * i][...] = g
            outs[4 * i + 1][...], outs[4 * i + 2][...], outs[4 * i + 3][...] = _adam_math(
                w_ref[...], g, m_ref[...], v_ref[...])

    out_shape = [jax.ShapeDtypeStruct(w.shape, F32) for w in ws for _ in range(4)]
    out = pl.pallas_call(body, name="adam_small", out_shape=out_shape)(*gathered, *ws, *ms, *vs)
    return [out[4 * i:4 * i + 4] for i in range(n)]


def _proj_weight_t(wt_in_full):
    w = wt_in_full
    z = lambda n: jnp.zeros((n, D_MODEL), w.dtype)
    return jnp.concatenate([w[3848:5896], z(256), w[0:3840], w[3840:3848], z(PROJ_COLS - COL_F - 8)], axis=0)


def _proj_weight_grad_t(dwt_r):
    return jnp.concatenate([dwt_r[COL_QA:COL_F], dwt_r[COL_F:COL_F + 8], dwt_r[0:2048]], axis=0)


def kernel(x, w_in, w_proj_a, w_proj_b, w_out, b_forget, w_ffn_gate, w_ffn_up, w_ffn_down, norm_mix_pre, norm_mix_post, norm_ffn_pre, norm_ffn_post, loss_target, m_w_in, m_w_proj_a, m_w_proj_b, m_w_out, m_b_forget, m_w_ffn_gate, m_w_ffn_up, m_w_ffn_down, m_norm_mix_pre, m_norm_mix_post, m_norm_ffn_pre, m_norm_ffn_post, v_w_in, v_w_proj_a, v_w_proj_b, v_w_out, v_b_forget, v_w_ffn_gate, v_w_ffn_up, v_w_ffn_down, v_norm_mix_pre, v_norm_mix_post, v_norm_ffn_pre, v_norm_ffn_post):
    d = D_MODEL
    names = ("w_in", "w_proj_a", "w_proj_b", "w_out", "w_ffn_gate", "w_ffn_up", "w_ffn_down")
    col_sharded = ("w_in", "w_proj_a", "w_proj_b", "w_ffn_gate", "w_ffn_up")

    def row_shards(arrs):
        return {k: (a[0].T if k in col_sharded else a[0]) for k, a in zip(names, arrs)}

    shards = row_shards((w_in, w_proj_a, w_proj_b, w_out, w_ffn_gate, w_ffn_up, w_ffn_down))
    moments_m = row_shards((m_w_in, m_w_proj_a, m_w_proj_b, m_w_out, m_w_ffn_gate, m_w_ffn_up, m_w_ffn_down))
    moments_v = row_shards((v_w_in, v_w_proj_a, v_w_proj_b, v_w_out, v_w_ffn_gate, v_w_ffn_up, v_w_ffn_down))
    core = lax.axis_index("c").astype(jnp.int32).reshape(1)
    chip = (2 * lax.axis_index("x") + lax.axis_index("y")).astype(jnp.int32).reshape(1)
    x2, target = x[0], loss_target[0]

    me = 4 * lax.axis_index("x") + 2 * lax.axis_index("y") + lax.axis_index("c")
    mix_names, ffn_names = names[:4], names[4:]
    first_names, later_names = names[:1], names[1:]
    shards16 = {k: shards[k].astype(BF16) for k in names}

    def landing(k):
        return lax.dynamic_update_slice(lax.empty((N_DEV,) + shards[k].shape, BF16), shards16[k][None], (me, 0, 0))

    ag_first = _exchange_start("ag_first_chips_start", _gather_chips_copies, [shards16[k] for k in first_names],
                               [landing(k) for k in first_names], 3 * len(first_names))
    h = _rowwise(lambda xb, g: xb * _rms_scale(xb) * g, "norm_mix_pre", SEQ, 256, [(x2, d, 0)], [norm_mix_pre],
                 [(d, BF16)], deps=[ag_first.token])[0]
    _, lands = _exchange_wait("ag_first_chips_wait", ag_first, [h])
    ag_first = _exchange_start("ag_first_sibling_start", _gather_sibling_copies, [], lands, 4 * len(first_names))
    ag_later = _exchange_start("ag_later_chips_start", _gather_chips_copies, [shards16[k] for k in later_names],
                               [landing(k) for k in later_names], 3 * len(later_names), after=[ag_first.token])
    gathered = dict(zip(first_names, _exchange_wait("ag_first_sibling_wait", ag_first, [ag_later.token])[1]))
    wt_r = _proj_weight_t(gathered["w_in"].reshape(IN_COLS, d))

    proj = _matmul([(h, wt_r)], "nt", F32, "in_proj", 1024, 896, 1024)
    tables = _rope_tables()
    qa_tok, ka_tok, va_tok = _dil_qkv(proj, tables)
    qa = _residue_major(qa_tok)
    ka, va = _residue_major(ka_tok, BLOCK), _residue_major(va_tok, BLOCK)
    o_dil, lse_dil = _dil_fwd(qa, ka, va)
    o_tok, lse_tok = _token_major(o_dil), _token_major(lse_dil)
    out_a = _dil_combine(o_tok, lse_tok)
    _, lands = _exchange_wait("ag_later_chips_wait", ag_later, [out_a])
    ag_later = _exchange_start("ag_later_sibling_start", _gather_sibling_copies, [], lands, 4 * len(later_names))

    b_pad = jnp.pad(b_forget, ((0, 0), (0, LANES - N_FOX_HEADS)))
    f_rows = _fox_gate(proj, b_pad)
    qb, kb, vb = _fox_qkv(proj, deps=[ag_later.token])
    out_b, lse_fox = _fox_fwd(qb, kb, vb, f_rows)

    gathered = dict(zip(later_names, _exchange_wait("ag_later_sibling_wait", ag_later, [out_b])[1]))
    wt_pa = gathered["w_proj_a"].reshape(d, DIL_OUT_WIDTH)
    wt_pb = gathered["w_proj_b"].reshape(d, FOX_WIDTH)
    w_o = gathered["w_out"].reshape(d, d)
    wt_g = gathered["w_ffn_gate"].reshape(D_FF, d)
    wt_u = gathered["w_ffn_up"].reshape(D_FF, d)
    w_d = gathered["w_ffn_down"].reshape(D_FF, d)
    merged, mix, x1, h2 = _mix_out(out_a, out_b, proj, x2, wt_pa, wt_pb, w_o, norm_mix_post, norm_ffn_pre)

    gate, up, act = _ffn_up(h2, wt_g, wt_u)
    ff = _matmul([(act, w_d)], "nn", F32, "ffn_down", 1024, 1024, 1408)
    dy, dff, loss_part, dg_ffn_post = _loss_head(ff, x1, target, norm_ffn_post)
    loss = lax.psum(loss_part[0, 0], ("x", "y", "c"))

    dgate, dup = _ffn_act_bwd(dff, w_d, gate, up)
    grads_t = {}
    grads_t["w_ffn_down"] = _matmul([(act, dff)], "tn", F32, "grad_w_ffn_down", 1408, 1024, 1024)
    grads_t["w_ffn_gate"] = _matmul([(dgate, h2)], "tn", F32, "grad_w_ffn_gate", 1408, 1024, 1024)
    grads_t["w_ffn_up"] = _matmul([(dup, h2)], "tn", F32, "grad_w_ffn_up", 1408, 1024, 1024)
    rs_ffn = _ReduceScatter("ffn", {k: grads_t[k] for k in ffn_names}, core, chip)
    dh2 = _matmul([(dgate, wt_g), (dup, wt_u)], "nn", F32, "ffn_up_bwd", 1024, 1024, 1408, deps=[rs_ffn.token])
    dx1, dmix, dg_ffn_pre, dg_mix_post = _post_ffn_bwd(dh2, x1, dy, mix, norm_ffn_pre, norm_mix_post)
    rs_ffn.start_chips([dmix])

    dga, dgb, dya, dyb, d_out_a, d_out_b = _mix_out_bwd(dmix, out_a, out_b, proj, wt_pa, wt_pb, w_o,
                                                        deps=[rs_ffn.token])
    grads_t["w_out"] = _matmul([(merged, dmix)], "tn", F32, "grad_w_out", 1024, 1024, 1024)
    grads_t["w_proj_a"] = _matmul([(dya, out_a)], "tn", F32, "grad_w_proj_a", 1024, DIL_OUT_WIDTH, SEQ)
    grads_t["w_proj_b"] = _matmul([(dyb, out_b)], "tn", F32, "grad_w_proj_b", 1024, FOX_WIDTH, SEQ)

    dq_fox, dk_fox, dv_fox, d_cum = _fox_bwd(qb, kb, vb, d_out_b, lse_fox, f_rows)
    d_cum_rows = jnp.pad(d_cum[:, :2].reshape(N_FOX_HEADS, SEQ), ((0, F_ROWS - N_FOX_HEADS), (0, 0)))
    dz, db_part = _fox_gate_bwd(d_cum_rows, proj, b_pad)

    do_tok, c_tok = _dil_combine_bwd(d_out_a, o_tok, lse_tok)
    dq_dil, dk_dil, dv_dil = _dil_bwd(qa, ka, va, _residue_major(do_tok), lse_dil, _residue_major(c_tok))
    dqa, dka = _dil_qk_bwd(_token_major(dq_dil), _token_major(dk_dil), tables)

    dproj = jnp.concatenate(
        [dga, dgb, jnp.zeros((SEQ, COL_QA - 2 * d), BF16), dqa, dka, _token_major(dv_dil).astype(BF16),
         dq_fox.astype(BF16), dk_fox.astype(BF16), dv_fox.astype(BF16), dz], axis=1)
    dwt_r = _matmul([(dproj, h)], "tn", F32, "grad_w_in", 896, 1024, 1024)
    grads_t["w_in"] = _proj_weight_grad_t(dwt_r)
    rs_mix = _ReduceScatter("mix", {k: grads_t[k] for k in mix_names}, core, chip)
    dh = _matmul([(dproj, wt_r)], "nn", F32, "in_proj_bwd", 1024, 1024, 896, deps=[rs_mix.token])
    grad_x, dg_mix_pre = _input_bwd(dh, x2, dx1, norm_mix_pre)

    small_all = _all_gather([dg_mix_pre, dg_mix_post, dg_ffn_pre, dg_ffn_post, db_part], "small_grads_all_gather")
    rs_mix.start_chips(small_all)
    grads = rs_ffn.finish([rs_mix.token])
    big = {k: _adam(shards[k], grads[k], moments_m[k], moments_v[k], "adam_" + k) for k in ffn_names}
    small = _adam_small(small_all, [norm_mix_pre, norm_mix_post, norm_ffn_pre, norm_ffn_post, b_forget],
                        [m_norm_mix_pre, m_norm_mix_post, m_norm_ffn_pre, m_norm_ffn_post, m_b_forget],
                        [v_norm_mix_pre, v_norm_mix_post, v_norm_ffn_pre, v_norm_ffn_post, v_b_forget])

    grads.update(rs_mix.finish([small[0][0]] + [big[k][0] for k in ffn_names]))
    big.update({k: _adam(shards[k], grads[k], moments_m[k], moments_v[k], "adam_" + k) for k in mix_names})

    def leaves(i):
        def nat(k):
            a = grads[k] if i == 0 else big[k][i - 1]
            return (a.T if k in col_sharded else a)[None]

        return [nat("w_in"), nat("w_proj_a"), nat("w_proj_b"), nat("w_out"), small[4][i],
                nat("w_ffn_gate"), nat("w_ffn_up"), nat("w_ffn_down"), *[small[r][i] for r in range(4)]]

    return (loss, grad_x[None], *leaves(0), *leaves(1), *leaves(2), *leaves(3))
```

```python
import functools
import math

import jax
import jax.numpy as jnp
from jax import lax
from jax.experimental import pallas as pl
from jax.experimental.pallas import tpu as pltpu

F32 = jnp.float32
BF16 = jnp.bfloat16
MESH = pl.DeviceIdType.MESH

D_MODEL = 1024
SEQ = 2048
HEAD_DIM = 64
BLOCK = 128
N_BLOCKS = SEQ // BLOCK
DILATIONS = (1, 4, 16)
N_FOX_HEADS = 8
DIL_WIDTH = 768
DIL_OUT_WIDTH = 256
FOX_WIDTH = 512
D_FF = 2816
ROPE_THETA = 500000.0
ROPE_DIM = HEAD_DIM // 4
ROPE_HALF = ROPE_DIM // 2
EPS = 1e-6
NEG_INF = -1e30
QK_SCALE = 1.0 / math.sqrt(HEAD_DIM)
IN_COLS = 5896
N_DEV = 8
IN_SHARD = IN_COLS // N_DEV

ADAM_LR = 0.001
ADAM_B1 = 0.9
ADAM_B2 = 0.999
ADAM_EPS = 1e-08
ADAM_WD = 0.01
ADAM_STEP = 10

V7X_VMEM_BYTES = 64 * 2**20
LANES = 128
SUBLANES = 8

PROJ_COLS = 6272
COL_GA, COL_GB = 0, 1024
COL_QA, COL_KA, COL_VA = 2304, 3072, 3840
COL_QB, COL_KB, COL_VB = 4608, 5120, 5632
COL_F = 6144
F_ROWS = 16


def _vmem_limit(block_bytes):
    want = 2 * block_bytes + 16 * 2**20
    return int(min(max(want, 32 * 2**20), V7X_VMEM_BYTES - 8 * 2**20))


def _nbytes(shape, dtype):
    return math.prod(shape) * jnp.dtype(dtype).itemsize


def _dot(a, b, dims):
    return lax.dot_general(a, b, (dims, ((), ())), preferred_element_type=F32)


def _dot_nn(a, b):
    return _dot(a, b, ((1,), (0,)))


def _dot_nt(a, b):
    return _dot(a, b, ((1,), (1,)))


def _dot_tn(a, b):
    return _dot(a, b, ((0,), (0,)))


def _sigmoid(z):
    return 1.0 / (1.0 + jnp.exp(-z))


def _split3(x):
    hi = x.astype(BF16)
    r1 = x - hi.astype(F32)
    mid = r1.astype(BF16)
    lo = (r1 - mid.astype(F32)).astype(BF16)
    return hi, mid, lo


def _dot3_nn(x, ones_matrix):
    hi, mid, lo = _split3(x)
    return (_dot_nn(hi, ones_matrix) + _dot_nn(mid, ones_matrix)) + _dot_nn(lo, ones_matrix)


def _rowwise(fn, name, n_rows, tm, row_ins, bcast_ins, row_outs, acc_outs=(), deps=()):
    n_in = len(row_ins) + len(bcast_ins)
    n_ro = len(row_outs)

    def body(*refs):
        res = fn(*[r[...] for r in refs[:n_in]])
        if not isinstance(res, (tuple, list)):
            res = (res,)
        outs = refs[n_in + len(deps):]
        for r, o in zip(res[:n_ro], outs[:n_ro]):
            o[...] = r.astype(o.dtype)
        first = pl.program_id(0) == 0
        for r, o in zip(res[n_ro:], outs[n_ro:]):
            _accumulate(o, r, first)

    in_specs = [pl.BlockSpec((tm, w), lambda i, cb=cb: (i, cb)) for _, w, cb in row_ins]
    in_specs += [pl.BlockSpec(a.shape, lambda i: (0, 0)) for a in bcast_ins]
    in_specs += [pl.BlockSpec(memory_space=pl.ANY)] * len(deps)
    out_specs = [pl.BlockSpec((tm, w), lambda i: (i, 0)) for w, _ in row_outs]
    out_specs += [pl.BlockSpec((1, w), lambda i: (0, 0)) for w in acc_outs]
    out_shape = [jax.ShapeDtypeStruct((n_rows, w), dt) for w, dt in row_outs]
    out_shape += [jax.ShapeDtypeStruct((1, w), F32) for w in acc_outs]
    blk = sum(_nbytes((tm, w), a.dtype) for a, w, _ in row_ins) + sum(_nbytes((tm, w), dt) for w, dt in row_outs)
    return pl.pallas_call(
        body, name=name, grid=(n_rows // tm,), in_specs=in_specs, out_specs=out_specs, out_shape=out_shape,
        compiler_params=pltpu.CompilerParams(
            dimension_semantics=("arbitrary" if acc_outs else "parallel",), vmem_limit_bytes=_vmem_limit(3 * blk)),
    )(*[a for a, _, _ in row_ins], *bcast_ins, *deps)


def _accumulate(o_ref, part, first):
    @pl.when(first)
    def _():
        o_ref[...] = part

    @pl.when(jnp.logical_not(first))
    def _():
        o_ref[...] += part


_MM_DIMS = {"nn": ((1,), (0,)), "nt": ((1,), (1,)), "tn": ((0,), (0,))}


def _matmul(pairs, mode, out_dtype, name, tm, tn, tk, deps=()):
    a0, b0 = pairs[0]
    if mode == "tn":
        kk, m = a0.shape
    else:
        m, kk = a0.shape
    n = b0.shape[0] if mode == "nt" else b0.shape[1]
    assert m % tm == 0 and n % tn == 0 and kk % tk == 0, (name, m, n, kk)
    nk = kk // tk
    n_pairs = len(pairs)
    dims = _MM_DIMS[mode]
    n_in = 2 * n_pairs + len(deps)

    def body(*refs):
        o_ref = refs[n_in]
        part = None
        for p in range(n_pairs):
            d = _dot(refs[2 * p][...].astype(BF16), refs[2 * p + 1][...].astype(BF16), dims)
            part = d if part is None else part + d
        if nk == 1:
            o_ref[...] = part.astype(o_ref.dtype)
            return
        acc = refs[n_in + 1]
        k = pl.program_id(2)

        @pl.when(k == 0)
        def _():
            acc[...] = part

        @pl.when(k > 0)
        def _():
            acc[...] += part

        @pl.when(k == nk - 1)
        def _():
            o_ref[...] = acc[...].astype(o_ref.dtype)

    if mode == "tn":
        a_spec = pl.BlockSpec((tk, tm), lambda i, j, k: (k, i))
    else:
        a_spec = pl.BlockSpec((tm, tk), lambda i, j, k: (i, k))
    if mode == "nt":
        b_spec = pl.BlockSpec((tn, tk), lambda i, j, k: (j, k))
    else:
        b_spec = pl.BlockSpec((tk, tn), lambda i, j, k: (k, j))
    blk = sum(_nbytes((tm, tk), a.dtype) + _nbytes((tk, tn), b.dtype) for a, b in pairs) + 2 * _nbytes((tm, tn), F32)
    flat = [a for pair in pairs for a in pair]
    return pl.pallas_call(
        body, name=name, grid=(m // tm, n // tn, nk),
        in_specs=[a_spec, b_spec] * n_pairs + [pl.BlockSpec(memory_space=pl.ANY)] * len(deps),
        out_specs=pl.BlockSpec((tm, tn), lambda i, j, k: (i, j)),
        out_shape=jax.ShapeDtypeStruct((m, n), out_dtype),
        scratch_shapes=[] if nk == 1 else [pltpu.VMEM((tm, tn), F32)],
        compiler_params=pltpu.CompilerParams(
            dimension_semantics=("parallel", "parallel", "arbitrary"), vmem_limit_bytes=_vmem_limit(blk)),
    )(*flat, *deps)


def _rms_scale(x):
    return lax.rsqrt(jnp.mean(x * x, axis=-1, keepdims=True) + EPS)


def _rms_bwd(xin, dyn, g):
    r = _rms_scale(xin)
    u = dyn * g
    dx = r * u - xin * (r * r * r) * jnp.mean(u * xin, axis=-1, keepdims=True)
    dg = jnp.sum(dyn * xin * r, axis=0, keepdims=True)
    return dx, dg


def _mesh_pos():
    return lax.axis_index("x"), lax.axis_index("y"), lax.axis_index("c")


def _all_gather(xs, name):
    n = len(xs)

    def body(*refs):
        x_refs, out_refs = refs[:n], refs[n:2 * n]
        send_sems, recv_sems, local_sems = refs[2 * n:]
        mx, my, mc = _mesh_pos()
        me, sib = (mx, my, mc), (mx, my, 1 - mc)
        chips = [(1 - mx, my), (mx, 1 - my), (1 - mx, 1 - my)]

        def slot(a, dev):
            px, py, pc = dev
            return out_refs[a].at[4 * px + 2 * py + pc]

        def copy(k, a, block, to, src=None):
            return pltpu.make_async_remote_copy(
                src_ref=slot(a, block) if src is None else src, dst_ref=slot(a, block),
                send_sem=send_sems.at[a * 7 + k], recv_sem=recv_sems.at[a * 7 + k],
                device_id=to, device_id_type=MESH)

        mine = [pltpu.make_async_copy(x_refs[a], slot(a, me), local_sems.at[a]) for a in range(n)]
        for cp in mine:
            cp.start()
        first = []
        for a in range(n):
            first.append(copy(0, a, me, sib, x_refs[a]))
            first += [copy(1 + j, a, me, (*chip, mc), x_refs[a]) for j, chip in enumerate(chips)]
        for cp in first:
            cp.start()
        passed = []
        for a in range(n):
            for j, chip in enumerate(chips):
                copy(1 + j, a, (*chip, mc), me).wait_recv()
                fwd = copy(4 + j, a, (*chip, mc), sib)
                fwd.start()
                passed.append(fwd)
        for a in range(n):
            copy(0, a, sib, me).wait_recv()
            for j, chip in enumerate(chips):
                copy(4 + j, a, (*chip, 1 - mc), me).wait_recv()
        for cp in first + passed:
            cp.wait_send()
        for cp in mine:
            cp.wait()

    hbm = pl.BlockSpec(memory_space=pl.ANY)
    return pl.pallas_call(
        body, name=name,
        out_shape=[jax.ShapeDtypeStruct((N_DEV,) + x.shape, x.dtype) for x in xs],
        in_specs=[hbm] * n, out_specs=[hbm] * n,
        scratch_shapes=[pltpu.SemaphoreType.DMA((7 * n,)), pltpu.SemaphoreType.DMA((7 * n,)),
                        pltpu.SemaphoreType.DMA((n,))],
    )(*xs)


_HBM = pl.BlockSpec(memory_space=pltpu.HBM)
_SEM = pl.BlockSpec(memory_space=pltpu.SEMAPHORE)
_ANY = pl.BlockSpec(memory_space=pl.ANY)
_DATAFLOW = pltpu.SideEffectType.DATAFLOW_SIDE_EFFECTING


def _flip_peer(flip):
    mx, my, mc = _mesh_pos()
    return (1 - mx if flip & 2 else mx, 1 - my if flip & 1 else my, mc)


def _remote(src, dst, send_sems, recv_sems, k, peer):
    return pltpu.make_async_remote_copy(src_ref=src, dst_ref=dst, send_sem=send_sems.at[k], recv_sem=recv_sems.at[k],
                                        device_id=peer, device_id_type=MESH)


def _gather_chips_copies(srcs, lands, send_sems, recv_sems):
    mx, my, mc = _mesh_pos()
    me = 4 * mx + 2 * my + mc
    return [_remote(srcs[a], lands[a].at[me], send_sems, recv_sems, 3 * a + flip - 1, _flip_peer(flip))
            for a in range(len(srcs)) for flip in (1, 2, 3)]


def _gather_sibling_copies(srcs, lands, send_sems, recv_sems):
    mx, my, mc = _mesh_pos()
    return [_remote(lands[a].at[2 * k + mc], lands[a].at[2 * k + mc], send_sems, recv_sems, 4 * a + k, (mx, my, 1 - mc))
            for a in range(len(lands)) for k in range(4)]


def _scatter_sibling_copies(srcs, lands, send_sems, recv_sems):
    mx, my, mc = _mesh_pos()
    return [_remote(srcs[a].at[k, 1 - mc], lands[a].at[k], send_sems, recv_sems, 4 * a + k, (mx, my, 1 - mc))
            for a in range(len(srcs)) for k in range(4)]


def _scatter_chips_copies(srcs, lands, send_sems, recv_sems):
    mx, my, _ = _mesh_pos()
    k0 = 2 * mx + my
    return [_remote(srcs[a].at[jnp.bitwise_xor(k0, flip)], lands[a].at[flip - 1], send_sems, recv_sems,
                    3 * a + flip - 1, _flip_peer(flip))
            for a in range(len(srcs)) for flip in (1, 2, 3)]


class _Exchange:
    def __init__(self, copies, n_src, send_sems, recv_sems, thru, token):
        self.copies, self.n_src, self.send_sems, self.recv_sems, self.thru, self.token = (
            copies, n_src, send_sems, recv_sems, thru, token)


def _exchange_start(name, copies, srcs, lands, n_copies, after=()):
    bufs = list(srcs) + list(lands)
    nb, ns = len(bufs), len(srcs)

    def body(*refs):
        send_sems, recv_sems = refs[nb + len(after)], refs[nb + len(after) + 1]
        for cp in copies(refs[:ns], refs[ns:nb], send_sems, recv_sems):
            cp.start()
        refs[-1][...] = jnp.zeros_like(refs[-1])

    out = pl.pallas_call(
        body, name=name,
        out_shape=(pltpu.SemaphoreType.DMA((n_copies,)), pltpu.SemaphoreType.DMA((n_copies,)),
                   *[pltpu.HBM(b.shape, b.dtype) for b in bufs], jax.ShapeDtypeStruct((SUBLANES, LANES), F32)),
        in_specs=[_HBM] * nb + [_ANY] * len(after),
        out_specs=(_SEM, _SEM, *[_HBM] * nb, pl.BlockSpec(memory_space=pltpu.VMEM)),
        input_output_aliases={i: 2 + i for i in range(nb)},
        compiler_params=pltpu.CompilerParams(has_side_effects=_DATAFLOW),
    )(*[pltpu.with_memory_space_constraint(b, pltpu.HBM) for b in bufs], *after)
    return _Exchange(copies, ns, out[0], out[1], list(out[2:2 + nb]), out[-1])


def _exchange_wait(name, ex, after):
    nb, ns = len(ex.thru), ex.n_src

    def body(*refs):
        for cp in ex.copies(refs[:ns], refs[ns:nb], refs[nb], refs[nb + 1]):
            cp.wait_send()
            cp.wait_recv()

    out = pl.pallas_call(
        body, name=name, out_shape=tuple(pltpu.HBM(b.shape, b.dtype) for b in ex.thru),
        in_specs=[_HBM] * nb + [_SEM, _SEM] + [_ANY] * len(after), out_specs=tuple([_HBM] * nb),
        input_output_aliases={i: i for i in range(nb)},
        compiler_params=pltpu.CompilerParams(has_side_effects=_DATAFLOW),
    )(*ex.thru, ex.send_sems, ex.recv_sems, *after)
    return list(out[:ns]), list(out[ns:])


def _col_tile(r, c):
    return next(t for t in (1024, 512, 256, 128) if c % t == 0 and (r * t * 4 <= 2**20 or t == 128))


def _add_sibling(g4, recv, core, name):
    _, _, r, c = g4.shape
    tc = _col_tile(r, c)

    def body(core_ref, g_ref, r_ref, o16_ref, o32_ref):
        s = g_ref[0, 0] + r_ref[0]
        o16_ref[0] = s.astype(BF16)
        o32_ref[0] = s

    out = pl.BlockSpec((1, r, tc), lambda k, j, core_ref: (k, 0, j))
    return pl.pallas_call(
        body, name=name,
        out_shape=[jax.ShapeDtypeStruct((4, r, c), BF16), jax.ShapeDtypeStruct((4, r, c), F32)],
        grid_spec=pltpu.PrefetchScalarGridSpec(
            num_scalar_prefetch=1, grid=(4, c // tc),
            in_specs=[pl.BlockSpec((1, 1, r, tc), lambda k, j, core_ref: (k, core_ref[0], 0, j)), out],
            out_specs=[out, out]),
        compiler_params=pltpu.CompilerParams(dimension_semantics=("parallel", "parallel")),
    )(core, g4, recv)


def _add_chips(p32, recv, chip, name):
    _, r, c = p32.shape
    tc = _col_tile(r, c)

    def body(chip_ref, p_ref, r_ref, o_ref):
        o_ref[...] = ((p_ref[0] + r_ref[0].astype(F32)) + r_ref[1].astype(F32)) + r_ref[2].astype(F32)

    return pl.pallas_call(
        body, name=name, out_shape=jax.ShapeDtypeStruct((r, c), F32),
        grid_spec=pltpu.PrefetchScalarGridSpec(
            num_scalar_prefetch=1, grid=(c // tc,),
            in_specs=[pl.BlockSpec((1, r, tc), lambda j, chip_ref: (chip_ref[0], 0, j)),
                      pl.BlockSpec((3, r, tc), lambda j, chip_ref: (0, 0, j))],
            out_specs=pl.BlockSpec((r, tc), lambda j, chip_ref: (0, j))),
        compiler_params=pltpu.CompilerParams(dimension_semantics=("parallel",)),
    )(chip, p32, recv)


class _ReduceScatter:
    def __init__(self, tag, grads_t, core, chip):
        self.tag, self.core, self.chip, self.names = tag, core, chip, list(grads_t)
        g4s = [g.reshape(4, 2, g.size // (N_DEV * g.shape[-1]), g.shape[-1]) for g in grads_t.values()]
        lands = [lax.empty((4,) + g.shape[2:], F32) for g in g4s]
        self.ex = _exchange_start(f"rs_{tag}_sibling_start", _scatter_sibling_copies, g4s, lands, 4 * len(g4s))
        self.token = self.ex.token

    def start_chips(self, after):
        g4s, from_sibling = _exchange_wait(f"rs_{self.tag}_sibling_wait", self.ex, after)
        parts = [_add_sibling(g4, rv, self.core, f"rs_add_sibling_{k}")
                 for k, g4, rv in zip(self.names, g4s, from_sibling)]
        self.p32s = [p32 for _, p32 in parts]
        p16s = [p16 for p16, _ in parts]
        lands = [lax.empty((3,) + p.shape[1:], BF16) for p in p16s]
        self.ex = _exchange_start(f"rs_{self.tag}_chips_start", _scatter_chips_copies, p16s, lands, 3 * len(p16s))
        self.token = self.ex.token

    def finish(self, after):
        _, from_chips = _exchange_wait(f"rs_{self.tag}_chips_wait", self.ex, after)
        return {k: _add_chips(p32, rv, self.chip, f"rs_add_chips_{k}")
                for k, p32, rv in zip(self.names, self.p32s, from_chips)}


def _rope_tables():
    positions = jnp.arange(SEQ, dtype=F32)
    inv_freq = jnp.power(ROPE_THETA, -jnp.arange(0, ROPE_DIM, 2, dtype=F32) / ROPE_DIM)
    ang = positions[:, None] * inv_freq[None, :]
    cos, sin = jnp.cos(ang), jnp.sin(ang)
    ones = jnp.ones((SEQ, HEAD_DIM - ROPE_DIM), F32)
    zeros8 = jnp.zeros((SEQ, ROPE_HALF), F32)
    zeros = jnp.zeros((SEQ, HEAD_DIM - ROPE_DIM), F32)
    c_head = jnp.concatenate([cos, cos, ones], axis=1)
    s1_head = jnp.concatenate([-sin, zeros8, zeros], axis=1)
    s2_head = jnp.concatenate([zeros8, sin, zeros], axis=1)
    return tuple(jnp.concatenate([t, t], axis=1) for t in (c_head, s1_head, s2_head))


def _tile_lanes(t, width):
    return jnp.concatenate([t] * (width // LANES), axis=1)


def _rope_apply(x, c, s1, s2):
    w = x.shape[1]
    return x * c + pltpu.roll(x, w - ROPE_HALF, 1) * s1 + pltpu.roll(x, ROPE_HALF, 1) * s2


def _rope_apply_t(dy, c, s1, s2):
    w = dy.shape[1]
    return dy * c + pltpu.roll(dy * s1, ROPE_HALF, 1) + pltpu.roll(dy * s2, w - ROPE_HALF, 1)


def _dil_qkv(proj, tables):
    def fn(q, k, v, c, s1, s2):
        c, s1, s2 = (_tile_lanes(t, DIL_WIDTH) for t in (c, s1, s2))
        return _rope_apply(q, c, s1, s2) * QK_SCALE, _rope_apply(k, c, s1, s2), v

    w = DIL_WIDTH
    return _rowwise(fn, "dil_rope", SEQ, 256,
                    [(proj, w, COL_QA // w), (proj, w, COL_KA // w), (proj, w, COL_VA // w)]
                    + [(t, LANES, 0) for t in tables], [],
                    [(w, BF16)] * 3)


def _dil_qk_bwd(dq, dk, tables):
    def fn(dq, dk, c, s1, s2):
        c, s1, s2 = (_tile_lanes(t, DIL_WIDTH) for t in (c, s1, s2))
        return _rope_apply_t(dq, c, s1, s2), _rope_apply_t(dk, c, s1, s2)

    w = DIL_WIDTH
    return _rowwise(fn, "dil_rope_bwd", SEQ, 256, [(dq, w, 0), (dk, w, 0)] + [(t, LANES, 0) for t in tables], [],
                    [(w, BF16)] * 2)


def _residue_major(a, front_pad=0):
    cols = []
    for g, d in enumerate(DILATIONS):
        part = a[:, g * 256:(g + 1) * 256]
        cols.append(part.reshape(SEQ // d, d, 256).transpose(1, 0, 2).reshape(SEQ, 256))
    out = jnp.concatenate(cols, axis=1)
    return jnp.pad(out, ((front_pad, 0), (0, 0))) if front_pad else out


def _token_major(a):
    cols = []
    for g, d in enumerate(DILATIONS):
        part = a[:, g * 256:(g + 1) * 256]
        cols.append(part.reshape(d, SEQ // d, 256).transpose(1, 0, 2).reshape(SEQ, 256))
    return jnp.concatenate(cols, axis=1)


def _dil_prev_limit(pair, n):
    g = pair // 2
    mask = jnp.where(g == 0, 15, jnp.where(g == 1, 3, 0))
    return jnp.where(jnp.bitwise_and(n, mask) != 0, 0, BLOCK)


def _dil_valid(limit):
    row = lax.broadcasted_iota(jnp.int32, (BLOCK, 2 * BLOCK), 0)
    col = lax.broadcasted_iota(jnp.int32, (BLOCK, 2 * BLOCK), 1)
    dist = col - row
    return jnp.logical_and(dist >= jnp.where(col < BLOCK, limit, -BLOCK), dist <= BLOCK)


def _upper_half():
    return lax.broadcasted_iota(jnp.int32, (1, LANES), 1) >= HEAD_DIM


def _dil_fwd(q, k, v):
    def body(q_ref, k_ref, v_ref, o_ref, lse_ref):
        pair = pl.program_id(0)
        upper = _upper_half()

        def block(n, carry):
            r0 = pl.multiple_of(n * BLOCK, BLOCK)
            qb = q_ref[pl.ds(r0, BLOCK), :]
            kw = k_ref[pl.ds(r0, 2 * BLOCK), :]
            vw = v_ref[pl.ds(r0, 2 * BLOCK), :]
            valid = _dil_valid(_dil_prev_limit(pair, n))
            outs, lses = [], []
            for head_mask in (jnp.logical_not(upper), upper):
                s = jnp.where(valid, _dot_nt(qb, jnp.where(head_mask, kw, 0)), NEG_INF)
                m = jnp.max(s, axis=-1, keepdims=True)
                p = jnp.exp(s - m)
                den = jnp.sum(p, axis=-1, keepdims=True)
                outs.append(_dot_nn((p / den).astype(BF16), jnp.where(head_mask, vw, 0)))
                lses.append(m + jnp.log(den))
            o_ref[pl.ds(r0, BLOCK), :] = outs[0] + outs[1]
            lse_ref[pl.ds(r0, BLOCK), :] = jnp.where(upper, lses[1], lses[0])
            return carry

        lax.fori_loop(0, N_BLOCKS, block, 0, unroll=2)

    return pl.pallas_call(
        body, name="dil_attn_fwd", grid=(DIL_WIDTH // LANES,),
        in_specs=[pl.BlockSpec((SEQ, LANES), lambda p: (0, p)), pl.BlockSpec((SEQ + BLOCK, LANES), lambda p: (0, p)),
                  pl.BlockSpec((SEQ + BLOCK, LANES), lambda p: (0, p))],
        out_specs=[pl.BlockSpec((SEQ, LANES), lambda p: (0, p))] * 2,
        out_shape=[jax.ShapeDtypeStruct((SEQ, DIL_WIDTH), F32)] * 2,
        compiler_params=pltpu.CompilerParams(dimension_semantics=("parallel",)),
    )(q, k, v)


def _dil_bwd(q, k, v, do, lse, c):
    def body(q_ref, k_ref, v_ref, do_ref, lse_ref, c_ref, dq_ref, dk_ref, dv_ref, dk_acc, dv_acc):
        pair = pl.program_id(0)
        upper = _upper_half()
        dk_acc[...] = jnp.zeros_like(dk_acc)
        dv_acc[...] = jnp.zeros_like(dv_acc)

        def block(n, carry):
            r0 = pl.multiple_of(n * BLOCK, BLOCK)
            qb, dob = q_ref[pl.ds(r0, BLOCK), :], do_ref[pl.ds(r0, BLOCK), :]
            kw = k_ref[pl.ds(r0, 2 * BLOCK), :]
            vw = v_ref[pl.ds(r0, 2 * BLOCK), :]
            lse_t, c_t = lse_ref[pl.ds(r0, BLOCK), :], c_ref[pl.ds(r0, BLOCK), :]
            valid = _dil_valid(_dil_prev_limit(pair, n))
            dq = None
            for e, head_mask in enumerate((jnp.logical_not(upper), upper)):
                km, vm = jnp.where(head_mask, kw, 0), jnp.where(head_mask, vw, 0)
                lse_col = lse_t[:, e * HEAD_DIM:e * HEAD_DIM + 1]
                c_col = c_t[:, e * HEAD_DIM:e * HEAD_DIM + 1]
                p = jnp.where(valid, jnp.exp(_dot_nt(qb, km) - lse_col), 0.0)
                ds = (p * (_dot_nt(dob, vm) + c_col)).astype(BF16)
                part = _dot_nn(ds, km)
                dq = part if dq is None else dq + part
                dk_acc[pl.ds(r0, 2 * BLOCK), :] += _dot_tn(ds, jnp.where(head_mask, qb, 0))
                dv_acc[pl.ds(r0, 2 * BLOCK), :] += _dot_tn(p.astype(BF16), jnp.where(head_mask, dob, 0))
            dq_ref[pl.ds(r0, BLOCK), :] = dq * QK_SCALE
            return carry

        lax.fori_loop(0, N_BLOCKS, block, 0, unroll=2)
        dk_ref[...] = dk_acc[BLOCK:, :]
        dv_ref[...] = dv_acc[BLOCK:, :]

    tok = pl.BlockSpec((SEQ, LANES), lambda p: (0, p))
    padded = pl.BlockSpec((SEQ + BLOCK, LANES), lambda p: (0, p))
    shape = jax.ShapeDtypeStruct((SEQ, DIL_WIDTH), F32)
    return pl.pallas_call(
        body, name="dil_attn_bwd", grid=(DIL_WIDTH // LANES,),
        in_specs=[tok, padded, padded, tok, tok, tok], out_specs=[tok, tok, tok], out_shape=[shape] * 3,
        scratch_shapes=[pltpu.VMEM((SEQ + BLOCK, LANES), F32)] * 2,
        compiler_params=pltpu.CompilerParams(dimension_semantics=("parallel",)),
    )(q, k, v, do, lse, c)


def _group_weights(l0, l1, l2):
    m = jnp.maximum(jnp.maximum(l0, l1), l2)
    e0, e1, e2 = jnp.exp(l0 - m), jnp.exp(l1 - m), jnp.exp(l2 - m)
    tot = e0 + e1 + e2
    return e0 / tot, e1 / tot, e2 / tot


def _dil_combine(o, lse, deps=()):
    def fn(o0, o1, o2, l0, l1, l2):
        w0, w1, w2 = _group_weights(l0, l1, l2)
        return w0 * o0 + w1 * o1 + w2 * o2

    w = DIL_OUT_WIDTH
    return _rowwise(fn, "dil_combine", SEQ, 512, [(o, w, g) for g in range(3)] + [(lse, w, g) for g in range(3)], [],
                    [(w, F32)], deps=deps)[0]


def _dil_combine_bwd(d_out, o, lse):
    w = DIL_OUT_WIDTH

    def fn(d, o0, o1, o2, l0, l1, l2):
        row = lax.broadcasted_iota(jnp.int32, (w, w), 0) // HEAD_DIM
        col = lax.broadcasted_iota(jnp.int32, (w, w), 1) // HEAD_DIM
        same_head = jnp.where(row == col, 1.0, 0.0).astype(BF16)
        ws = _group_weights(l0, l1, l2)
        dws = [_dot3_nn(d * og, same_head) for og in (o0, o1, o2)]
        mean = ws[0] * dws[0] + ws[1] * dws[1] + ws[2] * dws[2]
        do = jnp.concatenate([wg * d for wg in ws], axis=1)
        c = jnp.concatenate([-wg * mean for wg in ws], axis=1)
        return do, c

    return _rowwise(fn, "dil_combine_bwd", SEQ, 256,
                    [(d_out, w, 0)] + [(o, w, g) for g in range(3)] + [(lse, w, g) for g in range(3)], [],
                    [(DIL_WIDTH, BF16), (DIL_WIDTH, F32)])


def _fox_qkv(proj, deps=()):
    def fn(q, k, v):
        return q * QK_SCALE, k, v

    w = FOX_WIDTH
    return _rowwise(fn, "fox_cast", SEQ, 256,
                    [(proj, w, COL_QB // w), (proj, w, COL_KB // w), (proj, w, COL_VB // w)], [], [(w, BF16)] * 3,
                    deps=deps)


def _log1p(e):
    u = 1.0 + e
    return jnp.where(u == 1.0, e, jnp.log(u) * (e / (u - 1.0)))


def _fox_gate(proj, b_pad):
    def body(f_ref, b_ref, o_ref):
        z = f_ref[...] + b_ref[...]
        logf = (jnp.minimum(z, 0.0) - _log1p(jnp.exp(-jnp.abs(z)))).T[:F_ROWS]
        row = lax.broadcasted_iota(jnp.int32, (BLOCK, BLOCK), 0)
        col = lax.broadcasted_iota(jnp.int32, (BLOCK, BLOCK), 1)
        before = jnp.where(row <= col, 1.0, 0.0).astype(BF16)
        carry = jnp.zeros((F_ROWS, 1), F32)
        for blk in range(N_BLOCKS):
            run = _dot3_nn(logf[:, blk * BLOCK:(blk + 1) * BLOCK], before) + carry
            o_ref[:, blk * BLOCK:(blk + 1) * BLOCK] = run
            carry = run[:, BLOCK - 1:BLOCK]

    return pl.pallas_call(
        body, name="fox_gate", grid=(1,),
        in_specs=[pl.BlockSpec((SEQ, LANES), lambda i: (0, COL_F // LANES)), pl.BlockSpec((1, LANES), lambda i: (0, 0))],
        out_specs=pl.BlockSpec((F_ROWS, SEQ), lambda i: (0, 0)),
        out_shape=jax.ShapeDtypeStruct((F_ROWS, SEQ), F32),
    )(proj, b_pad)


def _fox_gate_bwd(d_cum, proj, b_pad):
    def body(d_ref, f_ref, b_ref, dz_ref, db_ref):
        row = lax.broadcasted_iota(jnp.int32, (BLOCK, BLOCK), 0)
        col = lax.broadcasted_iota(jnp.int32, (BLOCK, BLOCK), 1)
        after = jnp.where(row >= col, 1.0, 0.0).astype(BF16)
        carry = jnp.zeros((F_ROWS, 1), F32)
        parts = [None] * N_BLOCKS
        for blk in reversed(range(N_BLOCKS)):
            run = _dot3_nn(d_ref[:, blk * BLOCK:(blk + 1) * BLOCK], after) + carry
            parts[blk] = run
            carry = run[:, 0:1]
        dlogf = jnp.concatenate(parts, axis=1)
        dlogf = jnp.concatenate([dlogf, jnp.zeros((LANES - F_ROWS, SEQ), F32)], axis=0).T
        dz = dlogf * _sigmoid(-(f_ref[...] + b_ref[...]))
        dz_ref[...] = dz.astype(BF16)
        db_ref[...] = jnp.sum(dz, axis=0, keepdims=True)

    return pl.pallas_call(
        body, name="fox_gate_bwd", grid=(1,),
        in_specs=[pl.BlockSpec((F_ROWS, SEQ), lambda i: (0, 0)),
                  pl.BlockSpec((SEQ, LANES), lambda i: (0, COL_F // LANES)), pl.BlockSpec((1, LANES), lambda i: (0, 0))],
        out_specs=[pl.BlockSpec((SEQ, LANES), lambda i: (0, 0)), pl.BlockSpec((1, LANES), lambda i: (0, 0))],
        out_shape=[jax.ShapeDtypeStruct((SEQ, LANES), BF16), jax.ShapeDtypeStruct((1, LANES), F32)],
    )(d_cum, proj, b_pad)


FOX_TILE = 256
FOX_TILES = SEQ // FOX_TILE


def _row_to_col(row):
    n = row.shape[1]
    eye = lax.broadcasted_iota(jnp.int32, (n, n), 0) == lax.broadcasted_iota(jnp.int32, (n, n), 1)
    return jnp.sum(jnp.where(eye, row, 0.0), axis=1, keepdims=True)


def _fox_scores(q_tile, km, f_row, i):
    t = FOX_TILE
    ext = (i + 1) * t
    f_q = _row_to_col(f_row[:, i * t:(i + 1) * t])
    s = _dot_nt(q_tile, km[:ext]) + (f_q - f_row[:, :ext])
    row = lax.broadcasted_iota(jnp.int32, (t, ext), 0) + i * t
    col = lax.broadcasted_iota(jnp.int32, (t, ext), 1)
    return s, col <= row


def _fox_fwd(q, k, v, f_rows):
    t = FOX_TILE

    def body(q_ref, k_ref, v_ref, f_ref, o_ref, lse_ref):
        pair = pl.program_id(0)
        upper = _upper_half()
        masks = (jnp.logical_not(upper), upper)
        kms = [jnp.where(hm, k_ref[...], 0) for hm in masks]
        vms = [jnp.where(hm, v_ref[...], 0) for hm in masks]
        f_row = [f_ref[pl.ds(2 * pair + e, 1), :] for e in range(2)]
        for i in range(FOX_TILES):
            q_tile = q_ref[i * t:(i + 1) * t, :]
            outs, lses = [], []
            for e in range(2):
                s, causal = _fox_scores(q_tile, kms[e], f_row[e], i)
                s = jnp.where(causal, s, NEG_INF)
                m = jnp.max(s, axis=-1, keepdims=True)
                p = jnp.exp(s - m)
                den = jnp.sum(p, axis=-1, keepdims=True)
                outs.append(_dot_nn((p / den).astype(BF16), vms[e][:(i + 1) * t]))
                lses.append(m + jnp.log(den))
            o_ref[i * t:(i + 1) * t, :] = outs[0] + outs[1]
            lse_ref[i * t:(i + 1) * t, :] = jnp.where(upper, lses[1], lses[0])

    tok = pl.BlockSpec((SEQ, LANES), lambda p: (0, p))
    return pl.pallas_call(
        body, name="fox_attn_fwd", grid=(FOX_WIDTH // LANES,),
        in_specs=[tok, tok, tok, pl.BlockSpec((F_ROWS, SEQ), lambda p: (0, 0))], out_specs=[tok, tok],
        out_shape=[jax.ShapeDtypeStruct((SEQ, FOX_WIDTH), F32)] * 2,
        compiler_params=pltpu.CompilerParams(
            dimension_semantics=("parallel",), vmem_limit_bytes=_vmem_limit(8 * t * SEQ * 4)),
    )(q, k, v, f_rows)


def _fox_bwd(q, k, v, do, lse, f_rows):
    t = FOX_TILE

    def body(q_ref, k_ref, v_ref, do_ref, lse_ref, f_ref, dq_ref, dk_ref, dv_ref, df_ref):
        pair = pl.program_id(0)
        upper = _upper_half()
        masks = (jnp.logical_not(upper), upper)
        kms = [jnp.where(hm, k_ref[...], 0) for hm in masks]
        vms = [jnp.where(hm, v_ref[...], 0) for hm in masks]
        f_row = [f_ref[pl.ds(2 * pair + e, 1), :] for e in range(2)]
        dk_ref[...] = jnp.zeros_like(dk_ref)
        dv_ref[...] = jnp.zeros_like(dv_ref)
        df_ref[...] = jnp.zeros_like(df_ref)
        for i in range(FOX_TILES):
            ext = (i + 1) * t
            q_tile, do_tile = q_ref[i * t:(i + 1) * t, :], do_ref[i * t:(i + 1) * t, :]
            lse_t = lse_ref[i * t:(i + 1) * t, :]
            dq = None
            for e in range(2):
                s, causal = _fox_scores(q_tile, kms[e], f_row[e], i)
                p = jnp.where(causal, jnp.exp(s - lse_t[:, e * HEAD_DIM:e * HEAD_DIM + 1]), 0.0)
                dp = _dot_nt(do_tile, vms[e][:ext])
                ds = p * (dp - jnp.sum(p * dp, axis=-1, keepdims=True))
                df_ref[0, e:e + 1, :ext] -= jnp.sum(ds, axis=0, keepdims=True)
                ds = ds.astype(BF16)
                part = _dot_nn(ds, kms[e][:ext])
                dq = part if dq is None else dq + part
                dk_ref[:ext, :] += _dot_tn(ds, jnp.where(masks[e], q_tile, 0))
                dv_ref[:ext, :] += _dot_tn(p.astype(BF16), jnp.where(masks[e], do_tile, 0))
            dq_ref[i * t:(i + 1) * t, :] = dq * QK_SCALE

    tok = pl.BlockSpec((SEQ, LANES), lambda p: (0, p))
    shape = jax.ShapeDtypeStruct((SEQ, FOX_WIDTH), F32)
    return pl.pallas_call(
        body, name="fox_attn_bwd", grid=(FOX_WIDTH // LANES,),
        in_specs=[tok] * 5 + [pl.BlockSpec((F_ROWS, SEQ), lambda p: (0, 0))],
        out_specs=[tok, tok, tok, pl.BlockSpec((1, SUBLANES, SEQ), lambda p: (p, 0, 0))],
        out_shape=[shape, shape, shape, jax.ShapeDtypeStruct((FOX_WIDTH // LANES, SUBLANES, SEQ), F32)],
        compiler_params=pltpu.CompilerParams(
            dimension_semantics=("parallel",), vmem_limit_bytes=_vmem_limit(10 * t * SEQ * 4)),
    )(q, k, v, do, lse, f_rows)


MIX_TILE = 256


def _mix_out(out_a, out_b, proj, x, wt_pa, wt_pb, w_out, g_post, g_ffn_pre):
    tm = MIX_TILE

    def body(a_ref, b_ref, ga_ref, gb_ref, x_ref, wpa_ref, wpb_ref, wo_ref, g2_ref, g3_ref,
             merged_ref, mix_ref, x1_ref, h2_ref):
        ya = _dot_nt(a_ref[...].astype(BF16), wpa_ref[...])
        yb = _dot_nt(b_ref[...].astype(BF16), wpb_ref[...])
        merged = (_sigmoid(ga_ref[...]) * ya + _sigmoid(gb_ref[...]) * yb).astype(BF16)
        merged_ref[...] = merged
        mix = _dot_nn(merged, wo_ref[...])
        mix_ref[...] = mix
        x1 = x_ref[...] + mix * _rms_scale(mix) * g2_ref[...]
        x1_ref[...] = x1
        h2_ref[...] = (x1 * _rms_scale(x1) * g3_ref[...]).astype(BF16)

    def rows(w, cb=0):
        return pl.BlockSpec((tm, w), lambda i, cb=cb: (i, cb))

    def whole(a):
        return pl.BlockSpec(a.shape, lambda i: (0, 0))

    d = D_MODEL
    blk = _nbytes((tm, d), F32) * 6 + sum(_nbytes(a.shape, BF16) for a in (wt_pa, wt_pb, w_out))
    return pl.pallas_call(
        body, name="mix_out", grid=(SEQ // tm,),
        in_specs=[rows(DIL_OUT_WIDTH), rows(FOX_WIDTH), rows(d, COL_GA // d), rows(d, COL_GB // d), rows(d),
                  whole(wt_pa), whole(wt_pb), whole(w_out), whole(g_post), whole(g_ffn_pre)],
        out_specs=[rows(d)] * 4,
        out_shape=[jax.ShapeDtypeStruct((SEQ, d), dt) for dt in (BF16, F32, F32, BF16)],
        compiler_params=pltpu.CompilerParams(dimension_semantics=("parallel",), vmem_limit_bytes=_vmem_limit(blk)),
    )(out_a, out_b, proj, proj, x, wt_pa, wt_pb, w_out, g_post, g_ffn_pre)


def _mix_out_bwd(dmix, out_a, out_b, proj, wt_pa, wt_pb, w_out, deps=()):
    tm = MIX_TILE

    def body(dm_ref, a_ref, b_ref, ga_ref, gb_ref, wpa_ref, wpb_ref, wo_ref, *rest):
        dga_ref, dgb_ref, dya_ref, dyb_ref, da_ref, db_ref = rest[len(deps):]
        dmerged = _dot_nt(dm_ref[...], wo_ref[...])
        ya = _dot_nt(a_ref[...].astype(BF16), wpa_ref[...])
        yb = _dot_nt(b_ref[...].astype(BF16), wpb_ref[...])
        sa, sb = _sigmoid(ga_ref[...]), _sigmoid(gb_ref[...])
        dga_ref[...] = (dmerged * ya * (sa * (1.0 - sa))).astype(BF16)
        dgb_ref[...] = (dmerged * yb * (sb * (1.0 - sb))).astype(BF16)
        dya = (dmerged * sa).astype(BF16)
        dyb = (dmerged * sb).astype(BF16)
        dya_ref[...] = dya
        dyb_ref[...] = dyb
        da_ref[...] = _dot_nn(dya, wpa_ref[...])
        db_ref[...] = _dot_nn(dyb, wpb_ref[...]).astype(BF16)

    def rows(w, cb=0):
        return pl.BlockSpec((tm, w), lambda i, cb=cb: (i, cb))

    def whole(a):
        return pl.BlockSpec(a.shape, lambda i: (0, 0))

    d = D_MODEL
    blk = _nbytes((tm, d), F32) * 8 + sum(_nbytes(a.shape, BF16) for a in (wt_pa, wt_pb, w_out))
    return pl.pallas_call(
        body, name="mix_out_bwd", grid=(SEQ // tm,),
        in_specs=[rows(d), rows(DIL_OUT_WIDTH), rows(FOX_WIDTH), rows(d, COL_GA // d), rows(d, COL_GB // d),
                  whole(wt_pa), whole(wt_pb), whole(w_out)] + [_ANY] * len(deps),
        out_specs=[rows(d)] * 4 + [rows(DIL_OUT_WIDTH), rows(FOX_WIDTH)],
        out_shape=[jax.ShapeDtypeStruct((SEQ, d), BF16)] * 4
        + [jax.ShapeDtypeStruct((SEQ, DIL_OUT_WIDTH), F32), jax.ShapeDtypeStruct((SEQ, FOX_WIDTH), BF16)],
        compiler_params=pltpu.CompilerParams(dimension_semantics=("parallel",), vmem_limit_bytes=_vmem_limit(blk)),
    )(dmix, out_a, out_b, proj, proj, wt_pa, wt_pb, w_out, *deps)


FFN_TM, FFN_TN = 1024, 256


def _ffn_up(h2, wt_gate, wt_up):
    tm, tn = FFN_TM, FFN_TN

    def body(h_ref, wg_ref, wu_ref, gate_ref, up_ref, act_ref):
        gate = _dot_nt(h_ref[...], wg_ref[...])
        up = _dot_nt(h_ref[...], wu_ref[...])
        gate_ref[...] = gate
        up_ref[...] = up
        act_ref[...] = (gate * _sigmoid(gate) * up).astype(BF16)

    tile = pl.BlockSpec((tm, tn), lambda i, j: (i, j))
    w_spec = pl.BlockSpec((tn, D_MODEL), lambda i, j: (j, 0))
    return pl.pallas_call(
        body, name="ffn_up", grid=(SEQ // tm, D_FF // tn),
        in_specs=[pl.BlockSpec((tm, D_MODEL), lambda i, j: (i, 0)), w_spec, w_spec],
        out_specs=[tile, tile, tile],
        out_shape=[jax.ShapeDtypeStruct((SEQ, D_FF), dt) for dt in (F32, F32, BF16)],
        compiler_params=pltpu.CompilerParams(
            dimension_semantics=("parallel", "parallel"), vmem_limit_bytes=_vmem_limit(8 * 2**20)),
    )(h2, wt_gate, wt_up)


def _ffn_act_bwd(dff, w_down, gate, up):
    tm, tn = FFN_TM, FFN_TN

    def body(d_ref, wd_ref, gate_ref, up_ref, dgate_ref, dup_ref):
        dact = _dot_nt(d_ref[...], wd_ref[...])
        gate = gate_ref[...]
        sg = _sigmoid(gate)
        dgate_ref[...] = (dact * up_ref[...] * (sg * (1.0 + gate * (1.0 - sg)))).astype(BF16)
        dup_ref[...] = (dact * (gate * sg)).astype(BF16)

    tile = pl.BlockSpec((tm, tn), lambda i, j: (i, j))
    return pl.pallas_call(
        body, name="ffn_act_bwd", grid=(SEQ // tm, D_FF // tn),
        in_specs=[pl.BlockSpec((tm, D_MODEL), lambda i, j: (i, 0)), pl.BlockSpec((tn, D_MODEL), lambda i, j: (j, 0)),
                  tile, tile],
        out_specs=[tile, tile],
        out_shape=[jax.ShapeDtypeStruct((SEQ, D_FF), BF16)] * 2,
        compiler_params=pltpu.CompilerParams(
            dimension_semantics=("parallel", "parallel"), vmem_limit_bytes=_vmem_limit(8 * 2**20)),
    )(dff, w_down, gate, up)


def _loss_head(ff, x1, target, g_post):
    def fn(ff, x1, tgt, g):
        r = _rms_scale(ff)
        nrm = ff * r
        err = (x1 + nrm * g) - tgt
        loss = 0.5 * jnp.sum(jnp.mean(err * err, axis=-1, keepdims=True), axis=0, keepdims=True)
        dy = err * (1.0 / D_MODEL)
        u = dy * g
        dff = r * u - ff * (r * r * r) * jnp.mean(u * ff, axis=-1, keepdims=True)
        return dy, dff, jnp.broadcast_to(loss, (1, LANES)), jnp.sum(dy * nrm, axis=0, keepdims=True)

    d = D_MODEL
    return _rowwise(fn, "loss_head", SEQ, 256, [(ff, d, 0), (x1, d, 0), (target, d, 0)], [g_post],
                    [(d, F32), (d, BF16)], [LANES, d])


def _post_ffn_bwd(dh2, x1, dy, mix, g_ffn_pre, g_mix_post):
    def fn(dh2, x1, dy, mix, g3, g2):
        dx, dg3 = _rms_bwd(x1, dh2, g3)
        dx1 = dy + dx
        dmix, dg2 = _rms_bwd(mix, dx1, g2)
        return dx1, dmix, dg3, dg2

    d = D_MODEL
    return _rowwise(fn, "post_ffn_bwd", SEQ, 256, [(dh2, d, 0), (x1, d, 0), (dy, d, 0), (mix, d, 0)],
                    [g_ffn_pre, g_mix_post], [(d, F32), (d, BF16)], [d, d])


def _input_bwd(dh, x, dx1, g_pre, deps=()):
    def fn(dh, x, dx1, g):
        dx, dg = _rms_bwd(x, dh, g)
        return dx1 + dx, dg

    d = D_MODEL
    return _rowwise(fn, "input_bwd", SEQ, 256, [(dh, d, 0), (x, d, 0), (dx1, d, 0)], [g_pre], [(d, F32)], [d],
                    deps=deps)


def _adam_math(w, g, m, v):
    m = ADAM_B1 * m + (1.0 - ADAM_B1) * g
    v = ADAM_B2 * v + (1.0 - ADAM_B2) * (g * g)
    m_hat = m / (1.0 - ADAM_B1 ** ADAM_STEP)
    v_hat = v / (1.0 - ADAM_B2 ** ADAM_STEP)
    delta = -ADAM_LR * (m_hat / (jnp.sqrt(v_hat) + ADAM_EPS) + ADAM_WD * w)
    return delta, m, v


def _adam(w, g, m, v, name):
    r, c = w.shape
    tc = _col_tile(r, c)

    def body(w_ref, g_ref, m_ref, v_ref, d_ref, nm_ref, nv_ref):
        d_ref[...], nm_ref[...], nv_ref[...] = _adam_math(w_ref[...], g_ref[...], m_ref[...], v_ref[...])

    spec = pl.BlockSpec((r, tc), lambda j: (0, j))
    return pl.pallas_call(
        body, name=name, grid=(c // tc,), in_specs=[spec] * 4, out_specs=[spec] * 3,
        out_shape=[jax.ShapeDtypeStruct((r, c), F32)] * 3,
        compiler_params=pltpu.CompilerParams(dimension_semantics=("parallel",)),
    )(w, g, m, v)


def _adam_small(gathered, ws, ms, vs, deps=()):
    n = len(ws)

    def body(*refs):
        outs = refs[4 * n + len(deps):]
        for i in range(n):
            ga_ref, w_ref, m_ref, v_ref = (refs[j * n + i] for j in range(4))
            g = ga_ref[0]
            for dev in range(1, N_DEV):
                g = g + ga_ref[dev]
            g = g[:, :w_ref.shape[1]]
            outs[4 * i][...] = g
            outs[4 * i + 1][...], outs[4 * i + 2][...], outs[4 * i + 3][...] = _adam_math(
                w_ref[...], g, m_ref[...], v_ref[...])

    out_shape = [jax.ShapeDtypeStruct(w.shape, F32) for w in ws for _ in range(4)]
    out = pl.pallas_call(body, name="adam_small", out_shape=out_shape)(*gathered, *ws, *ms, *vs, *deps)
    return [out[4 * i:4 * i + 4] for i in range(n)]


_PROJ_SEGMENTS = ((3848, 5896), (None, 256), (0, 3840), (3840, 3848), (None, PROJ_COLS - COL_F - 8))


def _proj_weight_t(gathered):
    w = gathered.reshape(IN_COLS, D_MODEL)
    return jnp.concatenate([jnp.zeros((hi, D_MODEL), w.dtype) if lo is None else w[lo:hi] for lo, hi in _PROJ_SEGMENTS],
                           axis=0)


def _proj_weight_grad_slots(dwt_r):
    starts, at = [], 0
    for lo, hi in _PROJ_SEGMENTS:
        if lo is not None:
            starts.append((lo, hi, at))
        at += hi if lo is None else hi - lo
    slots = []
    for dev in range(N_DEV):
        pieces, lo, end = [], dev * IN_SHARD, (dev + 1) * IN_SHARD
        for seg_lo, seg_hi, seg_at in sorted(starts):
            a, b = max(lo, seg_lo), min(end, seg_hi)
            if a < b:
                pieces.append(dwt_r[seg_at + a - seg_lo:seg_at + b - seg_lo])
        slots.append(pieces[0] if len(pieces) == 1 else jnp.concatenate(pieces, axis=0))
    return jnp.stack(slots)


def kernel(x, w_in, w_proj_a, w_proj_b, w_out, b_forget, w_ffn_gate, w_ffn_up, w_ffn_down, norm_mix_pre, norm_mix_post, norm_ffn_pre, norm_ffn_post, loss_target, m_w_in, m_w_proj_a, m_w_proj_b, m_w_out, m_b_forget, m_w_ffn_gate, m_w_ffn_up, m_w_ffn_down, m_norm_mix_pre, m_norm_mix_post, m_norm_ffn_pre, m_norm_ffn_post, v_w_in, v_w_proj_a, v_w_proj_b, v_w_out, v_b_forget, v_w_ffn_gate, v_w_ffn_up, v_w_ffn_down, v_norm_mix_pre, v_norm_mix_post, v_norm_ffn_pre, v_norm_ffn_post):
    d = D_MODEL
    names = ("w_in", "w_proj_a", "w_proj_b", "w_out", "w_ffn_gate", "w_ffn_up", "w_ffn_down")
    col_sharded = ("w_in", "w_proj_a", "w_proj_b", "w_ffn_gate", "w_ffn_up")

    def row_shards(arrs):
        return {k: (a[0].T if k in col_sharded else a[0]) for k, a in zip(names, arrs)}

    shards = row_shards((w_in, w_proj_a, w_proj_b, w_out, w_ffn_gate, w_ffn_up, w_ffn_down))
    moments_m = row_shards((m_w_in, m_w_proj_a, m_w_proj_b, m_w_out, m_w_ffn_gate, m_w_ffn_up, m_w_ffn_down))
    moments_v = row_shards((v_w_in, v_w_proj_a, v_w_proj_b, v_w_out, v_w_ffn_gate, v_w_ffn_up, v_w_ffn_down))
    core = lax.axis_index("c").astype(jnp.int32).reshape(1)
    chip = (2 * lax.axis_index("x") + lax.axis_index("y")).astype(jnp.int32).reshape(1)
    x2, target = x[0], loss_target[0]

    me = 4 * lax.axis_index("x") + 2 * lax.axis_index("y") + lax.axis_index("c")
    mix_names, ffn_names = names[:4], names[4:]
    first_names, later_names = names[:1], names[1:]
    shards16 = {k: shards[k].astype(BF16) for k in names}

    def landing(k):
        return lax.dynamic_update_slice(lax.empty((N_DEV,) + shards[k].shape, BF16), shards16[k][None], (me, 0, 0))

    ag_first = _exchange_start("ag_first_chips_start", _gather_chips_copies, [shards16[k] for k in first_names],
                               [landing(k) for k in first_names], 3 * len(first_names))
    h = _rowwise(lambda xb, g: xb * _rms_scale(xb) * g, "norm_mix_pre", SEQ, 256, [(x2, d, 0)], [norm_mix_pre],
                 [(d, BF16)], deps=[ag_first.token])[0]
    _, lands = _exchange_wait("ag_first_chips_wait", ag_first, [h])
    ag_first = _exchange_start("ag_first_sibling_start", _gather_sibling_copies, [], lands, 4 * len(first_names))
    ag_later = _exchange_start("ag_later_chips_start", _gather_chips_copies, [shards16[k] for k in later_names],
                               [landing(k) for k in later_names], 3 * len(later_names), after=[ag_first.token])
    gathered = dict(zip(first_names, _exchange_wait("ag_first_sibling_wait", ag_first, [ag_later.token])[1]))
    wt_r = _proj_weight_t(gathered["w_in"])

    proj = _matmul([(h, wt_r)], "nt", F32, "in_proj", 1024, 896, 1024)
    tables = _rope_tables()
    qa_tok, ka_tok, va_tok = _dil_qkv(proj, tables)
    qa = _residue_major(qa_tok)
    ka, va = _residue_major(ka_tok, BLOCK), _residue_major(va_tok, BLOCK)
    o_dil, lse_dil = _dil_fwd(qa, ka, va)
    o_tok, lse_tok = _token_major(o_dil), _token_major(lse_dil)
    out_a = _dil_combine(o_tok, lse_tok)
    _, lands = _exchange_wait("ag_later_chips_wait", ag_later, [out_a])
    ag_later = _exchange_start("ag_later_sibling_start", _gather_sibling_copies, [], lands, 4 * len(later_names))

    b_pad = jnp.pad(b_forget, ((0, 0), (0, LANES - N_FOX_HEADS)))
    f_rows = _fox_gate(proj, b_pad)
    qb, kb, vb = _fox_qkv(proj, deps=[ag_later.token])
    out_b, lse_fox = _fox_fwd(qb, kb, vb, f_rows)

    gathered = dict(zip(later_names, _exchange_wait("ag_later_sibling_wait", ag_later, [out_b])[1]))
    wt_pa = gathered["w_proj_a"].reshape(d, DIL_OUT_WIDTH)
    wt_pb = gathered["w_proj_b"].reshape(d, FOX_WIDTH)
    w_o = gathered["w_out"].reshape(d, d)
    wt_g = gathered["w_ffn_gate"].reshape(D_FF, d)
    wt_u = gathered["w_ffn_up"].reshape(D_FF, d)
    w_d = gathered["w_ffn_down"].reshape(D_FF, d)
    merged, mix, x1, h2 = _mix_out(out_a, out_b, proj, x2, wt_pa, wt_pb, w_o, norm_mix_post, norm_ffn_pre)

    gate, up, act = _ffn_up(h2, wt_g, wt_u)
    ff = _matmul([(act, w_d)], "nn", F32, "ffn_down", 1024, 1024, 1408)
    dy, dff, loss_part, dg_ffn_post = _loss_head(ff, x1, target, norm_ffn_post)
    loss = lax.psum(loss_part[0, 0], ("x", "y", "c"))

    dgate, dup = _ffn_act_bwd(dff, w_d, gate, up)
    grads_t = {}
    grads_t["w_ffn_down"] = _matmul([(act, dff)], "tn", F32, "grad_w_ffn_down", 1408, 1024, 1024)
    grads_t["w_ffn_gate"] = _matmul([(dgate, h2)], "tn", F32, "grad_w_ffn_gate", 1408, 1024, 1024)
    grads_t["w_ffn_up"] = _matmul([(dup, h2)], "tn", F32, "grad_w_ffn_up", 1408, 1024, 1024)
    rs_ffn = _ReduceScatter("ffn", {k: grads_t[k] for k in ffn_names}, core, chip)
    dh2 = _matmul([(dgate, wt_g), (dup, wt_u)], "nn", F32, "ffn_up_bwd", 1024, 1024, 1408, deps=[rs_ffn.token])
    dx1, dmix, dg_ffn_pre, dg_mix_post = _post_ffn_bwd(dh2, x1, dy, mix, norm_ffn_pre, norm_mix_post)
    rs_ffn.start_chips([dmix])

    dga, dgb, dya, dyb, d_out_a, d_out_b = _mix_out_bwd(dmix, out_a, out_b, proj, wt_pa, wt_pb, w_o,
                                                        deps=[rs_ffn.token])
    grads_t["w_out"] = _matmul([(merged, dmix)], "tn", F32, "grad_w_out", 1024, 1024, 1024)
    grads_t["w_proj_a"] = _matmul([(dya, out_a)], "tn", F32, "grad_w_proj_a", 1024, DIL_OUT_WIDTH, SEQ)
    grads_t["w_proj_b"] = _matmul([(dyb, out_b)], "tn", F32, "grad_w_proj_b", 1024, FOX_WIDTH, SEQ)

    dq_fox, dk_fox, dv_fox, d_cum = _fox_bwd(qb, kb, vb, d_out_b, lse_fox, f_rows)
    d_cum_rows = jnp.pad(d_cum[:, :2].reshape(N_FOX_HEADS, SEQ), ((0, F_ROWS - N_FOX_HEADS), (0, 0)))
    dz, db_part = _fox_gate_bwd(d_cum_rows, proj, b_pad)

    do_tok, c_tok = _dil_combine_bwd(d_out_a, o_tok, lse_tok)
    dq_dil, dk_dil, dv_dil = _dil_bwd(qa, ka, va, _residue_major(do_tok), lse_dil, _residue_major(c_tok))
    dqa, dka = _dil_qk_bwd(_token_major(dq_dil), _token_major(dk_dil), tables)

    dproj = jnp.concatenate(
        [dga, dgb, jnp.zeros((SEQ, COL_QA - 2 * d), BF16), dqa, dka, _token_major(dv_dil).astype(BF16),
         dq_fox.astype(BF16), dk_fox.astype(BF16), dv_fox.astype(BF16), dz], axis=1)
    dwt_r = _matmul([(dproj, h)], "tn", F32, "grad_w_in", 896, 1024, 1024)
    grads_t["w_in"] = _proj_weight_grad_slots(dwt_r)
    rs_mix = _ReduceScatter("mix", {k: grads_t[k] for k in mix_names}, core, chip)
    grads = rs_ffn.finish([rs_mix.token])
    big = {k: _adam(shards[k], grads[k], moments_m[k], moments_v[k], "adam_" + k) for k in ffn_names}
    rs_mix.start_chips([big[k][0] for k in ffn_names])
    dh = _matmul([(dproj, wt_r)], "nn", F32, "in_proj_bwd", 1024, 1024, 896, deps=[rs_mix.token])
    grad_x, dg_mix_pre = _input_bwd(dh, x2, dx1, norm_mix_pre)

    small_all = _all_gather([dg_mix_pre, dg_mix_post, dg_ffn_pre, dg_ffn_post, db_part], "small_grads_all_gather")
    small = _adam_small(small_all, [norm_mix_pre, norm_mix_post, norm_ffn_pre, norm_ffn_post, b_forget],
                        [m_norm_mix_pre, m_norm_mix_post, m_norm_ffn_pre, m_norm_ffn_post, m_b_forget],
                        [v_norm_mix_pre, v_norm_mix_post, v_norm_ffn_pre, v_norm_ffn_post, v_b_forget],
                        deps=[loss.reshape(1, 1)])

    grads.update(rs_mix.finish([small[0][0], grad_x]))
    big.update({k: _adam(shards[k], grads[k], moments_m[k], moments_v[k], "adam_" + k) for k in mix_names})

    def leaves(i):
        def nat(k):
            a = grads[k] if i == 0 else big[k][i - 1]
            return (a.T if k in col_sharded else a)[None]

        return [nat("w_in"), nat("w_proj_a"), nat("w_proj_b"), nat("w_out"), small[4][i],
                nat("w_ffn_gate"), nat("w_ffn_up"), nat("w_ffn_down"), *[small[r][i] for r in range(4)]]

    return (loss, grad_x[None], *leaves(0), *leaves(1), *leaves(2), *leaves(3))
```

```python
import functools
import math

import jax
import jax.numpy as jnp
from jax import lax
from jax.experimental import pallas as pl
from jax.experimental.pallas import tpu as pltpu

F32 = jnp.float32
BF16 = jnp.bfloat16
MESH = pl.DeviceIdType.MESH

D_MODEL = 1024
SEQ = 2048
HEAD_DIM = 64
BLOCK = 128
N_BLOCKS = SEQ // BLOCK
DILATIONS = (1, 4, 16)
N_FOX_HEADS = 8
DIL_WIDTH = 768
DIL_OUT_WIDTH = 256
FOX_WIDTH = 512
D_FF = 2816
ROPE_THETA = 500000.0
ROPE_DIM = HEAD_DIM // 4
ROPE_HALF = ROPE_DIM // 2
EPS = 1e-6
NEG_INF = -1e30
QK_SCALE = 1.0 / math.sqrt(HEAD_DIM)
IN_COLS = 5896
N_DEV = 8
IN_SHARD = IN_COLS // N_DEV

ADAM_LR = 0.001
ADAM_B1 = 0.9
ADAM_B2 = 0.999
ADAM_EPS = 1e-08
ADAM_WD = 0.01
ADAM_STEP = 10

V7X_VMEM_BYTES = 64 * 2**20
LANES = 128
SUBLANES = 8

PROJ_COLS = 6272
COL_GA, COL_GB = 0, 1024
COL_DIL = 2304
COL_FOX = 4608
COL_F = 6144
F_ROWS = 16


def _vmem_limit(block_bytes):
    want = 2 * block_bytes + 16 * 2**20
    return int(min(max(want, 32 * 2**20), V7X_VMEM_BYTES - 8 * 2**20))


def _nbytes(shape, dtype):
    return math.prod(shape) * jnp.dtype(dtype).itemsize


def _dot(a, b, dims):
    return lax.dot_general(a, b, (dims, ((), ())), preferred_element_type=F32)


def _dot_nn(a, b):
    return _dot(a, b, ((1,), (0,)))


def _dot_nt(a, b):
    return _dot(a, b, ((1,), (1,)))


def _dot_tn(a, b):
    return _dot(a, b, ((0,), (0,)))


def _sigmoid(z):
    return 1.0 / (1.0 + jnp.exp(-z))


def _split3(x):
    hi = x.astype(BF16)
    r1 = x - hi.astype(F32)
    mid = r1.astype(BF16)
    lo = (r1 - mid.astype(F32)).astype(BF16)
    return hi, mid, lo


def _dot3_nn(x, ones_matrix):
    hi, mid, lo = _split3(x)
    return (_dot_nn(hi, ones_matrix) + _dot_nn(mid, ones_matrix)) + _dot_nn(lo, ones_matrix)


def _rowwise(fn, name, n_rows, tm, row_ins, bcast_ins, row_outs, acc_outs=(), deps=()):
    n_in = len(row_ins) + len(bcast_ins)
    n_ro = len(row_outs)

    def body(*refs):
        res = fn(*[r[...] for r in refs[:n_in]])
        if not isinstance(res, (tuple, list)):
            res = (res,)
        outs = refs[n_in + len(deps):]
        for r, o in zip(res[:n_ro], outs[:n_ro]):
            o[...] = r.astype(o.dtype)
        first = pl.program_id(0) == 0
        for r, o in zip(res[n_ro:], outs[n_ro:]):
            _accumulate(o, r, first)

    in_specs = [pl.BlockSpec((tm, w), lambda i, cb=cb: (i, cb)) for _, w, cb in row_ins]
    in_specs += [pl.BlockSpec(a.shape, lambda i: (0, 0)) for a in bcast_ins]
    in_specs += [pl.BlockSpec(memory_space=pl.ANY)] * len(deps)
    out_specs = [pl.BlockSpec((tm, w), lambda i: (i, 0)) for w, _ in row_outs]
    out_specs += [pl.BlockSpec((1, w), lambda i: (0, 0)) for w in acc_outs]
    out_shape = [jax.ShapeDtypeStruct((n_rows, w), dt) for w, dt in row_outs]
    out_shape += [jax.ShapeDtypeStruct((1, w), F32) for w in acc_outs]
    blk = sum(_nbytes((tm, w), a.dtype) for a, w, _ in row_ins) + sum(_nbytes((tm, w), dt) for w, dt in row_outs)
    return pl.pallas_call(
        body, name=name, grid=(n_rows // tm,), in_specs=in_specs, out_specs=out_specs, out_shape=out_shape,
        compiler_params=pltpu.CompilerParams(
            dimension_semantics=("arbitrary" if acc_outs else "parallel",), vmem_limit_bytes=_vmem_limit(3 * blk)),
    )(*[a for a, _, _ in row_ins], *bcast_ins, *deps)


def _accumulate(o_ref, part, first):
    @pl.when(first)
    def _():
        o_ref[...] = part

    @pl.when(jnp.logical_not(first))
    def _():
        o_ref[...] += part


_MM_DIMS = {"nn": ((1,), (0,)), "nt": ((1,), (1,)), "tn": ((0,), (0,))}


def _matmul(pairs, mode, out_dtype, name, tm, tn, tk, deps=()):
    a0, b0 = pairs[0]
    if mode == "tn":
        kk, m = a0.shape
    else:
        m, kk = a0.shape
    n = b0.shape[0] if mode == "nt" else b0.shape[1]
    assert m % tm == 0 and n % tn == 0 and kk % tk == 0, (name, m, n, kk)
    nk = kk // tk
    n_pairs = len(pairs)
    dims = _MM_DIMS[mode]
    n_in = 2 * n_pairs + len(deps)

    def body(*refs):
        o_ref = refs[n_in]
        part = None
        for p in range(n_pairs):
            d = _dot(refs[2 * p][...].astype(BF16), refs[2 * p + 1][...].astype(BF16), dims)
            part = d if part is None else part + d
        if nk == 1:
            o_ref[...] = part.astype(o_ref.dtype)
            return
        acc = refs[n_in + 1]
        k = pl.program_id(2)

        @pl.when(k == 0)
        def _():
            acc[...] = part

        @pl.when(k > 0)
        def _():
            acc[...] += part

        @pl.when(k == nk - 1)
        def _():
            o_ref[...] = acc[...].astype(o_ref.dtype)

    if mode == "tn":
        a_spec = pl.BlockSpec((tk, tm), lambda i, j, k: (k, i))
    else:
        a_spec = pl.BlockSpec((tm, tk), lambda i, j, k: (i, k))
    if mode == "nt":
        b_spec = pl.BlockSpec((tn, tk), lambda i, j, k: (j, k))
    else:
        b_spec = pl.BlockSpec((tk, tn), lambda i, j, k: (k, j))
    blk = sum(_nbytes((tm, tk), a.dtype) + _nbytes((tk, tn), b.dtype) for a, b in pairs) + 2 * _nbytes((tm, tn), F32)
    flat = [a for pair in pairs for a in pair]
    return pl.pallas_call(
        body, name=name, grid=(m // tm, n // tn, nk),
        in_specs=[a_spec, b_spec] * n_pairs + [pl.BlockSpec(memory_space=pl.ANY)] * len(deps),
        out_specs=pl.BlockSpec((tm, tn), lambda i, j, k: (i, j)),
        out_shape=jax.ShapeDtypeStruct((m, n), out_dtype),
        scratch_shapes=[] if nk == 1 else [pltpu.VMEM((tm, tn), F32)],
        compiler_params=pltpu.CompilerParams(
            dimension_semantics=("parallel", "parallel", "arbitrary"), vmem_limit_bytes=_vmem_limit(blk)),
    )(*flat, *deps)


def _rms_scale(x):
    return lax.rsqrt(jnp.mean(x * x, axis=-1, keepdims=True) + EPS)


def _rms_bwd(xin, dyn, g):
    r = _rms_scale(xin)
    u = dyn * g
    dx = r * u - xin * (r * r * r) * jnp.mean(u * xin, axis=-1, keepdims=True)
    dg = jnp.sum(dyn * xin * r, axis=0, keepdims=True)
    return dx, dg


def _mesh_pos():
    return lax.axis_index("x"), lax.axis_index("y"), lax.axis_index("c")


def _all_gather(xs, name):
    n = len(xs)

    def body(*refs):
        x_refs, out_refs = refs[:n], refs[n:2 * n]
        send_sems, recv_sems, local_sems = refs[2 * n:]
        mx, my, mc = _mesh_pos()
        me, sib = (mx, my, mc), (mx, my, 1 - mc)
        chips = [(1 - mx, my), (mx, 1 - my), (1 - mx, 1 - my)]

        def slot(a, dev):
            px, py, pc = dev
            return out_refs[a].at[4 * px + 2 * py + pc]

        def copy(k, a, block, to, src=None):
            return pltpu.make_async_remote_copy(
                src_ref=slot(a, block) if src is None else src, dst_ref=slot(a, block),
                send_sem=send_sems.at[a * 7 + k], recv_sem=recv_sems.at[a * 7 + k],
                device_id=to, device_id_type=MESH)

        mine = [pltpu.make_async_copy(x_refs[a], slot(a, me), local_sems.at[a]) for a in range(n)]
        for cp in mine:
            cp.start()
        first = []
        for a in range(n):
            first.append(copy(0, a, me, sib, x_refs[a]))
            first += [copy(1 + j, a, me, (*chip, mc), x_refs[a]) for j, chip in enumerate(chips)]
        for cp in first:
            cp.start()
        passed = []
        for a in range(n):
            for j, chip in enumerate(chips):
                copy(1 + j, a, (*chip, mc), me).wait_recv()
                fwd = copy(4 + j, a, (*chip, mc), sib)
                fwd.start()
                passed.append(fwd)
        for a in range(n):
            copy(0, a, sib, me).wait_recv()
            for j, chip in enumerate(chips):
                copy(4 + j, a, (*chip, 1 - mc), me).wait_recv()
        for cp in first + passed:
            cp.wait_send()
        for cp in mine:
            cp.wait()

    hbm = pl.BlockSpec(memory_space=pl.ANY)
    return pl.pallas_call(
        body, name=name,
        out_shape=[jax.ShapeDtypeStruct((N_DEV,) + x.shape, x.dtype) for x in xs],
        in_specs=[hbm] * n, out_specs=[hbm] * n,
        scratch_shapes=[pltpu.SemaphoreType.DMA((7 * n,)), pltpu.SemaphoreType.DMA((7 * n,)),
                        pltpu.SemaphoreType.DMA((n,))],
    )(*xs)


_HBM = pl.BlockSpec(memory_space=pltpu.HBM)
_SEM = pl.BlockSpec(memory_space=pltpu.SEMAPHORE)
_ANY = pl.BlockSpec(memory_space=pl.ANY)
_DATAFLOW = pltpu.SideEffectType.DATAFLOW_SIDE_EFFECTING


def _flip_peer(flip):
    mx, my, mc = _mesh_pos()
    return (1 - mx if flip & 2 else mx, 1 - my if flip & 1 else my, mc)


def _remote(src, dst, send_sems, recv_sems, k, peer):
    return pltpu.make_async_remote_copy(src_ref=src, dst_ref=dst, send_sem=send_sems.at[k], recv_sem=recv_sems.at[k],
                                        device_id=peer, device_id_type=MESH)


def _gather_chips_copies(srcs, lands, send_sems, recv_sems):
    mx, my, mc = _mesh_pos()
    me = 4 * mx + 2 * my + mc
    return [_remote(srcs[a], lands[a].at[me], send_sems, recv_sems, 3 * a + flip - 1, _flip_peer(flip))
            for a in range(len(srcs)) for flip in (1, 2, 3)]


def _gather_sibling_copies(srcs, lands, send_sems, recv_sems):
    mx, my, mc = _mesh_pos()
    return [_remote(lands[a].at[2 * k + mc], lands[a].at[2 * k + mc], send_sems, recv_sems, 4 * a + k, (mx, my, 1 - mc))
            for a in range(len(lands)) for k in range(4)]


def _scatter_sibling_copies(srcs, lands, send_sems, recv_sems):
    mx, my, mc = _mesh_pos()
    return [_remote(srcs[a].at[k, 1 - mc], lands[a].at[k], send_sems, recv_sems, 4 * a + k, (mx, my, 1 - mc))
            for a in range(len(srcs)) for k in range(4)]


def _scatter_chips_copies(srcs, lands, send_sems, recv_sems):
    mx, my, _ = _mesh_pos()
    k0 = 2 * mx + my
    return [_remote(srcs[a].at[jnp.bitwise_xor(k0, flip)], lands[a].at[flip - 1], send_sems, recv_sems,
                    3 * a + flip - 1, _flip_peer(flip))
            for a in range(len(srcs)) for flip in (1, 2, 3)]


class _Exchange:
    def __init__(self, copies, n_src, send_sems, recv_sems, thru, token):
        self.copies, self.n_src, self.send_sems, self.recv_sems, self.thru, self.token = (
            copies, n_src, send_sems, recv_sems, thru, token)


def _exchange_start(name, copies, srcs, lands, n_copies, after=()):
    bufs = list(srcs) + list(lands)
    nb, ns = len(bufs), len(srcs)

    def body(*refs):
        send_sems, recv_sems = refs[nb + len(after)], refs[nb + len(after) + 1]
        for cp in copies(refs[:ns], refs[ns:nb], send_sems, recv_sems):
            cp.start()
        refs[-1][...] = jnp.zeros_like(refs[-1])

    out = pl.pallas_call(
        body, name=name,
        out_shape=(pltpu.SemaphoreType.DMA((n_copies,)), pltpu.SemaphoreType.DMA((n_copies,)),
                   *[pltpu.HBM(b.shape, b.dtype) for b in bufs], jax.ShapeDtypeStruct((SUBLANES, LANES), F32)),
        in_specs=[_HBM] * nb + [_ANY] * len(after),
        out_specs=(_SEM, _SEM, *[_HBM] * nb, pl.BlockSpec(memory_space=pltpu.VMEM)),
        input_output_aliases={i: 2 + i for i in range(nb)},
        compiler_params=pltpu.CompilerParams(has_side_effects=_DATAFLOW),
    )(*[pltpu.with_memory_space_constraint(b, pltpu.HBM) for b in bufs], *after)
    return _Exchange(copies, ns, out[0], out[1], list(out[2:2 + nb]), out[-1])


def _exchange_wait(name, ex, after):
    nb, ns = len(ex.thru), ex.n_src

    def body(*refs):
        for cp in ex.copies(refs[:ns], refs[ns:nb], refs[nb], refs[nb + 1]):
            cp.wait_send()
            cp.wait_recv()

    out = pl.pallas_call(
        body, name=name, out_shape=tuple(pltpu.HBM(b.shape, b.dtype) for b in ex.thru),
        in_specs=[_HBM] * nb + [_SEM, _SEM] + [_ANY] * len(after), out_specs=tuple([_HBM] * nb),
        input_output_aliases={i: i for i in range(nb)},
        compiler_params=pltpu.CompilerParams(has_side_effects=_DATAFLOW),
    )(*ex.thru, ex.send_sems, ex.recv_sems, *after)
    return list(out[:ns]), list(out[ns:])


def _col_tile(r, c):
    return next(t for t in (1024, 512, 256, 128) if c % t == 0 and (r * t * 4 <= 2**20 or t == 128))


def _add_sibling(g4, recv, core, name):
    _, _, r, c = g4.shape
    tc = _col_tile(r, c)

    def body(core_ref, g_ref, r_ref, o16_ref, o32_ref):
        s = g_ref[0, 0] + r_ref[0]
        o16_ref[0] = s.astype(BF16)
        o32_ref[0] = s

    out = pl.BlockSpec((1, r, tc), lambda k, j, core_ref: (k, 0, j))
    return pl.pallas_call(
        body, name=name,
        out_shape=[jax.ShapeDtypeStruct((4, r, c), BF16), jax.ShapeDtypeStruct((4, r, c), F32)],
        grid_spec=pltpu.PrefetchScalarGridSpec(
            num_scalar_prefetch=1, grid=(4, c // tc),
            in_specs=[pl.BlockSpec((1, 1, r, tc), lambda k, j, core_ref: (k, core_ref[0], 0, j)), out],
            out_specs=[out, out]),
        compiler_params=pltpu.CompilerParams(dimension_semantics=("parallel", "parallel")),
    )(core, g4, recv)


def _add_chips(p32, recv, chip, name):
    _, r, c = p32.shape
    tc = _col_tile(r, c)

    def body(chip_ref, p_ref, r_ref, o_ref):
        o_ref[...] = ((p_ref[0] + r_ref[0].astype(F32)) + r_ref[1].astype(F32)) + r_ref[2].astype(F32)

    return pl.pallas_call(
        body, name=name, out_shape=jax.ShapeDtypeStruct((r, c), F32),
        grid_spec=pltpu.PrefetchScalarGridSpec(
            num_scalar_prefetch=1, grid=(c // tc,),
            in_specs=[pl.BlockSpec((1, r, tc), lambda j, chip_ref: (chip_ref[0], 0, j)),
                      pl.BlockSpec((3, r, tc), lambda j, chip_ref: (0, 0, j))],
            out_specs=pl.BlockSpec((r, tc), lambda j, chip_ref: (0, j))),
        compiler_params=pltpu.CompilerParams(dimension_semantics=("parallel",)),
    )(chip, p32, recv)


class _ReduceScatter:
    def __init__(self, tag, grads_t, core, chip):
        self.tag, self.core, self.chip, self.names = tag, core, chip, list(grads_t)
        g4s = [g.reshape(4, 2, g.size // (N_DEV * g.shape[-1]), g.shape[-1]) for g in grads_t.values()]
        lands = [lax.empty((4,) + g.shape[2:], F32) for g in g4s]
        self.ex = _exchange_start(f"rs_{tag}_sibling_start", _scatter_sibling_copies, g4s, lands, 4 * len(g4s))
        self.token = self.ex.token

    def start_chips(self, after):
        g4s, from_sibling = _exchange_wait(f"rs_{self.tag}_sibling_wait", self.ex, after)
        parts = [_add_sibling(g4, rv, self.core, f"rs_add_sibling_{k}")
                 for k, g4, rv in zip(self.names, g4s, from_sibling)]
        self.p32s = [p32 for _, p32 in parts]
        p16s = [p16 for p16, _ in parts]
        lands = [lax.empty((3,) + p.shape[1:], BF16) for p in p16s]
        self.ex = _exchange_start(f"rs_{self.tag}_chips_start", _scatter_chips_copies, p16s, lands, 3 * len(p16s))
        self.token = self.ex.token

    def finish(self, after):
        _, from_chips = _exchange_wait(f"rs_{self.tag}_chips_wait", self.ex, after)
        return {k: _add_chips(p32, rv, self.chip, f"rs_add_chips_{k}")
                for k, p32, rv in zip(self.names, self.p32s, from_chips)}


def _rope_tables():
    positions = jnp.arange(SEQ, dtype=F32)
    inv_freq = jnp.power(ROPE_THETA, -jnp.arange(0, ROPE_DIM, 2, dtype=F32) / ROPE_DIM)
    ang = positions[:, None] * inv_freq[None, :]
    cos, sin = jnp.cos(ang), jnp.sin(ang)
    ones = jnp.ones((SEQ, HEAD_DIM - ROPE_DIM), F32)
    zeros8 = jnp.zeros((SEQ, ROPE_HALF), F32)
    zeros = jnp.zeros((SEQ, HEAD_DIM - ROPE_DIM), F32)
    c_head = jnp.concatenate([cos, cos, ones], axis=1)
    s1_head = jnp.concatenate([-sin, zeros8, zeros], axis=1)
    s2_head = jnp.concatenate([zeros8, sin, zeros], axis=1)
    return tuple(jnp.concatenate([t, t], axis=1) for t in (c_head, s1_head, s2_head))


def _rope_apply(x, c, s1, s2):
    w = x.shape[1]
    return x * c + pltpu.roll(x, w - ROPE_HALF, 1) * s1 + pltpu.roll(x, ROPE_HALF, 1) * s2


def _rope_apply_t(dy, c, s1, s2):
    w = dy.shape[1]
    return dy * c + pltpu.roll(dy * s1, ROPE_HALF, 1) + pltpu.roll(dy * s2, w - ROPE_HALF, 1)


def _dil_prev_limit(has_prev):
    return jnp.where(has_prev, 0, BLOCK)


def _dil_valid(limit):
    row = lax.broadcasted_iota(jnp.int32, (BLOCK, 2 * BLOCK), 0)
    col = lax.broadcasted_iota(jnp.int32, (BLOCK, 2 * BLOCK), 1)
    dist = col - row
    return jnp.logical_and(dist >= jnp.where(col < BLOCK, limit, -BLOCK), dist <= BLOCK)


def _upper_half():
    return lax.broadcasted_iota(jnp.int32, (1, LANES), 1) >= HEAD_DIM


def _dil_rows(n, d):
    per = N_BLOCKS // d
    r, lb = n // per, n % per

    def rows(b):
        start = b * (BLOCK * d) + r
        return pl.ds(pl.multiple_of(start, BLOCK), BLOCK) if d == 1 else pl.ds(start, BLOCK, stride=d)

    return rows(lb), rows(jnp.maximum(lb - 1, 0)), lb > 0


def _dil_specs(g):
    base = COL_DIL // LANES + 6 * g

    def col(j):
        return pl.BlockSpec((SEQ, LANES), lambda p, j=j: (0, base + 3 * p + j))

    table = pl.BlockSpec((SEQ, LANES), lambda p: (0, 0))
    return [col(0), col(1), col(2)], [table] * 3


def _dil_fwd(g, proj, tables):
    d = DILATIONS[g]
    one_block = d == N_BLOCKS

    def body(q_ref, k_ref, v_ref, c_ref, s1_ref, s2_ref, o_ref, lse_ref):
        upper = _upper_half()

        def roped(ref, rows):
            return _rope_apply(ref[rows, :], c_ref[rows, :], s1_ref[rows, :], s2_ref[rows, :])

        def block(n, carry):
            rows, prev, has_prev = _dil_rows(n, d)
            qb = (roped(q_ref, rows) * QK_SCALE).astype(BF16)
            kw, vw = roped(k_ref, rows).astype(BF16), v_ref[rows, :].astype(BF16)
            if one_block:
                row = lax.broadcasted_iota(jnp.int32, (BLOCK, BLOCK), 0)
                valid = lax.broadcasted_iota(jnp.int32, (BLOCK, BLOCK), 1) <= row
            else:
                kw = jnp.concatenate([roped(k_ref, prev).astype(BF16), kw], axis=0)
                vw = jnp.concatenate([v_ref[prev, :].astype(BF16), vw], axis=0)
                valid = _dil_valid(_dil_prev_limit(has_prev))
            outs, lses = [], []
            for head_mask in (jnp.logical_not(upper), upper):
                s = jnp.where(valid, _dot_nt(qb, jnp.where(head_mask, kw, 0)), NEG_INF)
                m = jnp.max(s, axis=-1, keepdims=True)
                p = jnp.exp(s - m)
                den = jnp.sum(p, axis=-1, keepdims=True)
                outs.append(_dot_nn((p / den).astype(BF16), jnp.where(head_mask, vw, 0)))
                lses.append(m + jnp.log(den))
            o_ref[rows, :] = outs[0] + outs[1]
            lse_ref[rows, :] = jnp.where(upper, lses[1], lses[0])
            return carry

        lax.fori_loop(0, N_BLOCKS, block, 0, unroll=2)

    qkv, tabs = _dil_specs(g)
    out = pl.BlockSpec((SEQ, LANES), lambda p: (0, p))
    return pl.pallas_call(
        body, name=f"dil_attn_fwd_{g}", grid=(2,), in_specs=qkv + tabs, out_specs=[out, out],
        out_shape=[jax.ShapeDtypeStruct((SEQ, DIL_OUT_WIDTH), F32)] * 2,
        compiler_params=pltpu.CompilerParams(dimension_semantics=("parallel",)),
    )(proj, proj, proj, *tables)


def _dil_bwd(g, proj, tables, do, lse, c, dproj):
    d = DILATIONS[g]
    one_block = d == N_BLOCKS

    def body(q_ref, k_ref, v_ref, c_ref, s1_ref, s2_ref, do_ref, lse_ref, cc_ref, dproj_ref, out_ref,
             dq_acc, dk_acc, dv_acc):
        upper = _upper_half()
        dk_acc[...] = jnp.zeros_like(dk_acc)
        dv_acc[...] = jnp.zeros_like(dv_acc)

        def roped(ref, rows):
            return _rope_apply(ref[rows, :], c_ref[rows, :], s1_ref[rows, :], s2_ref[rows, :])

        def block(n, carry):
            rows, prev, has_prev = _dil_rows(n, d)
            qb = (roped(q_ref, rows) * QK_SCALE).astype(BF16)
            dob = do_ref[rows, :].astype(BF16)
            kw, vw = roped(k_ref, rows).astype(BF16), v_ref[rows, :].astype(BF16)
            if one_block:
                row = lax.broadcasted_iota(jnp.int32, (BLOCK, BLOCK), 0)
                valid = lax.broadcasted_iota(jnp.int32, (BLOCK, BLOCK), 1) <= row
            else:
                kw = jnp.concatenate([roped(k_ref, prev).astype(BF16), kw], axis=0)
                vw = jnp.concatenate([v_ref[prev, :].astype(BF16), vw], axis=0)
                valid = _dil_valid(_dil_prev_limit(has_prev))
            lse_t, c_t = lse_ref[rows, :], cc_ref[rows, :]
            dq, dk, dv = None, None, None
            for e, head_mask in enumerate((jnp.logical_not(upper), upper)):
                km, vm = jnp.where(head_mask, kw, 0), jnp.where(head_mask, vw, 0)
                lse_col = lse_t[:, e * HEAD_DIM:e * HEAD_DIM + 1]
                c_col = c_t[:, e * HEAD_DIM:e * HEAD_DIM + 1]
                p = jnp.where(valid, jnp.exp(_dot_nt(qb, km) - lse_col), 0.0)
                ds = (p * (_dot_nt(dob, vm) + c_col)).astype(BF16)
                parts = (_dot_nn(ds, km), _dot_tn(ds, jnp.where(head_mask, qb, 0)),
                         _dot_tn(p.astype(BF16), jnp.where(head_mask, dob, 0)))
                dq, dk, dv = parts if dq is None else (dq + parts[0], dk + parts[1], dv + parts[2])
            dq_acc[rows, :] = dq * QK_SCALE
            if one_block:
                dk_acc[rows, :] += dk
                dv_acc[rows, :] += dv
            else:
                dk_acc[prev, :] += dk[:BLOCK]
                dv_acc[prev, :] += dv[:BLOCK]
                dk_acc[rows, :] += dk[BLOCK:]
                dv_acc[rows, :] += dv[BLOCK:]
            return carry

        lax.fori_loop(0, N_BLOCKS, block, 0)
        tabs = (c_ref[...], s1_ref[...], s2_ref[...])
        out_ref[:, 0:LANES] = _rope_apply_t(dq_acc[...], *tabs).astype(BF16)
        out_ref[:, LANES:2 * LANES] = _rope_apply_t(dk_acc[...], *tabs).astype(BF16)
        out_ref[:, 2 * LANES:] = dv_acc[...].astype(BF16)

    qkv, tabs = _dil_specs(g)
    tok = pl.BlockSpec((SEQ, LANES), lambda p: (0, p))
    base = COL_DIL // (3 * LANES) + 2 * g
    return pl.pallas_call(
        body, name=f"dil_attn_bwd_{g}", grid=(2,),
        in_specs=qkv + tabs + [tok, tok, tok, _ANY],
        out_specs=pl.BlockSpec((SEQ, 3 * LANES), lambda p: (0, base + p)),
        out_shape=jax.ShapeDtypeStruct(dproj.shape, dproj.dtype),
        scratch_shapes=[pltpu.VMEM((SEQ, LANES), F32)] * 3,
        input_output_aliases={9: 0},
        compiler_params=pltpu.CompilerParams(dimension_semantics=("arbitrary",)),
    )(proj, proj, proj, *tables, do, lse, c, dproj)


def _group_weights(l0, l1, l2):
    m = jnp.maximum(jnp.maximum(l0, l1), l2)
    e0, e1, e2 = jnp.exp(l0 - m), jnp.exp(l1 - m), jnp.exp(l2 - m)
    tot = e0 + e1 + e2
    return e0 / tot, e1 / tot, e2 / tot


def _dil_combine(outs, lses, deps=()):
    def fn(o0, o1, o2, l0, l1, l2):
        w0, w1, w2 = _group_weights(l0, l1, l2)
        return w0 * o0 + w1 * o1 + w2 * o2

    w = DIL_OUT_WIDTH
    return _rowwise(fn, "dil_combine", SEQ, 512, [(a, w, 0) for a in list(outs) + list(lses)], [], [(w, F32)],
                    deps=deps)[0]


def _dil_combine_bwd(d_out, outs, lses):
    w = DIL_OUT_WIDTH

    def fn(d, o0, o1, o2, l0, l1, l2):
        row = lax.broadcasted_iota(jnp.int32, (w, w), 0) // HEAD_DIM
        col = lax.broadcasted_iota(jnp.int32, (w, w), 1) // HEAD_DIM
        same_head = jnp.where(row == col, 1.0, 0.0).astype(BF16)
        ws = _group_weights(l0, l1, l2)
        dws = [_dot3_nn(d * og, same_head) for og in (o0, o1, o2)]
        mean = ws[0] * dws[0] + ws[1] * dws[1] + ws[2] * dws[2]
        return tuple(wg * d for wg in ws) + tuple(-wg * mean for wg in ws)

    res = _rowwise(fn, "dil_combine_bwd", SEQ, 256, [(a, w, 0) for a in [d_out] + list(outs) + list(lses)], [],
                   [(w, F32)] * 6)
    return res[:3], res[3:]


def _log1p(e):
    u = 1.0 + e
    return jnp.where(u == 1.0, e, jnp.log(u) * (e / (u - 1.0)))


def _fox_gate(proj, b_pad, deps=()):
    def body(f_ref, b_ref, *rest):
        o_ref = rest[-1]
        z = f_ref[...] + b_ref[...]
        logf = (jnp.minimum(z, 0.0) - _log1p(jnp.exp(-jnp.abs(z)))).T[:F_ROWS]
        row = lax.broadcasted_iota(jnp.int32, (BLOCK, BLOCK), 0)
        col = lax.broadcasted_iota(jnp.int32, (BLOCK, BLOCK), 1)
        before = jnp.where(row <= col, 1.0, 0.0).astype(BF16)
        carry = jnp.zeros((F_ROWS, 1), F32)
        for blk in range(N_BLOCKS):
            run = _dot3_nn(logf[:, blk * BLOCK:(blk + 1) * BLOCK], before) + carry
            o_ref[:, blk * BLOCK:(blk + 1) * BLOCK] = run
            carry = run[:, BLOCK - 1:BLOCK]

    return pl.pallas_call(
        body, name="fox_gate", grid=(1,),
        in_specs=[pl.BlockSpec((SEQ, LANES), lambda i: (0, COL_F // LANES)), pl.BlockSpec((1, LANES), lambda i: (0, 0))]
        + [_ANY] * len(deps),
        out_specs=pl.BlockSpec((F_ROWS, SEQ), lambda i: (0, 0)),
        out_shape=jax.ShapeDtypeStruct((F_ROWS, SEQ), F32),
    )(proj, b_pad, *deps)


def _fox_gate_bwd(d_cum, proj, b_pad, dproj):
    def body(d_ref, f_ref, b_ref, dproj_ref, dz_ref, db_ref):
        row = lax.broadcasted_iota(jnp.int32, (BLOCK, BLOCK), 0)
        col = lax.broadcasted_iota(jnp.int32, (BLOCK, BLOCK), 1)
        after = jnp.where(row >= col, 1.0, 0.0).astype(BF16)
        carry = jnp.zeros((F_ROWS, 1), F32)
        parts = [None] * N_BLOCKS
        for blk in reversed(range(N_BLOCKS)):
            run = _dot3_nn(d_ref[:, blk * BLOCK:(blk + 1) * BLOCK], after) + carry
            parts[blk] = run
            carry = run[:, 0:1]
        dlogf = jnp.concatenate(parts, axis=1)
        dlogf = jnp.concatenate([dlogf, jnp.zeros((LANES - F_ROWS, SEQ), F32)], axis=0).T
        dz = dlogf * _sigmoid(-(f_ref[...] + b_ref[...]))
        dz_ref[...] = dz.astype(BF16)
        db_ref[...] = jnp.sum(dz, axis=0, keepdims=True)

    f_cols = pl.BlockSpec((SEQ, LANES), lambda i: (0, COL_F // LANES))
    return pl.pallas_call(
        body, name="fox_gate_bwd", grid=(1,),
        in_specs=[pl.BlockSpec((F_ROWS, SEQ), lambda i: (0, 0)), f_cols, pl.BlockSpec((1, LANES), lambda i: (0, 0)), _ANY],
        out_specs=[f_cols, pl.BlockSpec((1, LANES), lambda i: (0, 0))],
        out_shape=[jax.ShapeDtypeStruct(dproj.shape, dproj.dtype), jax.ShapeDtypeStruct((1, LANES), F32)],
        input_output_aliases={3: 0},
    )(d_cum, proj, b_pad, dproj)


FOX_TILE = 256
FOX_TILES = SEQ // FOX_TILE


def _row_to_col(row):
    n = row.shape[1]
    eye = lax.broadcasted_iota(jnp.int32, (n, n), 0) == lax.broadcasted_iota(jnp.int32, (n, n), 1)
    return jnp.sum(jnp.where(eye, row, 0.0), axis=1, keepdims=True)


def _fox_scores(q_tile, km, f_row, i):
    t = FOX_TILE
    ext = (i + 1) * t
    f_q = _row_to_col(f_row[:, i * t:(i + 1) * t])
    s = _dot_nt(q_tile, km[:ext]) + (f_q - f_row[:, :ext])
    row = lax.broadcasted_iota(jnp.int32, (t, ext), 0) + i * t
    col = lax.broadcasted_iota(jnp.int32, (t, ext), 1)
    return s, col <= row


def _fox_specs():
    base = COL_FOX // LANES
    qkv = [pl.BlockSpec((SEQ, LANES), lambda p, j=j: (0, base + 3 * p + j)) for j in range(3)]
    return qkv, pl.BlockSpec((F_ROWS, SEQ), lambda p: (0, 0))


def _fox_fwd(proj, f_rows):
    t = FOX_TILE

    def body(q_ref, k_ref, v_ref, f_ref, o_ref, lse_ref):
        pair = pl.program_id(0)
        upper = _upper_half()
        masks = (jnp.logical_not(upper), upper)
        k16, v16 = k_ref[...].astype(BF16), v_ref[...].astype(BF16)
        kms = [jnp.where(hm, k16, 0) for hm in masks]
        vms = [jnp.where(hm, v16, 0) for hm in masks]
        f_row = [f_ref[pl.ds(2 * pair + e, 1), :] for e in range(2)]
        for i in range(FOX_TILES):
            q_tile = (q_ref[i * t:(i + 1) * t, :] * QK_SCALE).astype(BF16)
            outs, lses = [], []
            for e in range(2):
                s, causal = _fox_scores(q_tile, kms[e], f_row[e], i)
                s = jnp.where(causal, s, NEG_INF)
                m = jnp.max(s, axis=-1, keepdims=True)
                p = jnp.exp(s - m)
                den = jnp.sum(p, axis=-1, keepdims=True)
                outs.append(_dot_nn((p / den).astype(BF16), vms[e][:(i + 1) * t]))
                lses.append(m + jnp.log(den))
            o_ref[i * t:(i + 1) * t, :] = outs[0] + outs[1]
            lse_ref[i * t:(i + 1) * t, :] = jnp.where(upper, lses[1], lses[0])

    qkv, f_spec = _fox_specs()
    tok = pl.BlockSpec((SEQ, LANES), lambda p: (0, p))
    return pl.pallas_call(
        body, name="fox_attn_fwd", grid=(FOX_WIDTH // LANES,),
        in_specs=qkv + [f_spec], out_specs=[tok, tok],
        out_shape=[jax.ShapeDtypeStruct((SEQ, FOX_WIDTH), F32)] * 2,
        compiler_params=pltpu.CompilerParams(
            dimension_semantics=("parallel",), vmem_limit_bytes=_vmem_limit(8 * t * SEQ * 4)),
    )(proj, proj, proj, f_rows)


def _fox_bwd(proj, do, lse, f_rows, dproj):
    t = FOX_TILE

    def body(q_ref, k_ref, v_ref, f_ref, do_ref, lse_ref, dproj_ref, out_ref, df_ref, dk_acc, dv_acc):
        pair = pl.program_id(0)
        upper = _upper_half()
        masks = (jnp.logical_not(upper), upper)
        k16, v16 = k_ref[...].astype(BF16), v_ref[...].astype(BF16)
        kms = [jnp.where(hm, k16, 0) for hm in masks]
        vms = [jnp.where(hm, v16, 0) for hm in masks]
        f_row = [f_ref[pl.ds(2 * pair + e, 1), :] for e in range(2)]
        dk_acc[...] = jnp.zeros_like(dk_acc)
        dv_acc[...] = jnp.zeros_like(dv_acc)
        df_ref[...] = jnp.zeros_like(df_ref)
        for i in range(FOX_TILES):
            ext = (i + 1) * t
            q_tile = (q_ref[i * t:(i + 1) * t, :] * QK_SCALE).astype(BF16)
            do_tile = do_ref[i * t:(i + 1) * t, :]
            lse_t = lse_ref[i * t:(i + 1) * t, :]
            dq = None
            for e in range(2):
                s, causal = _fox_scores(q_tile, kms[e], f_row[e], i)
                p = jnp.where(causal, jnp.exp(s - lse_t[:, e * HEAD_DIM:e * HEAD_DIM + 1]), 0.0)
                dp = _dot_nt(do_tile, vms[e][:ext])
                ds = p * (dp - jnp.sum(p * dp, axis=-1, keepdims=True))
                df_ref[0, e:e + 1, :ext] -= jnp.sum(ds, axis=0, keepdims=True)
                ds = ds.astype(BF16)
                part = _dot_nn(ds, kms[e][:ext])
                dq = part if dq is None else dq + part
                dk_acc[:ext, :] += _dot_tn(ds, jnp.where(masks[e], q_tile, 0))
                dv_acc[:ext, :] += _dot_tn(p.astype(BF16), jnp.where(masks[e], do_tile, 0))
            out_ref[i * t:(i + 1) * t, 0:LANES] = (dq * QK_SCALE).astype(BF16)
        out_ref[:, LANES:2 * LANES] = dk_acc[...].astype(BF16)
        out_ref[:, 2 * LANES:] = dv_acc[...].astype(BF16)

    qkv, f_spec = _fox_specs()
    tok = pl.BlockSpec((SEQ, LANES), lambda p: (0, p))
    return pl.pallas_call(
        body, name="fox_attn_bwd", grid=(FOX_WIDTH // LANES,),
        in_specs=qkv + [f_spec, tok, tok, _ANY],
        out_specs=[pl.BlockSpec((SEQ, 3 * LANES), lambda p: (0, COL_FOX // (3 * LANES) + p)),
                   pl.BlockSpec((1, SUBLANES, SEQ), lambda p: (p, 0, 0))],
        out_shape=[jax.ShapeDtypeStruct(dproj.shape, dproj.dtype),
                   jax.ShapeDtypeStruct((FOX_WIDTH // LANES, SUBLANES, SEQ), F32)],
        scratch_shapes=[pltpu.VMEM((SEQ, LANES), F32)] * 2,
        input_output_aliases={6: 0},
        compiler_params=pltpu.CompilerParams(
            dimension_semantics=("arbitrary",), vmem_limit_bytes=_vmem_limit(10 * t * SEQ * 4)),
    )(proj, proj, proj, f_rows, do, lse, dproj)


MIX_TILE = 256


def _mix_out(out_a, out_b, proj, x, wt_pa, wt_pb, w_out, g_post, g_ffn_pre):
    tm = MIX_TILE

    def body(a_ref, b_ref, ga_ref, gb_ref, x_ref, wpa_ref, wpb_ref, wo_ref, g2_ref, g3_ref,
             merged_ref, mix_ref, x1_ref, h2_ref):
        ya = _dot_nt(a_ref[...].astype(BF16), wpa_ref[...])
        yb = _dot_nt(b_ref[...].astype(BF16), wpb_ref[...])
        merged = (_sigmoid(ga_ref[...]) * ya + _sigmoid(gb_ref[...]) * yb).astype(BF16)
        merged_ref[...] = merged
        mix = _dot_nn(merged, wo_ref[...])
        mix_ref[...] = mix
        x1 = x_ref[...] + mix * _rms_scale(mix) * g2_ref[...]
        x1_ref[...] = x1
        h2_ref[...] = (x1 * _rms_scale(x1) * g3_ref[...]).astype(BF16)

    def rows(w, cb=0):
        return pl.BlockSpec((tm, w), lambda i, cb=cb: (i, cb))

    def whole(a):
        return pl.BlockSpec(a.shape, lambda i: (0, 0))

    d = D_MODEL
    blk = _nbytes((tm, d), F32) * 6 + sum(_nbytes(a.shape, BF16) for a in (wt_pa, wt_pb, w_out))
    return pl.pallas_call(
        body, name="mix_out", grid=(SEQ // tm,),
        in_specs=[rows(DIL_OUT_WIDTH), rows(FOX_WIDTH), rows(d, COL_GA // d), rows(d, COL_GB // d), rows(d),
                  whole(wt_pa), whole(wt_pb), whole(w_out), whole(g_post), whole(g_ffn_pre)],
        out_specs=[rows(d)] * 4,
        out_shape=[jax.ShapeDtypeStruct((SEQ, d), dt) for dt in (BF16, F32, F32, BF16)],
        compiler_params=pltpu.CompilerParams(dimension_semantics=("parallel",), vmem_limit_bytes=_vmem_limit(blk)),
    )(out_a, out_b, proj, proj, x, wt_pa, wt_pb, w_out, g_post, g_ffn_pre)


def _mix_out_bwd(dmix, out_a, out_b, proj, wt_pa, wt_pb, w_out, deps=()):
    tm = MIX_TILE

    def body(dm_ref, a_ref, b_ref, ga_ref, gb_ref, wpa_ref, wpb_ref, wo_ref, *rest):
        dproj_ref, dya_ref, dyb_ref, da_ref, db_ref = rest[len(deps):]
        dmerged = _dot_nt(dm_ref[...], wo_ref[...])
        ya = _dot_nt(a_ref[...].astype(BF16), wpa_ref[...])
        yb = _dot_nt(b_ref[...].astype(BF16), wpb_ref[...])
        sa, sb = _sigmoid(ga_ref[...]), _sigmoid(gb_ref[...])
        dproj_ref[:, COL_GA:COL_GA + D_MODEL] = (dmerged * ya * (sa * (1.0 - sa))).astype(BF16)
        dproj_ref[:, COL_GB:COL_GB + D_MODEL] = (dmerged * yb * (sb * (1.0 - sb))).astype(BF16)
        dproj_ref[:, COL_GB + D_MODEL:] = jnp.zeros((tm, COL_DIL - COL_GB - D_MODEL), BF16)
        dya = (dmerged * sa).astype(BF16)
        dyb = (dmerged * sb).astype(BF16)
        dya_ref[...] = dya
        dyb_ref[...] = dyb
        da_ref[...] = _dot_nn(dya, wpa_ref[...])
        db_ref[...] = _dot_nn(dyb, wpb_ref[...]).astype(BF16)

    def rows(w, cb=0):
        return pl.BlockSpec((tm, w), lambda i, cb=cb: (i, cb))

    def whole(a):
        return pl.BlockSpec(a.shape, lambda i: (0, 0))

    d = D_MODEL
    blk = _nbytes((tm, d), F32) * 8 + sum(_nbytes(a.shape, BF16) for a in (wt_pa, wt_pb, w_out))
    return pl.pallas_call(
        body, name="mix_out_bwd", grid=(SEQ // tm,),
        in_specs=[rows(d), rows(DIL_OUT_WIDTH), rows(FOX_WIDTH), rows(d, COL_GA // d), rows(d, COL_GB // d),
                  whole(wt_pa), whole(wt_pb), whole(w_out)] + [_ANY] * len(deps),
        out_specs=[rows(COL_DIL)] + [rows(d)] * 2 + [rows(DIL_OUT_WIDTH), rows(FOX_WIDTH)],
        out_shape=[jax.ShapeDtypeStruct((SEQ, PROJ_COLS), BF16)] + [jax.ShapeDtypeStruct((SEQ, d), BF16)] * 2
        + [jax.ShapeDtypeStruct((SEQ, DIL_OUT_WIDTH), F32), jax.ShapeDtypeStruct((SEQ, FOX_WIDTH), BF16)],
        compiler_params=pltpu.CompilerParams(dimension_semantics=("parallel",), vmem_limit_bytes=_vmem_limit(blk)),
    )(dmix, out_a, out_b, proj, proj, wt_pa, wt_pb, w_out, *deps)


FFN_TM, FFN_TN = 1024, 256


def _ffn_up(h2, wt_gate, wt_up):
    tm, tn = FFN_TM, FFN_TN

    def body(h_ref, wg_ref, wu_ref, gate_ref, up_ref, act_ref):
        gate = _dot_nt(h_ref[...], wg_ref[...])
        up = _dot_nt(h_ref[...], wu_ref[...])
        gate_ref[...] = gate
        up_ref[...] = up
        act_ref[...] = (gate * _sigmoid(gate) * up).astype(BF16)

    tile = pl.BlockSpec((tm, tn), lambda i, j: (i, j))
    w_spec = pl.BlockSpec((tn, D_MODEL), lambda i, j: (j, 0))
    return pl.pallas_call(
        body, name="ffn_up", grid=(SEQ // tm, D_FF // tn),
        in_specs=[pl.BlockSpec((tm, D_MODEL), lambda i, j: (i, 0)), w_spec, w_spec],
        out_specs=[tile, tile, tile],
        out_shape=[jax.ShapeDtypeStruct((SEQ, D_FF), dt) for dt in (F32, F32, BF16)],
        compiler_params=pltpu.CompilerParams(
            dimension_semantics=("parallel", "parallel"), vmem_limit_bytes=_vmem_limit(8 * 2**20)),
    )(h2, wt_gate, wt_up)


def _ffn_act_bwd(dff, w_down, gate, up):
    tm, tn = FFN_TM, FFN_TN

    def body(d_ref, wd_ref, gate_ref, up_ref, dgate_ref, dup_ref):
        dact = _dot_nt(d_ref[...], wd_ref[...])
        gate = gate_ref[...]
        sg = _sigmoid(gate)
        dgate_ref[...] = (dact * up_ref[...] * (sg * (1.0 + gate * (1.0 - sg)))).astype(BF16)
        dup_ref[...] = (dact * (gate * sg)).astype(BF16)

    tile = pl.BlockSpec((tm, tn), lambda i, j: (i, j))
    return pl.pallas_call(
        body, name="ffn_act_bwd", grid=(SEQ // tm, D_FF // tn),
        in_specs=[pl.BlockSpec((tm, D_MODEL), lambda i, j: (i, 0)), pl.BlockSpec((tn, D_MODEL), lambda i, j: (j, 0)),
                  tile, tile],
        out_specs=[tile, tile],
        out_shape=[jax.ShapeDtypeStruct((SEQ, D_FF), BF16)] * 2,
        compiler_params=pltpu.CompilerParams(
            dimension_semantics=("parallel", "parallel"), vmem_limit_bytes=_vmem_limit(8 * 2**20)),
    )(dff, w_down, gate, up)


def _loss_head(ff, x1, target, g_post):
    def fn(ff, x1, tgt, g):
        r = _rms_scale(ff)
        nrm = ff * r
        err = (x1 + nrm * g) - tgt
        loss = 0.5 * jnp.sum(jnp.mean(err * err, axis=-1, keepdims=True), axis=0, keepdims=True)
        dy = err * (1.0 / D_MODEL)
        u = dy * g
        dff = r * u - ff * (r * r * r) * jnp.mean(u * ff, axis=-1, keepdims=True)
        return dy, dff, jnp.broadcast_to(loss, (1, LANES)), jnp.sum(dy * nrm, axis=0, keepdims=True)

    d = D_MODEL
    return _rowwise(fn, "loss_head", SEQ, 256, [(ff, d, 0), (x1, d, 0), (target, d, 0)], [g_post],
                    [(d, F32), (d, BF16)], [LANES, d])


def _post_ffn_bwd(dh2, x1, dy, mix, g_ffn_pre, g_mix_post):
    def fn(dh2, x1, dy, mix, g3, g2):
        dx, dg3 = _rms_bwd(x1, dh2, g3)
        dx1 = dy + dx
        dmix, dg2 = _rms_bwd(mix, dx1, g2)
        return dx1, dmix, dg3, dg2

    d = D_MODEL
    return _rowwise(fn, "post_ffn_bwd", SEQ, 256, [(dh2, d, 0), (x1, d, 0), (dy, d, 0), (mix, d, 0)],
                    [g_ffn_pre, g_mix_post], [(d, F32), (d, BF16)], [d, d])


def _input_bwd(dh, x, dx1, g_pre, deps=()):
    def fn(dh, x, dx1, g):
        dx, dg = _rms_bwd(x, dh, g)
        return dx1 + dx, dg

    d = D_MODEL
    return _rowwise(fn, "input_bwd", SEQ, 256, [(dh, d, 0), (x, d, 0), (dx1, d, 0)], [g_pre], [(d, F32)], [d],
                    deps=deps)


def _adam_math(w, g, m, v):
    m = ADAM_B1 * m + (1.0 - ADAM_B1) * g
    v = ADAM_B2 * v + (1.0 - ADAM_B2) * (g * g)
    m_hat = m / (1.0 - ADAM_B1 ** ADAM_STEP)
    v_hat = v / (1.0 - ADAM_B2 ** ADAM_STEP)
    delta = -ADAM_LR * (m_hat / (jnp.sqrt(v_hat) + ADAM_EPS) + ADAM_WD * w)
    return delta, m, v


def _adam(w, g, m, v, name):
    r, c = w.shape
    tc = _col_tile(r, c)

    def body(w_ref, g_ref, m_ref, v_ref, d_ref, nm_ref, nv_ref):
        d_ref[...], nm_ref[...], nv_ref[...] = _adam_math(w_ref[...], g_ref[...], m_ref[...], v_ref[...])

    spec = pl.BlockSpec((r, tc), lambda j: (0, j))
    return pl.pallas_call(
        body, name=name, grid=(c // tc,), in_specs=[spec] * 4, out_specs=[spec] * 3,
        out_shape=[jax.ShapeDtypeStruct((r, c), F32)] * 3,
        compiler_params=pltpu.CompilerParams(dimension_semantics=("parallel",)),
    )(w, g, m, v)


def _adam_small(gathered, ws, ms, vs, loss_parts):
    n = len(ws)

    def body(*refs):
        outs = refs[4 * n + 1:]
        loss = refs[4 * n][0]
        for dev in range(1, N_DEV):
            loss = loss + refs[4 * n][dev]
        outs[4 * n][...] = loss
        for i in range(n):
            ga_ref, w_ref, m_ref, v_ref = (refs[j * n + i] for j in range(4))
            g = ga_ref[0]
            for dev in range(1, N_DEV):
                g = g + ga_ref[dev]
            g = g[:, :w_ref.shape[1]]
            outs[4 * i][...] = g
            outs[4 * i + 1][...], outs[4 * i + 2][...], outs[4 * i + 3][...] = _adam_math(
                w_ref[...], g, m_ref[...], v_ref[...])

    out_shape = [jax.ShapeDtypeStruct(w.shape, F32) for w in ws for _ in range(4)]
    out_shape.append(jax.ShapeDtypeStruct((1, LANES), F32))
    out = pl.pallas_call(body, name="adam_small", out_shape=out_shape)(*gathered, *ws, *ms, *vs, loss_parts)
    return [out[4 * i:4 * i + 4] for i in range(n)], out[4 * n]


def _proj_segments():
    segs = [(3848, 5896), (None, COL_DIL - 2 * D_MODEL)]
    for base, width in ((0, DIL_WIDTH), (3 * DIL_WIDTH, FOX_WIDTH)):
        for pair in range(width // LANES):
            segs += [(base + j * width + pair * LANES, base + j * width + (pair + 1) * LANES) for j in range(3)]
    return tuple(segs + [(3840, 3848), (None, PROJ_COLS - COL_F - 8)])


_PROJ_SEGMENTS = _proj_segments()


def _proj_weight_t(gathered):
    w = gathered.reshape(IN_COLS, D_MODEL)
    return jnp.concatenate([jnp.zeros((hi, D_MODEL), w.dtype) if lo is None else w[lo:hi] for lo, hi in _PROJ_SEGMENTS],
                           axis=0)


def _proj_weight_grad_slots(dwt_r):
    starts, at = [], 0
    for lo, hi in _PROJ_SEGMENTS:
        if lo is not None:
            starts.append((lo, hi, at))
        at += hi if lo is None else hi - lo
    flat = jnp.concatenate([dwt_r[seg_at:seg_at + hi - lo] for lo, hi, seg_at in sorted(starts)], axis=0)
    return flat.reshape(N_DEV, IN_SHARD, D_MODEL)


def kernel(x, w_in, w_proj_a, w_proj_b, w_out, b_forget, w_ffn_gate, w_ffn_up, w_ffn_down, norm_mix_pre, norm_mix_post, norm_ffn_pre, norm_ffn_post, loss_target, m_w_in, m_w_proj_a, m_w_proj_b, m_w_out, m_b_forget, m_w_ffn_gate, m_w_ffn_up, m_w_ffn_down, m_norm_mix_pre, m_norm_mix_post, m_norm_ffn_pre, m_norm_ffn_post, v_w_in, v_w_proj_a, v_w_proj_b, v_w_out, v_b_forget, v_w_ffn_gate, v_w_ffn_up, v_w_ffn_down, v_norm_mix_pre, v_norm_mix_post, v_norm_ffn_pre, v_norm_ffn_post):
    d = D_MODEL
    names = ("w_in", "w_proj_a", "w_proj_b", "w_out", "w_ffn_gate", "w_ffn_up", "w_ffn_down")
    col_sharded = ("w_in", "w_proj_a", "w_proj_b", "w_ffn_gate", "w_ffn_up")

    def row_shards(arrs):
        return {k: (a[0].T if k in col_sharded else a[0]) for k, a in zip(names, arrs)}

    shards = row_shards((w_in, w_proj_a, w_proj_b, w_out, w_ffn_gate, w_ffn_up, w_ffn_down))
    moments_m = row_shards((m_w_in, m_w_proj_a, m_w_proj_b, m_w_out, m_w_ffn_gate, m_w_ffn_up, m_w_ffn_down))
    moments_v = row_shards((v_w_in, v_w_proj_a, v_w_proj_b, v_w_out, v_w_ffn_gate, v_w_ffn_up, v_w_ffn_down))
    core = lax.axis_index("c").astype(jnp.int32).reshape(1)
    chip = (2 * lax.axis_index("x") + lax.axis_index("y")).astype(jnp.int32).reshape(1)
    x2, target = x[0], loss_target[0]

    me = 4 * lax.axis_index("x") + 2 * lax.axis_index("y") + lax.axis_index("c")
    mix_names, ffn_names = names[:4], names[4:]
    first_names, later_names = names[:1], names[1:]
    shards16 = {k: shards[k].astype(BF16) for k in names}

    def landing(k):
        return lax.dynamic_update_slice(lax.empty((N_DEV,) + shards[k].shape, BF16), shards16[k][None], (me, 0, 0))

    ag_first = _exchange_start("ag_first_chips_start", _gather_chips_copies, [shards16[k] for k in first_names],
                               [landing(k) for k in first_names], 3 * len(first_names))
    h = _rowwise(lambda xb, g: xb * _rms_scale(xb) * g, "norm_mix_pre", SEQ, 256, [(x2, d, 0)], [norm_mix_pre],
                 [(d, BF16)], deps=[ag_first.token])[0]
    _, lands = _exchange_wait("ag_first_chips_wait", ag_first, [h])
    ag_first = _exchange_start("ag_first_sibling_start", _gather_sibling_copies, [], lands, 4 * len(first_names))
    ag_later = _exchange_start("ag_later_chips_start", _gather_chips_copies, [shards16[k] for k in later_names],
                               [landing(k) for k in later_names], 3 * len(later_names), after=[ag_first.token])
    gathered = dict(zip(first_names, _exchange_wait("ag_first_sibling_wait", ag_first, [ag_later.token])[1]))
    wt_r = _proj_weight_t(gathered["w_in"])

    proj = _matmul([(h, wt_r)], "nt", F32, "in_proj", 1024, 896, 1024)
    tables = _rope_tables()
    o_dil, lse_dil = zip(*[_dil_fwd(g, proj, tables) for g in range(len(DILATIONS))])
    out_a = _dil_combine(o_dil, lse_dil)
    _, lands = _exchange_wait("ag_later_chips_wait", ag_later, [out_a])
    ag_later = _exchange_start("ag_later_sibling_start", _gather_sibling_copies, [], lands, 4 * len(later_names))

    b_pad = jnp.pad(b_forget, ((0, 0), (0, LANES - N_FOX_HEADS)))
    f_rows = _fox_gate(proj, b_pad, deps=[ag_later.token])
    out_b, lse_fox = _fox_fwd(proj, f_rows)

    gathered = dict(zip(later_names, _exchange_wait("ag_later_sibling_wait", ag_later, [out_b])[1]))
    wt_pa = gathered["w_proj_a"].reshape(d, DIL_OUT_WIDTH)
    wt_pb = gathered["w_proj_b"].reshape(d, FOX_WIDTH)
    w_o = gathered["w_out"].reshape(d, d)
    wt_g = gathered["w_ffn_gate"].reshape(D_FF, d)
    wt_u = gathered["w_ffn_up"].reshape(D_FF, d)
    w_d = gathered["w_ffn_down"].reshape(D_FF, d)
    merged, mix, x1, h2 = _mix_out(out_a, out_b, proj, x2, wt_pa, wt_pb, w_o, norm_mix_post, norm_ffn_pre)

    gate, up, act = _ffn_up(h2, wt_g, wt_u)
    ff = _matmul([(act, w_d)], "nn", F32, "ffn_down", 1024, 1024, 1408)
    dy, dff, loss_part, dg_ffn_post = _loss_head(ff, x1, target, norm_ffn_post)

    dgate, dup = _ffn_act_bwd(dff, w_d, gate, up)
    grads_t = {}
    grads_t["w_ffn_down"] = _matmul([(act, dff)], "tn", F32, "grad_w_ffn_down", 1408, 1024, 1024)
    grads_t["w_ffn_gate"] = _matmul([(dgate, h2)], "tn", F32, "grad_w_ffn_gate", 1408, 1024, 1024)
    grads_t["w_ffn_up"] = _matmul([(dup, h2)], "tn", F32, "grad_w_ffn_up", 1408, 1024, 1024)
    rs_ffn = _ReduceScatter("ffn", {k: grads_t[k] for k in ffn_names}, core, chip)
    dh2 = _matmul([(dgate, wt_g), (dup, wt_u)], "nn", F32, "ffn_up_bwd", 1024, 1024, 1408, deps=[rs_ffn.token])
    dx1, dmix, dg_ffn_pre, dg_mix_post = _post_ffn_bwd(dh2, x1, dy, mix, norm_ffn_pre, norm_mix_post)
    rs_ffn.start_chips([dmix])

    dproj, dya, dyb, d_out_a, d_out_b = _mix_out_bwd(dmix, out_a, out_b, proj, wt_pa, wt_pb, w_o, deps=[rs_ffn.token])
    grads_t["w_out"] = _matmul([(merged, dmix)], "tn", F32, "grad_w_out", 1024, 1024, 1024)
    grads_t["w_proj_a"] = _matmul([(dya, out_a)], "tn", F32, "grad_w_proj_a", 1024, DIL_OUT_WIDTH, SEQ)
    grads_t["w_proj_b"] = _matmul([(dyb, out_b)], "tn", F32, "grad_w_proj_b", 1024, FOX_WIDTH, SEQ)

    dproj, d_cum = _fox_bwd(proj, d_out_b, lse_fox, f_rows, dproj)
    d_cum_rows = jnp.pad(d_cum[:, :2].reshape(N_FOX_HEADS, SEQ), ((0, F_ROWS - N_FOX_HEADS), (0, 0)))
    dproj, db_part = _fox_gate_bwd(d_cum_rows, proj, b_pad, dproj)

    do_dil, c_dil = _dil_combine_bwd(d_out_a, o_dil, lse_dil)
    for g in range(len(DILATIONS)):
        dproj = _dil_bwd(g, proj, tables, do_dil[g], lse_dil[g], c_dil[g], dproj)
    dwt_r = _matmul([(dproj, h)], "tn", F32, "grad_w_in", 896, 1024, 1024)
    grads_t["w_in"] = _proj_weight_grad_slots(dwt_r)
    rs_mix = _ReduceScatter("mix", {k: grads_t[k] for k in mix_names}, core, chip)
    grads = rs_ffn.finish([rs_mix.token])
    big = {k: _adam(shards[k], grads[k], moments_m[k], moments_v[k], "adam_" + k) for k in ffn_names}
    rs_mix.start_chips([big[k][0] for k in ffn_names])
    dh = _matmul([(dproj, wt_r)], "nn", F32, "in_proj_bwd", 1024, 1024, 896, deps=[rs_mix.token])
    grad_x, dg_mix_pre = _input_bwd(dh, x2, dx1, norm_mix_pre)

    small_all = _all_gather([dg_mix_pre, dg_mix_post, dg_ffn_pre, dg_ffn_post, db_part, loss_part],
                            "small_grads_all_gather")
    small, loss = _adam_small(small_all[:5], [norm_mix_pre, norm_mix_post, norm_ffn_pre, norm_ffn_post, b_forget],
                              [m_norm_mix_pre, m_norm_mix_post, m_norm_ffn_pre, m_norm_ffn_post, m_b_forget],
                              [v_norm_mix_pre, v_norm_mix_post, v_norm_ffn_pre, v_norm_ffn_post, v_b_forget],
                              small_all[5])

    grads.update(rs_mix.finish([small[0][0], grad_x]))
    big.update({k: _adam(shards[k], grads[k], moments_m[k], moments_v[k], "adam_" + k) for k in mix_names})

    def leaves(i):
        def nat(k):
            a = grads[k] if i == 0 else big[k][i - 1]
            return (a.T if k in col_sharded else a)[None]

        return [nat("w_in"), nat("w_proj_a"), nat("w_proj_b"), nat("w_out"), small[4][i],
                nat("w_ffn_gate"), nat("w_ffn_up"), nat("w_ffn_down"), *[small[r][i] for r in range(4)]]

    return (loss[0, 0], grad_x[None], *leaves(0), *leaves(1), *leaves(2), *leaves(3))
```

```python
import functools
import math

import jax
import jax.numpy as jnp
from jax import lax
from jax.experimental import pallas as pl
from jax.experimental.pallas import tpu as pltpu

F32 = jnp.float32
BF16 = jnp.bfloat16
MESH = pl.DeviceIdType.MESH

D_MODEL = 1024
SEQ = 2048
HEAD_DIM = 64
BLOCK = 128
N_BLOCKS = SEQ // BLOCK
DILATIONS = (1, 4, 16)
N_FOX_HEADS = 8
DIL_WIDTH = 768
DIL_OUT_WIDTH = 256
FOX_WIDTH = 512
D_FF = 2816
ROPE_THETA = 500000.0
ROPE_DIM = HEAD_DIM // 4
ROPE_HALF = ROPE_DIM // 2
EPS = 1e-6
NEG_INF = -1e30
QK_SCALE = 1.0 / math.sqrt(HEAD_DIM)
IN_COLS = 5896
N_DEV = 8
IN_SHARD = IN_COLS // N_DEV

ADAM_LR = 0.001
ADAM_B1 = 0.9
ADAM_B2 = 0.999
ADAM_EPS = 1e-08
ADAM_WD = 0.01
ADAM_STEP = 10

V7X_VMEM_BYTES = 64 * 2**20
LANES = 128
SUBLANES = 8

PROJ_COLS = 6272
COL_GA, COL_GB = 0, 1024
COL_QA, COL_KA, COL_VA = 2304, 3072, 3840
COL_QB, COL_KB, COL_VB = 4608, 5120, 5632
COL_F = 6144
F_ROWS = 16


def _vmem_limit(block_bytes):
    want = 2 * block_bytes + 16 * 2**20
    return int(min(max(want, 32 * 2**20), V7X_VMEM_BYTES - 8 * 2**20))


def _nbytes(shape, dtype):
    return math.prod(shape) * jnp.dtype(dtype).itemsize


def _dot(a, b, dims):
    return lax.dot_general(a, b, (dims, ((), ())), preferred_element_type=F32)


def _dot_nn(a, b):
    return _dot(a, b, ((1,), (0,)))


def _dot_nt(a, b):
    return _dot(a, b, ((1,), (1,)))


def _dot_tn(a, b):
    return _dot(a, b, ((0,), (0,)))


def _sigmoid(z):
    return 1.0 / (1.0 + jnp.exp(-z))


def _split3(x):
    hi = x.astype(BF16)
    r1 = x - hi.astype(F32)
    mid = r1.astype(BF16)
    lo = (r1 - mid.astype(F32)).astype(BF16)
    return hi, mid, lo


def _dot3_nn(x, ones_matrix):
    hi, mid, lo = _split3(x)
    return (_dot_nn(hi, ones_matrix) + _dot_nn(mid, ones_matrix)) + _dot_nn(lo, ones_matrix)


def _rowwise(fn, name, n_rows, tm, row_ins, bcast_ins, row_outs, acc_outs=(), deps=()):
    n_in = len(row_ins) + len(bcast_ins)
    n_ro = len(row_outs)

    def body(*refs):
        res = fn(*[r[...] for r in refs[:n_in]])
        if not isinstance(res, (tuple, list)):
            res = (res,)
        outs = refs[n_in + len(deps):]
        for r, o in zip(res[:n_ro], outs[:n_ro]):
            o[...] = r.astype(o.dtype)
        first = pl.program_id(0) == 0
        for r, o in zip(res[n_ro:], outs[n_ro:]):
            _accumulate(o, r, first)

    in_specs = [pl.BlockSpec((tm, w), lambda i, cb=cb: (i, cb)) for _, w, cb in row_ins]
    in_specs += [pl.BlockSpec(a.shape, lambda i: (0, 0)) for a in bcast_ins]
    in_specs += [pl.BlockSpec(memory_space=pl.ANY)] * len(deps)
    out_specs = [pl.BlockSpec((tm, w), lambda i: (i, 0)) for w, _ in row_outs]
    out_specs += [pl.BlockSpec((1, w), lambda i: (0, 0)) for w in acc_outs]
    out_shape = [jax.ShapeDtypeStruct((n_rows, w), dt) for w, dt in row_outs]
    out_shape += [jax.ShapeDtypeStruct((1, w), F32) for w in acc_outs]
    blk = sum(_nbytes((tm, w), a.dtype) for a, w, _ in row_ins) + sum(_nbytes((tm, w), dt) for w, dt in row_outs)
    return pl.pallas_call(
        body, name=name, grid=(n_rows // tm,), in_specs=in_specs, out_specs=out_specs, out_shape=out_shape,
        compiler_params=pltpu.CompilerParams(
            dimension_semantics=("arbitrary" if acc_outs else "parallel",), vmem_limit_bytes=_vmem_limit(3 * blk)),
    )(*[a for a, _, _ in row_ins], *bcast_ins, *deps)


def _accumulate(o_ref, part, first):
    @pl.when(first)
    def _():
        o_ref[...] = part

    @pl.when(jnp.logical_not(first))
    def _():
        o_ref[...] += part


_MM_DIMS = {"nn": ((1,), (0,)), "nt": ((1,), (1,)), "tn": ((0,), (0,))}


def _matmul(pairs, mode, out_dtype, name, tm, tn, tk, deps=()):
    a0, b0 = pairs[0]
    if mode == "tn":
        kk, m = a0.shape
    else:
        m, kk = a0.shape
    n = b0.shape[0] if mode == "nt" else b0.shape[1]
    assert m % tm == 0 and n % tn == 0 and kk % tk == 0, (name, m, n, kk)
    nk = kk // tk
    n_pairs = len(pairs)
    dims = _MM_DIMS[mode]
    n_in = 2 * n_pairs + len(deps)

    def body(*refs):
        o_ref = refs[n_in]
        part = None
        for p in range(n_pairs):
            d = _dot(refs[2 * p][...].astype(BF16), refs[2 * p + 1][...].astype(BF16), dims)
            part = d if part is None else part + d
        if nk == 1:
            o_ref[...] = part.astype(o_ref.dtype)
            return
        acc = refs[n_in + 1]
        k = pl.program_id(2)

        @pl.when(k == 0)
        def _():
            acc[...] = part

        @pl.when(k > 0)
        def _():
            acc[...] += part

        @pl.when(k == nk - 1)
        def _():
            o_ref[...] = acc[...].astype(o_ref.dtype)

    if mode == "tn":
        a_spec = pl.BlockSpec((tk, tm), lambda i, j, k: (k, i))
    else:
        a_spec = pl.BlockSpec((tm, tk), lambda i, j, k: (i, k))
    if mode == "nt":
        b_spec = pl.BlockSpec((tn, tk), lambda i, j, k: (j, k))
    else:
        b_spec = pl.BlockSpec((tk, tn), lambda i, j, k: (k, j))
    blk = sum(_nbytes((tm, tk), a.dtype) + _nbytes((tk, tn), b.dtype) for a, b in pairs) + 2 * _nbytes((tm, tn), F32)
    flat = [a for pair in pairs for a in pair]
    return pl.pallas_call(
        body, name=name, grid=(m // tm, n // tn, nk),
        in_specs=[a_spec, b_spec] * n_pairs + [pl.BlockSpec(memory_space=pl.ANY)] * len(deps),
        out_specs=pl.BlockSpec((tm, tn), lambda i, j, k: (i, j)),
        out_shape=jax.ShapeDtypeStruct((m, n), out_dtype),
        scratch_shapes=[] if nk == 1 else [pltpu.VMEM((tm, tn), F32)],
        compiler_params=pltpu.CompilerParams(
            dimension_semantics=("parallel", "parallel", "arbitrary"), vmem_limit_bytes=_vmem_limit(blk)),
    )(*flat, *deps)


def _rms_scale(x):
    return lax.rsqrt(jnp.mean(x * x, axis=-1, keepdims=True) + EPS)


def _rms_bwd(xin, dyn, g):
    r = _rms_scale(xin)
    u = dyn * g
    dx = r * u - xin * (r * r * r) * jnp.mean(u * xin, axis=-1, keepdims=True)
    dg = jnp.sum(dyn * xin * r, axis=0, keepdims=True)
    return dx, dg


def _mesh_pos():
    return lax.axis_index("x"), lax.axis_index("y"), lax.axis_index("c")


def _all_gather(xs, name):
    n = len(xs)

    def body(*refs):
        x_refs, out_refs = refs[:n], refs[n:2 * n]
        send_sems, recv_sems, local_sems = refs[2 * n:]
        mx, my, mc = _mesh_pos()
        me, sib = (mx, my, mc), (mx, my, 1 - mc)
        chips = [(1 - mx, my), (mx, 1 - my), (1 - mx, 1 - my)]

        def slot(a, dev):
            px, py, pc = dev
            return out_refs[a].at[4 * px + 2 * py + pc]

        def copy(k, a, block, to, src=None):
            return pltpu.make_async_remote_copy(
                src_ref=slot(a, block) if src is None else src, dst_ref=slot(a, block),
                send_sem=send_sems.at[a * 7 + k], recv_sem=recv_sems.at[a * 7 + k],
                device_id=to, device_id_type=MESH)

        mine = [pltpu.make_async_copy(x_refs[a], slot(a, me), local_sems.at[a]) for a in range(n)]
        for cp in mine:
            cp.start()
        first = []
        for a in range(n):
            first.append(copy(0, a, me, sib, x_refs[a]))
            first += [copy(1 + j, a, me, (*chip, mc), x_refs[a]) for j, chip in enumerate(chips)]
        for cp in first:
            cp.start()
        passed = []
        for a in range(n):
            for j, chip in enumerate(chips):
                copy(1 + j, a, (*chip, mc), me).wait_recv()
                fwd = copy(4 + j, a, (*chip, mc), sib)
                fwd.start()
                passed.append(fwd)
        for a in range(n):
            copy(0, a, sib, me).wait_recv()
            for j, chip in enumerate(chips):
                copy(4 + j, a, (*chip, 1 - mc), me).wait_recv()
        for cp in first + passed:
            cp.wait_send()
        for cp in mine:
            cp.wait()

    hbm = pl.BlockSpec(memory_space=pl.ANY)
    return pl.pallas_call(
        body, name=name,
        out_shape=[jax.ShapeDtypeStruct((N_DEV,) + x.shape, x.dtype) for x in xs],
        in_specs=[hbm] * n, out_specs=[hbm] * n,
        scratch_shapes=[pltpu.SemaphoreType.DMA((7 * n,)), pltpu.SemaphoreType.DMA((7 * n,)),
                        pltpu.SemaphoreType.DMA((n,))],
    )(*xs)


_HBM = pl.BlockSpec(memory_space=pltpu.HBM)
_SEM = pl.BlockSpec(memory_space=pltpu.SEMAPHORE)
_ANY = pl.BlockSpec(memory_space=pl.ANY)
_DATAFLOW = pltpu.SideEffectType.DATAFLOW_SIDE_EFFECTING


def _flip_peer(flip):
    mx, my, mc = _mesh_pos()
    return (1 - mx if flip & 2 else mx, 1 - my if flip & 1 else my, mc)


def _remote(src, dst, send_sems, recv_sems, k, peer):
    return pltpu.make_async_remote_copy(src_ref=src, dst_ref=dst, send_sem=send_sems.at[k], recv_sem=recv_sems.at[k],
                                        device_id=peer, device_id_type=MESH)


def _gather_chips_copies(srcs, lands, send_sems, recv_sems):
    mx, my, mc = _mesh_pos()
    me = 4 * mx + 2 * my + mc
    return [_remote(srcs[a], lands[a].at[me], send_sems, recv_sems, 3 * a + flip - 1, _flip_peer(flip))
            for a in range(len(srcs)) for flip in (1, 2, 3)]


def _gather_sibling_copies(srcs, lands, send_sems, recv_sems):
    mx, my, mc = _mesh_pos()
    return [_remote(lands[a].at[2 * k + mc], lands[a].at[2 * k + mc], send_sems, recv_sems, 4 * a + k, (mx, my, 1 - mc))
            for a in range(len(lands)) for k in range(4)]


def _scatter_sibling_copies(srcs, lands, send_sems, recv_sems):
    mx, my, mc = _mesh_pos()
    return [_remote(srcs[a].at[k, 1 - mc], lands[a].at[k], send_sems, recv_sems, 4 * a + k, (mx, my, 1 - mc))
            for a in range(len(srcs)) for k in range(4)]


def _scatter_chips_copies(srcs, lands, send_sems, recv_sems):
    mx, my, _ = _mesh_pos()
    k0 = 2 * mx + my
    return [_remote(srcs[a].at[jnp.bitwise_xor(k0, flip)], lands[a].at[flip - 1], send_sems, recv_sems,
                    3 * a + flip - 1, _flip_peer(flip))
            for a in range(len(srcs)) for flip in (1, 2, 3)]


class _Exchange:
    def __init__(self, copies, n_src, send_sems, recv_sems, thru, token):
        self.copies, self.n_src, self.send_sems, self.recv_sems, self.thru, self.token = (
            copies, n_src, send_sems, recv_sems, thru, token)


def _exchange_start(name, copies, srcs, lands, n_copies, after=()):
    bufs = list(srcs) + list(lands)
    nb, ns = len(bufs), len(srcs)

    def body(*refs):
        send_sems, recv_sems = refs[nb + len(after)], refs[nb + len(after) + 1]
        for cp in copies(refs[:ns], refs[ns:nb], send_sems, recv_sems):
            cp.start()
        refs[-1][...] = jnp.zeros_like(refs[-1])

    out = pl.pallas_call(
        body, name=name,
        out_shape=(pltpu.SemaphoreType.DMA((n_copies,)), pltpu.SemaphoreType.DMA((n_copies,)),
                   *[pltpu.HBM(b.shape, b.dtype) for b in bufs], jax.ShapeDtypeStruct((SUBLANES, LANES), F32)),
        in_specs=[_HBM] * nb + [_ANY] * len(after),
        out_specs=(_SEM, _SEM, *[_HBM] * nb, pl.BlockSpec(memory_space=pltpu.VMEM)),
        input_output_aliases={i: 2 + i for i in range(nb)},
        compiler_params=pltpu.CompilerParams(has_side_effects=_DATAFLOW),
    )(*[pltpu.with_memory_space_constraint(b, pltpu.HBM) for b in bufs], *after)
    return _Exchange(copies, ns, out[0], out[1], list(out[2:2 + nb]), out[-1])


def _exchange_wait(name, ex, after):
    nb, ns = len(ex.thru), ex.n_src

    def body(*refs):
        for cp in ex.copies(refs[:ns], refs[ns:nb], refs[nb], refs[nb + 1]):
            cp.wait_send()
            cp.wait_recv()

    out = pl.pallas_call(
        body, name=name, out_shape=tuple(pltpu.HBM(b.shape, b.dtype) for b in ex.thru),
        in_specs=[_HBM] * nb + [_SEM, _SEM] + [_ANY] * len(after), out_specs=tuple([_HBM] * nb),
        input_output_aliases={i: i for i in range(nb)},
        compiler_params=pltpu.CompilerParams(has_side_effects=_DATAFLOW),
    )(*ex.thru, ex.send_sems, ex.recv_sems, *after)
    return list(out[:ns]), list(out[ns:])


def _col_tile(r, c):
    return next(t for t in (1024, 512, 256, 128) if c % t == 0 and (r * t * 4 <= 2**20 or t == 128))


def _add_sibling(g4, recv, core, name):
    _, _, r, c = g4.shape
    tc = _col_tile(r, c)

    def body(core_ref, g_ref, r_ref, o16_ref, o32_ref):
        s = g_ref[0, 0] + r_ref[0]
        o16_ref[0] = s.astype(BF16)
        o32_ref[0] = s

    out = pl.BlockSpec((1, r, tc), lambda k, j, core_ref: (k, 0, j))
    return pl.pallas_call(
        body, name=name,
        out_shape=[jax.ShapeDtypeStruct((4, r, c), BF16), jax.ShapeDtypeStruct((4, r, c), F32)],
        grid_spec=pltpu.PrefetchScalarGridSpec(
            num_scalar_prefetch=1, grid=(4, c // tc),
            in_specs=[pl.BlockSpec((1, 1, r, tc), lambda k, j, core_ref: (k, core_ref[0], 0, j)), out],
            out_specs=[out, out]),
        compiler_params=pltpu.CompilerParams(dimension_semantics=("parallel", "parallel")),
    )(core, g4, recv)


def _add_chips(p32, recv, chip, name):
    _, r, c = p32.shape
    tc = _col_tile(r, c)

    def body(chip_ref, p_ref, r_ref, o_ref):
        o_ref[...] = ((p_ref[0] + r_ref[0].astype(F32)) + r_ref[1].astype(F32)) + r_ref[2].astype(F32)

    return pl.pallas_call(
        body, name=name, out_shape=jax.ShapeDtypeStruct((r, c), F32),
        grid_spec=pltpu.PrefetchScalarGridSpec(
            num_scalar_prefetch=1, grid=(c // tc,),
            in_specs=[pl.BlockSpec((1, r, tc), lambda j, chip_ref: (chip_ref[0], 0, j)),
                      pl.BlockSpec((3, r, tc), lambda j, chip_ref: (0, 0, j))],
            out_specs=pl.BlockSpec((r, tc), lambda j, chip_ref: (0, j))),
        compiler_params=pltpu.CompilerParams(dimension_semantics=("parallel",)),
    )(chip, p32, recv)


class _ReduceScatter:
    def __init__(self, tag, grads_t, core, chip):
        self.tag, self.core, self.chip, self.names = tag, core, chip, list(grads_t)
        g4s = [g.reshape(4, 2, g.size // (N_DEV * g.shape[-1]), g.shape[-1]) for g in grads_t.values()]
        lands = [lax.empty((4,) + g.shape[2:], F32) for g in g4s]
        self.ex = _exchange_start(f"rs_{tag}_sibling_start", _scatter_sibling_copies, g4s, lands, 4 * len(g4s))
        self.token = self.ex.token

    def start_chips(self, after):
        g4s, from_sibling = _exchange_wait(f"rs_{self.tag}_sibling_wait", self.ex, after)
        parts = [_add_sibling(g4, rv, self.core, f"rs_add_sibling_{k}")
                 for k, g4, rv in zip(self.names, g4s, from_sibling)]
        self.p32s = [p32 for _, p32 in parts]
        p16s = [p16 for p16, _ in parts]
        lands = [lax.empty((3,) + p.shape[1:], BF16) for p in p16s]
        self.ex = _exchange_start(f"rs_{self.tag}_chips_start", _scatter_chips_copies, p16s, lands, 3 * len(p16s))
        self.token = self.ex.token

    def finish(self, after):
        _, from_chips = _exchange_wait(f"rs_{self.tag}_chips_wait", self.ex, after)
        return {k: _add_chips(p32, rv, self.chip, f"rs_add_chips_{k}")
                for k, p32, rv in zip(self.names, self.p32s, from_chips)}


def _rope_tables():
    positions = jnp.arange(SEQ, dtype=F32)
    inv_freq = jnp.power(ROPE_THETA, -jnp.arange(0, ROPE_DIM, 2, dtype=F32) / ROPE_DIM)
    ang = positions[:, None] * inv_freq[None, :]
    cos, sin = jnp.cos(ang), jnp.sin(ang)
    ones = jnp.ones((SEQ, HEAD_DIM - ROPE_DIM), F32)
    zeros8 = jnp.zeros((SEQ, ROPE_HALF), F32)
    zeros = jnp.zeros((SEQ, HEAD_DIM - ROPE_DIM), F32)
    c_head = jnp.concatenate([cos, cos, ones], axis=1)
    s1_head = jnp.concatenate([-sin, zeros8, zeros], axis=1)
    s2_head = jnp.concatenate([zeros8, sin, zeros], axis=1)
    return tuple(jnp.concatenate([t, t], axis=1) for t in (c_head, s1_head, s2_head))


def _rope_apply(x, c, s1, s2):
    w = x.shape[1]
    return x * c + pltpu.roll(x, w - ROPE_HALF, 1) * s1 + pltpu.roll(x, ROPE_HALF, 1) * s2


def _rope_apply_t(dy, c, s1, s2):
    w = dy.shape[1]
    return dy * c + pltpu.roll(dy * s1, ROPE_HALF, 1) + pltpu.roll(dy * s2, w - ROPE_HALF, 1)


def _dil_prev_limit(has_prev):
    return jnp.where(has_prev, 0, BLOCK)


def _dil_valid(limit):
    row = lax.broadcasted_iota(jnp.int32, (BLOCK, 2 * BLOCK), 0)
    col = lax.broadcasted_iota(jnp.int32, (BLOCK, 2 * BLOCK), 1)
    dist = col - row
    return jnp.logical_and(dist >= jnp.where(col < BLOCK, limit, -BLOCK), dist <= BLOCK)


def _upper_half():
    return lax.broadcasted_iota(jnp.int32, (1, LANES), 1) >= HEAD_DIM


def _dil_rows(n, d):
    per = N_BLOCKS // d
    r, lb = n // per, n % per

    def rows(b):
        start = b * (BLOCK * d) + r
        return pl.ds(pl.multiple_of(start, BLOCK), BLOCK) if d == 1 else pl.ds(start, BLOCK, stride=d)

    return rows(lb), rows(jnp.maximum(lb - 1, 0)), lb > 0


def _dil_specs(g):
    def col(base):
        return pl.BlockSpec((SEQ, LANES), lambda p: (0, base // LANES + 2 * g + p))

    table = pl.BlockSpec((SEQ, LANES), lambda p: (0, 0))
    return [col(COL_QA), col(COL_KA), col(COL_VA)], [table] * 3


def _store_columns(blocks, dproj_ref, cols, sem):
    copies = [pltpu.make_async_copy(b, dproj_ref.at[:, pl.ds(pl.multiple_of(c * LANES, LANES), LANES)], sem.at[i])
              for i, (b, c) in enumerate(zip(blocks, cols))]
    for cp in copies:
        cp.start()
    for cp in copies:
        cp.wait()


def _dil_fwd(g, proj, tables):
    d = DILATIONS[g]
    one_block = d == N_BLOCKS

    def body(q_ref, k_ref, v_ref, c_ref, s1_ref, s2_ref, o_ref, lse_ref):
        upper = _upper_half()

        def roped(ref, rows):
            return _rope_apply(ref[rows, :], c_ref[rows, :], s1_ref[rows, :], s2_ref[rows, :])

        def block(n, carry):
            rows, prev, has_prev = _dil_rows(n, d)
            qb = (roped(q_ref, rows) * QK_SCALE).astype(BF16)
            kw, vw = roped(k_ref, rows).astype(BF16), v_ref[rows, :].astype(BF16)
            if one_block:
                row = lax.broadcasted_iota(jnp.int32, (BLOCK, BLOCK), 0)
                valid = lax.broadcasted_iota(jnp.int32, (BLOCK, BLOCK), 1) <= row
            else:
                kw = jnp.concatenate([roped(k_ref, prev).astype(BF16), kw], axis=0)
                vw = jnp.concatenate([v_ref[prev, :].astype(BF16), vw], axis=0)
                valid = _dil_valid(_dil_prev_limit(has_prev))
            outs, lses = [], []
            for head_mask in (jnp.logical_not(upper), upper):
                s = jnp.where(valid, _dot_nt(qb, jnp.where(head_mask, kw, 0)), NEG_INF)
                m = jnp.max(s, axis=-1, keepdims=True)
                p = jnp.exp(s - m)
                den = jnp.sum(p, axis=-1, keepdims=True)
                outs.append(_dot_nn((p / den).astype(BF16), jnp.where(head_mask, vw, 0)))
                lses.append(m + jnp.log(den))
            o_ref[rows, :] = outs[0] + outs[1]
            lse_ref[rows, :] = jnp.where(upper, lses[1], lses[0])
            return carry

        lax.fori_loop(0, N_BLOCKS, block, 0, unroll=2)

    qkv, tabs = _dil_specs(g)
    out = pl.BlockSpec((SEQ, LANES), lambda p: (0, p))
    return pl.pallas_call(
        body, name=f"dil_attn_fwd_{g}", grid=(2,), in_specs=qkv + tabs, out_specs=[out, out],
        out_shape=[jax.ShapeDtypeStruct((SEQ, DIL_OUT_WIDTH), F32)] * 2,
        compiler_params=pltpu.CompilerParams(dimension_semantics=("parallel",)),
    )(proj, proj, proj, *tables)


def _dil_bwd(g, proj, tables, do, lse, c, dproj):
    d = DILATIONS[g]
    one_block = d == N_BLOCKS

    def body(q_ref, k_ref, v_ref, c_ref, s1_ref, s2_ref, do_ref, lse_ref, cc_ref, dproj_in, dproj_ref,
             dq_acc, dk_acc, dv_acc, dq_out, dk_out, dv_out, sem):
        upper = _upper_half()
        dk_acc[...] = jnp.zeros_like(dk_acc)
        dv_acc[...] = jnp.zeros_like(dv_acc)

        def roped(ref, rows):
            return _rope_apply(ref[rows, :], c_ref[rows, :], s1_ref[rows, :], s2_ref[rows, :])

        def block(n, carry):
            rows, prev, has_prev = _dil_rows(n, d)
            qb = (roped(q_ref, rows) * QK_SCALE).astype(BF16)
            dob = do_ref[rows, :].astype(BF16)
            kw, vw = roped(k_ref, rows).astype(BF16), v_ref[rows, :].astype(BF16)
            if one_block:
                row = lax.broadcasted_iota(jnp.int32, (BLOCK, BLOCK), 0)
                valid = lax.broadcasted_iota(jnp.int32, (BLOCK, BLOCK), 1) <= row
            else:
                kw = jnp.concatenate([roped(k_ref, prev).astype(BF16), kw], axis=0)
                vw = jnp.concatenate([v_ref[prev, :].astype(BF16), vw], axis=0)
                valid = _dil_valid(_dil_prev_limit(has_prev))
            lse_t, c_t = lse_ref[rows, :], cc_ref[rows, :]
            dq, dk, dv = None, None, None
            for e, head_mask in enumerate((jnp.logical_not(upper), upper)):
                km, vm = jnp.where(head_mask, kw, 0), jnp.where(head_mask, vw, 0)
                lse_col = lse_t[:, e * HEAD_DIM:e * HEAD_DIM + 1]
                c_col = c_t[:, e * HEAD_DIM:e * HEAD_DIM + 1]
                p = jnp.where(valid, jnp.exp(_dot_nt(qb, km) - lse_col), 0.0)
                ds = (p * (_dot_nt(dob, vm) + c_col)).astype(BF16)
                parts = (_dot_nn(ds, km), _dot_tn(ds, jnp.where(head_mask, qb, 0)),
                         _dot_tn(p.astype(BF16), jnp.where(head_mask, dob, 0)))
                dq, dk, dv = parts if dq is None else (dq + parts[0], dk + parts[1], dv + parts[2])
            dq_acc[rows, :] = dq * QK_SCALE
            if one_block:
                dk_acc[rows, :] += dk
                dv_acc[rows, :] += dv
            else:
                dk_acc[prev, :] += dk[:BLOCK]
                dv_acc[prev, :] += dv[:BLOCK]
                dk_acc[rows, :] += dk[BLOCK:]
                dv_acc[rows, :] += dv[BLOCK:]
            return carry

        lax.fori_loop(0, N_BLOCKS, block, 0, unroll=2)
        tabs = (c_ref[...], s1_ref[...], s2_ref[...])
        dq_out[...] = _rope_apply_t(dq_acc[...], *tabs).astype(BF16)
        dk_out[...] = _rope_apply_t(dk_acc[...], *tabs).astype(BF16)
        dv_out[...] = dv_acc[...].astype(BF16)
        pair = 2 * g + pl.program_id(0)
        _store_columns((dq_out, dk_out, dv_out), dproj_ref,
                       [base // LANES + pair for base in (COL_QA, COL_KA, COL_VA)], sem)

    qkv, tabs = _dil_specs(g)
    tok = pl.BlockSpec((SEQ, LANES), lambda p: (0, p))
    return pl.pallas_call(
        body, name=f"dil_attn_bwd_{g}", grid=(2,),
        in_specs=qkv + tabs + [tok, tok, tok, _ANY], out_specs=_ANY,
        out_shape=jax.ShapeDtypeStruct(dproj.shape, dproj.dtype),
        scratch_shapes=[pltpu.VMEM((SEQ, LANES), F32)] * 3 + [pltpu.VMEM((SEQ, LANES), BF16)] * 3
        + [pltpu.SemaphoreType.DMA((3,))],
        input_output_aliases={9: 0},
        compiler_params=pltpu.CompilerParams(dimension_semantics=("arbitrary",)),
    )(proj, proj, proj, *tables, do, lse, c, dproj)


def _group_weights(l0, l1, l2):
    m = jnp.maximum(jnp.maximum(l0, l1), l2)
    e0, e1, e2 = jnp.exp(l0 - m), jnp.exp(l1 - m), jnp.exp(l2 - m)
    tot = e0 + e1 + e2
    return e0 / tot, e1 / tot, e2 / tot


def _dil_combine(outs, lses, deps=()):
    def fn(o0, o1, o2, l0, l1, l2):
        w0, w1, w2 = _group_weights(l0, l1, l2)
        return w0 * o0 + w1 * o1 + w2 * o2

    w = DIL_OUT_WIDTH
    return _rowwise(fn, "dil_combine", SEQ, 512, [(a, w, 0) for a in list(outs) + list(lses)], [], [(w, F32)],
                    deps=deps)[0]


def _dil_combine_bwd(d_out, outs, lses):
    w = DIL_OUT_WIDTH

    def fn(d, o0, o1, o2, l0, l1, l2):
        row = lax.broadcasted_iota(jnp.int32, (w, w), 0) // HEAD_DIM
        col = lax.broadcasted_iota(jnp.int32, (w, w), 1) // HEAD_DIM
        same_head = jnp.where(row == col, 1.0, 0.0).astype(BF16)
        ws = _group_weights(l0, l1, l2)
        dws = [_dot3_nn(d * og, same_head) for og in (o0, o1, o2)]
        mean = ws[0] * dws[0] + ws[1] * dws[1] + ws[2] * dws[2]
        return tuple(wg * d for wg in ws) + tuple(-wg * mean for wg in ws)

    res = _rowwise(fn, "dil_combine_bwd", SEQ, 256, [(a, w, 0) for a in [d_out] + list(outs) + list(lses)], [],
                   [(w, F32)] * 6)
    return res[:3], res[3:]


def _log1p(e):
    u = 1.0 + e
    return jnp.where(u == 1.0, e, jnp.log(u) * (e / (u - 1.0)))


def _fox_gate(proj, b_pad, deps=()):
    def body(f_ref, b_ref, *rest):
        o_ref = rest[-1]
        z = f_ref[...] + b_ref[...]
        logf = (jnp.minimum(z, 0.0) - _log1p(jnp.exp(-jnp.abs(z)))).T[:F_ROWS]
        row = lax.broadcasted_iota(jnp.int32, (BLOCK, BLOCK), 0)
        col = lax.broadcasted_iota(jnp.int32, (BLOCK, BLOCK), 1)
        before = jnp.where(row <= col, 1.0, 0.0).astype(BF16)
        carry = jnp.zeros((F_ROWS, 1), F32)
        for blk in range(N_BLOCKS):
            run = _dot3_nn(logf[:, blk * BLOCK:(blk + 1) * BLOCK], before) + carry
            o_ref[:, blk * BLOCK:(blk + 1) * BLOCK] = run
            carry = run[:, BLOCK - 1:BLOCK]

    return pl.pallas_call(
        body, name="fox_gate", grid=(1,),
        in_specs=[pl.BlockSpec((SEQ, LANES), lambda i: (0, COL_F // LANES)), pl.BlockSpec((1, LANES), lambda i: (0, 0))]
        + [_ANY] * len(deps),
        out_specs=pl.BlockSpec((F_ROWS, SEQ), lambda i: (0, 0)),
        out_shape=jax.ShapeDtypeStruct((F_ROWS, SEQ), F32),
    )(proj, b_pad, *deps)


def _fox_gate_bwd(d_cum, proj, b_pad, dproj):
    def body(d_ref, f_ref, b_ref, dproj_ref, dz_ref, db_ref):
        row = lax.broadcasted_iota(jnp.int32, (BLOCK, BLOCK), 0)
        col = lax.broadcasted_iota(jnp.int32, (BLOCK, BLOCK), 1)
        after = jnp.where(row >= col, 1.0, 0.0).astype(BF16)
        carry = jnp.zeros((F_ROWS, 1), F32)
        parts = [None] * N_BLOCKS
        for blk in reversed(range(N_BLOCKS)):
            run = _dot3_nn(d_ref[:, blk * BLOCK:(blk + 1) * BLOCK], after) + carry
            parts[blk] = run
            carry = run[:, 0:1]
        dlogf = jnp.concatenate(parts, axis=1)
        dlogf = jnp.concatenate([dlogf, jnp.zeros((LANES - F_ROWS, SEQ), F32)], axis=0).T
        dz = dlogf * _sigmoid(-(f_ref[...] + b_ref[...]))
        dz_ref[...] = dz.astype(BF16)
        db_ref[...] = jnp.sum(dz, axis=0, keepdims=True)

    f_cols = pl.BlockSpec((SEQ, LANES), lambda i: (0, COL_F // LANES))
    return pl.pallas_call(
        body, name="fox_gate_bwd", grid=(1,),
        in_specs=[pl.BlockSpec((F_ROWS, SEQ), lambda i: (0, 0)), f_cols, pl.BlockSpec((1, LANES), lambda i: (0, 0)), _ANY],
        out_specs=[f_cols, pl.BlockSpec((1, LANES), lambda i: (0, 0))],
        out_shape=[jax.ShapeDtypeStruct(dproj.shape, dproj.dtype), jax.ShapeDtypeStruct((1, LANES), F32)],
        input_output_aliases={3: 0},
    )(d_cum, proj, b_pad, dproj)


FOX_TILE = 256
FOX_TILES = SEQ // FOX_TILE


def _row_to_col(row):
    n = row.shape[1]
    eye = lax.broadcasted_iota(jnp.int32, (n, n), 0) == lax.broadcasted_iota(jnp.int32, (n, n), 1)
    return jnp.sum(jnp.where(eye, row, 0.0), axis=1, keepdims=True)


def _fox_scores(q_tile, km, f_row, i):
    t = FOX_TILE
    ext = (i + 1) * t
    f_q = _row_to_col(f_row[:, i * t:(i + 1) * t])
    s = _dot_nt(q_tile, km[:ext]) + (f_q - f_row[:, :ext])
    row = lax.broadcasted_iota(jnp.int32, (t, ext), 0) + i * t
    col = lax.broadcasted_iota(jnp.int32, (t, ext), 1)
    return s, col <= row


def _fox_specs():
    qkv = [pl.BlockSpec((SEQ, LANES), lambda p, base=base: (0, base // LANES + p)) for base in (COL_QB, COL_KB, COL_VB)]
    return qkv, pl.BlockSpec((F_ROWS, SEQ), lambda p: (0, 0))


def _fox_fwd(proj, f_rows):
    t = FOX_TILE

    def body(q_ref, k_ref, v_ref, f_ref, o_ref, lse_ref):
        pair = pl.program_id(0)
        upper = _upper_half()
        masks = (jnp.logical_not(upper), upper)
        k16, v16 = k_ref[...].astype(BF16), v_ref[...].astype(BF16)
        kms = [jnp.where(hm, k16, 0) for hm in masks]
        vms = [jnp.where(hm, v16, 0) for hm in masks]
        f_row = [f_ref[pl.ds(2 * pair + e, 1), :] for e in range(2)]
        for i in range(FOX_TILES):
            q_tile = (q_ref[i * t:(i + 1) * t, :] * QK_SCALE).astype(BF16)
            outs, lses = [], []
            for e in range(2):
                s, causal = _fox_scores(q_tile, kms[e], f_row[e], i)
                s = jnp.where(causal, s, NEG_INF)
                m = jnp.max(s, axis=-1, keepdims=True)
                p = jnp.exp(s - m)
                den = jnp.sum(p, axis=-1, keepdims=True)
                outs.append(_dot_nn((p / den).astype(BF16), vms[e][:(i + 1) * t]))
                lses.append(m + jnp.log(den))
            o_ref[i * t:(i + 1) * t, :] = outs[0] + outs[1]
            lse_ref[i * t:(i + 1) * t, :] = jnp.where(upper, lses[1], lses[0])

    qkv, f_spec = _fox_specs()
    tok = pl.BlockSpec((SEQ, LANES), lambda p: (0, p))
    return pl.pallas_call(
        body, name="fox_attn_fwd", grid=(FOX_WIDTH // LANES,),
        in_specs=qkv + [f_spec], out_specs=[tok, tok],
        out_shape=[jax.ShapeDtypeStruct((SEQ, FOX_WIDTH), F32)] * 2,
        compiler_params=pltpu.CompilerParams(
            dimension_semantics=("parallel",), vmem_limit_bytes=_vmem_limit(8 * t * SEQ * 4)),
    )(proj, proj, proj, f_rows)


def _fox_bwd(proj, do, lse, f_rows, dproj):
    t = FOX_TILE

    def body(q_ref, k_ref, v_ref, f_ref, do_ref, lse_ref, dproj_in, dproj_ref, df_ref, dk_acc, dv_acc,
             dq_out, dk_out, dv_out, sem):
        pair = pl.program_id(0)
        upper = _upper_half()
        masks = (jnp.logical_not(upper), upper)
        k16, v16 = k_ref[...].astype(BF16), v_ref[...].astype(BF16)
        kms = [jnp.where(hm, k16, 0) for hm in masks]
        vms = [jnp.where(hm, v16, 0) for hm in masks]
        f_row = [f_ref[pl.ds(2 * pair + e, 1), :] for e in range(2)]
        dk_acc[...] = jnp.zeros_like(dk_acc)
        dv_acc[...] = jnp.zeros_like(dv_acc)
        df_ref[...] = jnp.zeros_like(df_ref)
        for i in range(FOX_TILES):
            ext = (i + 1) * t
            q_tile = (q_ref[i * t:(i + 1) * t, :] * QK_SCALE).astype(BF16)
            do_tile = do_ref[i * t:(i + 1) * t, :]
            lse_t = lse_ref[i * t:(i + 1) * t, :]
            dq = None
            for e in range(2):
                s, causal = _fox_scores(q_tile, kms[e], f_row[e], i)
                p = jnp.where(causal, jnp.exp(s - lse_t[:, e * HEAD_DIM:e * HEAD_DIM + 1]), 0.0)
                dp = _dot_nt(do_tile, vms[e][:ext])
                ds = p * (dp - jnp.sum(p * dp, axis=-1, keepdims=True))
                df_ref[0, e:e + 1, :ext] -= jnp.sum(ds, axis=0, keepdims=True)
                ds = ds.astype(BF16)
                part = _dot_nn(ds, kms[e][:ext])
                dq = part if dq is None else dq + part
                dk_acc[:ext, :] += _dot_tn(ds, jnp.where(masks[e], q_tile, 0))
                dv_acc[:ext, :] += _dot_tn(p.astype(BF16), jnp.where(masks[e], do_tile, 0))
            dq_out[i * t:(i + 1) * t, :] = (dq * QK_SCALE).astype(BF16)
        dk_out[...] = dk_acc[...].astype(BF16)
        dv_out[...] = dv_acc[...].astype(BF16)
        _store_columns((dq_out, dk_out, dv_out), dproj_ref, [base // LANES + pair for base in (COL_QB, COL_KB, COL_VB)],
                       sem)

    qkv, f_spec = _fox_specs()
    tok = pl.BlockSpec((SEQ, LANES), lambda p: (0, p))
    return pl.pallas_call(
        body, name="fox_attn_bwd", grid=(FOX_WIDTH // LANES,),
        in_specs=qkv + [f_spec, tok, tok, _ANY],
        out_specs=[_ANY, pl.BlockSpec((1, SUBLANES, SEQ), lambda p: (p, 0, 0))],
        out_shape=[jax.ShapeDtypeStruct(dproj.shape, dproj.dtype),
                   jax.ShapeDtypeStruct((FOX_WIDTH // LANES, SUBLANES, SEQ), F32)],
        scratch_shapes=[pltpu.VMEM((SEQ, LANES), F32)] * 2 + [pltpu.VMEM((SEQ, LANES), BF16)] * 3
        + [pltpu.SemaphoreType.DMA((3,))],
        input_output_aliases={6: 0},
        compiler_params=pltpu.CompilerParams(
            dimension_semantics=("arbitrary",), vmem_limit_bytes=_vmem_limit(10 * t * SEQ * 4)),
    )(proj, proj, proj, f_rows, do, lse, dproj)


MIX_TILE = 256


def _mix_out(out_a, out_b, proj, x, wt_pa, wt_pb, w_out, g_post, g_ffn_pre):
    tm = MIX_TILE

    def body(a_ref, b_ref, ga_ref, gb_ref, x_ref, wpa_ref, wpb_ref, wo_ref, g2_ref, g3_ref,
             merged_ref, mix_ref, x1_ref, h2_ref):
        ya = _dot_nt(a_ref[...].astype(BF16), wpa_ref[...])
        yb = _dot_nt(b_ref[...].astype(BF16), wpb_ref[...])
        merged = (_sigmoid(ga_ref[...]) * ya + _sigmoid(gb_ref[...]) * yb).astype(BF16)
        merged_ref[...] = merged
        mix = _dot_nn(merged, wo_ref[...])
        mix_ref[...] = mix
        x1 = x_ref[...] + mix * _rms_scale(mix) * g2_ref[...]
        x1_ref[...] = x1
        h2_ref[...] = (x1 * _rms_scale(x1) * g3_ref[...]).astype(BF16)

    def rows(w, cb=0):
        return pl.BlockSpec((tm, w), lambda i, cb=cb: (i, cb))

    def whole(a):
        return pl.BlockSpec(a.shape, lambda i: (0, 0))

    d = D_MODEL
    blk = _nbytes((tm, d), F32) * 6 + sum(_nbytes(a.shape, BF16) for a in (wt_pa, wt_pb, w_out))
    return pl.pallas_call(
        body, name="mix_out", grid=(SEQ // tm,),
        in_specs=[rows(DIL_OUT_WIDTH), rows(FOX_WIDTH), rows(d, COL_GA // d), rows(d, COL_GB // d), rows(d),
                  whole(wt_pa), whole(wt_pb), whole(w_out), whole(g_post), whole(g_ffn_pre)],
        out_specs=[rows(d)] * 4,
        out_shape=[jax.ShapeDtypeStruct((SEQ, d), dt) for dt in (BF16, F32, F32, BF16)],
        compiler_params=pltpu.CompilerParams(dimension_semantics=("parallel",), vmem_limit_bytes=_vmem_limit(blk)),
    )(out_a, out_b, proj, proj, x, wt_pa, wt_pb, w_out, g_post, g_ffn_pre)


def _mix_out_bwd(dmix, out_a, out_b, proj, wt_pa, wt_pb, w_out, deps=()):
    tm = MIX_TILE

    def body(dm_ref, a_ref, b_ref, ga_ref, gb_ref, wpa_ref, wpb_ref, wo_ref, *rest):
        dproj_ref, dya_ref, dyb_ref, da_ref, db_ref = rest[len(deps):]
        dmerged = _dot_nt(dm_ref[...], wo_ref[...])
        ya = _dot_nt(a_ref[...].astype(BF16), wpa_ref[...])
        yb = _dot_nt(b_ref[...].astype(BF16), wpb_ref[...])
        sa, sb = _sigmoid(ga_ref[...]), _sigmoid(gb_ref[...])
        dproj_ref[:, COL_GA:COL_GA + D_MODEL] = (dmerged * ya * (sa * (1.0 - sa))).astype(BF16)
        dproj_ref[:, COL_GB:COL_GB + D_MODEL] = (dmerged * yb * (sb * (1.0 - sb))).astype(BF16)
        dproj_ref[:, COL_GB + D_MODEL:] = jnp.zeros((tm, COL_QA - COL_GB - D_MODEL), BF16)
        dya = (dmerged * sa).astype(BF16)
        dyb = (dmerged * sb).astype(BF16)
        dya_ref[...] = dya
        dyb_ref[...] = dyb
        da_ref[...] = _dot_nn(dya, wpa_ref[...])
        db_ref[...] = _dot_nn(dyb, wpb_ref[...]).astype(BF16)

    def rows(w, cb=0):
        return pl.BlockSpec((tm, w), lambda i, cb=cb: (i, cb))

    def whole(a):
        return pl.BlockSpec(a.shape, lambda i: (0, 0))

    d = D_MODEL
    blk = _nbytes((tm, d), F32) * 8 + sum(_nbytes(a.shape, BF16) for a in (wt_pa, wt_pb, w_out))
    return pl.pallas_call(
        body, name="mix_out_bwd", grid=(SEQ // tm,),
        in_specs=[rows(d), rows(DIL_OUT_WIDTH), rows(FOX_WIDTH), rows(d, COL_GA // d), rows(d, COL_GB // d),
                  whole(wt_pa), whole(wt_pb), whole(w_out)] + [_ANY] * len(deps),
        out_specs=[rows(COL_QA)] + [rows(d)] * 2 + [rows(DIL_OUT_WIDTH), rows(FOX_WIDTH)],
        out_shape=[jax.ShapeDtypeStruct((SEQ, PROJ_COLS), BF16)] + [jax.ShapeDtypeStruct((SEQ, d), BF16)] * 2
        + [jax.ShapeDtypeStruct((SEQ, DIL_OUT_WIDTH), F32), jax.ShapeDtypeStruct((SEQ, FOX_WIDTH), BF16)],
        compiler_params=pltpu.CompilerParams(dimension_semantics=("parallel",), vmem_limit_bytes=_vmem_limit(blk)),
    )(dmix, out_a, out_b, proj, proj, wt_pa, wt_pb, w_out, *deps)


FFN_TM, FFN_TN = 1024, 256


def _ffn_up(h2, wt_gate, wt_up):
    tm, tn = FFN_TM, FFN_TN

    def body(h_ref, wg_ref, wu_ref, gate_ref, up_ref, act_ref):
        gate = _dot_nt(h_ref[...], wg_ref[...])
        up = _dot_nt(h_ref[...], wu_ref[...])
        gate_ref[...] = gate
        up_ref[...] = up
        act_ref[...] = (gate * _sigmoid(gate) * up).astype(BF16)

    tile = pl.BlockSpec((tm, tn), lambda i, j: (i, j))
    w_spec = pl.BlockSpec((tn, D_MODEL), lambda i, j: (j, 0))
    return pl.pallas_call(
        body, name="ffn_up", grid=(SEQ // tm, D_FF // tn),
        in_specs=[pl.BlockSpec((tm, D_MODEL), lambda i, j: (i, 0)), w_spec, w_spec],
        out_specs=[tile, tile, tile],
        out_shape=[jax.ShapeDtypeStruct((SEQ, D_FF), dt) for dt in (F32, F32, BF16)],
        compiler_params=pltpu.CompilerParams(
            dimension_semantics=("parallel", "parallel"), vmem_limit_bytes=_vmem_limit(8 * 2**20)),
    )(h2, wt_gate, wt_up)


def _ffn_act_bwd(dff, w_down, gate, up):
    tm, tn = FFN_TM, FFN_TN

    def body(d_ref, wd_ref, gate_ref, up_ref, dgate_ref, dup_ref):
        dact = _dot_nt(d_ref[...], wd_ref[...])
        gate = gate_ref[...]
        sg = _sigmoid(gate)
        dgate_ref[...] = (dact * up_ref[...] * (sg * (1.0 + gate * (1.0 - sg)))).astype(BF16)
        dup_ref[...] = (dact * (gate * sg)).astype(BF16)

    tile = pl.BlockSpec((tm, tn), lambda i, j: (i, j))
    return pl.pallas_call(
        body, name="ffn_act_bwd", grid=(SEQ // tm, D_FF // tn),
        in_specs=[pl.BlockSpec((tm, D_MODEL), lambda i, j: (i, 0)), pl.BlockSpec((tn, D_MODEL), lambda i, j: (j, 0)),
                  tile, tile],
        out_specs=[tile, tile],
        out_shape=[jax.ShapeDtypeStruct((SEQ, D_FF), BF16)] * 2,
        compiler_params=pltpu.CompilerParams(
            dimension_semantics=("parallel", "parallel"), vmem_limit_bytes=_vmem_limit(8 * 2**20)),
    )(dff, w_down, gate, up)


def _loss_head(ff, x1, target, g_post):
    def fn(ff, x1, tgt, g):
        r = _rms_scale(ff)
        nrm = ff * r
        err = (x1 + nrm * g) - tgt
        loss = 0.5 * jnp.sum(jnp.mean(err * err, axis=-1, keepdims=True), axis=0, keepdims=True)
        dy = err * (1.0 / D_MODEL)
        u = dy * g
        dff = r * u - ff * (r * r * r) * jnp.mean(u * ff, axis=-1, keepdims=True)
        return dy, dff, jnp.broadcast_to(loss, (1, LANES)), jnp.sum(dy * nrm, axis=0, keepdims=True)

    d = D_MODEL
    return _rowwise(fn, "loss_head", SEQ, 256, [(ff, d, 0), (x1, d, 0), (target, d, 0)], [g_post],
                    [(d, F32), (d, BF16)], [LANES, d])


def _post_ffn_bwd(dh2, x1, dy, mix, g_ffn_pre, g_mix_post):
    def fn(dh2, x1, dy, mix, g3, g2):
        dx, dg3 = _rms_bwd(x1, dh2, g3)
        dx1 = dy + dx
        dmix, dg2 = _rms_bwd(mix, dx1, g2)
        return dx1, dmix, dg3, dg2

    d = D_MODEL
    return _rowwise(fn, "post_ffn_bwd", SEQ, 256, [(dh2, d, 0), (x1, d, 0), (dy, d, 0), (mix, d, 0)],
                    [g_ffn_pre, g_mix_post], [(d, F32), (d, BF16)], [d, d])


def _input_bwd(dh, x, dx1, g_pre, deps=()):
    def fn(dh, x, dx1, g):
        dx, dg = _rms_bwd(x, dh, g)
        return dx1 + dx, dg

    d = D_MODEL
    return _rowwise(fn, "input_bwd", SEQ, 256, [(dh, d, 0), (x, d, 0), (dx1, d, 0)], [g_pre], [(d, F32)], [d],
                    deps=deps)


def _adam_math(w, g, m, v):
    m = ADAM_B1 * m + (1.0 - ADAM_B1) * g
    v = ADAM_B2 * v + (1.0 - ADAM_B2) * (g * g)
    m_hat = m / (1.0 - ADAM_B1 ** ADAM_STEP)
    v_hat = v / (1.0 - ADAM_B2 ** ADAM_STEP)
    delta = -ADAM_LR * (m_hat / (jnp.sqrt(v_hat) + ADAM_EPS) + ADAM_WD * w)
    return delta, m, v


def _adam(w, g, m, v, name):
    r, c = w.shape
    tc = _col_tile(r, c)

    def body(w_ref, g_ref, m_ref, v_ref, d_ref, nm_ref, nv_ref):
        d_ref[...], nm_ref[...], nv_ref[...] = _adam_math(w_ref[...], g_ref[...], m_ref[...], v_ref[...])

    spec = pl.BlockSpec((r, tc), lambda j: (0, j))
    return pl.pallas_call(
        body, name=name, grid=(c // tc,), in_specs=[spec] * 4, out_specs=[spec] * 3,
        out_shape=[jax.ShapeDtypeStruct((r, c), F32)] * 3,
        compiler_params=pltpu.CompilerParams(dimension_semantics=("parallel",)),
    )(w, g, m, v)


def _adam_small(gathered, ws, ms, vs, loss_parts):
    n = len(ws)

    def body(*refs):
        outs = refs[4 * n + 1:]
        loss = refs[4 * n][0]
        for dev in range(1, N_DEV):
            loss = loss + refs[4 * n][dev]
        outs[4 * n][...] = loss
        for i in range(n):
            ga_ref, w_ref, m_ref, v_ref = (refs[j * n + i] for j in range(4))
            g = ga_ref[0]
            for dev in range(1, N_DEV):
                g = g + ga_ref[dev]
            g = g[:, :w_ref.shape[1]]
            outs[4 * i][...] = g
            outs[4 * i + 1][...], outs[4 * i + 2][...], outs[4 * i + 3][...] = _adam_math(
                w_ref[...], g, m_ref[...], v_ref[...])

    out_shape = [jax.ShapeDtypeStruct(w.shape, F32) for w in ws for _ in range(4)]
    out_shape.append(jax.ShapeDtypeStruct((1, LANES), F32))
    out = pl.pallas_call(body, name="adam_small", out_shape=out_shape)(*gathered, *ws, *ms, *vs, loss_parts)
    return [out[4 * i:4 * i + 4] for i in range(n)], out[4 * n]


_PROJ_SEGMENTS = ((3848, 5896), (None, COL_QA - 2 * D_MODEL), (0, 3840), (3840, 3848), (None, PROJ_COLS - COL_F - 8))


def _proj_weight_t(gathered):
    w = gathered.reshape(IN_COLS, D_MODEL)
    return jnp.concatenate([jnp.zeros((hi, D_MODEL), w.dtype) if lo is None else w[lo:hi] for lo, hi in _PROJ_SEGMENTS],
                           axis=0)


def _proj_weight_grad_slots(dwt_r):
    starts, at = [], 0
    for lo, hi in _PROJ_SEGMENTS:
        if lo is not None:
            starts.append((lo, hi, at))
        at += hi if lo is None else hi - lo
    slots = []
    for dev in range(N_DEV):
        pieces, lo, end = [], dev * IN_SHARD, (dev + 1) * IN_SHARD
        for seg_lo, seg_hi, seg_at in sorted(starts):
            a, b = max(lo, seg_lo), min(end, seg_hi)
            if a < b:
                pieces.append(dwt_r[seg_at + a - seg_lo:seg_at + b - seg_lo])
        slots.append(pieces[0] if len(pieces) == 1 else jnp.concatenate(pieces, axis=0))
    return jnp.stack(slots)


def kernel(x, w_in, w_proj_a, w_proj_b, w_out, b_forget, w_ffn_gate, w_ffn_up, w_ffn_down, norm_mix_pre, norm_mix_post, norm_ffn_pre, norm_ffn_post, loss_target, m_w_in, m_w_proj_a, m_w_proj_b, m_w_out, m_b_forget, m_w_ffn_gate, m_w_ffn_up, m_w_ffn_down, m_norm_mix_pre, m_norm_mix_post, m_norm_ffn_pre, m_norm_ffn_post, v_w_in, v_w_proj_a, v_w_proj_b, v_w_out, v_b_forget, v_w_ffn_gate, v_w_ffn_up, v_w_ffn_down, v_norm_mix_pre, v_norm_mix_post, v_norm_ffn_pre, v_norm_ffn_post):
    d = D_MODEL
    names = ("w_in", "w_proj_a", "w_proj_b", "w_out", "w_ffn_gate", "w_ffn_up", "w_ffn_down")
    col_sharded = ("w_in", "w_proj_a", "w_proj_b", "w_ffn_gate", "w_ffn_up")

    def row_shards(arrs):
        return {k: (a[0].T if k in col_sharded else a[0]) for k, a in zip(names, arrs)}

    shards = row_shards((w_in, w_proj_a, w_proj_b, w_out, w_ffn_gate, w_ffn_up, w_ffn_down))
    moments_m = row_shards((m_w_in, m_w_proj_a, m_w_proj_b, m_w_out, m_w_ffn_gate, m_w_ffn_up, m_w_ffn_down))
    moments_v = row_shards((v_w_in, v_w_proj_a, v_w_proj_b, v_w_out, v_w_ffn_gate, v_w_ffn_up, v_w_ffn_down))
    core = lax.axis_index("c").astype(jnp.int32).reshape(1)
    chip = (2 * lax.axis_index("x") + lax.axis_index("y")).astype(jnp.int32).reshape(1)
    x2, target = x[0], loss_target[0]

    me = 4 * lax.axis_index("x") + 2 * lax.axis_index("y") + lax.axis_index("c")
    mix_names, ffn_names = names[:4], names[4:]
    first_names, later_names = names[:1], names[1:]
    shards16 = {k: shards[k].astype(BF16) for k in names}

    def landing(k):
        return lax.dynamic_update_slice(lax.empty((N_DEV,) + shards[k].shape, BF16), shards16[k][None], (me, 0, 0))

    ag_first = _exchange_start("ag_first_chips_start", _gather_chips_copies, [shards16[k] for k in first_names],
                               [landing(k) for k in first_names], 3 * len(first_names))
    h = _rowwise(lambda xb, g: xb * _rms_scale(xb) * g, "norm_mix_pre", SEQ, 256, [(x2, d, 0)], [norm_mix_pre],
                 [(d, BF16)], deps=[ag_first.token])[0]
    _, lands = _exchange_wait("ag_first_chips_wait", ag_first, [h])
    ag_first = _exchange_start("ag_first_sibling_start", _gather_sibling_copies, [], lands, 4 * len(first_names))
    ag_later = _exchange_start("ag_later_chips_start", _gather_chips_copies, [shards16[k] for k in later_names],
                               [landing(k) for k in later_names], 3 * len(later_names), after=[ag_first.token])
    gathered = dict(zip(first_names, _exchange_wait("ag_first_sibling_wait", ag_first, [ag_later.token])[1]))
    wt_r = _proj_weight_t(gathered["w_in"])

    proj = _matmul([(h, wt_r)], "nt", F32, "in_proj", 1024, 896, 1024)
    tables = _rope_tables()
    o_dil, lse_dil = zip(*[_dil_fwd(g, proj, tables) for g in range(len(DILATIONS))])
    out_a = _dil_combine(o_dil, lse_dil)
    _, lands = _exchange_wait("ag_later_chips_wait", ag_later, [out_a])
    ag_later = _exchange_start("ag_later_sibling_start", _gather_sibling_copies, [], lands, 4 * len(later_names))

    b_pad = jnp.pad(b_forget, ((0, 0), (0, LANES - N_FOX_HEADS)))
    f_rows = _fox_gate(proj, b_pad, deps=[ag_later.token])
    out_b, lse_fox = _fox_fwd(proj, f_rows)

    gathered = dict(zip(later_names, _exchange_wait("ag_later_sibling_wait", ag_later, [out_b])[1]))
    wt_pa = gathered["w_proj_a"].reshape(d, DIL_OUT_WIDTH)
    wt_pb = gathered["w_proj_b"].reshape(d, FOX_WIDTH)
    w_o = gathered["w_out"].reshape(d, d)
    wt_g = gathered["w_ffn_gate"].reshape(D_FF, d)
    wt_u = gathered["w_ffn_up"].reshape(D_FF, d)
    w_d = gathered["w_ffn_down"].reshape(D_FF, d)
    merged, mix, x1, h2 = _mix_out(out_a, out_b, proj, x2, wt_pa, wt_pb, w_o, norm_mix_post, norm_ffn_pre)

    gate, up, act = _ffn_up(h2, wt_g, wt_u)
    ff = _matmul([(act, w_d)], "nn", F32, "ffn_down", 1024, 1024, 1408)
    dy, dff, loss_part, dg_ffn_post = _loss_head(ff, x1, target, norm_ffn_post)

    dgate, dup = _ffn_act_bwd(dff, w_d, gate, up)
    grads_t = {}
    grads_t["w_ffn_down"] = _matmul([(act, dff)], "tn", F32, "grad_w_ffn_down", 1408, 1024, 1024)
    grads_t["w_ffn_gate"] = _matmul([(dgate, h2)], "tn", F32, "grad_w_ffn_gate", 1408, 1024, 1024)
    grads_t["w_ffn_up"] = _matmul([(dup, h2)], "tn", F32, "grad_w_ffn_up", 1408, 1024, 1024)
    rs_ffn = _ReduceScatter("ffn", {k: grads_t[k] for k in ffn_names}, core, chip)
    dh2 = _matmul([(dgate, wt_g), (dup, wt_u)], "nn", F32, "ffn_up_bwd", 1024, 1024, 1408, deps=[rs_ffn.token])
    dx1, dmix, dg_ffn_pre, dg_mix_post = _post_ffn_bwd(dh2, x1, dy, mix, norm_ffn_pre, norm_mix_post)
    rs_ffn.start_chips([dmix])

    dproj, dya, dyb, d_out_a, d_out_b = _mix_out_bwd(dmix, out_a, out_b, proj, wt_pa, wt_pb, w_o, deps=[rs_ffn.token])
    grads_t["w_out"] = _matmul([(merged, dmix)], "tn", F32, "grad_w_out", 1024, 1024, 1024)
    grads_t["w_proj_a"] = _matmul([(dya, out_a)], "tn", F32, "grad_w_proj_a", 1024, DIL_OUT_WIDTH, SEQ)
    grads_t["w_proj_b"] = _matmul([(dyb, out_b)], "tn", F32, "grad_w_proj_b", 1024, FOX_WIDTH, SEQ)

    dproj, d_cum = _fox_bwd(proj, d_out_b, lse_fox, f_rows, dproj)
    d_cum_rows = jnp.pad(d_cum[:, :2].reshape(N_FOX_HEADS, SEQ), ((0, F_ROWS - N_FOX_HEADS), (0, 0)))
    dproj, db_part = _fox_gate_bwd(d_cum_rows, proj, b_pad, dproj)

    do_dil, c_dil = _dil_combine_bwd(d_out_a, o_dil, lse_dil)
    for g in range(len(DILATIONS)):
        dproj = _dil_bwd(g, proj, tables, do_dil[g], lse_dil[g], c_dil[g], dproj)
    dwt_r = _matmul([(dproj, h)], "tn", F32, "grad_w_in", 896, 1024, 1024)
    grads_t["w_in"] = _proj_weight_grad_slots(dwt_r)
    rs_mix = _ReduceScatter("mix", {k: grads_t[k] for k in mix_names}, core, chip)
    grads = rs_ffn.finish([rs_mix.token])
    big = {k: _adam(shards[k], grads[k], moments_m[k], moments_v[k], "adam_" + k) for k in ffn_names}
    rs_mix.start_chips([big[k][0] for k in ffn_names])
    dh = _matmul([(dproj, wt_r)], "nn", F32, "in_proj_bwd", 1024, 1024, 896, deps=[rs_mix.token])
    grad_x, dg_mix_pre = _input_bwd(dh, x2, dx1, norm_mix_pre)

    small_all = _all_gather([dg_mix_pre, dg_mix_post, dg_ffn_pre, dg_ffn_post, db_part, loss_part],
                            "small_grads_all_gather")
    small, loss = _adam_small(small_all[:5], [norm_mix_pre, norm_mix_post, norm_ffn_pre, norm_ffn_post, b_forget],
                              [m_norm_mix_pre, m_norm_mix_post, m_norm_ffn_pre, m_norm_ffn_post, m_b_forget],
                              [v_norm_mix_pre, v_norm_mix_post, v_norm_ffn_pre, v_norm_ffn_post, v_b_forget],
                              small_all[5])

    grads.update(rs_mix.finish([small[0][0], grad_x]))
    big.update({k: _adam(shards[k], grads[k], moments_m[k], moments_v[k], "adam_" + k) for k in mix_names})

    def leaves(i):
        def nat(k):
            a = grads[k] if i == 0 else big[k][i - 1]
            return (a.T if k in col_sharded else a)[None]

        return [nat("w_in"), nat("w_proj_a"), nat("w_proj_b"), nat("w_out"), small[4][i],
                nat("w_ffn_gate"), nat("w_ffn_up"), nat("w_ffn_down"), *[small[r][i] for r in range(4)]]

    return (loss[0, 0], grad_x[None], *leaves(0), *leaves(1), *leaves(2), *leaves(3))
```

```python
import functools
import math

import jax
import jax.numpy as jnp
import numpy as np
from jax import lax
from jax.experimental import pallas as pl
from jax.experimental.pallas import tpu as pltpu

F32 = jnp.float32
BF16 = jnp.bfloat16
MESH = pl.DeviceIdType.MESH

D_MODEL = 1024
SEQ = 2048
HEAD_DIM = 64
BLOCK = 128
N_BLOCKS = SEQ // BLOCK
DILATIONS = (1, 4, 16)
N_FOX_HEADS = 8
DIL_WIDTH = 768
DIL_OUT_WIDTH = 256
FOX_WIDTH = 512
D_FF = 2816
ROPE_THETA = 500000.0
ROPE_DIM = HEAD_DIM // 4
ROPE_HALF = ROPE_DIM // 2
EPS = 1e-6
NEG_INF = -1e30
QK_SCALE = 1.0 / math.sqrt(HEAD_DIM)
IN_COLS = 5896
N_DEV = 8
IN_SHARD = IN_COLS // N_DEV

ADAM_LR = 0.001
ADAM_B1 = 0.9
ADAM_B2 = 0.999
ADAM_EPS = 1e-08
ADAM_WD = 0.01
ADAM_STEP = 10

V7X_VMEM_BYTES = 64 * 2**20
LANES = 128
SUBLANES = 8

PROJ_COLS = 6272
COL_GA, COL_GB = 0, 1024
COL_QA, COL_KA, COL_VA = 2304, 3072, 3840
COL_QB, COL_KB, COL_VB = 4608, 5120, 5632
COL_F = 6144
F_ROWS = 16


def _vmem_limit(block_bytes):
    want = 2 * block_bytes + 16 * 2**20
    return int(min(max(want, 32 * 2**20), V7X_VMEM_BYTES - 8 * 2**20))


def _nbytes(shape, dtype):
    return math.prod(shape) * jnp.dtype(dtype).itemsize


def _dot(a, b, dims):
    return lax.dot_general(a, b, (dims, ((), ())), preferred_element_type=F32)


def _dot_nn(a, b):
    return _dot(a, b, ((1,), (0,)))


def _dot_nt(a, b):
    return _dot(a, b, ((1,), (1,)))


def _dot_tn(a, b):
    return _dot(a, b, ((0,), (0,)))


def _sigmoid(z):
    return 1.0 / (1.0 + jnp.exp(-z))


def _split3(x):
    hi = x.astype(BF16)
    r1 = x - hi.astype(F32)
    mid = r1.astype(BF16)
    lo = (r1 - mid.astype(F32)).astype(BF16)
    return hi, mid, lo


def _dot3_nn(x, ones_matrix):
    hi, mid, lo = _split3(x)
    return (_dot_nn(hi, ones_matrix) + _dot_nn(mid, ones_matrix)) + _dot_nn(lo, ones_matrix)


def _rowwise(fn, name, n_rows, tm, row_ins, bcast_ins, row_outs, acc_outs=(), deps=()):
    n_in = len(row_ins) + len(bcast_ins)
    n_ro = len(row_outs)

    def body(*refs):
        res = fn(*[r[...] for r in refs[:n_in]])
        if not isinstance(res, (tuple, list)):
            res = (res,)
        outs = refs[n_in + len(deps):]
        for r, o in zip(res[:n_ro], outs[:n_ro]):
            o[...] = r.astype(o.dtype)
        first = pl.program_id(0) == 0
        for r, o in zip(res[n_ro:], outs[n_ro:]):
            _accumulate(o, r, first)

    in_specs = [pl.BlockSpec((tm, w), lambda i, cb=cb: (i, cb)) for _, w, cb in row_ins]
    in_specs += [pl.BlockSpec(a.shape, lambda i: (0, 0)) for a in bcast_ins]
    in_specs += [pl.BlockSpec(memory_space=pl.ANY)] * len(deps)
    out_specs = [pl.BlockSpec((tm, w), lambda i: (i, 0)) for w, _ in row_outs]
    out_specs += [pl.BlockSpec((1, w), lambda i: (0, 0)) for w in acc_outs]
    out_shape = [jax.ShapeDtypeStruct((n_rows, w), dt) for w, dt in row_outs]
    out_shape += [jax.ShapeDtypeStruct((1, w), F32) for w in acc_outs]
    blk = sum(_nbytes((tm, w), a.dtype) for a, w, _ in row_ins) + sum(_nbytes((tm, w), dt) for w, dt in row_outs)
    return pl.pallas_call(
        body, name=name, grid=(n_rows // tm,), in_specs=in_specs, out_specs=out_specs, out_shape=out_shape,
        compiler_params=pltpu.CompilerParams(
            dimension_semantics=("arbitrary" if acc_outs else "parallel",), vmem_limit_bytes=_vmem_limit(3 * blk)),
    )(*[a for a, _, _ in row_ins], *bcast_ins, *deps)


def _accumulate(o_ref, part, first):
    @pl.when(first)
    def _():
        o_ref[...] = part

    @pl.when(jnp.logical_not(first))
    def _():
        o_ref[...] += part


_MM_DIMS = {"nn": ((1,), (0,)), "nt": ((1,), (1,)), "tn": ((0,), (0,))}


def _matmul(pairs, mode, out_dtype, name, tm, tn, tk, deps=()):
    a0, b0 = pairs[0]
    if mode == "tn":
        kk, m = a0.shape
    else:
        m, kk = a0.shape
    n = b0.shape[0] if mode == "nt" else b0.shape[1]
    assert m % tm == 0 and n % tn == 0 and kk % tk == 0, (name, m, n, kk)
    nk = kk // tk
    n_pairs = len(pairs)
    dims = _MM_DIMS[mode]
    n_in = 2 * n_pairs + len(deps)

    def body(*refs):
        o_ref = refs[n_in]
        part = None
        for p in range(n_pairs):
            d = _dot(refs[2 * p][...].astype(BF16), refs[2 * p + 1][...].astype(BF16), dims)
            part = d if part is None else part + d
        if nk == 1:
            o_ref[...] = part.astype(o_ref.dtype)
            return
        acc = refs[n_in + 1]
        k = pl.program_id(2)

        @pl.when(k == 0)
        def _():
            acc[...] = part

        @pl.when(k > 0)
        def _():
            acc[...] += part

        @pl.when(k == nk - 1)
        def _():
            o_ref[...] = acc[...].astype(o_ref.dtype)

    if mode == "tn":
        a_spec = pl.BlockSpec((tk, tm), lambda i, j, k: (k, i))
    else:
        a_spec = pl.BlockSpec((tm, tk), lambda i, j, k: (i, k))
    if mode == "nt":
        b_spec = pl.BlockSpec((tn, tk), lambda i, j, k: (j, k))
    else:
        b_spec = pl.BlockSpec((tk, tn), lambda i, j, k: (k, j))
    blk = sum(_nbytes((tm, tk), a.dtype) + _nbytes((tk, tn), b.dtype) for a, b in pairs) + 2 * _nbytes((tm, tn), F32)
    flat = [a for pair in pairs for a in pair]
    return pl.pallas_call(
        body, name=name, grid=(m // tm, n // tn, nk),
        in_specs=[a_spec, b_spec] * n_pairs + [pl.BlockSpec(memory_space=pl.ANY)] * len(deps),
        out_specs=pl.BlockSpec((tm, tn), lambda i, j, k: (i, j)),
        out_shape=jax.ShapeDtypeStruct((m, n), out_dtype),
        scratch_shapes=[] if nk == 1 else [pltpu.VMEM((tm, tn), F32)],
        compiler_params=pltpu.CompilerParams(
            dimension_semantics=("parallel", "parallel", "arbitrary"), vmem_limit_bytes=_vmem_limit(blk)),
    )(*flat, *deps)


def _rms_scale(x):
    return lax.rsqrt(jnp.mean(x * x, axis=-1, keepdims=True) + EPS)


def _rms_bwd(xin, dyn, g):
    r = _rms_scale(xin)
    u = dyn * g
    dx = r * u - xin * (r * r * r) * jnp.mean(u * xin, axis=-1, keepdims=True)
    dg = jnp.sum(dyn * xin * r, axis=0, keepdims=True)
    return dx, dg


def _mesh_pos():
    return lax.axis_index("x"), lax.axis_index("y"), lax.axis_index("c")


def _all_gather(xs, name):
    n = len(xs)

    def body(*refs):
        x_refs, out_refs = refs[:n], refs[n:2 * n]
        send_sems, recv_sems, local_sems = refs[2 * n:]
        mx, my, mc = _mesh_pos()
        me, sib = (mx, my, mc), (mx, my, 1 - mc)
        chips = [(1 - mx, my), (mx, 1 - my), (1 - mx, 1 - my)]

        def slot(a, dev):
            px, py, pc = dev
            return out_refs[a].at[4 * px + 2 * py + pc]

        def copy(k, a, block, to, src=None):
            return pltpu.make_async_remote_copy(
                src_ref=slot(a, block) if src is None else src, dst_ref=slot(a, block),
                send_sem=send_sems.at[a * 7 + k], recv_sem=recv_sems.at[a * 7 + k],
                device_id=to, device_id_type=MESH)

        mine = [pltpu.make_async_copy(x_refs[a], slot(a, me), local_sems.at[a]) for a in range(n)]
        for cp in mine:
            cp.start()
        first = []
        for a in range(n):
            first.append(copy(0, a, me, sib, x_refs[a]))
            first += [copy(1 + j, a, me, (*chip, mc), x_refs[a]) for j, chip in enumerate(chips)]
        for cp in first:
            cp.start()
        passed = []
        for a in range(n):
            for j, chip in enumerate(chips):
                copy(1 + j, a, (*chip, mc), me).wait_recv()
                fwd = copy(4 + j, a, (*chip, mc), sib)
                fwd.start()
                passed.append(fwd)
        for a in range(n):
            copy(0, a, sib, me).wait_recv()
            for j, chip in enumerate(chips):
                copy(4 + j, a, (*chip, 1 - mc), me).wait_recv()
        for cp in first + passed:
            cp.wait_send()
        for cp in mine:
            cp.wait()

    hbm = pl.BlockSpec(memory_space=pl.ANY)
    return pl.pallas_call(
        body, name=name,
        out_shape=[jax.ShapeDtypeStruct((N_DEV,) + x.shape, x.dtype) for x in xs],
        in_specs=[hbm] * n, out_specs=[hbm] * n,
        scratch_shapes=[pltpu.SemaphoreType.DMA((7 * n,)), pltpu.SemaphoreType.DMA((7 * n,)),
                        pltpu.SemaphoreType.DMA((n,))],
    )(*xs)


_HBM = pl.BlockSpec(memory_space=pltpu.HBM)
_SEM = pl.BlockSpec(memory_space=pltpu.SEMAPHORE)
_ANY = pl.BlockSpec(memory_space=pl.ANY)
_DATAFLOW = pltpu.SideEffectType.DATAFLOW_SIDE_EFFECTING


def _flip_peer(flip):
    mx, my, mc = _mesh_pos()
    return (1 - mx if flip & 2 else mx, 1 - my if flip & 1 else my, mc)


def _remote(src, dst, send_sems, recv_sems, k, peer):
    return pltpu.make_async_remote_copy(src_ref=src, dst_ref=dst, send_sem=send_sems.at[k], recv_sem=recv_sems.at[k],
                                        device_id=peer, device_id_type=MESH)


def _gather_chips_copies(srcs, lands, send_sems, recv_sems):
    mx, my, mc = _mesh_pos()
    me = 4 * mx + 2 * my + mc
    return [_remote(srcs[a], lands[a].at[me], send_sems, recv_sems, 3 * a + flip - 1, _flip_peer(flip))
            for a in range(len(srcs)) for flip in (1, 2, 3)]


def _gather_sibling_copies(srcs, lands, send_sems, recv_sems):
    mx, my, mc = _mesh_pos()
    return [_remote(lands[a].at[2 * k + mc], lands[a].at[2 * k + mc], send_sems, recv_sems, 4 * a + k, (mx, my, 1 - mc))
            for a in range(len(lands)) for k in range(4)]


def _scatter_sibling_copies(srcs, lands, send_sems, recv_sems):
    mx, my, mc = _mesh_pos()
    return [_remote(srcs[a].at[k, 1 - mc], lands[a].at[k], send_sems, recv_sems, 4 * a + k, (mx, my, 1 - mc))
            for a in range(len(srcs)) for k in range(4)]


def _scatter_chips_copies(srcs, lands, send_sems, recv_sems):
    mx, my, _ = _mesh_pos()
    k0 = 2 * mx + my
    return [_remote(srcs[a].at[jnp.bitwise_xor(k0, flip)], lands[a].at[flip - 1], send_sems, recv_sems,
                    3 * a + flip - 1, _flip_peer(flip))
            for a in range(len(srcs)) for flip in (1, 2, 3)]


class _Exchange:
    def __init__(self, copies, n_src, send_sems, recv_sems, thru, token):
        self.copies, self.n_src, self.send_sems, self.recv_sems, self.thru, self.token = (
            copies, n_src, send_sems, recv_sems, thru, token)


def _exchange_start(name, copies, srcs, lands, n_copies, after=()):
    bufs = list(srcs) + list(lands)
    nb, ns = len(bufs), len(srcs)

    def body(*refs):
        send_sems, recv_sems = refs[nb + len(after)], refs[nb + len(after) + 1]
        for cp in copies(refs[:ns], refs[ns:nb], send_sems, recv_sems):
            cp.start()
        refs[-1][...] = jnp.zeros_like(refs[-1])

    out = pl.pallas_call(
        body, name=name,
        out_shape=(pltpu.SemaphoreType.DMA((n_copies,)), pltpu.SemaphoreType.DMA((n_copies,)),
                   *[pltpu.HBM(b.shape, b.dtype) for b in bufs], jax.ShapeDtypeStruct((SUBLANES, LANES), F32)),
        in_specs=[_HBM] * nb + [_ANY] * len(after),
        out_specs=(_SEM, _SEM, *[_HBM] * nb, pl.BlockSpec(memory_space=pltpu.VMEM)),
        input_output_aliases={i: 2 + i for i in range(nb)},
        compiler_params=pltpu.CompilerParams(has_side_effects=_DATAFLOW),
    )(*[pltpu.with_memory_space_constraint(b, pltpu.HBM) for b in bufs], *after)
    return _Exchange(copies, ns, out[0], out[1], list(out[2:2 + nb]), out[-1])


def _exchange_wait(name, ex, after):
    nb, ns = len(ex.thru), ex.n_src

    def body(*refs):
        for cp in ex.copies(refs[:ns], refs[ns:nb], refs[nb], refs[nb + 1]):
            cp.wait_send()
            cp.wait_recv()

    out = pl.pallas_call(
        body, name=name, out_shape=tuple(pltpu.HBM(b.shape, b.dtype) for b in ex.thru),
        in_specs=[_HBM] * nb + [_SEM, _SEM] + [_ANY] * len(after), out_specs=tuple([_HBM] * nb),
        input_output_aliases={i: i for i in range(nb)},
        compiler_params=pltpu.CompilerParams(has_side_effects=_DATAFLOW),
    )(*ex.thru, ex.send_sems, ex.recv_sems, *after)
    return list(out[:ns]), list(out[ns:])


def _col_tile(r, c):
    return next(t for t in (1024, 512, 256, 128) if c % t == 0 and (r * t * 4 <= 2**20 or t == 128))


def _add_sibling(g4, recv, core, name):
    _, _, r, c = g4.shape
    tc = _col_tile(r, c)

    def body(core_ref, g_ref, r_ref, o16_ref, o32_ref):
        s = g_ref[0, 0] + r_ref[0]
        o16_ref[0] = s.astype(BF16)
        o32_ref[0] = s

    out = pl.BlockSpec((1, r, tc), lambda k, j, core_ref: (k, 0, j))
    return pl.pallas_call(
        body, name=name,
        out_shape=[jax.ShapeDtypeStruct((4, r, c), BF16), jax.ShapeDtypeStruct((4, r, c), F32)],
        grid_spec=pltpu.PrefetchScalarGridSpec(
            num_scalar_prefetch=1, grid=(4, c // tc),
            in_specs=[pl.BlockSpec((1, 1, r, tc), lambda k, j, core_ref: (k, core_ref[0], 0, j)), out],
            out_specs=[out, out]),
        compiler_params=pltpu.CompilerParams(dimension_semantics=("parallel", "parallel")),
    )(core, g4, recv)


def _add_chips(p32, recv, chip, name):
    _, r, c = p32.shape
    tc = _col_tile(r, c)

    def body(chip_ref, p_ref, r_ref, o_ref):
        o_ref[...] = ((p_ref[0] + r_ref[0].astype(F32)) + r_ref[1].astype(F32)) + r_ref[2].astype(F32)

    return pl.pallas_call(
        body, name=name, out_shape=jax.ShapeDtypeStruct((r, c), F32),
        grid_spec=pltpu.PrefetchScalarGridSpec(
            num_scalar_prefetch=1, grid=(c // tc,),
            in_specs=[pl.BlockSpec((1, r, tc), lambda j, chip_ref: (chip_ref[0], 0, j)),
                      pl.BlockSpec((3, r, tc), lambda j, chip_ref: (0, 0, j))],
            out_specs=pl.BlockSpec((r, tc), lambda j, chip_ref: (0, j))),
        compiler_params=pltpu.CompilerParams(dimension_semantics=("parallel",)),
    )(chip, p32, recv)


class _ReduceScatter:
    def __init__(self, tag, grads_t, core, chip):
        self.tag, self.core, self.chip, self.names = tag, core, chip, list(grads_t)
        g4s = [g.reshape(4, 2, g.size // (N_DEV * g.shape[-1]), g.shape[-1]) for g in grads_t.values()]
        lands = [lax.empty((4,) + g.shape[2:], F32) for g in g4s]
        self.ex = _exchange_start(f"rs_{tag}_sibling_start", _scatter_sibling_copies, g4s, lands, 4 * len(g4s))
        self.token = self.ex.token

    def start_chips(self, after):
        g4s, from_sibling = _exchange_wait(f"rs_{self.tag}_sibling_wait", self.ex, after)
        parts = [_add_sibling(g4, rv, self.core, f"rs_add_sibling_{k}")
                 for k, g4, rv in zip(self.names, g4s, from_sibling)]
        self.p32s = [p32 for _, p32 in parts]
        p16s = [p16 for p16, _ in parts]
        lands = [lax.empty((3,) + p.shape[1:], BF16) for p in p16s]
        self.ex = _exchange_start(f"rs_{self.tag}_chips_start", _scatter_chips_copies, p16s, lands, 3 * len(p16s))
        self.token = self.ex.token

    def finish(self, after):
        _, from_chips = _exchange_wait(f"rs_{self.tag}_chips_wait", self.ex, after)
        return {k: _add_chips(p32, rv, self.chip, f"rs_add_chips_{k}")
                for k, p32, rv in zip(self.names, self.p32s, from_chips)}


def _rope_tables():
    positions = np.arange(SEQ, dtype=np.float32)
    inv_freq = np.power(np.float32(ROPE_THETA), -np.arange(0, ROPE_DIM, 2, dtype=np.float32) / np.float32(ROPE_DIM))
    ang = (positions[:, None] * inv_freq[None, :]).astype(np.float32)
    cos, sin = np.cos(ang).astype(np.float32), np.sin(ang).astype(np.float32)
    ones = np.ones((SEQ, HEAD_DIM - ROPE_DIM), np.float32)
    zeros8 = np.zeros((SEQ, ROPE_HALF), np.float32)
    zeros = np.zeros((SEQ, HEAD_DIM - ROPE_DIM), np.float32)
    c_head = np.concatenate([cos, cos, ones], axis=1)
    s1_head = np.concatenate([-sin, zeros8, zeros], axis=1)
    s2_head = np.concatenate([zeros8, sin, zeros], axis=1)
    return tuple(jnp.asarray(np.concatenate([t, t], axis=1)) for t in (c_head, s1_head, s2_head))


def _rope_apply(x, c, s1, s2):
    w = x.shape[1]
    return x * c + pltpu.roll(x, w - ROPE_HALF, 1) * s1 + pltpu.roll(x, ROPE_HALF, 1) * s2


def _rope_apply_t(dy, c, s1, s2):
    w = dy.shape[1]
    return dy * c + pltpu.roll(dy * s1, ROPE_HALF, 1) + pltpu.roll(dy * s2, w - ROPE_HALF, 1)


def _dil_prev_limit(has_prev):
    return jnp.where(has_prev, 0, BLOCK)


def _dil_valid(limit):
    row = lax.broadcasted_iota(jnp.int32, (BLOCK, 2 * BLOCK), 0)
    col = lax.broadcasted_iota(jnp.int32, (BLOCK, 2 * BLOCK), 1)
    dist = col - row
    return jnp.logical_and(dist >= jnp.where(col < BLOCK, limit, -BLOCK), dist <= BLOCK)


def _upper_half():
    return lax.broadcasted_iota(jnp.int32, (1, LANES), 1) >= HEAD_DIM


def _dil_rows(n, d):
    per = N_BLOCKS // d
    r, lb = n // per, n % per

    def rows(b):
        start = b * (BLOCK * d) + r
        return pl.ds(pl.multiple_of(start, BLOCK), BLOCK) if d == 1 else pl.ds(start, BLOCK, stride=d)

    return rows(lb), rows(jnp.maximum(lb - 1, 0)), lb > 0


def _dil_specs(g):
    def col(base):
        return pl.BlockSpec((SEQ, LANES), lambda p: (0, base // LANES + 2 * g + p))

    table = pl.BlockSpec((SEQ, LANES), lambda p: (0, 0))
    return [col(COL_QA), col(COL_KA), col(COL_VA)], [table] * 3


def _store_columns(blocks, dproj_ref, cols, sem):
    copies = [pltpu.make_async_copy(b, dproj_ref.at[:, pl.ds(pl.multiple_of(c * LANES, LANES), LANES)], sem.at[i])
              for i, (b, c) in enumerate(zip(blocks, cols))]
    for cp in copies:
        cp.start()
    for cp in copies:
        cp.wait()


def _dil_fwd(g, proj, tables):
    d = DILATIONS[g]
    one_block = d == N_BLOCKS

    def body(q_ref, k_ref, v_ref, c_ref, s1_ref, s2_ref, o_ref, lse_ref):
        upper = _upper_half()

        def roped(ref, rows):
            return _rope_apply(ref[rows, :], c_ref[rows, :], s1_ref[rows, :], s2_ref[rows, :])

        def block(n, carry):
            rows, prev, has_prev = _dil_rows(n, d)
            qb = (roped(q_ref, rows) * QK_SCALE).astype(BF16)
            kw, vw = roped(k_ref, rows).astype(BF16), v_ref[rows, :].astype(BF16)
            if one_block:
                row = lax.broadcasted_iota(jnp.int32, (BLOCK, BLOCK), 0)
                valid = lax.broadcasted_iota(jnp.int32, (BLOCK, BLOCK), 1) <= row
            else:
                kw = jnp.concatenate([roped(k_ref, prev).astype(BF16), kw], axis=0)
                vw = jnp.concatenate([v_ref[prev, :].astype(BF16), vw], axis=0)
                valid = _dil_valid(_dil_prev_limit(has_prev))
            outs, lses = [], []
            for head_mask in (jnp.logical_not(upper), upper):
                s = jnp.where(valid, _dot_nt(qb, jnp.where(head_mask, kw, 0)), NEG_INF)
                m = jnp.max(s, axis=-1, keepdims=True)
                p = jnp.exp(s - m)
                den = jnp.sum(p, axis=-1, keepdims=True)
                outs.append(_dot_nn((p / den).astype(BF16), jnp.where(head_mask, vw, 0)))
                lses.append(m + jnp.log(den))
            o_ref[rows, :] = outs[0] + outs[1]
            lse_ref[rows, :] = jnp.where(upper, lses[1], lses[0])
            return carry

        lax.fori_loop(0, N_BLOCKS, block, 0, unroll=2)

    qkv, tabs = _dil_specs(g)
    out = pl.BlockSpec((SEQ, LANES), lambda p: (0, p))
    return pl.pallas_call(
        body, name=f"dil_attn_fwd_{g}", grid=(2,), in_specs=qkv + tabs, out_specs=[out, out],
        out_shape=[jax.ShapeDtypeStruct((SEQ, DIL_OUT_WIDTH), F32)] * 2,
        compiler_params=pltpu.CompilerParams(dimension_semantics=("parallel",)),
    )(proj, proj, proj, *tables)


def _dil_bwd(g, proj, tables, do, lse, c, dproj):
    d = DILATIONS[g]
    one_block = d == N_BLOCKS

    def body(q_ref, k_ref, v_ref, c_ref, s1_ref, s2_ref, do_ref, lse_ref, cc_ref, dproj_in, dproj_ref,
             dq_acc, dk_acc, dv_acc, dq_out, dk_out, dv_out, sem):
        upper = _upper_half()
        dk_acc[...] = jnp.zeros_like(dk_acc)
        dv_acc[...] = jnp.zeros_like(dv_acc)

        def roped(ref, rows):
            return _rope_apply(ref[rows, :], c_ref[rows, :], s1_ref[rows, :], s2_ref[rows, :])

        def block(n, carry):
            rows, prev, has_prev = _dil_rows(n, d)
            qb = (roped(q_ref, rows) * QK_SCALE).astype(BF16)
            dob = do_ref[rows, :].astype(BF16)
            kw, vw = roped(k_ref, rows).astype(BF16), v_ref[rows, :].astype(BF16)
            if one_block:
                row = lax.broadcasted_iota(jnp.int32, (BLOCK, BLOCK), 0)
                valid = lax.broadcasted_iota(jnp.int32, (BLOCK, BLOCK), 1) <= row
            else:
                kw = jnp.concatenate([roped(k_ref, prev).astype(BF16), kw], axis=0)
                vw = jnp.concatenate([v_ref[prev, :].astype(BF16), vw], axis=0)
                valid = _dil_valid(_dil_prev_limit(has_prev))
            lse_t, c_t = lse_ref[rows, :], cc_ref[rows, :]
            dq, dk, dv = None, None, None
            for e, head_mask in enumerate((jnp.logical_not(upper), upper)):
                km, vm = jnp.where(head_mask, kw, 0), jnp.where(head_mask, vw, 0)
                lse_col = lse_t[:, e * HEAD_DIM:e * HEAD_DIM + 1]
                c_col = c_t[:, e * HEAD_DIM:e * HEAD_DIM + 1]
                p = jnp.where(valid, jnp.exp(_dot_nt(qb, km) - lse_col), 0.0)
                ds = (p * (_dot_nt(dob, vm) + c_col)).astype(BF16)
                parts = (_dot_nn(ds, km), _dot_tn(ds, jnp.where(head_mask, qb, 0)),
                         _dot_tn(p.astype(BF16), jnp.where(head_mask, dob, 0)))
                dq, dk, dv = parts if dq is None else (dq + parts[0], dk + parts[1], dv + parts[2])
            dq_acc[rows, :] = dq * QK_SCALE
            if one_block:
                dk_acc[rows, :] += dk
                dv_acc[rows, :] += dv
            else:
                dk_acc[prev, :] += dk[:BLOCK]
                dv_acc[prev, :] += dv[:BLOCK]
                dk_acc[rows, :] += dk[BLOCK:]
                dv_acc[rows, :] += dv[BLOCK:]
            return carry

        lax.fori_loop(0, N_BLOCKS, block, 0, unroll=2)
        tabs = (c_ref[...], s1_ref[...], s2_ref[...])
        dq_out[...] = _rope_apply_t(dq_acc[...], *tabs).astype(BF16)
        dk_out[...] = _rope_apply_t(dk_acc[...], *tabs).astype(BF16)
        dv_out[...] = dv_acc[...].astype(BF16)
        pair = 2 * g + pl.program_id(0)
        _store_columns((dq_out, dk_out, dv_out), dproj_ref,
                       [base // LANES + pair for base in (COL_QA, COL_KA, COL_VA)], sem)

    qkv, tabs = _dil_specs(g)
    tok = pl.BlockSpec((SEQ, LANES), lambda p: (0, p))
    return pl.pallas_call(
        body, name=f"dil_attn_bwd_{g}", grid=(2,),
        in_specs=qkv + tabs + [tok, tok, tok, _ANY], out_specs=_ANY,
        out_shape=jax.ShapeDtypeStruct(dproj.shape, dproj.dtype),
        scratch_shapes=[pltpu.VMEM((SEQ, LANES), F32)] * 3 + [pltpu.VMEM((SEQ, LANES), BF16)] * 3
        + [pltpu.SemaphoreType.DMA((3,))],
        input_output_aliases={9: 0},
        compiler_params=pltpu.CompilerParams(dimension_semantics=("arbitrary",)),
    )(proj, proj, proj, *tables, do, lse, c, dproj)


def _group_weights(l0, l1, l2):
    m = jnp.maximum(jnp.maximum(l0, l1), l2)
    e0, e1, e2 = jnp.exp(l0 - m), jnp.exp(l1 - m), jnp.exp(l2 - m)
    tot = e0 + e1 + e2
    return e0 / tot, e1 / tot, e2 / tot


def _dil_combine(outs, lses, deps=()):
    def fn(o0, o1, o2, l0, l1, l2):
        w0, w1, w2 = _group_weights(l0, l1, l2)
        return w0 * o0 + w1 * o1 + w2 * o2

    w = DIL_OUT_WIDTH
    return _rowwise(fn, "dil_combine", SEQ, 512, [(a, w, 0) for a in list(outs) + list(lses)], [], [(w, F32)],
                    deps=deps)[0]


def _dil_combine_bwd(d_out, outs, lses):
    w = DIL_OUT_WIDTH

    def fn(d, o0, o1, o2, l0, l1, l2):
        row = lax.broadcasted_iota(jnp.int32, (w, w), 0) // HEAD_DIM
        col = lax.broadcasted_iota(jnp.int32, (w, w), 1) // HEAD_DIM
        same_head = jnp.where(row == col, 1.0, 0.0).astype(BF16)
        ws = _group_weights(l0, l1, l2)
        dws = [_dot3_nn(d * og, same_head) for og in (o0, o1, o2)]
        mean = ws[0] * dws[0] + ws[1] * dws[1] + ws[2] * dws[2]
        return tuple(wg * d for wg in ws) + tuple(-wg * mean for wg in ws)

    res = _rowwise(fn, "dil_combine_bwd", SEQ, 256, [(a, w, 0) for a in [d_out] + list(outs) + list(lses)], [],
                   [(w, F32)] * 6)
    return res[:3], res[3:]


def _log1p(e):
    u = 1.0 + e
    return jnp.where(u == 1.0, e, jnp.log(u) * (e / (u - 1.0)))


def _fox_gate(proj, b_pad, deps=()):
    def body(f_ref, b_ref, *rest):
        o_ref = rest[-1]
        z = f_ref[...] + b_ref[...]
        logf = (jnp.minimum(z, 0.0) - _log1p(jnp.exp(-jnp.abs(z)))).T[:F_ROWS]
        row = lax.broadcasted_iota(jnp.int32, (BLOCK, BLOCK), 0)
        col = lax.broadcasted_iota(jnp.int32, (BLOCK, BLOCK), 1)
        before = jnp.where(row <= col, 1.0, 0.0).astype(BF16)
        carry = jnp.zeros((F_ROWS, 1), F32)
        for blk in range(N_BLOCKS):
            run = _dot3_nn(logf[:, blk * BLOCK:(blk + 1) * BLOCK], before) + carry
            o_ref[:, blk * BLOCK:(blk + 1) * BLOCK] = run
            carry = run[:, BLOCK - 1:BLOCK]

    return pl.pallas_call(
        body, name="fox_gate", grid=(1,),
        in_specs=[pl.BlockSpec((SEQ, LANES), lambda i: (0, COL_F // LANES)), pl.BlockSpec((1, LANES), lambda i: (0, 0))]
        + [_ANY] * len(deps),
        out_specs=pl.BlockSpec((F_ROWS, SEQ), lambda i: (0, 0)),
        out_shape=jax.ShapeDtypeStruct((F_ROWS, SEQ), F32),
    )(proj, b_pad, *deps)


def _fox_gate_bwd(d_cum, proj, b_pad, dproj):
    def body(d_ref, f_ref, b_ref, dproj_ref, dz_ref, db_ref):
        row = lax.broadcasted_iota(jnp.int32, (BLOCK, BLOCK), 0)
        col = lax.broadcasted_iota(jnp.int32, (BLOCK, BLOCK), 1)
        after = jnp.where(row >= col, 1.0, 0.0).astype(BF16)
        carry = jnp.zeros((F_ROWS, 1), F32)
        parts = [None] * N_BLOCKS
        for blk in reversed(range(N_BLOCKS)):
            run = _dot3_nn(d_ref[:, blk * BLOCK:(blk + 1) * BLOCK], after) + carry
            parts[blk] = run
            carry = run[:, 0:1]
        dlogf = jnp.concatenate(parts, axis=1)
        dlogf = jnp.concatenate([dlogf, jnp.zeros((LANES - F_ROWS, SEQ), F32)], axis=0).T
        dz = dlogf * _sigmoid(-(f_ref[...] + b_ref[...]))
        dz_ref[...] = dz.astype(BF16)
        db_ref[...] = jnp.sum(dz, axis=0, keepdims=True)

    f_cols = pl.BlockSpec((SEQ, LANES), lambda i: (0, COL_F // LANES))
    return pl.pallas_call(
        body, name="fox_gate_bwd", grid=(1,),
        in_specs=[pl.BlockSpec((F_ROWS, SEQ), lambda i: (0, 0)), f_cols, pl.BlockSpec((1, LANES), lambda i: (0, 0)), _ANY],
        out_specs=[f_cols, pl.BlockSpec((1, LANES), lambda i: (0, 0))],
        out_shape=[jax.ShapeDtypeStruct(dproj.shape, dproj.dtype), jax.ShapeDtypeStruct((1, LANES), F32)],
        input_output_aliases={3: 0},
    )(d_cum, proj, b_pad, dproj)


FOX_TILE = 256
FOX_TILES = SEQ // FOX_TILE


def _row_to_col(row):
    n = row.shape[1]
    eye = lax.broadcasted_iota(jnp.int32, (n, n), 0) == lax.broadcasted_iota(jnp.int32, (n, n), 1)
    return jnp.sum(jnp.where(eye, row, 0.0), axis=1, keepdims=True)


def _fox_scores(q_tile, km, f_row, i):
    t = FOX_TILE
    ext = (i + 1) * t
    f_q = _row_to_col(f_row[:, i * t:(i + 1) * t])
    s = _dot_nt(q_tile, km[:ext]) + (f_q - f_row[:, :ext])
    row = lax.broadcasted_iota(jnp.int32, (t, ext), 0) + i * t
    col = lax.broadcasted_iota(jnp.int32, (t, ext), 1)
    return s, col <= row


def _fox_specs():
    qkv = [pl.BlockSpec((SEQ, LANES), lambda p, base=base: (0, base // LANES + p)) for base in (COL_QB, COL_KB, COL_VB)]
    return qkv, pl.BlockSpec((F_ROWS, SEQ), lambda p: (0, 0))


def _fox_fwd(proj, f_rows):
    t = FOX_TILE

    def body(q_ref, k_ref, v_ref, f_ref, o_ref, lse_ref):
        pair = pl.program_id(0)
        upper = _upper_half()
        masks = (jnp.logical_not(upper), upper)
        k16, v16 = k_ref[...].astype(BF16), v_ref[...].astype(BF16)
        kms = [jnp.where(hm, k16, 0) for hm in masks]
        vms = [jnp.where(hm, v16, 0) for hm in masks]
        f_row = [f_ref[pl.ds(2 * pair + e, 1), :] for e in range(2)]
        for i in range(FOX_TILES):
            q_tile = (q_ref[i * t:(i + 1) * t, :] * QK_SCALE).astype(BF16)
            outs, lses = [], []
            for e in range(2):
                s, causal = _fox_scores(q_tile, kms[e], f_row[e], i)
                s = jnp.where(causal, s, NEG_INF)
                m = jnp.max(s, axis=-1, keepdims=True)
                p = jnp.exp(s - m)
                den = jnp.sum(p, axis=-1, keepdims=True)
                outs.append(_dot_nn((p / den).astype(BF16), vms[e][:(i + 1) * t]))
                lses.append(m + jnp.log(den))
            o_ref[i * t:(i + 1) * t, :] = outs[0] + outs[1]
            lse_ref[i * t:(i + 1) * t, :] = jnp.where(upper, lses[1], lses[0])

    qkv, f_spec = _fox_specs()
    tok = pl.BlockSpec((SEQ, LANES), lambda p: (0, p))
    return pl.pallas_call(
        body, name="fox_attn_fwd", grid=(FOX_WIDTH // LANES,),
        in_specs=qkv + [f_spec], out_specs=[tok, tok],
        out_shape=[jax.ShapeDtypeStruct((SEQ, FOX_WIDTH), F32)] * 2,
        compiler_params=pltpu.CompilerParams(
            dimension_semantics=("parallel",), vmem_limit_bytes=_vmem_limit(8 * t * SEQ * 4)),
    )(proj, proj, proj, f_rows)


def _fox_bwd(proj, do, lse, f_rows, dproj):
    t = FOX_TILE

    def body(q_ref, k_ref, v_ref, f_ref, do_ref, lse_ref, dproj_in, dproj_ref, df_ref, dk_acc, dv_acc,
             dq_out, dk_out, dv_out, sem):
        pair = pl.program_id(0)
        upper = _upper_half()
        masks = (jnp.logical_not(upper), upper)
        k16, v16 = k_ref[...].astype(BF16), v_ref[...].astype(BF16)
        kms = [jnp.where(hm, k16, 0) for hm in masks]
        vms = [jnp.where(hm, v16, 0) for hm in masks]
        f_row = [f_ref[pl.ds(2 * pair + e, 1), :] for e in range(2)]
        dk_acc[...] = jnp.zeros_like(dk_acc)
        dv_acc[...] = jnp.zeros_like(dv_acc)
        df_ref[...] = jnp.zeros_like(df_ref)
        for i in range(FOX_TILES):
            ext = (i + 1) * t
            q_tile = (q_ref[i * t:(i + 1) * t, :] * QK_SCALE).astype(BF16)
            do_tile = do_ref[i * t:(i + 1) * t, :]
            lse_t = lse_ref[i * t:(i + 1) * t, :]
            dq = None
            for e in range(2):
                s, causal = _fox_scores(q_tile, kms[e], f_row[e], i)
                p = jnp.where(causal, jnp.exp(s - lse_t[:, e * HEAD_DIM:e * HEAD_DIM + 1]), 0.0)
                dp = _dot_nt(do_tile, vms[e][:ext])
                ds = p * (dp - jnp.sum(p * dp, axis=-1, keepdims=True))
                df_ref[0, e:e + 1, :ext] -= jnp.sum(ds, axis=0, keepdims=True)
                ds = ds.astype(BF16)
                part = _dot_nn(ds, kms[e][:ext])
                dq = part if dq is None else dq + part
                dk_acc[:ext, :] += _dot_tn(ds, jnp.where(masks[e], q_tile, 0))
                dv_acc[:ext, :] += _dot_tn(p.astype(BF16), jnp.where(masks[e], do_tile, 0))
            dq_out[i * t:(i + 1) * t, :] = (dq * QK_SCALE).astype(BF16)
        dk_out[...] = dk_acc[...].astype(BF16)
        dv_out[...] = dv_acc[...].astype(BF16)
        _store_columns((dq_out, dk_out, dv_out), dproj_ref, [base // LANES + pair for base in (COL_QB, COL_KB, COL_VB)],
                       sem)

    qkv, f_spec = _fox_specs()
    tok = pl.BlockSpec((SEQ, LANES), lambda p: (0, p))
    return pl.pallas_call(
        body, name="fox_attn_bwd", grid=(FOX_WIDTH // LANES,),
        in_specs=qkv + [f_spec, tok, tok, _ANY],
        out_specs=[_ANY, pl.BlockSpec((1, SUBLANES, SEQ), lambda p: (p, 0, 0))],
        out_shape=[jax.ShapeDtypeStruct(dproj.shape, dproj.dtype),
                   jax.ShapeDtypeStruct((FOX_WIDTH // LANES, SUBLANES, SEQ), F32)],
        scratch_shapes=[pltpu.VMEM((SEQ, LANES), F32)] * 2 + [pltpu.VMEM((SEQ, LANES), BF16)] * 3
        + [pltpu.SemaphoreType.DMA((3,))],
        input_output_aliases={6: 0},
        compiler_params=pltpu.CompilerParams(
            dimension_semantics=("arbitrary",), vmem_limit_bytes=_vmem_limit(10 * t * SEQ * 4)),
    )(proj, proj, proj, f_rows, do, lse, dproj)


MIX_TILE = 256


def _mix_out(out_a, out_b, proj, x, wt_pa, wt_pb, w_out, g_post, g_ffn_pre):
    tm = MIX_TILE

    def body(a_ref, b_ref, ga_ref, gb_ref, x_ref, wpa_ref, wpb_ref, wo_ref, g2_ref, g3_ref,
             merged_ref, mix_ref, x1_ref, h2_ref):
        ya = _dot_nt(a_ref[...].astype(BF16), wpa_ref[...])
        yb = _dot_nt(b_ref[...].astype(BF16), wpb_ref[...])
        merged = (_sigmoid(ga_ref[...]) * ya + _sigmoid(gb_ref[...]) * yb).astype(BF16)
        merged_ref[...] = merged
        mix = _dot_nn(merged, wo_ref[...])
        mix_ref[...] = mix
        x1 = x_ref[...] + mix * _rms_scale(mix) * g2_ref[...]
        x1_ref[...] = x1
        h2_ref[...] = (x1 * _rms_scale(x1) * g3_ref[...]).astype(BF16)

    def rows(w, cb=0):
        return pl.BlockSpec((tm, w), lambda i, cb=cb: (i, cb))

    def whole(a):
        return pl.BlockSpec(a.shape, lambda i: (0, 0))

    d = D_MODEL
    blk = _nbytes((tm, d), F32) * 6 + sum(_nbytes(a.shape, BF16) for a in (wt_pa, wt_pb, w_out))
    return pl.pallas_call(
        body, name="mix_out", grid=(SEQ // tm,),
        in_specs=[rows(DIL_OUT_WIDTH), rows(FOX_WIDTH), rows(d, COL_GA // d), rows(d, COL_GB // d), rows(d),
                  whole(wt_pa), whole(wt_pb), whole(w_out), whole(g_post), whole(g_ffn_pre)],
        out_specs=[rows(d)] * 4,
        out_shape=[jax.ShapeDtypeStruct((SEQ, d), dt) for dt in (BF16, F32, F32, BF16)],
        compiler_params=pltpu.CompilerParams(dimension_semantics=("parallel",), vmem_limit_bytes=_vmem_limit(blk)),
    )(out_a, out_b, proj, proj, x, wt_pa, wt_pb, w_out, g_post, g_ffn_pre)


def _mix_out_bwd(dmix, out_a, out_b, proj, wt_pa, wt_pb, w_out, deps=()):
    tm = MIX_TILE

    def body(dm_ref, a_ref, b_ref, ga_ref, gb_ref, wpa_ref, wpb_ref, wo_ref, *rest):
        dproj_ref, dya_ref, dyb_ref, da_ref, db_ref = rest[len(deps):]
        dmerged = _dot_nt(dm_ref[...], wo_ref[...])
        ya = _dot_nt(a_ref[...].astype(BF16), wpa_ref[...])
        yb = _dot_nt(b_ref[...].astype(BF16), wpb_ref[...])
        sa, sb = _sigmoid(ga_ref[...]), _sigmoid(gb_ref[...])
        dproj_ref[:, COL_GA:COL_GA + D_MODEL] = (dmerged * ya * (sa * (1.0 - sa))).astype(BF16)
        dproj_ref[:, COL_GB:COL_GB + D_MODEL] = (dmerged * yb * (sb * (1.0 - sb))).astype(BF16)
        dproj_ref[:, COL_GB + D_MODEL:] = jnp.zeros((tm, COL_QA - COL_GB - D_MODEL), BF16)
        dya = (dmerged * sa).astype(BF16)
        dyb = (dmerged * sb).astype(BF16)
        dya_ref[...] = dya
        dyb_ref[...] = dyb
        da_ref[...] = _dot_nn(dya, wpa_ref[...])
        db_ref[...] = _dot_nn(dyb, wpb_ref[...]).astype(BF16)

    def rows(w, cb=0):
        return pl.BlockSpec((tm, w), lambda i, cb=cb: (i, cb))

    def whole(a):
        return pl.BlockSpec(a.shape, lambda i: (0, 0))

    d = D_MODEL
    blk = _nbytes((tm, d), F32) * 8 + sum(_nbytes(a.shape, BF16) for a in (wt_pa, wt_pb, w_out))
    return pl.pallas_call(
        body, name="mix_out_bwd", grid=(SEQ // tm,),
        in_specs=[rows(d), rows(DIL_OUT_WIDTH), rows(FOX_WIDTH), rows(d, COL_GA // d), rows(d, COL_GB // d),
                  whole(wt_pa), whole(wt_pb), whole(w_out)] + [_ANY] * len(deps),
        out_specs=[rows(COL_QA)] + [rows(d)] * 2 + [rows(DIL_OUT_WIDTH), rows(FOX_WIDTH)],
        out_shape=[jax.ShapeDtypeStruct((SEQ, PROJ_COLS), BF16)] + [jax.ShapeDtypeStruct((SEQ, d), BF16)] * 2
        + [jax.ShapeDtypeStruct((SEQ, DIL_OUT_WIDTH), F32), jax.ShapeDtypeStruct((SEQ, FOX_WIDTH), BF16)],
        compiler_params=pltpu.CompilerParams(dimension_semantics=("parallel",), vmem_limit_bytes=_vmem_limit(blk)),
    )(dmix, out_a, out_b, proj, proj, wt_pa, wt_pb, w_out, *deps)


FFN_TM, FFN_TN = 2048, 256


def _ffn_up(h2, wt_gate, wt_up):
    tm, tn = FFN_TM, FFN_TN

    def body(h_ref, wg_ref, wu_ref, gate_ref, up_ref, act_ref):
        gate = _dot_nt(h_ref[...], wg_ref[...])
        up = _dot_nt(h_ref[...], wu_ref[...])
        gate_ref[...] = gate
        up_ref[...] = up
        act_ref[...] = (gate * _sigmoid(gate) * up).astype(BF16)

    tile = pl.BlockSpec((tm, tn), lambda i, j: (i, j))
    w_spec = pl.BlockSpec((tn, D_MODEL), lambda i, j: (j, 0))
    return pl.pallas_call(
        body, name="ffn_up", grid=(SEQ // tm, D_FF // tn),
        in_specs=[pl.BlockSpec((tm, D_MODEL), lambda i, j: (i, 0)), w_spec, w_spec],
        out_specs=[tile, tile, tile],
        out_shape=[jax.ShapeDtypeStruct((SEQ, D_FF), dt) for dt in (F32, F32, BF16)],
        compiler_params=pltpu.CompilerParams(
            dimension_semantics=("parallel", "parallel"), vmem_limit_bytes=_vmem_limit(8 * 2**20)),
    )(h2, wt_gate, wt_up)


def _ffn_act_bwd(dff, w_down, gate, up):
    tm, tn = FFN_TM, FFN_TN

    def body(d_ref, wd_ref, gate_ref, up_ref, dgate_ref, dup_ref):
        dact = _dot_nt(d_ref[...], wd_ref[...])
        gate = gate_ref[...]
        sg = _sigmoid(gate)
        dgate_ref[...] = (dact * up_ref[...] * (sg * (1.0 + gate * (1.0 - sg)))).astype(BF16)
        dup_ref[...] = (dact * (gate * sg)).astype(BF16)

    tile = pl.BlockSpec((tm, tn), lambda i, j: (i, j))
    return pl.pallas_call(
        body, name="ffn_act_bwd", grid=(SEQ // tm, D_FF // tn),
        in_specs=[pl.BlockSpec((tm, D_MODEL), lambda i, j: (i, 0)), pl.BlockSpec((tn, D_MODEL), lambda i, j: (j, 0)),
                  tile, tile],
        out_specs=[tile, tile],
        out_shape=[jax.ShapeDtypeStruct((SEQ, D_FF), BF16)] * 2,
        compiler_params=pltpu.CompilerParams(
            dimension_semantics=("parallel", "parallel"), vmem_limit_bytes=_vmem_limit(8 * 2**20)),
    )(dff, w_down, gate, up)


def _loss_head(ff, x1, target, g_post):
    def fn(ff, x1, tgt, g):
        r = _rms_scale(ff)
        nrm = ff * r
        err = (x1 + nrm * g) - tgt
        loss = 0.5 * jnp.sum(jnp.mean(err * err, axis=-1, keepdims=True), axis=0, keepdims=True)
        dy = err * (1.0 / D_MODEL)
        u = dy * g
        dff = r * u - ff * (r * r * r) * jnp.mean(u * ff, axis=-1, keepdims=True)
        return dy, dff, jnp.broadcast_to(loss, (1, LANES)), jnp.sum(dy * nrm, axis=0, keepdims=True)

    d = D_MODEL
    return _rowwise(fn, "loss_head", SEQ, 256, [(ff, d, 0), (x1, d, 0), (target, d, 0)], [g_post],
                    [(d, F32), (d, BF16)], [LANES, d])


def _post_ffn_bwd(dh2, x1, dy, mix, g_ffn_pre, g_mix_post):
    def fn(dh2, x1, dy, mix, g3, g2):
        dx, dg3 = _rms_bwd(x1, dh2, g3)
        dx1 = dy + dx
        dmix, dg2 = _rms_bwd(mix, dx1, g2)
        return dx1, dmix, dg3, dg2

    d = D_MODEL
    return _rowwise(fn, "post_ffn_bwd", SEQ, 256, [(dh2, d, 0), (x1, d, 0), (dy, d, 0), (mix, d, 0)],
                    [g_ffn_pre, g_mix_post], [(d, F32), (d, BF16)], [d, d])


def _input_bwd(dh, x, dx1, g_pre, deps=()):
    def fn(dh, x, dx1, g):
        dx, dg = _rms_bwd(x, dh, g)
        return dx1 + dx, dg

    d = D_MODEL
    return _rowwise(fn, "input_bwd", SEQ, 256, [(dh, d, 0), (x, d, 0), (dx1, d, 0)], [g_pre], [(d, F32)], [d],
                    deps=deps)


def _adam_math(w, g, m, v):
    m = ADAM_B1 * m + (1.0 - ADAM_B1) * g
    v = ADAM_B2 * v + (1.0 - ADAM_B2) * (g * g)
    m_hat = m / (1.0 - ADAM_B1 ** ADAM_STEP)
    v_hat = v / (1.0 - ADAM_B2 ** ADAM_STEP)
    delta = -ADAM_LR * (m_hat / (jnp.sqrt(v_hat) + ADAM_EPS) + ADAM_WD * w)
    return delta, m, v


def _adam(w, g, m, v, name):
    r, c = w.shape
    tc = _col_tile(r, c)

    def body(w_ref, g_ref, m_ref, v_ref, d_ref, nm_ref, nv_ref):
        d_ref[...], nm_ref[...], nv_ref[...] = _adam_math(w_ref[...], g_ref[...], m_ref[...], v_ref[...])

    spec = pl.BlockSpec((r, tc), lambda j: (0, j))
    return pl.pallas_call(
        body, name=name, grid=(c // tc,), in_specs=[spec] * 4, out_specs=[spec] * 3,
        out_shape=[jax.ShapeDtypeStruct((r, c), F32)] * 3,
        compiler_params=pltpu.CompilerParams(dimension_semantics=("parallel",)),
    )(w, g, m, v)


def _adam_small(gathered, ws, ms, vs, loss_parts):
    n = len(ws)

    def body(*refs):
        outs = refs[4 * n + 1:]
        loss = refs[4 * n][0]
        for dev in range(1, N_DEV):
            loss = loss + refs[4 * n][dev]
        outs[4 * n][...] = loss
        for i in range(n):
            ga_ref, w_ref, m_ref, v_ref = (refs[j * n + i] for j in range(4))
            g = ga_ref[0]
            for dev in range(1, N_DEV):
                g = g + ga_ref[dev]
            g = g[:, :w_ref.shape[1]]
            outs[4 * i][...] = g
            outs[4 * i + 1][...], outs[4 * i + 2][...], outs[4 * i + 3][...] = _adam_math(
                w_ref[...], g, m_ref[...], v_ref[...])

    out_shape = [jax.ShapeDtypeStruct(w.shape, F32) for w in ws for _ in range(4)]
    out_shape.append(jax.ShapeDtypeStruct((1, LANES), F32))
    out = pl.pallas_call(body, name="adam_small", out_shape=out_shape)(*gathered, *ws, *ms, *vs, loss_parts)
    return [out[4 * i:4 * i + 4] for i in range(n)], out[4 * n]


_PROJ_SEGMENTS = ((3848, 5896), (None, COL_QA - 2 * D_MODEL), (0, 3840), (3840, 3848), (None, PROJ_COLS - COL_F - 8))


def _proj_weight_t(gathered):
    w = gathered.reshape(IN_COLS, D_MODEL)
    return jnp.concatenate([jnp.zeros((hi, D_MODEL), w.dtype) if lo is None else w[lo:hi] for lo, hi in _PROJ_SEGMENTS],
                           axis=0)


def _proj_weight_grad_slots(dwt_r):
    starts, at = [], 0
    for lo, hi in _PROJ_SEGMENTS:
        if lo is not None:
            starts.append((lo, hi, at))
        at += hi if lo is None else hi - lo
    slots = []
    for dev in range(N_DEV):
        pieces, lo, end = [], dev * IN_SHARD, (dev + 1) * IN_SHARD
        for seg_lo, seg_hi, seg_at in sorted(starts):
            a, b = max(lo, seg_lo), min(end, seg_hi)
            if a < b:
                pieces.append(dwt_r[seg_at + a - seg_lo:seg_at + b - seg_lo])
        slots.append(pieces[0] if len(pieces) == 1 else jnp.concatenate(pieces, axis=0))
    return jnp.stack(slots)


def kernel(x, w_in, w_proj_a, w_proj_b, w_out, b_forget, w_ffn_gate, w_ffn_up, w_ffn_down, norm_mix_pre, norm_mix_post, norm_ffn_pre, norm_ffn_post, loss_target, m_w_in, m_w_proj_a, m_w_proj_b, m_w_out, m_b_forget, m_w_ffn_gate, m_w_ffn_up, m_w_ffn_down, m_norm_mix_pre, m_norm_mix_post, m_norm_ffn_pre, m_norm_ffn_post, v_w_in, v_w_proj_a, v_w_proj_b, v_w_out, v_b_forget, v_w_ffn_gate, v_w_ffn_up, v_w_ffn_down, v_norm_mix_pre, v_norm_mix_post, v_norm_ffn_pre, v_norm_ffn_post):
    d = D_MODEL
    names = ("w_in", "w_proj_a", "w_proj_b", "w_out", "w_ffn_gate", "w_ffn_up", "w_ffn_down")
    col_sharded = ("w_in", "w_proj_a", "w_proj_b", "w_ffn_gate", "w_ffn_up")

    def row_shards(arrs):
        return {k: (a[0].T if k in col_sharded else a[0]) for k, a in zip(names, arrs)}

    shards = row_shards((w_in, w_proj_a, w_proj_b, w_out, w_ffn_gate, w_ffn_up, w_ffn_down))
    moments_m = row_shards((m_w_in, m_w_proj_a, m_w_proj_b, m_w_out, m_w_ffn_gate, m_w_ffn_up, m_w_ffn_down))
    moments_v = row_shards((v_w_in, v_w_proj_a, v_w_proj_b, v_w_out, v_w_ffn_gate, v_w_ffn_up, v_w_ffn_down))
    core = lax.axis_index("c").astype(jnp.int32).reshape(1)
    chip = (2 * lax.axis_index("x") + lax.axis_index("y")).astype(jnp.int32).reshape(1)
    x2, target = x[0], loss_target[0]

    me = 4 * lax.axis_index("x") + 2 * lax.axis_index("y") + lax.axis_index("c")
    mix_names, ffn_names = names[:4], names[4:]
    first_names, later_names = names[:1], names[1:]
    shards16 = {k: shards[k].astype(BF16) for k in names}

    def landing(k):
        return lax.dynamic_update_slice(lax.empty((N_DEV,) + shards[k].shape, BF16), shards16[k][None], (me, 0, 0))

    ag_first = _exchange_start("ag_first_chips_start", _gather_chips_copies, [shards16[k] for k in first_names],
                               [landing(k) for k in first_names], 3 * len(first_names))
    h = _rowwise(lambda xb, g: xb * _rms_scale(xb) * g, "norm_mix_pre", SEQ, 256, [(x2, d, 0)], [norm_mix_pre],
                 [(d, BF16)], deps=[ag_first.token])[0]
    _, lands = _exchange_wait("ag_first_chips_wait", ag_first, [h])
    ag_first = _exchange_start("ag_first_sibling_start", _gather_sibling_copies, [], lands, 4 * len(first_names))
    ag_later = _exchange_start("ag_later_chips_start", _gather_chips_copies, [shards16[k] for k in later_names],
                               [landing(k) for k in later_names], 3 * len(later_names), after=[ag_first.token])
    gathered = dict(zip(first_names, _exchange_wait("ag_first_sibling_wait", ag_first, [ag_later.token])[1]))
    wt_r = _proj_weight_t(gathered["w_in"])

    proj = _matmul([(h, wt_r)], "nt", F32, "in_proj", 2048, 896, 1024)
    tables = _rope_tables()
    o_dil, lse_dil = zip(*[_dil_fwd(g, proj, tables) for g in range(len(DILATIONS))])
    out_a = _dil_combine(o_dil, lse_dil)
    _, lands = _exchange_wait("ag_later_chips_wait", ag_later, [out_a])
    ag_later = _exchange_start("ag_later_sibling_start", _gather_sibling_copies, [], lands, 4 * len(later_names))

    b_pad = jnp.pad(b_forget, ((0, 0), (0, LANES - N_FOX_HEADS)))
    f_rows = _fox_gate(proj, b_pad, deps=[ag_later.token])
    out_b, lse_fox = _fox_fwd(proj, f_rows)

    gathered = dict(zip(later_names, _exchange_wait("ag_later_sibling_wait", ag_later, [out_b])[1]))
    wt_pa = gathered["w_proj_a"].reshape(d, DIL_OUT_WIDTH)
    wt_pb = gathered["w_proj_b"].reshape(d, FOX_WIDTH)
    w_o = gathered["w_out"].reshape(d, d)
    wt_g = gathered["w_ffn_gate"].reshape(D_FF, d)
    wt_u = gathered["w_ffn_up"].reshape(D_FF, d)
    w_d = gathered["w_ffn_down"].reshape(D_FF, d)
    merged, mix, x1, h2 = _mix_out(out_a, out_b, proj, x2, wt_pa, wt_pb, w_o, norm_mix_post, norm_ffn_pre)

    gate, up, act = _ffn_up(h2, wt_g, wt_u)
    ff = _matmul([(act, w_d)], "nn", F32, "ffn_down", 2048, 512, 1408)
    dy, dff, loss_part, dg_ffn_post = _loss_head(ff, x1, target, norm_ffn_post)

    dgate, dup = _ffn_act_bwd(dff, w_d, gate, up)
    grads_t = {}
    grads_t["w_ffn_down"] = _matmul([(act, dff)], "tn", F32, "grad_w_ffn_down", 1408, 1024, 2048)
    grads_t["w_ffn_gate"] = _matmul([(dgate, h2)], "tn", F32, "grad_w_ffn_gate", 1408, 1024, 2048)
    grads_t["w_ffn_up"] = _matmul([(dup, h2)], "tn", F32, "grad_w_ffn_up", 1408, 1024, 2048)
    rs_ffn = _ReduceScatter("ffn", {k: grads_t[k] for k in ffn_names}, core, chip)
    dh2 = _matmul([(dgate, wt_g), (dup, wt_u)], "nn", F32, "ffn_up_bwd", 2048, 512, 1408, deps=[rs_ffn.token])
    dx1, dmix, dg_ffn_pre, dg_mix_post = _post_ffn_bwd(dh2, x1, dy, mix, norm_ffn_pre, norm_mix_post)
    rs_ffn.start_chips([dmix])

    dproj, dya, dyb, d_out_a, d_out_b = _mix_out_bwd(dmix, out_a, out_b, proj, wt_pa, wt_pb, w_o, deps=[rs_ffn.token])
    grads_t["w_out"] = _matmul([(merged, dmix)], "tn", F32, "grad_w_out", 1024, 1024, 1024)
    grads_t["w_proj_a"] = _matmul([(dya, out_a)], "tn", F32, "grad_w_proj_a", 1024, DIL_OUT_WIDTH, SEQ)
    grads_t["w_proj_b"] = _matmul([(dyb, out_b)], "tn", F32, "grad_w_proj_b", 1024, FOX_WIDTH, SEQ)

    dproj, d_cum = _fox_bwd(proj, d_out_b, lse_fox, f_rows, dproj)
    d_cum_rows = jnp.pad(d_cum[:, :2].reshape(N_FOX_HEADS, SEQ), ((0, F_ROWS - N_FOX_HEADS), (0, 0)))
    dproj, db_part = _fox_gate_bwd(d_cum_rows, proj, b_pad, dproj)

    do_dil, c_dil = _dil_combine_bwd(d_out_a, o_dil, lse_dil)
    for g in range(len(DILATIONS)):
        dproj = _dil_bwd(g, proj, tables, do_dil[g], lse_dil[g], c_dil[g], dproj)
    dwt_r = _matmul([(dproj, h)], "tn", F32, "grad_w_in", 896, 1024, 2048)
    grads_t["w_in"] = _proj_weight_grad_slots(dwt_r)
    rs_mix = _ReduceScatter("mix", {k: grads_t[k] for k in mix_names}, core, chip)
    grads = rs_ffn.finish([rs_mix.token])
    big = {k: _adam(shards[k], grads[k], moments_m[k], moments_v[k], "adam_" + k) for k in ffn_names}
    rs_mix.start_chips([big[k][0] for k in ffn_names])
    dh = _matmul([(dproj, wt_r)], "nn", F32, "in_proj_bwd", 2048, 512, 896, deps=[rs_mix.token])
    grad_x, dg_mix_pre = _input_bwd(dh, x2, dx1, norm_mix_pre)

    small_all = _all_gather([dg_mix_pre, dg_mix_post, dg_ffn_pre, dg_ffn_post, db_part, loss_part],
                            "small_grads_all_gather")
    small, loss = _adam_small(small_all[:5], [norm_mix_pre, norm_mix_post, norm_ffn_pre, norm_ffn_post, b_forget],
                              [m_norm_mix_pre, m_norm_mix_post, m_norm_ffn_pre, m_norm_ffn_post, m_b_forget],
                              [v_norm_mix_pre, v_norm_mix_post, v_norm_ffn_pre, v_norm_ffn_post, v_b_forget],
                              small_all[5])

    grads.update(rs_mix.finish([small[0][0], grad_x]))
    big.update({k: _adam(shards[k], grads[k], moments_m[k], moments_v[k], "adam_" + k) for k in mix_names})

    def leaves(i):
        def nat(k):
            a = grads[k] if i == 0 else big[k][i - 1]
            return (a.T if k in col_sharded else a)[None]

        return [nat("w_in"), nat("w_proj_a"), nat("w_proj_b"), nat("w_out"), small[4][i],
                nat("w_ffn_gate"), nat("w_ffn_up"), nat("w_ffn_down"), *[small[r][i] for r in range(4)]]

    return (loss[0, 0], grad_x[None], *leaves(0), *leaves(1), *leaves(2), *leaves(3))
```

```python
import functools
import math

import jax
import jax.numpy as jnp
import numpy as np
from jax import lax
from jax.experimental import pallas as pl
from jax.experimental.pallas import tpu as pltpu

F32 = jnp.float32
BF16 = jnp.bfloat16
MESH = pl.DeviceIdType.MESH

D_MODEL = 1024
SEQ = 2048
HEAD_DIM = 64
BLOCK = 128
N_BLOCKS = SEQ // BLOCK
DILATIONS = (1, 4, 16)
N_FOX_HEADS = 8
DIL_WIDTH = 768
DIL_OUT_WIDTH = 256
FOX_WIDTH = 512
D_FF = 2816
ROPE_THETA = 500000.0
ROPE_DIM = HEAD_DIM // 4
ROPE_HALF = ROPE_DIM // 2
EPS = 1e-6
NEG_INF = -1e30
QK_SCALE = 1.0 / math.sqrt(HEAD_DIM)
IN_COLS = 5896
N_DEV = 8
IN_SHARD = IN_COLS // N_DEV

ADAM_LR = 0.001
ADAM_B1 = 0.9
ADAM_B2 = 0.999
ADAM_EPS = 1e-08
ADAM_WD = 0.01
ADAM_STEP = 10

V7X_VMEM_BYTES = 64 * 2**20
LANES = 128
SUBLANES = 8

PROJ_COLS = 6272
COL_GA, COL_GB = 0, 1024
COL_QA, COL_KA, COL_VA = 2304, 3072, 3840
COL_QB, COL_KB, COL_VB = 4608, 5120, 5632
COL_F = 6144
F_ROWS = 16


def _vmem_limit(block_bytes):
    want = 2 * block_bytes + 16 * 2**20
    return int(min(max(want, 32 * 2**20), V7X_VMEM_BYTES - 8 * 2**20))


def _nbytes(shape, dtype):
    return math.prod(shape) * jnp.dtype(dtype).itemsize


def _dot(a, b, dims):
    return lax.dot_general(a, b, (dims, ((), ())), preferred_element_type=F32)


def _dot_nn(a, b):
    return _dot(a, b, ((1,), (0,)))


def _dot_nt(a, b):
    return _dot(a, b, ((1,), (1,)))


def _dot_tn(a, b):
    return _dot(a, b, ((0,), (0,)))


def _sigmoid(z):
    return 1.0 / (1.0 + jnp.exp(-z))


def _split3(x):
    hi = x.astype(BF16)
    r1 = x - hi.astype(F32)
    mid = r1.astype(BF16)
    lo = (r1 - mid.astype(F32)).astype(BF16)
    return hi, mid, lo


def _dot3_nn(x, ones_matrix):
    hi, mid, lo = _split3(x)
    return (_dot_nn(hi, ones_matrix) + _dot_nn(mid, ones_matrix)) + _dot_nn(lo, ones_matrix)


def _rowwise(fn, name, n_rows, tm, row_ins, bcast_ins, row_outs, acc_outs=(), deps=()):
    n_in = len(row_ins) + len(bcast_ins)
    n_ro = len(row_outs)

    def body(*refs):
        res = fn(*[r[...] for r in refs[:n_in]])
        if not isinstance(res, (tuple, list)):
            res = (res,)
        outs = refs[n_in + len(deps):]
        for r, o in zip(res[:n_ro], outs[:n_ro]):
            o[...] = r.astype(o.dtype)
        first = pl.program_id(0) == 0
        for r, o in zip(res[n_ro:], outs[n_ro:]):
            _accumulate(o, r, first)

    in_specs = [pl.BlockSpec((tm, w), lambda i, cb=cb: (i, cb)) for _, w, cb in row_ins]
    in_specs += [pl.BlockSpec(a.shape, lambda i: (0, 0)) for a in bcast_ins]
    in_specs += [pl.BlockSpec(memory_space=pl.ANY)] * len(deps)
    out_specs = [pl.BlockSpec((tm, w), lambda i: (i, 0)) for w, _ in row_outs]
    out_specs += [pl.BlockSpec((1, w), lambda i: (0, 0)) for w in acc_outs]
    out_shape = [jax.ShapeDtypeStruct((n_rows, w), dt) for w, dt in row_outs]
    out_shape += [jax.ShapeDtypeStruct((1, w), F32) for w in acc_outs]
    blk = sum(_nbytes((tm, w), a.dtype) for a, w, _ in row_ins) + sum(_nbytes((tm, w), dt) for w, dt in row_outs)
    return pl.pallas_call(
        body, name=name, grid=(n_rows // tm,), in_specs=in_specs, out_specs=out_specs, out_shape=out_shape,
        compiler_params=pltpu.CompilerParams(
            dimension_semantics=("arbitrary" if acc_outs else "parallel",), vmem_limit_bytes=_vmem_limit(3 * blk)),
    )(*[a for a, _, _ in row_ins], *bcast_ins, *deps)


def _accumulate(o_ref, part, first):
    @pl.when(first)
    def _():
        o_ref[...] = part

    @pl.when(jnp.logical_not(first))
    def _():
        o_ref[...] += part


_MM_DIMS = {"nn": ((1,), (0,)), "nt": ((1,), (1,)), "tn": ((0,), (0,))}


def _matmul(pairs, mode, out_dtype, name, tm, tn, tk, deps=()):
    a0, b0 = pairs[0]
    if mode == "tn":
        kk, m = a0.shape
    else:
        m, kk = a0.shape
    n = b0.shape[0] if mode == "nt" else b0.shape[1]
    assert m % tm == 0 and n % tn == 0 and kk % tk == 0, (name, m, n, kk)
    nk = kk // tk
    n_pairs = len(pairs)
    dims = _MM_DIMS[mode]
    n_in = 2 * n_pairs + len(deps)

    def body(*refs):
        o_ref = refs[n_in]
        part = None
        for p in range(n_pairs):
            d = _dot(refs[2 * p][...].astype(BF16), refs[2 * p + 1][...].astype(BF16), dims)
            part = d if part is None else part + d
        if nk == 1:
            o_ref[...] = part.astype(o_ref.dtype)
            return
        acc = refs[n_in + 1]
        k = pl.program_id(2)

        @pl.when(k == 0)
        def _():
            acc[...] = part

        @pl.when(k > 0)
        def _():
            acc[...] += part

        @pl.when(k == nk - 1)
        def _():
            o_ref[...] = acc[...].astype(o_ref.dtype)

    if mode == "tn":
        a_spec = pl.BlockSpec((tk, tm), lambda i, j, k: (k, i))
    else:
        a_spec = pl.BlockSpec((tm, tk), lambda i, j, k: (i, k))
    if mode == "nt":
        b_spec = pl.BlockSpec((tn, tk), lambda i, j, k: (j, k))
    else:
        b_spec = pl.BlockSpec((tk, tn), lambda i, j, k: (k, j))
    blk = sum(_nbytes((tm, tk), a.dtype) + _nbytes((tk, tn), b.dtype) for a, b in pairs) + 2 * _nbytes((tm, tn), F32)
    flat = [a for pair in pairs for a in pair]
    return pl.pallas_call(
        body, name=name, grid=(m // tm, n // tn, nk),
        in_specs=[a_spec, b_spec] * n_pairs + [pl.BlockSpec(memory_space=pl.ANY)] * len(deps),
        out_specs=pl.BlockSpec((tm, tn), lambda i, j, k: (i, j)),
        out_shape=jax.ShapeDtypeStruct((m, n), out_dtype),
        scratch_shapes=[] if nk == 1 else [pltpu.VMEM((tm, tn), F32)],
        compiler_params=pltpu.CompilerParams(
            dimension_semantics=("parallel", "parallel", "arbitrary"), vmem_limit_bytes=_vmem_limit(blk)),
    )(*flat, *deps)


def _matmul_rowwise(pairs, fn, name, tm, row_ins, bcast_ins, row_outs, acc_outs=(), deps=()):
    m = pairs[0][0].shape[0]
    n_mm, n_in = 2 * len(pairs), len(row_ins) + len(bcast_ins)
    n_ro = len(row_outs)

    def body(*refs):
        prod = None
        for p in range(len(pairs)):
            part = _dot_nn(refs[2 * p][...].astype(BF16), refs[2 * p + 1][...].astype(BF16))
            prod = part if prod is None else prod + part
        res = fn(prod, *[r[...] for r in refs[n_mm:n_mm + n_in]])
        outs = refs[n_mm + n_in + len(deps):]
        for r, o in zip(res[:n_ro], outs[:n_ro]):
            o[...] = r.astype(o.dtype)
        first = pl.program_id(0) == 0
        for r, o in zip(res[n_ro:], outs[n_ro:]):
            _accumulate(o, r, first)

    in_specs = []
    for a, b in pairs:
        in_specs += [pl.BlockSpec((tm, a.shape[1]), lambda i: (i, 0)),
                     pl.BlockSpec(b.shape, lambda i: (0, 0), pipeline_mode=pl.Buffered(1))]
    in_specs += [pl.BlockSpec((tm, w), lambda i, cb=cb: (i, cb)) for _, w, cb in row_ins]
    in_specs += [pl.BlockSpec(a.shape, lambda i: (0, 0)) for a in bcast_ins]
    in_specs += [_ANY] * len(deps)
    out_specs = [pl.BlockSpec((tm, w), lambda i: (i, 0)) for w, _ in row_outs]
    out_specs += [pl.BlockSpec((1, w), lambda i: (0, 0)) for w in acc_outs]
    out_shape = [jax.ShapeDtypeStruct((m, w), dt) for w, dt in row_outs]
    out_shape += [jax.ShapeDtypeStruct((1, w), F32) for w in acc_outs]
    blk = sum(_nbytes((tm, a.shape[1]), a.dtype) + _nbytes(b.shape, b.dtype) // 2 for a, b in pairs)
    blk += sum(_nbytes((tm, w), a.dtype) for a, w, _ in row_ins) + sum(_nbytes((tm, w), dt) for w, dt in row_outs)
    return pl.pallas_call(
        body, name=name, grid=(m // tm,), in_specs=in_specs, out_specs=out_specs, out_shape=out_shape,
        compiler_params=pltpu.CompilerParams(dimension_semantics=("arbitrary",), vmem_limit_bytes=_vmem_limit(blk)),
    )(*[a for pair in pairs for a in pair], *[a for a, _, _ in row_ins], *bcast_ins, *deps)


def _rms_scale(x):
    return lax.rsqrt(jnp.mean(x * x, axis=-1, keepdims=True) + EPS)


def _rms_bwd(xin, dyn, g):
    r = _rms_scale(xin)
    u = dyn * g
    dx = r * u - xin * (r * r * r) * jnp.mean(u * xin, axis=-1, keepdims=True)
    dg = jnp.sum(dyn * xin * r, axis=0, keepdims=True)
    return dx, dg


def _mesh_pos():
    return lax.axis_index("x"), lax.axis_index("y"), lax.axis_index("c")


def _all_gather(xs, name):
    n = len(xs)

    def body(*refs):
        x_refs, out_refs = refs[:n], refs[n:2 * n]
        send_sems, recv_sems, local_sems = refs[2 * n:]
        mx, my, mc = _mesh_pos()
        me, sib = (mx, my, mc), (mx, my, 1 - mc)
        chips = [(1 - mx, my), (mx, 1 - my), (1 - mx, 1 - my)]

        def slot(a, dev):
            px, py, pc = dev
            return out_refs[a].at[4 * px + 2 * py + pc]

        def copy(k, a, block, to, src=None):
            return pltpu.make_async_remote_copy(
                src_ref=slot(a, block) if src is None else src, dst_ref=slot(a, block),
                send_sem=send_sems.at[a * 7 + k], recv_sem=recv_sems.at[a * 7 + k],
                device_id=to, device_id_type=MESH)

        mine = [pltpu.make_async_copy(x_refs[a], slot(a, me), local_sems.at[a]) for a in range(n)]
        for cp in mine:
            cp.start()
        first = []
        for a in range(n):
            first.append(copy(0, a, me, sib, x_refs[a]))
            first += [copy(1 + j, a, me, (*chip, mc), x_refs[a]) for j, chip in enumerate(chips)]
        for cp in first:
            cp.start()
        passed = []
        for a in range(n):
            for j, chip in enumerate(chips):
                copy(1 + j, a, (*chip, mc), me).wait_recv()
                fwd = copy(4 + j, a, (*chip, mc), sib)
                fwd.start()
                passed.append(fwd)
        for a in range(n):
            copy(0, a, sib, me).wait_recv()
            for j, chip in enumerate(chips):
                copy(4 + j, a, (*chip, 1 - mc), me).wait_recv()
        for cp in first + passed:
            cp.wait_send()
        for cp in mine:
            cp.wait()

    hbm = pl.BlockSpec(memory_space=pl.ANY)
    return pl.pallas_call(
        body, name=name,
        out_shape=[jax.ShapeDtypeStruct((N_DEV,) + x.shape, x.dtype) for x in xs],
        in_specs=[hbm] * n, out_specs=[hbm] * n,
        scratch_shapes=[pltpu.SemaphoreType.DMA((7 * n,)), pltpu.SemaphoreType.DMA((7 * n,)),
                        pltpu.SemaphoreType.DMA((n,))],
    )(*xs)


_HBM = pl.BlockSpec(memory_space=pltpu.HBM)
_SEM = pl.BlockSpec(memory_space=pltpu.SEMAPHORE)
_ANY = pl.BlockSpec(memory_space=pl.ANY)
_DATAFLOW = pltpu.SideEffectType.DATAFLOW_SIDE_EFFECTING


def _flip_peer(flip):
    mx, my, mc = _mesh_pos()
    return (1 - mx if flip & 2 else mx, 1 - my if flip & 1 else my, mc)


def _remote(src, dst, send_sems, recv_sems, k, peer):
    return pltpu.make_async_remote_copy(src_ref=src, dst_ref=dst, send_sem=send_sems.at[k], recv_sem=recv_sems.at[k],
                                        device_id=peer, device_id_type=MESH)


def _gather_chips_copies(srcs, lands, send_sems, recv_sems):
    mx, my, mc = _mesh_pos()
    me = 4 * mx + 2 * my + mc
    return [_remote(srcs[a], lands[a].at[me], send_sems, recv_sems, 3 * a + flip - 1, _flip_peer(flip))
            for a in range(len(srcs)) for flip in (1, 2, 3)]


def _gather_sibling_copies(srcs, lands, send_sems, recv_sems):
    mx, my, mc = _mesh_pos()
    return [_remote(lands[a].at[2 * k + mc], lands[a].at[2 * k + mc], send_sems, recv_sems, 4 * a + k, (mx, my, 1 - mc))
            for a in range(len(lands)) for k in range(4)]


def _scatter_sibling_copies(srcs, lands, send_sems, recv_sems):
    mx, my, mc = _mesh_pos()
    return [_remote(srcs[a].at[k, 1 - mc], lands[a].at[k], send_sems, recv_sems, 4 * a + k, (mx, my, 1 - mc))
            for a in range(len(srcs)) for k in range(4)]


def _scatter_chips_copies(srcs, lands, send_sems, recv_sems):
    mx, my, _ = _mesh_pos()
    k0 = 2 * mx + my
    return [_remote(srcs[a].at[jnp.bitwise_xor(k0, flip)], lands[a].at[flip - 1], send_sems, recv_sems,
                    3 * a + flip - 1, _flip_peer(flip))
            for a in range(len(srcs)) for flip in (1, 2, 3)]


class _Exchange:
    def __init__(self, copies, n_src, send_sems, recv_sems, thru, token):
        self.copies, self.n_src, self.send_sems, self.recv_sems, self.thru, self.token = (
            copies, n_src, send_sems, recv_sems, thru, token)


def _exchange_start(name, copies, srcs, lands, n_copies, after=()):
    bufs = list(srcs) + list(lands)
    nb, ns = len(bufs), len(srcs)

    def body(*refs):
        send_sems, recv_sems = refs[nb + len(after)], refs[nb + len(after) + 1]
        for cp in copies(refs[:ns], refs[ns:nb], send_sems, recv_sems):
            cp.start()
        refs[-1][...] = jnp.zeros_like(refs[-1])

    out = pl.pallas_call(
        body, name=name,
        out_shape=(pltpu.SemaphoreType.DMA((n_copies,)), pltpu.SemaphoreType.DMA((n_copies,)),
                   *[pltpu.HBM(b.shape, b.dtype) for b in bufs], jax.ShapeDtypeStruct((SUBLANES, LANES), F32)),
        in_specs=[_HBM] * nb + [_ANY] * len(after),
        out_specs=(_SEM, _SEM, *[_HBM] * nb, pl.BlockSpec(memory_space=pltpu.VMEM)),
        input_output_aliases={i: 2 + i for i in range(nb)},
        compiler_params=pltpu.CompilerParams(has_side_effects=_DATAFLOW),
    )(*[pltpu.with_memory_space_constraint(b, pltpu.HBM) for b in bufs], *after)
    return _Exchange(copies, ns, out[0], out[1], list(out[2:2 + nb]), out[-1])


def _exchange_wait(name, ex, after):
    nb, ns = len(ex.thru), ex.n_src

    def body(*refs):
        for cp in ex.copies(refs[:ns], refs[ns:nb], refs[nb], refs[nb + 1]):
            cp.wait_send()
            cp.wait_recv()

    out = pl.pallas_call(
        body, name=name, out_shape=tuple(pltpu.HBM(b.shape, b.dtype) for b in ex.thru),
        in_specs=[_HBM] * nb + [_SEM, _SEM] + [_ANY] * len(after), out_specs=tuple([_HBM] * nb),
        input_output_aliases={i: i for i in range(nb)},
        compiler_params=pltpu.CompilerParams(has_side_effects=_DATAFLOW),
    )(*ex.thru, ex.send_sems, ex.recv_sems, *after)
    return list(out[:ns]), list(out[ns:])


def _col_tile(r, c):
    return next(t for t in (1024, 512, 256, 128) if c % t == 0 and (r * t * 4 <= 2**20 or t == 128))


def _add_sibling(g4, recv, core, name):
    _, _, r, c = g4.shape
    tc = _col_tile(r, c)

    def body(core_ref, g_ref, r_ref, o16_ref, o32_ref):
        s = g_ref[0, 0] + r_ref[0]
        o16_ref[0] = s.astype(BF16)
        o32_ref[0] = s

    out = pl.BlockSpec((1, r, tc), lambda k, j, core_ref: (k, 0, j))
    return pl.pallas_call(
        body, name=name,
        out_shape=[jax.ShapeDtypeStruct((4, r, c), BF16), jax.ShapeDtypeStruct((4, r, c), F32)],
        grid_spec=pltpu.PrefetchScalarGridSpec(
            num_scalar_prefetch=1, grid=(4, c // tc),
            in_specs=[pl.BlockSpec((1, 1, r, tc), lambda k, j, core_ref: (k, core_ref[0], 0, j)), out],
            out_specs=[out, out]),
        compiler_params=pltpu.CompilerParams(dimension_semantics=("parallel", "parallel")),
    )(core, g4, recv)


def _add_chips(p32, recv, chip, name):
    _, r, c = p32.shape
    tc = _col_tile(r, c)

    def body(chip_ref, p_ref, r_ref, o_ref):
        o_ref[...] = ((p_ref[0] + r_ref[0].astype(F32)) + r_ref[1].astype(F32)) + r_ref[2].astype(F32)

    return pl.pallas_call(
        body, name=name, out_shape=jax.ShapeDtypeStruct((r, c), F32),
        grid_spec=pltpu.PrefetchScalarGridSpec(
            num_scalar_prefetch=1, grid=(c // tc,),
            in_specs=[pl.BlockSpec((1, r, tc), lambda j, chip_ref: (chip_ref[0], 0, j)),
                      pl.BlockSpec((3, r, tc), lambda j, chip_ref: (0, 0, j))],
            out_specs=pl.BlockSpec((r, tc), lambda j, chip_ref: (0, j))),
        compiler_params=pltpu.CompilerParams(dimension_semantics=("parallel",)),
    )(chip, p32, recv)


class _ReduceScatter:
    def __init__(self, tag, grads_t, core, chip):
        self.tag, self.core, self.chip, self.names = tag, core, chip, list(grads_t)
        g4s = [g.reshape(4, 2, g.size // (N_DEV * g.shape[-1]), g.shape[-1]) for g in grads_t.values()]
        lands = [lax.empty((4,) + g.shape[2:], F32) for g in g4s]
        self.ex = _exchange_start(f"rs_{tag}_sibling_start", _scatter_sibling_copies, g4s, lands, 4 * len(g4s))
        self.token = self.ex.token

    def start_chips(self, after):
        g4s, from_sibling = _exchange_wait(f"rs_{self.tag}_sibling_wait", self.ex, after)
        parts = [_add_sibling(g4, rv, self.core, f"rs_add_sibling_{k}")
                 for k, g4, rv in zip(self.names, g4s, from_sibling)]
        self.p32s = [p32 for _, p32 in parts]
        p16s = [p16 for p16, _ in parts]
        lands = [lax.empty((3,) + p.shape[1:], BF16) for p in p16s]
        self.ex = _exchange_start(f"rs_{self.tag}_chips_start", _scatter_chips_copies, p16s, lands, 3 * len(p16s))
        self.token = self.ex.token

    def finish(self, after):
        _, from_chips = _exchange_wait(f"rs_{self.tag}_chips_wait", self.ex, after)
        return {k: _add_chips(p32, rv, self.chip, f"rs_add_chips_{k}")
                for k, p32, rv in zip(self.names, self.p32s, from_chips)}


def _rope_tables():
    positions = np.arange(SEQ, dtype=np.float32)
    inv_freq = np.power(np.float32(ROPE_THETA), -np.arange(0, ROPE_DIM, 2, dtype=np.float32) / np.float32(ROPE_DIM))
    ang = (positions[:, None] * inv_freq[None, :]).astype(np.float32)
    cos, sin = np.cos(ang).astype(np.float32), np.sin(ang).astype(np.float32)
    ones = np.ones((SEQ, HEAD_DIM - ROPE_DIM), np.float32)
    zeros8 = np.zeros((SEQ, ROPE_HALF), np.float32)
    zeros = np.zeros((SEQ, HEAD_DIM - ROPE_DIM), np.float32)
    c_head = np.concatenate([cos, cos, ones], axis=1)
    s1_head = np.concatenate([-sin, zeros8, zeros], axis=1)
    s2_head = np.concatenate([zeros8, sin, zeros], axis=1)
    return tuple(jnp.asarray(np.concatenate([t, t], axis=1)) for t in (c_head, s1_head, s2_head))


def _rope_apply(x, c, s1, s2):
    w = x.shape[1]
    return x * c + pltpu.roll(x, w - ROPE_HALF, 1) * s1 + pltpu.roll(x, ROPE_HALF, 1) * s2


def _rope_apply_t(dy, c, s1, s2):
    w = dy.shape[1]
    return dy * c + pltpu.roll(dy * s1, ROPE_HALF, 1) + pltpu.roll(dy * s2, w - ROPE_HALF, 1)


def _dil_prev_limit(has_prev):
    return jnp.where(has_prev, 0, BLOCK)


def _dil_valid(limit):
    row = lax.broadcasted_iota(jnp.int32, (BLOCK, 2 * BLOCK), 0)
    col = lax.broadcasted_iota(jnp.int32, (BLOCK, 2 * BLOCK), 1)
    dist = col - row
    return jnp.logical_and(dist >= jnp.where(col < BLOCK, limit, -BLOCK), dist <= BLOCK)


def _upper_half():
    return lax.broadcasted_iota(jnp.int32, (1, LANES), 1) >= HEAD_DIM


def _dil_rows(n, d):
    per = N_BLOCKS // d
    r, lb = n // per, n % per

    def rows(b):
        start = b * (BLOCK * d) + r
        return pl.ds(pl.multiple_of(start, BLOCK), BLOCK) if d == 1 else pl.ds(start, BLOCK, stride=d)

    return rows(lb), rows(jnp.maximum(lb - 1, 0)), lb > 0


def _dil_specs(g):
    def col(base):
        return pl.BlockSpec((SEQ, LANES), lambda p: (0, base // LANES + 2 * g + p))

    table = pl.BlockSpec((SEQ, LANES), lambda p: (0, 0))
    return [col(COL_QA), col(COL_KA), col(COL_VA)], [table] * 3


def _store_columns(blocks, dproj_ref, cols, sem):
    copies = [pltpu.make_async_copy(b, dproj_ref.at[:, pl.ds(pl.multiple_of(c * LANES, LANES), LANES)], sem.at[i])
              for i, (b, c) in enumerate(zip(blocks, cols))]
    for cp in copies:
        cp.start()
    for cp in copies:
        cp.wait()


def _dil_fwd(g, proj, tables):
    d = DILATIONS[g]
    one_block = d == N_BLOCKS

    def body(q_ref, k_ref, v_ref, c_ref, s1_ref, s2_ref, o_ref, lse_ref):
        upper = _upper_half()

        def roped(ref, rows):
            return _rope_apply(ref[rows, :], c_ref[rows, :], s1_ref[rows, :], s2_ref[rows, :])

        def block(n, carry):
            rows, prev, has_prev = _dil_rows(n, d)
            qb = (roped(q_ref, rows) * QK_SCALE).astype(BF16)
            kw, vw = roped(k_ref, rows).astype(BF16), v_ref[rows, :].astype(BF16)
            if one_block:
                row = lax.broadcasted_iota(jnp.int32, (BLOCK, BLOCK), 0)
                valid = lax.broadcasted_iota(jnp.int32, (BLOCK, BLOCK), 1) <= row
            else:
                kw = jnp.concatenate([roped(k_ref, prev).astype(BF16), kw], axis=0)
                vw = jnp.concatenate([v_ref[prev, :].astype(BF16), vw], axis=0)
                valid = _dil_valid(_dil_prev_limit(has_prev))
            outs, lses = [], []
            for head_mask in (jnp.logical_not(upper), upper):
                s = jnp.where(valid, _dot_nt(qb, jnp.where(head_mask, kw, 0)), NEG_INF)
                m = jnp.max(s, axis=-1, keepdims=True)
                p = jnp.exp(s - m)
                den = jnp.sum(p, axis=-1, keepdims=True)
                outs.append(_dot_nn((p / den).astype(BF16), jnp.where(head_mask, vw, 0)))
                lses.append(m + jnp.log(den))
            o_ref[rows, :] = outs[0] + outs[1]
            lse_ref[rows, :] = jnp.where(upper, lses[1], lses[0])
            return carry

        lax.fori_loop(0, N_BLOCKS, block, 0, unroll=2)

    qkv, tabs = _dil_specs(g)
    out = pl.BlockSpec((SEQ, LANES), lambda p: (0, p))
    return pl.pallas_call(
        body, name=f"dil_attn_fwd_{g}", grid=(2,), in_specs=qkv + tabs, out_specs=[out, out],
        out_shape=[jax.ShapeDtypeStruct((SEQ, DIL_OUT_WIDTH), F32)] * 2,
        compiler_params=pltpu.CompilerParams(dimension_semantics=("parallel",)),
    )(proj, proj, proj, *tables)


def _dil_bwd(g, proj, tables, do, lse, c, dproj):
    d = DILATIONS[g]
    one_block = d == N_BLOCKS

    def body(q_ref, k_ref, v_ref, c_ref, s1_ref, s2_ref, do_ref, lse_ref, cc_ref, dproj_in, dproj_ref,
             dq_acc, dk_acc, dv_acc, dq_out, dk_out, dv_out, sem):
        upper = _upper_half()
        dk_acc[...] = jnp.zeros_like(dk_acc)
        dv_acc[...] = jnp.zeros_like(dv_acc)

        def roped(ref, rows):
            return _rope_apply(ref[rows, :], c_ref[rows, :], s1_ref[rows, :], s2_ref[rows, :])

        def block(n, carry):
            rows, prev, has_prev = _dil_rows(n, d)
            qb = (roped(q_ref, rows) * QK_SCALE).astype(BF16)
            dob = do_ref[rows, :].astype(BF16)
            kw, vw = roped(k_ref, rows).astype(BF16), v_ref[rows, :].astype(BF16)
            if one_block:
                row = lax.broadcasted_iota(jnp.int32, (BLOCK, BLOCK), 0)
                valid = lax.broadcasted_iota(jnp.int32, (BLOCK, BLOCK), 1) <= row
            else:
                kw = jnp.concatenate([roped(k_ref, prev).astype(BF16), kw], axis=0)
                vw = jnp.concatenate([v_ref[prev, :].astype(BF16), vw], axis=0)
                valid = _dil_valid(_dil_prev_limit(has_prev))
            lse_t, c_t = lse_ref[rows, :], cc_ref[rows, :]
            dq, dk, dv = None, None, None
            for e, head_mask in enumerate((jnp.logical_not(upper), upper)):
                km, vm = jnp.where(head_mask, kw, 0), jnp.where(head_mask, vw, 0)
                lse_col = lse_t[:, e * HEAD_DIM:e * HEAD_DIM + 1]
                c_col = c_t[:, e * HEAD_DIM:e * HEAD_DIM + 1]
                p = jnp.where(valid, jnp.exp(_dot_nt(qb, km) - lse_col), 0.0)
                ds = (p * (_dot_nt(dob, vm) + c_col)).astype(BF16)
                parts = (_dot_nn(ds, km), _dot_tn(ds, jnp.where(head_mask, qb, 0)),
                         _dot_tn(p.astype(BF16), jnp.where(head_mask, dob, 0)))
                dq, dk, dv = parts if dq is None else (dq + parts[0], dk + parts[1], dv + parts[2])
            dq_acc[rows, :] = dq * QK_SCALE
            if one_block:
                dk_acc[rows, :] += dk
                dv_acc[rows, :] += dv
            else:
                dk_acc[prev, :] += dk[:BLOCK]
                dv_acc[prev, :] += dv[:BLOCK]
                dk_acc[rows, :] += dk[BLOCK:]
                dv_acc[rows, :] += dv[BLOCK:]
            return carry

        lax.fori_loop(0, N_BLOCKS, block, 0, unroll=2)
        tabs = (c_ref[...], s1_ref[...], s2_ref[...])
        dq_out[...] = _rope_apply_t(dq_acc[...], *tabs).astype(BF16)
        dk_out[...] = _rope_apply_t(dk_acc[...], *tabs).astype(BF16)
        dv_out[...] = dv_acc[...].astype(BF16)
        pair = 2 * g + pl.program_id(0)
        _store_columns((dq_out, dk_out, dv_out), dproj_ref,
                       [base // LANES + pair for base in (COL_QA, COL_KA, COL_VA)], sem)

    qkv, tabs = _dil_specs(g)
    tok = pl.BlockSpec((SEQ, LANES), lambda p: (0, p))
    return pl.pallas_call(
        body, name=f"dil_attn_bwd_{g}", grid=(2,),
        in_specs=qkv + tabs + [tok, tok, tok, _ANY], out_specs=_ANY,
        out_shape=jax.ShapeDtypeStruct(dproj.shape, dproj.dtype),
        scratch_shapes=[pltpu.VMEM((SEQ, LANES), F32)] * 3 + [pltpu.VMEM((SEQ, LANES), BF16)] * 3
        + [pltpu.SemaphoreType.DMA((3,))],
        input_output_aliases={9: 0},
        compiler_params=pltpu.CompilerParams(dimension_semantics=("arbitrary",)),
    )(proj, proj, proj, *tables, do, lse, c, dproj)


def _group_weights(l0, l1, l2):
    m = jnp.maximum(jnp.maximum(l0, l1), l2)
    e0, e1, e2 = jnp.exp(l0 - m), jnp.exp(l1 - m), jnp.exp(l2 - m)
    tot = e0 + e1 + e2
    return e0 / tot, e1 / tot, e2 / tot


def _dil_combine(outs, lses, deps=()):
    def fn(o0, o1, o2, l0, l1, l2):
        w0, w1, w2 = _group_weights(l0, l1, l2)
        return w0 * o0 + w1 * o1 + w2 * o2

    w = DIL_OUT_WIDTH
    return _rowwise(fn, "dil_combine", SEQ, 512, [(a, w, 0) for a in list(outs) + list(lses)], [], [(w, F32)],
                    deps=deps)[0]


def _dil_combine_bwd(d_out, outs, lses):
    w = DIL_OUT_WIDTH

    def fn(d, o0, o1, o2, l0, l1, l2):
        row = lax.broadcasted_iota(jnp.int32, (w, w), 0) // HEAD_DIM
        col = lax.broadcasted_iota(jnp.int32, (w, w), 1) // HEAD_DIM
        same_head = jnp.where(row == col, 1.0, 0.0).astype(BF16)
        ws = _group_weights(l0, l1, l2)
        dws = [_dot3_nn(d * og, same_head) for og in (o0, o1, o2)]
        mean = ws[0] * dws[0] + ws[1] * dws[1] + ws[2] * dws[2]
        return tuple(wg * d for wg in ws) + tuple(-wg * mean for wg in ws)

    res = _rowwise(fn, "dil_combine_bwd", SEQ, 256, [(a, w, 0) for a in [d_out] + list(outs) + list(lses)], [],
                   [(w, F32)] * 6)
    return res[:3], res[3:]


def _log1p(e):
    u = 1.0 + e
    return jnp.where(u == 1.0, e, jnp.log(u) * (e / (u - 1.0)))


def _fox_gate(proj, b_pad, deps=()):
    def body(f_ref, b_ref, *rest):
        o_ref = rest[-1]
        z = f_ref[...] + b_ref[...]
        logf = (jnp.minimum(z, 0.0) - _log1p(jnp.exp(-jnp.abs(z)))).T[:F_ROWS]
        row = lax.broadcasted_iota(jnp.int32, (BLOCK, BLOCK), 0)
        col = lax.broadcasted_iota(jnp.int32, (BLOCK, BLOCK), 1)
        before = jnp.where(row <= col, 1.0, 0.0).astype(BF16)
        carry = jnp.zeros((F_ROWS, 1), F32)
        for blk in range(N_BLOCKS):
            run = _dot3_nn(logf[:, blk * BLOCK:(blk + 1) * BLOCK], before) + carry
            o_ref[:, blk * BLOCK:(blk + 1) * BLOCK] = run
            carry = run[:, BLOCK - 1:BLOCK]

    return pl.pallas_call(
        body, name="fox_gate", grid=(1,),
        in_specs=[pl.BlockSpec((SEQ, LANES), lambda i: (0, COL_F // LANES)), pl.BlockSpec((1, LANES), lambda i: (0, 0))]
        + [_ANY] * len(deps),
        out_specs=pl.BlockSpec((F_ROWS, SEQ), lambda i: (0, 0)),
        out_shape=jax.ShapeDtypeStruct((F_ROWS, SEQ), F32),
    )(proj, b_pad, *deps)


def _fox_gate_bwd(d_cum, proj, b_pad, dproj):
    def body(d_ref, f_ref, b_ref, dproj_ref, dz_ref, db_ref):
        row = lax.broadcasted_iota(jnp.int32, (BLOCK, BLOCK), 0)
        col = lax.broadcasted_iota(jnp.int32, (BLOCK, BLOCK), 1)
        after = jnp.where(row >= col, 1.0, 0.0).astype(BF16)
        carry = jnp.zeros((F_ROWS, 1), F32)
        parts = [None] * N_BLOCKS
        for blk in reversed(range(N_BLOCKS)):
            run = _dot3_nn(d_ref[:, blk * BLOCK:(blk + 1) * BLOCK], after) + carry
            parts[blk] = run
            carry = run[:, 0:1]
        dlogf = jnp.concatenate(parts, axis=1)
        dlogf = jnp.concatenate([dlogf, jnp.zeros((LANES - F_ROWS, SEQ), F32)], axis=0).T
        dz = dlogf * _sigmoid(-(f_ref[...] + b_ref[...]))
        dz_ref[...] = dz.astype(BF16)
        db_ref[...] = jnp.sum(dz, axis=0, keepdims=True)

    f_cols = pl.BlockSpec((SEQ, LANES), lambda i: (0, COL_F // LANES))
    return pl.pallas_call(
        body, name="fox_gate_bwd", grid=(1,),
        in_specs=[pl.BlockSpec((F_ROWS, SEQ), lambda i: (0, 0)), f_cols, pl.BlockSpec((1, LANES), lambda i: (0, 0)), _ANY],
        out_specs=[f_cols, pl.BlockSpec((1, LANES), lambda i: (0, 0))],
        out_shape=[jax.ShapeDtypeStruct(dproj.shape, dproj.dtype), jax.ShapeDtypeStruct((1, LANES), F32)],
        input_output_aliases={3: 0},
    )(d_cum, proj, b_pad, dproj)


FOX_TILE = 256
FOX_TILES = SEQ // FOX_TILE


def _row_to_col(row):
    n = row.shape[1]
    eye = lax.broadcasted_iota(jnp.int32, (n, n), 0) == lax.broadcasted_iota(jnp.int32, (n, n), 1)
    return jnp.sum(jnp.where(eye, row, 0.0), axis=1, keepdims=True)


def _fox_scores(q_tile, km, f_row, i):
    t = FOX_TILE
    ext = (i + 1) * t
    f_q = _row_to_col(f_row[:, i * t:(i + 1) * t])
    s = _dot_nt(q_tile, km[:ext]) + (f_q - f_row[:, :ext])
    row = lax.broadcasted_iota(jnp.int32, (t, ext), 0) + i * t
    col = lax.broadcasted_iota(jnp.int32, (t, ext), 1)
    return s, col <= row


def _fox_specs():
    qkv = [pl.BlockSpec((SEQ, LANES), lambda p, base=base: (0, base // LANES + p)) for base in (COL_QB, COL_KB, COL_VB)]
    return qkv, pl.BlockSpec((F_ROWS, SEQ), lambda p: (0, 0))


def _fox_fwd(proj, f_rows):
    t = FOX_TILE

    def body(q_ref, k_ref, v_ref, f_ref, o_ref, lse_ref):
        pair = pl.program_id(0)
        upper = _upper_half()
        masks = (jnp.logical_not(upper), upper)
        k16, v16 = k_ref[...].astype(BF16), v_ref[...].astype(BF16)
        kms = [jnp.where(hm, k16, 0) for hm in masks]
        vms = [jnp.where(hm, v16, 0) for hm in masks]
        f_row = [f_ref[pl.ds(2 * pair + e, 1), :] for e in range(2)]
        for i in range(FOX_TILES):
            q_tile = (q_ref[i * t:(i + 1) * t, :] * QK_SCALE).astype(BF16)
            outs, lses = [], []
            for e in range(2):
                s, causal = _fox_scores(q_tile, kms[e], f_row[e], i)
                s = jnp.where(causal, s, NEG_INF)
                m = jnp.max(s, axis=-1, keepdims=True)
                p = jnp.exp(s - m)
                den = jnp.sum(p, axis=-1, keepdims=True)
                outs.append(_dot_nn((p / den).astype(BF16), vms[e][:(i + 1) * t]))
                lses.append(m + jnp.log(den))
            o_ref[i * t:(i + 1) * t, :] = outs[0] + outs[1]
            lse_ref[i * t:(i + 1) * t, :] = jnp.where(upper, lses[1], lses[0])

    qkv, f_spec = _fox_specs()
    tok = pl.BlockSpec((SEQ, LANES), lambda p: (0, p))
    return pl.pallas_call(
        body, name="fox_attn_fwd", grid=(FOX_WIDTH // LANES,),
        in_specs=qkv + [f_spec], out_specs=[tok, tok],
        out_shape=[jax.ShapeDtypeStruct((SEQ, FOX_WIDTH), F32)] * 2,
        compiler_params=pltpu.CompilerParams(
            dimension_semantics=("parallel",), vmem_limit_bytes=_vmem_limit(8 * t * SEQ * 4)),
    )(proj, proj, proj, f_rows)


def _fox_bwd(proj, do, lse, f_rows, dproj):
    t = FOX_TILE

    def body(q_ref, k_ref, v_ref, f_ref, do_ref, lse_ref, dproj_in, dproj_ref, df_ref, dk_acc, dv_acc,
             dq_out, dk_out, dv_out, sem):
        pair = pl.program_id(0)
        upper = _upper_half()
        masks = (jnp.logical_not(upper), upper)
        k16, v16 = k_ref[...].astype(BF16), v_ref[...].astype(BF16)
        kms = [jnp.where(hm, k16, 0) for hm in masks]
        vms = [jnp.where(hm, v16, 0) for hm in masks]
        f_row = [f_ref[pl.ds(2 * pair + e, 1), :] for e in range(2)]
        dk_acc[...] = jnp.zeros_like(dk_acc)
        dv_acc[...] = jnp.zeros_like(dv_acc)
        df_ref[...] = jnp.zeros_like(df_ref)
        for i in range(FOX_TILES):
            ext = (i + 1) * t
            q_tile = (q_ref[i * t:(i + 1) * t, :] * QK_SCALE).astype(BF16)
            do_tile = do_ref[i * t:(i + 1) * t, :]
            lse_t = lse_ref[i * t:(i + 1) * t, :]
            dq = None
            for e in range(2):
                s, causal = _fox_scores(q_tile, kms[e], f_row[e], i)
                p = jnp.where(causal, jnp.exp(s - lse_t[:, e * HEAD_DIM:e * HEAD_DIM + 1]), 0.0)
                dp = _dot_nt(do_tile, vms[e][:ext])
                ds = p * (dp - jnp.sum(p * dp, axis=-1, keepdims=True))
                df_ref[0, e:e + 1, :ext] -= jnp.sum(ds, axis=0, keepdims=True)
                ds = ds.astype(BF16)
                part = _dot_nn(ds, kms[e][:ext])
                dq = part if dq is None else dq + part
                dk_acc[:ext, :] += _dot_tn(ds, jnp.where(masks[e], q_tile, 0))
                dv_acc[:ext, :] += _dot_tn(p.astype(BF16), jnp.where(masks[e], do_tile, 0))
            dq_out[i * t:(i + 1) * t, :] = (dq * QK_SCALE).astype(BF16)
        dk_out[...] = dk_acc[...].astype(BF16)
        dv_out[...] = dv_acc[...].astype(BF16)
        _store_columns((dq_out, dk_out, dv_out), dproj_ref, [base // LANES + pair for base in (COL_QB, COL_KB, COL_VB)],
                       sem)

    qkv, f_spec = _fox_specs()
    tok = pl.BlockSpec((SEQ, LANES), lambda p: (0, p))
    return pl.pallas_call(
        body, name="fox_attn_bwd", grid=(FOX_WIDTH // LANES,),
        in_specs=qkv + [f_spec, tok, tok, _ANY],
        out_specs=[_ANY, pl.BlockSpec((1, SUBLANES, SEQ), lambda p: (p, 0, 0))],
        out_shape=[jax.ShapeDtypeStruct(dproj.shape, dproj.dtype),
                   jax.ShapeDtypeStruct((FOX_WIDTH // LANES, SUBLANES, SEQ), F32)],
        scratch_shapes=[pltpu.VMEM((SEQ, LANES), F32)] * 2 + [pltpu.VMEM((SEQ, LANES), BF16)] * 3
        + [pltpu.SemaphoreType.DMA((3,))],
        input_output_aliases={6: 0},
        compiler_params=pltpu.CompilerParams(
            dimension_semantics=("arbitrary",), vmem_limit_bytes=_vmem_limit(10 * t * SEQ * 4)),
    )(proj, proj, proj, f_rows, do, lse, dproj)


MIX_TILE = 256


def _mix_out(out_a, out_b, proj, x, wt_pa, wt_pb, w_out, g_post, g_ffn_pre):
    tm = MIX_TILE

    def body(a_ref, b_ref, ga_ref, gb_ref, x_ref, wpa_ref, wpb_ref, wo_ref, g2_ref, g3_ref,
             merged_ref, mix_ref, x1_ref, h2_ref):
        ya = _dot_nt(a_ref[...].astype(BF16), wpa_ref[...])
        yb = _dot_nt(b_ref[...].astype(BF16), wpb_ref[...])
        merged = (_sigmoid(ga_ref[...]) * ya + _sigmoid(gb_ref[...]) * yb).astype(BF16)
        merged_ref[...] = merged
        mix = _dot_nn(merged, wo_ref[...])
        mix_ref[...] = mix
        x1 = x_ref[...] + mix * _rms_scale(mix) * g2_ref[...]
        x1_ref[...] = x1
        h2_ref[...] = (x1 * _rms_scale(x1) * g3_ref[...]).astype(BF16)

    def rows(w, cb=0):
        return pl.BlockSpec((tm, w), lambda i, cb=cb: (i, cb))

    def whole(a):
        return pl.BlockSpec(a.shape, lambda i: (0, 0))

    d = D_MODEL
    blk = _nbytes((tm, d), F32) * 6 + sum(_nbytes(a.shape, BF16) for a in (wt_pa, wt_pb, w_out))
    return pl.pallas_call(
        body, name="mix_out", grid=(SEQ // tm,),
        in_specs=[rows(DIL_OUT_WIDTH), rows(FOX_WIDTH), rows(d, COL_GA // d), rows(d, COL_GB // d), rows(d),
                  whole(wt_pa), whole(wt_pb), whole(w_out), whole(g_post), whole(g_ffn_pre)],
        out_specs=[rows(d)] * 4,
        out_shape=[jax.ShapeDtypeStruct((SEQ, d), dt) for dt in (BF16, F32, F32, BF16)],
        compiler_params=pltpu.CompilerParams(dimension_semantics=("parallel",), vmem_limit_bytes=_vmem_limit(blk)),
    )(out_a, out_b, proj, proj, x, wt_pa, wt_pb, w_out, g_post, g_ffn_pre)


def _mix_out_bwd(dmix, out_a, out_b, proj, wt_pa, wt_pb, w_out, deps=()):
    tm = MIX_TILE

    def body(dm_ref, a_ref, b_ref, ga_ref, gb_ref, wpa_ref, wpb_ref, wo_ref, *rest):
        dproj_ref, dya_ref, dyb_ref, da_ref, db_ref = rest[len(deps):]
        dmerged = _dot_nt(dm_ref[...], wo_ref[...])
        ya = _dot_nt(a_ref[...].astype(BF16), wpa_ref[...])
        yb = _dot_nt(b_ref[...].astype(BF16), wpb_ref[...])
        sa, sb = _sigmoid(ga_ref[...]), _sigmoid(gb_ref[...])
        dproj_ref[:, COL_GA:COL_GA + D_MODEL] = (dmerged * ya * (sa * (1.0 - sa))).astype(BF16)
        dproj_ref[:, COL_GB:COL_GB + D_MODEL] = (dmerged * yb * (sb * (1.0 - sb))).astype(BF16)
        dproj_ref[:, COL_GB + D_MODEL:] = jnp.zeros((tm, COL_QA - COL_GB - D_MODEL), BF16)
        dya = (dmerged * sa).astype(BF16)
        dyb = (dmerged * sb).astype(BF16)
        dya_ref[...] = dya
        dyb_ref[...] = dyb
        da_ref[...] = _dot_nn(dya, wpa_ref[...])
        db_ref[...] = _dot_nn(dyb, wpb_ref[...]).astype(BF16)

    def rows(w, cb=0):
        return pl.BlockSpec((tm, w), lambda i, cb=cb: (i, cb))

    def whole(a):
        return pl.BlockSpec(a.shape, lambda i: (0, 0))

    d = D_MODEL
    blk = _nbytes((tm, d), F32) * 8 + sum(_nbytes(a.shape, BF16) for a in (wt_pa, wt_pb, w_out))
    return pl.pallas_call(
        body, name="mix_out_bwd", grid=(SEQ // tm,),
        in_specs=[rows(d), rows(DIL_OUT_WIDTH), rows(FOX_WIDTH), rows(d, COL_GA // d), rows(d, COL_GB // d),
                  whole(wt_pa), whole(wt_pb), whole(w_out)] + [_ANY] * len(deps),
        out_specs=[rows(COL_QA)] + [rows(d)] * 2 + [rows(DIL_OUT_WIDTH), rows(FOX_WIDTH)],
        out_shape=[jax.ShapeDtypeStruct((SEQ, PROJ_COLS), BF16)] + [jax.ShapeDtypeStruct((SEQ, d), BF16)] * 2
        + [jax.ShapeDtypeStruct((SEQ, DIL_OUT_WIDTH), F32), jax.ShapeDtypeStruct((SEQ, FOX_WIDTH), BF16)],
        compiler_params=pltpu.CompilerParams(dimension_semantics=("parallel",), vmem_limit_bytes=_vmem_limit(blk)),
    )(dmix, out_a, out_b, proj, proj, wt_pa, wt_pb, w_out, *deps)


FFN_TM, FFN_TN = 2048, 256


def _ffn_up(h2, wt_gate, wt_up):
    tm, tn = FFN_TM, FFN_TN

    def body(h_ref, wg_ref, wu_ref, gate_ref, up_ref, act_ref):
        gate = _dot_nt(h_ref[...], wg_ref[...])
        up = _dot_nt(h_ref[...], wu_ref[...])
        gate_ref[...] = gate
        up_ref[...] = up
        act_ref[...] = (gate * _sigmoid(gate) * up).astype(BF16)

    tile = pl.BlockSpec((tm, tn), lambda i, j: (i, j))
    w_spec = pl.BlockSpec((tn, D_MODEL), lambda i, j: (j, 0))
    return pl.pallas_call(
        body, name="ffn_up", grid=(SEQ // tm, D_FF // tn),
        in_specs=[pl.BlockSpec((tm, D_MODEL), lambda i, j: (i, 0)), w_spec, w_spec],
        out_specs=[tile, tile, tile],
        out_shape=[jax.ShapeDtypeStruct((SEQ, D_FF), dt) for dt in (F32, F32, BF16)],
        compiler_params=pltpu.CompilerParams(
            dimension_semantics=("parallel", "parallel"), vmem_limit_bytes=_vmem_limit(8 * 2**20)),
    )(h2, wt_gate, wt_up)


def _ffn_act_bwd(dff, w_down, gate, up):
    tm, tn = FFN_TM, FFN_TN

    def body(d_ref, wd_ref, gate_ref, up_ref, dgate_ref, dup_ref):
        dact = _dot_nt(d_ref[...], wd_ref[...])
        gate = gate_ref[...]
        sg = _sigmoid(gate)
        dgate_ref[...] = (dact * up_ref[...] * (sg * (1.0 + gate * (1.0 - sg)))).astype(BF16)
        dup_ref[...] = (dact * (gate * sg)).astype(BF16)

    tile = pl.BlockSpec((tm, tn), lambda i, j: (i, j))
    return pl.pallas_call(
        body, name="ffn_act_bwd", grid=(SEQ // tm, D_FF // tn),
        in_specs=[pl.BlockSpec((tm, D_MODEL), lambda i, j: (i, 0)), pl.BlockSpec((tn, D_MODEL), lambda i, j: (j, 0)),
                  tile, tile],
        out_specs=[tile, tile],
        out_shape=[jax.ShapeDtypeStruct((SEQ, D_FF), BF16)] * 2,
        compiler_params=pltpu.CompilerParams(
            dimension_semantics=("parallel", "parallel"), vmem_limit_bytes=_vmem_limit(8 * 2**20)),
    )(dff, w_down, gate, up)


EPILOGUE_TM = 512


def _loss_head(act, w_down, x1, target, g_post):
    def fn(ff, x1, tgt, g):
        r = _rms_scale(ff)
        nrm = ff * r
        err = (x1 + nrm * g) - tgt
        loss = 0.5 * jnp.sum(jnp.mean(err * err, axis=-1, keepdims=True), axis=0, keepdims=True)
        dy = err * (1.0 / D_MODEL)
        u = dy * g
        dff = r * u - ff * (r * r * r) * jnp.mean(u * ff, axis=-1, keepdims=True)
        return dy, dff, jnp.broadcast_to(loss, (1, LANES)), jnp.sum(dy * nrm, axis=0, keepdims=True)

    d = D_MODEL
    return _matmul_rowwise([(act, w_down)], fn, "ffn_down_loss", EPILOGUE_TM, [(x1, d, 0), (target, d, 0)], [g_post],
                           [(d, F32), (d, BF16)], [LANES, d])


def _post_ffn_bwd(dgate, wt_gate, dup, wt_up, x1, dy, mix, g_ffn_pre, g_mix_post, deps=()):
    def fn(dh2, x1, dy, mix, g3, g2):
        dx, dg3 = _rms_bwd(x1, dh2, g3)
        dx1 = dy + dx
        dmix, dg2 = _rms_bwd(mix, dx1, g2)
        return dx1, dmix, dg3, dg2

    d = D_MODEL
    return _matmul_rowwise([(dgate, wt_gate), (dup, wt_up)], fn, "ffn_up_bwd", EPILOGUE_TM,
                           [(x1, d, 0), (dy, d, 0), (mix, d, 0)], [g_ffn_pre, g_mix_post],
                           [(d, F32), (d, BF16)], [d, d], deps=deps)


def _input_bwd(dproj, wt_r, x, dx1, g_pre, deps=()):
    def fn(dh, x, dx1, g):
        dx, dg = _rms_bwd(x, dh, g)
        return dx1 + dx, dg

    d = D_MODEL
    return _matmul_rowwise([(dproj, wt_r)], fn, "in_proj_bwd", EPILOGUE_TM, [(x, d, 0), (dx1, d, 0)], [g_pre],
                           [(d, F32)], [d], deps=deps)


def _adam_math(w, g, m, v):
    m = ADAM_B1 * m + (1.0 - ADAM_B1) * g
    v = ADAM_B2 * v + (1.0 - ADAM_B2) * (g * g)
    m_hat = m / (1.0 - ADAM_B1 ** ADAM_STEP)
    v_hat = v / (1.0 - ADAM_B2 ** ADAM_STEP)
    delta = -ADAM_LR * (m_hat / (jnp.sqrt(v_hat) + ADAM_EPS) + ADAM_WD * w)
    return delta, m, v


def _adam(w, g, m, v, name):
    r, c = w.shape
    tc = _col_tile(r, c)

    def body(w_ref, g_ref, m_ref, v_ref, d_ref, nm_ref, nv_ref):
        d_ref[...], nm_ref[...], nv_ref[...] = _adam_math(w_ref[...], g_ref[...], m_ref[...], v_ref[...])

    spec = pl.BlockSpec((r, tc), lambda j: (0, j))
    return pl.pallas_call(
        body, name=name, grid=(c // tc,), in_specs=[spec] * 4, out_specs=[spec] * 3,
        out_shape=[jax.ShapeDtypeStruct((r, c), F32)] * 3,
        compiler_params=pltpu.CompilerParams(dimension_semantics=("parallel",)),
    )(w, g, m, v)


def _adam_small(gathered, ws, ms, vs, loss_parts):
    n = len(ws)

    def body(*refs):
        outs = refs[4 * n + 1:]
        loss = refs[4 * n][0]
        for dev in range(1, N_DEV):
            loss = loss + refs[4 * n][dev]
        outs[4 * n][...] = loss
        for i in range(n):
            ga_ref, w_ref, m_ref, v_ref = (refs[j * n + i] for j in range(4))
            g = ga_ref[0]
            for dev in range(1, N_DEV):
                g = g + ga_ref[dev]
            g = g[:, :w_ref.shape[1]]
            outs[4 * i][...] = g
            outs[4 * i + 1][...], outs[4 * i + 2][...], outs[4 * i + 3][...] = _adam_math(
                w_ref[...], g, m_ref[...], v_ref[...])

    out_shape = [jax.ShapeDtypeStruct(w.shape, F32) for w in ws for _ in range(4)]
    out_shape.append(jax.ShapeDtypeStruct((1, LANES), F32))
    out = pl.pallas_call(body, name="adam_small", out_shape=out_shape)(*gathered, *ws, *ms, *vs, loss_parts)
    return [out[4 * i:4 * i + 4] for i in range(n)], out[4 * n]


_PROJ_SEGMENTS = ((3848, 5896), (None, COL_QA - 2 * D_MODEL), (0, 3840), (3840, 3848), (None, PROJ_COLS - COL_F - 8))


def _proj_weight_t(gathered):
    w = gathered.reshape(IN_COLS, D_MODEL)
    return jnp.concatenate([jnp.zeros((hi, D_MODEL), w.dtype) if lo is None else w[lo:hi] for lo, hi in _PROJ_SEGMENTS],
                           axis=0)


def _proj_weight_grad_slots(dwt_r):
    starts, at = [], 0
    for lo, hi in _PROJ_SEGMENTS:
        if lo is not None:
            starts.append((lo, hi, at))
        at += hi if lo is None else hi - lo
    slots = []
    for dev in range(N_DEV):
        pieces, lo, end = [], dev * IN_SHARD, (dev + 1) * IN_SHARD
        for seg_lo, seg_hi, seg_at in sorted(starts):
            a, b = max(lo, seg_lo), min(end, seg_hi)
            if a < b:
                pieces.append(dwt_r[seg_at + a - seg_lo:seg_at + b - seg_lo])
        slots.append(pieces[0] if len(pieces) == 1 else jnp.concatenate(pieces, axis=0))
    return jnp.stack(slots)


def kernel(x, w_in, w_proj_a, w_proj_b, w_out, b_forget, w_ffn_gate, w_ffn_up, w_ffn_down, norm_mix_pre, norm_mix_post, norm_ffn_pre, norm_ffn_post, loss_target, m_w_in, m_w_proj_a, m_w_proj_b, m_w_out, m_b_forget, m_w_ffn_gate, m_w_ffn_up, m_w_ffn_down, m_norm_mix_pre, m_norm_mix_post, m_norm_ffn_pre, m_norm_ffn_post, v_w_in, v_w_proj_a, v_w_proj_b, v_w_out, v_b_forget, v_w_ffn_gate, v_w_ffn_up, v_w_ffn_down, v_norm_mix_pre, v_norm_mix_post, v_norm_ffn_pre, v_norm_ffn_post):
    d = D_MODEL
    names = ("w_in", "w_proj_a", "w_proj_b", "w_out", "w_ffn_gate", "w_ffn_up", "w_ffn_down")
    col_sharded = ("w_in", "w_proj_a", "w_proj_b", "w_ffn_gate", "w_ffn_up")

    def row_shards(arrs):
        return {k: (a[0].T if k in col_sharded else a[0]) for k, a in zip(names, arrs)}

    shards = row_shards((w_in, w_proj_a, w_proj_b, w_out, w_ffn_gate, w_ffn_up, w_ffn_down))
    moments_m = row_shards((m_w_in, m_w_proj_a, m_w_proj_b, m_w_out, m_w_ffn_gate, m_w_ffn_up, m_w_ffn_down))
    moments_v = row_shards((v_w_in, v_w_proj_a, v_w_proj_b, v_w_out, v_w_ffn_gate, v_w_ffn_up, v_w_ffn_down))
    core = lax.axis_index("c").astype(jnp.int32).reshape(1)
    chip = (2 * lax.axis_index("x") + lax.axis_index("y")).astype(jnp.int32).reshape(1)
    x2, target = x[0], loss_target[0]

    me = 4 * lax.axis_index("x") + 2 * lax.axis_index("y") + lax.axis_index("c")
    mix_names, ffn_names = names[:4], names[4:]
    first_names, later_names = names[:1], names[1:]
    shards16 = {k: shards[k].astype(BF16) for k in names}

    def landing(k):
        return lax.dynamic_update_slice(lax.empty((N_DEV,) + shards[k].shape, BF16), shards16[k][None], (me, 0, 0))

    ag_first = _exchange_start("ag_first_chips_start", _gather_chips_copies, [shards16[k] for k in first_names],
                               [landing(k) for k in first_names], 3 * len(first_names))
    h = _rowwise(lambda xb, g: xb * _rms_scale(xb) * g, "norm_mix_pre", SEQ, 256, [(x2, d, 0)], [norm_mix_pre],
                 [(d, BF16)], deps=[ag_first.token])[0]
    _, lands = _exchange_wait("ag_first_chips_wait", ag_first, [h])
    ag_first = _exchange_start("ag_first_sibling_start", _gather_sibling_copies, [], lands, 4 * len(first_names))
    ag_later = _exchange_start("ag_later_chips_start", _gather_chips_copies, [shards16[k] for k in later_names],
                               [landing(k) for k in later_names], 3 * len(later_names), after=[ag_first.token])
    gathered = dict(zip(first_names, _exchange_wait("ag_first_sibling_wait", ag_first, [ag_later.token])[1]))
    wt_r = _proj_weight_t(gathered["w_in"])

    proj = _matmul([(h, wt_r)], "nt", F32, "in_proj", 1024, 896, 1024)
    tables = _rope_tables()
    o_dil, lse_dil = zip(*[_dil_fwd(g, proj, tables) for g in range(len(DILATIONS))])
    out_a = _dil_combine(o_dil, lse_dil)
    _, lands = _exchange_wait("ag_later_chips_wait", ag_later, [out_a])
    ag_later = _exchange_start("ag_later_sibling_start", _gather_sibling_copies, [], lands, 4 * len(later_names))

    b_pad = jnp.pad(b_forget, ((0, 0), (0, LANES - N_FOX_HEADS)))
    f_rows = _fox_gate(proj, b_pad, deps=[ag_later.token])
    out_b, lse_fox = _fox_fwd(proj, f_rows)

    gathered = dict(zip(later_names, _exchange_wait("ag_later_sibling_wait", ag_later, [out_b])[1]))
    wt_pa = gathered["w_proj_a"].reshape(d, DIL_OUT_WIDTH)
    wt_pb = gathered["w_proj_b"].reshape(d, FOX_WIDTH)
    w_o = gathered["w_out"].reshape(d, d)
    wt_g = gathered["w_ffn_gate"].reshape(D_FF, d)
    wt_u = gathered["w_ffn_up"].reshape(D_FF, d)
    w_d = gathered["w_ffn_down"].reshape(D_FF, d)
    merged, mix, x1, h2 = _mix_out(out_a, out_b, proj, x2, wt_pa, wt_pb, w_o, norm_mix_post, norm_ffn_pre)

    gate, up, act = _ffn_up(h2, wt_g, wt_u)
    dy, dff, loss_part, dg_ffn_post = _loss_head(act, w_d, x1, target, norm_ffn_post)

    dgate, dup = _ffn_act_bwd(dff, w_d, gate, up)
    grads_t = {}
    grads_t["w_ffn_down"] = _matmul([(act, dff)], "tn", F32, "grad_w_ffn_down", 1408, 1024, 2048)
    grads_t["w_ffn_gate"] = _matmul([(dgate, h2)], "tn", F32, "grad_w_ffn_gate", 1408, 1024, 2048)
    grads_t["w_ffn_up"] = _matmul([(dup, h2)], "tn", F32, "grad_w_ffn_up", 1408, 1024, 2048)
    rs_ffn = _ReduceScatter("ffn", {k: grads_t[k] for k in ffn_names}, core, chip)
    dx1, dmix, dg_ffn_pre, dg_mix_post = _post_ffn_bwd(dgate, wt_g, dup, wt_u, x1, dy, mix, norm_ffn_pre, norm_mix_post,
                                                       deps=[rs_ffn.token])
    rs_ffn.start_chips([dmix])

    dproj, dya, dyb, d_out_a, d_out_b = _mix_out_bwd(dmix, out_a, out_b, proj, wt_pa, wt_pb, w_o, deps=[rs_ffn.token])
    grads_t["w_out"] = _matmul([(merged, dmix)], "tn", F32, "grad_w_out", 1024, 1024, 1024)
    grads_t["w_proj_a"] = _matmul([(dya, out_a)], "tn", F32, "grad_w_proj_a", 1024, DIL_OUT_WIDTH, SEQ)
    grads_t["w_proj_b"] = _matmul([(dyb, out_b)], "tn", F32, "grad_w_proj_b", 1024, FOX_WIDTH, SEQ)

    dproj, d_cum = _fox_bwd(proj, d_out_b, lse_fox, f_rows, dproj)
    d_cum_rows = jnp.pad(d_cum[:, :2].reshape(N_FOX_HEADS, SEQ), ((0, F_ROWS - N_FOX_HEADS), (0, 0)))
    dproj, db_part = _fox_gate_bwd(d_cum_rows, proj, b_pad, dproj)

    do_dil, c_dil = _dil_combine_bwd(d_out_a, o_dil, lse_dil)
    for g in range(len(DILATIONS)):
        dproj = _dil_bwd(g, proj, tables, do_dil[g], lse_dil[g], c_dil[g], dproj)
    dwt_r = _matmul([(dproj, h)], "tn", F32, "grad_w_in", 896, 1024, 2048)
    grads_t["w_in"] = _proj_weight_grad_slots(dwt_r)
    rs_mix = _ReduceScatter("mix", {k: grads_t[k] for k in mix_names}, core, chip)
    grads = rs_ffn.finish([rs_mix.token])
    big = {k: _adam(shards[k], grads[k], moments_m[k], moments_v[k], "adam_" + k) for k in ffn_names}
    rs_mix.start_chips([big[k][0] for k in ffn_names])
    grad_x, dg_mix_pre = _input_bwd(dproj, wt_r, x2, dx1, norm_mix_pre, deps=[rs_mix.token])

    small_all = _all_gather([dg_mix_pre, dg_mix_post, dg_ffn_pre, dg_ffn_post, db_part, loss_part],
                            "small_grads_all_gather")
    small, loss = _adam_small(small_all[:5], [norm_mix_pre, norm_mix_post, norm_ffn_pre, norm_ffn_post, b_forget],
                              [m_norm_mix_pre, m_norm_mix_post, m_norm_ffn_pre, m_norm_ffn_post, m_b_forget],
                              [v_norm_mix_pre, v_norm_mix_post, v_norm_ffn_pre, v_norm_ffn_post, v_b_forget],
                              small_all[5])

    grads.update(rs_mix.finish([small[0][0], grad_x]))
    big.update({k: _adam(shards[k], grads[k], moments_m[k], moments_v[k], "adam_" + k) for k in mix_names})

    def leaves(i):
        def nat(k):
            a = grads[k] if i == 0 else big[k][i - 1]
            return (a.T if k in col_sharded else a)[None]

        return [nat("w_in"), nat("w_proj_a"), nat("w_proj_b"), nat("w_out"), small[4][i],
                nat("w_ffn_gate"), nat("w_ffn_up"), nat("w_ffn_down"), *[small[r][i] for r in range(4)]]

    return (loss[0, 0], grad_x[None], *leaves(0), *leaves(1), *leaves(2), *leaves(3))
```

```python
import functools
import math

import jax
import jax.numpy as jnp
import numpy as np
from jax import lax
from jax.experimental import pallas as pl
from jax.experimental.pallas import tpu as pltpu

F32 = jnp.float32
BF16 = jnp.bfloat16
MESH = pl.DeviceIdType.MESH

D_MODEL = 1024
SEQ = 2048
HEAD_DIM = 64
BLOCK = 128
N_BLOCKS = SEQ // BLOCK
DILATIONS = (1, 4, 16)
N_FOX_HEADS = 8
DIL_WIDTH = 768
DIL_OUT_WIDTH = 256
FOX_WIDTH = 512
D_FF = 2816
ROPE_THETA = 500000.0
ROPE_DIM = HEAD_DIM // 4
ROPE_HALF = ROPE_DIM // 2
EPS = 1e-6
NEG_INF = -1e30
QK_SCALE = 1.0 / math.sqrt(HEAD_DIM)
IN_COLS = 5896
N_DEV = 8
IN_SHARD = IN_COLS // N_DEV

ADAM_LR = 0.001
ADAM_B1 = 0.9
ADAM_B2 = 0.999
ADAM_EPS = 1e-08
ADAM_WD = 0.01
ADAM_STEP = 10

V7X_VMEM_BYTES = 64 * 2**20
LANES = 128
SUBLANES = 8

PROJ_COLS = 6272
COL_GA, COL_GB = 0, 1024
COL_QA, COL_KA, COL_VA = 2304, 3072, 3840
COL_QB, COL_KB, COL_VB = 4608, 5120, 5632
COL_F = 6144
F_ROWS = 16


def _vmem_limit(block_bytes):
    want = 2 * block_bytes + 16 * 2**20
    return int(min(max(want, 32 * 2**20), V7X_VMEM_BYTES - 8 * 2**20))


def _nbytes(shape, dtype):
    return math.prod(shape) * jnp.dtype(dtype).itemsize


def _in_hbm(*arrays):
    return [pltpu.with_memory_space_constraint(a, pltpu.HBM) for a in arrays]


def _dot(a, b, dims):
    return lax.dot_general(a, b, (dims, ((), ())), preferred_element_type=F32)


def _dot_nn(a, b):
    return _dot(a, b, ((1,), (0,)))


def _dot_nt(a, b):
    return _dot(a, b, ((1,), (1,)))


def _dot_tn(a, b):
    return _dot(a, b, ((0,), (0,)))


def _sigmoid(z):
    return 1.0 / (1.0 + jnp.exp(-z))


def _split3(x):
    hi = x.astype(BF16)
    r1 = x - hi.astype(F32)
    mid = r1.astype(BF16)
    lo = (r1 - mid.astype(F32)).astype(BF16)
    return hi, mid, lo


def _dot3_nn(x, ones_matrix):
    hi, mid, lo = _split3(x)
    return (_dot_nn(hi, ones_matrix) + _dot_nn(mid, ones_matrix)) + _dot_nn(lo, ones_matrix)


def _rowwise(fn, name, n_rows, tm, row_ins, bcast_ins, row_outs, acc_outs=(), deps=()):
    n_in = len(row_ins) + len(bcast_ins)
    n_ro = len(row_outs)

    def body(*refs):
        res = fn(*[r[...] for r in refs[:n_in]])
        if not isinstance(res, (tuple, list)):
            res = (res,)
        outs = refs[n_in + len(deps):]
        for r, o in zip(res[:n_ro], outs[:n_ro]):
            o[...] = r.astype(o.dtype)
        first = pl.program_id(0) == 0
        for r, o in zip(res[n_ro:], outs[n_ro:]):
            _accumulate(o, r, first)

    in_specs = [pl.BlockSpec((tm, w), lambda i, cb=cb: (i, cb)) for _, w, cb in row_ins]
    in_specs += [pl.BlockSpec(a.shape, lambda i: (0, 0)) for a in bcast_ins]
    in_specs += [pl.BlockSpec(memory_space=pl.ANY)] * len(deps)
    out_specs = [pl.BlockSpec((tm, w), lambda i: (i, 0)) for w, _ in row_outs]
    out_specs += [pl.BlockSpec((1, w), lambda i: (0, 0)) for w in acc_outs]
    out_shape = [jax.ShapeDtypeStruct((n_rows, w), dt) for w, dt in row_outs]
    out_shape += [jax.ShapeDtypeStruct((1, w), F32) for w in acc_outs]
    blk = sum(_nbytes((tm, w), a.dtype) for a, w, _ in row_ins) + sum(_nbytes((tm, w), dt) for w, dt in row_outs)
    return pl.pallas_call(
        body, name=name, grid=(n_rows // tm,), in_specs=in_specs, out_specs=out_specs, out_shape=out_shape,
        compiler_params=pltpu.CompilerParams(
            dimension_semantics=("arbitrary" if acc_outs else "parallel",), vmem_limit_bytes=_vmem_limit(3 * blk)),
    )(*_in_hbm(*[a for a, _, _ in row_ins], *bcast_ins), *deps)


def _accumulate(o_ref, part, first):
    @pl.when(first)
    def _():
        o_ref[...] = part

    @pl.when(jnp.logical_not(first))
    def _():
        o_ref[...] += part


_MM_DIMS = {"nn": ((1,), (0,)), "nt": ((1,), (1,)), "tn": ((0,), (0,))}


def _matmul(pairs, mode, out_dtype, name, tm, tn, tk, deps=()):
    a0, b0 = pairs[0]
    if mode == "tn":
        kk, m = a0.shape
    else:
        m, kk = a0.shape
    n = b0.shape[0] if mode == "nt" else b0.shape[1]
    assert m % tm == 0 and n % tn == 0 and kk % tk == 0, (name, m, n, kk)
    nk = kk // tk
    n_pairs = len(pairs)
    dims = _MM_DIMS[mode]
    n_in = 2 * n_pairs + len(deps)

    def body(*refs):
        o_ref = refs[n_in]
        part = None
        for p in range(n_pairs):
            d = _dot(refs[2 * p][...].astype(BF16), refs[2 * p + 1][...].astype(BF16), dims)
            part = d if part is None else part + d
        if nk == 1:
            o_ref[...] = part.astype(o_ref.dtype)
            return
        acc = refs[n_in + 1]
        k = pl.program_id(2)

        @pl.when(k == 0)
        def _():
            acc[...] = part

        @pl.when(k > 0)
        def _():
            acc[...] += part

        @pl.when(k == nk - 1)
        def _():
            o_ref[...] = acc[...].astype(o_ref.dtype)

    if mode == "tn":
        a_spec = pl.BlockSpec((tk, tm), lambda i, j, k: (k, i))
    else:
        a_spec = pl.BlockSpec((tm, tk), lambda i, j, k: (i, k))
    if mode == "nt":
        b_spec = pl.BlockSpec((tn, tk), lambda i, j, k: (j, k))
    else:
        b_spec = pl.BlockSpec((tk, tn), lambda i, j, k: (k, j))
    blk = sum(_nbytes((tm, tk), a.dtype) + _nbytes((tk, tn), b.dtype) for a, b in pairs) + 2 * _nbytes((tm, tn), F32)
    flat = [a for pair in pairs for a in pair]
    return pl.pallas_call(
        body, name=name, grid=(m // tm, n // tn, nk),
        in_specs=[a_spec, b_spec] * n_pairs + [pl.BlockSpec(memory_space=pl.ANY)] * len(deps),
        out_specs=pl.BlockSpec((tm, tn), lambda i, j, k: (i, j)),
        out_shape=jax.ShapeDtypeStruct((m, n), out_dtype),
        scratch_shapes=[] if nk == 1 else [pltpu.VMEM((tm, tn), F32)],
        compiler_params=pltpu.CompilerParams(
            dimension_semantics=("parallel", "parallel", "arbitrary"), vmem_limit_bytes=_vmem_limit(blk)),
    )(*_in_hbm(*flat), *deps)


def _matmul_rowwise(pairs, fn, name, tm, row_ins, bcast_ins, row_outs, acc_outs=(), deps=()):
    m = pairs[0][0].shape[0]
    n_mm, n_in = 2 * len(pairs), len(row_ins) + len(bcast_ins)
    n_ro = len(row_outs)

    def body(*refs):
        prod = None
        for p in range(len(pairs)):
            part = _dot_nn(refs[2 * p][...].astype(BF16), refs[2 * p + 1][...].astype(BF16))
            prod = part if prod is None else prod + part
        res = fn(prod, *[r[...] for r in refs[n_mm:n_mm + n_in]])
        outs = refs[n_mm + n_in + len(deps):]
        for r, o in zip(res[:n_ro], outs[:n_ro]):
            o[...] = r.astype(o.dtype)
        first = pl.program_id(0) == 0
        for r, o in zip(res[n_ro:], outs[n_ro:]):
            _accumulate(o, r, first)

    in_specs = []
    for a, b in pairs:
        in_specs += [pl.BlockSpec((tm, a.shape[1]), lambda i: (i, 0)),
                     pl.BlockSpec(b.shape, lambda i: (0, 0), pipeline_mode=pl.Buffered(1))]
    in_specs += [pl.BlockSpec((tm, w), lambda i, cb=cb: (i, cb)) for _, w, cb in row_ins]
    in_specs += [pl.BlockSpec(a.shape, lambda i: (0, 0)) for a in bcast_ins]
    in_specs += [_ANY] * len(deps)
    out_specs = [pl.BlockSpec((tm, w), lambda i: (i, 0)) for w, _ in row_outs]
    out_specs += [pl.BlockSpec((1, w), lambda i: (0, 0)) for w in acc_outs]
    out_shape = [jax.ShapeDtypeStruct((m, w), dt) for w, dt in row_outs]
    out_shape += [jax.ShapeDtypeStruct((1, w), F32) for w in acc_outs]
    blk = sum(_nbytes((tm, a.shape[1]), a.dtype) + _nbytes(b.shape, b.dtype) // 2 for a, b in pairs)
    blk += sum(_nbytes((tm, w), a.dtype) for a, w, _ in row_ins) + sum(_nbytes((tm, w), dt) for w, dt in row_outs)
    return pl.pallas_call(
        body, name=name, grid=(m // tm,), in_specs=in_specs, out_specs=out_specs, out_shape=out_shape,
        compiler_params=pltpu.CompilerParams(dimension_semantics=("arbitrary",), vmem_limit_bytes=_vmem_limit(blk)),
    )(*_in_hbm(*[a for pair in pairs for a in pair], *[a for a, _, _ in row_ins], *bcast_ins), *deps)


def _rms_scale(x):
    return lax.rsqrt(jnp.mean(x * x, axis=-1, keepdims=True) + EPS)


def _rms_bwd(xin, dyn, g):
    r = _rms_scale(xin)
    u = dyn * g
    dx = r * u - xin * (r * r * r) * jnp.mean(u * xin, axis=-1, keepdims=True)
    dg = jnp.sum(dyn * xin * r, axis=0, keepdims=True)
    return dx, dg


def _mesh_pos():
    return lax.axis_index("x"), lax.axis_index("y"), lax.axis_index("c")


def _all_gather(xs, name):
    n = len(xs)

    def body(*refs):
        x_refs, out_refs = refs[:n], refs[n:2 * n]
        send_sems, recv_sems, local_sems = refs[2 * n:]
        mx, my, mc = _mesh_pos()
        me, sib = (mx, my, mc), (mx, my, 1 - mc)
        chips = [(1 - mx, my), (mx, 1 - my), (1 - mx, 1 - my)]

        def slot(a, dev):
            px, py, pc = dev
            return out_refs[a].at[4 * px + 2 * py + pc]

        def copy(k, a, block, to, src=None):
            return pltpu.make_async_remote_copy(
                src_ref=slot(a, block) if src is None else src, dst_ref=slot(a, block),
                send_sem=send_sems.at[a * 7 + k], recv_sem=recv_sems.at[a * 7 + k],
                device_id=to, device_id_type=MESH)

        mine = [pltpu.make_async_copy(x_refs[a], slot(a, me), local_sems.at[a]) for a in range(n)]
        for cp in mine:
            cp.start()
        first = []
        for a in range(n):
            first.append(copy(0, a, me, sib, x_refs[a]))
            first += [copy(1 + j, a, me, (*chip, mc), x_refs[a]) for j, chip in enumerate(chips)]
        for cp in first:
            cp.start()
        passed = []
        for a in range(n):
            for j, chip in enumerate(chips):
                copy(1 + j, a, (*chip, mc), me).wait_recv()
                fwd = copy(4 + j, a, (*chip, mc), sib)
                fwd.start()
                passed.append(fwd)
        for a in range(n):
            copy(0, a, sib, me).wait_recv()
            for j, chip in enumerate(chips):
                copy(4 + j, a, (*chip, 1 - mc), me).wait_recv()
        for cp in first + passed:
            cp.wait_send()
        for cp in mine:
            cp.wait()

    hbm = pl.BlockSpec(memory_space=pl.ANY)
    return pl.pallas_call(
        body, name=name,
        out_shape=[jax.ShapeDtypeStruct((N_DEV,) + x.shape, x.dtype) for x in xs],
        in_specs=[hbm] * n, out_specs=[hbm] * n,
        scratch_shapes=[pltpu.SemaphoreType.DMA((7 * n,)), pltpu.SemaphoreType.DMA((7 * n,)),
                        pltpu.SemaphoreType.DMA((n,))],
    )(*xs)


_HBM = pl.BlockSpec(memory_space=pltpu.HBM)
_SEM = pl.BlockSpec(memory_space=pltpu.SEMAPHORE)
_ANY = pl.BlockSpec(memory_space=pl.ANY)
_DATAFLOW = pltpu.SideEffectType.DATAFLOW_SIDE_EFFECTING


def _flip_peer(flip):
    mx, my, mc = _mesh_pos()
    return (1 - mx if flip & 2 else mx, 1 - my if flip & 1 else my, mc)


def _remote(src, dst, send_sems, recv_sems, k, peer):
    return pltpu.make_async_remote_copy(src_ref=src, dst_ref=dst, send_sem=send_sems.at[k], recv_sem=recv_sems.at[k],
                                        device_id=peer, device_id_type=MESH)


def _gather_chips_copies(srcs, lands, send_sems, recv_sems):
    mx, my, mc = _mesh_pos()
    me = 4 * mx + 2 * my + mc
    return [_remote(srcs[a], lands[a].at[me], send_sems, recv_sems, 3 * a + flip - 1, _flip_peer(flip))
            for a in range(len(srcs)) for flip in (1, 2, 3)]


def _gather_sibling_copies(srcs, lands, send_sems, recv_sems):
    mx, my, mc = _mesh_pos()
    return [_remote(lands[a].at[2 * k + mc], lands[a].at[2 * k + mc], send_sems, recv_sems, 4 * a + k, (mx, my, 1 - mc))
            for a in range(len(lands)) for k in range(4)]


def _scatter_sibling_copies(srcs, lands, send_sems, recv_sems):
    mx, my, mc = _mesh_pos()
    return [_remote(srcs[a].at[k, 1 - mc], lands[a].at[k], send_sems, recv_sems, 4 * a + k, (mx, my, 1 - mc))
            for a in range(len(srcs)) for k in range(4)]


def _scatter_chips_copies(srcs, lands, send_sems, recv_sems):
    mx, my, _ = _mesh_pos()
    k0 = 2 * mx + my
    return [_remote(srcs[a].at[jnp.bitwise_xor(k0, flip)], lands[a].at[flip - 1], send_sems, recv_sems,
                    3 * a + flip - 1, _flip_peer(flip))
            for a in range(len(srcs)) for flip in (1, 2, 3)]


class _Exchange:
    def __init__(self, copies, n_src, send_sems, recv_sems, thru, token):
        self.copies, self.n_src, self.send_sems, self.recv_sems, self.thru, self.token = (
            copies, n_src, send_sems, recv_sems, thru, token)


def _exchange_start(name, copies, srcs, lands, n_copies, after=()):
    bufs = list(srcs) + list(lands)
    nb, ns = len(bufs), len(srcs)

    def body(*refs):
        send_sems, recv_sems = refs[nb + len(after)], refs[nb + len(after) + 1]
        for cp in copies(refs[:ns], refs[ns:nb], send_sems, recv_sems):
            cp.start()
        refs[-1][...] = jnp.zeros_like(refs[-1])

    out = pl.pallas_call(
        body, name=name,
        out_shape=(pltpu.SemaphoreType.DMA((n_copies,)), pltpu.SemaphoreType.DMA((n_copies,)),
                   *[pltpu.HBM(b.shape, b.dtype) for b in bufs], jax.ShapeDtypeStruct((SUBLANES, LANES), F32)),
        in_specs=[_HBM] * nb + [_ANY] * len(after),
        out_specs=(_SEM, _SEM, *[_HBM] * nb, pl.BlockSpec(memory_space=pltpu.VMEM)),
        input_output_aliases={i: 2 + i for i in range(nb)},
        compiler_params=pltpu.CompilerParams(has_side_effects=_DATAFLOW),
    )(*[pltpu.with_memory_space_constraint(b, pltpu.HBM) for b in bufs], *after)
    return _Exchange(copies, ns, out[0], out[1], list(out[2:2 + nb]), out[-1])


def _exchange_wait(name, ex, after):
    nb, ns = len(ex.thru), ex.n_src

    def body(*refs):
        for cp in ex.copies(refs[:ns], refs[ns:nb], refs[nb], refs[nb + 1]):
            cp.wait_send()
            cp.wait_recv()

    out = pl.pallas_call(
        body, name=name, out_shape=tuple(pltpu.HBM(b.shape, b.dtype) for b in ex.thru),
        in_specs=[_HBM] * nb + [_SEM, _SEM] + [_ANY] * len(after), out_specs=tuple([_HBM] * nb),
        input_output_aliases={i: i for i in range(nb)},
        compiler_params=pltpu.CompilerParams(has_side_effects=_DATAFLOW),
    )(*ex.thru, ex.send_sems, ex.recv_sems, *after)
    return list(out[:ns]), list(out[ns:])


def _col_tile(r, c):
    return next(t for t in (1024, 512, 256, 128) if c % t == 0 and (r * t * 4 <= 2**20 or t == 128))


def _add_sibling(g4, recv, core, name):
    _, _, r, c = g4.shape
    tc = _col_tile(r, c)

    def body(core_ref, g_ref, r_ref, o16_ref, o32_ref):
        s = g_ref[0, 0] + r_ref[0]
        o16_ref[0] = s.astype(BF16)
        o32_ref[0] = s

    out = pl.BlockSpec((1, r, tc), lambda k, j, core_ref: (k, 0, j))
    return pl.pallas_call(
        body, name=name,
        out_shape=[jax.ShapeDtypeStruct((4, r, c), BF16), jax.ShapeDtypeStruct((4, r, c), F32)],
        grid_spec=pltpu.PrefetchScalarGridSpec(
            num_scalar_prefetch=1, grid=(4, c // tc),
            in_specs=[pl.BlockSpec((1, 1, r, tc), lambda k, j, core_ref: (k, core_ref[0], 0, j)), out],
            out_specs=[out, out]),
        compiler_params=pltpu.CompilerParams(dimension_semantics=("parallel", "parallel")),
    )(core, *_in_hbm(g4, recv))


def _add_chips(p32, recv, chip, name):
    _, r, c = p32.shape
    tc = _col_tile(r, c)

    def body(chip_ref, p_ref, r_ref, o_ref):
        o_ref[...] = ((p_ref[0] + r_ref[0].astype(F32)) + r_ref[1].astype(F32)) + r_ref[2].astype(F32)

    return pl.pallas_call(
        body, name=name, out_shape=jax.ShapeDtypeStruct((r, c), F32),
        grid_spec=pltpu.PrefetchScalarGridSpec(
            num_scalar_prefetch=1, grid=(c // tc,),
            in_specs=[pl.BlockSpec((1, r, tc), lambda j, chip_ref: (chip_ref[0], 0, j)),
                      pl.BlockSpec((3, r, tc), lambda j, chip_ref: (0, 0, j))],
            out_specs=pl.BlockSpec((r, tc), lambda j, chip_ref: (0, j))),
        compiler_params=pltpu.CompilerParams(dimension_semantics=("parallel",)),
    )(chip, *_in_hbm(p32, recv))


class _ReduceScatter:
    def __init__(self, tag, grads_t, core, chip):
        self.tag, self.core, self.chip, self.names = tag, core, chip, list(grads_t)
        g4s = [g.reshape(4, 2, g.size // (N_DEV * g.shape[-1]), g.shape[-1]) for g in grads_t.values()]
        lands = [lax.empty((4,) + g.shape[2:], F32) for g in g4s]
        self.ex = _exchange_start(f"rs_{tag}_sibling_start", _scatter_sibling_copies, g4s, lands, 4 * len(g4s))
        self.token = self.ex.token

    def start_chips(self, after):
        g4s, from_sibling = _exchange_wait(f"rs_{self.tag}_sibling_wait", self.ex, after)
        parts = [_add_sibling(g4, rv, self.core, f"rs_add_sibling_{k}")
                 for k, g4, rv in zip(self.names, g4s, from_sibling)]
        self.p32s = [p32 for _, p32 in parts]
        p16s = [p16 for p16, _ in parts]
        lands = [lax.empty((3,) + p.shape[1:], BF16) for p in p16s]
        self.ex = _exchange_start(f"rs_{self.tag}_chips_start", _scatter_chips_copies, p16s, lands, 3 * len(p16s))
        self.token = self.ex.token

    def finish(self, after):
        _, from_chips = _exchange_wait(f"rs_{self.tag}_chips_wait", self.ex, after)
        return {k: _add_chips(p32, rv, self.chip, f"rs_add_chips_{k}")
                for k, p32, rv in zip(self.names, self.p32s, from_chips)}


def _rope_tables():
    positions = np.arange(SEQ, dtype=np.float32)
    inv_freq = np.power(np.float32(ROPE_THETA), -np.arange(0, ROPE_DIM, 2, dtype=np.float32) / np.float32(ROPE_DIM))
    ang = (positions[:, None] * inv_freq[None, :]).astype(np.float32)
    cos, sin = np.cos(ang).astype(np.float32), np.sin(ang).astype(np.float32)
    ones = np.ones((SEQ, HEAD_DIM - ROPE_DIM), np.float32)
    zeros8 = np.zeros((SEQ, ROPE_HALF), np.float32)
    zeros = np.zeros((SEQ, HEAD_DIM - ROPE_DIM), np.float32)
    c_head = np.concatenate([cos, cos, ones], axis=1)
    s1_head = np.concatenate([-sin, zeros8, zeros], axis=1)
    s2_head = np.concatenate([zeros8, sin, zeros], axis=1)
    return tuple(jnp.asarray(np.concatenate([t, t], axis=1)) for t in (c_head, s1_head, s2_head))


def _rope_apply(x, c, s1, s2):
    w = x.shape[1]
    return x * c + pltpu.roll(x, w - ROPE_HALF, 1) * s1 + pltpu.roll(x, ROPE_HALF, 1) * s2


def _rope_apply_t(dy, c, s1, s2):
    w = dy.shape[1]
    return dy * c + pltpu.roll(dy * s1, ROPE_HALF, 1) + pltpu.roll(dy * s2, w - ROPE_HALF, 1)


def _dil_prev_limit(has_prev):
    return jnp.where(has_prev, 0, BLOCK)


def _dil_valid(limit):
    row = lax.broadcasted_iota(jnp.int32, (BLOCK, 2 * BLOCK), 0)
    col = lax.broadcasted_iota(jnp.int32, (BLOCK, 2 * BLOCK), 1)
    dist = col - row
    return jnp.logical_and(dist >= jnp.where(col < BLOCK, limit, -BLOCK), dist <= BLOCK)


def _upper_half():
    return lax.broadcasted_iota(jnp.int32, (1, LANES), 1) >= HEAD_DIM


def _dil_rows(n, d):
    per = N_BLOCKS // d
    r, lb = n // per, n % per

    def rows(b):
        start = b * (BLOCK * d) + r
        return pl.ds(pl.multiple_of(start, BLOCK), BLOCK) if d == 1 else pl.ds(start, BLOCK, stride=d)

    return rows(lb), rows(jnp.maximum(lb - 1, 0)), lb > 0


def _dil_specs(g):
    def col(base):
        return pl.BlockSpec((SEQ, LANES), lambda p: (0, base // LANES + 2 * g + p))

    table = pl.BlockSpec((SEQ, LANES), lambda p: (0, 0))
    return [col(COL_QA), col(COL_KA), col(COL_VA)], [table] * 3


def _store_columns(blocks, dproj_ref, cols, sem):
    copies = [pltpu.make_async_copy(b, dproj_ref.at[:, pl.ds(pl.multiple_of(c * LANES, LANES), LANES)], sem.at[i])
              for i, (b, c) in enumerate(zip(blocks, cols))]
    for cp in copies:
        cp.start()
    for cp in copies:
        cp.wait()


def _dil_fwd(g, proj, tables):
    d = DILATIONS[g]
    one_block = d == N_BLOCKS

    def body(q_ref, k_ref, v_ref, c_ref, s1_ref, s2_ref, o_ref, lse_ref):
        upper = _upper_half()

        def roped(ref, rows):
            return _rope_apply(ref[rows, :], c_ref[rows, :], s1_ref[rows, :], s2_ref[rows, :])

        def block(n, carry):
            rows, prev, has_prev = _dil_rows(n, d)
            qb = (roped(q_ref, rows) * QK_SCALE).astype(BF16)
            kw, vw = roped(k_ref, rows).astype(BF16), v_ref[rows, :].astype(BF16)
            if one_block:
                row = lax.broadcasted_iota(jnp.int32, (BLOCK, BLOCK), 0)
                valid = lax.broadcasted_iota(jnp.int32, (BLOCK, BLOCK), 1) <= row
            else:
                kw = jnp.concatenate([roped(k_ref, prev).astype(BF16), kw], axis=0)
                vw = jnp.concatenate([v_ref[prev, :].astype(BF16), vw], axis=0)
                valid = _dil_valid(_dil_prev_limit(has_prev))
            outs, lses = [], []
            for head_mask in (jnp.logical_not(upper), upper):
                s = jnp.where(valid, _dot_nt(qb, jnp.where(head_mask, kw, 0)), NEG_INF)
                m = jnp.max(s, axis=-1, keepdims=True)
                p = jnp.exp(s - m)
                den = jnp.sum(p, axis=-1, keepdims=True)
                outs.append(_dot_nn((p / den).astype(BF16), jnp.where(head_mask, vw, 0)))
                lses.append(m + jnp.log(den))
            o_ref[rows, :] = outs[0] + outs[1]
            lse_ref[rows, :] = jnp.where(upper, lses[1], lses[0])
            return carry

        lax.fori_loop(0, N_BLOCKS, block, 0, unroll=2)

    qkv, tabs = _dil_specs(g)
    out = pl.BlockSpec((SEQ, LANES), lambda p: (0, p))
    return pl.pallas_call(
        body, name=f"dil_attn_fwd_{g}", grid=(2,), in_specs=qkv + tabs, out_specs=[out, out],
        out_shape=[jax.ShapeDtypeStruct((SEQ, DIL_OUT_WIDTH), F32)] * 2,
        compiler_params=pltpu.CompilerParams(dimension_semantics=("parallel",)),
    )(*_in_hbm(proj, proj, proj, *tables))


def _dil_bwd(g, proj, tables, do, lse, c, dproj):
    d = DILATIONS[g]
    one_block = d == N_BLOCKS

    def body(q_ref, k_ref, v_ref, c_ref, s1_ref, s2_ref, do_ref, lse_ref, cc_ref, dproj_in, dproj_ref,
             dq_acc, dk_acc, dv_acc, dq_out, dk_out, dv_out, sem):
        upper = _upper_half()
        dk_acc[...] = jnp.zeros_like(dk_acc)
        dv_acc[...] = jnp.zeros_like(dv_acc)

        def roped(ref, rows):
            return _rope_apply(ref[rows, :], c_ref[rows, :], s1_ref[rows, :], s2_ref[rows, :])

        def block(n, carry):
            rows, prev, has_prev = _dil_rows(n, d)
            qb = (roped(q_ref, rows) * QK_SCALE).astype(BF16)
            dob = do_ref[rows, :].astype(BF16)
            kw, vw = roped(k_ref, rows).astype(BF16), v_ref[rows, :].astype(BF16)
            if one_block:
                row = lax.broadcasted_iota(jnp.int32, (BLOCK, BLOCK), 0)
                valid = lax.broadcasted_iota(jnp.int32, (BLOCK, BLOCK), 1) <= row
            else:
                kw = jnp.concatenate([roped(k_ref, prev).astype(BF16), kw], axis=0)
                vw = jnp.concatenate([v_ref[prev, :].astype(BF16), vw], axis=0)
                valid = _dil_valid(_dil_prev_limit(has_prev))
            lse_t, c_t = lse_ref[rows, :], cc_ref[rows, :]
            dq, dk, dv = None, None, None
            for e, head_mask in enumerate((jnp.logical_not(upper), upper)):
                km, vm = jnp.where(head_mask, kw, 0), jnp.where(head_mask, vw, 0)
                lse_col = lse_t[:, e * HEAD_DIM:e * HEAD_DIM + 1]
                c_col = c_t[:, e * HEAD_DIM:e * HEAD_DIM + 1]
                p = jnp.where(valid, jnp.exp(_dot_nt(qb, km) - lse_col), 0.0)
                ds = (p * (_dot_nt(dob, vm) + c_col)).astype(BF16)
                parts = (_dot_nn(ds, km), _dot_tn(ds, jnp.where(head_mask, qb, 0)),
                         _dot_tn(p.astype(BF16), jnp.where(head_mask, dob, 0)))
                dq, dk, dv = parts if dq is None else (dq + parts[0], dk + parts[1], dv + parts[2])
            dq_acc[rows, :] = dq * QK_SCALE
            if one_block:
                dk_acc[rows, :] += dk
                dv_acc[rows, :] += dv
            else:
                dk_acc[prev, :] += dk[:BLOCK]
                dv_acc[prev, :] += dv[:BLOCK]
                dk_acc[rows, :] += dk[BLOCK:]
                dv_acc[rows, :] += dv[BLOCK:]
            return carry

        lax.fori_loop(0, N_BLOCKS, block, 0, unroll=2)
        tabs = (c_ref[...], s1_ref[...], s2_ref[...])
        dq_out[...] = _rope_apply_t(dq_acc[...], *tabs).astype(BF16)
        dk_out[...] = _rope_apply_t(dk_acc[...], *tabs).astype(BF16)
        dv_out[...] = dv_acc[...].astype(BF16)
        pair = 2 * g + pl.program_id(0)
        _store_columns((dq_out, dk_out, dv_out), dproj_ref,
                       [base // LANES + pair for base in (COL_QA, COL_KA, COL_VA)], sem)

    qkv, tabs = _dil_specs(g)
    tok = pl.BlockSpec((SEQ, LANES), lambda p: (0, p))
    return pl.pallas_call(
        body, name=f"dil_attn_bwd_{g}", grid=(2,),
        in_specs=qkv + tabs + [tok, tok, tok, _ANY], out_specs=_ANY,
        out_shape=jax.ShapeDtypeStruct(dproj.shape, dproj.dtype),
        scratch_shapes=[pltpu.VMEM((SEQ, LANES), F32)] * 3 + [pltpu.VMEM((SEQ, LANES), BF16)] * 3
        + [pltpu.SemaphoreType.DMA((3,))],
        input_output_aliases={9: 0},
        compiler_params=pltpu.CompilerParams(dimension_semantics=("arbitrary",)),
    )(*_in_hbm(proj, proj, proj, *tables, do, lse, c, dproj))


def _group_weights(l0, l1, l2):
    m = jnp.maximum(jnp.maximum(l0, l1), l2)
    e0, e1, e2 = jnp.exp(l0 - m), jnp.exp(l1 - m), jnp.exp(l2 - m)
    tot = e0 + e1 + e2
    return e0 / tot, e1 / tot, e2 / tot


def _dil_combine(outs, lses, deps=()):
    def fn(o0, o1, o2, l0, l1, l2):
        w0, w1, w2 = _group_weights(l0, l1, l2)
        return w0 * o0 + w1 * o1 + w2 * o2

    w = DIL_OUT_WIDTH
    return _rowwise(fn, "dil_combine", SEQ, 512, [(a, w, 0) for a in list(outs) + list(lses)], [], [(w, F32)],
                    deps=deps)[0]


def _dil_combine_bwd(d_out, outs, lses):
    w = DIL_OUT_WIDTH

    def fn(d, o0, o1, o2, l0, l1, l2):
        row = lax.broadcasted_iota(jnp.int32, (w, w), 0) // HEAD_DIM
        col = lax.broadcasted_iota(jnp.int32, (w, w), 1) // HEAD_DIM
        same_head = jnp.where(row == col, 1.0, 0.0).astype(BF16)
        ws = _group_weights(l0, l1, l2)
        dws = [_dot3_nn(d * og, same_head) for og in (o0, o1, o2)]
        mean = ws[0] * dws[0] + ws[1] * dws[1] + ws[2] * dws[2]
        return tuple(wg * d for wg in ws) + tuple(-wg * mean for wg in ws)

    res = _rowwise(fn, "dil_combine_bwd", SEQ, 256, [(a, w, 0) for a in [d_out] + list(outs) + list(lses)], [],
                   [(w, F32)] * 6)
    return res[:3], res[3:]


def _log1p(e):
    u = 1.0 + e
    return jnp.where(u == 1.0, e, jnp.log(u) * (e / (u - 1.0)))


def _fox_gate(proj, b_pad, deps=()):
    def body(f_ref, b_ref, *rest):
        o_ref = rest[-1]
        z = f_ref[...] + b_ref[...]
        logf = (jnp.minimum(z, 0.0) - _log1p(jnp.exp(-jnp.abs(z)))).T[:F_ROWS]
        row = lax.broadcasted_iota(jnp.int32, (BLOCK, BLOCK), 0)
        col = lax.broadcasted_iota(jnp.int32, (BLOCK, BLOCK), 1)
        before = jnp.where(row <= col, 1.0, 0.0).astype(BF16)
        carry = jnp.zeros((F_ROWS, 1), F32)
        for blk in range(N_BLOCKS):
            run = _dot3_nn(logf[:, blk * BLOCK:(blk + 1) * BLOCK], before) + carry
            o_ref[:, blk * BLOCK:(blk + 1) * BLOCK] = run
            carry = run[:, BLOCK - 1:BLOCK]

    return pl.pallas_call(
        body, name="fox_gate", grid=(1,),
        in_specs=[pl.BlockSpec((SEQ, LANES), lambda i: (0, COL_F // LANES)), pl.BlockSpec((1, LANES), lambda i: (0, 0))]
        + [_ANY] * len(deps),
        out_specs=pl.BlockSpec((F_ROWS, SEQ), lambda i: (0, 0)),
        out_shape=jax.ShapeDtypeStruct((F_ROWS, SEQ), F32),
    )(*_in_hbm(proj, b_pad), *deps)


def _fox_gate_bwd(d_cum, proj, b_pad, dproj):
    def body(d_ref, f_ref, b_ref, dproj_ref, dz_ref, db_ref):
        row = lax.broadcasted_iota(jnp.int32, (BLOCK, BLOCK), 0)
        col = lax.broadcasted_iota(jnp.int32, (BLOCK, BLOCK), 1)
        after = jnp.where(row >= col, 1.0, 0.0).astype(BF16)
        carry = jnp.zeros((F_ROWS, 1), F32)
        parts = [None] * N_BLOCKS
        for blk in reversed(range(N_BLOCKS)):
            run = _dot3_nn(d_ref[:, blk * BLOCK:(blk + 1) * BLOCK], after) + carry
            parts[blk] = run
            carry = run[:, 0:1]
        dlogf = jnp.concatenate(parts, axis=1)
        dlogf = jnp.concatenate([dlogf, jnp.zeros((LANES - F_ROWS, SEQ), F32)], axis=0).T
        dz = dlogf * _sigmoid(-(f_ref[...] + b_ref[...]))
        dz_ref[...] = dz.astype(BF16)
        db_ref[...] = jnp.sum(dz, axis=0, keepdims=True)

    f_cols = pl.BlockSpec((SEQ, LANES), lambda i: (0, COL_F // LANES))
    return pl.pallas_call(
        body, name="fox_gate_bwd", grid=(1,),
        in_specs=[pl.BlockSpec((F_ROWS, SEQ), lambda i: (0, 0)), f_cols, pl.BlockSpec((1, LANES), lambda i: (0, 0)), _ANY],
        out_specs=[f_cols, pl.BlockSpec((1, LANES), lambda i: (0, 0))],
        out_shape=[jax.ShapeDtypeStruct(dproj.shape, dproj.dtype), jax.ShapeDtypeStruct((1, LANES), F32)],
        input_output_aliases={3: 0},
    )(*_in_hbm(d_cum, proj, b_pad, dproj))


FOX_TILE = 256
FOX_TILES = SEQ // FOX_TILE


def _row_to_col(row):
    n = row.shape[1]
    eye = lax.broadcasted_iota(jnp.int32, (n, n), 0) == lax.broadcasted_iota(jnp.int32, (n, n), 1)
    return jnp.sum(jnp.where(eye, row, 0.0), axis=1, keepdims=True)


def _fox_scores(q_tile, km, f_row, i):
    t = FOX_TILE
    ext = (i + 1) * t
    f_q = _row_to_col(f_row[:, i * t:(i + 1) * t])
    s = _dot_nt(q_tile, km[:ext]) + (f_q - f_row[:, :ext])
    row = lax.broadcasted_iota(jnp.int32, (t, ext), 0) + i * t
    col = lax.broadcasted_iota(jnp.int32, (t, ext), 1)
    return s, col <= row


def _fox_specs():
    qkv = [pl.BlockSpec((SEQ, LANES), lambda p, base=base: (0, base // LANES + p)) for base in (COL_QB, COL_KB, COL_VB)]
    return qkv, pl.BlockSpec((F_ROWS, SEQ), lambda p: (0, 0))


def _fox_fwd(proj, f_rows):
    t = FOX_TILE

    def body(q_ref, k_ref, v_ref, f_ref, o_ref, lse_ref):
        pair = pl.program_id(0)
        upper = _upper_half()
        masks = (jnp.logical_not(upper), upper)
        k16, v16 = k_ref[...].astype(BF16), v_ref[...].astype(BF16)
        kms = [jnp.where(hm, k16, 0) for hm in masks]
        vms = [jnp.where(hm, v16, 0) for hm in masks]
        f_row = [f_ref[pl.ds(2 * pair + e, 1), :] for e in range(2)]
        for i in range(FOX_TILES):
            q_tile = (q_ref[i * t:(i + 1) * t, :] * QK_SCALE).astype(BF16)
            outs, lses = [], []
            for e in range(2):
                s, causal = _fox_scores(q_tile, kms[e], f_row[e], i)
                s = jnp.where(causal, s, NEG_INF)
                m = jnp.max(s, axis=-1, keepdims=True)
                p = jnp.exp(s - m)
                den = jnp.sum(p, axis=-1, keepdims=True)
                outs.append(_dot_nn((p / den).astype(BF16), vms[e][:(i + 1) * t]))
                lses.append(m + jnp.log(den))
            o_ref[i * t:(i + 1) * t, :] = outs[0] + outs[1]
            lse_ref[i * t:(i + 1) * t, :] = jnp.where(upper, lses[1], lses[0])

    qkv, f_spec = _fox_specs()
    tok = pl.BlockSpec((SEQ, LANES), lambda p: (0, p))
    return pl.pallas_call(
        body, name="fox_attn_fwd", grid=(FOX_WIDTH // LANES,),
        in_specs=qkv + [f_spec], out_specs=[tok, tok],
        out_shape=[jax.ShapeDtypeStruct((SEQ, FOX_WIDTH), F32)] * 2,
        compiler_params=pltpu.CompilerParams(
            dimension_semantics=("parallel",), vmem_limit_bytes=_vmem_limit(8 * t * SEQ * 4)),
    )(*_in_hbm(proj, proj, proj, f_rows))


def _fox_bwd(proj, do, lse, f_rows, dproj):
    t = FOX_TILE

    def body(q_ref, k_ref, v_ref, f_ref, do_ref, lse_ref, dproj_in, dproj_ref, df_ref, dk_acc, dv_acc,
             dq_out, dk_out, dv_out, sem):
        pair = pl.program_id(0)
        upper = _upper_half()
        masks = (jnp.logical_not(upper), upper)
        k16, v16 = k_ref[...].astype(BF16), v_ref[...].astype(BF16)
        kms = [jnp.where(hm, k16, 0) for hm in masks]
        vms = [jnp.where(hm, v16, 0) for hm in masks]
        f_row = [f_ref[pl.ds(2 * pair + e, 1), :] for e in range(2)]
        dk_acc[...] = jnp.zeros_like(dk_acc)
        dv_acc[...] = jnp.zeros_like(dv_acc)
        df_ref[...] = jnp.zeros_like(df_ref)
        for i in range(FOX_TILES):
            ext = (i + 1) * t
            q_tile = (q_ref[i * t:(i + 1) * t, :] * QK_SCALE).astype(BF16)
            do_tile = do_ref[i * t:(i + 1) * t, :]
            lse_t = lse_ref[i * t:(i + 1) * t, :]
            dq = None
            for e in range(2):
                s, causal = _fox_scores(q_tile, kms[e], f_row[e], i)
                p = jnp.where(causal, jnp.exp(s - lse_t[:, e * HEAD_DIM:e * HEAD_DIM + 1]), 0.0)
                dp = _dot_nt(do_tile, vms[e][:ext])
                ds = p * (dp - jnp.sum(p * dp, axis=-1, keepdims=True))
                df_ref[0, e:e + 1, :ext] -= jnp.sum(ds, axis=0, keepdims=True)
                ds = ds.astype(BF16)
                part = _dot_nn(ds, kms[e][:ext])
                dq = part if dq is None else dq + part
                dk_acc[:ext, :] += _dot_tn(ds, jnp.where(masks[e], q_tile, 0))
                dv_acc[:ext, :] += _dot_tn(p.astype(BF16), jnp.where(masks[e], do_tile, 0))
            dq_out[i * t:(i + 1) * t, :] = (dq * QK_SCALE).astype(BF16)
        dk_out[...] = dk_acc[...].astype(BF16)
        dv_out[...] = dv_acc[...].astype(BF16)
        _store_columns((dq_out, dk_out, dv_out), dproj_ref, [base // LANES + pair for base in (COL_QB, COL_KB, COL_VB)],
                       sem)

    qkv, f_spec = _fox_specs()
    tok = pl.BlockSpec((SEQ, LANES), lambda p: (0, p))
    return pl.pallas_call(
        body, name="fox_attn_bwd", grid=(FOX_WIDTH // LANES,),
        in_specs=qkv + [f_spec, tok, tok, _ANY],
        out_specs=[_ANY, pl.BlockSpec((1, SUBLANES, SEQ), lambda p: (p, 0, 0))],
        out_shape=[jax.ShapeDtypeStruct(dproj.shape, dproj.dtype),
                   jax.ShapeDtypeStruct((FOX_WIDTH // LANES, SUBLANES, SEQ), F32)],
        scratch_shapes=[pltpu.VMEM((SEQ, LANES), F32)] * 2 + [pltpu.VMEM((SEQ, LANES), BF16)] * 3
        + [pltpu.SemaphoreType.DMA((3,))],
        input_output_aliases={6: 0},
        compiler_params=pltpu.CompilerParams(
            dimension_semantics=("arbitrary",), vmem_limit_bytes=_vmem_limit(10 * t * SEQ * 4)),
    )(*_in_hbm(proj, proj, proj, f_rows, do, lse, dproj))


MIX_TILE = 256


def _mix_out(out_a, out_b, proj, x, wt_pa, wt_pb, w_out, g_post, g_ffn_pre):
    tm = MIX_TILE

    def body(a_ref, b_ref, ga_ref, gb_ref, x_ref, wpa_ref, wpb_ref, wo_ref, g2_ref, g3_ref,
             merged_ref, mix_ref, x1_ref, h2_ref):
        ya = _dot_nn(a_ref[...].astype(BF16), wpa_ref[...])
        yb = _dot_nn(b_ref[...].astype(BF16), wpb_ref[...])
        merged = (_sigmoid(ga_ref[...]) * ya + _sigmoid(gb_ref[...]) * yb).astype(BF16)
        merged_ref[...] = merged
        mix = _dot_nn(merged, wo_ref[...])
        mix_ref[...] = mix
        x1 = x_ref[...] + mix * _rms_scale(mix) * g2_ref[...]
        x1_ref[...] = x1
        h2_ref[...] = (x1 * _rms_scale(x1) * g3_ref[...]).astype(BF16)

    def rows(w, cb=0):
        return pl.BlockSpec((tm, w), lambda i, cb=cb: (i, cb))

    def whole(a):
        return pl.BlockSpec(a.shape, lambda i: (0, 0))

    d = D_MODEL
    blk = _nbytes((tm, d), F32) * 6 + sum(_nbytes(a.shape, BF16) for a in (wt_pa, wt_pb, w_out))
    return pl.pallas_call(
        body, name="mix_out", grid=(SEQ // tm,),
        in_specs=[rows(DIL_OUT_WIDTH), rows(FOX_WIDTH), rows(d, COL_GA // d), rows(d, COL_GB // d), rows(d),
                  whole(wt_pa), whole(wt_pb), whole(w_out), whole(g_post), whole(g_ffn_pre)],
        out_specs=[rows(d)] * 4,
        out_shape=[jax.ShapeDtypeStruct((SEQ, d), dt) for dt in (BF16, F32, F32, BF16)],
        compiler_params=pltpu.CompilerParams(dimension_semantics=("parallel",), vmem_limit_bytes=_vmem_limit(blk)),
    )(*_in_hbm(out_a, out_b, proj, proj, x, wt_pa, wt_pb, w_out, g_post, g_ffn_pre))


def _mix_out_bwd(dmix, out_a, out_b, proj, wt_pa, wt_pb, w_out, deps=()):
    tm = MIX_TILE

    def body(dm_ref, a_ref, b_ref, ga_ref, gb_ref, wpa_ref, wpb_ref, wo_ref, *rest):
        dproj_ref, dya_ref, dyb_ref, da_ref, db_ref = rest[len(deps):]
        dmerged = _dot_nt(dm_ref[...], wo_ref[...])
        ya = _dot_nn(a_ref[...].astype(BF16), wpa_ref[...])
        yb = _dot_nn(b_ref[...].astype(BF16), wpb_ref[...])
        sa, sb = _sigmoid(ga_ref[...]), _sigmoid(gb_ref[...])
        dproj_ref[:, COL_GA:COL_GA + D_MODEL] = (dmerged * ya * (sa * (1.0 - sa))).astype(BF16)
        dproj_ref[:, COL_GB:COL_GB + D_MODEL] = (dmerged * yb * (sb * (1.0 - sb))).astype(BF16)
        dproj_ref[:, COL_GB + D_MODEL:] = jnp.zeros((tm, COL_QA - COL_GB - D_MODEL), BF16)
        dya = (dmerged * sa).astype(BF16)
        dyb = (dmerged * sb).astype(BF16)
        dya_ref[...] = dya
        dyb_ref[...] = dyb
        da_ref[...] = _dot_nt(dya, wpa_ref[...])
        db_ref[...] = _dot_nt(dyb, wpb_ref[...]).astype(BF16)

    def rows(w, cb=0):
        return pl.BlockSpec((tm, w), lambda i, cb=cb: (i, cb))

    def whole(a):
        return pl.BlockSpec(a.shape, lambda i: (0, 0))

    d = D_MODEL
    blk = _nbytes((tm, d), F32) * 8 + sum(_nbytes(a.shape, BF16) for a in (wt_pa, wt_pb, w_out))
    return pl.pallas_call(
        body, name="mix_out_bwd", grid=(SEQ // tm,),
        in_specs=[rows(d), rows(DIL_OUT_WIDTH), rows(FOX_WIDTH), rows(d, COL_GA // d), rows(d, COL_GB // d),
                  whole(wt_pa), whole(wt_pb), whole(w_out)] + [_ANY] * len(deps),
        out_specs=[rows(COL_QA)] + [rows(d)] * 2 + [rows(DIL_OUT_WIDTH), rows(FOX_WIDTH)],
        out_shape=[jax.ShapeDtypeStruct((SEQ, PROJ_COLS), BF16)] + [jax.ShapeDtypeStruct((SEQ, d), BF16)] * 2
        + [jax.ShapeDtypeStruct((SEQ, DIL_OUT_WIDTH), F32), jax.ShapeDtypeStruct((SEQ, FOX_WIDTH), BF16)],
        compiler_params=pltpu.CompilerParams(dimension_semantics=("parallel",), vmem_limit_bytes=_vmem_limit(blk)),
    )(*_in_hbm(dmix, out_a, out_b, proj, proj, wt_pa, wt_pb, w_out), *deps)


FFN_TM, FFN_TN = 2048, 256


def _ffn_up(h2, wt_gate, wt_up):
    tm, tn = FFN_TM, FFN_TN

    def body(h_ref, wg_ref, wu_ref, gate_ref, up_ref, act_ref):
        gate = _dot_nt(h_ref[...], wg_ref[...])
        up = _dot_nt(h_ref[...], wu_ref[...])
        gate_ref[...] = gate
        up_ref[...] = up
        act_ref[...] = (gate * _sigmoid(gate) * up).astype(BF16)

    tile = pl.BlockSpec((tm, tn), lambda i, j: (i, j))
    w_spec = pl.BlockSpec((tn, D_MODEL), lambda i, j: (j, 0))
    return pl.pallas_call(
        body, name="ffn_up", grid=(SEQ // tm, D_FF // tn),
        in_specs=[pl.BlockSpec((tm, D_MODEL), lambda i, j: (i, 0)), w_spec, w_spec],
        out_specs=[tile, tile, tile],
        out_shape=[jax.ShapeDtypeStruct((SEQ, D_FF), dt) for dt in (F32, F32, BF16)],
        compiler_params=pltpu.CompilerParams(
            dimension_semantics=("parallel", "parallel"), vmem_limit_bytes=_vmem_limit(8 * 2**20)),
    )(*_in_hbm(h2, wt_gate, wt_up))


def _ffn_act_bwd(dff, w_down, gate, up):
    tm, tn = FFN_TM, FFN_TN

    def body(d_ref, wd_ref, gate_ref, up_ref, dgate_ref, dup_ref):
        dact = _dot_nt(d_ref[...], wd_ref[...])
        gate = gate_ref[...]
        sg = _sigmoid(gate)
        dgate_ref[...] = (dact * up_ref[...] * (sg * (1.0 + gate * (1.0 - sg)))).astype(BF16)
        dup_ref[...] = (dact * (gate * sg)).astype(BF16)

    tile = pl.BlockSpec((tm, tn), lambda i, j: (i, j))
    return pl.pallas_call(
        body, name="ffn_act_bwd", grid=(SEQ // tm, D_FF // tn),
        in_specs=[pl.BlockSpec((tm, D_MODEL), lambda i, j: (i, 0)), pl.BlockSpec((tn, D_MODEL), lambda i, j: (j, 0)),
                  tile, tile],
        out_specs=[tile, tile],
        out_shape=[jax.ShapeDtypeStruct((SEQ, D_FF), BF16)] * 2,
        compiler_params=pltpu.CompilerParams(
            dimension_semantics=("parallel", "parallel"), vmem_limit_bytes=_vmem_limit(8 * 2**20)),
    )(*_in_hbm(dff, w_down, gate, up))


EPILOGUE_TM = 512


def _loss_head(act, w_down, x1, target, g_post):
    def fn(ff, x1, tgt, g):
        r = _rms_scale(ff)
        nrm = ff * r
        err = (x1 + nrm * g) - tgt
        loss = 0.5 * jnp.sum(jnp.mean(err * err, axis=-1, keepdims=True), axis=0, keepdims=True)
        dy = err * (1.0 / D_MODEL)
        u = dy * g
        dff = r * u - ff * (r * r * r) * jnp.mean(u * ff, axis=-1, keepdims=True)
        return dy, dff, jnp.broadcast_to(loss, (1, LANES)), jnp.sum(dy * nrm, axis=0, keepdims=True)

    d = D_MODEL
    return _matmul_rowwise([(act, w_down)], fn, "ffn_down_loss", EPILOGUE_TM, [(x1, d, 0), (target, d, 0)], [g_post],
                           [(d, F32), (d, BF16)], [LANES, d])


def _post_ffn_bwd(dgate, wt_gate, dup, wt_up, x1, dy, mix, g_ffn_pre, g_mix_post, deps=()):
    def fn(dh2, x1, dy, mix, g3, g2):
        dx, dg3 = _rms_bwd(x1, dh2, g3)
        dx1 = dy + dx
        dmix, dg2 = _rms_bwd(mix, dx1, g2)
        return dx1, dmix, dg3, dg2

    d = D_MODEL
    return _matmul_rowwise([(dgate, wt_gate), (dup, wt_up)], fn, "ffn_up_bwd", EPILOGUE_TM,
                           [(x1, d, 0), (dy, d, 0), (mix, d, 0)], [g_ffn_pre, g_mix_post],
                           [(d, F32), (d, BF16)], [d, d], deps=deps)


def _input_bwd(dproj, wt_r, x, dx1, g_pre, deps=()):
    def fn(dh, x, dx1, g):
        dx, dg = _rms_bwd(x, dh, g)
        return dx1 + dx, dg

    d = D_MODEL
    return _matmul_rowwise([(dproj, wt_r)], fn, "in_proj_bwd", EPILOGUE_TM, [(x, d, 0), (dx1, d, 0)], [g_pre],
                           [(d, F32)], [d], deps=deps)


def _adam_math(w, g, m, v):
    m = ADAM_B1 * m + (1.0 - ADAM_B1) * g
    v = ADAM_B2 * v + (1.0 - ADAM_B2) * (g * g)
    m_hat = m / (1.0 - ADAM_B1 ** ADAM_STEP)
    v_hat = v / (1.0 - ADAM_B2 ** ADAM_STEP)
    delta = -ADAM_LR * (m_hat / (jnp.sqrt(v_hat) + ADAM_EPS) + ADAM_WD * w)
    return delta, m, v


def _adam(w, g, m, v, name):
    r, c = w.shape
    tc = _col_tile(r, c)

    def body(w_ref, g_ref, m_ref, v_ref, d_ref, nm_ref, nv_ref):
        d_ref[...], nm_ref[...], nv_ref[...] = _adam_math(w_ref[...], g_ref[...], m_ref[...], v_ref[...])

    spec = pl.BlockSpec((r, tc), lambda j: (0, j))
    return pl.pallas_call(
        body, name=name, grid=(c // tc,), in_specs=[spec] * 4, out_specs=[spec] * 3,
        out_shape=[jax.ShapeDtypeStruct((r, c), F32)] * 3,
        compiler_params=pltpu.CompilerParams(dimension_semantics=("parallel",)),
    )(*_in_hbm(w, g, m, v))


def _adam_small(gathered, ws, ms, vs, loss_parts):
    n = len(ws)

    def body(*refs):
        outs = refs[4 * n + 1:]
        loss = refs[4 * n][0]
        for dev in range(1, N_DEV):
            loss = loss + refs[4 * n][dev]
        outs[4 * n][...] = loss
        for i in range(n):
            ga_ref, w_ref, m_ref, v_ref = (refs[j * n + i] for j in range(4))
            g = ga_ref[0]
            for dev in range(1, N_DEV):
                g = g + ga_ref[dev]
            g = g[:, :w_ref.shape[1]]
            outs[4 * i][...] = g
            outs[4 * i + 1][...], outs[4 * i + 2][...], outs[4 * i + 3][...] = _adam_math(
                w_ref[...], g, m_ref[...], v_ref[...])

    out_shape = [jax.ShapeDtypeStruct(w.shape, F32) for w in ws for _ in range(4)]
    out_shape.append(jax.ShapeDtypeStruct((1, LANES), F32))
    out = pl.pallas_call(body, name="adam_small", out_shape=out_shape)(*gathered, *ws, *ms, *vs, loss_parts)
    return [out[4 * i:4 * i + 4] for i in range(n)], out[4 * n]


_PROJ_SEGMENTS = ((3848, 5896), (None, COL_QA - 2 * D_MODEL), (0, 3840), (3840, 3848), (None, PROJ_COLS - COL_F - 8))


def _proj_weight_t(gathered):
    w = gathered.reshape(IN_COLS, D_MODEL)
    return jnp.concatenate([jnp.zeros((hi, D_MODEL), w.dtype) if lo is None else w[lo:hi] for lo, hi in _PROJ_SEGMENTS],
                           axis=0)


def _proj_weight_grad_slots(dwt_r):
    starts, at = [], 0
    for lo, hi in _PROJ_SEGMENTS:
        if lo is not None:
            starts.append((lo, hi, at))
        at += hi if lo is None else hi - lo
    slots = []
    for dev in range(N_DEV):
        pieces, lo, end = [], dev * IN_SHARD, (dev + 1) * IN_SHARD
        for seg_lo, seg_hi, seg_at in sorted(starts):
            a, b = max(lo, seg_lo), min(end, seg_hi)
            if a < b:
                pieces.append(dwt_r[seg_at + a - seg_lo:seg_at + b - seg_lo])
        slots.append(pieces[0] if len(pieces) == 1 else jnp.concatenate(pieces, axis=0))
    return jnp.stack(slots)


def kernel(x, w_in, w_proj_a, w_proj_b, w_out, b_forget, w_ffn_gate, w_ffn_up, w_ffn_down, norm_mix_pre, norm_mix_post, norm_ffn_pre, norm_ffn_post, loss_target, m_w_in, m_w_proj_a, m_w_proj_b, m_w_out, m_b_forget, m_w_ffn_gate, m_w_ffn_up, m_w_ffn_down, m_norm_mix_pre, m_norm_mix_post, m_norm_ffn_pre, m_norm_ffn_post, v_w_in, v_w_proj_a, v_w_proj_b, v_w_out, v_b_forget, v_w_ffn_gate, v_w_ffn_up, v_w_ffn_down, v_norm_mix_pre, v_norm_mix_post, v_norm_ffn_pre, v_norm_ffn_post):
    d = D_MODEL
    names = ("w_in", "w_proj_a", "w_proj_b", "w_out", "w_ffn_gate", "w_ffn_up", "w_ffn_down")
    col_sharded = ("w_in", "w_ffn_gate", "w_ffn_up")

    def row_shards(arrs):
        return {k: (a[0].T if k in col_sharded else a[0]) for k, a in zip(names, arrs)}

    shards = row_shards((w_in, w_proj_a, w_proj_b, w_out, w_ffn_gate, w_ffn_up, w_ffn_down))
    moments_m = row_shards((m_w_in, m_w_proj_a, m_w_proj_b, m_w_out, m_w_ffn_gate, m_w_ffn_up, m_w_ffn_down))
    moments_v = row_shards((v_w_in, v_w_proj_a, v_w_proj_b, v_w_out, v_w_ffn_gate, v_w_ffn_up, v_w_ffn_down))
    core = lax.axis_index("c").astype(jnp.int32).reshape(1)
    chip = (2 * lax.axis_index("x") + lax.axis_index("y")).astype(jnp.int32).reshape(1)
    x2, target = x[0], loss_target[0]

    me = 4 * lax.axis_index("x") + 2 * lax.axis_index("y") + lax.axis_index("c")
    mix_names, ffn_names = names[:4], names[4:]
    first_names, later_names = names[:1], names[1:]
    shards16 = {k: shards[k].astype(BF16) for k in names}

    def landing(k):
        return lax.dynamic_update_slice(lax.empty((N_DEV,) + shards[k].shape, BF16), shards16[k][None], (me, 0, 0))

    ag_first = _exchange_start("ag_first_chips_start", _gather_chips_copies, [shards16[k] for k in first_names],
                               [landing(k) for k in first_names], 3 * len(first_names))
    h = _rowwise(lambda xb, g: xb * _rms_scale(xb) * g, "norm_mix_pre", SEQ, 256, [(x2, d, 0)], [norm_mix_pre],
                 [(d, BF16)], deps=[ag_first.token])[0]
    _, lands = _exchange_wait("ag_first_chips_wait", ag_first, [h])
    ag_first = _exchange_start("ag_first_sibling_start", _gather_sibling_copies, [], lands, 4 * len(first_names))
    ag_later = _exchange_start("ag_later_chips_start", _gather_chips_copies, [shards16[k] for k in later_names],
                               [landing(k) for k in later_names], 3 * len(later_names), after=[ag_first.token])
    gathered = dict(zip(first_names, _exchange_wait("ag_first_sibling_wait", ag_first, [ag_later.token])[1]))
    wt_r = _proj_weight_t(gathered["w_in"])

    proj = _matmul([(h, wt_r)], "nt", F32, "in_proj", 1024, 896, 1024)
    tables = _rope_tables()
    o_dil, lse_dil = zip(*[_dil_fwd(g, proj, tables) for g in range(len(DILATIONS))])
    out_a = _dil_combine(o_dil, lse_dil)
    _, lands = _exchange_wait("ag_later_chips_wait", ag_later, [out_a])
    ag_later = _exchange_start("ag_later_sibling_start", _gather_sibling_copies, [], lands, 4 * len(later_names))

    b_pad = jnp.pad(b_forget, ((0, 0), (0, LANES - N_FOX_HEADS)))
    f_rows = _fox_gate(proj, b_pad, deps=[ag_later.token])
    out_b, lse_fox = _fox_fwd(proj, f_rows)

    gathered = dict(zip(later_names, _exchange_wait("ag_later_sibling_wait", ag_later, [out_b])[1]))
    wt_pa = gathered["w_proj_a"].transpose(1, 0, 2).reshape(DIL_OUT_WIDTH, d)
    wt_pb = gathered["w_proj_b"].transpose(1, 0, 2).reshape(FOX_WIDTH, d)
    w_o = gathered["w_out"].reshape(d, d)
    wt_g = gathered["w_ffn_gate"].reshape(D_FF, d)
    wt_u = gathered["w_ffn_up"].reshape(D_FF, d)
    w_d = gathered["w_ffn_down"].reshape(D_FF, d)
    merged, mix, x1, h2 = _mix_out(out_a, out_b, proj, x2, wt_pa, wt_pb, w_o, norm_mix_post, norm_ffn_pre)

    gate, up, act = _ffn_up(h2, wt_g, wt_u)
    dy, dff, loss_part, dg_ffn_post = _loss_head(act, w_d, x1, target, norm_ffn_post)

    dgate, dup = _ffn_act_bwd(dff, w_d, gate, up)
    grads_t = {}
    grads_t["w_ffn_down"] = _matmul([(act, dff)], "tn", F32, "grad_w_ffn_down", 1408, 1024, 2048)
    grads_t["w_ffn_gate"] = _matmul([(dgate, h2)], "tn", F32, "grad_w_ffn_gate", 1408, 1024, 2048)
    grads_t["w_ffn_up"] = _matmul([(dup, h2)], "tn", F32, "grad_w_ffn_up", 1408, 1024, 2048)
    rs_ffn = _ReduceScatter("ffn", {k: grads_t[k] for k in ffn_names}, core, chip)
    dx1, dmix, dg_ffn_pre, dg_mix_post = _post_ffn_bwd(dgate, wt_g, dup, wt_u, x1, dy, mix, norm_ffn_pre, norm_mix_post,
                                                       deps=[rs_ffn.token])
    rs_ffn.start_chips([dmix])

    dproj, dya, dyb, d_out_a, d_out_b = _mix_out_bwd(dmix, out_a, out_b, proj, wt_pa, wt_pb, w_o, deps=[rs_ffn.token])
    grads_t["w_out"] = _matmul([(merged, dmix)], "tn", F32, "grad_w_out", 1024, 1024, 1024)
    def column_slots(g):
        return g.reshape(g.shape[0], N_DEV, LANES).transpose(1, 0, 2)

    grads_t["w_proj_a"] = column_slots(_matmul([(out_a, dya)], "tn", F32, "grad_w_proj_a", DIL_OUT_WIDTH, 1024, SEQ))
    grads_t["w_proj_b"] = column_slots(_matmul([(out_b, dyb)], "tn", F32, "grad_w_proj_b", FOX_WIDTH, 1024, SEQ))

    dproj, d_cum = _fox_bwd(proj, d_out_b, lse_fox, f_rows, dproj)
    d_cum_rows = jnp.pad(d_cum[:, :2].reshape(N_FOX_HEADS, SEQ), ((0, F_ROWS - N_FOX_HEADS), (0, 0)))
    dproj, db_part = _fox_gate_bwd(d_cum_rows, proj, b_pad, dproj)

    do_dil, c_dil = _dil_combine_bwd(d_out_a, o_dil, lse_dil)
    for g in range(len(DILATIONS)):
        dproj = _dil_bwd(g, proj, tables, do_dil[g], lse_dil[g], c_dil[g], dproj)
    dwt_r = _matmul([(dproj, h)], "tn", F32, "grad_w_in", 896, 1024, 2048)
    grads_t["w_in"] = _proj_weight_grad_slots(dwt_r)
    rs_mix = _ReduceScatter("mix", {k: grads_t[k] for k in mix_names}, core, chip)
    grads = rs_ffn.finish([rs_mix.token])
    big = {k: _adam(shards[k], grads[k], moments_m[k], moments_v[k], "adam_" + k) for k in ffn_names}
    rs_mix.start_chips([big[k][0] for k in ffn_names])
    grad_x, dg_mix_pre = _input_bwd(dproj, wt_r, x2, dx1, norm_mix_pre, deps=[rs_mix.token])

    small_all = _all_gather([dg_mix_pre, dg_mix_post, dg_ffn_pre, dg_ffn_post, db_part, loss_part],
                            "small_grads_all_gather")
    small, loss = _adam_small(small_all[:5], [norm_mix_pre, norm_mix_post, norm_ffn_pre, norm_ffn_post, b_forget],
                              [m_norm_mix_pre, m_norm_mix_post, m_norm_ffn_pre, m_norm_ffn_post, m_b_forget],
                              [v_norm_mix_pre, v_norm_mix_post, v_norm_ffn_pre, v_norm_ffn_post, v_b_forget],
                              small_all[5])

    grads.update(rs_mix.finish([small[0][0], grad_x]))
    big.update({k: _adam(shards[k], grads[k], moments_m[k], moments_v[k], "adam_" + k) for k in mix_names})

    def leaves(i):
        def nat(k):
            a = grads[k] if i == 0 else big[k][i - 1]
            return (a.T if k in col_sharded else a)[None]

        return [nat("w_in"), nat("w_proj_a"), nat("w_proj_b"), nat("w_out"), small[4][i],
                nat("w_ffn_gate"), nat("w_ffn_up"), nat("w_ffn_down"), *[small[r][i] for r in range(4)]]

    return (loss[0, 0], grad_x[None], *leaves(0), *leaves(1), *leaves(2), *leaves(3))
```

```python
import functools
import math

import jax
import jax.numpy as jnp
import numpy as np
from jax import lax
from jax.experimental import pallas as pl
from jax.experimental.pallas import tpu as pltpu

F32 = jnp.float32
BF16 = jnp.bfloat16
MESH = pl.DeviceIdType.MESH

D_MODEL = 1024
SEQ = 2048
HEAD_DIM = 64
BLOCK = 128
N_BLOCKS = SEQ // BLOCK
DILATIONS = (1, 4, 16)
N_FOX_HEADS = 8
DIL_WIDTH = 768
DIL_OUT_WIDTH = 256
FOX_WIDTH = 512
D_FF = 2816
ROPE_THETA = 500000.0
ROPE_DIM = HEAD_DIM // 4
ROPE_HALF = ROPE_DIM // 2
EPS = 1e-6
NEG_INF = -1e30
QK_SCALE = 1.0 / math.sqrt(HEAD_DIM)
IN_COLS = 5896
N_DEV = 8
IN_SHARD = IN_COLS // N_DEV

ADAM_LR = 0.001
ADAM_B1 = 0.9
ADAM_B2 = 0.999
ADAM_EPS = 1e-08
ADAM_WD = 0.01
ADAM_STEP = 10

V7X_VMEM_BYTES = 64 * 2**20
LANES = 128
SUBLANES = 8

PROJ_COLS = 6272
COL_GA, COL_GB = 0, 1024
COL_QA, COL_KA, COL_VA = 2304, 3072, 3840
COL_QB, COL_KB, COL_VB = 4608, 5120, 5632
COL_F = 6144
F_ROWS = 16


def _vmem_limit(block_bytes):
    want = 2 * block_bytes + 16 * 2**20
    return int(min(max(want, 32 * 2**20), V7X_VMEM_BYTES - 8 * 2**20))


def _nbytes(shape, dtype):
    return math.prod(shape) * jnp.dtype(dtype).itemsize


def _in_hbm(*arrays):
    return [pltpu.with_memory_space_constraint(a, pltpu.HBM) for a in arrays]


def _dot(a, b, dims):
    return lax.dot_general(a, b, (dims, ((), ())), preferred_element_type=F32)


def _dot_nn(a, b):
    return _dot(a, b, ((1,), (0,)))


def _dot_nt(a, b):
    return _dot(a, b, ((1,), (1,)))


def _dot_tn(a, b):
    return _dot(a, b, ((0,), (0,)))


def _sigmoid(z):
    return 1.0 / (1.0 + jnp.exp(-z))


def _split3(x):
    hi = x.astype(BF16)
    r1 = x - hi.astype(F32)
    mid = r1.astype(BF16)
    lo = (r1 - mid.astype(F32)).astype(BF16)
    return hi, mid, lo


def _dot3_nn(x, ones_matrix):
    hi, mid, lo = _split3(x)
    return (_dot_nn(hi, ones_matrix) + _dot_nn(mid, ones_matrix)) + _dot_nn(lo, ones_matrix)


def _rowwise(fn, name, n_rows, tm, row_ins, bcast_ins, row_outs, acc_outs=(), deps=()):
    n_in = len(row_ins) + len(bcast_ins)
    n_ro = len(row_outs)

    def body(*refs):
        res = fn(*[r[...] for r in refs[:n_in]])
        if not isinstance(res, (tuple, list)):
            res = (res,)
        outs = refs[n_in + len(deps):]
        for r, o in zip(res[:n_ro], outs[:n_ro]):
            o[...] = r.astype(o.dtype)
        first = pl.program_id(0) == 0
        for r, o in zip(res[n_ro:], outs[n_ro:]):
            _accumulate(o, r, first)

    in_specs = [pl.BlockSpec((tm, w), lambda i, cb=cb: (i, cb)) for _, w, cb in row_ins]
    in_specs += [pl.BlockSpec(a.shape, lambda i: (0, 0)) for a in bcast_ins]
    in_specs += [pl.BlockSpec(memory_space=pl.ANY)] * len(deps)
    out_specs = [pl.BlockSpec((tm, w), lambda i: (i, 0)) for w, _ in row_outs]
    out_specs += [pl.BlockSpec((1, w), lambda i: (0, 0)) for w in acc_outs]
    out_shape = [jax.ShapeDtypeStruct((n_rows, w), dt) for w, dt in row_outs]
    out_shape += [jax.ShapeDtypeStruct((1, w), F32) for w in acc_outs]
    blk = sum(_nbytes((tm, w), a.dtype) for a, w, _ in row_ins) + sum(_nbytes((tm, w), dt) for w, dt in row_outs)
    return pl.pallas_call(
        body, name=name, grid=(n_rows // tm,), in_specs=in_specs, out_specs=out_specs, out_shape=out_shape,
        compiler_params=pltpu.CompilerParams(
            dimension_semantics=("arbitrary" if acc_outs else "parallel",), vmem_limit_bytes=_vmem_limit(3 * blk)),
    )(*_in_hbm(*[a for a, _, _ in row_ins], *bcast_ins), *deps)


def _accumulate(o_ref, part, first):
    @pl.when(first)
    def _():
        o_ref[...] = part

    @pl.when(jnp.logical_not(first))
    def _():
        o_ref[...] += part


_MM_DIMS = {"nn": ((1,), (0,)), "nt": ((1,), (1,)), "tn": ((0,), (0,))}


def _matmul(pairs, mode, out_dtype, name, tm, tn, tk, deps=()):
    a0, b0 = pairs[0]
    if mode == "tn":
        kk, m = a0.shape
    else:
        m, kk = a0.shape
    n = b0.shape[0] if mode == "nt" else b0.shape[1]
    assert m % tm == 0 and n % tn == 0 and kk % tk == 0, (name, m, n, kk)
    nk = kk // tk
    n_pairs = len(pairs)
    dims = _MM_DIMS[mode]
    n_in = 2 * n_pairs + len(deps)

    def body(*refs):
        o_ref = refs[n_in]
        part = None
        for p in range(n_pairs):
            d = _dot(refs[2 * p][...].astype(BF16), refs[2 * p + 1][...].astype(BF16), dims)
            part = d if part is None else part + d
        if nk == 1:
            o_ref[...] = part.astype(o_ref.dtype)
            return
        acc = refs[n_in + 1]
        k = pl.program_id(2)

        @pl.when(k == 0)
        def _():
            acc[...] = part

        @pl.when(k > 0)
        def _():
            acc[...] += part

        @pl.when(k == nk - 1)
        def _():
            o_ref[...] = acc[...].astype(o_ref.dtype)

    if mode == "tn":
        a_spec = pl.BlockSpec((tk, tm), lambda i, j, k: (k, i))
    else:
        a_spec = pl.BlockSpec((tm, tk), lambda i, j, k: (i, k))
    if mode == "nt":
        b_spec = pl.BlockSpec((tn, tk), lambda i, j, k: (j, k))
    else:
        b_spec = pl.BlockSpec((tk, tn), lambda i, j, k: (k, j))
    blk = sum(_nbytes((tm, tk), a.dtype) + _nbytes((tk, tn), b.dtype) for a, b in pairs) + 2 * _nbytes((tm, tn), F32)
    flat = [a for pair in pairs for a in pair]
    return pl.pallas_call(
        body, name=name, grid=(m // tm, n // tn, nk),
        in_specs=[a_spec, b_spec] * n_pairs + [pl.BlockSpec(memory_space=pl.ANY)] * len(deps),
        out_specs=pl.BlockSpec((tm, tn), lambda i, j, k: (i, j)),
        out_shape=jax.ShapeDtypeStruct((m, n), out_dtype),
        scratch_shapes=[] if nk == 1 else [pltpu.VMEM((tm, tn), F32)],
        compiler_params=pltpu.CompilerParams(
            dimension_semantics=("parallel", "parallel", "arbitrary"), vmem_limit_bytes=_vmem_limit(blk)),
    )(*_in_hbm(*flat), *deps)


def _matmul_rowwise(pairs, fn, name, tm, row_ins, bcast_ins, row_outs, acc_outs=(), deps=()):
    m = pairs[0][0].shape[0]
    n_mm, n_in = 2 * len(pairs), len(row_ins) + len(bcast_ins)
    n_ro = len(row_outs)

    def body(*refs):
        prod = None
        for p in range(len(pairs)):
            part = _dot_nn(refs[2 * p][...].astype(BF16), refs[2 * p + 1][...].astype(BF16))
            prod = part if prod is None else prod + part
        res = fn(prod, *[r[...] for r in refs[n_mm:n_mm + n_in]])
        outs = refs[n_mm + n_in + len(deps):]
        for r, o in zip(res[:n_ro], outs[:n_ro]):
            o[...] = r.astype(o.dtype)
        first = pl.program_id(0) == 0
        for r, o in zip(res[n_ro:], outs[n_ro:]):
            _accumulate(o, r, first)

    in_specs = []
    for a, b in pairs:
        in_specs += [pl.BlockSpec((tm, a.shape[1]), lambda i: (i, 0)),
                     pl.BlockSpec(b.shape, lambda i: (0, 0), pipeline_mode=pl.Buffered(1))]
    in_specs += [pl.BlockSpec((tm, w), lambda i, cb=cb: (i, cb)) for _, w, cb in row_ins]
    in_specs += [pl.BlockSpec(a.shape, lambda i: (0, 0)) for a in bcast_ins]
    in_specs += [_ANY] * len(deps)
    out_specs = [pl.BlockSpec((tm, w), lambda i: (i, 0)) for w, _ in row_outs]
    out_specs += [pl.BlockSpec((1, w), lambda i: (0, 0)) for w in acc_outs]
    out_shape = [jax.ShapeDtypeStruct((m, w), dt) for w, dt in row_outs]
    out_shape += [jax.ShapeDtypeStruct((1, w), F32) for w in acc_outs]
    blk = sum(_nbytes((tm, a.shape[1]), a.dtype) + _nbytes(b.shape, b.dtype) // 2 for a, b in pairs)
    blk += sum(_nbytes((tm, w), a.dtype) for a, w, _ in row_ins) + sum(_nbytes((tm, w), dt) for w, dt in row_outs)
    return pl.pallas_call(
        body, name=name, grid=(m // tm,), in_specs=in_specs, out_specs=out_specs, out_shape=out_shape,
        compiler_params=pltpu.CompilerParams(dimension_semantics=("arbitrary",), vmem_limit_bytes=_vmem_limit(blk)),
    )(*_in_hbm(*[a for pair in pairs for a in pair], *[a for a, _, _ in row_ins], *bcast_ins), *deps)


def _rms_scale(x):
    return lax.rsqrt(jnp.mean(x * x, axis=-1, keepdims=True) + EPS)


def _rms_bwd(xin, dyn, g):
    r = _rms_scale(xin)
    u = dyn * g
    dx = r * u - xin * (r * r * r) * jnp.mean(u * xin, axis=-1, keepdims=True)
    dg = jnp.sum(dyn * xin * r, axis=0, keepdims=True)
    return dx, dg


def _mesh_pos():
    return lax.axis_index("x"), lax.axis_index("y"), lax.axis_index("c")


def _all_gather(xs, name, deps=()):
    n = len(xs)

    def body(*refs):
        x_refs, out_refs = refs[:n], refs[n + len(deps):2 * n + len(deps)]
        send_sems, recv_sems, local_sems = refs[2 * n + len(deps):]
        mx, my, mc = _mesh_pos()
        me, sib = (mx, my, mc), (mx, my, 1 - mc)
        chips = [(1 - mx, my), (mx, 1 - my), (1 - mx, 1 - my)]

        def slot(a, dev):
            px, py, pc = dev
            return out_refs[a].at[4 * px + 2 * py + pc]

        def copy(k, a, block, to, src=None):
            return pltpu.make_async_remote_copy(
                src_ref=slot(a, block) if src is None else src, dst_ref=slot(a, block),
                send_sem=send_sems.at[a * 7 + k], recv_sem=recv_sems.at[a * 7 + k],
                device_id=to, device_id_type=MESH)

        mine = [pltpu.make_async_copy(x_refs[a], slot(a, me), local_sems.at[a]) for a in range(n)]
        for cp in mine:
            cp.start()
        first = []
        for a in range(n):
            first.append(copy(0, a, me, sib, x_refs[a]))
            first += [copy(1 + j, a, me, (*chip, mc), x_refs[a]) for j, chip in enumerate(chips)]
        for cp in first:
            cp.start()
        passed = []
        for a in range(n):
            for j, chip in enumerate(chips):
                copy(1 + j, a, (*chip, mc), me).wait_recv()
                fwd = copy(4 + j, a, (*chip, mc), sib)
                fwd.start()
                passed.append(fwd)
        for a in range(n):
            copy(0, a, sib, me).wait_recv()
            for j, chip in enumerate(chips):
                copy(4 + j, a, (*chip, 1 - mc), me).wait_recv()
        for cp in first + passed:
            cp.wait_send()
        for cp in mine:
            cp.wait()

    hbm = pl.BlockSpec(memory_space=pl.ANY)
    return pl.pallas_call(
        body, name=name,
        out_shape=[jax.ShapeDtypeStruct((N_DEV,) + x.shape, x.dtype) for x in xs],
        in_specs=[hbm] * (n + len(deps)), out_specs=[hbm] * n,
        scratch_shapes=[pltpu.SemaphoreType.DMA((7 * n,)), pltpu.SemaphoreType.DMA((7 * n,)),
                        pltpu.SemaphoreType.DMA((n,))],
    )(*xs, *deps)


_HBM = pl.BlockSpec(memory_space=pltpu.HBM)
_SEM = pl.BlockSpec(memory_space=pltpu.SEMAPHORE)
_ANY = pl.BlockSpec(memory_space=pl.ANY)
_DATAFLOW = pltpu.SideEffectType.DATAFLOW_SIDE_EFFECTING


def _flip_peer(flip):
    mx, my, mc = _mesh_pos()
    return (1 - mx if flip & 2 else mx, 1 - my if flip & 1 else my, mc)


def _remote(src, dst, send_sems, recv_sems, k, peer):
    return pltpu.make_async_remote_copy(src_ref=src, dst_ref=dst, send_sem=send_sems.at[k], recv_sem=recv_sems.at[k],
                                        device_id=peer, device_id_type=MESH)


def _gather_chips_copies(srcs, lands, send_sems, recv_sems):
    mx, my, mc = _mesh_pos()
    me = 4 * mx + 2 * my + mc
    return [_remote(srcs[a], lands[a].at[me], send_sems, recv_sems, 3 * a + flip - 1, _flip_peer(flip))
            for a in range(len(srcs)) for flip in (1, 2, 3)]


def _gather_sibling_copies(srcs, lands, send_sems, recv_sems):
    mx, my, mc = _mesh_pos()
    return [_remote(lands[a].at[2 * k + mc], lands[a].at[2 * k + mc], send_sems, recv_sems, 4 * a + k, (mx, my, 1 - mc))
            for a in range(len(lands)) for k in range(4)]


def _scatter_sibling_copies(srcs, lands, send_sems, recv_sems):
    mx, my, mc = _mesh_pos()
    return [_remote(srcs[a].at[k, 1 - mc], lands[a].at[k], send_sems, recv_sems, 4 * a + k, (mx, my, 1 - mc))
            for a in range(len(srcs)) for k in range(4)]


def _scatter_chips_copies(srcs, lands, send_sems, recv_sems):
    mx, my, _ = _mesh_pos()
    k0 = 2 * mx + my
    return [_remote(srcs[a].at[jnp.bitwise_xor(k0, flip)], lands[a].at[flip - 1], send_sems, recv_sems,
                    3 * a + flip - 1, _flip_peer(flip))
            for a in range(len(srcs)) for flip in (1, 2, 3)]


class _Exchange:
    def __init__(self, copies, n_src, send_sems, recv_sems, thru, token):
        self.copies, self.n_src, self.send_sems, self.recv_sems, self.thru, self.token = (
            copies, n_src, send_sems, recv_sems, thru, token)


def _exchange_start(name, copies, srcs, lands, n_copies, after=()):
    bufs = list(srcs) + list(lands)
    nb, ns = len(bufs), len(srcs)

    def body(*refs):
        send_sems, recv_sems = refs[nb + len(after)], refs[nb + len(after) + 1]
        for cp in copies(refs[:ns], refs[ns:nb], send_sems, recv_sems):
            cp.start()
        refs[-1][...] = jnp.zeros_like(refs[-1])

    out = pl.pallas_call(
        body, name=name,
        out_shape=(pltpu.SemaphoreType.DMA((n_copies,)), pltpu.SemaphoreType.DMA((n_copies,)),
                   *[pltpu.HBM(b.shape, b.dtype) for b in bufs], jax.ShapeDtypeStruct((SUBLANES, LANES), F32)),
        in_specs=[_HBM] * nb + [_ANY] * len(after),
        out_specs=(_SEM, _SEM, *[_HBM] * nb, pl.BlockSpec(memory_space=pltpu.VMEM)),
        input_output_aliases={i: 2 + i for i in range(nb)},
        compiler_params=pltpu.CompilerParams(has_side_effects=_DATAFLOW),
    )(*[pltpu.with_memory_space_constraint(b, pltpu.HBM) for b in bufs], *after)
    return _Exchange(copies, ns, out[0], out[1], list(out[2:2 + nb]), out[-1])


def _exchange_wait(name, ex, after):
    nb, ns = len(ex.thru), ex.n_src

    def body(*refs):
        for cp in ex.copies(refs[:ns], refs[ns:nb], refs[nb], refs[nb + 1]):
            cp.wait_send()
            cp.wait_recv()

    out = pl.pallas_call(
        body, name=name, out_shape=tuple(pltpu.HBM(b.shape, b.dtype) for b in ex.thru),
        in_specs=[_HBM] * nb + [_SEM, _SEM] + [_ANY] * len(after), out_specs=tuple([_HBM] * nb),
        input_output_aliases={i: i for i in range(nb)},
        compiler_params=pltpu.CompilerParams(has_side_effects=_DATAFLOW),
    )(*ex.thru, ex.send_sems, ex.recv_sems, *after)
    return list(out[:ns]), list(out[ns:])


def _col_tile(r, c):
    return next(t for t in (1024, 512, 256, 128) if c % t == 0 and (r * t * 4 <= 2**20 or t == 128))


def _add_sibling(g4, recv, core, name):
    _, _, r, c = g4.shape
    tc = _col_tile(r, c)

    def body(core_ref, g_ref, r_ref, o16_ref, o32_ref):
        s = g_ref[0, 0] + r_ref[0]
        o16_ref[0] = s.astype(BF16)
        o32_ref[0] = s

    out = pl.BlockSpec((1, r, tc), lambda k, j, core_ref: (k, 0, j))
    return pl.pallas_call(
        body, name=name,
        out_shape=[jax.ShapeDtypeStruct((4, r, c), BF16), jax.ShapeDtypeStruct((4, r, c), F32)],
        grid_spec=pltpu.PrefetchScalarGridSpec(
            num_scalar_prefetch=1, grid=(4, c // tc),
            in_specs=[pl.BlockSpec((1, 1, r, tc), lambda k, j, core_ref: (k, core_ref[0], 0, j)), out],
            out_specs=[out, out]),
        compiler_params=pltpu.CompilerParams(dimension_semantics=("parallel", "parallel")),
    )(core, *_in_hbm(g4, recv))


def _add_chips(p32, recv, chip, name):
    _, r, c = p32.shape
    tc = _col_tile(r, c)

    def body(chip_ref, p_ref, r_ref, o_ref):
        o_ref[...] = ((p_ref[0] + r_ref[0].astype(F32)) + r_ref[1].astype(F32)) + r_ref[2].astype(F32)

    return pl.pallas_call(
        body, name=name, out_shape=jax.ShapeDtypeStruct((r, c), F32),
        grid_spec=pltpu.PrefetchScalarGridSpec(
            num_scalar_prefetch=1, grid=(c // tc,),
            in_specs=[pl.BlockSpec((1, r, tc), lambda j, chip_ref: (chip_ref[0], 0, j)),
                      pl.BlockSpec((3, r, tc), lambda j, chip_ref: (0, 0, j))],
            out_specs=pl.BlockSpec((r, tc), lambda j, chip_ref: (0, j))),
        compiler_params=pltpu.CompilerParams(dimension_semantics=("parallel",)),
    )(chip, *_in_hbm(p32, recv))


class _ReduceScatter:
    def __init__(self, tag, grads_t, core, chip):
        self.tag, self.core, self.chip, self.names = tag, core, chip, list(grads_t)
        g4s = [g.reshape(4, 2, g.size // (N_DEV * g.shape[-1]), g.shape[-1]) for g in grads_t.values()]
        lands = [lax.empty((4,) + g.shape[2:], F32) for g in g4s]
        self.ex = _exchange_start(f"rs_{tag}_sibling_start", _scatter_sibling_copies, g4s, lands, 4 * len(g4s))
        self.token = self.ex.token

    def start_chips(self, after):
        g4s, from_sibling = _exchange_wait(f"rs_{self.tag}_sibling_wait", self.ex, after)
        parts = [_add_sibling(g4, rv, self.core, f"rs_add_sibling_{k}")
                 for k, g4, rv in zip(self.names, g4s, from_sibling)]
        self.p32s = [p32 for _, p32 in parts]
        p16s = [p16 for p16, _ in parts]
        lands = [lax.empty((3,) + p.shape[1:], BF16) for p in p16s]
        self.ex = _exchange_start(f"rs_{self.tag}_chips_start", _scatter_chips_copies, p16s, lands, 3 * len(p16s))
        self.token = self.ex.token

    def finish(self, after):
        _, from_chips = _exchange_wait(f"rs_{self.tag}_chips_wait", self.ex, after)
        return {k: _add_chips(p32, rv, self.chip, f"rs_add_chips_{k}")
                for k, p32, rv in zip(self.names, self.p32s, from_chips)}


def _rope_tables():
    positions = np.arange(SEQ, dtype=np.float32)
    inv_freq = np.power(np.float32(ROPE_THETA), -np.arange(0, ROPE_DIM, 2, dtype=np.float32) / np.float32(ROPE_DIM))
    ang = (positions[:, None] * inv_freq[None, :]).astype(np.float32)
    cos, sin = np.cos(ang).astype(np.float32), np.sin(ang).astype(np.float32)
    ones = np.ones((SEQ, HEAD_DIM - ROPE_DIM), np.float32)
    zeros8 = np.zeros((SEQ, ROPE_HALF), np.float32)
    zeros = np.zeros((SEQ, HEAD_DIM - ROPE_DIM), np.float32)
    c_head = np.concatenate([cos, cos, ones], axis=1)
    s1_head = np.concatenate([-sin, zeros8, zeros], axis=1)
    s2_head = np.concatenate([zeros8, sin, zeros], axis=1)
    return tuple(jnp.asarray(np.concatenate([t, t], axis=1)) for t in (c_head, s1_head, s2_head))


def _rope_apply(x, c, s1, s2):
    w = x.shape[1]
    return x * c + pltpu.roll(x, w - ROPE_HALF, 1) * s1 + pltpu.roll(x, ROPE_HALF, 1) * s2


def _rope_apply_t(dy, c, s1, s2):
    w = dy.shape[1]
    return dy * c + pltpu.roll(dy * s1, ROPE_HALF, 1) + pltpu.roll(dy * s2, w - ROPE_HALF, 1)


def _dil_prev_limit(has_prev):
    return jnp.where(has_prev, 0, BLOCK)


def _dil_valid(limit):
    row = lax.broadcasted_iota(jnp.int32, (BLOCK, 2 * BLOCK), 0)
    col = lax.broadcasted_iota(jnp.int32, (BLOCK, 2 * BLOCK), 1)
    dist = col - row
    return jnp.logical_and(dist >= jnp.where(col < BLOCK, limit, -BLOCK), dist <= BLOCK)


def _upper_half():
    return lax.broadcasted_iota(jnp.int32, (1, LANES), 1) >= HEAD_DIM


def _dil_rows(n, d):
    per = N_BLOCKS // d
    r, lb = n // per, n % per

    def rows(b):
        start = b * (BLOCK * d) + r
        return pl.ds(pl.multiple_of(start, BLOCK), BLOCK) if d == 1 else pl.ds(start, BLOCK, stride=d)

    return rows(lb), rows(jnp.maximum(lb - 1, 0)), lb > 0


def _dil_specs(g):
    def col(base):
        return pl.BlockSpec((SEQ, LANES), lambda p: (0, base // LANES + 2 * g + p))

    table = pl.BlockSpec((SEQ, LANES), lambda p: (0, 0))
    return [col(COL_QA), col(COL_KA), col(COL_VA)], [table] * 3


def _store_columns(blocks, dproj_ref, cols, sem):
    copies = [pltpu.make_async_copy(b, dproj_ref.at[:, pl.ds(pl.multiple_of(c * LANES, LANES), LANES)], sem.at[i])
              for i, (b, c) in enumerate(zip(blocks, cols))]
    for cp in copies:
        cp.start()
    for cp in copies:
        cp.wait()


def _dil_fwd(g, proj, tables):
    d = DILATIONS[g]
    one_block = d == N_BLOCKS

    def body(q_ref, k_ref, v_ref, c_ref, s1_ref, s2_ref, o_ref, lse_ref):
        upper = _upper_half()

        def roped(ref, rows):
            return _rope_apply(ref[rows, :], c_ref[rows, :], s1_ref[rows, :], s2_ref[rows, :])

        def block(n, carry):
            rows, prev, has_prev = _dil_rows(n, d)
            qb = (roped(q_ref, rows) * QK_SCALE).astype(BF16)
            kw, vw = roped(k_ref, rows).astype(BF16), v_ref[rows, :].astype(BF16)
            if one_block:
                row = lax.broadcasted_iota(jnp.int32, (BLOCK, BLOCK), 0)
                valid = lax.broadcasted_iota(jnp.int32, (BLOCK, BLOCK), 1) <= row
            else:
                kw = jnp.concatenate([roped(k_ref, prev).astype(BF16), kw], axis=0)
                vw = jnp.concatenate([v_ref[prev, :].astype(BF16), vw], axis=0)
                valid = _dil_valid(_dil_prev_limit(has_prev))
            outs, lses = [], []
            for head_mask in (jnp.logical_not(upper), upper):
                s = jnp.where(valid, _dot_nt(qb, jnp.where(head_mask, kw, 0)), NEG_INF)
                m = jnp.max(s, axis=-1, keepdims=True)
                p = jnp.exp(s - m)
                den = jnp.sum(p, axis=-1, keepdims=True)
                outs.append(_dot_nn((p * (1.0 / den)).astype(BF16), jnp.where(head_mask, vw, 0)))
                lses.append(m + jnp.log(den))
            o_ref[rows, :] = outs[0] + outs[1]
            lse_ref[rows, :] = jnp.where(upper, lses[1], lses[0])
            return carry

        lax.fori_loop(0, N_BLOCKS, block, 0, unroll=4)

    qkv, tabs = _dil_specs(g)
    out = pl.BlockSpec((SEQ, LANES), lambda p: (0, p))
    return pl.pallas_call(
        body, name=f"dil_attn_fwd_{g}", grid=(2,), in_specs=qkv + tabs, out_specs=[out, out],
        out_shape=[jax.ShapeDtypeStruct((SEQ, DIL_OUT_WIDTH), F32)] * 2,
        compiler_params=pltpu.CompilerParams(dimension_semantics=("parallel",)),
    )(*_in_hbm(proj, proj, proj, *tables))


def _dil_bwd(g, proj, tables, do, lse, c, dproj, deps=()):
    d = DILATIONS[g]
    one_block = d == N_BLOCKS

    def body(q_ref, k_ref, v_ref, c_ref, s1_ref, s2_ref, do_ref, lse_ref, cc_ref, dproj_in, *rest):
        dproj_ref, dq_acc, dk_acc, dv_acc, dq_out, dk_out, dv_out, sem = rest[len(deps):]
        upper = _upper_half()
        dk_acc[...] = jnp.zeros_like(dk_acc)
        dv_acc[...] = jnp.zeros_like(dv_acc)

        def roped(ref, rows):
            return _rope_apply(ref[rows, :], c_ref[rows, :], s1_ref[rows, :], s2_ref[rows, :])

        def block(n, carry):
            rows, prev, has_prev = _dil_rows(n, d)
            qb = (roped(q_ref, rows) * QK_SCALE).astype(BF16)
            dob = do_ref[rows, :].astype(BF16)
            kw, vw = roped(k_ref, rows).astype(BF16), v_ref[rows, :].astype(BF16)
            if one_block:
                row = lax.broadcasted_iota(jnp.int32, (BLOCK, BLOCK), 0)
                valid = lax.broadcasted_iota(jnp.int32, (BLOCK, BLOCK), 1) <= row
            else:
                kw = jnp.concatenate([roped(k_ref, prev).astype(BF16), kw], axis=0)
                vw = jnp.concatenate([v_ref[prev, :].astype(BF16), vw], axis=0)
                valid = _dil_valid(_dil_prev_limit(has_prev))
            lse_t, c_t = lse_ref[rows, :], cc_ref[rows, :]
            dq, dk, dv = None, None, None
            for e, head_mask in enumerate((jnp.logical_not(upper), upper)):
                km, vm = jnp.where(head_mask, kw, 0), jnp.where(head_mask, vw, 0)
                lse_col = lse_t[:, e * HEAD_DIM:e * HEAD_DIM + 1]
                c_col = c_t[:, e * HEAD_DIM:e * HEAD_DIM + 1]
                p = jnp.where(valid, jnp.exp(_dot_nt(qb, km) - lse_col), 0.0)
                ds = (p * (_dot_nt(dob, vm) + c_col)).astype(BF16)
                parts = (_dot_nn(ds, km), _dot_tn(ds, jnp.where(head_mask, qb, 0)),
                         _dot_tn(p.astype(BF16), jnp.where(head_mask, dob, 0)))
                dq, dk, dv = parts if dq is None else (dq + parts[0], dk + parts[1], dv + parts[2])
            dq_acc[rows, :] = dq * QK_SCALE
            if one_block:
                dk_acc[rows, :] += dk
                dv_acc[rows, :] += dv
            else:
                dk_acc[prev, :] += dk[:BLOCK]
                dv_acc[prev, :] += dv[:BLOCK]
                dk_acc[rows, :] += dk[BLOCK:]
                dv_acc[rows, :] += dv[BLOCK:]
            return carry

        lax.fori_loop(0, N_BLOCKS, block, 0, unroll=4)
        tabs = (c_ref[...], s1_ref[...], s2_ref[...])
        dq_out[...] = _rope_apply_t(dq_acc[...], *tabs).astype(BF16)
        dk_out[...] = _rope_apply_t(dk_acc[...], *tabs).astype(BF16)
        dv_out[...] = dv_acc[...].astype(BF16)
        pair = 2 * g + pl.program_id(0)
        _store_columns((dq_out, dk_out, dv_out), dproj_ref,
                       [base // LANES + pair for base in (COL_QA, COL_KA, COL_VA)], sem)

    qkv, tabs = _dil_specs(g)
    tok = pl.BlockSpec((SEQ, LANES), lambda p: (0, p))
    return pl.pallas_call(
        body, name=f"dil_attn_bwd_{g}", grid=(2,),
        in_specs=qkv + tabs + [tok, tok, tok, _ANY] + [_ANY] * len(deps), out_specs=_ANY,
        out_shape=jax.ShapeDtypeStruct(dproj.shape, dproj.dtype),
        scratch_shapes=[pltpu.VMEM((SEQ, LANES), F32)] * 3 + [pltpu.VMEM((SEQ, LANES), BF16)] * 3
        + [pltpu.SemaphoreType.DMA((3,))],
        input_output_aliases={9: 0},
        compiler_params=pltpu.CompilerParams(dimension_semantics=("arbitrary",)),
    )(*_in_hbm(proj, proj, proj, *tables, do, lse, c, dproj), *deps)


def _group_weights(l0, l1, l2):
    m = jnp.maximum(jnp.maximum(l0, l1), l2)
    e0, e1, e2 = jnp.exp(l0 - m), jnp.exp(l1 - m), jnp.exp(l2 - m)
    tot = e0 + e1 + e2
    return e0 / tot, e1 / tot, e2 / tot


def _dil_combine(outs, lses, deps=()):
    def fn(o0, o1, o2, l0, l1, l2):
        w0, w1, w2 = _group_weights(l0, l1, l2)
        return w0 * o0 + w1 * o1 + w2 * o2

    w = DIL_OUT_WIDTH
    return _rowwise(fn, "dil_combine", SEQ, 512, [(a, w, 0) for a in list(outs) + list(lses)], [], [(w, F32)],
                    deps=deps)[0]


def _dil_combine_bwd(d_out, outs, lses, deps=()):
    w = DIL_OUT_WIDTH

    def fn(d, o0, o1, o2, l0, l1, l2):
        row = lax.broadcasted_iota(jnp.int32, (w, w), 0) // HEAD_DIM
        col = lax.broadcasted_iota(jnp.int32, (w, w), 1) // HEAD_DIM
        same_head = jnp.where(row == col, 1.0, 0.0).astype(BF16)
        ws = _group_weights(l0, l1, l2)
        dws = [_dot3_nn(d * og, same_head) for og in (o0, o1, o2)]
        mean = ws[0] * dws[0] + ws[1] * dws[1] + ws[2] * dws[2]
        return tuple(wg * d for wg in ws) + tuple(-wg * mean for wg in ws)

    res = _rowwise(fn, "dil_combine_bwd", SEQ, 256, [(a, w, 0) for a in [d_out] + list(outs) + list(lses)], [],
                   [(w, F32)] * 6, deps=deps)
    return res[:3], res[3:]


def _log1p(e):
    u = 1.0 + e
    return jnp.where(u == 1.0, e, jnp.log(u) * (e / (u - 1.0)))


def _fox_gate(proj, b_pad, deps=()):
    def body(f_ref, b_ref, *rest):
        o_ref = rest[-1]
        z = f_ref[...] + b_ref[...]
        logf = (jnp.minimum(z, 0.0) - _log1p(jnp.exp(-jnp.abs(z)))).T[:F_ROWS]
        row = lax.broadcasted_iota(jnp.int32, (BLOCK, BLOCK), 0)
        col = lax.broadcasted_iota(jnp.int32, (BLOCK, BLOCK), 1)
        before = jnp.where(row <= col, 1.0, 0.0).astype(BF16)
        carry = jnp.zeros((F_ROWS, 1), F32)
        for blk in range(N_BLOCKS):
            run = _dot3_nn(logf[:, blk * BLOCK:(blk + 1) * BLOCK], before) + carry
            o_ref[:, blk * BLOCK:(blk + 1) * BLOCK] = run
            carry = run[:, BLOCK - 1:BLOCK]

    return pl.pallas_call(
        body, name="fox_gate", grid=(1,),
        in_specs=[pl.BlockSpec((SEQ, LANES), lambda i: (0, COL_F // LANES)), pl.BlockSpec((1, LANES), lambda i: (0, 0))]
        + [_ANY] * len(deps),
        out_specs=pl.BlockSpec((F_ROWS, SEQ), lambda i: (0, 0)),
        out_shape=jax.ShapeDtypeStruct((F_ROWS, SEQ), F32),
    )(*_in_hbm(proj, b_pad), *deps)


def _fox_gate_bwd(d_cum, proj, b_pad, dproj):
    def body(d_ref, f_ref, b_ref, dproj_ref, dz_ref, db_ref):
        row = lax.broadcasted_iota(jnp.int32, (BLOCK, BLOCK), 0)
        col = lax.broadcasted_iota(jnp.int32, (BLOCK, BLOCK), 1)
        after = jnp.where(row >= col, 1.0, 0.0).astype(BF16)
        carry = jnp.zeros((F_ROWS, 1), F32)
        parts = [None] * N_BLOCKS
        for blk in reversed(range(N_BLOCKS)):
            run = _dot3_nn(d_ref[:, blk * BLOCK:(blk + 1) * BLOCK], after) + carry
            parts[blk] = run
            carry = run[:, 0:1]
        dlogf = jnp.concatenate(parts, axis=1)
        dlogf = jnp.concatenate([dlogf, jnp.zeros((LANES - F_ROWS, SEQ), F32)], axis=0).T
        dz = dlogf * _sigmoid(-(f_ref[...] + b_ref[...]))
        dz_ref[...] = dz.astype(BF16)
        db_ref[...] = jnp.sum(dz, axis=0, keepdims=True)

    f_cols = pl.BlockSpec((SEQ, LANES), lambda i: (0, COL_F // LANES))
    return pl.pallas_call(
        body, name="fox_gate_bwd", grid=(1,),
        in_specs=[pl.BlockSpec((F_ROWS, SEQ), lambda i: (0, 0)), f_cols, pl.BlockSpec((1, LANES), lambda i: (0, 0)), _ANY],
        out_specs=[f_cols, pl.BlockSpec((1, LANES), lambda i: (0, 0))],
        out_shape=[jax.ShapeDtypeStruct(dproj.shape, dproj.dtype), jax.ShapeDtypeStruct((1, LANES), F32)],
        input_output_aliases={3: 0},
    )(*_in_hbm(d_cum, proj, b_pad, dproj))


FOX_TILE = 256
FOX_TILES = SEQ // FOX_TILE


def _row_to_col(row):
    n = row.shape[1]
    eye = lax.broadcasted_iota(jnp.int32, (n, n), 0) == lax.broadcasted_iota(jnp.int32, (n, n), 1)
    return jnp.sum(jnp.where(eye, row, 0.0), axis=1, keepdims=True)


def _fox_scores(q_tile, km, f_row, i):
    t = FOX_TILE
    ext = (i + 1) * t
    f_q = _row_to_col(f_row[:, i * t:(i + 1) * t])
    s = _dot_nt(q_tile, km[:ext]) + (f_q - f_row[:, :ext])
    row = lax.broadcasted_iota(jnp.int32, (t, ext), 0) + i * t
    col = lax.broadcasted_iota(jnp.int32, (t, ext), 1)
    return s, col <= row


def _fox_specs():
    qkv = [pl.BlockSpec((SEQ, LANES), lambda p, base=base: (0, base // LANES + p)) for base in (COL_QB, COL_KB, COL_VB)]
    return qkv, pl.BlockSpec((F_ROWS, SEQ), lambda p: (0, 0))


def _fox_fwd(proj, f_rows):
    t = FOX_TILE

    def body(q_ref, k_ref, v_ref, f_ref, o_ref, lse_ref):
        pair = pl.program_id(0)
        upper = _upper_half()
        masks = (jnp.logical_not(upper), upper)
        k16, v16 = k_ref[...].astype(BF16), v_ref[...].astype(BF16)
        kms = [jnp.where(hm, k16, 0) for hm in masks]
        vms = [jnp.where(hm, v16, 0) for hm in masks]
        f_row = [f_ref[pl.ds(2 * pair + e, 1), :] for e in range(2)]
        for i in range(FOX_TILES):
            q_tile = (q_ref[i * t:(i + 1) * t, :] * QK_SCALE).astype(BF16)
            outs, lses = [], []
            for e in range(2):
                s, causal = _fox_scores(q_tile, kms[e], f_row[e], i)
                s = jnp.where(causal, s, NEG_INF)
                m = jnp.max(s, axis=-1, keepdims=True)
                p = jnp.exp(s - m)
                den = jnp.sum(p, axis=-1, keepdims=True)
                outs.append(_dot_nn((p * (1.0 / den)).astype(BF16), vms[e][:(i + 1) * t]))
                lses.append(m + jnp.log(den))
            o_ref[i * t:(i + 1) * t, :] = outs[0] + outs[1]
            lse_ref[i * t:(i + 1) * t, :] = jnp.where(upper, lses[1], lses[0])

    qkv, f_spec = _fox_specs()
    tok = pl.BlockSpec((SEQ, LANES), lambda p: (0, p))
    return pl.pallas_call(
        body, name="fox_attn_fwd", grid=(FOX_WIDTH // LANES,),
        in_specs=qkv + [f_spec], out_specs=[tok, tok],
        out_shape=[jax.ShapeDtypeStruct((SEQ, FOX_WIDTH), F32)] * 2,
        compiler_params=pltpu.CompilerParams(
            dimension_semantics=("parallel",), vmem_limit_bytes=_vmem_limit(8 * t * SEQ * 4)),
    )(*_in_hbm(proj, proj, proj, f_rows))


def _fox_bwd(proj, do, lse, f_rows, dproj):
    t = FOX_TILE

    def body(q_ref, k_ref, v_ref, f_ref, do_ref, lse_ref, dproj_in, dproj_ref, df_ref, dk_acc, dv_acc,
             dq_out, dk_out, dv_out, sem):
        pair = pl.program_id(0)
        upper = _upper_half()
        masks = (jnp.logical_not(upper), upper)
        k16, v16 = k_ref[...].astype(BF16), v_ref[...].astype(BF16)
        kms = [jnp.where(hm, k16, 0) for hm in masks]
        vms = [jnp.where(hm, v16, 0) for hm in masks]
        f_row = [f_ref[pl.ds(2 * pair + e, 1), :] for e in range(2)]
        dk_acc[...] = jnp.zeros_like(dk_acc)
        dv_acc[...] = jnp.zeros_like(dv_acc)
        df_ref[...] = jnp.zeros_like(df_ref)
        for i in range(FOX_TILES):
            ext = (i + 1) * t
            q_tile = (q_ref[i * t:(i + 1) * t, :] * QK_SCALE).astype(BF16)
            do_tile = do_ref[i * t:(i + 1) * t, :]
            lse_t = lse_ref[i * t:(i + 1) * t, :]
            dq = None
            for e in range(2):
                s, causal = _fox_scores(q_tile, kms[e], f_row[e], i)
                p = jnp.where(causal, jnp.exp(s - lse_t[:, e * HEAD_DIM:e * HEAD_DIM + 1]), 0.0)
                dp = _dot_nt(do_tile, vms[e][:ext])
                ds = p * (dp - jnp.sum(p * dp, axis=-1, keepdims=True))
                df_ref[0, e:e + 1, :ext] -= jnp.sum(ds, axis=0, keepdims=True)
                ds = ds.astype(BF16)
                part = _dot_nn(ds, kms[e][:ext])
                dq = part if dq is None else dq + part
                dk_acc[:ext, :] += _dot_tn(ds, jnp.where(masks[e], q_tile, 0))
                dv_acc[:ext, :] += _dot_tn(p.astype(BF16), jnp.where(masks[e], do_tile, 0))
            dq_out[i * t:(i + 1) * t, :] = (dq * QK_SCALE).astype(BF16)
        dk_out[...] = dk_acc[...].astype(BF16)
        dv_out[...] = dv_acc[...].astype(BF16)
        _store_columns((dq_out, dk_out, dv_out), dproj_ref, [base // LANES + pair for base in (COL_QB, COL_KB, COL_VB)],
                       sem)

    qkv, f_spec = _fox_specs()
    tok = pl.BlockSpec((SEQ, LANES), lambda p: (0, p))
    return pl.pallas_call(
        body, name="fox_attn_bwd", grid=(FOX_WIDTH // LANES,),
        in_specs=qkv + [f_spec, tok, tok, _ANY],
        out_specs=[_ANY, pl.BlockSpec((1, SUBLANES, SEQ), lambda p: (p, 0, 0))],
        out_shape=[jax.ShapeDtypeStruct(dproj.shape, dproj.dtype),
                   jax.ShapeDtypeStruct((FOX_WIDTH // LANES, SUBLANES, SEQ), F32)],
        scratch_shapes=[pltpu.VMEM((SEQ, LANES), F32)] * 2 + [pltpu.VMEM((SEQ, LANES), BF16)] * 3
        + [pltpu.SemaphoreType.DMA((3,))],
        input_output_aliases={6: 0},
        compiler_params=pltpu.CompilerParams(
            dimension_semantics=("arbitrary",), vmem_limit_bytes=_vmem_limit(10 * t * SEQ * 4)),
    )(*_in_hbm(proj, proj, proj, f_rows, do, lse, dproj))


MIX_TILE = 256


def _mix_out(out_a, out_b, proj, x, wt_pa, wt_pb, w_out, g_post, g_ffn_pre):
    tm = MIX_TILE

    def body(a_ref, b_ref, ga_ref, gb_ref, x_ref, wpa_ref, wpb_ref, wo_ref, g2_ref, g3_ref,
             merged_ref, mix_ref, x1_ref, h2_ref):
        ya = _dot_nn(a_ref[...].astype(BF16), wpa_ref[...])
        yb = _dot_nn(b_ref[...].astype(BF16), wpb_ref[...])
        merged = (_sigmoid(ga_ref[...]) * ya + _sigmoid(gb_ref[...]) * yb).astype(BF16)
        merged_ref[...] = merged
        mix = _dot_nn(merged, wo_ref[...])
        mix_ref[...] = mix
        x1 = x_ref[...] + mix * _rms_scale(mix) * g2_ref[...]
        x1_ref[...] = x1
        h2_ref[...] = (x1 * _rms_scale(x1) * g3_ref[...]).astype(BF16)

    def rows(w, cb=0):
        return pl.BlockSpec((tm, w), lambda i, cb=cb: (i, cb))

    def whole(a):
        return pl.BlockSpec(a.shape, lambda i: (0, 0))

    d = D_MODEL
    blk = _nbytes((tm, d), F32) * 6 + sum(_nbytes(a.shape, BF16) for a in (wt_pa, wt_pb, w_out))
    return pl.pallas_call(
        body, name="mix_out", grid=(SEQ // tm,),
        in_specs=[rows(DIL_OUT_WIDTH), rows(FOX_WIDTH), rows(d, COL_GA // d), rows(d, COL_GB // d), rows(d),
                  whole(wt_pa), whole(wt_pb), whole(w_out), whole(g_post), whole(g_ffn_pre)],
        out_specs=[rows(d)] * 4,
        out_shape=[jax.ShapeDtypeStruct((SEQ, d), dt) for dt in (BF16, F32, F32, BF16)],
        compiler_params=pltpu.CompilerParams(dimension_semantics=("parallel",), vmem_limit_bytes=_vmem_limit(blk)),
    )(*_in_hbm(out_a, out_b, proj, proj, x, wt_pa, wt_pb, w_out, g_post, g_ffn_pre))


def _mix_out_bwd(dmix, out_a, out_b, proj, wt_pa, wt_pb, w_out, deps=()):
    tm = MIX_TILE

    def body(dm_ref, a_ref, b_ref, ga_ref, gb_ref, wpa_ref, wpb_ref, wo_ref, *rest):
        dproj_ref, dya_ref, dyb_ref, da_ref, db_ref = rest[len(deps):]
        dmerged = _dot_nt(dm_ref[...], wo_ref[...])
        ya = _dot_nn(a_ref[...].astype(BF16), wpa_ref[...])
        yb = _dot_nn(b_ref[...].astype(BF16), wpb_ref[...])
        sa, sb = _sigmoid(ga_ref[...]), _sigmoid(gb_ref[...])
        dproj_ref[:, COL_GA:COL_GA + D_MODEL] = (dmerged * ya * (sa * (1.0 - sa))).astype(BF16)
        dproj_ref[:, COL_GB:COL_GB + D_MODEL] = (dmerged * yb * (sb * (1.0 - sb))).astype(BF16)
        dproj_ref[:, COL_GB + D_MODEL:] = jnp.zeros((tm, COL_QA - COL_GB - D_MODEL), BF16)
        dya = (dmerged * sa).astype(BF16)
        dyb = (dmerged * sb).astype(BF16)
        dya_ref[...] = dya
        dyb_ref[...] = dyb
        da_ref[...] = _dot_nt(dya, wpa_ref[...])
        db_ref[...] = _dot_nt(dyb, wpb_ref[...]).astype(BF16)

    def rows(w, cb=0):
        return pl.BlockSpec((tm, w), lambda i, cb=cb: (i, cb))

    def whole(a):
        return pl.BlockSpec(a.shape, lambda i: (0, 0))

    d = D_MODEL
    blk = _nbytes((tm, d), F32) * 8 + sum(_nbytes(a.shape, BF16) for a in (wt_pa, wt_pb, w_out))
    return pl.pallas_call(
        body, name="mix_out_bwd", grid=(SEQ // tm,),
        in_specs=[rows(d), rows(DIL_OUT_WIDTH), rows(FOX_WIDTH), rows(d, COL_GA // d), rows(d, COL_GB // d),
                  whole(wt_pa), whole(wt_pb), whole(w_out)] + [_ANY] * len(deps),
        out_specs=[rows(COL_QA)] + [rows(d)] * 2 + [rows(DIL_OUT_WIDTH), rows(FOX_WIDTH)],
        out_shape=[jax.ShapeDtypeStruct((SEQ, PROJ_COLS), BF16)] + [jax.ShapeDtypeStruct((SEQ, d), BF16)] * 2
        + [jax.ShapeDtypeStruct((SEQ, DIL_OUT_WIDTH), F32), jax.ShapeDtypeStruct((SEQ, FOX_WIDTH), BF16)],
        compiler_params=pltpu.CompilerParams(dimension_semantics=("parallel",), vmem_limit_bytes=_vmem_limit(blk)),
    )(*_in_hbm(dmix, out_a, out_b, proj, proj, wt_pa, wt_pb, w_out), *deps)


FFN_TM, FFN_TN = 2048, 256


def _ffn_up(h2, wt_gate, wt_up):
    tm, tn = FFN_TM, FFN_TN

    def body(h_ref, wg_ref, wu_ref, gate_ref, up_ref, act_ref):
        gate = _dot_nt(h_ref[...], wg_ref[...])
        up = _dot_nt(h_ref[...], wu_ref[...])
        gate_ref[...] = gate
        up_ref[...] = up
        act_ref[...] = (gate * _sigmoid(gate) * up).astype(BF16)

    tile = pl.BlockSpec((tm, tn), lambda i, j: (i, j))
    w_spec = pl.BlockSpec((tn, D_MODEL), lambda i, j: (j, 0))
    return pl.pallas_call(
        body, name="ffn_up", grid=(SEQ // tm, D_FF // tn),
        in_specs=[pl.BlockSpec((tm, D_MODEL), lambda i, j: (i, 0)), w_spec, w_spec],
        out_specs=[tile, tile, tile],
        out_shape=[jax.ShapeDtypeStruct((SEQ, D_FF), dt) for dt in (F32, F32, BF16)],
        compiler_params=pltpu.CompilerParams(
            dimension_semantics=("parallel", "parallel"), vmem_limit_bytes=_vmem_limit(8 * 2**20)),
    )(*_in_hbm(h2, wt_gate, wt_up))


def _ffn_act_bwd(dff, w_down, gate, up):
    tm, tn = FFN_TM, FFN_TN

    def body(d_ref, wd_ref, gate_ref, up_ref, dgate_ref, dup_ref):
        dact = _dot_nt(d_ref[...], wd_ref[...])
        gate = gate_ref[...]
        sg = _sigmoid(gate)
        dgate_ref[...] = (dact * up_ref[...] * (sg * (1.0 + gate * (1.0 - sg)))).astype(BF16)
        dup_ref[...] = (dact * (gate * sg)).astype(BF16)

    tile = pl.BlockSpec((tm, tn), lambda i, j: (i, j))
    return pl.pallas_call(
        body, name="ffn_act_bwd", grid=(SEQ // tm, D_FF // tn),
        in_specs=[pl.BlockSpec((tm, D_MODEL), lambda i, j: (i, 0)), pl.BlockSpec((tn, D_MODEL), lambda i, j: (j, 0)),
                  tile, tile],
        out_specs=[tile, tile],
        out_shape=[jax.ShapeDtypeStruct((SEQ, D_FF), BF16)] * 2,
        compiler_params=pltpu.CompilerParams(
            dimension_semantics=("parallel", "parallel"), vmem_limit_bytes=_vmem_limit(8 * 2**20)),
    )(*_in_hbm(dff, w_down, gate, up))


EPILOGUE_TM = 512


def _loss_head(act, w_down, x1, target, g_post):
    def fn(ff, x1, tgt, g):
        r = _rms_scale(ff)
        nrm = ff * r
        err = (x1 + nrm * g) - tgt
        loss = 0.5 * jnp.sum(jnp.mean(err * err, axis=-1, keepdims=True), axis=0, keepdims=True)
        dy = err * (1.0 / D_MODEL)
        u = dy * g
        dff = r * u - ff * (r * r * r) * jnp.mean(u * ff, axis=-1, keepdims=True)
        return dy, dff, jnp.broadcast_to(loss, (1, LANES)), jnp.sum(dy * nrm, axis=0, keepdims=True)

    d = D_MODEL
    return _matmul_rowwise([(act, w_down)], fn, "ffn_down_loss", EPILOGUE_TM, [(x1, d, 0), (target, d, 0)], [g_post],
                           [(d, F32), (d, BF16)], [LANES, d])


def _post_ffn_bwd(dgate, wt_gate, dup, wt_up, x1, dy, mix, g_ffn_pre, g_mix_post, deps=()):
    def fn(dh2, x1, dy, mix, g3, g2):
        dx, dg3 = _rms_bwd(x1, dh2, g3)
        dx1 = dy + dx
        dmix, dg2 = _rms_bwd(mix, dx1, g2)
        return dx1, dmix, dg3, dg2

    d = D_MODEL
    return _matmul_rowwise([(dgate, wt_gate), (dup, wt_up)], fn, "ffn_up_bwd", EPILOGUE_TM,
                           [(x1, d, 0), (dy, d, 0), (mix, d, 0)], [g_ffn_pre, g_mix_post],
                           [(d, F32), (d, BF16)], [d, d], deps=deps)


def _input_bwd(dproj, wt_r, x, dx1, g_pre, deps=()):
    def fn(dh, x, dx1, g):
        dx, dg = _rms_bwd(x, dh, g)
        return dx1 + dx, dg

    d = D_MODEL
    return _matmul_rowwise([(dproj, wt_r)], fn, "in_proj_bwd", EPILOGUE_TM, [(x, d, 0), (dx1, d, 0)], [g_pre],
                           [(d, F32)], [d], deps=deps)


def _adam_math(w, g, m, v):
    m = ADAM_B1 * m + (1.0 - ADAM_B1) * g
    v = ADAM_B2 * v + (1.0 - ADAM_B2) * (g * g)
    m_hat = m / (1.0 - ADAM_B1 ** ADAM_STEP)
    v_hat = v / (1.0 - ADAM_B2 ** ADAM_STEP)
    delta = -ADAM_LR * (m_hat / (jnp.sqrt(v_hat) + ADAM_EPS) + ADAM_WD * w)
    return delta, m, v


def _adam(w, g, m, v, name):
    r, c = w.shape
    tc = _col_tile(r, c)

    def body(w_ref, g_ref, m_ref, v_ref, d_ref, nm_ref, nv_ref):
        d_ref[...], nm_ref[...], nv_ref[...] = _adam_math(w_ref[...], g_ref[...], m_ref[...], v_ref[...])

    spec = pl.BlockSpec((r, tc), lambda j: (0, j))
    return pl.pallas_call(
        body, name=name, grid=(c // tc,), in_specs=[spec] * 4, out_specs=[spec] * 3,
        out_shape=[jax.ShapeDtypeStruct((r, c), F32)] * 3,
        compiler_params=pltpu.CompilerParams(dimension_semantics=("parallel",)),
    )(*_in_hbm(w, g, m, v))


def _adam_small(gathered, ws, ms, vs, loss_parts):
    n = len(ws)

    def body(*refs):
        outs = refs[4 * n + 1:]
        loss = refs[4 * n][0]
        for dev in range(1, N_DEV):
            loss = loss + refs[4 * n][dev]
        outs[4 * n][...] = loss
        for i in range(n):
            ga_ref, w_ref, m_ref, v_ref = (refs[j * n + i] for j in range(4))
            g = ga_ref[0]
            for dev in range(1, N_DEV):
                g = g + ga_ref[dev]
            g = g[:, :w_ref.shape[1]]
            outs[4 * i][...] = g
            outs[4 * i + 1][...], outs[4 * i + 2][...], outs[4 * i + 3][...] = _adam_math(
                w_ref[...], g, m_ref[...], v_ref[...])

    out_shape = [jax.ShapeDtypeStruct(w.shape, F32) for w in ws for _ in range(4)]
    out_shape.append(jax.ShapeDtypeStruct((1, LANES), F32))
    out = pl.pallas_call(body, name="adam_small", out_shape=out_shape)(*gathered, *ws, *ms, *vs, loss_parts)
    return [out[4 * i:4 * i + 4] for i in range(n)], out[4 * n]


_PROJ_SEGMENTS = ((3848, 5896), (None, COL_QA - 2 * D_MODEL), (0, 3840), (3840, 3848), (None, PROJ_COLS - COL_F - 8))


def _proj_weight_t(gathered):
    w = gathered.reshape(IN_COLS, D_MODEL)
    return jnp.concatenate([jnp.zeros((hi, D_MODEL), w.dtype) if lo is None else w[lo:hi] for lo, hi in _PROJ_SEGMENTS],
                           axis=0)


def _proj_weight_grad_slots(dwt_r):
    starts, at = [], 0
    for lo, hi in _PROJ_SEGMENTS:
        if lo is not None:
            starts.append((lo, hi, at))
        at += hi if lo is None else hi - lo
    slots = []
    for dev in range(N_DEV):
        pieces, lo, end = [], dev * IN_SHARD, (dev + 1) * IN_SHARD
        for seg_lo, seg_hi, seg_at in sorted(starts):
            a, b = max(lo, seg_lo), min(end, seg_hi)
            if a < b:
                pieces.append(dwt_r[seg_at + a - seg_lo:seg_at + b - seg_lo])
        slots.append(pieces[0] if len(pieces) == 1 else jnp.concatenate(pieces, axis=0))
    return jnp.stack(slots)


def kernel(x, w_in, w_proj_a, w_proj_b, w_out, b_forget, w_ffn_gate, w_ffn_up, w_ffn_down, norm_mix_pre, norm_mix_post, norm_ffn_pre, norm_ffn_post, loss_target, m_w_in, m_w_proj_a, m_w_proj_b, m_w_out, m_b_forget, m_w_ffn_gate, m_w_ffn_up, m_w_ffn_down, m_norm_mix_pre, m_norm_mix_post, m_norm_ffn_pre, m_norm_ffn_post, v_w_in, v_w_proj_a, v_w_proj_b, v_w_out, v_b_forget, v_w_ffn_gate, v_w_ffn_up, v_w_ffn_down, v_norm_mix_pre, v_norm_mix_post, v_norm_ffn_pre, v_norm_ffn_post):
    d = D_MODEL
    names = ("w_in", "w_proj_a", "w_proj_b", "w_out", "w_ffn_gate", "w_ffn_up", "w_ffn_down")
    col_sharded = ("w_in", "w_ffn_gate", "w_ffn_up")

    def row_shards(arrs):
        return {k: (a[0].T if k in col_sharded else a[0]) for k, a in zip(names, arrs)}

    shards = row_shards((w_in, w_proj_a, w_proj_b, w_out, w_ffn_gate, w_ffn_up, w_ffn_down))
    moments_m = row_shards((m_w_in, m_w_proj_a, m_w_proj_b, m_w_out, m_w_ffn_gate, m_w_ffn_up, m_w_ffn_down))
    moments_v = row_shards((v_w_in, v_w_proj_a, v_w_proj_b, v_w_out, v_w_ffn_gate, v_w_ffn_up, v_w_ffn_down))
    core = lax.axis_index("c").astype(jnp.int32).reshape(1)
    chip = (2 * lax.axis_index("x") + lax.axis_index("y")).astype(jnp.int32).reshape(1)
    x2, target = x[0], loss_target[0]

    me = 4 * lax.axis_index("x") + 2 * lax.axis_index("y") + lax.axis_index("c")
    mid_names, ffn_names = names[1:4], names[4:]
    first_names, later_names = names[:1], names[1:]
    shards16 = {k: shards[k].astype(BF16) for k in names}

    def landing(k):
        return lax.dynamic_update_slice(lax.empty((N_DEV,) + shards[k].shape, BF16), shards16[k][None], (me, 0, 0))

    ag_first = _exchange_start("ag_first_chips_start", _gather_chips_copies, [shards16[k] for k in first_names],
                               [landing(k) for k in first_names], 3 * len(first_names))
    h = _rowwise(lambda xb, g: xb * _rms_scale(xb) * g, "norm_mix_pre", SEQ, 256, [(x2, d, 0)], [norm_mix_pre],
                 [(d, BF16)], deps=[ag_first.token])[0]
    _, lands = _exchange_wait("ag_first_chips_wait", ag_first, [h])
    ag_first = _exchange_start("ag_first_sibling_start", _gather_sibling_copies, [], lands, 4 * len(first_names))
    ag_later = _exchange_start("ag_later_chips_start", _gather_chips_copies, [shards16[k] for k in later_names],
                               [landing(k) for k in later_names], 3 * len(later_names), after=[ag_first.token])
    gathered = dict(zip(first_names, _exchange_wait("ag_first_sibling_wait", ag_first, [ag_later.token])[1]))
    wt_r = _proj_weight_t(gathered["w_in"])

    proj = _matmul([(h, wt_r)], "nt", F32, "in_proj", 1024, 896, 1024)
    tables = _rope_tables()
    o_dil, lse_dil = zip(*[_dil_fwd(g, proj, tables) for g in range(len(DILATIONS))])
    out_a = _dil_combine(o_dil, lse_dil)
    _, lands = _exchange_wait("ag_later_chips_wait", ag_later, [out_a])
    ag_later = _exchange_start("ag_later_sibling_start", _gather_sibling_copies, [], lands, 4 * len(later_names))

    b_pad = jnp.pad(b_forget, ((0, 0), (0, LANES - N_FOX_HEADS)))
    f_rows = _fox_gate(proj, b_pad, deps=[ag_later.token])
    out_b, lse_fox = _fox_fwd(proj, f_rows)

    gathered = dict(zip(later_names, _exchange_wait("ag_later_sibling_wait", ag_later, [out_b])[1]))
    wt_pa = gathered["w_proj_a"].transpose(1, 0, 2).reshape(DIL_OUT_WIDTH, d)
    wt_pb = gathered["w_proj_b"].transpose(1, 0, 2).reshape(FOX_WIDTH, d)
    w_o = gathered["w_out"].reshape(d, d)
    wt_g = gathered["w_ffn_gate"].reshape(D_FF, d)
    wt_u = gathered["w_ffn_up"].reshape(D_FF, d)
    w_d = gathered["w_ffn_down"].reshape(D_FF, d)
    merged, mix, x1, h2 = _mix_out(out_a, out_b, proj, x2, wt_pa, wt_pb, w_o, norm_mix_post, norm_ffn_pre)

    gate, up, act = _ffn_up(h2, wt_g, wt_u)
    dy, dff, loss_part, dg_ffn_post = _loss_head(act, w_d, x1, target, norm_ffn_post)

    dgate, dup = _ffn_act_bwd(dff, w_d, gate, up)
    grads_t = {}
    grads_t["w_ffn_down"] = _matmul([(act, dff)], "tn", F32, "grad_w_ffn_down", 1408, 1024, 2048)
    grads_t["w_ffn_gate"] = _matmul([(dgate, h2)], "tn", F32, "grad_w_ffn_gate", 1408, 1024, 2048)
    grads_t["w_ffn_up"] = _matmul([(dup, h2)], "tn", F32, "grad_w_ffn_up", 1408, 1024, 2048)
    rs_ffn = _ReduceScatter("ffn", {k: grads_t[k] for k in ffn_names}, core, chip)
    dx1, dmix, dg_ffn_pre, dg_mix_post = _post_ffn_bwd(dgate, wt_g, dup, wt_u, x1, dy, mix, norm_ffn_pre, norm_mix_post,
                                                       deps=[rs_ffn.token])
    rs_ffn.start_chips([dmix])

    dproj, dya, dyb, d_out_a, d_out_b = _mix_out_bwd(dmix, out_a, out_b, proj, wt_pa, wt_pb, w_o, deps=[rs_ffn.token])
    grads_t["w_out"] = _matmul([(merged, dmix)], "tn", F32, "grad_w_out", 1024, 1024, 1024)
    def column_slots(g):
        return g.reshape(g.shape[0], N_DEV, LANES).transpose(1, 0, 2)

    grads_t["w_proj_a"] = column_slots(_matmul([(out_a, dya)], "tn", F32, "grad_w_proj_a", DIL_OUT_WIDTH, 1024, SEQ))
    grads_t["w_proj_b"] = column_slots(_matmul([(out_b, dyb)], "tn", F32, "grad_w_proj_b", FOX_WIDTH, 1024, SEQ))
    rs_mid = _ReduceScatter("mid", {k: grads_t[k] for k in mid_names}, core, chip)

    do_dil, c_dil = _dil_combine_bwd(d_out_a, o_dil, lse_dil, deps=[rs_mid.token])
    rs_mid.start_chips([c_dil[0]])
    dproj, d_cum = _fox_bwd(proj, d_out_b, lse_fox, f_rows, dproj)
    d_cum_rows = jnp.pad(d_cum[:, :2].reshape(N_FOX_HEADS, SEQ), ((0, F_ROWS - N_FOX_HEADS), (0, 0)))
    dproj, db_part = _fox_gate_bwd(d_cum_rows, proj, b_pad, dproj)
    for g in range(len(DILATIONS)):
        dproj = _dil_bwd(g, proj, tables, do_dil[g], lse_dil[g], c_dil[g], dproj, deps=[rs_mid.token])

    dwt_r = _matmul([(dproj, h)], "tn", F32, "grad_w_in", 896, 1024, 2048)
    rs_in = _ReduceScatter("in", {"w_in": _proj_weight_grad_slots(dwt_r)}, core, chip)
    grads = rs_ffn.finish([rs_in.token])
    big = {k: _adam(shards[k], grads[k], moments_m[k], moments_v[k], "adam_" + k) for k in ffn_names}
    rs_in.start_chips([big[k][0] for k in ffn_names])
    grad_x, dg_mix_pre = _input_bwd(dproj, wt_r, x2, dx1, norm_mix_pre, deps=[rs_in.token])
    grads.update(rs_mid.finish([grad_x]))
    big.update({k: _adam(shards[k], grads[k], moments_m[k], moments_v[k], "adam_" + k) for k in mid_names})

    small_all = _all_gather([dg_mix_pre, dg_mix_post, dg_ffn_pre, dg_ffn_post, db_part, loss_part],
                            "small_grads_all_gather", deps=[big[k][0] for k in mid_names])
    small, loss = _adam_small(small_all[:5], [norm_mix_pre, norm_mix_post, norm_ffn_pre, norm_ffn_post, b_forget],
                              [m_norm_mix_pre, m_norm_mix_post, m_norm_ffn_pre, m_norm_ffn_post, m_b_forget],
                              [v_norm_mix_pre, v_norm_mix_post, v_norm_ffn_pre, v_norm_ffn_post, v_b_forget],
                              small_all[5])

    grads.update(rs_in.finish([small[0][0]]))
    big["w_in"] = _adam(shards["w_in"], grads["w_in"], moments_m["w_in"], moments_v["w_in"], "adam_w_in")

    def leaves(i):
        def nat(k):
            a = grads[k] if i == 0 else big[k][i - 1]
            return (a.T if k in col_sharded else a)[None]

        return [nat("w_in"), nat("w_proj_a"), nat("w_proj_b"), nat("w_out"), small[4][i],
                nat("w_ffn_gate"), nat("w_ffn_up"), nat("w_ffn_down"), *[small[r][i] for r in range(4)]]

    return (loss[0, 0], grad_x[None], *leaves(0), *leaves(1), *leaves(2), *leaves(3))
```

```python
import functools
import math

import jax
import jax.numpy as jnp
import numpy as np
from jax import lax
from jax.experimental import pallas as pl
from jax.experimental.pallas import tpu as pltpu

F32 = jnp.float32
BF16 = jnp.bfloat16
MESH = pl.DeviceIdType.MESH

D_MODEL = 1024
SEQ = 2048
HEAD_DIM = 64
BLOCK = 128
N_BLOCKS = SEQ // BLOCK
DILATIONS = (1, 4, 16)
N_FOX_HEADS = 8
DIL_WIDTH = 768
DIL_OUT_WIDTH = 256
FOX_WIDTH = 512
D_FF = 2816
ROPE_THETA = 500000.0
ROPE_DIM = HEAD_DIM // 4
ROPE_HALF = ROPE_DIM // 2
EPS = 1e-6
NEG_INF = -1e30
QK_SCALE = 1.0 / math.sqrt(HEAD_DIM)
IN_COLS = 5896
N_DEV = 8
IN_SHARD = IN_COLS // N_DEV

ADAM_LR = 0.001
ADAM_B1 = 0.9
ADAM_B2 = 0.999
ADAM_EPS = 1e-08
ADAM_WD = 0.01
ADAM_STEP = 10

V7X_VMEM_BYTES = 64 * 2**20
LANES = 128
SUBLANES = 8

PROJ_COLS = 6272
COL_GA, COL_GB = 0, 1024
COL_QA, COL_KA, COL_VA = 2304, 3072, 3840
COL_QB, COL_KB, COL_VB = 4608, 5120, 5632
COL_F = 6144
F_ROWS = 16


def _vmem_limit(block_bytes):
    want = 2 * block_bytes + 16 * 2**20
    return int(min(max(want, 32 * 2**20), V7X_VMEM_BYTES - 8 * 2**20))


def _nbytes(shape, dtype):
    return math.prod(shape) * jnp.dtype(dtype).itemsize


def _in_hbm(*arrays):
    return [pltpu.with_memory_space_constraint(a, pltpu.HBM) for a in arrays]


def _dot(a, b, dims):
    return lax.dot_general(a, b, (dims, ((), ())), preferred_element_type=F32)


def _dot_nn(a, b):
    return _dot(a, b, ((1,), (0,)))


def _dot_nt(a, b):
    return _dot(a, b, ((1,), (1,)))


def _dot_tn(a, b):
    return _dot(a, b, ((0,), (0,)))


def _sigmoid(z):
    return 1.0 / (1.0 + jnp.exp(-z))


def _split3(x):
    hi = x.astype(BF16)
    r1 = x - hi.astype(F32)
    mid = r1.astype(BF16)
    lo = (r1 - mid.astype(F32)).astype(BF16)
    return hi, mid, lo


def _dot3_nn(x, ones_matrix):
    hi, mid, lo = _split3(x)
    return (_dot_nn(hi, ones_matrix) + _dot_nn(mid, ones_matrix)) + _dot_nn(lo, ones_matrix)


def _rowwise(fn, name, n_rows, tm, row_ins, bcast_ins, row_outs, acc_outs=(), deps=()):
    n_in = len(row_ins) + len(bcast_ins)
    n_ro = len(row_outs)

    def body(*refs):
        res = fn(*[r[...] for r in refs[:n_in]])
        if not isinstance(res, (tuple, list)):
            res = (res,)
        outs = refs[n_in + len(deps):]
        for r, o in zip(res[:n_ro], outs[:n_ro]):
            o[...] = r.astype(o.dtype)
        first = pl.program_id(0) == 0
        for r, o in zip(res[n_ro:], outs[n_ro:]):
            _accumulate(o, r, first)

    in_specs = [pl.BlockSpec((tm, w), lambda i, cb=cb: (i, cb)) for _, w, cb in row_ins]
    in_specs += [pl.BlockSpec(a.shape, lambda i: (0, 0)) for a in bcast_ins]
    in_specs += [pl.BlockSpec(memory_space=pl.ANY)] * len(deps)
    out_specs = [pl.BlockSpec((tm, w), lambda i: (i, 0)) for w, _ in row_outs]
    out_specs += [pl.BlockSpec((1, w), lambda i: (0, 0)) for w in acc_outs]
    out_shape = [jax.ShapeDtypeStruct((n_rows, w), dt) for w, dt in row_outs]
    out_shape += [jax.ShapeDtypeStruct((1, w), F32) for w in acc_outs]
    blk = sum(_nbytes((tm, w), a.dtype) for a, w, _ in row_ins) + sum(_nbytes((tm, w), dt) for w, dt in row_outs)
    return pl.pallas_call(
        body, name=name, grid=(n_rows // tm,), in_specs=in_specs, out_specs=out_specs, out_shape=out_shape,
        compiler_params=pltpu.CompilerParams(
            dimension_semantics=("arbitrary" if acc_outs else "parallel",), vmem_limit_bytes=_vmem_limit(3 * blk)),
    )(*_in_hbm(*[a for a, _, _ in row_ins], *bcast_ins), *deps)


def _accumulate(o_ref, part, first):
    @pl.when(first)
    def _():
        o_ref[...] = part

    @pl.when(jnp.logical_not(first))
    def _():
        o_ref[...] += part


_MM_DIMS = {"nn": ((1,), (0,)), "nt": ((1,), (1,)), "tn": ((0,), (0,))}


def _matmul(pairs, mode, out_dtype, name, tm, tn, tk, deps=()):
    a0, b0 = pairs[0]
    if mode == "tn":
        kk, m = a0.shape
    else:
        m, kk = a0.shape
    n = b0.shape[0] if mode == "nt" else b0.shape[1]
    assert m % tm == 0 and n % tn == 0 and kk % tk == 0, (name, m, n, kk)
    nk = kk // tk
    n_pairs = len(pairs)
    dims = _MM_DIMS[mode]
    n_in = 2 * n_pairs + len(deps)

    def body(*refs):
        o_ref = refs[n_in]
        part = None
        for p in range(n_pairs):
            d = _dot(refs[2 * p][...].astype(BF16), refs[2 * p + 1][...].astype(BF16), dims)
            part = d if part is None else part + d
        if nk == 1:
            o_ref[...] = part.astype(o_ref.dtype)
            return
        acc = refs[n_in + 1]
        k = pl.program_id(2)

        @pl.when(k == 0)
        def _():
            acc[...] = part

        @pl.when(k > 0)
        def _():
            acc[...] += part

        @pl.when(k == nk - 1)
        def _():
            o_ref[...] = acc[...].astype(o_ref.dtype)

    if mode == "tn":
        a_spec = pl.BlockSpec((tk, tm), lambda i, j, k: (k, i))
    else:
        a_spec = pl.BlockSpec((tm, tk), lambda i, j, k: (i, k))
    if mode == "nt":
        b_spec = pl.BlockSpec((tn, tk), lambda i, j, k: (j, k))
    else:
        b_spec = pl.BlockSpec((tk, tn), lambda i, j, k: (k, j))
    blk = sum(_nbytes((tm, tk), a.dtype) + _nbytes((tk, tn), b.dtype) for a, b in pairs) + 2 * _nbytes((tm, tn), F32)
    flat = [a for pair in pairs for a in pair]
    return pl.pallas_call(
        body, name=name, grid=(m // tm, n // tn, nk),
        in_specs=[a_spec, b_spec] * n_pairs + [pl.BlockSpec(memory_space=pl.ANY)] * len(deps),
        out_specs=pl.BlockSpec((tm, tn), lambda i, j, k: (i, j)),
        out_shape=jax.ShapeDtypeStruct((m, n), out_dtype),
        scratch_shapes=[] if nk == 1 else [pltpu.VMEM((tm, tn), F32)],
        compiler_params=pltpu.CompilerParams(
            dimension_semantics=("parallel", "parallel", "arbitrary"), vmem_limit_bytes=_vmem_limit(blk)),
    )(*_in_hbm(*flat), *deps)


def _matmul_rowwise(pairs, fn, name, tm, row_ins, bcast_ins, row_outs, acc_outs=(), deps=()):
    m = pairs[0][0].shape[0]
    n_mm, n_in = 2 * len(pairs), len(row_ins) + len(bcast_ins)
    n_ro = len(row_outs)

    def body(*refs):
        prod = None
        for p in range(len(pairs)):
            part = _dot_nn(refs[2 * p][...].astype(BF16), refs[2 * p + 1][...].astype(BF16))
            prod = part if prod is None else prod + part
        res = fn(prod, *[r[...] for r in refs[n_mm:n_mm + n_in]])
        outs = refs[n_mm + n_in + len(deps):]
        for r, o in zip(res[:n_ro], outs[:n_ro]):
            o[...] = r.astype(o.dtype)
        first = pl.program_id(0) == 0
        for r, o in zip(res[n_ro:], outs[n_ro:]):
            _accumulate(o, r, first)

    in_specs = []
    for a, b in pairs:
        in_specs += [pl.BlockSpec((tm, a.shape[1]), lambda i: (i, 0)),
                     pl.BlockSpec(b.shape, lambda i: (0, 0), pipeline_mode=pl.Buffered(1))]
    in_specs += [pl.BlockSpec((tm, w), lambda i, cb=cb: (i, cb)) for _, w, cb in row_ins]
    in_specs += [pl.BlockSpec(a.shape, lambda i: (0, 0)) for a in bcast_ins]
    in_specs += [_ANY] * len(deps)
    out_specs = [pl.BlockSpec((tm, w), lambda i: (i, 0)) for w, _ in row_outs]
    out_specs += [pl.BlockSpec((1, w), lambda i: (0, 0)) for w in acc_outs]
    out_shape = [jax.ShapeDtypeStruct((m, w), dt) for w, dt in row_outs]
    out_shape += [jax.ShapeDtypeStruct((1, w), F32) for w in acc_outs]
    blk = sum(_nbytes((tm, a.shape[1]), a.dtype) + _nbytes(b.shape, b.dtype) // 2 for a, b in pairs)
    blk += sum(_nbytes((tm, w), a.dtype) for a, w, _ in row_ins) + sum(_nbytes((tm, w), dt) for w, dt in row_outs)
    return pl.pallas_call(
        body, name=name, grid=(m // tm,), in_specs=in_specs, out_specs=out_specs, out_shape=out_shape,
        compiler_params=pltpu.CompilerParams(dimension_semantics=("arbitrary",), vmem_limit_bytes=_vmem_limit(blk)),
    )(*_in_hbm(*[a for pair in pairs for a in pair], *[a for a, _, _ in row_ins], *bcast_ins), *deps)


def _rms_scale(x):
    return lax.rsqrt(jnp.mean(x * x, axis=-1, keepdims=True) + EPS)


def _rms_bwd(xin, dyn, g):
    r = _rms_scale(xin)
    u = dyn * g
    dx = r * u - xin * (r * r * r) * jnp.mean(u * xin, axis=-1, keepdims=True)
    dg = jnp.sum(dyn * xin * r, axis=0, keepdims=True)
    return dx, dg


def _mesh_pos():
    return lax.axis_index("x"), lax.axis_index("y"), lax.axis_index("c")


def _all_gather(xs, name, deps=()):
    n = len(xs)

    def body(*refs):
        x_refs, out_refs = refs[:n], refs[n + len(deps):2 * n + len(deps)]
        send_sems, recv_sems, local_sems = refs[2 * n + len(deps):]
        mx, my, mc = _mesh_pos()
        me, sib = (mx, my, mc), (mx, my, 1 - mc)
        chips = [(1 - mx, my), (mx, 1 - my), (1 - mx, 1 - my)]

        def slot(a, dev):
            px, py, pc = dev
            return out_refs[a].at[4 * px + 2 * py + pc]

        def copy(k, a, block, to, src=None):
            return pltpu.make_async_remote_copy(
                src_ref=slot(a, block) if src is None else src, dst_ref=slot(a, block),
                send_sem=send_sems.at[a * 7 + k], recv_sem=recv_sems.at[a * 7 + k],
                device_id=to, device_id_type=MESH)

        mine = [pltpu.make_async_copy(x_refs[a], slot(a, me), local_sems.at[a]) for a in range(n)]
        for cp in mine:
            cp.start()
        first = []
        for a in range(n):
            first.append(copy(0, a, me, sib, x_refs[a]))
            first += [copy(1 + j, a, me, (*chip, mc), x_refs[a]) for j, chip in enumerate(chips)]
        for cp in first:
            cp.start()
        passed = []
        for a in range(n):
            for j, chip in enumerate(chips):
                copy(1 + j, a, (*chip, mc), me).wait_recv()
                fwd = copy(4 + j, a, (*chip, mc), sib)
                fwd.start()
                passed.append(fwd)
        for a in range(n):
            copy(0, a, sib, me).wait_recv()
            for j, chip in enumerate(chips):
                copy(4 + j, a, (*chip, 1 - mc), me).wait_recv()
        for cp in first + passed:
            cp.wait_send()
        for cp in mine:
            cp.wait()

    hbm = pl.BlockSpec(memory_space=pl.ANY)
    return pl.pallas_call(
        body, name=name,
        out_shape=[jax.ShapeDtypeStruct((N_DEV,) + x.shape, x.dtype) for x in xs],
        in_specs=[hbm] * (n + len(deps)), out_specs=[hbm] * n,
        scratch_shapes=[pltpu.SemaphoreType.DMA((7 * n,)), pltpu.SemaphoreType.DMA((7 * n,)),
                        pltpu.SemaphoreType.DMA((n,))],
    )(*xs, *deps)


_HBM = pl.BlockSpec(memory_space=pltpu.HBM)
_SEM = pl.BlockSpec(memory_space=pltpu.SEMAPHORE)
_ANY = pl.BlockSpec(memory_space=pl.ANY)
_DATAFLOW = pltpu.SideEffectType.DATAFLOW_SIDE_EFFECTING


def _flip_peer(flip):
    mx, my, mc = _mesh_pos()
    return (1 - mx if flip & 2 else mx, 1 - my if flip & 1 else my, mc)


def _remote(src, dst, send_sems, recv_sems, k, peer):
    return pltpu.make_async_remote_copy(src_ref=src, dst_ref=dst, send_sem=send_sems.at[k], recv_sem=recv_sems.at[k],
                                        device_id=peer, device_id_type=MESH)


def _gather_chips_copies(srcs, lands, send_sems, recv_sems):
    mx, my, mc = _mesh_pos()
    me = 4 * mx + 2 * my + mc
    return [_remote(srcs[a], lands[a].at[me], send_sems, recv_sems, 3 * a + flip - 1, _flip_peer(flip))
            for a in range(len(srcs)) for flip in (1, 2, 3)]


def _gather_sibling_copies(srcs, lands, send_sems, recv_sems):
    mx, my, mc = _mesh_pos()
    return [_remote(lands[a].at[2 * k + mc], lands[a].at[2 * k + mc], send_sems, recv_sems, 4 * a + k, (mx, my, 1 - mc))
            for a in range(len(lands)) for k in range(4)]


def _scatter_sibling_copies(srcs, lands, send_sems, recv_sems):
    mx, my, mc = _mesh_pos()
    return [_remote(srcs[a].at[k, 1 - mc], lands[a].at[k], send_sems, recv_sems, 4 * a + k, (mx, my, 1 - mc))
            for a in range(len(srcs)) for k in range(4)]


def _scatter_chips_copies(srcs, lands, send_sems, recv_sems):
    mx, my, _ = _mesh_pos()
    k0 = 2 * mx + my
    return [_remote(srcs[a].at[jnp.bitwise_xor(k0, flip)], lands[a].at[flip - 1], send_sems, recv_sems,
                    3 * a + flip - 1, _flip_peer(flip))
            for a in range(len(srcs)) for flip in (1, 2, 3)]


class _Exchange:
    def __init__(self, copies, n_src, send_sems, recv_sems, thru, token):
        self.copies, self.n_src, self.send_sems, self.recv_sems, self.thru, self.token = (
            copies, n_src, send_sems, recv_sems, thru, token)


def _exchange_start(name, copies, srcs, lands, n_copies, after=()):
    bufs = list(srcs) + list(lands)
    nb, ns = len(bufs), len(srcs)

    def body(*refs):
        send_sems, recv_sems = refs[nb + len(after)], refs[nb + len(after) + 1]
        for cp in copies(refs[:ns], refs[ns:nb], send_sems, recv_sems):
            cp.start()
        refs[-1][...] = jnp.zeros_like(refs[-1])

    out = pl.pallas_call(
        body, name=name,
        out_shape=(pltpu.SemaphoreType.DMA((n_copies,)), pltpu.SemaphoreType.DMA((n_copies,)),
                   *[pltpu.HBM(b.shape, b.dtype) for b in bufs], jax.ShapeDtypeStruct((SUBLANES, LANES), F32)),
        in_specs=[_HBM] * nb + [_ANY] * len(after),
        out_specs=(_SEM, _SEM, *[_HBM] * nb, pl.BlockSpec(memory_space=pltpu.VMEM)),
        input_output_aliases={i: 2 + i for i in range(nb)},
        compiler_params=pltpu.CompilerParams(has_side_effects=_DATAFLOW),
    )(*[pltpu.with_memory_space_constraint(b, pltpu.HBM) for b in bufs], *after)
    return _Exchange(copies, ns, out[0], out[1], list(out[2:2 + nb]), out[-1])


def _exchange_wait(name, ex, after):
    nb, ns = len(ex.thru), ex.n_src

    def body(*refs):
        for cp in ex.copies(refs[:ns], refs[ns:nb], refs[nb], refs[nb + 1]):
            cp.wait_send()
            cp.wait_recv()

    out = pl.pallas_call(
        body, name=name, out_shape=tuple(pltpu.HBM(b.shape, b.dtype) for b in ex.thru),
        in_specs=[_HBM] * nb + [_SEM, _SEM] + [_ANY] * len(after), out_specs=tuple([_HBM] * nb),
        input_output_aliases={i: i for i in range(nb)},
        compiler_params=pltpu.CompilerParams(has_side_effects=_DATAFLOW),
    )(*ex.thru, ex.send_sems, ex.recv_sems, *after)
    return list(out[:ns]), list(out[ns:])


def _col_tile(r, c):
    return next(t for t in (1024, 512, 256, 128) if c % t == 0 and (r * t * 4 <= 2**20 or t == 128))


def _add_sibling(g4, recv, pos, name):
    _, _, r, c = g4.shape
    tc = _col_tile(r, c)

    def body(pos_ref, g_ref, r_ref, o16_ref, mine_ref):
        s = g_ref[0, 0] + r_ref[0]
        o16_ref[0] = s.astype(BF16)

        @pl.when(pl.program_id(1) == pos_ref[1])
        def _():
            mine_ref[...] = s

    slot = pl.BlockSpec((1, r, tc), lambda j, k, pos_ref: (k, 0, j))
    return pl.pallas_call(
        body, name=name,
        out_shape=[jax.ShapeDtypeStruct((4, r, c), BF16), jax.ShapeDtypeStruct((r, c), F32)],
        grid_spec=pltpu.PrefetchScalarGridSpec(
            num_scalar_prefetch=1, grid=(c // tc, 4),
            in_specs=[pl.BlockSpec((1, 1, r, tc), lambda j, k, pos_ref: (k, pos_ref[0], 0, j)), slot],
            out_specs=[slot, pl.BlockSpec((r, tc), lambda j, k, pos_ref: (0, j))]),
        compiler_params=pltpu.CompilerParams(dimension_semantics=("parallel", "arbitrary")),
    )(pos, *_in_hbm(g4, recv))


class _ReduceScatter:
    def __init__(self, tag, grads_t, pos):
        self.tag, self.pos, self.names = tag, pos, list(grads_t)
        g4s = [g.reshape(4, 2, g.size // (N_DEV * g.shape[-1]), g.shape[-1]) for g in grads_t.values()]
        lands = [lax.empty((4,) + g.shape[2:], F32) for g in g4s]
        self.ex = _exchange_start(f"rs_{tag}_sibling_start", _scatter_sibling_copies, g4s, lands, 4 * len(g4s))
        self.token = self.ex.token

    def start_chips(self, after):
        g4s, from_sibling = _exchange_wait(f"rs_{self.tag}_sibling_wait", self.ex, after)
        parts = [_add_sibling(g4, rv, self.pos, f"rs_add_sibling_{k}")
                 for k, g4, rv in zip(self.names, g4s, from_sibling)]
        self.mine = [mine for _, mine in parts]
        p16s = [p16 for p16, _ in parts]
        lands = [lax.empty((3,) + p.shape[1:], BF16) for p in p16s]
        self.ex = _exchange_start(f"rs_{self.tag}_chips_start", _scatter_chips_copies, p16s, lands, 3 * len(p16s))
        self.token = self.ex.token

    def finish(self, after):
        _, from_chips = _exchange_wait(f"rs_{self.tag}_chips_wait", self.ex, after)
        return dict(zip(self.names, zip(self.mine, from_chips)))


def _rope_tables():
    positions = np.arange(SEQ, dtype=np.float32)
    inv_freq = np.power(np.float32(ROPE_THETA), -np.arange(0, ROPE_DIM, 2, dtype=np.float32) / np.float32(ROPE_DIM))
    ang = (positions[:, None] * inv_freq[None, :]).astype(np.float32)
    cos, sin = np.cos(ang).astype(np.float32), np.sin(ang).astype(np.float32)
    ones = np.ones((SEQ, HEAD_DIM - ROPE_DIM), np.float32)
    zeros8 = np.zeros((SEQ, ROPE_HALF), np.float32)
    zeros = np.zeros((SEQ, HEAD_DIM - ROPE_DIM), np.float32)
    c_head = np.concatenate([cos, cos, ones], axis=1)
    s1_head = np.concatenate([-sin, zeros8, zeros], axis=1)
    s2_head = np.concatenate([zeros8, sin, zeros], axis=1)
    return tuple(jnp.asarray(np.concatenate([t, t], axis=1)) for t in (c_head, s1_head, s2_head))


def _rope_apply(x, c, s1, s2):
    w = x.shape[1]
    return x * c + pltpu.roll(x, w - ROPE_HALF, 1) * s1 + pltpu.roll(x, ROPE_HALF, 1) * s2


def _rope_apply_t(dy, c, s1, s2):
    w = dy.shape[1]
    return dy * c + pltpu.roll(dy * s1, ROPE_HALF, 1) + pltpu.roll(dy * s2, w - ROPE_HALF, 1)


def _dil_prev_limit(has_prev):
    return jnp.where(has_prev, 0, BLOCK)


def _dil_valid(limit):
    row = lax.broadcasted_iota(jnp.int32, (BLOCK, 2 * BLOCK), 0)
    col = lax.broadcasted_iota(jnp.int32, (BLOCK, 2 * BLOCK), 1)
    dist = col - row
    return jnp.logical_and(dist >= jnp.where(col < BLOCK, limit, -BLOCK), dist <= BLOCK)


def _upper_half():
    return lax.broadcasted_iota(jnp.int32, (1, LANES), 1) >= HEAD_DIM


def _dil_rows(n, d):
    per = N_BLOCKS // d
    r, lb = n // per, n % per

    def rows(b):
        start = b * (BLOCK * d) + r
        return pl.ds(pl.multiple_of(start, BLOCK), BLOCK) if d == 1 else pl.ds(start, BLOCK, stride=d)

    return rows(lb), rows(jnp.maximum(lb - 1, 0)), lb > 0


def _dil_specs(g):
    def col(base):
        return pl.BlockSpec((SEQ, LANES), lambda p: (0, base // LANES + 2 * g + p))

    table = pl.BlockSpec((SEQ, LANES), lambda p: (0, 0))
    return [col(COL_QA), col(COL_KA), col(COL_VA)], [table] * 3


def _store_columns(blocks, dproj_ref, cols, sem):
    copies = [pltpu.make_async_copy(b, dproj_ref.at[:, pl.ds(pl.multiple_of(c * LANES, LANES), LANES)], sem.at[i])
              for i, (b, c) in enumerate(zip(blocks, cols))]
    for cp in copies:
        cp.start()
    for cp in copies:
        cp.wait()


def _dil_fwd(g, proj, tables):
    d = DILATIONS[g]
    one_block = d == N_BLOCKS

    def body(q_ref, k_ref, v_ref, c_ref, s1_ref, s2_ref, o_ref, lse_ref):
        upper = _upper_half()

        def roped(ref, rows):
            return _rope_apply(ref[rows, :], c_ref[rows, :], s1_ref[rows, :], s2_ref[rows, :])

        def block(n, carry):
            rows, prev, has_prev = _dil_rows(n, d)
            qb = (roped(q_ref, rows) * QK_SCALE).astype(BF16)
            kw, vw = roped(k_ref, rows).astype(BF16), v_ref[rows, :].astype(BF16)
            if one_block:
                row = lax.broadcasted_iota(jnp.int32, (BLOCK, BLOCK), 0)
                valid = lax.broadcasted_iota(jnp.int32, (BLOCK, BLOCK), 1) <= row
            else:
                kw = jnp.concatenate([roped(k_ref, prev).astype(BF16), kw], axis=0)
                vw = jnp.concatenate([v_ref[prev, :].astype(BF16), vw], axis=0)
                valid = _dil_valid(_dil_prev_limit(has_prev))
            outs, lses = [], []
            for head_mask in (jnp.logical_not(upper), upper):
                s = jnp.where(valid, _dot_nt(qb, jnp.where(head_mask, kw, 0)), NEG_INF)
                m = jnp.max(s, axis=-1, keepdims=True)
                p = jnp.exp(s - m)
                den = jnp.sum(p, axis=-1, keepdims=True)
                outs.append(_dot_nn((p * (1.0 / den)).astype(BF16), jnp.where(head_mask, vw, 0)))
                lses.append(m + jnp.log(den))
            o_ref[rows, :] = outs[0] + outs[1]
            lse_ref[rows, :] = jnp.where(upper, lses[1], lses[0])
            return carry

        lax.fori_loop(0, N_BLOCKS, block, 0, unroll=8)

    qkv, tabs = _dil_specs(g)
    out = pl.BlockSpec((SEQ, LANES), lambda p: (0, p))
    return pl.pallas_call(
        body, name=f"dil_attn_fwd_{g}", grid=(2,), in_specs=qkv + tabs, out_specs=[out, out],
        out_shape=[jax.ShapeDtypeStruct((SEQ, DIL_OUT_WIDTH), F32)] * 2,
        compiler_params=pltpu.CompilerParams(dimension_semantics=("parallel",)),
    )(*_in_hbm(proj, proj, proj, *tables))


def _dil_bwd(g, proj, tables, do, lse, c, dproj, deps=()):
    d = DILATIONS[g]
    one_block = d == N_BLOCKS

    def body(q_ref, k_ref, v_ref, c_ref, s1_ref, s2_ref, do_ref, lse_ref, cc_ref, dproj_in, *rest):
        dproj_ref, dq_acc, dk_acc, dv_acc, dq_out, dk_out, dv_out, sem = rest[len(deps):]
        upper = _upper_half()
        dk_acc[...] = jnp.zeros_like(dk_acc)
        dv_acc[...] = jnp.zeros_like(dv_acc)

        def roped(ref, rows):
            return _rope_apply(ref[rows, :], c_ref[rows, :], s1_ref[rows, :], s2_ref[rows, :])

        def block(n, carry):
            rows, prev, has_prev = _dil_rows(n, d)
            qb = (roped(q_ref, rows) * QK_SCALE).astype(BF16)
            dob = do_ref[rows, :].astype(BF16)
            kw, vw = roped(k_ref, rows).astype(BF16), v_ref[rows, :].astype(BF16)
            if one_block:
                row = lax.broadcasted_iota(jnp.int32, (BLOCK, BLOCK), 0)
                valid = lax.broadcasted_iota(jnp.int32, (BLOCK, BLOCK), 1) <= row
            else:
                kw = jnp.concatenate([roped(k_ref, prev).astype(BF16), kw], axis=0)
                vw = jnp.concatenate([v_ref[prev, :].astype(BF16), vw], axis=0)
                valid = _dil_valid(_dil_prev_limit(has_prev))
            lse_t, c_t = lse_ref[rows, :], cc_ref[rows, :]
            dq, dk, dv = None, None, None
            for e, head_mask in enumerate((jnp.logical_not(upper), upper)):
                km, vm = jnp.where(head_mask, kw, 0), jnp.where(head_mask, vw, 0)
                lse_col = lse_t[:, e * HEAD_DIM:e * HEAD_DIM + 1]
                c_col = c_t[:, e * HEAD_DIM:e * HEAD_DIM + 1]
                p = jnp.where(valid, jnp.exp(_dot_nt(qb, km) - lse_col), 0.0)
                ds = (p * (_dot_nt(dob, vm) + c_col)).astype(BF16)
                parts = (_dot_nn(ds, km), _dot_tn(ds, jnp.where(head_mask, qb, 0)),
                         _dot_tn(p.astype(BF16), jnp.where(head_mask, dob, 0)))
                dq, dk, dv = parts if dq is None else (dq + parts[0], dk + parts[1], dv + parts[2])
            dq_acc[rows, :] = dq * QK_SCALE
            if one_block:
                dk_acc[rows, :] += dk
                dv_acc[rows, :] += dv
            else:
                dk_acc[prev, :] += dk[:BLOCK]
                dv_acc[prev, :] += dv[:BLOCK]
                dk_acc[rows, :] += dk[BLOCK:]
                dv_acc[rows, :] += dv[BLOCK:]
            return carry

        lax.fori_loop(0, N_BLOCKS, block, 0, unroll=8)
        tabs = (c_ref[...], s1_ref[...], s2_ref[...])
        dq_out[...] = _rope_apply_t(dq_acc[...], *tabs).astype(BF16)
        dk_out[...] = _rope_apply_t(dk_acc[...], *tabs).astype(BF16)
        dv_out[...] = dv_acc[...].astype(BF16)
        pair = 2 * g + pl.program_id(0)
        _store_columns((dq_out, dk_out, dv_out), dproj_ref,
                       [base // LANES + pair for base in (COL_QA, COL_KA, COL_VA)], sem)

    qkv, tabs = _dil_specs(g)
    tok = pl.BlockSpec((SEQ, LANES), lambda p: (0, p))
    return pl.pallas_call(
        body, name=f"dil_attn_bwd_{g}", grid=(2,),
        in_specs=qkv + tabs + [tok, tok, tok, _ANY] + [_ANY] * len(deps), out_specs=_ANY,
        out_shape=jax.ShapeDtypeStruct(dproj.shape, dproj.dtype),
        scratch_shapes=[pltpu.VMEM((SEQ, LANES), F32)] * 3 + [pltpu.VMEM((SEQ, LANES), BF16)] * 3
        + [pltpu.SemaphoreType.DMA((3,))],
        input_output_aliases={9: 0},
        compiler_params=pltpu.CompilerParams(dimension_semantics=("arbitrary",)),
    )(*_in_hbm(proj, proj, proj, *tables, do, lse, c, dproj), *deps)


def _group_weights(l0, l1, l2):
    m = jnp.maximum(jnp.maximum(l0, l1), l2)
    e0, e1, e2 = jnp.exp(l0 - m), jnp.exp(l1 - m), jnp.exp(l2 - m)
    tot = e0 + e1 + e2
    return e0 / tot, e1 / tot, e2 / tot


def _dil_combine(outs, lses, deps=()):
    def fn(o0, o1, o2, l0, l1, l2):
        w0, w1, w2 = _group_weights(l0, l1, l2)
        return w0 * o0 + w1 * o1 + w2 * o2

    w = DIL_OUT_WIDTH
    return _rowwise(fn, "dil_combine", SEQ, 512, [(a, w, 0) for a in list(outs) + list(lses)], [], [(w, F32)],
                    deps=deps)[0]


def _dil_combine_bwd(d_out, outs, lses, deps=()):
    w = DIL_OUT_WIDTH

    def fn(d, o0, o1, o2, l0, l1, l2):
        row = lax.broadcasted_iota(jnp.int32, (w, w), 0) // HEAD_DIM
        col = lax.broadcasted_iota(jnp.int32, (w, w), 1) // HEAD_DIM
        same_head = jnp.where(row == col, 1.0, 0.0).astype(BF16)
        ws = _group_weights(l0, l1, l2)
        dws = [_dot3_nn(d * og, same_head) for og in (o0, o1, o2)]
        mean = ws[0] * dws[0] + ws[1] * dws[1] + ws[2] * dws[2]
        return tuple(wg * d for wg in ws) + tuple(-wg * mean for wg in ws)

    res = _rowwise(fn, "dil_combine_bwd", SEQ, 256, [(a, w, 0) for a in [d_out] + list(outs) + list(lses)], [],
                   [(w, F32)] * 6, deps=deps)
    return res[:3], res[3:]


def _log1p(e):
    u = 1.0 + e
    return jnp.where(u == 1.0, e, jnp.log(u) * (e / (u - 1.0)))


def _fox_gate(proj, b_pad, deps=()):
    def body(f_ref, b_ref, *rest):
        o_ref = rest[-1]
        z = f_ref[...] + b_ref[...]
        logf = (jnp.minimum(z, 0.0) - _log1p(jnp.exp(-jnp.abs(z)))).T[:F_ROWS]
        row = lax.broadcasted_iota(jnp.int32, (BLOCK, BLOCK), 0)
        col = lax.broadcasted_iota(jnp.int32, (BLOCK, BLOCK), 1)
        before = jnp.where(row <= col, 1.0, 0.0).astype(BF16)
        carry = jnp.zeros((F_ROWS, 1), F32)
        for blk in range(N_BLOCKS):
            run = _dot3_nn(logf[:, blk * BLOCK:(blk + 1) * BLOCK], before) + carry
            o_ref[:, blk * BLOCK:(blk + 1) * BLOCK] = run
            carry = run[:, BLOCK - 1:BLOCK]

    return pl.pallas_call(
        body, name="fox_gate", grid=(1,),
        in_specs=[pl.BlockSpec((SEQ, LANES), lambda i: (0, COL_F // LANES)), pl.BlockSpec((1, LANES), lambda i: (0, 0))]
        + [_ANY] * len(deps),
        out_specs=pl.BlockSpec((F_ROWS, SEQ), lambda i: (0, 0)),
        out_shape=jax.ShapeDtypeStruct((F_ROWS, SEQ), F32),
    )(*_in_hbm(proj, b_pad), *deps)


def _fox_gate_bwd(d_cum, proj, b_pad, dproj):
    def body(d_ref, f_ref, b_ref, dproj_ref, dz_ref, db_ref):
        row = lax.broadcasted_iota(jnp.int32, (BLOCK, BLOCK), 0)
        col = lax.broadcasted_iota(jnp.int32, (BLOCK, BLOCK), 1)
        after = jnp.where(row >= col, 1.0, 0.0).astype(BF16)
        carry = jnp.zeros((F_ROWS, 1), F32)
        parts = [None] * N_BLOCKS
        for blk in reversed(range(N_BLOCKS)):
            run = _dot3_nn(d_ref[:, blk * BLOCK:(blk + 1) * BLOCK], after) + carry
            parts[blk] = run
            carry = run[:, 0:1]
        dlogf = jnp.concatenate(parts, axis=1)
        dlogf = jnp.concatenate([dlogf, jnp.zeros((LANES - F_ROWS, SEQ), F32)], axis=0).T
        dz = dlogf * _sigmoid(-(f_ref[...] + b_ref[...]))
        dz_ref[...] = dz.astype(BF16)
        db_ref[...] = jnp.sum(dz, axis=0, keepdims=True)

    f_cols = pl.BlockSpec((SEQ, LANES), lambda i: (0, COL_F // LANES))
    return pl.pallas_call(
        body, name="fox_gate_bwd", grid=(1,),
        in_specs=[pl.BlockSpec((F_ROWS, SEQ), lambda i: (0, 0)), f_cols, pl.BlockSpec((1, LANES), lambda i: (0, 0)), _ANY],
        out_specs=[f_cols, pl.BlockSpec((1, LANES), lambda i: (0, 0))],
        out_shape=[jax.ShapeDtypeStruct(dproj.shape, dproj.dtype), jax.ShapeDtypeStruct((1, LANES), F32)],
        input_output_aliases={3: 0},
    )(*_in_hbm(d_cum, proj, b_pad, dproj))


FOX_TILE = 256
FOX_TILES = SEQ // FOX_TILE


def _row_to_col(row):
    n = row.shape[1]
    eye = lax.broadcasted_iota(jnp.int32, (n, n), 0) == lax.broadcasted_iota(jnp.int32, (n, n), 1)
    return jnp.sum(jnp.where(eye, row, 0.0), axis=1, keepdims=True)


def _fox_scores(q_tile, km, f_row, i):
    t = FOX_TILE
    ext = (i + 1) * t
    f_q = _row_to_col(f_row[:, i * t:(i + 1) * t])
    s = _dot_nt(q_tile, km[:ext]) + (f_q - f_row[:, :ext])
    row = lax.broadcasted_iota(jnp.int32, (t, ext), 0) + i * t
    col = lax.broadcasted_iota(jnp.int32, (t, ext), 1)
    return s, col <= row


def _fox_specs():
    qkv = [pl.BlockSpec((SEQ, LANES), lambda p, base=base: (0, base // LANES + p)) for base in (COL_QB, COL_KB, COL_VB)]
    return qkv, pl.BlockSpec((F_ROWS, SEQ), lambda p: (0, 0))


def _fox_fwd(proj, f_rows):
    t = FOX_TILE

    def body(q_ref, k_ref, v_ref, f_ref, o_ref, lse_ref):
        pair = pl.program_id(0)
        upper = _upper_half()
        masks = (jnp.logical_not(upper), upper)
        k16, v16 = k_ref[...].astype(BF16), v_ref[...].astype(BF16)
        kms = [jnp.where(hm, k16, 0) for hm in masks]
        vms = [jnp.where(hm, v16, 0) for hm in masks]
        f_row = [f_ref[pl.ds(2 * pair + e, 1), :] for e in range(2)]
        for i in range(FOX_TILES):
            q_tile = (q_ref[i * t:(i + 1) * t, :] * QK_SCALE).astype(BF16)
            outs, lses = [], []
            for e in range(2):
                s, causal = _fox_scores(q_tile, kms[e], f_row[e], i)
                s = jnp.where(causal, s, NEG_INF)
                m = jnp.max(s, axis=-1, keepdims=True)
                p = jnp.exp(s - m)
                den = jnp.sum(p, axis=-1, keepdims=True)
                outs.append(_dot_nn((p * (1.0 / den)).astype(BF16), vms[e][:(i + 1) * t]))
                lses.append(m + jnp.log(den))
            o_ref[i * t:(i + 1) * t, :] = outs[0] + outs[1]
            lse_ref[i * t:(i + 1) * t, :] = jnp.where(upper, lses[1], lses[0])

    qkv, f_spec = _fox_specs()
    tok = pl.BlockSpec((SEQ, LANES), lambda p: (0, p))
    return pl.pallas_call(
        body, name="fox_attn_fwd", grid=(FOX_WIDTH // LANES,),
        in_specs=qkv + [f_spec], out_specs=[tok, tok],
        out_shape=[jax.ShapeDtypeStruct((SEQ, FOX_WIDTH), F32)] * 2,
        compiler_params=pltpu.CompilerParams(
            dimension_semantics=("parallel",), vmem_limit_bytes=_vmem_limit(8 * t * SEQ * 4)),
    )(*_in_hbm(proj, proj, proj, f_rows))


def _fox_bwd(proj, do, lse, f_rows, dproj):
    t = FOX_TILE

    def body(q_ref, k_ref, v_ref, f_ref, do_ref, lse_ref, dproj_in, dproj_ref, df_ref, dk_acc, dv_acc,
             dq_out, dk_out, dv_out, sem):
        pair = pl.program_id(0)
        upper = _upper_half()
        masks = (jnp.logical_not(upper), upper)
        k16, v16 = k_ref[...].astype(BF16), v_ref[...].astype(BF16)
        kms = [jnp.where(hm, k16, 0) for hm in masks]
        vms = [jnp.where(hm, v16, 0) for hm in masks]
        f_row = [f_ref[pl.ds(2 * pair + e, 1), :] for e in range(2)]
        dk_acc[...] = jnp.zeros_like(dk_acc)
        dv_acc[...] = jnp.zeros_like(dv_acc)
        df_ref[...] = jnp.zeros_like(df_ref)
        for i in range(FOX_TILES):
            ext = (i + 1) * t
            q_tile = (q_ref[i * t:(i + 1) * t, :] * QK_SCALE).astype(BF16)
            do_tile = do_ref[i * t:(i + 1) * t, :]
            lse_t = lse_ref[i * t:(i + 1) * t, :]
            dq = None
            for e in range(2):
                s, causal = _fox_scores(q_tile, kms[e], f_row[e], i)
                p = jnp.where(causal, jnp.exp(s - lse_t[:, e * HEAD_DIM:e * HEAD_DIM + 1]), 0.0)
                dp = _dot_nt(do_tile, vms[e][:ext])
                ds = p * (dp - jnp.sum(p * dp, axis=-1, keepdims=True))
                df_ref[0, e:e + 1, :ext] -= jnp.sum(ds, axis=0, keepdims=True)
                ds = ds.astype(BF16)
                part = _dot_nn(ds, kms[e][:ext])
                dq = part if dq is None else dq + part
                dk_acc[:ext, :] += _dot_tn(ds, jnp.where(masks[e], q_tile, 0))
                dv_acc[:ext, :] += _dot_tn(p.astype(BF16), jnp.where(masks[e], do_tile, 0))
            dq_out[i * t:(i + 1) * t, :] = (dq * QK_SCALE).astype(BF16)
        dk_out[...] = dk_acc[...].astype(BF16)
        dv_out[...] = dv_acc[...].astype(BF16)
        _store_columns((dq_out, dk_out, dv_out), dproj_ref, [base // LANES + pair for base in (COL_QB, COL_KB, COL_VB)],
                       sem)

    qkv, f_spec = _fox_specs()
    tok = pl.BlockSpec((SEQ, LANES), lambda p: (0, p))
    return pl.pallas_call(
        body, name="fox_attn_bwd", grid=(FOX_WIDTH // LANES,),
        in_specs=qkv + [f_spec, tok, tok, _ANY],
        out_specs=[_ANY, pl.BlockSpec((1, SUBLANES, SEQ), lambda p: (p, 0, 0))],
        out_shape=[jax.ShapeDtypeStruct(dproj.shape, dproj.dtype),
                   jax.ShapeDtypeStruct((FOX_WIDTH // LANES, SUBLANES, SEQ), F32)],
        scratch_shapes=[pltpu.VMEM((SEQ, LANES), F32)] * 2 + [pltpu.VMEM((SEQ, LANES), BF16)] * 3
        + [pltpu.SemaphoreType.DMA((3,))],
        input_output_aliases={6: 0},
        compiler_params=pltpu.CompilerParams(
            dimension_semantics=("arbitrary",), vmem_limit_bytes=_vmem_limit(10 * t * SEQ * 4)),
    )(*_in_hbm(proj, proj, proj, f_rows, do, lse, dproj))


MIX_TILE = 256


def _mix_out(out_a, out_b, proj, x, wt_pa, wt_pb, w_out, g_post, g_ffn_pre):
    tm = MIX_TILE

    def body(a_ref, b_ref, ga_ref, gb_ref, x_ref, wpa_ref, wpb_ref, wo_ref, g2_ref, g3_ref,
             merged_ref, mix_ref, x1_ref, h2_ref):
        ya = _dot_nn(a_ref[...].astype(BF16), wpa_ref[...])
        yb = _dot_nn(b_ref[...].astype(BF16), wpb_ref[...])
        merged = (_sigmoid(ga_ref[...]) * ya + _sigmoid(gb_ref[...]) * yb).astype(BF16)
        merged_ref[...] = merged
        mix = _dot_nn(merged, wo_ref[...])
        mix_ref[...] = mix
        x1 = x_ref[...] + mix * _rms_scale(mix) * g2_ref[...]
        x1_ref[...] = x1
        h2_ref[...] = (x1 * _rms_scale(x1) * g3_ref[...]).astype(BF16)

    def rows(w, cb=0):
        return pl.BlockSpec((tm, w), lambda i, cb=cb: (i, cb))

    def whole(a):
        return pl.BlockSpec(a.shape, lambda i: (0, 0))

    d = D_MODEL
    blk = _nbytes((tm, d), F32) * 6 + sum(_nbytes(a.shape, BF16) for a in (wt_pa, wt_pb, w_out))
    return pl.pallas_call(
        body, name="mix_out", grid=(SEQ // tm,),
        in_specs=[rows(DIL_OUT_WIDTH), rows(FOX_WIDTH), rows(d, COL_GA // d), rows(d, COL_GB // d), rows(d),
                  whole(wt_pa), whole(wt_pb), whole(w_out), whole(g_post), whole(g_ffn_pre)],
        out_specs=[rows(d)] * 4,
        out_shape=[jax.ShapeDtypeStruct((SEQ, d), dt) for dt in (BF16, F32, F32, BF16)],
        compiler_params=pltpu.CompilerParams(dimension_semantics=("parallel",), vmem_limit_bytes=_vmem_limit(blk)),
    )(*_in_hbm(out_a, out_b, proj, proj, x, wt_pa, wt_pb, w_out, g_post, g_ffn_pre))


def _mix_out_bwd(dmix, out_a, out_b, proj, wt_pa, wt_pb, w_out, deps=()):
    tm = MIX_TILE

    def body(dm_ref, a_ref, b_ref, ga_ref, gb_ref, wpa_ref, wpb_ref, wo_ref, *rest):
        dproj_ref, dya_ref, dyb_ref, da_ref, db_ref = rest[len(deps):]
        dmerged = _dot_nt(dm_ref[...], wo_ref[...])
        ya = _dot_nn(a_ref[...].astype(BF16), wpa_ref[...])
        yb = _dot_nn(b_ref[...].astype(BF16), wpb_ref[...])
        sa, sb = _sigmoid(ga_ref[...]), _sigmoid(gb_ref[...])
        dproj_ref[:, COL_GA:COL_GA + D_MODEL] = (dmerged * ya * (sa * (1.0 - sa))).astype(BF16)
        dproj_ref[:, COL_GB:COL_GB + D_MODEL] = (dmerged * yb * (sb * (1.0 - sb))).astype(BF16)
        dproj_ref[:, COL_GB + D_MODEL:] = jnp.zeros((tm, COL_QA - COL_GB - D_MODEL), BF16)
        dya = (dmerged * sa).astype(BF16)
        dyb = (dmerged * sb).astype(BF16)
        dya_ref[...] = dya
        dyb_ref[...] = dyb
        da_ref[...] = _dot_nt(dya, wpa_ref[...])
        db_ref[...] = _dot_nt(dyb, wpb_ref[...]).astype(BF16)

    def rows(w, cb=0):
        return pl.BlockSpec((tm, w), lambda i, cb=cb: (i, cb))

    def whole(a):
        return pl.BlockSpec(a.shape, lambda i: (0, 0))

    d = D_MODEL
    blk = _nbytes((tm, d), F32) * 8 + sum(_nbytes(a.shape, BF16) for a in (wt_pa, wt_pb, w_out))
    return pl.pallas_call(
        body, name="mix_out_bwd", grid=(SEQ // tm,),
        in_specs=[rows(d), rows(DIL_OUT_WIDTH), rows(FOX_WIDTH), rows(d, COL_GA // d), rows(d, COL_GB // d),
                  whole(wt_pa), whole(wt_pb), whole(w_out)] + [_ANY] * len(deps),
        out_specs=[rows(COL_QA)] + [rows(d)] * 2 + [rows(DIL_OUT_WIDTH), rows(FOX_WIDTH)],
        out_shape=[jax.ShapeDtypeStruct((SEQ, PROJ_COLS), BF16)] + [jax.ShapeDtypeStruct((SEQ, d), BF16)] * 2
        + [jax.ShapeDtypeStruct((SEQ, DIL_OUT_WIDTH), F32), jax.ShapeDtypeStruct((SEQ, FOX_WIDTH), BF16)],
        compiler_params=pltpu.CompilerParams(dimension_semantics=("parallel",), vmem_limit_bytes=_vmem_limit(blk)),
    )(*_in_hbm(dmix, out_a, out_b, proj, proj, wt_pa, wt_pb, w_out), *deps)


FFN_TM, FFN_TN = 2048, 256


def _ffn_up(h2, wt_gate, wt_up):
    tm, tn = FFN_TM, FFN_TN

    def body(h_ref, wg_ref, wu_ref, gate_ref, up_ref, act_ref):
        gate = _dot_nt(h_ref[...], wg_ref[...])
        up = _dot_nt(h_ref[...], wu_ref[...])
        gate_ref[...] = gate
        up_ref[...] = up
        act_ref[...] = (gate * _sigmoid(gate) * up).astype(BF16)

    tile = pl.BlockSpec((tm, tn), lambda i, j: (i, j))
    w_spec = pl.BlockSpec((tn, D_MODEL), lambda i, j: (j, 0))
    return pl.pallas_call(
        body, name="ffn_up", grid=(SEQ // tm, D_FF // tn),
        in_specs=[pl.BlockSpec((tm, D_MODEL), lambda i, j: (i, 0)), w_spec, w_spec],
        out_specs=[tile, tile, tile],
        out_shape=[jax.ShapeDtypeStruct((SEQ, D_FF), dt) for dt in (F32, F32, BF16)],
        compiler_params=pltpu.CompilerParams(
            dimension_semantics=("parallel", "parallel"), vmem_limit_bytes=_vmem_limit(8 * 2**20)),
    )(*_in_hbm(h2, wt_gate, wt_up))


def _ffn_act_bwd(dff, w_down, gate, up):
    tm, tn = FFN_TM, FFN_TN

    def body(d_ref, wd_ref, gate_ref, up_ref, dgate_ref, dup_ref):
        dact = _dot_nt(d_ref[...], wd_ref[...])
        gate = gate_ref[...]
        sg = _sigmoid(gate)
        dgate_ref[...] = (dact * up_ref[...] * (sg * (1.0 + gate * (1.0 - sg)))).astype(BF16)
        dup_ref[...] = (dact * (gate * sg)).astype(BF16)

    tile = pl.BlockSpec((tm, tn), lambda i, j: (i, j))
    return pl.pallas_call(
        body, name="ffn_act_bwd", grid=(SEQ // tm, D_FF // tn),
        in_specs=[pl.BlockSpec((tm, D_MODEL), lambda i, j: (i, 0)), pl.BlockSpec((tn, D_MODEL), lambda i, j: (j, 0)),
                  tile, tile],
        out_specs=[tile, tile],
        out_shape=[jax.ShapeDtypeStruct((SEQ, D_FF), BF16)] * 2,
        compiler_params=pltpu.CompilerParams(
            dimension_semantics=("parallel", "parallel"), vmem_limit_bytes=_vmem_limit(8 * 2**20)),
    )(*_in_hbm(dff, w_down, gate, up))


EPILOGUE_TM = 512


def _loss_head(act, w_down, x1, target, g_post):
    def fn(ff, x1, tgt, g):
        r = _rms_scale(ff)
        nrm = ff * r
        err = (x1 + nrm * g) - tgt
        loss = 0.5 * jnp.sum(jnp.mean(err * err, axis=-1, keepdims=True), axis=0, keepdims=True)
        dy = err * (1.0 / D_MODEL)
        u = dy * g
        dff = r * u - ff * (r * r * r) * jnp.mean(u * ff, axis=-1, keepdims=True)
        return dy, dff, jnp.broadcast_to(loss, (1, LANES)), jnp.sum(dy * nrm, axis=0, keepdims=True)

    d = D_MODEL
    return _matmul_rowwise([(act, w_down)], fn, "ffn_down_loss", EPILOGUE_TM, [(x1, d, 0), (target, d, 0)], [g_post],
                           [(d, F32), (d, BF16)], [LANES, d])


def _post_ffn_bwd(dgate, wt_gate, dup, wt_up, x1, dy, mix, g_ffn_pre, g_mix_post, deps=()):
    def fn(dh2, x1, dy, mix, g3, g2):
        dx, dg3 = _rms_bwd(x1, dh2, g3)
        dx1 = dy + dx
        dmix, dg2 = _rms_bwd(mix, dx1, g2)
        return dx1, dmix, dg3, dg2

    d = D_MODEL
    return _matmul_rowwise([(dgate, wt_gate), (dup, wt_up)], fn, "ffn_up_bwd", EPILOGUE_TM,
                           [(x1, d, 0), (dy, d, 0), (mix, d, 0)], [g_ffn_pre, g_mix_post],
                           [(d, F32), (d, BF16)], [d, d], deps=deps)


def _input_bwd(dproj, wt_r, x, dx1, g_pre, deps=()):
    def fn(dh, x, dx1, g):
        dx, dg = _rms_bwd(x, dh, g)
        return dx1 + dx, dg

    d = D_MODEL
    return _matmul_rowwise([(dproj, wt_r)], fn, "in_proj_bwd", EPILOGUE_TM, [(x, d, 0), (dx1, d, 0)], [g_pre],
                           [(d, F32)], [d], deps=deps)


def _adam_math(w, g, m, v):
    m = ADAM_B1 * m + (1.0 - ADAM_B1) * g
    v = ADAM_B2 * v + (1.0 - ADAM_B2) * (g * g)
    m_hat = m / (1.0 - ADAM_B1 ** ADAM_STEP)
    v_hat = v / (1.0 - ADAM_B2 ** ADAM_STEP)
    delta = -ADAM_LR * (m_hat / (jnp.sqrt(v_hat) + ADAM_EPS) + ADAM_WD * w)
    return delta, m, v


def _adam(w, mine, recv, m, v, name):
    r, c = w.shape
    tc = _col_tile(r, c)

    def body(w_ref, p_ref, r_ref, m_ref, v_ref, g_ref, d_ref, nm_ref, nv_ref):
        g = ((p_ref[...] + r_ref[0].astype(F32)) + r_ref[1].astype(F32)) + r_ref[2].astype(F32)
        g_ref[...] = g
        d_ref[...], nm_ref[...], nv_ref[...] = _adam_math(w_ref[...], g, m_ref[...], v_ref[...])

    spec = pl.BlockSpec((r, tc), lambda j: (0, j))
    return pl.pallas_call(
        body, name=name, grid=(c // tc,),
        in_specs=[spec, spec, pl.BlockSpec((3, r, tc), lambda j: (0, 0, j)), spec, spec], out_specs=[spec] * 4,
        out_shape=[jax.ShapeDtypeStruct((r, c), F32)] * 4,
        compiler_params=pltpu.CompilerParams(dimension_semantics=("parallel",)),
    )(*_in_hbm(w, mine, recv, m, v))


def _adam_small(gathered, ws, ms, vs, loss_parts):
    n = len(ws)

    def body(*refs):
        outs = refs[4 * n + 1:]
        loss = refs[4 * n][0]
        for dev in range(1, N_DEV):
            loss = loss + refs[4 * n][dev]
        outs[4 * n][...] = loss
        for i in range(n):
            ga_ref, w_ref, m_ref, v_ref = (refs[j * n + i] for j in range(4))
            g = ga_ref[0]
            for dev in range(1, N_DEV):
                g = g + ga_ref[dev]
            g = g[:, :w_ref.shape[1]]
            outs[4 * i][...] = g
            outs[4 * i + 1][...], outs[4 * i + 2][...], outs[4 * i + 3][...] = _adam_math(
                w_ref[...], g, m_ref[...], v_ref[...])

    out_shape = [jax.ShapeDtypeStruct(w.shape, F32) for w in ws for _ in range(4)]
    out_shape.append(jax.ShapeDtypeStruct((1, LANES), F32))
    out = pl.pallas_call(body, name="adam_small", out_shape=out_shape)(*gathered, *ws, *ms, *vs, loss_parts)
    return [out[4 * i:4 * i + 4] for i in range(n)], out[4 * n]


_PROJ_SEGMENTS = ((3848, 5896), (None, COL_QA - 2 * D_MODEL), (0, 3840), (3840, 3848), (None, PROJ_COLS - COL_F - 8))


def _proj_weight_t(gathered):
    w = gathered.reshape(IN_COLS, D_MODEL)
    return jnp.concatenate([jnp.zeros((hi, D_MODEL), w.dtype) if lo is None else w[lo:hi] for lo, hi in _PROJ_SEGMENTS],
                           axis=0)


def _proj_weight_grad_slots(dwt_r):
    starts, at = [], 0
    for lo, hi in _PROJ_SEGMENTS:
        if lo is not None:
            starts.append((lo, hi, at))
        at += hi if lo is None else hi - lo
    slots = []
    for dev in range(N_DEV):
        pieces, lo, end = [], dev * IN_SHARD, (dev + 1) * IN_SHARD
        for seg_lo, seg_hi, seg_at in sorted(starts):
            a, b = max(lo, seg_lo), min(end, seg_hi)
            if a < b:
                pieces.append(dwt_r[seg_at + a - seg_lo:seg_at + b - seg_lo])
        slots.append(pieces[0] if len(pieces) == 1 else jnp.concatenate(pieces, axis=0))
    return jnp.stack(slots)


def kernel(x, w_in, w_proj_a, w_proj_b, w_out, b_forget, w_ffn_gate, w_ffn_up, w_ffn_down, norm_mix_pre, norm_mix_post, norm_ffn_pre, norm_ffn_post, loss_target, m_w_in, m_w_proj_a, m_w_proj_b, m_w_out, m_b_forget, m_w_ffn_gate, m_w_ffn_up, m_w_ffn_down, m_norm_mix_pre, m_norm_mix_post, m_norm_ffn_pre, m_norm_ffn_post, v_w_in, v_w_proj_a, v_w_proj_b, v_w_out, v_b_forget, v_w_ffn_gate, v_w_ffn_up, v_w_ffn_down, v_norm_mix_pre, v_norm_mix_post, v_norm_ffn_pre, v_norm_ffn_post):
    d = D_MODEL
    names = ("w_in", "w_proj_a", "w_proj_b", "w_out", "w_ffn_gate", "w_ffn_up", "w_ffn_down")
    col_sharded = ("w_in", "w_ffn_gate", "w_ffn_up")

    def row_shards(arrs):
        return {k: (a[0].T if k in col_sharded else a[0]) for k, a in zip(names, arrs)}

    shards = row_shards((w_in, w_proj_a, w_proj_b, w_out, w_ffn_gate, w_ffn_up, w_ffn_down))
    moments_m = row_shards((m_w_in, m_w_proj_a, m_w_proj_b, m_w_out, m_w_ffn_gate, m_w_ffn_up, m_w_ffn_down))
    moments_v = row_shards((v_w_in, v_w_proj_a, v_w_proj_b, v_w_out, v_w_ffn_gate, v_w_ffn_up, v_w_ffn_down))
    pos = jnp.stack([lax.axis_index("c"), 2 * lax.axis_index("x") + lax.axis_index("y")]).astype(jnp.int32)
    x2, target = x[0], loss_target[0]

    me = 4 * lax.axis_index("x") + 2 * lax.axis_index("y") + lax.axis_index("c")
    mid_names, ffn_names = names[1:4], names[4:]
    first_names, later_names = names[:1], names[1:]
    shards16 = {k: shards[k].astype(BF16) for k in names}

    def landing(k):
        return lax.dynamic_update_slice(lax.empty((N_DEV,) + shards[k].shape, BF16), shards16[k][None], (me, 0, 0))

    ag_first = _exchange_start("ag_first_chips_start", _gather_chips_copies, [shards16[k] for k in first_names],
                               [landing(k) for k in first_names], 3 * len(first_names))
    h = _rowwise(lambda xb, g: xb * _rms_scale(xb) * g, "norm_mix_pre", SEQ, 256, [(x2, d, 0)], [norm_mix_pre],
                 [(d, BF16)], deps=[ag_first.token])[0]
    _, lands = _exchange_wait("ag_first_chips_wait", ag_first, [h])
    ag_first = _exchange_start("ag_first_sibling_start", _gather_sibling_copies, [], lands, 4 * len(first_names))
    ag_later = _exchange_start("ag_later_chips_start", _gather_chips_copies, [shards16[k] for k in later_names],
                               [landing(k) for k in later_names], 3 * len(later_names), after=[ag_first.token])
    gathered = dict(zip(first_names, _exchange_wait("ag_first_sibling_wait", ag_first, [ag_later.token])[1]))
    wt_r = _proj_weight_t(gathered["w_in"])

    proj = _matmul([(h, wt_r)], "nt", F32, "in_proj", 1024, 896, 1024)
    tables = _rope_tables()
    o_dil, lse_dil = zip(*[_dil_fwd(g, proj, tables) for g in range(len(DILATIONS))])
    out_a = _dil_combine(o_dil, lse_dil)
    _, lands = _exchange_wait("ag_later_chips_wait", ag_later, [out_a])
    ag_later = _exchange_start("ag_later_sibling_start", _gather_sibling_copies, [], lands, 4 * len(later_names))

    b_pad = jnp.pad(b_forget, ((0, 0), (0, LANES - N_FOX_HEADS)))
    f_rows = _fox_gate(proj, b_pad, deps=[ag_later.token])
    out_b, lse_fox = _fox_fwd(proj, f_rows)

    gathered = dict(zip(later_names, _exchange_wait("ag_later_sibling_wait", ag_later, [out_b])[1]))
    wt_pa = gathered["w_proj_a"].transpose(1, 0, 2).reshape(DIL_OUT_WIDTH, d)
    wt_pb = gathered["w_proj_b"].transpose(1, 0, 2).reshape(FOX_WIDTH, d)
    w_o = gathered["w_out"].reshape(d, d)
    wt_g = gathered["w_ffn_gate"].reshape(D_FF, d)
    wt_u = gathered["w_ffn_up"].reshape(D_FF, d)
    w_d = gathered["w_ffn_down"].reshape(D_FF, d)
    merged, mix, x1, h2 = _mix_out(out_a, out_b, proj, x2, wt_pa, wt_pb, w_o, norm_mix_post, norm_ffn_pre)

    gate, up, act = _ffn_up(h2, wt_g, wt_u)
    dy, dff, loss_part, dg_ffn_post = _loss_head(act, w_d, x1, target, norm_ffn_post)

    dgate, dup = _ffn_act_bwd(dff, w_d, gate, up)
    grads_t = {}
    grads_t["w_ffn_down"] = _matmul([(act, dff)], "tn", F32, "grad_w_ffn_down", 1408, 1024, 2048)
    grads_t["w_ffn_gate"] = _matmul([(dgate, h2)], "tn", F32, "grad_w_ffn_gate", 1408, 1024, 2048)
    grads_t["w_ffn_up"] = _matmul([(dup, h2)], "tn", F32, "grad_w_ffn_up", 1408, 1024, 2048)
    rs_ffn = _ReduceScatter("ffn", {k: grads_t[k] for k in ffn_names}, pos)
    dx1, dmix, dg_ffn_pre, dg_mix_post = _post_ffn_bwd(dgate, wt_g, dup, wt_u, x1, dy, mix, norm_ffn_pre, norm_mix_post,
                                                       deps=[rs_ffn.token])
    rs_ffn.start_chips([dmix])

    dproj, dya, dyb, d_out_a, d_out_b = _mix_out_bwd(dmix, out_a, out_b, proj, wt_pa, wt_pb, w_o, deps=[rs_ffn.token])
    grads_t["w_out"] = _matmul([(merged, dmix)], "tn", F32, "grad_w_out", 1024, 1024, 1024)
    def column_slots(g):
        return g.reshape(g.shape[0], N_DEV, LANES).transpose(1, 0, 2)

    grads_t["w_proj_a"] = column_slots(_matmul([(out_a, dya)], "tn", F32, "grad_w_proj_a", DIL_OUT_WIDTH, 1024, SEQ))
    grads_t["w_proj_b"] = column_slots(_matmul([(out_b, dyb)], "tn", F32, "grad_w_proj_b", FOX_WIDTH, 1024, SEQ))
    rs_mid = _ReduceScatter("mid", {k: grads_t[k] for k in mid_names}, pos)

    do_dil, c_dil = _dil_combine_bwd(d_out_a, o_dil, lse_dil, deps=[rs_mid.token])
    rs_mid.start_chips([c_dil[0]])
    dproj, d_cum = _fox_bwd(proj, d_out_b, lse_fox, f_rows, dproj)
    d_cum_rows = jnp.pad(d_cum[:, :2].reshape(N_FOX_HEADS, SEQ), ((0, F_ROWS - N_FOX_HEADS), (0, 0)))
    dproj, db_part = _fox_gate_bwd(d_cum_rows, proj, b_pad, dproj)
    for g in range(len(DILATIONS)):
        dproj = _dil_bwd(g, proj, tables, do_dil[g], lse_dil[g], c_dil[g], dproj, deps=[rs_mid.token])

    dwt_r = _matmul([(dproj, h)], "tn", F32, "grad_w_in", 896, 1024, 2048)
    rs_in = _ReduceScatter("in", {"w_in": _proj_weight_grad_slots(dwt_r)}, pos)
    def finish(rs, after):
        return {k: _adam(shards[k], mine, recv, moments_m[k], moments_v[k], "adam_" + k)
                for k, (mine, recv) in rs.finish(after).items()}

    done = finish(rs_ffn, [rs_in.token])
    rs_in.start_chips([done[k][0] for k in ffn_names])
    grad_x, dg_mix_pre = _input_bwd(dproj, wt_r, x2, dx1, norm_mix_pre, deps=[rs_in.token])
    done.update(finish(rs_mid, [grad_x]))

    small_all = _all_gather([dg_mix_pre, dg_mix_post, dg_ffn_pre, dg_ffn_post, db_part, loss_part],
                            "small_grads_all_gather", deps=[done[k][0] for k in mid_names])
    small, loss = _adam_small(small_all[:5], [norm_mix_pre, norm_mix_post, norm_ffn_pre, norm_ffn_post, b_forget],
                              [m_norm_mix_pre, m_norm_mix_post, m_norm_ffn_pre, m_norm_ffn_post, m_b_forget],
                              [v_norm_mix_pre, v_norm_mix_post, v_norm_ffn_pre, v_norm_ffn_post, v_b_forget],
                              small_all[5])

    done.update(finish(rs_in, [small[0][0]]))

    def leaves(i):
        def nat(k):
            a = done[k][i]
            return (a.T if k in col_sharded else a)[None]

        return [nat("w_in"), nat("w_proj_a"), nat("w_proj_b"), nat("w_out"), small[4][i],
                nat("w_ffn_gate"), nat("w_ffn_up"), nat("w_ffn_down"), *[small[r][i] for r in range(4)]]

    return (loss[0, 0], grad_x[None], *leaves(0), *leaves(1), *leaves(2), *leaves(3))
```

```python
import functools
import math

import jax
import jax.numpy as jnp
import numpy as np
from jax import lax
from jax.experimental import pallas as pl
from jax.experimental.pallas import tpu as pltpu

F32 = jnp.float32
BF16 = jnp.bfloat16
MESH = pl.DeviceIdType.MESH

D_MODEL = 1024
SEQ = 2048
HEAD_DIM = 64
BLOCK = 128
N_BLOCKS = SEQ // BLOCK
DILATIONS = (1, 4, 16)
N_FOX_HEADS = 8
DIL_WIDTH = 768
DIL_OUT_WIDTH = 256
FOX_WIDTH = 512
D_FF = 2816
ROPE_THETA = 500000.0
ROPE_DIM = HEAD_DIM // 4
ROPE_HALF = ROPE_DIM // 2
EPS = 1e-6
NEG_INF = -1e30
QK_SCALE = 1.0 / math.sqrt(HEAD_DIM)
IN_COLS = 5896
N_DEV = 8
IN_SHARD = IN_COLS // N_DEV

ADAM_LR = 0.001
ADAM_B1 = 0.9
ADAM_B2 = 0.999
ADAM_EPS = 1e-08
ADAM_WD = 0.01
ADAM_STEP = 10

V7X_VMEM_BYTES = 64 * 2**20
LANES = 128
SUBLANES = 8

PROJ_COLS = 6272
COL_GA, COL_GB = 0, 1024
COL_QA, COL_KA, COL_VA = 2304, 3072, 3840
COL_QB, COL_KB, COL_VB = 4608, 5120, 5632
COL_F = 6144
F_ROWS = 16


def _vmem_limit(block_bytes):
    want = 2 * block_bytes + 16 * 2**20
    return int(min(max(want, 32 * 2**20), V7X_VMEM_BYTES - 8 * 2**20))


def _nbytes(shape, dtype):
    return math.prod(shape) * jnp.dtype(dtype).itemsize


def _in_hbm(*arrays):
    return [pltpu.with_memory_space_constraint(a, pltpu.HBM) for a in arrays]


def _dot(a, b, dims):
    return lax.dot_general(a, b, (dims, ((), ())), preferred_element_type=F32)


def _dot_nn(a, b):
    return _dot(a, b, ((1,), (0,)))


def _dot_nt(a, b):
    return _dot(a, b, ((1,), (1,)))


def _dot_tn(a, b):
    return _dot(a, b, ((0,), (0,)))


def _sigmoid(z):
    return 1.0 / (1.0 + jnp.exp(-z))


def _split3(x):
    hi = x.astype(BF16)
    r1 = x - hi.astype(F32)
    mid = r1.astype(BF16)
    lo = (r1 - mid.astype(F32)).astype(BF16)
    return hi, mid, lo


def _dot3_nn(x, ones_matrix):
    hi, mid, lo = _split3(x)
    return (_dot_nn(hi, ones_matrix) + _dot_nn(mid, ones_matrix)) + _dot_nn(lo, ones_matrix)


def _rowwise(fn, name, n_rows, tm, row_ins, bcast_ins, row_outs, acc_outs=(), deps=()):
    n_in = len(row_ins) + len(bcast_ins)
    n_ro = len(row_outs)

    def body(*refs):
        res = fn(*[r[...] for r in refs[:n_in]])
        if not isinstance(res, (tuple, list)):
            res = (res,)
        outs = refs[n_in + len(deps):]
        for r, o in zip(res[:n_ro], outs[:n_ro]):
            o[...] = r.astype(o.dtype)
        first = pl.program_id(0) == 0
        for r, o in zip(res[n_ro:], outs[n_ro:]):
            _accumulate(o, r, first)

    in_specs = [pl.BlockSpec((tm, w), lambda i, cb=cb: (i, cb)) for _, w, cb in row_ins]
    in_specs += [pl.BlockSpec(a.shape, lambda i: (0, 0)) for a in bcast_ins]
    in_specs += [pl.BlockSpec(memory_space=pl.ANY)] * len(deps)
    out_specs = [pl.BlockSpec((tm, w), lambda i: (i, 0)) for w, _ in row_outs]
    out_specs += [pl.BlockSpec((1, w), lambda i: (0, 0)) for w in acc_outs]
    out_shape = [jax.ShapeDtypeStruct((n_rows, w), dt) for w, dt in row_outs]
    out_shape += [jax.ShapeDtypeStruct((1, w), F32) for w in acc_outs]
    blk = sum(_nbytes((tm, w), a.dtype) for a, w, _ in row_ins) + sum(_nbytes((tm, w), dt) for w, dt in row_outs)
    return pl.pallas_call(
        body, name=name, grid=(n_rows // tm,), in_specs=in_specs, out_specs=out_specs, out_shape=out_shape,
        compiler_params=pltpu.CompilerParams(
            dimension_semantics=("arbitrary" if acc_outs else "parallel",), vmem_limit_bytes=_vmem_limit(3 * blk)),
    )(*_in_hbm(*[a for a, _, _ in row_ins], *bcast_ins), *deps)


def _accumulate(o_ref, part, first):
    @pl.when(first)
    def _():
        o_ref[...] = part

    @pl.when(jnp.logical_not(first))
    def _():
        o_ref[...] += part


_MM_DIMS = {"nn": ((1,), (0,)), "nt": ((1,), (1,)), "tn": ((0,), (0,))}


def _matmul(pairs, mode, out_dtype, name, tm, tn, tk, deps=()):
    a0, b0 = pairs[0]
    if mode == "tn":
        kk, m = a0.shape
    else:
        m, kk = a0.shape
    n = b0.shape[0] if mode == "nt" else b0.shape[1]
    assert m % tm == 0 and n % tn == 0 and kk % tk == 0, (name, m, n, kk)
    nk = kk // tk
    n_pairs = len(pairs)
    dims = _MM_DIMS[mode]
    n_in = 2 * n_pairs + len(deps)

    def body(*refs):
        o_ref = refs[n_in]
        part = None
        for p in range(n_pairs):
            d = _dot(refs[2 * p][...].astype(BF16), refs[2 * p + 1][...].astype(BF16), dims)
            part = d if part is None else part + d
        if nk == 1:
            o_ref[...] = part.astype(o_ref.dtype)
            return
        acc = refs[n_in + 1]
        k = pl.program_id(2)

        @pl.when(k == 0)
        def _():
            acc[...] = part

        @pl.when(k > 0)
        def _():
            acc[...] += part

        @pl.when(k == nk - 1)
        def _():
            o_ref[...] = acc[...].astype(o_ref.dtype)

    if mode == "tn":
        a_spec = pl.BlockSpec((tk, tm), lambda i, j, k: (k, i))
    else:
        a_spec = pl.BlockSpec((tm, tk), lambda i, j, k: (i, k))
    if mode == "nt":
        b_spec = pl.BlockSpec((tn, tk), lambda i, j, k: (j, k))
    else:
        b_spec = pl.BlockSpec((tk, tn), lambda i, j, k: (k, j))
    blk = sum(_nbytes((tm, tk), a.dtype) + _nbytes((tk, tn), b.dtype) for a, b in pairs) + 2 * _nbytes((tm, tn), F32)
    flat = [a for pair in pairs for a in pair]
    return pl.pallas_call(
        body, name=name, grid=(m // tm, n // tn, nk),
        in_specs=[a_spec, b_spec] * n_pairs + [pl.BlockSpec(memory_space=pl.ANY)] * len(deps),
        out_specs=pl.BlockSpec((tm, tn), lambda i, j, k: (i, j)),
        out_shape=jax.ShapeDtypeStruct((m, n), out_dtype),
        scratch_shapes=[] if nk == 1 else [pltpu.VMEM((tm, tn), F32)],
        compiler_params=pltpu.CompilerParams(
            dimension_semantics=("parallel", "parallel", "arbitrary"), vmem_limit_bytes=_vmem_limit(blk)),
    )(*flat, *deps)


def _matmul_rowwise(pairs, fn, name, tm, row_ins, bcast_ins, row_outs, acc_outs=(), deps=()):
    m = pairs[0][0].shape[0]
    n_mm, n_in = 2 * len(pairs), len(row_ins) + len(bcast_ins)
    n_ro = len(row_outs)

    def body(*refs):
        prod = None
        for p in range(len(pairs)):
            part = _dot_nn(refs[2 * p][...].astype(BF16), refs[2 * p + 1][...].astype(BF16))
            prod = part if prod is None else prod + part
        res = fn(prod, *[r[...] for r in refs[n_mm:n_mm + n_in]])
        outs = refs[n_mm + n_in + len(deps):]
        for r, o in zip(res[:n_ro], outs[:n_ro]):
            o[...] = r.astype(o.dtype)
        first = pl.program_id(0) == 0
        for r, o in zip(res[n_ro:], outs[n_ro:]):
            _accumulate(o, r, first)

    in_specs = []
    for a, b in pairs:
        in_specs += [pl.BlockSpec((tm, a.shape[1]), lambda i: (i, 0)),
                     pl.BlockSpec(b.shape, lambda i: (0, 0), pipeline_mode=pl.Buffered(1))]
    in_specs += [pl.BlockSpec((tm, w), lambda i, cb=cb: (i, cb)) for _, w, cb in row_ins]
    in_specs += [pl.BlockSpec(a.shape, lambda i: (0, 0)) for a in bcast_ins]
    in_specs += [_ANY] * len(deps)
    out_specs = [pl.BlockSpec((tm, w), lambda i: (i, 0)) for w, _ in row_outs]
    out_specs += [pl.BlockSpec((1, w), lambda i: (0, 0)) for w in acc_outs]
    out_shape = [jax.ShapeDtypeStruct((m, w), dt) for w, dt in row_outs]
    out_shape += [jax.ShapeDtypeStruct((1, w), F32) for w in acc_outs]
    blk = sum(_nbytes((tm, a.shape[1]), a.dtype) + _nbytes(b.shape, b.dtype) // 2 for a, b in pairs)
    blk += sum(_nbytes((tm, w), a.dtype) for a, w, _ in row_ins) + sum(_nbytes((tm, w), dt) for w, dt in row_outs)
    return pl.pallas_call(
        body, name=name, grid=(m // tm,), in_specs=in_specs, out_specs=out_specs, out_shape=out_shape,
        compiler_params=pltpu.CompilerParams(dimension_semantics=("arbitrary",), vmem_limit_bytes=_vmem_limit(blk)),
    )(*[a for pair in pairs for a in pair], *[a for a, _, _ in row_ins], *bcast_ins, *deps)


def _rms_scale(x):
    return lax.rsqrt(jnp.mean(x * x, axis=-1, keepdims=True) + EPS)


def _rms_bwd(xin, dyn, g):
    r = _rms_scale(xin)
    u = dyn * g
    dx = r * u - xin * (r * r * r) * jnp.mean(u * xin, axis=-1, keepdims=True)
    dg = jnp.sum(dyn * xin * r, axis=0, keepdims=True)
    return dx, dg


def _mesh_pos():
    return lax.axis_index("x"), lax.axis_index("y"), lax.axis_index("c")


def _all_gather(xs, name, deps=()):
    n = len(xs)

    def body(*refs):
        x_refs, out_refs = refs[:n], refs[n + len(deps):2 * n + len(deps)]
        send_sems, recv_sems, local_sems = refs[2 * n + len(deps):]
        mx, my, mc = _mesh_pos()
        me, sib = (mx, my, mc), (mx, my, 1 - mc)
        chips = [(1 - mx, my), (mx, 1 - my), (1 - mx, 1 - my)]

        def slot(a, dev):
            px, py, pc = dev
            return out_refs[a].at[4 * px + 2 * py + pc]

        def copy(k, a, block, to, src=None):
            return pltpu.make_async_remote_copy(
                src_ref=slot(a, block) if src is None else src, dst_ref=slot(a, block),
                send_sem=send_sems.at[a * 7 + k], recv_sem=recv_sems.at[a * 7 + k],
                device_id=to, device_id_type=MESH)

        mine = [pltpu.make_async_copy(x_refs[a], slot(a, me), local_sems.at[a]) for a in range(n)]
        for cp in mine:
            cp.start()
        first = []
        for a in range(n):
            first.append(copy(0, a, me, sib, x_refs[a]))
            first += [copy(1 + j, a, me, (*chip, mc), x_refs[a]) for j, chip in enumerate(chips)]
        for cp in first:
            cp.start()
        passed = []
        for a in range(n):
            for j, chip in enumerate(chips):
                copy(1 + j, a, (*chip, mc), me).wait_recv()
                fwd = copy(4 + j, a, (*chip, mc), sib)
                fwd.start()
                passed.append(fwd)
        for a in range(n):
            copy(0, a, sib, me).wait_recv()
            for j, chip in enumerate(chips):
                copy(4 + j, a, (*chip, 1 - mc), me).wait_recv()
        for cp in first + passed:
            cp.wait_send()
        for cp in mine:
            cp.wait()

    hbm = pl.BlockSpec(memory_space=pl.ANY)
    return pl.pallas_call(
        body, name=name,
        out_shape=[jax.ShapeDtypeStruct((N_DEV,) + x.shape, x.dtype) for x in xs],
        in_specs=[hbm] * (n + len(deps)), out_specs=[hbm] * n,
        scratch_shapes=[pltpu.SemaphoreType.DMA((7 * n,)), pltpu.SemaphoreType.DMA((7 * n,)),
                        pltpu.SemaphoreType.DMA((n,))],
    )(*xs, *deps)


_HBM = pl.BlockSpec(memory_space=pltpu.HBM)
_SEM = pl.BlockSpec(memory_space=pltpu.SEMAPHORE)
_ANY = pl.BlockSpec(memory_space=pl.ANY)
_DATAFLOW = pltpu.SideEffectType.DATAFLOW_SIDE_EFFECTING


def _flip_peer(flip):
    mx, my, mc = _mesh_pos()
    return (1 - mx if flip & 2 else mx, 1 - my if flip & 1 else my, mc)


def _remote(src, dst, send_sems, recv_sems, k, peer):
    return pltpu.make_async_remote_copy(src_ref=src, dst_ref=dst, send_sem=send_sems.at[k], recv_sem=recv_sems.at[k],
                                        device_id=peer, device_id_type=MESH)


def _gather_chips_copies(srcs, lands, send_sems, recv_sems):
    mx, my, mc = _mesh_pos()
    me = 4 * mx + 2 * my + mc
    return [_remote(srcs[a], lands[a].at[me], send_sems, recv_sems, 3 * a + flip - 1, _flip_peer(flip))
            for a in range(len(srcs)) for flip in (1, 2, 3)]


def _gather_sibling_copies(srcs, lands, send_sems, recv_sems):
    mx, my, mc = _mesh_pos()
    return [_remote(lands[a].at[2 * k + mc], lands[a].at[2 * k + mc], send_sems, recv_sems, 4 * a + k, (mx, my, 1 - mc))
            for a in range(len(lands)) for k in range(4)]


def _scatter_sibling_copies(srcs, lands, send_sems, recv_sems):
    mx, my, mc = _mesh_pos()
    return [_remote(srcs[a].at[k, 1 - mc], lands[a].at[k], send_sems, recv_sems, 4 * a + k, (mx, my, 1 - mc))
            for a in range(len(srcs)) for k in range(4)]


def _scatter_chips_copies(srcs, lands, send_sems, recv_sems):
    mx, my, _ = _mesh_pos()
    k0 = 2 * mx + my
    return [_remote(srcs[a].at[jnp.bitwise_xor(k0, flip)], lands[a].at[flip - 1], send_sems, recv_sems,
                    3 * a + flip - 1, _flip_peer(flip))
            for a in range(len(srcs)) for flip in (1, 2, 3)]


class _Exchange:
    def __init__(self, copies, n_src, send_sems, recv_sems, thru, token):
        self.copies, self.n_src, self.send_sems, self.recv_sems, self.thru, self.token = (
            copies, n_src, send_sems, recv_sems, thru, token)


def _exchange_start(name, copies, srcs, lands, n_copies, after=()):
    bufs = list(srcs) + list(lands)
    nb, ns = len(bufs), len(srcs)

    def body(*refs):
        send_sems, recv_sems = refs[nb + len(after)], refs[nb + len(after) + 1]
        for cp in copies(refs[:ns], refs[ns:nb], send_sems, recv_sems):
            cp.start()
        refs[-1][...] = jnp.zeros_like(refs[-1])

    out = pl.pallas_call(
        body, name=name,
        out_shape=(pltpu.SemaphoreType.DMA((n_copies,)), pltpu.SemaphoreType.DMA((n_copies,)),
                   *[pltpu.HBM(b.shape, b.dtype) for b in bufs], jax.ShapeDtypeStruct((SUBLANES, LANES), F32)),
        in_specs=[_HBM] * nb + [_ANY] * len(after),
        out_specs=(_SEM, _SEM, *[_HBM] * nb, pl.BlockSpec(memory_space=pltpu.VMEM)),
        input_output_aliases={i: 2 + i for i in range(nb)},
        compiler_params=pltpu.CompilerParams(has_side_effects=_DATAFLOW),
    )(*[pltpu.with_memory_space_constraint(b, pltpu.HBM) for b in bufs], *after)
    return _Exchange(copies, ns, out[0], out[1], list(out[2:2 + nb]), out[-1])


def _exchange_wait(name, ex, after):
    nb, ns = len(ex.thru), ex.n_src

    def body(*refs):
        for cp in ex.copies(refs[:ns], refs[ns:nb], refs[nb], refs[nb + 1]):
            cp.wait_send()
            cp.wait_recv()

    out = pl.pallas_call(
        body, name=name, out_shape=tuple(pltpu.HBM(b.shape, b.dtype) for b in ex.thru),
        in_specs=[_HBM] * nb + [_SEM, _SEM] + [_ANY] * len(after), out_specs=tuple([_HBM] * nb),
        input_output_aliases={i: i for i in range(nb)},
        compiler_params=pltpu.CompilerParams(has_side_effects=_DATAFLOW),
    )(*ex.thru, ex.send_sems, ex.recv_sems, *after)
    return list(out[:ns]), list(out[ns:])


def _col_tile(r, c):
    return next(t for t in (1024, 512, 256, 128) if c % t == 0 and (r * t * 4 <= 2**20 or t == 128))


def _add_sibling(g4, recv, pos, name):
    _, _, r, c = g4.shape
    tc = _col_tile(r, c)

    def body(pos_ref, g_ref, r_ref, o16_ref, mine_ref):
        s = g_ref[0, 0] + r_ref[0]
        o16_ref[0] = s.astype(BF16)

        @pl.when(pl.program_id(1) == pos_ref[1])
        def _():
            mine_ref[...] = s

    slot = pl.BlockSpec((1, r, tc), lambda j, k, pos_ref: (k, 0, j))
    return pl.pallas_call(
        body, name=name,
        out_shape=[jax.ShapeDtypeStruct((4, r, c), BF16), jax.ShapeDtypeStruct((r, c), F32)],
        grid_spec=pltpu.PrefetchScalarGridSpec(
            num_scalar_prefetch=1, grid=(c // tc, 4),
            in_specs=[pl.BlockSpec((1, 1, r, tc), lambda j, k, pos_ref: (k, pos_ref[0], 0, j)), slot],
            out_specs=[slot, pl.BlockSpec((r, tc), lambda j, k, pos_ref: (0, j))]),
        compiler_params=pltpu.CompilerParams(dimension_semantics=("parallel", "arbitrary")),
    )(pos, *_in_hbm(g4, recv))


class _ReduceScatter:
    def __init__(self, tag, grads_t, pos):
        self.tag, self.pos, self.names = tag, pos, list(grads_t)
        g4s = [g.reshape(4, 2, g.size // (N_DEV * g.shape[-1]), g.shape[-1]) for g in grads_t.values()]
        lands = [lax.empty((4,) + g.shape[2:], F32) for g in g4s]
        self.ex = _exchange_start(f"rs_{tag}_sibling_start", _scatter_sibling_copies, g4s, lands, 4 * len(g4s))
        self.token = self.ex.token

    def start_chips(self, after):
        g4s, from_sibling = _exchange_wait(f"rs_{self.tag}_sibling_wait", self.ex, after)
        parts = [_add_sibling(g4, rv, self.pos, f"rs_add_sibling_{k}")
                 for k, g4, rv in zip(self.names, g4s, from_sibling)]
        self.mine = [mine for _, mine in parts]
        p16s = [p16 for p16, _ in parts]
        lands = [lax.empty((3,) + p.shape[1:], BF16) for p in p16s]
        self.ex = _exchange_start(f"rs_{self.tag}_chips_start", _scatter_chips_copies, p16s, lands, 3 * len(p16s))
        self.token = self.ex.token

    def finish(self, after):
        _, from_chips = _exchange_wait(f"rs_{self.tag}_chips_wait", self.ex, after)
        return dict(zip(self.names, zip(self.mine, from_chips)))


def _rope_tables():
    positions = np.arange(SEQ, dtype=np.float32)
    inv_freq = np.power(np.float32(ROPE_THETA), -np.arange(0, ROPE_DIM, 2, dtype=np.float32) / np.float32(ROPE_DIM))
    ang = (positions[:, None] * inv_freq[None, :]).astype(np.float32)
    cos, sin = np.cos(ang).astype(np.float32), np.sin(ang).astype(np.float32)
    ones = np.ones((SEQ, HEAD_DIM - ROPE_DIM), np.float32)
    zeros8 = np.zeros((SEQ, ROPE_HALF), np.float32)
    zeros = np.zeros((SEQ, HEAD_DIM - ROPE_DIM), np.float32)
    c_head = np.concatenate([cos, cos, ones], axis=1)
    s1_head = np.concatenate([-sin, zeros8, zeros], axis=1)
    s2_head = np.concatenate([zeros8, sin, zeros], axis=1)
    return tuple(jnp.asarray(np.concatenate([t, t], axis=1)) for t in (c_head, s1_head, s2_head))


def _rope_apply(x, c, s1, s2):
    w = x.shape[1]
    return x * c + pltpu.roll(x, w - ROPE_HALF, 1) * s1 + pltpu.roll(x, ROPE_HALF, 1) * s2


def _rope_apply_t(dy, c, s1, s2):
    w = dy.shape[1]
    return dy * c + pltpu.roll(dy * s1, ROPE_HALF, 1) + pltpu.roll(dy * s2, w - ROPE_HALF, 1)


def _dil_prev_limit(has_prev):
    return jnp.where(has_prev, 0, BLOCK)


def _dil_valid(limit):
    row = lax.broadcasted_iota(jnp.int32, (BLOCK, 2 * BLOCK), 0)
    col = lax.broadcasted_iota(jnp.int32, (BLOCK, 2 * BLOCK), 1)
    dist = col - row
    return jnp.logical_and(dist >= jnp.where(col < BLOCK, limit, -BLOCK), dist <= BLOCK)


def _upper_half():
    return lax.broadcasted_iota(jnp.int32, (1, LANES), 1) >= HEAD_DIM


def _dil_rows(n, d):
    per = N_BLOCKS // d
    r, lb = n // per, n % per

    def rows(b):
        start = b * (BLOCK * d) + r
        return pl.ds(pl.multiple_of(start, BLOCK), BLOCK) if d == 1 else pl.ds(start, BLOCK, stride=d)

    return rows(lb), rows(jnp.maximum(lb - 1, 0)), lb > 0


def _dil_specs(g):
    def col(base):
        return pl.BlockSpec((SEQ, LANES), lambda p: (0, base // LANES + 2 * g + p))

    table = pl.BlockSpec((SEQ, LANES), lambda p: (0, 0))
    return [col(COL_QA), col(COL_KA), col(COL_VA)], [table] * 3


def _store_columns(blocks, dproj_ref, cols, sem):
    copies = [pltpu.make_async_copy(b, dproj_ref.at[:, pl.ds(pl.multiple_of(c * LANES, LANES), LANES)], sem.at[i])
              for i, (b, c) in enumerate(zip(blocks, cols))]
    for cp in copies:
        cp.start()
    for cp in copies:
        cp.wait()


def _dil_fwd(g, proj, tables):
    d = DILATIONS[g]
    one_block = d == N_BLOCKS

    def body(q_ref, k_ref, v_ref, c_ref, s1_ref, s2_ref, o_ref, lse_ref):
        upper = _upper_half()

        def roped(ref, rows):
            return _rope_apply(ref[rows, :], c_ref[rows, :], s1_ref[rows, :], s2_ref[rows, :])

        def block(n, carry):
            rows, prev, has_prev = _dil_rows(n, d)
            qb = (roped(q_ref, rows) * QK_SCALE).astype(BF16)
            kw, vw = roped(k_ref, rows).astype(BF16), v_ref[rows, :].astype(BF16)
            if one_block:
                row = lax.broadcasted_iota(jnp.int32, (BLOCK, BLOCK), 0)
                valid = lax.broadcasted_iota(jnp.int32, (BLOCK, BLOCK), 1) <= row
            else:
                kw = jnp.concatenate([roped(k_ref, prev).astype(BF16), kw], axis=0)
                vw = jnp.concatenate([v_ref[prev, :].astype(BF16), vw], axis=0)
                valid = _dil_valid(_dil_prev_limit(has_prev))
            outs, lses = [], []
            for head_mask in (jnp.logical_not(upper), upper):
                s = jnp.where(valid, _dot_nt(qb, jnp.where(head_mask, kw, 0)), NEG_INF)
                m = jnp.max(s, axis=-1, keepdims=True)
                p = jnp.exp(s - m)
                den = jnp.sum(p, axis=-1, keepdims=True)
                outs.append(_dot_nn((p * (1.0 / den)).astype(BF16), jnp.where(head_mask, vw, 0)))
                lses.append(m + jnp.log(den))
            o_ref[rows, :] = outs[0] + outs[1]
            lse_ref[rows, :] = jnp.where(upper, lses[1], lses[0])
            return carry

        lax.fori_loop(0, N_BLOCKS, block, 0, unroll=8)

    qkv, tabs = _dil_specs(g)
    out = pl.BlockSpec((SEQ, LANES), lambda p: (0, p))
    return pl.pallas_call(
        body, name=f"dil_attn_fwd_{g}", grid=(2,), in_specs=qkv + tabs, out_specs=[out, out],
        out_shape=[jax.ShapeDtypeStruct((SEQ, DIL_OUT_WIDTH), F32)] * 2,
        compiler_params=pltpu.CompilerParams(dimension_semantics=("parallel",)),
    )(*_in_hbm(proj, proj, proj, *tables))


def _dil_bwd(g, proj, tables, do, lse, c, dproj, deps=()):
    d = DILATIONS[g]
    one_block = d == N_BLOCKS

    def body(q_ref, k_ref, v_ref, c_ref, s1_ref, s2_ref, do_ref, lse_ref, cc_ref, dproj_in, *rest):
        dproj_ref, dq_acc, dk_acc, dv_acc, dq_out, dk_out, dv_out, sem = rest[len(deps):]
        upper = _upper_half()
        dk_acc[...] = jnp.zeros_like(dk_acc)
        dv_acc[...] = jnp.zeros_like(dv_acc)

        def roped(ref, rows):
            return _rope_apply(ref[rows, :], c_ref[rows, :], s1_ref[rows, :], s2_ref[rows, :])

        def block(n, carry):
            rows, prev, has_prev = _dil_rows(n, d)
            qb = (roped(q_ref, rows) * QK_SCALE).astype(BF16)
            dob = do_ref[rows, :].astype(BF16)
            kw, vw = roped(k_ref, rows).astype(BF16), v_ref[rows, :].astype(BF16)
            if one_block:
                row = lax.broadcasted_iota(jnp.int32, (BLOCK, BLOCK), 0)
                valid = lax.broadcasted_iota(jnp.int32, (BLOCK, BLOCK), 1) <= row
            else:
                kw = jnp.concatenate([roped(k_ref, prev).astype(BF16), kw], axis=0)
                vw = jnp.concatenate([v_ref[prev, :].astype(BF16), vw], axis=0)
                valid = _dil_valid(_dil_prev_limit(has_prev))
            lse_t, c_t = lse_ref[rows, :], cc_ref[rows, :]
            dq, dk, dv = None, None, None
            for e, head_mask in enumerate((jnp.logical_not(upper), upper)):
                km, vm = jnp.where(head_mask, kw, 0), jnp.where(head_mask, vw, 0)
                lse_col = lse_t[:, e * HEAD_DIM:e * HEAD_DIM + 1]
                c_col = c_t[:, e * HEAD_DIM:e * HEAD_DIM + 1]
                p = jnp.where(valid, jnp.exp(_dot_nt(qb, km) - lse_col), 0.0)
                ds = (p * (_dot_nt(dob, vm) + c_col)).astype(BF16)
                parts = (_dot_nn(ds, km), _dot_tn(ds, jnp.where(head_mask, qb, 0)),
                         _dot_tn(p.astype(BF16), jnp.where(head_mask, dob, 0)))
                dq, dk, dv = parts if dq is None else (dq + parts[0], dk + parts[1], dv + parts[2])
            dq_acc[rows, :] = dq * QK_SCALE
            if one_block:
                dk_acc[rows, :] += dk
                dv_acc[rows, :] += dv
            else:
                dk_acc[prev, :] += dk[:BLOCK]
                dv_acc[prev, :] += dv[:BLOCK]
                dk_acc[rows, :] += dk[BLOCK:]
                dv_acc[rows, :] += dv[BLOCK:]
            return carry

        lax.fori_loop(0, N_BLOCKS, block, 0, unroll=8)
        tabs = (c_ref[...], s1_ref[...], s2_ref[...])
        dq_out[...] = _rope_apply_t(dq_acc[...], *tabs).astype(BF16)
        dk_out[...] = _rope_apply_t(dk_acc[...], *tabs).astype(BF16)
        dv_out[...] = dv_acc[...].astype(BF16)
        pair = 2 * g + pl.program_id(0)
        _store_columns((dq_out, dk_out, dv_out), dproj_ref,
                       [base // LANES + pair for base in (COL_QA, COL_KA, COL_VA)], sem)

    qkv, tabs = _dil_specs(g)
    tok = pl.BlockSpec((SEQ, LANES), lambda p: (0, p))
    return pl.pallas_call(
        body, name=f"dil_attn_bwd_{g}", grid=(2,),
        in_specs=qkv + tabs + [tok, tok, tok, _ANY] + [_ANY] * len(deps), out_specs=_ANY,
        out_shape=jax.ShapeDtypeStruct(dproj.shape, dproj.dtype),
        scratch_shapes=[pltpu.VMEM((SEQ, LANES), F32)] * 3 + [pltpu.VMEM((SEQ, LANES), BF16)] * 3
        + [pltpu.SemaphoreType.DMA((3,))],
        input_output_aliases={9: 0},
        compiler_params=pltpu.CompilerParams(dimension_semantics=("arbitrary",)),
    )(*_in_hbm(proj, proj, proj, *tables, do, lse, c, dproj), *deps)


def _group_weights(l0, l1, l2):
    m = jnp.maximum(jnp.maximum(l0, l1), l2)
    e0, e1, e2 = jnp.exp(l0 - m), jnp.exp(l1 - m), jnp.exp(l2 - m)
    tot = e0 + e1 + e2
    return e0 / tot, e1 / tot, e2 / tot


def _dil_combine(outs, lses, deps=()):
    def fn(o0, o1, o2, l0, l1, l2):
        w0, w1, w2 = _group_weights(l0, l1, l2)
        return w0 * o0 + w1 * o1 + w2 * o2

    w = DIL_OUT_WIDTH
    return _rowwise(fn, "dil_combine", SEQ, 512, [(a, w, 0) for a in list(outs) + list(lses)], [], [(w, F32)],
                    deps=deps)[0]


def _dil_combine_bwd(d_out, outs, lses, deps=()):
    w = DIL_OUT_WIDTH

    def fn(d, o0, o1, o2, l0, l1, l2):
        row = lax.broadcasted_iota(jnp.int32, (w, w), 0) // HEAD_DIM
        col = lax.broadcasted_iota(jnp.int32, (w, w), 1) // HEAD_DIM
        same_head = jnp.where(row == col, 1.0, 0.0).astype(BF16)
        ws = _group_weights(l0, l1, l2)
        dws = [_dot3_nn(d * og, same_head) for og in (o0, o1, o2)]
        mean = ws[0] * dws[0] + ws[1] * dws[1] + ws[2] * dws[2]
        return tuple(wg * d for wg in ws) + tuple(-wg * mean for wg in ws)

    res = _rowwise(fn, "dil_combine_bwd", SEQ, 256, [(a, w, 0) for a in [d_out] + list(outs) + list(lses)], [],
                   [(w, F32)] * 6, deps=deps)
    return res[:3], res[3:]


def _log1p(e):
    u = 1.0 + e
    return jnp.where(u == 1.0, e, jnp.log(u) * (e / (u - 1.0)))


def _fox_gate(proj, b_pad, deps=()):
    def body(f_ref, b_ref, *rest):
        o_ref = rest[-1]
        z = f_ref[...] + b_ref[...]
        logf = (jnp.minimum(z, 0.0) - _log1p(jnp.exp(-jnp.abs(z)))).T[:F_ROWS]
        row = lax.broadcasted_iota(jnp.int32, (BLOCK, BLOCK), 0)
        col = lax.broadcasted_iota(jnp.int32, (BLOCK, BLOCK), 1)
        before = jnp.where(row <= col, 1.0, 0.0).astype(BF16)
        carry = jnp.zeros((F_ROWS, 1), F32)
        for blk in range(N_BLOCKS):
            run = _dot3_nn(logf[:, blk * BLOCK:(blk + 1) * BLOCK], before) + carry
            o_ref[:, blk * BLOCK:(blk + 1) * BLOCK] = run
            carry = run[:, BLOCK - 1:BLOCK]

    return pl.pallas_call(
        body, name="fox_gate", grid=(1,),
        in_specs=[pl.BlockSpec((SEQ, LANES), lambda i: (0, COL_F // LANES)), pl.BlockSpec((1, LANES), lambda i: (0, 0))]
        + [_ANY] * len(deps),
        out_specs=pl.BlockSpec((F_ROWS, SEQ), lambda i: (0, 0)),
        out_shape=jax.ShapeDtypeStruct((F_ROWS, SEQ), F32),
    )(*_in_hbm(proj, b_pad), *deps)


def _fox_gate_bwd(d_cum, proj, b_pad, dproj):
    def body(d_ref, f_ref, b_ref, dproj_ref, dz_ref, db_ref):
        row = lax.broadcasted_iota(jnp.int32, (BLOCK, BLOCK), 0)
        col = lax.broadcasted_iota(jnp.int32, (BLOCK, BLOCK), 1)
        after = jnp.where(row >= col, 1.0, 0.0).astype(BF16)
        carry = jnp.zeros((F_ROWS, 1), F32)
        parts = [None] * N_BLOCKS
        for blk in reversed(range(N_BLOCKS)):
            run = _dot3_nn(d_ref[:, blk * BLOCK:(blk + 1) * BLOCK], after) + carry
            parts[blk] = run
            carry = run[:, 0:1]
        dlogf = jnp.concatenate(parts, axis=1)
        dlogf = jnp.concatenate([dlogf, jnp.zeros((LANES - F_ROWS, SEQ), F32)], axis=0).T
        dz = dlogf * _sigmoid(-(f_ref[...] + b_ref[...]))
        dz_ref[...] = dz.astype(BF16)
        db_ref[...] = jnp.sum(dz, axis=0, keepdims=True)

    f_cols = pl.BlockSpec((SEQ, LANES), lambda i: (0, COL_F // LANES))
    return pl.pallas_call(
        body, name="fox_gate_bwd", grid=(1,),
        in_specs=[pl.BlockSpec((F_ROWS, SEQ), lambda i: (0, 0)), f_cols, pl.BlockSpec((1, LANES), lambda i: (0, 0)), _ANY],
        out_specs=[f_cols, pl.BlockSpec((1, LANES), lambda i: (0, 0))],
        out_shape=[jax.ShapeDtypeStruct(dproj.shape, dproj.dtype), jax.ShapeDtypeStruct((1, LANES), F32)],
        input_output_aliases={3: 0},
    )(*_in_hbm(d_cum, proj, b_pad, dproj))


FOX_TILE = 256
FOX_TILES = SEQ // FOX_TILE


def _row_to_col(row):
    n = row.shape[1]
    eye = lax.broadcasted_iota(jnp.int32, (n, n), 0) == lax.broadcasted_iota(jnp.int32, (n, n), 1)
    return jnp.sum(jnp.where(eye, row, 0.0), axis=1, keepdims=True)


def _fox_scores(q_tile, km, f_row, i):
    t = FOX_TILE
    ext = (i + 1) * t
    f_q = _row_to_col(f_row[:, i * t:(i + 1) * t])
    s = _dot_nt(q_tile, km[:ext]) + (f_q - f_row[:, :ext])
    row = lax.broadcasted_iota(jnp.int32, (t, ext), 0) + i * t
    col = lax.broadcasted_iota(jnp.int32, (t, ext), 1)
    return s, col <= row


def _fox_specs():
    qkv = [pl.BlockSpec((SEQ, LANES), lambda p, base=base: (0, base // LANES + p)) for base in (COL_QB, COL_KB, COL_VB)]
    return qkv, pl.BlockSpec((F_ROWS, SEQ), lambda p: (0, 0))


def _fox_fwd(proj, f_rows):
    t = FOX_TILE

    def body(q_ref, k_ref, v_ref, f_ref, o_ref, lse_ref):
        pair = pl.program_id(0)
        upper = _upper_half()
        masks = (jnp.logical_not(upper), upper)
        k16, v16 = k_ref[...].astype(BF16), v_ref[...].astype(BF16)
        kms = [jnp.where(hm, k16, 0) for hm in masks]
        vms = [jnp.where(hm, v16, 0) for hm in masks]
        f_row = [f_ref[pl.ds(2 * pair + e, 1), :] for e in range(2)]
        for i in range(FOX_TILES):
            q_tile = (q_ref[i * t:(i + 1) * t, :] * QK_SCALE).astype(BF16)
            outs, lses = [], []
            for e in range(2):
                s, causal = _fox_scores(q_tile, kms[e], f_row[e], i)
                s = jnp.where(causal, s, NEG_INF)
                m = jnp.max(s, axis=-1, keepdims=True)
                p = jnp.exp(s - m)
                den = jnp.sum(p, axis=-1, keepdims=True)
                outs.append(_dot_nn((p * (1.0 / den)).astype(BF16), vms[e][:(i + 1) * t]))
                lses.append(m + jnp.log(den))
            o_ref[i * t:(i + 1) * t, :] = outs[0] + outs[1]
            lse_ref[i * t:(i + 1) * t, :] = jnp.where(upper, lses[1], lses[0])

    qkv, f_spec = _fox_specs()
    tok = pl.BlockSpec((SEQ, LANES), lambda p: (0, p))
    return pl.pallas_call(
        body, name="fox_attn_fwd", grid=(FOX_WIDTH // LANES,),
        in_specs=qkv + [f_spec], out_specs=[tok, tok],
        out_shape=[jax.ShapeDtypeStruct((SEQ, FOX_WIDTH), F32)] * 2,
        compiler_params=pltpu.CompilerParams(
            dimension_semantics=("parallel",), vmem_limit_bytes=_vmem_limit(8 * t * SEQ * 4)),
    )(*_in_hbm(proj, proj, proj, f_rows))


def _fox_bwd(proj, do, lse, f_rows, dproj):
    t = FOX_TILE

    def body(q_ref, k_ref, v_ref, f_ref, do_ref, lse_ref, dproj_in, dproj_ref, df_ref, dk_acc, dv_acc,
             dq_out, dk_out, dv_out, sem):
        pair = pl.program_id(0)
        upper = _upper_half()
        masks = (jnp.logical_not(upper), upper)
        k16, v16 = k_ref[...].astype(BF16), v_ref[...].astype(BF16)
        kms = [jnp.where(hm, k16, 0) for hm in masks]
        vms = [jnp.where(hm, v16, 0) for hm in masks]
        f_row = [f_ref[pl.ds(2 * pair + e, 1), :] for e in range(2)]
        dk_acc[...] = jnp.zeros_like(dk_acc)
        dv_acc[...] = jnp.zeros_like(dv_acc)
        df_ref[...] = jnp.zeros_like(df_ref)
        for i in range(FOX_TILES):
            ext = (i + 1) * t
            q_tile = (q_ref[i * t:(i + 1) * t, :] * QK_SCALE).astype(BF16)
            do_tile = do_ref[i * t:(i + 1) * t, :]
            lse_t = lse_ref[i * t:(i + 1) * t, :]
            dq = None
            for e in range(2):
                s, causal = _fox_scores(q_tile, kms[e], f_row[e], i)
                p = jnp.where(causal, jnp.exp(s - lse_t[:, e * HEAD_DIM:e * HEAD_DIM + 1]), 0.0)
                dp = _dot_nt(do_tile, vms[e][:ext])
                ds = p * (dp - jnp.sum(p * dp, axis=-1, keepdims=True))
                df_ref[0, e:e + 1, :ext] -= jnp.sum(ds, axis=0, keepdims=True)
                ds = ds.astype(BF16)
                part = _dot_nn(ds, kms[e][:ext])
                dq = part if dq is None else dq + part
                dk_acc[:ext, :] += _dot_tn(ds, jnp.where(masks[e], q_tile, 0))
                dv_acc[:ext, :] += _dot_tn(p.astype(BF16), jnp.where(masks[e], do_tile, 0))
            dq_out[i * t:(i + 1) * t, :] = (dq * QK_SCALE).astype(BF16)
        dk_out[...] = dk_acc[...].astype(BF16)
        dv_out[...] = dv_acc[...].astype(BF16)
        _store_columns((dq_out, dk_out, dv_out), dproj_ref, [base // LANES + pair for base in (COL_QB, COL_KB, COL_VB)],
                       sem)

    qkv, f_spec = _fox_specs()
    tok = pl.BlockSpec((SEQ, LANES), lambda p: (0, p))
    return pl.pallas_call(
        body, name="fox_attn_bwd", grid=(FOX_WIDTH // LANES,),
        in_specs=qkv + [f_spec, tok, tok, _ANY],
        out_specs=[_ANY, pl.BlockSpec((1, SUBLANES, SEQ), lambda p: (p, 0, 0))],
        out_shape=[jax.ShapeDtypeStruct(dproj.shape, dproj.dtype),
                   jax.ShapeDtypeStruct((FOX_WIDTH // LANES, SUBLANES, SEQ), F32)],
        scratch_shapes=[pltpu.VMEM((SEQ, LANES), F32)] * 2 + [pltpu.VMEM((SEQ, LANES), BF16)] * 3
        + [pltpu.SemaphoreType.DMA((3,))],
        input_output_aliases={6: 0},
        compiler_params=pltpu.CompilerParams(
            dimension_semantics=("arbitrary",), vmem_limit_bytes=_vmem_limit(10 * t * SEQ * 4)),
    )(*_in_hbm(proj, proj, proj, f_rows, do, lse, dproj))


MIX_TILE = 256


def _mix_out(out_a, out_b, proj, x, wt_pa, wt_pb, w_out, g_post, g_ffn_pre):
    tm = MIX_TILE

    def body(a_ref, b_ref, ga_ref, gb_ref, x_ref, wpa_ref, wpb_ref, wo_ref, g2_ref, g3_ref,
             merged_ref, mix_ref, x1_ref, h2_ref):
        ya = _dot_nn(a_ref[...].astype(BF16), wpa_ref[...])
        yb = _dot_nn(b_ref[...].astype(BF16), wpb_ref[...])
        merged = (_sigmoid(ga_ref[...]) * ya + _sigmoid(gb_ref[...]) * yb).astype(BF16)
        merged_ref[...] = merged
        mix = _dot_nn(merged, wo_ref[...])
        mix_ref[...] = mix
        x1 = x_ref[...] + mix * _rms_scale(mix) * g2_ref[...]
        x1_ref[...] = x1
        h2_ref[...] = (x1 * _rms_scale(x1) * g3_ref[...]).astype(BF16)

    def rows(w, cb=0):
        return pl.BlockSpec((tm, w), lambda i, cb=cb: (i, cb))

    def whole(a):
        return pl.BlockSpec(a.shape, lambda i: (0, 0))

    d = D_MODEL
    blk = _nbytes((tm, d), F32) * 6 + sum(_nbytes(a.shape, BF16) for a in (wt_pa, wt_pb, w_out))
    return pl.pallas_call(
        body, name="mix_out", grid=(SEQ // tm,),
        in_specs=[rows(DIL_OUT_WIDTH), rows(FOX_WIDTH), rows(d, COL_GA // d), rows(d, COL_GB // d), rows(d),
                  whole(wt_pa), whole(wt_pb), whole(w_out), whole(g_post), whole(g_ffn_pre)],
        out_specs=[rows(d)] * 4,
        out_shape=[jax.ShapeDtypeStruct((SEQ, d), dt) for dt in (BF16, F32, F32, BF16)],
        compiler_params=pltpu.CompilerParams(dimension_semantics=("parallel",), vmem_limit_bytes=_vmem_limit(blk)),
    )(out_a, out_b, proj, proj, x, wt_pa, wt_pb, w_out, g_post, g_ffn_pre)


def _mix_out_bwd(dmix, out_a, out_b, proj, wt_pa, wt_pb, w_out, deps=()):
    tm = MIX_TILE

    def body(dm_ref, a_ref, b_ref, ga_ref, gb_ref, wpa_ref, wpb_ref, wo_ref, *rest):
        dproj_ref, dya_ref, dyb_ref, da_ref, db_ref = rest[len(deps):]
        dmerged = _dot_nt(dm_ref[...], wo_ref[...])
        ya = _dot_nn(a_ref[...].astype(BF16), wpa_ref[...])
        yb = _dot_nn(b_ref[...].astype(BF16), wpb_ref[...])
        sa, sb = _sigmoid(ga_ref[...]), _sigmoid(gb_ref[...])
        dproj_ref[:, COL_GA:COL_GA + D_MODEL] = (dmerged * ya * (sa * (1.0 - sa))).astype(BF16)
        dproj_ref[:, COL_GB:COL_GB + D_MODEL] = (dmerged * yb * (sb * (1.0 - sb))).astype(BF16)
        dproj_ref[:, COL_GB + D_MODEL:] = jnp.zeros((tm, COL_QA - COL_GB - D_MODEL), BF16)
        dya = (dmerged * sa).astype(BF16)
        dyb = (dmerged * sb).astype(BF16)
        dya_ref[...] = dya
        dyb_ref[...] = dyb
        da_ref[...] = _dot_nt(dya, wpa_ref[...])
        db_ref[...] = _dot_nt(dyb, wpb_ref[...]).astype(BF16)

    def rows(w, cb=0):
        return pl.BlockSpec((tm, w), lambda i, cb=cb: (i, cb))

    def whole(a):
        return pl.BlockSpec(a.shape, lambda i: (0, 0))

    d = D_MODEL
    blk = _nbytes((tm, d), F32) * 8 + sum(_nbytes(a.shape, BF16) for a in (wt_pa, wt_pb, w_out))
    return pl.pallas_call(
        body, name="mix_out_bwd", grid=(SEQ // tm,),
        in_specs=[rows(d), rows(DIL_OUT_WIDTH), rows(FOX_WIDTH), rows(d, COL_GA // d), rows(d, COL_GB // d),
                  whole(wt_pa), whole(wt_pb), whole(w_out)] + [_ANY] * len(deps),
        out_specs=[rows(COL_QA)] + [rows(d)] * 2 + [rows(DIL_OUT_WIDTH), rows(FOX_WIDTH)],
        out_shape=[jax.ShapeDtypeStruct((SEQ, PROJ_COLS), BF16)] + [jax.ShapeDtypeStruct((SEQ, d), BF16)] * 2
        + [jax.ShapeDtypeStruct((SEQ, DIL_OUT_WIDTH), F32), jax.ShapeDtypeStruct((SEQ, FOX_WIDTH), BF16)],
        compiler_params=pltpu.CompilerParams(dimension_semantics=("parallel",), vmem_limit_bytes=_vmem_limit(blk)),
    )(dmix, out_a, out_b, proj, proj, wt_pa, wt_pb, w_out, *deps)


FFN_TM, FFN_TN = 2048, 256


def _ffn_up(h2, wt_gate, wt_up):
    tm, tn = FFN_TM, FFN_TN

    def body(h_ref, wg_ref, wu_ref, gate_ref, up_ref, act_ref):
        gate = _dot_nt(h_ref[...], wg_ref[...])
        up = _dot_nt(h_ref[...], wu_ref[...])
        gate_ref[...] = gate
        up_ref[...] = up
        act_ref[...] = (gate * _sigmoid(gate) * up).astype(BF16)

    tile = pl.BlockSpec((tm, tn), lambda i, j: (i, j))
    w_spec = pl.BlockSpec((tn, D_MODEL), lambda i, j: (j, 0))
    return pl.pallas_call(
        body, name="ffn_up", grid=(SEQ // tm, D_FF // tn),
        in_specs=[pl.BlockSpec((tm, D_MODEL), lambda i, j: (i, 0)), w_spec, w_spec],
        out_specs=[tile, tile, tile],
        out_shape=[jax.ShapeDtypeStruct((SEQ, D_FF), dt) for dt in (F32, F32, BF16)],
        compiler_params=pltpu.CompilerParams(
            dimension_semantics=("parallel", "parallel"), vmem_limit_bytes=_vmem_limit(8 * 2**20)),
    )(h2, wt_gate, wt_up)


def _ffn_act_bwd(dff, w_down, gate, up):
    tm, tn = FFN_TM, FFN_TN

    def body(d_ref, wd_ref, gate_ref, up_ref, dgate_ref, dup_ref):
        dact = _dot_nt(d_ref[...], wd_ref[...])
        gate = gate_ref[...]
        sg = _sigmoid(gate)
        dgate_ref[...] = (dact * up_ref[...] * (sg * (1.0 + gate * (1.0 - sg)))).astype(BF16)
        dup_ref[...] = (dact * (gate * sg)).astype(BF16)

    tile = pl.BlockSpec((tm, tn), lambda i, j: (i, j))
    return pl.pallas_call(
        body, name="ffn_act_bwd", grid=(SEQ // tm, D_FF // tn),
        in_specs=[pl.BlockSpec((tm, D_MODEL), lambda i, j: (i, 0)), pl.BlockSpec((tn, D_MODEL), lambda i, j: (j, 0)),
                  tile, tile],
        out_specs=[tile, tile],
        out_shape=[jax.ShapeDtypeStruct((SEQ, D_FF), BF16)] * 2,
        compiler_params=pltpu.CompilerParams(
            dimension_semantics=("parallel", "parallel"), vmem_limit_bytes=_vmem_limit(8 * 2**20)),
    )(dff, w_down, gate, up)


EPILOGUE_TM = 512


def _loss_head(act, w_down, x1, target, g_post):
    def fn(ff, x1, tgt, g):
        r = _rms_scale(ff)
        nrm = ff * r
        err = (x1 + nrm * g) - tgt
        loss = 0.5 * jnp.sum(jnp.mean(err * err, axis=-1, keepdims=True), axis=0, keepdims=True)
        dy = err * (1.0 / D_MODEL)
        u = dy * g
        dff = r * u - ff * (r * r * r) * jnp.mean(u * ff, axis=-1, keepdims=True)
        return dy, dff, jnp.broadcast_to(loss, (1, LANES)), jnp.sum(dy * nrm, axis=0, keepdims=True)

    d = D_MODEL
    return _matmul_rowwise([(act, w_down)], fn, "ffn_down_loss", EPILOGUE_TM, [(x1, d, 0), (target, d, 0)], [g_post],
                           [(d, F32), (d, BF16)], [LANES, d])


def _post_ffn_bwd(dgate, wt_gate, dup, wt_up, x1, dy, mix, g_ffn_pre, g_mix_post, deps=()):
    def fn(dh2, x1, dy, mix, g3, g2):
        dx, dg3 = _rms_bwd(x1, dh2, g3)
        dx1 = dy + dx
        dmix, dg2 = _rms_bwd(mix, dx1, g2)
        return dx1, dmix, dg3, dg2

    d = D_MODEL
    return _matmul_rowwise([(dgate, wt_gate), (dup, wt_up)], fn, "ffn_up_bwd", EPILOGUE_TM,
                           [(x1, d, 0), (dy, d, 0), (mix, d, 0)], [g_ffn_pre, g_mix_post],
                           [(d, F32), (d, BF16)], [d, d], deps=deps)


def _input_bwd(dproj, wt_r, x, dx1, g_pre, deps=()):
    def fn(dh, x, dx1, g):
        dx, dg = _rms_bwd(x, dh, g)
        return dx1 + dx, dg

    d = D_MODEL
    return _matmul_rowwise([(dproj, wt_r)], fn, "in_proj_bwd", EPILOGUE_TM, [(x, d, 0), (dx1, d, 0)], [g_pre],
                           [(d, F32)], [d], deps=deps)


def _adam_math(w, g, m, v):
    m = ADAM_B1 * m + (1.0 - ADAM_B1) * g
    v = ADAM_B2 * v + (1.0 - ADAM_B2) * (g * g)
    m_hat = m / (1.0 - ADAM_B1 ** ADAM_STEP)
    v_hat = v / (1.0 - ADAM_B2 ** ADAM_STEP)
    delta = -ADAM_LR * (m_hat / (jnp.sqrt(v_hat) + ADAM_EPS) + ADAM_WD * w)
    return delta, m, v


def _adam(w, mine, recv, m, v, name):
    r, c = w.shape
    tc = _col_tile(r, c)

    def body(w_ref, p_ref, r_ref, m_ref, v_ref, g_ref, d_ref, nm_ref, nv_ref):
        g = ((p_ref[...] + r_ref[0].astype(F32)) + r_ref[1].astype(F32)) + r_ref[2].astype(F32)
        g_ref[...] = g
        d_ref[...], nm_ref[...], nv_ref[...] = _adam_math(w_ref[...], g, m_ref[...], v_ref[...])

    spec = pl.BlockSpec((r, tc), lambda j: (0, j))
    return pl.pallas_call(
        body, name=name, grid=(c // tc,),
        in_specs=[spec, spec, pl.BlockSpec((3, r, tc), lambda j: (0, 0, j)), spec, spec], out_specs=[spec] * 4,
        out_shape=[jax.ShapeDtypeStruct((r, c), F32)] * 4,
        compiler_params=pltpu.CompilerParams(dimension_semantics=("parallel",)),
    )(*_in_hbm(w, mine, recv, m, v))


def _adam_small(gathered, ws, ms, vs, loss_parts):
    n = len(ws)

    def body(*refs):
        outs = refs[4 * n + 1:]
        loss = refs[4 * n][0]
        for dev in range(1, N_DEV):
            loss = loss + refs[4 * n][dev]
        outs[4 * n][...] = loss
        for i in range(n):
            ga_ref, w_ref, m_ref, v_ref = (refs[j * n + i] for j in range(4))
            g = ga_ref[0]
            for dev in range(1, N_DEV):
                g = g + ga_ref[dev]
            g = g[:, :w_ref.shape[1]]
            outs[4 * i][...] = g
            outs[4 * i + 1][...], outs[4 * i + 2][...], outs[4 * i + 3][...] = _adam_math(
                w_ref[...], g, m_ref[...], v_ref[...])

    out_shape = [jax.ShapeDtypeStruct(w.shape, F32) for w in ws for _ in range(4)]
    out_shape.append(jax.ShapeDtypeStruct((1, LANES), F32))
    out = pl.pallas_call(body, name="adam_small", out_shape=out_shape)(*gathered, *ws, *ms, *vs, loss_parts)
    return [out[4 * i:4 * i + 4] for i in range(n)], out[4 * n]


_PROJ_SEGMENTS = ((3848, 5896), (None, COL_QA - 2 * D_MODEL), (0, 3840), (3840, 3848), (None, PROJ_COLS - COL_F - 8))


def _proj_weight_t(gathered):
    w = gathered.reshape(IN_COLS, D_MODEL)
    return jnp.concatenate([jnp.zeros((hi, D_MODEL), w.dtype) if lo is None else w[lo:hi] for lo, hi in _PROJ_SEGMENTS],
                           axis=0)


def _proj_weight_grad_slots(dwt_r):
    starts, at = [], 0
    for lo, hi in _PROJ_SEGMENTS:
        if lo is not None:
            starts.append((lo, hi, at))
        at += hi if lo is None else hi - lo
    slots = []
    for dev in range(N_DEV):
        pieces, lo, end = [], dev * IN_SHARD, (dev + 1) * IN_SHARD
        for seg_lo, seg_hi, seg_at in sorted(starts):
            a, b = max(lo, seg_lo), min(end, seg_hi)
            if a < b:
                pieces.append(dwt_r[seg_at + a - seg_lo:seg_at + b - seg_lo])
        slots.append(pieces[0] if len(pieces) == 1 else jnp.concatenate(pieces, axis=0))
    return jnp.stack(slots)


def kernel(x, w_in, w_proj_a, w_proj_b, w_out, b_forget, w_ffn_gate, w_ffn_up, w_ffn_down, norm_mix_pre, norm_mix_post, norm_ffn_pre, norm_ffn_post, loss_target, m_w_in, m_w_proj_a, m_w_proj_b, m_w_out, m_b_forget, m_w_ffn_gate, m_w_ffn_up, m_w_ffn_down, m_norm_mix_pre, m_norm_mix_post, m_norm_ffn_pre, m_norm_ffn_post, v_w_in, v_w_proj_a, v_w_proj_b, v_w_out, v_b_forget, v_w_ffn_gate, v_w_ffn_up, v_w_ffn_down, v_norm_mix_pre, v_norm_mix_post, v_norm_ffn_pre, v_norm_ffn_post):
    d = D_MODEL
    names = ("w_in", "w_proj_a", "w_proj_b", "w_out", "w_ffn_gate", "w_ffn_up", "w_ffn_down")
    col_sharded = ("w_in", "w_ffn_gate", "w_ffn_up")

    def row_shards(arrs):
        return {k: (a[0].T if k in col_sharded else a[0]) for k, a in zip(names, arrs)}

    shards = row_shards((w_in, w_proj_a, w_proj_b, w_out, w_ffn_gate, w_ffn_up, w_ffn_down))
    moments_m = row_shards((m_w_in, m_w_proj_a, m_w_proj_b, m_w_out, m_w_ffn_gate, m_w_ffn_up, m_w_ffn_down))
    moments_v = row_shards((v_w_in, v_w_proj_a, v_w_proj_b, v_w_out, v_w_ffn_gate, v_w_ffn_up, v_w_ffn_down))
    pos = jnp.stack([lax.axis_index("c"), 2 * lax.axis_index("x") + lax.axis_index("y")]).astype(jnp.int32)
    x2, target = x[0], loss_target[0]

    me = 4 * lax.axis_index("x") + 2 * lax.axis_index("y") + lax.axis_index("c")
    mid_names, ffn_names = names[1:4], names[4:]
    first_names, later_names = names[:1], names[1:]
    shards16 = {k: shards[k].astype(BF16) for k in names}

    def landing(k):
        return lax.dynamic_update_slice(lax.empty((N_DEV,) + shards[k].shape, BF16), shards16[k][None], (me, 0, 0))

    ag_first = _exchange_start("ag_first_chips_start", _gather_chips_copies, [shards16[k] for k in first_names],
                               [landing(k) for k in first_names], 3 * len(first_names))
    h = _rowwise(lambda xb, g: xb * _rms_scale(xb) * g, "norm_mix_pre", SEQ, 256, [(x2, d, 0)], [norm_mix_pre],
                 [(d, BF16)], deps=[ag_first.token])[0]
    _, lands = _exchange_wait("ag_first_chips_wait", ag_first, [h])
    ag_first = _exchange_start("ag_first_sibling_start", _gather_sibling_copies, [], lands, 4 * len(first_names))
    ag_later = _exchange_start("ag_later_chips_start", _gather_chips_copies, [shards16[k] for k in later_names],
                               [landing(k) for k in later_names], 3 * len(later_names), after=[ag_first.token])
    gathered = dict(zip(first_names, _exchange_wait("ag_first_sibling_wait", ag_first, [ag_later.token])[1]))
    wt_r = _proj_weight_t(gathered["w_in"])

    proj = _matmul([(h, wt_r)], "nt", F32, "in_proj", 1024, 896, 1024)
    tables = _rope_tables()
    o_dil, lse_dil = zip(*[_dil_fwd(g, proj, tables) for g in range(len(DILATIONS))])
    out_a = _dil_combine(o_dil, lse_dil)
    _, lands = _exchange_wait("ag_later_chips_wait", ag_later, [out_a])
    ag_later = _exchange_start("ag_later_sibling_start", _gather_sibling_copies, [], lands, 4 * len(later_names))

    b_pad = jnp.pad(b_forget, ((0, 0), (0, LANES - N_FOX_HEADS)))
    f_rows = _fox_gate(proj, b_pad, deps=[ag_later.token])
    out_b, lse_fox = _fox_fwd(proj, f_rows)

    gathered = dict(zip(later_names, _exchange_wait("ag_later_sibling_wait", ag_later, [out_b])[1]))
    wt_pa = gathered["w_proj_a"].transpose(1, 0, 2).reshape(DIL_OUT_WIDTH, d)
    wt_pb = gathered["w_proj_b"].transpose(1, 0, 2).reshape(FOX_WIDTH, d)
    w_o = gathered["w_out"].reshape(d, d)
    wt_g = gathered["w_ffn_gate"].reshape(D_FF, d)
    wt_u = gathered["w_ffn_up"].reshape(D_FF, d)
    w_d = gathered["w_ffn_down"].reshape(D_FF, d)
    merged, mix, x1, h2 = _mix_out(out_a, out_b, proj, x2, wt_pa, wt_pb, w_o, norm_mix_post, norm_ffn_pre)

    gate, up, act = _ffn_up(h2, wt_g, wt_u)
    dy, dff, loss_part, dg_ffn_post = _loss_head(act, w_d, x1, target, norm_ffn_post)

    dgate, dup = _ffn_act_bwd(dff, w_d, gate, up)
    grads_t = {}
    grads_t["w_ffn_down"] = _matmul([(act, dff)], "tn", F32, "grad_w_ffn_down", 1408, 1024, 2048)
    grads_t["w_ffn_gate"] = _matmul([(dgate, h2)], "tn", F32, "grad_w_ffn_gate", 1408, 1024, 2048)
    grads_t["w_ffn_up"] = _matmul([(dup, h2)], "tn", F32, "grad_w_ffn_up", 1408, 1024, 2048)
    rs_ffn = _ReduceScatter("ffn", {k: grads_t[k] for k in ffn_names}, pos)
    dx1, dmix, dg_ffn_pre, dg_mix_post = _post_ffn_bwd(dgate, wt_g, dup, wt_u, x1, dy, mix, norm_ffn_pre, norm_mix_post,
                                                       deps=[rs_ffn.token])
    rs_ffn.start_chips([dmix])

    dproj, dya, dyb, d_out_a, d_out_b = _mix_out_bwd(dmix, out_a, out_b, proj, wt_pa, wt_pb, w_o, deps=[rs_ffn.token])
    grads_t["w_out"] = _matmul([(merged, dmix)], "tn", F32, "grad_w_out", 1024, 1024, 1024)
    def column_slots(g):
        return g.reshape(g.shape[0], N_DEV, LANES).transpose(1, 0, 2)

    grads_t["w_proj_a"] = column_slots(_matmul([(out_a, dya)], "tn", F32, "grad_w_proj_a", DIL_OUT_WIDTH, 1024, SEQ))
    grads_t["w_proj_b"] = column_slots(_matmul([(out_b, dyb)], "tn", F32, "grad_w_proj_b", FOX_WIDTH, 1024, SEQ))
    rs_mid = _ReduceScatter("mid", {k: grads_t[k] for k in mid_names}, pos)

    do_dil, c_dil = _dil_combine_bwd(d_out_a, o_dil, lse_dil, deps=[rs_mid.token])
    rs_mid.start_chips([c_dil[0]])
    dproj, d_cum = _fox_bwd(proj, d_out_b, lse_fox, f_rows, dproj)
    d_cum_rows = jnp.pad(d_cum[:, :2].reshape(N_FOX_HEADS, SEQ), ((0, F_ROWS - N_FOX_HEADS), (0, 0)))
    dproj, db_part = _fox_gate_bwd(d_cum_rows, proj, b_pad, dproj)
    for g in range(len(DILATIONS)):
        dproj = _dil_bwd(g, proj, tables, do_dil[g], lse_dil[g], c_dil[g], dproj, deps=[rs_mid.token])

    dwt_r = _matmul([(dproj, h)], "tn", F32, "grad_w_in", 896, 1024, 2048)
    rs_in = _ReduceScatter("in", {"w_in": _proj_weight_grad_slots(dwt_r)}, pos)
    def finish(rs, after):
        return {k: _adam(shards[k], mine, recv, moments_m[k], moments_v[k], "adam_" + k)
                for k, (mine, recv) in rs.finish(after).items()}

    done = finish(rs_ffn, [rs_in.token])
    rs_in.start_chips([done[k][0] for k in ffn_names])
    grad_x, dg_mix_pre = _input_bwd(dproj, wt_r, x2, dx1, norm_mix_pre, deps=[rs_in.token])
    done.update(finish(rs_mid, [grad_x]))

    small_all = _all_gather([dg_mix_pre, dg_mix_post, dg_ffn_pre, dg_ffn_post, db_part, loss_part],
                            "small_grads_all_gather", deps=[done[k][0] for k in mid_names])
    small, loss = _adam_small(small_all[:5], [norm_mix_pre, norm_mix_post, norm_ffn_pre, norm_ffn_post, b_forget],
                              [m_norm_mix_pre, m_norm_mix_post, m_norm_ffn_pre, m_norm_ffn_post, m_b_forget],
                              [v_norm_mix_pre, v_norm_mix_post, v_norm_ffn_pre, v_norm_ffn_post, v_b_forget],
                              small_all[5])

    done.update(finish(rs_in, [small[0][0]]))

    def leaves(i):
        def nat(k):
            a = done[k][i]
            return (a.T if k in col_sharded else a)[None]

        return [nat("w_in"), nat("w_proj_a"), nat("w_proj_b"), nat("w_out"), small[4][i],
                nat("w_ffn_gate"), nat("w_ffn_up"), nat("w_ffn_down"), *[small[r][i] for r in range(4)]]

    return (loss[0, 0], grad_x[None], *leaves(0), *leaves(1), *leaves(2), *leaves(3))
```

```python
import functools
import math

import jax
import jax.numpy as jnp
import numpy as np
from jax import lax
from jax.experimental import pallas as pl
from jax.experimental.pallas import tpu as pltpu

F32 = jnp.float32
BF16 = jnp.bfloat16
MESH = pl.DeviceIdType.MESH

D_MODEL = 1024
SEQ = 2048
HEAD_DIM = 64
BLOCK = 128
N_BLOCKS = SEQ // BLOCK
DILATIONS = (1, 4, 16)
N_FOX_HEADS = 8
DIL_WIDTH = 768
DIL_OUT_WIDTH = 256
FOX_WIDTH = 512
D_FF = 2816
ROPE_THETA = 500000.0
ROPE_DIM = HEAD_DIM // 4
ROPE_HALF = ROPE_DIM // 2
EPS = 1e-6
NEG_INF = -1e30
QK_SCALE = 1.0 / math.sqrt(HEAD_DIM)
IN_COLS = 5896
N_DEV = 8
IN_SHARD = IN_COLS // N_DEV

ADAM_LR = 0.001
ADAM_B1 = 0.9
ADAM_B2 = 0.999
ADAM_EPS = 1e-08
ADAM_WD = 0.01
ADAM_STEP = 10

V7X_VMEM_BYTES = 64 * 2**20
LANES = 128
SUBLANES = 8

PROJ_COLS = 6272
COL_GA, COL_GB = 0, 1024
COL_QA, COL_KA, COL_VA = 2304, 3072, 3840
COL_QB, COL_KB, COL_VB = 4608, 5120, 5632
COL_F = 6144
F_ROWS = 16


def _vmem_limit(block_bytes):
    want = 2 * block_bytes + 16 * 2**20
    return int(min(max(want, 32 * 2**20), V7X_VMEM_BYTES - 8 * 2**20))


def _nbytes(shape, dtype):
    return math.prod(shape) * jnp.dtype(dtype).itemsize


def _in_hbm(*arrays):
    return [pltpu.with_memory_space_constraint(a, pltpu.HBM) for a in arrays]


def _dot(a, b, dims):
    return lax.dot_general(a, b, (dims, ((), ())), preferred_element_type=F32)


def _dot_nn(a, b):
    return _dot(a, b, ((1,), (0,)))


def _dot_nt(a, b):
    return _dot(a, b, ((1,), (1,)))


def _dot_tn(a, b):
    return _dot(a, b, ((0,), (0,)))


def _sigmoid(z):
    return 1.0 / (1.0 + jnp.exp(-z))


def _split3(x):
    hi = x.astype(BF16)
    r1 = x - hi.astype(F32)
    mid = r1.astype(BF16)
    lo = (r1 - mid.astype(F32)).astype(BF16)
    return hi, mid, lo


def _dot3_nn(x, ones_matrix):
    hi, mid, lo = _split3(x)
    return (_dot_nn(hi, ones_matrix) + _dot_nn(mid, ones_matrix)) + _dot_nn(lo, ones_matrix)


def _rowwise(fn, name, n_rows, tm, row_ins, bcast_ins, row_outs, acc_outs=(), deps=()):
    n_in = len(row_ins) + len(bcast_ins)
    n_ro = len(row_outs)

    def body(*refs):
        res = fn(*[r[...] for r in refs[:n_in]])
        if not isinstance(res, (tuple, list)):
            res = (res,)
        outs = refs[n_in + len(deps):]
        for r, o in zip(res[:n_ro], outs[:n_ro]):
            o[...] = r.astype(o.dtype)
        first = pl.program_id(0) == 0
        for r, o in zip(res[n_ro:], outs[n_ro:]):
            _accumulate(o, r, first)

    in_specs = [pl.BlockSpec((tm, w), lambda i, cb=cb: (i, cb)) for _, w, cb in row_ins]
    in_specs += [pl.BlockSpec(a.shape, lambda i: (0, 0)) for a in bcast_ins]
    in_specs += [pl.BlockSpec(memory_space=pl.ANY)] * len(deps)
    out_specs = [pl.BlockSpec((tm, w), lambda i: (i, 0)) for w, _ in row_outs]
    out_specs += [pl.BlockSpec((1, w), lambda i: (0, 0)) for w in acc_outs]
    out_shape = [jax.ShapeDtypeStruct((n_rows, w), dt) for w, dt in row_outs]
    out_shape += [jax.ShapeDtypeStruct((1, w), F32) for w in acc_outs]
    blk = sum(_nbytes((tm, w), a.dtype) for a, w, _ in row_ins) + sum(_nbytes((tm, w), dt) for w, dt in row_outs)
    return pl.pallas_call(
        body, name=name, grid=(n_rows // tm,), in_specs=in_specs, out_specs=out_specs, out_shape=out_shape,
        compiler_params=pltpu.CompilerParams(
            dimension_semantics=("arbitrary" if acc_outs else "parallel",), vmem_limit_bytes=_vmem_limit(3 * blk)),
    )(*_in_hbm(*[a for a, _, _ in row_ins], *bcast_ins), *deps)


def _accumulate(o_ref, part, first):
    @pl.when(first)
    def _():
        o_ref[...] = part

    @pl.when(jnp.logical_not(first))
    def _():
        o_ref[...] += part


_MM_DIMS = {"nn": ((1,), (0,)), "nt": ((1,), (1,)), "tn": ((0,), (0,))}


def _matmul(pairs, mode, out_dtype, name, tm, tn, tk, deps=()):
    a0, b0 = pairs[0]
    if mode == "tn":
        kk, m = a0.shape
    else:
        m, kk = a0.shape
    n = b0.shape[0] if mode == "nt" else b0.shape[1]
    assert m % tm == 0 and n % tn == 0 and kk % tk == 0, (name, m, n, kk)
    nk = kk // tk
    n_pairs = len(pairs)
    dims = _MM_DIMS[mode]
    n_in = 2 * n_pairs + len(deps)

    def body(*refs):
        o_ref = refs[n_in]
        part = None
        for p in range(n_pairs):
            d = _dot(refs[2 * p][...].astype(BF16), refs[2 * p + 1][...].astype(BF16), dims)
            part = d if part is None else part + d
        if nk == 1:
            o_ref[...] = part.astype(o_ref.dtype)
            return
        acc = refs[n_in + 1]
        k = pl.program_id(2)

        @pl.when(k == 0)
        def _():
            acc[...] = part

        @pl.when(k > 0)
        def _():
            acc[...] += part

        @pl.when(k == nk - 1)
        def _():
            o_ref[...] = acc[...].astype(o_ref.dtype)

    if mode == "tn":
        a_spec = pl.BlockSpec((tk, tm), lambda i, j, k: (k, i))
    else:
        a_spec = pl.BlockSpec((tm, tk), lambda i, j, k: (i, k))
    if mode == "nt":
        b_spec = pl.BlockSpec((tn, tk), lambda i, j, k: (j, k))
    else:
        b_spec = pl.BlockSpec((tk, tn), lambda i, j, k: (k, j))
    blk = sum(_nbytes((tm, tk), a.dtype) + _nbytes((tk, tn), b.dtype) for a, b in pairs) + 2 * _nbytes((tm, tn), F32)
    flat = [a for pair in pairs for a in pair]
    return pl.pallas_call(
        body, name=name, grid=(m // tm, n // tn, nk),
        in_specs=[a_spec, b_spec] * n_pairs + [pl.BlockSpec(memory_space=pl.ANY)] * len(deps),
        out_specs=pl.BlockSpec((tm, tn), lambda i, j, k: (i, j)),
        out_shape=jax.ShapeDtypeStruct((m, n), out_dtype),
        scratch_shapes=[] if nk == 1 else [pltpu.VMEM((tm, tn), F32)],
        compiler_params=pltpu.CompilerParams(
            dimension_semantics=("parallel", "parallel", "arbitrary"), vmem_limit_bytes=_vmem_limit(blk)),
    )(*flat, *deps)


def _matmul_rowwise(pairs, fn, name, tm, row_ins, bcast_ins, row_outs, acc_outs=(), deps=()):
    m = pairs[0][0].shape[0]
    n_tiles = m // tm
    n_mm, n_in = 2 * len(pairs), len(row_ins) + len(bcast_ins)
    n_ro = len(row_outs)

    def body(*refs):
        i = pl.program_id(0)
        outs, products = refs[n_mm + n_in + len(deps):-2], refs[-2:]

        def multiply(write):
            prod = None
            for p in range(len(pairs)):
                part = _dot_nn(refs[2 * p][...].astype(BF16), refs[2 * p + 1][...].astype(BF16))
                prod = part if prod is None else prod + part
            write[...] = prod

        def epilogue(read):
            res = fn(read[...], *[r[...] for r in refs[n_mm:n_mm + n_in]])
            for r, o in zip(res[:n_ro], outs[:n_ro]):
                o[...] = r.astype(o.dtype)
            for r, o in zip(res[n_ro:], outs[n_ro:]):
                o[...] += jnp.where(i >= 1, r, 0.0)

        @pl.when(i == 0)
        def _():
            products[1][...] = jnp.zeros_like(products[1])
            for o in outs[n_ro:]:
                o[...] = jnp.zeros_like(o)

        for parity in (0, 1):
            @pl.when(jnp.logical_and(i % 2 == parity, i < n_tiles))
            def _(parity=parity):
                epilogue(products[1 - parity])
                multiply(products[parity])

        @pl.when(i == n_tiles)
        def _():
            epilogue(products[(n_tiles - 1) % 2])

    def ahead(i):
        return jnp.minimum(i, n_tiles - 1)

    def behind(i):
        return jnp.maximum(i - 1, 0)

    in_specs = []
    for a, b in pairs:
        in_specs += [pl.BlockSpec((tm, a.shape[1]), lambda i: (ahead(i), 0)),
                     pl.BlockSpec(b.shape, lambda i: (0, 0), pipeline_mode=pl.Buffered(1))]
    in_specs += [pl.BlockSpec((tm, w), lambda i, cb=cb: (behind(i), cb)) for _, w, cb in row_ins]
    in_specs += [pl.BlockSpec(a.shape, lambda i: (0, 0)) for a in bcast_ins]
    in_specs += [_ANY] * len(deps)
    out_specs = [pl.BlockSpec((tm, w), lambda i: (behind(i), 0)) for w, _ in row_outs]
    out_specs += [pl.BlockSpec((1, w), lambda i: (0, 0)) for w in acc_outs]
    out_shape = [jax.ShapeDtypeStruct((m, w), dt) for w, dt in row_outs]
    out_shape += [jax.ShapeDtypeStruct((1, w), F32) for w in acc_outs]
    n_out = pairs[0][1].shape[1]
    blk = sum(_nbytes((tm, a.shape[1]), a.dtype) + _nbytes(b.shape, b.dtype) // 2 for a, b in pairs)
    blk += sum(_nbytes((tm, w), a.dtype) for a, w, _ in row_ins) + sum(_nbytes((tm, w), dt) for w, dt in row_outs)
    blk += _nbytes((tm, n_out), F32)
    return pl.pallas_call(
        body, name=name, grid=(n_tiles + 1,), in_specs=in_specs, out_specs=out_specs, out_shape=out_shape,
        scratch_shapes=[pltpu.VMEM((tm, n_out), F32)] * 2,
        compiler_params=pltpu.CompilerParams(dimension_semantics=("arbitrary",), vmem_limit_bytes=_vmem_limit(blk)),
    )(*[a for pair in pairs for a in pair], *[a for a, _, _ in row_ins], *bcast_ins, *deps)


def _rms_scale(x):
    return lax.rsqrt(jnp.mean(x * x, axis=-1, keepdims=True) + EPS)


def _rms_bwd(xin, dyn, g):
    r = _rms_scale(xin)
    u = dyn * g
    dx = r * u - xin * (r * r * r) * jnp.mean(u * xin, axis=-1, keepdims=True)
    dg = jnp.sum(dyn * xin * r, axis=0, keepdims=True)
    return dx, dg


def _mesh_pos():
    return lax.axis_index("x"), lax.axis_index("y"), lax.axis_index("c")


def _all_gather(xs, name, deps=()):
    n = len(xs)

    def body(*refs):
        x_refs, out_refs = refs[:n], refs[n + len(deps):2 * n + len(deps)]
        send_sems, recv_sems, local_sems = refs[2 * n + len(deps):]
        mx, my, mc = _mesh_pos()
        me, sib = (mx, my, mc), (mx, my, 1 - mc)
        chips = [(1 - mx, my), (mx, 1 - my), (1 - mx, 1 - my)]

        def slot(a, dev):
            px, py, pc = dev
            return out_refs[a].at[4 * px + 2 * py + pc]

        def copy(k, a, block, to, src=None):
            return pltpu.make_async_remote_copy(
                src_ref=slot(a, block) if src is None else src, dst_ref=slot(a, block),
                send_sem=send_sems.at[a * 7 + k], recv_sem=recv_sems.at[a * 7 + k],
                device_id=to, device_id_type=MESH)

        mine = [pltpu.make_async_copy(x_refs[a], slot(a, me), local_sems.at[a]) for a in range(n)]
        for cp in mine:
            cp.start()
        first = []
        for a in range(n):
            first.append(copy(0, a, me, sib, x_refs[a]))
            first += [copy(1 + j, a, me, (*chip, mc), x_refs[a]) for j, chip in enumerate(chips)]
        for cp in first:
            cp.start()
        passed = []
        for a in range(n):
            for j, chip in enumerate(chips):
                copy(1 + j, a, (*chip, mc), me).wait_recv()
                fwd = copy(4 + j, a, (*chip, mc), sib)
                fwd.start()
                passed.append(fwd)
        for a in range(n):
            copy(0, a, sib, me).wait_recv()
            for j, chip in enumerate(chips):
                copy(4 + j, a, (*chip, 1 - mc), me).wait_recv()
        for cp in first + passed:
            cp.wait_send()
        for cp in mine:
            cp.wait()

    hbm = pl.BlockSpec(memory_space=pl.ANY)
    return pl.pallas_call(
        body, name=name,
        out_shape=[jax.ShapeDtypeStruct((N_DEV,) + x.shape, x.dtype) for x in xs],
        in_specs=[hbm] * (n + len(deps)), out_specs=[hbm] * n,
        scratch_shapes=[pltpu.SemaphoreType.DMA((7 * n,)), pltpu.SemaphoreType.DMA((7 * n,)),
                        pltpu.SemaphoreType.DMA((n,))],
    )(*xs, *deps)


_HBM = pl.BlockSpec(memory_space=pltpu.HBM)
_SEM = pl.BlockSpec(memory_space=pltpu.SEMAPHORE)
_ANY = pl.BlockSpec(memory_space=pl.ANY)
_DATAFLOW = pltpu.SideEffectType.DATAFLOW_SIDE_EFFECTING


def _flip_peer(flip):
    mx, my, mc = _mesh_pos()
    return (1 - mx if flip & 2 else mx, 1 - my if flip & 1 else my, mc)


def _remote(src, dst, send_sems, recv_sems, k, peer):
    return pltpu.make_async_remote_copy(src_ref=src, dst_ref=dst, send_sem=send_sems.at[k], recv_sem=recv_sems.at[k],
                                        device_id=peer, device_id_type=MESH)


def _gather_chips_copies(srcs, lands, send_sems, recv_sems):
    mx, my, mc = _mesh_pos()
    me = 4 * mx + 2 * my + mc
    return [_remote(srcs[a], lands[a].at[me], send_sems, recv_sems, 3 * a + flip - 1, _flip_peer(flip))
            for a in range(len(srcs)) for flip in (1, 2, 3)]


def _gather_sibling_copies(srcs, lands, send_sems, recv_sems):
    mx, my, mc = _mesh_pos()
    return [_remote(lands[a].at[2 * k + mc], lands[a].at[2 * k + mc], send_sems, recv_sems, 4 * a + k, (mx, my, 1 - mc))
            for a in range(len(lands)) for k in range(4)]


def _scatter_sibling_copies(srcs, lands, send_sems, recv_sems):
    mx, my, mc = _mesh_pos()
    return [_remote(srcs[a].at[k, 1 - mc], lands[a].at[k], send_sems, recv_sems, 4 * a + k, (mx, my, 1 - mc))
            for a in range(len(srcs)) for k in range(4)]


def _scatter_chips_copies(srcs, lands, send_sems, recv_sems):
    mx, my, _ = _mesh_pos()
    k0 = 2 * mx + my
    return [_remote(srcs[a].at[jnp.bitwise_xor(k0, flip)], lands[a].at[flip - 1], send_sems, recv_sems,
                    3 * a + flip - 1, _flip_peer(flip))
            for a in range(len(srcs)) for flip in (1, 2, 3)]


class _Exchange:
    def __init__(self, copies, n_src, send_sems, recv_sems, thru, token):
        self.copies, self.n_src, self.send_sems, self.recv_sems, self.thru, self.token = (
            copies, n_src, send_sems, recv_sems, thru, token)


def _exchange_start(name, copies, srcs, lands, n_copies, after=()):
    bufs = list(srcs) + list(lands)
    nb, ns = len(bufs), len(srcs)

    def body(*refs):
        send_sems, recv_sems = refs[nb + len(after)], refs[nb + len(after) + 1]
        for cp in copies(refs[:ns], refs[ns:nb], send_sems, recv_sems):
            cp.start()
        refs[-1][...] = jnp.zeros_like(refs[-1])

    out = pl.pallas_call(
        body, name=name,
        out_shape=(pltpu.SemaphoreType.DMA((n_copies,)), pltpu.SemaphoreType.DMA((n_copies,)),
                   *[pltpu.HBM(b.shape, b.dtype) for b in bufs], jax.ShapeDtypeStruct((SUBLANES, LANES), F32)),
        in_specs=[_HBM] * nb + [_ANY] * len(after),
        out_specs=(_SEM, _SEM, *[_HBM] * nb, pl.BlockSpec(memory_space=pltpu.VMEM)),
        input_output_aliases={i: 2 + i for i in range(nb)},
        compiler_params=pltpu.CompilerParams(has_side_effects=_DATAFLOW),
    )(*[pltpu.with_memory_space_constraint(b, pltpu.HBM) for b in bufs], *after)
    return _Exchange(copies, ns, out[0], out[1], list(out[2:2 + nb]), out[-1])


def _exchange_wait(name, ex, after):
    nb, ns = len(ex.thru), ex.n_src

    def body(*refs):
        for cp in ex.copies(refs[:ns], refs[ns:nb], refs[nb], refs[nb + 1]):
            cp.wait_send()
            cp.wait_recv()

    out = pl.pallas_call(
        body, name=name, out_shape=tuple(pltpu.HBM(b.shape, b.dtype) for b in ex.thru),
        in_specs=[_HBM] * nb + [_SEM, _SEM] + [_ANY] * len(after), out_specs=tuple([_HBM] * nb),
        input_output_aliases={i: i for i in range(nb)},
        compiler_params=pltpu.CompilerParams(has_side_effects=_DATAFLOW),
    )(*ex.thru, ex.send_sems, ex.recv_sems, *after)
    return list(out[:ns]), list(out[ns:])


def _col_tile(r, c):
    return next(t for t in (1024, 512, 256, 128) if c % t == 0 and (r * t * 4 <= 2**20 or t == 128))


def _add_sibling(g4, recv, pos, name):
    _, _, r, c = g4.shape
    tc = _col_tile(r, c)

    def body(pos_ref, g_ref, r_ref, o16_ref, mine_ref):
        s = g_ref[0, 0] + r_ref[0]
        o16_ref[0] = s.astype(BF16)

        @pl.when(pl.program_id(1) == pos_ref[1])
        def _():
            mine_ref[...] = s

    slot = pl.BlockSpec((1, r, tc), lambda j, k, pos_ref: (k, 0, j))
    return pl.pallas_call(
        body, name=name,
        out_shape=[jax.ShapeDtypeStruct((4, r, c), BF16), jax.ShapeDtypeStruct((r, c), F32)],
        grid_spec=pltpu.PrefetchScalarGridSpec(
            num_scalar_prefetch=1, grid=(c // tc, 4),
            in_specs=[pl.BlockSpec((1, 1, r, tc), lambda j, k, pos_ref: (k, pos_ref[0], 0, j)), slot],
            out_specs=[slot, pl.BlockSpec((r, tc), lambda j, k, pos_ref: (0, j))]),
        compiler_params=pltpu.CompilerParams(dimension_semantics=("parallel", "arbitrary")),
    )(pos, *_in_hbm(g4, recv))


class _ReduceScatter:
    def __init__(self, tag, grads_t, pos):
        self.tag, self.pos, self.names = tag, pos, list(grads_t)
        g4s = [g.reshape(4, 2, g.size // (N_DEV * g.shape[-1]), g.shape[-1]) for g in grads_t.values()]
        lands = [lax.empty((4,) + g.shape[2:], F32) for g in g4s]
        self.ex = _exchange_start(f"rs_{tag}_sibling_start", _scatter_sibling_copies, g4s, lands, 4 * len(g4s))
        self.token = self.ex.token

    def start_chips(self, after):
        g4s, from_sibling = _exchange_wait(f"rs_{self.tag}_sibling_wait", self.ex, after)
        parts = [_add_sibling(g4, rv, self.pos, f"rs_add_sibling_{k}")
                 for k, g4, rv in zip(self.names, g4s, from_sibling)]
        self.mine = [mine for _, mine in parts]
        p16s = [p16 for p16, _ in parts]
        lands = [lax.empty((3,) + p.shape[1:], BF16) for p in p16s]
        self.ex = _exchange_start(f"rs_{self.tag}_chips_start", _scatter_chips_copies, p16s, lands, 3 * len(p16s))
        self.token = self.ex.token

    def finish(self, after):
        _, from_chips = _exchange_wait(f"rs_{self.tag}_chips_wait", self.ex, after)
        return dict(zip(self.names, zip(self.mine, from_chips)))


def _rope_tables():
    positions = np.arange(SEQ, dtype=np.float32)
    inv_freq = np.power(np.float32(ROPE_THETA), -np.arange(0, ROPE_DIM, 2, dtype=np.float32) / np.float32(ROPE_DIM))
    ang = (positions[:, None] * inv_freq[None, :]).astype(np.float32)
    cos, sin = np.cos(ang).astype(np.float32), np.sin(ang).astype(np.float32)
    ones = np.ones((SEQ, HEAD_DIM - ROPE_DIM), np.float32)
    zeros8 = np.zeros((SEQ, ROPE_HALF), np.float32)
    zeros = np.zeros((SEQ, HEAD_DIM - ROPE_DIM), np.float32)
    c_head = np.concatenate([cos, cos, ones], axis=1)
    s1_head = np.concatenate([-sin, zeros8, zeros], axis=1)
    s2_head = np.concatenate([zeros8, sin, zeros], axis=1)
    return tuple(jnp.asarray(np.concatenate([t, t], axis=1)) for t in (c_head, s1_head, s2_head))


def _rope_apply(x, c, s1, s2):
    w = x.shape[1]
    return x * c + pltpu.roll(x, w - ROPE_HALF, 1) * s1 + pltpu.roll(x, ROPE_HALF, 1) * s2


def _rope_apply_t(dy, c, s1, s2):
    w = dy.shape[1]
    return dy * c + pltpu.roll(dy * s1, ROPE_HALF, 1) + pltpu.roll(dy * s2, w - ROPE_HALF, 1)


def _dil_prev_limit(has_prev):
    return jnp.where(has_prev, 0, BLOCK)


def _dil_valid(limit):
    row = lax.broadcasted_iota(jnp.int32, (BLOCK, 2 * BLOCK), 0)
    col = lax.broadcasted_iota(jnp.int32, (BLOCK, 2 * BLOCK), 1)
    dist = col - row
    return jnp.logical_and(dist >= jnp.where(col < BLOCK, limit, -BLOCK), dist <= BLOCK)


def _upper_half():
    return lax.broadcasted_iota(jnp.int32, (1, LANES), 1) >= HEAD_DIM


def _dil_rows(n, d):
    per = N_BLOCKS // d
    r, lb = n // per, n % per

    def rows(b):
        start = b * (BLOCK * d) + r
        return pl.ds(pl.multiple_of(start, BLOCK), BLOCK) if d == 1 else pl.ds(start, BLOCK, stride=d)

    return rows(lb), rows(jnp.maximum(lb - 1, 0)), lb > 0


def _dil_specs(g):
    def col(base):
        return pl.BlockSpec((SEQ, LANES), lambda p: (0, base // LANES + 2 * g + p))

    table = pl.BlockSpec((SEQ, LANES), lambda p: (0, 0))
    return [col(COL_QA), col(COL_KA), col(COL_VA)], [table] * 3


def _store_columns(blocks, dproj_ref, cols, sem):
    copies = [pltpu.make_async_copy(b, dproj_ref.at[:, pl.ds(pl.multiple_of(c * LANES, LANES), LANES)], sem.at[i])
              for i, (b, c) in enumerate(zip(blocks, cols))]
    for cp in copies:
        cp.start()
    for cp in copies:
        cp.wait()


def _dil_fwd(g, proj, tables):
    d = DILATIONS[g]
    one_block = d == N_BLOCKS

    def body(q_ref, k_ref, v_ref, c_ref, s1_ref, s2_ref, o_ref, lse_ref):
        upper = _upper_half()

        def roped(ref, rows):
            return _rope_apply(ref[rows, :], c_ref[rows, :], s1_ref[rows, :], s2_ref[rows, :])

        def block(n, carry):
            rows, prev, has_prev = _dil_rows(n, d)
            qb = (roped(q_ref, rows) * QK_SCALE).astype(BF16)
            kw, vw = roped(k_ref, rows).astype(BF16), v_ref[rows, :].astype(BF16)
            if one_block:
                row = lax.broadcasted_iota(jnp.int32, (BLOCK, BLOCK), 0)
                valid = lax.broadcasted_iota(jnp.int32, (BLOCK, BLOCK), 1) <= row
            else:
                kw = jnp.concatenate([roped(k_ref, prev).astype(BF16), kw], axis=0)
                vw = jnp.concatenate([v_ref[prev, :].astype(BF16), vw], axis=0)
                valid = _dil_valid(_dil_prev_limit(has_prev))
            outs, lses = [], []
            for head_mask in (jnp.logical_not(upper), upper):
                s = jnp.where(valid, _dot_nt(qb, jnp.where(head_mask, kw, 0)), NEG_INF)
                m = jnp.max(s, axis=-1, keepdims=True)
                p = jnp.exp(s - m)
                den = jnp.sum(p, axis=-1, keepdims=True)
                outs.append(_dot_nn((p * (1.0 / den)).astype(BF16), jnp.where(head_mask, vw, 0)))
                lses.append(m + jnp.log(den))
            o_ref[rows, :] = outs[0] + outs[1]
            lse_ref[rows, :] = jnp.where(upper, lses[1], lses[0])
            return carry

        lax.fori_loop(0, N_BLOCKS, block, 0, unroll=8)

    qkv, tabs = _dil_specs(g)
    out = pl.BlockSpec((SEQ, LANES), lambda p: (0, p))
    return pl.pallas_call(
        body, name=f"dil_attn_fwd_{g}", grid=(2,), in_specs=qkv + tabs, out_specs=[out, out],
        out_shape=[jax.ShapeDtypeStruct((SEQ, DIL_OUT_WIDTH), F32)] * 2,
        compiler_params=pltpu.CompilerParams(dimension_semantics=("parallel",)),
    )(*_in_hbm(proj, proj, proj, *tables))


def _dil_bwd(g, proj, tables, do, lse, c, dproj, deps=()):
    d = DILATIONS[g]
    one_block = d == N_BLOCKS

    def body(q_ref, k_ref, v_ref, c_ref, s1_ref, s2_ref, do_ref, lse_ref, cc_ref, dproj_in, *rest):
        dproj_ref, dq_acc, dk_acc, dv_acc, dq_out, dk_out, dv_out, sem = rest[len(deps):]
        upper = _upper_half()
        dk_acc[...] = jnp.zeros_like(dk_acc)
        dv_acc[...] = jnp.zeros_like(dv_acc)

        def roped(ref, rows):
            return _rope_apply(ref[rows, :], c_ref[rows, :], s1_ref[rows, :], s2_ref[rows, :])

        def block(n, carry):
            rows, prev, has_prev = _dil_rows(n, d)
            qb = (roped(q_ref, rows) * QK_SCALE).astype(BF16)
            dob = do_ref[rows, :].astype(BF16)
            kw, vw = roped(k_ref, rows).astype(BF16), v_ref[rows, :].astype(BF16)
            if one_block:
                row = lax.broadcasted_iota(jnp.int32, (BLOCK, BLOCK), 0)
                valid = lax.broadcasted_iota(jnp.int32, (BLOCK, BLOCK), 1) <= row
            else:
                kw = jnp.concatenate([roped(k_ref, prev).astype(BF16), kw], axis=0)
                vw = jnp.concatenate([v_ref[prev, :].astype(BF16), vw], axis=0)
                valid = _dil_valid(_dil_prev_limit(has_prev))
            lse_t, c_t = lse_ref[rows, :], cc_ref[rows, :]
            dq, dk, dv = None, None, None
            for e, head_mask in enumerate((jnp.logical_not(upper), upper)):
                km, vm = jnp.where(head_mask, kw, 0), jnp.where(head_mask, vw, 0)
                lse_col = lse_t[:, e * HEAD_DIM:e * HEAD_DIM + 1]
                c_col = c_t[:, e * HEAD_DIM:e * HEAD_DIM + 1]
                p = jnp.where(valid, jnp.exp(_dot_nt(qb, km) - lse_col), 0.0)
                ds = (p * (_dot_nt(dob, vm) + c_col)).astype(BF16)
                parts = (_dot_nn(ds, km), _dot_tn(ds, jnp.where(head_mask, qb, 0)),
                         _dot_tn(p.astype(BF16), jnp.where(head_mask, dob, 0)))
                dq, dk, dv = parts if dq is None else (dq + parts[0], dk + parts[1], dv + parts[2])
            dq_acc[rows, :] = dq * QK_SCALE
            if one_block:
                dk_acc[rows, :] += dk
                dv_acc[rows, :] += dv
            else:
                dk_acc[prev, :] += dk[:BLOCK]
                dv_acc[prev, :] += dv[:BLOCK]
                dk_acc[rows, :] += dk[BLOCK:]
                dv_acc[rows, :] += dv[BLOCK:]
            return carry

        lax.fori_loop(0, N_BLOCKS, block, 0, unroll=8)
        tabs = (c_ref[...], s1_ref[...], s2_ref[...])
        dq_out[...] = _rope_apply_t(dq_acc[...], *tabs).astype(BF16)
        dk_out[...] = _rope_apply_t(dk_acc[...], *tabs).astype(BF16)
        dv_out[...] = dv_acc[...].astype(BF16)
        pair = 2 * g + pl.program_id(0)
        _store_columns((dq_out, dk_out, dv_out), dproj_ref,
                       [base // LANES + pair for base in (COL_QA, COL_KA, COL_VA)], sem)

    qkv, tabs = _dil_specs(g)
    tok = pl.BlockSpec((SEQ, LANES), lambda p: (0, p))
    return pl.pallas_call(
        body, name=f"dil_attn_bwd_{g}", grid=(2,),
        in_specs=qkv + tabs + [tok, tok, tok, _ANY] + [_ANY] * len(deps), out_specs=_ANY,
        out_shape=jax.ShapeDtypeStruct(dproj.shape, dproj.dtype),
        scratch_shapes=[pltpu.VMEM((SEQ, LANES), F32)] * 3 + [pltpu.VMEM((SEQ, LANES), BF16)] * 3
        + [pltpu.SemaphoreType.DMA((3,))],
        input_output_aliases={9: 0},
        compiler_params=pltpu.CompilerParams(dimension_semantics=("arbitrary",)),
    )(*_in_hbm(proj, proj, proj, *tables, do, lse, c, dproj), *deps)


def _group_weights(l0, l1, l2):
    m = jnp.maximum(jnp.maximum(l0, l1), l2)
    e0, e1, e2 = jnp.exp(l0 - m), jnp.exp(l1 - m), jnp.exp(l2 - m)
    tot = e0 + e1 + e2
    return e0 / tot, e1 / tot, e2 / tot


def _dil_combine(outs, lses, deps=()):
    def fn(o0, o1, o2, l0, l1, l2):
        w0, w1, w2 = _group_weights(l0, l1, l2)
        return w0 * o0 + w1 * o1 + w2 * o2

    w = DIL_OUT_WIDTH
    return _rowwise(fn, "dil_combine", SEQ, 512, [(a, w, 0) for a in list(outs) + list(lses)], [], [(w, F32)],
                    deps=deps)[0]


def _dil_combine_bwd(d_out, outs, lses, deps=()):
    w = DIL_OUT_WIDTH

    def fn(d, o0, o1, o2, l0, l1, l2):
        row = lax.broadcasted_iota(jnp.int32, (w, w), 0) // HEAD_DIM
        col = lax.broadcasted_iota(jnp.int32, (w, w), 1) // HEAD_DIM
        same_head = jnp.where(row == col, 1.0, 0.0).astype(BF16)
        ws = _group_weights(l0, l1, l2)
        dws = [_dot3_nn(d * og, same_head) for og in (o0, o1, o2)]
        mean = ws[0] * dws[0] + ws[1] * dws[1] + ws[2] * dws[2]
        return tuple(wg * d for wg in ws) + tuple(-wg * mean for wg in ws)

    res = _rowwise(fn, "dil_combine_bwd", SEQ, 256, [(a, w, 0) for a in [d_out] + list(outs) + list(lses)], [],
                   [(w, F32)] * 6, deps=deps)
    return res[:3], res[3:]


def _log1p(e):
    u = 1.0 + e
    return jnp.where(u == 1.0, e, jnp.log(u) * (e / (u - 1.0)))


def _fox_gate(proj, b_pad, deps=()):
    def body(f_ref, b_ref, *rest):
        o_ref = rest[-1]
        z = f_ref[...] + b_ref[...]
        logf = (jnp.minimum(z, 0.0) - _log1p(jnp.exp(-jnp.abs(z)))).T[:F_ROWS]
        row = lax.broadcasted_iota(jnp.int32, (BLOCK, BLOCK), 0)
        col = lax.broadcasted_iota(jnp.int32, (BLOCK, BLOCK), 1)
        before = jnp.where(row <= col, 1.0, 0.0).astype(BF16)
        carry = jnp.zeros((F_ROWS, 1), F32)
        for blk in range(N_BLOCKS):
            run = _dot3_nn(logf[:, blk * BLOCK:(blk + 1) * BLOCK], before) + carry
            o_ref[:, blk * BLOCK:(blk + 1) * BLOCK] = run
            carry = run[:, BLOCK - 1:BLOCK]

    return pl.pallas_call(
        body, name="fox_gate", grid=(1,),
        in_specs=[pl.BlockSpec((SEQ, LANES), lambda i: (0, COL_F // LANES)), pl.BlockSpec((1, LANES), lambda i: (0, 0))]
        + [_ANY] * len(deps),
        out_specs=pl.BlockSpec((F_ROWS, SEQ), lambda i: (0, 0)),
        out_shape=jax.ShapeDtypeStruct((F_ROWS, SEQ), F32),
    )(*_in_hbm(proj, b_pad), *deps)


def _fox_gate_bwd(d_cum, proj, b_pad, dproj):
    def body(d_ref, f_ref, b_ref, dproj_ref, dz_ref, db_ref):
        row = lax.broadcasted_iota(jnp.int32, (BLOCK, BLOCK), 0)
        col = lax.broadcasted_iota(jnp.int32, (BLOCK, BLOCK), 1)
        after = jnp.where(row >= col, 1.0, 0.0).astype(BF16)
        carry = jnp.zeros((F_ROWS, 1), F32)
        parts = [None] * N_BLOCKS
        for blk in reversed(range(N_BLOCKS)):
            run = _dot3_nn(d_ref[:, blk * BLOCK:(blk + 1) * BLOCK], after) + carry
            parts[blk] = run
            carry = run[:, 0:1]
        dlogf = jnp.concatenate(parts, axis=1)
        dlogf = jnp.concatenate([dlogf, jnp.zeros((LANES - F_ROWS, SEQ), F32)], axis=0).T
        dz = dlogf * _sigmoid(-(f_ref[...] + b_ref[...]))
        dz_ref[...] = dz.astype(BF16)
        db_ref[...] = jnp.sum(dz, axis=0, keepdims=True)

    f_cols = pl.BlockSpec((SEQ, LANES), lambda i: (0, COL_F // LANES))
    return pl.pallas_call(
        body, name="fox_gate_bwd", grid=(1,),
        in_specs=[pl.BlockSpec((F_ROWS, SEQ), lambda i: (0, 0)), f_cols, pl.BlockSpec((1, LANES), lambda i: (0, 0)), _ANY],
        out_specs=[f_cols, pl.BlockSpec((1, LANES), lambda i: (0, 0))],
        out_shape=[jax.ShapeDtypeStruct(dproj.shape, dproj.dtype), jax.ShapeDtypeStruct((1, LANES), F32)],
        input_output_aliases={3: 0},
    )(*_in_hbm(d_cum, proj, b_pad, dproj))


FOX_TILE = 256
FOX_TILES = SEQ // FOX_TILE


def _row_to_col(row):
    n = row.shape[1]
    eye = lax.broadcasted_iota(jnp.int32, (n, n), 0) == lax.broadcasted_iota(jnp.int32, (n, n), 1)
    return jnp.sum(jnp.where(eye, row, 0.0), axis=1, keepdims=True)


def _fox_scores(q_tile, km, f_row, i):
    t = FOX_TILE
    ext = (i + 1) * t
    f_q = _row_to_col(f_row[:, i * t:(i + 1) * t])
    s = _dot_nt(q_tile, km[:ext]) + (f_q - f_row[:, :ext])
    row = lax.broadcasted_iota(jnp.int32, (t, ext), 0) + i * t
    col = lax.broadcasted_iota(jnp.int32, (t, ext), 1)
    return s, col <= row


def _fox_specs():
    qkv = [pl.BlockSpec((SEQ, LANES), lambda p, base=base: (0, base // LANES + p)) for base in (COL_QB, COL_KB, COL_VB)]
    return qkv, pl.BlockSpec((F_ROWS, SEQ), lambda p: (0, 0))


def _fox_fwd(proj, f_rows):
    t = FOX_TILE

    def body(q_ref, k_ref, v_ref, f_ref, o_ref, lse_ref):
        pair = pl.program_id(0)
        upper = _upper_half()
        masks = (jnp.logical_not(upper), upper)
        k16, v16 = k_ref[...].astype(BF16), v_ref[...].astype(BF16)
        kms = [jnp.where(hm, k16, 0) for hm in masks]
        vms = [jnp.where(hm, v16, 0) for hm in masks]
        f_row = [f_ref[pl.ds(2 * pair + e, 1), :] for e in range(2)]
        for i in range(FOX_TILES):
            q_tile = (q_ref[i * t:(i + 1) * t, :] * QK_SCALE).astype(BF16)
            outs, lses = [], []
            for e in range(2):
                s, causal = _fox_scores(q_tile, kms[e], f_row[e], i)
                s = jnp.where(causal, s, NEG_INF)
                m = jnp.max(s, axis=-1, keepdims=True)
                p = jnp.exp(s - m)
                den = jnp.sum(p, axis=-1, keepdims=True)
                outs.append(_dot_nn((p * (1.0 / den)).astype(BF16), vms[e][:(i + 1) * t]))
                lses.append(m + jnp.log(den))
            o_ref[i * t:(i + 1) * t, :] = outs[0] + outs[1]
            lse_ref[i * t:(i + 1) * t, :] = jnp.where(upper, lses[1], lses[0])

    qkv, f_spec = _fox_specs()
    tok = pl.BlockSpec((SEQ, LANES), lambda p: (0, p))
    return pl.pallas_call(
        body, name="fox_attn_fwd", grid=(FOX_WIDTH // LANES,),
        in_specs=qkv + [f_spec], out_specs=[tok, tok],
        out_shape=[jax.ShapeDtypeStruct((SEQ, FOX_WIDTH), F32)] * 2,
        compiler_params=pltpu.CompilerParams(
            dimension_semantics=("parallel",), vmem_limit_bytes=_vmem_limit(8 * t * SEQ * 4)),
    )(*_in_hbm(proj, proj, proj, f_rows))


def _fox_bwd(proj, do, lse, f_rows, dproj):
    t = FOX_TILE

    def body(q_ref, k_ref, v_ref, f_ref, do_ref, lse_ref, dproj_in, dproj_ref, df_ref, dk_acc, dv_acc,
             dq_out, dk_out, dv_out, sem):
        pair = pl.program_id(0)
        upper = _upper_half()
        masks = (jnp.logical_not(upper), upper)
        k16, v16 = k_ref[...].astype(BF16), v_ref[...].astype(BF16)
        kms = [jnp.where(hm, k16, 0) for hm in masks]
        vms = [jnp.where(hm, v16, 0) for hm in masks]
        f_row = [f_ref[pl.ds(2 * pair + e, 1), :] for e in range(2)]
        dk_acc[...] = jnp.zeros_like(dk_acc)
        dv_acc[...] = jnp.zeros_like(dv_acc)
        df_ref[...] = jnp.zeros_like(df_ref)
        for i in range(FOX_TILES):
            ext = (i + 1) * t
            q_tile = (q_ref[i * t:(i + 1) * t, :] * QK_SCALE).astype(BF16)
            do_tile = do_ref[i * t:(i + 1) * t, :]
            lse_t = lse_ref[i * t:(i + 1) * t, :]
            dq = None
            for e in range(2):
                s, causal = _fox_scores(q_tile, kms[e], f_row[e], i)
                p = jnp.where(causal, jnp.exp(s - lse_t[:, e * HEAD_DIM:e * HEAD_DIM + 1]), 0.0)
                dp = _dot_nt(do_tile, vms[e][:ext])
                ds = p * (dp - jnp.sum(p * dp, axis=-1, keepdims=True))
                df_ref[0, e:e + 1, :ext] -= jnp.sum(ds, axis=0, keepdims=True)
                ds = ds.astype(BF16)
                part = _dot_nn(ds, kms[e][:ext])
                dq = part if dq is None else dq + part
                dk_acc[:ext, :] += _dot_tn(ds, jnp.where(masks[e], q_tile, 0))
                dv_acc[:ext, :] += _dot_tn(p.astype(BF16), jnp.where(masks[e], do_tile, 0))
            dq_out[i * t:(i + 1) * t, :] = (dq * QK_SCALE).astype(BF16)
        dk_out[...] = dk_acc[...].astype(BF16)
        dv_out[...] = dv_acc[...].astype(BF16)
        _store_columns((dq_out, dk_out, dv_out), dproj_ref, [base // LANES + pair for base in (COL_QB, COL_KB, COL_VB)],
                       sem)

    qkv, f_spec = _fox_specs()
    tok = pl.BlockSpec((SEQ, LANES), lambda p: (0, p))
    return pl.pallas_call(
        body, name="fox_attn_bwd", grid=(FOX_WIDTH // LANES,),
        in_specs=qkv + [f_spec, tok, tok, _ANY],
        out_specs=[_ANY, pl.BlockSpec((1, SUBLANES, SEQ), lambda p: (p, 0, 0))],
        out_shape=[jax.ShapeDtypeStruct(dproj.shape, dproj.dtype),
                   jax.ShapeDtypeStruct((FOX_WIDTH // LANES, SUBLANES, SEQ), F32)],
        scratch_shapes=[pltpu.VMEM((SEQ, LANES), F32)] * 2 + [pltpu.VMEM((SEQ, LANES), BF16)] * 3
        + [pltpu.SemaphoreType.DMA((3,))],
        input_output_aliases={6: 0},
        compiler_params=pltpu.CompilerParams(
            dimension_semantics=("arbitrary",), vmem_limit_bytes=_vmem_limit(10 * t * SEQ * 4)),
    )(*_in_hbm(proj, proj, proj, f_rows, do, lse, dproj))


MIX_TILE = 256


def _mix_out(out_a, out_b, proj, x, wt_pa, wt_pb, w_out, g_post, g_ffn_pre):
    tm = MIX_TILE

    def body(a_ref, b_ref, ga_ref, gb_ref, x_ref, wpa_ref, wpb_ref, wo_ref, g2_ref, g3_ref,
             merged_ref, mix_ref, x1_ref, h2_ref):
        ya = _dot_nn(a_ref[...].astype(BF16), wpa_ref[...])
        yb = _dot_nn(b_ref[...].astype(BF16), wpb_ref[...])
        merged = (_sigmoid(ga_ref[...]) * ya + _sigmoid(gb_ref[...]) * yb).astype(BF16)
        merged_ref[...] = merged
        mix = _dot_nn(merged, wo_ref[...])
        mix_ref[...] = mix
        x1 = x_ref[...] + mix * _rms_scale(mix) * g2_ref[...]
        x1_ref[...] = x1
        h2_ref[...] = (x1 * _rms_scale(x1) * g3_ref[...]).astype(BF16)

    def rows(w, cb=0):
        return pl.BlockSpec((tm, w), lambda i, cb=cb: (i, cb))

    def whole(a):
        return pl.BlockSpec(a.shape, lambda i: (0, 0))

    d = D_MODEL
    blk = _nbytes((tm, d), F32) * 6 + sum(_nbytes(a.shape, BF16) for a in (wt_pa, wt_pb, w_out))
    return pl.pallas_call(
        body, name="mix_out", grid=(SEQ // tm,),
        in_specs=[rows(DIL_OUT_WIDTH), rows(FOX_WIDTH), rows(d, COL_GA // d), rows(d, COL_GB // d), rows(d),
                  whole(wt_pa), whole(wt_pb), whole(w_out), whole(g_post), whole(g_ffn_pre)],
        out_specs=[rows(d)] * 4,
        out_shape=[jax.ShapeDtypeStruct((SEQ, d), dt) for dt in (BF16, F32, F32, BF16)],
        compiler_params=pltpu.CompilerParams(dimension_semantics=("parallel",), vmem_limit_bytes=_vmem_limit(blk)),
    )(out_a, out_b, proj, proj, x, wt_pa, wt_pb, w_out, g_post, g_ffn_pre)


def _mix_out_bwd(dmix, out_a, out_b, proj, wt_pa, wt_pb, w_out, deps=()):
    tm = MIX_TILE

    def body(dm_ref, a_ref, b_ref, ga_ref, gb_ref, wpa_ref, wpb_ref, wo_ref, *rest):
        dproj_ref, dya_ref, dyb_ref, da_ref, db_ref = rest[len(deps):]
        dmerged = _dot_nt(dm_ref[...], wo_ref[...])
        ya = _dot_nn(a_ref[...].astype(BF16), wpa_ref[...])
        yb = _dot_nn(b_ref[...].astype(BF16), wpb_ref[...])
        sa, sb = _sigmoid(ga_ref[...]), _sigmoid(gb_ref[...])
        dproj_ref[:, COL_GA:COL_GA + D_MODEL] = (dmerged * ya * (sa * (1.0 - sa))).astype(BF16)
        dproj_ref[:, COL_GB:COL_GB + D_MODEL] = (dmerged * yb * (sb * (1.0 - sb))).astype(BF16)
        dproj_ref[:, COL_GB + D_MODEL:] = jnp.zeros((tm, COL_QA - COL_GB - D_MODEL), BF16)
        dya = (dmerged * sa).astype(BF16)
        dyb = (dmerged * sb).astype(BF16)
        dya_ref[...] = dya
        dyb_ref[...] = dyb
        da_ref[...] = _dot_nt(dya, wpa_ref[...])
        db_ref[...] = _dot_nt(dyb, wpb_ref[...]).astype(BF16)

    def rows(w, cb=0):
        return pl.BlockSpec((tm, w), lambda i, cb=cb: (i, cb))

    def whole(a):
        return pl.BlockSpec(a.shape, lambda i: (0, 0))

    d = D_MODEL
    blk = _nbytes((tm, d), F32) * 8 + sum(_nbytes(a.shape, BF16) for a in (wt_pa, wt_pb, w_out))
    return pl.pallas_call(
        body, name="mix_out_bwd", grid=(SEQ // tm,),
        in_specs=[rows(d), rows(DIL_OUT_WIDTH), rows(FOX_WIDTH), rows(d, COL_GA // d), rows(d, COL_GB // d),
                  whole(wt_pa), whole(wt_pb), whole(w_out)] + [_ANY] * len(deps),
        out_specs=[rows(COL_QA)] + [rows(d)] * 2 + [rows(DIL_OUT_WIDTH), rows(FOX_WIDTH)],
        out_shape=[jax.ShapeDtypeStruct((SEQ, PROJ_COLS), BF16)] + [jax.ShapeDtypeStruct((SEQ, d), BF16)] * 2
        + [jax.ShapeDtypeStruct((SEQ, DIL_OUT_WIDTH), F32), jax.ShapeDtypeStruct((SEQ, FOX_WIDTH), BF16)],
        compiler_params=pltpu.CompilerParams(dimension_semantics=("parallel",), vmem_limit_bytes=_vmem_limit(blk)),
    )(dmix, out_a, out_b, proj, proj, wt_pa, wt_pb, w_out, *deps)


FFN_TM, FFN_TN = 2048, 256


def _ffn_up(h2, wt_gate, wt_up):
    tm, tn = FFN_TM, FFN_TN

    def body(h_ref, wg_ref, wu_ref, gate_ref, up_ref, act_ref):
        gate = _dot_nt(h_ref[...], wg_ref[...])
        up = _dot_nt(h_ref[...], wu_ref[...])
        gate_ref[...] = gate
        up_ref[...] = up
        act_ref[...] = (gate * _sigmoid(gate) * up).astype(BF16)

    tile = pl.BlockSpec((tm, tn), lambda i, j: (i, j))
    w_spec = pl.BlockSpec((tn, D_MODEL), lambda i, j: (j, 0))
    return pl.pallas_call(
        body, name="ffn_up", grid=(SEQ // tm, D_FF // tn),
        in_specs=[pl.BlockSpec((tm, D_MODEL), lambda i, j: (i, 0)), w_spec, w_spec],
        out_specs=[tile, tile, tile],
        out_shape=[jax.ShapeDtypeStruct((SEQ, D_FF), dt) for dt in (F32, F32, BF16)],
        compiler_params=pltpu.CompilerParams(
            dimension_semantics=("parallel", "parallel"), vmem_limit_bytes=_vmem_limit(8 * 2**20)),
    )(h2, wt_gate, wt_up)


def _ffn_act_bwd(dff, w_down, gate, up):
    tm, tn = FFN_TM, FFN_TN

    def body(d_ref, wd_ref, gate_ref, up_ref, dgate_ref, dup_ref):
        dact = _dot_nt(d_ref[...], wd_ref[...])
        gate = gate_ref[...]
        sg = _sigmoid(gate)
        dgate_ref[...] = (dact * up_ref[...] * (sg * (1.0 + gate * (1.0 - sg)))).astype(BF16)
        dup_ref[...] = (dact * (gate * sg)).astype(BF16)

    tile = pl.BlockSpec((tm, tn), lambda i, j: (i, j))
    return pl.pallas_call(
        body, name="ffn_act_bwd", grid=(SEQ // tm, D_FF // tn),
        in_specs=[pl.BlockSpec((tm, D_MODEL), lambda i, j: (i, 0)), pl.BlockSpec((tn, D_MODEL), lambda i, j: (j, 0)),
                  tile, tile],
        out_specs=[tile, tile],
        out_shape=[jax.ShapeDtypeStruct((SEQ, D_FF), BF16)] * 2,
        compiler_params=pltpu.CompilerParams(
            dimension_semantics=("parallel", "parallel"), vmem_limit_bytes=_vmem_limit(8 * 2**20)),
    )(dff, w_down, gate, up)


EPILOGUE_TM = 512


def _loss_head(act, w_down, x1, target, g_post):
    def fn(ff, x1, tgt, g):
        r = _rms_scale(ff)
        nrm = ff * r
        err = (x1 + nrm * g) - tgt
        loss = 0.5 * jnp.sum(jnp.mean(err * err, axis=-1, keepdims=True), axis=0, keepdims=True)
        dy = err * (1.0 / D_MODEL)
        u = dy * g
        dff = r * u - ff * (r * r * r) * jnp.mean(u * ff, axis=-1, keepdims=True)
        return dy, dff, jnp.broadcast_to(loss, (1, LANES)), jnp.sum(dy * nrm, axis=0, keepdims=True)

    d = D_MODEL
    return _matmul_rowwise([(act, w_down)], fn, "ffn_down_loss", EPILOGUE_TM, [(x1, d, 0), (target, d, 0)], [g_post],
                           [(d, F32), (d, BF16)], [LANES, d])


def _post_ffn_bwd(dgate, wt_gate, dup, wt_up, x1, dy, mix, g_ffn_pre, g_mix_post, deps=()):
    def fn(dh2, x1, dy, mix, g3, g2):
        dx, dg3 = _rms_bwd(x1, dh2, g3)
        dx1 = dy + dx
        dmix, dg2 = _rms_bwd(mix, dx1, g2)
        return dx1, dmix, dg3, dg2

    d = D_MODEL
    return _matmul_rowwise([(dgate, wt_gate), (dup, wt_up)], fn, "ffn_up_bwd", EPILOGUE_TM,
                           [(x1, d, 0), (dy, d, 0), (mix, d, 0)], [g_ffn_pre, g_mix_post],
                           [(d, F32), (d, BF16)], [d, d], deps=deps)


def _input_bwd(dproj, wt_r, x, dx1, g_pre, deps=()):
    def fn(dh, x, dx1, g):
        dx, dg = _rms_bwd(x, dh, g)
        return dx1 + dx, dg

    d = D_MODEL
    return _matmul_rowwise([(dproj, wt_r)], fn, "in_proj_bwd", EPILOGUE_TM, [(x, d, 0), (dx1, d, 0)], [g_pre],
                           [(d, F32)], [d], deps=deps)


def _adam_math(w, g, m, v):
    m = ADAM_B1 * m + (1.0 - ADAM_B1) * g
    v = ADAM_B2 * v + (1.0 - ADAM_B2) * (g * g)
    m_hat = m / (1.0 - ADAM_B1 ** ADAM_STEP)
    v_hat = v / (1.0 - ADAM_B2 ** ADAM_STEP)
    delta = -ADAM_LR * (m_hat / (jnp.sqrt(v_hat) + ADAM_EPS) + ADAM_WD * w)
    return delta, m, v


def _adam(w, mine, recv, m, v, name):
    r, c = w.shape
    tc = _col_tile(r, c)

    def body(w_ref, p_ref, r_ref, m_ref, v_ref, g_ref, d_ref, nm_ref, nv_ref):
        g = ((p_ref[...] + r_ref[0].astype(F32)) + r_ref[1].astype(F32)) + r_ref[2].astype(F32)
        g_ref[...] = g
        d_ref[...], nm_ref[...], nv_ref[...] = _adam_math(w_ref[...], g, m_ref[...], v_ref[...])

    spec = pl.BlockSpec((r, tc), lambda j: (0, j))
    return pl.pallas_call(
        body, name=name, grid=(c // tc,),
        in_specs=[spec, spec, pl.BlockSpec((3, r, tc), lambda j: (0, 0, j)), spec, spec], out_specs=[spec] * 4,
        out_shape=[jax.ShapeDtypeStruct((r, c), F32)] * 4,
        compiler_params=pltpu.CompilerParams(dimension_semantics=("parallel",)),
    )(*_in_hbm(w, mine, recv, m, v))


def _adam_small(gathered, ws, ms, vs, loss_parts):
    n = len(ws)

    def body(*refs):
        outs = refs[4 * n + 1:]
        loss = refs[4 * n][0]
        for dev in range(1, N_DEV):
            loss = loss + refs[4 * n][dev]
        outs[4 * n][...] = loss
        for i in range(n):
            ga_ref, w_ref, m_ref, v_ref = (refs[j * n + i] for j in range(4))
            g = ga_ref[0]
            for dev in range(1, N_DEV):
                g = g + ga_ref[dev]
            g = g[:, :w_ref.shape[1]]
            outs[4 * i][...] = g
            outs[4 * i + 1][...], outs[4 * i + 2][...], outs[4 * i + 3][...] = _adam_math(
                w_ref[...], g, m_ref[...], v_ref[...])

    out_shape = [jax.ShapeDtypeStruct(w.shape, F32) for w in ws for _ in range(4)]
    out_shape.append(jax.ShapeDtypeStruct((1, LANES), F32))
    out = pl.pallas_call(body, name="adam_small", out_shape=out_shape)(*gathered, *ws, *ms, *vs, loss_parts)
    return [out[4 * i:4 * i + 4] for i in range(n)], out[4 * n]


_PROJ_SEGMENTS = ((3848, 5896), (None, COL_QA - 2 * D_MODEL), (0, 3840), (3840, 3848), (None, PROJ_COLS - COL_F - 8))


def _proj_weight_t(gathered):
    w = gathered.reshape(IN_COLS, D_MODEL)
    return jnp.concatenate([jnp.zeros((hi, D_MODEL), w.dtype) if lo is None else w[lo:hi] for lo, hi in _PROJ_SEGMENTS],
                           axis=0)


def _proj_weight_grad_slots(dwt_r):
    starts, at = [], 0
    for lo, hi in _PROJ_SEGMENTS:
        if lo is not None:
            starts.append((lo, hi, at))
        at += hi if lo is None else hi - lo
    slots = []
    for dev in range(N_DEV):
        pieces, lo, end = [], dev * IN_SHARD, (dev + 1) * IN_SHARD
        for seg_lo, seg_hi, seg_at in sorted(starts):
            a, b = max(lo, seg_lo), min(end, seg_hi)
            if a < b:
                pieces.append(dwt_r[seg_at + a - seg_lo:seg_at + b - seg_lo])
        slots.append(pieces[0] if len(pieces) == 1 else jnp.concatenate(pieces, axis=0))
    return jnp.stack(slots)


def kernel(x, w_in, w_proj_a, w_proj_b, w_out, b_forget, w_ffn_gate, w_ffn_up, w_ffn_down, norm_mix_pre, norm_mix_post, norm_ffn_pre, norm_ffn_post, loss_target, m_w_in, m_w_proj_a, m_w_proj_b, m_w_out, m_b_forget, m_w_ffn_gate, m_w_ffn_up, m_w_ffn_down, m_norm_mix_pre, m_norm_mix_post, m_norm_ffn_pre, m_norm_ffn_post, v_w_in, v_w_proj_a, v_w_proj_b, v_w_out, v_b_forget, v_w_ffn_gate, v_w_ffn_up, v_w_ffn_down, v_norm_mix_pre, v_norm_mix_post, v_norm_ffn_pre, v_norm_ffn_post):
    d = D_MODEL
    names = ("w_in", "w_proj_a", "w_proj_b", "w_out", "w_ffn_gate", "w_ffn_up", "w_ffn_down")
    col_sharded = ("w_in", "w_ffn_gate", "w_ffn_up")

    def row_shards(arrs):
        return {k: (a[0].T if k in col_sharded else a[0]) for k, a in zip(names, arrs)}

    shards = row_shards((w_in, w_proj_a, w_proj_b, w_out, w_ffn_gate, w_ffn_up, w_ffn_down))
    moments_m = row_shards((m_w_in, m_w_proj_a, m_w_proj_b, m_w_out, m_w_ffn_gate, m_w_ffn_up, m_w_ffn_down))
    moments_v = row_shards((v_w_in, v_w_proj_a, v_w_proj_b, v_w_out, v_w_ffn_gate, v_w_ffn_up, v_w_ffn_down))
    pos = jnp.stack([lax.axis_index("c"), 2 * lax.axis_index("x") + lax.axis_index("y")]).astype(jnp.int32)
    x2, target = x[0], loss_target[0]

    me = 4 * lax.axis_index("x") + 2 * lax.axis_index("y") + lax.axis_index("c")
    mid_names, ffn_names = names[1:4], names[4:]
    first_names, later_names = names[:1], names[1:]
    shards16 = {k: shards[k].astype(BF16) for k in names}

    def landing(k):
        return lax.dynamic_update_slice(lax.empty((N_DEV,) + shards[k].shape, BF16), shards16[k][None], (me, 0, 0))

    ag_first = _exchange_start("ag_first_chips_start", _gather_chips_copies, [shards16[k] for k in first_names],
                               [landing(k) for k in first_names], 3 * len(first_names))
    h = _rowwise(lambda xb, g: xb * _rms_scale(xb) * g, "norm_mix_pre", SEQ, 256, [(x2, d, 0)], [norm_mix_pre],
                 [(d, BF16)], deps=[ag_first.token])[0]
    _, lands = _exchange_wait("ag_first_chips_wait", ag_first, [h])
    ag_first = _exchange_start("ag_first_sibling_start", _gather_sibling_copies, [], lands, 4 * len(first_names))
    ag_later = _exchange_start("ag_later_chips_start", _gather_chips_copies, [shards16[k] for k in later_names],
                               [landing(k) for k in later_names], 3 * len(later_names), after=[ag_first.token])
    gathered = dict(zip(first_names, _exchange_wait("ag_first_sibling_wait", ag_first, [ag_later.token])[1]))
    wt_r = _proj_weight_t(gathered["w_in"])

    proj = _matmul([(h, wt_r)], "nt", F32, "in_proj", 1024, 896, 1024)
    tables = _rope_tables()
    o_dil, lse_dil = zip(*[_dil_fwd(g, proj, tables) for g in range(len(DILATIONS))])
    out_a = _dil_combine(o_dil, lse_dil)
    _, lands = _exchange_wait("ag_later_chips_wait", ag_later, [out_a])
    ag_later = _exchange_start("ag_later_sibling_start", _gather_sibling_copies, [], lands, 4 * len(later_names))

    b_pad = jnp.pad(b_forget, ((0, 0), (0, LANES - N_FOX_HEADS)))
    f_rows = _fox_gate(proj, b_pad, deps=[ag_later.token])
    out_b, lse_fox = _fox_fwd(proj, f_rows)

    gathered = dict(zip(later_names, _exchange_wait("ag_later_sibling_wait", ag_later, [out_b])[1]))
    wt_pa = gathered["w_proj_a"].transpose(1, 0, 2).reshape(DIL_OUT_WIDTH, d)
    wt_pb = gathered["w_proj_b"].transpose(1, 0, 2).reshape(FOX_WIDTH, d)
    w_o = gathered["w_out"].reshape(d, d)
    wt_g = gathered["w_ffn_gate"].reshape(D_FF, d)
    wt_u = gathered["w_ffn_up"].reshape(D_FF, d)
    w_d = gathered["w_ffn_down"].reshape(D_FF, d)
    merged, mix, x1, h2 = _mix_out(out_a, out_b, proj, x2, wt_pa, wt_pb, w_o, norm_mix_post, norm_ffn_pre)

    gate, up, act = _ffn_up(h2, wt_g, wt_u)
    dy, dff, loss_part, dg_ffn_post = _loss_head(act, w_d, x1, target, norm_ffn_post)

    dgate, dup = _ffn_act_bwd(dff, w_d, gate, up)
    grads_t = {}
    grads_t["w_ffn_down"] = _matmul([(act, dff)], "tn", F32, "grad_w_ffn_down", 1408, 1024, 2048)
    grads_t["w_ffn_gate"] = _matmul([(dgate, h2)], "tn", F32, "grad_w_ffn_gate", 1408, 1024, 2048)
    grads_t["w_ffn_up"] = _matmul([(dup, h2)], "tn", F32, "grad_w_ffn_up", 1408, 1024, 2048)
    rs_ffn = _ReduceScatter("ffn", {k: grads_t[k] for k in ffn_names}, pos)
    dx1, dmix, dg_ffn_pre, dg_mix_post = _post_ffn_bwd(dgate, wt_g, dup, wt_u, x1, dy, mix, norm_ffn_pre, norm_mix_post,
                                                       deps=[rs_ffn.token])
    rs_ffn.start_chips([dmix])

    dproj, dya, dyb, d_out_a, d_out_b = _mix_out_bwd(dmix, out_a, out_b, proj, wt_pa, wt_pb, w_o, deps=[rs_ffn.token])
    grads_t["w_out"] = _matmul([(merged, dmix)], "tn", F32, "grad_w_out", 1024, 1024, 1024)
    def column_slots(g):
        return g.reshape(g.shape[0], N_DEV, LANES).transpose(1, 0, 2)

    grads_t["w_proj_a"] = column_slots(_matmul([(out_a, dya)], "tn", F32, "grad_w_proj_a", DIL_OUT_WIDTH, 1024, SEQ))
    grads_t["w_proj_b"] = column_slots(_matmul([(out_b, dyb)], "tn", F32, "grad_w_proj_b", FOX_WIDTH, 1024, SEQ))
    rs_mid = _ReduceScatter("mid", {k: grads_t[k] for k in mid_names}, pos)

    do_dil, c_dil = _dil_combine_bwd(d_out_a, o_dil, lse_dil, deps=[rs_mid.token])
    rs_mid.start_chips([c_dil[0]])
    dproj, d_cum = _fox_bwd(proj, d_out_b, lse_fox, f_rows, dproj)
    d_cum_rows = jnp.pad(d_cum[:, :2].reshape(N_FOX_HEADS, SEQ), ((0, F_ROWS - N_FOX_HEADS), (0, 0)))
    dproj, db_part = _fox_gate_bwd(d_cum_rows, proj, b_pad, dproj)
    for g in range(len(DILATIONS)):
        dproj = _dil_bwd(g, proj, tables, do_dil[g], lse_dil[g], c_dil[g], dproj, deps=[rs_mid.token])

    dwt_r = _matmul([(dproj, h)], "tn", F32, "grad_w_in", 896, 1024, 2048)
    rs_in = _ReduceScatter("in", {"w_in": _proj_weight_grad_slots(dwt_r)}, pos)
    def finish(rs, after):
        return {k: _adam(shards[k], mine, recv, moments_m[k], moments_v[k], "adam_" + k)
                for k, (mine, recv) in rs.finish(after).items()}

    done = finish(rs_ffn, [rs_in.token])
    rs_in.start_chips([done[k][0] for k in ffn_names])
    grad_x, dg_mix_pre = _input_bwd(dproj, wt_r, x2, dx1, norm_mix_pre, deps=[rs_in.token])
    done.update(finish(rs_mid, [grad_x]))

    small_all = _all_gather([dg_mix_pre, dg_mix_post, dg_ffn_pre, dg_ffn_post, db_part, loss_part],
                            "small_grads_all_gather", deps=[done[k][0] for k in mid_names])
    small, loss = _adam_small(small_all[:5], [norm_mix_pre, norm_mix_post, norm_ffn_pre, norm_ffn_post, b_forget],
                              [m_norm_mix_pre, m_norm_mix_post, m_norm_ffn_pre, m_norm_ffn_post, m_b_forget],
                              [v_norm_mix_pre, v_norm_mix_post, v_norm_ffn_pre, v_norm_ffn_post, v_b_forget],
                              small_all[5])

    done.update(finish(rs_in, [small[0][0]]))

    def leaves(i):
        def nat(k):
            a = done[k][i]
            return (a.T if k in col_sharded else a)[None]

        return [nat("w_in"), nat("w_proj_a"), nat("w_proj_b"), nat("w_out"), small[4][i],
                nat("w_ffn_gate"), nat("w_ffn_up"), nat("w_ffn_down"), *[small[r][i] for r in range(4)]]

    return (loss[0, 0], grad_x[None], *leaves(0), *leaves(1), *leaves(2), *leaves(3))
```

```python
import functools
import math

import jax
import jax.numpy as jnp
import numpy as np
from jax import lax
from jax.experimental import pallas as pl
from jax.experimental.pallas import tpu as pltpu

F32 = jnp.float32
BF16 = jnp.bfloat16
MESH = pl.DeviceIdType.MESH

D_MODEL = 1024
SEQ = 2048
HEAD_DIM = 64
BLOCK = 128
N_BLOCKS = SEQ // BLOCK
DILATIONS = (1, 4, 16)
N_FOX_HEADS = 8
DIL_WIDTH = 768
DIL_OUT_WIDTH = 256
FOX_WIDTH = 512
D_FF = 2816
ROPE_THETA = 500000.0
ROPE_DIM = HEAD_DIM // 4
ROPE_HALF = ROPE_DIM // 2
EPS = 1e-6
NEG_INF = -1e30
QK_SCALE = 1.0 / math.sqrt(HEAD_DIM)
IN_COLS = 5896
N_DEV = 8
IN_SHARD = IN_COLS // N_DEV

ADAM_LR = 0.001
ADAM_B1 = 0.9
ADAM_B2 = 0.999
ADAM_EPS = 1e-08
ADAM_WD = 0.01
ADAM_STEP = 10

V7X_VMEM_BYTES = 64 * 2**20
LANES = 128
SUBLANES = 8

PROJ_COLS = 6272
COL_GA, COL_GB = 0, 1024
COL_QA, COL_KA, COL_VA = 2304, 3072, 3840
COL_QB, COL_KB, COL_VB = 4608, 5120, 5632
COL_F = 6144
F_ROWS = 16


def _vmem_limit(block_bytes):
    want = 2 * block_bytes + 16 * 2**20
    return int(min(max(want, 32 * 2**20), V7X_VMEM_BYTES - 8 * 2**20))


def _nbytes(shape, dtype):
    return math.prod(shape) * jnp.dtype(dtype).itemsize


def _in_hbm(*arrays):
    return [pltpu.with_memory_space_constraint(a, pltpu.HBM) for a in arrays]


def _dot(a, b, dims):
    return lax.dot_general(a, b, (dims, ((), ())), preferred_element_type=F32)


def _dot_nn(a, b):
    return _dot(a, b, ((1,), (0,)))


def _dot_nt(a, b):
    return _dot(a, b, ((1,), (1,)))


def _dot_tn(a, b):
    return _dot(a, b, ((0,), (0,)))


def _sigmoid(z):
    return 1.0 / (1.0 + jnp.exp(-z))


def _split3(x):
    hi = x.astype(BF16)
    r1 = x - hi.astype(F32)
    mid = r1.astype(BF16)
    lo = (r1 - mid.astype(F32)).astype(BF16)
    return hi, mid, lo


def _dot3_nn(x, ones_matrix):
    hi, mid, lo = _split3(x)
    return (_dot_nn(hi, ones_matrix) + _dot_nn(mid, ones_matrix)) + _dot_nn(lo, ones_matrix)


def _rowwise(fn, name, n_rows, tm, row_ins, bcast_ins, row_outs, acc_outs=(), deps=()):
    n_in = len(row_ins) + len(bcast_ins)
    n_ro = len(row_outs)

    def body(*refs):
        res = fn(*[r[...] for r in refs[:n_in]])
        if not isinstance(res, (tuple, list)):
            res = (res,)
        outs = refs[n_in + len(deps):]
        for r, o in zip(res[:n_ro], outs[:n_ro]):
            o[...] = r.astype(o.dtype)
        first = pl.program_id(0) == 0
        for r, o in zip(res[n_ro:], outs[n_ro:]):
            _accumulate(o, r, first)

    in_specs = [pl.BlockSpec((tm, w), lambda i, cb=cb: (i, cb)) for _, w, cb in row_ins]
    in_specs += [pl.BlockSpec(a.shape, lambda i: (0, 0)) for a in bcast_ins]
    in_specs += [pl.BlockSpec(memory_space=pl.ANY)] * len(deps)
    out_specs = [pl.BlockSpec((tm, w), lambda i: (i, 0)) for w, _ in row_outs]
    out_specs += [pl.BlockSpec((1, w), lambda i: (0, 0)) for w in acc_outs]
    out_shape = [jax.ShapeDtypeStruct((n_rows, w), dt) for w, dt in row_outs]
    out_shape += [jax.ShapeDtypeStruct((1, w), F32) for w in acc_outs]
    blk = sum(_nbytes((tm, w), a.dtype) for a, w, _ in row_ins) + sum(_nbytes((tm, w), dt) for w, dt in row_outs)
    return pl.pallas_call(
        body, name=name, grid=(n_rows // tm,), in_specs=in_specs, out_specs=out_specs, out_shape=out_shape,
        compiler_params=pltpu.CompilerParams(
            dimension_semantics=("arbitrary" if acc_outs else "parallel",), vmem_limit_bytes=_vmem_limit(3 * blk)),
    )(*_in_hbm(*[a for a, _, _ in row_ins], *bcast_ins), *deps)


def _accumulate(o_ref, part, first):
    @pl.when(first)
    def _():
        o_ref[...] = part

    @pl.when(jnp.logical_not(first))
    def _():
        o_ref[...] += part


_MM_DIMS = {"nn": ((1,), (0,)), "nt": ((1,), (1,)), "tn": ((0,), (0,))}


def _matmul(pairs, mode, out_dtype, name, tm, tn, tk, deps=()):
    a0, b0 = pairs[0]
    if mode == "tn":
        kk, m = a0.shape
    else:
        m, kk = a0.shape
    n = b0.shape[0] if mode == "nt" else b0.shape[1]
    assert m % tm == 0 and n % tn == 0 and kk % tk == 0, (name, m, n, kk)
    nk = kk // tk
    n_pairs = len(pairs)
    dims = _MM_DIMS[mode]
    n_in = 2 * n_pairs + len(deps)

    def body(*refs):
        o_ref = refs[n_in]
        part = None
        for p in range(n_pairs):
            d = _dot(refs[2 * p][...].astype(BF16), refs[2 * p + 1][...].astype(BF16), dims)
            part = d if part is None else part + d
        if nk == 1:
            o_ref[...] = part.astype(o_ref.dtype)
            return
        acc = refs[n_in + 1]
        k = pl.program_id(2)

        @pl.when(k == 0)
        def _():
            acc[...] = part

        @pl.when(k > 0)
        def _():
            acc[...] += part

        @pl.when(k == nk - 1)
        def _():
            o_ref[...] = acc[...].astype(o_ref.dtype)

    if mode == "tn":
        a_spec = pl.BlockSpec((tk, tm), lambda i, j, k: (k, i))
    else:
        a_spec = pl.BlockSpec((tm, tk), lambda i, j, k: (i, k))
    if mode == "nt":
        b_spec = pl.BlockSpec((tn, tk), lambda i, j, k: (j, k))
    else:
        b_spec = pl.BlockSpec((tk, tn), lambda i, j, k: (k, j))
    blk = sum(_nbytes((tm, tk), a.dtype) + _nbytes((tk, tn), b.dtype) for a, b in pairs) + 2 * _nbytes((tm, tn), F32)
    flat = [a for pair in pairs for a in pair]
    return pl.pallas_call(
        body, name=name, grid=(m // tm, n // tn, nk),
        in_specs=[a_spec, b_spec] * n_pairs + [pl.BlockSpec(memory_space=pl.ANY)] * len(deps),
        out_specs=pl.BlockSpec((tm, tn), lambda i, j, k: (i, j)),
        out_shape=jax.ShapeDtypeStruct((m, n), out_dtype),
        scratch_shapes=[] if nk == 1 else [pltpu.VMEM((tm, tn), F32)],
        compiler_params=pltpu.CompilerParams(
            dimension_semantics=("parallel", "parallel", "arbitrary"), vmem_limit_bytes=_vmem_limit(blk)),
    )(*flat, *deps)


def _matmul_rowwise(pairs, fn, name, tm, row_ins, bcast_ins, row_outs, acc_outs=(), deps=()):
    m = pairs[0][0].shape[0]
    n_mm, n_in = 2 * len(pairs), len(row_ins) + len(bcast_ins)
    n_ro = len(row_outs)

    def body(*refs):
        prod = None
        for p in range(len(pairs)):
            part = _dot_nn(refs[2 * p][...].astype(BF16), refs[2 * p + 1][...].astype(BF16))
            prod = part if prod is None else prod + part
        res = fn(prod, *[r[...] for r in refs[n_mm:n_mm + n_in]])
        outs = refs[n_mm + n_in + len(deps):]
        for r, o in zip(res[:n_ro], outs[:n_ro]):
            o[...] = r.astype(o.dtype)
        first = pl.program_id(0) == 0
        for r, o in zip(res[n_ro:], outs[n_ro:]):
            _accumulate(o, r, first)

    in_specs = []
    for a, b in pairs:
        in_specs += [pl.BlockSpec((tm, a.shape[1]), lambda i: (i, 0)),
                     pl.BlockSpec(b.shape, lambda i: (0, 0), pipeline_mode=pl.Buffered(1))]
    in_specs += [pl.BlockSpec((tm, w), lambda i, cb=cb: (i, cb)) for _, w, cb in row_ins]
    in_specs += [pl.BlockSpec(a.shape, lambda i: (0, 0)) for a in bcast_ins]
    in_specs += [_ANY] * len(deps)
    out_specs = [pl.BlockSpec((tm, w), lambda i: (i, 0)) for w, _ in row_outs]
    out_specs += [pl.BlockSpec((1, w), lambda i: (0, 0)) for w in acc_outs]
    out_shape = [jax.ShapeDtypeStruct((m, w), dt) for w, dt in row_outs]
    out_shape += [jax.ShapeDtypeStruct((1, w), F32) for w in acc_outs]
    blk = sum(_nbytes((tm, a.shape[1]), a.dtype) + _nbytes(b.shape, b.dtype) // 2 for a, b in pairs)
    blk += sum(_nbytes((tm, w), a.dtype) for a, w, _ in row_ins) + sum(_nbytes((tm, w), dt) for w, dt in row_outs)
    return pl.pallas_call(
        body, name=name, grid=(m // tm,), in_specs=in_specs, out_specs=out_specs, out_shape=out_shape,
        compiler_params=pltpu.CompilerParams(dimension_semantics=("arbitrary",), vmem_limit_bytes=_vmem_limit(blk)),
    )(*[a for pair in pairs for a in pair], *[a for a, _, _ in row_ins], *bcast_ins, *deps)


def _rms_scale(x):
    return lax.rsqrt(jnp.mean(x * x, axis=-1, keepdims=True) + EPS)


def _rms_bwd(xin, dyn, g):
    r = _rms_scale(xin)
    u = dyn * g
    dx = r * u - xin * (r * r * r) * jnp.mean(u * xin, axis=-1, keepdims=True)
    dg = jnp.sum(dyn * xin * r, axis=0, keepdims=True)
    return dx, dg


def _mesh_pos():
    return lax.axis_index("x"), lax.axis_index("y"), lax.axis_index("c")


def _all_gather(xs, name, deps=()):
    n = len(xs)

    def body(*refs):
        x_refs, out_refs = refs[:n], refs[n + len(deps):2 * n + len(deps)]
        send_sems, recv_sems, local_sems = refs[2 * n + len(deps):]
        mx, my, mc = _mesh_pos()
        me, sib = (mx, my, mc), (mx, my, 1 - mc)
        chips = [(1 - mx, my), (mx, 1 - my), (1 - mx, 1 - my)]

        def slot(a, dev):
            px, py, pc = dev
            return out_refs[a].at[4 * px + 2 * py + pc]

        def copy(k, a, block, to, src=None):
            return pltpu.make_async_remote_copy(
                src_ref=slot(a, block) if src is None else src, dst_ref=slot(a, block),
                send_sem=send_sems.at[a * 7 + k], recv_sem=recv_sems.at[a * 7 + k],
                device_id=to, device_id_type=MESH)

        mine = [pltpu.make_async_copy(x_refs[a], slot(a, me), local_sems.at[a]) for a in range(n)]
        for cp in mine:
            cp.start()
        first = []
        for a in range(n):
            first.append(copy(0, a, me, sib, x_refs[a]))
            first += [copy(1 + j, a, me, (*chip, mc), x_refs[a]) for j, chip in enumerate(chips)]
        for cp in first:
            cp.start()
        passed = []
        for a in range(n):
            for j, chip in enumerate(chips):
                copy(1 + j, a, (*chip, mc), me).wait_recv()
                fwd = copy(4 + j, a, (*chip, mc), sib)
                fwd.start()
                passed.append(fwd)
        for a in range(n):
            copy(0, a, sib, me).wait_recv()
            for j, chip in enumerate(chips):
                copy(4 + j, a, (*chip, 1 - mc), me).wait_recv()
        for cp in first + passed:
            cp.wait_send()
        for cp in mine:
            cp.wait()

    hbm = pl.BlockSpec(memory_space=pl.ANY)
    return pl.pallas_call(
        body, name=name,
        out_shape=[jax.ShapeDtypeStruct((N_DEV,) + x.shape, x.dtype) for x in xs],
        in_specs=[hbm] * (n + len(deps)), out_specs=[hbm] * n,
        scratch_shapes=[pltpu.SemaphoreType.DMA((7 * n,)), pltpu.SemaphoreType.DMA((7 * n,)),
                        pltpu.SemaphoreType.DMA((n,))],
    )(*xs, *deps)


_HBM = pl.BlockSpec(memory_space=pltpu.HBM)
_SEM = pl.BlockSpec(memory_space=pltpu.SEMAPHORE)
_ANY = pl.BlockSpec(memory_space=pl.ANY)
_DATAFLOW = pltpu.SideEffectType.DATAFLOW_SIDE_EFFECTING


def _flip_peer(flip):
    mx, my, mc = _mesh_pos()
    return (1 - mx if flip & 2 else mx, 1 - my if flip & 1 else my, mc)


def _remote(src, dst, send_sems, recv_sems, k, peer):
    return pltpu.make_async_remote_copy(src_ref=src, dst_ref=dst, send_sem=send_sems.at[k], recv_sem=recv_sems.at[k],
                                        device_id=peer, device_id_type=MESH)


def _gather_chips_copies(srcs, lands, send_sems, recv_sems):
    mx, my, mc = _mesh_pos()
    me = 4 * mx + 2 * my + mc
    return [_remote(srcs[a], lands[a].at[me], send_sems, recv_sems, 3 * a + flip - 1, _flip_peer(flip))
            for a in range(len(srcs)) for flip in (1, 2, 3)]


def _gather_sibling_copies(srcs, lands, send_sems, recv_sems):
    mx, my, mc = _mesh_pos()
    return [_remote(lands[a].at[2 * k + mc], lands[a].at[2 * k + mc], send_sems, recv_sems, 4 * a + k, (mx, my, 1 - mc))
            for a in range(len(lands)) for k in range(4)]


def _scatter_sibling_copies(srcs, lands, send_sems, recv_sems):
    mx, my, mc = _mesh_pos()
    return [_remote(srcs[a].at[k, 1 - mc], lands[a].at[k], send_sems, recv_sems, 4 * a + k, (mx, my, 1 - mc))
            for a in range(len(srcs)) for k in range(4)]


def _scatter_chips_copies(srcs, lands, send_sems, recv_sems):
    mx, my, _ = _mesh_pos()
    k0 = 2 * mx + my
    return [_remote(srcs[a].at[jnp.bitwise_xor(k0, flip)], lands[a].at[flip - 1], send_sems, recv_sems,
                    3 * a + flip - 1, _flip_peer(flip))
            for a in range(len(srcs)) for flip in (1, 2, 3)]


class _Exchange:
    def __init__(self, copies, n_src, send_sems, recv_sems, thru, token):
        self.copies, self.n_src, self.send_sems, self.recv_sems, self.thru, self.token = (
            copies, n_src, send_sems, recv_sems, thru, token)


def _exchange_start(name, copies, srcs, lands, n_copies, after=()):
    bufs = list(srcs) + list(lands)
    nb, ns = len(bufs), len(srcs)

    def body(*refs):
        send_sems, recv_sems = refs[nb + len(after)], refs[nb + len(after) + 1]
        for cp in copies(refs[:ns], refs[ns:nb], send_sems, recv_sems):
            cp.start()
        refs[-1][...] = jnp.zeros_like(refs[-1])

    out = pl.pallas_call(
        body, name=name,
        out_shape=(pltpu.SemaphoreType.DMA((n_copies,)), pltpu.SemaphoreType.DMA((n_copies,)),
                   *[pltpu.HBM(b.shape, b.dtype) for b in bufs], jax.ShapeDtypeStruct((SUBLANES, LANES), F32)),
        in_specs=[_HBM] * nb + [_ANY] * len(after),
        out_specs=(_SEM, _SEM, *[_HBM] * nb, pl.BlockSpec(memory_space=pltpu.VMEM)),
        input_output_aliases={i: 2 + i for i in range(nb)},
        compiler_params=pltpu.CompilerParams(has_side_effects=_DATAFLOW),
    )(*[pltpu.with_memory_space_constraint(b, pltpu.HBM) for b in bufs], *after)
    return _Exchange(copies, ns, out[0], out[1], list(out[2:2 + nb]), out[-1])


def _exchange_wait(name, ex, after):
    nb, ns = len(ex.thru), ex.n_src

    def body(*refs):
        for cp in ex.copies(refs[:ns], refs[ns:nb], refs[nb], refs[nb + 1]):
            cp.wait_send()
            cp.wait_recv()

    out = pl.pallas_call(
        body, name=name, out_shape=tuple(pltpu.HBM(b.shape, b.dtype) for b in ex.thru),
        in_specs=[_HBM] * nb + [_SEM, _SEM] + [_ANY] * len(after), out_specs=tuple([_HBM] * nb),
        input_output_aliases={i: i for i in range(nb)},
        compiler_params=pltpu.CompilerParams(has_side_effects=_DATAFLOW),
    )(*ex.thru, ex.send_sems, ex.recv_sems, *after)
    return list(out[:ns]), list(out[ns:])


def _col_tile(r, c):
    return next(t for t in (1024, 512, 256, 128) if c % t == 0 and (r * t * 4 <= 2**20 or t == 128))


def _add_sibling(g4, recv, pos, name):
    _, _, r, c = g4.shape
    tc = _col_tile(r, c)

    def body(pos_ref, g_ref, r_ref, o16_ref, mine_ref):
        s = g_ref[0, 0] + r_ref[0]
        o16_ref[0] = s.astype(BF16)

        @pl.when(pl.program_id(1) == pos_ref[1])
        def _():
            mine_ref[...] = s

    slot = pl.BlockSpec((1, r, tc), lambda j, k, pos_ref: (k, 0, j))
    return pl.pallas_call(
        body, name=name,
        out_shape=[jax.ShapeDtypeStruct((4, r, c), BF16), jax.ShapeDtypeStruct((r, c), F32)],
        grid_spec=pltpu.PrefetchScalarGridSpec(
            num_scalar_prefetch=1, grid=(c // tc, 4),
            in_specs=[pl.BlockSpec((1, 1, r, tc), lambda j, k, pos_ref: (k, pos_ref[0], 0, j)), slot],
            out_specs=[slot, pl.BlockSpec((r, tc), lambda j, k, pos_ref: (0, j))]),
        compiler_params=pltpu.CompilerParams(dimension_semantics=("parallel", "arbitrary")),
    )(pos, *_in_hbm(g4, recv))


class _ReduceScatter:
    def __init__(self, tag, grads_t, pos):
        self.tag, self.pos, self.names = tag, pos, list(grads_t)
        g4s = [g.reshape(4, 2, g.size // (N_DEV * g.shape[-1]), g.shape[-1]) for g in grads_t.values()]
        lands = [lax.empty((4,) + g.shape[2:], F32) for g in g4s]
        self.ex = _exchange_start(f"rs_{tag}_sibling_start", _scatter_sibling_copies, g4s, lands, 4 * len(g4s))
        self.token = self.ex.token

    def start_chips(self, after):
        g4s, from_sibling = _exchange_wait(f"rs_{self.tag}_sibling_wait", self.ex, after)
        parts = [_add_sibling(g4, rv, self.pos, f"rs_add_sibling_{k}")
                 for k, g4, rv in zip(self.names, g4s, from_sibling)]
        self.mine = [mine for _, mine in parts]
        p16s = [p16 for p16, _ in parts]
        lands = [lax.empty((3,) + p.shape[1:], BF16) for p in p16s]
        self.ex = _exchange_start(f"rs_{self.tag}_chips_start", _scatter_chips_copies, p16s, lands, 3 * len(p16s))
        self.token = self.ex.token

    def finish(self, after):
        _, from_chips = _exchange_wait(f"rs_{self.tag}_chips_wait", self.ex, after)
        return dict(zip(self.names, zip(self.mine, from_chips)))


def _rope_tables():
    positions = np.arange(SEQ, dtype=np.float32)
    inv_freq = np.power(np.float32(ROPE_THETA), -np.arange(0, ROPE_DIM, 2, dtype=np.float32) / np.float32(ROPE_DIM))
    ang = (positions[:, None] * inv_freq[None, :]).astype(np.float32)
    cos, sin = np.cos(ang).astype(np.float32), np.sin(ang).astype(np.float32)
    ones = np.ones((SEQ, HEAD_DIM - ROPE_DIM), np.float32)
    zeros8 = np.zeros((SEQ, ROPE_HALF), np.float32)
    zeros = np.zeros((SEQ, HEAD_DIM - ROPE_DIM), np.float32)
    c_head = np.concatenate([cos, cos, ones], axis=1)
    s1_head = np.concatenate([-sin, zeros8, zeros], axis=1)
    s2_head = np.concatenate([zeros8, sin, zeros], axis=1)
    return tuple(jnp.asarray(np.concatenate([t, t], axis=1)) for t in (c_head, s1_head, s2_head))


def _rope_apply(x, c, s1, s2):
    w = x.shape[1]
    return x * c + pltpu.roll(x, w - ROPE_HALF, 1) * s1 + pltpu.roll(x, ROPE_HALF, 1) * s2


def _rope_apply_t(dy, c, s1, s2):
    w = dy.shape[1]
    return dy * c + pltpu.roll(dy * s1, ROPE_HALF, 1) + pltpu.roll(dy * s2, w - ROPE_HALF, 1)


def _dil_prev_limit(has_prev):
    return jnp.where(has_prev, 0, BLOCK)


def _dil_valid(limit):
    row = lax.broadcasted_iota(jnp.int32, (BLOCK, 2 * BLOCK), 0)
    col = lax.broadcasted_iota(jnp.int32, (BLOCK, 2 * BLOCK), 1)
    dist = col - row
    return jnp.logical_and(dist >= jnp.where(col < BLOCK, limit, -BLOCK), dist <= BLOCK)


def _upper_half():
    return lax.broadcasted_iota(jnp.int32, (1, LANES), 1) >= HEAD_DIM


def _dil_rows(n, d):
    per = N_BLOCKS // d
    r, lb = n // per, n % per

    def rows(b):
        start = b * (BLOCK * d) + r
        return pl.ds(pl.multiple_of(start, BLOCK), BLOCK) if d == 1 else pl.ds(start, BLOCK, stride=d)

    return rows(lb), rows(jnp.maximum(lb - 1, 0)), lb > 0


def _dil_specs(g):
    def col(base):
        return pl.BlockSpec((SEQ, LANES), lambda p: (0, base // LANES + 2 * g + p))

    table = pl.BlockSpec((SEQ, LANES), lambda p: (0, 0))
    return [col(COL_QA), col(COL_KA), col(COL_VA)], [table] * 3


def _store_columns(blocks, dproj_ref, cols, sem):
    copies = [pltpu.make_async_copy(b, dproj_ref.at[:, pl.ds(pl.multiple_of(c * LANES, LANES), LANES)], sem.at[i])
              for i, (b, c) in enumerate(zip(blocks, cols))]
    for cp in copies:
        cp.start()
    for cp in copies:
        cp.wait()


def _dil_fwd(g, proj, tables):
    d = DILATIONS[g]
    one_block = d == N_BLOCKS

    def body(q_ref, k_ref, v_ref, c_ref, s1_ref, s2_ref, o_ref, lse_ref):
        upper = _upper_half()

        def roped(ref, rows):
            return _rope_apply(ref[rows, :], c_ref[rows, :], s1_ref[rows, :], s2_ref[rows, :])

        def block(n, carry):
            rows, prev, has_prev = _dil_rows(n, d)
            qb = (roped(q_ref, rows) * QK_SCALE).astype(BF16)
            kw, vw = roped(k_ref, rows).astype(BF16), v_ref[rows, :].astype(BF16)
            if one_block:
                row = lax.broadcasted_iota(jnp.int32, (BLOCK, BLOCK), 0)
                valid = lax.broadcasted_iota(jnp.int32, (BLOCK, BLOCK), 1) <= row
            else:
                kw = jnp.concatenate([roped(k_ref, prev).astype(BF16), kw], axis=0)
                vw = jnp.concatenate([v_ref[prev, :].astype(BF16), vw], axis=0)
                valid = _dil_valid(_dil_prev_limit(has_prev))
            outs, lses = [], []
            for head_mask in (jnp.logical_not(upper), upper):
                s = jnp.where(valid, _dot_nt(qb, jnp.where(head_mask, kw, 0)), NEG_INF)
                m = jnp.max(s, axis=-1, keepdims=True)
                p = jnp.exp(s - m)
                den = jnp.sum(p, axis=-1, keepdims=True)
                outs.append(_dot_nn((p * (1.0 / den)).astype(BF16), jnp.where(head_mask, vw, 0)))
                lses.append(m + jnp.log(den))
            o_ref[rows, :] = outs[0] + outs[1]
            lse_ref[rows, :] = jnp.where(upper, lses[1], lses[0])
            return carry

        lax.fori_loop(0, N_BLOCKS, block, 0, unroll=8)

    qkv, tabs = _dil_specs(g)
    out = pl.BlockSpec((SEQ, LANES), lambda p: (0, p))
    return pl.pallas_call(
        body, name=f"dil_attn_fwd_{g}", grid=(2,), in_specs=qkv + tabs, out_specs=[out, out],
        out_shape=[jax.ShapeDtypeStruct((SEQ, DIL_OUT_WIDTH), F32)] * 2,
        compiler_params=pltpu.CompilerParams(dimension_semantics=("parallel",)),
    )(*_in_hbm(proj, proj, proj, *tables))


def _dil_bwd(g, proj, tables, do, lse, c, dproj, deps=()):
    d = DILATIONS[g]
    one_block = d == N_BLOCKS

    def body(q_ref, k_ref, v_ref, c_ref, s1_ref, s2_ref, do_ref, lse_ref, cc_ref, dproj_in, *rest):
        dproj_ref, dq_acc, dk_acc, dv_acc, dq_out, dk_out, dv_out, sem = rest[len(deps):]
        upper = _upper_half()
        dk_acc[...] = jnp.zeros_like(dk_acc)
        dv_acc[...] = jnp.zeros_like(dv_acc)

        def roped(ref, rows):
            return _rope_apply(ref[rows, :], c_ref[rows, :], s1_ref[rows, :], s2_ref[rows, :])

        def block(n, carry):
            rows, prev, has_prev = _dil_rows(n, d)
            qb = (roped(q_ref, rows) * QK_SCALE).astype(BF16)
            dob = do_ref[rows, :].astype(BF16)
            kw, vw = roped(k_ref, rows).astype(BF16), v_ref[rows, :].astype(BF16)
            if one_block:
                row = lax.broadcasted_iota(jnp.int32, (BLOCK, BLOCK), 0)
                valid = lax.broadcasted_iota(jnp.int32, (BLOCK, BLOCK), 1) <= row
            else:
                kw = jnp.concatenate([roped(k_ref, prev).astype(BF16), kw], axis=0)
                vw = jnp.concatenate([v_ref[prev, :].astype(BF16), vw], axis=0)
                valid = _dil_valid(_dil_prev_limit(has_prev))
            lse_t, c_t = lse_ref[rows, :], cc_ref[rows, :]
            dq, dk, dv = None, None, None
            for e, head_mask in enumerate((jnp.logical_not(upper), upper)):
                km, vm = jnp.where(head_mask, kw, 0), jnp.where(head_mask, vw, 0)
                lse_col = lse_t[:, e * HEAD_DIM:e * HEAD_DIM + 1]
                c_col = c_t[:, e * HEAD_DIM:e * HEAD_DIM + 1]
                p = jnp.where(valid, jnp.exp(_dot_nt(qb, km) - lse_col), 0.0)
                ds = (p * (_dot_nt(dob, vm) + c_col)).astype(BF16)
                parts = (_dot_nn(ds, km), _dot_tn(ds, jnp.where(head_mask, qb, 0)),
                         _dot_tn(p.astype(BF16), jnp.where(head_mask, dob, 0)))
                dq, dk, dv = parts if dq is None else (dq + parts[0], dk + parts[1], dv + parts[2])
            dq_acc[rows, :] = dq * QK_SCALE
            if one_block:
                dk_acc[rows, :] += dk
                dv_acc[rows, :] += dv
            else:
                dk_acc[prev, :] += dk[:BLOCK]
                dv_acc[prev, :] += dv[:BLOCK]
                dk_acc[rows, :] += dk[BLOCK:]
                dv_acc[rows, :] += dv[BLOCK:]
            return carry

        lax.fori_loop(0, N_BLOCKS, block, 0, unroll=8)
        tabs = (c_ref[...], s1_ref[...], s2_ref[...])
        dq_out[...] = _rope_apply_t(dq_acc[...], *tabs).astype(BF16)
        dk_out[...] = _rope_apply_t(dk_acc[...], *tabs).astype(BF16)
        dv_out[...] = dv_acc[...].astype(BF16)
        pair = 2 * g + pl.program_id(0)
        _store_columns((dq_out, dk_out, dv_out), dproj_ref,
                       [base // LANES + pair for base in (COL_QA, COL_KA, COL_VA)], sem)

    qkv, tabs = _dil_specs(g)
    tok = pl.BlockSpec((SEQ, LANES), lambda p: (0, p))
    return pl.pallas_call(
        body, name=f"dil_attn_bwd_{g}", grid=(2,),
        in_specs=qkv + tabs + [tok, tok, tok, _ANY] + [_ANY] * len(deps), out_specs=_ANY,
        out_shape=jax.ShapeDtypeStruct(dproj.shape, dproj.dtype),
        scratch_shapes=[pltpu.VMEM((SEQ, LANES), F32)] * 3 + [pltpu.VMEM((SEQ, LANES), BF16)] * 3
        + [pltpu.SemaphoreType.DMA((3,))],
        input_output_aliases={9: 0},
        compiler_params=pltpu.CompilerParams(dimension_semantics=("arbitrary",)),
    )(*_in_hbm(proj, proj, proj, *tables, do, lse, c, dproj), *deps)


def _group_weights(l0, l1, l2):
    m = jnp.maximum(jnp.maximum(l0, l1), l2)
    e0, e1, e2 = jnp.exp(l0 - m), jnp.exp(l1 - m), jnp.exp(l2 - m)
    tot = e0 + e1 + e2
    return e0 / tot, e1 / tot, e2 / tot


def _dil_combine(outs, lses, deps=()):
    def fn(o0, o1, o2, l0, l1, l2):
        w0, w1, w2 = _group_weights(l0, l1, l2)
        return w0 * o0 + w1 * o1 + w2 * o2

    w = DIL_OUT_WIDTH
    return _rowwise(fn, "dil_combine", SEQ, 512, [(a, w, 0) for a in list(outs) + list(lses)], [], [(w, F32)],
                    deps=deps)[0]


def _dil_combine_bwd(d_out, outs, lses, deps=()):
    w = DIL_OUT_WIDTH

    def fn(d, o0, o1, o2, l0, l1, l2):
        row = lax.broadcasted_iota(jnp.int32, (w, w), 0) // HEAD_DIM
        col = lax.broadcasted_iota(jnp.int32, (w, w), 1) // HEAD_DIM
        same_head = jnp.where(row == col, 1.0, 0.0).astype(BF16)
        ws = _group_weights(l0, l1, l2)
        dws = [_dot3_nn(d * og, same_head) for og in (o0, o1, o2)]
        mean = ws[0] * dws[0] + ws[1] * dws[1] + ws[2] * dws[2]
        return tuple(wg * d for wg in ws) + tuple(-wg * mean for wg in ws)

    res = _rowwise(fn, "dil_combine_bwd", SEQ, 256, [(a, w, 0) for a in [d_out] + list(outs) + list(lses)], [],
                   [(w, F32)] * 6, deps=deps)
    return res[:3], res[3:]


def _log1p(e):
    u = 1.0 + e
    return jnp.where(u == 1.0, e, jnp.log(u) * (e / (u - 1.0)))


def _fox_gate(proj, b_pad, deps=()):
    def body(f_ref, b_ref, *rest):
        o_ref = rest[-1]
        z = f_ref[...] + b_ref[...]
        logf = (jnp.minimum(z, 0.0) - _log1p(jnp.exp(-jnp.abs(z)))).T[:F_ROWS]
        row = lax.broadcasted_iota(jnp.int32, (BLOCK, BLOCK), 0)
        col = lax.broadcasted_iota(jnp.int32, (BLOCK, BLOCK), 1)
        before = jnp.where(row <= col, 1.0, 0.0).astype(BF16)
        carry = jnp.zeros((F_ROWS, 1), F32)
        for blk in range(N_BLOCKS):
            run = _dot3_nn(logf[:, blk * BLOCK:(blk + 1) * BLOCK], before) + carry
            o_ref[:, blk * BLOCK:(blk + 1) * BLOCK] = run
            carry = run[:, BLOCK - 1:BLOCK]

    return pl.pallas_call(
        body, name="fox_gate", grid=(1,),
        in_specs=[pl.BlockSpec((SEQ, LANES), lambda i: (0, COL_F // LANES)), pl.BlockSpec((1, LANES), lambda i: (0, 0))]
        + [_ANY] * len(deps),
        out_specs=pl.BlockSpec((F_ROWS, SEQ), lambda i: (0, 0)),
        out_shape=jax.ShapeDtypeStruct((F_ROWS, SEQ), F32),
    )(*_in_hbm(proj, b_pad), *deps)


def _fox_gate_bwd(d_cum, proj, b_pad, dproj):
    def body(d_ref, f_ref, b_ref, dproj_ref, dz_ref, db_ref):
        row = lax.broadcasted_iota(jnp.int32, (BLOCK, BLOCK), 0)
        col = lax.broadcasted_iota(jnp.int32, (BLOCK, BLOCK), 1)
        after = jnp.where(row >= col, 1.0, 0.0).astype(BF16)
        carry = jnp.zeros((F_ROWS, 1), F32)
        parts = [None] * N_BLOCKS
        for blk in reversed(range(N_BLOCKS)):
            run = _dot3_nn(d_ref[:, blk * BLOCK:(blk + 1) * BLOCK], after) + carry
            parts[blk] = run
            carry = run[:, 0:1]
        dlogf = jnp.concatenate(parts, axis=1)
        dlogf = jnp.concatenate([dlogf, jnp.zeros((LANES - F_ROWS, SEQ), F32)], axis=0).T
        dz = dlogf * _sigmoid(-(f_ref[...] + b_ref[...]))
        dz_ref[...] = dz.astype(BF16)
        db_ref[...] = jnp.sum(dz, axis=0, keepdims=True)

    f_cols = pl.BlockSpec((SEQ, LANES), lambda i: (0, COL_F // LANES))
    return pl.pallas_call(
        body, name="fox_gate_bwd", grid=(1,),
        in_specs=[pl.BlockSpec((F_ROWS, SEQ), lambda i: (0, 0)), f_cols, pl.BlockSpec((1, LANES), lambda i: (0, 0)), _ANY],
        out_specs=[f_cols, pl.BlockSpec((1, LANES), lambda i: (0, 0))],
        out_shape=[jax.ShapeDtypeStruct(dproj.shape, dproj.dtype), jax.ShapeDtypeStruct((1, LANES), F32)],
        input_output_aliases={3: 0},
    )(*_in_hbm(d_cum, proj, b_pad, dproj))


FOX_TILE = 512
FOX_TILES = SEQ // FOX_TILE


def _row_to_col(row):
    n = row.shape[1]
    eye = lax.broadcasted_iota(jnp.int32, (n, n), 0) == lax.broadcasted_iota(jnp.int32, (n, n), 1)
    return jnp.sum(jnp.where(eye, row, 0.0), axis=1, keepdims=True)


def _fox_scores(q_tile, km, f_row, i):
    t = FOX_TILE
    ext = (i + 1) * t
    f_q = _row_to_col(f_row[:, i * t:(i + 1) * t])
    s = _dot_nt(q_tile, km[:ext]) + (f_q - f_row[:, :ext])
    row = lax.broadcasted_iota(jnp.int32, (t, ext), 0) + i * t
    col = lax.broadcasted_iota(jnp.int32, (t, ext), 1)
    return s, col <= row


def _fox_specs():
    qkv = [pl.BlockSpec((SEQ, LANES), lambda p, base=base: (0, base // LANES + p)) for base in (COL_QB, COL_KB, COL_VB)]
    return qkv, pl.BlockSpec((F_ROWS, SEQ), lambda p: (0, 0))


def _fox_fwd(proj, f_rows):
    t = FOX_TILE

    def body(q_ref, k_ref, v_ref, f_ref, o_ref, lse_ref):
        pair = pl.program_id(0)
        upper = _upper_half()
        masks = (jnp.logical_not(upper), upper)
        k16, v16 = k_ref[...].astype(BF16), v_ref[...].astype(BF16)
        kms = [jnp.where(hm, k16, 0) for hm in masks]
        vms = [jnp.where(hm, v16, 0) for hm in masks]
        f_row = [f_ref[pl.ds(2 * pair + e, 1), :] for e in range(2)]
        for i in range(FOX_TILES):
            q_tile = (q_ref[i * t:(i + 1) * t, :] * QK_SCALE).astype(BF16)
            outs, lses = [], []
            for e in range(2):
                s, causal = _fox_scores(q_tile, kms[e], f_row[e], i)
                s = jnp.where(causal, s, NEG_INF)
                m = jnp.max(s, axis=-1, keepdims=True)
                p = jnp.exp(s - m)
                den = jnp.sum(p, axis=-1, keepdims=True)
                outs.append(_dot_nn((p * (1.0 / den)).astype(BF16), vms[e][:(i + 1) * t]))
                lses.append(m + jnp.log(den))
            o_ref[i * t:(i + 1) * t, :] = outs[0] + outs[1]
            lse_ref[i * t:(i + 1) * t, :] = jnp.where(upper, lses[1], lses[0])

    qkv, f_spec = _fox_specs()
    tok = pl.BlockSpec((SEQ, LANES), lambda p: (0, p))
    return pl.pallas_call(
        body, name="fox_attn_fwd", grid=(FOX_WIDTH // LANES,),
        in_specs=qkv + [f_spec], out_specs=[tok, tok],
        out_shape=[jax.ShapeDtypeStruct((SEQ, FOX_WIDTH), F32)] * 2,
        compiler_params=pltpu.CompilerParams(
            dimension_semantics=("parallel",), vmem_limit_bytes=_vmem_limit(8 * t * SEQ * 4)),
    )(*_in_hbm(proj, proj, proj, f_rows))


def _fox_bwd(proj, do, lse, f_rows, dproj):
    t = FOX_TILE

    def body(q_ref, k_ref, v_ref, f_ref, do_ref, lse_ref, dproj_in, dproj_ref, df_ref, dk_acc, dv_acc,
             dq_out, dk_out, dv_out, sem):
        pair = pl.program_id(0)
        upper = _upper_half()
        masks = (jnp.logical_not(upper), upper)
        k16, v16 = k_ref[...].astype(BF16), v_ref[...].astype(BF16)
        kms = [jnp.where(hm, k16, 0) for hm in masks]
        vms = [jnp.where(hm, v16, 0) for hm in masks]
        f_row = [f_ref[pl.ds(2 * pair + e, 1), :] for e in range(2)]
        dk_acc[...] = jnp.zeros_like(dk_acc)
        dv_acc[...] = jnp.zeros_like(dv_acc)
        df_ref[...] = jnp.zeros_like(df_ref)
        for i in range(FOX_TILES):
            ext = (i + 1) * t
            q_tile = (q_ref[i * t:(i + 1) * t, :] * QK_SCALE).astype(BF16)
            do_tile = do_ref[i * t:(i + 1) * t, :]
            lse_t = lse_ref[i * t:(i + 1) * t, :]
            dq = None
            for e in range(2):
                s, causal = _fox_scores(q_tile, kms[e], f_row[e], i)
                p = jnp.where(causal, jnp.exp(s - lse_t[:, e * HEAD_DIM:e * HEAD_DIM + 1]), 0.0)
                dp = _dot_nt(do_tile, vms[e][:ext])
                ds = p * (dp - jnp.sum(p * dp, axis=-1, keepdims=True))
                df_ref[0, e:e + 1, :ext] -= jnp.sum(ds, axis=0, keepdims=True)
                ds = ds.astype(BF16)
                part = _dot_nn(ds, kms[e][:ext])
                dq = part if dq is None else dq + part
                dk_acc[:ext, :] += _dot_tn(ds, jnp.where(masks[e], q_tile, 0))
                dv_acc[:ext, :] += _dot_tn(p.astype(BF16), jnp.where(masks[e], do_tile, 0))
            dq_out[i * t:(i + 1) * t, :] = (dq * QK_SCALE).astype(BF16)
        dk_out[...] = dk_acc[...].astype(BF16)
        dv_out[...] = dv_acc[...].astype(BF16)
        _store_columns((dq_out, dk_out, dv_out), dproj_ref, [base // LANES + pair for base in (COL_QB, COL_KB, COL_VB)],
                       sem)

    qkv, f_spec = _fox_specs()
    tok = pl.BlockSpec((SEQ, LANES), lambda p: (0, p))
    return pl.pallas_call(
        body, name="fox_attn_bwd", grid=(FOX_WIDTH // LANES,),
        in_specs=qkv + [f_spec, tok, tok, _ANY],
        out_specs=[_ANY, pl.BlockSpec((1, SUBLANES, SEQ), lambda p: (p, 0, 0))],
        out_shape=[jax.ShapeDtypeStruct(dproj.shape, dproj.dtype),
                   jax.ShapeDtypeStruct((FOX_WIDTH // LANES, SUBLANES, SEQ), F32)],
        scratch_shapes=[pltpu.VMEM((SEQ, LANES), F32)] * 2 + [pltpu.VMEM((SEQ, LANES), BF16)] * 3
        + [pltpu.SemaphoreType.DMA((3,))],
        input_output_aliases={6: 0},
        compiler_params=pltpu.CompilerParams(
            dimension_semantics=("arbitrary",), vmem_limit_bytes=_vmem_limit(10 * t * SEQ * 4)),
    )(*_in_hbm(proj, proj, proj, f_rows, do, lse, dproj))


MIX_TILE = 256


def _mix_out(out_a, out_b, proj, x, wt_pa, wt_pb, w_out, g_post, g_ffn_pre):
    tm = MIX_TILE

    def body(a_ref, b_ref, ga_ref, gb_ref, x_ref, wpa_ref, wpb_ref, wo_ref, g2_ref, g3_ref,
             merged_ref, mix_ref, x1_ref, h2_ref):
        ya = _dot_nn(a_ref[...].astype(BF16), wpa_ref[...])
        yb = _dot_nn(b_ref[...].astype(BF16), wpb_ref[...])
        merged = (_sigmoid(ga_ref[...]) * ya + _sigmoid(gb_ref[...]) * yb).astype(BF16)
        merged_ref[...] = merged
        mix = _dot_nn(merged, wo_ref[...])
        mix_ref[...] = mix
        x1 = x_ref[...] + mix * _rms_scale(mix) * g2_ref[...]
        x1_ref[...] = x1
        h2_ref[...] = (x1 * _rms_scale(x1) * g3_ref[...]).astype(BF16)

    def rows(w, cb=0):
        return pl.BlockSpec((tm, w), lambda i, cb=cb: (i, cb))

    def whole(a):
        return pl.BlockSpec(a.shape, lambda i: (0, 0))

    d = D_MODEL
    blk = _nbytes((tm, d), F32) * 6 + sum(_nbytes(a.shape, BF16) for a in (wt_pa, wt_pb, w_out))
    return pl.pallas_call(
        body, name="mix_out", grid=(SEQ // tm,),
        in_specs=[rows(DIL_OUT_WIDTH), rows(FOX_WIDTH), rows(d, COL_GA // d), rows(d, COL_GB // d), rows(d),
                  whole(wt_pa), whole(wt_pb), whole(w_out), whole(g_post), whole(g_ffn_pre)],
        out_specs=[rows(d)] * 4,
        out_shape=[jax.ShapeDtypeStruct((SEQ, d), dt) for dt in (BF16, F32, F32, BF16)],
        compiler_params=pltpu.CompilerParams(dimension_semantics=("parallel",), vmem_limit_bytes=_vmem_limit(blk)),
    )(out_a, out_b, proj, proj, x, wt_pa, wt_pb, w_out, g_post, g_ffn_pre)


def _mix_out_bwd(dmix, out_a, out_b, proj, wt_pa, wt_pb, w_out, deps=()):
    tm = MIX_TILE

    def body(dm_ref, a_ref, b_ref, ga_ref, gb_ref, wpa_ref, wpb_ref, wo_ref, *rest):
        dproj_ref, dya_ref, dyb_ref, da_ref, db_ref = rest[len(deps):]
        dmerged = _dot_nt(dm_ref[...], wo_ref[...])
        ya = _dot_nn(a_ref[...].astype(BF16), wpa_ref[...])
        yb = _dot_nn(b_ref[...].astype(BF16), wpb_ref[...])
        sa, sb = _sigmoid(ga_ref[...]), _sigmoid(gb_ref[...])
        dproj_ref[:, COL_GA:COL_GA + D_MODEL] = (dmerged * ya * (sa * (1.0 - sa))).astype(BF16)
        dproj_ref[:, COL_GB:COL_GB + D_MODEL] = (dmerged * yb * (sb * (1.0 - sb))).astype(BF16)
        dproj_ref[:, COL_GB + D_MODEL:] = jnp.zeros((tm, COL_QA - COL_GB - D_MODEL), BF16)
        dya = (dmerged * sa).astype(BF16)
        dyb = (dmerged * sb).astype(BF16)
        dya_ref[...] = dya
        dyb_ref[...] = dyb
        da_ref[...] = _dot_nt(dya, wpa_ref[...])
        db_ref[...] = _dot_nt(dyb, wpb_ref[...]).astype(BF16)

    def rows(w, cb=0):
        return pl.BlockSpec((tm, w), lambda i, cb=cb: (i, cb))

    def whole(a):
        return pl.BlockSpec(a.shape, lambda i: (0, 0))

    d = D_MODEL
    blk = _nbytes((tm, d), F32) * 8 + sum(_nbytes(a.shape, BF16) for a in (wt_pa, wt_pb, w_out))
    return pl.pallas_call(
        body, name="mix_out_bwd", grid=(SEQ // tm,),
        in_specs=[rows(d), rows(DIL_OUT_WIDTH), rows(FOX_WIDTH), rows(d, COL_GA // d), rows(d, COL_GB // d),
                  whole(wt_pa), whole(wt_pb), whole(w_out)] + [_ANY] * len(deps),
        out_specs=[rows(COL_QA)] + [rows(d)] * 2 + [rows(DIL_OUT_WIDTH), rows(FOX_WIDTH)],
        out_shape=[jax.ShapeDtypeStruct((SEQ, PROJ_COLS), BF16)] + [jax.ShapeDtypeStruct((SEQ, d), BF16)] * 2
        + [jax.ShapeDtypeStruct((SEQ, DIL_OUT_WIDTH), F32), jax.ShapeDtypeStruct((SEQ, FOX_WIDTH), BF16)],
        compiler_params=pltpu.CompilerParams(dimension_semantics=("parallel",), vmem_limit_bytes=_vmem_limit(blk)),
    )(dmix, out_a, out_b, proj, proj, wt_pa, wt_pb, w_out, *deps)


FFN_TM, FFN_TN = 2048, 256


def _ffn_up(h2, wt_gate, wt_up):
    tm, tn = FFN_TM, FFN_TN

    def body(h_ref, wg_ref, wu_ref, gate_ref, up_ref, act_ref):
        gate = _dot_nt(h_ref[...], wg_ref[...])
        up = _dot_nt(h_ref[...], wu_ref[...])
        gate_ref[...] = gate
        up_ref[...] = up
        act_ref[...] = (gate * _sigmoid(gate) * up).astype(BF16)

    tile = pl.BlockSpec((tm, tn), lambda i, j: (i, j))
    w_spec = pl.BlockSpec((tn, D_MODEL), lambda i, j: (j, 0))
    return pl.pallas_call(
        body, name="ffn_up", grid=(SEQ // tm, D_FF // tn),
        in_specs=[pl.BlockSpec((tm, D_MODEL), lambda i, j: (i, 0)), w_spec, w_spec],
        out_specs=[tile, tile, tile],
        out_shape=[jax.ShapeDtypeStruct((SEQ, D_FF), dt) for dt in (F32, F32, BF16)],
        compiler_params=pltpu.CompilerParams(
            dimension_semantics=("parallel", "parallel"), vmem_limit_bytes=_vmem_limit(8 * 2**20)),
    )(h2, wt_gate, wt_up)


def _ffn_act_bwd(dff, w_down, gate, up):
    tm, tn = FFN_TM, FFN_TN

    def body(d_ref, wd_ref, gate_ref, up_ref, dgate_ref, dup_ref):
        dact = _dot_nt(d_ref[...], wd_ref[...])
        gate = gate_ref[...]
        sg = _sigmoid(gate)
        dgate_ref[...] = (dact * up_ref[...] * (sg * (1.0 + gate * (1.0 - sg)))).astype(BF16)
        dup_ref[...] = (dact * (gate * sg)).astype(BF16)

    tile = pl.BlockSpec((tm, tn), lambda i, j: (i, j))
    return pl.pallas_call(
        body, name="ffn_act_bwd", grid=(SEQ // tm, D_FF // tn),
        in_specs=[pl.BlockSpec((tm, D_MODEL), lambda i, j: (i, 0)), pl.BlockSpec((tn, D_MODEL), lambda i, j: (j, 0)),
                  tile, tile],
        out_specs=[tile, tile],
        out_shape=[jax.ShapeDtypeStruct((SEQ, D_FF), BF16)] * 2,
        compiler_params=pltpu.CompilerParams(
            dimension_semantics=("parallel", "parallel"), vmem_limit_bytes=_vmem_limit(8 * 2**20)),
    )(dff, w_down, gate, up)


EPILOGUE_TM = 512


def _loss_head(act, w_down, x1, target, g_post):
    def fn(ff, x1, tgt, g):
        r = _rms_scale(ff)
        nrm = ff * r
        err = (x1 + nrm * g) - tgt
        loss = 0.5 * jnp.sum(jnp.mean(err * err, axis=-1, keepdims=True), axis=0, keepdims=True)
        dy = err * (1.0 / D_MODEL)
        u = dy * g
        dff = r * u - ff * (r * r * r) * jnp.mean(u * ff, axis=-1, keepdims=True)
        return dy, dff, jnp.broadcast_to(loss, (1, LANES)), jnp.sum(dy * nrm, axis=0, keepdims=True)

    d = D_MODEL
    return _matmul_rowwise([(act, w_down)], fn, "ffn_down_loss", EPILOGUE_TM, [(x1, d, 0), (target, d, 0)], [g_post],
                           [(d, F32), (d, BF16)], [LANES, d])


def _post_ffn_bwd(dgate, wt_gate, dup, wt_up, x1, dy, mix, g_ffn_pre, g_mix_post, deps=()):
    def fn(dh2, x1, dy, mix, g3, g2):
        dx, dg3 = _rms_bwd(x1, dh2, g3)
        dx1 = dy + dx
        dmix, dg2 = _rms_bwd(mix, dx1, g2)
        return dx1, dmix, dg3, dg2

    d = D_MODEL
    return _matmul_rowwise([(dgate, wt_gate), (dup, wt_up)], fn, "ffn_up_bwd", EPILOGUE_TM,
                           [(x1, d, 0), (dy, d, 0), (mix, d, 0)], [g_ffn_pre, g_mix_post],
                           [(d, F32), (d, BF16)], [d, d], deps=deps)


def _input_bwd(dproj, wt_r, x, dx1, g_pre, deps=()):
    def fn(dh, x, dx1, g):
        dx, dg = _rms_bwd(x, dh, g)
        return dx1 + dx, dg

    d = D_MODEL
    return _matmul_rowwise([(dproj, wt_r)], fn, "in_proj_bwd", EPILOGUE_TM, [(x, d, 0), (dx1, d, 0)], [g_pre],
                           [(d, F32)], [d], deps=deps)


def _adam_math(w, g, m, v):
    m = ADAM_B1 * m + (1.0 - ADAM_B1) * g
    v = ADAM_B2 * v + (1.0 - ADAM_B2) * (g * g)
    m_hat = m / (1.0 - ADAM_B1 ** ADAM_STEP)
    v_hat = v / (1.0 - ADAM_B2 ** ADAM_STEP)
    delta = -ADAM_LR * (m_hat / (jnp.sqrt(v_hat) + ADAM_EPS) + ADAM_WD * w)
    return delta, m, v


def _adam(w, mine, recv, m, v, name):
    r, c = w.shape
    tc = _col_tile(r, c)

    def body(w_ref, p_ref, r_ref, m_ref, v_ref, g_ref, d_ref, nm_ref, nv_ref):
        g = ((p_ref[...] + r_ref[0].astype(F32)) + r_ref[1].astype(F32)) + r_ref[2].astype(F32)
        g_ref[...] = g
        d_ref[...], nm_ref[...], nv_ref[...] = _adam_math(w_ref[...], g, m_ref[...], v_ref[...])

    spec = pl.BlockSpec((r, tc), lambda j: (0, j))
    return pl.pallas_call(
        body, name=name, grid=(c // tc,),
        in_specs=[spec, spec, pl.BlockSpec((3, r, tc), lambda j: (0, 0, j)), spec, spec], out_specs=[spec] * 4,
        out_shape=[jax.ShapeDtypeStruct((r, c), F32)] * 4,
        compiler_params=pltpu.CompilerParams(dimension_semantics=("parallel",)),
    )(*_in_hbm(w, mine, recv, m, v))


def _adam_small(gathered, ws, ms, vs, loss_parts):
    n = len(ws)

    def body(*refs):
        outs = refs[4 * n + 1:]
        loss = refs[4 * n][0]
        for dev in range(1, N_DEV):
            loss = loss + refs[4 * n][dev]
        outs[4 * n][...] = loss
        for i in range(n):
            ga_ref, w_ref, m_ref, v_ref = (refs[j * n + i] for j in range(4))
            g = ga_ref[0]
            for dev in range(1, N_DEV):
                g = g + ga_ref[dev]
            g = g[:, :w_ref.shape[1]]
            outs[4 * i][...] = g
            outs[4 * i + 1][...], outs[4 * i + 2][...], outs[4 * i + 3][...] = _adam_math(
                w_ref[...], g, m_ref[...], v_ref[...])

    out_shape = [jax.ShapeDtypeStruct(w.shape, F32) for w in ws for _ in range(4)]
    out_shape.append(jax.ShapeDtypeStruct((1, LANES), F32))
    out = pl.pallas_call(body, name="adam_small", out_shape=out_shape)(*gathered, *ws, *ms, *vs, loss_parts)
    return [out[4 * i:4 * i + 4] for i in range(n)], out[4 * n]


_PROJ_SEGMENTS = ((3848, 5896), (None, COL_QA - 2 * D_MODEL), (0, 3840), (3840, 3848), (None, PROJ_COLS - COL_F - 8))


def _proj_weight_t(gathered):
    w = gathered.reshape(IN_COLS, D_MODEL)
    return jnp.concatenate([jnp.zeros((hi, D_MODEL), w.dtype) if lo is None else w[lo:hi] for lo, hi in _PROJ_SEGMENTS],
                           axis=0)


def _proj_weight_grad_slots(dwt_r):
    starts, at = [], 0
    for lo, hi in _PROJ_SEGMENTS:
        if lo is not None:
            starts.append((lo, hi, at))
        at += hi if lo is None else hi - lo
    slots = []
    for dev in range(N_DEV):
        pieces, lo, end = [], dev * IN_SHARD, (dev + 1) * IN_SHARD
        for seg_lo, seg_hi, seg_at in sorted(starts):
            a, b = max(lo, seg_lo), min(end, seg_hi)
            if a < b:
                pieces.append(dwt_r[seg_at + a - seg_lo:seg_at + b - seg_lo])
        slots.append(pieces[0] if len(pieces) == 1 else jnp.concatenate(pieces, axis=0))
    return jnp.stack(slots)


def kernel(x, w_in, w_proj_a, w_proj_b, w_out, b_forget, w_ffn_gate, w_ffn_up, w_ffn_down, norm_mix_pre, norm_mix_post, norm_ffn_pre, norm_ffn_post, loss_target, m_w_in, m_w_proj_a, m_w_proj_b, m_w_out, m_b_forget, m_w_ffn_gate, m_w_ffn_up, m_w_ffn_down, m_norm_mix_pre, m_norm_mix_post, m_norm_ffn_pre, m_norm_ffn_post, v_w_in, v_w_proj_a, v_w_proj_b, v_w_out, v_b_forget, v_w_ffn_gate, v_w_ffn_up, v_w_ffn_down, v_norm_mix_pre, v_norm_mix_post, v_norm_ffn_pre, v_norm_ffn_post):
    d = D_MODEL
    names = ("w_in", "w_proj_a", "w_proj_b", "w_out", "w_ffn_gate", "w_ffn_up", "w_ffn_down")
    col_sharded = ("w_in", "w_ffn_gate", "w_ffn_up")

    def row_shards(arrs):
        return {k: (a[0].T if k in col_sharded else a[0]) for k, a in zip(names, arrs)}

    shards = row_shards((w_in, w_proj_a, w_proj_b, w_out, w_ffn_gate, w_ffn_up, w_ffn_down))
    moments_m = row_shards((m_w_in, m_w_proj_a, m_w_proj_b, m_w_out, m_w_ffn_gate, m_w_ffn_up, m_w_ffn_down))
    moments_v = row_shards((v_w_in, v_w_proj_a, v_w_proj_b, v_w_out, v_w_ffn_gate, v_w_ffn_up, v_w_ffn_down))
    pos = jnp.stack([lax.axis_index("c"), 2 * lax.axis_index("x") + lax.axis_index("y")]).astype(jnp.int32)
    x2, target = x[0], loss_target[0]

    me = 4 * lax.axis_index("x") + 2 * lax.axis_index("y") + lax.axis_index("c")
    mid_names, ffn_names = names[1:4], names[4:]
    first_names, later_names = names[:1], names[1:]
    shards16 = {k: shards[k].astype(BF16) for k in names}

    def landing(k):
        return lax.dynamic_update_slice(lax.empty((N_DEV,) + shards[k].shape, BF16), shards16[k][None], (me, 0, 0))

    ag_first = _exchange_start("ag_first_chips_start", _gather_chips_copies, [shards16[k] for k in first_names],
                               [landing(k) for k in first_names], 3 * len(first_names))
    h = _rowwise(lambda xb, g: xb * _rms_scale(xb) * g, "norm_mix_pre", SEQ, 256, [(x2, d, 0)], [norm_mix_pre],
                 [(d, BF16)], deps=[ag_first.token])[0]
    _, lands = _exchange_wait("ag_first_chips_wait", ag_first, [h])
    ag_first = _exchange_start("ag_first_sibling_start", _gather_sibling_copies, [], lands, 4 * len(first_names))
    ag_later = _exchange_start("ag_later_chips_start", _gather_chips_copies, [shards16[k] for k in later_names],
                               [landing(k) for k in later_names], 3 * len(later_names), after=[ag_first.token])
    gathered = dict(zip(first_names, _exchange_wait("ag_first_sibling_wait", ag_first, [ag_later.token])[1]))
    wt_r = _proj_weight_t(gathered["w_in"])

    proj = _matmul([(h, wt_r)], "nt", F32, "in_proj", 1024, 896, 1024)
    tables = _rope_tables()
    o_dil, lse_dil = zip(*[_dil_fwd(g, proj, tables) for g in range(len(DILATIONS))])
    out_a = _dil_combine(o_dil, lse_dil)
    _, lands = _exchange_wait("ag_later_chips_wait", ag_later, [out_a])
    ag_later = _exchange_start("ag_later_sibling_start", _gather_sibling_copies, [], lands, 4 * len(later_names))

    b_pad = jnp.pad(b_forget, ((0, 0), (0, LANES - N_FOX_HEADS)))
    f_rows = _fox_gate(proj, b_pad, deps=[ag_later.token])
    out_b, lse_fox = _fox_fwd(proj, f_rows)

    gathered = dict(zip(later_names, _exchange_wait("ag_later_sibling_wait", ag_later, [out_b])[1]))
    wt_pa = gathered["w_proj_a"].transpose(1, 0, 2).reshape(DIL_OUT_WIDTH, d)
    wt_pb = gathered["w_proj_b"].transpose(1, 0, 2).reshape(FOX_WIDTH, d)
    w_o = gathered["w_out"].reshape(d, d)
    wt_g = gathered["w_ffn_gate"].reshape(D_FF, d)
    wt_u = gathered["w_ffn_up"].reshape(D_FF, d)
    w_d = gathered["w_ffn_down"].reshape(D_FF, d)
    merged, mix, x1, h2 = _mix_out(out_a, out_b, proj, x2, wt_pa, wt_pb, w_o, norm_mix_post, norm_ffn_pre)

    gate, up, act = _ffn_up(h2, wt_g, wt_u)
    dy, dff, loss_part, dg_ffn_post = _loss_head(act, w_d, x1, target, norm_ffn_post)

    dgate, dup = _ffn_act_bwd(dff, w_d, gate, up)
    grads_t = {}
    grads_t["w_ffn_down"] = _matmul([(act, dff)], "tn", F32, "grad_w_ffn_down", 1408, 512, 2048)
    grads_t["w_ffn_gate"] = _matmul([(dgate, h2)], "tn", F32, "grad_w_ffn_gate", 1408, 512, 2048)
    grads_t["w_ffn_up"] = _matmul([(dup, h2)], "tn", F32, "grad_w_ffn_up", 1408, 512, 2048)
    rs_ffn = _ReduceScatter("ffn", {k: grads_t[k] for k in ffn_names}, pos)
    dx1, dmix, dg_ffn_pre, dg_mix_post = _post_ffn_bwd(dgate, wt_g, dup, wt_u, x1, dy, mix, norm_ffn_pre, norm_mix_post,
                                                       deps=[rs_ffn.token])
    rs_ffn.start_chips([dmix])

    dproj, dya, dyb, d_out_a, d_out_b = _mix_out_bwd(dmix, out_a, out_b, proj, wt_pa, wt_pb, w_o, deps=[rs_ffn.token])
    grads_t["w_out"] = _matmul([(merged, dmix)], "tn", F32, "grad_w_out", 1024, 1024, 1024)
    def column_slots(g):
        return g.reshape(g.shape[0], N_DEV, LANES).transpose(1, 0, 2)

    grads_t["w_proj_a"] = column_slots(_matmul([(out_a, dya)], "tn", F32, "grad_w_proj_a", DIL_OUT_WIDTH, 1024, SEQ))
    grads_t["w_proj_b"] = column_slots(_matmul([(out_b, dyb)], "tn", F32, "grad_w_proj_b", FOX_WIDTH, 1024, SEQ))
    rs_mid = _ReduceScatter("mid", {k: grads_t[k] for k in mid_names}, pos)

    do_dil, c_dil = _dil_combine_bwd(d_out_a, o_dil, lse_dil, deps=[rs_mid.token])
    rs_mid.start_chips([c_dil[0]])
    dproj, d_cum = _fox_bwd(proj, d_out_b, lse_fox, f_rows, dproj)
    d_cum_rows = jnp.pad(d_cum[:, :2].reshape(N_FOX_HEADS, SEQ), ((0, F_ROWS - N_FOX_HEADS), (0, 0)))
    dproj, db_part = _fox_gate_bwd(d_cum_rows, proj, b_pad, dproj)
    for g in range(len(DILATIONS)):
        dproj = _dil_bwd(g, proj, tables, do_dil[g], lse_dil[g], c_dil[g], dproj, deps=[rs_mid.token])

    dwt_r = _matmul([(dproj, h)], "tn", F32, "grad_w_in", 896, 1024, 2048)
    rs_in = _ReduceScatter("in", {"w_in": _proj_weight_grad_slots(dwt_r)}, pos)
    def finish(rs, after):
        return {k: _adam(shards[k], mine, recv, moments_m[k], moments_v[k], "adam_" + k)
                for k, (mine, recv) in rs.finish(after).items()}

    done = finish(rs_ffn, [rs_in.token])
    rs_in.start_chips([done[k][0] for k in ffn_names])
    grad_x, dg_mix_pre = _input_bwd(dproj, wt_r, x2, dx1, norm_mix_pre, deps=[rs_in.token])
    done.update(finish(rs_mid, [grad_x]))

    small_all = _all_gather([dg_mix_pre, dg_mix_post, dg_ffn_pre, dg_ffn_post, db_part, loss_part],
                            "small_grads_all_gather", deps=[done[k][0] for k in mid_names])
    small, loss = _adam_small(small_all[:5], [norm_mix_pre, norm_mix_post, norm_ffn_pre, norm_ffn_post, b_forget],
                              [m_norm_mix_pre, m_norm_mix_post, m_norm_ffn_pre, m_norm_ffn_post, m_b_forget],
                              [v_norm_mix_pre, v_norm_mix_post, v_norm_ffn_pre, v_norm_ffn_post, v_b_forget],
                              small_all[5])

    done.update(finish(rs_in, [small[0][0]]))

    def leaves(i):
        def nat(k):
            a = done[k][i]
            return (a.T if k in col_sharded else a)[None]

        return [nat("w_in"), nat("w_proj_a"), nat("w_proj_b"), nat("w_out"), small[4][i],
                nat("w_ffn_gate"), nat("w_ffn_up"), nat("w_ffn_down"), *[small[r][i] for r in range(4)]]

    return (loss[0, 0], grad_x[None], *leaves(0), *leaves(1), *leaves(2), *leaves(3))
```

```python
import functools
import math

import jax
import jax.numpy as jnp
import numpy as np
from jax import lax
from jax.experimental import pallas as pl
from jax.experimental.pallas import tpu as pltpu

F32 = jnp.float32
BF16 = jnp.bfloat16
MESH = pl.DeviceIdType.MESH

D_MODEL = 1024
SEQ = 2048
HEAD_DIM = 64
BLOCK = 128
N_BLOCKS = SEQ // BLOCK
DILATIONS = (1, 4, 16)
N_FOX_HEADS = 8
DIL_WIDTH = 768
DIL_OUT_WIDTH = 256
FOX_WIDTH = 512
D_FF = 2816
ROPE_THETA = 500000.0
ROPE_DIM = HEAD_DIM // 4
ROPE_HALF = ROPE_DIM // 2
EPS = 1e-6
NEG_INF = -1e30
QK_SCALE = 1.0 / math.sqrt(HEAD_DIM)
IN_COLS = 5896
N_DEV = 8
IN_SHARD = IN_COLS // N_DEV

ADAM_LR = 0.001
ADAM_B1 = 0.9
ADAM_B2 = 0.999
ADAM_EPS = 1e-08
ADAM_WD = 0.01
ADAM_STEP = 10

V7X_VMEM_BYTES = 64 * 2**20
LANES = 128
SUBLANES = 8

PROJ_COLS = 6272
COL_GA, COL_GB = 0, 1024
COL_QA, COL_KA, COL_VA = 2304, 3072, 3840
COL_QB, COL_KB, COL_VB = 4608, 5120, 5632
COL_F = 6144
F_ROWS = 16


def _vmem_limit(block_bytes):
    want = 2 * block_bytes + 16 * 2**20
    return int(min(max(want, 32 * 2**20), V7X_VMEM_BYTES - 8 * 2**20))


def _nbytes(shape, dtype):
    return math.prod(shape) * jnp.dtype(dtype).itemsize


def _in_hbm(*arrays):
    return [pltpu.with_memory_space_constraint(a, pltpu.HBM) for a in arrays]


def _dot(a, b, dims):
    return lax.dot_general(a, b, (dims, ((), ())), preferred_element_type=F32)


def _dot_nn(a, b):
    return _dot(a, b, ((1,), (0,)))


def _dot_nt(a, b):
    return _dot(a, b, ((1,), (1,)))


def _dot_tn(a, b):
    return _dot(a, b, ((0,), (0,)))


def _sigmoid(z):
    return 1.0 / (1.0 + jnp.exp(-z))


def _split3(x):
    hi = x.astype(BF16)
    r1 = x - hi.astype(F32)
    mid = r1.astype(BF16)
    lo = (r1 - mid.astype(F32)).astype(BF16)
    return hi, mid, lo


def _dot3_nn(x, ones_matrix):
    hi, mid, lo = _split3(x)
    return (_dot_nn(hi, ones_matrix) + _dot_nn(mid, ones_matrix)) + _dot_nn(lo, ones_matrix)


def _rowwise(fn, name, n_rows, tm, row_ins, bcast_ins, row_outs, acc_outs=(), deps=()):
    n_in = len(row_ins) + len(bcast_ins)
    n_ro = len(row_outs)

    def body(*refs):
        res = fn(*[r[...] for r in refs[:n_in]])
        if not isinstance(res, (tuple, list)):
            res = (res,)
        outs = refs[n_in + len(deps):]
        for r, o in zip(res[:n_ro], outs[:n_ro]):
            o[...] = r.astype(o.dtype)
        first = pl.program_id(0) == 0
        for r, o in zip(res[n_ro:], outs[n_ro:]):
            _accumulate(o, r, first)

    in_specs = [pl.BlockSpec((tm, w), lambda i, cb=cb: (i, cb)) for _, w, cb in row_ins]
    in_specs += [pl.BlockSpec(a.shape, lambda i: (0, 0)) for a in bcast_ins]
    in_specs += [pl.BlockSpec(memory_space=pl.ANY)] * len(deps)
    out_specs = [pl.BlockSpec((tm, w), lambda i: (i, 0)) for w, _ in row_outs]
    out_specs += [pl.BlockSpec((1, w), lambda i: (0, 0)) for w in acc_outs]
    out_shape = [jax.ShapeDtypeStruct((n_rows, w), dt) for w, dt in row_outs]
    out_shape += [jax.ShapeDtypeStruct((1, w), F32) for w in acc_outs]
    blk = sum(_nbytes((tm, w), a.dtype) for a, w, _ in row_ins) + sum(_nbytes((tm, w), dt) for w, dt in row_outs)
    return pl.pallas_call(
        body, name=name, grid=(n_rows // tm,), in_specs=in_specs, out_specs=out_specs, out_shape=out_shape,
        compiler_params=pltpu.CompilerParams(
            dimension_semantics=("arbitrary" if acc_outs else "parallel",), vmem_limit_bytes=_vmem_limit(3 * blk)),
    )(*_in_hbm(*[a for a, _, _ in row_ins], *bcast_ins), *deps)


def _accumulate(o_ref, part, first):
    @pl.when(first)
    def _():
        o_ref[...] = part

    @pl.when(jnp.logical_not(first))
    def _():
        o_ref[...] += part


_MM_DIMS = {"nn": ((1,), (0,)), "nt": ((1,), (1,)), "tn": ((0,), (0,))}


def _matmul(pairs, mode, out_dtype, name, tm, tn, tk, deps=()):
    a0, b0 = pairs[0]
    if mode == "tn":
        kk, m = a0.shape
    else:
        m, kk = a0.shape
    n = b0.shape[0] if mode == "nt" else b0.shape[1]
    assert m % tm == 0 and n % tn == 0 and kk % tk == 0, (name, m, n, kk)
    nk = kk // tk
    n_pairs = len(pairs)
    dims = _MM_DIMS[mode]
    n_in = 2 * n_pairs + len(deps)

    def body(*refs):
        o_ref = refs[n_in]
        part = None
        for p in range(n_pairs):
            d = _dot(refs[2 * p][...].astype(BF16), refs[2 * p + 1][...].astype(BF16), dims)
            part = d if part is None else part + d
        if nk == 1:
            o_ref[...] = part.astype(o_ref.dtype)
            return
        acc = refs[n_in + 1]
        k = pl.program_id(2)

        @pl.when(k == 0)
        def _():
            acc[...] = part

        @pl.when(k > 0)
        def _():
            acc[...] += part

        @pl.when(k == nk - 1)
        def _():
            o_ref[...] = acc[...].astype(o_ref.dtype)

    if mode == "tn":
        a_spec = pl.BlockSpec((tk, tm), lambda i, j, k: (k, i))
    else:
        a_spec = pl.BlockSpec((tm, tk), lambda i, j, k: (i, k))
    if mode == "nt":
        b_spec = pl.BlockSpec((tn, tk), lambda i, j, k: (j, k))
    else:
        b_spec = pl.BlockSpec((tk, tn), lambda i, j, k: (k, j))
    blk = sum(_nbytes((tm, tk), a.dtype) + _nbytes((tk, tn), b.dtype) for a, b in pairs) + 2 * _nbytes((tm, tn), F32)
    flat = [a for pair in pairs for a in pair]
    return pl.pallas_call(
        body, name=name, grid=(m // tm, n // tn, nk),
        in_specs=[a_spec, b_spec] * n_pairs + [pl.BlockSpec(memory_space=pl.ANY)] * len(deps),
        out_specs=pl.BlockSpec((tm, tn), lambda i, j, k: (i, j)),
        out_shape=jax.ShapeDtypeStruct((m, n), out_dtype),
        scratch_shapes=[] if nk == 1 else [pltpu.VMEM((tm, tn), F32)],
        compiler_params=pltpu.CompilerParams(
            dimension_semantics=("parallel", "parallel", "arbitrary"), vmem_limit_bytes=_vmem_limit(blk)),
    )(*flat, *deps)


def _matmul_rowwise(pairs, fn, name, tm, row_ins, bcast_ins, row_outs, acc_outs=(), deps=()):
    m = pairs[0][0].shape[0]
    n_mm, n_in = 2 * len(pairs), len(row_ins) + len(bcast_ins)
    n_ro = len(row_outs)

    def body(*refs):
        prod = None
        for p in range(len(pairs)):
            part = _dot_nn(refs[2 * p][...].astype(BF16), refs[2 * p + 1][...].astype(BF16))
            prod = part if prod is None else prod + part
        res = fn(prod, *[r[...] for r in refs[n_mm:n_mm + n_in]])
        outs = refs[n_mm + n_in + len(deps):]
        for r, o in zip(res[:n_ro], outs[:n_ro]):
            o[...] = r.astype(o.dtype)
        first = pl.program_id(0) == 0
        for r, o in zip(res[n_ro:], outs[n_ro:]):
            _accumulate(o, r, first)

    in_specs = []
    for a, b in pairs:
        in_specs += [pl.BlockSpec((tm, a.shape[1]), lambda i: (i, 0)),
                     pl.BlockSpec(b.shape, lambda i: (0, 0), pipeline_mode=pl.Buffered(1))]
    in_specs += [pl.BlockSpec((tm, w), lambda i, cb=cb: (i, cb)) for _, w, cb in row_ins]
    in_specs += [pl.BlockSpec(a.shape, lambda i: (0, 0)) for a in bcast_ins]
    in_specs += [_ANY] * len(deps)
    out_specs = [pl.BlockSpec((tm, w), lambda i: (i, 0)) for w, _ in row_outs]
    out_specs += [pl.BlockSpec((1, w), lambda i: (0, 0)) for w in acc_outs]
    out_shape = [jax.ShapeDtypeStruct((m, w), dt) for w, dt in row_outs]
    out_shape += [jax.ShapeDtypeStruct((1, w), F32) for w in acc_outs]
    blk = sum(_nbytes((tm, a.shape[1]), a.dtype) + _nbytes(b.shape, b.dtype) // 2 for a, b in pairs)
    blk += sum(_nbytes((tm, w), a.dtype) for a, w, _ in row_ins) + sum(_nbytes((tm, w), dt) for w, dt in row_outs)
    return pl.pallas_call(
        body, name=name, grid=(m // tm,), in_specs=in_specs, out_specs=out_specs, out_shape=out_shape,
        compiler_params=pltpu.CompilerParams(dimension_semantics=("arbitrary",), vmem_limit_bytes=_vmem_limit(blk)),
    )(*[a for pair in pairs for a in pair], *[a for a, _, _ in row_ins], *bcast_ins, *deps)


def _rms_scale(x):
    return lax.rsqrt(jnp.mean(x * x, axis=-1, keepdims=True) + EPS)


def _rms_bwd(xin, dyn, g):
    r = _rms_scale(xin)
    u = dyn * g
    dx = r * u - xin * (r * r * r) * jnp.mean(u * xin, axis=-1, keepdims=True)
    dg = jnp.sum(dyn * xin * r, axis=0, keepdims=True)
    return dx, dg


def _mesh_pos():
    return lax.axis_index("x"), lax.axis_index("y"), lax.axis_index("c")


def _all_gather(xs, name, deps=()):
    n = len(xs)

    def body(*refs):
        x_refs, out_refs = refs[:n], refs[n + len(deps):2 * n + len(deps)]
        send_sems, recv_sems, local_sems = refs[2 * n + len(deps):]
        mx, my, mc = _mesh_pos()
        me, sib = (mx, my, mc), (mx, my, 1 - mc)
        chips = [(1 - mx, my), (mx, 1 - my), (1 - mx, 1 - my)]

        def slot(a, dev):
            px, py, pc = dev
            return out_refs[a].at[4 * px + 2 * py + pc]

        def copy(k, a, block, to, src=None):
            return pltpu.make_async_remote_copy(
                src_ref=slot(a, block) if src is None else src, dst_ref=slot(a, block),
                send_sem=send_sems.at[a * 7 + k], recv_sem=recv_sems.at[a * 7 + k],
                device_id=to, device_id_type=MESH)

        mine = [pltpu.make_async_copy(x_refs[a], slot(a, me), local_sems.at[a]) for a in range(n)]
        for cp in mine:
            cp.start()
        first = []
        for a in range(n):
            first.append(copy(0, a, me, sib, x_refs[a]))
            first += [copy(1 + j, a, me, (*chip, mc), x_refs[a]) for j, chip in enumerate(chips)]
        for cp in first:
            cp.start()
        passed = []
        for a in range(n):
            for j, chip in enumerate(chips):
                copy(1 + j, a, (*chip, mc), me).wait_recv()
                fwd = copy(4 + j, a, (*chip, mc), sib)
                fwd.start()
                passed.append(fwd)
        for a in range(n):
            copy(0, a, sib, me).wait_recv()
            for j, chip in enumerate(chips):
                copy(4 + j, a, (*chip, 1 - mc), me).wait_recv()
        for cp in first + passed:
            cp.wait_send()
        for cp in mine:
            cp.wait()

    hbm = pl.BlockSpec(memory_space=pl.ANY)
    return pl.pallas_call(
        body, name=name,
        out_shape=[jax.ShapeDtypeStruct((N_DEV,) + x.shape, x.dtype) for x in xs],
        in_specs=[hbm] * (n + len(deps)), out_specs=[hbm] * n,
        scratch_shapes=[pltpu.SemaphoreType.DMA((7 * n,)), pltpu.SemaphoreType.DMA((7 * n,)),
                        pltpu.SemaphoreType.DMA((n,))],
    )(*xs, *deps)


_HBM = pl.BlockSpec(memory_space=pltpu.HBM)
_SEM = pl.BlockSpec(memory_space=pltpu.SEMAPHORE)
_ANY = pl.BlockSpec(memory_space=pl.ANY)
_DATAFLOW = pltpu.SideEffectType.DATAFLOW_SIDE_EFFECTING


def _flip_peer(flip):
    mx, my, mc = _mesh_pos()
    return (1 - mx if flip & 2 else mx, 1 - my if flip & 1 else my, mc)


def _remote(src, dst, send_sems, recv_sems, k, peer):
    return pltpu.make_async_remote_copy(src_ref=src, dst_ref=dst, send_sem=send_sems.at[k], recv_sem=recv_sems.at[k],
                                        device_id=peer, device_id_type=MESH)


def _gather_chips_copies(srcs, lands, send_sems, recv_sems):
    mx, my, mc = _mesh_pos()
    me = 4 * mx + 2 * my + mc
    return [_remote(srcs[a], lands[a].at[me], send_sems, recv_sems, 3 * a + flip - 1, _flip_peer(flip))
            for a in range(len(srcs)) for flip in (1, 2, 3)]


def _gather_sibling_copies(srcs, lands, send_sems, recv_sems):
    mx, my, mc = _mesh_pos()
    return [_remote(lands[a].at[2 * k + mc], lands[a].at[2 * k + mc], send_sems, recv_sems, 4 * a + k, (mx, my, 1 - mc))
            for a in range(len(lands)) for k in range(4)]


def _scatter_sibling_copies(srcs, lands, send_sems, recv_sems):
    mx, my, mc = _mesh_pos()
    return [_remote(srcs[a].at[k, 1 - mc], lands[a].at[k], send_sems, recv_sems, 4 * a + k, (mx, my, 1 - mc))
            for a in range(len(srcs)) for k in range(4)]


def _scatter_chips_copies(srcs, lands, send_sems, recv_sems):
    mx, my, _ = _mesh_pos()
    k0 = 2 * mx + my
    return [_remote(srcs[a].at[jnp.bitwise_xor(k0, flip)], lands[a].at[flip - 1], send_sems, recv_sems,
                    3 * a + flip - 1, _flip_peer(flip))
            for a in range(len(srcs)) for flip in (1, 2, 3)]


class _Exchange:
    def __init__(self, copies, n_src, send_sems, recv_sems, thru, token):
        self.copies, self.n_src, self.send_sems, self.recv_sems, self.thru, self.token = (
            copies, n_src, send_sems, recv_sems, thru, token)


def _exchange_start(name, copies, srcs, lands, n_copies, after=()):
    bufs = list(srcs) + list(lands)
    nb, ns = len(bufs), len(srcs)

    def body(*refs):
        send_sems, recv_sems = refs[nb + len(after)], refs[nb + len(after) + 1]
        for cp in copies(refs[:ns], refs[ns:nb], send_sems, recv_sems):
            cp.start()
        refs[-1][...] = jnp.zeros_like(refs[-1])

    out = pl.pallas_call(
        body, name=name,
        out_shape=(pltpu.SemaphoreType.DMA((n_copies,)), pltpu.SemaphoreType.DMA((n_copies,)),
                   *[pltpu.HBM(b.shape, b.dtype) for b in bufs], jax.ShapeDtypeStruct((SUBLANES, LANES), F32)),
        in_specs=[_HBM] * nb + [_ANY] * len(after),
        out_specs=(_SEM, _SEM, *[_HBM] * nb, pl.BlockSpec(memory_space=pltpu.VMEM)),
        input_output_aliases={i: 2 + i for i in range(nb)},
        compiler_params=pltpu.CompilerParams(has_side_effects=_DATAFLOW),
    )(*[pltpu.with_memory_space_constraint(b, pltpu.HBM) for b in bufs], *after)
    return _Exchange(copies, ns, out[0], out[1], list(out[2:2 + nb]), out[-1])


def _exchange_wait(name, ex, after):
    nb, ns = len(ex.thru), ex.n_src

    def body(*refs):
        for cp in ex.copies(refs[:ns], refs[ns:nb], refs[nb], refs[nb + 1]):
            cp.wait_send()
            cp.wait_recv()

    out = pl.pallas_call(
        body, name=name, out_shape=tuple(pltpu.HBM(b.shape, b.dtype) for b in ex.thru),
        in_specs=[_HBM] * nb + [_SEM, _SEM] + [_ANY] * len(after), out_specs=tuple([_HBM] * nb),
        input_output_aliases={i: i for i in range(nb)},
        compiler_params=pltpu.CompilerParams(has_side_effects=_DATAFLOW),
    )(*ex.thru, ex.send_sems, ex.recv_sems, *after)
    return list(out[:ns]), list(out[ns:])


def _col_tile(r, c):
    return next(t for t in (1024, 512, 256, 128) if c % t == 0 and (r * t * 4 <= 2**20 or t == 128))


def _add_sibling(g4, recv, pos, name):
    _, _, r, c = g4.shape
    tc = _col_tile(r, c)

    def body(pos_ref, g_ref, r_ref, o16_ref, mine_ref):
        s = g_ref[0, 0] + r_ref[0]
        o16_ref[0] = s.astype(BF16)

        @pl.when(pl.program_id(1) == pos_ref[1])
        def _():
            mine_ref[...] = s

    slot = pl.BlockSpec((1, r, tc), lambda j, k, pos_ref: (k, 0, j))
    return pl.pallas_call(
        body, name=name,
        out_shape=[jax.ShapeDtypeStruct((4, r, c), BF16), jax.ShapeDtypeStruct((r, c), F32)],
        grid_spec=pltpu.PrefetchScalarGridSpec(
            num_scalar_prefetch=1, grid=(c // tc, 4),
            in_specs=[pl.BlockSpec((1, 1, r, tc), lambda j, k, pos_ref: (k, pos_ref[0], 0, j)), slot],
            out_specs=[slot, pl.BlockSpec((r, tc), lambda j, k, pos_ref: (0, j))]),
        compiler_params=pltpu.CompilerParams(dimension_semantics=("parallel", "arbitrary")),
    )(pos, *_in_hbm(g4, recv))


class _ReduceScatter:
    def __init__(self, tag, grads_t, pos):
        self.tag, self.pos, self.names = tag, pos, list(grads_t)
        g4s = [g.reshape(4, 2, g.size // (N_DEV * g.shape[-1]), g.shape[-1]) for g in grads_t.values()]
        lands = [lax.empty((4,) + g.shape[2:], F32) for g in g4s]
        self.ex = _exchange_start(f"rs_{tag}_sibling_start", _scatter_sibling_copies, g4s, lands, 4 * len(g4s))
        self.token = self.ex.token

    def start_chips(self, after):
        g4s, from_sibling = _exchange_wait(f"rs_{self.tag}_sibling_wait", self.ex, after)
        parts = [_add_sibling(g4, rv, self.pos, f"rs_add_sibling_{k}")
                 for k, g4, rv in zip(self.names, g4s, from_sibling)]
        self.mine = [mine for _, mine in parts]
        p16s = [p16 for p16, _ in parts]
        lands = [lax.empty((3,) + p.shape[1:], BF16) for p in p16s]
        self.ex = _exchange_start(f"rs_{self.tag}_chips_start", _scatter_chips_copies, p16s, lands, 3 * len(p16s))
        self.token = self.ex.token

    def finish(self, after):
        _, from_chips = _exchange_wait(f"rs_{self.tag}_chips_wait", self.ex, after)
        return dict(zip(self.names, zip(self.mine, from_chips)))


def _rope_tables():
    positions = np.arange(SEQ, dtype=np.float32)
    inv_freq = np.power(np.float32(ROPE_THETA), -np.arange(0, ROPE_DIM, 2, dtype=np.float32) / np.float32(ROPE_DIM))
    ang = (positions[:, None] * inv_freq[None, :]).astype(np.float32)
    cos, sin = np.cos(ang).astype(np.float32), np.sin(ang).astype(np.float32)
    ones = np.ones((SEQ, HEAD_DIM - ROPE_DIM), np.float32)
    zeros8 = np.zeros((SEQ, ROPE_HALF), np.float32)
    zeros = np.zeros((SEQ, HEAD_DIM - ROPE_DIM), np.float32)
    c_head = np.concatenate([cos, cos, ones], axis=1)
    s1_head = np.concatenate([-sin, zeros8, zeros], axis=1)
    s2_head = np.concatenate([zeros8, sin, zeros], axis=1)
    return tuple(jnp.asarray(np.concatenate([t, t], axis=1)) for t in (c_head, s1_head, s2_head))


def _rope_apply(x, c, s1, s2):
    w = x.shape[1]
    return x * c + pltpu.roll(x, w - ROPE_HALF, 1) * s1 + pltpu.roll(x, ROPE_HALF, 1) * s2


def _rope_apply_t(dy, c, s1, s2):
    w = dy.shape[1]
    return dy * c + pltpu.roll(dy * s1, ROPE_HALF, 1) + pltpu.roll(dy * s2, w - ROPE_HALF, 1)


def _dil_prev_limit(has_prev):
    return jnp.where(has_prev, 0, BLOCK)


def _dil_valid(limit):
    row = lax.broadcasted_iota(jnp.int32, (BLOCK, 2 * BLOCK), 0)
    col = lax.broadcasted_iota(jnp.int32, (BLOCK, 2 * BLOCK), 1)
    dist = col - row
    return jnp.logical_and(dist >= jnp.where(col < BLOCK, limit, -BLOCK), dist <= BLOCK)


def _upper_half():
    return lax.broadcasted_iota(jnp.int32, (1, LANES), 1) >= HEAD_DIM


def _stack_heads(x):
    upper = _upper_half()
    return jnp.concatenate([jnp.where(upper, 0, x), jnp.where(upper, x, 0)], axis=0)


def _unstack_heads(y):
    n = y.shape[0] // 2
    return jnp.where(_upper_half(), y[n:], y[:n])


def _head_columns(t):
    return jnp.concatenate([t[:, 0:1], t[:, HEAD_DIM:HEAD_DIM + 1]], axis=0)


def _dil_rows(n, d):
    per = N_BLOCKS // d
    r, lb = n // per, n % per

    def rows(b):
        start = b * (BLOCK * d) + r
        return pl.ds(pl.multiple_of(start, BLOCK), BLOCK) if d == 1 else pl.ds(start, BLOCK, stride=d)

    return rows(lb), rows(jnp.maximum(lb - 1, 0)), lb > 0


def _dil_specs(g):
    def col(base):
        return pl.BlockSpec((SEQ, LANES), lambda p: (0, base // LANES + 2 * g + p))

    table = pl.BlockSpec((SEQ, LANES), lambda p: (0, 0))
    return [col(COL_QA), col(COL_KA), col(COL_VA)], [table] * 3


def _store_columns(blocks, dproj_ref, cols, sem):
    copies = [pltpu.make_async_copy(b, dproj_ref.at[:, pl.ds(pl.multiple_of(c * LANES, LANES), LANES)], sem.at[i])
              for i, (b, c) in enumerate(zip(blocks, cols))]
    for cp in copies:
        cp.start()
    for cp in copies:
        cp.wait()


def _dil_fwd(g, proj, tables):
    d = DILATIONS[g]
    one_block = d == N_BLOCKS

    def body(q_ref, k_ref, v_ref, c_ref, s1_ref, s2_ref, o_ref, lse_ref):
        upper = _upper_half()

        def roped(ref, rows):
            return _rope_apply(ref[rows, :], c_ref[rows, :], s1_ref[rows, :], s2_ref[rows, :])

        def block(n, carry):
            rows, prev, has_prev = _dil_rows(n, d)
            qb = (roped(q_ref, rows) * QK_SCALE).astype(BF16)
            kw, vw = roped(k_ref, rows).astype(BF16), v_ref[rows, :].astype(BF16)
            if one_block:
                row = lax.broadcasted_iota(jnp.int32, (BLOCK, BLOCK), 0)
                valid = lax.broadcasted_iota(jnp.int32, (BLOCK, BLOCK), 1) <= row
            else:
                kw = jnp.concatenate([roped(k_ref, prev).astype(BF16), kw], axis=0)
                vw = jnp.concatenate([v_ref[prev, :].astype(BF16), vw], axis=0)
                valid = _dil_valid(_dil_prev_limit(has_prev))
            s = jnp.where(jnp.concatenate([valid, valid], axis=0), _dot_nt(_stack_heads(qb), kw), NEG_INF)
            m = jnp.max(s, axis=-1, keepdims=True)
            p = jnp.exp(s - m)
            den = jnp.sum(p, axis=-1, keepdims=True)
            o_ref[rows, :] = _unstack_heads(_dot_nn((p * (1.0 / den)).astype(BF16), vw))
            lse = m + jnp.log(den)
            lse_ref[rows, :] = jnp.where(upper, lse[BLOCK:], lse[:BLOCK])
            return carry

        lax.fori_loop(0, N_BLOCKS, block, 0, unroll=8)

    qkv, tabs = _dil_specs(g)
    out = pl.BlockSpec((SEQ, LANES), lambda p: (0, p))
    return pl.pallas_call(
        body, name=f"dil_attn_fwd_{g}", grid=(2,), in_specs=qkv + tabs, out_specs=[out, out],
        out_shape=[jax.ShapeDtypeStruct((SEQ, DIL_OUT_WIDTH), F32)] * 2,
        compiler_params=pltpu.CompilerParams(dimension_semantics=("parallel",)),
    )(*_in_hbm(proj, proj, proj, *tables))


def _dil_bwd(g, proj, tables, do, lse, c, dproj, deps=()):
    d = DILATIONS[g]
    one_block = d == N_BLOCKS

    def body(q_ref, k_ref, v_ref, c_ref, s1_ref, s2_ref, do_ref, lse_ref, cc_ref, dproj_in, *rest):
        dproj_ref, dq_acc, dk_acc, dv_acc, dq_out, dk_out, dv_out, sem = rest[len(deps):]
        upper = _upper_half()
        dk_acc[...] = jnp.zeros_like(dk_acc)
        dv_acc[...] = jnp.zeros_like(dv_acc)

        def roped(ref, rows):
            return _rope_apply(ref[rows, :], c_ref[rows, :], s1_ref[rows, :], s2_ref[rows, :])

        def block(n, carry):
            rows, prev, has_prev = _dil_rows(n, d)
            qb = (roped(q_ref, rows) * QK_SCALE).astype(BF16)
            dob = do_ref[rows, :].astype(BF16)
            kw, vw = roped(k_ref, rows).astype(BF16), v_ref[rows, :].astype(BF16)
            if one_block:
                row = lax.broadcasted_iota(jnp.int32, (BLOCK, BLOCK), 0)
                valid = lax.broadcasted_iota(jnp.int32, (BLOCK, BLOCK), 1) <= row
            else:
                kw = jnp.concatenate([roped(k_ref, prev).astype(BF16), kw], axis=0)
                vw = jnp.concatenate([v_ref[prev, :].astype(BF16), vw], axis=0)
                valid = _dil_valid(_dil_prev_limit(has_prev))
            q2, do2 = _stack_heads(qb), _stack_heads(dob)
            lse_col, c_col = _head_columns(lse_ref[rows, :]), _head_columns(cc_ref[rows, :])
            p = jnp.where(jnp.concatenate([valid, valid], axis=0), jnp.exp(_dot_nt(q2, kw) - lse_col), 0.0)
            ds = (p * (_dot_nt(do2, vw) + c_col)).astype(BF16)
            dk, dv = _dot_tn(ds, q2), _dot_tn(p.astype(BF16), do2)
            dq_acc[rows, :] = _unstack_heads(_dot_nn(ds, kw)) * QK_SCALE
            if one_block:
                dk_acc[rows, :] += dk
                dv_acc[rows, :] += dv
            else:
                dk_acc[prev, :] += dk[:BLOCK]
                dv_acc[prev, :] += dv[:BLOCK]
                dk_acc[rows, :] += dk[BLOCK:]
                dv_acc[rows, :] += dv[BLOCK:]
            return carry

        lax.fori_loop(0, N_BLOCKS, block, 0, unroll=8)
        tabs = (c_ref[...], s1_ref[...], s2_ref[...])
        dq_out[...] = _rope_apply_t(dq_acc[...], *tabs).astype(BF16)
        dk_out[...] = _rope_apply_t(dk_acc[...], *tabs).astype(BF16)
        dv_out[...] = dv_acc[...].astype(BF16)
        pair = 2 * g + pl.program_id(0)
        _store_columns((dq_out, dk_out, dv_out), dproj_ref,
                       [base // LANES + pair for base in (COL_QA, COL_KA, COL_VA)], sem)

    qkv, tabs = _dil_specs(g)
    tok = pl.BlockSpec((SEQ, LANES), lambda p: (0, p))
    return pl.pallas_call(
        body, name=f"dil_attn_bwd_{g}", grid=(2,),
        in_specs=qkv + tabs + [tok, tok, tok, _ANY] + [_ANY] * len(deps), out_specs=_ANY,
        out_shape=jax.ShapeDtypeStruct(dproj.shape, dproj.dtype),
        scratch_shapes=[pltpu.VMEM((SEQ, LANES), F32)] * 3 + [pltpu.VMEM((SEQ, LANES), BF16)] * 3
        + [pltpu.SemaphoreType.DMA((3,))],
        input_output_aliases={9: 0},
        compiler_params=pltpu.CompilerParams(dimension_semantics=("arbitrary",)),
    )(*_in_hbm(proj, proj, proj, *tables, do, lse, c, dproj), *deps)


def _group_weights(l0, l1, l2):
    m = jnp.maximum(jnp.maximum(l0, l1), l2)
    e0, e1, e2 = jnp.exp(l0 - m), jnp.exp(l1 - m), jnp.exp(l2 - m)
    tot = e0 + e1 + e2
    return e0 / tot, e1 / tot, e2 / tot


def _dil_combine(outs, lses, deps=()):
    def fn(o0, o1, o2, l0, l1, l2):
        w0, w1, w2 = _group_weights(l0, l1, l2)
        return w0 * o0 + w1 * o1 + w2 * o2

    w = DIL_OUT_WIDTH
    return _rowwise(fn, "dil_combine", SEQ, 512, [(a, w, 0) for a in list(outs) + list(lses)], [], [(w, F32)],
                    deps=deps)[0]


def _dil_combine_bwd(d_out, outs, lses, deps=()):
    w = DIL_OUT_WIDTH

    def fn(d, o0, o1, o2, l0, l1, l2):
        row = lax.broadcasted_iota(jnp.int32, (w, w), 0) // HEAD_DIM
        col = lax.broadcasted_iota(jnp.int32, (w, w), 1) // HEAD_DIM
        same_head = jnp.where(row == col, 1.0, 0.0).astype(BF16)
        ws = _group_weights(l0, l1, l2)
        dws = [_dot3_nn(d * og, same_head) for og in (o0, o1, o2)]
        mean = ws[0] * dws[0] + ws[1] * dws[1] + ws[2] * dws[2]
        return tuple(wg * d for wg in ws) + tuple(-wg * mean for wg in ws)

    res = _rowwise(fn, "dil_combine_bwd", SEQ, 256, [(a, w, 0) for a in [d_out] + list(outs) + list(lses)], [],
                   [(w, F32)] * 6, deps=deps)
    return res[:3], res[3:]


def _log1p(e):
    u = 1.0 + e
    return jnp.where(u == 1.0, e, jnp.log(u) * (e / (u - 1.0)))


def _fox_gate(proj, b_pad, deps=()):
    def body(f_ref, b_ref, *rest):
        o_ref = rest[-1]
        z = f_ref[...] + b_ref[...]
        logf = (jnp.minimum(z, 0.0) - _log1p(jnp.exp(-jnp.abs(z)))).T[:F_ROWS]
        row = lax.broadcasted_iota(jnp.int32, (BLOCK, BLOCK), 0)
        col = lax.broadcasted_iota(jnp.int32, (BLOCK, BLOCK), 1)
        before = jnp.where(row <= col, 1.0, 0.0).astype(BF16)
        carry = jnp.zeros((F_ROWS, 1), F32)
        for blk in range(N_BLOCKS):
            run = _dot3_nn(logf[:, blk * BLOCK:(blk + 1) * BLOCK], before) + carry
            o_ref[:, blk * BLOCK:(blk + 1) * BLOCK] = run
            carry = run[:, BLOCK - 1:BLOCK]

    return pl.pallas_call(
        body, name="fox_gate", grid=(1,),
        in_specs=[pl.BlockSpec((SEQ, LANES), lambda i: (0, COL_F // LANES)), pl.BlockSpec((1, LANES), lambda i: (0, 0))]
        + [_ANY] * len(deps),
        out_specs=pl.BlockSpec((F_ROWS, SEQ), lambda i: (0, 0)),
        out_shape=jax.ShapeDtypeStruct((F_ROWS, SEQ), F32),
    )(*_in_hbm(proj, b_pad), *deps)


def _fox_gate_bwd(d_cum, proj, b_pad, dproj):
    def body(d_ref, f_ref, b_ref, dproj_ref, dz_ref, db_ref):
        row = lax.broadcasted_iota(jnp.int32, (BLOCK, BLOCK), 0)
        col = lax.broadcasted_iota(jnp.int32, (BLOCK, BLOCK), 1)
        after = jnp.where(row >= col, 1.0, 0.0).astype(BF16)
        carry = jnp.zeros((F_ROWS, 1), F32)
        parts = [None] * N_BLOCKS
        for blk in reversed(range(N_BLOCKS)):
            run = _dot3_nn(d_ref[:, blk * BLOCK:(blk + 1) * BLOCK], after) + carry
            parts[blk] = run
            carry = run[:, 0:1]
        dlogf = jnp.concatenate(parts, axis=1)
        dlogf = jnp.concatenate([dlogf, jnp.zeros((LANES - F_ROWS, SEQ), F32)], axis=0).T
        dz = dlogf * _sigmoid(-(f_ref[...] + b_ref[...]))
        dz_ref[...] = dz.astype(BF16)
        db_ref[...] = jnp.sum(dz, axis=0, keepdims=True)

    f_cols = pl.BlockSpec((SEQ, LANES), lambda i: (0, COL_F // LANES))
    return pl.pallas_call(
        body, name="fox_gate_bwd", grid=(1,),
        in_specs=[pl.BlockSpec((F_ROWS, SEQ), lambda i: (0, 0)), f_cols, pl.BlockSpec((1, LANES), lambda i: (0, 0)), _ANY],
        out_specs=[f_cols, pl.BlockSpec((1, LANES), lambda i: (0, 0))],
        out_shape=[jax.ShapeDtypeStruct(dproj.shape, dproj.dtype), jax.ShapeDtypeStruct((1, LANES), F32)],
        input_output_aliases={3: 0},
    )(*_in_hbm(d_cum, proj, b_pad, dproj))


FOX_TILE = 256
FOX_TILES = SEQ // FOX_TILE


def _row_to_col(row):
    n = row.shape[1]
    eye = lax.broadcasted_iota(jnp.int32, (n, n), 0) == lax.broadcasted_iota(jnp.int32, (n, n), 1)
    return jnp.sum(jnp.where(eye, row, 0.0), axis=1, keepdims=True)


def _fox_bias(f_row, i):
    t = FOX_TILE
    ext = (i + 1) * t
    bias = _row_to_col(f_row[:, i * t:(i + 1) * t]) - f_row[:, :ext]
    row = lax.broadcasted_iota(jnp.int32, (t, ext), 0) + i * t
    col = lax.broadcasted_iota(jnp.int32, (t, ext), 1)
    return bias, col <= row


def _fox_specs():
    qkv = [pl.BlockSpec((SEQ, LANES), lambda p, base=base: (0, base // LANES + p)) for base in (COL_QB, COL_KB, COL_VB)]
    return qkv, pl.BlockSpec((F_ROWS, SEQ), lambda p: (0, 0))


def _fox_fwd(proj, f_rows):
    t = FOX_TILE

    def body(q_ref, k_ref, v_ref, f_ref, o_ref, lse_ref):
        pair = pl.program_id(0)
        upper = _upper_half()
        k16, v16 = k_ref[...].astype(BF16), v_ref[...].astype(BF16)
        f_row = [f_ref[pl.ds(2 * pair + e, 1), :] for e in range(2)]
        for i in range(FOX_TILES):
            ext = (i + 1) * t
            q_tile = (q_ref[i * t:(i + 1) * t, :] * QK_SCALE).astype(BF16)
            s2 = _dot_nt(_stack_heads(q_tile), k16[:ext])
            pns, lses = [], []
            for e in range(2):
                bias, causal = _fox_bias(f_row[e], i)
                s = jnp.where(causal, s2[e * t:(e + 1) * t] + bias, NEG_INF)
                m = jnp.max(s, axis=-1, keepdims=True)
                p = jnp.exp(s - m)
                den = jnp.sum(p, axis=-1, keepdims=True)
                pns.append((p * (1.0 / den)).astype(BF16))
                lses.append(m + jnp.log(den))
            o_ref[i * t:(i + 1) * t, :] = _unstack_heads(_dot_nn(jnp.concatenate(pns, axis=0), v16[:ext]))
            lse_ref[i * t:(i + 1) * t, :] = jnp.where(upper, lses[1], lses[0])

    qkv, f_spec = _fox_specs()
    tok = pl.BlockSpec((SEQ, LANES), lambda p: (0, p))
    return pl.pallas_call(
        body, name="fox_attn_fwd", grid=(FOX_WIDTH // LANES,),
        in_specs=qkv + [f_spec], out_specs=[tok, tok],
        out_shape=[jax.ShapeDtypeStruct((SEQ, FOX_WIDTH), F32)] * 2,
        compiler_params=pltpu.CompilerParams(
            dimension_semantics=("parallel",), vmem_limit_bytes=_vmem_limit(8 * t * SEQ * 4)),
    )(*_in_hbm(proj, proj, proj, f_rows))


def _fox_bwd(proj, do, lse, f_rows, dproj):
    t = FOX_TILE

    def body(q_ref, k_ref, v_ref, f_ref, do_ref, lse_ref, dproj_in, dproj_ref, df_ref, dk_acc, dv_acc,
             dq_out, dk_out, dv_out, sem):
        pair = pl.program_id(0)
        upper = _upper_half()
        k16, v16 = k_ref[...].astype(BF16), v_ref[...].astype(BF16)
        f_row = [f_ref[pl.ds(2 * pair + e, 1), :] for e in range(2)]
        dk_acc[...] = jnp.zeros_like(dk_acc)
        dv_acc[...] = jnp.zeros_like(dv_acc)
        df_ref[...] = jnp.zeros_like(df_ref)
        for i in range(FOX_TILES):
            ext = (i + 1) * t
            q_tile = (q_ref[i * t:(i + 1) * t, :] * QK_SCALE).astype(BF16)
            do_tile = do_ref[i * t:(i + 1) * t, :]
            lse_t = lse_ref[i * t:(i + 1) * t, :]
            q2, do2 = _stack_heads(q_tile), _stack_heads(do_tile)
            s2, dp2 = _dot_nt(q2, k16[:ext]), _dot_nt(do2, v16[:ext])
            ps, dss = [], []
            for e in range(2):
                bias, causal = _fox_bias(f_row[e], i)
                s = s2[e * t:(e + 1) * t] + bias
                p = jnp.where(causal, jnp.exp(s - lse_t[:, e * HEAD_DIM:e * HEAD_DIM + 1]), 0.0)
                dp = dp2[e * t:(e + 1) * t]
                ds = p * (dp - jnp.sum(p * dp, axis=-1, keepdims=True))
                df_ref[0, e:e + 1, :ext] -= jnp.sum(ds, axis=0, keepdims=True)
                ps.append(p.astype(BF16))
                dss.append(ds.astype(BF16))
            ds2, p2 = jnp.concatenate(dss, axis=0), jnp.concatenate(ps, axis=0)
            dq_out[i * t:(i + 1) * t, :] = (_unstack_heads(_dot_nn(ds2, k16[:ext])) * QK_SCALE).astype(BF16)
            dk_acc[:ext, :] += _dot_tn(ds2, q2)
            dv_acc[:ext, :] += _dot_tn(p2, do2)
        dk_out[...] = dk_acc[...].astype(BF16)
        dv_out[...] = dv_acc[...].astype(BF16)
        _store_columns((dq_out, dk_out, dv_out), dproj_ref, [base // LANES + pair for base in (COL_QB, COL_KB, COL_VB)],
                       sem)

    qkv, f_spec = _fox_specs()
    tok = pl.BlockSpec((SEQ, LANES), lambda p: (0, p))
    return pl.pallas_call(
        body, name="fox_attn_bwd", grid=(FOX_WIDTH // LANES,),
        in_specs=qkv + [f_spec, tok, tok, _ANY],
        out_specs=[_ANY, pl.BlockSpec((1, SUBLANES, SEQ), lambda p: (p, 0, 0))],
        out_shape=[jax.ShapeDtypeStruct(dproj.shape, dproj.dtype),
                   jax.ShapeDtypeStruct((FOX_WIDTH // LANES, SUBLANES, SEQ), F32)],
        scratch_shapes=[pltpu.VMEM((SEQ, LANES), F32)] * 2 + [pltpu.VMEM((SEQ, LANES), BF16)] * 3
        + [pltpu.SemaphoreType.DMA((3,))],
        input_output_aliases={6: 0},
        compiler_params=pltpu.CompilerParams(
            dimension_semantics=("arbitrary",), vmem_limit_bytes=_vmem_limit(10 * t * SEQ * 4)),
    )(*_in_hbm(proj, proj, proj, f_rows, do, lse, dproj))


MIX_TILE = 256


def _mix_out(out_a, out_b, proj, x, wt_pa, wt_pb, w_out, g_post, g_ffn_pre):
    tm = MIX_TILE

    def body(a_ref, b_ref, ga_ref, gb_ref, x_ref, wpa_ref, wpb_ref, wo_ref, g2_ref, g3_ref,
             merged_ref, mix_ref, x1_ref, h2_ref):
        ya = _dot_nn(a_ref[...].astype(BF16), wpa_ref[...])
        yb = _dot_nn(b_ref[...].astype(BF16), wpb_ref[...])
        merged = (_sigmoid(ga_ref[...]) * ya + _sigmoid(gb_ref[...]) * yb).astype(BF16)
        merged_ref[...] = merged
        mix = _dot_nn(merged, wo_ref[...])
        mix_ref[...] = mix
        x1 = x_ref[...] + mix * _rms_scale(mix) * g2_ref[...]
        x1_ref[...] = x1
        h2_ref[...] = (x1 * _rms_scale(x1) * g3_ref[...]).astype(BF16)

    def rows(w, cb=0):
        return pl.BlockSpec((tm, w), lambda i, cb=cb: (i, cb))

    def whole(a):
        return pl.BlockSpec(a.shape, lambda i: (0, 0))

    d = D_MODEL
    blk = _nbytes((tm, d), F32) * 6 + sum(_nbytes(a.shape, BF16) for a in (wt_pa, wt_pb, w_out))
    return pl.pallas_call(
        body, name="mix_out", grid=(SEQ // tm,),
        in_specs=[rows(DIL_OUT_WIDTH), rows(FOX_WIDTH), rows(d, COL_GA // d), rows(d, COL_GB // d), rows(d),
                  whole(wt_pa), whole(wt_pb), whole(w_out), whole(g_post), whole(g_ffn_pre)],
        out_specs=[rows(d)] * 4,
        out_shape=[jax.ShapeDtypeStruct((SEQ, d), dt) for dt in (BF16, F32, F32, BF16)],
        compiler_params=pltpu.CompilerParams(dimension_semantics=("parallel",), vmem_limit_bytes=_vmem_limit(blk)),
    )(out_a, out_b, proj, proj, x, wt_pa, wt_pb, w_out, g_post, g_ffn_pre)


def _mix_out_bwd(dmix, out_a, out_b, proj, wt_pa, wt_pb, w_out, deps=()):
    tm = MIX_TILE

    def body(dm_ref, a_ref, b_ref, ga_ref, gb_ref, wpa_ref, wpb_ref, wo_ref, *rest):
        dproj_ref, dya_ref, dyb_ref, da_ref, db_ref = rest[len(deps):]
        dmerged = _dot_nt(dm_ref[...], wo_ref[...])
        ya = _dot_nn(a_ref[...].astype(BF16), wpa_ref[...])
        yb = _dot_nn(b_ref[...].astype(BF16), wpb_ref[...])
        sa, sb = _sigmoid(ga_ref[...]), _sigmoid(gb_ref[...])
        dproj_ref[:, COL_GA:COL_GA + D_MODEL] = (dmerged * ya * (sa * (1.0 - sa))).astype(BF16)
        dproj_ref[:, COL_GB:COL_GB + D_MODEL] = (dmerged * yb * (sb * (1.0 - sb))).astype(BF16)
        dproj_ref[:, COL_GB + D_MODEL:] = jnp.zeros((tm, COL_QA - COL_GB - D_MODEL), BF16)
        dya = (dmerged * sa).astype(BF16)
        dyb = (dmerged * sb).astype(BF16)
        dya_ref[...] = dya
        dyb_ref[...] = dyb
        da_ref[...] = _dot_nt(dya, wpa_ref[...])
        db_ref[...] = _dot_nt(dyb, wpb_ref[...]).astype(BF16)

    def rows(w, cb=0):
        return pl.BlockSpec((tm, w), lambda i, cb=cb: (i, cb))

    def whole(a):
        return pl.BlockSpec(a.shape, lambda i: (0, 0))

    d = D_MODEL
    blk = _nbytes((tm, d), F32) * 8 + sum(_nbytes(a.shape, BF16) for a in (wt_pa, wt_pb, w_out))
    return pl.pallas_call(
        body, name="mix_out_bwd", grid=(SEQ // tm,),
        in_specs=[rows(d), rows(DIL_OUT_WIDTH), rows(FOX_WIDTH), rows(d, COL_GA // d), rows(d, COL_GB // d),
                  whole(wt_pa), whole(wt_pb), whole(w_out)] + [_ANY] * len(deps),
        out_specs=[rows(COL_QA)] + [rows(d)] * 2 + [rows(DIL_OUT_WIDTH), rows(FOX_WIDTH)],
        out_shape=[jax.ShapeDtypeStruct((SEQ, PROJ_COLS), BF16)] + [jax.ShapeDtypeStruct((SEQ, d), BF16)] * 2
        + [jax.ShapeDtypeStruct((SEQ, DIL_OUT_WIDTH), F32), jax.ShapeDtypeStruct((SEQ, FOX_WIDTH), BF16)],
        compiler_params=pltpu.CompilerParams(dimension_semantics=("parallel",), vmem_limit_bytes=_vmem_limit(blk)),
    )(dmix, out_a, out_b, proj, proj, wt_pa, wt_pb, w_out, *deps)


FFN_TM, FFN_TN = 2048, 256


def _ffn_up(h2, wt_gate, wt_up):
    tm, tn = FFN_TM, FFN_TN

    def body(h_ref, wg_ref, wu_ref, gate_ref, up_ref, act_ref):
        gate = _dot_nt(h_ref[...], wg_ref[...])
        up = _dot_nt(h_ref[...], wu_ref[...])
        gate_ref[...] = gate
        up_ref[...] = up
        act_ref[...] = (gate * _sigmoid(gate) * up).astype(BF16)

    tile = pl.BlockSpec((tm, tn), lambda i, j: (i, j))
    w_spec = pl.BlockSpec((tn, D_MODEL), lambda i, j: (j, 0))
    return pl.pallas_call(
        body, name="ffn_up", grid=(SEQ // tm, D_FF // tn),
        in_specs=[pl.BlockSpec((tm, D_MODEL), lambda i, j: (i, 0)), w_spec, w_spec],
        out_specs=[tile, tile, tile],
        out_shape=[jax.ShapeDtypeStruct((SEQ, D_FF), dt) for dt in (F32, F32, BF16)],
        compiler_params=pltpu.CompilerParams(
            dimension_semantics=("parallel", "parallel"), vmem_limit_bytes=_vmem_limit(8 * 2**20)),
    )(h2, wt_gate, wt_up)


def _ffn_act_bwd(dff, w_down, gate, up):
    tm, tn = FFN_TM, FFN_TN

    def body(d_ref, wd_ref, gate_ref, up_ref, dgate_ref, dup_ref):
        dact = _dot_nt(d_ref[...], wd_ref[...])
        gate = gate_ref[...]
        sg = _sigmoid(gate)
        dgate_ref[...] = (dact * up_ref[...] * (sg * (1.0 + gate * (1.0 - sg)))).astype(BF16)
        dup_ref[...] = (dact * (gate * sg)).astype(BF16)

    tile = pl.BlockSpec((tm, tn), lambda i, j: (i, j))
    return pl.pallas_call(
        body, name="ffn_act_bwd", grid=(SEQ // tm, D_FF // tn),
        in_specs=[pl.BlockSpec((tm, D_MODEL), lambda i, j: (i, 0)), pl.BlockSpec((tn, D_MODEL), lambda i, j: (j, 0)),
                  tile, tile],
        out_specs=[tile, tile],
        out_shape=[jax.ShapeDtypeStruct((SEQ, D_FF), BF16)] * 2,
        compiler_params=pltpu.CompilerParams(
            dimension_semantics=("parallel", "parallel"), vmem_limit_bytes=_vmem_limit(8 * 2**20)),
    )(dff, w_down, gate, up)


EPILOGUE_TM = 512


def _loss_head(act, w_down, x1, target, g_post):
    def fn(ff, x1, tgt, g):
        r = _rms_scale(ff)
        nrm = ff * r
        err = (x1 + nrm * g) - tgt
        loss = 0.5 * jnp.sum(jnp.mean(err * err, axis=-1, keepdims=True), axis=0, keepdims=True)
        dy = err * (1.0 / D_MODEL)
        u = dy * g
        dff = r * u - ff * (r * r * r) * jnp.mean(u * ff, axis=-1, keepdims=True)
        return dy, dff, jnp.broadcast_to(loss, (1, LANES)), jnp.sum(dy * nrm, axis=0, keepdims=True)

    d = D_MODEL
    return _matmul_rowwise([(act, w_down)], fn, "ffn_down_loss", EPILOGUE_TM, [(x1, d, 0), (target, d, 0)], [g_post],
                           [(d, F32), (d, BF16)], [LANES, d])


def _post_ffn_bwd(dgate, wt_gate, dup, wt_up, x1, dy, mix, g_ffn_pre, g_mix_post, deps=()):
    def fn(dh2, x1, dy, mix, g3, g2):
        dx, dg3 = _rms_bwd(x1, dh2, g3)
        dx1 = dy + dx
        dmix, dg2 = _rms_bwd(mix, dx1, g2)
        return dx1, dmix, dg3, dg2

    d = D_MODEL
    return _matmul_rowwise([(dgate, wt_gate), (dup, wt_up)], fn, "ffn_up_bwd", EPILOGUE_TM,
                           [(x1, d, 0), (dy, d, 0), (mix, d, 0)], [g_ffn_pre, g_mix_post],
                           [(d, F32), (d, BF16)], [d, d], deps=deps)


def _input_bwd(dproj, wt_r, x, dx1, g_pre, deps=()):
    def fn(dh, x, dx1, g):
        dx, dg = _rms_bwd(x, dh, g)
        return dx1 + dx, dg

    d = D_MODEL
    return _matmul_rowwise([(dproj, wt_r)], fn, "in_proj_bwd", EPILOGUE_TM, [(x, d, 0), (dx1, d, 0)], [g_pre],
                           [(d, F32)], [d], deps=deps)


def _adam_math(w, g, m, v):
    m = ADAM_B1 * m + (1.0 - ADAM_B1) * g
    v = ADAM_B2 * v + (1.0 - ADAM_B2) * (g * g)
    m_hat = m / (1.0 - ADAM_B1 ** ADAM_STEP)
    v_hat = v / (1.0 - ADAM_B2 ** ADAM_STEP)
    delta = -ADAM_LR * (m_hat / (jnp.sqrt(v_hat) + ADAM_EPS) + ADAM_WD * w)
    return delta, m, v


def _adam(w, mine, recv, m, v, name):
    r, c = w.shape
    tc = _col_tile(r, c)

    def body(w_ref, p_ref, r_ref, m_ref, v_ref, g_ref, d_ref, nm_ref, nv_ref):
        g = ((p_ref[...] + r_ref[0].astype(F32)) + r_ref[1].astype(F32)) + r_ref[2].astype(F32)
        g_ref[...] = g
        d_ref[...], nm_ref[...], nv_ref[...] = _adam_math(w_ref[...], g, m_ref[...], v_ref[...])

    spec = pl.BlockSpec((r, tc), lambda j: (0, j))
    return pl.pallas_call(
        body, name=name, grid=(c // tc,),
        in_specs=[spec, spec, pl.BlockSpec((3, r, tc), lambda j: (0, 0, j)), spec, spec], out_specs=[spec] * 4,
        out_shape=[jax.ShapeDtypeStruct((r, c), F32)] * 4,
        compiler_params=pltpu.CompilerParams(dimension_semantics=("parallel",)),
    )(*_in_hbm(w, mine, recv, m, v))


def _adam_small(gathered, ws, ms, vs, loss_parts):
    n = len(ws)

    def body(*refs):
        outs = refs[4 * n + 1:]
        loss = refs[4 * n][0]
        for dev in range(1, N_DEV):
            loss = loss + refs[4 * n][dev]
        outs[4 * n][...] = loss
        for i in range(n):
            ga_ref, w_ref, m_ref, v_ref = (refs[j * n + i] for j in range(4))
            g = ga_ref[0]
            for dev in range(1, N_DEV):
                g = g + ga_ref[dev]
            g = g[:, :w_ref.shape[1]]
            outs[4 * i][...] = g
            outs[4 * i + 1][...], outs[4 * i + 2][...], outs[4 * i + 3][...] = _adam_math(
                w_ref[...], g, m_ref[...], v_ref[...])

    out_shape = [jax.ShapeDtypeStruct(w.shape, F32) for w in ws for _ in range(4)]
    out_shape.append(jax.ShapeDtypeStruct((1, LANES), F32))
    out = pl.pallas_call(body, name="adam_small", out_shape=out_shape)(*gathered, *ws, *ms, *vs, loss_parts)
    return [out[4 * i:4 * i + 4] for i in range(n)], out[4 * n]


_PROJ_SEGMENTS = ((3848, 5896), (None, COL_QA - 2 * D_MODEL), (0, 3840), (3840, 3848), (None, PROJ_COLS - COL_F - 8))


def _proj_weight_t(gathered):
    w = gathered.reshape(IN_COLS, D_MODEL)
    return jnp.concatenate([jnp.zeros((hi, D_MODEL), w.dtype) if lo is None else w[lo:hi] for lo, hi in _PROJ_SEGMENTS],
                           axis=0)


def _proj_weight_grad_slots(dwt_r):
    starts, at = [], 0
    for lo, hi in _PROJ_SEGMENTS:
        if lo is not None:
            starts.append((lo, hi, at))
        at += hi if lo is None else hi - lo
    slots = []
    for dev in range(N_DEV):
        pieces, lo, end = [], dev * IN_SHARD, (dev + 1) * IN_SHARD
        for seg_lo, seg_hi, seg_at in sorted(starts):
            a, b = max(lo, seg_lo), min(end, seg_hi)
            if a < b:
                pieces.append(dwt_r[seg_at + a - seg_lo:seg_at + b - seg_lo])
        slots.append(pieces[0] if len(pieces) == 1 else jnp.concatenate(pieces, axis=0))
    return jnp.stack(slots)


def kernel(x, w_in, w_proj_a, w_proj_b, w_out, b_forget, w_ffn_gate, w_ffn_up, w_ffn_down, norm_mix_pre, norm_mix_post, norm_ffn_pre, norm_ffn_post, loss_target, m_w_in, m_w_proj_a, m_w_proj_b, m_w_out, m_b_forget, m_w_ffn_gate, m_w_ffn_up, m_w_ffn_down, m_norm_mix_pre, m_norm_mix_post, m_norm_ffn_pre, m_norm_ffn_post, v_w_in, v_w_proj_a, v_w_proj_b, v_w_out, v_b_forget, v_w_ffn_gate, v_w_ffn_up, v_w_ffn_down, v_norm_mix_pre, v_norm_mix_post, v_norm_ffn_pre, v_norm_ffn_post):
    d = D_MODEL
    names = ("w_in", "w_proj_a", "w_proj_b", "w_out", "w_ffn_gate", "w_ffn_up", "w_ffn_down")
    col_sharded = ("w_in", "w_ffn_gate", "w_ffn_up")

    def row_shards(arrs):
        return {k: (a[0].T if k in col_sharded else a[0]) for k, a in zip(names, arrs)}

    shards = row_shards((w_in, w_proj_a, w_proj_b, w_out, w_ffn_gate, w_ffn_up, w_ffn_down))
    moments_m = row_shards((m_w_in, m_w_proj_a, m_w_proj_b, m_w_out, m_w_ffn_gate, m_w_ffn_up, m_w_ffn_down))
    moments_v = row_shards((v_w_in, v_w_proj_a, v_w_proj_b, v_w_out, v_w_ffn_gate, v_w_ffn_up, v_w_ffn_down))
    pos = jnp.stack([lax.axis_index("c"), 2 * lax.axis_index("x") + lax.axis_index("y")]).astype(jnp.int32)
    x2, target = x[0], loss_target[0]

    me = 4 * lax.axis_index("x") + 2 * lax.axis_index("y") + lax.axis_index("c")
    mid_names, ffn_names = names[1:4], names[4:]
    first_names, later_names = names[:1], names[1:]
    shards16 = {k: shards[k].astype(BF16) for k in names}

    def landing(k):
        return lax.dynamic_update_slice(lax.empty((N_DEV,) + shards[k].shape, BF16), shards16[k][None], (me, 0, 0))

    ag_first = _exchange_start("ag_first_chips_start", _gather_chips_copies, [shards16[k] for k in first_names],
                               [landing(k) for k in first_names], 3 * len(first_names))
    h = _rowwise(lambda xb, g: xb * _rms_scale(xb) * g, "norm_mix_pre", SEQ, 256, [(x2, d, 0)], [norm_mix_pre],
                 [(d, BF16)], deps=[ag_first.token])[0]
    _, lands = _exchange_wait("ag_first_chips_wait", ag_first, [h])
    ag_first = _exchange_start("ag_first_sibling_start", _gather_sibling_copies, [], lands, 4 * len(first_names))
    ag_later = _exchange_start("ag_later_chips_start", _gather_chips_copies, [shards16[k] for k in later_names],
                               [landing(k) for k in later_names], 3 * len(later_names), after=[ag_first.token])
    gathered = dict(zip(first_names, _exchange_wait("ag_first_sibling_wait", ag_first, [ag_later.token])[1]))
    wt_r = _proj_weight_t(gathered["w_in"])

    proj = _matmul([(h, wt_r)], "nt", F32, "in_proj", 1024, 896, 1024)
    tables = _rope_tables()
    o_dil, lse_dil = zip(*[_dil_fwd(g, proj, tables) for g in range(len(DILATIONS))])
    out_a = _dil_combine(o_dil, lse_dil)
    _, lands = _exchange_wait("ag_later_chips_wait", ag_later, [out_a])
    ag_later = _exchange_start("ag_later_sibling_start", _gather_sibling_copies, [], lands, 4 * len(later_names))

    b_pad = jnp.pad(b_forget, ((0, 0), (0, LANES - N_FOX_HEADS)))
    f_rows = _fox_gate(proj, b_pad, deps=[ag_later.token])
    out_b, lse_fox = _fox_fwd(proj, f_rows)

    gathered = dict(zip(later_names, _exchange_wait("ag_later_sibling_wait", ag_later, [out_b])[1]))
    wt_pa = gathered["w_proj_a"].transpose(1, 0, 2).reshape(DIL_OUT_WIDTH, d)
    wt_pb = gathered["w_proj_b"].transpose(1, 0, 2).reshape(FOX_WIDTH, d)
    w_o = gathered["w_out"].reshape(d, d)
    wt_g = gathered["w_ffn_gate"].reshape(D_FF, d)
    wt_u = gathered["w_ffn_up"].reshape(D_FF, d)
    w_d = gathered["w_ffn_down"].reshape(D_FF, d)
    merged, mix, x1, h2 = _mix_out(out_a, out_b, proj, x2, wt_pa, wt_pb, w_o, norm_mix_post, norm_ffn_pre)

    gate, up, act = _ffn_up(h2, wt_g, wt_u)
    dy, dff, loss_part, dg_ffn_post = _loss_head(act, w_d, x1, target, norm_ffn_post)

    dgate, dup = _ffn_act_bwd(dff, w_d, gate, up)
    grads_t = {}
    grads_t["w_ffn_down"] = _matmul([(act, dff)], "tn", F32, "grad_w_ffn_down", 1408, 512, 2048)
    grads_t["w_ffn_gate"] = _matmul([(dgate, h2)], "tn", F32, "grad_w_ffn_gate", 1408, 512, 2048)
    grads_t["w_ffn_up"] = _matmul([(dup, h2)], "tn", F32, "grad_w_ffn_up", 1408, 512, 2048)
    rs_ffn = _ReduceScatter("ffn", {k: grads_t[k] for k in ffn_names}, pos)
    dx1, dmix, dg_ffn_pre, dg_mix_post = _post_ffn_bwd(dgate, wt_g, dup, wt_u, x1, dy, mix, norm_ffn_pre, norm_mix_post,
                                                       deps=[rs_ffn.token])
    rs_ffn.start_chips([dmix])

    dproj, dya, dyb, d_out_a, d_out_b = _mix_out_bwd(dmix, out_a, out_b, proj, wt_pa, wt_pb, w_o, deps=[rs_ffn.token])
    grads_t["w_out"] = _matmul([(merged, dmix)], "tn", F32, "grad_w_out", 1024, 1024, 1024)
    def column_slots(g):
        return g.reshape(g.shape[0], N_DEV, LANES).transpose(1, 0, 2)

    grads_t["w_proj_a"] = column_slots(_matmul([(out_a, dya)], "tn", F32, "grad_w_proj_a", DIL_OUT_WIDTH, 1024, SEQ))
    grads_t["w_proj_b"] = column_slots(_matmul([(out_b, dyb)], "tn", F32, "grad_w_proj_b", FOX_WIDTH, 1024, SEQ))
    rs_mid = _ReduceScatter("mid", {k: grads_t[k] for k in mid_names}, pos)

    do_dil, c_dil = _dil_combine_bwd(d_out_a, o_dil, lse_dil, deps=[rs_mid.token])
    rs_mid.start_chips([c_dil[0]])
    dproj, d_cum = _fox_bwd(proj, d_out_b, lse_fox, f_rows, dproj)
    d_cum_rows = jnp.pad(d_cum[:, :2].reshape(N_FOX_HEADS, SEQ), ((0, F_ROWS - N_FOX_HEADS), (0, 0)))
    dproj, db_part = _fox_gate_bwd(d_cum_rows, proj, b_pad, dproj)
    for g in range(len(DILATIONS)):
        dproj = _dil_bwd(g, proj, tables, do_dil[g], lse_dil[g], c_dil[g], dproj, deps=[rs_mid.token])

    dwt_r = _matmul([(dproj, h)], "tn", F32, "grad_w_in", 896, 1024, 2048)
    rs_in = _ReduceScatter("in", {"w_in": _proj_weight_grad_slots(dwt_r)}, pos)
    def finish(rs, after):
        return {k: _adam(shards[k], mine, recv, moments_m[k], moments_v[k], "adam_" + k)
                for k, (mine, recv) in rs.finish(after).items()}

    done = finish(rs_ffn, [rs_in.token])
    rs_in.start_chips([done[k][0] for k in ffn_names])
    grad_x, dg_mix_pre = _input_bwd(dproj, wt_r, x2, dx1, norm_mix_pre, deps=[rs_in.token])
    done.update(finish(rs_mid, [grad_x]))

    small_all = _all_gather([dg_mix_pre, dg_mix_post, dg_ffn_pre, dg_ffn_post, db_part, loss_part],
                            "small_grads_all_gather", deps=[done[k][0] for k in mid_names])
    small, loss = _adam_small(small_all[:5], [norm_mix_pre, norm_mix_post, norm_ffn_pre, norm_ffn_post, b_forget],
                              [m_norm_mix_pre, m_norm_mix_post, m_norm_ffn_pre, m_norm_ffn_post, m_b_forget],
                              [v_norm_mix_pre, v_norm_mix_post, v_norm_ffn_pre, v_norm_ffn_post, v_b_forget],
                              small_all[5])

    done.update(finish(rs_in, [small[0][0]]))

    def leaves(i):
        def nat(k):
            a = done[k][i]
            return (a.T if k in col_sharded else a)[None]

        return [nat("w_in"), nat("w_proj_a"), nat("w_proj_b"), nat("w_out"), small[4][i],
                nat("w_ffn_gate"), nat("w_ffn_up"), nat("w_ffn_down"), *[small[r][i] for r in range(4)]]

    return (loss[0, 0], grad_x[None], *leaves(0), *leaves(1), *leaves(2), *leaves(3))
```

```python
import functools
import math

import jax
import jax.numpy as jnp
import numpy as np
from jax import lax
from jax.experimental import pallas as pl
from jax.experimental.pallas import tpu as pltpu

F32 = jnp.float32
BF16 = jnp.bfloat16
MESH = pl.DeviceIdType.MESH

D_MODEL = 1024
SEQ = 2048
HEAD_DIM = 64
BLOCK = 128
N_BLOCKS = SEQ // BLOCK
DILATIONS = (1, 4, 16)
N_FOX_HEADS = 8
DIL_WIDTH = 768
DIL_OUT_WIDTH = 256
FOX_WIDTH = 512
D_FF = 2816
ROPE_THETA = 500000.0
ROPE_DIM = HEAD_DIM // 4
ROPE_HALF = ROPE_DIM // 2
EPS = 1e-6
NEG_INF = -1e30
QK_SCALE = 1.0 / math.sqrt(HEAD_DIM)
IN_COLS = 5896
N_DEV = 8
IN_SHARD = IN_COLS // N_DEV

ADAM_LR = 0.001
ADAM_B1 = 0.9
ADAM_B2 = 0.999
ADAM_EPS = 1e-08
ADAM_WD = 0.01
ADAM_STEP = 10

V7X_VMEM_BYTES = 64 * 2**20
LANES = 128
SUBLANES = 8

PROJ_COLS = 6272
COL_GA, COL_GB = 0, 1024
COL_QA, COL_KA, COL_VA = 2304, 3072, 3840
COL_QB, COL_KB, COL_VB = 4608, 5120, 5632
COL_F = 6144
F_ROWS = 16


def _vmem_limit(block_bytes):
    want = 2 * block_bytes + 16 * 2**20
    return int(min(max(want, 32 * 2**20), V7X_VMEM_BYTES - 8 * 2**20))


def _nbytes(shape, dtype):
    return math.prod(shape) * jnp.dtype(dtype).itemsize


def _in_hbm(*arrays):
    return [pltpu.with_memory_space_constraint(a, pltpu.HBM) for a in arrays]


def _dot(a, b, dims):
    return lax.dot_general(a, b, (dims, ((), ())), preferred_element_type=F32)


def _dot_nn(a, b):
    return _dot(a, b, ((1,), (0,)))


def _dot_nt(a, b):
    return _dot(a, b, ((1,), (1,)))


def _dot_tn(a, b):
    return _dot(a, b, ((0,), (0,)))


def _sigmoid(z):
    return 1.0 / (1.0 + jnp.exp(-z))


def _split3(x):
    hi = x.astype(BF16)
    r1 = x - hi.astype(F32)
    mid = r1.astype(BF16)
    lo = (r1 - mid.astype(F32)).astype(BF16)
    return hi, mid, lo


def _dot3_nn(x, ones_matrix):
    hi, mid, lo = _split3(x)
    return (_dot_nn(hi, ones_matrix) + _dot_nn(mid, ones_matrix)) + _dot_nn(lo, ones_matrix)


def _rowwise(fn, name, n_rows, tm, row_ins, bcast_ins, row_outs, acc_outs=(), deps=()):
    n_in = len(row_ins) + len(bcast_ins)
    n_ro = len(row_outs)

    def body(*refs):
        res = fn(*[r[...] for r in refs[:n_in]])
        if not isinstance(res, (tuple, list)):
            res = (res,)
        outs = refs[n_in + len(deps):]
        for r, o in zip(res[:n_ro], outs[:n_ro]):
            o[...] = r.astype(o.dtype)
        first = pl.program_id(0) == 0
        for r, o in zip(res[n_ro:], outs[n_ro:]):
            _accumulate(o, r, first)

    in_specs = [pl.BlockSpec((tm, w), lambda i, cb=cb: (i, cb)) for _, w, cb in row_ins]
    in_specs += [pl.BlockSpec(a.shape, lambda i: (0, 0)) for a in bcast_ins]
    in_specs += [pl.BlockSpec(memory_space=pl.ANY)] * len(deps)
    out_specs = [pl.BlockSpec((tm, w), lambda i: (i, 0)) for w, _ in row_outs]
    out_specs += [pl.BlockSpec((1, w), lambda i: (0, 0)) for w in acc_outs]
    out_shape = [jax.ShapeDtypeStruct((n_rows, w), dt) for w, dt in row_outs]
    out_shape += [jax.ShapeDtypeStruct((1, w), F32) for w in acc_outs]
    blk = sum(_nbytes((tm, w), a.dtype) for a, w, _ in row_ins) + sum(_nbytes((tm, w), dt) for w, dt in row_outs)
    return pl.pallas_call(
        body, name=name, grid=(n_rows // tm,), in_specs=in_specs, out_specs=out_specs, out_shape=out_shape,
        compiler_params=pltpu.CompilerParams(
            dimension_semantics=("arbitrary" if acc_outs else "parallel",), vmem_limit_bytes=_vmem_limit(3 * blk)),
    )(*_in_hbm(*[a for a, _, _ in row_ins], *bcast_ins), *deps)


def _accumulate(o_ref, part, first):
    @pl.when(first)
    def _():
        o_ref[...] = part

    @pl.when(jnp.logical_not(first))
    def _():
        o_ref[...] += part


_MM_DIMS = {"nn": ((1,), (0,)), "nt": ((1,), (1,)), "tn": ((0,), (0,))}


def _matmul(pairs, mode, out_dtype, name, tm, tn, tk, deps=()):
    a0, b0 = pairs[0]
    if mode == "tn":
        kk, m = a0.shape
    else:
        m, kk = a0.shape
    n = b0.shape[0] if mode == "nt" else b0.shape[1]
    assert m % tm == 0 and n % tn == 0 and kk % tk == 0, (name, m, n, kk)
    nk = kk // tk
    n_pairs = len(pairs)
    dims = _MM_DIMS[mode]
    n_in = 2 * n_pairs + len(deps)

    def body(*refs):
        o_ref = refs[n_in]
        part = None
        for p in range(n_pairs):
            d = _dot(refs[2 * p][...].astype(BF16), refs[2 * p + 1][...].astype(BF16), dims)
            part = d if part is None else part + d
        if nk == 1:
            o_ref[...] = part.astype(o_ref.dtype)
            return
        acc = refs[n_in + 1]
        k = pl.program_id(2)

        @pl.when(k == 0)
        def _():
            acc[...] = part

        @pl.when(k > 0)
        def _():
            acc[...] += part

        @pl.when(k == nk - 1)
        def _():
            o_ref[...] = acc[...].astype(o_ref.dtype)

    if mode == "tn":
        a_spec = pl.BlockSpec((tk, tm), lambda i, j, k: (k, i))
    else:
        a_spec = pl.BlockSpec((tm, tk), lambda i, j, k: (i, k))
    if mode == "nt":
        b_spec = pl.BlockSpec((tn, tk), lambda i, j, k: (j, k))
    else:
        b_spec = pl.BlockSpec((tk, tn), lambda i, j, k: (k, j))
    blk = sum(_nbytes((tm, tk), a.dtype) + _nbytes((tk, tn), b.dtype) for a, b in pairs) + 2 * _nbytes((tm, tn), F32)
    flat = [a for pair in pairs for a in pair]
    return pl.pallas_call(
        body, name=name, grid=(m // tm, n // tn, nk),
        in_specs=[a_spec, b_spec] * n_pairs + [pl.BlockSpec(memory_space=pl.ANY)] * len(deps),
        out_specs=pl.BlockSpec((tm, tn), lambda i, j, k: (i, j)),
        out_shape=jax.ShapeDtypeStruct((m, n), out_dtype),
        scratch_shapes=[] if nk == 1 else [pltpu.VMEM((tm, tn), F32)],
        compiler_params=pltpu.CompilerParams(
            dimension_semantics=("parallel", "parallel", "arbitrary"), vmem_limit_bytes=_vmem_limit(blk)),
    )(*flat, *deps)


def _matmul_rowwise(pairs, fn, name, tm, row_ins, bcast_ins, row_outs, acc_outs=(), deps=()):
    m = pairs[0][0].shape[0]
    n_mm, n_in = 2 * len(pairs), len(row_ins) + len(bcast_ins)
    n_ro = len(row_outs)

    def body(*refs):
        prod = None
        for p in range(len(pairs)):
            part = _dot_nn(refs[2 * p][...].astype(BF16), refs[2 * p + 1][...].astype(BF16))
            prod = part if prod is None else prod + part
        res = fn(prod, *[r[...] for r in refs[n_mm:n_mm + n_in]])
        outs = refs[n_mm + n_in + len(deps):]
        for r, o in zip(res[:n_ro], outs[:n_ro]):
            o[...] = r.astype(o.dtype)
        first = pl.program_id(0) == 0
        for r, o in zip(res[n_ro:], outs[n_ro:]):
            _accumulate(o, r, first)

    in_specs = []
    for a, b in pairs:
        in_specs += [pl.BlockSpec((tm, a.shape[1]), lambda i: (i, 0)),
                     pl.BlockSpec(b.shape, lambda i: (0, 0), pipeline_mode=pl.Buffered(1))]
    in_specs += [pl.BlockSpec((tm, w), lambda i, cb=cb: (i, cb)) for _, w, cb in row_ins]
    in_specs += [pl.BlockSpec(a.shape, lambda i: (0, 0)) for a in bcast_ins]
    in_specs += [_ANY] * len(deps)
    out_specs = [pl.BlockSpec((tm, w), lambda i: (i, 0)) for w, _ in row_outs]
    out_specs += [pl.BlockSpec((1, w), lambda i: (0, 0)) for w in acc_outs]
    out_shape = [jax.ShapeDtypeStruct((m, w), dt) for w, dt in row_outs]
    out_shape += [jax.ShapeDtypeStruct((1, w), F32) for w in acc_outs]
    blk = sum(_nbytes((tm, a.shape[1]), a.dtype) + _nbytes(b.shape, b.dtype) // 2 for a, b in pairs)
    blk += sum(_nbytes((tm, w), a.dtype) for a, w, _ in row_ins) + sum(_nbytes((tm, w), dt) for w, dt in row_outs)
    return pl.pallas_call(
        body, name=name, grid=(m // tm,), in_specs=in_specs, out_specs=out_specs, out_shape=out_shape,
        compiler_params=pltpu.CompilerParams(dimension_semantics=("arbitrary",), vmem_limit_bytes=_vmem_limit(blk)),
    )(*[a for pair in pairs for a in pair], *[a for a, _, _ in row_ins], *bcast_ins, *deps)


def _rms_scale(x):
    return lax.rsqrt(jnp.mean(x * x, axis=-1, keepdims=True) + EPS)


def _rms_bwd(xin, dyn, g):
    r = _rms_scale(xin)
    u = dyn * g
    dx = r * u - xin * (r * r * r) * jnp.mean(u * xin, axis=-1, keepdims=True)
    dg = jnp.sum(dyn * xin * r, axis=0, keepdims=True)
    return dx, dg


def _mesh_pos():
    return lax.axis_index("x"), lax.axis_index("y"), lax.axis_index("c")


def _all_gather(xs, name, deps=()):
    n = len(xs)

    def body(*refs):
        x_refs, out_refs = refs[:n], refs[n + len(deps):2 * n + len(deps)]
        send_sems, recv_sems, local_sems = refs[2 * n + len(deps):]
        mx, my, mc = _mesh_pos()
        me, sib = (mx, my, mc), (mx, my, 1 - mc)
        chips = [(1 - mx, my), (mx, 1 - my), (1 - mx, 1 - my)]

        def slot(a, dev):
            px, py, pc = dev
            return out_refs[a].at[4 * px + 2 * py + pc]

        def copy(k, a, block, to, src=None):
            return pltpu.make_async_remote_copy(
                src_ref=slot(a, block) if src is None else src, dst_ref=slot(a, block),
                send_sem=send_sems.at[a * 7 + k], recv_sem=recv_sems.at[a * 7 + k],
                device_id=to, device_id_type=MESH)

        mine = [pltpu.make_async_copy(x_refs[a], slot(a, me), local_sems.at[a]) for a in range(n)]
        for cp in mine:
            cp.start()
        first = []
        for a in range(n):
            first.append(copy(0, a, me, sib, x_refs[a]))
            first += [copy(1 + j, a, me, (*chip, mc), x_refs[a]) for j, chip in enumerate(chips)]
        for cp in first:
            cp.start()
        passed = []
        for a in range(n):
            for j, chip in enumerate(chips):
                copy(1 + j, a, (*chip, mc), me).wait_recv()
                fwd = copy(4 + j, a, (*chip, mc), sib)
                fwd.start()
                passed.append(fwd)
        for a in range(n):
            copy(0, a, sib, me).wait_recv()
            for j, chip in enumerate(chips):
                copy(4 + j, a, (*chip, 1 - mc), me).wait_recv()
        for cp in first + passed:
            cp.wait_send()
        for cp in mine:
            cp.wait()

    hbm = pl.BlockSpec(memory_space=pl.ANY)
    return pl.pallas_call(
        body, name=name,
        out_shape=[jax.ShapeDtypeStruct((N_DEV,) + x.shape, x.dtype) for x in xs],
        in_specs=[hbm] * (n + len(deps)), out_specs=[hbm] * n,
        scratch_shapes=[pltpu.SemaphoreType.DMA((7 * n,)), pltpu.SemaphoreType.DMA((7 * n,)),
                        pltpu.SemaphoreType.DMA((n,))],
    )(*xs, *deps)


_HBM = pl.BlockSpec(memory_space=pltpu.HBM)
_SEM = pl.BlockSpec(memory_space=pltpu.SEMAPHORE)
_ANY = pl.BlockSpec(memory_space=pl.ANY)
_DATAFLOW = pltpu.SideEffectType.DATAFLOW_SIDE_EFFECTING


def _flip_peer(flip):
    mx, my, mc = _mesh_pos()
    return (1 - mx if flip & 2 else mx, 1 - my if flip & 1 else my, mc)


def _remote(src, dst, send_sems, recv_sems, k, peer):
    return pltpu.make_async_remote_copy(src_ref=src, dst_ref=dst, send_sem=send_sems.at[k], recv_sem=recv_sems.at[k],
                                        device_id=peer, device_id_type=MESH)


def _gather_chips_copies(srcs, lands, send_sems, recv_sems):
    mx, my, mc = _mesh_pos()
    me = 4 * mx + 2 * my + mc
    return [_remote(srcs[a], lands[a].at[me], send_sems, recv_sems, 3 * a + flip - 1, _flip_peer(flip))
            for a in range(len(srcs)) for flip in (1, 2, 3)]


def _gather_sibling_copies(srcs, lands, send_sems, recv_sems):
    mx, my, mc = _mesh_pos()
    return [_remote(lands[a].at[2 * k + mc], lands[a].at[2 * k + mc], send_sems, recv_sems, 4 * a + k, (mx, my, 1 - mc))
            for a in range(len(lands)) for k in range(4)]


def _scatter_sibling_copies(srcs, lands, send_sems, recv_sems):
    mx, my, mc = _mesh_pos()
    return [_remote(srcs[a].at[k, 1 - mc], lands[a].at[k], send_sems, recv_sems, 4 * a + k, (mx, my, 1 - mc))
            for a in range(len(srcs)) for k in range(4)]


def _scatter_chips_copies(srcs, lands, send_sems, recv_sems):
    mx, my, _ = _mesh_pos()
    k0 = 2 * mx + my
    return [_remote(srcs[a].at[jnp.bitwise_xor(k0, flip)], lands[a].at[flip - 1], send_sems, recv_sems,
                    3 * a + flip - 1, _flip_peer(flip))
            for a in range(len(srcs)) for flip in (1, 2, 3)]


class _Exchange:
    def __init__(self, copies, n_src, send_sems, recv_sems, thru, token):
        self.copies, self.n_src, self.send_sems, self.recv_sems, self.thru, self.token = (
            copies, n_src, send_sems, recv_sems, thru, token)


def _exchange_start(name, copies, srcs, lands, n_copies, after=()):
    bufs = list(srcs) + list(lands)
    nb, ns = len(bufs), len(srcs)

    def body(*refs):
        send_sems, recv_sems = refs[nb + len(after)], refs[nb + len(after) + 1]
        for cp in copies(refs[:ns], refs[ns:nb], send_sems, recv_sems):
            cp.start()
        refs[-1][...] = jnp.zeros_like(refs[-1])

    out = pl.pallas_call(
        body, name=name,
        out_shape=(pltpu.SemaphoreType.DMA((n_copies,)), pltpu.SemaphoreType.DMA((n_copies,)),
                   *[pltpu.HBM(b.shape, b.dtype) for b in bufs], jax.ShapeDtypeStruct((SUBLANES, LANES), F32)),
        in_specs=[_HBM] * nb + [_ANY] * len(after),
        out_specs=(_SEM, _SEM, *[_HBM] * nb, pl.BlockSpec(memory_space=pltpu.VMEM)),
        input_output_aliases={i: 2 + i for i in range(nb)},
        compiler_params=pltpu.CompilerParams(has_side_effects=_DATAFLOW),
    )(*[pltpu.with_memory_space_constraint(b, pltpu.HBM) for b in bufs], *after)
    return _Exchange(copies, ns, out[0], out[1], list(out[2:2 + nb]), out[-1])


def _exchange_wait(name, ex, after):
    nb, ns = len(ex.thru), ex.n_src

    def body(*refs):
        for cp in ex.copies(refs[:ns], refs[ns:nb], refs[nb], refs[nb + 1]):
            cp.wait_send()
            cp.wait_recv()

    out = pl.pallas_call(
        body, name=name, out_shape=tuple(pltpu.HBM(b.shape, b.dtype) for b in ex.thru),
        in_specs=[_HBM] * nb + [_SEM, _SEM] + [_ANY] * len(after), out_specs=tuple([_HBM] * nb),
        input_output_aliases={i: i for i in range(nb)},
        compiler_params=pltpu.CompilerParams(has_side_effects=_DATAFLOW),
    )(*ex.thru, ex.send_sems, ex.recv_sems, *after)
    return list(out[:ns]), list(out[ns:])


def _col_tile(r, c):
    return next(t for t in (1024, 512, 256, 128) if c % t == 0 and (r * t * 4 <= 2**20 or t == 128))


def _add_sibling(g4, recv, pos, name):
    _, _, r, c = g4.shape
    tc = _col_tile(r, c)

    def body(pos_ref, g_ref, r_ref, o16_ref, mine_ref):
        s = g_ref[0, 0] + r_ref[0]
        o16_ref[0] = s.astype(BF16)

        @pl.when(pl.program_id(1) == pos_ref[1])
        def _():
            mine_ref[...] = s

    slot = pl.BlockSpec((1, r, tc), lambda j, k, pos_ref: (k, 0, j))
    return pl.pallas_call(
        body, name=name,
        out_shape=[jax.ShapeDtypeStruct((4, r, c), BF16), jax.ShapeDtypeStruct((r, c), F32)],
        grid_spec=pltpu.PrefetchScalarGridSpec(
            num_scalar_prefetch=1, grid=(c // tc, 4),
            in_specs=[pl.BlockSpec((1, 1, r, tc), lambda j, k, pos_ref: (k, pos_ref[0], 0, j)), slot],
            out_specs=[slot, pl.BlockSpec((r, tc), lambda j, k, pos_ref: (0, j))]),
        compiler_params=pltpu.CompilerParams(dimension_semantics=("parallel", "arbitrary")),
    )(pos, *_in_hbm(g4, recv))


class _ReduceScatter:
    def __init__(self, tag, grads_t, pos):
        self.tag, self.pos, self.names = tag, pos, list(grads_t)
        g4s = [g.reshape(4, 2, g.size // (N_DEV * g.shape[-1]), g.shape[-1]) for g in grads_t.values()]
        lands = [lax.empty((4,) + g.shape[2:], F32) for g in g4s]
        self.ex = _exchange_start(f"rs_{tag}_sibling_start", _scatter_sibling_copies, g4s, lands, 4 * len(g4s))
        self.token = self.ex.token

    def start_chips(self, after):
        g4s, from_sibling = _exchange_wait(f"rs_{self.tag}_sibling_wait", self.ex, after)
        parts = [_add_sibling(g4, rv, self.pos, f"rs_add_sibling_{k}")
                 for k, g4, rv in zip(self.names, g4s, from_sibling)]
        self.mine = [mine for _, mine in parts]
        p16s = [p16 for p16, _ in parts]
        lands = [lax.empty((3,) + p.shape[1:], BF16) for p in p16s]
        self.ex = _exchange_start(f"rs_{self.tag}_chips_start", _scatter_chips_copies, p16s, lands, 3 * len(p16s))
        self.token = self.ex.token

    def finish(self, after):
        _, from_chips = _exchange_wait(f"rs_{self.tag}_chips_wait", self.ex, after)
        return dict(zip(self.names, zip(self.mine, from_chips)))


def _rope_tables():
    positions = np.arange(SEQ, dtype=np.float32)
    inv_freq = np.power(np.float32(ROPE_THETA), -np.arange(0, ROPE_DIM, 2, dtype=np.float32) / np.float32(ROPE_DIM))
    ang = (positions[:, None] * inv_freq[None, :]).astype(np.float32)
    cos, sin = np.cos(ang).astype(np.float32), np.sin(ang).astype(np.float32)
    ones = np.ones((SEQ, HEAD_DIM - ROPE_DIM), np.float32)
    zeros8 = np.zeros((SEQ, ROPE_HALF), np.float32)
    zeros = np.zeros((SEQ, HEAD_DIM - ROPE_DIM), np.float32)
    c_head = np.concatenate([cos, cos, ones], axis=1)
    s1_head = np.concatenate([-sin, zeros8, zeros], axis=1)
    s2_head = np.concatenate([zeros8, sin, zeros], axis=1)
    return tuple(jnp.asarray(np.concatenate([t, t], axis=1)) for t in (c_head, s1_head, s2_head))


def _rope_apply(x, c, s1, s2):
    w = x.shape[1]
    return x * c + pltpu.roll(x, w - ROPE_HALF, 1) * s1 + pltpu.roll(x, ROPE_HALF, 1) * s2


def _rope_apply_t(dy, c, s1, s2):
    w = dy.shape[1]
    return dy * c + pltpu.roll(dy * s1, ROPE_HALF, 1) + pltpu.roll(dy * s2, w - ROPE_HALF, 1)


def _dil_prev_limit(has_prev):
    return jnp.where(has_prev, 0, BLOCK)


def _dil_valid(limit):
    row = lax.broadcasted_iota(jnp.int32, (BLOCK, 2 * BLOCK), 0)
    col = lax.broadcasted_iota(jnp.int32, (BLOCK, 2 * BLOCK), 1)
    dist = col - row
    return jnp.logical_and(dist >= jnp.where(col < BLOCK, limit, -BLOCK), dist <= BLOCK)


def _upper_half():
    return lax.broadcasted_iota(jnp.int32, (1, LANES), 1) >= HEAD_DIM


def _stack_heads(x):
    upper = _upper_half()
    return jnp.concatenate([jnp.where(upper, 0, x), jnp.where(upper, x, 0)], axis=0)


def _unstack_heads(y):
    n = y.shape[0] // 2
    return jnp.where(_upper_half(), y[n:], y[:n])


def _head_columns(t):
    return jnp.concatenate([t[:, 0:1], t[:, HEAD_DIM:HEAD_DIM + 1]], axis=0)


def _dil_rows(n, d):
    per = N_BLOCKS // d
    r, lb = n // per, n % per

    def rows(b):
        start = b * (BLOCK * d) + r
        return pl.ds(pl.multiple_of(start, BLOCK), BLOCK) if d == 1 else pl.ds(start, BLOCK, stride=d)

    return rows(lb), rows(jnp.maximum(lb - 1, 0)), lb > 0


def _dil_rotate(q_ref, k_ref, c_ref, s1_ref, s2_ref, q_rot, k_rot):
    tabs = (c_ref[...], s1_ref[...], s2_ref[...])
    q_rot[...] = _rope_apply(q_ref[...], *tabs) * QK_SCALE
    k_rot[...] = _rope_apply(k_ref[...], *tabs)


def _dil_specs(g):
    def col(base):
        return pl.BlockSpec((SEQ, LANES), lambda p: (0, base // LANES + 2 * g + p))

    table = pl.BlockSpec((SEQ, LANES), lambda p: (0, 0))
    return [col(COL_QA), col(COL_KA), col(COL_VA)], [table] * 3


def _store_columns(blocks, dproj_ref, cols, sem):
    copies = [pltpu.make_async_copy(b, dproj_ref.at[:, pl.ds(pl.multiple_of(c * LANES, LANES), LANES)], sem.at[i])
              for i, (b, c) in enumerate(zip(blocks, cols))]
    for cp in copies:
        cp.start()
    for cp in copies:
        cp.wait()


def _dil_fwd(g, proj, tables):
    d = DILATIONS[g]
    one_block = d == N_BLOCKS

    def body(q_ref, k_ref, v_ref, c_ref, s1_ref, s2_ref, o_ref, lse_ref, q_rot, k_rot):
        upper = _upper_half()
        _dil_rotate(q_ref, k_ref, c_ref, s1_ref, s2_ref, q_rot, k_rot)

        def block(n, carry):
            rows, prev, has_prev = _dil_rows(n, d)
            qb = q_rot[rows, :].astype(BF16)
            kw, vw = k_rot[rows, :].astype(BF16), v_ref[rows, :].astype(BF16)
            if one_block:
                row = lax.broadcasted_iota(jnp.int32, (BLOCK, BLOCK), 0)
                valid = lax.broadcasted_iota(jnp.int32, (BLOCK, BLOCK), 1) <= row
            else:
                kw = jnp.concatenate([k_rot[prev, :].astype(BF16), kw], axis=0)
                vw = jnp.concatenate([v_ref[prev, :].astype(BF16), vw], axis=0)
                valid = _dil_valid(_dil_prev_limit(has_prev))
            s = jnp.where(jnp.concatenate([valid, valid], axis=0), _dot_nt(_stack_heads(qb), kw), NEG_INF)
            m = jnp.max(s, axis=-1, keepdims=True)
            p = jnp.exp(s - m)
            den = jnp.sum(p, axis=-1, keepdims=True)
            o_ref[rows, :] = _unstack_heads(_dot_nn((p * (1.0 / den)).astype(BF16), vw))
            lse = m + jnp.log(den)
            lse_ref[rows, :] = jnp.where(upper, lse[BLOCK:], lse[:BLOCK])
            return carry

        lax.fori_loop(0, N_BLOCKS, block, 0, unroll=4)

    qkv, tabs = _dil_specs(g)
    out = pl.BlockSpec((SEQ, LANES), lambda p: (0, p))
    return pl.pallas_call(
        body, name=f"dil_attn_fwd_{g}", grid=(2,), in_specs=qkv + tabs, out_specs=[out, out],
        out_shape=[jax.ShapeDtypeStruct((SEQ, DIL_OUT_WIDTH), F32)] * 2,
        scratch_shapes=[pltpu.VMEM((SEQ, LANES), F32)] * 2,
        compiler_params=pltpu.CompilerParams(dimension_semantics=("parallel",)),
    )(*_in_hbm(proj, proj, proj, *tables))


def _dil_bwd(g, proj, tables, do, lse, c, dproj, deps=()):
    d = DILATIONS[g]
    one_block = d == N_BLOCKS

    def body(q_ref, k_ref, v_ref, c_ref, s1_ref, s2_ref, do_ref, lse_ref, cc_ref, dproj_in, *rest):
        dproj_ref, dq_acc, dk_acc, dv_acc, dq_out, dk_out, dv_out, q_rot, k_rot, sem = rest[len(deps):]
        dk_acc[...] = jnp.zeros_like(dk_acc)
        dv_acc[...] = jnp.zeros_like(dv_acc)
        _dil_rotate(q_ref, k_ref, c_ref, s1_ref, s2_ref, q_rot, k_rot)

        def block(n, carry):
            rows, prev, has_prev = _dil_rows(n, d)
            qb = q_rot[rows, :].astype(BF16)
            dob = do_ref[rows, :].astype(BF16)
            kw, vw = k_rot[rows, :].astype(BF16), v_ref[rows, :].astype(BF16)
            if one_block:
                row = lax.broadcasted_iota(jnp.int32, (BLOCK, BLOCK), 0)
                valid = lax.broadcasted_iota(jnp.int32, (BLOCK, BLOCK), 1) <= row
            else:
                kw = jnp.concatenate([k_rot[prev, :].astype(BF16), kw], axis=0)
                vw = jnp.concatenate([v_ref[prev, :].astype(BF16), vw], axis=0)
                valid = _dil_valid(_dil_prev_limit(has_prev))
            q2, do2 = _stack_heads(qb), _stack_heads(dob)
            lse_col, c_col = _head_columns(lse_ref[rows, :]), _head_columns(cc_ref[rows, :])
            p = jnp.where(jnp.concatenate([valid, valid], axis=0), jnp.exp(_dot_nt(q2, kw) - lse_col), 0.0)
            ds = (p * (_dot_nt(do2, vw) + c_col)).astype(BF16)
            dk, dv = _dot_tn(ds, q2), _dot_tn(p.astype(BF16), do2)
            dq_acc[rows, :] = _unstack_heads(_dot_nn(ds, kw)) * QK_SCALE
            if one_block:
                dk_acc[rows, :] += dk
                dv_acc[rows, :] += dv
            else:
                dk_acc[prev, :] += dk[:BLOCK]
                dv_acc[prev, :] += dv[:BLOCK]
                dk_acc[rows, :] += dk[BLOCK:]
                dv_acc[rows, :] += dv[BLOCK:]
            return carry

        lax.fori_loop(0, N_BLOCKS, block, 0, unroll=4)
        tabs = (c_ref[...], s1_ref[...], s2_ref[...])
        dq_out[...] = _rope_apply_t(dq_acc[...], *tabs).astype(BF16)
        dk_out[...] = _rope_apply_t(dk_acc[...], *tabs).astype(BF16)
        dv_out[...] = dv_acc[...].astype(BF16)
        pair = 2 * g + pl.program_id(0)
        _store_columns((dq_out, dk_out, dv_out), dproj_ref,
                       [base // LANES + pair for base in (COL_QA, COL_KA, COL_VA)], sem)

    qkv, tabs = _dil_specs(g)
    tok = pl.BlockSpec((SEQ, LANES), lambda p: (0, p))
    return pl.pallas_call(
        body, name=f"dil_attn_bwd_{g}", grid=(2,),
        in_specs=qkv + tabs + [tok, tok, tok, _ANY] + [_ANY] * len(deps), out_specs=_ANY,
        out_shape=jax.ShapeDtypeStruct(dproj.shape, dproj.dtype),
        scratch_shapes=[pltpu.VMEM((SEQ, LANES), F32)] * 3 + [pltpu.VMEM((SEQ, LANES), BF16)] * 3
        + [pltpu.VMEM((SEQ, LANES), F32)] * 2 + [pltpu.SemaphoreType.DMA((3,))],
        input_output_aliases={9: 0},
        compiler_params=pltpu.CompilerParams(dimension_semantics=("arbitrary",)),
    )(*_in_hbm(proj, proj, proj, *tables, do, lse, c, dproj), *deps)


def _group_weights(l0, l1, l2):
    m = jnp.maximum(jnp.maximum(l0, l1), l2)
    e0, e1, e2 = jnp.exp(l0 - m), jnp.exp(l1 - m), jnp.exp(l2 - m)
    tot = e0 + e1 + e2
    return e0 / tot, e1 / tot, e2 / tot


def _dil_combine(outs, lses, deps=()):
    def fn(o0, o1, o2, l0, l1, l2):
        w0, w1, w2 = _group_weights(l0, l1, l2)
        return w0 * o0 + w1 * o1 + w2 * o2

    w = DIL_OUT_WIDTH
    return _rowwise(fn, "dil_combine", SEQ, 512, [(a, w, 0) for a in list(outs) + list(lses)], [], [(w, F32)],
                    deps=deps)[0]


def _dil_combine_bwd(d_out, outs, lses, deps=()):
    w = DIL_OUT_WIDTH

    def fn(d, o0, o1, o2, l0, l1, l2):
        row = lax.broadcasted_iota(jnp.int32, (w, w), 0) // HEAD_DIM
        col = lax.broadcasted_iota(jnp.int32, (w, w), 1) // HEAD_DIM
        same_head = jnp.where(row == col, 1.0, 0.0).astype(BF16)
        ws = _group_weights(l0, l1, l2)
        dws = [_dot3_nn(d * og, same_head) for og in (o0, o1, o2)]
        mean = ws[0] * dws[0] + ws[1] * dws[1] + ws[2] * dws[2]
        return tuple(wg * d for wg in ws) + tuple(-wg * mean for wg in ws)

    res = _rowwise(fn, "dil_combine_bwd", SEQ, 256, [(a, w, 0) for a in [d_out] + list(outs) + list(lses)], [],
                   [(w, F32)] * 6, deps=deps)
    return res[:3], res[3:]


def _log1p(e):
    u = 1.0 + e
    return jnp.where(u == 1.0, e, jnp.log(u) * (e / (u - 1.0)))


def _fox_gate(proj, b_pad, deps=()):
    def body(f_ref, b_ref, *rest):
        o_ref = rest[-1]
        z = f_ref[...] + b_ref[...]
        logf = (jnp.minimum(z, 0.0) - _log1p(jnp.exp(-jnp.abs(z)))).T[:F_ROWS]
        row = lax.broadcasted_iota(jnp.int32, (BLOCK, BLOCK), 0)
        col = lax.broadcasted_iota(jnp.int32, (BLOCK, BLOCK), 1)
        before = jnp.where(row <= col, 1.0, 0.0).astype(BF16)
        carry = jnp.zeros((F_ROWS, 1), F32)
        for blk in range(N_BLOCKS):
            run = _dot3_nn(logf[:, blk * BLOCK:(blk + 1) * BLOCK], before) + carry
            o_ref[:, blk * BLOCK:(blk + 1) * BLOCK] = run
            carry = run[:, BLOCK - 1:BLOCK]

    return pl.pallas_call(
        body, name="fox_gate", grid=(1,),
        in_specs=[pl.BlockSpec((SEQ, LANES), lambda i: (0, COL_F // LANES)), pl.BlockSpec((1, LANES), lambda i: (0, 0))]
        + [_ANY] * len(deps),
        out_specs=pl.BlockSpec((F_ROWS, SEQ), lambda i: (0, 0)),
        out_shape=jax.ShapeDtypeStruct((F_ROWS, SEQ), F32),
    )(*_in_hbm(proj, b_pad), *deps)


def _fox_gate_bwd(d_cum, proj, b_pad, dproj):
    def body(d_ref, f_ref, b_ref, dproj_ref, dz_ref, db_ref):
        row = lax.broadcasted_iota(jnp.int32, (BLOCK, BLOCK), 0)
        col = lax.broadcasted_iota(jnp.int32, (BLOCK, BLOCK), 1)
        after = jnp.where(row >= col, 1.0, 0.0).astype(BF16)
        carry = jnp.zeros((F_ROWS, 1), F32)
        parts = [None] * N_BLOCKS
        for blk in reversed(range(N_BLOCKS)):
            run = _dot3_nn(d_ref[:, blk * BLOCK:(blk + 1) * BLOCK], after) + carry
            parts[blk] = run
            carry = run[:, 0:1]
        dlogf = jnp.concatenate(parts, axis=1)
        dlogf = jnp.concatenate([dlogf, jnp.zeros((LANES - F_ROWS, SEQ), F32)], axis=0).T
        dz = dlogf * _sigmoid(-(f_ref[...] + b_ref[...]))
        dz_ref[...] = dz.astype(BF16)
        db_ref[...] = jnp.sum(dz, axis=0, keepdims=True)

    f_cols = pl.BlockSpec((SEQ, LANES), lambda i: (0, COL_F // LANES))
    return pl.pallas_call(
        body, name="fox_gate_bwd", grid=(1,),
        in_specs=[pl.BlockSpec((F_ROWS, SEQ), lambda i: (0, 0)), f_cols, pl.BlockSpec((1, LANES), lambda i: (0, 0)), _ANY],
        out_specs=[f_cols, pl.BlockSpec((1, LANES), lambda i: (0, 0))],
        out_shape=[jax.ShapeDtypeStruct(dproj.shape, dproj.dtype), jax.ShapeDtypeStruct((1, LANES), F32)],
        input_output_aliases={3: 0},
    )(*_in_hbm(d_cum, proj, b_pad, dproj))


FOX_TILE = 256
FOX_TILES = SEQ // FOX_TILE


def _row_to_col(row):
    n = row.shape[1]
    eye = lax.broadcasted_iota(jnp.int32, (n, n), 0) == lax.broadcasted_iota(jnp.int32, (n, n), 1)
    return jnp.sum(jnp.where(eye, row, 0.0), axis=1, keepdims=True)


def _fox_bias(f_row, i):
    t = FOX_TILE
    ext = (i + 1) * t
    bias = _row_to_col(f_row[:, i * t:(i + 1) * t]) - f_row[:, :ext]
    row = lax.broadcasted_iota(jnp.int32, (t, ext), 0) + i * t
    col = lax.broadcasted_iota(jnp.int32, (t, ext), 1)
    return bias, col <= row


def _fox_specs():
    qkv = [pl.BlockSpec((SEQ, LANES), lambda p, base=base: (0, base // LANES + p)) for base in (COL_QB, COL_KB, COL_VB)]
    return qkv, pl.BlockSpec((F_ROWS, SEQ), lambda p: (0, 0))


def _fox_fwd(proj, f_rows):
    t = FOX_TILE

    def body(q_ref, k_ref, v_ref, f_ref, o_ref, lse_ref):
        pair = pl.program_id(0)
        upper = _upper_half()
        k16, v16 = k_ref[...].astype(BF16), v_ref[...].astype(BF16)
        f_row = [f_ref[pl.ds(2 * pair + e, 1), :] for e in range(2)]
        for i in range(FOX_TILES):
            ext = (i + 1) * t
            q_tile = (q_ref[i * t:(i + 1) * t, :] * QK_SCALE).astype(BF16)
            s2 = _dot_nt(_stack_heads(q_tile), k16[:ext])
            pns, lses = [], []
            for e in range(2):
                bias, causal = _fox_bias(f_row[e], i)
                s = jnp.where(causal, s2[e * t:(e + 1) * t] + bias, NEG_INF)
                m = jnp.max(s, axis=-1, keepdims=True)
                p = jnp.exp(s - m)
                den = jnp.sum(p, axis=-1, keepdims=True)
                pns.append((p * (1.0 / den)).astype(BF16))
                lses.append(m + jnp.log(den))
            o_ref[i * t:(i + 1) * t, :] = _unstack_heads(_dot_nn(jnp.concatenate(pns, axis=0), v16[:ext]))
            lse_ref[i * t:(i + 1) * t, :] = jnp.where(upper, lses[1], lses[0])

    qkv, f_spec = _fox_specs()
    tok = pl.BlockSpec((SEQ, LANES), lambda p: (0, p))
    return pl.pallas_call(
        body, name="fox_attn_fwd", grid=(FOX_WIDTH // LANES,),
        in_specs=qkv + [f_spec], out_specs=[tok, tok],
        out_shape=[jax.ShapeDtypeStruct((SEQ, FOX_WIDTH), F32)] * 2,
        compiler_params=pltpu.CompilerParams(
            dimension_semantics=("parallel",), vmem_limit_bytes=_vmem_limit(8 * t * SEQ * 4)),
    )(*_in_hbm(proj, proj, proj, f_rows))


def _fox_bwd(proj, do, lse, f_rows, dproj):
    t = FOX_TILE

    def body(q_ref, k_ref, v_ref, f_ref, do_ref, lse_ref, dproj_in, dproj_ref, df_ref, dk_acc, dv_acc,
             dq_out, dk_out, dv_out, sem):
        pair = pl.program_id(0)
        upper = _upper_half()
        k16, v16 = k_ref[...].astype(BF16), v_ref[...].astype(BF16)
        f_row = [f_ref[pl.ds(2 * pair + e, 1), :] for e in range(2)]
        dk_acc[...] = jnp.zeros_like(dk_acc)
        dv_acc[...] = jnp.zeros_like(dv_acc)
        df_ref[...] = jnp.zeros_like(df_ref)
        for i in range(FOX_TILES):
            ext = (i + 1) * t
            q_tile = (q_ref[i * t:(i + 1) * t, :] * QK_SCALE).astype(BF16)
            do_tile = do_ref[i * t:(i + 1) * t, :]
            lse_t = lse_ref[i * t:(i + 1) * t, :]
            q2, do2 = _stack_heads(q_tile), _stack_heads(do_tile)
            s2, dp2 = _dot_nt(q2, k16[:ext]), _dot_nt(do2, v16[:ext])
            ps, dss = [], []
            for e in range(2):
                bias, causal = _fox_bias(f_row[e], i)
                s = s2[e * t:(e + 1) * t] + bias
                p = jnp.where(causal, jnp.exp(s - lse_t[:, e * HEAD_DIM:e * HEAD_DIM + 1]), 0.0)
                dp = dp2[e * t:(e + 1) * t]
                ds = p * (dp - jnp.sum(p * dp, axis=-1, keepdims=True))
                df_ref[0, e:e + 1, :ext] -= jnp.sum(ds, axis=0, keepdims=True)
                ps.append(p.astype(BF16))
                dss.append(ds.astype(BF16))
            ds2, p2 = jnp.concatenate(dss, axis=0), jnp.concatenate(ps, axis=0)
            dq_out[i * t:(i + 1) * t, :] = (_unstack_heads(_dot_nn(ds2, k16[:ext])) * QK_SCALE).astype(BF16)
            dk_acc[:ext, :] += _dot_tn(ds2, q2)
            dv_acc[:ext, :] += _dot_tn(p2, do2)
        dk_out[...] = dk_acc[...].astype(BF16)
        dv_out[...] = dv_acc[...].astype(BF16)
        _store_columns((dq_out, dk_out, dv_out), dproj_ref, [base // LANES + pair for base in (COL_QB, COL_KB, COL_VB)],
                       sem)

    qkv, f_spec = _fox_specs()
    tok = pl.BlockSpec((SEQ, LANES), lambda p: (0, p))
    return pl.pallas_call(
        body, name="fox_attn_bwd", grid=(FOX_WIDTH // LANES,),
        in_specs=qkv + [f_spec, tok, tok, _ANY],
        out_specs=[_ANY, pl.BlockSpec((1, SUBLANES, SEQ), lambda p: (p, 0, 0))],
        out_shape=[jax.ShapeDtypeStruct(dproj.shape, dproj.dtype),
                   jax.ShapeDtypeStruct((FOX_WIDTH // LANES, SUBLANES, SEQ), F32)],
        scratch_shapes=[pltpu.VMEM((SEQ, LANES), F32)] * 2 + [pltpu.VMEM((SEQ, LANES), BF16)] * 3
        + [pltpu.SemaphoreType.DMA((3,))],
        input_output_aliases={6: 0},
        compiler_params=pltpu.CompilerParams(
            dimension_semantics=("arbitrary",), vmem_limit_bytes=_vmem_limit(10 * t * SEQ * 4)),
    )(*_in_hbm(proj, proj, proj, f_rows, do, lse, dproj))


MIX_TILE = 256


def _mix_out(out_a, out_b, proj, x, wt_pa, wt_pb, w_out, g_post, g_ffn_pre):
    tm = MIX_TILE

    def body(a_ref, b_ref, ga_ref, gb_ref, x_ref, wpa_ref, wpb_ref, wo_ref, g2_ref, g3_ref,
             merged_ref, mix_ref, x1_ref, h2_ref):
        ya = _dot_nn(a_ref[...].astype(BF16), wpa_ref[...])
        yb = _dot_nn(b_ref[...].astype(BF16), wpb_ref[...])
        merged = (_sigmoid(ga_ref[...]) * ya + _sigmoid(gb_ref[...]) * yb).astype(BF16)
        merged_ref[...] = merged
        mix = _dot_nn(merged, wo_ref[...])
        mix_ref[...] = mix
        x1 = x_ref[...] + mix * _rms_scale(mix) * g2_ref[...]
        x1_ref[...] = x1
        h2_ref[...] = (x1 * _rms_scale(x1) * g3_ref[...]).astype(BF16)

    def rows(w, cb=0):
        return pl.BlockSpec((tm, w), lambda i, cb=cb: (i, cb))

    def whole(a):
        return pl.BlockSpec(a.shape, lambda i: (0, 0))

    d = D_MODEL
    blk = _nbytes((tm, d), F32) * 6 + sum(_nbytes(a.shape, BF16) for a in (wt_pa, wt_pb, w_out))
    return pl.pallas_call(
        body, name="mix_out", grid=(SEQ // tm,),
        in_specs=[rows(DIL_OUT_WIDTH), rows(FOX_WIDTH), rows(d, COL_GA // d), rows(d, COL_GB // d), rows(d),
                  whole(wt_pa), whole(wt_pb), whole(w_out), whole(g_post), whole(g_ffn_pre)],
        out_specs=[rows(d)] * 4,
        out_shape=[jax.ShapeDtypeStruct((SEQ, d), dt) for dt in (BF16, F32, F32, BF16)],
        compiler_params=pltpu.CompilerParams(dimension_semantics=("parallel",), vmem_limit_bytes=_vmem_limit(blk)),
    )(out_a, out_b, proj, proj, x, wt_pa, wt_pb, w_out, g_post, g_ffn_pre)


def _mix_out_bwd(dmix, out_a, out_b, proj, wt_pa, wt_pb, w_out, deps=()):
    tm = MIX_TILE

    def body(dm_ref, a_ref, b_ref, ga_ref, gb_ref, wpa_ref, wpb_ref, wo_ref, *rest):
        dproj_ref, dya_ref, dyb_ref, da_ref, db_ref = rest[len(deps):]
        dmerged = _dot_nt(dm_ref[...], wo_ref[...])
        ya = _dot_nn(a_ref[...].astype(BF16), wpa_ref[...])
        yb = _dot_nn(b_ref[...].astype(BF16), wpb_ref[...])
        sa, sb = _sigmoid(ga_ref[...]), _sigmoid(gb_ref[...])
        dproj_ref[:, COL_GA:COL_GA + D_MODEL] = (dmerged * ya * (sa * (1.0 - sa))).astype(BF16)
        dproj_ref[:, COL_GB:COL_GB + D_MODEL] = (dmerged * yb * (sb * (1.0 - sb))).astype(BF16)
        dproj_ref[:, COL_GB + D_MODEL:] = jnp.zeros((tm, COL_QA - COL_GB - D_MODEL), BF16)
        dya = (dmerged * sa).astype(BF16)
        dyb = (dmerged * sb).astype(BF16)
        dya_ref[...] = dya
        dyb_ref[...] = dyb
        da_ref[...] = _dot_nt(dya, wpa_ref[...])
        db_ref[...] = _dot_nt(dyb, wpb_ref[...]).astype(BF16)

    def rows(w, cb=0):
        return pl.BlockSpec((tm, w), lambda i, cb=cb: (i, cb))

    def whole(a):
        return pl.BlockSpec(a.shape, lambda i: (0, 0))

    d = D_MODEL
    blk = _nbytes((tm, d), F32) * 8 + sum(_nbytes(a.shape, BF16) for a in (wt_pa, wt_pb, w_out))
    return pl.pallas_call(
        body, name="mix_out_bwd", grid=(SEQ // tm,),
        in_specs=[rows(d), rows(DIL_OUT_WIDTH), rows(FOX_WIDTH), rows(d, COL_GA // d), rows(d, COL_GB // d),
                  whole(wt_pa), whole(wt_pb), whole(w_out)] + [_ANY] * len(deps),
        out_specs=[rows(COL_QA)] + [rows(d)] * 2 + [rows(DIL_OUT_WIDTH), rows(FOX_WIDTH)],
        out_shape=[jax.ShapeDtypeStruct((SEQ, PROJ_COLS), BF16)] + [jax.ShapeDtypeStruct((SEQ, d), BF16)] * 2
        + [jax.ShapeDtypeStruct((SEQ, DIL_OUT_WIDTH), F32), jax.ShapeDtypeStruct((SEQ, FOX_WIDTH), BF16)],
        compiler_params=pltpu.CompilerParams(dimension_semantics=("parallel",), vmem_limit_bytes=_vmem_limit(blk)),
    )(dmix, out_a, out_b, proj, proj, wt_pa, wt_pb, w_out, *deps)


FFN_TM, FFN_TN = 2048, 256


def _ffn_up(h2, wt_gate, wt_up):
    tm, tn = FFN_TM, FFN_TN

    def body(h_ref, wg_ref, wu_ref, gate_ref, up_ref, act_ref):
        gate = _dot_nt(h_ref[...], wg_ref[...])
        up = _dot_nt(h_ref[...], wu_ref[...])
        gate_ref[...] = gate
        up_ref[...] = up
        act_ref[...] = (gate * _sigmoid(gate) * up).astype(BF16)

    tile = pl.BlockSpec((tm, tn), lambda i, j: (i, j))
    w_spec = pl.BlockSpec((tn, D_MODEL), lambda i, j: (j, 0))
    return pl.pallas_call(
        body, name="ffn_up", grid=(SEQ // tm, D_FF // tn),
        in_specs=[pl.BlockSpec((tm, D_MODEL), lambda i, j: (i, 0)), w_spec, w_spec],
        out_specs=[tile, tile, tile],
        out_shape=[jax.ShapeDtypeStruct((SEQ, D_FF), dt) for dt in (F32, F32, BF16)],
        compiler_params=pltpu.CompilerParams(
            dimension_semantics=("parallel", "parallel"), vmem_limit_bytes=_vmem_limit(8 * 2**20)),
    )(h2, wt_gate, wt_up)


def _ffn_act_bwd(dff, w_down, gate, up):
    tm, tn = FFN_TM, FFN_TN

    def body(d_ref, wd_ref, gate_ref, up_ref, dgate_ref, dup_ref):
        dact = _dot_nt(d_ref[...], wd_ref[...])
        gate = gate_ref[...]
        sg = _sigmoid(gate)
        dgate_ref[...] = (dact * up_ref[...] * (sg * (1.0 + gate * (1.0 - sg)))).astype(BF16)
        dup_ref[...] = (dact * (gate * sg)).astype(BF16)

    tile = pl.BlockSpec((tm, tn), lambda i, j: (i, j))
    return pl.pallas_call(
        body, name="ffn_act_bwd", grid=(SEQ // tm, D_FF // tn),
        in_specs=[pl.BlockSpec((tm, D_MODEL), lambda i, j: (i, 0)), pl.BlockSpec((tn, D_MODEL), lambda i, j: (j, 0)),
                  tile, tile],
        out_specs=[tile, tile],
        out_shape=[jax.ShapeDtypeStruct((SEQ, D_FF), BF16)] * 2,
        compiler_params=pltpu.CompilerParams(
            dimension_semantics=("parallel", "parallel"), vmem_limit_bytes=_vmem_limit(8 * 2**20)),
    )(dff, w_down, gate, up)


EPILOGUE_TM = 512


def _loss_head(act, w_down, x1, target, g_post):
    def fn(ff, x1, tgt, g):
        r = _rms_scale(ff)
        nrm = ff * r
        err = (x1 + nrm * g) - tgt
        loss = 0.5 * jnp.sum(jnp.mean(err * err, axis=-1, keepdims=True), axis=0, keepdims=True)
        dy = err * (1.0 / D_MODEL)
        u = dy * g
        dff = r * u - ff * (r * r * r) * jnp.mean(u * ff, axis=-1, keepdims=True)
        return dy, dff, jnp.broadcast_to(loss, (1, LANES)), jnp.sum(dy * nrm, axis=0, keepdims=True)

    d = D_MODEL
    return _matmul_rowwise([(act, w_down)], fn, "ffn_down_loss", EPILOGUE_TM, [(x1, d, 0), (target, d, 0)], [g_post],
                           [(d, F32), (d, BF16)], [LANES, d])


def _post_ffn_bwd(dgate, wt_gate, dup, wt_up, x1, dy, mix, g_ffn_pre, g_mix_post, deps=()):
    def fn(dh2, x1, dy, mix, g3, g2):
        dx, dg3 = _rms_bwd(x1, dh2, g3)
        dx1 = dy + dx
        dmix, dg2 = _rms_bwd(mix, dx1, g2)
        return dx1, dmix, dg3, dg2

    d = D_MODEL
    return _matmul_rowwise([(dgate, wt_gate), (dup, wt_up)], fn, "ffn_up_bwd", EPILOGUE_TM,
                           [(x1, d, 0), (dy, d, 0), (mix, d, 0)], [g_ffn_pre, g_mix_post],
                           [(d, F32), (d, BF16)], [d, d], deps=deps)


def _input_bwd(dproj, wt_r, x, dx1, g_pre, deps=()):
    def fn(dh, x, dx1, g):
        dx, dg = _rms_bwd(x, dh, g)
        return dx1 + dx, dg

    d = D_MODEL
    return _matmul_rowwise([(dproj, wt_r)], fn, "in_proj_bwd", EPILOGUE_TM, [(x, d, 0), (dx1, d, 0)], [g_pre],
                           [(d, F32)], [d], deps=deps)


def _adam_math(w, g, m, v):
    m = ADAM_B1 * m + (1.0 - ADAM_B1) * g
    v = ADAM_B2 * v + (1.0 - ADAM_B2) * (g * g)
    m_hat = m / (1.0 - ADAM_B1 ** ADAM_STEP)
    v_hat = v / (1.0 - ADAM_B2 ** ADAM_STEP)
    delta = -ADAM_LR * (m_hat / (jnp.sqrt(v_hat) + ADAM_EPS) + ADAM_WD * w)
    return delta, m, v


def _adam(w, mine, recv, m, v, name):
    r, c = w.shape
    tc = _col_tile(r, c)

    def body(w_ref, p_ref, r_ref, m_ref, v_ref, g_ref, d_ref, nm_ref, nv_ref):
        g = ((p_ref[...] + r_ref[0].astype(F32)) + r_ref[1].astype(F32)) + r_ref[2].astype(F32)
        g_ref[...] = g
        d_ref[...], nm_ref[...], nv_ref[...] = _adam_math(w_ref[...], g, m_ref[...], v_ref[...])

    spec = pl.BlockSpec((r, tc), lambda j: (0, j))
    return pl.pallas_call(
        body, name=name, grid=(c // tc,),
        in_specs=[spec, spec, pl.BlockSpec((3, r, tc), lambda j: (0, 0, j)), spec, spec], out_specs=[spec] * 4,
        out_shape=[jax.ShapeDtypeStruct((r, c), F32)] * 4,
        compiler_params=pltpu.CompilerParams(dimension_semantics=("parallel",)),
    )(*_in_hbm(w, mine, recv, m, v))


def _adam_small(gathered, ws, ms, vs, loss_parts):
    n = len(ws)

    def body(*refs):
        outs = refs[4 * n + 1:]
        loss = refs[4 * n][0]
        for dev in range(1, N_DEV):
            loss = loss + refs[4 * n][dev]
        outs[4 * n][...] = loss
        for i in range(n):
            ga_ref, w_ref, m_ref, v_ref = (refs[j * n + i] for j in range(4))
            g = ga_ref[0]
            for dev in range(1, N_DEV):
                g = g + ga_ref[dev]
            g = g[:, :w_ref.shape[1]]
            outs[4 * i][...] = g
            outs[4 * i + 1][...], outs[4 * i + 2][...], outs[4 * i + 3][...] = _adam_math(
                w_ref[...], g, m_ref[...], v_ref[...])

    out_shape = [jax.ShapeDtypeStruct(w.shape, F32) for w in ws for _ in range(4)]
    out_shape.append(jax.ShapeDtypeStruct((1, LANES), F32))
    out = pl.pallas_call(body, name="adam_small", out_shape=out_shape)(*gathered, *ws, *ms, *vs, loss_parts)
    return [out[4 * i:4 * i + 4] for i in range(n)], out[4 * n]


_PROJ_SEGMENTS = ((3848, 5896), (None, COL_QA - 2 * D_MODEL), (0, 3840), (3840, 3848), (None, PROJ_COLS - COL_F - 8))


def _proj_weight_t(gathered):
    w = gathered.reshape(IN_COLS, D_MODEL)
    return jnp.concatenate([jnp.zeros((hi, D_MODEL), w.dtype) if lo is None else w[lo:hi] for lo, hi in _PROJ_SEGMENTS],
                           axis=0)


def _proj_weight_grad_slots(dwt_r):
    starts, at = [], 0
    for lo, hi in _PROJ_SEGMENTS:
        if lo is not None:
            starts.append((lo, hi, at))
        at += hi if lo is None else hi - lo
    slots = []
    for dev in range(N_DEV):
        pieces, lo, end = [], dev * IN_SHARD, (dev + 1) * IN_SHARD
        for seg_lo, seg_hi, seg_at in sorted(starts):
            a, b = max(lo, seg_lo), min(end, seg_hi)
            if a < b:
                pieces.append(dwt_r[seg_at + a - seg_lo:seg_at + b - seg_lo])
        slots.append(pieces[0] if len(pieces) == 1 else jnp.concatenate(pieces, axis=0))
    return jnp.stack(slots)


def kernel(x, w_in, w_proj_a, w_proj_b, w_out, b_forget, w_ffn_gate, w_ffn_up, w_ffn_down, norm_mix_pre, norm_mix_post, norm_ffn_pre, norm_ffn_post, loss_target, m_w_in, m_w_proj_a, m_w_proj_b, m_w_out, m_b_forget, m_w_ffn_gate, m_w_ffn_up, m_w_ffn_down, m_norm_mix_pre, m_norm_mix_post, m_norm_ffn_pre, m_norm_ffn_post, v_w_in, v_w_proj_a, v_w_proj_b, v_w_out, v_b_forget, v_w_ffn_gate, v_w_ffn_up, v_w_ffn_down, v_norm_mix_pre, v_norm_mix_post, v_norm_ffn_pre, v_norm_ffn_post):
    d = D_MODEL
    names = ("w_in", "w_proj_a", "w_proj_b", "w_out", "w_ffn_gate", "w_ffn_up", "w_ffn_down")
    col_sharded = ("w_in", "w_ffn_gate", "w_ffn_up")

    def row_shards(arrs):
        return {k: (a[0].T if k in col_sharded else a[0]) for k, a in zip(names, arrs)}

    shards = row_shards((w_in, w_proj_a, w_proj_b, w_out, w_ffn_gate, w_ffn_up, w_ffn_down))
    moments_m = row_shards((m_w_in, m_w_proj_a, m_w_proj_b, m_w_out, m_w_ffn_gate, m_w_ffn_up, m_w_ffn_down))
    moments_v = row_shards((v_w_in, v_w_proj_a, v_w_proj_b, v_w_out, v_w_ffn_gate, v_w_ffn_up, v_w_ffn_down))
    pos = jnp.stack([lax.axis_index("c"), 2 * lax.axis_index("x") + lax.axis_index("y")]).astype(jnp.int32)
    x2, target = x[0], loss_target[0]

    me = 4 * lax.axis_index("x") + 2 * lax.axis_index("y") + lax.axis_index("c")
    mid_names, ffn_names = names[1:4], names[4:]
    first_names, later_names = names[:1], names[1:]
    shards16 = {k: shards[k].astype(BF16) for k in names}

    def landing(k):
        return lax.dynamic_update_slice(lax.empty((N_DEV,) + shards[k].shape, BF16), shards16[k][None], (me, 0, 0))

    ag_first = _exchange_start("ag_first_chips_start", _gather_chips_copies, [shards16[k] for k in first_names],
                               [landing(k) for k in first_names], 3 * len(first_names))
    h = _rowwise(lambda xb, g: xb * _rms_scale(xb) * g, "norm_mix_pre", SEQ, 256, [(x2, d, 0)], [norm_mix_pre],
                 [(d, BF16)], deps=[ag_first.token])[0]
    _, lands = _exchange_wait("ag_first_chips_wait", ag_first, [h])
    ag_first = _exchange_start("ag_first_sibling_start", _gather_sibling_copies, [], lands, 4 * len(first_names))
    ag_later = _exchange_start("ag_later_chips_start", _gather_chips_copies, [shards16[k] for k in later_names],
                               [landing(k) for k in later_names], 3 * len(later_names), after=[ag_first.token])
    gathered = dict(zip(first_names, _exchange_wait("ag_first_sibling_wait", ag_first, [ag_later.token])[1]))
    wt_r = _proj_weight_t(gathered["w_in"])

    proj = _matmul([(h, wt_r)], "nt", F32, "in_proj", 1024, 896, 1024)
    tables = _rope_tables()
    o_dil, lse_dil = zip(*[_dil_fwd(g, proj, tables) for g in range(len(DILATIONS))])
    out_a = _dil_combine(o_dil, lse_dil)
    _, lands = _exchange_wait("ag_later_chips_wait", ag_later, [out_a])
    ag_later = _exchange_start("ag_later_sibling_start", _gather_sibling_copies, [], lands, 4 * len(later_names))

    b_pad = jnp.pad(b_forget, ((0, 0), (0, LANES - N_FOX_HEADS)))
    f_rows = _fox_gate(proj, b_pad, deps=[ag_later.token])
    out_b, lse_fox = _fox_fwd(proj, f_rows)

    gathered = dict(zip(later_names, _exchange_wait("ag_later_sibling_wait", ag_later, [out_b])[1]))
    wt_pa = gathered["w_proj_a"].transpose(1, 0, 2).reshape(DIL_OUT_WIDTH, d)
    wt_pb = gathered["w_proj_b"].transpose(1, 0, 2).reshape(FOX_WIDTH, d)
    w_o = gathered["w_out"].reshape(d, d)
    wt_g = gathered["w_ffn_gate"].reshape(D_FF, d)
    wt_u = gathered["w_ffn_up"].reshape(D_FF, d)
    w_d = gathered["w_ffn_down"].reshape(D_FF, d)
    merged, mix, x1, h2 = _mix_out(out_a, out_b, proj, x2, wt_pa, wt_pb, w_o, norm_mix_post, norm_ffn_pre)

    gate, up, act = _ffn_up(h2, wt_g, wt_u)
    dy, dff, loss_part, dg_ffn_post = _loss_head(act, w_d, x1, target, norm_ffn_post)

    dgate, dup = _ffn_act_bwd(dff, w_d, gate, up)
    grads_t = {}
    grads_t["w_ffn_down"] = _matmul([(act, dff)], "tn", F32, "grad_w_ffn_down", 1408, 512, 2048)
    grads_t["w_ffn_gate"] = _matmul([(dgate, h2)], "tn", F32, "grad_w_ffn_gate", 1408, 512, 2048)
    grads_t["w_ffn_up"] = _matmul([(dup, h2)], "tn", F32, "grad_w_ffn_up", 1408, 512, 2048)
    rs_ffn = _ReduceScatter("ffn", {k: grads_t[k] for k in ffn_names}, pos)
    dx1, dmix, dg_ffn_pre, dg_mix_post = _post_ffn_bwd(dgate, wt_g, dup, wt_u, x1, dy, mix, norm_ffn_pre, norm_mix_post,
                                                       deps=[rs_ffn.token])
    rs_ffn.start_chips([dmix])

    dproj, dya, dyb, d_out_a, d_out_b = _mix_out_bwd(dmix, out_a, out_b, proj, wt_pa, wt_pb, w_o, deps=[rs_ffn.token])
    grads_t["w_out"] = _matmul([(merged, dmix)], "tn", F32, "grad_w_out", 1024, 1024, 1024)
    def column_slots(g):
        return g.reshape(g.shape[0], N_DEV, LANES).transpose(1, 0, 2)

    grads_t["w_proj_a"] = column_slots(_matmul([(out_a, dya)], "tn", F32, "grad_w_proj_a", DIL_OUT_WIDTH, 1024, SEQ))
    grads_t["w_proj_b"] = column_slots(_matmul([(out_b, dyb)], "tn", F32, "grad_w_proj_b", FOX_WIDTH, 1024, SEQ))
    rs_mid = _ReduceScatter("mid", {k: grads_t[k] for k in mid_names}, pos)

    do_dil, c_dil = _dil_combine_bwd(d_out_a, o_dil, lse_dil, deps=[rs_mid.token])
    rs_mid.start_chips([c_dil[0]])
    dproj, d_cum = _fox_bwd(proj, d_out_b, lse_fox, f_rows, dproj)
    d_cum_rows = jnp.pad(d_cum[:, :2].reshape(N_FOX_HEADS, SEQ), ((0, F_ROWS - N_FOX_HEADS), (0, 0)))
    dproj, db_part = _fox_gate_bwd(d_cum_rows, proj, b_pad, dproj)
    for g in range(len(DILATIONS)):
        dproj = _dil_bwd(g, proj, tables, do_dil[g], lse_dil[g], c_dil[g], dproj, deps=[rs_mid.token])

    dwt_r = _matmul([(dproj, h)], "tn", F32, "grad_w_in", 896, 1024, 2048)
    rs_in = _ReduceScatter("in", {"w_in": _proj_weight_grad_slots(dwt_r)}, pos)
    def finish(rs, after):
        return {k: _adam(shards[k], mine, recv, moments_m[k], moments_v[k], "adam_" + k)
                for k, (mine, recv) in rs.finish(after).items()}

    done = finish(rs_ffn, [rs_in.token])
    rs_in.start_chips([done[k][0] for k in ffn_names])
    grad_x, dg_mix_pre = _input_bwd(dproj, wt_r, x2, dx1, norm_mix_pre, deps=[rs_in.token])
    done.update(finish(rs_mid, [grad_x]))

    small_all = _all_gather([dg_mix_pre, dg_mix_post, dg_ffn_pre, dg_ffn_post, db_part, loss_part],
                            "small_grads_all_gather", deps=[done[k][0] for k in mid_names])
    small, loss = _adam_small(small_all[:5], [norm_mix_pre, norm_mix_post, norm_ffn_pre, norm_ffn_post, b_forget],
                              [m_norm_mix_pre, m_norm_mix_post, m_norm_ffn_pre, m_norm_ffn_post, m_b_forget],
                              [v_norm_mix_pre, v_norm_mix_post, v_norm_ffn_pre, v_norm_ffn_post, v_b_forget],
                              small_all[5])

    done.update(finish(rs_in, [small[0][0]]))

    def leaves(i):
        def nat(k):
            a = done[k][i]
            return (a.T if k in col_sharded else a)[None]

        return [nat("w_in"), nat("w_proj_a"), nat("w_proj_b"), nat("w_out"), small[4][i],
                nat("w_ffn_gate"), nat("w_ffn_up"), nat("w_ffn_down"), *[small[r][i] for r in range(4)]]

    return (loss[0, 0], grad_x[None], *leaves(0), *leaves(1), *leaves(2), *leaves(3))
```

```python
import functools
import math

import jax
import jax.numpy as jnp
import numpy as np
from jax import lax
from jax.experimental import pallas as pl
from jax.experimental.pallas import tpu as pltpu

F32 = jnp.float32
BF16 = jnp.bfloat16
MESH = pl.DeviceIdType.MESH

D_MODEL = 1024
SEQ = 2048
HEAD_DIM = 64
BLOCK = 128
N_BLOCKS = SEQ // BLOCK
DILATIONS = (1, 4, 16)
N_FOX_HEADS = 8
DIL_WIDTH = 768
DIL_OUT_WIDTH = 256
FOX_WIDTH = 512
D_FF = 2816
ROPE_THETA = 500000.0
ROPE_DIM = HEAD_DIM // 4
ROPE_HALF = ROPE_DIM // 2
EPS = 1e-6
NEG_INF = -1e30
QK_SCALE = 1.0 / math.sqrt(HEAD_DIM)
IN_COLS = 5896
N_DEV = 8
IN_SHARD = IN_COLS // N_DEV

ADAM_LR = 0.001
ADAM_B1 = 0.9
ADAM_B2 = 0.999
ADAM_EPS = 1e-08
ADAM_WD = 0.01
ADAM_STEP = 10

V7X_VMEM_BYTES = 64 * 2**20
LANES = 128
SUBLANES = 8

PROJ_COLS = 6272
COL_GA, COL_GB = 0, 1024
COL_QA, COL_KA, COL_VA = 2304, 3072, 3840
COL_QB, COL_KB, COL_VB = 4608, 5120, 5632
COL_F = 6144
F_ROWS = 16


def _vmem_limit(block_bytes):
    want = 2 * block_bytes + 16 * 2**20
    return int(min(max(want, 32 * 2**20), V7X_VMEM_BYTES - 8 * 2**20))


def _nbytes(shape, dtype):
    return math.prod(shape) * jnp.dtype(dtype).itemsize


def _in_hbm(*arrays):
    return [pltpu.with_memory_space_constraint(a, pltpu.HBM) for a in arrays]


def _dot(a, b, dims):
    return lax.dot_general(a, b, (dims, ((), ())), preferred_element_type=F32)


def _dot_nn(a, b):
    return _dot(a, b, ((1,), (0,)))


def _dot_nt(a, b):
    return _dot(a, b, ((1,), (1,)))


def _dot_tn(a, b):
    return _dot(a, b, ((0,), (0,)))


def _sigmoid(z):
    return 1.0 / (1.0 + jnp.exp(-z))


def _split3(x):
    hi = x.astype(BF16)
    r1 = x - hi.astype(F32)
    mid = r1.astype(BF16)
    lo = (r1 - mid.astype(F32)).astype(BF16)
    return hi, mid, lo


def _dot3_nn(x, ones_matrix):
    hi, mid, lo = _split3(x)
    return (_dot_nn(hi, ones_matrix) + _dot_nn(mid, ones_matrix)) + _dot_nn(lo, ones_matrix)


def _rowwise(fn, name, n_rows, tm, row_ins, bcast_ins, row_outs, acc_outs=(), deps=()):
    n_in = len(row_ins) + len(bcast_ins)
    n_ro = len(row_outs)

    def body(*refs):
        res = fn(*[r[...] for r in refs[:n_in]])
        if not isinstance(res, (tuple, list)):
            res = (res,)
        outs = refs[n_in + len(deps):]
        for r, o in zip(res[:n_ro], outs[:n_ro]):
            o[...] = r.astype(o.dtype)
        first = pl.program_id(0) == 0
        for r, o in zip(res[n_ro:], outs[n_ro:]):
            _accumulate(o, r, first)

    in_specs = [pl.BlockSpec((tm, w), lambda i, cb=cb: (i, cb)) for _, w, cb in row_ins]
    in_specs += [pl.BlockSpec(a.shape, lambda i: (0, 0)) for a in bcast_ins]
    in_specs += [pl.BlockSpec(memory_space=pl.ANY)] * len(deps)
    out_specs = [pl.BlockSpec((tm, w), lambda i: (i, 0)) for w, _ in row_outs]
    out_specs += [pl.BlockSpec((1, w), lambda i: (0, 0)) for w in acc_outs]
    out_shape = [jax.ShapeDtypeStruct((n_rows, w), dt) for w, dt in row_outs]
    out_shape += [jax.ShapeDtypeStruct((1, w), F32) for w in acc_outs]
    blk = sum(_nbytes((tm, w), a.dtype) for a, w, _ in row_ins) + sum(_nbytes((tm, w), dt) for w, dt in row_outs)
    return pl.pallas_call(
        body, name=name, grid=(n_rows // tm,), in_specs=in_specs, out_specs=out_specs, out_shape=out_shape,
        compiler_params=pltpu.CompilerParams(
            dimension_semantics=("arbitrary" if acc_outs else "parallel",), vmem_limit_bytes=_vmem_limit(3 * blk)),
    )(*_in_hbm(*[a for a, _, _ in row_ins], *bcast_ins), *deps)


def _accumulate(o_ref, part, first):
    @pl.when(first)
    def _():
        o_ref[...] = part

    @pl.when(jnp.logical_not(first))
    def _():
        o_ref[...] += part


_MM_DIMS = {"nn": ((1,), (0,)), "nt": ((1,), (1,)), "tn": ((0,), (0,))}


def _matmul(pairs, mode, out_dtype, name, tm, tn, tk, deps=()):
    a0, b0 = pairs[0]
    if mode == "tn":
        kk, m = a0.shape
    else:
        m, kk = a0.shape
    n = b0.shape[0] if mode == "nt" else b0.shape[1]
    assert m % tm == 0 and n % tn == 0 and kk % tk == 0, (name, m, n, kk)
    nk = kk // tk
    n_pairs = len(pairs)
    dims = _MM_DIMS[mode]
    n_in = 2 * n_pairs + len(deps)

    def body(*refs):
        o_ref = refs[n_in]
        part = None
        for p in range(n_pairs):
            d = _dot(refs[2 * p][...].astype(BF16), refs[2 * p + 1][...].astype(BF16), dims)
            part = d if part is None else part + d
        if nk == 1:
            o_ref[...] = part.astype(o_ref.dtype)
            return
        acc = refs[n_in + 1]
        k = pl.program_id(2)

        @pl.when(k == 0)
        def _():
            acc[...] = part

        @pl.when(k > 0)
        def _():
            acc[...] += part

        @pl.when(k == nk - 1)
        def _():
            o_ref[...] = acc[...].astype(o_ref.dtype)

    if mode == "tn":
        a_spec = pl.BlockSpec((tk, tm), lambda i, j, k: (k, i))
    else:
        a_spec = pl.BlockSpec((tm, tk), lambda i, j, k: (i, k))
    if mode == "nt":
        b_spec = pl.BlockSpec((tn, tk), lambda i, j, k: (j, k))
    else:
        b_spec = pl.BlockSpec((tk, tn), lambda i, j, k: (k, j))
    blk = sum(_nbytes((tm, tk), a.dtype) + _nbytes((tk, tn), b.dtype) for a, b in pairs) + 2 * _nbytes((tm, tn), F32)
    flat = [a for pair in pairs for a in pair]
    return pl.pallas_call(
        body, name=name, grid=(m // tm, n // tn, nk),
        in_specs=[a_spec, b_spec] * n_pairs + [pl.BlockSpec(memory_space=pl.ANY)] * len(deps),
        out_specs=pl.BlockSpec((tm, tn), lambda i, j, k: (i, j)),
        out_shape=jax.ShapeDtypeStruct((m, n), out_dtype),
        scratch_shapes=[] if nk == 1 else [pltpu.VMEM((tm, tn), F32)],
        compiler_params=pltpu.CompilerParams(
            dimension_semantics=("parallel", "parallel", "arbitrary"), vmem_limit_bytes=_vmem_limit(blk)),
    )(*flat, *deps)


def _matmul_rowwise(pairs, fn, name, tm, row_ins, bcast_ins, row_outs, acc_outs=(), deps=()):
    m = pairs[0][0].shape[0]
    n_mm, n_in = 2 * len(pairs), len(row_ins) + len(bcast_ins)
    n_ro = len(row_outs)

    def body(*refs):
        prod = None
        for p in range(len(pairs)):
            part = _dot_nn(refs[2 * p][...].astype(BF16), refs[2 * p + 1][...].astype(BF16))
            prod = part if prod is None else prod + part
        res = fn(prod, *[r[...] for r in refs[n_mm:n_mm + n_in]])
        outs = refs[n_mm + n_in + len(deps):]
        for r, o in zip(res[:n_ro], outs[:n_ro]):
            o[...] = r.astype(o.dtype)
        first = pl.program_id(0) == 0
        for r, o in zip(res[n_ro:], outs[n_ro:]):
            _accumulate(o, r, first)

    in_specs = []
    for a, b in pairs:
        in_specs += [pl.BlockSpec((tm, a.shape[1]), lambda i: (i, 0)),
                     pl.BlockSpec(b.shape, lambda i: (0, 0), pipeline_mode=pl.Buffered(1))]
    in_specs += [pl.BlockSpec((tm, w), lambda i, cb=cb: (i, cb)) for _, w, cb in row_ins]
    in_specs += [pl.BlockSpec(a.shape, lambda i: (0, 0)) for a in bcast_ins]
    in_specs += [_ANY] * len(deps)
    out_specs = [pl.BlockSpec((tm, w), lambda i: (i, 0)) for w, _ in row_outs]
    out_specs += [pl.BlockSpec((1, w), lambda i: (0, 0)) for w in acc_outs]
    out_shape = [jax.ShapeDtypeStruct((m, w), dt) for w, dt in row_outs]
    out_shape += [jax.ShapeDtypeStruct((1, w), F32) for w in acc_outs]
    blk = sum(_nbytes((tm, a.shape[1]), a.dtype) + _nbytes(b.shape, b.dtype) // 2 for a, b in pairs)
    blk += sum(_nbytes((tm, w), a.dtype) for a, w, _ in row_ins) + sum(_nbytes((tm, w), dt) for w, dt in row_outs)
    return pl.pallas_call(
        body, name=name, grid=(m // tm,), in_specs=in_specs, out_specs=out_specs, out_shape=out_shape,
        compiler_params=pltpu.CompilerParams(dimension_semantics=("arbitrary",), vmem_limit_bytes=_vmem_limit(blk)),
    )(*[a for pair in pairs for a in pair], *[a for a, _, _ in row_ins], *bcast_ins, *deps)


def _rms_scale(x):
    return lax.rsqrt(jnp.mean(x * x, axis=-1, keepdims=True) + EPS)


def _rms_bwd(xin, dyn, g):
    r = _rms_scale(xin)
    u = dyn * g
    dx = r * u - xin * (r * r * r) * jnp.mean(u * xin, axis=-1, keepdims=True)
    dg = jnp.sum(dyn * xin * r, axis=0, keepdims=True)
    return dx, dg


def _mesh_pos():
    return lax.axis_index("x"), lax.axis_index("y"), lax.axis_index("c")


def _all_gather(xs, name, deps=()):
    n = len(xs)

    def body(*refs):
        x_refs, out_refs = refs[:n], refs[n + len(deps):2 * n + len(deps)]
        send_sems, recv_sems, local_sems = refs[2 * n + len(deps):]
        mx, my, mc = _mesh_pos()
        me, sib = (mx, my, mc), (mx, my, 1 - mc)
        chips = [(1 - mx, my), (mx, 1 - my), (1 - mx, 1 - my)]

        def slot(a, dev):
            px, py, pc = dev
            return out_refs[a].at[4 * px + 2 * py + pc]

        def copy(k, a, block, to, src=None):
            return pltpu.make_async_remote_copy(
                src_ref=slot(a, block) if src is None else src, dst_ref=slot(a, block),
                send_sem=send_sems.at[a * 7 + k], recv_sem=recv_sems.at[a * 7 + k],
                device_id=to, device_id_type=MESH)

        mine = [pltpu.make_async_copy(x_refs[a], slot(a, me), local_sems.at[a]) for a in range(n)]
        for cp in mine:
            cp.start()
        first = []
        for a in range(n):
            first.append(copy(0, a, me, sib, x_refs[a]))
            first += [copy(1 + j, a, me, (*chip, mc), x_refs[a]) for j, chip in enumerate(chips)]
        for cp in first:
            cp.start()
        passed = []
        for a in range(n):
            for j, chip in enumerate(chips):
                copy(1 + j, a, (*chip, mc), me).wait_recv()
                fwd = copy(4 + j, a, (*chip, mc), sib)
                fwd.start()
                passed.append(fwd)
        for a in range(n):
            copy(0, a, sib, me).wait_recv()
            for j, chip in enumerate(chips):
                copy(4 + j, a, (*chip, 1 - mc), me).wait_recv()
        for cp in first + passed:
            cp.wait_send()
        for cp in mine:
            cp.wait()

    hbm = pl.BlockSpec(memory_space=pl.ANY)
    return pl.pallas_call(
        body, name=name,
        out_shape=[jax.ShapeDtypeStruct((N_DEV,) + x.shape, x.dtype) for x in xs],
        in_specs=[hbm] * (n + len(deps)), out_specs=[hbm] * n,
        scratch_shapes=[pltpu.SemaphoreType.DMA((7 * n,)), pltpu.SemaphoreType.DMA((7 * n,)),
                        pltpu.SemaphoreType.DMA((n,))],
    )(*xs, *deps)


_HBM = pl.BlockSpec(memory_space=pltpu.HBM)
_SEM = pl.BlockSpec(memory_space=pltpu.SEMAPHORE)
_ANY = pl.BlockSpec(memory_space=pl.ANY)
_DATAFLOW = pltpu.SideEffectType.DATAFLOW_SIDE_EFFECTING


def _flip_peer(flip):
    mx, my, mc = _mesh_pos()
    return (1 - mx if flip & 2 else mx, 1 - my if flip & 1 else my, mc)


def _remote(src, dst, send_sems, recv_sems, k, peer):
    return pltpu.make_async_remote_copy(src_ref=src, dst_ref=dst, send_sem=send_sems.at[k], recv_sem=recv_sems.at[k],
                                        device_id=peer, device_id_type=MESH)


def _gather_chips_copies(srcs, lands, send_sems, recv_sems):
    mx, my, mc = _mesh_pos()
    me = 4 * mx + 2 * my + mc
    return [_remote(srcs[a], lands[a].at[me], send_sems, recv_sems, 3 * a + flip - 1, _flip_peer(flip))
            for a in range(len(srcs)) for flip in (1, 2, 3)]


def _gather_sibling_copies(srcs, lands, send_sems, recv_sems):
    mx, my, mc = _mesh_pos()
    return [_remote(lands[a].at[2 * k + mc], lands[a].at[2 * k + mc], send_sems, recv_sems, 4 * a + k, (mx, my, 1 - mc))
            for a in range(len(lands)) for k in range(4)]


def _scatter_sibling_copies(srcs, lands, send_sems, recv_sems):
    mx, my, mc = _mesh_pos()
    return [_remote(srcs[a].at[k, 1 - mc], lands[a].at[k], send_sems, recv_sems, 4 * a + k, (mx, my, 1 - mc))
            for a in range(len(srcs)) for k in range(4)]


def _scatter_chips_copies(srcs, lands, send_sems, recv_sems):
    mx, my, _ = _mesh_pos()
    k0 = 2 * mx + my
    return [_remote(srcs[a].at[jnp.bitwise_xor(k0, flip)], lands[a].at[flip - 1], send_sems, recv_sems,
                    3 * a + flip - 1, _flip_peer(flip))
            for a in range(len(srcs)) for flip in (1, 2, 3)]


class _Exchange:
    def __init__(self, copies, n_src, send_sems, recv_sems, thru, token):
        self.copies, self.n_src, self.send_sems, self.recv_sems, self.thru, self.token = (
            copies, n_src, send_sems, recv_sems, thru, token)


def _exchange_start(name, copies, srcs, lands, n_copies, after=()):
    bufs = list(srcs) + list(lands)
    nb, ns = len(bufs), len(srcs)

    def body(*refs):
        send_sems, recv_sems = refs[nb + len(after)], refs[nb + len(after) + 1]
        for cp in copies(refs[:ns], refs[ns:nb], send_sems, recv_sems):
            cp.start()
        refs[-1][...] = jnp.zeros_like(refs[-1])

    out = pl.pallas_call(
        body, name=name,
        out_shape=(pltpu.SemaphoreType.DMA((n_copies,)), pltpu.SemaphoreType.DMA((n_copies,)),
                   *[pltpu.HBM(b.shape, b.dtype) for b in bufs], jax.ShapeDtypeStruct((SUBLANES, LANES), F32)),
        in_specs=[_HBM] * nb + [_ANY] * len(after),
        out_specs=(_SEM, _SEM, *[_HBM] * nb, pl.BlockSpec(memory_space=pltpu.VMEM)),
        input_output_aliases={i: 2 + i for i in range(nb)},
        compiler_params=pltpu.CompilerParams(has_side_effects=_DATAFLOW),
    )(*[pltpu.with_memory_space_constraint(b, pltpu.HBM) for b in bufs], *after)
    return _Exchange(copies, ns, out[0], out[1], list(out[2:2 + nb]), out[-1])


def _exchange_wait(name, ex, after):
    nb, ns = len(ex.thru), ex.n_src

    def body(*refs):
        for cp in ex.copies(refs[:ns], refs[ns:nb], refs[nb], refs[nb + 1]):
            cp.wait_send()
            cp.wait_recv()

    out = pl.pallas_call(
        body, name=name, out_shape=tuple(pltpu.HBM(b.shape, b.dtype) for b in ex.thru),
        in_specs=[_HBM] * nb + [_SEM, _SEM] + [_ANY] * len(after), out_specs=tuple([_HBM] * nb),
        input_output_aliases={i: i for i in range(nb)},
        compiler_params=pltpu.CompilerParams(has_side_effects=_DATAFLOW),
    )(*ex.thru, ex.send_sems, ex.recv_sems, *after)
    return list(out[:ns]), list(out[ns:])


def _col_tile(r, c):
    return next(t for t in (1024, 512, 256, 128) if c % t == 0 and (r * t * 4 <= 2**20 or t == 128))


def _add_sibling(g4, recv, pos, name):
    _, _, r, c = g4.shape
    tc = _col_tile(r, c)

    def body(pos_ref, g_ref, r_ref, o16_ref, mine_ref):
        s = g_ref[0, 0] + r_ref[0]
        o16_ref[0] = s.astype(BF16)

        @pl.when(pl.program_id(1) == pos_ref[1])
        def _():
            mine_ref[...] = s

    slot = pl.BlockSpec((1, r, tc), lambda j, k, pos_ref: (k, 0, j))
    return pl.pallas_call(
        body, name=name,
        out_shape=[jax.ShapeDtypeStruct((4, r, c), BF16), jax.ShapeDtypeStruct((r, c), F32)],
        grid_spec=pltpu.PrefetchScalarGridSpec(
            num_scalar_prefetch=1, grid=(c // tc, 4),
            in_specs=[pl.BlockSpec((1, 1, r, tc), lambda j, k, pos_ref: (k, pos_ref[0], 0, j)), slot],
            out_specs=[slot, pl.BlockSpec((r, tc), lambda j, k, pos_ref: (0, j))]),
        compiler_params=pltpu.CompilerParams(dimension_semantics=("parallel", "arbitrary")),
    )(pos, *_in_hbm(g4, recv))


class _ReduceScatter:
    def __init__(self, tag, grads_t, pos):
        self.tag, self.pos, self.names = tag, pos, list(grads_t)
        g4s = [g.reshape(4, 2, g.size // (N_DEV * g.shape[-1]), g.shape[-1]) for g in grads_t.values()]
        lands = [lax.empty((4,) + g.shape[2:], F32) for g in g4s]
        self.ex = _exchange_start(f"rs_{tag}_sibling_start", _scatter_sibling_copies, g4s, lands, 4 * len(g4s))
        self.token = self.ex.token

    def start_chips(self, after):
        g4s, from_sibling = _exchange_wait(f"rs_{self.tag}_sibling_wait", self.ex, after)
        parts = [_add_sibling(g4, rv, self.pos, f"rs_add_sibling_{k}")
                 for k, g4, rv in zip(self.names, g4s, from_sibling)]
        self.mine = [mine for _, mine in parts]
        p16s = [p16 for p16, _ in parts]
        lands = [lax.empty((3,) + p.shape[1:], BF16) for p in p16s]
        self.ex = _exchange_start(f"rs_{self.tag}_chips_start", _scatter_chips_copies, p16s, lands, 3 * len(p16s))
        self.token = self.ex.token

    def finish(self, after):
        _, from_chips = _exchange_wait(f"rs_{self.tag}_chips_wait", self.ex, after)
        return dict(zip(self.names, zip(self.mine, from_chips)))


def _rope_tables():
    positions = np.arange(SEQ, dtype=np.float32)
    inv_freq = np.power(np.float32(ROPE_THETA), -np.arange(0, ROPE_DIM, 2, dtype=np.float32) / np.float32(ROPE_DIM))
    ang = (positions[:, None] * inv_freq[None, :]).astype(np.float32)
    cos, sin = np.cos(ang).astype(np.float32), np.sin(ang).astype(np.float32)
    ones = np.ones((SEQ, HEAD_DIM - ROPE_DIM), np.float32)
    zeros8 = np.zeros((SEQ, ROPE_HALF), np.float32)
    zeros = np.zeros((SEQ, HEAD_DIM - ROPE_DIM), np.float32)
    c_head = np.concatenate([cos, cos, ones], axis=1)
    s1_head = np.concatenate([-sin, zeros8, zeros], axis=1)
    s2_head = np.concatenate([zeros8, sin, zeros], axis=1)
    return tuple(jnp.asarray(np.concatenate([t, t], axis=1)) for t in (c_head, s1_head, s2_head))


def _rope_apply(x, c, s1, s2):
    w = x.shape[1]
    return x * c + pltpu.roll(x, w - ROPE_HALF, 1) * s1 + pltpu.roll(x, ROPE_HALF, 1) * s2


def _rope_apply_t(dy, c, s1, s2):
    w = dy.shape[1]
    return dy * c + pltpu.roll(dy * s1, ROPE_HALF, 1) + pltpu.roll(dy * s2, w - ROPE_HALF, 1)


def _dil_prev_limit(has_prev):
    return jnp.where(has_prev, 0, BLOCK)


def _dil_valid(limit):
    row = lax.broadcasted_iota(jnp.int32, (BLOCK, 2 * BLOCK), 0)
    col = lax.broadcasted_iota(jnp.int32, (BLOCK, 2 * BLOCK), 1)
    dist = col - row
    return jnp.logical_and(dist >= jnp.where(col < BLOCK, limit, -BLOCK), dist <= BLOCK)


def _upper_half():
    return lax.broadcasted_iota(jnp.int32, (1, LANES), 1) >= HEAD_DIM


def _stack_heads(x):
    upper = _upper_half()
    return jnp.concatenate([jnp.where(upper, 0, x), jnp.where(upper, x, 0)], axis=0)


def _unstack_heads(y):
    n = y.shape[0] // 2
    return jnp.where(_upper_half(), y[n:], y[:n])


def _head_columns(t):
    return jnp.concatenate([t[:, 0:1], t[:, HEAD_DIM:HEAD_DIM + 1]], axis=0)


def _dil_rows(n, d):
    per = N_BLOCKS // d
    r, lb = n // per, n % per

    def rows(b):
        start = b * (BLOCK * d) + r
        return pl.ds(pl.multiple_of(start, BLOCK), BLOCK) if d == 1 else pl.ds(start, BLOCK, stride=d)

    return rows(lb), rows(jnp.maximum(lb - 1, 0)), lb > 0


def _dil_rotate(q_ref, k_ref, c_ref, s1_ref, s2_ref, q_rot, k_rot):
    tabs = (c_ref[...], s1_ref[...], s2_ref[...])
    q_rot[...] = _rope_apply(q_ref[...], *tabs) * QK_SCALE
    k_rot[...] = _rope_apply(k_ref[...], *tabs)


def _dil_specs(g):
    def col(base):
        return pl.BlockSpec((SEQ, LANES), lambda p: (0, base // LANES + 2 * g + p))

    table = pl.BlockSpec((SEQ, LANES), lambda p: (0, 0))
    return [col(COL_QA), col(COL_KA), col(COL_VA)], [table] * 3


def _store_columns(blocks, dproj_ref, cols, sem):
    copies = [pltpu.make_async_copy(b, dproj_ref.at[:, pl.ds(pl.multiple_of(c * LANES, LANES), LANES)], sem.at[i])
              for i, (b, c) in enumerate(zip(blocks, cols))]
    for cp in copies:
        cp.start()
    for cp in copies:
        cp.wait()


def _dil_fwd(g, proj, tables):
    d = DILATIONS[g]
    one_block = d == N_BLOCKS

    def body(q_ref, k_ref, v_ref, c_ref, s1_ref, s2_ref, o_ref, lse_ref, q_rot, k_rot):
        upper = _upper_half()
        _dil_rotate(q_ref, k_ref, c_ref, s1_ref, s2_ref, q_rot, k_rot)

        def block(n, carry):
            rows, prev, has_prev = _dil_rows(n, d)
            qb = q_rot[rows, :].astype(BF16)
            kw, vw = k_rot[rows, :].astype(BF16), v_ref[rows, :].astype(BF16)
            if one_block:
                row = lax.broadcasted_iota(jnp.int32, (BLOCK, BLOCK), 0)
                valid = lax.broadcasted_iota(jnp.int32, (BLOCK, BLOCK), 1) <= row
            else:
                kw = jnp.concatenate([k_rot[prev, :].astype(BF16), kw], axis=0)
                vw = jnp.concatenate([v_ref[prev, :].astype(BF16), vw], axis=0)
                valid = _dil_valid(_dil_prev_limit(has_prev))
            s = jnp.where(jnp.concatenate([valid, valid], axis=0), _dot_nt(_stack_heads(qb), kw), NEG_INF)
            m = jnp.max(s, axis=-1, keepdims=True)
            p = jnp.exp(s - m)
            den = jnp.sum(p, axis=-1, keepdims=True)
            o_ref[rows, :] = _unstack_heads(_dot_nn((p * (1.0 / den)).astype(BF16), vw))
            lse = m + jnp.log(den)
            lse_ref[rows, :] = jnp.where(upper, lse[BLOCK:], lse[:BLOCK])
            return carry

        lax.fori_loop(0, N_BLOCKS, block, 0, unroll=4)

    qkv, tabs = _dil_specs(g)
    out = pl.BlockSpec((SEQ, LANES), lambda p: (0, p))
    return pl.pallas_call(
        body, name=f"dil_attn_fwd_{g}", grid=(2,), in_specs=qkv + tabs, out_specs=[out, out],
        out_shape=[jax.ShapeDtypeStruct((SEQ, DIL_OUT_WIDTH), F32)] * 2,
        scratch_shapes=[pltpu.VMEM((SEQ, LANES), F32)] * 2,
        compiler_params=pltpu.CompilerParams(dimension_semantics=("parallel",)),
    )(*_in_hbm(proj, proj, proj, *tables))


def _dil_bwd(g, proj, tables, do, lse, c, dproj, deps=()):
    d = DILATIONS[g]
    one_block = d == N_BLOCKS

    def body(q_ref, k_ref, v_ref, c_ref, s1_ref, s2_ref, do_ref, lse_ref, cc_ref, dproj_in, *rest):
        dproj_ref, dq_acc, dk_acc, dv_acc, dq_out, dk_out, dv_out, q_rot, k_rot, sem = rest[len(deps):]
        dk_acc[...] = jnp.zeros_like(dk_acc)
        dv_acc[...] = jnp.zeros_like(dv_acc)
        _dil_rotate(q_ref, k_ref, c_ref, s1_ref, s2_ref, q_rot, k_rot)

        def block(n, carry):
            rows, prev, has_prev = _dil_rows(n, d)
            qb = q_rot[rows, :].astype(BF16)
            dob = do_ref[rows, :].astype(BF16)
            kw, vw = k_rot[rows, :].astype(BF16), v_ref[rows, :].astype(BF16)
            if one_block:
                row = lax.broadcasted_iota(jnp.int32, (BLOCK, BLOCK), 0)
                valid = lax.broadcasted_iota(jnp.int32, (BLOCK, BLOCK), 1) <= row
            else:
                kw = jnp.concatenate([k_rot[prev, :].astype(BF16), kw], axis=0)
                vw = jnp.concatenate([v_ref[prev, :].astype(BF16), vw], axis=0)
                valid = _dil_valid(_dil_prev_limit(has_prev))
            q2, do2 = _stack_heads(qb), _stack_heads(dob)
            lse_col, c_col = _head_columns(lse_ref[rows, :]), _head_columns(cc_ref[rows, :])
            p = jnp.where(jnp.concatenate([valid, valid], axis=0), jnp.exp(_dot_nt(q2, kw) - lse_col), 0.0)
            ds = (p * (_dot_nt(do2, vw) + c_col)).astype(BF16)
            dk, dv = _dot_tn(ds, q2), _dot_tn(p.astype(BF16), do2)
            dq_acc[rows, :] = _unstack_heads(_dot_nn(ds, kw)) * QK_SCALE
            if one_block:
                dk_acc[rows, :] += dk
                dv_acc[rows, :] += dv
            else:
                dk_acc[prev, :] += dk[:BLOCK]
                dv_acc[prev, :] += dv[:BLOCK]
                dk_acc[rows, :] += dk[BLOCK:]
                dv_acc[rows, :] += dv[BLOCK:]
            return carry

        lax.fori_loop(0, N_BLOCKS, block, 0, unroll=4)
        tabs = (c_ref[...], s1_ref[...], s2_ref[...])
        dq_out[...] = _rope_apply_t(dq_acc[...], *tabs).astype(BF16)
        dk_out[...] = _rope_apply_t(dk_acc[...], *tabs).astype(BF16)
        dv_out[...] = dv_acc[...].astype(BF16)
        pair = 2 * g + pl.program_id(0)
        _store_columns((dq_out, dk_out, dv_out), dproj_ref,
                       [base // LANES + pair for base in (COL_QA, COL_KA, COL_VA)], sem)

    qkv, tabs = _dil_specs(g)
    tok = pl.BlockSpec((SEQ, LANES), lambda p: (0, p))
    return pl.pallas_call(
        body, name=f"dil_attn_bwd_{g}", grid=(2,),
        in_specs=qkv + tabs + [tok, tok, tok, _ANY] + [_ANY] * len(deps), out_specs=_ANY,
        out_shape=jax.ShapeDtypeStruct(dproj.shape, dproj.dtype),
        scratch_shapes=[pltpu.VMEM((SEQ, LANES), F32)] * 3 + [pltpu.VMEM((SEQ, LANES), BF16)] * 3
        + [pltpu.VMEM((SEQ, LANES), F32)] * 2 + [pltpu.SemaphoreType.DMA((3,))],
        input_output_aliases={9: 0},
        compiler_params=pltpu.CompilerParams(dimension_semantics=("arbitrary",)),
    )(*_in_hbm(proj, proj, proj, *tables, do, lse, c, dproj), *deps)


def _group_weights(l0, l1, l2):
    m = jnp.maximum(jnp.maximum(l0, l1), l2)
    e0, e1, e2 = jnp.exp(l0 - m), jnp.exp(l1 - m), jnp.exp(l2 - m)
    tot = e0 + e1 + e2
    return e0 / tot, e1 / tot, e2 / tot


def _dil_combine(outs, lses, deps=()):
    def fn(o0, o1, o2, l0, l1, l2):
        w0, w1, w2 = _group_weights(l0, l1, l2)
        return w0 * o0 + w1 * o1 + w2 * o2

    w = DIL_OUT_WIDTH
    return _rowwise(fn, "dil_combine", SEQ, 512, [(a, w, 0) for a in list(outs) + list(lses)], [], [(w, F32)],
                    deps=deps)[0]


def _dil_combine_bwd(d_out, outs, lses, deps=()):
    w = DIL_OUT_WIDTH

    def fn(d, o0, o1, o2, l0, l1, l2):
        row = lax.broadcasted_iota(jnp.int32, (w, w), 0) // HEAD_DIM
        col = lax.broadcasted_iota(jnp.int32, (w, w), 1) // HEAD_DIM
        same_head = jnp.where(row == col, 1.0, 0.0).astype(BF16)
        ws = _group_weights(l0, l1, l2)
        dws = [_dot3_nn(d * og, same_head) for og in (o0, o1, o2)]
        mean = ws[0] * dws[0] + ws[1] * dws[1] + ws[2] * dws[2]
        return tuple(wg * d for wg in ws) + tuple(-wg * mean for wg in ws)

    res = _rowwise(fn, "dil_combine_bwd", SEQ, 256, [(a, w, 0) for a in [d_out] + list(outs) + list(lses)], [],
                   [(w, F32)] * 6, deps=deps)
    return res[:3], res[3:]


def _log1p(e):
    u = 1.0 + e
    return jnp.where(u == 1.0, e, jnp.log(u) * (e / (u - 1.0)))


def _fox_gate(proj, b_pad, deps=()):
    def body(f_ref, b_ref, *rest):
        o_ref = rest[-1]
        z = f_ref[...] + b_ref[...]
        logf = (jnp.minimum(z, 0.0) - _log1p(jnp.exp(-jnp.abs(z)))).T[:F_ROWS]
        row = lax.broadcasted_iota(jnp.int32, (BLOCK, BLOCK), 0)
        col = lax.broadcasted_iota(jnp.int32, (BLOCK, BLOCK), 1)
        before = jnp.where(row <= col, 1.0, 0.0).astype(BF16)
        carry = jnp.zeros((F_ROWS, 1), F32)
        for blk in range(N_BLOCKS):
            run = _dot3_nn(logf[:, blk * BLOCK:(blk + 1) * BLOCK], before) + carry
            o_ref[:, blk * BLOCK:(blk + 1) * BLOCK] = run
            carry = run[:, BLOCK - 1:BLOCK]

    return pl.pallas_call(
        body, name="fox_gate", grid=(1,),
        in_specs=[pl.BlockSpec((SEQ, LANES), lambda i: (0, COL_F // LANES)), pl.BlockSpec((1, LANES), lambda i: (0, 0))]
        + [_ANY] * len(deps),
        out_specs=pl.BlockSpec((F_ROWS, SEQ), lambda i: (0, 0)),
        out_shape=jax.ShapeDtypeStruct((F_ROWS, SEQ), F32),
    )(*_in_hbm(proj, b_pad), *deps)


def _fox_gate_bwd(d_cum, proj, b_pad, dproj):
    def body(d_ref, f_ref, b_ref, dproj_ref, dz_ref, db_ref):
        row = lax.broadcasted_iota(jnp.int32, (BLOCK, BLOCK), 0)
        col = lax.broadcasted_iota(jnp.int32, (BLOCK, BLOCK), 1)
        after = jnp.where(row >= col, 1.0, 0.0).astype(BF16)
        carry = jnp.zeros((F_ROWS, 1), F32)
        parts = [None] * N_BLOCKS
        for blk in reversed(range(N_BLOCKS)):
            run = _dot3_nn(d_ref[:, blk * BLOCK:(blk + 1) * BLOCK], after) + carry
            parts[blk] = run
            carry = run[:, 0:1]
        dlogf = jnp.concatenate(parts, axis=1)
        dlogf = jnp.concatenate([dlogf, jnp.zeros((LANES - F_ROWS, SEQ), F32)], axis=0).T
        dz = dlogf * _sigmoid(-(f_ref[...] + b_ref[...]))
        dz_ref[...] = dz.astype(BF16)
        db_ref[...] = jnp.sum(dz, axis=0, keepdims=True)

    f_cols = pl.BlockSpec((SEQ, LANES), lambda i: (0, COL_F // LANES))
    return pl.pallas_call(
        body, name="fox_gate_bwd", grid=(1,),
        in_specs=[pl.BlockSpec((F_ROWS, SEQ), lambda i: (0, 0)), f_cols, pl.BlockSpec((1, LANES), lambda i: (0, 0)), _ANY],
        out_specs=[f_cols, pl.BlockSpec((1, LANES), lambda i: (0, 0))],
        out_shape=[jax.ShapeDtypeStruct(dproj.shape, dproj.dtype), jax.ShapeDtypeStruct((1, LANES), F32)],
        input_output_aliases={3: 0},
    )(*_in_hbm(d_cum, proj, b_pad, dproj))


FOX_TILE = 256
FOX_TILES = SEQ // FOX_TILE


def _row_to_col(row):
    n = row.shape[1]
    eye = lax.broadcasted_iota(jnp.int32, (n, n), 0) == lax.broadcasted_iota(jnp.int32, (n, n), 1)
    return jnp.sum(jnp.where(eye, row, 0.0), axis=1, keepdims=True)


def _fox_bias(f_row, i):
    t = FOX_TILE
    ext = (i + 1) * t
    bias = _row_to_col(f_row[:, i * t:(i + 1) * t]) - f_row[:, :ext]
    row = lax.broadcasted_iota(jnp.int32, (t, ext), 0) + i * t
    col = lax.broadcasted_iota(jnp.int32, (t, ext), 1)
    return bias, col <= row


def _fox_specs():
    qkv = [pl.BlockSpec((SEQ, LANES), lambda p, base=base: (0, base // LANES + p)) for base in (COL_QB, COL_KB, COL_VB)]
    return qkv, pl.BlockSpec((F_ROWS, SEQ), lambda p: (0, 0))


def _fox_fwd(proj, f_rows):
    t = FOX_TILE

    def body(q_ref, k_ref, v_ref, f_ref, o_ref, lse_ref):
        pair = pl.program_id(0)
        upper = _upper_half()
        k16, v16 = k_ref[...].astype(BF16), v_ref[...].astype(BF16)
        f_row = [f_ref[pl.ds(2 * pair + e, 1), :] for e in range(2)]
        for i in range(FOX_TILES):
            ext = (i + 1) * t
            q_tile = (q_ref[i * t:(i + 1) * t, :] * QK_SCALE).astype(BF16)
            s2 = _dot_nt(_stack_heads(q_tile), k16[:ext])
            pns, lses = [], []
            for e in range(2):
                bias, causal = _fox_bias(f_row[e], i)
                s = jnp.where(causal, s2[e * t:(e + 1) * t] + bias, NEG_INF)
                m = jnp.max(s, axis=-1, keepdims=True)
                p = jnp.exp(s - m)
                den = jnp.sum(p, axis=-1, keepdims=True)
                pns.append((p * (1.0 / den)).astype(BF16))
                lses.append(m + jnp.log(den))
            o_ref[i * t:(i + 1) * t, :] = _unstack_heads(_dot_nn(jnp.concatenate(pns, axis=0), v16[:ext]))
            lse_ref[i * t:(i + 1) * t, :] = jnp.where(upper, lses[1], lses[0])

    qkv, f_spec = _fox_specs()
    tok = pl.BlockSpec((SEQ, LANES), lambda p: (0, p))
    return pl.pallas_call(
        body, name="fox_attn_fwd", grid=(FOX_WIDTH // LANES,),
        in_specs=qkv + [f_spec], out_specs=[tok, tok],
        out_shape=[jax.ShapeDtypeStruct((SEQ, FOX_WIDTH), F32)] * 2,
        compiler_params=pltpu.CompilerParams(
            dimension_semantics=("parallel",), vmem_limit_bytes=_vmem_limit(8 * t * SEQ * 4)),
    )(*_in_hbm(proj, proj, proj, f_rows))


def _fox_bwd(proj, do, lse, f_rows, dproj):
    t = FOX_TILE

    def body(q_ref, k_ref, v_ref, f_ref, do_ref, lse_ref, dproj_in, dproj_ref, df_ref, dk_acc, dv_acc,
             dq_out, dk_out, dv_out, sem):
        pair = pl.program_id(0)
        upper = _upper_half()
        k16, v16 = k_ref[...].astype(BF16), v_ref[...].astype(BF16)
        f_row = [f_ref[pl.ds(2 * pair + e, 1), :] for e in range(2)]
        dk_acc[...] = jnp.zeros_like(dk_acc)
        dv_acc[...] = jnp.zeros_like(dv_acc)
        df_ref[...] = jnp.zeros_like(df_ref)
        for i in range(FOX_TILES):
            ext = (i + 1) * t
            q_tile = (q_ref[i * t:(i + 1) * t, :] * QK_SCALE).astype(BF16)
            do_tile = do_ref[i * t:(i + 1) * t, :]
            lse_t = lse_ref[i * t:(i + 1) * t, :]
            q2, do2 = _stack_heads(q_tile), _stack_heads(do_tile)
            s2, dp2 = _dot_nt(q2, k16[:ext]), _dot_nt(do2, v16[:ext])
            ps, dss = [], []
            for e in range(2):
                bias, causal = _fox_bias(f_row[e], i)
                s = s2[e * t:(e + 1) * t] + bias
                p = jnp.where(causal, jnp.exp(s - lse_t[:, e * HEAD_DIM:e * HEAD_DIM + 1]), 0.0)
                dp = dp2[e * t:(e + 1) * t]
                ds = p * (dp - jnp.sum(p * dp, axis=-1, keepdims=True))
                df_ref[0, e:e + 1, :ext] -= jnp.sum(ds, axis=0, keepdims=True)
                ps.append(p.astype(BF16))
                dss.append(ds.astype(BF16))
            ds2, p2 = jnp.concatenate(dss, axis=0), jnp.concatenate(ps, axis=0)
            dq_out[i * t:(i + 1) * t, :] = (_unstack_heads(_dot_nn(ds2, k16[:ext])) * QK_SCALE).astype(BF16)
            dk_acc[:ext, :] += _dot_tn(ds2, q2)
            dv_acc[:ext, :] += _dot_tn(p2, do2)
        dk_out[...] = dk_acc[...].astype(BF16)
        dv_out[...] = dv_acc[...].astype(BF16)
        _store_columns((dq_out, dk_out, dv_out), dproj_ref, [base // LANES + pair for base in (COL_QB, COL_KB, COL_VB)],
                       sem)

    qkv, f_spec = _fox_specs()
    tok = pl.BlockSpec((SEQ, LANES), lambda p: (0, p))
    return pl.pallas_call(
        body, name="fox_attn_bwd", grid=(FOX_WIDTH // LANES,),
        in_specs=qkv + [f_spec, tok, tok, _ANY],
        out_specs=[_ANY, pl.BlockSpec((1, SUBLANES, SEQ), lambda p: (p, 0, 0))],
        out_shape=[jax.ShapeDtypeStruct(dproj.shape, dproj.dtype),
                   jax.ShapeDtypeStruct((FOX_WIDTH // LANES, SUBLANES, SEQ), F32)],
        scratch_shapes=[pltpu.VMEM((SEQ, LANES), F32)] * 2 + [pltpu.VMEM((SEQ, LANES), BF16)] * 3
        + [pltpu.SemaphoreType.DMA((3,))],
        input_output_aliases={6: 0},
        compiler_params=pltpu.CompilerParams(
            dimension_semantics=("arbitrary",), vmem_limit_bytes=_vmem_limit(10 * t * SEQ * 4)),
    )(*_in_hbm(proj, proj, proj, f_rows, do, lse, dproj))


MIX_TILE = 512


def _mix_out(out_a, out_b, proj, x, wt_pa, wt_pb, w_out, g_post, g_ffn_pre):
    tm = MIX_TILE

    def body(a_ref, b_ref, ga_ref, gb_ref, x_ref, wpa_ref, wpb_ref, wo_ref, g2_ref, g3_ref,
             merged_ref, mix_ref, x1_ref, h2_ref):
        ya = _dot_nn(a_ref[...].astype(BF16), wpa_ref[...])
        yb = _dot_nn(b_ref[...].astype(BF16), wpb_ref[...])
        merged = (_sigmoid(ga_ref[...]) * ya + _sigmoid(gb_ref[...]) * yb).astype(BF16)
        merged_ref[...] = merged
        mix = _dot_nn(merged, wo_ref[...])
        mix_ref[...] = mix
        x1 = x_ref[...] + mix * _rms_scale(mix) * g2_ref[...]
        x1_ref[...] = x1
        h2_ref[...] = (x1 * _rms_scale(x1) * g3_ref[...]).astype(BF16)

    def rows(w, cb=0):
        return pl.BlockSpec((tm, w), lambda i, cb=cb: (i, cb))

    def whole(a):
        return pl.BlockSpec(a.shape, lambda i: (0, 0))

    d = D_MODEL
    blk = _nbytes((tm, d), F32) * 6 + sum(_nbytes(a.shape, BF16) for a in (wt_pa, wt_pb, w_out))
    return pl.pallas_call(
        body, name="mix_out", grid=(SEQ // tm,),
        in_specs=[rows(DIL_OUT_WIDTH), rows(FOX_WIDTH), rows(d, COL_GA // d), rows(d, COL_GB // d), rows(d),
                  whole(wt_pa), whole(wt_pb), whole(w_out), whole(g_post), whole(g_ffn_pre)],
        out_specs=[rows(d)] * 4,
        out_shape=[jax.ShapeDtypeStruct((SEQ, d), dt) for dt in (BF16, F32, F32, BF16)],
        compiler_params=pltpu.CompilerParams(dimension_semantics=("parallel",), vmem_limit_bytes=_vmem_limit(blk)),
    )(out_a, out_b, proj, proj, x, wt_pa, wt_pb, w_out, g_post, g_ffn_pre)


def _mix_out_bwd(dmix, out_a, out_b, proj, wt_pa, wt_pb, w_out, deps=()):
    tm = MIX_TILE

    def body(dm_ref, a_ref, b_ref, ga_ref, gb_ref, wpa_ref, wpb_ref, wo_ref, *rest):
        dproj_ref, dya_ref, dyb_ref, da_ref, db_ref = rest[len(deps):]
        dmerged = _dot_nt(dm_ref[...], wo_ref[...])
        ya = _dot_nn(a_ref[...].astype(BF16), wpa_ref[...])
        yb = _dot_nn(b_ref[...].astype(BF16), wpb_ref[...])
        sa, sb = _sigmoid(ga_ref[...]), _sigmoid(gb_ref[...])
        dproj_ref[:, COL_GA:COL_GA + D_MODEL] = (dmerged * ya * (sa * (1.0 - sa))).astype(BF16)
        dproj_ref[:, COL_GB:COL_GB + D_MODEL] = (dmerged * yb * (sb * (1.0 - sb))).astype(BF16)
        dproj_ref[:, COL_GB + D_MODEL:] = jnp.zeros((tm, COL_QA - COL_GB - D_MODEL), BF16)
        dya = (dmerged * sa).astype(BF16)
        dyb = (dmerged * sb).astype(BF16)
        dya_ref[...] = dya
        dyb_ref[...] = dyb
        da_ref[...] = _dot_nt(dya, wpa_ref[...])
        db_ref[...] = _dot_nt(dyb, wpb_ref[...]).astype(BF16)

    def rows(w, cb=0):
        return pl.BlockSpec((tm, w), lambda i, cb=cb: (i, cb))

    def whole(a):
        return pl.BlockSpec(a.shape, lambda i: (0, 0))

    d = D_MODEL
    blk = _nbytes((tm, d), F32) * 8 + sum(_nbytes(a.shape, BF16) for a in (wt_pa, wt_pb, w_out))
    return pl.pallas_call(
        body, name="mix_out_bwd", grid=(SEQ // tm,),
        in_specs=[rows(d), rows(DIL_OUT_WIDTH), rows(FOX_WIDTH), rows(d, COL_GA // d), rows(d, COL_GB // d),
                  whole(wt_pa), whole(wt_pb), whole(w_out)] + [_ANY] * len(deps),
        out_specs=[rows(COL_QA)] + [rows(d)] * 2 + [rows(DIL_OUT_WIDTH), rows(FOX_WIDTH)],
        out_shape=[jax.ShapeDtypeStruct((SEQ, PROJ_COLS), BF16)] + [jax.ShapeDtypeStruct((SEQ, d), BF16)] * 2
        + [jax.ShapeDtypeStruct((SEQ, DIL_OUT_WIDTH), F32), jax.ShapeDtypeStruct((SEQ, FOX_WIDTH), BF16)],
        compiler_params=pltpu.CompilerParams(dimension_semantics=("parallel",), vmem_limit_bytes=_vmem_limit(blk)),
    )(dmix, out_a, out_b, proj, proj, wt_pa, wt_pb, w_out, *deps)


FFN_TM, FFN_TN = 2048, 256


def _ffn_up(h2, wt_gate, wt_up):
    tm, tn = FFN_TM, FFN_TN

    def body(h_ref, wg_ref, wu_ref, gate_ref, up_ref, act_ref):
        gate = _dot_nt(h_ref[...], wg_ref[...])
        up = _dot_nt(h_ref[...], wu_ref[...])
        gate_ref[...] = gate
        up_ref[...] = up
        act_ref[...] = (gate * _sigmoid(gate) * up).astype(BF16)

    tile = pl.BlockSpec((tm, tn), lambda i, j: (i, j))
    w_spec = pl.BlockSpec((tn, D_MODEL), lambda i, j: (j, 0))
    return pl.pallas_call(
        body, name="ffn_up", grid=(SEQ // tm, D_FF // tn),
        in_specs=[pl.BlockSpec((tm, D_MODEL), lambda i, j: (i, 0)), w_spec, w_spec],
        out_specs=[tile, tile, tile],
        out_shape=[jax.ShapeDtypeStruct((SEQ, D_FF), dt) for dt in (F32, F32, BF16)],
        compiler_params=pltpu.CompilerParams(
            dimension_semantics=("parallel", "parallel"), vmem_limit_bytes=_vmem_limit(8 * 2**20)),
    )(h2, wt_gate, wt_up)


def _ffn_act_bwd(dff, w_down, gate, up):
    tm, tn = FFN_TM, FFN_TN

    def body(d_ref, wd_ref, gate_ref, up_ref, dgate_ref, dup_ref):
        dact = _dot_nt(d_ref[...], wd_ref[...])
        gate = gate_ref[...]
        sg = _sigmoid(gate)
        dgate_ref[...] = (dact * up_ref[...] * (sg * (1.0 + gate * (1.0 - sg)))).astype(BF16)
        dup_ref[...] = (dact * (gate * sg)).astype(BF16)

    tile = pl.BlockSpec((tm, tn), lambda i, j: (i, j))
    return pl.pallas_call(
        body, name="ffn_act_bwd", grid=(SEQ // tm, D_FF // tn),
        in_specs=[pl.BlockSpec((tm, D_MODEL), lambda i, j: (i, 0)), pl.BlockSpec((tn, D_MODEL), lambda i, j: (j, 0)),
                  tile, tile],
        out_specs=[tile, tile],
        out_shape=[jax.ShapeDtypeStruct((SEQ, D_FF), BF16)] * 2,
        compiler_params=pltpu.CompilerParams(
            dimension_semantics=("parallel", "parallel"), vmem_limit_bytes=_vmem_limit(8 * 2**20)),
    )(dff, w_down, gate, up)


EPILOGUE_TM = 512


def _loss_head(act, w_down, x1, target, g_post):
    def fn(ff, x1, tgt, g):
        r = _rms_scale(ff)
        nrm = ff * r
        err = (x1 + nrm * g) - tgt
        loss = 0.5 * jnp.sum(jnp.mean(err * err, axis=-1, keepdims=True), axis=0, keepdims=True)
        dy = err * (1.0 / D_MODEL)
        u = dy * g
        dff = r * u - ff * (r * r * r) * jnp.mean(u * ff, axis=-1, keepdims=True)
        return dy, dff, jnp.broadcast_to(loss, (1, LANES)), jnp.sum(dy * nrm, axis=0, keepdims=True)

    d = D_MODEL
    return _matmul_rowwise([(act, w_down)], fn, "ffn_down_loss", EPILOGUE_TM, [(x1, d, 0), (target, d, 0)], [g_post],
                           [(d, F32), (d, BF16)], [LANES, d])


def _post_ffn_bwd(dgate, wt_gate, dup, wt_up, x1, dy, mix, g_ffn_pre, g_mix_post, deps=()):
    def fn(dh2, x1, dy, mix, g3, g2):
        dx, dg3 = _rms_bwd(x1, dh2, g3)
        dx1 = dy + dx
        dmix, dg2 = _rms_bwd(mix, dx1, g2)
        return dx1, dmix, dg3, dg2

    d = D_MODEL
    return _matmul_rowwise([(dgate, wt_gate), (dup, wt_up)], fn, "ffn_up_bwd", EPILOGUE_TM,
                           [(x1, d, 0), (dy, d, 0), (mix, d, 0)], [g_ffn_pre, g_mix_post],
                           [(d, F32), (d, BF16)], [d, d], deps=deps)


def _input_bwd(dproj, wt_r, x, dx1, g_pre, deps=()):
    def fn(dh, x, dx1, g):
        dx, dg = _rms_bwd(x, dh, g)
        return dx1 + dx, dg

    d = D_MODEL
    return _matmul_rowwise([(dproj, wt_r)], fn, "in_proj_bwd", EPILOGUE_TM, [(x, d, 0), (dx1, d, 0)], [g_pre],
                           [(d, F32)], [d], deps=deps)


def _adam_math(w, g, m, v):
    m = ADAM_B1 * m + (1.0 - ADAM_B1) * g
    v = ADAM_B2 * v + (1.0 - ADAM_B2) * (g * g)
    m_hat = m / (1.0 - ADAM_B1 ** ADAM_STEP)
    v_hat = v / (1.0 - ADAM_B2 ** ADAM_STEP)
    delta = -ADAM_LR * (m_hat / (jnp.sqrt(v_hat) + ADAM_EPS) + ADAM_WD * w)
    return delta, m, v


def _adam(w, mine, recv, m, v, name):
    r, c = w.shape
    tc = _col_tile(r, c)

    def body(w_ref, p_ref, r_ref, m_ref, v_ref, g_ref, d_ref, nm_ref, nv_ref):
        g = ((p_ref[...] + r_ref[0].astype(F32)) + r_ref[1].astype(F32)) + r_ref[2].astype(F32)
        g_ref[...] = g
        d_ref[...], nm_ref[...], nv_ref[...] = _adam_math(w_ref[...], g, m_ref[...], v_ref[...])

    spec = pl.BlockSpec((r, tc), lambda j: (0, j))
    return pl.pallas_call(
        body, name=name, grid=(c // tc,),
        in_specs=[spec, spec, pl.BlockSpec((3, r, tc), lambda j: (0, 0, j)), spec, spec], out_specs=[spec] * 4,
        out_shape=[jax.ShapeDtypeStruct((r, c), F32)] * 4,
        compiler_params=pltpu.CompilerParams(dimension_semantics=("parallel",)),
    )(*_in_hbm(w, mine, recv, m, v))


def _adam_small(gathered, ws, ms, vs, loss_parts):
    n = len(ws)

    def body(*refs):
        outs = refs[4 * n + 1:]
        loss = refs[4 * n][0]
        for dev in range(1, N_DEV):
            loss = loss + refs[4 * n][dev]
        outs[4 * n][...] = loss
        for i in range(n):
            ga_ref, w_ref, m_ref, v_ref = (refs[j * n + i] for j in range(4))
            g = ga_ref[0]
            for dev in range(1, N_DEV):
                g = g + ga_ref[dev]
            g = g[:, :w_ref.shape[1]]
            outs[4 * i][...] = g
            outs[4 * i + 1][...], outs[4 * i + 2][...], outs[4 * i + 3][...] = _adam_math(
                w_ref[...], g, m_ref[...], v_ref[...])

    out_shape = [jax.ShapeDtypeStruct(w.shape, F32) for w in ws for _ in range(4)]
    out_shape.append(jax.ShapeDtypeStruct((1, LANES), F32))
    out = pl.pallas_call(body, name="adam_small", out_shape=out_shape)(*gathered, *ws, *ms, *vs, loss_parts)
    return [out[4 * i:4 * i + 4] for i in range(n)], out[4 * n]


_PROJ_SEGMENTS = ((3848, 5896), (None, COL_QA - 2 * D_MODEL), (0, 3840), (3840, 3848), (None, PROJ_COLS - COL_F - 8))


def _proj_weight_t(gathered):
    w = gathered.reshape(IN_COLS, D_MODEL)
    return jnp.concatenate([jnp.zeros((hi, D_MODEL), w.dtype) if lo is None else w[lo:hi] for lo, hi in _PROJ_SEGMENTS],
                           axis=0)


def _proj_weight_grad_slots(dwt_r):
    starts, at = [], 0
    for lo, hi in _PROJ_SEGMENTS:
        if lo is not None:
            starts.append((lo, hi, at))
        at += hi if lo is None else hi - lo
    slots = []
    for dev in range(N_DEV):
        pieces, lo, end = [], dev * IN_SHARD, (dev + 1) * IN_SHARD
        for seg_lo, seg_hi, seg_at in sorted(starts):
            a, b = max(lo, seg_lo), min(end, seg_hi)
            if a < b:
                pieces.append(dwt_r[seg_at + a - seg_lo:seg_at + b - seg_lo])
        slots.append(pieces[0] if len(pieces) == 1 else jnp.concatenate(pieces, axis=0))
    return jnp.stack(slots)


def kernel(x, w_in, w_proj_a, w_proj_b, w_out, b_forget, w_ffn_gate, w_ffn_up, w_ffn_down, norm_mix_pre, norm_mix_post, norm_ffn_pre, norm_ffn_post, loss_target, m_w_in, m_w_proj_a, m_w_proj_b, m_w_out, m_b_forget, m_w_ffn_gate, m_w_ffn_up, m_w_ffn_down, m_norm_mix_pre, m_norm_mix_post, m_norm_ffn_pre, m_norm_ffn_post, v_w_in, v_w_proj_a, v_w_proj_b, v_w_out, v_b_forget, v_w_ffn_gate, v_w_ffn_up, v_w_ffn_down, v_norm_mix_pre, v_norm_mix_post, v_norm_ffn_pre, v_norm_ffn_post):
    d = D_MODEL
    names = ("w_in", "w_proj_a", "w_proj_b", "w_out", "w_ffn_gate", "w_ffn_up", "w_ffn_down")
    col_sharded = ("w_in", "w_ffn_gate", "w_ffn_up")

    def row_shards(arrs):
        return {k: (a[0].T if k in col_sharded else a[0]) for k, a in zip(names, arrs)}

    shards = row_shards((w_in, w_proj_a, w_proj_b, w_out, w_ffn_gate, w_ffn_up, w_ffn_down))
    moments_m = row_shards((m_w_in, m_w_proj_a, m_w_proj_b, m_w_out, m_w_ffn_gate, m_w_ffn_up, m_w_ffn_down))
    moments_v = row_shards((v_w_in, v_w_proj_a, v_w_proj_b, v_w_out, v_w_ffn_gate, v_w_ffn_up, v_w_ffn_down))
    pos = jnp.stack([lax.axis_index("c"), 2 * lax.axis_index("x") + lax.axis_index("y")]).astype(jnp.int32)
    x2, target = x[0], loss_target[0]

    me = 4 * lax.axis_index("x") + 2 * lax.axis_index("y") + lax.axis_index("c")
    mid_names, ffn_names = names[1:4], names[4:]
    first_names, later_names = names[:1], names[1:]
    shards16 = {k: shards[k].astype(BF16) for k in names}

    def landing(k):
        return lax.dynamic_update_slice(lax.empty((N_DEV,) + shards[k].shape, BF16), shards16[k][None], (me, 0, 0))

    ag_first = _exchange_start("ag_first_chips_start", _gather_chips_copies, [shards16[k] for k in first_names],
                               [landing(k) for k in first_names], 3 * len(first_names))
    h = _rowwise(lambda xb, g: xb * _rms_scale(xb) * g, "norm_mix_pre", SEQ, 256, [(x2, d, 0)], [norm_mix_pre],
                 [(d, BF16)], deps=[ag_first.token])[0]
    _, lands = _exchange_wait("ag_first_chips_wait", ag_first, [h])
    ag_first = _exchange_start("ag_first_sibling_start", _gather_sibling_copies, [], lands, 4 * len(first_names))
    ag_later = _exchange_start("ag_later_chips_start", _gather_chips_copies, [shards16[k] for k in later_names],
                               [landing(k) for k in later_names], 3 * len(later_names), after=[ag_first.token])
    gathered = dict(zip(first_names, _exchange_wait("ag_first_sibling_wait", ag_first, [ag_later.token])[1]))
    wt_r = _proj_weight_t(gathered["w_in"])

    proj = _matmul([(h, wt_r)], "nt", F32, "in_proj", 1024, 896, 1024)
    tables = _rope_tables()
    o_dil, lse_dil = zip(*[_dil_fwd(g, proj, tables) for g in range(len(DILATIONS))])
    out_a = _dil_combine(o_dil, lse_dil)
    _, lands = _exchange_wait("ag_later_chips_wait", ag_later, [out_a])
    ag_later = _exchange_start("ag_later_sibling_start", _gather_sibling_copies, [], lands, 4 * len(later_names))

    b_pad = jnp.pad(b_forget, ((0, 0), (0, LANES - N_FOX_HEADS)))
    f_rows = _fox_gate(proj, b_pad, deps=[ag_later.token])
    out_b, lse_fox = _fox_fwd(proj, f_rows)

    gathered = dict(zip(later_names, _exchange_wait("ag_later_sibling_wait", ag_later, [out_b])[1]))
    wt_pa = gathered["w_proj_a"].transpose(1, 0, 2).reshape(DIL_OUT_WIDTH, d)
    wt_pb = gathered["w_proj_b"].transpose(1, 0, 2).reshape(FOX_WIDTH, d)
    w_o = gathered["w_out"].reshape(d, d)
    wt_g = gathered["w_ffn_gate"].reshape(D_FF, d)
    wt_u = gathered["w_ffn_up"].reshape(D_FF, d)
    w_d = gathered["w_ffn_down"].reshape(D_FF, d)
    merged, mix, x1, h2 = _mix_out(out_a, out_b, proj, x2, wt_pa, wt_pb, w_o, norm_mix_post, norm_ffn_pre)

    gate, up, act = _ffn_up(h2, wt_g, wt_u)
    dy, dff, loss_part, dg_ffn_post = _loss_head(act, w_d, x1, target, norm_ffn_post)

    dgate, dup = _ffn_act_bwd(dff, w_d, gate, up)
    grads_t = {}
    grads_t["w_ffn_down"] = _matmul([(act, dff)], "tn", F32, "grad_w_ffn_down", 1408, 512, 2048)
    grads_t["w_ffn_gate"] = _matmul([(dgate, h2)], "tn", F32, "grad_w_ffn_gate", 1408, 512, 2048)
    grads_t["w_ffn_up"] = _matmul([(dup, h2)], "tn", F32, "grad_w_ffn_up", 1408, 512, 2048)
    rs_ffn = _ReduceScatter("ffn", {k: grads_t[k] for k in ffn_names}, pos)
    dx1, dmix, dg_ffn_pre, dg_mix_post = _post_ffn_bwd(dgate, wt_g, dup, wt_u, x1, dy, mix, norm_ffn_pre, norm_mix_post,
                                                       deps=[rs_ffn.token])
    rs_ffn.start_chips([dmix])

    dproj, dya, dyb, d_out_a, d_out_b = _mix_out_bwd(dmix, out_a, out_b, proj, wt_pa, wt_pb, w_o, deps=[rs_ffn.token])
    grads_t["w_out"] = _matmul([(merged, dmix)], "tn", F32, "grad_w_out", 1024, 1024, 1024)
    def column_slots(g):
        return g.reshape(g.shape[0], N_DEV, LANES).transpose(1, 0, 2)

    grads_t["w_proj_a"] = column_slots(_matmul([(out_a, dya)], "tn", F32, "grad_w_proj_a", DIL_OUT_WIDTH, 1024, SEQ))
    grads_t["w_proj_b"] = column_slots(_matmul([(out_b, dyb)], "tn", F32, "grad_w_proj_b", FOX_WIDTH, 1024, SEQ))
    rs_mid = _ReduceScatter("mid", {k: grads_t[k] for k in mid_names}, pos)

    do_dil, c_dil = _dil_combine_bwd(d_out_a, o_dil, lse_dil, deps=[rs_mid.token])
    rs_mid.start_chips([c_dil[0]])
    dproj, d_cum = _fox_bwd(proj, d_out_b, lse_fox, f_rows, dproj)
    d_cum_rows = jnp.pad(d_cum[:, :2].reshape(N_FOX_HEADS, SEQ), ((0, F_ROWS - N_FOX_HEADS), (0, 0)))
    dproj, db_part = _fox_gate_bwd(d_cum_rows, proj, b_pad, dproj)
    for g in range(len(DILATIONS)):
        dproj = _dil_bwd(g, proj, tables, do_dil[g], lse_dil[g], c_dil[g], dproj, deps=[rs_mid.token])

    dwt_r = _matmul([(dproj, h)], "tn", F32, "grad_w_in", 896, 1024, 2048)
    rs_in = _ReduceScatter("in", {"w_in": _proj_weight_grad_slots(dwt_r)}, pos)
    def finish(rs, after):
        return {k: _adam(shards[k], mine, recv, moments_m[k], moments_v[k], "adam_" + k)
                for k, (mine, recv) in rs.finish(after).items()}

    done = finish(rs_ffn, [rs_in.token])
    rs_in.start_chips([done[k][0] for k in ffn_names])
    grad_x, dg_mix_pre = _input_bwd(dproj, wt_r, x2, dx1, norm_mix_pre, deps=[rs_in.token])
    done.update(finish(rs_mid, [grad_x]))

    small_all = _all_gather([dg_mix_pre, dg_mix_post, dg_ffn_pre, dg_ffn_post, db_part, loss_part],
                            "small_grads_all_gather", deps=[done[k][0] for k in mid_names])
    small, loss = _adam_small(small_all[:5], [norm_mix_pre, norm_mix_post, norm_ffn_pre, norm_ffn_post, b_forget],
                              [m_norm_mix_pre, m_norm_mix_post, m_norm_ffn_pre, m_norm_ffn_post, m_b_forget],
                              [v_norm_mix_pre, v_norm_mix_post, v_norm_ffn_pre, v_norm_ffn_post, v_b_forget],
                              small_all[5])

    done.update(finish(rs_in, [small[0][0]]))

    def leaves(i):
        def nat(k):
            a = done[k][i]
            return (a.T if k in col_sharded else a)[None]

        return [nat("w_in"), nat("w_proj_a"), nat("w_proj_b"), nat("w_out"), small[4][i],
                nat("w_ffn_gate"), nat("w_ffn_up"), nat("w_ffn_down"), *[small[r][i] for r in range(4)]]

    return (loss[0, 0], grad_x[None], *leaves(0), *leaves(1), *leaves(2), *leaves(3))
```

```python
import functools
import math

import jax
import jax.numpy as jnp
import numpy as np
from jax import lax
from jax.experimental import pallas as pl
from jax.experimental.pallas import tpu as pltpu

F32 = jnp.float32
BF16 = jnp.bfloat16
MESH = pl.DeviceIdType.MESH

D_MODEL = 1024
SEQ = 2048
HEAD_DIM = 64
BLOCK = 128
N_BLOCKS = SEQ // BLOCK
DILATIONS = (1, 4, 16)
N_FOX_HEADS = 8
DIL_WIDTH = 768
DIL_OUT_WIDTH = 256
FOX_WIDTH = 512
D_FF = 2816
ROPE_THETA = 500000.0
ROPE_DIM = HEAD_DIM // 4
ROPE_HALF = ROPE_DIM // 2
EPS = 1e-6
NEG_INF = -1e30
QK_SCALE = 1.0 / math.sqrt(HEAD_DIM)
IN_COLS = 5896
N_DEV = 8
IN_SHARD = IN_COLS // N_DEV

ADAM_LR = 0.001
ADAM_B1 = 0.9
ADAM_B2 = 0.999
ADAM_EPS = 1e-08
ADAM_WD = 0.01
ADAM_STEP = 10

V7X_VMEM_BYTES = 64 * 2**20
LANES = 128
SUBLANES = 8

PROJ_COLS = 6272
COL_GA, COL_GB = 0, 1024
COL_QA, COL_KA, COL_VA = 2304, 3072, 3840
COL_QB, COL_KB, COL_VB = 4608, 5120, 5632
COL_F = 6144
F_ROWS = 16


def _vmem_limit(block_bytes):
    want = 2 * block_bytes + 16 * 2**20
    return int(min(max(want, 32 * 2**20), V7X_VMEM_BYTES - 8 * 2**20))


def _nbytes(shape, dtype):
    return math.prod(shape) * jnp.dtype(dtype).itemsize


def _in_hbm(*arrays):
    return [pltpu.with_memory_space_constraint(a, pltpu.HBM) for a in arrays]


def _dot(a, b, dims):
    return lax.dot_general(a, b, (dims, ((), ())), preferred_element_type=F32)


def _dot_nn(a, b):
    return _dot(a, b, ((1,), (0,)))


def _dot_nt(a, b):
    return _dot(a, b, ((1,), (1,)))


def _dot_tn(a, b):
    return _dot(a, b, ((0,), (0,)))


def _sigmoid(z):
    return 1.0 / (1.0 + jnp.exp(-z))


def _split3(x):
    hi = x.astype(BF16)
    r1 = x - hi.astype(F32)
    mid = r1.astype(BF16)
    lo = (r1 - mid.astype(F32)).astype(BF16)
    return hi, mid, lo


def _dot3_nn(x, ones_matrix):
    hi, mid, lo = _split3(x)
    return (_dot_nn(hi, ones_matrix) + _dot_nn(mid, ones_matrix)) + _dot_nn(lo, ones_matrix)


def _rowwise(fn, name, n_rows, tm, row_ins, bcast_ins, row_outs, acc_outs=(), deps=()):
    n_in = len(row_ins) + len(bcast_ins)
    n_ro = len(row_outs)

    def body(*refs):
        res = fn(*[r[...] for r in refs[:n_in]])
        if not isinstance(res, (tuple, list)):
            res = (res,)
        outs = refs[n_in + len(deps):]
        for r, o in zip(res[:n_ro], outs[:n_ro]):
            o[...] = r.astype(o.dtype)
        first = pl.program_id(0) == 0
        for r, o in zip(res[n_ro:], outs[n_ro:]):
            _accumulate(o, r, first)

    in_specs = [pl.BlockSpec((tm, w), lambda i, cb=cb: (i, cb)) for _, w, cb in row_ins]
    in_specs += [pl.BlockSpec(a.shape, lambda i: (0, 0)) for a in bcast_ins]
    in_specs += [pl.BlockSpec(memory_space=pl.ANY)] * len(deps)
    out_specs = [pl.BlockSpec((tm, w), lambda i: (i, 0)) for w, _ in row_outs]
    out_specs += [pl.BlockSpec((1, w), lambda i: (0, 0)) for w in acc_outs]
    out_shape = [jax.ShapeDtypeStruct((n_rows, w), dt) for w, dt in row_outs]
    out_shape += [jax.ShapeDtypeStruct((1, w), F32) for w in acc_outs]
    blk = sum(_nbytes((tm, w), a.dtype) for a, w, _ in row_ins) + sum(_nbytes((tm, w), dt) for w, dt in row_outs)
    return pl.pallas_call(
        body, name=name, grid=(n_rows // tm,), in_specs=in_specs, out_specs=out_specs, out_shape=out_shape,
        compiler_params=pltpu.CompilerParams(
            dimension_semantics=("arbitrary" if acc_outs else "parallel",), vmem_limit_bytes=_vmem_limit(3 * blk)),
    )(*_in_hbm(*[a for a, _, _ in row_ins], *bcast_ins), *deps)


def _accumulate(o_ref, part, first):
    @pl.when(first)
    def _():
        o_ref[...] = part

    @pl.when(jnp.logical_not(first))
    def _():
        o_ref[...] += part


_MM_DIMS = {"nn": ((1,), (0,)), "nt": ((1,), (1,)), "tn": ((0,), (0,))}


def _matmul(pairs, mode, out_dtype, name, tm, tn, tk, deps=()):
    a0, b0 = pairs[0]
    if mode == "tn":
        kk, m = a0.shape
    else:
        m, kk = a0.shape
    n = b0.shape[0] if mode == "nt" else b0.shape[1]
    assert m % tm == 0 and n % tn == 0 and kk % tk == 0, (name, m, n, kk)
    nk = kk // tk
    n_pairs = len(pairs)
    dims = _MM_DIMS[mode]
    n_in = 2 * n_pairs + len(deps)

    def body(*refs):
        o_ref = refs[n_in]
        part = None
        for p in range(n_pairs):
            d = _dot(refs[2 * p][...].astype(BF16), refs[2 * p + 1][...].astype(BF16), dims)
            part = d if part is None else part + d
        if nk == 1:
            o_ref[...] = part.astype(o_ref.dtype)
            return
        acc = refs[n_in + 1]
        k = pl.program_id(2)

        @pl.when(k == 0)
        def _():
            acc[...] = part

        @pl.when(k > 0)
        def _():
            acc[...] += part

        @pl.when(k == nk - 1)
        def _():
            o_ref[...] = acc[...].astype(o_ref.dtype)

    if mode == "tn":
        a_spec = pl.BlockSpec((tk, tm), lambda i, j, k: (k, i))
    else:
        a_spec = pl.BlockSpec((tm, tk), lambda i, j, k: (i, k))
    if mode == "nt":
        b_spec = pl.BlockSpec((tn, tk), lambda i, j, k: (j, k))
    else:
        b_spec = pl.BlockSpec((tk, tn), lambda i, j, k: (k, j))
    blk = sum(_nbytes((tm, tk), a.dtype) + _nbytes((tk, tn), b.dtype) for a, b in pairs) + 2 * _nbytes((tm, tn), F32)
    flat = [a for pair in pairs for a in pair]
    return pl.pallas_call(
        body, name=name, grid=(m // tm, n // tn, nk),
        in_specs=[a_spec, b_spec] * n_pairs + [pl.BlockSpec(memory_space=pl.ANY)] * len(deps),
        out_specs=pl.BlockSpec((tm, tn), lambda i, j, k: (i, j)),
        out_shape=jax.ShapeDtypeStruct((m, n), out_dtype),
        scratch_shapes=[] if nk == 1 else [pltpu.VMEM((tm, tn), F32)],
        compiler_params=pltpu.CompilerParams(
            dimension_semantics=("parallel", "parallel", "arbitrary"), vmem_limit_bytes=_vmem_limit(blk)),
    )(*flat, *deps)


def _matmul_rowwise(pairs, fn, name, tm, row_ins, bcast_ins, row_outs, acc_outs=(), deps=()):
    m = pairs[0][0].shape[0]
    n_mm, n_in = 2 * len(pairs), len(row_ins) + len(bcast_ins)
    n_ro = len(row_outs)

    def body(*refs):
        prod = None
        for p in range(len(pairs)):
            part = _dot_nn(refs[2 * p][...].astype(BF16), refs[2 * p + 1][...].astype(BF16))
            prod = part if prod is None else prod + part
        res = fn(prod, *[r[...] for r in refs[n_mm:n_mm + n_in]])
        outs = refs[n_mm + n_in + len(deps):]
        for r, o in zip(res[:n_ro], outs[:n_ro]):
            o[...] = r.astype(o.dtype)
        first = pl.program_id(0) == 0
        for r, o in zip(res[n_ro:], outs[n_ro:]):
            _accumulate(o, r, first)

    in_specs = []
    for a, b in pairs:
        in_specs += [pl.BlockSpec((tm, a.shape[1]), lambda i: (i, 0)),
                     pl.BlockSpec(b.shape, lambda i: (0, 0), pipeline_mode=pl.Buffered(1))]
    in_specs += [pl.BlockSpec((tm, w), lambda i, cb=cb: (i, cb)) for _, w, cb in row_ins]
    in_specs += [pl.BlockSpec(a.shape, lambda i: (0, 0)) for a in bcast_ins]
    in_specs += [_ANY] * len(deps)
    out_specs = [pl.BlockSpec((tm, w), lambda i: (i, 0)) for w, _ in row_outs]
    out_specs += [pl.BlockSpec((1, w), lambda i: (0, 0)) for w in acc_outs]
    out_shape = [jax.ShapeDtypeStruct((m, w), dt) for w, dt in row_outs]
    out_shape += [jax.ShapeDtypeStruct((1, w), F32) for w in acc_outs]
    blk = sum(_nbytes((tm, a.shape[1]), a.dtype) + _nbytes(b.shape, b.dtype) // 2 for a, b in pairs)
    blk += sum(_nbytes((tm, w), a.dtype) for a, w, _ in row_ins) + sum(_nbytes((tm, w), dt) for w, dt in row_outs)
    return pl.pallas_call(
        body, name=name, grid=(m // tm,), in_specs=in_specs, out_specs=out_specs, out_shape=out_shape,
        compiler_params=pltpu.CompilerParams(dimension_semantics=("arbitrary",), vmem_limit_bytes=_vmem_limit(blk)),
    )(*[a for pair in pairs for a in pair], *[a for a, _, _ in row_ins], *bcast_ins, *deps)


def _rms_scale(x):
    return lax.rsqrt(jnp.mean(x * x, axis=-1, keepdims=True) + EPS)


def _rms_bwd(xin, dyn, g):
    r = _rms_scale(xin)
    u = dyn * g
    dx = r * u - xin * (r * r * r) * jnp.mean(u * xin, axis=-1, keepdims=True)
    dg = jnp.sum(dyn * xin * r, axis=0, keepdims=True)
    return dx, dg


def _mesh_pos():
    return lax.axis_index("x"), lax.axis_index("y"), lax.axis_index("c")


def _all_gather(xs, name, deps=()):
    n = len(xs)

    def body(*refs):
        x_refs, out_refs = refs[:n], refs[n + len(deps):2 * n + len(deps)]
        send_sems, recv_sems, local_sems = refs[2 * n + len(deps):]
        mx, my, mc = _mesh_pos()
        me, sib = (mx, my, mc), (mx, my, 1 - mc)
        chips = [(1 - mx, my), (mx, 1 - my), (1 - mx, 1 - my)]

        def slot(a, dev):
            px, py, pc = dev
            return out_refs[a].at[4 * px + 2 * py + pc]

        def copy(k, a, block, to, src=None):
            return pltpu.make_async_remote_copy(
                src_ref=slot(a, block) if src is None else src, dst_ref=slot(a, block),
                send_sem=send_sems.at[a * 7 + k], recv_sem=recv_sems.at[a * 7 + k],
                device_id=to, device_id_type=MESH)

        mine = [pltpu.make_async_copy(x_refs[a], slot(a, me), local_sems.at[a]) for a in range(n)]
        for cp in mine:
            cp.start()
        first = []
        for a in range(n):
            first.append(copy(0, a, me, sib, x_refs[a]))
            first += [copy(1 + j, a, me, (*chip, mc), x_refs[a]) for j, chip in enumerate(chips)]
        for cp in first:
            cp.start()
        passed = []
        for a in range(n):
            for j, chip in enumerate(chips):
                copy(1 + j, a, (*chip, mc), me).wait_recv()
                fwd = copy(4 + j, a, (*chip, mc), sib)
                fwd.start()
                passed.append(fwd)
        for a in range(n):
            copy(0, a, sib, me).wait_recv()
            for j, chip in enumerate(chips):
                copy(4 + j, a, (*chip, 1 - mc), me).wait_recv()
        for cp in first + passed:
            cp.wait_send()
        for cp in mine:
            cp.wait()

    hbm = pl.BlockSpec(memory_space=pl.ANY)
    return pl.pallas_call(
        body, name=name,
        out_shape=[jax.ShapeDtypeStruct((N_DEV,) + x.shape, x.dtype) for x in xs],
        in_specs=[hbm] * (n + len(deps)), out_specs=[hbm] * n,
        scratch_shapes=[pltpu.SemaphoreType.DMA((7 * n,)), pltpu.SemaphoreType.DMA((7 * n,)),
                        pltpu.SemaphoreType.DMA((n,))],
    )(*xs, *deps)


_HBM = pl.BlockSpec(memory_space=pltpu.HBM)
_SEM = pl.BlockSpec(memory_space=pltpu.SEMAPHORE)
_ANY = pl.BlockSpec(memory_space=pl.ANY)
_DATAFLOW = pltpu.SideEffectType.DATAFLOW_SIDE_EFFECTING


def _flip_peer(flip):
    mx, my, mc = _mesh_pos()
    return (1 - mx if flip & 2 else mx, 1 - my if flip & 1 else my, mc)


def _remote(src, dst, send_sems, recv_sems, k, peer):
    return pltpu.make_async_remote_copy(src_ref=src, dst_ref=dst, send_sem=send_sems.at[k], recv_sem=recv_sems.at[k],
                                        device_id=peer, device_id_type=MESH)


def _gather_chips_copies(srcs, lands, send_sems, recv_sems):
    mx, my, mc = _mesh_pos()
    me = 4 * mx + 2 * my + mc
    return [_remote(srcs[a], lands[a].at[me], send_sems, recv_sems, 3 * a + flip - 1, _flip_peer(flip))
            for a in range(len(srcs)) for flip in (1, 2, 3)]


def _gather_sibling_copies(srcs, lands, send_sems, recv_sems):
    mx, my, mc = _mesh_pos()
    return [_remote(lands[a].at[2 * k + mc], lands[a].at[2 * k + mc], send_sems, recv_sems, 4 * a + k, (mx, my, 1 - mc))
            for a in range(len(lands)) for k in range(4)]


def _scatter_sibling_copies(srcs, lands, send_sems, recv_sems):
    mx, my, mc = _mesh_pos()
    return [_remote(srcs[a].at[k, 1 - mc], lands[a].at[k], send_sems, recv_sems, 4 * a + k, (mx, my, 1 - mc))
            for a in range(len(srcs)) for k in range(4)]


def _scatter_chips_copies(srcs, lands, send_sems, recv_sems):
    mx, my, _ = _mesh_pos()
    k0 = 2 * mx + my
    return [_remote(srcs[a].at[jnp.bitwise_xor(k0, flip)], lands[a].at[flip - 1], send_sems, recv_sems,
                    3 * a + flip - 1, _flip_peer(flip))
            for a in range(len(srcs)) for flip in (1, 2, 3)]


class _Exchange:
    def __init__(self, copies, n_src, send_sems, recv_sems, thru, token):
        self.copies, self.n_src, self.send_sems, self.recv_sems, self.thru, self.token = (
            copies, n_src, send_sems, recv_sems, thru, token)


def _exchange_start(name, copies, srcs, lands, n_copies, after=()):
    bufs = list(srcs) + list(lands)
    nb, ns = len(bufs), len(srcs)

    def body(*refs):
        send_sems, recv_sems = refs[nb + len(after)], refs[nb + len(after) + 1]
        for cp in copies(refs[:ns], refs[ns:nb], send_sems, recv_sems):
            cp.start()
        refs[-1][...] = jnp.zeros_like(refs[-1])

    out = pl.pallas_call(
        body, name=name,
        out_shape=(pltpu.SemaphoreType.DMA((n_copies,)), pltpu.SemaphoreType.DMA((n_copies,)),
                   *[pltpu.HBM(b.shape, b.dtype) for b in bufs], jax.ShapeDtypeStruct((SUBLANES, LANES), F32)),
        in_specs=[_HBM] * nb + [_ANY] * len(after),
        out_specs=(_SEM, _SEM, *[_HBM] * nb, pl.BlockSpec(memory_space=pltpu.VMEM)),
        input_output_aliases={i: 2 + i for i in range(nb)},
        compiler_params=pltpu.CompilerParams(has_side_effects=_DATAFLOW),
    )(*[pltpu.with_memory_space_constraint(b, pltpu.HBM) for b in bufs], *after)
    return _Exchange(copies, ns, out[0], out[1], list(out[2:2 + nb]), out[-1])


def _exchange_wait(name, ex, after):
    nb, ns = len(ex.thru), ex.n_src

    def body(*refs):
        for cp in ex.copies(refs[:ns], refs[ns:nb], refs[nb], refs[nb + 1]):
            cp.wait_send()
            cp.wait_recv()

    out = pl.pallas_call(
        body, name=name, out_shape=tuple(pltpu.HBM(b.shape, b.dtype) for b in ex.thru),
        in_specs=[_HBM] * nb + [_SEM, _SEM] + [_ANY] * len(after), out_specs=tuple([_HBM] * nb),
        input_output_aliases={i: i for i in range(nb)},
        compiler_params=pltpu.CompilerParams(has_side_effects=_DATAFLOW),
    )(*ex.thru, ex.send_sems, ex.recv_sems, *after)
    return list(out[:ns]), list(out[ns:])


def _col_tile(r, c):
    return next(t for t in (1024, 512, 256, 128) if c % t == 0 and (r * t * 4 <= 2**20 or t == 128))


def _add_sibling(g4, recv, pos, name):
    _, _, r, c = g4.shape
    tc = _col_tile(r, c)

    def body(pos_ref, g_ref, r_ref, o16_ref, mine_ref):
        s = g_ref[0, 0] + r_ref[0]
        o16_ref[0] = s.astype(BF16)

        @pl.when(pl.program_id(1) == pos_ref[1])
        def _():
            mine_ref[...] = s

    slot = pl.BlockSpec((1, r, tc), lambda j, k, pos_ref: (k, 0, j))
    return pl.pallas_call(
        body, name=name,
        out_shape=[jax.ShapeDtypeStruct((4, r, c), BF16), jax.ShapeDtypeStruct((r, c), F32)],
        grid_spec=pltpu.PrefetchScalarGridSpec(
            num_scalar_prefetch=1, grid=(c // tc, 4),
            in_specs=[pl.BlockSpec((1, 1, r, tc), lambda j, k, pos_ref: (k, pos_ref[0], 0, j)), slot],
            out_specs=[slot, pl.BlockSpec((r, tc), lambda j, k, pos_ref: (0, j))]),
        compiler_params=pltpu.CompilerParams(dimension_semantics=("parallel", "arbitrary")),
    )(pos, *_in_hbm(g4, recv))


class _ReduceScatter:
    def __init__(self, tag, grads_t, pos):
        self.tag, self.pos, self.names = tag, pos, list(grads_t)
        g4s = [g.reshape(4, 2, g.size // (N_DEV * g.shape[-1]), g.shape[-1]) for g in grads_t.values()]
        lands = [lax.empty((4,) + g.shape[2:], F32) for g in g4s]
        self.ex = _exchange_start(f"rs_{tag}_sibling_start", _scatter_sibling_copies, g4s, lands, 4 * len(g4s))
        self.token = self.ex.token

    def start_chips(self, after):
        g4s, from_sibling = _exchange_wait(f"rs_{self.tag}_sibling_wait", self.ex, after)
        parts = [_add_sibling(g4, rv, self.pos, f"rs_add_sibling_{k}")
                 for k, g4, rv in zip(self.names, g4s, from_sibling)]
        self.mine = [mine for _, mine in parts]
        p16s = [p16 for p16, _ in parts]
        lands = [lax.empty((3,) + p.shape[1:], BF16) for p in p16s]
        self.ex = _exchange_start(f"rs_{self.tag}_chips_start", _scatter_chips_copies, p16s, lands, 3 * len(p16s))
        self.token = self.ex.token

    def finish(self, after):
        _, from_chips = _exchange_wait(f"rs_{self.tag}_chips_wait", self.ex, after)
        return dict(zip(self.names, zip(self.mine, from_chips)))


def _rope_tables():
    positions = np.arange(SEQ, dtype=np.float32)
    inv_freq = np.power(np.float32(ROPE_THETA), -np.arange(0, ROPE_DIM, 2, dtype=np.float32) / np.float32(ROPE_DIM))
    ang = (positions[:, None] * inv_freq[None, :]).astype(np.float32)
    cos, sin = np.cos(ang).astype(np.float32), np.sin(ang).astype(np.float32)
    ones = np.ones((SEQ, HEAD_DIM - ROPE_DIM), np.float32)
    zeros8 = np.zeros((SEQ, ROPE_HALF), np.float32)
    zeros = np.zeros((SEQ, HEAD_DIM - ROPE_DIM), np.float32)
    c_head = np.concatenate([cos, cos, ones], axis=1)
    s1_head = np.concatenate([-sin, zeros8, zeros], axis=1)
    s2_head = np.concatenate([zeros8, sin, zeros], axis=1)
    return tuple(jnp.asarray(np.concatenate([t, t], axis=1)) for t in (c_head, s1_head, s2_head))


def _rope_apply(x, c, s1, s2):
    w = x.shape[1]
    return x * c + pltpu.roll(x, w - ROPE_HALF, 1) * s1 + pltpu.roll(x, ROPE_HALF, 1) * s2


def _rope_apply_t(dy, c, s1, s2):
    w = dy.shape[1]
    return dy * c + pltpu.roll(dy * s1, ROPE_HALF, 1) + pltpu.roll(dy * s2, w - ROPE_HALF, 1)


def _dil_prev_limit(has_prev):
    return jnp.where(has_prev, 0, BLOCK)


def _dil_valid(limit):
    row = lax.broadcasted_iota(jnp.int32, (BLOCK, 2 * BLOCK), 0)
    col = lax.broadcasted_iota(jnp.int32, (BLOCK, 2 * BLOCK), 1)
    dist = col - row
    return jnp.logical_and(dist >= jnp.where(col < BLOCK, limit, -BLOCK), dist <= BLOCK)


def _upper_half():
    return lax.broadcasted_iota(jnp.int32, (1, LANES), 1) >= HEAD_DIM


def _stack_heads(x):
    upper = _upper_half()
    return jnp.concatenate([jnp.where(upper, 0, x), jnp.where(upper, x, 0)], axis=0)


def _unstack_heads(y):
    n = y.shape[0] // 2
    return jnp.where(_upper_half(), y[n:], y[:n])


def _head_columns(t):
    return jnp.concatenate([t[:, 0:1], t[:, HEAD_DIM:HEAD_DIM + 1]], axis=0)


def _dil_rows(n, d):
    per = N_BLOCKS // d
    r, lb = n // per, n % per

    def rows(b):
        start = b * (BLOCK * d) + r
        return pl.ds(pl.multiple_of(start, BLOCK), BLOCK) if d == 1 else pl.ds(start, BLOCK, stride=d)

    return rows(lb), rows(jnp.maximum(lb - 1, 0)), lb > 0


def _dil_rotate(q_ref, k_ref, c_ref, s1_ref, s2_ref, q_rot, k_rot):
    tabs = (c_ref[...], s1_ref[...], s2_ref[...])
    q_rot[...] = _rope_apply(q_ref[...], *tabs) * QK_SCALE
    k_rot[...] = _rope_apply(k_ref[...], *tabs)


def _dil_specs(g):
    def col(base):
        return pl.BlockSpec((SEQ, LANES), lambda p: (0, base // LANES + 2 * g + p))

    table = pl.BlockSpec((SEQ, LANES), lambda p: (0, 0))
    return [col(COL_QA), col(COL_KA), col(COL_VA)], [table] * 3


def _store_columns(blocks, dproj_ref, cols, sem):
    copies = [pltpu.make_async_copy(b, dproj_ref.at[:, pl.ds(pl.multiple_of(c * LANES, LANES), LANES)], sem.at[i])
              for i, (b, c) in enumerate(zip(blocks, cols))]
    for cp in copies:
        cp.start()
    for cp in copies:
        cp.wait()


def _dil_fwd(g, proj, tables):
    d = DILATIONS[g]
    one_block = d == N_BLOCKS

    def body(q_ref, k_ref, v_ref, c_ref, s1_ref, s2_ref, o_ref, lse_ref, q_rot, k_rot):
        upper = _upper_half()
        _dil_rotate(q_ref, k_ref, c_ref, s1_ref, s2_ref, q_rot, k_rot)

        def block(n, carry):
            rows, prev, has_prev = _dil_rows(n, d)
            qb = q_rot[rows, :].astype(BF16)
            kw, vw = k_rot[rows, :].astype(BF16), v_ref[rows, :].astype(BF16)
            if one_block:
                row = lax.broadcasted_iota(jnp.int32, (BLOCK, BLOCK), 0)
                valid = lax.broadcasted_iota(jnp.int32, (BLOCK, BLOCK), 1) <= row
            else:
                kw = jnp.concatenate([k_rot[prev, :].astype(BF16), kw], axis=0)
                vw = jnp.concatenate([v_ref[prev, :].astype(BF16), vw], axis=0)
                valid = _dil_valid(_dil_prev_limit(has_prev))
            s = jnp.where(jnp.concatenate([valid, valid], axis=0), _dot_nt(_stack_heads(qb), kw), NEG_INF)
            m = jnp.max(s, axis=-1, keepdims=True)
            p = jnp.exp(s - m)
            den = jnp.sum(p, axis=-1, keepdims=True)
            o_ref[rows, :] = _unstack_heads(_dot_nn((p * (1.0 / den)).astype(BF16), vw))
            lse = m + jnp.log(den)
            lse_ref[rows, :] = jnp.where(upper, lse[BLOCK:], lse[:BLOCK])
            return carry

        lax.fori_loop(0, N_BLOCKS, block, 0, unroll=4)

    qkv, tabs = _dil_specs(g)
    out = pl.BlockSpec((SEQ, LANES), lambda p: (0, p))
    return pl.pallas_call(
        body, name=f"dil_attn_fwd_{g}", grid=(2,), in_specs=qkv + tabs, out_specs=[out, out],
        out_shape=[jax.ShapeDtypeStruct((SEQ, DIL_OUT_WIDTH), F32)] * 2,
        scratch_shapes=[pltpu.VMEM((SEQ, LANES), F32)] * 2,
        compiler_params=pltpu.CompilerParams(dimension_semantics=("parallel",)),
    )(*_in_hbm(proj, proj, proj, *tables))


def _dil_bwd(g, proj, tables, do, lse, c, dproj, deps=()):
    d = DILATIONS[g]
    one_block = d == N_BLOCKS

    def body(q_ref, k_ref, v_ref, c_ref, s1_ref, s2_ref, do_ref, lse_ref, cc_ref, dproj_in, *rest):
        dproj_ref, dq_acc, dk_acc, dv_acc, dq_out, dk_out, dv_out, q_rot, k_rot, sem = rest[len(deps):]
        dk_acc[...] = jnp.zeros_like(dk_acc)
        dv_acc[...] = jnp.zeros_like(dv_acc)
        _dil_rotate(q_ref, k_ref, c_ref, s1_ref, s2_ref, q_rot, k_rot)

        def block(n, carry):
            rows, prev, has_prev = _dil_rows(n, d)
            qb = q_rot[rows, :].astype(BF16)
            dob = do_ref[rows, :].astype(BF16)
            kw, vw = k_rot[rows, :].astype(BF16), v_ref[rows, :].astype(BF16)
            if one_block:
                row = lax.broadcasted_iota(jnp.int32, (BLOCK, BLOCK), 0)
                valid = lax.broadcasted_iota(jnp.int32, (BLOCK, BLOCK), 1) <= row
            else:
                kw = jnp.concatenate([k_rot[prev, :].astype(BF16), kw], axis=0)
                vw = jnp.concatenate([v_ref[prev, :].astype(BF16), vw], axis=0)
                valid = _dil_valid(_dil_prev_limit(has_prev))
            q2, do2 = _stack_heads(qb), _stack_heads(dob)
            lse_col, c_col = _head_columns(lse_ref[rows, :]), _head_columns(cc_ref[rows, :])
            p = jnp.where(jnp.concatenate([valid, valid], axis=0), jnp.exp(_dot_nt(q2, kw) - lse_col), 0.0)
            ds = (p * (_dot_nt(do2, vw) + c_col)).astype(BF16)
            dk, dv = _dot_tn(ds, q2), _dot_tn(p.astype(BF16), do2)
            dq_acc[rows, :] = _unstack_heads(_dot_nn(ds, kw)) * QK_SCALE
            if one_block:
                dk_acc[rows, :] += dk
                dv_acc[rows, :] += dv
            else:
                dk_acc[prev, :] += dk[:BLOCK]
                dv_acc[prev, :] += dv[:BLOCK]
                dk_acc[rows, :] += dk[BLOCK:]
                dv_acc[rows, :] += dv[BLOCK:]
            return carry

        lax.fori_loop(0, N_BLOCKS, block, 0, unroll=4)
        tabs = (c_ref[...], s1_ref[...], s2_ref[...])
        dq_out[...] = _rope_apply_t(dq_acc[...], *tabs).astype(BF16)
        dk_out[...] = _rope_apply_t(dk_acc[...], *tabs).astype(BF16)
        dv_out[...] = dv_acc[...].astype(BF16)
        pair = 2 * g + pl.program_id(0)
        _store_columns((dq_out, dk_out, dv_out), dproj_ref,
                       [base // LANES + pair for base in (COL_QA, COL_KA, COL_VA)], sem)

    qkv, tabs = _dil_specs(g)
    tok = pl.BlockSpec((SEQ, LANES), lambda p: (0, p))
    return pl.pallas_call(
        body, name=f"dil_attn_bwd_{g}", grid=(2,),
        in_specs=qkv + tabs + [tok, tok, tok, _ANY] + [_ANY] * len(deps), out_specs=_ANY,
        out_shape=jax.ShapeDtypeStruct(dproj.shape, dproj.dtype),
        scratch_shapes=[pltpu.VMEM((SEQ, LANES), F32)] * 3 + [pltpu.VMEM((SEQ, LANES), BF16)] * 3
        + [pltpu.VMEM((SEQ, LANES), F32)] * 2 + [pltpu.SemaphoreType.DMA((3,))],
        input_output_aliases={9: 0},
        compiler_params=pltpu.CompilerParams(dimension_semantics=("arbitrary",)),
    )(*_in_hbm(proj, proj, proj, *tables, do, lse, c, dproj), *deps)


def _group_weights(l0, l1, l2):
    m = jnp.maximum(jnp.maximum(l0, l1), l2)
    e0, e1, e2 = jnp.exp(l0 - m), jnp.exp(l1 - m), jnp.exp(l2 - m)
    tot = e0 + e1 + e2
    return e0 / tot, e1 / tot, e2 / tot


def _dil_combine(outs, lses, deps=()):
    def fn(o0, o1, o2, l0, l1, l2):
        w0, w1, w2 = _group_weights(l0, l1, l2)
        return w0 * o0 + w1 * o1 + w2 * o2

    w = DIL_OUT_WIDTH
    return _rowwise(fn, "dil_combine", SEQ, 512, [(a, w, 0) for a in list(outs) + list(lses)], [], [(w, F32)],
                    deps=deps)[0]


def _dil_combine_bwd(d_out, outs, lses, deps=()):
    w = DIL_OUT_WIDTH

    def fn(d, o0, o1, o2, l0, l1, l2):
        row = lax.broadcasted_iota(jnp.int32, (w, w), 0) // HEAD_DIM
        col = lax.broadcasted_iota(jnp.int32, (w, w), 1) // HEAD_DIM
        same_head = jnp.where(row == col, 1.0, 0.0).astype(BF16)
        ws = _group_weights(l0, l1, l2)
        dws = [_dot3_nn(d * og, same_head) for og in (o0, o1, o2)]
        mean = ws[0] * dws[0] + ws[1] * dws[1] + ws[2] * dws[2]
        return tuple(wg * d for wg in ws) + tuple(-wg * mean for wg in ws)

    res = _rowwise(fn, "dil_combine_bwd", SEQ, 256, [(a, w, 0) for a in [d_out] + list(outs) + list(lses)], [],
                   [(w, F32)] * 6, deps=deps)
    return res[:3], res[3:]


def _log1p(e):
    u = 1.0 + e
    return jnp.where(u == 1.0, e, jnp.log(u) * (e / (u - 1.0)))


def _fox_gate(proj, b_pad, deps=()):
    def body(f_ref, b_ref, *rest):
        o_ref = rest[-1]
        z = f_ref[...] + b_ref[...]
        logf = (jnp.minimum(z, 0.0) - _log1p(jnp.exp(-jnp.abs(z)))).T[:F_ROWS]
        row = lax.broadcasted_iota(jnp.int32, (BLOCK, BLOCK), 0)
        col = lax.broadcasted_iota(jnp.int32, (BLOCK, BLOCK), 1)
        before = jnp.where(row <= col, 1.0, 0.0).astype(BF16)
        carry = jnp.zeros((F_ROWS, 1), F32)
        for blk in range(N_BLOCKS):
            run = _dot3_nn(logf[:, blk * BLOCK:(blk + 1) * BLOCK], before) + carry
            o_ref[:, blk * BLOCK:(blk + 1) * BLOCK] = run
            carry = run[:, BLOCK - 1:BLOCK]

    return pl.pallas_call(
        body, name="fox_gate", grid=(1,),
        in_specs=[pl.BlockSpec((SEQ, LANES), lambda i: (0, COL_F // LANES)), pl.BlockSpec((1, LANES), lambda i: (0, 0))]
        + [_ANY] * len(deps),
        out_specs=pl.BlockSpec((F_ROWS, SEQ), lambda i: (0, 0)),
        out_shape=jax.ShapeDtypeStruct((F_ROWS, SEQ), F32),
    )(*_in_hbm(proj, b_pad), *deps)


def _fox_gate_bwd(d_cum, proj, b_pad, dproj):
    def body(d_ref, f_ref, b_ref, dproj_ref, dz_ref, db_ref):
        row = lax.broadcasted_iota(jnp.int32, (BLOCK, BLOCK), 0)
        col = lax.broadcasted_iota(jnp.int32, (BLOCK, BLOCK), 1)
        after = jnp.where(row >= col, 1.0, 0.0).astype(BF16)
        carry = jnp.zeros((F_ROWS, 1), F32)
        parts = [None] * N_BLOCKS
        for blk in reversed(range(N_BLOCKS)):
            run = _dot3_nn(d_ref[:, blk * BLOCK:(blk + 1) * BLOCK], after) + carry
            parts[blk] = run
            carry = run[:, 0:1]
        dlogf = jnp.concatenate(parts, axis=1)
        dlogf = jnp.concatenate([dlogf, jnp.zeros((LANES - F_ROWS, SEQ), F32)], axis=0).T
        dz = dlogf * _sigmoid(-(f_ref[...] + b_ref[...]))
        dz_ref[...] = dz.astype(BF16)
        db_ref[...] = jnp.sum(dz, axis=0, keepdims=True)

    f_cols = pl.BlockSpec((SEQ, LANES), lambda i: (0, COL_F // LANES))
    return pl.pallas_call(
        body, name="fox_gate_bwd", grid=(1,),
        in_specs=[pl.BlockSpec((F_ROWS, SEQ), lambda i: (0, 0)), f_cols, pl.BlockSpec((1, LANES), lambda i: (0, 0)), _ANY],
        out_specs=[f_cols, pl.BlockSpec((1, LANES), lambda i: (0, 0))],
        out_shape=[jax.ShapeDtypeStruct(dproj.shape, dproj.dtype), jax.ShapeDtypeStruct((1, LANES), F32)],
        input_output_aliases={3: 0},
    )(*_in_hbm(d_cum, proj, b_pad, dproj))


FOX_TILE = 256
FOX_TILES = SEQ // FOX_TILE


def _row_to_col(row):
    n = row.shape[1]
    eye = lax.broadcasted_iota(jnp.int32, (n, n), 0) == lax.broadcasted_iota(jnp.int32, (n, n), 1)
    return jnp.sum(jnp.where(eye, row, 0.0), axis=1, keepdims=True)


def _fox_bias(f_row, i):
    t = FOX_TILE
    ext = (i + 1) * t
    bias = _row_to_col(f_row[:, i * t:(i + 1) * t]) - f_row[:, :ext]
    row = lax.broadcasted_iota(jnp.int32, (t, ext), 0) + i * t
    col = lax.broadcasted_iota(jnp.int32, (t, ext), 1)
    return bias, col <= row


def _fox_specs():
    qkv = [pl.BlockSpec((SEQ, LANES), lambda p, base=base: (0, base // LANES + p)) for base in (COL_QB, COL_KB, COL_VB)]
    return qkv, pl.BlockSpec((F_ROWS, SEQ), lambda p: (0, 0))


def _fox_fwd(proj, f_rows):
    t = FOX_TILE

    def body(q_ref, k_ref, v_ref, f_ref, o_ref, lse_ref):
        pair = pl.program_id(0)
        upper = _upper_half()
        k16, v16 = k_ref[...].astype(BF16), v_ref[...].astype(BF16)
        f_row = [f_ref[pl.ds(2 * pair + e, 1), :] for e in range(2)]
        for i in range(FOX_TILES):
            ext = (i + 1) * t
            q_tile = (q_ref[i * t:(i + 1) * t, :] * QK_SCALE).astype(BF16)
            s2 = _dot_nt(_stack_heads(q_tile), k16[:ext])
            pns, lses = [], []
            for e in range(2):
                bias, causal = _fox_bias(f_row[e], i)
                s = jnp.where(causal, s2[e * t:(e + 1) * t] + bias, NEG_INF)
                m = jnp.max(s, axis=-1, keepdims=True)
                p = jnp.exp(s - m)
                den = jnp.sum(p, axis=-1, keepdims=True)
                pns.append((p * (1.0 / den)).astype(BF16))
                lses.append(m + jnp.log(den))
            o_ref[i * t:(i + 1) * t, :] = _unstack_heads(_dot_nn(jnp.concatenate(pns, axis=0), v16[:ext]))
            lse_ref[i * t:(i + 1) * t, :] = jnp.where(upper, lses[1], lses[0])

    qkv, f_spec = _fox_specs()
    tok = pl.BlockSpec((SEQ, LANES), lambda p: (0, p))
    return pl.pallas_call(
        body, name="fox_attn_fwd", grid=(FOX_WIDTH // LANES,),
        in_specs=qkv + [f_spec], out_specs=[tok, tok],
        out_shape=[jax.ShapeDtypeStruct((SEQ, FOX_WIDTH), F32)] * 2,
        compiler_params=pltpu.CompilerParams(
            dimension_semantics=("parallel",), vmem_limit_bytes=_vmem_limit(8 * t * SEQ * 4)),
    )(*_in_hbm(proj, proj, proj, f_rows))


def _fox_bwd(proj, do, lse, f_rows, dproj):
    t = FOX_TILE

    def body(q_ref, k_ref, v_ref, f_ref, do_ref, lse_ref, dproj_in, dproj_ref, df_ref, dk_acc, dv_acc,
             dq_out, dk_out, dv_out, sem):
        pair = pl.program_id(0)
        upper = _upper_half()
        k16, v16 = k_ref[...].astype(BF16), v_ref[...].astype(BF16)
        f_row = [f_ref[pl.ds(2 * pair + e, 1), :] for e in range(2)]
        dk_acc[...] = jnp.zeros_like(dk_acc)
        dv_acc[...] = jnp.zeros_like(dv_acc)
        df_ref[...] = jnp.zeros_like(df_ref)
        for i in range(FOX_TILES):
            ext = (i + 1) * t
            q_tile = (q_ref[i * t:(i + 1) * t, :] * QK_SCALE).astype(BF16)
            do_tile = do_ref[i * t:(i + 1) * t, :]
            lse_t = lse_ref[i * t:(i + 1) * t, :]
            q2, do2 = _stack_heads(q_tile), _stack_heads(do_tile)
            s2, dp2 = _dot_nt(q2, k16[:ext]), _dot_nt(do2, v16[:ext])
            ps, dss = [], []
            for e in range(2):
                bias, causal = _fox_bias(f_row[e], i)
                s = s2[e * t:(e + 1) * t] + bias
                p = jnp.where(causal, jnp.exp(s - lse_t[:, e * HEAD_DIM:e * HEAD_DIM + 1]), 0.0)
                dp = dp2[e * t:(e + 1) * t]
                ds = p * (dp - jnp.sum(p * dp, axis=-1, keepdims=True))
                df_ref[0, e:e + 1, :ext] -= jnp.sum(ds, axis=0, keepdims=True)
                ps.append(p.astype(BF16))
                dss.append(ds.astype(BF16))
            ds2, p2 = jnp.concatenate(dss, axis=0), jnp.concatenate(ps, axis=0)
            dq_out[i * t:(i + 1) * t, :] = (_unstack_heads(_dot_nn(ds2, k16[:ext])) * QK_SCALE).astype(BF16)
            dk_acc[:ext, :] += _dot_tn(ds2, q2)
            dv_acc[:ext, :] += _dot_tn(p2, do2)
        dk_out[...] = dk_acc[...].astype(BF16)
        dv_out[...] = dv_acc[...].astype(BF16)
        _store_columns((dq_out, dk_out, dv_out), dproj_ref, [base // LANES + pair for base in (COL_QB, COL_KB, COL_VB)],
                       sem)

    qkv, f_spec = _fox_specs()
    tok = pl.BlockSpec((SEQ, LANES), lambda p: (0, p))
    return pl.pallas_call(
        body, name="fox_attn_bwd", grid=(FOX_WIDTH // LANES,),
        in_specs=qkv + [f_spec, tok, tok, _ANY],
        out_specs=[_ANY, pl.BlockSpec((1, SUBLANES, SEQ), lambda p: (p, 0, 0))],
        out_shape=[jax.ShapeDtypeStruct(dproj.shape, dproj.dtype),
                   jax.ShapeDtypeStruct((FOX_WIDTH // LANES, SUBLANES, SEQ), F32)],
        scratch_shapes=[pltpu.VMEM((SEQ, LANES), F32)] * 2 + [pltpu.VMEM((SEQ, LANES), BF16)] * 3
        + [pltpu.SemaphoreType.DMA((3,))],
        input_output_aliases={6: 0},
        compiler_params=pltpu.CompilerParams(
            dimension_semantics=("arbitrary",), vmem_limit_bytes=_vmem_limit(10 * t * SEQ * 4)),
    )(*_in_hbm(proj, proj, proj, f_rows, do, lse, dproj))


MIX_TILE = 256


def _mix_out(out_a, out_b, proj, x, wt_pa, wt_pb, w_out, g_post, g_ffn_pre):
    tm = MIX_TILE

    def body(a_ref, b_ref, ga_ref, gb_ref, x_ref, wpa_ref, wpb_ref, wo_ref, g2_ref, g3_ref,
             merged_ref, mix_ref, x1_ref, h2_ref):
        ya = _dot_nn(a_ref[...].astype(BF16), wpa_ref[...])
        yb = _dot_nn(b_ref[...].astype(BF16), wpb_ref[...])
        merged = (_sigmoid(ga_ref[...]) * ya + _sigmoid(gb_ref[...]) * yb).astype(BF16)
        merged_ref[...] = merged
        mix = _dot_nn(merged, wo_ref[...])
        mix_ref[...] = mix
        x1 = x_ref[...] + mix * _rms_scale(mix) * g2_ref[...]
        x1_ref[...] = x1
        h2_ref[...] = (x1 * _rms_scale(x1) * g3_ref[...]).astype(BF16)

    def rows(w, cb=0):
        return pl.BlockSpec((tm, w), lambda i, cb=cb: (i, cb))

    def whole(a):
        return pl.BlockSpec(a.shape, lambda i: (0, 0))

    d = D_MODEL
    blk = _nbytes((tm, d), F32) * 6 + sum(_nbytes(a.shape, BF16) for a in (wt_pa, wt_pb, w_out))
    return pl.pallas_call(
        body, name="mix_out", grid=(SEQ // tm,),
        in_specs=[rows(DIL_OUT_WIDTH), rows(FOX_WIDTH), rows(d, COL_GA // d), rows(d, COL_GB // d), rows(d),
                  whole(wt_pa), whole(wt_pb), whole(w_out), whole(g_post), whole(g_ffn_pre)],
        out_specs=[rows(d)] * 4,
        out_shape=[jax.ShapeDtypeStruct((SEQ, d), dt) for dt in (BF16, F32, F32, BF16)],
        compiler_params=pltpu.CompilerParams(dimension_semantics=("parallel",), vmem_limit_bytes=_vmem_limit(blk)),
    )(out_a, out_b, proj, proj, x, wt_pa, wt_pb, w_out, g_post, g_ffn_pre)


def _mix_out_bwd(dmix, out_a, out_b, proj, wt_pa, wt_pb, w_out, deps=()):
    tm = MIX_TILE

    def body(dm_ref, a_ref, b_ref, ga_ref, gb_ref, wpa_ref, wpb_ref, wo_ref, *rest):
        dproj_ref, dya_ref, dyb_ref, da_ref, db_ref = rest[len(deps):]
        dmerged = _dot_nt(dm_ref[...], wo_ref[...])
        ya = _dot_nn(a_ref[...].astype(BF16), wpa_ref[...])
        yb = _dot_nn(b_ref[...].astype(BF16), wpb_ref[...])
        sa, sb = _sigmoid(ga_ref[...]), _sigmoid(gb_ref[...])
        dproj_ref[:, COL_GA:COL_GA + D_MODEL] = (dmerged * ya * (sa * (1.0 - sa))).astype(BF16)
        dproj_ref[:, COL_GB:COL_GB + D_MODEL] = (dmerged * yb * (sb * (1.0 - sb))).astype(BF16)
        dproj_ref[:, COL_GB + D_MODEL:] = jnp.zeros((tm, COL_QA - COL_GB - D_MODEL), BF16)
        dya = (dmerged * sa).astype(BF16)
        dyb = (dmerged * sb).astype(BF16)
        dya_ref[...] = dya
        dyb_ref[...] = dyb
        da_ref[...] = _dot_nt(dya, wpa_ref[...])
        db_ref[...] = _dot_nt(dyb, wpb_ref[...]).astype(BF16)

    def rows(w, cb=0):
        return pl.BlockSpec((tm, w), lambda i, cb=cb: (i, cb))

    def whole(a):
        return pl.BlockSpec(a.shape, lambda i: (0, 0))

    d = D_MODEL
    blk = _nbytes((tm, d), F32) * 8 + sum(_nbytes(a.shape, BF16) for a in (wt_pa, wt_pb, w_out))
    return pl.pallas_call(
        body, name="mix_out_bwd", grid=(SEQ // tm,),
        in_specs=[rows(d), rows(DIL_OUT_WIDTH), rows(FOX_WIDTH), rows(d, COL_GA // d), rows(d, COL_GB // d),
                  whole(wt_pa), whole(wt_pb), whole(w_out)] + [_ANY] * len(deps),
        out_specs=[rows(COL_QA)] + [rows(d)] * 2 + [rows(DIL_OUT_WIDTH), rows(FOX_WIDTH)],
        out_shape=[jax.ShapeDtypeStruct((SEQ, PROJ_COLS), BF16)] + [jax.ShapeDtypeStruct((SEQ, d), BF16)] * 2
        + [jax.ShapeDtypeStruct((SEQ, DIL_OUT_WIDTH), F32), jax.ShapeDtypeStruct((SEQ, FOX_WIDTH), BF16)],
        compiler_params=pltpu.CompilerParams(dimension_semantics=("parallel",), vmem_limit_bytes=_vmem_limit(blk)),
    )(dmix, out_a, out_b, proj, proj, wt_pa, wt_pb, w_out, *deps)


FFN_TM, FFN_TN = 2048, 256


def _ffn_up(h2, wt_gate, wt_up):
    tm, tn = FFN_TM, FFN_TN

    def body(h_ref, wg_ref, wu_ref, gate_ref, up_ref, act_ref):
        for rows in (slice(0, tm // 2), slice(tm // 2, tm)):
            gate = _dot_nt(h_ref[rows, :], wg_ref[...])
            up = _dot_nt(h_ref[rows, :], wu_ref[...])
            gate_ref[rows, :] = gate
            up_ref[rows, :] = up
            act_ref[rows, :] = (gate * _sigmoid(gate) * up).astype(BF16)

    tile = pl.BlockSpec((tm, tn), lambda i, j: (i, j))
    w_spec = pl.BlockSpec((tn, D_MODEL), lambda i, j: (j, 0))
    return pl.pallas_call(
        body, name="ffn_up", grid=(SEQ // tm, D_FF // tn),
        in_specs=[pl.BlockSpec((tm, D_MODEL), lambda i, j: (i, 0)), w_spec, w_spec],
        out_specs=[tile, tile, tile],
        out_shape=[jax.ShapeDtypeStruct((SEQ, D_FF), dt) for dt in (F32, F32, BF16)],
        compiler_params=pltpu.CompilerParams(
            dimension_semantics=("parallel", "parallel"), vmem_limit_bytes=_vmem_limit(8 * 2**20)),
    )(h2, wt_gate, wt_up)


def _ffn_act_bwd(dff, w_down, gate, up):
    tm, tn = FFN_TM, FFN_TN

    def body(d_ref, wd_ref, gate_ref, up_ref, dgate_ref, dup_ref):
        for rows in (slice(0, tm // 2), slice(tm // 2, tm)):
            dact = _dot_nt(d_ref[rows, :], wd_ref[...])
            gate = gate_ref[rows, :]
            sg = _sigmoid(gate)
            dgate_ref[rows, :] = (dact * up_ref[rows, :] * (sg * (1.0 + gate * (1.0 - sg)))).astype(BF16)
            dup_ref[rows, :] = (dact * (gate * sg)).astype(BF16)

    tile = pl.BlockSpec((tm, tn), lambda i, j: (i, j))
    return pl.pallas_call(
        body, name="ffn_act_bwd", grid=(SEQ // tm, D_FF // tn),
        in_specs=[pl.BlockSpec((tm, D_MODEL), lambda i, j: (i, 0)), pl.BlockSpec((tn, D_MODEL), lambda i, j: (j, 0)),
                  tile, tile],
        out_specs=[tile, tile],
        out_shape=[jax.ShapeDtypeStruct((SEQ, D_FF), BF16)] * 2,
        compiler_params=pltpu.CompilerParams(
            dimension_semantics=("parallel", "parallel"), vmem_limit_bytes=_vmem_limit(8 * 2**20)),
    )(dff, w_down, gate, up)


EPILOGUE_TM = 512


def _loss_head(act, w_down, x1, target, g_post):
    def fn(ff, x1, tgt, g):
        r = _rms_scale(ff)
        nrm = ff * r
        err = (x1 + nrm * g) - tgt
        loss = 0.5 * jnp.sum(jnp.mean(err * err, axis=-1, keepdims=True), axis=0, keepdims=True)
        dy = err * (1.0 / D_MODEL)
        u = dy * g
        dff = r * u - ff * (r * r * r) * jnp.mean(u * ff, axis=-1, keepdims=True)
        return dy, dff, jnp.broadcast_to(loss, (1, LANES)), jnp.sum(dy * nrm, axis=0, keepdims=True)

    d = D_MODEL
    return _matmul_rowwise([(act, w_down)], fn, "ffn_down_loss", EPILOGUE_TM, [(x1, d, 0), (target, d, 0)], [g_post],
                           [(d, F32), (d, BF16)], [LANES, d])


def _post_ffn_bwd(dgate, wt_gate, dup, wt_up, x1, dy, mix, g_ffn_pre, g_mix_post, deps=()):
    def fn(dh2, x1, dy, mix, g3, g2):
        dx, dg3 = _rms_bwd(x1, dh2, g3)
        dx1 = dy + dx
        dmix, dg2 = _rms_bwd(mix, dx1, g2)
        return dx1, dmix, dg3, dg2

    d = D_MODEL
    return _matmul_rowwise([(dgate, wt_gate), (dup, wt_up)], fn, "ffn_up_bwd", EPILOGUE_TM,
                           [(x1, d, 0), (dy, d, 0), (mix, d, 0)], [g_ffn_pre, g_mix_post],
                           [(d, F32), (d, BF16)], [d, d], deps=deps)


def _input_bwd(dproj, wt_r, x, dx1, g_pre, deps=()):
    def fn(dh, x, dx1, g):
        dx, dg = _rms_bwd(x, dh, g)
        return dx1 + dx, dg

    d = D_MODEL
    return _matmul_rowwise([(dproj, wt_r)], fn, "in_proj_bwd", EPILOGUE_TM, [(x, d, 0), (dx1, d, 0)], [g_pre],
                           [(d, F32)], [d], deps=deps)


def _adam_math(w, g, m, v):
    m = ADAM_B1 * m + (1.0 - ADAM_B1) * g
    v = ADAM_B2 * v + (1.0 - ADAM_B2) * (g * g)
    m_hat = m / (1.0 - ADAM_B1 ** ADAM_STEP)
    v_hat = v / (1.0 - ADAM_B2 ** ADAM_STEP)
    delta = -ADAM_LR * (m_hat / (jnp.sqrt(v_hat) + ADAM_EPS) + ADAM_WD * w)
    return delta, m, v


def _adam(w, mine, recv, m, v, name):
    r, c = w.shape
    tc = _col_tile(r, c)

    def body(w_ref, p_ref, r_ref, m_ref, v_ref, g_ref, d_ref, nm_ref, nv_ref):
        g = ((p_ref[...] + r_ref[0].astype(F32)) + r_ref[1].astype(F32)) + r_ref[2].astype(F32)
        g_ref[...] = g
        d_ref[...], nm_ref[...], nv_ref[...] = _adam_math(w_ref[...], g, m_ref[...], v_ref[...])

    spec = pl.BlockSpec((r, tc), lambda j: (0, j))
    return pl.pallas_call(
        body, name=name, grid=(c // tc,),
        in_specs=[spec, spec, pl.BlockSpec((3, r, tc), lambda j: (0, 0, j)), spec, spec], out_specs=[spec] * 4,
        out_shape=[jax.ShapeDtypeStruct((r, c), F32)] * 4,
        compiler_params=pltpu.CompilerParams(dimension_semantics=("parallel",)),
    )(*_in_hbm(w, mine, recv, m, v))


def _adam_small(gathered, ws, ms, vs, loss_parts):
    n = len(ws)

    def body(*refs):
        outs = refs[4 * n + 1:]
        loss = refs[4 * n][0]
        for dev in range(1, N_DEV):
            loss = loss + refs[4 * n][dev]
        outs[4 * n][...] = loss
        for i in range(n):
            ga_ref, w_ref, m_ref, v_ref = (refs[j * n + i] for j in range(4))
            g = ga_ref[0]
            for dev in range(1, N_DEV):
                g = g + ga_ref[dev]
            g = g[:, :w_ref.shape[1]]
            outs[4 * i][...] = g
            outs[4 * i + 1][...], outs[4 * i + 2][...], outs[4 * i + 3][...] = _adam_math(
                w_ref[...], g, m_ref[...], v_ref[...])

    out_shape = [jax.ShapeDtypeStruct(w.shape, F32) for w in ws for _ in range(4)]
    out_shape.append(jax.ShapeDtypeStruct((1, LANES), F32))
    out = pl.pallas_call(body, name="adam_small", out_shape=out_shape)(*gathered, *ws, *ms, *vs, loss_parts)
    return [out[4 * i:4 * i + 4] for i in range(n)], out[4 * n]


_PROJ_SEGMENTS = ((3848, 5896), (None, COL_QA - 2 * D_MODEL), (0, 3840), (3840, 3848), (None, PROJ_COLS - COL_F - 8))


def _proj_weight_t(gathered):
    w = gathered.reshape(IN_COLS, D_MODEL)
    return jnp.concatenate([jnp.zeros((hi, D_MODEL), w.dtype) if lo is None else w[lo:hi] for lo, hi in _PROJ_SEGMENTS],
                           axis=0)


def _proj_weight_grad_slots(dwt_r):
    starts, at = [], 0
    for lo, hi in _PROJ_SEGMENTS:
        if lo is not None:
            starts.append((lo, hi, at))
        at += hi if lo is None else hi - lo
    slots = []
    for dev in range(N_DEV):
        pieces, lo, end = [], dev * IN_SHARD, (dev + 1) * IN_SHARD
        for seg_lo, seg_hi, seg_at in sorted(starts):
            a, b = max(lo, seg_lo), min(end, seg_hi)
            if a < b:
                pieces.append(dwt_r[seg_at + a - seg_lo:seg_at + b - seg_lo])
        slots.append(pieces[0] if len(pieces) == 1 else jnp.concatenate(pieces, axis=0))
    return jnp.stack(slots)


def kernel(x, w_in, w_proj_a, w_proj_b, w_out, b_forget, w_ffn_gate, w_ffn_up, w_ffn_down, norm_mix_pre, norm_mix_post, norm_ffn_pre, norm_ffn_post, loss_target, m_w_in, m_w_proj_a, m_w_proj_b, m_w_out, m_b_forget, m_w_ffn_gate, m_w_ffn_up, m_w_ffn_down, m_norm_mix_pre, m_norm_mix_post, m_norm_ffn_pre, m_norm_ffn_post, v_w_in, v_w_proj_a, v_w_proj_b, v_w_out, v_b_forget, v_w_ffn_gate, v_w_ffn_up, v_w_ffn_down, v_norm_mix_pre, v_norm_mix_post, v_norm_ffn_pre, v_norm_ffn_post):
    d = D_MODEL
    names = ("w_in", "w_proj_a", "w_proj_b", "w_out", "w_ffn_gate", "w_ffn_up", "w_ffn_down")
    col_sharded = ("w_in", "w_ffn_gate", "w_ffn_up")

    def row_shards(arrs):
        return {k: (a[0].T if k in col_sharded else a[0]) for k, a in zip(names, arrs)}

    shards = row_shards((w_in, w_proj_a, w_proj_b, w_out, w_ffn_gate, w_ffn_up, w_ffn_down))
    moments_m = row_shards((m_w_in, m_w_proj_a, m_w_proj_b, m_w_out, m_w_ffn_gate, m_w_ffn_up, m_w_ffn_down))
    moments_v = row_shards((v_w_in, v_w_proj_a, v_w_proj_b, v_w_out, v_w_ffn_gate, v_w_ffn_up, v_w_ffn_down))
    pos = jnp.stack([lax.axis_index("c"), 2 * lax.axis_index("x") + lax.axis_index("y")]).astype(jnp.int32)
    x2, target = x[0], loss_target[0]

    me = 4 * lax.axis_index("x") + 2 * lax.axis_index("y") + lax.axis_index("c")
    mid_names, ffn_names = names[1:4], names[4:]
    first_names, later_names = names[:1], names[1:]
    shards16 = {k: shards[k].astype(BF16) for k in names}

    def landing(k):
        return lax.dynamic_update_slice(lax.empty((N_DEV,) + shards[k].shape, BF16), shards16[k][None], (me, 0, 0))

    ag_first = _exchange_start("ag_first_chips_start", _gather_chips_copies, [shards16[k] for k in first_names],
                               [landing(k) for k in first_names], 3 * len(first_names))
    h = _rowwise(lambda xb, g: xb * _rms_scale(xb) * g, "norm_mix_pre", SEQ, 256, [(x2, d, 0)], [norm_mix_pre],
                 [(d, BF16)], deps=[ag_first.token])[0]
    later_lands = [landing(k) for k in later_names]
    _, lands = _exchange_wait("ag_first_chips_wait", ag_first,
                              [h, shards["w_in"], moments_m["w_in"], moments_v["w_in"], *later_lands])
    ag_first = _exchange_start("ag_first_sibling_start", _gather_sibling_copies, [], lands, 4 * len(first_names))
    ag_later = _exchange_start("ag_later_chips_start", _gather_chips_copies, [shards16[k] for k in later_names],
                               later_lands, 3 * len(later_names), after=[ag_first.token])
    gathered = dict(zip(first_names, _exchange_wait("ag_first_sibling_wait", ag_first, [ag_later.token])[1]))
    wt_r = _proj_weight_t(gathered["w_in"])

    proj = _matmul([(h, wt_r)], "nt", F32, "in_proj", 1024, 896, 1024)
    tables = _rope_tables()
    o_dil, lse_dil = zip(*[_dil_fwd(g, proj, tables) for g in range(len(DILATIONS))])
    out_a = _dil_combine(o_dil, lse_dil)
    _, lands = _exchange_wait("ag_later_chips_wait", ag_later, [out_a])
    ag_later = _exchange_start("ag_later_sibling_start", _gather_sibling_copies, [], lands, 4 * len(later_names))

    b_pad = jnp.pad(b_forget, ((0, 0), (0, LANES - N_FOX_HEADS)))
    f_rows = _fox_gate(proj, b_pad, deps=[ag_later.token])
    out_b, lse_fox = _fox_fwd(proj, f_rows)

    gathered = dict(zip(later_names, _exchange_wait("ag_later_sibling_wait", ag_later, [out_b])[1]))
    wt_pa = gathered["w_proj_a"].transpose(1, 0, 2).reshape(DIL_OUT_WIDTH, d)
    wt_pb = gathered["w_proj_b"].transpose(1, 0, 2).reshape(FOX_WIDTH, d)
    w_o = gathered["w_out"].reshape(d, d)
    wt_g = gathered["w_ffn_gate"].reshape(D_FF, d)
    wt_u = gathered["w_ffn_up"].reshape(D_FF, d)
    w_d = gathered["w_ffn_down"].reshape(D_FF, d)
    merged, mix, x1, h2 = _mix_out(out_a, out_b, proj, x2, wt_pa, wt_pb, w_o, norm_mix_post, norm_ffn_pre)

    gate, up, act = _ffn_up(h2, wt_g, wt_u)
    dy, dff, loss_part, dg_ffn_post = _loss_head(act, w_d, x1, target, norm_ffn_post)

    dgate, dup = _ffn_act_bwd(dff, w_d, gate, up)
    grads_t = {}
    grads_t["w_ffn_down"] = _matmul([(act, dff)], "tn", F32, "grad_w_ffn_down", 1408, 512, 2048)
    grads_t["w_ffn_gate"] = _matmul([(dgate, h2)], "tn", F32, "grad_w_ffn_gate", 1408, 512, 2048)
    grads_t["w_ffn_up"] = _matmul([(dup, h2)], "tn", F32, "grad_w_ffn_up", 1408, 512, 2048)
    rs_ffn = _ReduceScatter("ffn", {k: grads_t[k] for k in ffn_names}, pos)
    dx1, dmix, dg_ffn_pre, dg_mix_post = _post_ffn_bwd(dgate, wt_g, dup, wt_u, x1, dy, mix, norm_ffn_pre, norm_mix_post,
                                                       deps=[rs_ffn.token])
    rs_ffn.start_chips([dmix])

    dproj, dya, dyb, d_out_a, d_out_b = _mix_out_bwd(dmix, out_a, out_b, proj, wt_pa, wt_pb, w_o, deps=[rs_ffn.token])
    grads_t["w_out"] = _matmul([(merged, dmix)], "tn", F32, "grad_w_out", 1024, 1024, 1024)
    def column_slots(g):
        return g.reshape(g.shape[0], N_DEV, LANES).transpose(1, 0, 2)

    grads_t["w_proj_a"] = column_slots(_matmul([(out_a, dya)], "tn", F32, "grad_w_proj_a", DIL_OUT_WIDTH, 1024, SEQ))
    grads_t["w_proj_b"] = column_slots(_matmul([(out_b, dyb)], "tn", F32, "grad_w_proj_b", FOX_WIDTH, 1024, SEQ))
    rs_mid = _ReduceScatter("mid", {k: grads_t[k] for k in mid_names}, pos)

    do_dil, c_dil = _dil_combine_bwd(d_out_a, o_dil, lse_dil, deps=[rs_mid.token])
    rs_mid.start_chips([c_dil[0]])
    dproj, d_cum = _fox_bwd(proj, d_out_b, lse_fox, f_rows, dproj)
    d_cum_rows = jnp.pad(d_cum[:, :2].reshape(N_FOX_HEADS, SEQ), ((0, F_ROWS - N_FOX_HEADS), (0, 0)))
    dproj, db_part = _fox_gate_bwd(d_cum_rows, proj, b_pad, dproj)
    for g in range(len(DILATIONS)):
        dproj = _dil_bwd(g, proj, tables, do_dil[g], lse_dil[g], c_dil[g], dproj, deps=[rs_mid.token])

    dwt_r = _matmul([(dproj, h)], "tn", F32, "grad_w_in", 896, 1024, 2048)
    rs_in = _ReduceScatter("in", {"w_in": _proj_weight_grad_slots(dwt_r)}, pos)
    def finish(rs, after):
        return {k: _adam(shards[k], mine, recv, moments_m[k], moments_v[k], "adam_" + k)
                for k, (mine, recv) in rs.finish(after).items()}

    done = finish(rs_ffn, [rs_in.token])
    rs_in.start_chips([done[k][0] for k in ffn_names])
    grad_x, dg_mix_pre = _input_bwd(dproj, wt_r, x2, dx1, norm_mix_pre, deps=[rs_in.token])
    done.update(finish(rs_mid, [grad_x]))

    small_all = _all_gather([dg_mix_pre, dg_mix_post, dg_ffn_pre, dg_ffn_post, db_part, loss_part],
                            "small_grads_all_gather", deps=[done[k][0] for k in mid_names])
    small, loss = _adam_small(small_all[:5], [norm_mix_pre, norm_mix_post, norm_ffn_pre, norm_ffn_post, b_forget],
                              [m_norm_mix_pre, m_norm_mix_post, m_norm_ffn_pre, m_norm_ffn_post, m_b_forget],
                              [v_norm_mix_pre, v_norm_mix_post, v_norm_ffn_pre, v_norm_ffn_post, v_b_forget],
                              small_all[5])

    done.update(finish(rs_in, [small[0][0]]))

    def leaves(i):
        def nat(k):
            a = done[k][i]
            return (a.T if k in col_sharded else a)[None]

        return [nat("w_in"), nat("w_proj_a"), nat("w_proj_b"), nat("w_out"), small[4][i],
                nat("w_ffn_gate"), nat("w_ffn_up"), nat("w_ffn_down"), *[small[r][i] for r in range(4)]]

    return (loss[0, 0], grad_x[None], *leaves(0), *leaves(1), *leaves(2), *leaves(3))
```

```python
import functools
import math

import jax
import jax.numpy as jnp
import numpy as np
from jax import lax
from jax.experimental import pallas as pl
from jax.experimental.pallas import tpu as pltpu

F32 = jnp.float32
BF16 = jnp.bfloat16
MESH = pl.DeviceIdType.MESH

D_MODEL = 1024
SEQ = 2048
HEAD_DIM = 64
BLOCK = 128
N_BLOCKS = SEQ // BLOCK
DILATIONS = (1, 4, 16)
N_FOX_HEADS = 8
DIL_WIDTH = 768
DIL_OUT_WIDTH = 256
FOX_WIDTH = 512
D_FF = 2816
ROPE_THETA = 500000.0
ROPE_DIM = HEAD_DIM // 4
ROPE_HALF = ROPE_DIM // 2
EPS = 1e-6
NEG_INF = -1e30
QK_SCALE = 1.0 / math.sqrt(HEAD_DIM)
IN_COLS = 5896
N_DEV = 8
IN_SHARD = IN_COLS // N_DEV

ADAM_LR = 0.001
ADAM_B1 = 0.9
ADAM_B2 = 0.999
ADAM_EPS = 1e-08
ADAM_WD = 0.01
ADAM_STEP = 10

V7X_VMEM_BYTES = 64 * 2**20
LANES = 128
SUBLANES = 8

PROJ_COLS = 6272
COL_GA, COL_GB = 0, 1024
COL_QA, COL_KA, COL_VA = 2304, 3072, 3840
COL_QB, COL_KB, COL_VB = 4608, 5120, 5632
COL_F = 6144
F_ROWS = 16


def _vmem_limit(block_bytes):
    want = 2 * block_bytes + 16 * 2**20
    return int(min(max(want, 32 * 2**20), V7X_VMEM_BYTES - 8 * 2**20))


def _nbytes(shape, dtype):
    return math.prod(shape) * jnp.dtype(dtype).itemsize


def _in_hbm(*arrays):
    return [pltpu.with_memory_space_constraint(a, pltpu.HBM) for a in arrays]


def _dot(a, b, dims):
    return lax.dot_general(a, b, (dims, ((), ())), preferred_element_type=F32)


def _dot_nn(a, b):
    return _dot(a, b, ((1,), (0,)))


def _dot_nt(a, b):
    return _dot(a, b, ((1,), (1,)))


def _dot_tn(a, b):
    return _dot(a, b, ((0,), (0,)))


def _sigmoid(z):
    return 1.0 / (1.0 + jnp.exp(-z))


def _split3(x):
    hi = x.astype(BF16)
    r1 = x - hi.astype(F32)
    mid = r1.astype(BF16)
    lo = (r1 - mid.astype(F32)).astype(BF16)
    return hi, mid, lo


def _dot3_nn(x, ones_matrix):
    hi, mid, lo = _split3(x)
    return (_dot_nn(hi, ones_matrix) + _dot_nn(mid, ones_matrix)) + _dot_nn(lo, ones_matrix)


def _rowwise(fn, name, n_rows, tm, row_ins, bcast_ins, row_outs, acc_outs=(), deps=()):
    n_in = len(row_ins) + len(bcast_ins)
    n_ro = len(row_outs)

    def body(*refs):
        res = fn(*[r[...] for r in refs[:n_in]])
        if not isinstance(res, (tuple, list)):
            res = (res,)
        outs = refs[n_in + len(deps):]
        for r, o in zip(res[:n_ro], outs[:n_ro]):
            o[...] = r.astype(o.dtype)
        first = pl.program_id(0) == 0
        for r, o in zip(res[n_ro:], outs[n_ro:]):
            _accumulate(o, r, first)

    in_specs = [pl.BlockSpec((tm, w), lambda i, cb=cb: (i, cb)) for _, w, cb in row_ins]
    in_specs += [pl.BlockSpec(a.shape, lambda i: (0, 0)) for a in bcast_ins]
    in_specs += [pl.BlockSpec(memory_space=pl.ANY)] * len(deps)
    out_specs = [pl.BlockSpec((tm, w), lambda i: (i, 0)) for w, _ in row_outs]
    out_specs += [pl.BlockSpec((1, w), lambda i: (0, 0)) for w in acc_outs]
    out_shape = [jax.ShapeDtypeStruct((n_rows, w), dt) for w, dt in row_outs]
    out_shape += [jax.ShapeDtypeStruct((1, w), F32) for w in acc_outs]
    blk = sum(_nbytes((tm, w), a.dtype) for a, w, _ in row_ins) + sum(_nbytes((tm, w), dt) for w, dt in row_outs)
    return pl.pallas_call(
        body, name=name, grid=(n_rows // tm,), in_specs=in_specs, out_specs=out_specs, out_shape=out_shape,
        compiler_params=pltpu.CompilerParams(
            dimension_semantics=("arbitrary" if acc_outs else "parallel",), vmem_limit_bytes=_vmem_limit(3 * blk)),
    )(*_in_hbm(*[a for a, _, _ in row_ins], *bcast_ins), *deps)


def _accumulate(o_ref, part, first):
    @pl.when(first)
    def _():
        o_ref[...] = part

    @pl.when(jnp.logical_not(first))
    def _():
        o_ref[...] += part


_MM_DIMS = {"nn": ((1,), (0,)), "nt": ((1,), (1,)), "tn": ((0,), (0,))}


def _matmul(pairs, mode, out_dtype, name, tm, tn, tk, deps=()):
    a0, b0 = pairs[0]
    if mode == "tn":
        kk, m = a0.shape
    else:
        m, kk = a0.shape
    n = b0.shape[0] if mode == "nt" else b0.shape[1]
    assert m % tm == 0 and n % tn == 0 and kk % tk == 0, (name, m, n, kk)
    nk = kk // tk
    n_pairs = len(pairs)
    dims = _MM_DIMS[mode]
    n_in = 2 * n_pairs + len(deps)

    def body(*refs):
        o_ref = refs[n_in]
        part = None
        for p in range(n_pairs):
            d = _dot(refs[2 * p][...].astype(BF16), refs[2 * p + 1][...].astype(BF16), dims)
            part = d if part is None else part + d
        if nk == 1:
            o_ref[...] = part.astype(o_ref.dtype)
            return
        acc = refs[n_in + 1]
        k = pl.program_id(2)

        @pl.when(k == 0)
        def _():
            acc[...] = part

        @pl.when(k > 0)
        def _():
            acc[...] += part

        @pl.when(k == nk - 1)
        def _():
            o_ref[...] = acc[...].astype(o_ref.dtype)

    if mode == "tn":
        a_spec = pl.BlockSpec((tk, tm), lambda i, j, k: (k, i))
    else:
        a_spec = pl.BlockSpec((tm, tk), lambda i, j, k: (i, k))
    if mode == "nt":
        b_spec = pl.BlockSpec((tn, tk), lambda i, j, k: (j, k))
    else:
        b_spec = pl.BlockSpec((tk, tn), lambda i, j, k: (k, j))
    blk = sum(_nbytes((tm, tk), a.dtype) + _nbytes((tk, tn), b.dtype) for a, b in pairs) + 2 * _nbytes((tm, tn), F32)
    flat = [a for pair in pairs for a in pair]
    return pl.pallas_call(
        body, name=name, grid=(m // tm, n // tn, nk),
        in_specs=[a_spec, b_spec] * n_pairs + [pl.BlockSpec(memory_space=pl.ANY)] * len(deps),
        out_specs=pl.BlockSpec((tm, tn), lambda i, j, k: (i, j)),
        out_shape=jax.ShapeDtypeStruct((m, n), out_dtype),
        scratch_shapes=[] if nk == 1 else [pltpu.VMEM((tm, tn), F32)],
        compiler_params=pltpu.CompilerParams(
            dimension_semantics=("parallel", "parallel", "arbitrary"), vmem_limit_bytes=_vmem_limit(blk)),
    )(*flat, *deps)


def _matmul_rowwise(pairs, fn, name, tm, row_ins, bcast_ins, row_outs, acc_outs=(), deps=()):
    m = pairs[0][0].shape[0]
    n_mm, n_in = 2 * len(pairs), len(row_ins) + len(bcast_ins)
    n_ro = len(row_outs)

    def body(*refs):
        prod = None
        for p in range(len(pairs)):
            part = _dot_nn(refs[2 * p][...].astype(BF16), refs[2 * p + 1][...].astype(BF16))
            prod = part if prod is None else prod + part
        res = fn(prod, *[r[...] for r in refs[n_mm:n_mm + n_in]])
        outs = refs[n_mm + n_in + len(deps):]
        for r, o in zip(res[:n_ro], outs[:n_ro]):
            o[...] = r.astype(o.dtype)
        first = pl.program_id(0) == 0
        for r, o in zip(res[n_ro:], outs[n_ro:]):
            _accumulate(o, r, first)

    in_specs = []
    for a, b in pairs:
        in_specs += [pl.BlockSpec((tm, a.shape[1]), lambda i: (i, 0)),
                     pl.BlockSpec(b.shape, lambda i: (0, 0), pipeline_mode=pl.Buffered(1))]
    in_specs += [pl.BlockSpec((tm, w), lambda i, cb=cb: (i, cb)) for _, w, cb in row_ins]
    in_specs += [pl.BlockSpec(a.shape, lambda i: (0, 0)) for a in bcast_ins]
    in_specs += [_ANY] * len(deps)
    out_specs = [pl.BlockSpec((tm, w), lambda i: (i, 0)) for w, _ in row_outs]
    out_specs += [pl.BlockSpec((1, w), lambda i: (0, 0)) for w in acc_outs]
    out_shape = [jax.ShapeDtypeStruct((m, w), dt) for w, dt in row_outs]
    out_shape += [jax.ShapeDtypeStruct((1, w), F32) for w in acc_outs]
    blk = sum(_nbytes((tm, a.shape[1]), a.dtype) + _nbytes(b.shape, b.dtype) // 2 for a, b in pairs)
    blk += sum(_nbytes((tm, w), a.dtype) for a, w, _ in row_ins) + sum(_nbytes((tm, w), dt) for w, dt in row_outs)
    return pl.pallas_call(
        body, name=name, grid=(m // tm,), in_specs=in_specs, out_specs=out_specs, out_shape=out_shape,
        compiler_params=pltpu.CompilerParams(dimension_semantics=("arbitrary",), vmem_limit_bytes=_vmem_limit(blk)),
    )(*[a for pair in pairs for a in pair], *[a for a, _, _ in row_ins], *bcast_ins, *deps)


def _rms_scale(x):
    return lax.rsqrt(jnp.mean(x * x, axis=-1, keepdims=True) + EPS)


def _rms_bwd(xin, dyn, g):
    r = _rms_scale(xin)
    u = dyn * g
    dx = r * u - xin * (r * r * r) * jnp.mean(u * xin, axis=-1, keepdims=True)
    dg = jnp.sum(dyn * xin * r, axis=0, keepdims=True)
    return dx, dg


def _mesh_pos():
    return lax.axis_index("x"), lax.axis_index("y"), lax.axis_index("c")


def _all_gather(xs, name, deps=()):
    n = len(xs)

    def body(*refs):
        x_refs, out_refs = refs[:n], refs[n + len(deps):2 * n + len(deps)]
        send_sems, recv_sems, local_sems = refs[2 * n + len(deps):]
        mx, my, mc = _mesh_pos()
        me, sib = (mx, my, mc), (mx, my, 1 - mc)
        chips = [(1 - mx, my), (mx, 1 - my), (1 - mx, 1 - my)]

        def slot(a, dev):
            px, py, pc = dev
            return out_refs[a].at[4 * px + 2 * py + pc]

        def copy(k, a, block, to, src=None):
            return pltpu.make_async_remote_copy(
                src_ref=slot(a, block) if src is None else src, dst_ref=slot(a, block),
                send_sem=send_sems.at[a * 7 + k], recv_sem=recv_sems.at[a * 7 + k],
                device_id=to, device_id_type=MESH)

        mine = [pltpu.make_async_copy(x_refs[a], slot(a, me), local_sems.at[a]) for a in range(n)]
        for cp in mine:
            cp.start()
        first = []
        for a in range(n):
            first.append(copy(0, a, me, sib, x_refs[a]))
            first += [copy(1 + j, a, me, (*chip, mc), x_refs[a]) for j, chip in enumerate(chips)]
        for cp in first:
            cp.start()
        passed = []
        for a in range(n):
            for j, chip in enumerate(chips):
                copy(1 + j, a, (*chip, mc), me).wait_recv()
                fwd = copy(4 + j, a, (*chip, mc), sib)
                fwd.start()
                passed.append(fwd)
        for a in range(n):
            copy(0, a, sib, me).wait_recv()
            for j, chip in enumerate(chips):
                copy(4 + j, a, (*chip, 1 - mc), me).wait_recv()
        for cp in first + passed:
            cp.wait_send()
        for cp in mine:
            cp.wait()

    hbm = pl.BlockSpec(memory_space=pl.ANY)
    return pl.pallas_call(
        body, name=name,
        out_shape=[jax.ShapeDtypeStruct((N_DEV,) + x.shape, x.dtype) for x in xs],
        in_specs=[hbm] * (n + len(deps)), out_specs=[hbm] * n,
        scratch_shapes=[pltpu.SemaphoreType.DMA((7 * n,)), pltpu.SemaphoreType.DMA((7 * n,)),
                        pltpu.SemaphoreType.DMA((n,))],
    )(*xs, *deps)


_HBM = pl.BlockSpec(memory_space=pltpu.HBM)
_SEM = pl.BlockSpec(memory_space=pltpu.SEMAPHORE)
_ANY = pl.BlockSpec(memory_space=pl.ANY)
_DATAFLOW = pltpu.SideEffectType.DATAFLOW_SIDE_EFFECTING


def _flip_peer(flip):
    mx, my, mc = _mesh_pos()
    return (1 - mx if flip & 2 else mx, 1 - my if flip & 1 else my, mc)


def _remote(src, dst, send_sems, recv_sems, k, peer):
    return pltpu.make_async_remote_copy(src_ref=src, dst_ref=dst, send_sem=send_sems.at[k], recv_sem=recv_sems.at[k],
                                        device_id=peer, device_id_type=MESH)


def _gather_chips_copies(srcs, lands, send_sems, recv_sems):
    mx, my, mc = _mesh_pos()
    me = 4 * mx + 2 * my + mc
    return [_remote(srcs[a], lands[a].at[me], send_sems, recv_sems, 3 * a + flip - 1, _flip_peer(flip))
            for a in range(len(srcs)) for flip in (1, 2, 3)]


def _gather_sibling_copies(srcs, lands, send_sems, recv_sems):
    mx, my, mc = _mesh_pos()
    return [_remote(lands[a].at[2 * k + mc], lands[a].at[2 * k + mc], send_sems, recv_sems, 4 * a + k, (mx, my, 1 - mc))
            for a in range(len(lands)) for k in range(4)]


def _scatter_sibling_copies(srcs, lands, send_sems, recv_sems):
    mx, my, mc = _mesh_pos()
    return [_remote(srcs[a].at[k, 1 - mc], lands[a].at[k], send_sems, recv_sems, 4 * a + k, (mx, my, 1 - mc))
            for a in range(len(srcs)) for k in range(4)]


def _scatter_chips_copies(srcs, lands, send_sems, recv_sems):
    mx, my, _ = _mesh_pos()
    k0 = 2 * mx + my
    return [_remote(srcs[a].at[jnp.bitwise_xor(k0, flip)], lands[a].at[flip - 1], send_sems, recv_sems,
                    3 * a + flip - 1, _flip_peer(flip))
            for a in range(len(srcs)) for flip in (1, 2, 3)]


class _Exchange:
    def __init__(self, copies, n_src, send_sems, recv_sems, thru, token):
        self.copies, self.n_src, self.send_sems, self.recv_sems, self.thru, self.token = (
            copies, n_src, send_sems, recv_sems, thru, token)


def _exchange_start(name, copies, srcs, lands, n_copies, after=()):
    bufs = list(srcs) + list(lands)
    nb, ns = len(bufs), len(srcs)

    def body(*refs):
        send_sems, recv_sems = refs[nb + len(after)], refs[nb + len(after) + 1]
        for cp in copies(refs[:ns], refs[ns:nb], send_sems, recv_sems):
            cp.start()
        refs[-1][...] = jnp.zeros_like(refs[-1])

    out = pl.pallas_call(
        body, name=name,
        out_shape=(pltpu.SemaphoreType.DMA((n_copies,)), pltpu.SemaphoreType.DMA((n_copies,)),
                   *[pltpu.HBM(b.shape, b.dtype) for b in bufs], jax.ShapeDtypeStruct((SUBLANES, LANES), F32)),
        in_specs=[_HBM] * nb + [_ANY] * len(after),
        out_specs=(_SEM, _SEM, *[_HBM] * nb, pl.BlockSpec(memory_space=pltpu.VMEM)),
        input_output_aliases={i: 2 + i for i in range(nb)},
        compiler_params=pltpu.CompilerParams(has_side_effects=_DATAFLOW),
    )(*[pltpu.with_memory_space_constraint(b, pltpu.HBM) for b in bufs], *after)
    return _Exchange(copies, ns, out[0], out[1], list(out[2:2 + nb]), out[-1])


def _exchange_wait(name, ex, after):
    nb, ns = len(ex.thru), ex.n_src

    def body(*refs):
        for cp in ex.copies(refs[:ns], refs[ns:nb], refs[nb], refs[nb + 1]):
            cp.wait_send()
            cp.wait_recv()

    out = pl.pallas_call(
        body, name=name, out_shape=tuple(pltpu.HBM(b.shape, b.dtype) for b in ex.thru),
        in_specs=[_HBM] * nb + [_SEM, _SEM] + [_ANY] * len(after), out_specs=tuple([_HBM] * nb),
        input_output_aliases={i: i for i in range(nb)},
        compiler_params=pltpu.CompilerParams(has_side_effects=_DATAFLOW),
    )(*ex.thru, ex.send_sems, ex.recv_sems, *after)
    return list(out[:ns]), list(out[ns:])


def _col_tile(r, c):
    return next(t for t in (1024, 512, 256, 128) if c % t == 0 and (r * t * 4 <= 2**20 or t == 128))


def _add_sibling(g4, recv, pos, name):
    _, _, r, c = g4.shape
    tc = _col_tile(r, c)

    def body(pos_ref, g_ref, r_ref, o16_ref, mine_ref):
        s = g_ref[0, 0] + r_ref[0]
        o16_ref[0] = s.astype(BF16)

        @pl.when(pl.program_id(1) == pos_ref[1])
        def _():
            mine_ref[...] = s

    slot = pl.BlockSpec((1, r, tc), lambda j, k, pos_ref: (k, 0, j))
    return pl.pallas_call(
        body, name=name,
        out_shape=[jax.ShapeDtypeStruct((4, r, c), BF16), jax.ShapeDtypeStruct((r, c), F32)],
        grid_spec=pltpu.PrefetchScalarGridSpec(
            num_scalar_prefetch=1, grid=(c // tc, 4),
            in_specs=[pl.BlockSpec((1, 1, r, tc), lambda j, k, pos_ref: (k, pos_ref[0], 0, j)), slot],
            out_specs=[slot, pl.BlockSpec((r, tc), lambda j, k, pos_ref: (0, j))]),
        compiler_params=pltpu.CompilerParams(dimension_semantics=("parallel", "arbitrary")),
    )(pos, *_in_hbm(g4, recv))


class _ReduceScatter:
    def __init__(self, tag, grads_t, pos):
        self.tag, self.pos, self.names = tag, pos, list(grads_t)
        g4s = [g.reshape(4, 2, g.size // (N_DEV * g.shape[-1]), g.shape[-1]) for g in grads_t.values()]
        lands = [lax.empty((4,) + g.shape[2:], F32) for g in g4s]
        self.ex = _exchange_start(f"rs_{tag}_sibling_start", _scatter_sibling_copies, g4s, lands, 4 * len(g4s))
        self.token = self.ex.token

    def start_chips(self, after):
        g4s, from_sibling = _exchange_wait(f"rs_{self.tag}_sibling_wait", self.ex, after)
        parts = [_add_sibling(g4, rv, self.pos, f"rs_add_sibling_{k}")
                 for k, g4, rv in zip(self.names, g4s, from_sibling)]
        self.mine = [mine for _, mine in parts]
        p16s = [p16 for p16, _ in parts]
        lands = [lax.empty((3,) + p.shape[1:], BF16) for p in p16s]
        self.ex = _exchange_start(f"rs_{self.tag}_chips_start", _scatter_chips_copies, p16s, lands, 3 * len(p16s))
        self.token = self.ex.token

    def finish(self, after):
        _, from_chips = _exchange_wait(f"rs_{self.tag}_chips_wait", self.ex, after)
        return dict(zip(self.names, zip(self.mine, from_chips)))


def _rope_tables():
    positions = np.arange(SEQ, dtype=np.float32)
    inv_freq = np.power(np.float32(ROPE_THETA), -np.arange(0, ROPE_DIM, 2, dtype=np.float32) / np.float32(ROPE_DIM))
    ang = (positions[:, None] * inv_freq[None, :]).astype(np.float32)
    cos, sin = np.cos(ang).astype(np.float32), np.sin(ang).astype(np.float32)
    ones = np.ones((SEQ, HEAD_DIM - ROPE_DIM), np.float32)
    zeros8 = np.zeros((SEQ, ROPE_HALF), np.float32)
    zeros = np.zeros((SEQ, HEAD_DIM - ROPE_DIM), np.float32)
    c_head = np.concatenate([cos, cos, ones], axis=1)
    s1_head = np.concatenate([-sin, zeros8, zeros], axis=1)
    s2_head = np.concatenate([zeros8, sin, zeros], axis=1)
    return tuple(jnp.asarray(np.concatenate([t, t], axis=1)) for t in (c_head, s1_head, s2_head))


def _rope_apply(x, c, s1, s2):
    w = x.shape[1]
    return x * c + pltpu.roll(x, w - ROPE_HALF, 1) * s1 + pltpu.roll(x, ROPE_HALF, 1) * s2


def _rope_apply_t(dy, c, s1, s2):
    w = dy.shape[1]
    return dy * c + pltpu.roll(dy * s1, ROPE_HALF, 1) + pltpu.roll(dy * s2, w - ROPE_HALF, 1)


def _dil_prev_limit(has_prev):
    return jnp.where(has_prev, 0, BLOCK)


def _dil_valid(limit):
    row = lax.broadcasted_iota(jnp.int32, (BLOCK, 2 * BLOCK), 0)
    col = lax.broadcasted_iota(jnp.int32, (BLOCK, 2 * BLOCK), 1)
    dist = col - row
    return jnp.logical_and(dist >= jnp.where(col < BLOCK, limit, -BLOCK), dist <= BLOCK)


def _upper_half():
    return lax.broadcasted_iota(jnp.int32, (1, LANES), 1) >= HEAD_DIM


def _stack_heads(x):
    upper = _upper_half()
    return jnp.concatenate([jnp.where(upper, 0, x), jnp.where(upper, x, 0)], axis=0)


def _unstack_heads(y):
    n = y.shape[0] // 2
    return jnp.where(_upper_half(), y[n:], y[:n])


def _head_columns(t):
    return jnp.concatenate([t[:, 0:1], t[:, HEAD_DIM:HEAD_DIM + 1]], axis=0)


def _dil_rows(n, d):
    per = N_BLOCKS // d
    r, lb = n // per, n % per

    def rows(b):
        start = b * (BLOCK * d) + r
        return pl.ds(pl.multiple_of(start, BLOCK), BLOCK) if d == 1 else pl.ds(start, BLOCK, stride=d)

    return rows(lb), rows(jnp.maximum(lb - 1, 0)), lb > 0


def _dil_rotate(q_ref, k_ref, c_ref, s1_ref, s2_ref, q_rot, k_rot):
    tabs = (c_ref[...], s1_ref[...], s2_ref[...])
    q_rot[...] = _rope_apply(q_ref[...], *tabs) * QK_SCALE
    k_rot[...] = _rope_apply(k_ref[...], *tabs)


def _dil_specs():
    def col(base):
        return pl.BlockSpec((SEQ, LANES), lambda p: (0, base // LANES + p))

    table = pl.BlockSpec((SEQ, LANES), lambda p: (0, 0))
    return [col(COL_QA), col(COL_KA), col(COL_VA)], [table] * 3


def _store_columns(blocks, dproj_ref, cols, sem):
    copies = [pltpu.make_async_copy(b, dproj_ref.at[:, pl.ds(pl.multiple_of(c * LANES, LANES), LANES)], sem.at[i])
              for i, (b, c) in enumerate(zip(blocks, cols))]
    for cp in copies:
        cp.start()
    for cp in copies:
        cp.wait()


def _dil_window(d, n, k_rot, v_ref):
    rows, prev, has_prev = _dil_rows(n, d)
    kw, vw = k_rot[rows, :].astype(BF16), v_ref[rows, :].astype(BF16)
    if d == N_BLOCKS:
        row = lax.broadcasted_iota(jnp.int32, (BLOCK, BLOCK), 0)
        valid = lax.broadcasted_iota(jnp.int32, (BLOCK, BLOCK), 1) <= row
    else:
        kw = jnp.concatenate([k_rot[prev, :].astype(BF16), kw], axis=0)
        vw = jnp.concatenate([v_ref[prev, :].astype(BF16), vw], axis=0)
        valid = _dil_valid(_dil_prev_limit(has_prev))
    return rows, prev, kw, vw, jnp.concatenate([valid, valid], axis=0)


def _dil_fwd(proj, tables):
    def body(q_ref, k_ref, v_ref, c_ref, s1_ref, s2_ref, o_ref, lse_ref, q_rot, k_rot):
        upper = _upper_half()
        _dil_rotate(q_ref, k_ref, c_ref, s1_ref, s2_ref, q_rot, k_rot)

        def blocks_of(d):
            def block(n, carry):
                rows, _, kw, vw, valid = _dil_window(d, n, k_rot, v_ref)
                s = jnp.where(valid, _dot_nt(_stack_heads(q_rot[rows, :].astype(BF16)), kw), NEG_INF)
                m = jnp.max(s, axis=-1, keepdims=True)
                p = jnp.exp(s - m)
                den = jnp.sum(p, axis=-1, keepdims=True)
                o_ref[rows, :] = _unstack_heads(_dot_nn((p * (1.0 / den)).astype(BF16), vw))
                lse = m + jnp.log(den)
                lse_ref[rows, :] = jnp.where(upper, lse[BLOCK:], lse[:BLOCK])
                return carry

            lax.fori_loop(0, N_BLOCKS, block, 0, unroll=4)

        for g, d in enumerate(DILATIONS):
            pl.when(pl.program_id(0) // 2 == g)(functools.partial(blocks_of, d))

    qkv, tabs = _dil_specs()
    out = pl.BlockSpec((SEQ, LANES), lambda p: (0, p))
    return pl.pallas_call(
        body, name="dil_attn_fwd", grid=(DIL_WIDTH // LANES,), in_specs=qkv + tabs, out_specs=[out, out],
        out_shape=[jax.ShapeDtypeStruct((SEQ, DIL_WIDTH), F32)] * 2,
        scratch_shapes=[pltpu.VMEM((SEQ, LANES), F32)] * 2,
        compiler_params=pltpu.CompilerParams(dimension_semantics=("parallel",)),
    )(*_in_hbm(proj, proj, proj, *tables))


def _dil_bwd(proj, tables, do, lse, c, dproj, deps=()):
    def body(q_ref, k_ref, v_ref, c_ref, s1_ref, s2_ref, do_ref, lse_ref, cc_ref, dproj_in, *rest):
        dproj_ref, dq_acc, dk_acc, dv_acc, dq_out, dk_out, dv_out, q_rot, k_rot, sem = rest[len(deps):]
        dk_acc[...] = jnp.zeros_like(dk_acc)
        dv_acc[...] = jnp.zeros_like(dv_acc)
        _dil_rotate(q_ref, k_ref, c_ref, s1_ref, s2_ref, q_rot, k_rot)

        def blocks_of(d):
            def block(n, carry):
                rows, prev, kw, vw, valid = _dil_window(d, n, k_rot, v_ref)
                q2 = _stack_heads(q_rot[rows, :].astype(BF16))
                do2 = _stack_heads(do_ref[rows, :].astype(BF16))
                lse_col, c_col = _head_columns(lse_ref[rows, :]), _head_columns(cc_ref[rows, :])
                p = jnp.where(valid, jnp.exp(_dot_nt(q2, kw) - lse_col), 0.0)
                ds = (p * (_dot_nt(do2, vw) + c_col)).astype(BF16)
                dk, dv = _dot_tn(ds, q2), _dot_tn(p.astype(BF16), do2)
                dq_acc[rows, :] = _unstack_heads(_dot_nn(ds, kw)) * QK_SCALE
                if d == N_BLOCKS:
                    dk_acc[rows, :] += dk
                    dv_acc[rows, :] += dv
                else:
                    dk_acc[prev, :] += dk[:BLOCK]
                    dv_acc[prev, :] += dv[:BLOCK]
                    dk_acc[rows, :] += dk[BLOCK:]
                    dv_acc[rows, :] += dv[BLOCK:]
                return carry

            lax.fori_loop(0, N_BLOCKS, block, 0, unroll=4)

        pair = pl.program_id(0)
        for g, d in enumerate(DILATIONS):
            pl.when(pair // 2 == g)(functools.partial(blocks_of, d))
        tabs = (c_ref[...], s1_ref[...], s2_ref[...])
        dq_out[...] = _rope_apply_t(dq_acc[...], *tabs).astype(BF16)
        dk_out[...] = _rope_apply_t(dk_acc[...], *tabs).astype(BF16)
        dv_out[...] = dv_acc[...].astype(BF16)
        _store_columns((dq_out, dk_out, dv_out), dproj_ref,
                       [base // LANES + pair for base in (COL_QA, COL_KA, COL_VA)], sem)

    qkv, tabs = _dil_specs()
    tok = pl.BlockSpec((SEQ, LANES), lambda p: (0, p))
    return pl.pallas_call(
        body, name="dil_attn_bwd", grid=(DIL_WIDTH // LANES,),
        in_specs=qkv + tabs + [tok, tok, tok, _ANY] + [_ANY] * len(deps), out_specs=_ANY,
        out_shape=jax.ShapeDtypeStruct(dproj.shape, dproj.dtype),
        scratch_shapes=[pltpu.VMEM((SEQ, LANES), F32)] * 3 + [pltpu.VMEM((SEQ, LANES), BF16)] * 3
        + [pltpu.VMEM((SEQ, LANES), F32)] * 2 + [pltpu.SemaphoreType.DMA((3,))],
        input_output_aliases={9: 0},
        compiler_params=pltpu.CompilerParams(dimension_semantics=("arbitrary",)),
    )(*_in_hbm(proj, proj, proj, *tables, do, lse, c, dproj), *deps)


def _group_weights(l0, l1, l2):
    m = jnp.maximum(jnp.maximum(l0, l1), l2)
    e0, e1, e2 = jnp.exp(l0 - m), jnp.exp(l1 - m), jnp.exp(l2 - m)
    tot = e0 + e1 + e2
    return e0 / tot, e1 / tot, e2 / tot


def _dil_combine(o, lse, deps=()):
    def fn(o0, o1, o2, l0, l1, l2):
        w0, w1, w2 = _group_weights(l0, l1, l2)
        return w0 * o0 + w1 * o1 + w2 * o2

    w = DIL_OUT_WIDTH
    return _rowwise(fn, "dil_combine", SEQ, 512, [(o, w, g) for g in range(3)] + [(lse, w, g) for g in range(3)], [],
                    [(w, F32)], deps=deps)[0]


def _dil_combine_bwd(d_out, o, lse, deps=()):
    w = DIL_OUT_WIDTH

    def fn(d, o0, o1, o2, l0, l1, l2):
        row = lax.broadcasted_iota(jnp.int32, (w, w), 0) // HEAD_DIM
        col = lax.broadcasted_iota(jnp.int32, (w, w), 1) // HEAD_DIM
        same_head = jnp.where(row == col, 1.0, 0.0).astype(BF16)
        ws = _group_weights(l0, l1, l2)
        dws = [_dot3_nn(d * og, same_head) for og in (o0, o1, o2)]
        mean = ws[0] * dws[0] + ws[1] * dws[1] + ws[2] * dws[2]
        return jnp.concatenate([wg * d for wg in ws], axis=1), jnp.concatenate([-wg * mean for wg in ws], axis=1)

    return _rowwise(fn, "dil_combine_bwd", SEQ, 256,
                    [(d_out, w, 0)] + [(o, w, g) for g in range(3)] + [(lse, w, g) for g in range(3)], [],
                    [(DIL_WIDTH, F32)] * 2, deps=deps)


def _log1p(e):
    u = 1.0 + e
    return jnp.where(u == 1.0, e, jnp.log(u) * (e / (u - 1.0)))


def _fox_gate(proj, b_pad, deps=()):
    def body(f_ref, b_ref, *rest):
        o_ref = rest[-1]
        z = f_ref[...] + b_ref[...]
        logf = (jnp.minimum(z, 0.0) - _log1p(jnp.exp(-jnp.abs(z)))).T[:F_ROWS]
        row = lax.broadcasted_iota(jnp.int32, (BLOCK, BLOCK), 0)
        col = lax.broadcasted_iota(jnp.int32, (BLOCK, BLOCK), 1)
        before = jnp.where(row <= col, 1.0, 0.0).astype(BF16)
        carry = jnp.zeros((F_ROWS, 1), F32)
        for blk in range(N_BLOCKS):
            run = _dot3_nn(logf[:, blk * BLOCK:(blk + 1) * BLOCK], before) + carry
            o_ref[:, blk * BLOCK:(blk + 1) * BLOCK] = run
            carry = run[:, BLOCK - 1:BLOCK]

    return pl.pallas_call(
        body, name="fox_gate", grid=(1,),
        in_specs=[pl.BlockSpec((SEQ, LANES), lambda i: (0, COL_F // LANES)), pl.BlockSpec((1, LANES), lambda i: (0, 0))]
        + [_ANY] * len(deps),
        out_specs=pl.BlockSpec((F_ROWS, SEQ), lambda i: (0, 0)),
        out_shape=jax.ShapeDtypeStruct((F_ROWS, SEQ), F32),
    )(*_in_hbm(proj, b_pad), *deps)


def _fox_gate_bwd(d_cum, proj, b_pad, dproj):
    def body(d_ref, f_ref, b_ref, dproj_ref, dz_ref, db_ref):
        row = lax.broadcasted_iota(jnp.int32, (BLOCK, BLOCK), 0)
        col = lax.broadcasted_iota(jnp.int32, (BLOCK, BLOCK), 1)
        after = jnp.where(row >= col, 1.0, 0.0).astype(BF16)
        carry = jnp.zeros((F_ROWS, 1), F32)
        parts = [None] * N_BLOCKS
        for blk in reversed(range(N_BLOCKS)):
            run = _dot3_nn(d_ref[:, blk * BLOCK:(blk + 1) * BLOCK], after) + carry
            parts[blk] = run
            carry = run[:, 0:1]
        dlogf = jnp.concatenate(parts, axis=1)
        dlogf = jnp.concatenate([dlogf, jnp.zeros((LANES - F_ROWS, SEQ), F32)], axis=0).T
        dz = dlogf * _sigmoid(-(f_ref[...] + b_ref[...]))
        dz_ref[...] = dz.astype(BF16)
        db_ref[...] = jnp.sum(dz, axis=0, keepdims=True)

    f_cols = pl.BlockSpec((SEQ, LANES), lambda i: (0, COL_F // LANES))
    return pl.pallas_call(
        body, name="fox_gate_bwd", grid=(1,),
        in_specs=[pl.BlockSpec((F_ROWS, SEQ), lambda i: (0, 0)), f_cols, pl.BlockSpec((1, LANES), lambda i: (0, 0)), _ANY],
        out_specs=[f_cols, pl.BlockSpec((1, LANES), lambda i: (0, 0))],
        out_shape=[jax.ShapeDtypeStruct(dproj.shape, dproj.dtype), jax.ShapeDtypeStruct((1, LANES), F32)],
        input_output_aliases={3: 0},
    )(*_in_hbm(d_cum, proj, b_pad, dproj))


FOX_TILE = 256
FOX_TILES = SEQ // FOX_TILE


def _row_to_col(row):
    n = row.shape[1]
    eye = lax.broadcasted_iota(jnp.int32, (n, n), 0) == lax.broadcasted_iota(jnp.int32, (n, n), 1)
    return jnp.sum(jnp.where(eye, row, 0.0), axis=1, keepdims=True)


def _fox_bias(f_row, i):
    t = FOX_TILE
    ext = (i + 1) * t
    bias = _row_to_col(f_row[:, i * t:(i + 1) * t]) - f_row[:, :ext]
    row = lax.broadcasted_iota(jnp.int32, (t, ext), 0) + i * t
    col = lax.broadcasted_iota(jnp.int32, (t, ext), 1)
    return bias, col <= row


def _fox_specs():
    qkv = [pl.BlockSpec((SEQ, LANES), lambda p, base=base: (0, base // LANES + p)) for base in (COL_QB, COL_KB, COL_VB)]
    return qkv, pl.BlockSpec((F_ROWS, SEQ), lambda p: (0, 0))


def _fox_fwd(proj, f_rows):
    t = FOX_TILE

    def body(q_ref, k_ref, v_ref, f_ref, o_ref, lse_ref):
        pair = pl.program_id(0)
        upper = _upper_half()
        k16, v16 = k_ref[...].astype(BF16), v_ref[...].astype(BF16)
        f_row = [f_ref[pl.ds(2 * pair + e, 1), :] for e in range(2)]
        for i in range(FOX_TILES):
            ext = (i + 1) * t
            q_tile = (q_ref[i * t:(i + 1) * t, :] * QK_SCALE).astype(BF16)
            s2 = _dot_nt(_stack_heads(q_tile), k16[:ext])
            pns, lses = [], []
            for e in range(2):
                bias, causal = _fox_bias(f_row[e], i)
                s = jnp.where(causal, s2[e * t:(e + 1) * t] + bias, NEG_INF)
                m = jnp.max(s, axis=-1, keepdims=True)
                p = jnp.exp(s - m)
                den = jnp.sum(p, axis=-1, keepdims=True)
                pns.append((p * (1.0 / den)).astype(BF16))
                lses.append(m + jnp.log(den))
            o_ref[i * t:(i + 1) * t, :] = _unstack_heads(_dot_nn(jnp.concatenate(pns, axis=0), v16[:ext]))
            lse_ref[i * t:(i + 1) * t, :] = jnp.where(upper, lses[1], lses[0])

    qkv, f_spec = _fox_specs()
    tok = pl.BlockSpec((SEQ, LANES), lambda p: (0, p))
    return pl.pallas_call(
        body, name="fox_attn_fwd", grid=(FOX_WIDTH // LANES,),
        in_specs=qkv + [f_spec], out_specs=[tok, tok],
        out_shape=[jax.ShapeDtypeStruct((SEQ, FOX_WIDTH), F32)] * 2,
        compiler_params=pltpu.CompilerParams(
            dimension_semantics=("parallel",), vmem_limit_bytes=_vmem_limit(8 * t * SEQ * 4)),
    )(*_in_hbm(proj, proj, proj, f_rows))


def _fox_bwd(proj, do, lse, f_rows, dproj):
    t = FOX_TILE

    def body(q_ref, k_ref, v_ref, f_ref, do_ref, lse_ref, dproj_in, dproj_ref, df_ref, dk_acc, dv_acc,
             dq_out, dk_out, dv_out, sem):
        pair = pl.program_id(0)
        upper = _upper_half()
        k16, v16 = k_ref[...].astype(BF16), v_ref[...].astype(BF16)
        f_row = [f_ref[pl.ds(2 * pair + e, 1), :] for e in range(2)]
        dk_acc[...] = jnp.zeros_like(dk_acc)
        dv_acc[...] = jnp.zeros_like(dv_acc)
        df_ref[...] = jnp.zeros_like(df_ref)
        for i in range(FOX_TILES):
            ext = (i + 1) * t
            q_tile = (q_ref[i * t:(i + 1) * t, :] * QK_SCALE).astype(BF16)
            do_tile = do_ref[i * t:(i + 1) * t, :]
            lse_t = lse_ref[i * t:(i + 1) * t, :]
            q2, do2 = _stack_heads(q_tile), _stack_heads(do_tile)
            s2, dp2 = _dot_nt(q2, k16[:ext]), _dot_nt(do2, v16[:ext])
            ps, dss = [], []
            for e in range(2):
                bias, causal = _fox_bias(f_row[e], i)
                s = s2[e * t:(e + 1) * t] + bias
                p = jnp.where(causal, jnp.exp(s - lse_t[:, e * HEAD_DIM:e * HEAD_DIM + 1]), 0.0)
                dp = dp2[e * t:(e + 1) * t]
                ds = p * (dp - jnp.sum(p * dp, axis=-1, keepdims=True))
                df_ref[0, e:e + 1, :ext] -= jnp.sum(ds, axis=0, keepdims=True)
                ps.append(p.astype(BF16))
                dss.append(ds.astype(BF16))
            ds2, p2 = jnp.concatenate(dss, axis=0), jnp.concatenate(ps, axis=0)
            dq_out[i * t:(i + 1) * t, :] = (_unstack_heads(_dot_nn(ds2, k16[:ext])) * QK_SCALE).astype(BF16)
            dk_acc[:ext, :] += _dot_tn(ds2, q2)
            dv_acc[:ext, :] += _dot_tn(p2, do2)
        dk_out[...] = dk_acc[...].astype(BF16)
        dv_out[...] = dv_acc[...].astype(BF16)
        _store_columns((dq_out, dk_out, dv_out), dproj_ref, [base // LANES + pair for base in (COL_QB, COL_KB, COL_VB)],
                       sem)

    qkv, f_spec = _fox_specs()
    tok = pl.BlockSpec((SEQ, LANES), lambda p: (0, p))
    return pl.pallas_call(
        body, name="fox_attn_bwd", grid=(FOX_WIDTH // LANES,),
        in_specs=qkv + [f_spec, tok, tok, _ANY],
        out_specs=[_ANY, pl.BlockSpec((1, SUBLANES, SEQ), lambda p: (p, 0, 0))],
        out_shape=[jax.ShapeDtypeStruct(dproj.shape, dproj.dtype),
                   jax.ShapeDtypeStruct((FOX_WIDTH // LANES, SUBLANES, SEQ), F32)],
        scratch_shapes=[pltpu.VMEM((SEQ, LANES), F32)] * 2 + [pltpu.VMEM((SEQ, LANES), BF16)] * 3
        + [pltpu.SemaphoreType.DMA((3,))],
        input_output_aliases={6: 0},
        compiler_params=pltpu.CompilerParams(
            dimension_semantics=("arbitrary",), vmem_limit_bytes=_vmem_limit(10 * t * SEQ * 4)),
    )(*_in_hbm(proj, proj, proj, f_rows, do, lse, dproj))


MIX_TILE = 256


def _mix_out(out_a, out_b, proj, x, wt_pa, wt_pb, w_out, g_post, g_ffn_pre):
    tm = MIX_TILE

    def body(a_ref, b_ref, ga_ref, gb_ref, x_ref, wpa_ref, wpb_ref, wo_ref, g2_ref, g3_ref,
             merged_ref, mix_ref, x1_ref, h2_ref):
        ya = _dot_nn(a_ref[...].astype(BF16), wpa_ref[...])
        yb = _dot_nn(b_ref[...].astype(BF16), wpb_ref[...])
        merged = (_sigmoid(ga_ref[...]) * ya + _sigmoid(gb_ref[...]) * yb).astype(BF16)
        merged_ref[...] = merged
        mix = _dot_nn(merged, wo_ref[...])
        mix_ref[...] = mix
        x1 = x_ref[...] + mix * _rms_scale(mix) * g2_ref[...]
        x1_ref[...] = x1
        h2_ref[...] = (x1 * _rms_scale(x1) * g3_ref[...]).astype(BF16)

    def rows(w, cb=0):
        return pl.BlockSpec((tm, w), lambda i, cb=cb: (i, cb))

    def whole(a):
        return pl.BlockSpec(a.shape, lambda i: (0, 0))

    d = D_MODEL
    blk = _nbytes((tm, d), F32) * 6 + sum(_nbytes(a.shape, BF16) for a in (wt_pa, wt_pb, w_out))
    return pl.pallas_call(
        body, name="mix_out", grid=(SEQ // tm,),
        in_specs=[rows(DIL_OUT_WIDTH), rows(FOX_WIDTH), rows(d, COL_GA // d), rows(d, COL_GB // d), rows(d),
                  whole(wt_pa), whole(wt_pb), whole(w_out), whole(g_post), whole(g_ffn_pre)],
        out_specs=[rows(d)] * 4,
        out_shape=[jax.ShapeDtypeStruct((SEQ, d), dt) for dt in (BF16, F32, F32, BF16)],
        compiler_params=pltpu.CompilerParams(dimension_semantics=("parallel",), vmem_limit_bytes=_vmem_limit(blk)),
    )(out_a, out_b, proj, proj, x, wt_pa, wt_pb, w_out, g_post, g_ffn_pre)


def _mix_out_bwd(dmix, out_a, out_b, proj, wt_pa, wt_pb, w_out, deps=()):
    tm = MIX_TILE

    def body(dm_ref, a_ref, b_ref, ga_ref, gb_ref, wpa_ref, wpb_ref, wo_ref, *rest):
        dproj_ref, dya_ref, dyb_ref, da_ref, db_ref = rest[len(deps):]
        dmerged = _dot_nt(dm_ref[...], wo_ref[...])
        ya = _dot_nn(a_ref[...].astype(BF16), wpa_ref[...])
        yb = _dot_nn(b_ref[...].astype(BF16), wpb_ref[...])
        sa, sb = _sigmoid(ga_ref[...]), _sigmoid(gb_ref[...])
        dproj_ref[:, COL_GA:COL_GA + D_MODEL] = (dmerged * ya * (sa * (1.0 - sa))).astype(BF16)
        dproj_ref[:, COL_GB:COL_GB + D_MODEL] = (dmerged * yb * (sb * (1.0 - sb))).astype(BF16)
        dproj_ref[:, COL_GB + D_MODEL:] = jnp.zeros((tm, COL_QA - COL_GB - D_MODEL), BF16)
        dya = (dmerged * sa).astype(BF16)
        dyb = (dmerged * sb).astype(BF16)
        dya_ref[...] = dya
        dyb_ref[...] = dyb
        da_ref[...] = _dot_nt(dya, wpa_ref[...])
        db_ref[...] = _dot_nt(dyb, wpb_ref[...]).astype(BF16)

    def rows(w, cb=0):
        return pl.BlockSpec((tm, w), lambda i, cb=cb: (i, cb))

    def whole(a):
        return pl.BlockSpec(a.shape, lambda i: (0, 0))

    d = D_MODEL
    blk = _nbytes((tm, d), F32) * 8 + sum(_nbytes(a.shape, BF16) for a in (wt_pa, wt_pb, w_out))
    return pl.pallas_call(
        body, name="mix_out_bwd", grid=(SEQ // tm,),
        in_specs=[rows(d), rows(DIL_OUT_WIDTH), rows(FOX_WIDTH), rows(d, COL_GA // d), rows(d, COL_GB // d),
                  whole(wt_pa), whole(wt_pb), whole(w_out)] + [_ANY] * len(deps),
        out_specs=[rows(COL_QA)] + [rows(d)] * 2 + [rows(DIL_OUT_WIDTH), rows(FOX_WIDTH)],
        out_shape=[jax.ShapeDtypeStruct((SEQ, PROJ_COLS), BF16)] + [jax.ShapeDtypeStruct((SEQ, d), BF16)] * 2
        + [jax.ShapeDtypeStruct((SEQ, DIL_OUT_WIDTH), F32), jax.ShapeDtypeStruct((SEQ, FOX_WIDTH), BF16)],
        compiler_params=pltpu.CompilerParams(dimension_semantics=("parallel",), vmem_limit_bytes=_vmem_limit(blk)),
    )(dmix, out_a, out_b, proj, proj, wt_pa, wt_pb, w_out, *deps)


FFN_TM, FFN_TN = 2048, 256


def _ffn_up(h2, wt_gate, wt_up):
    tm, tn = FFN_TM, FFN_TN

    def body(h_ref, wg_ref, wu_ref, gate_ref, up_ref, act_ref):
        for rows in (slice(0, tm // 2), slice(tm // 2, tm)):
            gate = _dot_nt(h_ref[rows, :], wg_ref[...])
            up = _dot_nt(h_ref[rows, :], wu_ref[...])
            gate_ref[rows, :] = gate
            up_ref[rows, :] = up
            act_ref[rows, :] = (gate * _sigmoid(gate) * up).astype(BF16)

    tile = pl.BlockSpec((tm, tn), lambda i, j: (i, j))
    w_spec = pl.BlockSpec((tn, D_MODEL), lambda i, j: (j, 0))
    return pl.pallas_call(
        body, name="ffn_up", grid=(SEQ // tm, D_FF // tn),
        in_specs=[pl.BlockSpec((tm, D_MODEL), lambda i, j: (i, 0)), w_spec, w_spec],
        out_specs=[tile, tile, tile],
        out_shape=[jax.ShapeDtypeStruct((SEQ, D_FF), dt) for dt in (F32, F32, BF16)],
        compiler_params=pltpu.CompilerParams(
            dimension_semantics=("parallel", "parallel"), vmem_limit_bytes=_vmem_limit(8 * 2**20)),
    )(h2, wt_gate, wt_up)


def _ffn_act_bwd(dff, w_down, gate, up):
    tm, tn = FFN_TM, FFN_TN

    def body(d_ref, wd_ref, gate_ref, up_ref, dgate_ref, dup_ref):
        for rows in (slice(0, tm // 2), slice(tm // 2, tm)):
            dact = _dot_nt(d_ref[rows, :], wd_ref[...])
            gate = gate_ref[rows, :]
            sg = _sigmoid(gate)
            dgate_ref[rows, :] = (dact * up_ref[rows, :] * (sg * (1.0 + gate * (1.0 - sg)))).astype(BF16)
            dup_ref[rows, :] = (dact * (gate * sg)).astype(BF16)

    tile = pl.BlockSpec((tm, tn), lambda i, j: (i, j))
    return pl.pallas_call(
        body, name="ffn_act_bwd", grid=(SEQ // tm, D_FF // tn),
        in_specs=[pl.BlockSpec((tm, D_MODEL), lambda i, j: (i, 0)), pl.BlockSpec((tn, D_MODEL), lambda i, j: (j, 0)),
                  tile, tile],
        out_specs=[tile, tile],
        out_shape=[jax.ShapeDtypeStruct((SEQ, D_FF), BF16)] * 2,
        compiler_params=pltpu.CompilerParams(
            dimension_semantics=("parallel", "parallel"), vmem_limit_bytes=_vmem_limit(8 * 2**20)),
    )(dff, w_down, gate, up)


EPILOGUE_TM = 512


def _loss_head(act, w_down, x1, target, g_post):
    def fn(ff, x1, tgt, g):
        r = _rms_scale(ff)
        nrm = ff * r
        err = (x1 + nrm * g) - tgt
        loss = 0.5 * jnp.sum(jnp.mean(err * err, axis=-1, keepdims=True), axis=0, keepdims=True)
        dy = err * (1.0 / D_MODEL)
        u = dy * g
        dff = r * u - ff * (r * r * r) * jnp.mean(u * ff, axis=-1, keepdims=True)
        return dy, dff, jnp.broadcast_to(loss, (1, LANES)), jnp.sum(dy * nrm, axis=0, keepdims=True)

    d = D_MODEL
    return _matmul_rowwise([(act, w_down)], fn, "ffn_down_loss", EPILOGUE_TM, [(x1, d, 0), (target, d, 0)], [g_post],
                           [(d, F32), (d, BF16)], [LANES, d])


def _post_ffn_bwd(dgate, wt_gate, dup, wt_up, x1, dy, mix, g_ffn_pre, g_mix_post, deps=()):
    def fn(dh2, x1, dy, mix, g3, g2):
        dx, dg3 = _rms_bwd(x1, dh2, g3)
        dx1 = dy + dx
        dmix, dg2 = _rms_bwd(mix, dx1, g2)
        return dx1, dmix, dg3, dg2

    d = D_MODEL
    return _matmul_rowwise([(dgate, wt_gate), (dup, wt_up)], fn, "ffn_up_bwd", EPILOGUE_TM,
                           [(x1, d, 0), (dy, d, 0), (mix, d, 0)], [g_ffn_pre, g_mix_post],
                           [(d, F32), (d, BF16)], [d, d], deps=deps)


def _input_bwd(dproj, wt_r, x, dx1, g_pre, deps=()):
    def fn(dh, x, dx1, g):
        dx, dg = _rms_bwd(x, dh, g)
        return dx1 + dx, dg

    d = D_MODEL
    return _matmul_rowwise([(dproj, wt_r)], fn, "in_proj_bwd", EPILOGUE_TM, [(x, d, 0), (dx1, d, 0)], [g_pre],
                           [(d, F32)], [d], deps=deps)


def _adam_math(w, g, m, v):
    m = ADAM_B1 * m + (1.0 - ADAM_B1) * g
    v = ADAM_B2 * v + (1.0 - ADAM_B2) * (g * g)
    m_hat = m / (1.0 - ADAM_B1 ** ADAM_STEP)
    v_hat = v / (1.0 - ADAM_B2 ** ADAM_STEP)
    delta = -ADAM_LR * (m_hat / (jnp.sqrt(v_hat) + ADAM_EPS) + ADAM_WD * w)
    return delta, m, v


def _adam(w, mine, recv, m, v, name):
    r, c = w.shape
    tc = _col_tile(r, c)

    def body(w_ref, p_ref, r_ref, m_ref, v_ref, g_ref, d_ref, nm_ref, nv_ref):
        g = ((p_ref[...] + r_ref[0].astype(F32)) + r_ref[1].astype(F32)) + r_ref[2].astype(F32)
        g_ref[...] = g
        d_ref[...], nm_ref[...], nv_ref[...] = _adam_math(w_ref[...], g, m_ref[...], v_ref[...])

    spec = pl.BlockSpec((r, tc), lambda j: (0, j))
    return pl.pallas_call(
        body, name=name, grid=(c // tc,),
        in_specs=[spec, spec, pl.BlockSpec((3, r, tc), lambda j: (0, 0, j)), spec, spec], out_specs=[spec] * 4,
        out_shape=[jax.ShapeDtypeStruct((r, c), F32)] * 4,
        compiler_params=pltpu.CompilerParams(dimension_semantics=("parallel",)),
    )(*_in_hbm(w, mine, recv, m, v))


def _adam_small(gathered, ws, ms, vs, loss_parts):
    n = len(ws)

    def body(*refs):
        outs = refs[4 * n + 1:]
        loss = refs[4 * n][0]
        for dev in range(1, N_DEV):
            loss = loss + refs[4 * n][dev]
        outs[4 * n][...] = loss
        for i in range(n):
            ga_ref, w_ref, m_ref, v_ref = (refs[j * n + i] for j in range(4))
            g = ga_ref[0]
            for dev in range(1, N_DEV):
                g = g + ga_ref[dev]
            g = g[:, :w_ref.shape[1]]
            outs[4 * i][...] = g
            outs[4 * i + 1][...], outs[4 * i + 2][...], outs[4 * i + 3][...] = _adam_math(
                w_ref[...], g, m_ref[...], v_ref[...])

    out_shape = [jax.ShapeDtypeStruct(w.shape, F32) for w in ws for _ in range(4)]
    out_shape.append(jax.ShapeDtypeStruct((1, LANES), F32))
    out = pl.pallas_call(body, name="adam_small", out_shape=out_shape)(*gathered, *ws, *ms, *vs, loss_parts)
    return [out[4 * i:4 * i + 4] for i in range(n)], out[4 * n]


_PROJ_SEGMENTS = ((3848, 5896), (None, COL_QA - 2 * D_MODEL), (0, 3840), (3840, 3848), (None, PROJ_COLS - COL_F - 8))


def _proj_weight_t(gathered):
    w = gathered.reshape(IN_COLS, D_MODEL)
    return jnp.concatenate([jnp.zeros((hi, D_MODEL), w.dtype) if lo is None else w[lo:hi] for lo, hi in _PROJ_SEGMENTS],
                           axis=0)


def _proj_weight_grad_slots(dwt_r):
    starts, at = [], 0
    for lo, hi in _PROJ_SEGMENTS:
        if lo is not None:
            starts.append((lo, hi, at))
        at += hi if lo is None else hi - lo
    slots = []
    for dev in range(N_DEV):
        pieces, lo, end = [], dev * IN_SHARD, (dev + 1) * IN_SHARD
        for seg_lo, seg_hi, seg_at in sorted(starts):
            a, b = max(lo, seg_lo), min(end, seg_hi)
            if a < b:
                pieces.append(dwt_r[seg_at + a - seg_lo:seg_at + b - seg_lo])
        slots.append(pieces[0] if len(pieces) == 1 else jnp.concatenate(pieces, axis=0))
    return jnp.stack(slots)


def kernel(x, w_in, w_proj_a, w_proj_b, w_out, b_forget, w_ffn_gate, w_ffn_up, w_ffn_down, norm_mix_pre, norm_mix_post, norm_ffn_pre, norm_ffn_post, loss_target, m_w_in, m_w_proj_a, m_w_proj_b, m_w_out, m_b_forget, m_w_ffn_gate, m_w_ffn_up, m_w_ffn_down, m_norm_mix_pre, m_norm_mix_post, m_norm_ffn_pre, m_norm_ffn_post, v_w_in, v_w_proj_a, v_w_proj_b, v_w_out, v_b_forget, v_w_ffn_gate, v_w_ffn_up, v_w_ffn_down, v_norm_mix_pre, v_norm_mix_post, v_norm_ffn_pre, v_norm_ffn_post):
    d = D_MODEL
    names = ("w_in", "w_proj_a", "w_proj_b", "w_out", "w_ffn_gate", "w_ffn_up", "w_ffn_down")
    col_sharded = ("w_in", "w_ffn_gate", "w_ffn_up")

    def row_shards(arrs):
        return {k: (a[0].T if k in col_sharded else a[0]) for k, a in zip(names, arrs)}

    shards = row_shards((w_in, w_proj_a, w_proj_b, w_out, w_ffn_gate, w_ffn_up, w_ffn_down))
    moments_m = row_shards((m_w_in, m_w_proj_a, m_w_proj_b, m_w_out, m_w_ffn_gate, m_w_ffn_up, m_w_ffn_down))
    moments_v = row_shards((v_w_in, v_w_proj_a, v_w_proj_b, v_w_out, v_w_ffn_gate, v_w_ffn_up, v_w_ffn_down))
    pos = jnp.stack([lax.axis_index("c"), 2 * lax.axis_index("x") + lax.axis_index("y")]).astype(jnp.int32)
    x2, target = x[0], loss_target[0]

    me = 4 * lax.axis_index("x") + 2 * lax.axis_index("y") + lax.axis_index("c")
    mid_names, ffn_names = names[1:4], names[4:]
    first_names, later_names = names[:1], names[1:]
    shards16 = {k: shards[k].astype(BF16) for k in names}

    def landing(k):
        return lax.dynamic_update_slice(lax.empty((N_DEV,) + shards[k].shape, BF16), shards16[k][None], (me, 0, 0))

    ag_first = _exchange_start("ag_first_chips_start", _gather_chips_copies, [shards16[k] for k in first_names],
                               [landing(k) for k in first_names], 3 * len(first_names))
    h = _rowwise(lambda xb, g: xb * _rms_scale(xb) * g, "norm_mix_pre", SEQ, 256, [(x2, d, 0)], [norm_mix_pre],
                 [(d, BF16)], deps=[ag_first.token])[0]
    later_lands = [landing(k) for k in later_names]
    _, lands = _exchange_wait("ag_first_chips_wait", ag_first,
                              [h, shards["w_in"], moments_m["w_in"], moments_v["w_in"], *later_lands])
    ag_first = _exchange_start("ag_first_sibling_start", _gather_sibling_copies, [], lands, 4 * len(first_names))
    ag_later = _exchange_start("ag_later_chips_start", _gather_chips_copies, [shards16[k] for k in later_names],
                               later_lands, 3 * len(later_names), after=[ag_first.token])
    gathered = dict(zip(first_names, _exchange_wait("ag_first_sibling_wait", ag_first, [ag_later.token])[1]))
    wt_r = _proj_weight_t(gathered["w_in"])

    proj = _matmul([(h, wt_r)], "nt", F32, "in_proj", 1024, 896, 1024)
    tables = _rope_tables()
    o_dil, lse_dil = _dil_fwd(proj, tables)
    out_a = _dil_combine(o_dil, lse_dil)
    _, lands = _exchange_wait("ag_later_chips_wait", ag_later, [out_a])
    ag_later = _exchange_start("ag_later_sibling_start", _gather_sibling_copies, [], lands, 4 * len(later_names))

    b_pad = jnp.pad(b_forget, ((0, 0), (0, LANES - N_FOX_HEADS)))
    f_rows = _fox_gate(proj, b_pad, deps=[ag_later.token])
    out_b, lse_fox = _fox_fwd(proj, f_rows)

    gathered = dict(zip(later_names, _exchange_wait("ag_later_sibling_wait", ag_later, [out_b])[1]))
    wt_pa = gathered["w_proj_a"].transpose(1, 0, 2).reshape(DIL_OUT_WIDTH, d)
    wt_pb = gathered["w_proj_b"].transpose(1, 0, 2).reshape(FOX_WIDTH, d)
    w_o = gathered["w_out"].reshape(d, d)
    wt_g = gathered["w_ffn_gate"].reshape(D_FF, d)
    wt_u = gathered["w_ffn_up"].reshape(D_FF, d)
    w_d = gathered["w_ffn_down"].reshape(D_FF, d)
    merged, mix, x1, h2 = _mix_out(out_a, out_b, proj, x2, wt_pa, wt_pb, w_o, norm_mix_post, norm_ffn_pre)

    gate, up, act = _ffn_up(h2, wt_g, wt_u)
    dy, dff, loss_part, dg_ffn_post = _loss_head(act, w_d, x1, target, norm_ffn_post)

    dgate, dup = _ffn_act_bwd(dff, w_d, gate, up)
    grads_t = {}
    grads_t["w_ffn_down"] = _matmul([(act, dff)], "tn", F32, "grad_w_ffn_down", 1408, 512, 2048)
    grads_t["w_ffn_gate"] = _matmul([(dgate, h2)], "tn", F32, "grad_w_ffn_gate", 1408, 512, 2048)
    grads_t["w_ffn_up"] = _matmul([(dup, h2)], "tn", F32, "grad_w_ffn_up", 1408, 512, 2048)
    rs_ffn = _ReduceScatter("ffn", {k: grads_t[k] for k in ffn_names}, pos)
    dx1, dmix, dg_ffn_pre, dg_mix_post = _post_ffn_bwd(dgate, wt_g, dup, wt_u, x1, dy, mix, norm_ffn_pre, norm_mix_post,
                                                       deps=[rs_ffn.token])
    rs_ffn.start_chips([dmix])

    dproj, dya, dyb, d_out_a, d_out_b = _mix_out_bwd(dmix, out_a, out_b, proj, wt_pa, wt_pb, w_o, deps=[rs_ffn.token])
    grads_t["w_out"] = _matmul([(merged, dmix)], "tn", F32, "grad_w_out", 1024, 1024, 1024)
    def column_slots(g):
        return g.reshape(g.shape[0], N_DEV, LANES).transpose(1, 0, 2)

    grads_t["w_proj_a"] = column_slots(_matmul([(out_a, dya)], "tn", F32, "grad_w_proj_a", DIL_OUT_WIDTH, 1024, SEQ))
    grads_t["w_proj_b"] = column_slots(_matmul([(out_b, dyb)], "tn", F32, "grad_w_proj_b", FOX_WIDTH, 1024, SEQ))
    rs_mid = _ReduceScatter("mid", {k: grads_t[k] for k in mid_names}, pos)

    do_dil, c_dil = _dil_combine_bwd(d_out_a, o_dil, lse_dil, deps=[rs_mid.token])
    rs_mid.start_chips([c_dil])
    dproj, d_cum = _fox_bwd(proj, d_out_b, lse_fox, f_rows, dproj)
    d_cum_rows = jnp.pad(d_cum[:, :2].reshape(N_FOX_HEADS, SEQ), ((0, F_ROWS - N_FOX_HEADS), (0, 0)))
    dproj, db_part = _fox_gate_bwd(d_cum_rows, proj, b_pad, dproj)
    dproj = _dil_bwd(proj, tables, do_dil, lse_dil, c_dil, dproj, deps=[rs_mid.token])

    dwt_r = _matmul([(dproj, h)], "tn", F32, "grad_w_in", 896, 1024, 2048)
    rs_in = _ReduceScatter("in", {"w_in": _proj_weight_grad_slots(dwt_r)}, pos)
    def finish(rs, after):
        return {k: _adam(shards[k], mine, recv, moments_m[k], moments_v[k], "adam_" + k)
                for k, (mine, recv) in rs.finish(after).items()}

    done = finish(rs_ffn, [rs_in.token])
    rs_in.start_chips([done[k][0] for k in ffn_names])
    grad_x, dg_mix_pre = _input_bwd(dproj, wt_r, x2, dx1, norm_mix_pre, deps=[rs_in.token])
    done.update(finish(rs_mid, [grad_x]))

    small_all = _all_gather([dg_mix_pre, dg_mix_post, dg_ffn_pre, dg_ffn_post, db_part, loss_part],
                            "small_grads_all_gather", deps=[done[k][0] for k in mid_names])
    small, loss = _adam_small(small_all[:5], [norm_mix_pre, norm_mix_post, norm_ffn_pre, norm_ffn_post, b_forget],
                              [m_norm_mix_pre, m_norm_mix_post, m_norm_ffn_pre, m_norm_ffn_post, m_b_forget],
                              [v_norm_mix_pre, v_norm_mix_post, v_norm_ffn_pre, v_norm_ffn_post, v_b_forget],
                              small_all[5])

    done.update(finish(rs_in, [small[0][0]]))

    def leaves(i):
        def nat(k):
            a = done[k][i]
            return (a.T if k in col_sharded else a)[None]

        return [nat("w_in"), nat("w_proj_a"), nat("w_proj_b"), nat("w_out"), small[4][i],
                nat("w_ffn_gate"), nat("w_ffn_up"), nat("w_ffn_down"), *[small[r][i] for r in range(4)]]

    return (loss[0, 0], grad_x[None], *leaves(0), *leaves(1), *leaves(2), *leaves(3))
```

```python
import functools
import math

import jax
import jax.numpy as jnp
import numpy as np
from jax import lax
from jax.experimental import pallas as pl
from jax.experimental.pallas import tpu as pltpu

F32 = jnp.float32
BF16 = jnp.bfloat16
MESH = pl.DeviceIdType.MESH

D_MODEL = 1024
SEQ = 2048
HEAD_DIM = 64
BLOCK = 128
N_BLOCKS = SEQ // BLOCK
DILATIONS = (1, 4, 16)
N_FOX_HEADS = 8
DIL_WIDTH = 768
DIL_OUT_WIDTH = 256
FOX_WIDTH = 512
D_FF = 2816
ROPE_THETA = 500000.0
ROPE_DIM = HEAD_DIM // 4
ROPE_HALF = ROPE_DIM // 2
EPS = 1e-6
NEG_INF = -1e30
QK_SCALE = 1.0 / math.sqrt(HEAD_DIM)
IN_COLS = 5896
N_DEV = 8
IN_SHARD = IN_COLS // N_DEV

ADAM_LR = 0.001
ADAM_B1 = 0.9
ADAM_B2 = 0.999
ADAM_EPS = 1e-08
ADAM_WD = 0.01
ADAM_STEP = 10

V7X_VMEM_BYTES = 64 * 2**20
LANES = 128
SUBLANES = 8

PROJ_COLS = 6272
COL_GA, COL_GB = 0, 1024
COL_QA, COL_KA, COL_VA = 2304, 3072, 3840
COL_QB, COL_KB, COL_VB = 4608, 5120, 5632
COL_F = 6144
F_ROWS = 16


def _vmem_limit(block_bytes):
    want = 2 * block_bytes + 16 * 2**20
    return int(min(max(want, 32 * 2**20), V7X_VMEM_BYTES - 8 * 2**20))


def _nbytes(shape, dtype):
    return math.prod(shape) * jnp.dtype(dtype).itemsize


def _in_hbm(*arrays):
    return [pltpu.with_memory_space_constraint(a, pltpu.HBM) for a in arrays]


def _dot(a, b, dims):
    return lax.dot_general(a, b, (dims, ((), ())), preferred_element_type=F32)


def _dot_nn(a, b):
    return _dot(a, b, ((1,), (0,)))


def _dot_nt(a, b):
    return _dot(a, b, ((1,), (1,)))


def _dot_tn(a, b):
    return _dot(a, b, ((0,), (0,)))


def _sigmoid(z):
    return 1.0 / (1.0 + jnp.exp(-z))


def _split3(x):
    hi = x.astype(BF16)
    r1 = x - hi.astype(F32)
    mid = r1.astype(BF16)
    lo = (r1 - mid.astype(F32)).astype(BF16)
    return hi, mid, lo


def _dot3_nn(x, ones_matrix):
    hi, mid, lo = _split3(x)
    return (_dot_nn(hi, ones_matrix) + _dot_nn(mid, ones_matrix)) + _dot_nn(lo, ones_matrix)


def _rowwise(fn, name, n_rows, tm, row_ins, bcast_ins, row_outs, acc_outs=(), deps=()):
    n_in = len(row_ins) + len(bcast_ins)
    n_ro = len(row_outs)

    def body(*refs):
        res = fn(*[r[...] for r in refs[:n_in]])
        if not isinstance(res, (tuple, list)):
            res = (res,)
        outs = refs[n_in + len(deps):]
        for r, o in zip(res[:n_ro], outs[:n_ro]):
            o[...] = r.astype(o.dtype)
        first = pl.program_id(0) == 0
        for r, o in zip(res[n_ro:], outs[n_ro:]):
            _accumulate(o, r, first)

    in_specs = [pl.BlockSpec((tm, w), lambda i, cb=cb: (i, cb)) for _, w, cb in row_ins]
    in_specs += [pl.BlockSpec(a.shape, lambda i: (0, 0)) for a in bcast_ins]
    in_specs += [pl.BlockSpec(memory_space=pl.ANY)] * len(deps)
    out_specs = [pl.BlockSpec((tm, w), lambda i: (i, 0)) for w, _ in row_outs]
    out_specs += [pl.BlockSpec((1, w), lambda i: (0, 0)) for w in acc_outs]
    out_shape = [pltpu.HBM((n_rows, w), dt) for w, dt in row_outs]
    out_shape += [pltpu.HBM((1, w), F32) for w in acc_outs]
    blk = sum(_nbytes((tm, w), a.dtype) for a, w, _ in row_ins) + sum(_nbytes((tm, w), dt) for w, dt in row_outs)
    return pl.pallas_call(
        body, name=name, grid=(n_rows // tm,), in_specs=in_specs, out_specs=out_specs, out_shape=out_shape,
        compiler_params=pltpu.CompilerParams(
            dimension_semantics=("arbitrary" if acc_outs else "parallel",), vmem_limit_bytes=_vmem_limit(3 * blk)),
    )(*_in_hbm(*[a for a, _, _ in row_ins], *bcast_ins), *deps)


def _accumulate(o_ref, part, first):
    @pl.when(first)
    def _():
        o_ref[...] = part

    @pl.when(jnp.logical_not(first))
    def _():
        o_ref[...] += part


_MM_DIMS = {"nn": ((1,), (0,)), "nt": ((1,), (1,)), "tn": ((0,), (0,))}


def _matmul(pairs, mode, out_dtype, name, tm, tn, tk, deps=()):
    a0, b0 = pairs[0]
    if mode == "tn":
        kk, m = a0.shape
    else:
        m, kk = a0.shape
    n = b0.shape[0] if mode == "nt" else b0.shape[1]
    assert m % tm == 0 and n % tn == 0 and kk % tk == 0, (name, m, n, kk)
    nk = kk // tk
    n_pairs = len(pairs)
    dims = _MM_DIMS[mode]
    n_in = 2 * n_pairs + len(deps)

    def body(*refs):
        o_ref = refs[n_in]
        part = None
        for p in range(n_pairs):
            d = _dot(refs[2 * p][...].astype(BF16), refs[2 * p + 1][...].astype(BF16), dims)
            part = d if part is None else part + d
        if nk == 1:
            o_ref[...] = part.astype(o_ref.dtype)
            return
        acc = refs[n_in + 1]
        k = pl.program_id(2)

        @pl.when(k == 0)
        def _():
            acc[...] = part

        @pl.when(k > 0)
        def _():
            acc[...] += part

        @pl.when(k == nk - 1)
        def _():
            o_ref[...] = acc[...].astype(o_ref.dtype)

    if mode == "tn":
        a_spec = pl.BlockSpec((tk, tm), lambda i, j, k: (k, i))
    else:
        a_spec = pl.BlockSpec((tm, tk), lambda i, j, k: (i, k))
    if mode == "nt":
        b_spec = pl.BlockSpec((tn, tk), lambda i, j, k: (j, k))
    else:
        b_spec = pl.BlockSpec((tk, tn), lambda i, j, k: (k, j))
    blk = sum(_nbytes((tm, tk), a.dtype) + _nbytes((tk, tn), b.dtype) for a, b in pairs) + 2 * _nbytes((tm, tn), F32)
    flat = [a for pair in pairs for a in pair]
    return pl.pallas_call(
        body, name=name, grid=(m // tm, n // tn, nk),
        in_specs=[a_spec, b_spec] * n_pairs + [pl.BlockSpec(memory_space=pl.ANY)] * len(deps),
        out_specs=pl.BlockSpec((tm, tn), lambda i, j, k: (i, j)),
        out_shape=pltpu.HBM((m, n), out_dtype),
        scratch_shapes=[] if nk == 1 else [pltpu.VMEM((tm, tn), F32)],
        compiler_params=pltpu.CompilerParams(
            dimension_semantics=("parallel", "parallel", "arbitrary"), vmem_limit_bytes=_vmem_limit(blk)),
    )(*flat, *deps)


def _matmul_rowwise(pairs, fn, name, tm, row_ins, bcast_ins, row_outs, acc_outs=(), deps=()):
    m = pairs[0][0].shape[0]
    n_mm, n_in = 2 * len(pairs), len(row_ins) + len(bcast_ins)
    n_ro = len(row_outs)

    def body(*refs):
        prod = None
        for p in range(len(pairs)):
            part = _dot_nn(refs[2 * p][...].astype(BF16), refs[2 * p + 1][...].astype(BF16))
            prod = part if prod is None else prod + part
        res = fn(prod, *[r[...] for r in refs[n_mm:n_mm + n_in]])
        outs = refs[n_mm + n_in + len(deps):]
        for r, o in zip(res[:n_ro], outs[:n_ro]):
            o[...] = r.astype(o.dtype)
        first = pl.program_id(0) == 0
        for r, o in zip(res[n_ro:], outs[n_ro:]):
            _accumulate(o, r, first)

    in_specs = []
    for a, b in pairs:
        in_specs += [pl.BlockSpec((tm, a.shape[1]), lambda i: (i, 0)),
                     pl.BlockSpec(b.shape, lambda i: (0, 0), pipeline_mode=pl.Buffered(1))]
    in_specs += [pl.BlockSpec((tm, w), lambda i, cb=cb: (i, cb)) for _, w, cb in row_ins]
    in_specs += [pl.BlockSpec(a.shape, lambda i: (0, 0)) for a in bcast_ins]
    in_specs += [_ANY] * len(deps)
    out_specs = [pl.BlockSpec((tm, w), lambda i: (i, 0)) for w, _ in row_outs]
    out_specs += [pl.BlockSpec((1, w), lambda i: (0, 0)) for w in acc_outs]
    out_shape = [pltpu.HBM((m, w), dt) for w, dt in row_outs]
    out_shape += [pltpu.HBM((1, w), F32) for w in acc_outs]
    blk = sum(_nbytes((tm, a.shape[1]), a.dtype) + _nbytes(b.shape, b.dtype) // 2 for a, b in pairs)
    blk += sum(_nbytes((tm, w), a.dtype) for a, w, _ in row_ins) + sum(_nbytes((tm, w), dt) for w, dt in row_outs)
    return pl.pallas_call(
        body, name=name, grid=(m // tm,), in_specs=in_specs, out_specs=out_specs, out_shape=out_shape,
        compiler_params=pltpu.CompilerParams(dimension_semantics=("arbitrary",), vmem_limit_bytes=_vmem_limit(blk)),
    )(*[a for pair in pairs for a in pair], *[a for a, _, _ in row_ins], *bcast_ins, *deps)


def _rms_scale(x):
    return lax.rsqrt(jnp.mean(x * x, axis=-1, keepdims=True) + EPS)


def _rms_bwd(xin, dyn, g):
    r = _rms_scale(xin)
    u = dyn * g
    dx = r * u - xin * (r * r * r) * jnp.mean(u * xin, axis=-1, keepdims=True)
    dg = jnp.sum(dyn * xin * r, axis=0, keepdims=True)
    return dx, dg


def _mesh_pos():
    return lax.axis_index("x"), lax.axis_index("y"), lax.axis_index("c")


def _all_gather(xs, name, deps=()):
    n = len(xs)

    def body(*refs):
        x_refs, out_refs = refs[:n], refs[n + len(deps):2 * n + len(deps)]
        send_sems, recv_sems, local_sems = refs[2 * n + len(deps):]
        mx, my, mc = _mesh_pos()
        me, sib = (mx, my, mc), (mx, my, 1 - mc)
        chips = [(1 - mx, my), (mx, 1 - my), (1 - mx, 1 - my)]

        def slot(a, dev):
            px, py, pc = dev
            return out_refs[a].at[4 * px + 2 * py + pc]

        def copy(k, a, block, to, src=None):
            return pltpu.make_async_remote_copy(
                src_ref=slot(a, block) if src is None else src, dst_ref=slot(a, block),
                send_sem=send_sems.at[a * 7 + k], recv_sem=recv_sems.at[a * 7 + k],
                device_id=to, device_id_type=MESH)

        mine = [pltpu.make_async_copy(x_refs[a], slot(a, me), local_sems.at[a]) for a in range(n)]
        for cp in mine:
            cp.start()
        first = []
        for a in range(n):
            first.append(copy(0, a, me, sib, x_refs[a]))
            first += [copy(1 + j, a, me, (*chip, mc), x_refs[a]) for j, chip in enumerate(chips)]
        for cp in first:
            cp.start()
        passed = []
        for a in range(n):
            for j, chip in enumerate(chips):
                copy(1 + j, a, (*chip, mc), me).wait_recv()
                fwd = copy(4 + j, a, (*chip, mc), sib)
                fwd.start()
                passed.append(fwd)
        for a in range(n):
            copy(0, a, sib, me).wait_recv()
            for j, chip in enumerate(chips):
                copy(4 + j, a, (*chip, 1 - mc), me).wait_recv()
        for cp in first + passed:
            cp.wait_send()
        for cp in mine:
            cp.wait()

    hbm = pl.BlockSpec(memory_space=pl.ANY)
    return pl.pallas_call(
        body, name=name,
        out_shape=[pltpu.HBM((N_DEV,) + x.shape, x.dtype) for x in xs],
        in_specs=[hbm] * (n + len(deps)), out_specs=[hbm] * n,
        scratch_shapes=[pltpu.SemaphoreType.DMA((7 * n,)), pltpu.SemaphoreType.DMA((7 * n,)),
                        pltpu.SemaphoreType.DMA((n,))],
    )(*xs, *deps)


_HBM = pl.BlockSpec(memory_space=pltpu.HBM)
_SEM = pl.BlockSpec(memory_space=pltpu.SEMAPHORE)
_ANY = pl.BlockSpec(memory_space=pl.ANY)
_DATAFLOW = pltpu.SideEffectType.DATAFLOW_SIDE_EFFECTING


def _flip_peer(flip):
    mx, my, mc = _mesh_pos()
    return (1 - mx if flip & 2 else mx, 1 - my if flip & 1 else my, mc)


def _remote(src, dst, send_sems, recv_sems, k, peer):
    return pltpu.make_async_remote_copy(src_ref=src, dst_ref=dst, send_sem=send_sems.at[k], recv_sem=recv_sems.at[k],
                                        device_id=peer, device_id_type=MESH)


def _gather_chips_copies(srcs, lands, send_sems, recv_sems):
    mx, my, mc = _mesh_pos()
    me = 4 * mx + 2 * my + mc
    return [_remote(srcs[a], lands[a].at[me], send_sems, recv_sems, 3 * a + flip - 1, _flip_peer(flip))
            for a in range(len(srcs)) for flip in (1, 2, 3)]


def _gather_sibling_copies(srcs, lands, send_sems, recv_sems):
    mx, my, mc = _mesh_pos()
    return [_remote(lands[a].at[2 * k + mc], lands[a].at[2 * k + mc], send_sems, recv_sems, 4 * a + k, (mx, my, 1 - mc))
            for a in range(len(lands)) for k in range(4)]


def _scatter_sibling_copies(srcs, lands, send_sems, recv_sems):
    mx, my, mc = _mesh_pos()
    return [_remote(srcs[a].at[k, 1 - mc], lands[a].at[k], send_sems, recv_sems, 4 * a + k, (mx, my, 1 - mc))
            for a in range(len(srcs)) for k in range(4)]


def _scatter_chips_copies(srcs, lands, send_sems, recv_sems):
    mx, my, _ = _mesh_pos()
    k0 = 2 * mx + my
    return [_remote(srcs[a].at[jnp.bitwise_xor(k0, flip)], lands[a].at[flip - 1], send_sems, recv_sems,
                    3 * a + flip - 1, _flip_peer(flip))
            for a in range(len(srcs)) for flip in (1, 2, 3)]


class _Exchange:
    def __init__(self, copies, n_src, send_sems, recv_sems, thru, token):
        self.copies, self.n_src, self.send_sems, self.recv_sems, self.thru, self.token = (
            copies, n_src, send_sems, recv_sems, thru, token)


def _exchange_start(name, copies, srcs, lands, n_copies, after=()):
    bufs = list(srcs) + list(lands)
    nb, ns = len(bufs), len(srcs)

    def body(*refs):
        send_sems, recv_sems = refs[nb + len(after)], refs[nb + len(after) + 1]
        for cp in copies(refs[:ns], refs[ns:nb], send_sems, recv_sems):
            cp.start()
        refs[-1][...] = jnp.zeros_like(refs[-1])

    out = pl.pallas_call(
        body, name=name,
        out_shape=(pltpu.SemaphoreType.DMA((n_copies,)), pltpu.SemaphoreType.DMA((n_copies,)),
                   *[pltpu.HBM(b.shape, b.dtype) for b in bufs], pltpu.HBM((SUBLANES, LANES), F32)),
        in_specs=[_HBM] * nb + [_ANY] * len(after),
        out_specs=(_SEM, _SEM, *[_HBM] * nb, pl.BlockSpec(memory_space=pltpu.VMEM)),
        input_output_aliases={i: 2 + i for i in range(nb)},
        compiler_params=pltpu.CompilerParams(has_side_effects=_DATAFLOW),
    )(*[pltpu.with_memory_space_constraint(b, pltpu.HBM) for b in bufs], *after)
    return _Exchange(copies, ns, out[0], out[1], list(out[2:2 + nb]), out[-1])


def _exchange_wait(name, ex, after):
    nb, ns = len(ex.thru), ex.n_src

    def body(*refs):
        for cp in ex.copies(refs[:ns], refs[ns:nb], refs[nb], refs[nb + 1]):
            cp.wait_send()
            cp.wait_recv()

    out = pl.pallas_call(
        body, name=name, out_shape=tuple(pltpu.HBM(b.shape, b.dtype) for b in ex.thru),
        in_specs=[_HBM] * nb + [_SEM, _SEM] + [_ANY] * len(after), out_specs=tuple([_HBM] * nb),
        input_output_aliases={i: i for i in range(nb)},
        compiler_params=pltpu.CompilerParams(has_side_effects=_DATAFLOW),
    )(*ex.thru, ex.send_sems, ex.recv_sems, *after)
    return list(out[:ns]), list(out[ns:])


def _col_tile(r, c):
    return next(t for t in (1024, 512, 256, 128) if c % t == 0 and (r * t * 4 <= 2**20 or t == 128))


def _add_sibling(g4, recv, pos, name):
    _, _, r, c = g4.shape
    tc = _col_tile(r, c)

    def body(pos_ref, g_ref, r_ref, o16_ref, mine_ref):
        s = g_ref[0, 0] + r_ref[0]
        o16_ref[0] = s.astype(BF16)

        @pl.when(pl.program_id(1) == pos_ref[1])
        def _():
            mine_ref[...] = s

    slot = pl.BlockSpec((1, r, tc), lambda j, k, pos_ref: (k, 0, j))
    return pl.pallas_call(
        body, name=name,
        out_shape=[pltpu.HBM((4, r, c), BF16), pltpu.HBM((r, c), F32)],
        grid_spec=pltpu.PrefetchScalarGridSpec(
            num_scalar_prefetch=1, grid=(c // tc, 4),
            in_specs=[pl.BlockSpec((1, 1, r, tc), lambda j, k, pos_ref: (k, pos_ref[0], 0, j)), slot],
            out_specs=[slot, pl.BlockSpec((r, tc), lambda j, k, pos_ref: (0, j))]),
        compiler_params=pltpu.CompilerParams(dimension_semantics=("parallel", "arbitrary")),
    )(pos, *_in_hbm(g4, recv))


class _ReduceScatter:
    def __init__(self, tag, grads_t, pos):
        self.tag, self.pos, self.names = tag, pos, list(grads_t)
        g4s = [g.reshape(4, 2, g.size // (N_DEV * g.shape[-1]), g.shape[-1]) for g in grads_t.values()]
        lands = [lax.empty((4,) + g.shape[2:], F32) for g in g4s]
        self.ex = _exchange_start(f"rs_{tag}_sibling_start", _scatter_sibling_copies, g4s, lands, 4 * len(g4s))
        self.token = self.ex.token

    def start_chips(self, after):
        g4s, from_sibling = _exchange_wait(f"rs_{self.tag}_sibling_wait", self.ex, after)
        parts = [_add_sibling(g4, rv, self.pos, f"rs_add_sibling_{k}")
                 for k, g4, rv in zip(self.names, g4s, from_sibling)]
        self.mine = [mine for _, mine in parts]
        p16s = [p16 for p16, _ in parts]
        lands = [lax.empty((3,) + p.shape[1:], BF16) for p in p16s]
        self.ex = _exchange_start(f"rs_{self.tag}_chips_start", _scatter_chips_copies, p16s, lands, 3 * len(p16s))
        self.token = self.ex.token

    def finish(self, after):
        _, from_chips = _exchange_wait(f"rs_{self.tag}_chips_wait", self.ex, after)
        return dict(zip(self.names, zip(self.mine, from_chips)))


def _rope_tables():
    positions = np.arange(SEQ, dtype=np.float32)
    inv_freq = np.power(np.float32(ROPE_THETA), -np.arange(0, ROPE_DIM, 2, dtype=np.float32) / np.float32(ROPE_DIM))
    ang = (positions[:, None] * inv_freq[None, :]).astype(np.float32)
    cos, sin = np.cos(ang).astype(np.float32), np.sin(ang).astype(np.float32)
    ones = np.ones((SEQ, HEAD_DIM - ROPE_DIM), np.float32)
    zeros8 = np.zeros((SEQ, ROPE_HALF), np.float32)
    zeros = np.zeros((SEQ, HEAD_DIM - ROPE_DIM), np.float32)
    c_head = np.concatenate([cos, cos, ones], axis=1)
    s1_head = np.concatenate([-sin, zeros8, zeros], axis=1)
    s2_head = np.concatenate([zeros8, sin, zeros], axis=1)
    return tuple(jnp.asarray(np.concatenate([t, t], axis=1)) for t in (c_head, s1_head, s2_head))


def _rope_apply(x, c, s1, s2):
    w = x.shape[1]
    return x * c + pltpu.roll(x, w - ROPE_HALF, 1) * s1 + pltpu.roll(x, ROPE_HALF, 1) * s2


def _rope_apply_t(dy, c, s1, s2):
    w = dy.shape[1]
    return dy * c + pltpu.roll(dy * s1, ROPE_HALF, 1) + pltpu.roll(dy * s2, w - ROPE_HALF, 1)


def _dil_prev_limit(has_prev):
    return jnp.where(has_prev, 0, BLOCK)


def _dil_valid(limit):
    row = lax.broadcasted_iota(jnp.int32, (BLOCK, 2 * BLOCK), 0)
    col = lax.broadcasted_iota(jnp.int32, (BLOCK, 2 * BLOCK), 1)
    dist = col - row
    return jnp.logical_and(dist >= jnp.where(col < BLOCK, limit, -BLOCK), dist <= BLOCK)


def _upper_half():
    return lax.broadcasted_iota(jnp.int32, (1, LANES), 1) >= HEAD_DIM


def _stack_heads(x):
    upper = _upper_half()
    return jnp.concatenate([jnp.where(upper, 0, x), jnp.where(upper, x, 0)], axis=0)


def _unstack_heads(y):
    n = y.shape[0] // 2
    return jnp.where(_upper_half(), y[n:], y[:n])


def _head_columns(t):
    return jnp.concatenate([t[:, 0:1], t[:, HEAD_DIM:HEAD_DIM + 1]], axis=0)


def _dil_rows(n, d):
    per = N_BLOCKS // d
    r, lb = n // per, n % per

    def rows(b):
        start = b * (BLOCK * d) + r
        return pl.ds(pl.multiple_of(start, BLOCK), BLOCK) if d == 1 else pl.ds(start, BLOCK, stride=d)

    return rows(lb), rows(jnp.maximum(lb - 1, 0)), lb > 0


def _dil_rotate(q_ref, k_ref, c_ref, s1_ref, s2_ref, q_rot, k_rot):
    tabs = (c_ref[...], s1_ref[...], s2_ref[...])
    q_rot[...] = _rope_apply(q_ref[...], *tabs) * QK_SCALE
    k_rot[...] = _rope_apply(k_ref[...], *tabs)


def _dil_specs():
    def col(base):
        return pl.BlockSpec((SEQ, LANES), lambda p: (0, base // LANES + p))

    table = pl.BlockSpec((SEQ, LANES), lambda p: (0, 0))
    return [col(COL_QA), col(COL_KA), col(COL_VA)], [table] * 3


def _store_columns(blocks, dproj_ref, cols, sem):
    copies = [pltpu.make_async_copy(b, dproj_ref.at[:, pl.ds(pl.multiple_of(c * LANES, LANES), LANES)], sem.at[i])
              for i, (b, c) in enumerate(zip(blocks, cols))]
    for cp in copies:
        cp.start()
    for cp in copies:
        cp.wait()


def _dil_window(d, n, k_rot, v_ref):
    rows, prev, has_prev = _dil_rows(n, d)
    kw, vw = k_rot[rows, :].astype(BF16), v_ref[rows, :].astype(BF16)
    if d == N_BLOCKS:
        row = lax.broadcasted_iota(jnp.int32, (BLOCK, BLOCK), 0)
        valid = lax.broadcasted_iota(jnp.int32, (BLOCK, BLOCK), 1) <= row
    else:
        kw = jnp.concatenate([k_rot[prev, :].astype(BF16), kw], axis=0)
        vw = jnp.concatenate([v_ref[prev, :].astype(BF16), vw], axis=0)
        valid = _dil_valid(_dil_prev_limit(has_prev))
    return rows, prev, kw, vw, jnp.concatenate([valid, valid], axis=0)


def _dil_fwd(proj, tables):
    def body(q_ref, k_ref, v_ref, c_ref, s1_ref, s2_ref, o_ref, lse_ref, q_rot, k_rot):
        upper = _upper_half()
        _dil_rotate(q_ref, k_ref, c_ref, s1_ref, s2_ref, q_rot, k_rot)

        def blocks_of(d):
            def block(n, carry):
                rows, _, kw, vw, valid = _dil_window(d, n, k_rot, v_ref)
                s = jnp.where(valid, _dot_nt(_stack_heads(q_rot[rows, :].astype(BF16)), kw), NEG_INF)
                m = jnp.max(s, axis=-1, keepdims=True)
                p = jnp.exp(s - m)
                den = jnp.sum(p, axis=-1, keepdims=True)
                o_ref[rows, :] = _unstack_heads(_dot_nn((p * (1.0 / den)).astype(BF16), vw))
                lse = m + jnp.log(den)
                lse_ref[rows, :] = jnp.where(upper, lse[BLOCK:], lse[:BLOCK])
                return carry

            lax.fori_loop(0, N_BLOCKS, block, 0, unroll=4)

        for g, d in enumerate(DILATIONS):
            pl.when(pl.program_id(0) // 2 == g)(functools.partial(blocks_of, d))

    qkv, tabs = _dil_specs()
    out = pl.BlockSpec((SEQ, LANES), lambda p: (0, p))
    return pl.pallas_call(
        body, name="dil_attn_fwd", grid=(DIL_WIDTH // LANES,), in_specs=qkv + tabs, out_specs=[out, out],
        out_shape=[pltpu.HBM((SEQ, DIL_WIDTH), F32)] * 2,
        scratch_shapes=[pltpu.VMEM((SEQ, LANES), F32)] * 2,
        compiler_params=pltpu.CompilerParams(dimension_semantics=("parallel",)),
    )(*_in_hbm(proj, proj, proj, *tables))


def _dil_bwd(proj, tables, do, lse, c, dproj, deps=()):
    def body(q_ref, k_ref, v_ref, c_ref, s1_ref, s2_ref, do_ref, lse_ref, cc_ref, dproj_in, *rest):
        dproj_ref, dq_acc, dk_acc, dv_acc, dq_out, dk_out, dv_out, q_rot, k_rot, sem = rest[len(deps):]
        dk_acc[...] = jnp.zeros_like(dk_acc)
        dv_acc[...] = jnp.zeros_like(dv_acc)
        _dil_rotate(q_ref, k_ref, c_ref, s1_ref, s2_ref, q_rot, k_rot)

        def blocks_of(d):
            def block(n, carry):
                rows, prev, kw, vw, valid = _dil_window(d, n, k_rot, v_ref)
                q2 = _stack_heads(q_rot[rows, :].astype(BF16))
                do2 = _stack_heads(do_ref[rows, :].astype(BF16))
                lse_col, c_col = _head_columns(lse_ref[rows, :]), _head_columns(cc_ref[rows, :])
                p = jnp.where(valid, jnp.exp(_dot_nt(q2, kw) - lse_col), 0.0)
                ds = (p * (_dot_nt(do2, vw) + c_col)).astype(BF16)
                dk, dv = _dot_tn(ds, q2), _dot_tn(p.astype(BF16), do2)
                dq_acc[rows, :] = _unstack_heads(_dot_nn(ds, kw)) * QK_SCALE
                if d == N_BLOCKS:
                    dk_acc[rows, :] += dk
                    dv_acc[rows, :] += dv
                else:
                    dk_acc[prev, :] += dk[:BLOCK]
                    dv_acc[prev, :] += dv[:BLOCK]
                    dk_acc[rows, :] += dk[BLOCK:]
                    dv_acc[rows, :] += dv[BLOCK:]
                return carry

            lax.fori_loop(0, N_BLOCKS, block, 0, unroll=4)

        pair = pl.program_id(0)
        for g, d in enumerate(DILATIONS):
            pl.when(pair // 2 == g)(functools.partial(blocks_of, d))
        tabs = (c_ref[...], s1_ref[...], s2_ref[...])
        dq_out[...] = _rope_apply_t(dq_acc[...], *tabs).astype(BF16)
        dk_out[...] = _rope_apply_t(dk_acc[...], *tabs).astype(BF16)
        dv_out[...] = dv_acc[...].astype(BF16)
        _store_columns((dq_out, dk_out, dv_out), dproj_ref,
                       [base // LANES + pair for base in (COL_QA, COL_KA, COL_VA)], sem)

    qkv, tabs = _dil_specs()
    tok = pl.BlockSpec((SEQ, LANES), lambda p: (0, p))
    return pl.pallas_call(
        body, name="dil_attn_bwd", grid=(DIL_WIDTH // LANES,),
        in_specs=qkv + tabs + [tok, tok, tok, _ANY] + [_ANY] * len(deps), out_specs=_ANY,
        out_shape=pltpu.HBM(dproj.shape, dproj.dtype),
        scratch_shapes=[pltpu.VMEM((SEQ, LANES), F32)] * 3 + [pltpu.VMEM((SEQ, LANES), BF16)] * 3
        + [pltpu.VMEM((SEQ, LANES), F32)] * 2 + [pltpu.SemaphoreType.DMA((3,))],
        input_output_aliases={9: 0},
        compiler_params=pltpu.CompilerParams(dimension_semantics=("arbitrary",)),
    )(*_in_hbm(proj, proj, proj, *tables, do, lse, c, dproj), *deps)


def _group_weights(l0, l1, l2):
    m = jnp.maximum(jnp.maximum(l0, l1), l2)
    e0, e1, e2 = jnp.exp(l0 - m), jnp.exp(l1 - m), jnp.exp(l2 - m)
    tot = e0 + e1 + e2
    return e0 / tot, e1 / tot, e2 / tot


def _dil_combine(o, lse, deps=()):
    def fn(o0, o1, o2, l0, l1, l2):
        w0, w1, w2 = _group_weights(l0, l1, l2)
        return w0 * o0 + w1 * o1 + w2 * o2

    w = DIL_OUT_WIDTH
    return _rowwise(fn, "dil_combine", SEQ, 512, [(o, w, g) for g in range(3)] + [(lse, w, g) for g in range(3)], [],
                    [(w, F32)], deps=deps)[0]


def _dil_combine_bwd(d_out, o, lse, deps=()):
    w = DIL_OUT_WIDTH

    def fn(d, o0, o1, o2, l0, l1, l2):
        row = lax.broadcasted_iota(jnp.int32, (w, w), 0) // HEAD_DIM
        col = lax.broadcasted_iota(jnp.int32, (w, w), 1) // HEAD_DIM
        same_head = jnp.where(row == col, 1.0, 0.0).astype(BF16)
        ws = _group_weights(l0, l1, l2)
        dws = [_dot3_nn(d * og, same_head) for og in (o0, o1, o2)]
        mean = ws[0] * dws[0] + ws[1] * dws[1] + ws[2] * dws[2]
        return jnp.concatenate([wg * d for wg in ws], axis=1), jnp.concatenate([-wg * mean for wg in ws], axis=1)

    return _rowwise(fn, "dil_combine_bwd", SEQ, 256,
                    [(d_out, w, 0)] + [(o, w, g) for g in range(3)] + [(lse, w, g) for g in range(3)], [],
                    [(DIL_WIDTH, F32)] * 2, deps=deps)


def _log1p(e):
    u = 1.0 + e
    return jnp.where(u == 1.0, e, jnp.log(u) * (e / (u - 1.0)))


def _fox_gate(proj, b_pad, deps=()):
    def body(f_ref, b_ref, *rest):
        o_ref = rest[-1]
        z = f_ref[...] + b_ref[...]
        logf = (jnp.minimum(z, 0.0) - _log1p(jnp.exp(-jnp.abs(z)))).T[:F_ROWS]
        row = lax.broadcasted_iota(jnp.int32, (BLOCK, BLOCK), 0)
        col = lax.broadcasted_iota(jnp.int32, (BLOCK, BLOCK), 1)
        before = jnp.where(row <= col, 1.0, 0.0).astype(BF16)
        carry = jnp.zeros((F_ROWS, 1), F32)
        for blk in range(N_BLOCKS):
            run = _dot3_nn(logf[:, blk * BLOCK:(blk + 1) * BLOCK], before) + carry
            o_ref[:, blk * BLOCK:(blk + 1) * BLOCK] = run
            carry = run[:, BLOCK - 1:BLOCK]

    return pl.pallas_call(
        body, name="fox_gate", grid=(1,),
        in_specs=[pl.BlockSpec((SEQ, LANES), lambda i: (0, COL_F // LANES)), pl.BlockSpec((1, LANES), lambda i: (0, 0))]
        + [_ANY] * len(deps),
        out_specs=pl.BlockSpec((F_ROWS, SEQ), lambda i: (0, 0)),
        out_shape=pltpu.HBM((F_ROWS, SEQ), F32),
    )(*_in_hbm(proj, b_pad), *deps)


def _fox_gate_bwd(d_cum, proj, b_pad, dproj):
    def body(d_ref, f_ref, b_ref, dproj_ref, dz_ref, db_ref):
        row = lax.broadcasted_iota(jnp.int32, (BLOCK, BLOCK), 0)
        col = lax.broadcasted_iota(jnp.int32, (BLOCK, BLOCK), 1)
        after = jnp.where(row >= col, 1.0, 0.0).astype(BF16)
        carry = jnp.zeros((F_ROWS, 1), F32)
        parts = [None] * N_BLOCKS
        for blk in reversed(range(N_BLOCKS)):
            run = _dot3_nn(d_ref[:, blk * BLOCK:(blk + 1) * BLOCK], after) + carry
            parts[blk] = run
            carry = run[:, 0:1]
        dlogf = jnp.concatenate(parts, axis=1)
        dlogf = jnp.concatenate([dlogf, jnp.zeros((LANES - F_ROWS, SEQ), F32)], axis=0).T
        dz = dlogf * _sigmoid(-(f_ref[...] + b_ref[...]))
        dz_ref[...] = dz.astype(BF16)
        db_ref[...] = jnp.sum(dz, axis=0, keepdims=True)

    f_cols = pl.BlockSpec((SEQ, LANES), lambda i: (0, COL_F // LANES))
    return pl.pallas_call(
        body, name="fox_gate_bwd", grid=(1,),
        in_specs=[pl.BlockSpec((F_ROWS, SEQ), lambda i: (0, 0)), f_cols, pl.BlockSpec((1, LANES), lambda i: (0, 0)), _ANY],
        out_specs=[f_cols, pl.BlockSpec((1, LANES), lambda i: (0, 0))],
        out_shape=[pltpu.HBM(dproj.shape, dproj.dtype), pltpu.HBM((1, LANES), F32)],
        input_output_aliases={3: 0},
    )(*_in_hbm(d_cum, proj, b_pad, dproj))


FOX_TILE = 256
FOX_TILES = SEQ // FOX_TILE


def _row_to_col(row):
    n = row.shape[1]
    eye = lax.broadcasted_iota(jnp.int32, (n, n), 0) == lax.broadcasted_iota(jnp.int32, (n, n), 1)
    return jnp.sum(jnp.where(eye, row, 0.0), axis=1, keepdims=True)


def _fox_bias(f_row, i):
    t = FOX_TILE
    ext = (i + 1) * t
    bias = _row_to_col(f_row[:, i * t:(i + 1) * t]) - f_row[:, :ext]
    row = lax.broadcasted_iota(jnp.int32, (t, ext), 0) + i * t
    col = lax.broadcasted_iota(jnp.int32, (t, ext), 1)
    return bias, col <= row


def _fox_specs():
    qkv = [pl.BlockSpec((SEQ, LANES), lambda p, base=base: (0, base // LANES + p)) for base in (COL_QB, COL_KB, COL_VB)]
    return qkv, pl.BlockSpec((F_ROWS, SEQ), lambda p: (0, 0))


def _fox_fwd(proj, f_rows):
    t = FOX_TILE

    def body(q_ref, k_ref, v_ref, f_ref, o_ref, lse_ref):
        pair = pl.program_id(0)
        upper = _upper_half()
        k16, v16 = k_ref[...].astype(BF16), v_ref[...].astype(BF16)
        f_row = [f_ref[pl.ds(2 * pair + e, 1), :] for e in range(2)]
        for i in range(FOX_TILES):
            ext = (i + 1) * t
            q_tile = (q_ref[i * t:(i + 1) * t, :] * QK_SCALE).astype(BF16)
            s2 = _dot_nt(_stack_heads(q_tile), k16[:ext])
            pns, lses = [], []
            for e in range(2):
                bias, causal = _fox_bias(f_row[e], i)
                s = jnp.where(causal, s2[e * t:(e + 1) * t] + bias, NEG_INF)
                m = jnp.max(s, axis=-1, keepdims=True)
                p = jnp.exp(s - m)
                den = jnp.sum(p, axis=-1, keepdims=True)
                pns.append((p * (1.0 / den)).astype(BF16))
                lses.append(m + jnp.log(den))
            o_ref[i * t:(i + 1) * t, :] = _unstack_heads(_dot_nn(jnp.concatenate(pns, axis=0), v16[:ext]))
            lse_ref[i * t:(i + 1) * t, :] = jnp.where(upper, lses[1], lses[0])

    qkv, f_spec = _fox_specs()
    tok = pl.BlockSpec((SEQ, LANES), lambda p: (0, p))
    return pl.pallas_call(
        body, name="fox_attn_fwd", grid=(FOX_WIDTH // LANES,),
        in_specs=qkv + [f_spec], out_specs=[tok, tok],
        out_shape=[pltpu.HBM((SEQ, FOX_WIDTH), F32)] * 2,
        compiler_params=pltpu.CompilerParams(
            dimension_semantics=("parallel",), vmem_limit_bytes=_vmem_limit(8 * t * SEQ * 4)),
    )(*_in_hbm(proj, proj, proj, f_rows))


def _fox_bwd(proj, do, lse, f_rows, dproj):
    t = FOX_TILE

    def body(q_ref, k_ref, v_ref, f_ref, do_ref, lse_ref, dproj_in, dproj_ref, df_ref, dk_acc, dv_acc,
             dq_out, dk_out, dv_out, sem):
        pair = pl.program_id(0)
        upper = _upper_half()
        k16, v16 = k_ref[...].astype(BF16), v_ref[...].astype(BF16)
        f_row = [f_ref[pl.ds(2 * pair + e, 1), :] for e in range(2)]
        dk_acc[...] = jnp.zeros_like(dk_acc)
        dv_acc[...] = jnp.zeros_like(dv_acc)
        df_ref[...] = jnp.zeros_like(df_ref)
        for i in range(FOX_TILES):
            ext = (i + 1) * t
            q_tile = (q_ref[i * t:(i + 1) * t, :] * QK_SCALE).astype(BF16)
            do_tile = do_ref[i * t:(i + 1) * t, :]
            lse_t = lse_ref[i * t:(i + 1) * t, :]
            q2, do2 = _stack_heads(q_tile), _stack_heads(do_tile)
            s2, dp2 = _dot_nt(q2, k16[:ext]), _dot_nt(do2, v16[:ext])
            ps, dss = [], []
            for e in range(2):
                bias, causal = _fox_bias(f_row[e], i)
                s = s2[e * t:(e + 1) * t] + bias
                p = jnp.where(causal, jnp.exp(s - lse_t[:, e * HEAD_DIM:e * HEAD_DIM + 1]), 0.0)
                dp = dp2[e * t:(e + 1) * t]
                ds = p * (dp - jnp.sum(p * dp, axis=-1, keepdims=True))
                df_ref[0, e:e + 1, :ext] -= jnp.sum(ds, axis=0, keepdims=True)
                ps.append(p.astype(BF16))
                dss.append(ds.astype(BF16))
            ds2, p2 = jnp.concatenate(dss, axis=0), jnp.concatenate(ps, axis=0)
            dq_out[i * t:(i + 1) * t, :] = (_unstack_heads(_dot_nn(ds2, k16[:ext])) * QK_SCALE).astype(BF16)
            dk_acc[:ext, :] += _dot_tn(ds2, q2)
            dv_acc[:ext, :] += _dot_tn(p2, do2)
        dk_out[...] = dk_acc[...].astype(BF16)
        dv_out[...] = dv_acc[...].astype(BF16)
        _store_columns((dq_out, dk_out, dv_out), dproj_ref, [base // LANES + pair for base in (COL_QB, COL_KB, COL_VB)],
                       sem)

    qkv, f_spec = _fox_specs()
    tok = pl.BlockSpec((SEQ, LANES), lambda p: (0, p))
    return pl.pallas_call(
        body, name="fox_attn_bwd", grid=(FOX_WIDTH // LANES,),
        in_specs=qkv + [f_spec, tok, tok, _ANY],
        out_specs=[_ANY, pl.BlockSpec((1, SUBLANES, SEQ), lambda p: (p, 0, 0))],
        out_shape=[pltpu.HBM(dproj.shape, dproj.dtype),
                   pltpu.HBM((FOX_WIDTH // LANES, SUBLANES, SEQ), F32)],
        scratch_shapes=[pltpu.VMEM((SEQ, LANES), F32)] * 2 + [pltpu.VMEM((SEQ, LANES), BF16)] * 3
        + [pltpu.SemaphoreType.DMA((3,))],
        input_output_aliases={6: 0},
        compiler_params=pltpu.CompilerParams(
            dimension_semantics=("arbitrary",), vmem_limit_bytes=_vmem_limit(10 * t * SEQ * 4)),
    )(*_in_hbm(proj, proj, proj, f_rows, do, lse, dproj))


MIX_TILE = 256


def _mix_out(out_a, out_b, proj, x, wt_pa, wt_pb, w_out, g_post, g_ffn_pre):
    tm = MIX_TILE

    def body(a_ref, b_ref, ga_ref, gb_ref, x_ref, wpa_ref, wpb_ref, wo_ref, g2_ref, g3_ref,
             merged_ref, mix_ref, x1_ref, h2_ref):
        ya = _dot_nn(a_ref[...].astype(BF16), wpa_ref[...])
        yb = _dot_nn(b_ref[...].astype(BF16), wpb_ref[...])
        merged = (_sigmoid(ga_ref[...]) * ya + _sigmoid(gb_ref[...]) * yb).astype(BF16)
        merged_ref[...] = merged
        mix = _dot_nn(merged, wo_ref[...])
        mix_ref[...] = mix
        x1 = x_ref[...] + mix * _rms_scale(mix) * g2_ref[...]
        x1_ref[...] = x1
        h2_ref[...] = (x1 * _rms_scale(x1) * g3_ref[...]).astype(BF16)

    def rows(w, cb=0):
        return pl.BlockSpec((tm, w), lambda i, cb=cb: (i, cb))

    def whole(a):
        return pl.BlockSpec(a.shape, lambda i: (0, 0))

    d = D_MODEL
    blk = _nbytes((tm, d), F32) * 6 + sum(_nbytes(a.shape, BF16) for a in (wt_pa, wt_pb, w_out))
    return pl.pallas_call(
        body, name="mix_out", grid=(SEQ // tm,),
        in_specs=[rows(DIL_OUT_WIDTH), rows(FOX_WIDTH), rows(d, COL_GA // d), rows(d, COL_GB // d), rows(d),
                  whole(wt_pa), whole(wt_pb), whole(w_out), whole(g_post), whole(g_ffn_pre)],
        out_specs=[rows(d)] * 4,
        out_shape=[pltpu.HBM((SEQ, d), dt) for dt in (BF16, F32, F32, BF16)],
        compiler_params=pltpu.CompilerParams(dimension_semantics=("parallel",), vmem_limit_bytes=_vmem_limit(blk)),
    )(out_a, out_b, proj, proj, x, wt_pa, wt_pb, w_out, g_post, g_ffn_pre)


def _mix_out_bwd(dmix, out_a, out_b, proj, wt_pa, wt_pb, w_out, deps=()):
    tm = MIX_TILE

    def body(dm_ref, a_ref, b_ref, ga_ref, gb_ref, wpa_ref, wpb_ref, wo_ref, *rest):
        dproj_ref, dya_ref, dyb_ref, da_ref, db_ref = rest[len(deps):]
        dmerged = _dot_nt(dm_ref[...], wo_ref[...])
        ya = _dot_nn(a_ref[...].astype(BF16), wpa_ref[...])
        yb = _dot_nn(b_ref[...].astype(BF16), wpb_ref[...])
        sa, sb = _sigmoid(ga_ref[...]), _sigmoid(gb_ref[...])
        dproj_ref[:, COL_GA:COL_GA + D_MODEL] = (dmerged * ya * (sa * (1.0 - sa))).astype(BF16)
        dproj_ref[:, COL_GB:COL_GB + D_MODEL] = (dmerged * yb * (sb * (1.0 - sb))).astype(BF16)
        dproj_ref[:, COL_GB + D_MODEL:] = jnp.zeros((tm, COL_QA - COL_GB - D_MODEL), BF16)
        dya = (dmerged * sa).astype(BF16)
        dyb = (dmerged * sb).astype(BF16)
        dya_ref[...] = dya
        dyb_ref[...] = dyb
        da_ref[...] = _dot_nt(dya, wpa_ref[...])
        db_ref[...] = _dot_nt(dyb, wpb_ref[...]).astype(BF16)

    def rows(w, cb=0):
        return pl.BlockSpec((tm, w), lambda i, cb=cb: (i, cb))

    def whole(a):
        return pl.BlockSpec(a.shape, lambda i: (0, 0))

    d = D_MODEL
    blk = _nbytes((tm, d), F32) * 8 + sum(_nbytes(a.shape, BF16) for a in (wt_pa, wt_pb, w_out))
    return pl.pallas_call(
        body, name="mix_out_bwd", grid=(SEQ // tm,),
        in_specs=[rows(d), rows(DIL_OUT_WIDTH), rows(FOX_WIDTH), rows(d, COL_GA // d), rows(d, COL_GB // d),
                  whole(wt_pa), whole(wt_pb), whole(w_out)] + [_ANY] * len(deps),
        out_specs=[rows(COL_QA)] + [rows(d)] * 2 + [rows(DIL_OUT_WIDTH), rows(FOX_WIDTH)],
        out_shape=[pltpu.HBM((SEQ, PROJ_COLS), BF16)] + [pltpu.HBM((SEQ, d), BF16)] * 2
        + [pltpu.HBM((SEQ, DIL_OUT_WIDTH), F32), pltpu.HBM((SEQ, FOX_WIDTH), BF16)],
        compiler_params=pltpu.CompilerParams(dimension_semantics=("parallel",), vmem_limit_bytes=_vmem_limit(blk)),
    )(dmix, out_a, out_b, proj, proj, wt_pa, wt_pb, w_out, *deps)


FFN_TM, FFN_TN = 2048, 256


def _ffn_up(h2, wt_gate, wt_up):
    tm, tn = FFN_TM, FFN_TN

    def body(h_ref, wg_ref, wu_ref, gate_ref, up_ref, act_ref):
        for rows in (slice(0, tm // 2), slice(tm // 2, tm)):
            gate = _dot_nt(h_ref[rows, :], wg_ref[...])
            up = _dot_nt(h_ref[rows, :], wu_ref[...])
            gate_ref[rows, :] = gate
            up_ref[rows, :] = up
            act_ref[rows, :] = (gate * _sigmoid(gate) * up).astype(BF16)

    tile = pl.BlockSpec((tm, tn), lambda i, j: (i, j))
    w_spec = pl.BlockSpec((tn, D_MODEL), lambda i, j: (j, 0))
    return pl.pallas_call(
        body, name="ffn_up", grid=(SEQ // tm, D_FF // tn),
        in_specs=[pl.BlockSpec((tm, D_MODEL), lambda i, j: (i, 0)), w_spec, w_spec],
        out_specs=[tile, tile, tile],
        out_shape=[pltpu.HBM((SEQ, D_FF), dt) for dt in (F32, F32, BF16)],
        compiler_params=pltpu.CompilerParams(
            dimension_semantics=("parallel", "parallel"), vmem_limit_bytes=_vmem_limit(8 * 2**20)),
    )(h2, wt_gate, wt_up)


def _ffn_act_bwd(dff, w_down, gate, up):
    tm, tn = FFN_TM, FFN_TN

    def body(d_ref, wd_ref, gate_ref, up_ref, dgate_ref, dup_ref):
        for rows in (slice(0, tm // 2), slice(tm // 2, tm)):
            dact = _dot_nt(d_ref[rows, :], wd_ref[...])
            gate = gate_ref[rows, :]
            sg = _sigmoid(gate)
            dgate_ref[rows, :] = (dact * up_ref[rows, :] * (sg * (1.0 + gate * (1.0 - sg)))).astype(BF16)
            dup_ref[rows, :] = (dact * (gate * sg)).astype(BF16)

    tile = pl.BlockSpec((tm, tn), lambda i, j: (i, j))
    return pl.pallas_call(
        body, name="ffn_act_bwd", grid=(SEQ // tm, D_FF // tn),
        in_specs=[pl.BlockSpec((tm, D_MODEL), lambda i, j: (i, 0)), pl.BlockSpec((tn, D_MODEL), lambda i, j: (j, 0)),
                  tile, tile],
        out_specs=[tile, tile],
        out_shape=[pltpu.HBM((SEQ, D_FF), BF16)] * 2,
        compiler_params=pltpu.CompilerParams(
            dimension_semantics=("parallel", "parallel"), vmem_limit_bytes=_vmem_limit(8 * 2**20)),
    )(dff, w_down, gate, up)


EPILOGUE_TM = 512


def _loss_head(act, w_down, x1, target, g_post):
    def fn(ff, x1, tgt, g):
        r = _rms_scale(ff)
        nrm = ff * r
        err = (x1 + nrm * g) - tgt
        loss = 0.5 * jnp.sum(jnp.mean(err * err, axis=-1, keepdims=True), axis=0, keepdims=True)
        dy = err * (1.0 / D_MODEL)
        u = dy * g
        dff = r * u - ff * (r * r * r) * jnp.mean(u * ff, axis=-1, keepdims=True)
        return dy, dff, jnp.broadcast_to(loss, (1, LANES)), jnp.sum(dy * nrm, axis=0, keepdims=True)

    d = D_MODEL
    return _matmul_rowwise([(act, w_down)], fn, "ffn_down_loss", EPILOGUE_TM, [(x1, d, 0), (target, d, 0)], [g_post],
                           [(d, F32), (d, BF16)], [LANES, d])


def _post_ffn_bwd(dgate, wt_gate, dup, wt_up, x1, dy, mix, g_ffn_pre, g_mix_post, deps=()):
    def fn(dh2, x1, dy, mix, g3, g2):
        dx, dg3 = _rms_bwd(x1, dh2, g3)
        dx1 = dy + dx
        dmix, dg2 = _rms_bwd(mix, dx1, g2)
        return dx1, dmix, dg3, dg2

    d = D_MODEL
    return _matmul_rowwise([(dgate, wt_gate), (dup, wt_up)], fn, "ffn_up_bwd", EPILOGUE_TM,
                           [(x1, d, 0), (dy, d, 0), (mix, d, 0)], [g_ffn_pre, g_mix_post],
                           [(d, F32), (d, BF16)], [d, d], deps=deps)


def _input_bwd(dproj, wt_r, x, dx1, g_pre, deps=()):
    def fn(dh, x, dx1, g):
        dx, dg = _rms_bwd(x, dh, g)
        return dx1 + dx, dg

    d = D_MODEL
    return _matmul_rowwise([(dproj, wt_r)], fn, "in_proj_bwd", EPILOGUE_TM, [(x, d, 0), (dx1, d, 0)], [g_pre],
                           [(d, F32)], [d], deps=deps)


def _adam_math(w, g, m, v):
    m = ADAM_B1 * m + (1.0 - ADAM_B1) * g
    v = ADAM_B2 * v + (1.0 - ADAM_B2) * (g * g)
    m_hat = m / (1.0 - ADAM_B1 ** ADAM_STEP)
    v_hat = v / (1.0 - ADAM_B2 ** ADAM_STEP)
    delta = -ADAM_LR * (m_hat / (jnp.sqrt(v_hat) + ADAM_EPS) + ADAM_WD * w)
    return delta, m, v


def _adam(w, mine, recv, m, v, name):
    r, c = w.shape
    tc = _col_tile(r, c)

    def body(w_ref, p_ref, r_ref, m_ref, v_ref, g_ref, d_ref, nm_ref, nv_ref):
        g = ((p_ref[...] + r_ref[0].astype(F32)) + r_ref[1].astype(F32)) + r_ref[2].astype(F32)
        g_ref[...] = g
        d_ref[...], nm_ref[...], nv_ref[...] = _adam_math(w_ref[...], g, m_ref[...], v_ref[...])

    spec = pl.BlockSpec((r, tc), lambda j: (0, j))
    return pl.pallas_call(
        body, name=name, grid=(c // tc,),
        in_specs=[spec, spec, pl.BlockSpec((3, r, tc), lambda j: (0, 0, j)), spec, spec], out_specs=[spec] * 4,
        out_shape=[pltpu.HBM((r, c), F32)] * 4,
        compiler_params=pltpu.CompilerParams(dimension_semantics=("parallel",)),
    )(*_in_hbm(w, mine, recv, m, v))


def _adam_small(gathered, ws, ms, vs, loss_parts):
    n = len(ws)

    def body(*refs):
        outs = refs[4 * n + 1:]
        loss = refs[4 * n][0]
        for dev in range(1, N_DEV):
            loss = loss + refs[4 * n][dev]
        outs[4 * n][...] = loss
        for i in range(n):
            ga_ref, w_ref, m_ref, v_ref = (refs[j * n + i] for j in range(4))
            g = ga_ref[0]
            for dev in range(1, N_DEV):
                g = g + ga_ref[dev]
            g = g[:, :w_ref.shape[1]]
            outs[4 * i][...] = g
            outs[4 * i + 1][...], outs[4 * i + 2][...], outs[4 * i + 3][...] = _adam_math(
                w_ref[...], g, m_ref[...], v_ref[...])

    out_shape = [pltpu.HBM(w.shape, F32) for w in ws for _ in range(4)]
    out_shape.append(pltpu.HBM((1, LANES), F32))
    out = pl.pallas_call(body, name="adam_small", out_shape=out_shape)(*gathered, *ws, *ms, *vs, loss_parts)
    return [out[4 * i:4 * i + 4] for i in range(n)], out[4 * n]


_PROJ_SEGMENTS = ((3848, 5896), (None, COL_QA - 2 * D_MODEL), (0, 3840), (3840, 3848), (None, PROJ_COLS - COL_F - 8))


def _proj_weight_t(gathered):
    w = gathered.reshape(IN_COLS, D_MODEL)
    return jnp.concatenate([jnp.zeros((hi, D_MODEL), w.dtype) if lo is None else w[lo:hi] for lo, hi in _PROJ_SEGMENTS],
                           axis=0)


def _proj_weight_grad_slots(dwt_r):
    starts, at = [], 0
    for lo, hi in _PROJ_SEGMENTS:
        if lo is not None:
            starts.append((lo, hi, at))
        at += hi if lo is None else hi - lo
    slots = []
    for dev in range(N_DEV):
        pieces, lo, end = [], dev * IN_SHARD, (dev + 1) * IN_SHARD
        for seg_lo, seg_hi, seg_at in sorted(starts):
            a, b = max(lo, seg_lo), min(end, seg_hi)
            if a < b:
                pieces.append(dwt_r[seg_at + a - seg_lo:seg_at + b - seg_lo])
        slots.append(pieces[0] if len(pieces) == 1 else jnp.concatenate(pieces, axis=0))
    return jnp.stack(slots)


def kernel(x, w_in, w_proj_a, w_proj_b, w_out, b_forget, w_ffn_gate, w_ffn_up, w_ffn_down, norm_mix_pre, norm_mix_post, norm_ffn_pre, norm_ffn_post, loss_target, m_w_in, m_w_proj_a, m_w_proj_b, m_w_out, m_b_forget, m_w_ffn_gate, m_w_ffn_up, m_w_ffn_down, m_norm_mix_pre, m_norm_mix_post, m_norm_ffn_pre, m_norm_ffn_post, v_w_in, v_w_proj_a, v_w_proj_b, v_w_out, v_b_forget, v_w_ffn_gate, v_w_ffn_up, v_w_ffn_down, v_norm_mix_pre, v_norm_mix_post, v_norm_ffn_pre, v_norm_ffn_post):
    d = D_MODEL
    names = ("w_in", "w_proj_a", "w_proj_b", "w_out", "w_ffn_gate", "w_ffn_up", "w_ffn_down")
    col_sharded = ("w_in", "w_ffn_gate", "w_ffn_up")

    def row_shards(arrs):
        return {k: (a[0].T if k in col_sharded else a[0]) for k, a in zip(names, arrs)}

    shards = row_shards((w_in, w_proj_a, w_proj_b, w_out, w_ffn_gate, w_ffn_up, w_ffn_down))
    moments_m = row_shards((m_w_in, m_w_proj_a, m_w_proj_b, m_w_out, m_w_ffn_gate, m_w_ffn_up, m_w_ffn_down))
    moments_v = row_shards((v_w_in, v_w_proj_a, v_w_proj_b, v_w_out, v_w_ffn_gate, v_w_ffn_up, v_w_ffn_down))
    pos = jnp.stack([lax.axis_index("c"), 2 * lax.axis_index("x") + lax.axis_index("y")]).astype(jnp.int32)
    x2, target = x[0], loss_target[0]

    me = 4 * lax.axis_index("x") + 2 * lax.axis_index("y") + lax.axis_index("c")
    mid_names, ffn_names = names[1:4], names[4:]
    first_names, later_names = names[:1], names[1:]
    shards16 = {k: shards[k].astype(BF16) for k in names}

    def landing(k):
        return lax.dynamic_update_slice(lax.empty((N_DEV,) + shards[k].shape, BF16), shards16[k][None], (me, 0, 0))

    ag_first = _exchange_start("ag_first_chips_start", _gather_chips_copies, [shards16[k] for k in first_names],
                               [landing(k) for k in first_names], 3 * len(first_names))
    h = _rowwise(lambda xb, g: xb * _rms_scale(xb) * g, "norm_mix_pre", SEQ, 256, [(x2, d, 0)], [norm_mix_pre],
                 [(d, BF16)], deps=[ag_first.token])[0]
    later_lands = [landing(k) for k in later_names]
    _, lands = _exchange_wait("ag_first_chips_wait", ag_first,
                              [h, shards["w_in"], moments_m["w_in"], moments_v["w_in"], *later_lands])
    ag_first = _exchange_start("ag_first_sibling_start", _gather_sibling_copies, [], lands, 4 * len(first_names))
    ag_later = _exchange_start("ag_later_chips_start", _gather_chips_copies, [shards16[k] for k in later_names],
                               later_lands, 3 * len(later_names), after=[ag_first.token])
    gathered = dict(zip(first_names, _exchange_wait("ag_first_sibling_wait", ag_first, [ag_later.token])[1]))
    wt_r = _proj_weight_t(gathered["w_in"])

    proj = _matmul([(h, wt_r)], "nt", F32, "in_proj", 1024, 896, 1024)
    tables = _rope_tables()
    o_dil, lse_dil = _dil_fwd(proj, tables)
    out_a = _dil_combine(o_dil, lse_dil)
    _, lands = _exchange_wait("ag_later_chips_wait", ag_later, [out_a])
    ag_later = _exchange_start("ag_later_sibling_start", _gather_sibling_copies, [], lands, 4 * len(later_names))

    b_pad = jnp.pad(b_forget, ((0, 0), (0, LANES - N_FOX_HEADS)))
    f_rows = _fox_gate(proj, b_pad, deps=[ag_later.token])
    out_b, lse_fox = _fox_fwd(proj, f_rows)

    gathered = dict(zip(later_names, _exchange_wait("ag_later_sibling_wait", ag_later, [out_b])[1]))
    wt_pa = gathered["w_proj_a"].transpose(1, 0, 2).reshape(DIL_OUT_WIDTH, d)
    wt_pb = gathered["w_proj_b"].transpose(1, 0, 2).reshape(FOX_WIDTH, d)
    w_o = gathered["w_out"].reshape(d, d)
    wt_g = gathered["w_ffn_gate"].reshape(D_FF, d)
    wt_u = gathered["w_ffn_up"].reshape(D_FF, d)
    w_d = gathered["w_ffn_down"].reshape(D_FF, d)
    merged, mix, x1, h2 = _mix_out(out_a, out_b, proj, x2, wt_pa, wt_pb, w_o, norm_mix_post, norm_ffn_pre)

    gate, up, act = _ffn_up(h2, wt_g, wt_u)
    dy, dff, loss_part, dg_ffn_post = _loss_head(act, w_d, x1, target, norm_ffn_post)

    dgate, dup = _ffn_act_bwd(dff, w_d, gate, up)
    grads_t = {}
    grads_t["w_ffn_down"] = _matmul([(act, dff)], "tn", F32, "grad_w_ffn_down", 1408, 512, 2048)
    grads_t["w_ffn_gate"] = _matmul([(dgate, h2)], "tn", F32, "grad_w_ffn_gate", 1408, 512, 2048)
    grads_t["w_ffn_up"] = _matmul([(dup, h2)], "tn", F32, "grad_w_ffn_up", 1408, 512, 2048)
    rs_ffn = _ReduceScatter("ffn", {k: grads_t[k] for k in ffn_names}, pos)
    dx1, dmix, dg_ffn_pre, dg_mix_post = _post_ffn_bwd(dgate, wt_g, dup, wt_u, x1, dy, mix, norm_ffn_pre, norm_mix_post,
                                                       deps=[rs_ffn.token])
    rs_ffn.start_chips([dmix])

    dproj, dya, dyb, d_out_a, d_out_b = _mix_out_bwd(dmix, out_a, out_b, proj, wt_pa, wt_pb, w_o, deps=[rs_ffn.token])
    grads_t["w_out"] = _matmul([(merged, dmix)], "tn", F32, "grad_w_out", 1024, 1024, 1024)
    def column_slots(g):
        return g.reshape(g.shape[0], N_DEV, LANES).transpose(1, 0, 2)

    grads_t["w_proj_a"] = column_slots(_matmul([(out_a, dya)], "tn", F32, "grad_w_proj_a", DIL_OUT_WIDTH, 1024, SEQ))
    grads_t["w_proj_b"] = column_slots(_matmul([(out_b, dyb)], "tn", F32, "grad_w_proj_b", FOX_WIDTH, 1024, SEQ))
    rs_mid = _ReduceScatter("mid", {k: grads_t[k] for k in mid_names}, pos)

    do_dil, c_dil = _dil_combine_bwd(d_out_a, o_dil, lse_dil, deps=[rs_mid.token])
    rs_mid.start_chips([c_dil])
    dproj, d_cum = _fox_bwd(proj, d_out_b, lse_fox, f_rows, dproj)
    d_cum_rows = jnp.pad(d_cum[:, :2].reshape(N_FOX_HEADS, SEQ), ((0, F_ROWS - N_FOX_HEADS), (0, 0)))
    dproj, db_part = _fox_gate_bwd(d_cum_rows, proj, b_pad, dproj)
    dproj = _dil_bwd(proj, tables, do_dil, lse_dil, c_dil, dproj, deps=[rs_mid.token])

    dwt_r = _matmul([(dproj, h)], "tn", F32, "grad_w_in", 896, 1024, 2048)
    rs_in = _ReduceScatter("in", {"w_in": _proj_weight_grad_slots(dwt_r)}, pos)
    def finish(rs, after):
        return {k: _adam(shards[k], mine, recv, moments_m[k], moments_v[k], "adam_" + k)
                for k, (mine, recv) in rs.finish(after).items()}

    done = finish(rs_ffn, [rs_in.token])
    rs_in.start_chips([done[k][0] for k in ffn_names])
    grad_x, dg_mix_pre = _input_bwd(dproj, wt_r, x2, dx1, norm_mix_pre, deps=[rs_in.token])
    done.update(finish(rs_mid, [grad_x]))

    small_all = _all_gather([dg_mix_pre, dg_mix_post, dg_ffn_pre, dg_ffn_post, db_part, loss_part],
                            "small_grads_all_gather", deps=[done[k][0] for k in mid_names])
    small, loss = _adam_small(small_all[:5], [norm_mix_pre, norm_mix_post, norm_ffn_pre, norm_ffn_post, b_forget],
                              [m_norm_mix_pre, m_norm_mix_post, m_norm_ffn_pre, m_norm_ffn_post, m_b_forget],
                              [v_norm_mix_pre, v_norm_mix_post, v_norm_ffn_pre, v_norm_ffn_post, v_b_forget],
                              small_all[5])

    done.update(finish(rs_in, [small[0][0]]))

    def leaves(i):
        def nat(k):
            a = done[k][i]
            return (a.T if k in col_sharded else a)[None]

        return [nat("w_in"), nat("w_proj_a"), nat("w_proj_b"), nat("w_out"), small[4][i],
                nat("w_ffn_gate"), nat("w_ffn_up"), nat("w_ffn_down"), *[small[r][i] for r in range(4)]]

    return (loss[0, 0], grad_x[None], *leaves(0), *leaves(1), *leaves(2), *leaves(3))
```

```python
import functools
import math

import jax
import jax.numpy as jnp
import numpy as np
from jax import lax
from jax.experimental import pallas as pl
from jax.experimental.pallas import tpu as pltpu

F32 = jnp.float32
BF16 = jnp.bfloat16
MESH = pl.DeviceIdType.MESH

D_MODEL = 1024
SEQ = 2048
HEAD_DIM = 64
BLOCK = 128
N_BLOCKS = SEQ // BLOCK
DILATIONS = (1, 4, 16)
N_FOX_HEADS = 8
DIL_WIDTH = 768
DIL_OUT_WIDTH = 256
FOX_WIDTH = 512
D_FF = 2816
ROPE_THETA = 500000.0
ROPE_DIM = HEAD_DIM // 4
ROPE_HALF = ROPE_DIM // 2
EPS = 1e-6
NEG_INF = -1e30
QK_SCALE = 1.0 / math.sqrt(HEAD_DIM)
IN_COLS = 5896
N_DEV = 8
IN_SHARD = IN_COLS // N_DEV

ADAM_LR = 0.001
ADAM_B1 = 0.9
ADAM_B2 = 0.999
ADAM_EPS = 1e-08
ADAM_WD = 0.01
ADAM_STEP = 10

V7X_VMEM_BYTES = 64 * 2**20
LANES = 128
SUBLANES = 8

PROJ_COLS = 6272
COL_GA, COL_GB = 0, 1024
COL_QA, COL_KA, COL_VA = 2304, 3072, 3840
COL_QB, COL_KB, COL_VB = 4608, 5120, 5632
COL_F = 6144
F_ROWS = 16


def _vmem_limit(block_bytes):
    want = 2 * block_bytes + 16 * 2**20
    return int(min(max(want, 32 * 2**20), V7X_VMEM_BYTES - 8 * 2**20))


def _nbytes(shape, dtype):
    return math.prod(shape) * jnp.dtype(dtype).itemsize


def _in_hbm(*arrays):
    return [pltpu.with_memory_space_constraint(a, pltpu.HBM) for a in arrays]


def _dot(a, b, dims):
    return lax.dot_general(a, b, (dims, ((), ())), preferred_element_type=F32)


def _dot_nn(a, b):
    return _dot(a, b, ((1,), (0,)))


def _dot_nt(a, b):
    return _dot(a, b, ((1,), (1,)))


def _dot_tn(a, b):
    return _dot(a, b, ((0,), (0,)))


def _sigmoid(z):
    return 1.0 / (1.0 + jnp.exp(-z))


def _split3(x):
    hi = x.astype(BF16)
    r1 = x - hi.astype(F32)
    mid = r1.astype(BF16)
    lo = (r1 - mid.astype(F32)).astype(BF16)
    return hi, mid, lo


def _dot3_nn(x, ones_matrix):
    hi, mid, lo = _split3(x)
    return (_dot_nn(hi, ones_matrix) + _dot_nn(mid, ones_matrix)) + _dot_nn(lo, ones_matrix)


def _rowwise(fn, name, n_rows, tm, row_ins, bcast_ins, row_outs, acc_outs=(), deps=()):
    n_in = len(row_ins) + len(bcast_ins)
    n_ro = len(row_outs)

    def body(*refs):
        res = fn(*[r[...] for r in refs[:n_in]])
        if not isinstance(res, (tuple, list)):
            res = (res,)
        outs = refs[n_in + len(deps):]
        for r, o in zip(res[:n_ro], outs[:n_ro]):
            o[...] = r.astype(o.dtype)
        first = pl.program_id(0) == 0
        for r, o in zip(res[n_ro:], outs[n_ro:]):
            _accumulate(o, r, first)

    in_specs = [pl.BlockSpec((tm, w), lambda i, cb=cb: (i, cb)) for _, w, cb in row_ins]
    in_specs += [pl.BlockSpec(a.shape, lambda i: (0, 0)) for a in bcast_ins]
    in_specs += [pl.BlockSpec(memory_space=pl.ANY)] * len(deps)
    out_specs = [pl.BlockSpec((tm, w), lambda i: (i, 0)) for w, _ in row_outs]
    out_specs += [pl.BlockSpec((1, w), lambda i: (0, 0)) for w in acc_outs]
    out_shape = [pltpu.HBM((n_rows, w), dt) for w, dt in row_outs]
    out_shape += [pltpu.HBM((1, w), F32) for w in acc_outs]
    blk = sum(_nbytes((tm, w), a.dtype) for a, w, _ in row_ins) + sum(_nbytes((tm, w), dt) for w, dt in row_outs)
    return pl.pallas_call(
        body, name=name, grid=(n_rows // tm,), in_specs=in_specs, out_specs=out_specs, out_shape=out_shape,
        compiler_params=pltpu.CompilerParams(
            dimension_semantics=("arbitrary" if acc_outs else "parallel",), vmem_limit_bytes=_vmem_limit(3 * blk)),
    )(*_in_hbm(*[a for a, _, _ in row_ins], *bcast_ins), *deps)


def _accumulate(o_ref, part, first):
    @pl.when(first)
    def _():
        o_ref[...] = part

    @pl.when(jnp.logical_not(first))
    def _():
        o_ref[...] += part


_MM_DIMS = {"nn": ((1,), (0,)), "nt": ((1,), (1,)), "tn": ((0,), (0,))}


def _matmul(pairs, mode, out_dtype, name, tm, tn, tk, deps=()):
    a0, b0 = pairs[0]
    if mode == "tn":
        kk, m = a0.shape
    else:
        m, kk = a0.shape
    n = b0.shape[0] if mode == "nt" else b0.shape[1]
    assert m % tm == 0 and n % tn == 0 and kk % tk == 0, (name, m, n, kk)
    nk = kk // tk
    n_pairs = len(pairs)
    dims = _MM_DIMS[mode]
    n_in = 2 * n_pairs + len(deps)

    def body(*refs):
        o_ref = refs[n_in]
        part = None
        for p in range(n_pairs):
            d = _dot(refs[2 * p][...].astype(BF16), refs[2 * p + 1][...].astype(BF16), dims)
            part = d if part is None else part + d
        if nk == 1:
            o_ref[...] = part.astype(o_ref.dtype)
            return
        acc = refs[n_in + 1]
        k = pl.program_id(2)

        @pl.when(k == 0)
        def _():
            acc[...] = part

        @pl.when(k > 0)
        def _():
            acc[...] += part

        @pl.when(k == nk - 1)
        def _():
            o_ref[...] = acc[...].astype(o_ref.dtype)

    if mode == "tn":
        a_spec = pl.BlockSpec((tk, tm), lambda i, j, k: (k, i))
    else:
        a_spec = pl.BlockSpec((tm, tk), lambda i, j, k: (i, k))
    if mode == "nt":
        b_spec = pl.BlockSpec((tn, tk), lambda i, j, k: (j, k))
    else:
        b_spec = pl.BlockSpec((tk, tn), lambda i, j, k: (k, j))
    blk = sum(_nbytes((tm, tk), a.dtype) + _nbytes((tk, tn), b.dtype) for a, b in pairs) + 2 * _nbytes((tm, tn), F32)
    flat = [a for pair in pairs for a in pair]
    return pl.pallas_call(
        body, name=name, grid=(m // tm, n // tn, nk),
        in_specs=[a_spec, b_spec] * n_pairs + [pl.BlockSpec(memory_space=pl.ANY)] * len(deps),
        out_specs=pl.BlockSpec((tm, tn), lambda i, j, k: (i, j)),
        out_shape=pltpu.HBM((m, n), out_dtype),
        scratch_shapes=[] if nk == 1 else [pltpu.VMEM((tm, tn), F32)],
        compiler_params=pltpu.CompilerParams(
            dimension_semantics=("parallel", "parallel", "arbitrary"), vmem_limit_bytes=_vmem_limit(blk)),
    )(*flat, *deps)


def _matmul_rowwise(pairs, fn, name, tm, row_ins, bcast_ins, row_outs, acc_outs=(), deps=()):
    m = pairs[0][0].shape[0]
    n_mm, n_in = 2 * len(pairs), len(row_ins) + len(bcast_ins)
    n_ro = len(row_outs)

    def body(*refs):
        outs = refs[n_mm + n_in + len(deps):]
        partials = []
        for rows in (slice(0, tm // 2), slice(tm // 2, tm)):
            prod = None
            for p in range(len(pairs)):
                part = _dot_nn(refs[2 * p][rows, :].astype(BF16), refs[2 * p + 1][...].astype(BF16))
                prod = part if prod is None else prod + part
            res = fn(prod, *[r[rows, :] for r in refs[n_mm:n_mm + len(row_ins)]],
                     *[r[...] for r in refs[n_mm + len(row_ins):n_mm + n_in]])
            for r, o in zip(res[:n_ro], outs[:n_ro]):
                o[rows, :] = r.astype(o.dtype)
            partials.append(res[n_ro:])
        first = pl.program_id(0) == 0
        for r0, r1, o in zip(*partials, outs[n_ro:]):
            _accumulate(o, r0 + r1, first)

    in_specs = []
    for a, b in pairs:
        in_specs += [pl.BlockSpec((tm, a.shape[1]), lambda i: (i, 0)),
                     pl.BlockSpec(b.shape, lambda i: (0, 0), pipeline_mode=pl.Buffered(1))]
    in_specs += [pl.BlockSpec((tm, w), lambda i, cb=cb: (i, cb)) for _, w, cb in row_ins]
    in_specs += [pl.BlockSpec(a.shape, lambda i: (0, 0)) for a in bcast_ins]
    in_specs += [_ANY] * len(deps)
    out_specs = [pl.BlockSpec((tm, w), lambda i: (i, 0)) for w, _ in row_outs]
    out_specs += [pl.BlockSpec((1, w), lambda i: (0, 0)) for w in acc_outs]
    out_shape = [pltpu.HBM((m, w), dt) for w, dt in row_outs]
    out_shape += [pltpu.HBM((1, w), F32) for w in acc_outs]
    blk = sum(_nbytes((tm, a.shape[1]), a.dtype) + _nbytes(b.shape, b.dtype) // 2 for a, b in pairs)
    blk += sum(_nbytes((tm, w), a.dtype) for a, w, _ in row_ins) + sum(_nbytes((tm, w), dt) for w, dt in row_outs)
    return pl.pallas_call(
        body, name=name, grid=(m // tm,), in_specs=in_specs, out_specs=out_specs, out_shape=out_shape,
        compiler_params=pltpu.CompilerParams(dimension_semantics=("arbitrary",), vmem_limit_bytes=_vmem_limit(blk)),
    )(*[a for pair in pairs for a in pair], *[a for a, _, _ in row_ins], *bcast_ins, *deps)


def _rms_scale(x):
    return lax.rsqrt(jnp.mean(x * x, axis=-1, keepdims=True) + EPS)


def _rms_bwd(xin, dyn, g):
    r = _rms_scale(xin)
    u = dyn * g
    dx = r * u - xin * (r * r * r) * jnp.mean(u * xin, axis=-1, keepdims=True)
    dg = jnp.sum(dyn * xin * r, axis=0, keepdims=True)
    return dx, dg


def _mesh_pos():
    return lax.axis_index("x"), lax.axis_index("y"), lax.axis_index("c")


def _all_gather(xs, name, deps=()):
    n = len(xs)

    def body(*refs):
        x_refs, out_refs = refs[:n], refs[n + len(deps):2 * n + len(deps)]
        send_sems, recv_sems, local_sems = refs[2 * n + len(deps):]
        mx, my, mc = _mesh_pos()
        me, sib = (mx, my, mc), (mx, my, 1 - mc)
        chips = [(1 - mx, my), (mx, 1 - my), (1 - mx, 1 - my)]

        def slot(a, dev):
            px, py, pc = dev
            return out_refs[a].at[4 * px + 2 * py + pc]

        def copy(k, a, block, to, src=None):
            return pltpu.make_async_remote_copy(
                src_ref=slot(a, block) if src is None else src, dst_ref=slot(a, block),
                send_sem=send_sems.at[a * 7 + k], recv_sem=recv_sems.at[a * 7 + k],
                device_id=to, device_id_type=MESH)

        mine = [pltpu.make_async_copy(x_refs[a], slot(a, me), local_sems.at[a]) for a in range(n)]
        for cp in mine:
            cp.start()
        first = []
        for a in range(n):
            first.append(copy(0, a, me, sib, x_refs[a]))
            first += [copy(1 + j, a, me, (*chip, mc), x_refs[a]) for j, chip in enumerate(chips)]
        for cp in first:
            cp.start()
        passed = []
        for a in range(n):
            for j, chip in enumerate(chips):
                copy(1 + j, a, (*chip, mc), me).wait_recv()
                fwd = copy(4 + j, a, (*chip, mc), sib)
                fwd.start()
                passed.append(fwd)
        for a in range(n):
            copy(0, a, sib, me).wait_recv()
            for j, chip in enumerate(chips):
                copy(4 + j, a, (*chip, 1 - mc), me).wait_recv()
        for cp in first + passed:
            cp.wait_send()
        for cp in mine:
            cp.wait()

    hbm = pl.BlockSpec(memory_space=pl.ANY)
    return pl.pallas_call(
        body, name=name,
        out_shape=[pltpu.HBM((N_DEV,) + x.shape, x.dtype) for x in xs],
        in_specs=[hbm] * (n + len(deps)), out_specs=[hbm] * n,
        scratch_shapes=[pltpu.SemaphoreType.DMA((7 * n,)), pltpu.SemaphoreType.DMA((7 * n,)),
                        pltpu.SemaphoreType.DMA((n,))],
    )(*xs, *deps)


_HBM = pl.BlockSpec(memory_space=pltpu.HBM)
_SEM = pl.BlockSpec(memory_space=pltpu.SEMAPHORE)
_ANY = pl.BlockSpec(memory_space=pl.ANY)
_DATAFLOW = pltpu.SideEffectType.DATAFLOW_SIDE_EFFECTING


def _flip_peer(flip):
    mx, my, mc = _mesh_pos()
    return (1 - mx if flip & 2 else mx, 1 - my if flip & 1 else my, mc)


def _remote(src, dst, send_sems, recv_sems, k, peer):
    return pltpu.make_async_remote_copy(src_ref=src, dst_ref=dst, send_sem=send_sems.at[k], recv_sem=recv_sems.at[k],
                                        device_id=peer, device_id_type=MESH)


def _gather_chips_copies(srcs, lands, send_sems, recv_sems):
    mx, my, mc = _mesh_pos()
    me = 4 * mx + 2 * my + mc
    return [_remote(srcs[a], lands[a].at[me], send_sems, recv_sems, 3 * a + flip - 1, _flip_peer(flip))
            for a in range(len(srcs)) for flip in (1, 2, 3)]


def _gather_sibling_copies(srcs, lands, send_sems, recv_sems):
    mx, my, mc = _mesh_pos()
    return [_remote(lands[a].at[2 * k + mc], lands[a].at[2 * k + mc], send_sems, recv_sems, 4 * a + k, (mx, my, 1 - mc))
            for a in range(len(lands)) for k in range(4)]


def _scatter_sibling_copies(srcs, lands, send_sems, recv_sems):
    mx, my, mc = _mesh_pos()
    return [_remote(srcs[a].at[k, 1 - mc], lands[a].at[k], send_sems, recv_sems, 4 * a + k, (mx, my, 1 - mc))
            for a in range(len(srcs)) for k in range(4)]


def _scatter_chips_copies(srcs, lands, send_sems, recv_sems):
    mx, my, _ = _mesh_pos()
    k0 = 2 * mx + my
    return [_remote(srcs[a].at[jnp.bitwise_xor(k0, flip)], lands[a].at[flip - 1], send_sems, recv_sems,
                    3 * a + flip - 1, _flip_peer(flip))
            for a in range(len(srcs)) for flip in (1, 2, 3)]


class _Exchange:
    def __init__(self, copies, n_src, send_sems, recv_sems, thru, token):
        self.copies, self.n_src, self.send_sems, self.recv_sems, self.thru, self.token = (
            copies, n_src, send_sems, recv_sems, thru, token)


def _exchange_start(name, copies, srcs, lands, n_copies, after=()):
    bufs = list(srcs) + list(lands)
    nb, ns = len(bufs), len(srcs)

    def body(*refs):
        send_sems, recv_sems = refs[nb + len(after)], refs[nb + len(after) + 1]
        for cp in copies(refs[:ns], refs[ns:nb], send_sems, recv_sems):
            cp.start()
        refs[-1][...] = jnp.zeros_like(refs[-1])

    out = pl.pallas_call(
        body, name=name,
        out_shape=(pltpu.SemaphoreType.DMA((n_copies,)), pltpu.SemaphoreType.DMA((n_copies,)),
                   *[pltpu.HBM(b.shape, b.dtype) for b in bufs], pltpu.HBM((SUBLANES, LANES), F32)),
        in_specs=[_HBM] * nb + [_ANY] * len(after),
        out_specs=(_SEM, _SEM, *[_HBM] * nb, pl.BlockSpec(memory_space=pltpu.VMEM)),
        input_output_aliases={i: 2 + i for i in range(nb)},
        compiler_params=pltpu.CompilerParams(has_side_effects=_DATAFLOW),
    )(*[pltpu.with_memory_space_constraint(b, pltpu.HBM) for b in bufs], *after)
    return _Exchange(copies, ns, out[0], out[1], list(out[2:2 + nb]), out[-1])


def _exchange_wait(name, ex, after):
    nb, ns = len(ex.thru), ex.n_src

    def body(*refs):
        for cp in ex.copies(refs[:ns], refs[ns:nb], refs[nb], refs[nb + 1]):
            cp.wait_send()
            cp.wait_recv()

    out = pl.pallas_call(
        body, name=name, out_shape=tuple(pltpu.HBM(b.shape, b.dtype) for b in ex.thru),
        in_specs=[_HBM] * nb + [_SEM, _SEM] + [_ANY] * len(after), out_specs=tuple([_HBM] * nb),
        input_output_aliases={i: i for i in range(nb)},
        compiler_params=pltpu.CompilerParams(has_side_effects=_DATAFLOW),
    )(*ex.thru, ex.send_sems, ex.recv_sems, *after)
    return list(out[:ns]), list(out[ns:])


def _col_tile(r, c):
    return next(t for t in (1024, 512, 256, 128) if c % t == 0 and (r * t * 4 <= 2**20 or t == 128))


def _add_sibling(g4, recv, pos, name):
    _, _, r, c = g4.shape
    tc = _col_tile(r, c)

    def body(pos_ref, g_ref, r_ref, o16_ref, mine_ref):
        s = g_ref[0, 0] + r_ref[0]
        o16_ref[0] = s.astype(BF16)

        @pl.when(pl.program_id(1) == pos_ref[1])
        def _():
            mine_ref[...] = s

    slot = pl.BlockSpec((1, r, tc), lambda j, k, pos_ref: (k, 0, j))
    return pl.pallas_call(
        body, name=name,
        out_shape=[pltpu.HBM((4, r, c), BF16), pltpu.HBM((r, c), F32)],
        grid_spec=pltpu.PrefetchScalarGridSpec(
            num_scalar_prefetch=1, grid=(c // tc, 4),
            in_specs=[pl.BlockSpec((1, 1, r, tc), lambda j, k, pos_ref: (k, pos_ref[0], 0, j)), slot],
            out_specs=[slot, pl.BlockSpec((r, tc), lambda j, k, pos_ref: (0, j))]),
        compiler_params=pltpu.CompilerParams(dimension_semantics=("parallel", "arbitrary")),
    )(pos, *_in_hbm(g4, recv))


class _ReduceScatter:
    def __init__(self, tag, grads_t, pos):
        self.tag, self.pos, self.names = tag, pos, list(grads_t)
        g4s = [g.reshape(4, 2, g.size // (N_DEV * g.shape[-1]), g.shape[-1]) for g in grads_t.values()]
        lands = [lax.empty((4,) + g.shape[2:], F32) for g in g4s]
        self.ex = _exchange_start(f"rs_{tag}_sibling_start", _scatter_sibling_copies, g4s, lands, 4 * len(g4s))
        self.token = self.ex.token

    def start_chips(self, after):
        g4s, from_sibling = _exchange_wait(f"rs_{self.tag}_sibling_wait", self.ex, after)
        parts = [_add_sibling(g4, rv, self.pos, f"rs_add_sibling_{k}")
                 for k, g4, rv in zip(self.names, g4s, from_sibling)]
        self.mine = [mine for _, mine in parts]
        p16s = [p16 for p16, _ in parts]
        lands = [lax.empty((3,) + p.shape[1:], BF16) for p in p16s]
        self.ex = _exchange_start(f"rs_{self.tag}_chips_start", _scatter_chips_copies, p16s, lands, 3 * len(p16s))
        self.token = self.ex.token

    def finish(self, after):
        _, from_chips = _exchange_wait(f"rs_{self.tag}_chips_wait", self.ex, after)
        return dict(zip(self.names, zip(self.mine, from_chips)))


def _rope_tables():
    positions = np.arange(SEQ, dtype=np.float32)
    inv_freq = np.power(np.float32(ROPE_THETA), -np.arange(0, ROPE_DIM, 2, dtype=np.float32) / np.float32(ROPE_DIM))
    ang = (positions[:, None] * inv_freq[None, :]).astype(np.float32)
    cos, sin = np.cos(ang).astype(np.float32), np.sin(ang).astype(np.float32)
    ones = np.ones((SEQ, HEAD_DIM - ROPE_DIM), np.float32)
    zeros8 = np.zeros((SEQ, ROPE_HALF), np.float32)
    zeros = np.zeros((SEQ, HEAD_DIM - ROPE_DIM), np.float32)
    c_head = np.concatenate([cos, cos, ones], axis=1)
    s1_head = np.concatenate([-sin, zeros8, zeros], axis=1)
    s2_head = np.concatenate([zeros8, sin, zeros], axis=1)
    return tuple(jnp.asarray(np.concatenate([t, t], axis=1)) for t in (c_head, s1_head, s2_head))


def _rope_apply(x, c, s1, s2):
    w = x.shape[1]
    return x * c + pltpu.roll(x, w - ROPE_HALF, 1) * s1 + pltpu.roll(x, ROPE_HALF, 1) * s2


def _rope_apply_t(dy, c, s1, s2):
    w = dy.shape[1]
    return dy * c + pltpu.roll(dy * s1, ROPE_HALF, 1) + pltpu.roll(dy * s2, w - ROPE_HALF, 1)


def _dil_prev_limit(has_prev):
    return jnp.where(has_prev, 0, BLOCK)


def _dil_valid(limit):
    row = lax.broadcasted_iota(jnp.int32, (BLOCK, 2 * BLOCK), 0)
    col = lax.broadcasted_iota(jnp.int32, (BLOCK, 2 * BLOCK), 1)
    dist = col - row
    return jnp.logical_and(dist >= jnp.where(col < BLOCK, limit, -BLOCK), dist <= BLOCK)


def _upper_half():
    return lax.broadcasted_iota(jnp.int32, (1, LANES), 1) >= HEAD_DIM


def _stack_heads(x):
    upper = _upper_half()
    return jnp.concatenate([jnp.where(upper, 0, x), jnp.where(upper, x, 0)], axis=0)


def _unstack_heads(y):
    n = y.shape[0] // 2
    return jnp.where(_upper_half(), y[n:], y[:n])


def _head_columns(t):
    return jnp.concatenate([t[:, 0:1], t[:, HEAD_DIM:HEAD_DIM + 1]], axis=0)


def _dil_rows(n, d):
    per = N_BLOCKS // d
    r, lb = n // per, n % per

    def rows(b):
        start = b * (BLOCK * d) + r
        return pl.ds(pl.multiple_of(start, BLOCK), BLOCK) if d == 1 else pl.ds(start, BLOCK, stride=d)

    return rows(lb), rows(jnp.maximum(lb - 1, 0)), lb > 0


def _dil_rotate(q_ref, k_ref, c_ref, s1_ref, s2_ref, q_rot, k_rot):
    tabs = (c_ref[...], s1_ref[...], s2_ref[...])
    q_rot[...] = _rope_apply(q_ref[...], *tabs) * QK_SCALE
    k_rot[...] = _rope_apply(k_ref[...], *tabs)


def _dil_specs():
    def col(base):
        return pl.BlockSpec((SEQ, LANES), lambda p: (0, base // LANES + p))

    table = pl.BlockSpec((SEQ, LANES), lambda p: (0, 0))
    return [col(COL_QA), col(COL_KA), col(COL_VA)], [table] * 3


def _store_columns(blocks, dproj_ref, cols, sem):
    copies = [pltpu.make_async_copy(b, dproj_ref.at[:, pl.ds(pl.multiple_of(c * LANES, LANES), LANES)], sem.at[i])
              for i, (b, c) in enumerate(zip(blocks, cols))]
    for cp in copies:
        cp.start()
    for cp in copies:
        cp.wait()


def _dil_window(d, n, k_rot, v_ref):
    rows, prev, has_prev = _dil_rows(n, d)
    kw, vw = k_rot[rows, :].astype(BF16), v_ref[rows, :].astype(BF16)
    if d == N_BLOCKS:
        row = lax.broadcasted_iota(jnp.int32, (BLOCK, BLOCK), 0)
        valid = lax.broadcasted_iota(jnp.int32, (BLOCK, BLOCK), 1) <= row
    else:
        kw = jnp.concatenate([k_rot[prev, :].astype(BF16), kw], axis=0)
        vw = jnp.concatenate([v_ref[prev, :].astype(BF16), vw], axis=0)
        valid = _dil_valid(_dil_prev_limit(has_prev))
    return rows, prev, kw, vw, jnp.concatenate([valid, valid], axis=0)


def _dil_fwd(proj, tables):
    def body(q_ref, k_ref, v_ref, c_ref, s1_ref, s2_ref, o_ref, lse_ref, q_rot, k_rot):
        upper = _upper_half()
        _dil_rotate(q_ref, k_ref, c_ref, s1_ref, s2_ref, q_rot, k_rot)

        def blocks_of(d):
            def block(n, carry):
                rows, _, kw, vw, valid = _dil_window(d, n, k_rot, v_ref)
                s = jnp.where(valid, _dot_nt(_stack_heads(q_rot[rows, :].astype(BF16)), kw), NEG_INF)
                m = jnp.max(s, axis=-1, keepdims=True)
                p = jnp.exp(s - m)
                den = jnp.sum(p, axis=-1, keepdims=True)
                o_ref[rows, :] = _unstack_heads(_dot_nn((p * (1.0 / den)).astype(BF16), vw))
                lse = m + jnp.log(den)
                lse_ref[rows, :] = jnp.where(upper, lse[BLOCK:], lse[:BLOCK])
                return carry

            lax.fori_loop(0, N_BLOCKS, block, 0, unroll=4)

        for g, d in enumerate(DILATIONS):
            pl.when(pl.program_id(0) // 2 == g)(functools.partial(blocks_of, d))

    qkv, tabs = _dil_specs()
    out = pl.BlockSpec((SEQ, LANES), lambda p: (0, p))
    return pl.pallas_call(
        body, name="dil_attn_fwd", grid=(DIL_WIDTH // LANES,), in_specs=qkv + tabs, out_specs=[out, out],
        out_shape=[pltpu.HBM((SEQ, DIL_WIDTH), F32)] * 2,
        scratch_shapes=[pltpu.VMEM((SEQ, LANES), F32)] * 2,
        compiler_params=pltpu.CompilerParams(dimension_semantics=("parallel",)),
    )(*_in_hbm(proj, proj, proj, *tables))


def _dil_bwd(proj, tables, do, lse, c, dproj, deps=()):
    def body(q_ref, k_ref, v_ref, c_ref, s1_ref, s2_ref, do_ref, lse_ref, cc_ref, dproj_in, *rest):
        dproj_ref, dq_acc, dk_acc, dv_acc, dq_out, dk_out, dv_out, q_rot, k_rot, sem = rest[len(deps):]
        dk_acc[...] = jnp.zeros_like(dk_acc)
        dv_acc[...] = jnp.zeros_like(dv_acc)
        _dil_rotate(q_ref, k_ref, c_ref, s1_ref, s2_ref, q_rot, k_rot)

        def blocks_of(d):
            def block(n, carry):
                rows, prev, kw, vw, valid = _dil_window(d, n, k_rot, v_ref)
                q2 = _stack_heads(q_rot[rows, :].astype(BF16))
                do2 = _stack_heads(do_ref[rows, :].astype(BF16))
                lse_col, c_col = _head_columns(lse_ref[rows, :]), _head_columns(cc_ref[rows, :])
                p = jnp.where(valid, jnp.exp(_dot_nt(q2, kw) - lse_col), 0.0)
                ds = (p * (_dot_nt(do2, vw) + c_col)).astype(BF16)
                dk, dv = _dot_tn(ds, q2), _dot_tn(p.astype(BF16), do2)
                dq_acc[rows, :] = _unstack_heads(_dot_nn(ds, kw)) * QK_SCALE
                if d == N_BLOCKS:
                    dk_acc[rows, :] += dk
                    dv_acc[rows, :] += dv
                else:
                    dk_acc[prev, :] += dk[:BLOCK]
                    dv_acc[prev, :] += dv[:BLOCK]
                    dk_acc[rows, :] += dk[BLOCK:]
                    dv_acc[rows, :] += dv[BLOCK:]
                return carry

            lax.fori_loop(0, N_BLOCKS, block, 0, unroll=4)

        pair = pl.program_id(0)
        for g, d in enumerate(DILATIONS):
            pl.when(pair // 2 == g)(functools.partial(blocks_of, d))
        tabs = (c_ref[...], s1_ref[...], s2_ref[...])
        dq_out[...] = _rope_apply_t(dq_acc[...], *tabs).astype(BF16)
        dk_out[...] = _rope_apply_t(dk_acc[...], *tabs).astype(BF16)
        dv_out[...] = dv_acc[...].astype(BF16)
        _store_columns((dq_out, dk_out, dv_out), dproj_ref,
                       [base // LANES + pair for base in (COL_QA, COL_KA, COL_VA)], sem)

    qkv, tabs = _dil_specs()
    tok = pl.BlockSpec((SEQ, LANES), lambda p: (0, p))
    return pl.pallas_call(
        body, name="dil_attn_bwd", grid=(DIL_WIDTH // LANES,),
        in_specs=qkv + tabs + [tok, tok, tok, _ANY] + [_ANY] * len(deps), out_specs=_ANY,
        out_shape=pltpu.HBM(dproj.shape, dproj.dtype),
        scratch_shapes=[pltpu.VMEM((SEQ, LANES), F32)] * 3 + [pltpu.VMEM((SEQ, LANES), BF16)] * 3
        + [pltpu.VMEM((SEQ, LANES), F32)] * 2 + [pltpu.SemaphoreType.DMA((3,))],
        input_output_aliases={9: 0},
        compiler_params=pltpu.CompilerParams(dimension_semantics=("arbitrary",)),
    )(*_in_hbm(proj, proj, proj, *tables, do, lse, c, dproj), *deps)


def _group_weights(l0, l1, l2):
    m = jnp.maximum(jnp.maximum(l0, l1), l2)
    e0, e1, e2 = jnp.exp(l0 - m), jnp.exp(l1 - m), jnp.exp(l2 - m)
    tot = e0 + e1 + e2
    return e0 / tot, e1 / tot, e2 / tot


def _dil_combine(o, lse, deps=()):
    def fn(o0, o1, o2, l0, l1, l2):
        w0, w1, w2 = _group_weights(l0, l1, l2)
        return w0 * o0 + w1 * o1 + w2 * o2

    w = DIL_OUT_WIDTH
    return _rowwise(fn, "dil_combine", SEQ, 512, [(o, w, g) for g in range(3)] + [(lse, w, g) for g in range(3)], [],
                    [(w, F32)], deps=deps)[0]


def _dil_combine_bwd(d_out, o, lse, deps=()):
    w = DIL_OUT_WIDTH

    def fn(d, o0, o1, o2, l0, l1, l2):
        row = lax.broadcasted_iota(jnp.int32, (w, w), 0) // HEAD_DIM
        col = lax.broadcasted_iota(jnp.int32, (w, w), 1) // HEAD_DIM
        same_head = jnp.where(row == col, 1.0, 0.0).astype(BF16)
        ws = _group_weights(l0, l1, l2)
        dws = [_dot3_nn(d * og, same_head) for og in (o0, o1, o2)]
        mean = ws[0] * dws[0] + ws[1] * dws[1] + ws[2] * dws[2]
        return jnp.concatenate([wg * d for wg in ws], axis=1), jnp.concatenate([-wg * mean for wg in ws], axis=1)

    return _rowwise(fn, "dil_combine_bwd", SEQ, 256,
                    [(d_out, w, 0)] + [(o, w, g) for g in range(3)] + [(lse, w, g) for g in range(3)], [],
                    [(DIL_WIDTH, F32)] * 2, deps=deps)


def _log1p(e):
    u = 1.0 + e
    return jnp.where(u == 1.0, e, jnp.log(u) * (e / (u - 1.0)))


def _fox_gate(proj, b_pad, deps=()):
    def body(f_ref, b_ref, *rest):
        o_ref = rest[-1]
        z = f_ref[...] + b_ref[...]
        logf = (jnp.minimum(z, 0.0) - _log1p(jnp.exp(-jnp.abs(z)))).T[:F_ROWS]
        row = lax.broadcasted_iota(jnp.int32, (BLOCK, BLOCK), 0)
        col = lax.broadcasted_iota(jnp.int32, (BLOCK, BLOCK), 1)
        before = jnp.where(row <= col, 1.0, 0.0).astype(BF16)
        carry = jnp.zeros((F_ROWS, 1), F32)
        for blk in range(N_BLOCKS):
            run = _dot3_nn(logf[:, blk * BLOCK:(blk + 1) * BLOCK], before) + carry
            o_ref[:, blk * BLOCK:(blk + 1) * BLOCK] = run
            carry = run[:, BLOCK - 1:BLOCK]

    return pl.pallas_call(
        body, name="fox_gate", grid=(1,),
        in_specs=[pl.BlockSpec((SEQ, LANES), lambda i: (0, COL_F // LANES)), pl.BlockSpec((1, LANES), lambda i: (0, 0))]
        + [_ANY] * len(deps),
        out_specs=pl.BlockSpec((F_ROWS, SEQ), lambda i: (0, 0)),
        out_shape=pltpu.HBM((F_ROWS, SEQ), F32),
    )(*_in_hbm(proj, b_pad), *deps)


def _fox_gate_bwd(d_cum, proj, b_pad, dproj):
    def body(d_ref, f_ref, b_ref, dproj_ref, dz_ref, db_ref):
        row = lax.broadcasted_iota(jnp.int32, (BLOCK, BLOCK), 0)
        col = lax.broadcasted_iota(jnp.int32, (BLOCK, BLOCK), 1)
        after = jnp.where(row >= col, 1.0, 0.0).astype(BF16)
        carry = jnp.zeros((F_ROWS, 1), F32)
        parts = [None] * N_BLOCKS
        for blk in reversed(range(N_BLOCKS)):
            run = _dot3_nn(d_ref[:, blk * BLOCK:(blk + 1) * BLOCK], after) + carry
            parts[blk] = run
            carry = run[:, 0:1]
        dlogf = jnp.concatenate(parts, axis=1)
        dlogf = jnp.concatenate([dlogf, jnp.zeros((LANES - F_ROWS, SEQ), F32)], axis=0).T
        dz = dlogf * _sigmoid(-(f_ref[...] + b_ref[...]))
        dz_ref[...] = dz.astype(BF16)
        db_ref[...] = jnp.sum(dz, axis=0, keepdims=True)

    f_cols = pl.BlockSpec((SEQ, LANES), lambda i: (0, COL_F // LANES))
    return pl.pallas_call(
        body, name="fox_gate_bwd", grid=(1,),
        in_specs=[pl.BlockSpec((F_ROWS, SEQ), lambda i: (0, 0)), f_cols, pl.BlockSpec((1, LANES), lambda i: (0, 0)), _ANY],
        out_specs=[f_cols, pl.BlockSpec((1, LANES), lambda i: (0, 0))],
        out_shape=[pltpu.HBM(dproj.shape, dproj.dtype), pltpu.HBM((1, LANES), F32)],
        input_output_aliases={3: 0},
    )(*_in_hbm(d_cum, proj, b_pad, dproj))


FOX_TILE = 256
FOX_TILES = SEQ // FOX_TILE


def _row_to_col(row):
    n = row.shape[1]
    eye = lax.broadcasted_iota(jnp.int32, (n, n), 0) == lax.broadcasted_iota(jnp.int32, (n, n), 1)
    return jnp.sum(jnp.where(eye, row, 0.0), axis=1, keepdims=True)


def _fox_bias(f_row, i):
    t = FOX_TILE
    ext = (i + 1) * t
    bias = _row_to_col(f_row[:, i * t:(i + 1) * t]) - f_row[:, :ext]
    row = lax.broadcasted_iota(jnp.int32, (t, ext), 0) + i * t
    col = lax.broadcasted_iota(jnp.int32, (t, ext), 1)
    return bias, col <= row


def _fox_specs():
    qkv = [pl.BlockSpec((SEQ, LANES), lambda p, base=base: (0, base // LANES + p)) for base in (COL_QB, COL_KB, COL_VB)]
    return qkv, pl.BlockSpec((F_ROWS, SEQ), lambda p: (0, 0))


def _fox_fwd(proj, f_rows):
    t = FOX_TILE

    def body(q_ref, k_ref, v_ref, f_ref, o_ref, lse_ref):
        pair = pl.program_id(0)
        upper = _upper_half()
        k16, v16 = k_ref[...].astype(BF16), v_ref[...].astype(BF16)
        f_row = [f_ref[pl.ds(2 * pair + e, 1), :] for e in range(2)]
        for i in range(FOX_TILES):
            ext = (i + 1) * t
            q_tile = (q_ref[i * t:(i + 1) * t, :] * QK_SCALE).astype(BF16)
            s2 = _dot_nt(_stack_heads(q_tile), k16[:ext])
            pns, lses = [], []
            for e in range(2):
                bias, causal = _fox_bias(f_row[e], i)
                s = jnp.where(causal, s2[e * t:(e + 1) * t] + bias, NEG_INF)
                m = jnp.max(s, axis=-1, keepdims=True)
                p = jnp.exp(s - m)
                den = jnp.sum(p, axis=-1, keepdims=True)
                pns.append((p * (1.0 / den)).astype(BF16))
                lses.append(m + jnp.log(den))
            o_ref[i * t:(i + 1) * t, :] = _unstack_heads(_dot_nn(jnp.concatenate(pns, axis=0), v16[:ext]))
            lse_ref[i * t:(i + 1) * t, :] = jnp.where(upper, lses[1], lses[0])

    qkv, f_spec = _fox_specs()
    tok = pl.BlockSpec((SEQ, LANES), lambda p: (0, p))
    return pl.pallas_call(
        body, name="fox_attn_fwd", grid=(FOX_WIDTH // LANES,),
        in_specs=qkv + [f_spec], out_specs=[tok, tok],
        out_shape=[pltpu.HBM((SEQ, FOX_WIDTH), F32)] * 2,
        compiler_params=pltpu.CompilerParams(
            dimension_semantics=("parallel",), vmem_limit_bytes=_vmem_limit(8 * t * SEQ * 4)),
    )(*_in_hbm(proj, proj, proj, f_rows))


def _fox_bwd(proj, do, lse, f_rows, dproj):
    t = FOX_TILE

    def body(q_ref, k_ref, v_ref, f_ref, do_ref, lse_ref, dproj_in, dproj_ref, df_ref, dk_acc, dv_acc,
             dq_out, dk_out, dv_out, sem):
        pair = pl.program_id(0)
        upper = _upper_half()
        k16, v16 = k_ref[...].astype(BF16), v_ref[...].astype(BF16)
        f_row = [f_ref[pl.ds(2 * pair + e, 1), :] for e in range(2)]
        dk_acc[...] = jnp.zeros_like(dk_acc)
        dv_acc[...] = jnp.zeros_like(dv_acc)
        df_ref[...] = jnp.zeros_like(df_ref)
        for i in range(FOX_TILES):
            ext = (i + 1) * t
            q_tile = (q_ref[i * t:(i + 1) * t, :] * QK_SCALE).astype(BF16)
            do_tile = do_ref[i * t:(i + 1) * t, :]
            lse_t = lse_ref[i * t:(i + 1) * t, :]
            q2, do2 = _stack_heads(q_tile), _stack_heads(do_tile)
            s2, dp2 = _dot_nt(q2, k16[:ext]), _dot_nt(do2, v16[:ext])
            ps, dss = [], []
            for e in range(2):
                bias, causal = _fox_bias(f_row[e], i)
                s = s2[e * t:(e + 1) * t] + bias
                p = jnp.where(causal, jnp.exp(s - lse_t[:, e * HEAD_DIM:e * HEAD_DIM + 1]), 0.0)
                dp = dp2[e * t:(e + 1) * t]
                ds = p * (dp - jnp.sum(p * dp, axis=-1, keepdims=True))
                df_ref[0, e:e + 1, :ext] -= jnp.sum(ds, axis=0, keepdims=True)
                ps.append(p.astype(BF16))
                dss.append(ds.astype(BF16))
            ds2, p2 = jnp.concatenate(dss, axis=0), jnp.concatenate(ps, axis=0)
            dq_out[i * t:(i + 1) * t, :] = (_unstack_heads(_dot_nn(ds2, k16[:ext])) * QK_SCALE).astype(BF16)
            dk_acc[:ext, :] += _dot_tn(ds2, q2)
            dv_acc[:ext, :] += _dot_tn(p2, do2)
        dk_out[...] = dk_acc[...].astype(BF16)
        dv_out[...] = dv_acc[...].astype(BF16)
        _store_columns((dq_out, dk_out, dv_out), dproj_ref, [base // LANES + pair for base in (COL_QB, COL_KB, COL_VB)],
                       sem)

    qkv, f_spec = _fox_specs()
    tok = pl.BlockSpec((SEQ, LANES), lambda p: (0, p))
    return pl.pallas_call(
        body, name="fox_attn_bwd", grid=(FOX_WIDTH // LANES,),
        in_specs=qkv + [f_spec, tok, tok, _ANY],
        out_specs=[_ANY, pl.BlockSpec((1, SUBLANES, SEQ), lambda p: (p, 0, 0))],
        out_shape=[pltpu.HBM(dproj.shape, dproj.dtype),
                   pltpu.HBM((FOX_WIDTH // LANES, SUBLANES, SEQ), F32)],
        scratch_shapes=[pltpu.VMEM((SEQ, LANES), F32)] * 2 + [pltpu.VMEM((SEQ, LANES), BF16)] * 3
        + [pltpu.SemaphoreType.DMA((3,))],
        input_output_aliases={6: 0},
        compiler_params=pltpu.CompilerParams(
            dimension_semantics=("arbitrary",), vmem_limit_bytes=_vmem_limit(10 * t * SEQ * 4)),
    )(*_in_hbm(proj, proj, proj, f_rows, do, lse, dproj))


MIX_TILE = 256


def _mix_out(out_a, out_b, proj, x, wt_pa, wt_pb, w_out, g_post, g_ffn_pre):
    tm = MIX_TILE

    def body(a_ref, b_ref, ga_ref, gb_ref, x_ref, wpa_ref, wpb_ref, wo_ref, g2_ref, g3_ref,
             merged_ref, mix_ref, x1_ref, h2_ref):
        ya = _dot_nn(a_ref[...].astype(BF16), wpa_ref[...])
        yb = _dot_nn(b_ref[...].astype(BF16), wpb_ref[...])
        merged = (_sigmoid(ga_ref[...]) * ya + _sigmoid(gb_ref[...]) * yb).astype(BF16)
        merged_ref[...] = merged
        mix = _dot_nn(merged, wo_ref[...])
        mix_ref[...] = mix
        x1 = x_ref[...] + mix * _rms_scale(mix) * g2_ref[...]
        x1_ref[...] = x1
        h2_ref[...] = (x1 * _rms_scale(x1) * g3_ref[...]).astype(BF16)

    def rows(w, cb=0):
        return pl.BlockSpec((tm, w), lambda i, cb=cb: (i, cb))

    def whole(a):
        return pl.BlockSpec(a.shape, lambda i: (0, 0))

    d = D_MODEL
    blk = _nbytes((tm, d), F32) * 6 + sum(_nbytes(a.shape, BF16) for a in (wt_pa, wt_pb, w_out))
    return pl.pallas_call(
        body, name="mix_out", grid=(SEQ // tm,),
        in_specs=[rows(DIL_OUT_WIDTH), rows(FOX_WIDTH), rows(d, COL_GA // d), rows(d, COL_GB // d), rows(d),
                  whole(wt_pa), whole(wt_pb), whole(w_out), whole(g_post), whole(g_ffn_pre)],
        out_specs=[rows(d)] * 4,
        out_shape=[pltpu.HBM((SEQ, d), dt) for dt in (BF16, F32, F32, BF16)],
        compiler_params=pltpu.CompilerParams(dimension_semantics=("parallel",), vmem_limit_bytes=_vmem_limit(blk)),
    )(out_a, out_b, proj, proj, x, wt_pa, wt_pb, w_out, g_post, g_ffn_pre)


def _mix_out_bwd(dmix, out_a, out_b, proj, wt_pa, wt_pb, w_out, deps=()):
    tm = MIX_TILE

    def body(dm_ref, a_ref, b_ref, ga_ref, gb_ref, wpa_ref, wpb_ref, wo_ref, *rest):
        dproj_ref, dya_ref, dyb_ref, da_ref, db_ref = rest[len(deps):]
        dmerged = _dot_nt(dm_ref[...], wo_ref[...])
        ya = _dot_nn(a_ref[...].astype(BF16), wpa_ref[...])
        yb = _dot_nn(b_ref[...].astype(BF16), wpb_ref[...])
        sa, sb = _sigmoid(ga_ref[...]), _sigmoid(gb_ref[...])
        dproj_ref[:, COL_GA:COL_GA + D_MODEL] = (dmerged * ya * (sa * (1.0 - sa))).astype(BF16)
        dproj_ref[:, COL_GB:COL_GB + D_MODEL] = (dmerged * yb * (sb * (1.0 - sb))).astype(BF16)
        dproj_ref[:, COL_GB + D_MODEL:] = jnp.zeros((tm, COL_QA - COL_GB - D_MODEL), BF16)
        dya = (dmerged * sa).astype(BF16)
        dyb = (dmerged * sb).astype(BF16)
        dya_ref[...] = dya
        dyb_ref[...] = dyb
        da_ref[...] = _dot_nt(dya, wpa_ref[...])
        db_ref[...] = _dot_nt(dyb, wpb_ref[...]).astype(BF16)

    def rows(w, cb=0):
        return pl.BlockSpec((tm, w), lambda i, cb=cb: (i, cb))

    def whole(a):
        return pl.BlockSpec(a.shape, lambda i: (0, 0))

    d = D_MODEL
    blk = _nbytes((tm, d), F32) * 8 + sum(_nbytes(a.shape, BF16) for a in (wt_pa, wt_pb, w_out))
    return pl.pallas_call(
        body, name="mix_out_bwd", grid=(SEQ // tm,),
        in_specs=[rows(d), rows(DIL_OUT_WIDTH), rows(FOX_WIDTH), rows(d, COL_GA // d), rows(d, COL_GB // d),
                  whole(wt_pa), whole(wt_pb), whole(w_out)] + [_ANY] * len(deps),
        out_specs=[rows(COL_QA)] + [rows(d)] * 2 + [rows(DIL_OUT_WIDTH), rows(FOX_WIDTH)],
        out_shape=[pltpu.HBM((SEQ, PROJ_COLS), BF16)] + [pltpu.HBM((SEQ, d), BF16)] * 2
        + [pltpu.HBM((SEQ, DIL_OUT_WIDTH), F32), pltpu.HBM((SEQ, FOX_WIDTH), BF16)],
        compiler_params=pltpu.CompilerParams(dimension_semantics=("parallel",), vmem_limit_bytes=_vmem_limit(blk)),
    )(dmix, out_a, out_b, proj, proj, wt_pa, wt_pb, w_out, *deps)


FFN_TM, FFN_TN = 2048, 256


def _ffn_up(h2, wt_gate, wt_up):
    tm, tn = FFN_TM, FFN_TN

    def body(h_ref, wg_ref, wu_ref, gate_ref, up_ref, act_ref):
        for rows in (slice(0, tm // 2), slice(tm // 2, tm)):
            gate = _dot_nt(h_ref[rows, :], wg_ref[...])
            up = _dot_nt(h_ref[rows, :], wu_ref[...])
            gate_ref[rows, :] = gate
            up_ref[rows, :] = up
            act_ref[rows, :] = (gate * _sigmoid(gate) * up).astype(BF16)

    tile = pl.BlockSpec((tm, tn), lambda i, j: (i, j))
    w_spec = pl.BlockSpec((tn, D_MODEL), lambda i, j: (j, 0))
    return pl.pallas_call(
        body, name="ffn_up", grid=(SEQ // tm, D_FF // tn),
        in_specs=[pl.BlockSpec((tm, D_MODEL), lambda i, j: (i, 0)), w_spec, w_spec],
        out_specs=[tile, tile, tile],
        out_shape=[pltpu.HBM((SEQ, D_FF), dt) for dt in (F32, F32, BF16)],
        compiler_params=pltpu.CompilerParams(
            dimension_semantics=("parallel", "parallel"), vmem_limit_bytes=_vmem_limit(8 * 2**20)),
    )(h2, wt_gate, wt_up)


def _ffn_act_bwd(dff, w_down, gate, up):
    tm, tn = FFN_TM, FFN_TN

    def body(d_ref, wd_ref, gate_ref, up_ref, dgate_ref, dup_ref):
        for rows in (slice(0, tm // 2), slice(tm // 2, tm)):
            dact = _dot_nt(d_ref[rows, :], wd_ref[...])
            gate = gate_ref[rows, :]
            sg = _sigmoid(gate)
            dgate_ref[rows, :] = (dact * up_ref[rows, :] * (sg * (1.0 + gate * (1.0 - sg)))).astype(BF16)
            dup_ref[rows, :] = (dact * (gate * sg)).astype(BF16)

    tile = pl.BlockSpec((tm, tn), lambda i, j: (i, j))
    return pl.pallas_call(
        body, name="ffn_act_bwd", grid=(SEQ // tm, D_FF // tn),
        in_specs=[pl.BlockSpec((tm, D_MODEL), lambda i, j: (i, 0)), pl.BlockSpec((tn, D_MODEL), lambda i, j: (j, 0)),
                  tile, tile],
        out_specs=[tile, tile],
        out_shape=[pltpu.HBM((SEQ, D_FF), BF16)] * 2,
        compiler_params=pltpu.CompilerParams(
            dimension_semantics=("parallel", "parallel"), vmem_limit_bytes=_vmem_limit(8 * 2**20)),
    )(dff, w_down, gate, up)


EPILOGUE_TM = 512


def _loss_head(act, w_down, x1, target, g_post):
    def fn(ff, x1, tgt, g):
        r = _rms_scale(ff)
        nrm = ff * r
        err = (x1 + nrm * g) - tgt
        loss = 0.5 * jnp.sum(jnp.mean(err * err, axis=-1, keepdims=True), axis=0, keepdims=True)
        dy = err * (1.0 / D_MODEL)
        u = dy * g
        dff = r * u - ff * (r * r * r) * jnp.mean(u * ff, axis=-1, keepdims=True)
        return dy, dff, jnp.broadcast_to(loss, (1, LANES)), jnp.sum(dy * nrm, axis=0, keepdims=True)

    d = D_MODEL
    return _matmul_rowwise([(act, w_down)], fn, "ffn_down_loss", EPILOGUE_TM, [(x1, d, 0), (target, d, 0)], [g_post],
                           [(d, F32), (d, BF16)], [LANES, d])


def _post_ffn_bwd(dgate, wt_gate, dup, wt_up, x1, dy, mix, g_ffn_pre, g_mix_post, deps=()):
    def fn(dh2, x1, dy, mix, g3, g2):
        dx, dg3 = _rms_bwd(x1, dh2, g3)
        dx1 = dy + dx
        dmix, dg2 = _rms_bwd(mix, dx1, g2)
        return dx1, dmix, dg3, dg2

    d = D_MODEL
    return _matmul_rowwise([(dgate, wt_gate), (dup, wt_up)], fn, "ffn_up_bwd", EPILOGUE_TM,
                           [(x1, d, 0), (dy, d, 0), (mix, d, 0)], [g_ffn_pre, g_mix_post],
                           [(d, F32), (d, BF16)], [d, d], deps=deps)


def _input_bwd(dproj, wt_r, x, dx1, g_pre, deps=()):
    def fn(dh, x, dx1, g):
        dx, dg = _rms_bwd(x, dh, g)
        return dx1 + dx, dg

    d = D_MODEL
    return _matmul_rowwise([(dproj, wt_r)], fn, "in_proj_bwd", EPILOGUE_TM, [(x, d, 0), (dx1, d, 0)], [g_pre],
                           [(d, F32)], [d], deps=deps)


def _adam_math(w, g, m, v):
    m = ADAM_B1 * m + (1.0 - ADAM_B1) * g
    v = ADAM_B2 * v + (1.0 - ADAM_B2) * (g * g)
    m_hat = m / (1.0 - ADAM_B1 ** ADAM_STEP)
    v_hat = v / (1.0 - ADAM_B2 ** ADAM_STEP)
    delta = -ADAM_LR * (m_hat / (jnp.sqrt(v_hat) + ADAM_EPS) + ADAM_WD * w)
    return delta, m, v


def _adam(w, mine, recv, m, v, name):
    r, c = w.shape
    tc = _col_tile(r, c)

    def body(w_ref, p_ref, r_ref, m_ref, v_ref, g_ref, d_ref, nm_ref, nv_ref):
        g = ((p_ref[...] + r_ref[0].astype(F32)) + r_ref[1].astype(F32)) + r_ref[2].astype(F32)
        g_ref[...] = g
        d_ref[...], nm_ref[...], nv_ref[...] = _adam_math(w_ref[...], g, m_ref[...], v_ref[...])

    spec = pl.BlockSpec((r, tc), lambda j: (0, j))
    return pl.pallas_call(
        body, name=name, grid=(c // tc,),
        in_specs=[spec, spec, pl.BlockSpec((3, r, tc), lambda j: (0, 0, j)), spec, spec], out_specs=[spec] * 4,
        out_shape=[pltpu.HBM((r, c), F32)] * 4,
        compiler_params=pltpu.CompilerParams(dimension_semantics=("parallel",)),
    )(*_in_hbm(w, mine, recv, m, v))


def _adam_small(gathered, ws, ms, vs, loss_parts):
    n = len(ws)

    def body(*refs):
        outs = refs[4 * n + 1:]
        loss = refs[4 * n][0]
        for dev in range(1, N_DEV):
            loss = loss + refs[4 * n][dev]
        outs[4 * n][...] = loss
        for i in range(n):
            ga_ref, w_ref, m_ref, v_ref = (refs[j * n + i] for j in range(4))
            g = ga_ref[0]
            for dev in range(1, N_DEV):
                g = g + ga_ref[dev]
            g = g[:, :w_ref.shape[1]]
            outs[4 * i][...] = g
            outs[4 * i + 1][...], outs[4 * i + 2][...], outs[4 * i + 3][...] = _adam_math(
                w_ref[...], g, m_ref[...], v_ref[...])

    out_shape = [pltpu.HBM(w.shape, F32) for w in ws for _ in range(4)]
    out_shape.append(pltpu.HBM((1, LANES), F32))
    out = pl.pallas_call(body, name="adam_small", out_shape=out_shape)(*gathered, *ws, *ms, *vs, loss_parts)
    return [out[4 * i:4 * i + 4] for i in range(n)], out[4 * n]


_PROJ_SEGMENTS = ((3848, 5896), (None, COL_QA - 2 * D_MODEL), (0, 3840), (3840, 3848), (None, PROJ_COLS - COL_F - 8))


def _proj_weight_t(gathered):
    w = gathered.reshape(IN_COLS, D_MODEL)
    return jnp.concatenate([jnp.zeros((hi, D_MODEL), w.dtype) if lo is None else w[lo:hi] for lo, hi in _PROJ_SEGMENTS],
                           axis=0)


def _proj_weight_grad_slots(dwt_r):
    starts, at = [], 0
    for lo, hi in _PROJ_SEGMENTS:
        if lo is not None:
            starts.append((lo, hi, at))
        at += hi if lo is None else hi - lo
    slots = []
    for dev in range(N_DEV):
        pieces, lo, end = [], dev * IN_SHARD, (dev + 1) * IN_SHARD
        for seg_lo, seg_hi, seg_at in sorted(starts):
            a, b = max(lo, seg_lo), min(end, seg_hi)
            if a < b:
                pieces.append(dwt_r[seg_at + a - seg_lo:seg_at + b - seg_lo])
        slots.append(pieces[0] if len(pieces) == 1 else jnp.concatenate(pieces, axis=0))
    return jnp.stack(slots)


def kernel(x, w_in, w_proj_a, w_proj_b, w_out, b_forget, w_ffn_gate, w_ffn_up, w_ffn_down, norm_mix_pre, norm_mix_post, norm_ffn_pre, norm_ffn_post, loss_target, m_w_in, m_w_proj_a, m_w_proj_b, m_w_out, m_b_forget, m_w_ffn_gate, m_w_ffn_up, m_w_ffn_down, m_norm_mix_pre, m_norm_mix_post, m_norm_ffn_pre, m_norm_ffn_post, v_w_in, v_w_proj_a, v_w_proj_b, v_w_out, v_b_forget, v_w_ffn_gate, v_w_ffn_up, v_w_ffn_down, v_norm_mix_pre, v_norm_mix_post, v_norm_ffn_pre, v_norm_ffn_post):
    d = D_MODEL
    names = ("w_in", "w_proj_a", "w_proj_b", "w_out", "w_ffn_gate", "w_ffn_up", "w_ffn_down")
    col_sharded = ("w_in", "w_ffn_gate", "w_ffn_up")

    def row_shards(arrs):
        return {k: (a[0].T if k in col_sharded else a[0]) for k, a in zip(names, arrs)}

    shards = row_shards((w_in, w_proj_a, w_proj_b, w_out, w_ffn_gate, w_ffn_up, w_ffn_down))
    moments_m = row_shards((m_w_in, m_w_proj_a, m_w_proj_b, m_w_out, m_w_ffn_gate, m_w_ffn_up, m_w_ffn_down))
    moments_v = row_shards((v_w_in, v_w_proj_a, v_w_proj_b, v_w_out, v_w_ffn_gate, v_w_ffn_up, v_w_ffn_down))
    pos = jnp.stack([lax.axis_index("c"), 2 * lax.axis_index("x") + lax.axis_index("y")]).astype(jnp.int32)
    x2, target = x[0], loss_target[0]

    me = 4 * lax.axis_index("x") + 2 * lax.axis_index("y") + lax.axis_index("c")
    mid_names, ffn_names = names[1:4], names[4:]
    first_names, later_names = names[:1], names[1:]
    shards16 = {k: shards[k].astype(BF16) for k in names}

    def landing(k):
        return lax.dynamic_update_slice(lax.empty((N_DEV,) + shards[k].shape, BF16), shards16[k][None], (me, 0, 0))

    ag_first = _exchange_start("ag_first_chips_start", _gather_chips_copies, [shards16[k] for k in first_names],
                               [landing(k) for k in first_names], 3 * len(first_names))
    h = _rowwise(lambda xb, g: xb * _rms_scale(xb) * g, "norm_mix_pre", SEQ, 256, [(x2, d, 0)], [norm_mix_pre],
                 [(d, BF16)], deps=[ag_first.token])[0]
    later_lands = [landing(k) for k in later_names]
    _, lands = _exchange_wait("ag_first_chips_wait", ag_first,
                              [h, shards["w_in"], moments_m["w_in"], moments_v["w_in"], *later_lands])
    ag_first = _exchange_start("ag_first_sibling_start", _gather_sibling_copies, [], lands, 4 * len(first_names))
    ag_later = _exchange_start("ag_later_chips_start", _gather_chips_copies, [shards16[k] for k in later_names],
                               later_lands, 3 * len(later_names), after=[ag_first.token])
    gathered = dict(zip(first_names, _exchange_wait("ag_first_sibling_wait", ag_first, [ag_later.token])[1]))
    wt_r = _proj_weight_t(gathered["w_in"])

    proj = _matmul([(h, wt_r)], "nt", F32, "in_proj", 1024, 896, 1024)
    tables = _rope_tables()
    o_dil, lse_dil = _dil_fwd(proj, tables)
    out_a = _dil_combine(o_dil, lse_dil)
    _, lands = _exchange_wait("ag_later_chips_wait", ag_later, [out_a])
    ag_later = _exchange_start("ag_later_sibling_start", _gather_sibling_copies, [], lands, 4 * len(later_names))

    b_pad = jnp.pad(b_forget, ((0, 0), (0, LANES - N_FOX_HEADS)))
    f_rows = _fox_gate(proj, b_pad, deps=[ag_later.token])
    out_b, lse_fox = _fox_fwd(proj, f_rows)

    gathered = dict(zip(later_names, _exchange_wait("ag_later_sibling_wait", ag_later, [out_b])[1]))
    wt_pa = gathered["w_proj_a"].transpose(1, 0, 2).reshape(DIL_OUT_WIDTH, d)
    wt_pb = gathered["w_proj_b"].transpose(1, 0, 2).reshape(FOX_WIDTH, d)
    w_o = gathered["w_out"].reshape(d, d)
    wt_g = gathered["w_ffn_gate"].reshape(D_FF, d)
    wt_u = gathered["w_ffn_up"].reshape(D_FF, d)
    w_d = gathered["w_ffn_down"].reshape(D_FF, d)
    merged, mix, x1, h2 = _mix_out(out_a, out_b, proj, x2, wt_pa, wt_pb, w_o, norm_mix_post, norm_ffn_pre)

    gate, up, act = _ffn_up(h2, wt_g, wt_u)
    dy, dff, loss_part, dg_ffn_post = _loss_head(act, w_d, x1, target, norm_ffn_post)

    dgate, dup = _ffn_act_bwd(dff, w_d, gate, up)
    grads_t = {}
    grads_t["w_ffn_down"] = _matmul([(act, dff)], "tn", F32, "grad_w_ffn_down", 1408, 512, 2048)
    grads_t["w_ffn_gate"] = _matmul([(dgate, h2)], "tn", F32, "grad_w_ffn_gate", 1408, 512, 2048)
    grads_t["w_ffn_up"] = _matmul([(dup, h2)], "tn", F32, "grad_w_ffn_up", 1408, 512, 2048)
    rs_ffn = _ReduceScatter("ffn", {k: grads_t[k] for k in ffn_names}, pos)
    dx1, dmix, dg_ffn_pre, dg_mix_post = _post_ffn_bwd(dgate, wt_g, dup, wt_u, x1, dy, mix, norm_ffn_pre, norm_mix_post,
                                                       deps=[rs_ffn.token])
    rs_ffn.start_chips([dmix])

    dproj, dya, dyb, d_out_a, d_out_b = _mix_out_bwd(dmix, out_a, out_b, proj, wt_pa, wt_pb, w_o, deps=[rs_ffn.token])
    grads_t["w_out"] = _matmul([(merged, dmix)], "tn", F32, "grad_w_out", 1024, 1024, 1024)
    def column_slots(g):
        return g.reshape(g.shape[0], N_DEV, LANES).transpose(1, 0, 2)

    grads_t["w_proj_a"] = column_slots(_matmul([(out_a, dya)], "tn", F32, "grad_w_proj_a", DIL_OUT_WIDTH, 1024, SEQ))
    grads_t["w_proj_b"] = column_slots(_matmul([(out_b, dyb)], "tn", F32, "grad_w_proj_b", FOX_WIDTH, 1024, SEQ))
    rs_mid = _ReduceScatter("mid", {k: grads_t[k] for k in mid_names}, pos)

    do_dil, c_dil = _dil_combine_bwd(d_out_a, o_dil, lse_dil, deps=[rs_mid.token])
    rs_mid.start_chips([c_dil])
    dproj, d_cum = _fox_bwd(proj, d_out_b, lse_fox, f_rows, dproj)
    d_cum_rows = jnp.pad(d_cum[:, :2].reshape(N_FOX_HEADS, SEQ), ((0, F_ROWS - N_FOX_HEADS), (0, 0)))
    dproj, db_part = _fox_gate_bwd(d_cum_rows, proj, b_pad, dproj)
    dproj = _dil_bwd(proj, tables, do_dil, lse_dil, c_dil, dproj, deps=[rs_mid.token])

    dwt_r = _matmul([(dproj, h)], "tn", F32, "grad_w_in", 896, 1024, 2048)
    rs_in = _ReduceScatter("in", {"w_in": _proj_weight_grad_slots(dwt_r)}, pos)
    def finish(rs, after):
        return {k: _adam(shards[k], mine, recv, moments_m[k], moments_v[k], "adam_" + k)
                for k, (mine, recv) in rs.finish(after).items()}

    done = finish(rs_ffn, [rs_in.token])
    rs_in.start_chips([done[k][0] for k in ffn_names])
    grad_x, dg_mix_pre = _input_bwd(dproj, wt_r, x2, dx1, norm_mix_pre, deps=[rs_in.token])
    done.update(finish(rs_mid, [grad_x]))

    small_all = _all_gather([dg_mix_pre, dg_mix_post, dg_ffn_pre, dg_ffn_post, db_part, loss_part],
                            "small_grads_all_gather", deps=[done[k][0] for k in mid_names])
    small, loss = _adam_small(small_all[:5], [norm_mix_pre, norm_mix_post, norm_ffn_pre, norm_ffn_post, b_forget],
                              [m_norm_mix_pre, m_norm_mix_post, m_norm_ffn_pre, m_norm_ffn_post, m_b_forget],
                              [v_norm_mix_pre, v_norm_mix_post, v_norm_ffn_pre, v_norm_ffn_post, v_b_forget],
                              small_all[5])

    done.update(finish(rs_in, [small[0][0]]))

    def leaves(i):
        def nat(k):
            a = done[k][i]
            return (a.T if k in col_sharded else a)[None]

        return [nat("w_in"), nat("w_proj_a"), nat("w_proj_b"), nat("w_out"), small[4][i],
                nat("w_ffn_gate"), nat("w_ffn_up"), nat("w_ffn_down"), *[small[r][i] for r in range(4)]]

    return (loss[0, 0], grad_x[None], *leaves(0), *leaves(1), *leaves(2), *leaves(3))
```

```python
import functools
import math

import jax
import jax.numpy as jnp
import numpy as np
from jax import lax
from jax.experimental import pallas as pl
from jax.experimental.pallas import tpu as pltpu

F32 = jnp.float32
BF16 = jnp.bfloat16
MESH = pl.DeviceIdType.MESH

D_MODEL = 1024
SEQ = 2048
HEAD_DIM = 64
BLOCK = 128
N_BLOCKS = SEQ // BLOCK
DILATIONS = (1, 4, 16)
N_FOX_HEADS = 8
DIL_WIDTH = 768
DIL_OUT_WIDTH = 256
FOX_WIDTH = 512
D_FF = 2816
ROPE_THETA = 500000.0
ROPE_DIM = HEAD_DIM // 4
ROPE_HALF = ROPE_DIM // 2
EPS = 1e-6
NEG_INF = -1e30
QK_SCALE = 1.0 / math.sqrt(HEAD_DIM)
IN_COLS = 5896
N_DEV = 8
IN_SHARD = IN_COLS // N_DEV

ADAM_LR = 0.001
ADAM_B1 = 0.9
ADAM_B2 = 0.999
ADAM_EPS = 1e-08
ADAM_WD = 0.01
ADAM_STEP = 10

V7X_VMEM_BYTES = 64 * 2**20
LANES = 128
SUBLANES = 8

PROJ_COLS = 6272
COL_GA, COL_GB = 0, 1024
COL_QA, COL_KA, COL_VA = 2304, 3072, 3840
COL_QB, COL_KB, COL_VB = 4608, 5120, 5632
COL_F = 6144
F_ROWS = 16


def _vmem_limit(block_bytes):
    want = 2 * block_bytes + 16 * 2**20
    return int(min(max(want, 32 * 2**20), V7X_VMEM_BYTES - 8 * 2**20))


def _nbytes(shape, dtype):
    return math.prod(shape) * jnp.dtype(dtype).itemsize


def _in_hbm(*arrays):
    return [pltpu.with_memory_space_constraint(a, pltpu.HBM) for a in arrays]


def _dot(a, b, dims):
    return lax.dot_general(a, b, (dims, ((), ())), preferred_element_type=F32)


def _dot_nn(a, b):
    return _dot(a, b, ((1,), (0,)))


def _dot_nt(a, b):
    return _dot(a, b, ((1,), (1,)))


def _dot_tn(a, b):
    return _dot(a, b, ((0,), (0,)))


def _sigmoid(z):
    return 1.0 / (1.0 + jnp.exp(-z))


def _split3(x):
    hi = x.astype(BF16)
    r1 = x - hi.astype(F32)
    mid = r1.astype(BF16)
    lo = (r1 - mid.astype(F32)).astype(BF16)
    return hi, mid, lo


def _dot3_nn(x, ones_matrix):
    hi, mid, lo = _split3(x)
    return (_dot_nn(hi, ones_matrix) + _dot_nn(mid, ones_matrix)) + _dot_nn(lo, ones_matrix)


def _rowwise(fn, name, n_rows, tm, row_ins, bcast_ins, row_outs, acc_outs=(), deps=()):
    n_in = len(row_ins) + len(bcast_ins)
    n_ro = len(row_outs)

    def body(*refs):
        res = fn(*[r[...] for r in refs[:n_in]])
        if not isinstance(res, (tuple, list)):
            res = (res,)
        outs = refs[n_in + len(deps):]
        for r, o in zip(res[:n_ro], outs[:n_ro]):
            o[...] = r.astype(o.dtype)
        first = pl.program_id(0) == 0
        for r, o in zip(res[n_ro:], outs[n_ro:]):
            _accumulate(o, r, first)

    in_specs = [pl.BlockSpec((tm, w), lambda i, cb=cb: (i, cb)) for _, w, cb in row_ins]
    in_specs += [pl.BlockSpec(a.shape, lambda i: (0, 0)) for a in bcast_ins]
    in_specs += [pl.BlockSpec(memory_space=pl.ANY)] * len(deps)
    out_specs = [pl.BlockSpec((tm, w), lambda i: (i, 0)) for w, _ in row_outs]
    out_specs += [pl.BlockSpec((1, w), lambda i: (0, 0)) for w in acc_outs]
    out_shape = [pltpu.HBM((n_rows, w), dt) for w, dt in row_outs]
    out_shape += [pltpu.HBM((1, w), F32) for w in acc_outs]
    blk = sum(_nbytes((tm, w), a.dtype) for a, w, _ in row_ins) + sum(_nbytes((tm, w), dt) for w, dt in row_outs)
    return pl.pallas_call(
        body, name=name, grid=(n_rows // tm,), in_specs=in_specs, out_specs=out_specs, out_shape=out_shape,
        compiler_params=pltpu.CompilerParams(
            dimension_semantics=("arbitrary" if acc_outs else "parallel",), vmem_limit_bytes=_vmem_limit(3 * blk)),
    )(*_in_hbm(*[a for a, _, _ in row_ins], *bcast_ins), *deps)


def _accumulate(o_ref, part, first):
    @pl.when(first)
    def _():
        o_ref[...] = part

    @pl.when(jnp.logical_not(first))
    def _():
        o_ref[...] += part


_MM_DIMS = {"nn": ((1,), (0,)), "nt": ((1,), (1,)), "tn": ((0,), (0,))}


def _matmul(pairs, mode, out_dtype, name, tm, tn, tk, deps=(), staged=True):
    a0, b0 = pairs[0]
    if mode == "tn":
        kk, m = a0.shape
    else:
        m, kk = a0.shape
    n = b0.shape[0] if mode == "nt" else b0.shape[1]
    assert m % tm == 0 and n % tn == 0 and kk % tk == 0, (name, m, n, kk)
    nk = kk // tk
    n_pairs = len(pairs)
    dims = _MM_DIMS[mode]
    n_in = 2 * n_pairs + len(deps)

    def body(*refs):
        o_ref = refs[n_in]
        part = None
        for p in range(n_pairs):
            d = _dot(refs[2 * p][...].astype(BF16), refs[2 * p + 1][...].astype(BF16), dims)
            part = d if part is None else part + d
        if nk == 1:
            o_ref[...] = part.astype(o_ref.dtype)
            return
        acc = refs[n_in + 1]
        k = pl.program_id(2)

        @pl.when(k == 0)
        def _():
            acc[...] = part

        @pl.when(k > 0)
        def _():
            acc[...] += part

        @pl.when(k == nk - 1)
        def _():
            o_ref[...] = acc[...].astype(o_ref.dtype)

    if mode == "tn":
        a_spec = pl.BlockSpec((tk, tm), lambda i, j, k: (k, i))
    else:
        a_spec = pl.BlockSpec((tm, tk), lambda i, j, k: (i, k))
    if mode == "nt":
        b_spec = pl.BlockSpec((tn, tk), lambda i, j, k: (j, k))
    else:
        b_spec = pl.BlockSpec((tk, tn), lambda i, j, k: (k, j))
    blk = sum(_nbytes((tm, tk), a.dtype) + _nbytes((tk, tn), b.dtype) for a, b in pairs) + 2 * _nbytes((tm, tn), F32)
    flat = [a for pair in pairs for a in pair]
    return pl.pallas_call(
        body, name=name, grid=(m // tm, n // tn, nk),
        in_specs=[a_spec, b_spec] * n_pairs + [pl.BlockSpec(memory_space=pl.ANY)] * len(deps),
        out_specs=pl.BlockSpec((tm, tn), lambda i, j, k: (i, j)),
        out_shape=pltpu.HBM((m, n), out_dtype),
        scratch_shapes=[] if nk == 1 else [pltpu.VMEM((tm, tn), F32)],
        compiler_params=pltpu.CompilerParams(
            dimension_semantics=("parallel", "parallel", "arbitrary"), vmem_limit_bytes=_vmem_limit(blk)),
    )(*(flat if staged else _in_hbm(*flat)), *deps)


def _matmul_rowwise(pairs, fn, name, tm, row_ins, bcast_ins, row_outs, acc_outs=(), deps=()):
    m = pairs[0][0].shape[0]
    n_mm, n_in = 2 * len(pairs), len(row_ins) + len(bcast_ins)
    n_ro = len(row_outs)

    def body(*refs):
        prod = None
        for p in range(len(pairs)):
            part = _dot_nn(refs[2 * p][...].astype(BF16), refs[2 * p + 1][...].astype(BF16))
            prod = part if prod is None else prod + part
        res = fn(prod, *[r[...] for r in refs[n_mm:n_mm + n_in]])
        outs = refs[n_mm + n_in + len(deps):]
        for r, o in zip(res[:n_ro], outs[:n_ro]):
            o[...] = r.astype(o.dtype)
        first = pl.program_id(0) == 0
        for r, o in zip(res[n_ro:], outs[n_ro:]):
            _accumulate(o, r, first)

    in_specs = []
    for a, b in pairs:
        in_specs += [pl.BlockSpec((tm, a.shape[1]), lambda i: (i, 0)),
                     pl.BlockSpec(b.shape, lambda i: (0, 0), pipeline_mode=pl.Buffered(1))]
    in_specs += [pl.BlockSpec((tm, w), lambda i, cb=cb: (i, cb)) for _, w, cb in row_ins]
    in_specs += [pl.BlockSpec(a.shape, lambda i: (0, 0)) for a in bcast_ins]
    in_specs += [_ANY] * len(deps)
    out_specs = [pl.BlockSpec((tm, w), lambda i: (i, 0)) for w, _ in row_outs]
    out_specs += [pl.BlockSpec((1, w), lambda i: (0, 0)) for w in acc_outs]
    out_shape = [pltpu.HBM((m, w), dt) for w, dt in row_outs]
    out_shape += [pltpu.HBM((1, w), F32) for w in acc_outs]
    blk = sum(_nbytes((tm, a.shape[1]), a.dtype) + _nbytes(b.shape, b.dtype) // 2 for a, b in pairs)
    blk += sum(_nbytes((tm, w), a.dtype) for a, w, _ in row_ins) + sum(_nbytes((tm, w), dt) for w, dt in row_outs)
    return pl.pallas_call(
        body, name=name, grid=(m // tm,), in_specs=in_specs, out_specs=out_specs, out_shape=out_shape,
        compiler_params=pltpu.CompilerParams(dimension_semantics=("arbitrary",), vmem_limit_bytes=_vmem_limit(blk)),
    )(*[a for pair in pairs for a in pair], *[a for a, _, _ in row_ins], *bcast_ins, *deps)


def _rms_scale(x):
    return lax.rsqrt(jnp.mean(x * x, axis=-1, keepdims=True) + EPS)


def _rms_bwd(xin, dyn, g):
    r = _rms_scale(xin)
    u = dyn * g
    dx = r * u - xin * (r * r * r) * jnp.mean(u * xin, axis=-1, keepdims=True)
    dg = jnp.sum(dyn * xin * r, axis=0, keepdims=True)
    return dx, dg


def _mesh_pos():
    return lax.axis_index("x"), lax.axis_index("y"), lax.axis_index("c")


def _all_gather(xs, name, deps=()):
    n = len(xs)

    def body(*refs):
        x_refs, out_refs = refs[:n], refs[n + len(deps):2 * n + len(deps)]
        send_sems, recv_sems, local_sems = refs[2 * n + len(deps):]
        mx, my, mc = _mesh_pos()
        me, sib = (mx, my, mc), (mx, my, 1 - mc)
        chips = [(1 - mx, my), (mx, 1 - my), (1 - mx, 1 - my)]

        def slot(a, dev):
            px, py, pc = dev
            return out_refs[a].at[4 * px + 2 * py + pc]

        def copy(k, a, block, to, src=None):
            return pltpu.make_async_remote_copy(
                src_ref=slot(a, block) if src is None else src, dst_ref=slot(a, block),
                send_sem=send_sems.at[a * 7 + k], recv_sem=recv_sems.at[a * 7 + k],
                device_id=to, device_id_type=MESH)

        mine = [pltpu.make_async_copy(x_refs[a], slot(a, me), local_sems.at[a]) for a in range(n)]
        for cp in mine:
            cp.start()
        first = []
        for a in range(n):
            first.append(copy(0, a, me, sib, x_refs[a]))
            first += [copy(1 + j, a, me, (*chip, mc), x_refs[a]) for j, chip in enumerate(chips)]
        for cp in first:
            cp.start()
        passed = []
        for a in range(n):
            for j, chip in enumerate(chips):
                copy(1 + j, a, (*chip, mc), me).wait_recv()
                fwd = copy(4 + j, a, (*chip, mc), sib)
                fwd.start()
                passed.append(fwd)
        for a in range(n):
            copy(0, a, sib, me).wait_recv()
            for j, chip in enumerate(chips):
                copy(4 + j, a, (*chip, 1 - mc), me).wait_recv()
        for cp in first + passed:
            cp.wait_send()
        for cp in mine:
            cp.wait()

    hbm = pl.BlockSpec(memory_space=pl.ANY)
    return pl.pallas_call(
        body, name=name,
        out_shape=[pltpu.HBM((N_DEV,) + x.shape, x.dtype) for x in xs],
        in_specs=[hbm] * (n + len(deps)), out_specs=[hbm] * n,
        scratch_shapes=[pltpu.SemaphoreType.DMA((7 * n,)), pltpu.SemaphoreType.DMA((7 * n,)),
                        pltpu.SemaphoreType.DMA((n,))],
    )(*xs, *deps)


_HBM = pl.BlockSpec(memory_space=pltpu.HBM)
_SEM = pl.BlockSpec(memory_space=pltpu.SEMAPHORE)
_ANY = pl.BlockSpec(memory_space=pl.ANY)
_DATAFLOW = pltpu.SideEffectType.DATAFLOW_SIDE_EFFECTING


def _flip_peer(flip):
    mx, my, mc = _mesh_pos()
    return (1 - mx if flip & 2 else mx, 1 - my if flip & 1 else my, mc)


def _remote(src, dst, send_sems, recv_sems, k, peer):
    return pltpu.make_async_remote_copy(src_ref=src, dst_ref=dst, send_sem=send_sems.at[k], recv_sem=recv_sems.at[k],
                                        device_id=peer, device_id_type=MESH)


def _gather_chips_copies(srcs, lands, send_sems, recv_sems):
    mx, my, mc = _mesh_pos()
    me = 4 * mx + 2 * my + mc
    return [_remote(srcs[a], lands[a].at[me], send_sems, recv_sems, 3 * a + flip - 1, _flip_peer(flip))
            for a in range(len(srcs)) for flip in (1, 2, 3)]


def _gather_sibling_copies(srcs, lands, send_sems, recv_sems):
    mx, my, mc = _mesh_pos()
    return [_remote(lands[a].at[2 * k + mc], lands[a].at[2 * k + mc], send_sems, recv_sems, 4 * a + k, (mx, my, 1 - mc))
            for a in range(len(lands)) for k in range(4)]


def _scatter_sibling_copies(srcs, lands, send_sems, recv_sems):
    mx, my, mc = _mesh_pos()
    return [_remote(srcs[a].at[k, 1 - mc], lands[a].at[k], send_sems, recv_sems, 4 * a + k, (mx, my, 1 - mc))
            for a in range(len(srcs)) for k in range(4)]


def _scatter_chips_copies(srcs, lands, send_sems, recv_sems):
    mx, my, _ = _mesh_pos()
    k0 = 2 * mx + my
    return [_remote(srcs[a].at[jnp.bitwise_xor(k0, flip)], lands[a].at[flip - 1], send_sems, recv_sems,
                    3 * a + flip - 1, _flip_peer(flip))
            for a in range(len(srcs)) for flip in (1, 2, 3)]


class _Exchange:
    def __init__(self, copies, n_src, send_sems, recv_sems, thru, token):
        self.copies, self.n_src, self.send_sems, self.recv_sems, self.thru, self.token = (
            copies, n_src, send_sems, recv_sems, thru, token)


def _exchange_start(name, copies, srcs, lands, n_copies, after=()):
    bufs = list(srcs) + list(lands)
    nb, ns = len(bufs), len(srcs)

    def body(*refs):
        send_sems, recv_sems = refs[nb + len(after)], refs[nb + len(after) + 1]
        for cp in copies(refs[:ns], refs[ns:nb], send_sems, recv_sems):
            cp.start()
        refs[-1][...] = jnp.zeros_like(refs[-1])

    out = pl.pallas_call(
        body, name=name,
        out_shape=(pltpu.SemaphoreType.DMA((n_copies,)), pltpu.SemaphoreType.DMA((n_copies,)),
                   *[pltpu.HBM(b.shape, b.dtype) for b in bufs], pltpu.HBM((SUBLANES, LANES), F32)),
        in_specs=[_HBM] * nb + [_ANY] * len(after),
        out_specs=(_SEM, _SEM, *[_HBM] * nb, pl.BlockSpec(memory_space=pltpu.VMEM)),
        input_output_aliases={i: 2 + i for i in range(nb)},
        compiler_params=pltpu.CompilerParams(has_side_effects=_DATAFLOW),
    )(*[pltpu.with_memory_space_constraint(b, pltpu.HBM) for b in bufs], *after)
    return _Exchange(copies, ns, out[0], out[1], list(out[2:2 + nb]), out[-1])


def _exchange_wait(name, ex, after):
    nb, ns = len(ex.thru), ex.n_src

    def body(*refs):
        for cp in ex.copies(refs[:ns], refs[ns:nb], refs[nb], refs[nb + 1]):
            cp.wait_send()
            cp.wait_recv()

    out = pl.pallas_call(
        body, name=name, out_shape=tuple(pltpu.HBM(b.shape, b.dtype) for b in ex.thru),
        in_specs=[_HBM] * nb + [_SEM, _SEM] + [_ANY] * len(after), out_specs=tuple([_HBM] * nb),
        input_output_aliases={i: i for i in range(nb)},
        compiler_params=pltpu.CompilerParams(has_side_effects=_DATAFLOW),
    )(*ex.thru, ex.send_sems, ex.recv_sems, *after)
    return list(out[:ns]), list(out[ns:])


def _col_tile(r, c):
    return next(t for t in (1024, 512, 256, 128) if c % t == 0 and (r * t * 4 <= 2**20 or t == 128))


def _add_sibling(g4, recv, pos, name):
    _, _, r, c = g4.shape
    tc = _col_tile(r, c)

    def body(pos_ref, g_ref, r_ref, o16_ref, mine_ref):
        s = g_ref[0, 0] + r_ref[0]
        o16_ref[0] = s.astype(BF16)

        @pl.when(pl.program_id(1) == pos_ref[1])
        def _():
            mine_ref[...] = s

    slot = pl.BlockSpec((1, r, tc), lambda j, k, pos_ref: (k, 0, j))
    return pl.pallas_call(
        body, name=name,
        out_shape=[pltpu.HBM((4, r, c), BF16), pltpu.HBM((r, c), F32)],
        grid_spec=pltpu.PrefetchScalarGridSpec(
            num_scalar_prefetch=1, grid=(c // tc, 4),
            in_specs=[pl.BlockSpec((1, 1, r, tc), lambda j, k, pos_ref: (k, pos_ref[0], 0, j)), slot],
            out_specs=[slot, pl.BlockSpec((r, tc), lambda j, k, pos_ref: (0, j))]),
        compiler_params=pltpu.CompilerParams(dimension_semantics=("parallel", "arbitrary")),
    )(pos, *_in_hbm(g4, recv))


class _ReduceScatter:
    def __init__(self, tag, grads_t, pos):
        self.tag, self.pos, self.names = tag, pos, list(grads_t)
        g4s = [g.reshape(4, 2, g.size // (N_DEV * g.shape[-1]), g.shape[-1]) for g in grads_t.values()]
        lands = [lax.empty((4,) + g.shape[2:], F32) for g in g4s]
        self.ex = _exchange_start(f"rs_{tag}_sibling_start", _scatter_sibling_copies, g4s, lands, 4 * len(g4s))
        self.token = self.ex.token

    def start_chips(self, after):
        g4s, from_sibling = _exchange_wait(f"rs_{self.tag}_sibling_wait", self.ex, after)
        parts = [_add_sibling(g4, rv, self.pos, f"rs_add_sibling_{k}")
                 for k, g4, rv in zip(self.names, g4s, from_sibling)]
        self.mine = [mine for _, mine in parts]
        p16s = [p16 for p16, _ in parts]
        lands = [lax.empty((3,) + p.shape[1:], BF16) for p in p16s]
        self.ex = _exchange_start(f"rs_{self.tag}_chips_start", _scatter_chips_copies, p16s, lands, 3 * len(p16s))
        self.token = self.ex.token

    def finish(self, after):
        _, from_chips = _exchange_wait(f"rs_{self.tag}_chips_wait", self.ex, after)
        return dict(zip(self.names, zip(self.mine, from_chips)))


def _rope_tables():
    positions = np.arange(SEQ, dtype=np.float32)
    inv_freq = np.power(np.float32(ROPE_THETA), -np.arange(0, ROPE_DIM, 2, dtype=np.float32) / np.float32(ROPE_DIM))
    ang = (positions[:, None] * inv_freq[None, :]).astype(np.float32)
    cos, sin = np.cos(ang).astype(np.float32), np.sin(ang).astype(np.float32)
    ones = np.ones((SEQ, HEAD_DIM - ROPE_DIM), np.float32)
    zeros8 = np.zeros((SEQ, ROPE_HALF), np.float32)
    zeros = np.zeros((SEQ, HEAD_DIM - ROPE_DIM), np.float32)
    c_head = np.concatenate([cos, cos, ones], axis=1)
    s1_head = np.concatenate([-sin, zeros8, zeros], axis=1)
    s2_head = np.concatenate([zeros8, sin, zeros], axis=1)
    return tuple(jnp.asarray(np.concatenate([t, t], axis=1)) for t in (c_head, s1_head, s2_head))


def _rope_apply(x, c, s1, s2):
    w = x.shape[1]
    return x * c + pltpu.roll(x, w - ROPE_HALF, 1) * s1 + pltpu.roll(x, ROPE_HALF, 1) * s2


def _rope_apply_t(dy, c, s1, s2):
    w = dy.shape[1]
    return dy * c + pltpu.roll(dy * s1, ROPE_HALF, 1) + pltpu.roll(dy * s2, w - ROPE_HALF, 1)


def _dil_prev_limit(has_prev):
    return jnp.where(has_prev, 0, BLOCK)


def _dil_valid(limit):
    row = lax.broadcasted_iota(jnp.int32, (BLOCK, 2 * BLOCK), 0)
    col = lax.broadcasted_iota(jnp.int32, (BLOCK, 2 * BLOCK), 1)
    dist = col - row
    return jnp.logical_and(dist >= jnp.where(col < BLOCK, limit, -BLOCK), dist <= BLOCK)


def _upper_half():
    return lax.broadcasted_iota(jnp.int32, (1, LANES), 1) >= HEAD_DIM


def _stack_heads(x):
    upper = _upper_half()
    return jnp.concatenate([jnp.where(upper, 0, x), jnp.where(upper, x, 0)], axis=0)


def _unstack_heads(y):
    n = y.shape[0] // 2
    return jnp.where(_upper_half(), y[n:], y[:n])


def _head_columns(t):
    return jnp.concatenate([t[:, 0:1], t[:, HEAD_DIM:HEAD_DIM + 1]], axis=0)


def _dil_rows(n, d):
    per = N_BLOCKS // d
    r, lb = n // per, n % per

    def rows(b):
        start = b * (BLOCK * d) + r
        return pl.ds(pl.multiple_of(start, BLOCK), BLOCK) if d == 1 else pl.ds(start, BLOCK, stride=d)

    return rows(lb), rows(jnp.maximum(lb - 1, 0)), lb > 0


def _dil_rotate(q_ref, k_ref, c_ref, s1_ref, s2_ref, q_rot, k_rot):
    tabs = (c_ref[...], s1_ref[...], s2_ref[...])
    q_rot[...] = _rope_apply(q_ref[...], *tabs) * QK_SCALE
    k_rot[...] = _rope_apply(k_ref[...], *tabs)


def _dil_specs():
    def col(base):
        return pl.BlockSpec((SEQ, LANES), lambda p: (0, base // LANES + p))

    table = pl.BlockSpec((SEQ, LANES), lambda p: (0, 0))
    return [col(COL_QA), col(COL_KA), col(COL_VA)], [table] * 3


def _store_columns(blocks, dproj_ref, cols, sem):
    copies = [pltpu.make_async_copy(b, dproj_ref.at[:, pl.ds(pl.multiple_of(c * LANES, LANES), LANES)], sem.at[i])
              for i, (b, c) in enumerate(zip(blocks, cols))]
    for cp in copies:
        cp.start()
    for cp in copies:
        cp.wait()


def _dil_window(d, n, k_rot, v_ref):
    rows, prev, has_prev = _dil_rows(n, d)
    kw, vw = k_rot[rows, :].astype(BF16), v_ref[rows, :].astype(BF16)
    if d == N_BLOCKS:
        row = lax.broadcasted_iota(jnp.int32, (BLOCK, BLOCK), 0)
        valid = lax.broadcasted_iota(jnp.int32, (BLOCK, BLOCK), 1) <= row
    else:
        kw = jnp.concatenate([k_rot[prev, :].astype(BF16), kw], axis=0)
        vw = jnp.concatenate([v_ref[prev, :].astype(BF16), vw], axis=0)
        valid = _dil_valid(_dil_prev_limit(has_prev))
    return rows, prev, kw, vw, jnp.concatenate([valid, valid], axis=0)


def _dil_fwd(proj, tables):
    def body(q_ref, k_ref, v_ref, c_ref, s1_ref, s2_ref, o_ref, lse_ref, q_rot, k_rot):
        upper = _upper_half()
        _dil_rotate(q_ref, k_ref, c_ref, s1_ref, s2_ref, q_rot, k_rot)

        def blocks_of(d):
            def block(n, carry):
                rows, _, kw, vw, valid = _dil_window(d, n, k_rot, v_ref)
                s = jnp.where(valid, _dot_nt(_stack_heads(q_rot[rows, :].astype(BF16)), kw), NEG_INF)
                m = jnp.max(s, axis=-1, keepdims=True)
                p = jnp.exp(s - m)
                den = jnp.sum(p, axis=-1, keepdims=True)
                o_ref[rows, :] = _unstack_heads(_dot_nn((p * (1.0 / den)).astype(BF16), vw))
                lse = m + jnp.log(den)
                lse_ref[rows, :] = jnp.where(upper, lse[BLOCK:], lse[:BLOCK])
                return carry

            lax.fori_loop(0, N_BLOCKS, block, 0, unroll=4)

        for g, d in enumerate(DILATIONS):
            pl.when(pl.program_id(0) // 2 == g)(functools.partial(blocks_of, d))

    qkv, tabs = _dil_specs()
    out = pl.BlockSpec((SEQ, LANES), lambda p: (0, p))
    return pl.pallas_call(
        body, name="dil_attn_fwd", grid=(DIL_WIDTH // LANES,), in_specs=qkv + tabs, out_specs=[out, out],
        out_shape=[pltpu.HBM((SEQ, DIL_WIDTH), F32)] * 2,
        scratch_shapes=[pltpu.VMEM((SEQ, LANES), F32)] * 2,
        compiler_params=pltpu.CompilerParams(dimension_semantics=("parallel",)),
    )(*_in_hbm(proj, proj, proj, *tables))


def _dil_bwd(proj, tables, do, lse, c, dproj, deps=()):
    def body(q_ref, k_ref, v_ref, c_ref, s1_ref, s2_ref, do_ref, lse_ref, cc_ref, dproj_in, *rest):
        dproj_ref, dq_acc, dk_acc, dv_acc, dq_out, dk_out, dv_out, q_rot, k_rot, sem = rest[len(deps):]
        dk_acc[...] = jnp.zeros_like(dk_acc)
        dv_acc[...] = jnp.zeros_like(dv_acc)
        _dil_rotate(q_ref, k_ref, c_ref, s1_ref, s2_ref, q_rot, k_rot)

        def blocks_of(d):
            def block(n, carry):
                rows, prev, kw, vw, valid = _dil_window(d, n, k_rot, v_ref)
                q2 = _stack_heads(q_rot[rows, :].astype(BF16))
                do2 = _stack_heads(do_ref[rows, :].astype(BF16))
                lse_col, c_col = _head_columns(lse_ref[rows, :]), _head_columns(cc_ref[rows, :])
                p = jnp.where(valid, jnp.exp(_dot_nt(q2, kw) - lse_col), 0.0)
                ds = (p * (_dot_nt(do2, vw) + c_col)).astype(BF16)
                dk, dv = _dot_tn(ds, q2), _dot_tn(p.astype(BF16), do2)
                dq_acc[rows, :] = _unstack_heads(_dot_nn(ds, kw)) * QK_SCALE
                if d == N_BLOCKS:
                    dk_acc[rows, :] += dk
                    dv_acc[rows, :] += dv
                else:
                    dk_acc[prev, :] += dk[:BLOCK]
                    dv_acc[prev, :] += dv[:BLOCK]
                    dk_acc[rows, :] += dk[BLOCK:]
                    dv_acc[rows, :] += dv[BLOCK:]
                return carry

            lax.fori_loop(0, N_BLOCKS, block, 0, unroll=4)

        pair = pl.program_id(0)
        for g, d in enumerate(DILATIONS):
            pl.when(pair // 2 == g)(functools.partial(blocks_of, d))
        tabs = (c_ref[...], s1_ref[...], s2_ref[...])
        dq_out[...] = _rope_apply_t(dq_acc[...], *tabs).astype(BF16)
        dk_out[...] = _rope_apply_t(dk_acc[...], *tabs).astype(BF16)
        dv_out[...] = dv_acc[...].astype(BF16)
        _store_columns((dq_out, dk_out, dv_out), dproj_ref,
                       [base // LANES + pair for base in (COL_QA, COL_KA, COL_VA)], sem)

    qkv, tabs = _dil_specs()
    tok = pl.BlockSpec((SEQ, LANES), lambda p: (0, p))
    return pl.pallas_call(
        body, name="dil_attn_bwd", grid=(DIL_WIDTH // LANES,),
        in_specs=qkv + tabs + [tok, tok, tok, _ANY] + [_ANY] * len(deps), out_specs=_ANY,
        out_shape=pltpu.HBM(dproj.shape, dproj.dtype),
        scratch_shapes=[pltpu.VMEM((SEQ, LANES), F32)] * 3 + [pltpu.VMEM((SEQ, LANES), BF16)] * 3
        + [pltpu.VMEM((SEQ, LANES), F32)] * 2 + [pltpu.SemaphoreType.DMA((3,))],
        input_output_aliases={9: 0},
        compiler_params=pltpu.CompilerParams(dimension_semantics=("arbitrary",)),
    )(*_in_hbm(proj, proj, proj, *tables, do, lse, c, dproj), *deps)


def _group_weights(l0, l1, l2):
    m = jnp.maximum(jnp.maximum(l0, l1), l2)
    e0, e1, e2 = jnp.exp(l0 - m), jnp.exp(l1 - m), jnp.exp(l2 - m)
    tot = e0 + e1 + e2
    return e0 / tot, e1 / tot, e2 / tot


def _dil_combine(o, lse, deps=()):
    def fn(o0, o1, o2, l0, l1, l2):
        w0, w1, w2 = _group_weights(l0, l1, l2)
        return w0 * o0 + w1 * o1 + w2 * o2

    w = DIL_OUT_WIDTH
    return _rowwise(fn, "dil_combine", SEQ, 512, [(o, w, g) for g in range(3)] + [(lse, w, g) for g in range(3)], [],
                    [(w, F32)], deps=deps)[0]


def _dil_combine_bwd(d_out, o, lse, deps=()):
    w = DIL_OUT_WIDTH

    def fn(d, o0, o1, o2, l0, l1, l2):
        row = lax.broadcasted_iota(jnp.int32, (w, w), 0) // HEAD_DIM
        col = lax.broadcasted_iota(jnp.int32, (w, w), 1) // HEAD_DIM
        same_head = jnp.where(row == col, 1.0, 0.0).astype(BF16)
        ws = _group_weights(l0, l1, l2)
        dws = [_dot3_nn(d * og, same_head) for og in (o0, o1, o2)]
        mean = ws[0] * dws[0] + ws[1] * dws[1] + ws[2] * dws[2]
        return jnp.concatenate([wg * d for wg in ws], axis=1), jnp.concatenate([-wg * mean for wg in ws], axis=1)

    return _rowwise(fn, "dil_combine_bwd", SEQ, 256,
                    [(d_out, w, 0)] + [(o, w, g) for g in range(3)] + [(lse, w, g) for g in range(3)], [],
                    [(DIL_WIDTH, F32)] * 2, deps=deps)


def _log1p(e):
    u = 1.0 + e
    return jnp.where(u == 1.0, e, jnp.log(u) * (e / (u - 1.0)))


def _fox_gate(proj, b_pad, deps=()):
    def body(f_ref, b_ref, *rest):
        o_ref = rest[-1]
        z = f_ref[...] + b_ref[...]
        logf = (jnp.minimum(z, 0.0) - _log1p(jnp.exp(-jnp.abs(z)))).T[:F_ROWS]
        row = lax.broadcasted_iota(jnp.int32, (BLOCK, BLOCK), 0)
        col = lax.broadcasted_iota(jnp.int32, (BLOCK, BLOCK), 1)
        before = jnp.where(row <= col, 1.0, 0.0).astype(BF16)
        carry = jnp.zeros((F_ROWS, 1), F32)
        for blk in range(N_BLOCKS):
            run = _dot3_nn(logf[:, blk * BLOCK:(blk + 1) * BLOCK], before) + carry
            o_ref[:, blk * BLOCK:(blk + 1) * BLOCK] = run
            carry = run[:, BLOCK - 1:BLOCK]

    return pl.pallas_call(
        body, name="fox_gate", grid=(1,),
        in_specs=[pl.BlockSpec((SEQ, LANES), lambda i: (0, COL_F // LANES)), pl.BlockSpec((1, LANES), lambda i: (0, 0))]
        + [_ANY] * len(deps),
        out_specs=pl.BlockSpec((F_ROWS, SEQ), lambda i: (0, 0)),
        out_shape=pltpu.HBM((F_ROWS, SEQ), F32),
    )(*_in_hbm(proj, b_pad), *deps)


def _fox_gate_bwd(d_cum, proj, b_pad, dproj):
    def body(d_ref, f_ref, b_ref, dproj_ref, dz_ref, db_ref):
        row = lax.broadcasted_iota(jnp.int32, (BLOCK, BLOCK), 0)
        col = lax.broadcasted_iota(jnp.int32, (BLOCK, BLOCK), 1)
        after = jnp.where(row >= col, 1.0, 0.0).astype(BF16)
        carry = jnp.zeros((F_ROWS, 1), F32)
        parts = [None] * N_BLOCKS
        for blk in reversed(range(N_BLOCKS)):
            run = _dot3_nn(d_ref[:, blk * BLOCK:(blk + 1) * BLOCK], after) + carry
            parts[blk] = run
            carry = run[:, 0:1]
        dlogf = jnp.concatenate(parts, axis=1)
        dlogf = jnp.concatenate([dlogf, jnp.zeros((LANES - F_ROWS, SEQ), F32)], axis=0).T
        dz = dlogf * _sigmoid(-(f_ref[...] + b_ref[...]))
        dz_ref[...] = dz.astype(BF16)
        db_ref[...] = jnp.sum(dz, axis=0, keepdims=True)

    f_cols = pl.BlockSpec((SEQ, LANES), lambda i: (0, COL_F // LANES))
    return pl.pallas_call(
        body, name="fox_gate_bwd", grid=(1,),
        in_specs=[pl.BlockSpec((F_ROWS, SEQ), lambda i: (0, 0)), f_cols, pl.BlockSpec((1, LANES), lambda i: (0, 0)), _ANY],
        out_specs=[f_cols, pl.BlockSpec((1, LANES), lambda i: (0, 0))],
        out_shape=[pltpu.HBM(dproj.shape, dproj.dtype), pltpu.HBM((1, LANES), F32)],
        input_output_aliases={3: 0},
    )(*_in_hbm(d_cum, proj, b_pad, dproj))


FOX_TILE = 256
FOX_TILES = SEQ // FOX_TILE


def _row_to_col(row):
    n = row.shape[1]
    eye = lax.broadcasted_iota(jnp.int32, (n, n), 0) == lax.broadcasted_iota(jnp.int32, (n, n), 1)
    return jnp.sum(jnp.where(eye, row, 0.0), axis=1, keepdims=True)


def _fox_bias(f_row, i):
    t = FOX_TILE
    ext = (i + 1) * t
    bias = _row_to_col(f_row[:, i * t:(i + 1) * t]) - f_row[:, :ext]
    row = lax.broadcasted_iota(jnp.int32, (t, ext), 0) + i * t
    col = lax.broadcasted_iota(jnp.int32, (t, ext), 1)
    return bias, col <= row


def _fox_specs():
    qkv = [pl.BlockSpec((SEQ, LANES), lambda p, base=base: (0, base // LANES + p)) for base in (COL_QB, COL_KB, COL_VB)]
    return qkv, pl.BlockSpec((F_ROWS, SEQ), lambda p: (0, 0))


def _fox_fwd(proj, f_rows):
    t = FOX_TILE

    def body(q_ref, k_ref, v_ref, f_ref, o_ref, lse_ref):
        pair = pl.program_id(0)
        upper = _upper_half()
        k16, v16 = k_ref[...].astype(BF16), v_ref[...].astype(BF16)
        f_row = [f_ref[pl.ds(2 * pair + e, 1), :] for e in range(2)]
        for i in range(FOX_TILES):
            ext = (i + 1) * t
            q_tile = (q_ref[i * t:(i + 1) * t, :] * QK_SCALE).astype(BF16)
            s2 = _dot_nt(_stack_heads(q_tile), k16[:ext])
            pns, lses = [], []
            for e in range(2):
                bias, causal = _fox_bias(f_row[e], i)
                s = jnp.where(causal, s2[e * t:(e + 1) * t] + bias, NEG_INF)
                m = jnp.max(s, axis=-1, keepdims=True)
                p = jnp.exp(s - m)
                den = jnp.sum(p, axis=-1, keepdims=True)
                pns.append((p * (1.0 / den)).astype(BF16))
                lses.append(m + jnp.log(den))
            o_ref[i * t:(i + 1) * t, :] = _unstack_heads(_dot_nn(jnp.concatenate(pns, axis=0), v16[:ext]))
            lse_ref[i * t:(i + 1) * t, :] = jnp.where(upper, lses[1], lses[0])

    qkv, f_spec = _fox_specs()
    tok = pl.BlockSpec((SEQ, LANES), lambda p: (0, p))
    return pl.pallas_call(
        body, name="fox_attn_fwd", grid=(FOX_WIDTH // LANES,),
        in_specs=qkv + [f_spec], out_specs=[tok, tok],
        out_shape=[pltpu.HBM((SEQ, FOX_WIDTH), F32)] * 2,
        compiler_params=pltpu.CompilerParams(
            dimension_semantics=("parallel",), vmem_limit_bytes=_vmem_limit(8 * t * SEQ * 4)),
    )(*_in_hbm(proj, proj, proj, f_rows))


def _fox_bwd(proj, do, lse, f_rows, dproj):
    t = FOX_TILE

    def body(q_ref, k_ref, v_ref, f_ref, do_ref, lse_ref, dproj_in, dproj_ref, df_ref, dk_acc, dv_acc,
             dq_out, dk_out, dv_out, sem):
        pair = pl.program_id(0)
        upper = _upper_half()
        k16, v16 = k_ref[...].astype(BF16), v_ref[...].astype(BF16)
        f_row = [f_ref[pl.ds(2 * pair + e, 1), :] for e in range(2)]
        dk_acc[...] = jnp.zeros_like(dk_acc)
        dv_acc[...] = jnp.zeros_like(dv_acc)
        df_ref[...] = jnp.zeros_like(df_ref)
        for i in range(FOX_TILES):
            ext = (i + 1) * t
            q_tile = (q_ref[i * t:(i + 1) * t, :] * QK_SCALE).astype(BF16)
            do_tile = do_ref[i * t:(i + 1) * t, :]
            lse_t = lse_ref[i * t:(i + 1) * t, :]
            q2, do2 = _stack_heads(q_tile), _stack_heads(do_tile)
            s2, dp2 = _dot_nt(q2, k16[:ext]), _dot_nt(do2, v16[:ext])
            ps, dss = [], []
            for e in range(2):
                bias, causal = _fox_bias(f_row[e], i)
                s = s2[e * t:(e + 1) * t] + bias
                p = jnp.where(causal, jnp.exp(s - lse_t[:, e * HEAD_DIM:e * HEAD_DIM + 1]), 0.0)
                dp = dp2[e * t:(e + 1) * t]
                ds = p * (dp - jnp.sum(p * dp, axis=-1, keepdims=True))
                df_ref[0, e:e + 1, :ext] -= jnp.sum(ds, axis=0, keepdims=True)
                ps.append(p.astype(BF16))
                dss.append(ds.astype(BF16))
            ds2, p2 = jnp.concatenate(dss, axis=0), jnp.concatenate(ps, axis=0)
            dq_out[i * t:(i + 1) * t, :] = (_unstack_heads(_dot_nn(ds2, k16[:ext])) * QK_SCALE).astype(BF16)
            dk_acc[:ext, :] += _dot_tn(ds2, q2)
            dv_acc[:ext, :] += _dot_tn(p2, do2)
        dk_out[...] = dk_acc[...].astype(BF16)
        dv_out[...] = dv_acc[...].astype(BF16)
        _store_columns((dq_out, dk_out, dv_out), dproj_ref, [base // LANES + pair for base in (COL_QB, COL_KB, COL_VB)],
                       sem)

    qkv, f_spec = _fox_specs()
    tok = pl.BlockSpec((SEQ, LANES), lambda p: (0, p))
    return pl.pallas_call(
        body, name="fox_attn_bwd", grid=(FOX_WIDTH // LANES,),
        in_specs=qkv + [f_spec, tok, tok, _ANY],
        out_specs=[_ANY, pl.BlockSpec((1, SUBLANES, SEQ), lambda p: (p, 0, 0))],
        out_shape=[pltpu.HBM(dproj.shape, dproj.dtype),
                   pltpu.HBM((FOX_WIDTH // LANES, SUBLANES, SEQ), F32)],
        scratch_shapes=[pltpu.VMEM((SEQ, LANES), F32)] * 2 + [pltpu.VMEM((SEQ, LANES), BF16)] * 3
        + [pltpu.SemaphoreType.DMA((3,))],
        input_output_aliases={6: 0},
        compiler_params=pltpu.CompilerParams(
            dimension_semantics=("arbitrary",), vmem_limit_bytes=_vmem_limit(10 * t * SEQ * 4)),
    )(*_in_hbm(proj, proj, proj, f_rows, do, lse, dproj))


MIX_TILE = 256


def _mix_out(out_a, out_b, proj, x, wt_pa, wt_pb, w_out, g_post, g_ffn_pre):
    tm = MIX_TILE

    def body(a_ref, b_ref, ga_ref, gb_ref, x_ref, wpa_ref, wpb_ref, wo_ref, g2_ref, g3_ref,
             merged_ref, mix_ref, x1_ref, h2_ref):
        ya = _dot_nn(a_ref[...].astype(BF16), wpa_ref[...])
        yb = _dot_nn(b_ref[...].astype(BF16), wpb_ref[...])
        merged = (_sigmoid(ga_ref[...]) * ya + _sigmoid(gb_ref[...]) * yb).astype(BF16)
        merged_ref[...] = merged
        mix = _dot_nn(merged, wo_ref[...])
        mix_ref[...] = mix
        x1 = x_ref[...] + mix * _rms_scale(mix) * g2_ref[...]
        x1_ref[...] = x1
        h2_ref[...] = (x1 * _rms_scale(x1) * g3_ref[...]).astype(BF16)

    def rows(w, cb=0):
        return pl.BlockSpec((tm, w), lambda i, cb=cb: (i, cb))

    def whole(a):
        return pl.BlockSpec(a.shape, lambda i: (0, 0))

    d = D_MODEL
    blk = _nbytes((tm, d), F32) * 6 + sum(_nbytes(a.shape, BF16) for a in (wt_pa, wt_pb, w_out))
    return pl.pallas_call(
        body, name="mix_out", grid=(SEQ // tm,),
        in_specs=[rows(DIL_OUT_WIDTH), rows(FOX_WIDTH), rows(d, COL_GA // d), rows(d, COL_GB // d), rows(d),
                  whole(wt_pa), whole(wt_pb), whole(w_out), whole(g_post), whole(g_ffn_pre)],
        out_specs=[rows(d)] * 4,
        out_shape=[pltpu.HBM((SEQ, d), dt) for dt in (BF16, F32, F32, BF16)],
        compiler_params=pltpu.CompilerParams(dimension_semantics=("parallel",), vmem_limit_bytes=_vmem_limit(blk)),
    )(*_in_hbm(out_a, out_b, proj, proj, x, wt_pa, wt_pb, w_out, g_post, g_ffn_pre))


def _mix_out_bwd(dmix, out_a, out_b, proj, wt_pa, wt_pb, w_out, deps=()):
    tm = MIX_TILE

    def body(dm_ref, a_ref, b_ref, ga_ref, gb_ref, wpa_ref, wpb_ref, wo_ref, *rest):
        dproj_ref, dya_ref, dyb_ref, da_ref, db_ref = rest[len(deps):]
        dmerged = _dot_nt(dm_ref[...], wo_ref[...])
        ya = _dot_nn(a_ref[...].astype(BF16), wpa_ref[...])
        yb = _dot_nn(b_ref[...].astype(BF16), wpb_ref[...])
        sa, sb = _sigmoid(ga_ref[...]), _sigmoid(gb_ref[...])
        dproj_ref[:, COL_GA:COL_GA + D_MODEL] = (dmerged * ya * (sa * (1.0 - sa))).astype(BF16)
        dproj_ref[:, COL_GB:COL_GB + D_MODEL] = (dmerged * yb * (sb * (1.0 - sb))).astype(BF16)
        dproj_ref[:, COL_GB + D_MODEL:] = jnp.zeros((tm, COL_QA - COL_GB - D_MODEL), BF16)
        dya = (dmerged * sa).astype(BF16)
        dyb = (dmerged * sb).astype(BF16)
        dya_ref[...] = dya
        dyb_ref[...] = dyb
        da_ref[...] = _dot_nt(dya, wpa_ref[...])
        db_ref[...] = _dot_nt(dyb, wpb_ref[...]).astype(BF16)

    def rows(w, cb=0):
        return pl.BlockSpec((tm, w), lambda i, cb=cb: (i, cb))

    def whole(a):
        return pl.BlockSpec(a.shape, lambda i: (0, 0))

    d = D_MODEL
    blk = _nbytes((tm, d), F32) * 8 + sum(_nbytes(a.shape, BF16) for a in (wt_pa, wt_pb, w_out))
    return pl.pallas_call(
        body, name="mix_out_bwd", grid=(SEQ // tm,),
        in_specs=[rows(d), rows(DIL_OUT_WIDTH), rows(FOX_WIDTH), rows(d, COL_GA // d), rows(d, COL_GB // d),
                  whole(wt_pa), whole(wt_pb), whole(w_out)] + [_ANY] * len(deps),
        out_specs=[rows(COL_QA)] + [rows(d)] * 2 + [rows(DIL_OUT_WIDTH), rows(FOX_WIDTH)],
        out_shape=[pltpu.HBM((SEQ, PROJ_COLS), BF16)] + [pltpu.HBM((SEQ, d), BF16)] * 2
        + [pltpu.HBM((SEQ, DIL_OUT_WIDTH), F32), pltpu.HBM((SEQ, FOX_WIDTH), BF16)],
        compiler_params=pltpu.CompilerParams(dimension_semantics=("parallel",), vmem_limit_bytes=_vmem_limit(blk)),
    )(*_in_hbm(dmix, out_a, out_b, proj, proj, wt_pa, wt_pb, w_out), *deps)


FFN_TM, FFN_TN = 2048, 256


def _ffn_up(h2, wt_gate, wt_up):
    tm, tn = FFN_TM, FFN_TN

    def body(h_ref, wg_ref, wu_ref, gate_ref, up_ref, act_ref):
        for rows in (slice(0, tm // 2), slice(tm // 2, tm)):
            gate = _dot_nt(h_ref[rows, :], wg_ref[...])
            up = _dot_nt(h_ref[rows, :], wu_ref[...])
            gate_ref[rows, :] = gate
            up_ref[rows, :] = up
            act_ref[rows, :] = (gate * _sigmoid(gate) * up).astype(BF16)

    tile = pl.BlockSpec((tm, tn), lambda i, j: (i, j))
    w_spec = pl.BlockSpec((tn, D_MODEL), lambda i, j: (j, 0))
    return pl.pallas_call(
        body, name="ffn_up", grid=(SEQ // tm, D_FF // tn),
        in_specs=[pl.BlockSpec((tm, D_MODEL), lambda i, j: (i, 0)), w_spec, w_spec],
        out_specs=[tile, tile, tile],
        out_shape=[pltpu.HBM((SEQ, D_FF), dt) for dt in (F32, F32, BF16)],
        compiler_params=pltpu.CompilerParams(
            dimension_semantics=("parallel", "parallel"), vmem_limit_bytes=_vmem_limit(8 * 2**20)),
    )(h2, wt_gate, wt_up)


def _ffn_act_bwd(dff, w_down, gate, up):
    tm, tn = FFN_TM, FFN_TN

    def body(d_ref, wd_ref, gate_ref, up_ref, dgate_ref, dup_ref):
        for rows in (slice(0, tm // 2), slice(tm // 2, tm)):
            dact = _dot_nt(d_ref[rows, :], wd_ref[...])
            gate = gate_ref[rows, :]
            sg = _sigmoid(gate)
            dgate_ref[rows, :] = (dact * up_ref[rows, :] * (sg * (1.0 + gate * (1.0 - sg)))).astype(BF16)
            dup_ref[rows, :] = (dact * (gate * sg)).astype(BF16)

    tile = pl.BlockSpec((tm, tn), lambda i, j: (i, j))
    return pl.pallas_call(
        body, name="ffn_act_bwd", grid=(SEQ // tm, D_FF // tn),
        in_specs=[pl.BlockSpec((tm, D_MODEL), lambda i, j: (i, 0)), pl.BlockSpec((tn, D_MODEL), lambda i, j: (j, 0)),
                  tile, tile],
        out_specs=[tile, tile],
        out_shape=[pltpu.HBM((SEQ, D_FF), BF16)] * 2,
        compiler_params=pltpu.CompilerParams(
            dimension_semantics=("parallel", "parallel"), vmem_limit_bytes=_vmem_limit(8 * 2**20)),
    )(dff, w_down, gate, up)


EPILOGUE_TM = 512


def _loss_head(act, w_down, x1, target, g_post):
    def fn(ff, x1, tgt, g):
        r = _rms_scale(ff)
        nrm = ff * r
        err = (x1 + nrm * g) - tgt
        loss = 0.5 * jnp.sum(jnp.mean(err * err, axis=-1, keepdims=True), axis=0, keepdims=True)
        dy = err * (1.0 / D_MODEL)
        u = dy * g
        dff = r * u - ff * (r * r * r) * jnp.mean(u * ff, axis=-1, keepdims=True)
        return dy, dff, jnp.broadcast_to(loss, (1, LANES)), jnp.sum(dy * nrm, axis=0, keepdims=True)

    d = D_MODEL
    return _matmul_rowwise([(act, w_down)], fn, "ffn_down_loss", EPILOGUE_TM, [(x1, d, 0), (target, d, 0)], [g_post],
                           [(d, F32), (d, BF16)], [LANES, d])


def _post_ffn_bwd(dgate, wt_gate, dup, wt_up, x1, dy, mix, g_ffn_pre, g_mix_post, deps=()):
    def fn(dh2, x1, dy, mix, g3, g2):
        dx, dg3 = _rms_bwd(x1, dh2, g3)
        dx1 = dy + dx
        dmix, dg2 = _rms_bwd(mix, dx1, g2)
        return dx1, dmix, dg3, dg2

    d = D_MODEL
    return _matmul_rowwise([(dgate, wt_gate), (dup, wt_up)], fn, "ffn_up_bwd", EPILOGUE_TM,
                           [(x1, d, 0), (dy, d, 0), (mix, d, 0)], [g_ffn_pre, g_mix_post],
                           [(d, F32), (d, BF16)], [d, d], deps=deps)


def _input_bwd(dproj, wt_r, x, dx1, g_pre, deps=()):
    def fn(dh, x, dx1, g):
        dx, dg = _rms_bwd(x, dh, g)
        return dx1 + dx, dg

    d = D_MODEL
    return _matmul_rowwise([(dproj, wt_r)], fn, "in_proj_bwd", EPILOGUE_TM, [(x, d, 0), (dx1, d, 0)], [g_pre],
                           [(d, F32)], [d], deps=deps)


def _adam_math(w, g, m, v):
    m = ADAM_B1 * m + (1.0 - ADAM_B1) * g
    v = ADAM_B2 * v + (1.0 - ADAM_B2) * (g * g)
    m_hat = m / (1.0 - ADAM_B1 ** ADAM_STEP)
    v_hat = v / (1.0 - ADAM_B2 ** ADAM_STEP)
    delta = -ADAM_LR * (m_hat / (jnp.sqrt(v_hat) + ADAM_EPS) + ADAM_WD * w)
    return delta, m, v


def _adam(w, mine, recv, m, v, name):
    r, c = w.shape
    tc = _col_tile(r, c)

    def body(w_ref, p_ref, r_ref, m_ref, v_ref, g_ref, d_ref, nm_ref, nv_ref):
        g = ((p_ref[...] + r_ref[0].astype(F32)) + r_ref[1].astype(F32)) + r_ref[2].astype(F32)
        g_ref[...] = g
        d_ref[...], nm_ref[...], nv_ref[...] = _adam_math(w_ref[...], g, m_ref[...], v_ref[...])

    spec = pl.BlockSpec((r, tc), lambda j: (0, j))
    return pl.pallas_call(
        body, name=name, grid=(c // tc,),
        in_specs=[spec, spec, pl.BlockSpec((3, r, tc), lambda j: (0, 0, j)), spec, spec], out_specs=[spec] * 4,
        out_shape=[pltpu.HBM((r, c), F32)] * 4,
        compiler_params=pltpu.CompilerParams(dimension_semantics=("parallel",)),
    )(*_in_hbm(w, mine, recv, m, v))


def _adam_small(gathered, ws, ms, vs, loss_parts):
    n = len(ws)

    def body(*refs):
        outs = refs[4 * n + 1:]
        loss = refs[4 * n][0]
        for dev in range(1, N_DEV):
            loss = loss + refs[4 * n][dev]
        outs[4 * n][...] = loss
        for i in range(n):
            ga_ref, w_ref, m_ref, v_ref = (refs[j * n + i] for j in range(4))
            g = ga_ref[0]
            for dev in range(1, N_DEV):
                g = g + ga_ref[dev]
            g = g[:, :w_ref.shape[1]]
            outs[4 * i][...] = g
            outs[4 * i + 1][...], outs[4 * i + 2][...], outs[4 * i + 3][...] = _adam_math(
                w_ref[...], g, m_ref[...], v_ref[...])

    out_shape = [pltpu.HBM(w.shape, F32) for w in ws for _ in range(4)]
    out_shape.append(pltpu.HBM((1, LANES), F32))
    out = pl.pallas_call(body, name="adam_small", out_shape=out_shape)(*gathered, *ws, *ms, *vs, loss_parts)
    return [out[4 * i:4 * i + 4] for i in range(n)], out[4 * n]


_PROJ_SEGMENTS = ((3848, 5896), (None, COL_QA - 2 * D_MODEL), (0, 3840), (3840, 3848), (None, PROJ_COLS - COL_F - 8))


def _proj_weight_t(gathered):
    w = gathered.reshape(IN_COLS, D_MODEL)
    return jnp.concatenate([jnp.zeros((hi, D_MODEL), w.dtype) if lo is None else w[lo:hi] for lo, hi in _PROJ_SEGMENTS],
                           axis=0)


def _proj_weight_grad_slots(dwt_r):
    starts, at = [], 0
    for lo, hi in _PROJ_SEGMENTS:
        if lo is not None:
            starts.append((lo, hi, at))
        at += hi if lo is None else hi - lo
    slots = []
    for dev in range(N_DEV):
        pieces, lo, end = [], dev * IN_SHARD, (dev + 1) * IN_SHARD
        for seg_lo, seg_hi, seg_at in sorted(starts):
            a, b = max(lo, seg_lo), min(end, seg_hi)
            if a < b:
                pieces.append(dwt_r[seg_at + a - seg_lo:seg_at + b - seg_lo])
        slots.append(pieces[0] if len(pieces) == 1 else jnp.concatenate(pieces, axis=0))
    return jnp.stack(slots)


def kernel(x, w_in, w_proj_a, w_proj_b, w_out, b_forget, w_ffn_gate, w_ffn_up, w_ffn_down, norm_mix_pre, norm_mix_post, norm_ffn_pre, norm_ffn_post, loss_target, m_w_in, m_w_proj_a, m_w_proj_b, m_w_out, m_b_forget, m_w_ffn_gate, m_w_ffn_up, m_w_ffn_down, m_norm_mix_pre, m_norm_mix_post, m_norm_ffn_pre, m_norm_ffn_post, v_w_in, v_w_proj_a, v_w_proj_b, v_w_out, v_b_forget, v_w_ffn_gate, v_w_ffn_up, v_w_ffn_down, v_norm_mix_pre, v_norm_mix_post, v_norm_ffn_pre, v_norm_ffn_post):
    d = D_MODEL
    names = ("w_in", "w_proj_a", "w_proj_b", "w_out", "w_ffn_gate", "w_ffn_up", "w_ffn_down")
    col_sharded = ("w_in", "w_ffn_gate", "w_ffn_up")

    def row_shards(arrs):
        return {k: (a[0].T if k in col_sharded else a[0]) for k, a in zip(names, arrs)}

    shards = row_shards((w_in, w_proj_a, w_proj_b, w_out, w_ffn_gate, w_ffn_up, w_ffn_down))
    moments_m = row_shards((m_w_in, m_w_proj_a, m_w_proj_b, m_w_out, m_w_ffn_gate, m_w_ffn_up, m_w_ffn_down))
    moments_v = row_shards((v_w_in, v_w_proj_a, v_w_proj_b, v_w_out, v_w_ffn_gate, v_w_ffn_up, v_w_ffn_down))
    pos = jnp.stack([lax.axis_index("c"), 2 * lax.axis_index("x") + lax.axis_index("y")]).astype(jnp.int32)
    x2, target = x[0], loss_target[0]

    me = 4 * lax.axis_index("x") + 2 * lax.axis_index("y") + lax.axis_index("c")
    mid_names, ffn_names = names[1:4], names[4:]
    first_names, later_names = names[:1], names[1:]
    shards16 = {k: shards[k].astype(BF16) for k in names}

    def landing(k):
        return lax.dynamic_update_slice(lax.empty((N_DEV,) + shards[k].shape, BF16), shards16[k][None], (me, 0, 0))

    ag_first = _exchange_start("ag_first_chips_start", _gather_chips_copies, [shards16[k] for k in first_names],
                               [landing(k) for k in first_names], 3 * len(first_names))
    h = _rowwise(lambda xb, g: xb * _rms_scale(xb) * g, "norm_mix_pre", SEQ, 256, [(x2, d, 0)], [norm_mix_pre],
                 [(d, BF16)], deps=[ag_first.token])[0]
    later_lands = [landing(k) for k in later_names]
    _, lands = _exchange_wait("ag_first_chips_wait", ag_first,
                              [h, shards["w_in"], moments_m["w_in"], moments_v["w_in"], *later_lands,
                               *[shards16[k] for k in later_names]])
    ag_first = _exchange_start("ag_first_sibling_start", _gather_sibling_copies, [], lands, 4 * len(first_names))
    ag_later = _exchange_start("ag_later_chips_start", _gather_chips_copies, [shards16[k] for k in later_names],
                               later_lands, 3 * len(later_names), after=[ag_first.token])
    gathered = dict(zip(first_names, _exchange_wait("ag_first_sibling_wait", ag_first, [ag_later.token])[1]))
    wt_r = _proj_weight_t(gathered["w_in"])

    proj = _matmul([(h, wt_r)], "nt", F32, "in_proj", 1024, 896, 1024)
    tables = _rope_tables()
    o_dil, lse_dil = _dil_fwd(proj, tables)
    out_a = _dil_combine(o_dil, lse_dil)
    _, lands = _exchange_wait("ag_later_chips_wait", ag_later, [out_a])
    ag_later = _exchange_start("ag_later_sibling_start", _gather_sibling_copies, [], lands, 4 * len(later_names))

    b_pad = jnp.pad(b_forget, ((0, 0), (0, LANES - N_FOX_HEADS)))
    f_rows = _fox_gate(proj, b_pad, deps=[ag_later.token])
    out_b, lse_fox = _fox_fwd(proj, f_rows)

    gathered = dict(zip(later_names, _exchange_wait("ag_later_sibling_wait", ag_later, [out_b])[1]))
    wt_pa = gathered["w_proj_a"].transpose(1, 0, 2).reshape(DIL_OUT_WIDTH, d)
    wt_pb = gathered["w_proj_b"].transpose(1, 0, 2).reshape(FOX_WIDTH, d)
    w_o = gathered["w_out"].reshape(d, d)
    wt_g = gathered["w_ffn_gate"].reshape(D_FF, d)
    wt_u = gathered["w_ffn_up"].reshape(D_FF, d)
    w_d = gathered["w_ffn_down"].reshape(D_FF, d)
    merged, mix, x1, h2 = _mix_out(out_a, out_b, proj, x2, wt_pa, wt_pb, w_o, norm_mix_post, norm_ffn_pre)

    gate, up, act = _ffn_up(h2, wt_g, wt_u)
    dy, dff, loss_part, dg_ffn_post = _loss_head(act, w_d, x1, target, norm_ffn_post)

    dgate, dup = _ffn_act_bwd(dff, w_d, gate, up)
    grads_t = {}
    grads_t["w_ffn_down"] = _matmul([(act, dff)], "tn", F32, "grad_w_ffn_down", 1408, 512, 2048, staged=False)
    grads_t["w_ffn_gate"] = _matmul([(dgate, h2)], "tn", F32, "grad_w_ffn_gate", 1408, 512, 2048)
    grads_t["w_ffn_up"] = _matmul([(dup, h2)], "tn", F32, "grad_w_ffn_up", 1408, 512, 2048)
    rs_ffn = _ReduceScatter("ffn", {k: grads_t[k] for k in ffn_names}, pos)
    dx1, dmix, dg_ffn_pre, dg_mix_post = _post_ffn_bwd(dgate, wt_g, dup, wt_u, x1, dy, mix, norm_ffn_pre, norm_mix_post,
                                                       deps=[rs_ffn.token])
    rs_ffn.start_chips([dmix])

    dproj, dya, dyb, d_out_a, d_out_b = _mix_out_bwd(dmix, out_a, out_b, proj, wt_pa, wt_pb, w_o, deps=[rs_ffn.token])
    grads_t["w_out"] = _matmul([(merged, dmix)], "tn", F32, "grad_w_out", 1024, 1024, 1024, staged=False)
    def column_slots(g):
        return g.reshape(g.shape[0], N_DEV, LANES).transpose(1, 0, 2)

    grads_t["w_proj_a"] = column_slots(_matmul([(out_a, dya)], "tn", F32, "grad_w_proj_a", DIL_OUT_WIDTH, 1024, SEQ,
                                                   staged=False))
    grads_t["w_proj_b"] = column_slots(_matmul([(out_b, dyb)], "tn", F32, "grad_w_proj_b", FOX_WIDTH, 1024, SEQ,
                                                   staged=False))
    rs_mid = _ReduceScatter("mid", {k: grads_t[k] for k in mid_names}, pos)

    do_dil, c_dil = _dil_combine_bwd(d_out_a, o_dil, lse_dil, deps=[rs_mid.token])
    rs_mid.start_chips([c_dil])
    dproj, d_cum = _fox_bwd(proj, d_out_b, lse_fox, f_rows, dproj)
    d_cum_rows = jnp.pad(d_cum[:, :2].reshape(N_FOX_HEADS, SEQ), ((0, F_ROWS - N_FOX_HEADS), (0, 0)))
    dproj, db_part = _fox_gate_bwd(d_cum_rows, proj, b_pad, dproj)
    dproj = _dil_bwd(proj, tables, do_dil, lse_dil, c_dil, dproj, deps=[rs_mid.token])

    dwt_r = _matmul([(dproj, h)], "tn", F32, "grad_w_in", 896, 1024, 2048)
    rs_in = _ReduceScatter("in", {"w_in": _proj_weight_grad_slots(dwt_r)}, pos)
    def finish(rs, after):
        return {k: _adam(shards[k], mine, recv, moments_m[k], moments_v[k], "adam_" + k)
                for k, (mine, recv) in rs.finish(after).items()}

    done = finish(rs_ffn, [rs_in.token])
    rs_in.start_chips([done[k][0] for k in ffn_names])
    grad_x, dg_mix_pre = _input_bwd(dproj, wt_r, x2, dx1, norm_mix_pre, deps=[rs_in.token])
    done.update(finish(rs_mid, [grad_x]))

    small_all = _all_gather([dg_mix_pre, dg_mix_post, dg_ffn_pre, dg_ffn_post, db_part, loss_part],
                            "small_grads_all_gather", deps=[done[k][0] for k in mid_names])
    small, loss = _adam_small(small_all[:5], [norm_mix_pre, norm_mix_post, norm_ffn_pre, norm_ffn_post, b_forget],
                              [m_norm_mix_pre, m_norm_mix_post, m_norm_ffn_pre, m_norm_ffn_post, m_b_forget],
                              [v_norm_mix_pre, v_norm_mix_post, v_norm_ffn_pre, v_norm_ffn_post, v_b_forget],
                              small_all[5])

    done.update(finish(rs_in, [small[0][0]]))

    def leaves(i):
        def nat(k):
            a = done[k][i]
            return (a.T if k in col_sharded else a)[None]

        return [nat("w_in"), nat("w_proj_a"), nat("w_proj_b"), nat("w_out"), small[4][i],
                nat("w_ffn_gate"), nat("w_ffn_up"), nat("w_ffn_down"), *[small[r][i] for r in range(4)]]

    return (loss[0, 0], grad_x[None], *leaves(0), *leaves(1), *leaves(2), *leaves(3))
```

```python
import functools
import math

import jax
import jax.numpy as jnp
import numpy as np
from jax import lax
from jax.experimental import pallas as pl
from jax.experimental.pallas import tpu as pltpu

F32 = jnp.float32
BF16 = jnp.bfloat16
MESH = pl.DeviceIdType.MESH

D_MODEL = 1024
SEQ = 2048
HEAD_DIM = 64
BLOCK = 128
N_BLOCKS = SEQ // BLOCK
DILATIONS = (1, 4, 16)
N_FOX_HEADS = 8
DIL_WIDTH = 768
DIL_OUT_WIDTH = 256
FOX_WIDTH = 512
D_FF = 2816
ROPE_THETA = 500000.0
ROPE_DIM = HEAD_DIM // 4
ROPE_HALF = ROPE_DIM // 2
EPS = 1e-6
NEG_INF = -1e30
QK_SCALE = 1.0 / math.sqrt(HEAD_DIM)
IN_COLS = 5896
N_DEV = 8
IN_SHARD = IN_COLS // N_DEV

ADAM_LR = 0.001
ADAM_B1 = 0.9
ADAM_B2 = 0.999
ADAM_EPS = 1e-08
ADAM_WD = 0.01
ADAM_STEP = 10

V7X_VMEM_BYTES = 64 * 2**20
LANES = 128
SUBLANES = 8

PROJ_COLS = 6272
COL_GA, COL_GB = 0, 1024
COL_QA, COL_KA, COL_VA = 2304, 3072, 3840
COL_QB, COL_KB, COL_VB = 4608, 5120, 5632
COL_F = 6144
F_ROWS = 16


def _vmem_limit(block_bytes):
    want = 2 * block_bytes + 16 * 2**20
    return int(min(max(want, 32 * 2**20), V7X_VMEM_BYTES - 8 * 2**20))


def _nbytes(shape, dtype):
    return math.prod(shape) * jnp.dtype(dtype).itemsize


def _in_hbm(*arrays):
    return [pltpu.with_memory_space_constraint(a, pltpu.HBM) for a in arrays]


def _dot(a, b, dims):
    return lax.dot_general(a, b, (dims, ((), ())), preferred_element_type=F32)


def _dot_nn(a, b):
    return _dot(a, b, ((1,), (0,)))


def _dot_nt(a, b):
    return _dot(a, b, ((1,), (1,)))


def _dot_tn(a, b):
    return _dot(a, b, ((0,), (0,)))


def _sigmoid(z):
    return 1.0 / (1.0 + jnp.exp(-z))


def _split3(x):
    hi = x.astype(BF16)
    r1 = x - hi.astype(F32)
    mid = r1.astype(BF16)
    lo = (r1 - mid.astype(F32)).astype(BF16)
    return hi, mid, lo


def _dot3_nn(x, ones_matrix):
    hi, mid, lo = _split3(x)
    return (_dot_nn(hi, ones_matrix) + _dot_nn(mid, ones_matrix)) + _dot_nn(lo, ones_matrix)


def _rowwise(fn, name, n_rows, tm, row_ins, bcast_ins, row_outs, acc_outs=(), deps=()):
    n_in = len(row_ins) + len(bcast_ins)
    n_ro = len(row_outs)

    def body(*refs):
        res = fn(*[r[...] for r in refs[:n_in]])
        if not isinstance(res, (tuple, list)):
            res = (res,)
        outs = refs[n_in + len(deps):]
        for r, o in zip(res[:n_ro], outs[:n_ro]):
            o[...] = r.astype(o.dtype)
        first = pl.program_id(0) == 0
        for r, o in zip(res[n_ro:], outs[n_ro:]):
            _accumulate(o, r, first)

    in_specs = [pl.BlockSpec((tm, w), lambda i, cb=cb: (i, cb)) for _, w, cb in row_ins]
    in_specs += [pl.BlockSpec(a.shape, lambda i: (0, 0)) for a in bcast_ins]
    in_specs += [pl.BlockSpec(memory_space=pl.ANY)] * len(deps)
    out_specs = [pl.BlockSpec((tm, w), lambda i: (i, 0)) for w, _ in row_outs]
    out_specs += [pl.BlockSpec((1, w), lambda i: (0, 0)) for w in acc_outs]
    out_shape = [pltpu.HBM((n_rows, w), dt) for w, dt in row_outs]
    out_shape += [pltpu.HBM((1, w), F32) for w in acc_outs]
    blk = sum(_nbytes((tm, w), a.dtype) for a, w, _ in row_ins) + sum(_nbytes((tm, w), dt) for w, dt in row_outs)
    return pl.pallas_call(
        body, name=name, grid=(n_rows // tm,), in_specs=in_specs, out_specs=out_specs, out_shape=out_shape,
        compiler_params=pltpu.CompilerParams(
            dimension_semantics=("arbitrary" if acc_outs else "parallel",), vmem_limit_bytes=_vmem_limit(3 * blk)),
    )(*_in_hbm(*[a for a, _, _ in row_ins], *bcast_ins), *deps)


def _accumulate(o_ref, part, first):
    @pl.when(first)
    def _():
        o_ref[...] = part

    @pl.when(jnp.logical_not(first))
    def _():
        o_ref[...] += part


_MM_DIMS = {"nn": ((1,), (0,)), "nt": ((1,), (1,)), "tn": ((0,), (0,))}


def _matmul(pairs, mode, out_dtype, name, tm, tn, tk, deps=(), staged=True):
    a0, b0 = pairs[0]
    if mode == "tn":
        kk, m = a0.shape
    else:
        m, kk = a0.shape
    n = b0.shape[0] if mode == "nt" else b0.shape[1]
    assert m % tm == 0 and n % tn == 0 and kk % tk == 0, (name, m, n, kk)
    nk = kk // tk
    n_pairs = len(pairs)
    dims = _MM_DIMS[mode]
    n_in = 2 * n_pairs + len(deps)

    def body(*refs):
        o_ref = refs[n_in]
        part = None
        for p in range(n_pairs):
            d = _dot(refs[2 * p][...].astype(BF16), refs[2 * p + 1][...].astype(BF16), dims)
            part = d if part is None else part + d
        if nk == 1:
            o_ref[...] = part.astype(o_ref.dtype)
            return
        acc = refs[n_in + 1]
        k = pl.program_id(2)

        @pl.when(k == 0)
        def _():
            acc[...] = part

        @pl.when(k > 0)
        def _():
            acc[...] += part

        @pl.when(k == nk - 1)
        def _():
            o_ref[...] = acc[...].astype(o_ref.dtype)

    if mode == "tn":
        a_spec = pl.BlockSpec((tk, tm), lambda i, j, k: (k, i))
    else:
        a_spec = pl.BlockSpec((tm, tk), lambda i, j, k: (i, k))
    if mode == "nt":
        b_spec = pl.BlockSpec((tn, tk), lambda i, j, k: (j, k))
    else:
        b_spec = pl.BlockSpec((tk, tn), lambda i, j, k: (k, j))
    blk = sum(_nbytes((tm, tk), a.dtype) + _nbytes((tk, tn), b.dtype) for a, b in pairs) + 2 * _nbytes((tm, tn), F32)
    flat = [a for pair in pairs for a in pair]
    return pl.pallas_call(
        body, name=name, grid=(m // tm, n // tn, nk),
        in_specs=[a_spec, b_spec] * n_pairs + [pl.BlockSpec(memory_space=pl.ANY)] * len(deps),
        out_specs=pl.BlockSpec((tm, tn), lambda i, j, k: (i, j)),
        out_shape=pltpu.HBM((m, n), out_dtype),
        scratch_shapes=[] if nk == 1 else [pltpu.VMEM((tm, tn), F32)],
        compiler_params=pltpu.CompilerParams(
            dimension_semantics=("parallel", "parallel", "arbitrary"), vmem_limit_bytes=_vmem_limit(blk)),
    )(*(flat if staged else _in_hbm(*flat)), *deps)


def _matmul_rowwise(pairs, fn, name, tm, row_ins, bcast_ins, row_outs, acc_outs=(), deps=()):
    m = pairs[0][0].shape[0]
    n_mm, n_in = 2 * len(pairs), len(row_ins) + len(bcast_ins)
    n_ro = len(row_outs)

    def body(*refs):
        prod = None
        for p in range(len(pairs)):
            part = _dot_nn(refs[2 * p][...].astype(BF16), refs[2 * p + 1][...].astype(BF16))
            prod = part if prod is None else prod + part
        res = fn(prod, *[r[...] for r in refs[n_mm:n_mm + n_in]])
        outs = refs[n_mm + n_in + len(deps):]
        for r, o in zip(res[:n_ro], outs[:n_ro]):
            o[...] = r.astype(o.dtype)
        first = pl.program_id(0) == 0
        for r, o in zip(res[n_ro:], outs[n_ro:]):
            _accumulate(o, r, first)

    in_specs = []
    for a, b in pairs:
        in_specs += [pl.BlockSpec((tm, a.shape[1]), lambda i: (i, 0)),
                     pl.BlockSpec(b.shape, lambda i: (0, 0), pipeline_mode=pl.Buffered(1))]
    in_specs += [pl.BlockSpec((tm, w), lambda i, cb=cb: (i, cb)) for _, w, cb in row_ins]
    in_specs += [pl.BlockSpec(a.shape, lambda i: (0, 0)) for a in bcast_ins]
    in_specs += [_ANY] * len(deps)
    out_specs = [pl.BlockSpec((tm, w), lambda i: (i, 0)) for w, _ in row_outs]
    out_specs += [pl.BlockSpec((1, w), lambda i: (0, 0)) for w in acc_outs]
    out_shape = [pltpu.HBM((m, w), dt) for w, dt in row_outs]
    out_shape += [pltpu.HBM((1, w), F32) for w in acc_outs]
    blk = sum(_nbytes((tm, a.shape[1]), a.dtype) + _nbytes(b.shape, b.dtype) // 2 for a, b in pairs)
    blk += sum(_nbytes((tm, w), a.dtype) for a, w, _ in row_ins) + sum(_nbytes((tm, w), dt) for w, dt in row_outs)
    return pl.pallas_call(
        body, name=name, grid=(m // tm,), in_specs=in_specs, out_specs=out_specs, out_shape=out_shape,
        compiler_params=pltpu.CompilerParams(dimension_semantics=("arbitrary",), vmem_limit_bytes=_vmem_limit(blk)),
    )(*[a for pair in pairs for a in pair], *[a for a, _, _ in row_ins], *bcast_ins, *deps)


def _rms_scale(x):
    return lax.rsqrt(jnp.mean(x * x, axis=-1, keepdims=True) + EPS)


def _rms_bwd(xin, dyn, g):
    r = _rms_scale(xin)
    u = dyn * g
    dx = r * u - xin * (r * r * r) * jnp.mean(u * xin, axis=-1, keepdims=True)
    dg = jnp.sum(dyn * xin * r, axis=0, keepdims=True)
    return dx, dg


def _mesh_pos():
    return lax.axis_index("x"), lax.axis_index("y"), lax.axis_index("c")


def _all_gather(xs, name, deps=()):
    n = len(xs)

    def body(*refs):
        x_refs, out_refs = refs[:n], refs[n + len(deps):2 * n + len(deps)]
        send_sems, recv_sems, local_sems = refs[2 * n + len(deps):]
        mx, my, mc = _mesh_pos()
        me, sib = (mx, my, mc), (mx, my, 1 - mc)
        chips = [(1 - mx, my), (mx, 1 - my), (1 - mx, 1 - my)]

        def slot(a, dev):
            px, py, pc = dev
            return out_refs[a].at[4 * px + 2 * py + pc]

        def copy(k, a, block, to, src=None):
            return pltpu.make_async_remote_copy(
                src_ref=slot(a, block) if src is None else src, dst_ref=slot(a, block),
                send_sem=send_sems.at[a * 7 + k], recv_sem=recv_sems.at[a * 7 + k],
                device_id=to, device_id_type=MESH)

        mine = [pltpu.make_async_copy(x_refs[a], slot(a, me), local_sems.at[a]) for a in range(n)]
        for cp in mine:
            cp.start()
        first = []
        for a in range(n):
            first.append(copy(0, a, me, sib, x_refs[a]))
            first += [copy(1 + j, a, me, (*chip, mc), x_refs[a]) for j, chip in enumerate(chips)]
        for cp in first:
            cp.start()
        passed = []
        for a in range(n):
            for j, chip in enumerate(chips):
                copy(1 + j, a, (*chip, mc), me).wait_recv()
                fwd = copy(4 + j, a, (*chip, mc), sib)
                fwd.start()
                passed.append(fwd)
        for a in range(n):
            copy(0, a, sib, me).wait_recv()
            for j, chip in enumerate(chips):
                copy(4 + j, a, (*chip, 1 - mc), me).wait_recv()
        for cp in first + passed:
            cp.wait_send()
        for cp in mine:
            cp.wait()

    hbm = pl.BlockSpec(memory_space=pl.ANY)
    return pl.pallas_call(
        body, name=name,
        out_shape=[pltpu.HBM((N_DEV,) + x.shape, x.dtype) for x in xs],
        in_specs=[hbm] * (n + len(deps)), out_specs=[hbm] * n,
        scratch_shapes=[pltpu.SemaphoreType.DMA((7 * n,)), pltpu.SemaphoreType.DMA((7 * n,)),
                        pltpu.SemaphoreType.DMA((n,))],
    )(*xs, *deps)


_HBM = pl.BlockSpec(memory_space=pltpu.HBM)
_SEM = pl.BlockSpec(memory_space=pltpu.SEMAPHORE)
_ANY = pl.BlockSpec(memory_space=pl.ANY)
_DATAFLOW = pltpu.SideEffectType.DATAFLOW_SIDE_EFFECTING


def _flip_peer(flip):
    mx, my, mc = _mesh_pos()
    return (1 - mx if flip & 2 else mx, 1 - my if flip & 1 else my, mc)


def _remote(src, dst, send_sems, recv_sems, k, peer):
    return pltpu.make_async_remote_copy(src_ref=src, dst_ref=dst, send_sem=send_sems.at[k], recv_sem=recv_sems.at[k],
                                        device_id=peer, device_id_type=MESH)


def _gather_chips_copies(srcs, lands, send_sems, recv_sems):
    mx, my, mc = _mesh_pos()
    me = 4 * mx + 2 * my + mc
    return [_remote(srcs[a], lands[a].at[me], send_sems, recv_sems, 3 * a + flip - 1, _flip_peer(flip))
            for a in range(len(srcs)) for flip in (1, 2, 3)]


def _gather_sibling_copies(srcs, lands, send_sems, recv_sems):
    mx, my, mc = _mesh_pos()
    return [_remote(lands[a].at[2 * k + mc], lands[a].at[2 * k + mc], send_sems, recv_sems, 4 * a + k, (mx, my, 1 - mc))
            for a in range(len(lands)) for k in range(4)]


def _scatter_sibling_copies(srcs, lands, send_sems, recv_sems):
    mx, my, mc = _mesh_pos()
    return [_remote(srcs[a].at[k, 1 - mc], lands[a].at[k], send_sems, recv_sems, 4 * a + k, (mx, my, 1 - mc))
            for a in range(len(srcs)) for k in range(4)]


def _scatter_chips_copies(srcs, lands, send_sems, recv_sems):
    mx, my, _ = _mesh_pos()
    k0 = 2 * mx + my
    return [_remote(srcs[a].at[jnp.bitwise_xor(k0, flip)], lands[a].at[flip - 1], send_sems, recv_sems,
                    3 * a + flip - 1, _flip_peer(flip))
            for a in range(len(srcs)) for flip in (1, 2, 3)]


class _Exchange:
    def __init__(self, copies, n_src, send_sems, recv_sems, thru, token):
        self.copies, self.n_src, self.send_sems, self.recv_sems, self.thru, self.token = (
            copies, n_src, send_sems, recv_sems, thru, token)


def _exchange_start(name, copies, srcs, lands, n_copies, after=()):
    bufs = list(srcs) + list(lands)
    nb, ns = len(bufs), len(srcs)

    def body(*refs):
        send_sems, recv_sems = refs[nb + len(after)], refs[nb + len(after) + 1]
        for cp in copies(refs[:ns], refs[ns:nb], send_sems, recv_sems):
            cp.start()
        refs[-1][...] = jnp.zeros_like(refs[-1])

    out = pl.pallas_call(
        body, name=name,
        out_shape=(pltpu.SemaphoreType.DMA((n_copies,)), pltpu.SemaphoreType.DMA((n_copies,)),
                   *[pltpu.HBM(b.shape, b.dtype) for b in bufs], pltpu.HBM((SUBLANES, LANES), F32)),
        in_specs=[_HBM] * nb + [_ANY] * len(after),
        out_specs=(_SEM, _SEM, *[_HBM] * nb, pl.BlockSpec(memory_space=pltpu.VMEM)),
        input_output_aliases={i: 2 + i for i in range(nb)},
        compiler_params=pltpu.CompilerParams(has_side_effects=_DATAFLOW),
    )(*[pltpu.with_memory_space_constraint(b, pltpu.HBM) for b in bufs], *after)
    return _Exchange(copies, ns, out[0], out[1], list(out[2:2 + nb]), out[-1])


def _exchange_wait(name, ex, after):
    nb, ns = len(ex.thru), ex.n_src

    def body(*refs):
        for cp in ex.copies(refs[:ns], refs[ns:nb], refs[nb], refs[nb + 1]):
            cp.wait_send()
            cp.wait_recv()

    out = pl.pallas_call(
        body, name=name, out_shape=tuple(pltpu.HBM(b.shape, b.dtype) for b in ex.thru),
        in_specs=[_HBM] * nb + [_SEM, _SEM] + [_ANY] * len(after), out_specs=tuple([_HBM] * nb),
        input_output_aliases={i: i for i in range(nb)},
        compiler_params=pltpu.CompilerParams(has_side_effects=_DATAFLOW),
    )(*ex.thru, ex.send_sems, ex.recv_sems, *after)
    return list(out[:ns]), list(out[ns:])


def _col_tile(r, c):
    return next(t for t in (1024, 512, 256, 128) if c % t == 0 and (r * t * 4 <= 2**20 or t == 128))


def _add_sibling(g4, recv, pos, name):
    _, _, r, c = g4.shape
    tc = _col_tile(r, c)

    def body(pos_ref, g_ref, r_ref, o16_ref, mine_ref):
        s = g_ref[0, 0] + r_ref[0]
        o16_ref[0] = s.astype(BF16)

        @pl.when(pl.program_id(1) == pos_ref[1])
        def _():
            mine_ref[...] = s

    slot = pl.BlockSpec((1, r, tc), lambda j, k, pos_ref: (k, 0, j))
    return pl.pallas_call(
        body, name=name,
        out_shape=[pltpu.HBM((4, r, c), BF16), pltpu.HBM((r, c), F32)],
        grid_spec=pltpu.PrefetchScalarGridSpec(
            num_scalar_prefetch=1, grid=(c // tc, 4),
            in_specs=[pl.BlockSpec((1, 1, r, tc), lambda j, k, pos_ref: (k, pos_ref[0], 0, j)), slot],
            out_specs=[slot, pl.BlockSpec((r, tc), lambda j, k, pos_ref: (0, j))]),
        compiler_params=pltpu.CompilerParams(dimension_semantics=("parallel", "arbitrary")),
    )(pos, *_in_hbm(g4, recv))


class _ReduceScatter:
    def __init__(self, tag, grads_t, pos):
        self.tag, self.pos, self.names = tag, pos, list(grads_t)
        g4s = [g.reshape(4, 2, g.size // (N_DEV * g.shape[-1]), g.shape[-1]) for g in grads_t.values()]
        lands = [lax.empty((4,) + g.shape[2:], F32) for g in g4s]
        self.ex = _exchange_start(f"rs_{tag}_sibling_start", _scatter_sibling_copies, g4s, lands, 4 * len(g4s))
        self.token = self.ex.token

    def start_chips(self, after):
        g4s, from_sibling = _exchange_wait(f"rs_{self.tag}_sibling_wait", self.ex, after)
        parts = [_add_sibling(g4, rv, self.pos, f"rs_add_sibling_{k}")
                 for k, g4, rv in zip(self.names, g4s, from_sibling)]
        self.mine = [mine for _, mine in parts]
        p16s = [p16 for p16, _ in parts]
        lands = [lax.empty((3,) + p.shape[1:], BF16) for p in p16s]
        self.ex = _exchange_start(f"rs_{self.tag}_chips_start", _scatter_chips_copies, p16s, lands, 3 * len(p16s))
        self.token = self.ex.token

    def finish(self, after):
        _, from_chips = _exchange_wait(f"rs_{self.tag}_chips_wait", self.ex, after)
        return dict(zip(self.names, zip(self.mine, from_chips)))


def _rope_tables():
    positions = np.arange(SEQ, dtype=np.float32)
    inv_freq = np.power(np.float32(ROPE_THETA), -np.arange(0, ROPE_DIM, 2, dtype=np.float32) / np.float32(ROPE_DIM))
    ang = (positions[:, None] * inv_freq[None, :]).astype(np.float32)
    cos, sin = np.cos(ang).astype(np.float32), np.sin(ang).astype(np.float32)
    ones = np.ones((SEQ, HEAD_DIM - ROPE_DIM), np.float32)
    zeros8 = np.zeros((SEQ, ROPE_HALF), np.float32)
    zeros = np.zeros((SEQ, HEAD_DIM - ROPE_DIM), np.float32)
    c_head = np.concatenate([cos, cos, ones], axis=1)
    s1_head = np.concatenate([-sin, zeros8, zeros], axis=1)
    s2_head = np.concatenate([zeros8, sin, zeros], axis=1)
    return tuple(jnp.asarray(np.concatenate([t, t], axis=1)) for t in (c_head, s1_head, s2_head))


def _rope_apply(x, c, s1, s2):
    w = x.shape[1]
    return x * c + pltpu.roll(x, w - ROPE_HALF, 1) * s1 + pltpu.roll(x, ROPE_HALF, 1) * s2


def _rope_apply_t(dy, c, s1, s2):
    w = dy.shape[1]
    return dy * c + pltpu.roll(dy * s1, ROPE_HALF, 1) + pltpu.roll(dy * s2, w - ROPE_HALF, 1)


def _dil_prev_limit(has_prev):
    return jnp.where(has_prev, 0, BLOCK)


def _dil_valid(limit):
    row = lax.broadcasted_iota(jnp.int32, (BLOCK, 2 * BLOCK), 0)
    col = lax.broadcasted_iota(jnp.int32, (BLOCK, 2 * BLOCK), 1)
    dist = col - row
    return jnp.logical_and(dist >= jnp.where(col < BLOCK, limit, -BLOCK), dist <= BLOCK)


def _upper_half():
    return lax.broadcasted_iota(jnp.int32, (1, LANES), 1) >= HEAD_DIM


def _stack_heads(x):
    upper = _upper_half()
    return jnp.concatenate([jnp.where(upper, 0, x), jnp.where(upper, x, 0)], axis=0)


def _unstack_heads(y):
    n = y.shape[0] // 2
    return jnp.where(_upper_half(), y[n:], y[:n])


def _head_columns(t):
    return jnp.concatenate([t[:, 0:1], t[:, HEAD_DIM:HEAD_DIM + 1]], axis=0)


def _dil_rows(n, d):
    per = N_BLOCKS // d
    r, lb = n // per, n % per

    def rows(b):
        start = b * (BLOCK * d) + r
        return pl.ds(pl.multiple_of(start, BLOCK), BLOCK) if d == 1 else pl.ds(start, BLOCK, stride=d)

    return rows(lb), rows(jnp.maximum(lb - 1, 0)), lb > 0


def _dil_rotate(q_ref, k_ref, c_ref, s1_ref, s2_ref, q_rot, k_rot):
    tabs = (c_ref[...], s1_ref[...], s2_ref[...])
    q_rot[...] = _rope_apply(q_ref[...], *tabs) * QK_SCALE
    k_rot[...] = _rope_apply(k_ref[...], *tabs)


def _dil_specs():
    def col(base):
        return pl.BlockSpec((SEQ, LANES), lambda p: (0, base // LANES + p))

    table = pl.BlockSpec((SEQ, LANES), lambda p: (0, 0))
    return [col(COL_QA), col(COL_KA), col(COL_VA)], [table] * 3


def _store_columns(blocks, dproj_ref, cols, sem):
    copies = [pltpu.make_async_copy(b, dproj_ref.at[:, pl.ds(pl.multiple_of(c * LANES, LANES), LANES)], sem.at[i])
              for i, (b, c) in enumerate(zip(blocks, cols))]
    for cp in copies:
        cp.start()
    for cp in copies:
        cp.wait()


def _dil_window(d, n, k_rot, v_ref):
    rows, prev, has_prev = _dil_rows(n, d)
    kw, vw = k_rot[rows, :].astype(BF16), v_ref[rows, :].astype(BF16)
    if d == N_BLOCKS:
        row = lax.broadcasted_iota(jnp.int32, (BLOCK, BLOCK), 0)
        valid = lax.broadcasted_iota(jnp.int32, (BLOCK, BLOCK), 1) <= row
    else:
        kw = jnp.concatenate([k_rot[prev, :].astype(BF16), kw], axis=0)
        vw = jnp.concatenate([v_ref[prev, :].astype(BF16), vw], axis=0)
        valid = _dil_valid(_dil_prev_limit(has_prev))
    return rows, prev, kw, vw, jnp.concatenate([valid, valid], axis=0)


def _dil_fwd(proj, tables):
    def body(q_ref, k_ref, v_ref, c_ref, s1_ref, s2_ref, o_ref, lse_ref, q_rot, k_rot):
        upper = _upper_half()
        _dil_rotate(q_ref, k_ref, c_ref, s1_ref, s2_ref, q_rot, k_rot)

        def blocks_of(d):
            def block(n, carry):
                rows, _, kw, vw, valid = _dil_window(d, n, k_rot, v_ref)
                s = jnp.where(valid, _dot_nt(_stack_heads(q_rot[rows, :].astype(BF16)), kw), NEG_INF)
                m = jnp.max(s, axis=-1, keepdims=True)
                p = jnp.exp(s - m)
                den = jnp.sum(p, axis=-1, keepdims=True)
                o_ref[rows, :] = _unstack_heads(_dot_nn((p * (1.0 / den)).astype(BF16), vw))
                lse = m + jnp.log(den)
                lse_ref[rows, :] = jnp.where(upper, lse[BLOCK:], lse[:BLOCK])
                return carry

            lax.fori_loop(0, N_BLOCKS, block, 0, unroll=4)

        for g, d in enumerate(DILATIONS):
            pl.when(pl.program_id(0) // 2 == g)(functools.partial(blocks_of, d))

    qkv, tabs = _dil_specs()
    out = pl.BlockSpec((SEQ, LANES), lambda p: (0, p))
    return pl.pallas_call(
        body, name="dil_attn_fwd", grid=(DIL_WIDTH // LANES,), in_specs=qkv + tabs, out_specs=[out, out],
        out_shape=[pltpu.HBM((SEQ, DIL_WIDTH), F32)] * 2,
        scratch_shapes=[pltpu.VMEM((SEQ, LANES), F32)] * 2,
        compiler_params=pltpu.CompilerParams(dimension_semantics=("parallel",)),
    )(*_in_hbm(proj, proj, proj, *tables))


def _dil_bwd(proj, tables, do, lse, c, dproj, deps=()):
    def body(q_ref, k_ref, v_ref, c_ref, s1_ref, s2_ref, do_ref, lse_ref, cc_ref, dproj_in, *rest):
        dproj_ref, dq_acc, dk_acc, dv_acc, dq_out, dk_out, dv_out, q_rot, k_rot, sem = rest[len(deps):]
        dk_acc[...] = jnp.zeros_like(dk_acc)
        dv_acc[...] = jnp.zeros_like(dv_acc)
        _dil_rotate(q_ref, k_ref, c_ref, s1_ref, s2_ref, q_rot, k_rot)

        def blocks_of(d):
            def block(n, carry):
                rows, prev, kw, vw, valid = _dil_window(d, n, k_rot, v_ref)
                q2 = _stack_heads(q_rot[rows, :].astype(BF16))
                do2 = _stack_heads(do_ref[rows, :].astype(BF16))
                lse_col, c_col = _head_columns(lse_ref[rows, :]), _head_columns(cc_ref[rows, :])
                p = jnp.where(valid, jnp.exp(_dot_nt(q2, kw) - lse_col), 0.0)
                ds = (p * (_dot_nt(do2, vw) + c_col)).astype(BF16)
                dk, dv = _dot_tn(ds, q2), _dot_tn(p.astype(BF16), do2)
                dq_acc[rows, :] = _unstack_heads(_dot_nn(ds, kw)) * QK_SCALE
                if d == N_BLOCKS:
                    dk_acc[rows, :] += dk
                    dv_acc[rows, :] += dv
                else:
                    dk_acc[prev, :] += dk[:BLOCK]
                    dv_acc[prev, :] += dv[:BLOCK]
                    dk_acc[rows, :] += dk[BLOCK:]
                    dv_acc[rows, :] += dv[BLOCK:]
                return carry

            lax.fori_loop(0, N_BLOCKS, block, 0, unroll=4)

        pair = pl.program_id(0)
        for g, d in enumerate(DILATIONS):
            pl.when(pair // 2 == g)(functools.partial(blocks_of, d))
        tabs = (c_ref[...], s1_ref[...], s2_ref[...])
        dq_out[...] = _rope_apply_t(dq_acc[...], *tabs).astype(BF16)
        dk_out[...] = _rope_apply_t(dk_acc[...], *tabs).astype(BF16)
        dv_out[...] = dv_acc[...].astype(BF16)
        _store_columns((dq_out, dk_out, dv_out), dproj_ref,
                       [base // LANES + pair for base in (COL_QA, COL_KA, COL_VA)], sem)

    qkv, tabs = _dil_specs()
    tok = pl.BlockSpec((SEQ, LANES), lambda p: (0, p))
    return pl.pallas_call(
        body, name="dil_attn_bwd", grid=(DIL_WIDTH // LANES,),
        in_specs=qkv + tabs + [tok, tok, tok, _ANY] + [_ANY] * len(deps), out_specs=_ANY,
        out_shape=pltpu.HBM(dproj.shape, dproj.dtype),
        scratch_shapes=[pltpu.VMEM((SEQ, LANES), F32)] * 3 + [pltpu.VMEM((SEQ, LANES), BF16)] * 3
        + [pltpu.VMEM((SEQ, LANES), F32)] * 2 + [pltpu.SemaphoreType.DMA((3,))],
        input_output_aliases={9: 0},
        compiler_params=pltpu.CompilerParams(dimension_semantics=("arbitrary",)),
    )(*_in_hbm(proj, proj, proj, *tables, do, lse, c, dproj), *deps)


def _group_weights(l0, l1, l2):
    m = jnp.maximum(jnp.maximum(l0, l1), l2)
    e0, e1, e2 = jnp.exp(l0 - m), jnp.exp(l1 - m), jnp.exp(l2 - m)
    tot = e0 + e1 + e2
    return e0 / tot, e1 / tot, e2 / tot


def _dil_combine(o, lse, deps=()):
    def fn(o0, o1, o2, l0, l1, l2):
        w0, w1, w2 = _group_weights(l0, l1, l2)
        return w0 * o0 + w1 * o1 + w2 * o2

    w = DIL_OUT_WIDTH
    return _rowwise(fn, "dil_combine", SEQ, 512, [(o, w, g) for g in range(3)] + [(lse, w, g) for g in range(3)], [],
                    [(w, F32)], deps=deps)[0]


def _dil_combine_bwd(d_out, o, lse, deps=()):
    w = DIL_OUT_WIDTH

    def fn(d, o0, o1, o2, l0, l1, l2):
        row = lax.broadcasted_iota(jnp.int32, (w, w), 0) // HEAD_DIM
        col = lax.broadcasted_iota(jnp.int32, (w, w), 1) // HEAD_DIM
        same_head = jnp.where(row == col, 1.0, 0.0).astype(BF16)
        ws = _group_weights(l0, l1, l2)
        dws = [_dot3_nn(d * og, same_head) for og in (o0, o1, o2)]
        mean = ws[0] * dws[0] + ws[1] * dws[1] + ws[2] * dws[2]
        return jnp.concatenate([wg * d for wg in ws], axis=1), jnp.concatenate([-wg * mean for wg in ws], axis=1)

    return _rowwise(fn, "dil_combine_bwd", SEQ, 256,
                    [(d_out, w, 0)] + [(o, w, g) for g in range(3)] + [(lse, w, g) for g in range(3)], [],
                    [(DIL_WIDTH, F32)] * 2, deps=deps)


def _log1p(e):
    u = 1.0 + e
    return jnp.where(u == 1.0, e, jnp.log(u) * (e / (u - 1.0)))


def _fox_gate(proj, b_pad, deps=()):
    def body(f_ref, b_ref, *rest):
        o_ref = rest[-1]
        z = f_ref[...] + b_ref[...]
        logf = (jnp.minimum(z, 0.0) - _log1p(jnp.exp(-jnp.abs(z)))).T[:F_ROWS]
        row = lax.broadcasted_iota(jnp.int32, (BLOCK, BLOCK), 0)
        col = lax.broadcasted_iota(jnp.int32, (BLOCK, BLOCK), 1)
        before = jnp.where(row <= col, 1.0, 0.0).astype(BF16)
        carry = jnp.zeros((F_ROWS, 1), F32)
        for blk in range(N_BLOCKS):
            run = _dot3_nn(logf[:, blk * BLOCK:(blk + 1) * BLOCK], before) + carry
            o_ref[:, blk * BLOCK:(blk + 1) * BLOCK] = run
            carry = run[:, BLOCK - 1:BLOCK]

    return pl.pallas_call(
        body, name="fox_gate", grid=(1,),
        in_specs=[pl.BlockSpec((SEQ, LANES), lambda i: (0, COL_F // LANES)), pl.BlockSpec((1, LANES), lambda i: (0, 0))]
        + [_ANY] * len(deps),
        out_specs=pl.BlockSpec((F_ROWS, SEQ), lambda i: (0, 0)),
        out_shape=pltpu.HBM((F_ROWS, SEQ), F32),
    )(*_in_hbm(proj, b_pad), *deps)


def _fox_gate_bwd(d_cum, proj, b_pad, dproj):
    def body(d_ref, f_ref, b_ref, dproj_ref, dz_ref, db_ref):
        row = lax.broadcasted_iota(jnp.int32, (BLOCK, BLOCK), 0)
        col = lax.broadcasted_iota(jnp.int32, (BLOCK, BLOCK), 1)
        after = jnp.where(row >= col, 1.0, 0.0).astype(BF16)
        carry = jnp.zeros((F_ROWS, 1), F32)
        parts = [None] * N_BLOCKS
        for blk in reversed(range(N_BLOCKS)):
            run = _dot3_nn(d_ref[:, blk * BLOCK:(blk + 1) * BLOCK], after) + carry
            parts[blk] = run
            carry = run[:, 0:1]
        dlogf = jnp.concatenate(parts, axis=1)
        dlogf = jnp.concatenate([dlogf, jnp.zeros((LANES - F_ROWS, SEQ), F32)], axis=0).T
        dz = dlogf * _sigmoid(-(f_ref[...] + b_ref[...]))
        dz_ref[...] = dz.astype(BF16)
        db_ref[...] = jnp.sum(dz, axis=0, keepdims=True)

    f_cols = pl.BlockSpec((SEQ, LANES), lambda i: (0, COL_F // LANES))
    return pl.pallas_call(
        body, name="fox_gate_bwd", grid=(1,),
        in_specs=[pl.BlockSpec((F_ROWS, SEQ), lambda i: (0, 0)), f_cols, pl.BlockSpec((1, LANES), lambda i: (0, 0)), _ANY],
        out_specs=[f_cols, pl.BlockSpec((1, LANES), lambda i: (0, 0))],
        out_shape=[pltpu.HBM(dproj.shape, dproj.dtype), pltpu.HBM((1, LANES), F32)],
        input_output_aliases={3: 0},
    )(*_in_hbm(d_cum, proj, b_pad, dproj))


FOX_TILE = 256
FOX_TILES = SEQ // FOX_TILE


def _row_to_col(row):
    n = row.shape[1]
    eye = lax.broadcasted_iota(jnp.int32, (n, n), 0) == lax.broadcasted_iota(jnp.int32, (n, n), 1)
    return jnp.sum(jnp.where(eye, row, 0.0), axis=1, keepdims=True)


def _fox_bias(f_row, i):
    t = FOX_TILE
    ext = (i + 1) * t
    bias = _row_to_col(f_row[:, i * t:(i + 1) * t]) - f_row[:, :ext]
    row = lax.broadcasted_iota(jnp.int32, (t, ext), 0) + i * t
    col = lax.broadcasted_iota(jnp.int32, (t, ext), 1)
    return bias, col <= row


def _fox_specs():
    qkv = [pl.BlockSpec((SEQ, LANES), lambda p, base=base: (0, base // LANES + p)) for base in (COL_QB, COL_KB, COL_VB)]
    return qkv, pl.BlockSpec((F_ROWS, SEQ), lambda p: (0, 0))


def _fox_fwd(proj, f_rows):
    t = FOX_TILE

    def body(q_ref, k_ref, v_ref, f_ref, o_ref, lse_ref):
        pair = pl.program_id(0)
        upper = _upper_half()
        k16, v16 = k_ref[...].astype(BF16), v_ref[...].astype(BF16)
        f_row = [f_ref[pl.ds(2 * pair + e, 1), :] for e in range(2)]
        for i in range(FOX_TILES):
            ext = (i + 1) * t
            q_tile = (q_ref[i * t:(i + 1) * t, :] * QK_SCALE).astype(BF16)
            s2 = _dot_nt(_stack_heads(q_tile), k16[:ext])
            pns, lses = [], []
            for e in range(2):
                bias, causal = _fox_bias(f_row[e], i)
                s = jnp.where(causal, s2[e * t:(e + 1) * t] + bias, NEG_INF)
                m = jnp.max(s, axis=-1, keepdims=True)
                p = jnp.exp(s - m)
                den = jnp.sum(p, axis=-1, keepdims=True)
                pns.append((p * (1.0 / den)).astype(BF16))
                lses.append(m + jnp.log(den))
            o_ref[i * t:(i + 1) * t, :] = _unstack_heads(_dot_nn(jnp.concatenate(pns, axis=0), v16[:ext]))
            lse_ref[i * t:(i + 1) * t, :] = jnp.where(upper, lses[1], lses[0])

    qkv, f_spec = _fox_specs()
    tok = pl.BlockSpec((SEQ, LANES), lambda p: (0, p))
    return pl.pallas_call(
        body, name="fox_attn_fwd", grid=(FOX_WIDTH // LANES,),
        in_specs=qkv + [f_spec], out_specs=[tok, tok],
        out_shape=[pltpu.HBM((SEQ, FOX_WIDTH), F32)] * 2,
        compiler_params=pltpu.CompilerParams(
            dimension_semantics=("parallel",), vmem_limit_bytes=_vmem_limit(8 * t * SEQ * 4)),
    )(*_in_hbm(proj, proj, proj, f_rows))


def _fox_bwd(proj, do, lse, f_rows, dproj):
    t = FOX_TILE

    def body(q_ref, k_ref, v_ref, f_ref, do_ref, lse_ref, dproj_in, dproj_ref, df_ref, dk_acc, dv_acc,
             dq_out, dk_out, dv_out, sem):
        pair = pl.program_id(0)
        upper = _upper_half()
        k16, v16 = k_ref[...].astype(BF16), v_ref[...].astype(BF16)
        f_row = [f_ref[pl.ds(2 * pair + e, 1), :] for e in range(2)]
        dk_acc[...] = jnp.zeros_like(dk_acc)
        dv_acc[...] = jnp.zeros_like(dv_acc)
        df_ref[...] = jnp.zeros_like(df_ref)
        for i in range(FOX_TILES):
            ext = (i + 1) * t
            q_tile = (q_ref[i * t:(i + 1) * t, :] * QK_SCALE).astype(BF16)
            do_tile = do_ref[i * t:(i + 1) * t, :]
            lse_t = lse_ref[i * t:(i + 1) * t, :]
            q2, do2 = _stack_heads(q_tile), _stack_heads(do_tile)
            s2, dp2 = _dot_nt(q2, k16[:ext]), _dot_nt(do2, v16[:ext])
            ps, dss = [], []
            for e in range(2):
                bias, causal = _fox_bias(f_row[e], i)
                s = s2[e * t:(e + 1) * t] + bias
                p = jnp.where(causal, jnp.exp(s - lse_t[:, e * HEAD_DIM:e * HEAD_DIM + 1]), 0.0)
                dp = dp2[e * t:(e + 1) * t]
                ds = p * (dp - jnp.sum(p * dp, axis=-1, keepdims=True))
                df_ref[0, e:e + 1, :ext] -= jnp.sum(ds, axis=0, keepdims=True)
                ps.append(p.astype(BF16))
                dss.append(ds.astype(BF16))
            ds2, p2 = jnp.concatenate(dss, axis=0), jnp.concatenate(ps, axis=0)
            dq_out[i * t:(i + 1) * t, :] = (_unstack_heads(_dot_nn(ds2, k16[:ext])) * QK_SCALE).astype(BF16)
            dk_acc[:ext, :] += _dot_tn(ds2, q2)
            dv_acc[:ext, :] += _dot_tn(p2, do2)
        dk_out[...] = dk_acc[...].astype(BF16)
        dv_out[...] = dv_acc[...].astype(BF16)
        _store_columns((dq_out, dk_out, dv_out), dproj_ref, [base // LANES + pair for base in (COL_QB, COL_KB, COL_VB)],
                       sem)

    qkv, f_spec = _fox_specs()
    tok = pl.BlockSpec((SEQ, LANES), lambda p: (0, p))
    return pl.pallas_call(
        body, name="fox_attn_bwd", grid=(FOX_WIDTH // LANES,),
        in_specs=qkv + [f_spec, tok, tok, _ANY],
        out_specs=[_ANY, pl.BlockSpec((1, SUBLANES, SEQ), lambda p: (p, 0, 0))],
        out_shape=[pltpu.HBM(dproj.shape, dproj.dtype),
                   pltpu.HBM((FOX_WIDTH // LANES, SUBLANES, SEQ), F32)],
        scratch_shapes=[pltpu.VMEM((SEQ, LANES), F32)] * 2 + [pltpu.VMEM((SEQ, LANES), BF16)] * 3
        + [pltpu.SemaphoreType.DMA((3,))],
        input_output_aliases={6: 0},
        compiler_params=pltpu.CompilerParams(
            dimension_semantics=("arbitrary",), vmem_limit_bytes=_vmem_limit(10 * t * SEQ * 4)),
    )(*_in_hbm(proj, proj, proj, f_rows, do, lse, dproj))


MIX_TILE = 256


def _mix_out(out_a, out_b, proj, x, wt_pa, wt_pb, w_out, g_post, g_ffn_pre):
    tm = MIX_TILE

    def body(a_ref, b_ref, ga_ref, gb_ref, x_ref, wpa_ref, wpb_ref, wo_ref, g2_ref, g3_ref,
             merged_ref, mix_ref, x1_ref, h2_ref):
        ya = _dot_nn(a_ref[...].astype(BF16), wpa_ref[...])
        yb = _dot_nn(b_ref[...].astype(BF16), wpb_ref[...])
        merged = (_sigmoid(ga_ref[...]) * ya + _sigmoid(gb_ref[...]) * yb).astype(BF16)
        merged_ref[...] = merged
        mix = _dot_nn(merged, wo_ref[...])
        mix_ref[...] = mix
        x1 = x_ref[...] + mix * _rms_scale(mix) * g2_ref[...]
        x1_ref[...] = x1
        h2_ref[...] = (x1 * _rms_scale(x1) * g3_ref[...]).astype(BF16)

    def rows(w, cb=0):
        return pl.BlockSpec((tm, w), lambda i, cb=cb: (i, cb))

    def whole(a):
        return pl.BlockSpec(a.shape, lambda i: (0, 0))

    d = D_MODEL
    blk = _nbytes((tm, d), F32) * 6 + sum(_nbytes(a.shape, BF16) for a in (wt_pa, wt_pb, w_out))
    return pl.pallas_call(
        body, name="mix_out", grid=(SEQ // tm,),
        in_specs=[rows(DIL_OUT_WIDTH), rows(FOX_WIDTH), rows(d, COL_GA // d), rows(d, COL_GB // d), rows(d),
                  whole(wt_pa), whole(wt_pb), whole(w_out), whole(g_post), whole(g_ffn_pre)],
        out_specs=[rows(d)] * 4,
        out_shape=[pltpu.HBM((SEQ, d), dt) for dt in (BF16, F32, F32, BF16)],
        compiler_params=pltpu.CompilerParams(dimension_semantics=("parallel",), vmem_limit_bytes=_vmem_limit(blk)),
    )(*_in_hbm(out_a, out_b, proj, proj, x, wt_pa, wt_pb, w_out, g_post, g_ffn_pre))


def _mix_out_bwd(dmix, out_a, out_b, proj, wt_pa, wt_pb, w_out, deps=()):
    tm = MIX_TILE

    def body(dm_ref, a_ref, b_ref, ga_ref, gb_ref, wpa_ref, wpb_ref, wo_ref, *rest):
        dproj_ref, dya_ref, dyb_ref, da_ref, db_ref = rest[len(deps):]
        dmerged = _dot_nt(dm_ref[...], wo_ref[...])
        ya = _dot_nn(a_ref[...].astype(BF16), wpa_ref[...])
        yb = _dot_nn(b_ref[...].astype(BF16), wpb_ref[...])
        sa, sb = _sigmoid(ga_ref[...]), _sigmoid(gb_ref[...])
        dproj_ref[:, COL_GA:COL_GA + D_MODEL] = (dmerged * ya * (sa * (1.0 - sa))).astype(BF16)
        dproj_ref[:, COL_GB:COL_GB + D_MODEL] = (dmerged * yb * (sb * (1.0 - sb))).astype(BF16)
        dproj_ref[:, COL_GB + D_MODEL:] = jnp.zeros((tm, COL_QA - COL_GB - D_MODEL), BF16)
        dya = (dmerged * sa).astype(BF16)
        dyb = (dmerged * sb).astype(BF16)
        dya_ref[...] = dya
        dyb_ref[...] = dyb
        da_ref[...] = _dot_nt(dya, wpa_ref[...])
        db_ref[...] = _dot_nt(dyb, wpb_ref[...]).astype(BF16)

    def rows(w, cb=0):
        return pl.BlockSpec((tm, w), lambda i, cb=cb: (i, cb))

    def whole(a):
        return pl.BlockSpec(a.shape, lambda i: (0, 0))

    d = D_MODEL
    blk = _nbytes((tm, d), F32) * 8 + sum(_nbytes(a.shape, BF16) for a in (wt_pa, wt_pb, w_out))
    return pl.pallas_call(
        body, name="mix_out_bwd", grid=(SEQ // tm,),
        in_specs=[rows(d), rows(DIL_OUT_WIDTH), rows(FOX_WIDTH), rows(d, COL_GA // d), rows(d, COL_GB // d),
                  whole(wt_pa), whole(wt_pb), whole(w_out)] + [_ANY] * len(deps),
        out_specs=[rows(COL_QA)] + [rows(d)] * 2 + [rows(DIL_OUT_WIDTH), rows(FOX_WIDTH)],
        out_shape=[pltpu.HBM((SEQ, PROJ_COLS), BF16)] + [pltpu.HBM((SEQ, d), BF16)] * 2
        + [pltpu.HBM((SEQ, DIL_OUT_WIDTH), F32), pltpu.HBM((SEQ, FOX_WIDTH), BF16)],
        compiler_params=pltpu.CompilerParams(dimension_semantics=("parallel",), vmem_limit_bytes=_vmem_limit(blk)),
    )(*_in_hbm(dmix, out_a, out_b, proj, proj, wt_pa, wt_pb, w_out), *deps)


FFN_TM, FFN_TN = 2048, 256


def _ffn_up(h2, wt_gate, wt_up):
    tm, tn = FFN_TM, FFN_TN

    def body(h_ref, wg_ref, wu_ref, gate_ref, up_ref, act_ref):
        for rows in (slice(0, tm // 2), slice(tm // 2, tm)):
            gate = _dot_nt(h_ref[rows, :], wg_ref[...])
            up = _dot_nt(h_ref[rows, :], wu_ref[...])
            gate_ref[rows, :] = gate
            up_ref[rows, :] = up
            act_ref[rows, :] = (gate * _sigmoid(gate) * up).astype(BF16)

    tile = pl.BlockSpec((tm, tn), lambda i, j: (i, j))
    w_spec = pl.BlockSpec((tn, D_MODEL), lambda i, j: (j, 0))
    return pl.pallas_call(
        body, name="ffn_up", grid=(SEQ // tm, D_FF // tn),
        in_specs=[pl.BlockSpec((tm, D_MODEL), lambda i, j: (i, 0)), w_spec, w_spec],
        out_specs=[tile, tile, tile],
        out_shape=[pltpu.HBM((SEQ, D_FF), dt) for dt in (F32, F32, BF16)],
        compiler_params=pltpu.CompilerParams(
            dimension_semantics=("parallel", "parallel"), vmem_limit_bytes=_vmem_limit(8 * 2**20)),
    )(h2, wt_gate, wt_up)


def _ffn_act_bwd(dff, w_down, gate, up):
    tm, tn = FFN_TM, FFN_TN

    def body(d_ref, wd_ref, gate_ref, up_ref, dgate_ref, dup_ref):
        for rows in (slice(0, tm // 2), slice(tm // 2, tm)):
            dact = _dot_nt(d_ref[rows, :], wd_ref[...])
            gate = gate_ref[rows, :]
            sg = _sigmoid(gate)
            dgate_ref[rows, :] = (dact * up_ref[rows, :] * (sg * (1.0 + gate * (1.0 - sg)))).astype(BF16)
            dup_ref[rows, :] = (dact * (gate * sg)).astype(BF16)

    tile = pl.BlockSpec((tm, tn), lambda i, j: (i, j))
    return pl.pallas_call(
        body, name="ffn_act_bwd", grid=(SEQ // tm, D_FF // tn),
        in_specs=[pl.BlockSpec((tm, D_MODEL), lambda i, j: (i, 0)), pl.BlockSpec((tn, D_MODEL), lambda i, j: (j, 0)),
                  tile, tile],
        out_specs=[tile, tile],
        out_shape=[pltpu.HBM((SEQ, D_FF), BF16)] * 2,
        compiler_params=pltpu.CompilerParams(
            dimension_semantics=("parallel", "parallel"), vmem_limit_bytes=_vmem_limit(8 * 2**20)),
    )(dff, w_down, gate, up)


EPILOGUE_TM = 512


def _loss_head(act, w_down, x1, target, g_post):
    def fn(ff, x1, tgt, g):
        r = _rms_scale(ff)
        nrm = ff * r
        err = (x1 + nrm * g) - tgt
        loss = 0.5 * jnp.sum(jnp.mean(err * err, axis=-1, keepdims=True), axis=0, keepdims=True)
        dy = err * (1.0 / D_MODEL)
        u = dy * g
        dff = r * u - ff * (r * r * r) * jnp.mean(u * ff, axis=-1, keepdims=True)
        return dy, dff, jnp.broadcast_to(loss, (1, LANES)), jnp.sum(dy * nrm, axis=0, keepdims=True)

    d = D_MODEL
    return _matmul_rowwise([(act, w_down)], fn, "ffn_down_loss", EPILOGUE_TM, [(x1, d, 0), (target, d, 0)], [g_post],
                           [(d, F32), (d, BF16)], [LANES, d])


def _post_ffn_bwd(dgate, wt_gate, dup, wt_up, x1, dy, mix, g_ffn_pre, g_mix_post, deps=()):
    def fn(dh2, x1, dy, mix, g3, g2):
        dx, dg3 = _rms_bwd(x1, dh2, g3)
        dx1 = dy + dx
        dmix, dg2 = _rms_bwd(mix, dx1, g2)
        return dx1, dmix, dg3, dg2

    d = D_MODEL
    return _matmul_rowwise([(dgate, wt_gate), (dup, wt_up)], fn, "ffn_up_bwd", EPILOGUE_TM,
                           [(x1, d, 0), (dy, d, 0), (mix, d, 0)], [g_ffn_pre, g_mix_post],
                           [(d, F32), (d, BF16)], [d, d], deps=deps)


def _input_bwd(dproj, wt_r, x, dx1, g_pre, deps=()):
    def fn(dh, x, dx1, g):
        dx, dg = _rms_bwd(x, dh, g)
        return dx1 + dx, dg

    d = D_MODEL
    return _matmul_rowwise([(dproj, wt_r)], fn, "in_proj_bwd", EPILOGUE_TM, [(x, d, 0), (dx1, d, 0)], [g_pre],
                           [(d, F32)], [d], deps=deps)


def _adam_math(w, g, m, v):
    m = ADAM_B1 * m + (1.0 - ADAM_B1) * g
    v = ADAM_B2 * v + (1.0 - ADAM_B2) * (g * g)
    m_hat = m / (1.0 - ADAM_B1 ** ADAM_STEP)
    v_hat = v / (1.0 - ADAM_B2 ** ADAM_STEP)
    delta = -ADAM_LR * (m_hat / (jnp.sqrt(v_hat) + ADAM_EPS) + ADAM_WD * w)
    return delta, m, v


def _adam(w, mine, recv, m, v, name):
    r, c = w.shape
    tc = _col_tile(r, c)

    def body(w_ref, p_ref, r_ref, m_ref, v_ref, g_ref, d_ref, nm_ref, nv_ref):
        g = ((p_ref[...] + r_ref[0].astype(F32)) + r_ref[1].astype(F32)) + r_ref[2].astype(F32)
        g_ref[...] = g
        d_ref[...], nm_ref[...], nv_ref[...] = _adam_math(w_ref[...], g, m_ref[...], v_ref[...])

    spec = pl.BlockSpec((r, tc), lambda j: (0, j))
    return pl.pallas_call(
        body, name=name, grid=(c // tc,),
        in_specs=[spec, spec, pl.BlockSpec((3, r, tc), lambda j: (0, 0, j)), spec, spec], out_specs=[spec] * 4,
        out_shape=[pltpu.HBM((r, c), F32)] * 4,
        compiler_params=pltpu.CompilerParams(dimension_semantics=("parallel",)),
    )(*_in_hbm(w, mine, recv, m, v))


def _adam_small(gathered, ws, ms, vs, loss_parts):
    n = len(ws)

    def body(*refs):
        outs = refs[4 * n + 1:]
        loss = refs[4 * n][0]
        for dev in range(1, N_DEV):
            loss = loss + refs[4 * n][dev]
        outs[4 * n][...] = loss
        for i in range(n):
            ga_ref, w_ref, m_ref, v_ref = (refs[j * n + i] for j in range(4))
            g = ga_ref[0]
            for dev in range(1, N_DEV):
                g = g + ga_ref[dev]
            g = g[:, :w_ref.shape[1]]
            outs[4 * i][...] = g
            outs[4 * i + 1][...], outs[4 * i + 2][...], outs[4 * i + 3][...] = _adam_math(
                w_ref[...], g, m_ref[...], v_ref[...])

    out_shape = [pltpu.HBM(w.shape, F32) for w in ws for _ in range(4)]
    out_shape.append(pltpu.HBM((1, LANES), F32))
    out = pl.pallas_call(body, name="adam_small", out_shape=out_shape)(*gathered, *ws, *ms, *vs, loss_parts)
    return [out[4 * i:4 * i + 4] for i in range(n)], out[4 * n]


_PROJ_SEGMENTS = ((3848, 5896), (None, COL_QA - 2 * D_MODEL), (0, 3840), (3840, 3848), (None, PROJ_COLS - COL_F - 8))


def _proj_weight_t(gathered):
    w = gathered.reshape(IN_COLS, D_MODEL)
    return jnp.concatenate([jnp.zeros((hi, D_MODEL), w.dtype) if lo is None else w[lo:hi] for lo, hi in _PROJ_SEGMENTS],
                           axis=0)


def _proj_weight_grad_slots(dwt_r):
    starts, at = [], 0
    for lo, hi in _PROJ_SEGMENTS:
        if lo is not None:
            starts.append((lo, hi, at))
        at += hi if lo is None else hi - lo
    slots = []
    for dev in range(N_DEV):
        pieces, lo, end = [], dev * IN_SHARD, (dev + 1) * IN_SHARD
        for seg_lo, seg_hi, seg_at in sorted(starts):
            a, b = max(lo, seg_lo), min(end, seg_hi)
            if a < b:
                pieces.append(dwt_r[seg_at + a - seg_lo:seg_at + b - seg_lo])
        slots.append(pieces[0] if len(pieces) == 1 else jnp.concatenate(pieces, axis=0))
    return jnp.stack(slots)


def kernel(x, w_in, w_proj_a, w_proj_b, w_out, b_forget, w_ffn_gate, w_ffn_up, w_ffn_down, norm_mix_pre, norm_mix_post, norm_ffn_pre, norm_ffn_post, loss_target, m_w_in, m_w_proj_a, m_w_proj_b, m_w_out, m_b_forget, m_w_ffn_gate, m_w_ffn_up, m_w_ffn_down, m_norm_mix_pre, m_norm_mix_post, m_norm_ffn_pre, m_norm_ffn_post, v_w_in, v_w_proj_a, v_w_proj_b, v_w_out, v_b_forget, v_w_ffn_gate, v_w_ffn_up, v_w_ffn_down, v_norm_mix_pre, v_norm_mix_post, v_norm_ffn_pre, v_norm_ffn_post):
    d = D_MODEL
    names = ("w_in", "w_proj_a", "w_proj_b", "w_out", "w_ffn_gate", "w_ffn_up", "w_ffn_down")
    col_sharded = ("w_in", "w_ffn_gate", "w_ffn_up")

    def row_shards(arrs):
        return {k: (a[0].T if k in col_sharded else a[0]) for k, a in zip(names, arrs)}

    shards = row_shards((w_in, w_proj_a, w_proj_b, w_out, w_ffn_gate, w_ffn_up, w_ffn_down))
    moments_m = row_shards((m_w_in, m_w_proj_a, m_w_proj_b, m_w_out, m_w_ffn_gate, m_w_ffn_up, m_w_ffn_down))
    moments_v = row_shards((v_w_in, v_w_proj_a, v_w_proj_b, v_w_out, v_w_ffn_gate, v_w_ffn_up, v_w_ffn_down))
    pos = jnp.stack([lax.axis_index("c"), 2 * lax.axis_index("x") + lax.axis_index("y")]).astype(jnp.int32)
    x2, target = x[0], loss_target[0]

    me = 4 * lax.axis_index("x") + 2 * lax.axis_index("y") + lax.axis_index("c")
    mid_names, ffn_names = names[1:4], names[4:]
    first_names, later_names = names[:1], names[1:]
    shards16 = {k: shards[k].astype(BF16) for k in names}

    def landing(k):
        return lax.dynamic_update_slice(lax.empty((N_DEV,) + shards[k].shape, BF16), shards16[k][None], (me, 0, 0))

    ag_first = _exchange_start("ag_first_chips_start", _gather_chips_copies, [shards16[k] for k in first_names],
                               [landing(k) for k in first_names], 3 * len(first_names))
    h = _rowwise(lambda xb, g: xb * _rms_scale(xb) * g, "norm_mix_pre", SEQ, 256, [(x2, d, 0)], [norm_mix_pre],
                 [(d, BF16)], deps=[ag_first.token])[0]
    later_lands = [landing(k) for k in later_names]
    _, lands = _exchange_wait("ag_first_chips_wait", ag_first,
                              [h, shards["w_in"], moments_m["w_in"], moments_v["w_in"], *later_lands,
                               *[shards16[k] for k in later_names]])
    ag_first = _exchange_start("ag_first_sibling_start", _gather_sibling_copies, [], lands, 4 * len(first_names))
    ag_later = _exchange_start("ag_later_chips_start", _gather_chips_copies, [shards16[k] for k in later_names],
                               later_lands, 3 * len(later_names), after=[ag_first.token])
    gathered = dict(zip(first_names, _exchange_wait("ag_first_sibling_wait", ag_first, [ag_later.token])[1]))
    wt_r = _proj_weight_t(gathered["w_in"])

    proj = _matmul([(h, *_in_hbm(wt_r))], "nt", F32, "in_proj", 1024, 896, 1024)
    tables = _rope_tables()
    o_dil, lse_dil = _dil_fwd(proj, tables)
    out_a = _dil_combine(o_dil, lse_dil)
    _, lands = _exchange_wait("ag_later_chips_wait", ag_later, [out_a])
    ag_later = _exchange_start("ag_later_sibling_start", _gather_sibling_copies, [], lands, 4 * len(later_names))

    b_pad = jnp.pad(b_forget, ((0, 0), (0, LANES - N_FOX_HEADS)))
    f_rows = _fox_gate(proj, b_pad, deps=[ag_later.token])
    out_b, lse_fox = _fox_fwd(proj, f_rows)

    gathered = dict(zip(later_names, _exchange_wait("ag_later_sibling_wait", ag_later, [out_b])[1]))
    wt_pa = gathered["w_proj_a"].transpose(1, 0, 2).reshape(DIL_OUT_WIDTH, d)
    wt_pb = gathered["w_proj_b"].transpose(1, 0, 2).reshape(FOX_WIDTH, d)
    w_o = gathered["w_out"].reshape(d, d)
    wt_g = gathered["w_ffn_gate"].reshape(D_FF, d)
    wt_u = gathered["w_ffn_up"].reshape(D_FF, d)
    w_d = gathered["w_ffn_down"].reshape(D_FF, d)
    merged, mix, x1, h2 = _mix_out(out_a, out_b, proj, x2, wt_pa, wt_pb, w_o, norm_mix_post, norm_ffn_pre)

    gate, up, act = _ffn_up(h2, wt_g, wt_u)
    dy, dff, loss_part, dg_ffn_post = _loss_head(act, w_d, x1, target, norm_ffn_post)

    dgate, dup = _ffn_act_bwd(dff, w_d, gate, up)
    grads_t = {}
    grads_t["w_ffn_down"] = _matmul([(act, dff)], "tn", F32, "grad_w_ffn_down", 1408, 512, 2048, staged=False)
    grads_t["w_ffn_gate"] = _matmul([(dgate, h2)], "tn", F32, "grad_w_ffn_gate", 1408, 512, 2048)
    grads_t["w_ffn_up"] = _matmul([(dup, h2)], "tn", F32, "grad_w_ffn_up", 1408, 512, 2048)
    rs_ffn = _ReduceScatter("ffn", {k: grads_t[k] for k in ffn_names}, pos)
    dx1, dmix, dg_ffn_pre, dg_mix_post = _post_ffn_bwd(dgate, wt_g, dup, wt_u, x1, dy, mix, norm_ffn_pre, norm_mix_post,
                                                       deps=[rs_ffn.token])
    rs_ffn.start_chips([dmix])

    dproj, dya, dyb, d_out_a, d_out_b = _mix_out_bwd(dmix, out_a, out_b, proj, wt_pa, wt_pb, w_o, deps=[rs_ffn.token])
    grads_t["w_out"] = _matmul([(merged, dmix)], "tn", F32, "grad_w_out", 1024, 1024, 1024)
    def column_slots(g):
        return g.reshape(g.shape[0], N_DEV, LANES).transpose(1, 0, 2)

    grads_t["w_proj_a"] = column_slots(_matmul([(out_a, dya)], "tn", F32, "grad_w_proj_a", DIL_OUT_WIDTH, 1024, SEQ))
    grads_t["w_proj_b"] = column_slots(_matmul([(out_b, dyb)], "tn", F32, "grad_w_proj_b", FOX_WIDTH, 1024, SEQ))
    rs_mid = _ReduceScatter("mid", {k: grads_t[k] for k in mid_names}, pos)

    do_dil, c_dil = _dil_combine_bwd(d_out_a, o_dil, lse_dil, deps=[rs_mid.token])
    rs_mid.start_chips([c_dil])
    dproj, d_cum = _fox_bwd(proj, d_out_b, lse_fox, f_rows, dproj)
    d_cum_rows = jnp.pad(d_cum[:, :2].reshape(N_FOX_HEADS, SEQ), ((0, F_ROWS - N_FOX_HEADS), (0, 0)))
    dproj, db_part = _fox_gate_bwd(d_cum_rows, proj, b_pad, dproj)
    dproj = _dil_bwd(proj, tables, do_dil, lse_dil, c_dil, dproj, deps=[rs_mid.token])

    dwt_r = _matmul([(dproj, h)], "tn", F32, "grad_w_in", 896, 1024, 2048)
    rs_in = _ReduceScatter("in", {"w_in": _proj_weight_grad_slots(dwt_r)}, pos)
    def finish(rs, after):
        return {k: _adam(shards[k], mine, recv, moments_m[k], moments_v[k], "adam_" + k)
                for k, (mine, recv) in rs.finish(after).items()}

    done = finish(rs_ffn, [rs_in.token])
    rs_in.start_chips([done[k][0] for k in ffn_names])
    grad_x, dg_mix_pre = _input_bwd(dproj, wt_r, x2, dx1, norm_mix_pre, deps=[rs_in.token])
    done.update(finish(rs_mid, [grad_x]))

    small_all = _all_gather([dg_mix_pre, dg_mix_post, dg_ffn_pre, dg_ffn_post, db_part, loss_part],
                            "small_grads_all_gather", deps=[done[k][0] for k in mid_names])
    small, loss = _adam_small(small_all[:5], [norm_mix_pre, norm_mix_post, norm_ffn_pre, norm_ffn_post, b_forget],
                              [m_norm_mix_pre, m_norm_mix_post, m_norm_ffn_pre, m_norm_ffn_post, m_b_forget],
                              [v_norm_mix_pre, v_norm_mix_post, v_norm_ffn_pre, v_norm_ffn_post, v_b_forget],
                              small_all[5])

    done.update(finish(rs_in, [small[0][0]]))

    def leaves(i):
        def nat(k):
            a = done[k][i]
            return (a.T if k in col_sharded else a)[None]

        return [nat("w_in"), nat("w_proj_a"), nat("w_proj_b"), nat("w_out"), small[4][i],
                nat("w_ffn_gate"), nat("w_ffn_up"), nat("w_ffn_down"), *[small[r][i] for r in range(4)]]

    return (loss[0, 0], grad_x[None], *leaves(0), *leaves(1), *leaves(2), *leaves(3))
```

```python
import functools
import math

import jax
import jax.numpy as jnp
import numpy as np
from jax import lax
from jax.experimental import pallas as pl
from jax.experimental.pallas import tpu as pltpu

F32 = jnp.float32
BF16 = jnp.bfloat16
MESH = pl.DeviceIdType.MESH

D_MODEL = 1024
SEQ = 2048
HEAD_DIM = 64
BLOCK = 128
N_BLOCKS = SEQ // BLOCK
DILATIONS = (1, 4, 16)
N_FOX_HEADS = 8
DIL_WIDTH = 768
DIL_OUT_WIDTH = 256
FOX_WIDTH = 512
D_FF = 2816
ROPE_THETA = 500000.0
ROPE_DIM = HEAD_DIM // 4
ROPE_HALF = ROPE_DIM // 2
EPS = 1e-6
NEG_INF = -1e30
QK_SCALE = 1.0 / math.sqrt(HEAD_DIM)
IN_COLS = 5896
N_DEV = 8
IN_SHARD = IN_COLS // N_DEV

ADAM_LR = 0.001
ADAM_B1 = 0.9
ADAM_B2 = 0.999
ADAM_EPS = 1e-08
ADAM_WD = 0.01
ADAM_STEP = 10

V7X_VMEM_BYTES = 64 * 2**20
LANES = 128
SUBLANES = 8

PROJ_COLS = 6272
COL_GA, COL_GB = 0, 1024
COL_QA, COL_KA, COL_VA = 2304, 3072, 3840
COL_QB, COL_KB, COL_VB = 4608, 5120, 5632
COL_F = 6144
F_ROWS = 16


def _vmem_limit(block_bytes):
    want = 2 * block_bytes + 16 * 2**20
    return int(min(max(want, 32 * 2**20), V7X_VMEM_BYTES - 8 * 2**20))


def _nbytes(shape, dtype):
    return math.prod(shape) * jnp.dtype(dtype).itemsize


def _in_hbm(*arrays):
    return [pltpu.with_memory_space_constraint(a, pltpu.HBM) for a in arrays]


def _dot(a, b, dims):
    return lax.dot_general(a, b, (dims, ((), ())), preferred_element_type=F32)


def _dot_nn(a, b):
    return _dot(a, b, ((1,), (0,)))


def _dot_nt(a, b):
    return _dot(a, b, ((1,), (1,)))


def _dot_tn(a, b):
    return _dot(a, b, ((0,), (0,)))


def _sigmoid(z):
    return 1.0 / (1.0 + jnp.exp(-z))


def _split3(x):
    hi = x.astype(BF16)
    r1 = x - hi.astype(F32)
    mid = r1.astype(BF16)
    lo = (r1 - mid.astype(F32)).astype(BF16)
    return hi, mid, lo


def _dot3_nn(x, ones_matrix):
    hi, mid, lo = _split3(x)
    return (_dot_nn(hi, ones_matrix) + _dot_nn(mid, ones_matrix)) + _dot_nn(lo, ones_matrix)


def _rowwise(fn, name, n_rows, tm, row_ins, bcast_ins, row_outs, acc_outs=(), deps=()):
    n_in = len(row_ins) + len(bcast_ins)
    n_ro = len(row_outs)

    def body(*refs):
        res = fn(*[r[...] for r in refs[:n_in]])
        if not isinstance(res, (tuple, list)):
            res = (res,)
        outs = refs[n_in + len(deps):]
        for r, o in zip(res[:n_ro], outs[:n_ro]):
            o[...] = r.astype(o.dtype)
        first = pl.program_id(0) == 0
        for r, o in zip(res[n_ro:], outs[n_ro:]):
            _accumulate(o, r, first)

    in_specs = [pl.BlockSpec((tm, w), lambda i, cb=cb: (i, cb)) for _, w, cb in row_ins]
    in_specs += [pl.BlockSpec(a.shape, lambda i: (0, 0)) for a in bcast_ins]
    in_specs += [pl.BlockSpec(memory_space=pl.ANY)] * len(deps)
    out_specs = [pl.BlockSpec((tm, w), lambda i: (i, 0)) for w, _ in row_outs]
    out_specs += [pl.BlockSpec((1, w), lambda i: (0, 0)) for w in acc_outs]
    out_shape = [pltpu.HBM((n_rows, w), dt) for w, dt in row_outs]
    out_shape += [pltpu.HBM((1, w), F32) for w in acc_outs]
    blk = sum(_nbytes((tm, w), a.dtype) for a, w, _ in row_ins) + sum(_nbytes((tm, w), dt) for w, dt in row_outs)
    return pl.pallas_call(
        body, name=name, grid=(n_rows // tm,), in_specs=in_specs, out_specs=out_specs, out_shape=out_shape,
        compiler_params=pltpu.CompilerParams(
            dimension_semantics=("arbitrary" if acc_outs else "parallel",), vmem_limit_bytes=_vmem_limit(3 * blk)),
    )(*_in_hbm(*[a for a, _, _ in row_ins], *bcast_ins), *deps)


def _accumulate(o_ref, part, first):
    @pl.when(first)
    def _():
        o_ref[...] = part

    @pl.when(jnp.logical_not(first))
    def _():
        o_ref[...] += part


_MM_DIMS = {"nn": ((1,), (0,)), "nt": ((1,), (1,)), "tn": ((0,), (0,))}


def _matmul(pairs, mode, out_dtype, name, tm, tn, tk, deps=(), staged=True):
    a0, b0 = pairs[0]
    if mode == "tn":
        kk, m = a0.shape
    else:
        m, kk = a0.shape
    n = b0.shape[0] if mode == "nt" else b0.shape[1]
    assert m % tm == 0 and n % tn == 0 and kk % tk == 0, (name, m, n, kk)
    nk = kk // tk
    n_pairs = len(pairs)
    dims = _MM_DIMS[mode]
    n_in = 2 * n_pairs + len(deps)

    def body(*refs):
        o_ref = refs[n_in]
        part = None
        for p in range(n_pairs):
            d = _dot(refs[2 * p][...].astype(BF16), refs[2 * p + 1][...].astype(BF16), dims)
            part = d if part is None else part + d
        if nk == 1:
            o_ref[...] = part.astype(o_ref.dtype)
            return
        acc = refs[n_in + 1]
        k = pl.program_id(2)

        @pl.when(k == 0)
        def _():
            acc[...] = part

        @pl.when(k > 0)
        def _():
            acc[...] += part

        @pl.when(k == nk - 1)
        def _():
            o_ref[...] = acc[...].astype(o_ref.dtype)

    if mode == "tn":
        a_spec = pl.BlockSpec((tk, tm), lambda i, j, k: (k, i))
    else:
        a_spec = pl.BlockSpec((tm, tk), lambda i, j, k: (i, k))
    if mode == "nt":
        b_spec = pl.BlockSpec((tn, tk), lambda i, j, k: (j, k))
    else:
        b_spec = pl.BlockSpec((tk, tn), lambda i, j, k: (k, j))
    blk = sum(_nbytes((tm, tk), a.dtype) + _nbytes((tk, tn), b.dtype) for a, b in pairs) + 2 * _nbytes((tm, tn), F32)
    flat = [a for pair in pairs for a in pair]
    return pl.pallas_call(
        body, name=name, grid=(m // tm, n // tn, nk),
        in_specs=[a_spec, b_spec] * n_pairs + [pl.BlockSpec(memory_space=pl.ANY)] * len(deps),
        out_specs=pl.BlockSpec((tm, tn), lambda i, j, k: (i, j)),
        out_shape=pltpu.HBM((m, n), out_dtype),
        scratch_shapes=[] if nk == 1 else [pltpu.VMEM((tm, tn), F32)],
        compiler_params=pltpu.CompilerParams(
            dimension_semantics=("parallel", "parallel", "arbitrary"), vmem_limit_bytes=_vmem_limit(blk)),
    )(*(flat if staged else _in_hbm(*flat)), *deps)


def _matmul_rowwise(pairs, fn, name, tm, row_ins, bcast_ins, row_outs, acc_outs=(), deps=()):
    m = pairs[0][0].shape[0]
    n_mm, n_in = 2 * len(pairs), len(row_ins) + len(bcast_ins)
    n_ro = len(row_outs)

    def body(*refs):
        prod = None
        for p in range(len(pairs)):
            part = _dot_nn(refs[2 * p][...].astype(BF16), refs[2 * p + 1][...].astype(BF16))
            prod = part if prod is None else prod + part
        res = fn(prod, *[r[...] for r in refs[n_mm:n_mm + n_in]])
        outs = refs[n_mm + n_in + len(deps):]
        for r, o in zip(res[:n_ro], outs[:n_ro]):
            o[...] = r.astype(o.dtype)
        first = pl.program_id(0) == 0
        for r, o in zip(res[n_ro:], outs[n_ro:]):
            _accumulate(o, r, first)

    in_specs = []
    for a, b in pairs:
        in_specs += [pl.BlockSpec((tm, a.shape[1]), lambda i: (i, 0)),
                     pl.BlockSpec(b.shape, lambda i: (0, 0), pipeline_mode=pl.Buffered(1))]
    in_specs += [pl.BlockSpec((tm, w), lambda i, cb=cb: (i, cb)) for _, w, cb in row_ins]
    in_specs += [pl.BlockSpec(a.shape, lambda i: (0, 0)) for a in bcast_ins]
    in_specs += [_ANY] * len(deps)
    out_specs = [pl.BlockSpec((tm, w), lambda i: (i, 0)) for w, _ in row_outs]
    out_specs += [pl.BlockSpec((1, w), lambda i: (0, 0)) for w in acc_outs]
    out_shape = [pltpu.HBM((m, w), dt) for w, dt in row_outs]
    out_shape += [pltpu.HBM((1, w), F32) for w in acc_outs]
    blk = sum(_nbytes((tm, a.shape[1]), a.dtype) + _nbytes(b.shape, b.dtype) // 2 for a, b in pairs)
    blk += sum(_nbytes((tm, w), a.dtype) for a, w, _ in row_ins) + sum(_nbytes((tm, w), dt) for w, dt in row_outs)
    return pl.pallas_call(
        body, name=name, grid=(m // tm,), in_specs=in_specs, out_specs=out_specs, out_shape=out_shape,
        compiler_params=pltpu.CompilerParams(dimension_semantics=("arbitrary",), vmem_limit_bytes=_vmem_limit(blk)),
    )(*[a for pair in pairs for a in pair], *[a for a, _, _ in row_ins], *bcast_ins, *deps)


def _rms_scale(x):
    return lax.rsqrt(jnp.mean(x * x, axis=-1, keepdims=True) + EPS)


def _rms_bwd(xin, dyn, g):
    r = _rms_scale(xin)
    u = dyn * g
    dx = r * u - xin * (r * r * r) * jnp.mean(u * xin, axis=-1, keepdims=True)
    dg = jnp.sum(dyn * xin * r, axis=0, keepdims=True)
    return dx, dg


def _mesh_pos():
    return lax.axis_index("x"), lax.axis_index("y"), lax.axis_index("c")


def _all_gather(xs, name, deps=()):
    n = len(xs)

    def body(*refs):
        x_refs, out_refs = refs[:n], refs[n + len(deps):2 * n + len(deps)]
        send_sems, recv_sems, local_sems = refs[2 * n + len(deps):]
        mx, my, mc = _mesh_pos()
        me, sib = (mx, my, mc), (mx, my, 1 - mc)
        chips = [(1 - mx, my), (mx, 1 - my), (1 - mx, 1 - my)]

        def slot(a, dev):
            px, py, pc = dev
            return out_refs[a].at[4 * px + 2 * py + pc]

        def copy(k, a, block, to, src=None):
            return pltpu.make_async_remote_copy(
                src_ref=slot(a, block) if src is None else src, dst_ref=slot(a, block),
                send_sem=send_sems.at[a * 7 + k], recv_sem=recv_sems.at[a * 7 + k],
                device_id=to, device_id_type=MESH)

        mine = [pltpu.make_async_copy(x_refs[a], slot(a, me), local_sems.at[a]) for a in range(n)]
        for cp in mine:
            cp.start()
        first = []
        for a in range(n):
            first.append(copy(0, a, me, sib, x_refs[a]))
            first += [copy(1 + j, a, me, (*chip, mc), x_refs[a]) for j, chip in enumerate(chips)]
        for cp in first:
            cp.start()
        passed = []
        for a in range(n):
            for j, chip in enumerate(chips):
                copy(1 + j, a, (*chip, mc), me).wait_recv()
                fwd = copy(4 + j, a, (*chip, mc), sib)
                fwd.start()
                passed.append(fwd)
        for a in range(n):
            copy(0, a, sib, me).wait_recv()
            for j, chip in enumerate(chips):
                copy(4 + j, a, (*chip, 1 - mc), me).wait_recv()
        for cp in first + passed:
            cp.wait_send()
        for cp in mine:
            cp.wait()

    hbm = pl.BlockSpec(memory_space=pl.ANY)
    return pl.pallas_call(
        body, name=name,
        out_shape=[pltpu.HBM((N_DEV,) + x.shape, x.dtype) for x in xs],
        in_specs=[hbm] * (n + len(deps)), out_specs=[hbm] * n,
        scratch_shapes=[pltpu.SemaphoreType.DMA((7 * n,)), pltpu.SemaphoreType.DMA((7 * n,)),
                        pltpu.SemaphoreType.DMA((n,))],
    )(*xs, *deps)


_HBM = pl.BlockSpec(memory_space=pltpu.HBM)
_SEM = pl.BlockSpec(memory_space=pltpu.SEMAPHORE)
_ANY = pl.BlockSpec(memory_space=pl.ANY)
_DATAFLOW = pltpu.SideEffectType.DATAFLOW_SIDE_EFFECTING


def _flip_peer(flip):
    mx, my, mc = _mesh_pos()
    return (1 - mx if flip & 2 else mx, 1 - my if flip & 1 else my, mc)


def _remote(src, dst, send_sems, recv_sems, k, peer):
    return pltpu.make_async_remote_copy(src_ref=src, dst_ref=dst, send_sem=send_sems.at[k], recv_sem=recv_sems.at[k],
                                        device_id=peer, device_id_type=MESH)


def _gather_chips_copies(srcs, lands, send_sems, recv_sems):
    mx, my, mc = _mesh_pos()
    me = 4 * mx + 2 * my + mc
    return [_remote(srcs[a], lands[a].at[me], send_sems, recv_sems, 3 * a + flip - 1, _flip_peer(flip))
            for a in range(len(srcs)) for flip in (1, 2, 3)]


def _gather_sibling_copies(srcs, lands, send_sems, recv_sems):
    mx, my, mc = _mesh_pos()
    return [_remote(lands[a].at[2 * k + mc], lands[a].at[2 * k + mc], send_sems, recv_sems, 4 * a + k, (mx, my, 1 - mc))
            for a in range(len(lands)) for k in range(4)]


def _scatter_sibling_copies(srcs, lands, send_sems, recv_sems):
    mx, my, mc = _mesh_pos()
    return [_remote(srcs[a].at[k, 1 - mc], lands[a].at[k], send_sems, recv_sems, 4 * a + k, (mx, my, 1 - mc))
            for a in range(len(srcs)) for k in range(4)]


def _scatter_chips_copies(srcs, lands, send_sems, recv_sems):
    mx, my, _ = _mesh_pos()
    k0 = 2 * mx + my
    return [_remote(srcs[a].at[jnp.bitwise_xor(k0, flip)], lands[a].at[flip - 1], send_sems, recv_sems,
                    3 * a + flip - 1, _flip_peer(flip))
            for a in range(len(srcs)) for flip in (1, 2, 3)]


class _Exchange:
    def __init__(self, copies, n_src, send_sems, recv_sems, thru, token):
        self.copies, self.n_src, self.send_sems, self.recv_sems, self.thru, self.token = (
            copies, n_src, send_sems, recv_sems, thru, token)


def _exchange_start(name, copies, srcs, lands, n_copies, after=()):
    bufs = list(srcs) + list(lands)
    nb, ns = len(bufs), len(srcs)

    def body(*refs):
        send_sems, recv_sems = refs[nb + len(after)], refs[nb + len(after) + 1]
        for cp in copies(refs[:ns], refs[ns:nb], send_sems, recv_sems):
            cp.start()
        refs[-1][...] = jnp.zeros_like(refs[-1])

    out = pl.pallas_call(
        body, name=name,
        out_shape=(pltpu.SemaphoreType.DMA((n_copies,)), pltpu.SemaphoreType.DMA((n_copies,)),
                   *[pltpu.HBM(b.shape, b.dtype) for b in bufs], pltpu.HBM((SUBLANES, LANES), F32)),
        in_specs=[_HBM] * nb + [_ANY] * len(after),
        out_specs=(_SEM, _SEM, *[_HBM] * nb, pl.BlockSpec(memory_space=pltpu.VMEM)),
        input_output_aliases={i: 2 + i for i in range(nb)},
        compiler_params=pltpu.CompilerParams(has_side_effects=_DATAFLOW),
    )(*[pltpu.with_memory_space_constraint(b, pltpu.HBM) for b in bufs], *after)
    return _Exchange(copies, ns, out[0], out[1], list(out[2:2 + nb]), out[-1])


def _exchange_wait(name, ex, after):
    nb, ns = len(ex.thru), ex.n_src

    def body(*refs):
        for cp in ex.copies(refs[:ns], refs[ns:nb], refs[nb], refs[nb + 1]):
            cp.wait_send()
            cp.wait_recv()

    out = pl.pallas_call(
        body, name=name, out_shape=tuple(pltpu.HBM(b.shape, b.dtype) for b in ex.thru),
        in_specs=[_HBM] * nb + [_SEM, _SEM] + [_ANY] * len(after), out_specs=tuple([_HBM] * nb),
        input_output_aliases={i: i for i in range(nb)},
        compiler_params=pltpu.CompilerParams(has_side_effects=_DATAFLOW),
    )(*ex.thru, ex.send_sems, ex.recv_sems, *after)
    return list(out[:ns]), list(out[ns:])


def _gather_relay(name, ex, after):
    nb, ns = len(ex.thru), ex.n_src
    n = nb - ns

    def body(*refs):
        lands, send_sems, recv_sems = refs[ns:nb], refs[nb + 2 + len(after)], refs[nb + 3 + len(after)]
        chips = ex.copies(refs[:ns], lands, refs[nb], refs[nb + 1])
        mx, my, mc = _mesh_pos()
        k0 = 2 * mx + my

        def to_sibling(a, k):
            return _remote(lands[a].at[2 * k + mc], lands[a].at[2 * k + mc], send_sems, recv_sems, 4 * a + k, (mx, my, 1 - mc))

        for a in range(n):
            to_sibling(a, k0).start()
        for flip in (1, 2, 3):
            for a in range(n):
                chips[3 * a + flip - 1].wait_recv()
                to_sibling(a, jnp.bitwise_xor(k0, flip)).start()
        for cp in chips:
            cp.wait_send()
        refs[-1][...] = jnp.zeros_like(refs[-1])

    out = pl.pallas_call(
        body, name=name,
        out_shape=(pltpu.SemaphoreType.DMA((4 * n,)), pltpu.SemaphoreType.DMA((4 * n,)),
                   *[pltpu.HBM(b.shape, b.dtype) for b in ex.thru], pltpu.HBM((SUBLANES, LANES), F32)),
        in_specs=[_HBM] * nb + [_SEM, _SEM] + [_ANY] * len(after),
        out_specs=(_SEM, _SEM, *[_HBM] * nb, pl.BlockSpec(memory_space=pltpu.VMEM)),
        input_output_aliases={i: 2 + i for i in range(nb)},
        compiler_params=pltpu.CompilerParams(has_side_effects=_DATAFLOW),
    )(*ex.thru, ex.send_sems, ex.recv_sems, *after)
    return _Exchange(_gather_sibling_copies, 0, out[0], out[1], list(out[2 + ns:2 + nb]), out[-1])


def _col_tile(r, c):
    return next(t for t in (1024, 512, 256, 128) if c % t == 0 and (r * t * 4 <= 2**20 or t == 128))


def _add_sibling(g4, recv, pos, name):
    _, _, r, c = g4.shape
    tc = _col_tile(r, c)

    def body(pos_ref, g_ref, r_ref, o16_ref, mine_ref):
        s = g_ref[0, 0] + r_ref[0]
        o16_ref[0] = s.astype(BF16)

        @pl.when(pl.program_id(1) == pos_ref[1])
        def _():
            mine_ref[...] = s

    slot = pl.BlockSpec((1, r, tc), lambda j, k, pos_ref: (k, 0, j))
    return pl.pallas_call(
        body, name=name,
        out_shape=[pltpu.HBM((4, r, c), BF16), pltpu.HBM((r, c), F32)],
        grid_spec=pltpu.PrefetchScalarGridSpec(
            num_scalar_prefetch=1, grid=(c // tc, 4),
            in_specs=[pl.BlockSpec((1, 1, r, tc), lambda j, k, pos_ref: (k, pos_ref[0], 0, j)), slot],
            out_specs=[slot, pl.BlockSpec((r, tc), lambda j, k, pos_ref: (0, j))]),
        compiler_params=pltpu.CompilerParams(dimension_semantics=("parallel", "arbitrary")),
    )(pos, *_in_hbm(g4, recv))


class _ReduceScatter:
    def __init__(self, tag, grads_t, pos):
        self.tag, self.pos, self.names = tag, pos, list(grads_t)
        g4s = [g.reshape(4, 2, g.size // (N_DEV * g.shape[-1]), g.shape[-1]) for g in grads_t.values()]
        lands = [lax.empty((4,) + g.shape[2:], F32) for g in g4s]
        self.ex = _exchange_start(f"rs_{tag}_sibling_start", _scatter_sibling_copies, g4s, lands, 4 * len(g4s))
        self.token = self.ex.token

    def start_chips(self, after):
        g4s, from_sibling = _exchange_wait(f"rs_{self.tag}_sibling_wait", self.ex, after)
        parts = [_add_sibling(g4, rv, self.pos, f"rs_add_sibling_{k}")
                 for k, g4, rv in zip(self.names, g4s, from_sibling)]
        self.mine = [mine for _, mine in parts]
        p16s = [p16 for p16, _ in parts]
        lands = [lax.empty((3,) + p.shape[1:], BF16) for p in p16s]
        self.ex = _exchange_start(f"rs_{self.tag}_chips_start", _scatter_chips_copies, p16s, lands, 3 * len(p16s))
        self.token = self.ex.token

    def finish(self, after):
        _, from_chips = _exchange_wait(f"rs_{self.tag}_chips_wait", self.ex, after)
        return dict(zip(self.names, zip(self.mine, from_chips)))


def _rope_tables():
    positions = np.arange(SEQ, dtype=np.float32)
    inv_freq = np.power(np.float32(ROPE_THETA), -np.arange(0, ROPE_DIM, 2, dtype=np.float32) / np.float32(ROPE_DIM))
    ang = (positions[:, None] * inv_freq[None, :]).astype(np.float32)
    cos, sin = np.cos(ang).astype(np.float32), np.sin(ang).astype(np.float32)
    ones = np.ones((SEQ, HEAD_DIM - ROPE_DIM), np.float32)
    zeros8 = np.zeros((SEQ, ROPE_HALF), np.float32)
    zeros = np.zeros((SEQ, HEAD_DIM - ROPE_DIM), np.float32)
    c_head = np.concatenate([cos, cos, ones], axis=1)
    s1_head = np.concatenate([-sin, zeros8, zeros], axis=1)
    s2_head = np.concatenate([zeros8, sin, zeros], axis=1)
    return tuple(jnp.asarray(np.concatenate([t, t], axis=1)) for t in (c_head, s1_head, s2_head))


def _rope_apply(x, c, s1, s2):
    w = x.shape[1]
    return x * c + pltpu.roll(x, w - ROPE_HALF, 1) * s1 + pltpu.roll(x, ROPE_HALF, 1) * s2


def _rope_apply_t(dy, c, s1, s2):
    w = dy.shape[1]
    return dy * c + pltpu.roll(dy * s1, ROPE_HALF, 1) + pltpu.roll(dy * s2, w - ROPE_HALF, 1)


def _dil_prev_limit(has_prev):
    return jnp.where(has_prev, 0, BLOCK)


def _dil_valid(limit):
    row = lax.broadcasted_iota(jnp.int32, (BLOCK, 2 * BLOCK), 0)
    col = lax.broadcasted_iota(jnp.int32, (BLOCK, 2 * BLOCK), 1)
    dist = col - row
    return jnp.logical_and(dist >= jnp.where(col < BLOCK, limit, -BLOCK), dist <= BLOCK)


def _upper_half():
    return lax.broadcasted_iota(jnp.int32, (1, LANES), 1) >= HEAD_DIM


def _stack_heads(x):
    upper = _upper_half()
    return jnp.concatenate([jnp.where(upper, 0, x), jnp.where(upper, x, 0)], axis=0)


def _unstack_heads(y):
    n = y.shape[0] // 2
    return jnp.where(_upper_half(), y[n:], y[:n])


def _head_columns(t):
    return jnp.concatenate([t[:, 0:1], t[:, HEAD_DIM:HEAD_DIM + 1]], axis=0)


def _dil_rows(n, d):
    per = N_BLOCKS // d
    r, lb = n // per, n % per

    def rows(b):
        start = b * (BLOCK * d) + r
        return pl.ds(pl.multiple_of(start, BLOCK), BLOCK) if d == 1 else pl.ds(start, BLOCK, stride=d)

    return rows(lb), rows(jnp.maximum(lb - 1, 0)), lb > 0


def _dil_rotate(q_ref, k_ref, c_ref, s1_ref, s2_ref, q_rot, k_rot):
    tabs = (c_ref[...], s1_ref[...], s2_ref[...])
    q_rot[...] = _rope_apply(q_ref[...], *tabs) * QK_SCALE
    k_rot[...] = _rope_apply(k_ref[...], *tabs)


def _dil_specs():
    def col(base):
        return pl.BlockSpec((SEQ, LANES), lambda p: (0, base // LANES + p))

    table = pl.BlockSpec((SEQ, LANES), lambda p: (0, 0))
    return [col(COL_QA), col(COL_KA), col(COL_VA)], [table] * 3


def _store_columns(blocks, dproj_ref, cols, sem):
    copies = [pltpu.make_async_copy(b, dproj_ref.at[:, pl.ds(pl.multiple_of(c * LANES, LANES), LANES)], sem.at[i])
              for i, (b, c) in enumerate(zip(blocks, cols))]
    for cp in copies:
        cp.start()
    for cp in copies:
        cp.wait()


def _dil_window(d, n, k_rot, v_ref):
    rows, prev, has_prev = _dil_rows(n, d)
    kw, vw = k_rot[rows, :].astype(BF16), v_ref[rows, :].astype(BF16)
    if d == N_BLOCKS:
        row = lax.broadcasted_iota(jnp.int32, (BLOCK, BLOCK), 0)
        valid = lax.broadcasted_iota(jnp.int32, (BLOCK, BLOCK), 1) <= row
    else:
        kw = jnp.concatenate([k_rot[prev, :].astype(BF16), kw], axis=0)
        vw = jnp.concatenate([v_ref[prev, :].astype(BF16), vw], axis=0)
        valid = _dil_valid(_dil_prev_limit(has_prev))
    return rows, prev, kw, vw, jnp.concatenate([valid, valid], axis=0)


def _dil_fwd(proj, tables):
    def body(q_ref, k_ref, v_ref, c_ref, s1_ref, s2_ref, o_ref, lse_ref, q_rot, k_rot):
        upper = _upper_half()
        _dil_rotate(q_ref, k_ref, c_ref, s1_ref, s2_ref, q_rot, k_rot)

        def blocks_of(d):
            def block(n, carry):
                rows, _, kw, vw, valid = _dil_window(d, n, k_rot, v_ref)
                s = jnp.where(valid, _dot_nt(_stack_heads(q_rot[rows, :].astype(BF16)), kw), NEG_INF)
                m = jnp.max(s, axis=-1, keepdims=True)
                p = jnp.exp(s - m)
                den = jnp.sum(p, axis=-1, keepdims=True)
                o_ref[rows, :] = _unstack_heads(_dot_nn((p * (1.0 / den)).astype(BF16), vw))
                lse = m + jnp.log(den)
                lse_ref[rows, :] = jnp.where(upper, lse[BLOCK:], lse[:BLOCK])
                return carry

            lax.fori_loop(0, N_BLOCKS, block, 0, unroll=4)

        for g, d in enumerate(DILATIONS):
            pl.when(pl.program_id(0) // 2 == g)(functools.partial(blocks_of, d))

    qkv, tabs = _dil_specs()
    out = pl.BlockSpec((SEQ, LANES), lambda p: (0, p))
    return pl.pallas_call(
        body, name="dil_attn_fwd", grid=(DIL_WIDTH // LANES,), in_specs=qkv + tabs, out_specs=[out, out],
        out_shape=[pltpu.HBM((SEQ, DIL_WIDTH), F32)] * 2,
        scratch_shapes=[pltpu.VMEM((SEQ, LANES), F32)] * 2,
        compiler_params=pltpu.CompilerParams(dimension_semantics=("parallel",)),
    )(*_in_hbm(proj, proj, proj, *tables))


def _dil_bwd(proj, tables, do, lse, c, dproj, deps=()):
    def body(q_ref, k_ref, v_ref, c_ref, s1_ref, s2_ref, do_ref, lse_ref, cc_ref, dproj_in, *rest):
        dproj_ref, dq_acc, dk_acc, dv_acc, dq_out, dk_out, dv_out, q_rot, k_rot, sem = rest[len(deps):]
        dk_acc[...] = jnp.zeros_like(dk_acc)
        dv_acc[...] = jnp.zeros_like(dv_acc)
        _dil_rotate(q_ref, k_ref, c_ref, s1_ref, s2_ref, q_rot, k_rot)

        def blocks_of(d):
            def block(n, carry):
                rows, prev, kw, vw, valid = _dil_window(d, n, k_rot, v_ref)
                q2 = _stack_heads(q_rot[rows, :].astype(BF16))
                do2 = _stack_heads(do_ref[rows, :].astype(BF16))
                lse_col, c_col = _head_columns(lse_ref[rows, :]), _head_columns(cc_ref[rows, :])
                p = jnp.where(valid, jnp.exp(_dot_nt(q2, kw) - lse_col), 0.0)
                ds = (p * (_dot_nt(do2, vw) + c_col)).astype(BF16)
                dk, dv = _dot_tn(ds, q2), _dot_tn(p.astype(BF16), do2)
                dq_acc[rows, :] = _unstack_heads(_dot_nn(ds, kw)) * QK_SCALE
                if d == N_BLOCKS:
                    dk_acc[rows, :] += dk
                    dv_acc[rows, :] += dv
                else:
                    dk_acc[prev, :] += dk[:BLOCK]
                    dv_acc[prev, :] += dv[:BLOCK]
                    dk_acc[rows, :] += dk[BLOCK:]
                    dv_acc[rows, :] += dv[BLOCK:]
                return carry

            lax.fori_loop(0, N_BLOCKS, block, 0, unroll=4)

        pair = pl.program_id(0)
        for g, d in enumerate(DILATIONS):
            pl.when(pair // 2 == g)(functools.partial(blocks_of, d))
        tabs = (c_ref[...], s1_ref[...], s2_ref[...])
        dq_out[...] = _rope_apply_t(dq_acc[...], *tabs).astype(BF16)
        dk_out[...] = _rope_apply_t(dk_acc[...], *tabs).astype(BF16)
        dv_out[...] = dv_acc[...].astype(BF16)
        _store_columns((dq_out, dk_out, dv_out), dproj_ref,
                       [base // LANES + pair for base in (COL_QA, COL_KA, COL_VA)], sem)

    qkv, tabs = _dil_specs()
    tok = pl.BlockSpec((SEQ, LANES), lambda p: (0, p))
    return pl.pallas_call(
        body, name="dil_attn_bwd", grid=(DIL_WIDTH // LANES,),
        in_specs=qkv + tabs + [tok, tok, tok, _ANY] + [_ANY] * len(deps), out_specs=_ANY,
        out_shape=pltpu.HBM(dproj.shape, dproj.dtype),
        scratch_shapes=[pltpu.VMEM((SEQ, LANES), F32)] * 3 + [pltpu.VMEM((SEQ, LANES), BF16)] * 3
        + [pltpu.VMEM((SEQ, LANES), F32)] * 2 + [pltpu.SemaphoreType.DMA((3,))],
        input_output_aliases={9: 0},
        compiler_params=pltpu.CompilerParams(dimension_semantics=("arbitrary",)),
    )(*_in_hbm(proj, proj, proj, *tables, do, lse, c, dproj), *deps)


def _group_weights(l0, l1, l2):
    m = jnp.maximum(jnp.maximum(l0, l1), l2)
    e0, e1, e2 = jnp.exp(l0 - m), jnp.exp(l1 - m), jnp.exp(l2 - m)
    tot = e0 + e1 + e2
    return e0 / tot, e1 / tot, e2 / tot


def _dil_combine(o, lse, deps=()):
    def fn(o0, o1, o2, l0, l1, l2):
        w0, w1, w2 = _group_weights(l0, l1, l2)
        return w0 * o0 + w1 * o1 + w2 * o2

    w = DIL_OUT_WIDTH
    return _rowwise(fn, "dil_combine", SEQ, 512, [(o, w, g) for g in range(3)] + [(lse, w, g) for g in range(3)], [],
                    [(w, F32)], deps=deps)[0]


def _dil_combine_bwd(d_out, o, lse, deps=()):
    w = DIL_OUT_WIDTH

    def fn(d, o0, o1, o2, l0, l1, l2):
        row = lax.broadcasted_iota(jnp.int32, (w, w), 0) // HEAD_DIM
        col = lax.broadcasted_iota(jnp.int32, (w, w), 1) // HEAD_DIM
        same_head = jnp.where(row == col, 1.0, 0.0).astype(BF16)
        ws = _group_weights(l0, l1, l2)
        dws = [_dot3_nn(d * og, same_head) for og in (o0, o1, o2)]
        mean = ws[0] * dws[0] + ws[1] * dws[1] + ws[2] * dws[2]
        return jnp.concatenate([wg * d for wg in ws], axis=1), jnp.concatenate([-wg * mean for wg in ws], axis=1)

    return _rowwise(fn, "dil_combine_bwd", SEQ, 256,
                    [(d_out, w, 0)] + [(o, w, g) for g in range(3)] + [(lse, w, g) for g in range(3)], [],
                    [(DIL_WIDTH, F32)] * 2, deps=deps)


def _log1p(e):
    u = 1.0 + e
    return jnp.where(u == 1.0, e, jnp.log(u) * (e / (u - 1.0)))


def _fox_gate(proj, b_pad, deps=()):
    def body(f_ref, b_ref, *rest):
        o_ref = rest[-1]
        z = f_ref[...] + b_ref[...]
        logf = (jnp.minimum(z, 0.0) - _log1p(jnp.exp(-jnp.abs(z)))).T[:F_ROWS]
        row = lax.broadcasted_iota(jnp.int32, (BLOCK, BLOCK), 0)
        col = lax.broadcasted_iota(jnp.int32, (BLOCK, BLOCK), 1)
        before = jnp.where(row <= col, 1.0, 0.0).astype(BF16)
        carry = jnp.zeros((F_ROWS, 1), F32)
        for blk in range(N_BLOCKS):
            run = _dot3_nn(logf[:, blk * BLOCK:(blk + 1) * BLOCK], before) + carry
            o_ref[:, blk * BLOCK:(blk + 1) * BLOCK] = run
            carry = run[:, BLOCK - 1:BLOCK]

    return pl.pallas_call(
        body, name="fox_gate", grid=(1,),
        in_specs=[pl.BlockSpec((SEQ, LANES), lambda i: (0, COL_F // LANES)), pl.BlockSpec((1, LANES), lambda i: (0, 0))]
        + [_ANY] * len(deps),
        out_specs=pl.BlockSpec((F_ROWS, SEQ), lambda i: (0, 0)),
        out_shape=pltpu.HBM((F_ROWS, SEQ), F32),
    )(*_in_hbm(proj, b_pad), *deps)


def _fox_gate_bwd(d_cum, proj, b_pad, dproj):
    def body(d_ref, f_ref, b_ref, dproj_ref, dz_ref, db_ref):
        row = lax.broadcasted_iota(jnp.int32, (BLOCK, BLOCK), 0)
        col = lax.broadcasted_iota(jnp.int32, (BLOCK, BLOCK), 1)
        after = jnp.where(row >= col, 1.0, 0.0).astype(BF16)
        carry = jnp.zeros((F_ROWS, 1), F32)
        parts = [None] * N_BLOCKS
        for blk in reversed(range(N_BLOCKS)):
            run = _dot3_nn(d_ref[:, blk * BLOCK:(blk + 1) * BLOCK], after) + carry
            parts[blk] = run
            carry = run[:, 0:1]
        dlogf = jnp.concatenate(parts, axis=1)
        dlogf = jnp.concatenate([dlogf, jnp.zeros((LANES - F_ROWS, SEQ), F32)], axis=0).T
        dz = dlogf * _sigmoid(-(f_ref[...] + b_ref[...]))
        dz_ref[...] = dz.astype(BF16)
        db_ref[...] = jnp.sum(dz, axis=0, keepdims=True)

    f_cols = pl.BlockSpec((SEQ, LANES), lambda i: (0, COL_F // LANES))
    return pl.pallas_call(
        body, name="fox_gate_bwd", grid=(1,),
        in_specs=[pl.BlockSpec((F_ROWS, SEQ), lambda i: (0, 0)), f_cols, pl.BlockSpec((1, LANES), lambda i: (0, 0)), _ANY],
        out_specs=[f_cols, pl.BlockSpec((1, LANES), lambda i: (0, 0))],
        out_shape=[pltpu.HBM(dproj.shape, dproj.dtype), pltpu.HBM((1, LANES), F32)],
        input_output_aliases={3: 0},
    )(*_in_hbm(d_cum, proj, b_pad, dproj))


FOX_TILE = 256
FOX_TILES = SEQ // FOX_TILE


def _row_to_col(row):
    n = row.shape[1]
    eye = lax.broadcasted_iota(jnp.int32, (n, n), 0) == lax.broadcasted_iota(jnp.int32, (n, n), 1)
    return jnp.sum(jnp.where(eye, row, 0.0), axis=1, keepdims=True)


def _fox_bias(f_row, i):
    t = FOX_TILE
    ext = (i + 1) * t
    bias = _row_to_col(f_row[:, i * t:(i + 1) * t]) - f_row[:, :ext]
    row = lax.broadcasted_iota(jnp.int32, (t, ext), 0) + i * t
    col = lax.broadcasted_iota(jnp.int32, (t, ext), 1)
    return bias, col <= row


def _fox_specs():
    qkv = [pl.BlockSpec((SEQ, LANES), lambda p, base=base: (0, base // LANES + p)) for base in (COL_QB, COL_KB, COL_VB)]
    return qkv, pl.BlockSpec((F_ROWS, SEQ), lambda p: (0, 0))


def _fox_fwd(proj, f_rows):
    t = FOX_TILE

    def body(q_ref, k_ref, v_ref, f_ref, o_ref, lse_ref):
        pair = pl.program_id(0)
        upper = _upper_half()
        k16, v16 = k_ref[...].astype(BF16), v_ref[...].astype(BF16)
        f_row = [f_ref[pl.ds(2 * pair + e, 1), :] for e in range(2)]
        for i in range(FOX_TILES):
            ext = (i + 1) * t
            q_tile = (q_ref[i * t:(i + 1) * t, :] * QK_SCALE).astype(BF16)
            s2 = _dot_nt(_stack_heads(q_tile), k16[:ext])
            pns, lses = [], []
            for e in range(2):
                bias, causal = _fox_bias(f_row[e], i)
                s = jnp.where(causal, s2[e * t:(e + 1) * t] + bias, NEG_INF)
                m = jnp.max(s, axis=-1, keepdims=True)
                p = jnp.exp(s - m)
                den = jnp.sum(p, axis=-1, keepdims=True)
                pns.append((p * (1.0 / den)).astype(BF16))
                lses.append(m + jnp.log(den))
            o_ref[i * t:(i + 1) * t, :] = _unstack_heads(_dot_nn(jnp.concatenate(pns, axis=0), v16[:ext]))
            lse_ref[i * t:(i + 1) * t, :] = jnp.where(upper, lses[1], lses[0])

    qkv, f_spec = _fox_specs()
    tok = pl.BlockSpec((SEQ, LANES), lambda p: (0, p))
    return pl.pallas_call(
        body, name="fox_attn_fwd", grid=(FOX_WIDTH // LANES,),
        in_specs=qkv + [f_spec], out_specs=[tok, tok],
        out_shape=[pltpu.HBM((SEQ, FOX_WIDTH), F32)] * 2,
        compiler_params=pltpu.CompilerParams(
            dimension_semantics=("parallel",), vmem_limit_bytes=_vmem_limit(8 * t * SEQ * 4)),
    )(*_in_hbm(proj, proj, proj, f_rows))


def _fox_bwd(proj, do, lse, f_rows, dproj):
    t = FOX_TILE

    def body(q_ref, k_ref, v_ref, f_ref, do_ref, lse_ref, dproj_in, dproj_ref, df_ref, dk_acc, dv_acc,
             dq_out, dk_out, dv_out, sem):
        pair = pl.program_id(0)
        upper = _upper_half()
        k16, v16 = k_ref[...].astype(BF16), v_ref[...].astype(BF16)
        f_row = [f_ref[pl.ds(2 * pair + e, 1), :] for e in range(2)]
        dk_acc[...] = jnp.zeros_like(dk_acc)
        dv_acc[...] = jnp.zeros_like(dv_acc)
        df_ref[...] = jnp.zeros_like(df_ref)
        for i in range(FOX_TILES):
            ext = (i + 1) * t
            q_tile = (q_ref[i * t:(i + 1) * t, :] * QK_SCALE).astype(BF16)
            do_tile = do_ref[i * t:(i + 1) * t, :]
            lse_t = lse_ref[i * t:(i + 1) * t, :]
            q2, do2 = _stack_heads(q_tile), _stack_heads(do_tile)
            s2, dp2 = _dot_nt(q2, k16[:ext]), _dot_nt(do2, v16[:ext])
            ps, dss = [], []
            for e in range(2):
                bias, causal = _fox_bias(f_row[e], i)
                s = s2[e * t:(e + 1) * t] + bias
                p = jnp.where(causal, jnp.exp(s - lse_t[:, e * HEAD_DIM:e * HEAD_DIM + 1]), 0.0)
                dp = dp2[e * t:(e + 1) * t]
                ds = p * (dp - jnp.sum(p * dp, axis=-1, keepdims=True))
                df_ref[0, e:e + 1, :ext] -= jnp.sum(ds, axis=0, keepdims=True)
                ps.append(p.astype(BF16))
                dss.append(ds.astype(BF16))
            ds2, p2 = jnp.concatenate(dss, axis=0), jnp.concatenate(ps, axis=0)
            dq_out[i * t:(i + 1) * t, :] = (_unstack_heads(_dot_nn(ds2, k16[:ext])) * QK_SCALE).astype(BF16)
            dk_acc[:ext, :] += _dot_tn(ds2, q2)
            dv_acc[:ext, :] += _dot_tn(p2, do2)
        dk_out[...] = dk_acc[...].astype(BF16)
        dv_out[...] = dv_acc[...].astype(BF16)
        _store_columns((dq_out, dk_out, dv_out), dproj_ref, [base // LANES + pair for base in (COL_QB, COL_KB, COL_VB)],
                       sem)

    qkv, f_spec = _fox_specs()
    tok = pl.BlockSpec((SEQ, LANES), lambda p: (0, p))
    return pl.pallas_call(
        body, name="fox_attn_bwd", grid=(FOX_WIDTH // LANES,),
        in_specs=qkv + [f_spec, tok, tok, _ANY],
        out_specs=[_ANY, pl.BlockSpec((1, SUBLANES, SEQ), lambda p: (p, 0, 0))],
        out_shape=[pltpu.HBM(dproj.shape, dproj.dtype),
                   pltpu.HBM((FOX_WIDTH // LANES, SUBLANES, SEQ), F32)],
        scratch_shapes=[pltpu.VMEM((SEQ, LANES), F32)] * 2 + [pltpu.VMEM((SEQ, LANES), BF16)] * 3
        + [pltpu.SemaphoreType.DMA((3,))],
        input_output_aliases={6: 0},
        compiler_params=pltpu.CompilerParams(
            dimension_semantics=("arbitrary",), vmem_limit_bytes=_vmem_limit(10 * t * SEQ * 4)),
    )(*_in_hbm(proj, proj, proj, f_rows, do, lse, dproj))


MIX_TILE = 256


def _mix_out(out_a, out_b, proj, x, wt_pa, wt_pb, w_out, g_post, g_ffn_pre):
    tm = MIX_TILE

    def body(a_ref, b_ref, ga_ref, gb_ref, x_ref, wpa_ref, wpb_ref, wo_ref, g2_ref, g3_ref,
             merged_ref, mix_ref, x1_ref, h2_ref):
        ya = _dot_nn(a_ref[...].astype(BF16), wpa_ref[...])
        yb = _dot_nn(b_ref[...].astype(BF16), wpb_ref[...])
        merged = (_sigmoid(ga_ref[...]) * ya + _sigmoid(gb_ref[...]) * yb).astype(BF16)
        merged_ref[...] = merged
        mix = _dot_nn(merged, wo_ref[...])
        mix_ref[...] = mix
        x1 = x_ref[...] + mix * _rms_scale(mix) * g2_ref[...]
        x1_ref[...] = x1
        h2_ref[...] = (x1 * _rms_scale(x1) * g3_ref[...]).astype(BF16)

    def rows(w, cb=0):
        return pl.BlockSpec((tm, w), lambda i, cb=cb: (i, cb))

    def whole(a):
        return pl.BlockSpec(a.shape, lambda i: (0, 0))

    d = D_MODEL
    blk = _nbytes((tm, d), F32) * 6 + sum(_nbytes(a.shape, BF16) for a in (wt_pa, wt_pb, w_out))
    return pl.pallas_call(
        body, name="mix_out", grid=(SEQ // tm,),
        in_specs=[rows(DIL_OUT_WIDTH), rows(FOX_WIDTH), rows(d, COL_GA // d), rows(d, COL_GB // d), rows(d),
                  whole(wt_pa), whole(wt_pb), whole(w_out), whole(g_post), whole(g_ffn_pre)],
        out_specs=[rows(d)] * 4,
        out_shape=[pltpu.HBM((SEQ, d), dt) for dt in (BF16, F32, F32, BF16)],
        compiler_params=pltpu.CompilerParams(dimension_semantics=("parallel",), vmem_limit_bytes=_vmem_limit(blk)),
    )(*_in_hbm(out_a, out_b, proj, proj, x, wt_pa, wt_pb, w_out, g_post, g_ffn_pre))


def _mix_out_bwd(dmix, out_a, out_b, proj, wt_pa, wt_pb, w_out, deps=()):
    tm = MIX_TILE

    def body(dm_ref, a_ref, b_ref, ga_ref, gb_ref, wpa_ref, wpb_ref, wo_ref, *rest):
        dproj_ref, dya_ref, dyb_ref, da_ref, db_ref = rest[len(deps):]
        dmerged = _dot_nt(dm_ref[...], wo_ref[...])
        ya = _dot_nn(a_ref[...].astype(BF16), wpa_ref[...])
        yb = _dot_nn(b_ref[...].astype(BF16), wpb_ref[...])
        sa, sb = _sigmoid(ga_ref[...]), _sigmoid(gb_ref[...])
        dproj_ref[:, COL_GA:COL_GA + D_MODEL] = (dmerged * ya * (sa * (1.0 - sa))).astype(BF16)
        dproj_ref[:, COL_GB:COL_GB + D_MODEL] = (dmerged * yb * (sb * (1.0 - sb))).astype(BF16)
        dproj_ref[:, COL_GB + D_MODEL:] = jnp.zeros((tm, COL_QA - COL_GB - D_MODEL), BF16)
        dya = (dmerged * sa).astype(BF16)
        dyb = (dmerged * sb).astype(BF16)
        dya_ref[...] = dya
        dyb_ref[...] = dyb
        da_ref[...] = _dot_nt(dya, wpa_ref[...])
        db_ref[...] = _dot_nt(dyb, wpb_ref[...]).astype(BF16)

    def rows(w, cb=0):
        return pl.BlockSpec((tm, w), lambda i, cb=cb: (i, cb))

    def whole(a):
        return pl.BlockSpec(a.shape, lambda i: (0, 0))

    d = D_MODEL
    blk = _nbytes((tm, d), F32) * 8 + sum(_nbytes(a.shape, BF16) for a in (wt_pa, wt_pb, w_out))
    return pl.pallas_call(
        body, name="mix_out_bwd", grid=(SEQ // tm,),
        in_specs=[rows(d), rows(DIL_OUT_WIDTH), rows(FOX_WIDTH), rows(d, COL_GA // d), rows(d, COL_GB // d),
                  whole(wt_pa), whole(wt_pb), whole(w_out)] + [_ANY] * len(deps),
        out_specs=[rows(COL_QA)] + [rows(d)] * 2 + [rows(DIL_OUT_WIDTH), rows(FOX_WIDTH)],
        out_shape=[pltpu.HBM((SEQ, PROJ_COLS), BF16)] + [pltpu.HBM((SEQ, d), BF16)] * 2
        + [pltpu.HBM((SEQ, DIL_OUT_WIDTH), F32), pltpu.HBM((SEQ, FOX_WIDTH), BF16)],
        compiler_params=pltpu.CompilerParams(dimension_semantics=("parallel",), vmem_limit_bytes=_vmem_limit(blk)),
    )(*_in_hbm(dmix, out_a, out_b, proj, proj, wt_pa, wt_pb, w_out), *deps)


FFN_TM, FFN_TN = 2048, 256


def _ffn_up(h2, wt_gate, wt_up):
    tm, tn = FFN_TM, FFN_TN

    def body(h_ref, wg_ref, wu_ref, gate_ref, up_ref, act_ref):
        for rows in (slice(0, tm // 2), slice(tm // 2, tm)):
            gate = _dot_nt(h_ref[rows, :], wg_ref[...])
            up = _dot_nt(h_ref[rows, :], wu_ref[...])
            gate_ref[rows, :] = gate
            up_ref[rows, :] = up
            act_ref[rows, :] = (gate * _sigmoid(gate) * up).astype(BF16)

    tile = pl.BlockSpec((tm, tn), lambda i, j: (i, j))
    w_spec = pl.BlockSpec((tn, D_MODEL), lambda i, j: (j, 0))
    return pl.pallas_call(
        body, name="ffn_up", grid=(SEQ // tm, D_FF // tn),
        in_specs=[pl.BlockSpec((tm, D_MODEL), lambda i, j: (i, 0)), w_spec, w_spec],
        out_specs=[tile, tile, tile],
        out_shape=[pltpu.HBM((SEQ, D_FF), dt) for dt in (F32, F32, BF16)],
        compiler_params=pltpu.CompilerParams(
            dimension_semantics=("parallel", "parallel"), vmem_limit_bytes=_vmem_limit(8 * 2**20)),
    )(h2, wt_gate, wt_up)


def _ffn_act_bwd(dff, w_down, gate, up):
    tm, tn = FFN_TM, FFN_TN

    def body(d_ref, wd_ref, gate_ref, up_ref, dgate_ref, dup_ref):
        for rows in (slice(0, tm // 2), slice(tm // 2, tm)):
            dact = _dot_nt(d_ref[rows, :], wd_ref[...])
            gate = gate_ref[rows, :]
            sg = _sigmoid(gate)
            dgate_ref[rows, :] = (dact * up_ref[rows, :] * (sg * (1.0 + gate * (1.0 - sg)))).astype(BF16)
            dup_ref[rows, :] = (dact * (gate * sg)).astype(BF16)

    tile = pl.BlockSpec((tm, tn), lambda i, j: (i, j))
    return pl.pallas_call(
        body, name="ffn_act_bwd", grid=(SEQ // tm, D_FF // tn),
        in_specs=[pl.BlockSpec((tm, D_MODEL), lambda i, j: (i, 0)), pl.BlockSpec((tn, D_MODEL), lambda i, j: (j, 0)),
                  tile, tile],
        out_specs=[tile, tile],
        out_shape=[pltpu.HBM((SEQ, D_FF), BF16)] * 2,
        compiler_params=pltpu.CompilerParams(
            dimension_semantics=("parallel", "parallel"), vmem_limit_bytes=_vmem_limit(8 * 2**20)),
    )(dff, w_down, gate, up)


EPILOGUE_TM = 512


def _loss_head(act, w_down, x1, target, g_post):
    def fn(ff, x1, tgt, g):
        r = _rms_scale(ff)
        nrm = ff * r
        err = (x1 + nrm * g) - tgt
        loss = 0.5 * jnp.sum(jnp.mean(err * err, axis=-1, keepdims=True), axis=0, keepdims=True)
        dy = err * (1.0 / D_MODEL)
        u = dy * g
        dff = r * u - ff * (r * r * r) * jnp.mean(u * ff, axis=-1, keepdims=True)
        return dy, dff, jnp.broadcast_to(loss, (1, LANES)), jnp.sum(dy * nrm, axis=0, keepdims=True)

    d = D_MODEL
    return _matmul_rowwise([(act, w_down)], fn, "ffn_down_loss", EPILOGUE_TM, [(x1, d, 0), (target, d, 0)], [g_post],
                           [(d, F32), (d, BF16)], [LANES, d])


def _post_ffn_bwd(dgate, wt_gate, dup, wt_up, x1, dy, mix, g_ffn_pre, g_mix_post, deps=()):
    def fn(dh2, x1, dy, mix, g3, g2):
        dx, dg3 = _rms_bwd(x1, dh2, g3)
        dx1 = dy + dx
        dmix, dg2 = _rms_bwd(mix, dx1, g2)
        return dx1, dmix, dg3, dg2

    d = D_MODEL
    return _matmul_rowwise([(dgate, wt_gate), (dup, wt_up)], fn, "ffn_up_bwd", EPILOGUE_TM,
                           [(x1, d, 0), (dy, d, 0), (mix, d, 0)], [g_ffn_pre, g_mix_post],
                           [(d, F32), (d, BF16)], [d, d], deps=deps)


def _input_bwd(dproj, wt_r, x, dx1, g_pre, deps=()):
    def fn(dh, x, dx1, g):
        dx, dg = _rms_bwd(x, dh, g)
        return dx1 + dx, dg

    d = D_MODEL
    return _matmul_rowwise([(dproj, wt_r)], fn, "in_proj_bwd", EPILOGUE_TM, [(x, d, 0), (dx1, d, 0)], [g_pre],
                           [(d, F32)], [d], deps=deps)


def _adam_math(w, g, m, v):
    m = ADAM_B1 * m + (1.0 - ADAM_B1) * g
    v = ADAM_B2 * v + (1.0 - ADAM_B2) * (g * g)
    m_hat = m / (1.0 - ADAM_B1 ** ADAM_STEP)
    v_hat = v / (1.0 - ADAM_B2 ** ADAM_STEP)
    delta = -ADAM_LR * (m_hat / (jnp.sqrt(v_hat) + ADAM_EPS) + ADAM_WD * w)
    return delta, m, v


def _adam(w, mine, recv, m, v, name):
    r, c = w.shape
    tc = _col_tile(r, c)

    def body(w_ref, p_ref, r_ref, m_ref, v_ref, g_ref, d_ref, nm_ref, nv_ref):
        g = ((p_ref[...] + r_ref[0].astype(F32)) + r_ref[1].astype(F32)) + r_ref[2].astype(F32)
        g_ref[...] = g
        d_ref[...], nm_ref[...], nv_ref[...] = _adam_math(w_ref[...], g, m_ref[...], v_ref[...])

    spec = pl.BlockSpec((r, tc), lambda j: (0, j))
    return pl.pallas_call(
        body, name=name, grid=(c // tc,),
        in_specs=[spec, spec, pl.BlockSpec((3, r, tc), lambda j: (0, 0, j)), spec, spec], out_specs=[spec] * 4,
        out_shape=[pltpu.HBM((r, c), F32)] * 4,
        compiler_params=pltpu.CompilerParams(dimension_semantics=("parallel",)),
    )(*_in_hbm(w, mine, recv, m, v))


def _adam_small(gathered, ws, ms, vs, loss_parts):
    n = len(ws)

    def body(*refs):
        outs = refs[4 * n + 1:]
        loss = refs[4 * n][0]
        for dev in range(1, N_DEV):
            loss = loss + refs[4 * n][dev]
        outs[4 * n][...] = loss
        for i in range(n):
            ga_ref, w_ref, m_ref, v_ref = (refs[j * n + i] for j in range(4))
            g = ga_ref[0]
            for dev in range(1, N_DEV):
                g = g + ga_ref[dev]
            g = g[:, :w_ref.shape[1]]
            outs[4 * i][...] = g
            outs[4 * i + 1][...], outs[4 * i + 2][...], outs[4 * i + 3][...] = _adam_math(
                w_ref[...], g, m_ref[...], v_ref[...])

    out_shape = [pltpu.HBM(w.shape, F32) for w in ws for _ in range(4)]
    out_shape.append(pltpu.HBM((1, LANES), F32))
    out = pl.pallas_call(body, name="adam_small", out_shape=out_shape)(*gathered, *ws, *ms, *vs, loss_parts)
    return [out[4 * i:4 * i + 4] for i in range(n)], out[4 * n]


_PROJ_SEGMENTS = ((3848, 5896), (None, COL_QA - 2 * D_MODEL), (0, 3840), (3840, 3848), (None, PROJ_COLS - COL_F - 8))


def _proj_weight_t(gathered):
    w = gathered.reshape(IN_COLS, D_MODEL)
    return jnp.concatenate([jnp.zeros((hi, D_MODEL), w.dtype) if lo is None else w[lo:hi] for lo, hi in _PROJ_SEGMENTS],
                           axis=0)


def _proj_weight_grad_slots(dwt_r):
    starts, at = [], 0
    for lo, hi in _PROJ_SEGMENTS:
        if lo is not None:
            starts.append((lo, hi, at))
        at += hi if lo is None else hi - lo
    slots = []
    for dev in range(N_DEV):
        pieces, lo, end = [], dev * IN_SHARD, (dev + 1) * IN_SHARD
        for seg_lo, seg_hi, seg_at in sorted(starts):
            a, b = max(lo, seg_lo), min(end, seg_hi)
            if a < b:
                pieces.append(dwt_r[seg_at + a - seg_lo:seg_at + b - seg_lo])
        slots.append(pieces[0] if len(pieces) == 1 else jnp.concatenate(pieces, axis=0))
    return jnp.stack(slots)


def kernel(x, w_in, w_proj_a, w_proj_b, w_out, b_forget, w_ffn_gate, w_ffn_up, w_ffn_down, norm_mix_pre, norm_mix_post, norm_ffn_pre, norm_ffn_post, loss_target, m_w_in, m_w_proj_a, m_w_proj_b, m_w_out, m_b_forget, m_w_ffn_gate, m_w_ffn_up, m_w_ffn_down, m_norm_mix_pre, m_norm_mix_post, m_norm_ffn_pre, m_norm_ffn_post, v_w_in, v_w_proj_a, v_w_proj_b, v_w_out, v_b_forget, v_w_ffn_gate, v_w_ffn_up, v_w_ffn_down, v_norm_mix_pre, v_norm_mix_post, v_norm_ffn_pre, v_norm_ffn_post):
    d = D_MODEL
    names = ("w_in", "w_proj_a", "w_proj_b", "w_out", "w_ffn_gate", "w_ffn_up", "w_ffn_down")
    col_sharded = ("w_in", "w_ffn_gate", "w_ffn_up")

    def row_shards(arrs):
        return {k: (a[0].T if k in col_sharded else a[0]) for k, a in zip(names, arrs)}

    shards = row_shards((w_in, w_proj_a, w_proj_b, w_out, w_ffn_gate, w_ffn_up, w_ffn_down))
    moments_m = row_shards((m_w_in, m_w_proj_a, m_w_proj_b, m_w_out, m_w_ffn_gate, m_w_ffn_up, m_w_ffn_down))
    moments_v = row_shards((v_w_in, v_w_proj_a, v_w_proj_b, v_w_out, v_w_ffn_gate, v_w_ffn_up, v_w_ffn_down))
    pos = jnp.stack([lax.axis_index("c"), 2 * lax.axis_index("x") + lax.axis_index("y")]).astype(jnp.int32)
    x2, target = x[0], loss_target[0]

    me = 4 * lax.axis_index("x") + 2 * lax.axis_index("y") + lax.axis_index("c")
    mid_names, ffn_names = names[1:4], names[4:]
    first_names, later_names = names[:1], names[1:]
    shards16 = {k: shards[k].astype(BF16) for k in names}

    def landing(k):
        return lax.dynamic_update_slice(lax.empty((N_DEV,) + shards[k].shape, BF16), shards16[k][None], (me, 0, 0))

    ag_first = _exchange_start("ag_first_chips_start", _gather_chips_copies, [shards16[k] for k in first_names],
                               [landing(k) for k in first_names], 3 * len(first_names))
    h = _rowwise(lambda xb, g: xb * _rms_scale(xb) * g, "norm_mix_pre", SEQ, 256, [(x2, d, 0)], [norm_mix_pre],
                 [(d, BF16)], deps=[ag_first.token])[0]
    later_lands = [landing(k) for k in later_names]
    ag_first = _gather_relay("ag_first_chips_relay", ag_first,
                             [h, shards["w_in"], moments_m["w_in"], moments_v["w_in"], *later_lands,
                              *[shards16[k] for k in later_names]])
    ag_later = _exchange_start("ag_later_chips_start", _gather_chips_copies, [shards16[k] for k in later_names],
                               later_lands, 3 * len(later_names), after=[ag_first.token])
    gathered = dict(zip(first_names, _exchange_wait("ag_first_sibling_wait", ag_first, [ag_later.token])[1]))
    wt_r = _proj_weight_t(gathered["w_in"])

    proj = _matmul([(h, *_in_hbm(wt_r))], "nt", F32, "in_proj", 1024, 896, 1024)
    tables = _rope_tables()
    o_dil, lse_dil = _dil_fwd(proj, tables)
    out_a = _dil_combine(o_dil, lse_dil)
    _, lands = _exchange_wait("ag_later_chips_wait", ag_later, [out_a])
    ag_later = _exchange_start("ag_later_sibling_start", _gather_sibling_copies, [], lands, 4 * len(later_names))

    b_pad = jnp.pad(b_forget, ((0, 0), (0, LANES - N_FOX_HEADS)))
    f_rows = _fox_gate(proj, b_pad, deps=[ag_later.token])
    out_b, lse_fox = _fox_fwd(proj, f_rows)

    gathered = dict(zip(later_names, _exchange_wait("ag_later_sibling_wait", ag_later, [out_b])[1]))
    wt_pa = gathered["w_proj_a"].transpose(1, 0, 2).reshape(DIL_OUT_WIDTH, d)
    wt_pb = gathered["w_proj_b"].transpose(1, 0, 2).reshape(FOX_WIDTH, d)
    w_o = gathered["w_out"].reshape(d, d)
    wt_g = gathered["w_ffn_gate"].reshape(D_FF, d)
    wt_u = gathered["w_ffn_up"].reshape(D_FF, d)
    w_d = gathered["w_ffn_down"].reshape(D_FF, d)
    merged, mix, x1, h2 = _mix_out(out_a, out_b, proj, x2, wt_pa, wt_pb, w_o, norm_mix_post, norm_ffn_pre)

    gate, up, act = _ffn_up(h2, wt_g, wt_u)
    dy, dff, loss_part, dg_ffn_post = _loss_head(act, w_d, x1, target, norm_ffn_post)

    dgate, dup = _ffn_act_bwd(dff, w_d, gate, up)
    grads_t = {}
    grads_t["w_ffn_down"] = _matmul([(act, dff)], "tn", F32, "grad_w_ffn_down", 1408, 512, 2048, staged=False)
    grads_t["w_ffn_gate"] = _matmul([(dgate, h2)], "tn", F32, "grad_w_ffn_gate", 1408, 512, 2048)
    grads_t["w_ffn_up"] = _matmul([(dup, h2)], "tn", F32, "grad_w_ffn_up", 1408, 512, 2048)
    rs_ffn = _ReduceScatter("ffn", {k: grads_t[k] for k in ffn_names}, pos)
    dx1, dmix, dg_ffn_pre, dg_mix_post = _post_ffn_bwd(dgate, wt_g, dup, wt_u, x1, dy, mix, norm_ffn_pre, norm_mix_post,
                                                       deps=[rs_ffn.token])
    rs_ffn.start_chips([dmix])

    dproj, dya, dyb, d_out_a, d_out_b = _mix_out_bwd(dmix, out_a, out_b, proj, wt_pa, wt_pb, w_o, deps=[rs_ffn.token])
    grads_t["w_out"] = _matmul([(merged, dmix)], "tn", F32, "grad_w_out", 1024, 1024, 1024)
    def column_slots(g):
        return g.reshape(g.shape[0], N_DEV, LANES).transpose(1, 0, 2)

    grads_t["w_proj_a"] = column_slots(_matmul([(out_a, dya)], "tn", F32, "grad_w_proj_a", DIL_OUT_WIDTH, 1024, SEQ))
    grads_t["w_proj_b"] = column_slots(_matmul([(out_b, dyb)], "tn", F32, "grad_w_proj_b", FOX_WIDTH, 1024, SEQ))
    rs_mid = _ReduceScatter("mid", {k: grads_t[k] for k in mid_names}, pos)

    do_dil, c_dil = _dil_combine_bwd(d_out_a, o_dil, lse_dil, deps=[rs_mid.token])
    rs_mid.start_chips([c_dil])
    dproj, d_cum = _fox_bwd(proj, d_out_b, lse_fox, f_rows, dproj)
    d_cum_rows = jnp.pad(d_cum[:, :2].reshape(N_FOX_HEADS, SEQ), ((0, F_ROWS - N_FOX_HEADS), (0, 0)))
    dproj, db_part = _fox_gate_bwd(d_cum_rows, proj, b_pad, dproj)
    dproj = _dil_bwd(proj, tables, do_dil, lse_dil, c_dil, dproj, deps=[rs_mid.token])

    dwt_r = _matmul([(dproj, h)], "tn", F32, "grad_w_in", 896, 1024, 2048)
    rs_in = _ReduceScatter("in", {"w_in": _proj_weight_grad_slots(dwt_r)}, pos)
    def finish(rs, after):
        return {k: _adam(shards[k], mine, recv, moments_m[k], moments_v[k], "adam_" + k)
                for k, (mine, recv) in rs.finish(after).items()}

    done = finish(rs_ffn, [rs_in.token])
    rs_in.start_chips([done[k][0] for k in ffn_names])
    grad_x, dg_mix_pre = _input_bwd(dproj, wt_r, x2, dx1, norm_mix_pre, deps=[rs_in.token])
    done.update(finish(rs_mid, [grad_x]))

    small_all = _all_gather([dg_mix_pre, dg_mix_post, dg_ffn_pre, dg_ffn_post, db_part, loss_part],
                            "small_grads_all_gather", deps=[done[k][0] for k in mid_names])
    small, loss = _adam_small(small_all[:5], [norm_mix_pre, norm_mix_post, norm_ffn_pre, norm_ffn_post, b_forget],
                              [m_norm_mix_pre, m_norm_mix_post, m_norm_ffn_pre, m_norm_ffn_post, m_b_forget],
                              [v_norm_mix_pre, v_norm_mix_post, v_norm_ffn_pre, v_norm_ffn_post, v_b_forget],
                              small_all[5])

    done.update(finish(rs_in, [small[0][0]]))

    def leaves(i):
        def nat(k):
            a = done[k][i]
            return (a.T if k in col_sharded else a)[None]

        return [nat("w_in"), nat("w_proj_a"), nat("w_proj_b"), nat("w_out"), small[4][i],
                nat("w_ffn_gate"), nat("w_ffn_up"), nat("w_ffn_down"), *[small[r][i] for r in range(4)]]

    return (loss[0, 0], grad_x[None], *leaves(0), *leaves(1), *leaves(2), *leaves(3))
```

```python
import functools
import math

import jax
import jax.numpy as jnp
import numpy as np
from jax import lax
from jax.experimental import pallas as pl
from jax.experimental.pallas import tpu as pltpu

F32 = jnp.float32
BF16 = jnp.bfloat16
MESH = pl.DeviceIdType.MESH

D_MODEL = 1024
SEQ = 2048
HEAD_DIM = 64
BLOCK = 128
N_BLOCKS = SEQ // BLOCK
DILATIONS = (1, 4, 16)
N_FOX_HEADS = 8
DIL_WIDTH = 768
DIL_OUT_WIDTH = 256
FOX_WIDTH = 512
D_FF = 2816
ROPE_THETA = 500000.0
ROPE_DIM = HEAD_DIM // 4
ROPE_HALF = ROPE_DIM // 2
EPS = 1e-6
NEG_INF = -1e30
QK_SCALE = 1.0 / math.sqrt(HEAD_DIM)
IN_COLS = 5896
N_DEV = 8
IN_SHARD = IN_COLS // N_DEV

ADAM_LR = 0.001
ADAM_B1 = 0.9
ADAM_B2 = 0.999
ADAM_EPS = 1e-08
ADAM_WD = 0.01
ADAM_STEP = 10

V7X_VMEM_BYTES = 64 * 2**20
LANES = 128
SUBLANES = 8

PROJ_COLS = 6272
COL_GA, COL_GB = 0, 1024
COL_QA, COL_KA, COL_VA = 2304, 3072, 3840
COL_QB, COL_KB, COL_VB = 4608, 5120, 5632
COL_F = 6144
F_ROWS = 16


def _vmem_limit(block_bytes):
    want = 2 * block_bytes + 16 * 2**20
    return int(min(max(want, 32 * 2**20), V7X_VMEM_BYTES - 8 * 2**20))


def _nbytes(shape, dtype):
    return math.prod(shape) * jnp.dtype(dtype).itemsize


def _in_hbm(*arrays):
    return [pltpu.with_memory_space_constraint(a, pltpu.HBM) for a in arrays]


def _dot(a, b, dims):
    return lax.dot_general(a, b, (dims, ((), ())), preferred_element_type=F32)


def _dot_nn(a, b):
    return _dot(a, b, ((1,), (0,)))


def _dot_nt(a, b):
    return _dot(a, b, ((1,), (1,)))


def _dot_tn(a, b):
    return _dot(a, b, ((0,), (0,)))


def _sigmoid(z):
    return 1.0 / (1.0 + jnp.exp(-z))


def _split3(x):
    hi = x.astype(BF16)
    r1 = x - hi.astype(F32)
    mid = r1.astype(BF16)
    lo = (r1 - mid.astype(F32)).astype(BF16)
    return hi, mid, lo


def _dot3_nn(x, ones_matrix):
    hi, mid, lo = _split3(x)
    return (_dot_nn(hi, ones_matrix) + _dot_nn(mid, ones_matrix)) + _dot_nn(lo, ones_matrix)


def _rowwise(fn, name, n_rows, tm, row_ins, bcast_ins, row_outs, acc_outs=(), deps=()):
    n_in = len(row_ins) + len(bcast_ins)
    n_ro = len(row_outs)

    def body(*refs):
        res = fn(*[r[...] for r in refs[:n_in]])
        if not isinstance(res, (tuple, list)):
            res = (res,)
        outs = refs[n_in + len(deps):]
        for r, o in zip(res[:n_ro], outs[:n_ro]):
            o[...] = r.astype(o.dtype)
        first = pl.program_id(0) == 0
        for r, o in zip(res[n_ro:], outs[n_ro:]):
            _accumulate(o, r, first)

    in_specs = [pl.BlockSpec((tm, w), lambda i, cb=cb: (i, cb)) for _, w, cb in row_ins]
    in_specs += [pl.BlockSpec(a.shape, lambda i: (0, 0)) for a in bcast_ins]
    in_specs += [pl.BlockSpec(memory_space=pl.ANY)] * len(deps)
    out_specs = [pl.BlockSpec((tm, w), lambda i: (i, 0)) for w, _ in row_outs]
    out_specs += [pl.BlockSpec((1, w), lambda i: (0, 0)) for w in acc_outs]
    out_shape = [pltpu.HBM((n_rows, w), dt) for w, dt in row_outs]
    out_shape += [pltpu.HBM((1, w), F32) for w in acc_outs]
    blk = sum(_nbytes((tm, w), a.dtype) for a, w, _ in row_ins) + sum(_nbytes((tm, w), dt) for w, dt in row_outs)
    return pl.pallas_call(
        body, name=name, grid=(n_rows // tm,), in_specs=in_specs, out_specs=out_specs, out_shape=out_shape,
        compiler_params=pltpu.CompilerParams(
            dimension_semantics=("arbitrary" if acc_outs else "parallel",), vmem_limit_bytes=_vmem_limit(3 * blk)),
    )(*_in_hbm(*[a for a, _, _ in row_ins], *bcast_ins), *deps)


def _accumulate(o_ref, part, first):
    @pl.when(first)
    def _():
        o_ref[...] = part

    @pl.when(jnp.logical_not(first))
    def _():
        o_ref[...] += part


_MM_DIMS = {"nn": ((1,), (0,)), "nt": ((1,), (1,)), "tn": ((0,), (0,))}


def _matmul(pairs, mode, out_dtype, name, tm, tn, tk, deps=(), staged=True):
    a0, b0 = pairs[0]
    if mode == "tn":
        kk, m = a0.shape
    else:
        m, kk = a0.shape
    n = b0.shape[0] if mode == "nt" else b0.shape[1]
    assert m % tm == 0 and n % tn == 0 and kk % tk == 0, (name, m, n, kk)
    nk = kk // tk
    n_pairs = len(pairs)
    dims = _MM_DIMS[mode]
    n_in = 2 * n_pairs + len(deps)

    def body(*refs):
        o_ref = refs[n_in]
        part = None
        for p in range(n_pairs):
            d = _dot(refs[2 * p][...].astype(BF16), refs[2 * p + 1][...].astype(BF16), dims)
            part = d if part is None else part + d
        if nk == 1:
            o_ref[...] = part.astype(o_ref.dtype)
            return
        acc = refs[n_in + 1]
        k = pl.program_id(2)

        @pl.when(k == 0)
        def _():
            acc[...] = part

        @pl.when(k > 0)
        def _():
            acc[...] += part

        @pl.when(k == nk - 1)
        def _():
            o_ref[...] = acc[...].astype(o_ref.dtype)

    if mode == "tn":
        a_spec = pl.BlockSpec((tk, tm), lambda i, j, k: (k, i))
    else:
        a_spec = pl.BlockSpec((tm, tk), lambda i, j, k: (i, k))
    if mode == "nt":
        b_spec = pl.BlockSpec((tn, tk), lambda i, j, k: (j, k))
    else:
        b_spec = pl.BlockSpec((tk, tn), lambda i, j, k: (k, j))
    blk = sum(_nbytes((tm, tk), a.dtype) + _nbytes((tk, tn), b.dtype) for a, b in pairs) + 2 * _nbytes((tm, tn), F32)
    flat = [a for pair in pairs for a in pair]
    return pl.pallas_call(
        body, name=name, grid=(m // tm, n // tn, nk),
        in_specs=[a_spec, b_spec] * n_pairs + [pl.BlockSpec(memory_space=pl.ANY)] * len(deps),
        out_specs=pl.BlockSpec((tm, tn), lambda i, j, k: (i, j)),
        out_shape=pltpu.HBM((m, n), out_dtype),
        scratch_shapes=[] if nk == 1 else [pltpu.VMEM((tm, tn), F32)],
        compiler_params=pltpu.CompilerParams(
            dimension_semantics=("parallel", "parallel", "arbitrary"), vmem_limit_bytes=_vmem_limit(blk)),
    )(*(flat if staged else _in_hbm(*flat)), *deps)


def _matmul_rowwise(pairs, fn, name, tm, row_ins, bcast_ins, row_outs, acc_outs=(), deps=()):
    m = pairs[0][0].shape[0]
    n_mm, n_in = 2 * len(pairs), len(row_ins) + len(bcast_ins)
    n_ro = len(row_outs)

    def body(*refs):
        prod = None
        for p in range(len(pairs)):
            part = _dot_nn(refs[2 * p][...].astype(BF16), refs[2 * p + 1][...].astype(BF16))
            prod = part if prod is None else prod + part
        res = fn(prod, *[r[...] for r in refs[n_mm:n_mm + n_in]])
        outs = refs[n_mm + n_in + len(deps):]
        for r, o in zip(res[:n_ro], outs[:n_ro]):
            o[...] = r.astype(o.dtype)
        first = pl.program_id(0) == 0
        for r, o in zip(res[n_ro:], outs[n_ro:]):
            _accumulate(o, r, first)

    in_specs = []
    for a, b in pairs:
        in_specs += [pl.BlockSpec((tm, a.shape[1]), lambda i: (i, 0)),
                     pl.BlockSpec(b.shape, lambda i: (0, 0), pipeline_mode=pl.Buffered(1))]
    in_specs += [pl.BlockSpec((tm, w), lambda i, cb=cb: (i, cb)) for _, w, cb in row_ins]
    in_specs += [pl.BlockSpec(a.shape, lambda i: (0, 0)) for a in bcast_ins]
    in_specs += [_ANY] * len(deps)
    out_specs = [pl.BlockSpec((tm, w), lambda i: (i, 0)) for w, _ in row_outs]
    out_specs += [pl.BlockSpec((1, w), lambda i: (0, 0)) for w in acc_outs]
    out_shape = [pltpu.HBM((m, w), dt) for w, dt in row_outs]
    out_shape += [pltpu.HBM((1, w), F32) for w in acc_outs]
    blk = sum(_nbytes((tm, a.shape[1]), a.dtype) + _nbytes(b.shape, b.dtype) // 2 for a, b in pairs)
    blk += sum(_nbytes((tm, w), a.dtype) for a, w, _ in row_ins) + sum(_nbytes((tm, w), dt) for w, dt in row_outs)
    return pl.pallas_call(
        body, name=name, grid=(m // tm,), in_specs=in_specs, out_specs=out_specs, out_shape=out_shape,
        compiler_params=pltpu.CompilerParams(dimension_semantics=("arbitrary",), vmem_limit_bytes=_vmem_limit(blk)),
    )(*[a for pair in pairs for a in pair], *[a for a, _, _ in row_ins], *bcast_ins, *deps)


def _rms_scale(x):
    return lax.rsqrt(jnp.mean(x * x, axis=-1, keepdims=True) + EPS)


def _rms_bwd(xin, dyn, g):
    r = _rms_scale(xin)
    u = dyn * g
    dx = r * u - xin * (r * r * r) * jnp.mean(u * xin, axis=-1, keepdims=True)
    dg = jnp.sum(dyn * xin * r, axis=0, keepdims=True)
    return dx, dg


def _mesh_pos():
    return lax.axis_index("x"), lax.axis_index("y"), lax.axis_index("c")


def _all_gather(xs, name, deps=()):
    n = len(xs)

    def body(*refs):
        x_refs, out_refs = refs[:n], refs[n + len(deps):2 * n + len(deps)]
        send_sems, recv_sems, local_sems = refs[2 * n + len(deps):]
        mx, my, mc = _mesh_pos()
        me, sib = (mx, my, mc), (mx, my, 1 - mc)
        chips = [(1 - mx, my), (mx, 1 - my), (1 - mx, 1 - my)]

        def slot(a, dev):
            px, py, pc = dev
            return out_refs[a].at[4 * px + 2 * py + pc]

        def copy(k, a, block, to, src=None):
            return pltpu.make_async_remote_copy(
                src_ref=slot(a, block) if src is None else src, dst_ref=slot(a, block),
                send_sem=send_sems.at[a * 7 + k], recv_sem=recv_sems.at[a * 7 + k],
                device_id=to, device_id_type=MESH)

        mine = [pltpu.make_async_copy(x_refs[a], slot(a, me), local_sems.at[a]) for a in range(n)]
        for cp in mine:
            cp.start()
        first = []
        for a in range(n):
            first.append(copy(0, a, me, sib, x_refs[a]))
            first += [copy(1 + j, a, me, (*chip, mc), x_refs[a]) for j, chip in enumerate(chips)]
        for cp in first:
            cp.start()
        passed = []
        for a in range(n):
            for j, chip in enumerate(chips):
                copy(1 + j, a, (*chip, mc), me).wait_recv()
                fwd = copy(4 + j, a, (*chip, mc), sib)
                fwd.start()
                passed.append(fwd)
        for a in range(n):
            copy(0, a, sib, me).wait_recv()
            for j, chip in enumerate(chips):
                copy(4 + j, a, (*chip, 1 - mc), me).wait_recv()
        for cp in first + passed:
            cp.wait_send()
        for cp in mine:
            cp.wait()

    hbm = pl.BlockSpec(memory_space=pl.ANY)
    return pl.pallas_call(
        body, name=name,
        out_shape=[pltpu.HBM((N_DEV,) + x.shape, x.dtype) for x in xs],
        in_specs=[hbm] * (n + len(deps)), out_specs=[hbm] * n,
        scratch_shapes=[pltpu.SemaphoreType.DMA((7 * n,)), pltpu.SemaphoreType.DMA((7 * n,)),
                        pltpu.SemaphoreType.DMA((n,))],
    )(*xs, *deps)


_HBM = pl.BlockSpec(memory_space=pltpu.HBM)
_SEM = pl.BlockSpec(memory_space=pltpu.SEMAPHORE)
_ANY = pl.BlockSpec(memory_space=pl.ANY)
_DATAFLOW = pltpu.SideEffectType.DATAFLOW_SIDE_EFFECTING


def _flip_peer(flip):
    mx, my, mc = _mesh_pos()
    return (1 - mx if flip & 2 else mx, 1 - my if flip & 1 else my, mc)


def _remote(src, dst, send_sems, recv_sems, k, peer):
    return pltpu.make_async_remote_copy(src_ref=src, dst_ref=dst, send_sem=send_sems.at[k], recv_sem=recv_sems.at[k],
                                        device_id=peer, device_id_type=MESH)


def _gather_chips_copies(srcs, lands, send_sems, recv_sems):
    mx, my, mc = _mesh_pos()
    me = 4 * mx + 2 * my + mc
    return [_remote(srcs[a], lands[a].at[me], send_sems, recv_sems, 3 * a + flip - 1, _flip_peer(flip))
            for a in range(len(srcs)) for flip in (1, 2, 3)]


def _gather_sibling_copies(srcs, lands, send_sems, recv_sems):
    mx, my, mc = _mesh_pos()
    return [_remote(lands[a].at[2 * k + mc], lands[a].at[2 * k + mc], send_sems, recv_sems, 4 * a + k, (mx, my, 1 - mc))
            for a in range(len(lands)) for k in range(4)]


def _scatter_sibling_copies(srcs, lands, send_sems, recv_sems):
    mx, my, mc = _mesh_pos()
    return [_remote(srcs[a].at[k, 1 - mc], lands[a].at[k], send_sems, recv_sems, 4 * a + k, (mx, my, 1 - mc))
            for a in range(len(srcs)) for k in range(4)]


def _scatter_chips_copies(srcs, lands, send_sems, recv_sems):
    mx, my, _ = _mesh_pos()
    k0 = 2 * mx + my
    return [_remote(srcs[a].at[jnp.bitwise_xor(k0, flip)], lands[a].at[flip - 1], send_sems, recv_sems,
                    3 * a + flip - 1, _flip_peer(flip))
            for a in range(len(srcs)) for flip in (1, 2, 3)]


class _Exchange:
    def __init__(self, copies, n_src, send_sems, recv_sems, thru, token):
        self.copies, self.n_src, self.send_sems, self.recv_sems, self.thru, self.token = (
            copies, n_src, send_sems, recv_sems, thru, token)


def _exchange_start(name, copies, srcs, lands, n_copies, after=()):
    bufs = list(srcs) + list(lands)
    nb, ns = len(bufs), len(srcs)

    def body(*refs):
        send_sems, recv_sems = refs[nb + len(after)], refs[nb + len(after) + 1]
        for cp in copies(refs[:ns], refs[ns:nb], send_sems, recv_sems):
            cp.start()
        refs[-1][...] = jnp.zeros_like(refs[-1])

    out = pl.pallas_call(
        body, name=name,
        out_shape=(pltpu.SemaphoreType.DMA((n_copies,)), pltpu.SemaphoreType.DMA((n_copies,)),
                   *[pltpu.HBM(b.shape, b.dtype) for b in bufs], pltpu.HBM((SUBLANES, LANES), F32)),
        in_specs=[_HBM] * nb + [_ANY] * len(after),
        out_specs=(_SEM, _SEM, *[_HBM] * nb, pl.BlockSpec(memory_space=pltpu.VMEM)),
        input_output_aliases={i: 2 + i for i in range(nb)},
        compiler_params=pltpu.CompilerParams(has_side_effects=_DATAFLOW),
    )(*[pltpu.with_memory_space_constraint(b, pltpu.HBM) for b in bufs], *after)
    return _Exchange(copies, ns, out[0], out[1], list(out[2:2 + nb]), out[-1])


def _exchange_wait(name, ex, after):
    nb, ns = len(ex.thru), ex.n_src

    def body(*refs):
        for cp in ex.copies(refs[:ns], refs[ns:nb], refs[nb], refs[nb + 1]):
            cp.wait_send()
            cp.wait_recv()

    out = pl.pallas_call(
        body, name=name, out_shape=tuple(pltpu.HBM(b.shape, b.dtype) for b in ex.thru),
        in_specs=[_HBM] * nb + [_SEM, _SEM] + [_ANY] * len(after), out_specs=tuple([_HBM] * nb),
        input_output_aliases={i: i for i in range(nb)},
        compiler_params=pltpu.CompilerParams(has_side_effects=_DATAFLOW),
    )(*ex.thru, ex.send_sems, ex.recv_sems, *after)
    return list(out[:ns]), list(out[ns:])


def _halves(ref):
    half = ref.shape[1] // 2
    assert half % LANES == 0, ref.shape
    return ref.at[:, pl.ds(0, half)], ref.at[:, pl.ds(half, half)]


def _gather_two_route_copies(srcs, lands, send_sems, recv_sems):
    mx, my, mc = _mesh_pos()
    me = 4 * mx + 2 * my + mc
    return [_remote(_halves(srcs[a])[h], _halves(lands[a].at[me])[h], send_sems, recv_sems, 4 * a + i, _flip_peer(flip))
            for a in range(len(srcs)) for i, (flip, h) in enumerate(((2, 0), (1, 1), (2, 1), (1, 0)))]


def _to_sibling(land, flip, h, send_sems, recv_sems, k):
    mx, my, mc = _mesh_pos()
    part = _halves(land.at[2 * jnp.bitwise_xor(2 * mx + my, flip) + mc])[h]
    return _remote(part, part, send_sems, recv_sems, k, (mx, my, 1 - mc))


def _second_hop(land, send_sems, recv_sems, k):
    mx, my, mc = _mesh_pos()
    k0 = 2 * mx + my
    from_y = _halves(land.at[2 * jnp.bitwise_xor(k0, 1) + mc])[1]
    from_x = _halves(land.at[2 * jnp.bitwise_xor(k0, 2) + mc])[0]
    return (_remote(from_y, from_y, send_sems, recv_sems, k, _flip_peer(2)),
            _remote(from_x, from_x, send_sems, recv_sems, k + 1, _flip_peer(1)))


def _relay_call(name, body, bufs, sems, n_new, after):
    nb, n_in = len(bufs), len(bufs) + len(sems) + len(after)

    def call_body(*refs):
        body(refs[:nb], refs[nb:nb + len(sems)], refs[n_in], refs[n_in + 1])
        refs[-1][...] = jnp.zeros_like(refs[-1])

    out = pl.pallas_call(
        call_body, name=name,
        out_shape=(pltpu.SemaphoreType.DMA((n_new,)), pltpu.SemaphoreType.DMA((n_new,)),
                   *[pltpu.HBM(b.shape, b.dtype) for b in bufs], pltpu.HBM((SUBLANES, LANES), F32)),
        in_specs=[_HBM] * nb + [_SEM] * len(sems) + [_ANY] * len(after),
        out_specs=(_SEM, _SEM, *[_HBM] * nb, pl.BlockSpec(memory_space=pltpu.VMEM)),
        input_output_aliases={i: 2 + i for i in range(nb)},
        compiler_params=pltpu.CompilerParams(has_side_effects=_DATAFLOW),
    )(*bufs, *sems, *after)
    return out[0], out[1], list(out[2:2 + nb]), out[-1]


def _gather_relay(ex, tag, after):
    ns = ex.n_src
    n = len(ex.thru) - ns
    assert n == 1

    def first(bufs, sems, send, recv):
        land = bufs[ns]
        x_first, y_second, x_second, y_first = ex.copies(bufs[:ns], bufs[ns:], *sems)
        to_x, to_y = _second_hop(land, send, recv, 8)
        for h in (0, 1):
            _to_sibling(land, 0, h, send, recv, h).start()
        x_first.wait_recv()
        to_y.start()
        _to_sibling(land, 2, 0, send, recv, 4).start()
        y_second.wait_recv()
        to_x.start()
        _to_sibling(land, 1, 1, send, recv, 3).start()
        x_second.wait_recv()
        _to_sibling(land, 2, 1, send, recv, 5).start()
        y_first.wait_recv()
        _to_sibling(land, 1, 0, send, recv, 2).start()
        for cp in (x_first, y_second, x_second, y_first):
            cp.wait_send()

    def second(bufs, sems, send, recv):
        to_x, to_y = _second_hop(bufs[0], *sems, 8)
        to_y.wait_recv()
        _to_sibling(bufs[0], 3, 0, send, recv, 0).start()
        to_x.wait_recv()
        _to_sibling(bufs[0], 3, 1, send, recv, 1).start()
        to_x.wait_send()
        to_y.wait_send()

    def last(bufs, sems, send, recv):
        for flip in range(4):
            for h in (0, 1):
                s, r, k = (sems[2], sems[3], h) if flip == 3 else (sems[0], sems[1], 2 * flip + h)
                cp = _to_sibling(bufs[0], flip, h, s, r, k)
                cp.wait_send()
                cp.wait_recv()

    send1, recv1, bufs, _ = _relay_call(f"{tag}_chips_relay", first, ex.thru, [ex.send_sems, ex.recv_sems], 10, after)
    send2, recv2, lands, token = _relay_call(f"{tag}_diagonal_relay", second, bufs[ns:], [send1, recv1], 2, [])
    return token, lambda after_last: _relay_call(f"{tag}_sibling_wait", last, lands, [send1, recv1, send2, recv2], 1,
                                                 after_last)[2]


def _col_tile(r, c):
    return next(t for t in (1024, 512, 256, 128) if c % t == 0 and (r * t * 4 <= 2**20 or t == 128))


def _add_sibling(g4, recv, pos, name):
    _, _, r, c = g4.shape
    tc = _col_tile(r, c)

    def body(pos_ref, g_ref, r_ref, o16_ref, mine_ref):
        s = g_ref[0, 0] + r_ref[0]
        o16_ref[0] = s.astype(BF16)

        @pl.when(pl.program_id(1) == pos_ref[1])
        def _():
            mine_ref[...] = s

    slot = pl.BlockSpec((1, r, tc), lambda j, k, pos_ref: (k, 0, j))
    return pl.pallas_call(
        body, name=name,
        out_shape=[pltpu.HBM((4, r, c), BF16), pltpu.HBM((r, c), F32)],
        grid_spec=pltpu.PrefetchScalarGridSpec(
            num_scalar_prefetch=1, grid=(c // tc, 4),
            in_specs=[pl.BlockSpec((1, 1, r, tc), lambda j, k, pos_ref: (k, pos_ref[0], 0, j)), slot],
            out_specs=[slot, pl.BlockSpec((r, tc), lambda j, k, pos_ref: (0, j))]),
        compiler_params=pltpu.CompilerParams(dimension_semantics=("parallel", "arbitrary")),
    )(pos, *_in_hbm(g4, recv))


class _ReduceScatter:
    def __init__(self, tag, grads_t, pos):
        self.tag, self.pos, self.names = tag, pos, list(grads_t)
        g4s = [g.reshape(4, 2, g.size // (N_DEV * g.shape[-1]), g.shape[-1]) for g in grads_t.values()]
        lands = [lax.empty((4,) + g.shape[2:], F32) for g in g4s]
        self.ex = _exchange_start(f"rs_{tag}_sibling_start", _scatter_sibling_copies, g4s, lands, 4 * len(g4s))
        self.token = self.ex.token

    def start_chips(self, after):
        g4s, from_sibling = _exchange_wait(f"rs_{self.tag}_sibling_wait", self.ex, after)
        parts = [_add_sibling(g4, rv, self.pos, f"rs_add_sibling_{k}")
                 for k, g4, rv in zip(self.names, g4s, from_sibling)]
        self.mine = [mine for _, mine in parts]
        p16s = [p16 for p16, _ in parts]
        lands = [lax.empty((3,) + p.shape[1:], BF16) for p in p16s]
        self.ex = _exchange_start(f"rs_{self.tag}_chips_start", _scatter_chips_copies, p16s, lands, 3 * len(p16s))
        self.token = self.ex.token

    def finish(self, after):
        _, from_chips = _exchange_wait(f"rs_{self.tag}_chips_wait", self.ex, after)
        return dict(zip(self.names, zip(self.mine, from_chips)))


def _rope_tables():
    positions = np.arange(SEQ, dtype=np.float32)
    inv_freq = np.power(np.float32(ROPE_THETA), -np.arange(0, ROPE_DIM, 2, dtype=np.float32) / np.float32(ROPE_DIM))
    ang = (positions[:, None] * inv_freq[None, :]).astype(np.float32)
    cos, sin = np.cos(ang).astype(np.float32), np.sin(ang).astype(np.float32)
    ones = np.ones((SEQ, HEAD_DIM - ROPE_DIM), np.float32)
    zeros8 = np.zeros((SEQ, ROPE_HALF), np.float32)
    zeros = np.zeros((SEQ, HEAD_DIM - ROPE_DIM), np.float32)
    c_head = np.concatenate([cos, cos, ones], axis=1)
    s1_head = np.concatenate([-sin, zeros8, zeros], axis=1)
    s2_head = np.concatenate([zeros8, sin, zeros], axis=1)
    return tuple(jnp.asarray(np.concatenate([t, t], axis=1)) for t in (c_head, s1_head, s2_head))


def _rope_apply(x, c, s1, s2):
    w = x.shape[1]
    return x * c + pltpu.roll(x, w - ROPE_HALF, 1) * s1 + pltpu.roll(x, ROPE_HALF, 1) * s2


def _rope_apply_t(dy, c, s1, s2):
    w = dy.shape[1]
    return dy * c + pltpu.roll(dy * s1, ROPE_HALF, 1) + pltpu.roll(dy * s2, w - ROPE_HALF, 1)


def _dil_prev_limit(has_prev):
    return jnp.where(has_prev, 0, BLOCK)


def _dil_valid(limit):
    row = lax.broadcasted_iota(jnp.int32, (BLOCK, 2 * BLOCK), 0)
    col = lax.broadcasted_iota(jnp.int32, (BLOCK, 2 * BLOCK), 1)
    dist = col - row
    return jnp.logical_and(dist >= jnp.where(col < BLOCK, limit, -BLOCK), dist <= BLOCK)


def _upper_half():
    return lax.broadcasted_iota(jnp.int32, (1, LANES), 1) >= HEAD_DIM


def _stack_heads(x):
    upper = _upper_half()
    return jnp.concatenate([jnp.where(upper, 0, x), jnp.where(upper, x, 0)], axis=0)


def _unstack_heads(y):
    n = y.shape[0] // 2
    return jnp.where(_upper_half(), y[n:], y[:n])


def _head_columns(t):
    return jnp.concatenate([t[:, 0:1], t[:, HEAD_DIM:HEAD_DIM + 1]], axis=0)


def _dil_rows(n, d):
    per = N_BLOCKS // d
    r, lb = n // per, n % per

    def rows(b):
        start = b * (BLOCK * d) + r
        return pl.ds(pl.multiple_of(start, BLOCK), BLOCK) if d == 1 else pl.ds(start, BLOCK, stride=d)

    return rows(lb), rows(jnp.maximum(lb - 1, 0)), lb > 0


def _dil_rotate(q_ref, k_ref, c_ref, s1_ref, s2_ref, q_rot, k_rot):
    tabs = (c_ref[...], s1_ref[...], s2_ref[...])
    q_rot[...] = _rope_apply(q_ref[...], *tabs) * QK_SCALE
    k_rot[...] = _rope_apply(k_ref[...], *tabs)


def _dil_specs():
    def col(base):
        return pl.BlockSpec((SEQ, LANES), lambda p: (0, base // LANES + p))

    table = pl.BlockSpec((SEQ, LANES), lambda p: (0, 0))
    return [col(COL_QA), col(COL_KA), col(COL_VA)], [table] * 3


def _store_columns(blocks, dproj_ref, cols, sem):
    copies = [pltpu.make_async_copy(b, dproj_ref.at[:, pl.ds(pl.multiple_of(c * LANES, LANES), LANES)], sem.at[i])
              for i, (b, c) in enumerate(zip(blocks, cols))]
    for cp in copies:
        cp.start()
    for cp in copies:
        cp.wait()


def _dil_window(d, n, k_rot, v_ref):
    rows, prev, has_prev = _dil_rows(n, d)
    kw, vw = k_rot[rows, :].astype(BF16), v_ref[rows, :].astype(BF16)
    if d == N_BLOCKS:
        row = lax.broadcasted_iota(jnp.int32, (BLOCK, BLOCK), 0)
        valid = lax.broadcasted_iota(jnp.int32, (BLOCK, BLOCK), 1) <= row
    else:
        kw = jnp.concatenate([k_rot[prev, :].astype(BF16), kw], axis=0)
        vw = jnp.concatenate([v_ref[prev, :].astype(BF16), vw], axis=0)
        valid = _dil_valid(_dil_prev_limit(has_prev))
    return rows, prev, kw, vw, jnp.concatenate([valid, valid], axis=0)


def _dil_fwd(proj, tables):
    def body(q_ref, k_ref, v_ref, c_ref, s1_ref, s2_ref, o_ref, lse_ref, q_rot, k_rot):
        upper = _upper_half()
        _dil_rotate(q_ref, k_ref, c_ref, s1_ref, s2_ref, q_rot, k_rot)

        def blocks_of(d):
            def block(n, carry):
                rows, _, kw, vw, valid = _dil_window(d, n, k_rot, v_ref)
                s = jnp.where(valid, _dot_nt(_stack_heads(q_rot[rows, :].astype(BF16)), kw), NEG_INF)
                m = jnp.max(s, axis=-1, keepdims=True)
                p = jnp.exp(s - m)
                den = jnp.sum(p, axis=-1, keepdims=True)
                o_ref[rows, :] = _unstack_heads(_dot_nn((p * (1.0 / den)).astype(BF16), vw))
                lse = m + jnp.log(den)
                lse_ref[rows, :] = jnp.where(upper, lse[BLOCK:], lse[:BLOCK])
                return carry

            lax.fori_loop(0, N_BLOCKS, block, 0, unroll=4)

        for g, d in enumerate(DILATIONS):
            pl.when(pl.program_id(0) // 2 == g)(functools.partial(blocks_of, d))

    qkv, tabs = _dil_specs()
    out = pl.BlockSpec((SEQ, LANES), lambda p: (0, p))
    return pl.pallas_call(
        body, name="dil_attn_fwd", grid=(DIL_WIDTH // LANES,), in_specs=qkv + tabs, out_specs=[out, out],
        out_shape=[pltpu.HBM((SEQ, DIL_WIDTH), F32)] * 2,
        scratch_shapes=[pltpu.VMEM((SEQ, LANES), F32)] * 2,
        compiler_params=pltpu.CompilerParams(dimension_semantics=("parallel",)),
    )(*_in_hbm(proj, proj, proj, *tables))


def _dil_bwd(proj, tables, do, lse, c, dproj, deps=()):
    def body(q_ref, k_ref, v_ref, c_ref, s1_ref, s2_ref, do_ref, lse_ref, cc_ref, dproj_in, *rest):
        dproj_ref, dq_acc, dk_acc, dv_acc, dq_out, dk_out, dv_out, q_rot, k_rot, sem = rest[len(deps):]
        dk_acc[...] = jnp.zeros_like(dk_acc)
        dv_acc[...] = jnp.zeros_like(dv_acc)
        _dil_rotate(q_ref, k_ref, c_ref, s1_ref, s2_ref, q_rot, k_rot)

        def blocks_of(d):
            def block(n, carry):
                rows, prev, kw, vw, valid = _dil_window(d, n, k_rot, v_ref)
                q2 = _stack_heads(q_rot[rows, :].astype(BF16))
                do2 = _stack_heads(do_ref[rows, :].astype(BF16))
                lse_col, c_col = _head_columns(lse_ref[rows, :]), _head_columns(cc_ref[rows, :])
                p = jnp.where(valid, jnp.exp(_dot_nt(q2, kw) - lse_col), 0.0)
                ds = (p * (_dot_nt(do2, vw) + c_col)).astype(BF16)
                dk, dv = _dot_tn(ds, q2), _dot_tn(p.astype(BF16), do2)
                dq_acc[rows, :] = _unstack_heads(_dot_nn(ds, kw)) * QK_SCALE
                if d == N_BLOCKS:
                    dk_acc[rows, :] += dk
                    dv_acc[rows, :] += dv
                else:
                    dk_acc[prev, :] += dk[:BLOCK]
                    dv_acc[prev, :] += dv[:BLOCK]
                    dk_acc[rows, :] += dk[BLOCK:]
                    dv_acc[rows, :] += dv[BLOCK:]
                return carry

            lax.fori_loop(0, N_BLOCKS, block, 0, unroll=4)

        pair = pl.program_id(0)
        for g, d in enumerate(DILATIONS):
            pl.when(pair // 2 == g)(functools.partial(blocks_of, d))
        tabs = (c_ref[...], s1_ref[...], s2_ref[...])
        dq_out[...] = _rope_apply_t(dq_acc[...], *tabs).astype(BF16)
        dk_out[...] = _rope_apply_t(dk_acc[...], *tabs).astype(BF16)
        dv_out[...] = dv_acc[...].astype(BF16)
        _store_columns((dq_out, dk_out, dv_out), dproj_ref,
                       [base // LANES + pair for base in (COL_QA, COL_KA, COL_VA)], sem)

    qkv, tabs = _dil_specs()
    tok = pl.BlockSpec((SEQ, LANES), lambda p: (0, p))
    return pl.pallas_call(
        body, name="dil_attn_bwd", grid=(DIL_WIDTH // LANES,),
        in_specs=qkv + tabs + [tok, tok, tok, _ANY] + [_ANY] * len(deps), out_specs=_ANY,
        out_shape=pltpu.HBM(dproj.shape, dproj.dtype),
        scratch_shapes=[pltpu.VMEM((SEQ, LANES), F32)] * 3 + [pltpu.VMEM((SEQ, LANES), BF16)] * 3
        + [pltpu.VMEM((SEQ, LANES), F32)] * 2 + [pltpu.SemaphoreType.DMA((3,))],
        input_output_aliases={9: 0},
        compiler_params=pltpu.CompilerParams(dimension_semantics=("arbitrary",)),
    )(*_in_hbm(proj, proj, proj, *tables, do, lse, c, dproj), *deps)


def _group_weights(l0, l1, l2):
    m = jnp.maximum(jnp.maximum(l0, l1), l2)
    e0, e1, e2 = jnp.exp(l0 - m), jnp.exp(l1 - m), jnp.exp(l2 - m)
    tot = e0 + e1 + e2
    return e0 / tot, e1 / tot, e2 / tot


def _dil_combine(o, lse, deps=()):
    def fn(o0, o1, o2, l0, l1, l2):
        w0, w1, w2 = _group_weights(l0, l1, l2)
        return w0 * o0 + w1 * o1 + w2 * o2

    w = DIL_OUT_WIDTH
    return _rowwise(fn, "dil_combine", SEQ, 512, [(o, w, g) for g in range(3)] + [(lse, w, g) for g in range(3)], [],
                    [(w, F32)], deps=deps)[0]


def _dil_combine_bwd(d_out, o, lse, deps=()):
    w = DIL_OUT_WIDTH

    def fn(d, o0, o1, o2, l0, l1, l2):
        row = lax.broadcasted_iota(jnp.int32, (w, w), 0) // HEAD_DIM
        col = lax.broadcasted_iota(jnp.int32, (w, w), 1) // HEAD_DIM
        same_head = jnp.where(row == col, 1.0, 0.0).astype(BF16)
        ws = _group_weights(l0, l1, l2)
        dws = [_dot3_nn(d * og, same_head) for og in (o0, o1, o2)]
        mean = ws[0] * dws[0] + ws[1] * dws[1] + ws[2] * dws[2]
        return jnp.concatenate([wg * d for wg in ws], axis=1), jnp.concatenate([-wg * mean for wg in ws], axis=1)

    return _rowwise(fn, "dil_combine_bwd", SEQ, 256,
                    [(d_out, w, 0)] + [(o, w, g) for g in range(3)] + [(lse, w, g) for g in range(3)], [],
                    [(DIL_WIDTH, F32)] * 2, deps=deps)


def _log1p(e):
    u = 1.0 + e
    return jnp.where(u == 1.0, e, jnp.log(u) * (e / (u - 1.0)))


def _fox_gate(proj, b_pad, deps=()):
    def body(f_ref, b_ref, *rest):
        o_ref = rest[-1]
        z = f_ref[...] + b_ref[...]
        logf = (jnp.minimum(z, 0.0) - _log1p(jnp.exp(-jnp.abs(z)))).T[:F_ROWS]
        row = lax.broadcasted_iota(jnp.int32, (BLOCK, BLOCK), 0)
        col = lax.broadcasted_iota(jnp.int32, (BLOCK, BLOCK), 1)
        before = jnp.where(row <= col, 1.0, 0.0).astype(BF16)
        carry = jnp.zeros((F_ROWS, 1), F32)
        for blk in range(N_BLOCKS):
            run = _dot3_nn(logf[:, blk * BLOCK:(blk + 1) * BLOCK], before) + carry
            o_ref[:, blk * BLOCK:(blk + 1) * BLOCK] = run
            carry = run[:, BLOCK - 1:BLOCK]

    return pl.pallas_call(
        body, name="fox_gate", grid=(1,),
        in_specs=[pl.BlockSpec((SEQ, LANES), lambda i: (0, COL_F // LANES)), pl.BlockSpec((1, LANES), lambda i: (0, 0))]
        + [_ANY] * len(deps),
        out_specs=pl.BlockSpec((F_ROWS, SEQ), lambda i: (0, 0)),
        out_shape=pltpu.HBM((F_ROWS, SEQ), F32),
    )(*_in_hbm(proj, b_pad), *deps)


def _fox_gate_bwd(d_cum, proj, b_pad, dproj):
    def body(d_ref, f_ref, b_ref, dproj_ref, dz_ref, db_ref):
        row = lax.broadcasted_iota(jnp.int32, (BLOCK, BLOCK), 0)
        col = lax.broadcasted_iota(jnp.int32, (BLOCK, BLOCK), 1)
        after = jnp.where(row >= col, 1.0, 0.0).astype(BF16)
        carry = jnp.zeros((F_ROWS, 1), F32)
        parts = [None] * N_BLOCKS
        for blk in reversed(range(N_BLOCKS)):
            run = _dot3_nn(d_ref[:, blk * BLOCK:(blk + 1) * BLOCK], after) + carry
            parts[blk] = run
            carry = run[:, 0:1]
        dlogf = jnp.concatenate(parts, axis=1)
        dlogf = jnp.concatenate([dlogf, jnp.zeros((LANES - F_ROWS, SEQ), F32)], axis=0).T
        dz = dlogf * _sigmoid(-(f_ref[...] + b_ref[...]))
        dz_ref[...] = dz.astype(BF16)
        db_ref[...] = jnp.sum(dz, axis=0, keepdims=True)

    f_cols = pl.BlockSpec((SEQ, LANES), lambda i: (0, COL_F // LANES))
    return pl.pallas_call(
        body, name="fox_gate_bwd", grid=(1,),
        in_specs=[pl.BlockSpec((F_ROWS, SEQ), lambda i: (0, 0)), f_cols, pl.BlockSpec((1, LANES), lambda i: (0, 0)), _ANY],
        out_specs=[f_cols, pl.BlockSpec((1, LANES), lambda i: (0, 0))],
        out_shape=[pltpu.HBM(dproj.shape, dproj.dtype), pltpu.HBM((1, LANES), F32)],
        input_output_aliases={3: 0},
    )(*_in_hbm(d_cum, proj, b_pad, dproj))


FOX_TILE = 256
FOX_TILES = SEQ // FOX_TILE


def _row_to_col(row):
    n = row.shape[1]
    eye = lax.broadcasted_iota(jnp.int32, (n, n), 0) == lax.broadcasted_iota(jnp.int32, (n, n), 1)
    return jnp.sum(jnp.where(eye, row, 0.0), axis=1, keepdims=True)


def _fox_bias(f_row, i):
    t = FOX_TILE
    ext = (i + 1) * t
    bias = _row_to_col(f_row[:, i * t:(i + 1) * t]) - f_row[:, :ext]
    row = lax.broadcasted_iota(jnp.int32, (t, ext), 0) + i * t
    col = lax.broadcasted_iota(jnp.int32, (t, ext), 1)
    return bias, col <= row


def _fox_specs():
    qkv = [pl.BlockSpec((SEQ, LANES), lambda p, base=base: (0, base // LANES + p)) for base in (COL_QB, COL_KB, COL_VB)]
    return qkv, pl.BlockSpec((F_ROWS, SEQ), lambda p: (0, 0))


def _fox_fwd(proj, f_rows):
    t = FOX_TILE

    def body(q_ref, k_ref, v_ref, f_ref, o_ref, lse_ref):
        pair = pl.program_id(0)
        upper = _upper_half()
        k16, v16 = k_ref[...].astype(BF16), v_ref[...].astype(BF16)
        f_row = [f_ref[pl.ds(2 * pair + e, 1), :] for e in range(2)]
        for i in range(FOX_TILES):
            ext = (i + 1) * t
            q_tile = (q_ref[i * t:(i + 1) * t, :] * QK_SCALE).astype(BF16)
            s2 = _dot_nt(_stack_heads(q_tile), k16[:ext])
            pns, lses = [], []
            for e in range(2):
                bias, causal = _fox_bias(f_row[e], i)
                s = jnp.where(causal, s2[e * t:(e + 1) * t] + bias, NEG_INF)
                m = jnp.max(s, axis=-1, keepdims=True)
                p = jnp.exp(s - m)
                den = jnp.sum(p, axis=-1, keepdims=True)
                pns.append((p * (1.0 / den)).astype(BF16))
                lses.append(m + jnp.log(den))
            o_ref[i * t:(i + 1) * t, :] = _unstack_heads(_dot_nn(jnp.concatenate(pns, axis=0), v16[:ext]))
            lse_ref[i * t:(i + 1) * t, :] = jnp.where(upper, lses[1], lses[0])

    qkv, f_spec = _fox_specs()
    tok = pl.BlockSpec((SEQ, LANES), lambda p: (0, p))
    return pl.pallas_call(
        body, name="fox_attn_fwd", grid=(FOX_WIDTH // LANES,),
        in_specs=qkv + [f_spec], out_specs=[tok, tok],
        out_shape=[pltpu.HBM((SEQ, FOX_WIDTH), F32)] * 2,
        compiler_params=pltpu.CompilerParams(
            dimension_semantics=("parallel",), vmem_limit_bytes=_vmem_limit(8 * t * SEQ * 4)),
    )(*_in_hbm(proj, proj, proj, f_rows))


def _fox_bwd(proj, do, lse, f_rows, dproj):
    t = FOX_TILE

    def body(q_ref, k_ref, v_ref, f_ref, do_ref, lse_ref, dproj_in, dproj_ref, df_ref, dk_acc, dv_acc,
             dq_out, dk_out, dv_out, sem):
        pair = pl.program_id(0)
        upper = _upper_half()
        k16, v16 = k_ref[...].astype(BF16), v_ref[...].astype(BF16)
        f_row = [f_ref[pl.ds(2 * pair + e, 1), :] for e in range(2)]
        dk_acc[...] = jnp.zeros_like(dk_acc)
        dv_acc[...] = jnp.zeros_like(dv_acc)
        df_ref[...] = jnp.zeros_like(df_ref)
        for i in range(FOX_TILES):
            ext = (i + 1) * t
            q_tile = (q_ref[i * t:(i + 1) * t, :] * QK_SCALE).astype(BF16)
            do_tile = do_ref[i * t:(i + 1) * t, :]
            lse_t = lse_ref[i * t:(i + 1) * t, :]
            q2, do2 = _stack_heads(q_tile), _stack_heads(do_tile)
            s2, dp2 = _dot_nt(q2, k16[:ext]), _dot_nt(do2, v16[:ext])
            ps, dss = [], []
            for e in range(2):
                bias, causal = _fox_bias(f_row[e], i)
                s = s2[e * t:(e + 1) * t] + bias
                p = jnp.where(causal, jnp.exp(s - lse_t[:, e * HEAD_DIM:e * HEAD_DIM + 1]), 0.0)
                dp = dp2[e * t:(e + 1) * t]
                ds = p * (dp - jnp.sum(p * dp, axis=-1, keepdims=True))
                df_ref[0, e:e + 1, :ext] -= jnp.sum(ds, axis=0, keepdims=True)
                ps.append(p.astype(BF16))
                dss.append(ds.astype(BF16))
            ds2, p2 = jnp.concatenate(dss, axis=0), jnp.concatenate(ps, axis=0)
            dq_out[i * t:(i + 1) * t, :] = (_unstack_heads(_dot_nn(ds2, k16[:ext])) * QK_SCALE).astype(BF16)
            dk_acc[:ext, :] += _dot_tn(ds2, q2)
            dv_acc[:ext, :] += _dot_tn(p2, do2)
        dk_out[...] = dk_acc[...].astype(BF16)
        dv_out[...] = dv_acc[...].astype(BF16)
        _store_columns((dq_out, dk_out, dv_out), dproj_ref, [base // LANES + pair for base in (COL_QB, COL_KB, COL_VB)],
                       sem)

    qkv, f_spec = _fox_specs()
    tok = pl.BlockSpec((SEQ, LANES), lambda p: (0, p))
    return pl.pallas_call(
        body, name="fox_attn_bwd", grid=(FOX_WIDTH // LANES,),
        in_specs=qkv + [f_spec, tok, tok, _ANY],
        out_specs=[_ANY, pl.BlockSpec((1, SUBLANES, SEQ), lambda p: (p, 0, 0))],
        out_shape=[pltpu.HBM(dproj.shape, dproj.dtype),
                   pltpu.HBM((FOX_WIDTH // LANES, SUBLANES, SEQ), F32)],
        scratch_shapes=[pltpu.VMEM((SEQ, LANES), F32)] * 2 + [pltpu.VMEM((SEQ, LANES), BF16)] * 3
        + [pltpu.SemaphoreType.DMA((3,))],
        input_output_aliases={6: 0},
        compiler_params=pltpu.CompilerParams(
            dimension_semantics=("arbitrary",), vmem_limit_bytes=_vmem_limit(10 * t * SEQ * 4)),
    )(*_in_hbm(proj, proj, proj, f_rows, do, lse, dproj))


MIX_TILE = 256


def _mix_out(out_a, out_b, proj, x, wt_pa, wt_pb, w_out, g_post, g_ffn_pre):
    tm = MIX_TILE

    def body(a_ref, b_ref, ga_ref, gb_ref, x_ref, wpa_ref, wpb_ref, wo_ref, g2_ref, g3_ref,
             merged_ref, mix_ref, x1_ref, h2_ref):
        ya = _dot_nn(a_ref[...].astype(BF16), wpa_ref[...])
        yb = _dot_nn(b_ref[...].astype(BF16), wpb_ref[...])
        merged = (_sigmoid(ga_ref[...]) * ya + _sigmoid(gb_ref[...]) * yb).astype(BF16)
        merged_ref[...] = merged
        mix = _dot_nn(merged, wo_ref[...])
        mix_ref[...] = mix
        x1 = x_ref[...] + mix * _rms_scale(mix) * g2_ref[...]
        x1_ref[...] = x1
        h2_ref[...] = (x1 * _rms_scale(x1) * g3_ref[...]).astype(BF16)

    def rows(w, cb=0):
        return pl.BlockSpec((tm, w), lambda i, cb=cb: (i, cb))

    def whole(a):
        return pl.BlockSpec(a.shape, lambda i: (0, 0))

    d = D_MODEL
    blk = _nbytes((tm, d), F32) * 6 + sum(_nbytes(a.shape, BF16) for a in (wt_pa, wt_pb, w_out))
    return pl.pallas_call(
        body, name="mix_out", grid=(SEQ // tm,),
        in_specs=[rows(DIL_OUT_WIDTH), rows(FOX_WIDTH), rows(d, COL_GA // d), rows(d, COL_GB // d), rows(d),
                  whole(wt_pa), whole(wt_pb), whole(w_out), whole(g_post), whole(g_ffn_pre)],
        out_specs=[rows(d)] * 4,
        out_shape=[pltpu.HBM((SEQ, d), dt) for dt in (BF16, F32, F32, BF16)],
        compiler_params=pltpu.CompilerParams(dimension_semantics=("parallel",), vmem_limit_bytes=_vmem_limit(blk)),
    )(*_in_hbm(out_a, out_b, proj, proj, x, wt_pa, wt_pb, w_out, g_post, g_ffn_pre))


def _mix_out_bwd(dmix, out_a, out_b, proj, wt_pa, wt_pb, w_out, deps=()):
    tm = MIX_TILE

    def body(dm_ref, a_ref, b_ref, ga_ref, gb_ref, wpa_ref, wpb_ref, wo_ref, *rest):
        dproj_ref, dya_ref, dyb_ref, da_ref, db_ref = rest[len(deps):]
        dmerged = _dot_nt(dm_ref[...], wo_ref[...])
        ya = _dot_nn(a_ref[...].astype(BF16), wpa_ref[...])
        yb = _dot_nn(b_ref[...].astype(BF16), wpb_ref[...])
        sa, sb = _sigmoid(ga_ref[...]), _sigmoid(gb_ref[...])
        dproj_ref[:, COL_GA:COL_GA + D_MODEL] = (dmerged * ya * (sa * (1.0 - sa))).astype(BF16)
        dproj_ref[:, COL_GB:COL_GB + D_MODEL] = (dmerged * yb * (sb * (1.0 - sb))).astype(BF16)
        dproj_ref[:, COL_GB + D_MODEL:] = jnp.zeros((tm, COL_QA - COL_GB - D_MODEL), BF16)
        dya = (dmerged * sa).astype(BF16)
        dyb = (dmerged * sb).astype(BF16)
        dya_ref[...] = dya
        dyb_ref[...] = dyb
        da_ref[...] = _dot_nt(dya, wpa_ref[...])
        db_ref[...] = _dot_nt(dyb, wpb_ref[...]).astype(BF16)

    def rows(w, cb=0):
        return pl.BlockSpec((tm, w), lambda i, cb=cb: (i, cb))

    def whole(a):
        return pl.BlockSpec(a.shape, lambda i: (0, 0))

    d = D_MODEL
    blk = _nbytes((tm, d), F32) * 8 + sum(_nbytes(a.shape, BF16) for a in (wt_pa, wt_pb, w_out))
    return pl.pallas_call(
        body, name="mix_out_bwd", grid=(SEQ // tm,),
        in_specs=[rows(d), rows(DIL_OUT_WIDTH), rows(FOX_WIDTH), rows(d, COL_GA // d), rows(d, COL_GB // d),
                  whole(wt_pa), whole(wt_pb), whole(w_out)] + [_ANY] * len(deps),
        out_specs=[rows(COL_QA)] + [rows(d)] * 2 + [rows(DIL_OUT_WIDTH), rows(FOX_WIDTH)],
        out_shape=[pltpu.HBM((SEQ, PROJ_COLS), BF16)] + [pltpu.HBM((SEQ, d), BF16)] * 2
        + [pltpu.HBM((SEQ, DIL_OUT_WIDTH), F32), pltpu.HBM((SEQ, FOX_WIDTH), BF16)],
        compiler_params=pltpu.CompilerParams(dimension_semantics=("parallel",), vmem_limit_bytes=_vmem_limit(blk)),
    )(*_in_hbm(dmix, out_a, out_b, proj, proj, wt_pa, wt_pb, w_out), *deps)


FFN_TM, FFN_TN = 2048, 256


def _ffn_up(h2, wt_gate, wt_up):
    tm, tn = FFN_TM, FFN_TN

    def body(h_ref, wg_ref, wu_ref, gate_ref, up_ref, act_ref):
        for rows in (slice(0, tm // 2), slice(tm // 2, tm)):
            gate = _dot_nt(h_ref[rows, :], wg_ref[...])
            up = _dot_nt(h_ref[rows, :], wu_ref[...])
            gate_ref[rows, :] = gate
            up_ref[rows, :] = up
            act_ref[rows, :] = (gate * _sigmoid(gate) * up).astype(BF16)

    tile = pl.BlockSpec((tm, tn), lambda i, j: (i, j))
    w_spec = pl.BlockSpec((tn, D_MODEL), lambda i, j: (j, 0))
    return pl.pallas_call(
        body, name="ffn_up", grid=(SEQ // tm, D_FF // tn),
        in_specs=[pl.BlockSpec((tm, D_MODEL), lambda i, j: (i, 0)), w_spec, w_spec],
        out_specs=[tile, tile, tile],
        out_shape=[pltpu.HBM((SEQ, D_FF), dt) for dt in (F32, F32, BF16)],
        compiler_params=pltpu.CompilerParams(
            dimension_semantics=("parallel", "parallel"), vmem_limit_bytes=_vmem_limit(8 * 2**20)),
    )(h2, wt_gate, wt_up)


def _ffn_act_bwd(dff, w_down, gate, up):
    tm, tn = FFN_TM, FFN_TN

    def body(d_ref, wd_ref, gate_ref, up_ref, dgate_ref, dup_ref):
        for rows in (slice(0, tm // 2), slice(tm // 2, tm)):
            dact = _dot_nt(d_ref[rows, :], wd_ref[...])
            gate = gate_ref[rows, :]
            sg = _sigmoid(gate)
            dgate_ref[rows, :] = (dact * up_ref[rows, :] * (sg * (1.0 + gate * (1.0 - sg)))).astype(BF16)
            dup_ref[rows, :] = (dact * (gate * sg)).astype(BF16)

    tile = pl.BlockSpec((tm, tn), lambda i, j: (i, j))
    return pl.pallas_call(
        body, name="ffn_act_bwd", grid=(SEQ // tm, D_FF // tn),
        in_specs=[pl.BlockSpec((tm, D_MODEL), lambda i, j: (i, 0)), pl.BlockSpec((tn, D_MODEL), lambda i, j: (j, 0)),
                  tile, tile],
        out_specs=[tile, tile],
        out_shape=[pltpu.HBM((SEQ, D_FF), BF16)] * 2,
        compiler_params=pltpu.CompilerParams(
            dimension_semantics=("parallel", "parallel"), vmem_limit_bytes=_vmem_limit(8 * 2**20)),
    )(dff, w_down, gate, up)


EPILOGUE_TM = 512


def _loss_head(act, w_down, x1, target, g_post):
    def fn(ff, x1, tgt, g):
        r = _rms_scale(ff)
        nrm = ff * r
        err = (x1 + nrm * g) - tgt
        loss = 0.5 * jnp.sum(jnp.mean(err * err, axis=-1, keepdims=True), axis=0, keepdims=True)
        dy = err * (1.0 / D_MODEL)
        u = dy * g
        dff = r * u - ff * (r * r * r) * jnp.mean(u * ff, axis=-1, keepdims=True)
        return dy, dff, jnp.broadcast_to(loss, (1, LANES)), jnp.sum(dy * nrm, axis=0, keepdims=True)

    d = D_MODEL
    return _matmul_rowwise([(act, w_down)], fn, "ffn_down_loss", EPILOGUE_TM, [(x1, d, 0), (target, d, 0)], [g_post],
                           [(d, F32), (d, BF16)], [LANES, d])


def _post_ffn_bwd(dgate, wt_gate, dup, wt_up, x1, dy, mix, g_ffn_pre, g_mix_post, deps=()):
    def fn(dh2, x1, dy, mix, g3, g2):
        dx, dg3 = _rms_bwd(x1, dh2, g3)
        dx1 = dy + dx
        dmix, dg2 = _rms_bwd(mix, dx1, g2)
        return dx1, dmix, dg3, dg2

    d = D_MODEL
    return _matmul_rowwise([(dgate, wt_gate), (dup, wt_up)], fn, "ffn_up_bwd", EPILOGUE_TM,
                           [(x1, d, 0), (dy, d, 0), (mix, d, 0)], [g_ffn_pre, g_mix_post],
                           [(d, F32), (d, BF16)], [d, d], deps=deps)


def _input_bwd(dproj, wt_r, x, dx1, g_pre, deps=()):
    def fn(dh, x, dx1, g):
        dx, dg = _rms_bwd(x, dh, g)
        return dx1 + dx, dg

    d = D_MODEL
    return _matmul_rowwise([(dproj, wt_r)], fn, "in_proj_bwd", EPILOGUE_TM, [(x, d, 0), (dx1, d, 0)], [g_pre],
                           [(d, F32)], [d], deps=deps)


def _adam_math(w, g, m, v):
    m = ADAM_B1 * m + (1.0 - ADAM_B1) * g
    v = ADAM_B2 * v + (1.0 - ADAM_B2) * (g * g)
    m_hat = m / (1.0 - ADAM_B1 ** ADAM_STEP)
    v_hat = v / (1.0 - ADAM_B2 ** ADAM_STEP)
    delta = -ADAM_LR * (m_hat / (jnp.sqrt(v_hat) + ADAM_EPS) + ADAM_WD * w)
    return delta, m, v


def _adam(w, mine, recv, m, v, name):
    r, c = w.shape
    tc = _col_tile(r, c)

    def body(w_ref, p_ref, r_ref, m_ref, v_ref, g_ref, d_ref, nm_ref, nv_ref):
        g = ((p_ref[...] + r_ref[0].astype(F32)) + r_ref[1].astype(F32)) + r_ref[2].astype(F32)
        g_ref[...] = g
        d_ref[...], nm_ref[...], nv_ref[...] = _adam_math(w_ref[...], g, m_ref[...], v_ref[...])

    spec = pl.BlockSpec((r, tc), lambda j: (0, j))
    return pl.pallas_call(
        body, name=name, grid=(c // tc,),
        in_specs=[spec, spec, pl.BlockSpec((3, r, tc), lambda j: (0, 0, j)), spec, spec], out_specs=[spec] * 4,
        out_shape=[pltpu.HBM((r, c), F32)] * 4,
        compiler_params=pltpu.CompilerParams(dimension_semantics=("parallel",)),
    )(*_in_hbm(w, mine, recv, m, v))


def _adam_small(gathered, ws, ms, vs, loss_parts):
    n = len(ws)

    def body(*refs):
        outs = refs[4 * n + 1:]
        loss = refs[4 * n][0]
        for dev in range(1, N_DEV):
            loss = loss + refs[4 * n][dev]
        outs[4 * n][...] = loss
        for i in range(n):
            ga_ref, w_ref, m_ref, v_ref = (refs[j * n + i] for j in range(4))
            g = ga_ref[0]
            for dev in range(1, N_DEV):
                g = g + ga_ref[dev]
            g = g[:, :w_ref.shape[1]]
            outs[4 * i][...] = g
            outs[4 * i + 1][...], outs[4 * i + 2][...], outs[4 * i + 3][...] = _adam_math(
                w_ref[...], g, m_ref[...], v_ref[...])

    out_shape = [pltpu.HBM(w.shape, F32) for w in ws for _ in range(4)]
    out_shape.append(pltpu.HBM((1, LANES), F32))
    out = pl.pallas_call(body, name="adam_small", out_shape=out_shape)(*gathered, *ws, *ms, *vs, loss_parts)
    return [out[4 * i:4 * i + 4] for i in range(n)], out[4 * n]


_PROJ_SEGMENTS = ((3848, 5896), (None, COL_QA - 2 * D_MODEL), (0, 3840), (3840, 3848), (None, PROJ_COLS - COL_F - 8))


def _proj_weight_t(gathered):
    w = gathered.reshape(IN_COLS, D_MODEL)
    return jnp.concatenate([jnp.zeros((hi, D_MODEL), w.dtype) if lo is None else w[lo:hi] for lo, hi in _PROJ_SEGMENTS],
                           axis=0)


def _proj_weight_grad_slots(dwt_r):
    starts, at = [], 0
    for lo, hi in _PROJ_SEGMENTS:
        if lo is not None:
            starts.append((lo, hi, at))
        at += hi if lo is None else hi - lo
    slots = []
    for dev in range(N_DEV):
        pieces, lo, end = [], dev * IN_SHARD, (dev + 1) * IN_SHARD
        for seg_lo, seg_hi, seg_at in sorted(starts):
            a, b = max(lo, seg_lo), min(end, seg_hi)
            if a < b:
                pieces.append(dwt_r[seg_at + a - seg_lo:seg_at + b - seg_lo])
        slots.append(pieces[0] if len(pieces) == 1 else jnp.concatenate(pieces, axis=0))
    return jnp.stack(slots)


def kernel(x, w_in, w_proj_a, w_proj_b, w_out, b_forget, w_ffn_gate, w_ffn_up, w_ffn_down, norm_mix_pre, norm_mix_post, norm_ffn_pre, norm_ffn_post, loss_target, m_w_in, m_w_proj_a, m_w_proj_b, m_w_out, m_b_forget, m_w_ffn_gate, m_w_ffn_up, m_w_ffn_down, m_norm_mix_pre, m_norm_mix_post, m_norm_ffn_pre, m_norm_ffn_post, v_w_in, v_w_proj_a, v_w_proj_b, v_w_out, v_b_forget, v_w_ffn_gate, v_w_ffn_up, v_w_ffn_down, v_norm_mix_pre, v_norm_mix_post, v_norm_ffn_pre, v_norm_ffn_post):
    d = D_MODEL
    names = ("w_in", "w_proj_a", "w_proj_b", "w_out", "w_ffn_gate", "w_ffn_up", "w_ffn_down")
    col_sharded = ("w_in", "w_ffn_gate", "w_ffn_up")

    def row_shards(arrs):
        return {k: (a[0].T if k in col_sharded else a[0]) for k, a in zip(names, arrs)}

    shards = row_shards((w_in, w_proj_a, w_proj_b, w_out, w_ffn_gate, w_ffn_up, w_ffn_down))
    moments_m = row_shards((m_w_in, m_w_proj_a, m_w_proj_b, m_w_out, m_w_ffn_gate, m_w_ffn_up, m_w_ffn_down))
    moments_v = row_shards((v_w_in, v_w_proj_a, v_w_proj_b, v_w_out, v_w_ffn_gate, v_w_ffn_up, v_w_ffn_down))
    pos = jnp.stack([lax.axis_index("c"), 2 * lax.axis_index("x") + lax.axis_index("y")]).astype(jnp.int32)
    x2, target = x[0], loss_target[0]

    me = 4 * lax.axis_index("x") + 2 * lax.axis_index("y") + lax.axis_index("c")
    mid_names, ffn_names = names[1:4], names[4:]
    first_names, later_names = names[:1], names[1:]
    shards16 = {k: shards[k].astype(BF16) for k in names}

    def landing(k):
        return lax.dynamic_update_slice(lax.empty((N_DEV,) + shards[k].shape, BF16), shards16[k][None], (me, 0, 0))

    ag_first = _exchange_start("ag_first_chips_start", _gather_two_route_copies, [shards16[k] for k in first_names],
                               [landing(k) for k in first_names], 4 * len(first_names))
    h = _rowwise(lambda xb, g: xb * _rms_scale(xb) * g, "norm_mix_pre", SEQ, 256, [(x2, d, 0)], [norm_mix_pre],
                 [(d, BF16)], deps=[ag_first.token])[0]
    later_lands = [landing(k) for k in later_names]
    relayed, ag_first_last = _gather_relay(ag_first, "ag_first", [h, shards["w_in"], moments_m["w_in"], moments_v["w_in"],
                                                                 *later_lands, *[shards16[k] for k in later_names]])
    ag_later = _exchange_start("ag_later_chips_start", _gather_chips_copies, [shards16[k] for k in later_names],
                               later_lands, 3 * len(later_names), after=[relayed])
    gathered = dict(zip(first_names, ag_first_last([ag_later.token])))
    wt_r = _proj_weight_t(gathered["w_in"])

    proj = _matmul([(h, *_in_hbm(wt_r))], "nt", F32, "in_proj", 1024, 896, 1024)
    tables = _rope_tables()
    o_dil, lse_dil = _dil_fwd(proj, tables)
    out_a = _dil_combine(o_dil, lse_dil)
    _, lands = _exchange_wait("ag_later_chips_wait", ag_later, [out_a])
    ag_later = _exchange_start("ag_later_sibling_start", _gather_sibling_copies, [], lands, 4 * len(later_names))

    b_pad = jnp.pad(b_forget, ((0, 0), (0, LANES - N_FOX_HEADS)))
    f_rows = _fox_gate(proj, b_pad, deps=[ag_later.token])
    out_b, lse_fox = _fox_fwd(proj, f_rows)

    gathered = dict(zip(later_names, _exchange_wait("ag_later_sibling_wait", ag_later, [out_b])[1]))
    wt_pa = gathered["w_proj_a"].transpose(1, 0, 2).reshape(DIL_OUT_WIDTH, d)
    wt_pb = gathered["w_proj_b"].transpose(1, 0, 2).reshape(FOX_WIDTH, d)
    w_o = gathered["w_out"].reshape(d, d)
    wt_g = gathered["w_ffn_gate"].reshape(D_FF, d)
    wt_u = gathered["w_ffn_up"].reshape(D_FF, d)
    w_d = gathered["w_ffn_down"].reshape(D_FF, d)
    merged, mix, x1, h2 = _mix_out(out_a, out_b, proj, x2, wt_pa, wt_pb, w_o, norm_mix_post, norm_ffn_pre)

    gate, up, act = _ffn_up(h2, wt_g, wt_u)
    dy, dff, loss_part, dg_ffn_post = _loss_head(act, w_d, x1, target, norm_ffn_post)

    dgate, dup = _ffn_act_bwd(dff, w_d, gate, up)
    grads_t = {}
    grads_t["w_ffn_down"] = _matmul([(act, dff)], "tn", F32, "grad_w_ffn_down", 1408, 512, 2048, staged=False)
    grads_t["w_ffn_gate"] = _matmul([(dgate, h2)], "tn", F32, "grad_w_ffn_gate", 1408, 512, 2048)
    grads_t["w_ffn_up"] = _matmul([(dup, h2)], "tn", F32, "grad_w_ffn_up", 1408, 512, 2048)
    rs_ffn = _ReduceScatter("ffn", {k: grads_t[k] for k in ffn_names}, pos)
    dx1, dmix, dg_ffn_pre, dg_mix_post = _post_ffn_bwd(dgate, wt_g, dup, wt_u, x1, dy, mix, norm_ffn_pre, norm_mix_post,
                                                       deps=[rs_ffn.token])
    rs_ffn.start_chips([dmix])

    dproj, dya, dyb, d_out_a, d_out_b = _mix_out_bwd(dmix, out_a, out_b, proj, wt_pa, wt_pb, w_o, deps=[rs_ffn.token])
    grads_t["w_out"] = _matmul([(merged, dmix)], "tn", F32, "grad_w_out", 1024, 1024, 1024)
    def column_slots(g):
        return g.reshape(g.shape[0], N_DEV, LANES).transpose(1, 0, 2)

    grads_t["w_proj_a"] = column_slots(_matmul([(out_a, dya)], "tn", F32, "grad_w_proj_a", DIL_OUT_WIDTH, 1024, SEQ))
    grads_t["w_proj_b"] = column_slots(_matmul([(out_b, dyb)], "tn", F32, "grad_w_proj_b", FOX_WIDTH, 1024, SEQ))
    rs_mid = _ReduceScatter("mid", {k: grads_t[k] for k in mid_names}, pos)

    do_dil, c_dil = _dil_combine_bwd(d_out_a, o_dil, lse_dil, deps=[rs_mid.token])
    rs_mid.start_chips([c_dil])
    dproj, d_cum = _fox_bwd(proj, d_out_b, lse_fox, f_rows, dproj)
    d_cum_rows = jnp.pad(d_cum[:, :2].reshape(N_FOX_HEADS, SEQ), ((0, F_ROWS - N_FOX_HEADS), (0, 0)))
    dproj, db_part = _fox_gate_bwd(d_cum_rows, proj, b_pad, dproj)
    dproj = _dil_bwd(proj, tables, do_dil, lse_dil, c_dil, dproj, deps=[rs_mid.token])

    dwt_r = _matmul([(dproj, h)], "tn", F32, "grad_w_in", 896, 1024, 2048)
    rs_in = _ReduceScatter("in", {"w_in": _proj_weight_grad_slots(dwt_r)}, pos)
    def finish(rs, after):
        return {k: _adam(shards[k], mine, recv, moments_m[k], moments_v[k], "adam_" + k)
                for k, (mine, recv) in rs.finish(after).items()}

    done = finish(rs_ffn, [rs_in.token])
    rs_in.start_chips([done[k][0] for k in ffn_names])
    grad_x, dg_mix_pre = _input_bwd(dproj, wt_r, x2, dx1, norm_mix_pre, deps=[rs_in.token])
    done.update(finish(rs_mid, [grad_x]))

    small_all = _all_gather([dg_mix_pre, dg_mix_post, dg_ffn_pre, dg_ffn_post, db_part, loss_part],
                            "small_grads_all_gather", deps=[done[k][0] for k in mid_names])
    small, loss = _adam_small(small_all[:5], [norm_mix_pre, norm_mix_post, norm_ffn_pre, norm_ffn_post, b_forget],
                              [m_norm_mix_pre, m_norm_mix_post, m_norm_ffn_pre, m_norm_ffn_post, m_b_forget],
                              [v_norm_mix_pre, v_norm_mix_post, v_norm_ffn_pre, v_norm_ffn_post, v_b_forget],
                              small_all[5])

    done.update(finish(rs_in, [small[0][0]]))

    def leaves(i):
        def nat(k):
            a = done[k][i]
            return (a.T if k in col_sharded else a)[None]

        return [nat("w_in"), nat("w_proj_a"), nat("w_proj_b"), nat("w_out"), small[4][i],
                nat("w_ffn_gate"), nat("w_ffn_up"), nat("w_ffn_down"), *[small[r][i] for r in range(4)]]

    return (loss[0, 0], grad_x[None], *leaves(0), *leaves(1), *leaves(2), *leaves(3))
```

```python
import functools
import math

import jax
import jax.numpy as jnp
import numpy as np
from jax import lax
from jax.experimental import pallas as pl
from jax.experimental.pallas import tpu as pltpu

F32 = jnp.float32
BF16 = jnp.bfloat16
MESH = pl.DeviceIdType.MESH

D_MODEL = 1024
SEQ = 2048
HEAD_DIM = 64
BLOCK = 128
N_BLOCKS = SEQ // BLOCK
DILATIONS = (1, 4, 16)
N_FOX_HEADS = 8
DIL_WIDTH = 768
DIL_OUT_WIDTH = 256
FOX_WIDTH = 512
D_FF = 2816
ROPE_THETA = 500000.0
ROPE_DIM = HEAD_DIM // 4
ROPE_HALF = ROPE_DIM // 2
EPS = 1e-6
NEG_INF = -1e30
QK_SCALE = 1.0 / math.sqrt(HEAD_DIM)
IN_COLS = 5896
N_DEV = 8
IN_SHARD = IN_COLS // N_DEV

ADAM_LR = 0.001
ADAM_B1 = 0.9
ADAM_B2 = 0.999
ADAM_EPS = 1e-08
ADAM_WD = 0.01
ADAM_STEP = 10

V7X_VMEM_BYTES = 64 * 2**20
LANES = 128
SUBLANES = 8

PROJ_COLS = 6272
COL_GA, COL_GB = 0, 1024
COL_QA, COL_KA, COL_VA = 2304, 3072, 3840
COL_QB, COL_KB, COL_VB = 4608, 5120, 5632
COL_F = 6144
F_ROWS = 16


def _vmem_limit(block_bytes):
    want = 2 * block_bytes + 16 * 2**20
    return int(min(max(want, 32 * 2**20), V7X_VMEM_BYTES - 8 * 2**20))


def _nbytes(shape, dtype):
    return math.prod(shape) * jnp.dtype(dtype).itemsize


def _in_hbm(*arrays):
    return [pltpu.with_memory_space_constraint(a, pltpu.HBM) for a in arrays]


def _dot(a, b, dims):
    return lax.dot_general(a, b, (dims, ((), ())), preferred_element_type=F32)


def _dot_nn(a, b):
    return _dot(a, b, ((1,), (0,)))


def _dot_nt(a, b):
    return _dot(a, b, ((1,), (1,)))


def _dot_tn(a, b):
    return _dot(a, b, ((0,), (0,)))


def _sigmoid(z):
    return 1.0 / (1.0 + jnp.exp(-z))


def _split3(x):
    hi = x.astype(BF16)
    r1 = x - hi.astype(F32)
    mid = r1.astype(BF16)
    lo = (r1 - mid.astype(F32)).astype(BF16)
    return hi, mid, lo


def _dot3_nn(x, ones_matrix):
    hi, mid, lo = _split3(x)
    return (_dot_nn(hi, ones_matrix) + _dot_nn(mid, ones_matrix)) + _dot_nn(lo, ones_matrix)


def _rowwise(fn, name, n_rows, tm, row_ins, bcast_ins, row_outs, acc_outs=(), deps=()):
    n_in = len(row_ins) + len(bcast_ins)
    n_ro = len(row_outs)

    def body(*refs):
        res = fn(*[r[...] for r in refs[:n_in]])
        if not isinstance(res, (tuple, list)):
            res = (res,)
        outs = refs[n_in + len(deps):]
        for r, o in zip(res[:n_ro], outs[:n_ro]):
            o[...] = r.astype(o.dtype)
        first = pl.program_id(0) == 0
        for r, o in zip(res[n_ro:], outs[n_ro:]):
            _accumulate(o, r, first)

    in_specs = [pl.BlockSpec((tm, w), lambda i, cb=cb: (i, cb)) for _, w, cb in row_ins]
    in_specs += [pl.BlockSpec(a.shape, lambda i: (0, 0)) for a in bcast_ins]
    in_specs += [pl.BlockSpec(memory_space=pl.ANY)] * len(deps)
    out_specs = [pl.BlockSpec((tm, w), lambda i: (i, 0)) for w, _ in row_outs]
    out_specs += [pl.BlockSpec((1, w), lambda i: (0, 0)) for w in acc_outs]
    out_shape = [pltpu.HBM((n_rows, w), dt) for w, dt in row_outs]
    out_shape += [pltpu.HBM((1, w), F32) for w in acc_outs]
    blk = sum(_nbytes((tm, w), a.dtype) for a, w, _ in row_ins) + sum(_nbytes((tm, w), dt) for w, dt in row_outs)
    return pl.pallas_call(
        body, name=name, grid=(n_rows // tm,), in_specs=in_specs, out_specs=out_specs, out_shape=out_shape,
        compiler_params=pltpu.CompilerParams(
            dimension_semantics=("arbitrary" if acc_outs else "parallel",), vmem_limit_bytes=_vmem_limit(3 * blk)),
    )(*_in_hbm(*[a for a, _, _ in row_ins], *bcast_ins), *deps)


def _accumulate(o_ref, part, first):
    @pl.when(first)
    def _():
        o_ref[...] = part

    @pl.when(jnp.logical_not(first))
    def _():
        o_ref[...] += part


_MM_DIMS = {"nn": ((1,), (0,)), "nt": ((1,), (1,)), "tn": ((0,), (0,))}


def _matmul(pairs, mode, out_dtype, name, tm, tn, tk, deps=(), staged=True):
    a0, b0 = pairs[0]
    if mode == "tn":
        kk, m = a0.shape
    else:
        m, kk = a0.shape
    n = b0.shape[0] if mode == "nt" else b0.shape[1]
    assert m % tm == 0 and n % tn == 0 and kk % tk == 0, (name, m, n, kk)
    nk = kk // tk
    n_pairs = len(pairs)
    dims = _MM_DIMS[mode]
    n_in = 2 * n_pairs + len(deps)

    def body(*refs):
        o_ref = refs[n_in]
        part = None
        for p in range(n_pairs):
            d = _dot(refs[2 * p][...].astype(BF16), refs[2 * p + 1][...].astype(BF16), dims)
            part = d if part is None else part + d
        if nk == 1:
            o_ref[...] = part.astype(o_ref.dtype)
            return
        acc = refs[n_in + 1]
        k = pl.program_id(2)

        @pl.when(k == 0)
        def _():
            acc[...] = part

        @pl.when(k > 0)
        def _():
            acc[...] += part

        @pl.when(k == nk - 1)
        def _():
            o_ref[...] = acc[...].astype(o_ref.dtype)

    if mode == "tn":
        a_spec = pl.BlockSpec((tk, tm), lambda i, j, k: (k, i))
    else:
        a_spec = pl.BlockSpec((tm, tk), lambda i, j, k: (i, k))
    if mode == "nt":
        b_spec = pl.BlockSpec((tn, tk), lambda i, j, k: (j, k))
    else:
        b_spec = pl.BlockSpec((tk, tn), lambda i, j, k: (k, j))
    blk = sum(_nbytes((tm, tk), a.dtype) + _nbytes((tk, tn), b.dtype) for a, b in pairs) + 2 * _nbytes((tm, tn), F32)
    flat = [a for pair in pairs for a in pair]
    return pl.pallas_call(
        body, name=name, grid=(m // tm, n // tn, nk),
        in_specs=[a_spec, b_spec] * n_pairs + [pl.BlockSpec(memory_space=pl.ANY)] * len(deps),
        out_specs=pl.BlockSpec((tm, tn), lambda i, j, k: (i, j)),
        out_shape=pltpu.HBM((m, n), out_dtype),
        scratch_shapes=[] if nk == 1 else [pltpu.VMEM((tm, tn), F32)],
        compiler_params=pltpu.CompilerParams(
            dimension_semantics=("parallel", "parallel", "arbitrary"), vmem_limit_bytes=_vmem_limit(blk)),
    )(*(flat if staged else _in_hbm(*flat)), *deps)


def _matmul_rowwise(pairs, fn, name, tm, row_ins, bcast_ins, row_outs, acc_outs=(), deps=()):
    m = pairs[0][0].shape[0]
    n_mm, n_in = 2 * len(pairs), len(row_ins) + len(bcast_ins)
    n_ro = len(row_outs)

    def body(*refs):
        prod = None
        for p in range(len(pairs)):
            part = _dot_nn(refs[2 * p][...].astype(BF16), refs[2 * p + 1][...].astype(BF16))
            prod = part if prod is None else prod + part
        res = fn(prod, *[r[...] for r in refs[n_mm:n_mm + n_in]])
        outs = refs[n_mm + n_in + len(deps):]
        for r, o in zip(res[:n_ro], outs[:n_ro]):
            o[...] = r.astype(o.dtype)
        first = pl.program_id(0) == 0
        for r, o in zip(res[n_ro:], outs[n_ro:]):
            _accumulate(o, r, first)

    in_specs = []
    for a, b in pairs:
        in_specs += [pl.BlockSpec((tm, a.shape[1]), lambda i: (i, 0)),
                     pl.BlockSpec(b.shape, lambda i: (0, 0), pipeline_mode=pl.Buffered(1))]
    in_specs += [pl.BlockSpec((tm, w), lambda i, cb=cb: (i, cb)) for _, w, cb in row_ins]
    in_specs += [pl.BlockSpec(a.shape, lambda i: (0, 0)) for a in bcast_ins]
    in_specs += [_ANY] * len(deps)
    out_specs = [pl.BlockSpec((tm, w), lambda i: (i, 0)) for w, _ in row_outs]
    out_specs += [pl.BlockSpec((1, w), lambda i: (0, 0)) for w in acc_outs]
    out_shape = [pltpu.HBM((m, w), dt) for w, dt in row_outs]
    out_shape += [pltpu.HBM((1, w), F32) for w in acc_outs]
    blk = sum(_nbytes((tm, a.shape[1]), a.dtype) + _nbytes(b.shape, b.dtype) // 2 for a, b in pairs)
    blk += sum(_nbytes((tm, w), a.dtype) for a, w, _ in row_ins) + sum(_nbytes((tm, w), dt) for w, dt in row_outs)
    return pl.pallas_call(
        body, name=name, grid=(m // tm,), in_specs=in_specs, out_specs=out_specs, out_shape=out_shape,
        compiler_params=pltpu.CompilerParams(dimension_semantics=("arbitrary",), vmem_limit_bytes=_vmem_limit(blk)),
    )(*[a for pair in pairs for a in pair], *[a for a, _, _ in row_ins], *bcast_ins, *deps)


def _rms_scale(x):
    return lax.rsqrt(jnp.mean(x * x, axis=-1, keepdims=True) + EPS)


def _rms_bwd(xin, dyn, g):
    r = _rms_scale(xin)
    u = dyn * g
    dx = r * u - xin * (r * r * r) * jnp.mean(u * xin, axis=-1, keepdims=True)
    dg = jnp.sum(dyn * xin * r, axis=0, keepdims=True)
    return dx, dg


def _mesh_pos():
    return lax.axis_index("x"), lax.axis_index("y"), lax.axis_index("c")


def _all_gather(xs, name, deps=()):
    n = len(xs)

    def body(*refs):
        x_refs, out_refs = refs[:n], refs[n + len(deps):2 * n + len(deps)]
        send_sems, recv_sems, local_sems = refs[2 * n + len(deps):]
        mx, my, mc = _mesh_pos()
        me, sib = (mx, my, mc), (mx, my, 1 - mc)
        chips = [(1 - mx, my), (mx, 1 - my), (1 - mx, 1 - my)]

        def slot(a, dev):
            px, py, pc = dev
            return out_refs[a].at[4 * px + 2 * py + pc]

        def copy(k, a, block, to, src=None):
            return pltpu.make_async_remote_copy(
                src_ref=slot(a, block) if src is None else src, dst_ref=slot(a, block),
                send_sem=send_sems.at[a * 7 + k], recv_sem=recv_sems.at[a * 7 + k],
                device_id=to, device_id_type=MESH)

        mine = [pltpu.make_async_copy(x_refs[a], slot(a, me), local_sems.at[a]) for a in range(n)]
        for cp in mine:
            cp.start()
        first = []
        for a in range(n):
            first.append(copy(0, a, me, sib, x_refs[a]))
            first += [copy(1 + j, a, me, (*chip, mc), x_refs[a]) for j, chip in enumerate(chips)]
        for cp in first:
            cp.start()
        passed = []
        for a in range(n):
            for j, chip in enumerate(chips):
                copy(1 + j, a, (*chip, mc), me).wait_recv()
                fwd = copy(4 + j, a, (*chip, mc), sib)
                fwd.start()
                passed.append(fwd)
        for a in range(n):
            copy(0, a, sib, me).wait_recv()
            for j, chip in enumerate(chips):
                copy(4 + j, a, (*chip, 1 - mc), me).wait_recv()
        for cp in first + passed:
            cp.wait_send()
        for cp in mine:
            cp.wait()

    hbm = pl.BlockSpec(memory_space=pl.ANY)
    return pl.pallas_call(
        body, name=name,
        out_shape=[pltpu.HBM((N_DEV,) + x.shape, x.dtype) for x in xs],
        in_specs=[hbm] * (n + len(deps)), out_specs=[hbm] * n,
        scratch_shapes=[pltpu.SemaphoreType.DMA((7 * n,)), pltpu.SemaphoreType.DMA((7 * n,)),
                        pltpu.SemaphoreType.DMA((n,))],
    )(*xs, *deps)


_HBM = pl.BlockSpec(memory_space=pltpu.HBM)
_SEM = pl.BlockSpec(memory_space=pltpu.SEMAPHORE)
_ANY = pl.BlockSpec(memory_space=pl.ANY)
_DATAFLOW = pltpu.SideEffectType.DATAFLOW_SIDE_EFFECTING


def _flip_peer(flip):
    mx, my, mc = _mesh_pos()
    return (1 - mx if flip & 2 else mx, 1 - my if flip & 1 else my, mc)


def _remote(src, dst, send_sems, recv_sems, k, peer):
    return pltpu.make_async_remote_copy(src_ref=src, dst_ref=dst, send_sem=send_sems.at[k], recv_sem=recv_sems.at[k],
                                        device_id=peer, device_id_type=MESH)


def _gather_chips_copies(srcs, lands, send_sems, recv_sems):
    mx, my, mc = _mesh_pos()
    me = 4 * mx + 2 * my + mc
    return [_remote(srcs[a], lands[a].at[me], send_sems, recv_sems, 3 * a + flip - 1, _flip_peer(flip))
            for a in range(len(srcs)) for flip in (1, 2, 3)]


def _gather_sibling_copies(srcs, lands, send_sems, recv_sems):
    mx, my, mc = _mesh_pos()
    return [_remote(lands[a].at[2 * k + mc], lands[a].at[2 * k + mc], send_sems, recv_sems, 4 * a + k, (mx, my, 1 - mc))
            for a in range(len(lands)) for k in range(4)]


def _scatter_sibling_copies(srcs, lands, send_sems, recv_sems):
    mx, my, mc = _mesh_pos()
    return [_remote(srcs[a].at[k, 1 - mc], lands[a].at[k], send_sems, recv_sems, 4 * a + k, (mx, my, 1 - mc))
            for a in range(len(srcs)) for k in range(4)]


def _scatter_chips_copies(srcs, lands, send_sems, recv_sems):
    mx, my, _ = _mesh_pos()
    k0 = 2 * mx + my
    return [_remote(srcs[a].at[jnp.bitwise_xor(k0, flip)], lands[a].at[flip - 1], send_sems, recv_sems,
                    3 * a + flip - 1, _flip_peer(flip))
            for a in range(len(srcs)) for flip in (1, 2, 3)]


class _Exchange:
    def __init__(self, copies, n_src, send_sems, recv_sems, thru, token):
        self.copies, self.n_src, self.send_sems, self.recv_sems, self.thru, self.token = (
            copies, n_src, send_sems, recv_sems, thru, token)


def _exchange_start(name, copies, srcs, lands, n_copies, after=()):
    bufs = list(srcs) + list(lands)
    nb, ns = len(bufs), len(srcs)

    def body(*refs):
        send_sems, recv_sems = refs[nb + len(after)], refs[nb + len(after) + 1]
        for cp in copies(refs[:ns], refs[ns:nb], send_sems, recv_sems):
            cp.start()
        refs[-1][...] = jnp.zeros_like(refs[-1])

    out = pl.pallas_call(
        body, name=name,
        out_shape=(pltpu.SemaphoreType.DMA((n_copies,)), pltpu.SemaphoreType.DMA((n_copies,)),
                   *[pltpu.HBM(b.shape, b.dtype) for b in bufs], pltpu.HBM((SUBLANES, LANES), F32)),
        in_specs=[_HBM] * nb + [_ANY] * len(after),
        out_specs=(_SEM, _SEM, *[_HBM] * nb, pl.BlockSpec(memory_space=pltpu.VMEM)),
        input_output_aliases={i: 2 + i for i in range(nb)},
        compiler_params=pltpu.CompilerParams(has_side_effects=_DATAFLOW),
    )(*[pltpu.with_memory_space_constraint(b, pltpu.HBM) for b in bufs], *after)
    return _Exchange(copies, ns, out[0], out[1], list(out[2:2 + nb]), out[-1])


def _exchange_wait(name, ex, after):
    nb, ns = len(ex.thru), ex.n_src

    def body(*refs):
        for cp in ex.copies(refs[:ns], refs[ns:nb], refs[nb], refs[nb + 1]):
            cp.wait_send()
            cp.wait_recv()

    out = pl.pallas_call(
        body, name=name, out_shape=tuple(pltpu.HBM(b.shape, b.dtype) for b in ex.thru),
        in_specs=[_HBM] * nb + [_SEM, _SEM] + [_ANY] * len(after), out_specs=tuple([_HBM] * nb),
        input_output_aliases={i: i for i in range(nb)},
        compiler_params=pltpu.CompilerParams(has_side_effects=_DATAFLOW),
    )(*ex.thru, ex.send_sems, ex.recv_sems, *after)
    return list(out[:ns]), list(out[ns:])


def _halves(ref):
    half = ref.shape[1] // 2
    assert half % LANES == 0, ref.shape
    return ref.at[:, pl.ds(0, half)], ref.at[:, pl.ds(half, half)]


def _gather_two_route_copies(srcs, lands, send_sems, recv_sems):
    mx, my, mc = _mesh_pos()
    me = 4 * mx + 2 * my + mc
    return [_remote(_halves(srcs[a])[h], _halves(lands[a].at[me])[h], send_sems, recv_sems, 4 * a + i, _flip_peer(flip))
            for a in range(len(srcs)) for i, (flip, h) in enumerate(((2, 0), (1, 1), (2, 1), (1, 0)))]


def _to_sibling(land, flip, h, send_sems, recv_sems, k):
    mx, my, mc = _mesh_pos()
    part = _halves(land.at[2 * jnp.bitwise_xor(2 * mx + my, flip) + mc])[h]
    return _remote(part, part, send_sems, recv_sems, k, (mx, my, 1 - mc))


def _second_hop(land, send_sems, recv_sems, k):
    mx, my, mc = _mesh_pos()
    k0 = 2 * mx + my
    from_y = _halves(land.at[2 * jnp.bitwise_xor(k0, 1) + mc])[1]
    from_x = _halves(land.at[2 * jnp.bitwise_xor(k0, 2) + mc])[0]
    return (_remote(from_y, from_y, send_sems, recv_sems, k, _flip_peer(2)),
            _remote(from_x, from_x, send_sems, recv_sems, k + 1, _flip_peer(1)))


def _relay_call(name, body, bufs, sems, n_new, after):
    nb, n_in = len(bufs), len(bufs) + len(sems) + len(after)

    def call_body(*refs):
        body(refs[:nb], refs[nb:nb + len(sems)], refs[n_in], refs[n_in + 1])
        refs[-1][...] = jnp.zeros_like(refs[-1])

    out = pl.pallas_call(
        call_body, name=name,
        out_shape=(pltpu.SemaphoreType.DMA((n_new,)), pltpu.SemaphoreType.DMA((n_new,)),
                   *[pltpu.HBM(b.shape, b.dtype) for b in bufs], pltpu.HBM((SUBLANES, LANES), F32)),
        in_specs=[_HBM] * nb + [_SEM] * len(sems) + [_ANY] * len(after),
        out_specs=(_SEM, _SEM, *[_HBM] * nb, pl.BlockSpec(memory_space=pltpu.VMEM)),
        input_output_aliases={i: 2 + i for i in range(nb)},
        compiler_params=pltpu.CompilerParams(has_side_effects=_DATAFLOW),
    )(*bufs, *sems, *after)
    return out[0], out[1], list(out[2:2 + nb]), out[-1]


def _gather_relay(ex, tag, after):
    ns = ex.n_src
    n = len(ex.thru) - ns
    assert n == 1

    def first(bufs, sems, send, recv):
        land = bufs[ns]
        x_first, y_second, x_second, y_first = ex.copies(bufs[:ns], bufs[ns:], *sems)
        to_x, to_y = _second_hop(land, send, recv, 8)
        for h in (0, 1):
            _to_sibling(land, 0, h, send, recv, h).start()
        x_first.wait_recv()
        to_y.start()
        _to_sibling(land, 2, 0, send, recv, 4).start()
        y_second.wait_recv()
        to_x.start()
        _to_sibling(land, 1, 1, send, recv, 3).start()
        x_second.wait_recv()
        _to_sibling(land, 2, 1, send, recv, 5).start()
        y_first.wait_recv()
        _to_sibling(land, 1, 0, send, recv, 2).start()
        for cp in (x_first, y_second, x_second, y_first):
            cp.wait_send()

    def second(bufs, sems, send, recv):
        to_x, to_y = _second_hop(bufs[0], *sems, 8)
        to_y.wait_recv()
        _to_sibling(bufs[0], 3, 0, send, recv, 0).start()
        to_x.wait_recv()
        _to_sibling(bufs[0], 3, 1, send, recv, 1).start()
        to_x.wait_send()
        to_y.wait_send()

    def last(bufs, sems, send, recv):
        for flip in range(4):
            for h in (0, 1):
                s, r, k = (sems[2], sems[3], h) if flip == 3 else (sems[0], sems[1], 2 * flip + h)
                cp = _to_sibling(bufs[0], flip, h, s, r, k)
                cp.wait_send()
                cp.wait_recv()

    send1, recv1, bufs, _ = _relay_call(f"{tag}_chips_relay", first, ex.thru, [ex.send_sems, ex.recv_sems], 10, after)
    send2, recv2, lands, token = _relay_call(f"{tag}_diagonal_relay", second, bufs[ns:], [send1, recv1], 2, [])
    return token, lambda after_last: _relay_call(f"{tag}_sibling_wait", last, lands, [send1, recv1, send2, recv2], 1,
                                                 after_last)[2]


def _col_tile(r, c):
    return next(t for t in (1024, 512, 256, 128) if c % t == 0 and (r * t * 4 <= 2**20 or t == 128))


def _add_sibling(g4, recv, pos, name):
    _, _, r, c = g4.shape
    tc = _col_tile(r, c)

    def body(pos_ref, g_ref, r_ref, o16_ref, mine_ref):
        s = g_ref[0, 0] + r_ref[0]
        o16_ref[0] = s.astype(BF16)

        @pl.when(pl.program_id(1) == pos_ref[1])
        def _():
            mine_ref[...] = s

    slot = pl.BlockSpec((1, r, tc), lambda j, k, pos_ref: (k, 0, j))
    return pl.pallas_call(
        body, name=name,
        out_shape=[pltpu.HBM((4, r, c), BF16), pltpu.HBM((r, c), F32)],
        grid_spec=pltpu.PrefetchScalarGridSpec(
            num_scalar_prefetch=1, grid=(c // tc, 4),
            in_specs=[pl.BlockSpec((1, 1, r, tc), lambda j, k, pos_ref: (k, pos_ref[0], 0, j)), slot],
            out_specs=[slot, pl.BlockSpec((r, tc), lambda j, k, pos_ref: (0, j))]),
        compiler_params=pltpu.CompilerParams(dimension_semantics=("parallel", "arbitrary")),
    )(pos, *_in_hbm(g4, recv))


class _ReduceScatter:
    def __init__(self, tag, grads_t, pos):
        self.tag, self.pos, self.names = tag, pos, list(grads_t)
        g4s = [g.reshape(4, 2, g.size // (N_DEV * g.shape[-1]), g.shape[-1]) for g in grads_t.values()]
        lands = [lax.empty((4,) + g.shape[2:], F32) for g in g4s]
        self.ex = _exchange_start(f"rs_{tag}_sibling_start", _scatter_sibling_copies, g4s, lands, 4 * len(g4s))
        self.token = self.ex.token

    def start_chips(self, after):
        g4s, from_sibling = _exchange_wait(f"rs_{self.tag}_sibling_wait", self.ex, after)
        parts = [_add_sibling(g4, rv, self.pos, f"rs_add_sibling_{k}")
                 for k, g4, rv in zip(self.names, g4s, from_sibling)]
        self.mine = [mine for _, mine in parts]
        p16s = [p16 for p16, _ in parts]
        lands = [lax.empty((3,) + p.shape[1:], BF16) for p in p16s]
        self.ex = _exchange_start(f"rs_{self.tag}_chips_start", _scatter_chips_copies, p16s, lands, 3 * len(p16s))
        self.token = self.ex.token

    def finish(self, after):
        _, from_chips = _exchange_wait(f"rs_{self.tag}_chips_wait", self.ex, after)
        return dict(zip(self.names, zip(self.mine, from_chips)))


def _rope_tables():
    positions = np.arange(SEQ, dtype=np.float32)
    inv_freq = np.power(np.float32(ROPE_THETA), -np.arange(0, ROPE_DIM, 2, dtype=np.float32) / np.float32(ROPE_DIM))
    ang = (positions[:, None] * inv_freq[None, :]).astype(np.float32)
    cos, sin = np.cos(ang).astype(np.float32), np.sin(ang).astype(np.float32)
    ones = np.ones((SEQ, HEAD_DIM - ROPE_DIM), np.float32)
    zeros8 = np.zeros((SEQ, ROPE_HALF), np.float32)
    zeros = np.zeros((SEQ, HEAD_DIM - ROPE_DIM), np.float32)
    c_head = np.concatenate([cos, cos, ones], axis=1)
    s1_head = np.concatenate([-sin, zeros8, zeros], axis=1)
    s2_head = np.concatenate([zeros8, sin, zeros], axis=1)
    return tuple(jnp.asarray(np.concatenate([t, t], axis=1)) for t in (c_head, s1_head, s2_head))


def _rope_apply(x, c, s1, s2):
    w = x.shape[1]
    return x * c + pltpu.roll(x, w - ROPE_HALF, 1) * s1 + pltpu.roll(x, ROPE_HALF, 1) * s2


def _rope_apply_t(dy, c, s1, s2):
    w = dy.shape[1]
    return dy * c + pltpu.roll(dy * s1, ROPE_HALF, 1) + pltpu.roll(dy * s2, w - ROPE_HALF, 1)


def _dil_prev_limit(has_prev):
    return jnp.where(has_prev, 0, BLOCK)


def _dil_valid(limit):
    row = lax.broadcasted_iota(jnp.int32, (BLOCK, 2 * BLOCK), 0)
    col = lax.broadcasted_iota(jnp.int32, (BLOCK, 2 * BLOCK), 1)
    dist = col - row
    return jnp.logical_and(dist >= jnp.where(col < BLOCK, limit, -BLOCK), dist <= BLOCK)


def _upper_half():
    return lax.broadcasted_iota(jnp.int32, (1, LANES), 1) >= HEAD_DIM


def _stack_heads(x):
    upper = _upper_half()
    return jnp.concatenate([jnp.where(upper, 0, x), jnp.where(upper, x, 0)], axis=0)


def _unstack_heads(y):
    n = y.shape[0] // 2
    return jnp.where(_upper_half(), y[n:], y[:n])


def _head_columns(t):
    return jnp.concatenate([t[:, 0:1], t[:, HEAD_DIM:HEAD_DIM + 1]], axis=0)


def _dil_rows(n, d):
    per = N_BLOCKS // d
    r, lb = n // per, n % per

    def rows(b):
        start = b * (BLOCK * d) + r
        return pl.ds(pl.multiple_of(start, BLOCK), BLOCK) if d == 1 else pl.ds(start, BLOCK, stride=d)

    return rows(lb), rows(jnp.maximum(lb - 1, 0)), lb > 0


def _dil_rotate(q_ref, k_ref, c_ref, s1_ref, s2_ref, q_rot, k_rot):
    tabs = (c_ref[...], s1_ref[...], s2_ref[...])
    q_rot[...] = _rope_apply(q_ref[...], *tabs) * QK_SCALE
    k_rot[...] = _rope_apply(k_ref[...], *tabs)


def _dil_specs():
    def col(base):
        return pl.BlockSpec((SEQ, LANES), lambda p: (0, base // LANES + p))

    table = pl.BlockSpec((SEQ, LANES), lambda p: (0, 0))
    return [col(COL_QA), col(COL_KA), col(COL_VA)], [table] * 3


def _store_columns(blocks, dproj_ref, cols, sem):
    copies = [pltpu.make_async_copy(b, dproj_ref.at[:, pl.ds(pl.multiple_of(c * LANES, LANES), LANES)], sem.at[i])
              for i, (b, c) in enumerate(zip(blocks, cols))]
    for cp in copies:
        cp.start()
    for cp in copies:
        cp.wait()


def _dil_window(d, n, k_rot, v_ref):
    rows, prev, has_prev = _dil_rows(n, d)
    kw, vw = k_rot[rows, :].astype(BF16), v_ref[rows, :].astype(BF16)
    if d == N_BLOCKS:
        row = lax.broadcasted_iota(jnp.int32, (BLOCK, BLOCK), 0)
        valid = lax.broadcasted_iota(jnp.int32, (BLOCK, BLOCK), 1) <= row
    else:
        kw = jnp.concatenate([k_rot[prev, :].astype(BF16), kw], axis=0)
        vw = jnp.concatenate([v_ref[prev, :].astype(BF16), vw], axis=0)
        valid = _dil_valid(_dil_prev_limit(has_prev))
    return rows, prev, kw, vw, jnp.concatenate([valid, valid], axis=0)


def _dil_fwd(proj, tables):
    def body(q_ref, k_ref, v_ref, c_ref, s1_ref, s2_ref, o_ref, lse_ref, q_rot, k_rot):
        upper = _upper_half()
        _dil_rotate(q_ref, k_ref, c_ref, s1_ref, s2_ref, q_rot, k_rot)

        def blocks_of(d):
            def block(n, carry):
                rows, _, kw, vw, valid = _dil_window(d, n, k_rot, v_ref)
                s = jnp.where(valid, _dot_nt(_stack_heads(q_rot[rows, :].astype(BF16)), kw), NEG_INF)
                m = jnp.max(s, axis=-1, keepdims=True)
                p = jnp.exp(s - m)
                den = jnp.sum(p, axis=-1, keepdims=True)
                o_ref[rows, :] = _unstack_heads(_dot_nn((p * (1.0 / den)).astype(BF16), vw))
                lse = m + jnp.log(den)
                lse_ref[rows, :] = jnp.where(upper, lse[BLOCK:], lse[:BLOCK])
                return carry

            lax.fori_loop(0, N_BLOCKS, block, 0, unroll=4)

        for g, d in enumerate(DILATIONS):
            pl.when(pl.program_id(0) // 2 == g)(functools.partial(blocks_of, d))

    qkv, tabs = _dil_specs()
    out = pl.BlockSpec((SEQ, LANES), lambda p: (0, p))
    return pl.pallas_call(
        body, name="dil_attn_fwd", grid=(DIL_WIDTH // LANES,), in_specs=qkv + tabs, out_specs=[out, out],
        out_shape=[pltpu.HBM((SEQ, DIL_WIDTH), F32)] * 2,
        scratch_shapes=[pltpu.VMEM((SEQ, LANES), F32)] * 2,
        compiler_params=pltpu.CompilerParams(dimension_semantics=("parallel",)),
    )(*_in_hbm(proj, proj, proj, *tables))


def _dil_bwd(proj, tables, do, lse, c, dproj, deps=()):
    def body(q_ref, k_ref, v_ref, c_ref, s1_ref, s2_ref, do_ref, lse_ref, cc_ref, dproj_in, *rest):
        dproj_ref, dq_acc, dk_acc, dv_acc, dq_out, dk_out, dv_out, q_rot, k_rot, sem = rest[len(deps):]
        dk_acc[...] = jnp.zeros_like(dk_acc)
        dv_acc[...] = jnp.zeros_like(dv_acc)
        _dil_rotate(q_ref, k_ref, c_ref, s1_ref, s2_ref, q_rot, k_rot)

        def blocks_of(d):
            def block(n, carry):
                rows, prev, kw, vw, valid = _dil_window(d, n, k_rot, v_ref)
                q2 = _stack_heads(q_rot[rows, :].astype(BF16))
                do2 = _stack_heads(do_ref[rows, :].astype(BF16))
                lse_col, c_col = _head_columns(lse_ref[rows, :]), _head_columns(cc_ref[rows, :])
                p = jnp.where(valid, jnp.exp(_dot_nt(q2, kw) - lse_col), 0.0)
                ds = (p * (_dot_nt(do2, vw) + c_col)).astype(BF16)
                dk, dv = _dot_tn(ds, q2), _dot_tn(p.astype(BF16), do2)
                dq_acc[rows, :] = _unstack_heads(_dot_nn(ds, kw)) * QK_SCALE
                if d == N_BLOCKS:
                    dk_acc[rows, :] += dk
                    dv_acc[rows, :] += dv
                else:
                    dk_acc[prev, :] += dk[:BLOCK]
                    dv_acc[prev, :] += dv[:BLOCK]
                    dk_acc[rows, :] += dk[BLOCK:]
                    dv_acc[rows, :] += dv[BLOCK:]
                return carry

            lax.fori_loop(0, N_BLOCKS, block, 0, unroll=4)

        pair = pl.program_id(0)
        for g, d in enumerate(DILATIONS):
            pl.when(pair // 2 == g)(functools.partial(blocks_of, d))
        tabs = (c_ref[...], s1_ref[...], s2_ref[...])
        dq_out[...] = _rope_apply_t(dq_acc[...], *tabs).astype(BF16)
        dk_out[...] = _rope_apply_t(dk_acc[...], *tabs).astype(BF16)
        dv_out[...] = dv_acc[...].astype(BF16)
        _store_columns((dq_out, dk_out, dv_out), dproj_ref,
                       [base // LANES + pair for base in (COL_QA, COL_KA, COL_VA)], sem)

    qkv, tabs = _dil_specs()
    tok = pl.BlockSpec((SEQ, LANES), lambda p: (0, p))
    return pl.pallas_call(
        body, name="dil_attn_bwd", grid=(DIL_WIDTH // LANES,),
        in_specs=qkv + tabs + [tok, tok, tok, _ANY] + [_ANY] * len(deps), out_specs=_ANY,
        out_shape=pltpu.HBM(dproj.shape, dproj.dtype),
        scratch_shapes=[pltpu.VMEM((SEQ, LANES), F32)] * 3 + [pltpu.VMEM((SEQ, LANES), BF16)] * 3
        + [pltpu.VMEM((SEQ, LANES), F32)] * 2 + [pltpu.SemaphoreType.DMA((3,))],
        input_output_aliases={9: 0},
        compiler_params=pltpu.CompilerParams(dimension_semantics=("arbitrary",)),
    )(*_in_hbm(proj, proj, proj, *tables, do, lse, c, dproj), *deps)


def _group_weights(l0, l1, l2):
    m = jnp.maximum(jnp.maximum(l0, l1), l2)
    e0, e1, e2 = jnp.exp(l0 - m), jnp.exp(l1 - m), jnp.exp(l2 - m)
    tot = e0 + e1 + e2
    return e0 / tot, e1 / tot, e2 / tot


def _dil_combine(o, lse, deps=()):
    def fn(o0, o1, o2, l0, l1, l2):
        w0, w1, w2 = _group_weights(l0, l1, l2)
        return w0 * o0 + w1 * o1 + w2 * o2

    w = DIL_OUT_WIDTH
    return _rowwise(fn, "dil_combine", SEQ, 512, [(o, w, g) for g in range(3)] + [(lse, w, g) for g in range(3)], [],
                    [(w, F32)], deps=deps)[0]


def _dil_combine_bwd(d_out, o, lse, deps=()):
    w = DIL_OUT_WIDTH

    def fn(d, o0, o1, o2, l0, l1, l2):
        row = lax.broadcasted_iota(jnp.int32, (w, w), 0) // HEAD_DIM
        col = lax.broadcasted_iota(jnp.int32, (w, w), 1) // HEAD_DIM
        same_head = jnp.where(row == col, 1.0, 0.0).astype(BF16)
        ws = _group_weights(l0, l1, l2)
        dws = [_dot3_nn(d * og, same_head) for og in (o0, o1, o2)]
        mean = ws[0] * dws[0] + ws[1] * dws[1] + ws[2] * dws[2]
        return jnp.concatenate([wg * d for wg in ws], axis=1), jnp.concatenate([-wg * mean for wg in ws], axis=1)

    return _rowwise(fn, "dil_combine_bwd", SEQ, 256,
                    [(d_out, w, 0)] + [(o, w, g) for g in range(3)] + [(lse, w, g) for g in range(3)], [],
                    [(DIL_WIDTH, F32)] * 2, deps=deps)


def _log1p(e):
    u = 1.0 + e
    return jnp.where(u == 1.0, e, jnp.log(u) * (e / (u - 1.0)))


def _fox_gate(proj, b_pad, deps=()):
    def body(f_ref, b_ref, *rest):
        o_ref = rest[-1]
        z = f_ref[...] + b_ref[...]
        logf = (jnp.minimum(z, 0.0) - _log1p(jnp.exp(-jnp.abs(z)))).T[:F_ROWS]
        row = lax.broadcasted_iota(jnp.int32, (BLOCK, BLOCK), 0)
        col = lax.broadcasted_iota(jnp.int32, (BLOCK, BLOCK), 1)
        before = jnp.where(row <= col, 1.0, 0.0).astype(BF16)
        carry = jnp.zeros((F_ROWS, 1), F32)
        for blk in range(N_BLOCKS):
            run = _dot3_nn(logf[:, blk * BLOCK:(blk + 1) * BLOCK], before) + carry
            o_ref[:, blk * BLOCK:(blk + 1) * BLOCK] = run
            carry = run[:, BLOCK - 1:BLOCK]

    return pl.pallas_call(
        body, name="fox_gate", grid=(1,),
        in_specs=[pl.BlockSpec((SEQ, LANES), lambda i: (0, COL_F // LANES)), pl.BlockSpec((1, LANES), lambda i: (0, 0))]
        + [_ANY] * len(deps),
        out_specs=pl.BlockSpec((F_ROWS, SEQ), lambda i: (0, 0)),
        out_shape=pltpu.HBM((F_ROWS, SEQ), F32),
    )(*_in_hbm(proj, b_pad), *deps)


def _fox_gate_bwd(d_cum, proj, b_pad, dproj):
    def body(d_ref, f_ref, b_ref, dproj_ref, dz_ref, db_ref):
        row = lax.broadcasted_iota(jnp.int32, (BLOCK, BLOCK), 0)
        col = lax.broadcasted_iota(jnp.int32, (BLOCK, BLOCK), 1)
        after = jnp.where(row >= col, 1.0, 0.0).astype(BF16)
        carry = jnp.zeros((F_ROWS, 1), F32)
        parts = [None] * N_BLOCKS
        for blk in reversed(range(N_BLOCKS)):
            run = _dot3_nn(d_ref[:, blk * BLOCK:(blk + 1) * BLOCK], after) + carry
            parts[blk] = run
            carry = run[:, 0:1]
        dlogf = jnp.concatenate(parts, axis=1)
        dlogf = jnp.concatenate([dlogf, jnp.zeros((LANES - F_ROWS, SEQ), F32)], axis=0).T
        dz = dlogf * _sigmoid(-(f_ref[...] + b_ref[...]))
        dz_ref[...] = dz.astype(BF16)
        db_ref[...] = jnp.sum(dz, axis=0, keepdims=True)

    f_cols = pl.BlockSpec((SEQ, LANES), lambda i: (0, COL_F // LANES))
    return pl.pallas_call(
        body, name="fox_gate_bwd", grid=(1,),
        in_specs=[pl.BlockSpec((F_ROWS, SEQ), lambda i: (0, 0)), f_cols, pl.BlockSpec((1, LANES), lambda i: (0, 0)), _ANY],
        out_specs=[f_cols, pl.BlockSpec((1, LANES), lambda i: (0, 0))],
        out_shape=[pltpu.HBM(dproj.shape, dproj.dtype), pltpu.HBM((1, LANES), F32)],
        input_output_aliases={3: 0},
    )(*_in_hbm(d_cum, proj, b_pad, dproj))


FOX_TILE = 256
FOX_TILES = SEQ // FOX_TILE


def _row_to_col(row):
    n = row.shape[1]
    eye = lax.broadcasted_iota(jnp.int32, (n, n), 0) == lax.broadcasted_iota(jnp.int32, (n, n), 1)
    return jnp.sum(jnp.where(eye, row, 0.0), axis=1, keepdims=True)


def _fox_bias(f_row, i):
    t = FOX_TILE
    ext = (i + 1) * t
    bias = _row_to_col(f_row[:, i * t:(i + 1) * t]) - f_row[:, :ext]
    row = lax.broadcasted_iota(jnp.int32, (t, ext), 0) + i * t
    col = lax.broadcasted_iota(jnp.int32, (t, ext), 1)
    return bias, col <= row


def _fox_specs():
    qkv = [pl.BlockSpec((SEQ, LANES), lambda p, base=base: (0, base // LANES + p)) for base in (COL_QB, COL_KB, COL_VB)]
    return qkv, pl.BlockSpec((F_ROWS, SEQ), lambda p: (0, 0))


def _fox_fwd(proj, f_rows):
    t = FOX_TILE

    def body(q_ref, k_ref, v_ref, f_ref, o_ref, lse_ref):
        pair = pl.program_id(0)
        upper = _upper_half()
        k16, v16 = k_ref[...].astype(BF16), v_ref[...].astype(BF16)
        f_row = [f_ref[pl.ds(2 * pair + e, 1), :] for e in range(2)]
        for i in range(FOX_TILES):
            ext = (i + 1) * t
            q_tile = (q_ref[i * t:(i + 1) * t, :] * QK_SCALE).astype(BF16)
            s2 = _dot_nt(_stack_heads(q_tile), k16[:ext])
            pns, lses = [], []
            for e in range(2):
                bias, causal = _fox_bias(f_row[e], i)
                s = jnp.where(causal, s2[e * t:(e + 1) * t] + bias, NEG_INF)
                m = jnp.max(s, axis=-1, keepdims=True)
                p = jnp.exp(s - m)
                den = jnp.sum(p, axis=-1, keepdims=True)
                pns.append((p * (1.0 / den)).astype(BF16))
                lses.append(m + jnp.log(den))
            o_ref[i * t:(i + 1) * t, :] = _unstack_heads(_dot_nn(jnp.concatenate(pns, axis=0), v16[:ext]))
            lse_ref[i * t:(i + 1) * t, :] = jnp.where(upper, lses[1], lses[0])

    qkv, f_spec = _fox_specs()
    tok = pl.BlockSpec((SEQ, LANES), lambda p: (0, p))
    return pl.pallas_call(
        body, name="fox_attn_fwd", grid=(FOX_WIDTH // LANES,),
        in_specs=qkv + [f_spec], out_specs=[tok, tok],
        out_shape=[pltpu.HBM((SEQ, FOX_WIDTH), F32)] * 2,
        compiler_params=pltpu.CompilerParams(
            dimension_semantics=("parallel",), vmem_limit_bytes=_vmem_limit(8 * t * SEQ * 4)),
    )(*_in_hbm(proj, proj, proj, f_rows))


def _fox_bwd(proj, do, lse, f_rows, dproj):
    t = FOX_TILE

    def body(q_ref, k_ref, v_ref, f_ref, do_ref, lse_ref, dproj_in, dproj_ref, df_ref, dk_acc, dv_acc,
             dq_out, dk_out, dv_out, sem):
        pair = pl.program_id(0)
        upper = _upper_half()
        k16, v16 = k_ref[...].astype(BF16), v_ref[...].astype(BF16)
        f_row = [f_ref[pl.ds(2 * pair + e, 1), :] for e in range(2)]
        dk_acc[...] = jnp.zeros_like(dk_acc)
        dv_acc[...] = jnp.zeros_like(dv_acc)
        df_ref[...] = jnp.zeros_like(df_ref)
        for i in range(FOX_TILES):
            ext = (i + 1) * t
            q_tile = (q_ref[i * t:(i + 1) * t, :] * QK_SCALE).astype(BF16)
            do_tile = do_ref[i * t:(i + 1) * t, :]
            lse_t = lse_ref[i * t:(i + 1) * t, :]
            q2, do2 = _stack_heads(q_tile), _stack_heads(do_tile)
            s2, dp2 = _dot_nt(q2, k16[:ext]), _dot_nt(do2, v16[:ext])
            ps, dss = [], []
            for e in range(2):
                bias, causal = _fox_bias(f_row[e], i)
                s = s2[e * t:(e + 1) * t] + bias
                p = jnp.where(causal, jnp.exp(s - lse_t[:, e * HEAD_DIM:e * HEAD_DIM + 1]), 0.0)
                dp = dp2[e * t:(e + 1) * t]
                ds = p * (dp - jnp.sum(p * dp, axis=-1, keepdims=True))
                df_ref[0, e:e + 1, :ext] -= jnp.sum(ds, axis=0, keepdims=True)
                ps.append(p.astype(BF16))
                dss.append(ds.astype(BF16))
            ds2, p2 = jnp.concatenate(dss, axis=0), jnp.concatenate(ps, axis=0)
            dq_out[i * t:(i + 1) * t, :] = (_unstack_heads(_dot_nn(ds2, k16[:ext])) * QK_SCALE).astype(BF16)
            dk_acc[:ext, :] += _dot_tn(ds2, q2)
            dv_acc[:ext, :] += _dot_tn(p2, do2)
        dk_out[...] = dk_acc[...].astype(BF16)
        dv_out[...] = dv_acc[...].astype(BF16)
        _store_columns((dq_out, dk_out, dv_out), dproj_ref, [base // LANES + pair for base in (COL_QB, COL_KB, COL_VB)],
                       sem)

    qkv, f_spec = _fox_specs()
    tok = pl.BlockSpec((SEQ, LANES), lambda p: (0, p))
    return pl.pallas_call(
        body, name="fox_attn_bwd", grid=(FOX_WIDTH // LANES,),
        in_specs=qkv + [f_spec, tok, tok, _ANY],
        out_specs=[_ANY, pl.BlockSpec((1, SUBLANES, SEQ), lambda p: (p, 0, 0))],
        out_shape=[pltpu.HBM(dproj.shape, dproj.dtype),
                   pltpu.HBM((FOX_WIDTH // LANES, SUBLANES, SEQ), F32)],
        scratch_shapes=[pltpu.VMEM((SEQ, LANES), F32)] * 2 + [pltpu.VMEM((SEQ, LANES), BF16)] * 3
        + [pltpu.SemaphoreType.DMA((3,))],
        input_output_aliases={6: 0},
        compiler_params=pltpu.CompilerParams(
            dimension_semantics=("arbitrary",), vmem_limit_bytes=_vmem_limit(10 * t * SEQ * 4)),
    )(*_in_hbm(proj, proj, proj, f_rows, do, lse, dproj))


MIX_TILE = 256


def _mix_out(out_a, out_b, proj, x, wt_pa, wt_pb, w_out, g_post, g_ffn_pre):
    tm = MIX_TILE

    def body(a_ref, b_ref, ga_ref, gb_ref, x_ref, wpa_ref, wpb_ref, wo_ref, g2_ref, g3_ref,
             merged_ref, mix_ref, x1_ref, h2_ref):
        ya = _dot_nn(a_ref[...].astype(BF16), wpa_ref[...])
        yb = _dot_nn(b_ref[...].astype(BF16), wpb_ref[...])
        merged = (_sigmoid(ga_ref[...]) * ya + _sigmoid(gb_ref[...]) * yb).astype(BF16)
        merged_ref[...] = merged
        mix = _dot_nn(merged, wo_ref[...])
        mix_ref[...] = mix
        x1 = x_ref[...] + mix * _rms_scale(mix) * g2_ref[...]
        x1_ref[...] = x1
        h2_ref[...] = (x1 * _rms_scale(x1) * g3_ref[...]).astype(BF16)

    def rows(w, cb=0):
        return pl.BlockSpec((tm, w), lambda i, cb=cb: (i, cb))

    def whole(a):
        return pl.BlockSpec(a.shape, lambda i: (0, 0))

    d = D_MODEL
    blk = _nbytes((tm, d), F32) * 6 + sum(_nbytes(a.shape, BF16) for a in (wt_pa, wt_pb, w_out))
    return pl.pallas_call(
        body, name="mix_out", grid=(SEQ // tm,),
        in_specs=[rows(DIL_OUT_WIDTH), rows(FOX_WIDTH), rows(d, COL_GA // d), rows(d, COL_GB // d), rows(d),
                  whole(wt_pa), whole(wt_pb), whole(w_out), whole(g_post), whole(g_ffn_pre)],
        out_specs=[rows(d)] * 4,
        out_shape=[pltpu.HBM((SEQ, d), dt) for dt in (BF16, F32, F32, BF16)],
        compiler_params=pltpu.CompilerParams(dimension_semantics=("parallel",), vmem_limit_bytes=_vmem_limit(blk)),
    )(*_in_hbm(out_a, out_b, proj, proj, x, wt_pa, wt_pb, w_out, g_post, g_ffn_pre))


def _mix_out_bwd(dmix, out_a, out_b, proj, wt_pa, wt_pb, w_out, deps=()):
    tm = MIX_TILE

    def body(dm_ref, a_ref, b_ref, ga_ref, gb_ref, wpa_ref, wpb_ref, wo_ref, *rest):
        dproj_ref, dya_ref, dyb_ref, da_ref, db_ref = rest[len(deps):]
        dmerged = _dot_nt(dm_ref[...], wo_ref[...])
        ya = _dot_nn(a_ref[...].astype(BF16), wpa_ref[...])
        yb = _dot_nn(b_ref[...].astype(BF16), wpb_ref[...])
        sa, sb = _sigmoid(ga_ref[...]), _sigmoid(gb_ref[...])
        dproj_ref[:, COL_GA:COL_GA + D_MODEL] = (dmerged * ya * (sa * (1.0 - sa))).astype(BF16)
        dproj_ref[:, COL_GB:COL_GB + D_MODEL] = (dmerged * yb * (sb * (1.0 - sb))).astype(BF16)
        dproj_ref[:, COL_GB + D_MODEL:] = jnp.zeros((tm, COL_QA - COL_GB - D_MODEL), BF16)
        dya = (dmerged * sa).astype(BF16)
        dyb = (dmerged * sb).astype(BF16)
        dya_ref[...] = dya
        dyb_ref[...] = dyb
        da_ref[...] = _dot_nt(dya, wpa_ref[...])
        db_ref[...] = _dot_nt(dyb, wpb_ref[...]).astype(BF16)

    def rows(w, cb=0):
        return pl.BlockSpec((tm, w), lambda i, cb=cb: (i, cb))

    def whole(a):
        return pl.BlockSpec(a.shape, lambda i: (0, 0))

    d = D_MODEL
    blk = _nbytes((tm, d), F32) * 8 + sum(_nbytes(a.shape, BF16) for a in (wt_pa, wt_pb, w_out))
    return pl.pallas_call(
        body, name="mix_out_bwd", grid=(SEQ // tm,),
        in_specs=[rows(d), rows(DIL_OUT_WIDTH), rows(FOX_WIDTH), rows(d, COL_GA // d), rows(d, COL_GB // d),
                  whole(wt_pa), whole(wt_pb), whole(w_out)] + [_ANY] * len(deps),
        out_specs=[rows(COL_QA)] + [rows(d)] * 2 + [rows(DIL_OUT_WIDTH), rows(FOX_WIDTH)],
        out_shape=[pltpu.HBM((SEQ, PROJ_COLS), BF16)] + [pltpu.HBM((SEQ, d), BF16)] * 2
        + [pltpu.HBM((SEQ, DIL_OUT_WIDTH), F32), pltpu.HBM((SEQ, FOX_WIDTH), BF16)],
        compiler_params=pltpu.CompilerParams(dimension_semantics=("parallel",), vmem_limit_bytes=_vmem_limit(blk)),
    )(*_in_hbm(dmix, out_a, out_b, proj, proj, wt_pa, wt_pb, w_out), *deps)


FFN_TM, FFN_TN = 2048, 256


def _ffn_up(h2, wt_gate, wt_up):
    tm, tn = FFN_TM, FFN_TN

    def body(h_ref, wg_ref, wu_ref, gate_ref, up_ref, act_ref):
        for rows in (slice(0, tm // 2), slice(tm // 2, tm)):
            gate = _dot_nt(h_ref[rows, :], wg_ref[...])
            up = _dot_nt(h_ref[rows, :], wu_ref[...])
            gate_ref[rows, :] = gate
            up_ref[rows, :] = up
            act_ref[rows, :] = (gate * _sigmoid(gate) * up).astype(BF16)

    tile = pl.BlockSpec((tm, tn), lambda i, j: (i, j))
    w_spec = pl.BlockSpec((tn, D_MODEL), lambda i, j: (j, 0))
    return pl.pallas_call(
        body, name="ffn_up", grid=(SEQ // tm, D_FF // tn),
        in_specs=[pl.BlockSpec((tm, D_MODEL), lambda i, j: (i, 0)), w_spec, w_spec],
        out_specs=[tile, tile, tile],
        out_shape=[pltpu.HBM((SEQ, D_FF), dt) for dt in (F32, F32, BF16)],
        compiler_params=pltpu.CompilerParams(
            dimension_semantics=("parallel", "parallel"), vmem_limit_bytes=_vmem_limit(8 * 2**20)),
    )(h2, wt_gate, wt_up)


def _ffn_act_bwd(dff, w_down, gate, up):
    tm, tn = FFN_TM, FFN_TN

    def body(d_ref, wd_ref, gate_ref, up_ref, dgate_ref, dup_ref):
        for rows in (slice(0, tm // 2), slice(tm // 2, tm)):
            dact = _dot_nt(d_ref[rows, :], wd_ref[...])
            gate = gate_ref[rows, :]
            sg = _sigmoid(gate)
            dgate_ref[rows, :] = (dact * up_ref[rows, :] * (sg * (1.0 + gate * (1.0 - sg)))).astype(BF16)
            dup_ref[rows, :] = (dact * (gate * sg)).astype(BF16)

    tile = pl.BlockSpec((tm, tn), lambda i, j: (i, j))
    return pl.pallas_call(
        body, name="ffn_act_bwd", grid=(SEQ // tm, D_FF // tn),
        in_specs=[pl.BlockSpec((tm, D_MODEL), lambda i, j: (i, 0)), pl.BlockSpec((tn, D_MODEL), lambda i, j: (j, 0)),
                  tile, tile],
        out_specs=[tile, tile],
        out_shape=[pltpu.HBM((SEQ, D_FF), BF16)] * 2,
        compiler_params=pltpu.CompilerParams(
            dimension_semantics=("parallel", "parallel"), vmem_limit_bytes=_vmem_limit(8 * 2**20)),
    )(dff, w_down, gate, up)


EPILOGUE_TM = 512


def _loss_head(act, w_down, x1, target, g_post):
    def fn(ff, x1, tgt, g):
        r = _rms_scale(ff)
        nrm = ff * r
        err = (x1 + nrm * g) - tgt
        loss = 0.5 * jnp.sum(jnp.mean(err * err, axis=-1, keepdims=True), axis=0, keepdims=True)
        dy = err * (1.0 / D_MODEL)
        u = dy * g
        dff = r * u - ff * (r * r * r) * jnp.mean(u * ff, axis=-1, keepdims=True)
        return dy, dff, jnp.broadcast_to(loss, (1, LANES)), jnp.sum(dy * nrm, axis=0, keepdims=True)

    d = D_MODEL
    return _matmul_rowwise([(act, w_down)], fn, "ffn_down_loss", EPILOGUE_TM, [(x1, d, 0), (target, d, 0)], [g_post],
                           [(d, F32), (d, BF16)], [LANES, d])


def _post_ffn_bwd(dgate, wt_gate, dup, wt_up, x1, dy, mix, g_ffn_pre, g_mix_post, deps=()):
    def fn(dh2, x1, dy, mix, g3, g2):
        dx, dg3 = _rms_bwd(x1, dh2, g3)
        dx1 = dy + dx
        dmix, dg2 = _rms_bwd(mix, dx1, g2)
        return dx1, dmix, dg3, dg2

    d = D_MODEL
    return _matmul_rowwise([(dgate, wt_gate), (dup, wt_up)], fn, "ffn_up_bwd", EPILOGUE_TM,
                           [(x1, d, 0), (dy, d, 0), (mix, d, 0)], [g_ffn_pre, g_mix_post],
                           [(d, F32), (d, BF16)], [d, d], deps=deps)


def _input_bwd(dproj, wt_r, x, dx1, g_pre, deps=()):
    def fn(dh, x, dx1, g):
        dx, dg = _rms_bwd(x, dh, g)
        return dx1 + dx, dg

    d = D_MODEL
    return _matmul_rowwise([(dproj, wt_r)], fn, "in_proj_bwd", EPILOGUE_TM, [(x, d, 0), (dx1, d, 0)], [g_pre],
                           [(d, F32)], [d], deps=deps)


def _adam_math(w, g, m, v):
    m = ADAM_B1 * m + (1.0 - ADAM_B1) * g
    v = ADAM_B2 * v + (1.0 - ADAM_B2) * (g * g)
    m_hat = m / (1.0 - ADAM_B1 ** ADAM_STEP)
    v_hat = v / (1.0 - ADAM_B2 ** ADAM_STEP)
    delta = -ADAM_LR * (m_hat / (jnp.sqrt(v_hat) + ADAM_EPS) + ADAM_WD * w)
    return delta, m, v


def _adam(w, mine, recv, m, v, name):
    r, c = w.shape
    tc = _col_tile(r, c)

    def body(w_ref, p_ref, r_ref, m_ref, v_ref, g_ref, d_ref, nm_ref, nv_ref):
        g = ((p_ref[...] + r_ref[0].astype(F32)) + r_ref[1].astype(F32)) + r_ref[2].astype(F32)
        g_ref[...] = g
        d_ref[...], nm_ref[...], nv_ref[...] = _adam_math(w_ref[...], g, m_ref[...], v_ref[...])

    spec = pl.BlockSpec((r, tc), lambda j: (0, j))
    return pl.pallas_call(
        body, name=name, grid=(c // tc,),
        in_specs=[spec, spec, pl.BlockSpec((3, r, tc), lambda j: (0, 0, j)), spec, spec], out_specs=[spec] * 4,
        out_shape=[pltpu.HBM((r, c), F32)] * 4,
        compiler_params=pltpu.CompilerParams(dimension_semantics=("parallel",)),
    )(*_in_hbm(w, mine, recv, m, v))


def _adam_small(gathered, ws, ms, vs, loss_parts):
    n = len(ws)

    def body(*refs):
        outs = refs[4 * n + 1:]
        loss = refs[4 * n][0]
        for dev in range(1, N_DEV):
            loss = loss + refs[4 * n][dev]
        outs[4 * n][...] = loss
        for i in range(n):
            ga_ref, w_ref, m_ref, v_ref = (refs[j * n + i] for j in range(4))
            g = ga_ref[0]
            for dev in range(1, N_DEV):
                g = g + ga_ref[dev]
            g = g[:, :w_ref.shape[1]]
            outs[4 * i][...] = g
            outs[4 * i + 1][...], outs[4 * i + 2][...], outs[4 * i + 3][...] = _adam_math(
                w_ref[...], g, m_ref[...], v_ref[...])

    out_shape = [pltpu.HBM(w.shape, F32) for w in ws for _ in range(4)]
    out_shape.append(pltpu.HBM((1, LANES), F32))
    out = pl.pallas_call(body, name="adam_small", out_shape=out_shape)(*gathered, *ws, *ms, *vs, loss_parts)
    return [out[4 * i:4 * i + 4] for i in range(n)], out[4 * n]


_PROJ_SEGMENTS = ((3848, 5896), (None, COL_QA - 2 * D_MODEL), (0, 3840), (3840, 3848), (None, PROJ_COLS - COL_F - 8))


def _proj_weight_t(gathered):
    pieces, zeros, at = [], [], 0
    for lo, hi in _PROJ_SEGMENTS:
        if lo is None:
            zeros.append((at, hi))
            at += hi
            continue
        for dev in range(lo // IN_SHARD, (hi - 1) // IN_SHARD + 1):
            a, b = max(lo, dev * IN_SHARD), min(hi, (dev + 1) * IN_SHARD)
            pieces.append((dev, a - dev * IN_SHARD, at + a - lo, b - a))
        at += hi - lo
    assert at == PROJ_COLS
    tc = 2 * LANES

    def body(g_ref, o_ref, shards, rows):
        for dev in range(N_DEV):
            shards[dev] = g_ref[dev].astype(F32)
        for dev, src, dst, n in pieces:
            rows[pl.ds(dst, n), :] = shards[dev, pl.ds(src, n), :]
        for dst, n in zeros:
            rows[pl.ds(dst, n), :] = jnp.zeros((n, tc), F32)
        o_ref[...] = rows[...].astype(o_ref.dtype)

    return pl.pallas_call(
        body, name="w_in_rows", grid=(D_MODEL // tc,),
        in_specs=[pl.BlockSpec((N_DEV, IN_SHARD, tc), lambda j: (0, 0, j))],
        out_specs=pl.BlockSpec((PROJ_COLS, tc), lambda j: (0, j)),
        out_shape=pltpu.HBM((PROJ_COLS, D_MODEL), gathered.dtype),
        scratch_shapes=[pltpu.VMEM((N_DEV, IN_SHARD, tc), F32), pltpu.VMEM((PROJ_COLS, tc), F32)],
        compiler_params=pltpu.CompilerParams(
            dimension_semantics=("parallel",), vmem_limit_bytes=_vmem_limit(2 * _nbytes((PROJ_COLS, tc), F32))),
    )(*_in_hbm(gathered))


def _proj_weight_grad_slots(dwt_r):
    starts, at = [], 0
    for lo, hi in _PROJ_SEGMENTS:
        if lo is not None:
            starts.append((lo, hi, at))
        at += hi if lo is None else hi - lo
    pieces = []
    for dev in range(N_DEV):
        lo, end = dev * IN_SHARD, (dev + 1) * IN_SHARD
        for seg_lo, seg_hi, seg_at in sorted(starts):
            a, b = max(lo, seg_lo), min(end, seg_hi)
            if a < b:
                pieces.append((dev, a - lo, seg_at + a - seg_lo, b - a))

    def body(g_ref, o_ref):
        for dev, dst, src, rows in pieces:
            o_ref[dev, pl.ds(dst, rows), :] = g_ref[pl.ds(src, rows), :]

    tc = 2 * LANES
    return pl.pallas_call(
        body, name="grad_w_in_slots", grid=(D_MODEL // tc,),
        in_specs=[pl.BlockSpec((PROJ_COLS, tc), lambda j: (0, j))],
        out_specs=pl.BlockSpec((N_DEV, IN_SHARD, tc), lambda j: (0, 0, j)),
        out_shape=pltpu.HBM((N_DEV, IN_SHARD, D_MODEL), F32),
        compiler_params=pltpu.CompilerParams(
            dimension_semantics=("parallel",), vmem_limit_bytes=_vmem_limit(2 * _nbytes((PROJ_COLS, tc), F32))),
    )(*_in_hbm(dwt_r))


def kernel(x, w_in, w_proj_a, w_proj_b, w_out, b_forget, w_ffn_gate, w_ffn_up, w_ffn_down, norm_mix_pre, norm_mix_post, norm_ffn_pre, norm_ffn_post, loss_target, m_w_in, m_w_proj_a, m_w_proj_b, m_w_out, m_b_forget, m_w_ffn_gate, m_w_ffn_up, m_w_ffn_down, m_norm_mix_pre, m_norm_mix_post, m_norm_ffn_pre, m_norm_ffn_post, v_w_in, v_w_proj_a, v_w_proj_b, v_w_out, v_b_forget, v_w_ffn_gate, v_w_ffn_up, v_w_ffn_down, v_norm_mix_pre, v_norm_mix_post, v_norm_ffn_pre, v_norm_ffn_post):
    d = D_MODEL
    names = ("w_in", "w_proj_a", "w_proj_b", "w_out", "w_ffn_gate", "w_ffn_up", "w_ffn_down")
    col_sharded = ("w_in", "w_ffn_gate", "w_ffn_up")

    def row_shards(arrs):
        return {k: (a[0].T if k in col_sharded else a[0]) for k, a in zip(names, arrs)}

    shards = row_shards((w_in, w_proj_a, w_proj_b, w_out, w_ffn_gate, w_ffn_up, w_ffn_down))
    moments_m = row_shards((m_w_in, m_w_proj_a, m_w_proj_b, m_w_out, m_w_ffn_gate, m_w_ffn_up, m_w_ffn_down))
    moments_v = row_shards((v_w_in, v_w_proj_a, v_w_proj_b, v_w_out, v_w_ffn_gate, v_w_ffn_up, v_w_ffn_down))
    pos = jnp.stack([lax.axis_index("c"), 2 * lax.axis_index("x") + lax.axis_index("y")]).astype(jnp.int32)
    x2, target = x[0], loss_target[0]

    me = 4 * lax.axis_index("x") + 2 * lax.axis_index("y") + lax.axis_index("c")
    mid_names, ffn_names = names[1:4], names[4:]
    first_names, later_names = names[:1], names[1:]
    shards16 = {k: shards[k].astype(BF16) for k in names}

    def landing(k):
        return lax.dynamic_update_slice(lax.empty((N_DEV,) + shards[k].shape, BF16), shards16[k][None], (me, 0, 0))

    ag_first = _exchange_start("ag_first_chips_start", _gather_two_route_copies, [shards16[k] for k in first_names],
                               [landing(k) for k in first_names], 4 * len(first_names))
    h = _rowwise(lambda xb, g: xb * _rms_scale(xb) * g, "norm_mix_pre", SEQ, 256, [(x2, d, 0)], [norm_mix_pre],
                 [(d, BF16)], deps=[ag_first.token])[0]
    later_lands = [landing(k) for k in later_names]
    relayed, ag_first_last = _gather_relay(ag_first, "ag_first", [h, shards["w_in"], moments_m["w_in"], moments_v["w_in"],
                                                                 *later_lands, *[shards16[k] for k in later_names]])
    ag_later = _exchange_start("ag_later_chips_start", _gather_chips_copies, [shards16[k] for k in later_names],
                               later_lands, 3 * len(later_names), after=[relayed])
    gathered = dict(zip(first_names, ag_first_last([ag_later.token])))
    wt_r = _proj_weight_t(gathered["w_in"])

    proj = _matmul([(h, *_in_hbm(wt_r))], "nt", F32, "in_proj", 1024, 896, 1024)
    tables = _rope_tables()
    o_dil, lse_dil = _dil_fwd(proj, tables)
    out_a = _dil_combine(o_dil, lse_dil)
    _, lands = _exchange_wait("ag_later_chips_wait", ag_later, [out_a])
    ag_later = _exchange_start("ag_later_sibling_start", _gather_sibling_copies, [], lands, 4 * len(later_names))

    b_pad = jnp.pad(b_forget, ((0, 0), (0, LANES - N_FOX_HEADS)))
    f_rows = _fox_gate(proj, b_pad, deps=[ag_later.token])
    out_b, lse_fox = _fox_fwd(proj, f_rows)

    gathered = dict(zip(later_names, _exchange_wait("ag_later_sibling_wait", ag_later, [out_b])[1]))
    wt_pa = gathered["w_proj_a"].transpose(1, 0, 2).reshape(DIL_OUT_WIDTH, d)
    wt_pb = gathered["w_proj_b"].transpose(1, 0, 2).reshape(FOX_WIDTH, d)
    w_o = gathered["w_out"].reshape(d, d)
    wt_g = gathered["w_ffn_gate"].reshape(D_FF, d)
    wt_u = gathered["w_ffn_up"].reshape(D_FF, d)
    w_d = gathered["w_ffn_down"].reshape(D_FF, d)
    merged, mix, x1, h2 = _mix_out(out_a, out_b, proj, x2, wt_pa, wt_pb, w_o, norm_mix_post, norm_ffn_pre)

    gate, up, act = _ffn_up(h2, wt_g, wt_u)
    dy, dff, loss_part, dg_ffn_post = _loss_head(act, w_d, x1, target, norm_ffn_post)

    dgate, dup = _ffn_act_bwd(dff, w_d, gate, up)
    grads_t = {}
    grads_t["w_ffn_down"] = _matmul([(act, dff)], "tn", F32, "grad_w_ffn_down", 1408, 512, 2048, staged=False)
    grads_t["w_ffn_gate"] = _matmul([(dgate, h2)], "tn", F32, "grad_w_ffn_gate", 1408, 512, 2048)
    grads_t["w_ffn_up"] = _matmul([(dup, h2)], "tn", F32, "grad_w_ffn_up", 1408, 512, 2048)
    rs_ffn = _ReduceScatter("ffn", {k: grads_t[k] for k in ffn_names}, pos)
    dx1, dmix, dg_ffn_pre, dg_mix_post = _post_ffn_bwd(dgate, wt_g, dup, wt_u, x1, dy, mix, norm_ffn_pre, norm_mix_post,
                                                       deps=[rs_ffn.token])
    rs_ffn.start_chips([dmix])

    dproj, dya, dyb, d_out_a, d_out_b = _mix_out_bwd(dmix, out_a, out_b, proj, wt_pa, wt_pb, w_o, deps=[rs_ffn.token])
    grads_t["w_out"] = _matmul([(merged, dmix)], "tn", F32, "grad_w_out", 1024, 1024, 1024)
    def column_slots(g):
        return g.reshape(g.shape[0], N_DEV, LANES).transpose(1, 0, 2)

    grads_t["w_proj_a"] = column_slots(_matmul([(out_a, dya)], "tn", F32, "grad_w_proj_a", DIL_OUT_WIDTH, 1024, SEQ))
    grads_t["w_proj_b"] = column_slots(_matmul([(out_b, dyb)], "tn", F32, "grad_w_proj_b", FOX_WIDTH, 1024, SEQ))
    rs_mid = _ReduceScatter("mid", {k: grads_t[k] for k in mid_names}, pos)

    do_dil, c_dil = _dil_combine_bwd(d_out_a, o_dil, lse_dil, deps=[rs_mid.token])
    rs_mid.start_chips([c_dil])
    dproj, d_cum = _fox_bwd(proj, d_out_b, lse_fox, f_rows, dproj)
    d_cum_rows = jnp.pad(d_cum[:, :2].reshape(N_FOX_HEADS, SEQ), ((0, F_ROWS - N_FOX_HEADS), (0, 0)))
    dproj, db_part = _fox_gate_bwd(d_cum_rows, proj, b_pad, dproj)
    dproj = _dil_bwd(proj, tables, do_dil, lse_dil, c_dil, dproj, deps=[rs_mid.token])

    dwt_r = _matmul([(dproj, h)], "tn", F32, "grad_w_in", 896, 1024, 2048)
    rs_in = _ReduceScatter("in", {"w_in": _proj_weight_grad_slots(dwt_r)}, pos)
    def finish(rs, after):
        return {k: _adam(shards[k], mine, recv, moments_m[k], moments_v[k], "adam_" + k)
                for k, (mine, recv) in rs.finish(after).items()}

    done = finish(rs_ffn, [rs_in.token])
    rs_in.start_chips([done[k][0] for k in ffn_names])
    grad_x, dg_mix_pre = _input_bwd(dproj, wt_r, x2, dx1, norm_mix_pre, deps=[rs_in.token])
    done.update(finish(rs_mid, [grad_x]))

    small_all = _all_gather([dg_mix_pre, dg_mix_post, dg_ffn_pre, dg_ffn_post, db_part, loss_part],
                            "small_grads_all_gather", deps=[done[k][0] for k in mid_names])
    small, loss = _adam_small(small_all[:5], [norm_mix_pre, norm_mix_post, norm_ffn_pre, norm_ffn_post, b_forget],
                              [m_norm_mix_pre, m_norm_mix_post, m_norm_ffn_pre, m_norm_ffn_post, m_b_forget],
                              [v_norm_mix_pre, v_norm_mix_post, v_norm_ffn_pre, v_norm_ffn_post, v_b_forget],
                              small_all[5])

    done.update(finish(rs_in, [small[0][0]]))

    def leaves(i):
        def nat(k):
            a = done[k][i]
            return (a.T if k in col_sharded else a)[None]

        return [nat("w_in"), nat("w_proj_a"), nat("w_proj_b"), nat("w_out"), small[4][i],
                nat("w_ffn_gate"), nat("w_ffn_up"), nat("w_ffn_down"), *[small[r][i] for r in range(4)]]

    return (loss[0, 0], grad_x[None], *leaves(0), *leaves(1), *leaves(2), *leaves(3))
```

```python
import functools
import math

import jax
import jax.numpy as jnp
import numpy as np
from jax import lax
from jax.experimental import pallas as pl
from jax.experimental.pallas import tpu as pltpu

F32 = jnp.float32
BF16 = jnp.bfloat16
MESH = pl.DeviceIdType.MESH

D_MODEL = 1024
SEQ = 2048
HEAD_DIM = 64
BLOCK = 128
N_BLOCKS = SEQ // BLOCK
DILATIONS = (1, 4, 16)
N_FOX_HEADS = 8
DIL_WIDTH = 768
DIL_OUT_WIDTH = 256
FOX_WIDTH = 512
D_FF = 2816
ROPE_THETA = 500000.0
ROPE_DIM = HEAD_DIM // 4
ROPE_HALF = ROPE_DIM // 2
EPS = 1e-6
NEG_INF = -1e30
QK_SCALE = 1.0 / math.sqrt(HEAD_DIM)
IN_COLS = 5896
N_DEV = 8
IN_SHARD = IN_COLS // N_DEV

ADAM_LR = 0.001
ADAM_B1 = 0.9
ADAM_B2 = 0.999
ADAM_EPS = 1e-08
ADAM_WD = 0.01
ADAM_STEP = 10

V7X_VMEM_BYTES = 64 * 2**20
LANES = 128
SUBLANES = 8

PROJ_COLS = 6272
COL_GA, COL_GB = 0, 1024
COL_QA, COL_KA, COL_VA = 2304, 3072, 3840
COL_QB, COL_KB, COL_VB = 4608, 5120, 5632
COL_F = 6144
F_ROWS = 16


def _vmem_limit(block_bytes):
    want = 2 * block_bytes + 16 * 2**20
    return int(min(max(want, 32 * 2**20), V7X_VMEM_BYTES - 8 * 2**20))


def _nbytes(shape, dtype):
    return math.prod(shape) * jnp.dtype(dtype).itemsize


def _in_hbm(*arrays):
    return [pltpu.with_memory_space_constraint(a, pltpu.HBM) for a in arrays]


def _dot(a, b, dims):
    return lax.dot_general(a, b, (dims, ((), ())), preferred_element_type=F32)


def _dot_nn(a, b):
    return _dot(a, b, ((1,), (0,)))


def _dot_nt(a, b):
    return _dot(a, b, ((1,), (1,)))


def _dot_tn(a, b):
    return _dot(a, b, ((0,), (0,)))


def _sigmoid(z):
    return 1.0 / (1.0 + jnp.exp(-z))


def _split3(x):
    hi = x.astype(BF16)
    r1 = x - hi.astype(F32)
    mid = r1.astype(BF16)
    lo = (r1 - mid.astype(F32)).astype(BF16)
    return hi, mid, lo


def _dot3_nn(x, ones_matrix):
    hi, mid, lo = _split3(x)
    return (_dot_nn(hi, ones_matrix) + _dot_nn(mid, ones_matrix)) + _dot_nn(lo, ones_matrix)


def _rowwise(fn, name, n_rows, tm, row_ins, bcast_ins, row_outs, acc_outs=(), deps=()):
    n_in = len(row_ins) + len(bcast_ins)
    n_ro = len(row_outs)

    def body(*refs):
        res = fn(*[r[...] for r in refs[:n_in]])
        if not isinstance(res, (tuple, list)):
            res = (res,)
        outs = refs[n_in + len(deps):]
        for r, o in zip(res[:n_ro], outs[:n_ro]):
            o[...] = r.astype(o.dtype)
        first = pl.program_id(0) == 0
        for r, o in zip(res[n_ro:], outs[n_ro:]):
            _accumulate(o, r, first)

    in_specs = [pl.BlockSpec((tm, w), lambda i, cb=cb: (i, cb)) for _, w, cb in row_ins]
    in_specs += [pl.BlockSpec(a.shape, lambda i: (0, 0)) for a in bcast_ins]
    in_specs += [pl.BlockSpec(memory_space=pl.ANY)] * len(deps)
    out_specs = [pl.BlockSpec((tm, w), lambda i: (i, 0)) for w, _ in row_outs]
    out_specs += [pl.BlockSpec((1, w), lambda i: (0, 0)) for w in acc_outs]
    out_shape = [pltpu.HBM((n_rows, w), dt) for w, dt in row_outs]
    out_shape += [pltpu.HBM((1, w), F32) for w in acc_outs]
    blk = sum(_nbytes((tm, w), a.dtype) for a, w, _ in row_ins) + sum(_nbytes((tm, w), dt) for w, dt in row_outs)
    return pl.pallas_call(
        body, name=name, grid=(n_rows // tm,), in_specs=in_specs, out_specs=out_specs, out_shape=out_shape,
        compiler_params=pltpu.CompilerParams(
            dimension_semantics=("arbitrary" if acc_outs else "parallel",), vmem_limit_bytes=_vmem_limit(3 * blk)),
    )(*_in_hbm(*[a for a, _, _ in row_ins], *bcast_ins), *deps)


def _accumulate(o_ref, part, first):
    @pl.when(first)
    def _():
        o_ref[...] = part

    @pl.when(jnp.logical_not(first))
    def _():
        o_ref[...] += part


_MM_DIMS = {"nn": ((1,), (0,)), "nt": ((1,), (1,)), "tn": ((0,), (0,))}


def _matmul(pairs, mode, out_dtype, name, tm, tn, tk, deps=(), staged=True):
    a0, b0 = pairs[0]
    if mode == "tn":
        kk, m = a0.shape
    else:
        m, kk = a0.shape
    n = b0.shape[0] if mode == "nt" else b0.shape[1]
    assert m % tm == 0 and n % tn == 0 and kk % tk == 0, (name, m, n, kk)
    nk = kk // tk
    n_pairs = len(pairs)
    dims = _MM_DIMS[mode]
    n_in = 2 * n_pairs + len(deps)

    def body(*refs):
        o_ref = refs[n_in]
        part = None
        for p in range(n_pairs):
            d = _dot(refs[2 * p][...].astype(BF16), refs[2 * p + 1][...].astype(BF16), dims)
            part = d if part is None else part + d
        if nk == 1:
            o_ref[...] = part.astype(o_ref.dtype)
            return
        acc = refs[n_in + 1]
        k = pl.program_id(2)

        @pl.when(k == 0)
        def _():
            acc[...] = part

        @pl.when(k > 0)
        def _():
            acc[...] += part

        @pl.when(k == nk - 1)
        def _():
            o_ref[...] = acc[...].astype(o_ref.dtype)

    if mode == "tn":
        a_spec = pl.BlockSpec((tk, tm), lambda i, j, k: (k, i))
    else:
        a_spec = pl.BlockSpec((tm, tk), lambda i, j, k: (i, k))
    if mode == "nt":
        b_spec = pl.BlockSpec((tn, tk), lambda i, j, k: (j, k))
    else:
        b_spec = pl.BlockSpec((tk, tn), lambda i, j, k: (k, j))
    blk = sum(_nbytes((tm, tk), a.dtype) + _nbytes((tk, tn), b.dtype) for a, b in pairs) + 2 * _nbytes((tm, tn), F32)
    flat = [a for pair in pairs for a in pair]
    return pl.pallas_call(
        body, name=name, grid=(m // tm, n // tn, nk),
        in_specs=[a_spec, b_spec] * n_pairs + [pl.BlockSpec(memory_space=pl.ANY)] * len(deps),
        out_specs=pl.BlockSpec((tm, tn), lambda i, j, k: (i, j)),
        out_shape=pltpu.HBM((m, n), out_dtype),
        scratch_shapes=[] if nk == 1 else [pltpu.VMEM((tm, tn), F32)],
        compiler_params=pltpu.CompilerParams(
            dimension_semantics=("parallel", "parallel", "arbitrary"), vmem_limit_bytes=_vmem_limit(blk)),
    )(*(flat if staged else _in_hbm(*flat)), *deps)


def _matmul_rowwise(pairs, fn, name, tm, row_ins, bcast_ins, row_outs, acc_outs=(), deps=()):
    m = pairs[0][0].shape[0]
    n_mm, n_in = 2 * len(pairs), len(row_ins) + len(bcast_ins)
    n_ro = len(row_outs)

    def body(*refs):
        prod = None
        for p in range(len(pairs)):
            part = _dot_nn(refs[2 * p][...].astype(BF16), refs[2 * p + 1][...].astype(BF16))
            prod = part if prod is None else prod + part
        res = fn(prod, *[r[...] for r in refs[n_mm:n_mm + n_in]])
        outs = refs[n_mm + n_in + len(deps):]
        for r, o in zip(res[:n_ro], outs[:n_ro]):
            o[...] = r.astype(o.dtype)
        first = pl.program_id(0) == 0
        for r, o in zip(res[n_ro:], outs[n_ro:]):
            _accumulate(o, r, first)

    in_specs = []
    for a, b in pairs:
        in_specs += [pl.BlockSpec((tm, a.shape[1]), lambda i: (i, 0)),
                     pl.BlockSpec(b.shape, lambda i: (0, 0), pipeline_mode=pl.Buffered(1))]
    in_specs += [pl.BlockSpec((tm, w), lambda i, cb=cb: (i, cb)) for _, w, cb in row_ins]
    in_specs += [pl.BlockSpec(a.shape, lambda i: (0, 0)) for a in bcast_ins]
    in_specs += [_ANY] * len(deps)
    out_specs = [pl.BlockSpec((tm, w), lambda i: (i, 0)) for w, _ in row_outs]
    out_specs += [pl.BlockSpec((1, w), lambda i: (0, 0)) for w in acc_outs]
    out_shape = [pltpu.HBM((m, w), dt) for w, dt in row_outs]
    out_shape += [pltpu.HBM((1, w), F32) for w in acc_outs]
    blk = sum(_nbytes((tm, a.shape[1]), a.dtype) + _nbytes(b.shape, b.dtype) // 2 for a, b in pairs)
    blk += sum(_nbytes((tm, w), a.dtype) for a, w, _ in row_ins) + sum(_nbytes((tm, w), dt) for w, dt in row_outs)
    return pl.pallas_call(
        body, name=name, grid=(m // tm,), in_specs=in_specs, out_specs=out_specs, out_shape=out_shape,
        compiler_params=pltpu.CompilerParams(dimension_semantics=("arbitrary",), vmem_limit_bytes=_vmem_limit(blk)),
    )(*[a for pair in pairs for a in pair], *[a for a, _, _ in row_ins], *bcast_ins, *deps)


def _rms_scale(x):
    return lax.rsqrt(jnp.mean(x * x, axis=-1, keepdims=True) + EPS)


def _rms_bwd(xin, dyn, g):
    r = _rms_scale(xin)
    u = dyn * g
    dx = r * u - xin * (r * r * r) * jnp.mean(u * xin, axis=-1, keepdims=True)
    dg = jnp.sum(dyn * xin * r, axis=0, keepdims=True)
    return dx, dg


def _mesh_pos():
    return lax.axis_index("x"), lax.axis_index("y"), lax.axis_index("c")


def _all_gather(xs, name, deps=()):
    n = len(xs)

    def body(*refs):
        x_refs, out_refs = refs[:n], refs[n + len(deps):2 * n + len(deps)]
        send_sems, recv_sems, local_sems = refs[2 * n + len(deps):]
        mx, my, mc = _mesh_pos()
        me, sib = (mx, my, mc), (mx, my, 1 - mc)
        chips = [(1 - mx, my), (mx, 1 - my), (1 - mx, 1 - my)]

        def slot(a, dev):
            px, py, pc = dev
            return out_refs[a].at[4 * px + 2 * py + pc]

        def copy(k, a, block, to, src=None):
            return pltpu.make_async_remote_copy(
                src_ref=slot(a, block) if src is None else src, dst_ref=slot(a, block),
                send_sem=send_sems.at[a * 7 + k], recv_sem=recv_sems.at[a * 7 + k],
                device_id=to, device_id_type=MESH)

        mine = [pltpu.make_async_copy(x_refs[a], slot(a, me), local_sems.at[a]) for a in range(n)]
        for cp in mine:
            cp.start()
        first = []
        for a in range(n):
            first.append(copy(0, a, me, sib, x_refs[a]))
            first += [copy(1 + j, a, me, (*chip, mc), x_refs[a]) for j, chip in enumerate(chips)]
        for cp in first:
            cp.start()
        passed = []
        for a in range(n):
            for j, chip in enumerate(chips):
                copy(1 + j, a, (*chip, mc), me).wait_recv()
                fwd = copy(4 + j, a, (*chip, mc), sib)
                fwd.start()
                passed.append(fwd)
        for a in range(n):
            copy(0, a, sib, me).wait_recv()
            for j, chip in enumerate(chips):
                copy(4 + j, a, (*chip, 1 - mc), me).wait_recv()
        for cp in first + passed:
            cp.wait_send()
        for cp in mine:
            cp.wait()

    hbm = pl.BlockSpec(memory_space=pl.ANY)
    return pl.pallas_call(
        body, name=name,
        out_shape=[pltpu.HBM((N_DEV,) + x.shape, x.dtype) for x in xs],
        in_specs=[hbm] * (n + len(deps)), out_specs=[hbm] * n,
        scratch_shapes=[pltpu.SemaphoreType.DMA((7 * n,)), pltpu.SemaphoreType.DMA((7 * n,)),
                        pltpu.SemaphoreType.DMA((n,))],
    )(*xs, *deps)


_HBM = pl.BlockSpec(memory_space=pltpu.HBM)
_SEM = pl.BlockSpec(memory_space=pltpu.SEMAPHORE)
_ANY = pl.BlockSpec(memory_space=pl.ANY)
_DATAFLOW = pltpu.SideEffectType.DATAFLOW_SIDE_EFFECTING


def _flip_peer(flip):
    mx, my, mc = _mesh_pos()
    return (1 - mx if flip & 2 else mx, 1 - my if flip & 1 else my, mc)


def _remote(src, dst, send_sems, recv_sems, k, peer):
    return pltpu.make_async_remote_copy(src_ref=src, dst_ref=dst, send_sem=send_sems.at[k], recv_sem=recv_sems.at[k],
                                        device_id=peer, device_id_type=MESH)


def _scatter_sibling_copies(srcs, lands, send_sems, recv_sems):
    mx, my, mc = _mesh_pos()
    return [_remote(srcs[a].at[k, 1 - mc], lands[a].at[k], send_sems, recv_sems, 4 * a + k, (mx, my, 1 - mc))
            for a in range(len(srcs)) for k in range(4)]


def _scatter_chips_copies(srcs, lands, send_sems, recv_sems):
    mx, my, _ = _mesh_pos()
    k0 = 2 * mx + my
    return [_remote(srcs[a].at[jnp.bitwise_xor(k0, flip)], lands[a].at[flip - 1], send_sems, recv_sems,
                    3 * a + flip - 1, _flip_peer(flip))
            for a in range(len(srcs)) for flip in (1, 2, 3)]


class _Exchange:
    def __init__(self, copies, n_src, send_sems, recv_sems, thru, token):
        self.copies, self.n_src, self.send_sems, self.recv_sems, self.thru, self.token = (
            copies, n_src, send_sems, recv_sems, thru, token)


def _exchange_start(name, copies, srcs, lands, n_copies, after=()):
    bufs = list(srcs) + list(lands)
    nb, ns = len(bufs), len(srcs)

    def body(*refs):
        send_sems, recv_sems = refs[nb + len(after)], refs[nb + len(after) + 1]
        for cp in copies(refs[:ns], refs[ns:nb], send_sems, recv_sems):
            cp.start()
        refs[-1][...] = jnp.zeros_like(refs[-1])

    out = pl.pallas_call(
        body, name=name,
        out_shape=(pltpu.SemaphoreType.DMA((n_copies,)), pltpu.SemaphoreType.DMA((n_copies,)),
                   *[pltpu.HBM(b.shape, b.dtype) for b in bufs], pltpu.HBM((SUBLANES, LANES), F32)),
        in_specs=[_HBM] * nb + [_ANY] * len(after),
        out_specs=(_SEM, _SEM, *[_HBM] * nb, pl.BlockSpec(memory_space=pltpu.VMEM)),
        input_output_aliases={i: 2 + i for i in range(nb)},
        compiler_params=pltpu.CompilerParams(has_side_effects=_DATAFLOW),
    )(*[pltpu.with_memory_space_constraint(b, pltpu.HBM) for b in bufs], *after)
    return _Exchange(copies, ns, out[0], out[1], list(out[2:2 + nb]), out[-1])


def _exchange_wait(name, ex, after):
    nb, ns = len(ex.thru), ex.n_src

    def body(*refs):
        for cp in ex.copies(refs[:ns], refs[ns:nb], refs[nb], refs[nb + 1]):
            cp.wait_send()
            cp.wait_recv()

    out = pl.pallas_call(
        body, name=name, out_shape=tuple(pltpu.HBM(b.shape, b.dtype) for b in ex.thru),
        in_specs=[_HBM] * nb + [_SEM, _SEM] + [_ANY] * len(after), out_specs=tuple([_HBM] * nb),
        input_output_aliases={i: i for i in range(nb)},
        compiler_params=pltpu.CompilerParams(has_side_effects=_DATAFLOW),
    )(*ex.thru, ex.send_sems, ex.recv_sems, *after)
    return list(out[:ns]), list(out[ns:])


def _halves(ref):
    half = ref.shape[1] // 2
    if half % LANES == 0:
        return ref.at[:, pl.ds(0, half)], ref.at[:, pl.ds(half, half)]
    half = ref.shape[0] // 2
    assert half % (2 * SUBLANES) == 0, ref.shape
    return ref.at[pl.ds(0, half)], ref.at[pl.ds(half, half)]


def _gather_two_route_copies(srcs, lands, send_sems, recv_sems):
    mx, my, mc = _mesh_pos()
    me = 4 * mx + 2 * my + mc
    return [_remote(_halves(srcs[a])[h], _halves(lands[a].at[me])[h], send_sems, recv_sems, 4 * a + i, _flip_peer(flip))
            for a in range(len(srcs)) for i, (flip, h) in enumerate(((2, 0), (1, 1), (2, 1), (1, 0)))]


def _to_sibling(land, flip, h, send_sems, recv_sems, k):
    mx, my, mc = _mesh_pos()
    part = _halves(land.at[2 * jnp.bitwise_xor(2 * mx + my, flip) + mc])[h]
    return _remote(part, part, send_sems, recv_sems, k, (mx, my, 1 - mc))


def _second_hop(land, send_sems, recv_sems, k):
    mx, my, mc = _mesh_pos()
    k0 = 2 * mx + my
    from_y = _halves(land.at[2 * jnp.bitwise_xor(k0, 1) + mc])[1]
    from_x = _halves(land.at[2 * jnp.bitwise_xor(k0, 2) + mc])[0]
    return (_remote(from_y, from_y, send_sems, recv_sems, k, _flip_peer(2)),
            _remote(from_x, from_x, send_sems, recv_sems, k + 1, _flip_peer(1)))


def _relay_call(name, body, bufs, sems, n_new, after):
    nb, n_in = len(bufs), len(bufs) + len(sems) + len(after)

    def call_body(*refs):
        body(refs[:nb], refs[nb:nb + len(sems)], refs[n_in], refs[n_in + 1])
        refs[-1][...] = jnp.zeros_like(refs[-1])

    out = pl.pallas_call(
        call_body, name=name,
        out_shape=(pltpu.SemaphoreType.DMA((n_new,)), pltpu.SemaphoreType.DMA((n_new,)),
                   *[pltpu.HBM(b.shape, b.dtype) for b in bufs], pltpu.HBM((SUBLANES, LANES), F32)),
        in_specs=[_HBM] * nb + [_SEM] * len(sems) + [_ANY] * len(after),
        out_specs=(_SEM, _SEM, *[_HBM] * nb, pl.BlockSpec(memory_space=pltpu.VMEM)),
        input_output_aliases={i: 2 + i for i in range(nb)},
        compiler_params=pltpu.CompilerParams(has_side_effects=_DATAFLOW),
    )(*bufs, *sems, *after)
    return out[0], out[1], list(out[2:2 + nb]), out[-1]


def _gather_relay(ex, tag, after_first, after_second):
    ns = ex.n_src
    n = len(ex.thru) - ns

    def first(bufs, sems, send, recv):
        lands, first_hop = bufs[ns:], ex.copies(bufs[:ns], bufs[ns:], *sems)
        for a in range(n):
            for h in (0, 1):
                _to_sibling(lands[a], 0, h, send, recv, 10 * a + h).start()
        for a in range(n):
            x_first, y_second, x_second, y_first = first_hop[4 * a:4 * a + 4]
            to_x, to_y = _second_hop(lands[a], send, recv, 10 * a + 8)
            x_first.wait_recv()
            to_y.start()
            _to_sibling(lands[a], 2, 0, send, recv, 10 * a + 4).start()
            y_second.wait_recv()
            to_x.start()
            _to_sibling(lands[a], 1, 1, send, recv, 10 * a + 3).start()
            x_second.wait_recv()
            _to_sibling(lands[a], 2, 1, send, recv, 10 * a + 5).start()
            y_first.wait_recv()
            _to_sibling(lands[a], 1, 0, send, recv, 10 * a + 2).start()
        for cp in first_hop:
            cp.wait_send()

    def second(lands, sems, send, recv):
        for a in range(n):
            to_x, to_y = _second_hop(lands[a], *sems, 10 * a + 8)
            to_y.wait_recv()
            _to_sibling(lands[a], 3, 0, send, recv, 2 * a).start()
            to_x.wait_recv()
            _to_sibling(lands[a], 3, 1, send, recv, 2 * a + 1).start()
            to_x.wait_send()
            to_y.wait_send()

    def last(lands, sems, send, recv):
        for a in range(n):
            for flip in range(4):
                for h in (0, 1):
                    s, r, k = (sems[2], sems[3], 2 * a + h) if flip == 3 else (sems[0], sems[1], 10 * a + 2 * flip + h)
                    cp = _to_sibling(lands[a], flip, h, s, r, k)
                    cp.wait_send()
                    cp.wait_recv()

    send1, recv1, bufs, _ = _relay_call(f"{tag}_chips_relay", first, ex.thru, [ex.send_sems, ex.recv_sems], 10 * n,
                                        after_first)
    send2, recv2, lands, token = _relay_call(f"{tag}_diagonal_relay", second, bufs[ns:], [send1, recv1], 2 * n,
                                             after_second)
    return token, lambda after_last: _relay_call(f"{tag}_sibling_wait", last, lands, [send1, recv1, send2, recv2], 1,
                                                 after_last)[2]


def _col_tile(r, c):
    return next(t for t in (1024, 512, 256, 128) if c % t == 0 and (r * t * 4 <= 2**20 or t == 128))


def _add_sibling(g4, recv, pos, name):
    _, _, r, c = g4.shape
    tc = _col_tile(r, c)

    def body(pos_ref, g_ref, r_ref, o16_ref, mine_ref):
        s = g_ref[0, 0] + r_ref[0]
        o16_ref[0] = s.astype(BF16)

        @pl.when(pl.program_id(1) == pos_ref[1])
        def _():
            mine_ref[...] = s

    slot = pl.BlockSpec((1, r, tc), lambda j, k, pos_ref: (k, 0, j))
    return pl.pallas_call(
        body, name=name,
        out_shape=[pltpu.HBM((4, r, c), BF16), pltpu.HBM((r, c), F32)],
        grid_spec=pltpu.PrefetchScalarGridSpec(
            num_scalar_prefetch=1, grid=(c // tc, 4),
            in_specs=[pl.BlockSpec((1, 1, r, tc), lambda j, k, pos_ref: (k, pos_ref[0], 0, j)), slot],
            out_specs=[slot, pl.BlockSpec((r, tc), lambda j, k, pos_ref: (0, j))]),
        compiler_params=pltpu.CompilerParams(dimension_semantics=("parallel", "arbitrary")),
    )(pos, *_in_hbm(g4, recv))


class _ReduceScatter:
    def __init__(self, tag, grads_t, pos):
        self.tag, self.pos, self.names = tag, pos, list(grads_t)
        g4s = [g.reshape(4, 2, g.size // (N_DEV * g.shape[-1]), g.shape[-1]) for g in grads_t.values()]
        lands = [lax.empty((4,) + g.shape[2:], F32) for g in g4s]
        self.ex = _exchange_start(f"rs_{tag}_sibling_start", _scatter_sibling_copies, g4s, lands, 4 * len(g4s))
        self.token = self.ex.token

    def start_chips(self, after):
        g4s, from_sibling = _exchange_wait(f"rs_{self.tag}_sibling_wait", self.ex, after)
        parts = [_add_sibling(g4, rv, self.pos, f"rs_add_sibling_{k}")
                 for k, g4, rv in zip(self.names, g4s, from_sibling)]
        self.mine = [mine for _, mine in parts]
        p16s = [p16 for p16, _ in parts]
        lands = [lax.empty((3,) + p.shape[1:], BF16) for p in p16s]
        self.ex = _exchange_start(f"rs_{self.tag}_chips_start", _scatter_chips_copies, p16s, lands, 3 * len(p16s))
        self.token = self.ex.token

    def finish(self, after):
        _, from_chips = _exchange_wait(f"rs_{self.tag}_chips_wait", self.ex, after)
        return dict(zip(self.names, zip(self.mine, from_chips)))


def _rope_tables():
    positions = np.arange(SEQ, dtype=np.float32)
    inv_freq = np.power(np.float32(ROPE_THETA), -np.arange(0, ROPE_DIM, 2, dtype=np.float32) / np.float32(ROPE_DIM))
    ang = (positions[:, None] * inv_freq[None, :]).astype(np.float32)
    cos, sin = np.cos(ang).astype(np.float32), np.sin(ang).astype(np.float32)
    ones = np.ones((SEQ, HEAD_DIM - ROPE_DIM), np.float32)
    zeros8 = np.zeros((SEQ, ROPE_HALF), np.float32)
    zeros = np.zeros((SEQ, HEAD_DIM - ROPE_DIM), np.float32)
    c_head = np.concatenate([cos, cos, ones], axis=1)
    s1_head = np.concatenate([-sin, zeros8, zeros], axis=1)
    s2_head = np.concatenate([zeros8, sin, zeros], axis=1)
    return tuple(jnp.asarray(np.concatenate([t, t], axis=1)) for t in (c_head, s1_head, s2_head))


def _rope_apply(x, c, s1, s2):
    w = x.shape[1]
    return x * c + pltpu.roll(x, w - ROPE_HALF, 1) * s1 + pltpu.roll(x, ROPE_HALF, 1) * s2


def _rope_apply_t(dy, c, s1, s2):
    w = dy.shape[1]
    return dy * c + pltpu.roll(dy * s1, ROPE_HALF, 1) + pltpu.roll(dy * s2, w - ROPE_HALF, 1)


def _dil_prev_limit(has_prev):
    return jnp.where(has_prev, 0, BLOCK)


def _dil_valid(limit):
    row = lax.broadcasted_iota(jnp.int32, (BLOCK, 2 * BLOCK), 0)
    col = lax.broadcasted_iota(jnp.int32, (BLOCK, 2 * BLOCK), 1)
    dist = col - row
    return jnp.logical_and(dist >= jnp.where(col < BLOCK, limit, -BLOCK), dist <= BLOCK)


def _upper_half():
    return lax.broadcasted_iota(jnp.int32, (1, LANES), 1) >= HEAD_DIM


def _stack_heads(x):
    upper = _upper_half()
    return jnp.concatenate([jnp.where(upper, 0, x), jnp.where(upper, x, 0)], axis=0)


def _unstack_heads(y):
    n = y.shape[0] // 2
    return jnp.where(_upper_half(), y[n:], y[:n])


def _head_columns(t):
    return jnp.concatenate([t[:, 0:1], t[:, HEAD_DIM:HEAD_DIM + 1]], axis=0)


def _dil_rows(n, d):
    per = N_BLOCKS // d
    r, lb = n // per, n % per

    def rows(b):
        start = b * (BLOCK * d) + r
        return pl.ds(pl.multiple_of(start, BLOCK), BLOCK) if d == 1 else pl.ds(start, BLOCK, stride=d)

    return rows(lb), rows(jnp.maximum(lb - 1, 0)), lb > 0


def _dil_rotate(q_ref, k_ref, c_ref, s1_ref, s2_ref, q_rot, k_rot):
    tabs = (c_ref[...], s1_ref[...], s2_ref[...])
    q_rot[...] = _rope_apply(q_ref[...], *tabs) * QK_SCALE
    k_rot[...] = _rope_apply(k_ref[...], *tabs)


def _dil_specs():
    def col(base):
        return pl.BlockSpec((SEQ, LANES), lambda p: (0, base // LANES + p))

    table = pl.BlockSpec((SEQ, LANES), lambda p: (0, 0))
    return [col(COL_QA), col(COL_KA), col(COL_VA)], [table] * 3


def _store_columns(blocks, dproj_ref, cols, sem):
    copies = [pltpu.make_async_copy(b, dproj_ref.at[:, pl.ds(pl.multiple_of(c * LANES, LANES), LANES)], sem.at[i])
              for i, (b, c) in enumerate(zip(blocks, cols))]
    for cp in copies:
        cp.start()
    for cp in copies:
        cp.wait()


def _dil_window(d, n, k_rot, v_ref):
    rows, prev, has_prev = _dil_rows(n, d)
    kw, vw = k_rot[rows, :].astype(BF16), v_ref[rows, :].astype(BF16)
    if d == N_BLOCKS:
        row = lax.broadcasted_iota(jnp.int32, (BLOCK, BLOCK), 0)
        valid = lax.broadcasted_iota(jnp.int32, (BLOCK, BLOCK), 1) <= row
    else:
        kw = jnp.concatenate([k_rot[prev, :].astype(BF16), kw], axis=0)
        vw = jnp.concatenate([v_ref[prev, :].astype(BF16), vw], axis=0)
        valid = _dil_valid(_dil_prev_limit(has_prev))
    return rows, prev, kw, vw, jnp.concatenate([valid, valid], axis=0)


def _dil_fwd(proj, tables):
    def body(q_ref, k_ref, v_ref, c_ref, s1_ref, s2_ref, o_ref, lse_ref, q_rot, k_rot):
        upper = _upper_half()
        _dil_rotate(q_ref, k_ref, c_ref, s1_ref, s2_ref, q_rot, k_rot)

        def blocks_of(d):
            def block(n, carry):
                rows, _, kw, vw, valid = _dil_window(d, n, k_rot, v_ref)
                s = jnp.where(valid, _dot_nt(_stack_heads(q_rot[rows, :].astype(BF16)), kw), NEG_INF)
                m = jnp.max(s, axis=-1, keepdims=True)
                p = jnp.exp(s - m)
                den = jnp.sum(p, axis=-1, keepdims=True)
                o_ref[rows, :] = _unstack_heads(_dot_nn((p * (1.0 / den)).astype(BF16), vw))
                lse = m + jnp.log(den)
                lse_ref[rows, :] = jnp.where(upper, lse[BLOCK:], lse[:BLOCK])
                return carry

            lax.fori_loop(0, N_BLOCKS, block, 0, unroll=4)

        for g, d in enumerate(DILATIONS):
            pl.when(pl.program_id(0) // 2 == g)(functools.partial(blocks_of, d))

    qkv, tabs = _dil_specs()
    out = pl.BlockSpec((SEQ, LANES), lambda p: (0, p))
    return pl.pallas_call(
        body, name="dil_attn_fwd", grid=(DIL_WIDTH // LANES,), in_specs=qkv + tabs, out_specs=[out, out],
        out_shape=[pltpu.HBM((SEQ, DIL_WIDTH), F32)] * 2,
        scratch_shapes=[pltpu.VMEM((SEQ, LANES), F32)] * 2,
        compiler_params=pltpu.CompilerParams(dimension_semantics=("parallel",)),
    )(*_in_hbm(proj, proj, proj, *tables))


def _dil_bwd(proj, tables, do, lse, c, dproj, deps=()):
    def body(q_ref, k_ref, v_ref, c_ref, s1_ref, s2_ref, do_ref, lse_ref, cc_ref, dproj_in, *rest):
        dproj_ref, dq_acc, dk_acc, dv_acc, dq_out, dk_out, dv_out, q_rot, k_rot, sem = rest[len(deps):]
        dk_acc[...] = jnp.zeros_like(dk_acc)
        dv_acc[...] = jnp.zeros_like(dv_acc)
        _dil_rotate(q_ref, k_ref, c_ref, s1_ref, s2_ref, q_rot, k_rot)

        def blocks_of(d):
            def block(n, carry):
                rows, prev, kw, vw, valid = _dil_window(d, n, k_rot, v_ref)
                q2 = _stack_heads(q_rot[rows, :].astype(BF16))
                do2 = _stack_heads(do_ref[rows, :].astype(BF16))
                lse_col, c_col = _head_columns(lse_ref[rows, :]), _head_columns(cc_ref[rows, :])
                p = jnp.where(valid, jnp.exp(_dot_nt(q2, kw) - lse_col), 0.0)
                ds = (p * (_dot_nt(do2, vw) + c_col)).astype(BF16)
                dk, dv = _dot_tn(ds, q2), _dot_tn(p.astype(BF16), do2)
                dq_acc[rows, :] = _unstack_heads(_dot_nn(ds, kw)) * QK_SCALE
                if d == N_BLOCKS:
                    dk_acc[rows, :] += dk
                    dv_acc[rows, :] += dv
                else:
                    dk_acc[prev, :] += dk[:BLOCK]
                    dv_acc[prev, :] += dv[:BLOCK]
                    dk_acc[rows, :] += dk[BLOCK:]
                    dv_acc[rows, :] += dv[BLOCK:]
                return carry

            lax.fori_loop(0, N_BLOCKS, block, 0, unroll=4)

        pair = pl.program_id(0)
        for g, d in enumerate(DILATIONS):
            pl.when(pair // 2 == g)(functools.partial(blocks_of, d))
        tabs = (c_ref[...], s1_ref[...], s2_ref[...])
        dq_out[...] = _rope_apply_t(dq_acc[...], *tabs).astype(BF16)
        dk_out[...] = _rope_apply_t(dk_acc[...], *tabs).astype(BF16)
        dv_out[...] = dv_acc[...].astype(BF16)
        _store_columns((dq_out, dk_out, dv_out), dproj_ref,
                       [base // LANES + pair for base in (COL_QA, COL_KA, COL_VA)], sem)

    qkv, tabs = _dil_specs()
    tok = pl.BlockSpec((SEQ, LANES), lambda p: (0, p))
    return pl.pallas_call(
        body, name="dil_attn_bwd", grid=(DIL_WIDTH // LANES,),
        in_specs=qkv + tabs + [tok, tok, tok, _ANY] + [_ANY] * len(deps), out_specs=_ANY,
        out_shape=pltpu.HBM(dproj.shape, dproj.dtype),
        scratch_shapes=[pltpu.VMEM((SEQ, LANES), F32)] * 3 + [pltpu.VMEM((SEQ, LANES), BF16)] * 3
        + [pltpu.VMEM((SEQ, LANES), F32)] * 2 + [pltpu.SemaphoreType.DMA((3,))],
        input_output_aliases={9: 0},
        compiler_params=pltpu.CompilerParams(dimension_semantics=("arbitrary",)),
    )(*_in_hbm(proj, proj, proj, *tables, do, lse, c, dproj), *deps)


def _group_weights(l0, l1, l2):
    m = jnp.maximum(jnp.maximum(l0, l1), l2)
    e0, e1, e2 = jnp.exp(l0 - m), jnp.exp(l1 - m), jnp.exp(l2 - m)
    tot = e0 + e1 + e2
    return e0 / tot, e1 / tot, e2 / tot


def _dil_combine(o, lse, deps=()):
    def fn(o0, o1, o2, l0, l1, l2):
        w0, w1, w2 = _group_weights(l0, l1, l2)
        return w0 * o0 + w1 * o1 + w2 * o2

    w = DIL_OUT_WIDTH
    return _rowwise(fn, "dil_combine", SEQ, 512, [(o, w, g) for g in range(3)] + [(lse, w, g) for g in range(3)], [],
                    [(w, F32)], deps=deps)[0]


def _dil_combine_bwd(d_out, o, lse, deps=()):
    w = DIL_OUT_WIDTH

    def fn(d, o0, o1, o2, l0, l1, l2):
        row = lax.broadcasted_iota(jnp.int32, (w, w), 0) // HEAD_DIM
        col = lax.broadcasted_iota(jnp.int32, (w, w), 1) // HEAD_DIM
        same_head = jnp.where(row == col, 1.0, 0.0).astype(BF16)
        ws = _group_weights(l0, l1, l2)
        dws = [_dot3_nn(d * og, same_head) for og in (o0, o1, o2)]
        mean = ws[0] * dws[0] + ws[1] * dws[1] + ws[2] * dws[2]
        return jnp.concatenate([wg * d for wg in ws], axis=1), jnp.concatenate([-wg * mean for wg in ws], axis=1)

    return _rowwise(fn, "dil_combine_bwd", SEQ, 256,
                    [(d_out, w, 0)] + [(o, w, g) for g in range(3)] + [(lse, w, g) for g in range(3)], [],
                    [(DIL_WIDTH, F32)] * 2, deps=deps)


def _log1p(e):
    u = 1.0 + e
    return jnp.where(u == 1.0, e, jnp.log(u) * (e / (u - 1.0)))


def _fox_gate(proj, b_pad, deps=()):
    def body(f_ref, b_ref, *rest):
        o_ref = rest[-1]
        z = f_ref[...] + b_ref[...]
        logf = (jnp.minimum(z, 0.0) - _log1p(jnp.exp(-jnp.abs(z)))).T[:F_ROWS]
        row = lax.broadcasted_iota(jnp.int32, (BLOCK, BLOCK), 0)
        col = lax.broadcasted_iota(jnp.int32, (BLOCK, BLOCK), 1)
        before = jnp.where(row <= col, 1.0, 0.0).astype(BF16)
        carry = jnp.zeros((F_ROWS, 1), F32)
        for blk in range(N_BLOCKS):
            run = _dot3_nn(logf[:, blk * BLOCK:(blk + 1) * BLOCK], before) + carry
            o_ref[:, blk * BLOCK:(blk + 1) * BLOCK] = run
            carry = run[:, BLOCK - 1:BLOCK]

    return pl.pallas_call(
        body, name="fox_gate", grid=(1,),
        in_specs=[pl.BlockSpec((SEQ, LANES), lambda i: (0, COL_F // LANES)), pl.BlockSpec((1, LANES), lambda i: (0, 0))]
        + [_ANY] * len(deps),
        out_specs=pl.BlockSpec((F_ROWS, SEQ), lambda i: (0, 0)),
        out_shape=pltpu.HBM((F_ROWS, SEQ), F32),
    )(*_in_hbm(proj, b_pad), *deps)


def _fox_gate_bwd(d_cum, proj, b_pad, dproj):
    def body(d_ref, f_ref, b_ref, dproj_ref, dz_ref, db_ref):
        row = lax.broadcasted_iota(jnp.int32, (BLOCK, BLOCK), 0)
        col = lax.broadcasted_iota(jnp.int32, (BLOCK, BLOCK), 1)
        after = jnp.where(row >= col, 1.0, 0.0).astype(BF16)
        carry = jnp.zeros((F_ROWS, 1), F32)
        parts = [None] * N_BLOCKS
        for blk in reversed(range(N_BLOCKS)):
            run = _dot3_nn(d_ref[:, blk * BLOCK:(blk + 1) * BLOCK], after) + carry
            parts[blk] = run
            carry = run[:, 0:1]
        dlogf = jnp.concatenate(parts, axis=1)
        dlogf = jnp.concatenate([dlogf, jnp.zeros((LANES - F_ROWS, SEQ), F32)], axis=0).T
        dz = dlogf * _sigmoid(-(f_ref[...] + b_ref[...]))
        dz_ref[...] = dz.astype(BF16)
        db_ref[...] = jnp.sum(dz, axis=0, keepdims=True)

    f_cols = pl.BlockSpec((SEQ, LANES), lambda i: (0, COL_F // LANES))
    return pl.pallas_call(
        body, name="fox_gate_bwd", grid=(1,),
        in_specs=[pl.BlockSpec((F_ROWS, SEQ), lambda i: (0, 0)), f_cols, pl.BlockSpec((1, LANES), lambda i: (0, 0)), _ANY],
        out_specs=[f_cols, pl.BlockSpec((1, LANES), lambda i: (0, 0))],
        out_shape=[pltpu.HBM(dproj.shape, dproj.dtype), pltpu.HBM((1, LANES), F32)],
        input_output_aliases={3: 0},
    )(*_in_hbm(d_cum, proj, b_pad, dproj))


FOX_TILE = 256
FOX_TILES = SEQ // FOX_TILE


def _row_to_col(row):
    n = row.shape[1]
    eye = lax.broadcasted_iota(jnp.int32, (n, n), 0) == lax.broadcasted_iota(jnp.int32, (n, n), 1)
    return jnp.sum(jnp.where(eye, row, 0.0), axis=1, keepdims=True)


def _fox_bias(f_row, i):
    t = FOX_TILE
    ext = (i + 1) * t
    bias = _row_to_col(f_row[:, i * t:(i + 1) * t]) - f_row[:, :ext]
    row = lax.broadcasted_iota(jnp.int32, (t, ext), 0) + i * t
    col = lax.broadcasted_iota(jnp.int32, (t, ext), 1)
    return bias, col <= row


def _fox_specs():
    qkv = [pl.BlockSpec((SEQ, LANES), lambda p, base=base: (0, base // LANES + p)) for base in (COL_QB, COL_KB, COL_VB)]
    return qkv, pl.BlockSpec((F_ROWS, SEQ), lambda p: (0, 0))


def _fox_fwd(proj, f_rows):
    t = FOX_TILE

    def body(q_ref, k_ref, v_ref, f_ref, o_ref, lse_ref):
        pair = pl.program_id(0)
        upper = _upper_half()
        k16, v16 = k_ref[...].astype(BF16), v_ref[...].astype(BF16)
        f_row = [f_ref[pl.ds(2 * pair + e, 1), :] for e in range(2)]
        for i in range(FOX_TILES):
            ext = (i + 1) * t
            q_tile = (q_ref[i * t:(i + 1) * t, :] * QK_SCALE).astype(BF16)
            s2 = _dot_nt(_stack_heads(q_tile), k16[:ext])
            pns, lses = [], []
            for e in range(2):
                bias, causal = _fox_bias(f_row[e], i)
                s = jnp.where(causal, s2[e * t:(e + 1) * t] + bias, NEG_INF)
                m = jnp.max(s, axis=-1, keepdims=True)
                p = jnp.exp(s - m)
                den = jnp.sum(p, axis=-1, keepdims=True)
                pns.append((p * (1.0 / den)).astype(BF16))
                lses.append(m + jnp.log(den))
            o_ref[i * t:(i + 1) * t, :] = _unstack_heads(_dot_nn(jnp.concatenate(pns, axis=0), v16[:ext]))
            lse_ref[i * t:(i + 1) * t, :] = jnp.where(upper, lses[1], lses[0])

    qkv, f_spec = _fox_specs()
    tok = pl.BlockSpec((SEQ, LANES), lambda p: (0, p))
    return pl.pallas_call(
        body, name="fox_attn_fwd", grid=(FOX_WIDTH // LANES,),
        in_specs=qkv + [f_spec], out_specs=[tok, tok],
        out_shape=[pltpu.HBM((SEQ, FOX_WIDTH), F32)] * 2,
        compiler_params=pltpu.CompilerParams(
            dimension_semantics=("parallel",), vmem_limit_bytes=_vmem_limit(8 * t * SEQ * 4)),
    )(*_in_hbm(proj, proj, proj, f_rows))


def _fox_bwd(proj, do, lse, f_rows, dproj):
    t = FOX_TILE

    def body(q_ref, k_ref, v_ref, f_ref, do_ref, lse_ref, dproj_in, dproj_ref, df_ref, dk_acc, dv_acc,
             dq_out, dk_out, dv_out, sem):
        pair = pl.program_id(0)
        upper = _upper_half()
        k16, v16 = k_ref[...].astype(BF16), v_ref[...].astype(BF16)
        f_row = [f_ref[pl.ds(2 * pair + e, 1), :] for e in range(2)]
        dk_acc[...] = jnp.zeros_like(dk_acc)
        dv_acc[...] = jnp.zeros_like(dv_acc)
        df_ref[...] = jnp.zeros_like(df_ref)
        for i in range(FOX_TILES):
            ext = (i + 1) * t
            q_tile = (q_ref[i * t:(i + 1) * t, :] * QK_SCALE).astype(BF16)
            do_tile = do_ref[i * t:(i + 1) * t, :]
            lse_t = lse_ref[i * t:(i + 1) * t, :]
            q2, do2 = _stack_heads(q_tile), _stack_heads(do_tile)
            s2, dp2 = _dot_nt(q2, k16[:ext]), _dot_nt(do2, v16[:ext])
            ps, dss = [], []
            for e in range(2):
                bias, causal = _fox_bias(f_row[e], i)
                s = s2[e * t:(e + 1) * t] + bias
                p = jnp.where(causal, jnp.exp(s - lse_t[:, e * HEAD_DIM:e * HEAD_DIM + 1]), 0.0)
                dp = dp2[e * t:(e + 1) * t]
                ds = p * (dp - jnp.sum(p * dp, axis=-1, keepdims=True))
                df_ref[0, e:e + 1, :ext] -= jnp.sum(ds, axis=0, keepdims=True)
                ps.append(p.astype(BF16))
                dss.append(ds.astype(BF16))
            ds2, p2 = jnp.concatenate(dss, axis=0), jnp.concatenate(ps, axis=0)
            dq_out[i * t:(i + 1) * t, :] = (_unstack_heads(_dot_nn(ds2, k16[:ext])) * QK_SCALE).astype(BF16)
            dk_acc[:ext, :] += _dot_tn(ds2, q2)
            dv_acc[:ext, :] += _dot_tn(p2, do2)
        dk_out[...] = dk_acc[...].astype(BF16)
        dv_out[...] = dv_acc[...].astype(BF16)
        _store_columns((dq_out, dk_out, dv_out), dproj_ref, [base // LANES + pair for base in (COL_QB, COL_KB, COL_VB)],
                       sem)

    qkv, f_spec = _fox_specs()
    tok = pl.BlockSpec((SEQ, LANES), lambda p: (0, p))
    return pl.pallas_call(
        body, name="fox_attn_bwd", grid=(FOX_WIDTH // LANES,),
        in_specs=qkv + [f_spec, tok, tok, _ANY],
        out_specs=[_ANY, pl.BlockSpec((1, SUBLANES, SEQ), lambda p: (p, 0, 0))],
        out_shape=[pltpu.HBM(dproj.shape, dproj.dtype),
                   pltpu.HBM((FOX_WIDTH // LANES, SUBLANES, SEQ), F32)],
        scratch_shapes=[pltpu.VMEM((SEQ, LANES), F32)] * 2 + [pltpu.VMEM((SEQ, LANES), BF16)] * 3
        + [pltpu.SemaphoreType.DMA((3,))],
        input_output_aliases={6: 0},
        compiler_params=pltpu.CompilerParams(
            dimension_semantics=("arbitrary",), vmem_limit_bytes=_vmem_limit(10 * t * SEQ * 4)),
    )(*_in_hbm(proj, proj, proj, f_rows, do, lse, dproj))


MIX_TILE = 256


def _mix_out(out_a, out_b, proj, x, wt_pa, wt_pb, w_out, g_post, g_ffn_pre):
    tm = MIX_TILE

    def body(a_ref, b_ref, ga_ref, gb_ref, x_ref, wpa_ref, wpb_ref, wo_ref, g2_ref, g3_ref,
             merged_ref, mix_ref, x1_ref, h2_ref):
        ya = _dot_nn(a_ref[...].astype(BF16), wpa_ref[...])
        yb = _dot_nn(b_ref[...].astype(BF16), wpb_ref[...])
        merged = (_sigmoid(ga_ref[...]) * ya + _sigmoid(gb_ref[...]) * yb).astype(BF16)
        merged_ref[...] = merged
        mix = _dot_nn(merged, wo_ref[...])
        mix_ref[...] = mix
        x1 = x_ref[...] + mix * _rms_scale(mix) * g2_ref[...]
        x1_ref[...] = x1
        h2_ref[...] = (x1 * _rms_scale(x1) * g3_ref[...]).astype(BF16)

    def rows(w, cb=0):
        return pl.BlockSpec((tm, w), lambda i, cb=cb: (i, cb))

    def whole(a):
        return pl.BlockSpec(a.shape, lambda i: (0, 0))

    d = D_MODEL
    blk = _nbytes((tm, d), F32) * 6 + sum(_nbytes(a.shape, BF16) for a in (wt_pa, wt_pb, w_out))
    return pl.pallas_call(
        body, name="mix_out", grid=(SEQ // tm,),
        in_specs=[rows(DIL_OUT_WIDTH), rows(FOX_WIDTH), rows(d, COL_GA // d), rows(d, COL_GB // d), rows(d),
                  whole(wt_pa), whole(wt_pb), whole(w_out), whole(g_post), whole(g_ffn_pre)],
        out_specs=[rows(d)] * 4,
        out_shape=[pltpu.HBM((SEQ, d), dt) for dt in (BF16, F32, F32, BF16)],
        compiler_params=pltpu.CompilerParams(dimension_semantics=("parallel",), vmem_limit_bytes=_vmem_limit(blk)),
    )(*_in_hbm(out_a, out_b, proj, proj, x, wt_pa, wt_pb, w_out, g_post, g_ffn_pre))


def _mix_out_bwd(dmix, out_a, out_b, proj, wt_pa, wt_pb, w_out, deps=()):
    tm = MIX_TILE

    def body(dm_ref, a_ref, b_ref, ga_ref, gb_ref, wpa_ref, wpb_ref, wo_ref, *rest):
        dproj_ref, dya_ref, dyb_ref, da_ref, db_ref = rest[len(deps):]
        dmerged = _dot_nt(dm_ref[...], wo_ref[...])
        ya = _dot_nn(a_ref[...].astype(BF16), wpa_ref[...])
        yb = _dot_nn(b_ref[...].astype(BF16), wpb_ref[...])
        sa, sb = _sigmoid(ga_ref[...]), _sigmoid(gb_ref[...])
        dproj_ref[:, COL_GA:COL_GA + D_MODEL] = (dmerged * ya * (sa * (1.0 - sa))).astype(BF16)
        dproj_ref[:, COL_GB:COL_GB + D_MODEL] = (dmerged * yb * (sb * (1.0 - sb))).astype(BF16)
        dproj_ref[:, COL_GB + D_MODEL:] = jnp.zeros((tm, COL_QA - COL_GB - D_MODEL), BF16)
        dya = (dmerged * sa).astype(BF16)
        dyb = (dmerged * sb).astype(BF16)
        dya_ref[...] = dya
        dyb_ref[...] = dyb
        da_ref[...] = _dot_nt(dya, wpa_ref[...])
        db_ref[...] = _dot_nt(dyb, wpb_ref[...]).astype(BF16)

    def rows(w, cb=0):
        return pl.BlockSpec((tm, w), lambda i, cb=cb: (i, cb))

    def whole(a):
        return pl.BlockSpec(a.shape, lambda i: (0, 0))

    d = D_MODEL
    blk = _nbytes((tm, d), F32) * 8 + sum(_nbytes(a.shape, BF16) for a in (wt_pa, wt_pb, w_out))
    return pl.pallas_call(
        body, name="mix_out_bwd", grid=(SEQ // tm,),
        in_specs=[rows(d), rows(DIL_OUT_WIDTH), rows(FOX_WIDTH), rows(d, COL_GA // d), rows(d, COL_GB // d),
                  whole(wt_pa), whole(wt_pb), whole(w_out)] + [_ANY] * len(deps),
        out_specs=[rows(COL_QA)] + [rows(d)] * 2 + [rows(DIL_OUT_WIDTH), rows(FOX_WIDTH)],
        out_shape=[pltpu.HBM((SEQ, PROJ_COLS), BF16)] + [pltpu.HBM((SEQ, d), BF16)] * 2
        + [pltpu.HBM((SEQ, DIL_OUT_WIDTH), F32), pltpu.HBM((SEQ, FOX_WIDTH), BF16)],
        compiler_params=pltpu.CompilerParams(dimension_semantics=("parallel",), vmem_limit_bytes=_vmem_limit(blk)),
    )(*_in_hbm(dmix, out_a, out_b, proj, proj, wt_pa, wt_pb, w_out), *deps)


FFN_TM, FFN_TN = 2048, 256


def _ffn_up(h2, wt_gate, wt_up):
    tm, tn = FFN_TM, FFN_TN

    def body(h_ref, wg_ref, wu_ref, gate_ref, up_ref, act_ref):
        for rows in (slice(0, tm // 2), slice(tm // 2, tm)):
            gate = _dot_nt(h_ref[rows, :], wg_ref[...])
            up = _dot_nt(h_ref[rows, :], wu_ref[...])
            gate_ref[rows, :] = gate
            up_ref[rows, :] = up
            act_ref[rows, :] = (gate * _sigmoid(gate) * up).astype(BF16)

    tile = pl.BlockSpec((tm, tn), lambda i, j: (i, j))
    w_spec = pl.BlockSpec((tn, D_MODEL), lambda i, j: (j, 0))
    return pl.pallas_call(
        body, name="ffn_up", grid=(SEQ // tm, D_FF // tn),
        in_specs=[pl.BlockSpec((tm, D_MODEL), lambda i, j: (i, 0)), w_spec, w_spec],
        out_specs=[tile, tile, tile],
        out_shape=[pltpu.HBM((SEQ, D_FF), dt) for dt in (F32, F32, BF16)],
        compiler_params=pltpu.CompilerParams(
            dimension_semantics=("parallel", "parallel"), vmem_limit_bytes=_vmem_limit(8 * 2**20)),
    )(h2, wt_gate, wt_up)


def _ffn_act_bwd(dff, w_down, gate, up):
    tm, tn = FFN_TM, FFN_TN

    def body(d_ref, wd_ref, gate_ref, up_ref, dgate_ref, dup_ref):
        for rows in (slice(0, tm // 2), slice(tm // 2, tm)):
            dact = _dot_nt(d_ref[rows, :], wd_ref[...])
            gate = gate_ref[rows, :]
            sg = _sigmoid(gate)
            dgate_ref[rows, :] = (dact * up_ref[rows, :] * (sg * (1.0 + gate * (1.0 - sg)))).astype(BF16)
            dup_ref[rows, :] = (dact * (gate * sg)).astype(BF16)

    tile = pl.BlockSpec((tm, tn), lambda i, j: (i, j))
    return pl.pallas_call(
        body, name="ffn_act_bwd", grid=(SEQ // tm, D_FF // tn),
        in_specs=[pl.BlockSpec((tm, D_MODEL), lambda i, j: (i, 0)), pl.BlockSpec((tn, D_MODEL), lambda i, j: (j, 0)),
                  tile, tile],
        out_specs=[tile, tile],
        out_shape=[pltpu.HBM((SEQ, D_FF), BF16)] * 2,
        compiler_params=pltpu.CompilerParams(
            dimension_semantics=("parallel", "parallel"), vmem_limit_bytes=_vmem_limit(8 * 2**20)),
    )(dff, w_down, gate, up)


EPILOGUE_TM = 512


def _loss_head(act, w_down, x1, target, g_post):
    def fn(ff, x1, tgt, g):
        r = _rms_scale(ff)
        nrm = ff * r
        err = (x1 + nrm * g) - tgt
        loss = 0.5 * jnp.sum(jnp.mean(err * err, axis=-1, keepdims=True), axis=0, keepdims=True)
        dy = err * (1.0 / D_MODEL)
        u = dy * g
        dff = r * u - ff * (r * r * r) * jnp.mean(u * ff, axis=-1, keepdims=True)
        return dy, dff, jnp.broadcast_to(loss, (1, LANES)), jnp.sum(dy * nrm, axis=0, keepdims=True)

    d = D_MODEL
    return _matmul_rowwise([(act, w_down)], fn, "ffn_down_loss", EPILOGUE_TM, [(x1, d, 0), (target, d, 0)], [g_post],
                           [(d, F32), (d, BF16)], [LANES, d])


def _post_ffn_bwd(dgate, wt_gate, dup, wt_up, x1, dy, mix, g_ffn_pre, g_mix_post, deps=()):
    def fn(dh2, x1, dy, mix, g3, g2):
        dx, dg3 = _rms_bwd(x1, dh2, g3)
        dx1 = dy + dx
        dmix, dg2 = _rms_bwd(mix, dx1, g2)
        return dx1, dmix, dg3, dg2

    d = D_MODEL
    return _matmul_rowwise([(dgate, wt_gate), (dup, wt_up)], fn, "ffn_up_bwd", EPILOGUE_TM,
                           [(x1, d, 0), (dy, d, 0), (mix, d, 0)], [g_ffn_pre, g_mix_post],
                           [(d, F32), (d, BF16)], [d, d], deps=deps)


def _input_bwd(dproj, wt_r, x, dx1, g_pre, deps=()):
    def fn(dh, x, dx1, g):
        dx, dg = _rms_bwd(x, dh, g)
        return dx1 + dx, dg

    d = D_MODEL
    return _matmul_rowwise([(dproj, wt_r)], fn, "in_proj_bwd", EPILOGUE_TM, [(x, d, 0), (dx1, d, 0)], [g_pre],
                           [(d, F32)], [d], deps=deps)


def _adam_math(w, g, m, v):
    m = ADAM_B1 * m + (1.0 - ADAM_B1) * g
    v = ADAM_B2 * v + (1.0 - ADAM_B2) * (g * g)
    m_hat = m / (1.0 - ADAM_B1 ** ADAM_STEP)
    v_hat = v / (1.0 - ADAM_B2 ** ADAM_STEP)
    delta = -ADAM_LR * (m_hat / (jnp.sqrt(v_hat) + ADAM_EPS) + ADAM_WD * w)
    return delta, m, v


def _adam(w, mine, recv, m, v, name):
    r, c = w.shape
    tc = _col_tile(r, c)

    def body(w_ref, p_ref, r_ref, m_ref, v_ref, g_ref, d_ref, nm_ref, nv_ref):
        g = ((p_ref[...] + r_ref[0].astype(F32)) + r_ref[1].astype(F32)) + r_ref[2].astype(F32)
        g_ref[...] = g
        d_ref[...], nm_ref[...], nv_ref[...] = _adam_math(w_ref[...], g, m_ref[...], v_ref[...])

    spec = pl.BlockSpec((r, tc), lambda j: (0, j))
    return pl.pallas_call(
        body, name=name, grid=(c // tc,),
        in_specs=[spec, spec, pl.BlockSpec((3, r, tc), lambda j: (0, 0, j)), spec, spec], out_specs=[spec] * 4,
        out_shape=[pltpu.HBM((r, c), F32)] * 4,
        compiler_params=pltpu.CompilerParams(dimension_semantics=("parallel",)),
    )(*_in_hbm(w, mine, recv, m, v))


def _adam_small(gathered, ws, ms, vs, loss_parts):
    n = len(ws)

    def body(*refs):
        outs = refs[4 * n + 1:]
        loss = refs[4 * n][0]
        for dev in range(1, N_DEV):
            loss = loss + refs[4 * n][dev]
        outs[4 * n][...] = loss
        for i in range(n):
            ga_ref, w_ref, m_ref, v_ref = (refs[j * n + i] for j in range(4))
            g = ga_ref[0]
            for dev in range(1, N_DEV):
                g = g + ga_ref[dev]
            g = g[:, :w_ref.shape[1]]
            outs[4 * i][...] = g
            outs[4 * i + 1][...], outs[4 * i + 2][...], outs[4 * i + 3][...] = _adam_math(
                w_ref[...], g, m_ref[...], v_ref[...])

    out_shape = [pltpu.HBM(w.shape, F32) for w in ws for _ in range(4)]
    out_shape.append(pltpu.HBM((1, LANES), F32))
    out = pl.pallas_call(body, name="adam_small", out_shape=out_shape)(*gathered, *ws, *ms, *vs, loss_parts)
    return [out[4 * i:4 * i + 4] for i in range(n)], out[4 * n]


_PROJ_SEGMENTS = ((3848, 5896), (None, COL_QA - 2 * D_MODEL), (0, 3840), (3840, 3848), (None, PROJ_COLS - COL_F - 8))


def _proj_weight_t(gathered):
    pieces, zeros, at = [], [], 0
    for lo, hi in _PROJ_SEGMENTS:
        if lo is None:
            zeros.append((at, hi))
            at += hi
            continue
        for dev in range(lo // IN_SHARD, (hi - 1) // IN_SHARD + 1):
            a, b = max(lo, dev * IN_SHARD), min(hi, (dev + 1) * IN_SHARD)
            pieces.append((dev, a - dev * IN_SHARD, at + a - lo, b - a))
        at += hi - lo
    assert at == PROJ_COLS
    tc = 2 * LANES

    def body(g_ref, o_ref, shards, rows):
        for dev in range(N_DEV):
            shards[dev] = g_ref[dev].astype(F32)
        for dev, src, dst, n in pieces:
            rows[pl.ds(dst, n), :] = shards[dev, pl.ds(src, n), :]
        for dst, n in zeros:
            rows[pl.ds(dst, n), :] = jnp.zeros((n, tc), F32)
        o_ref[...] = rows[...].astype(o_ref.dtype)

    return pl.pallas_call(
        body, name="w_in_rows", grid=(D_MODEL // tc,),
        in_specs=[pl.BlockSpec((N_DEV, IN_SHARD, tc), lambda j: (0, 0, j))],
        out_specs=pl.BlockSpec((PROJ_COLS, tc), lambda j: (0, j)),
        out_shape=pltpu.HBM((PROJ_COLS, D_MODEL), gathered.dtype),
        scratch_shapes=[pltpu.VMEM((N_DEV, IN_SHARD, tc), F32), pltpu.VMEM((PROJ_COLS, tc), F32)],
        compiler_params=pltpu.CompilerParams(
            dimension_semantics=("parallel",), vmem_limit_bytes=_vmem_limit(2 * _nbytes((PROJ_COLS, tc), F32))),
    )(*_in_hbm(gathered))


def _proj_weight_grad_slots(dwt_r):
    starts, at = [], 0
    for lo, hi in _PROJ_SEGMENTS:
        if lo is not None:
            starts.append((lo, hi, at))
        at += hi if lo is None else hi - lo
    pieces = []
    for dev in range(N_DEV):
        lo, end = dev * IN_SHARD, (dev + 1) * IN_SHARD
        for seg_lo, seg_hi, seg_at in sorted(starts):
            a, b = max(lo, seg_lo), min(end, seg_hi)
            if a < b:
                pieces.append((dev, a - lo, seg_at + a - seg_lo, b - a))

    def body(g_ref, o_ref):
        for dev, dst, src, rows in pieces:
            o_ref[dev, pl.ds(dst, rows), :] = g_ref[pl.ds(src, rows), :]

    tc = 2 * LANES
    return pl.pallas_call(
        body, name="grad_w_in_slots", grid=(D_MODEL // tc,),
        in_specs=[pl.BlockSpec((PROJ_COLS, tc), lambda j: (0, j))],
        out_specs=pl.BlockSpec((N_DEV, IN_SHARD, tc), lambda j: (0, 0, j)),
        out_shape=pltpu.HBM((N_DEV, IN_SHARD, D_MODEL), F32),
        compiler_params=pltpu.CompilerParams(
            dimension_semantics=("parallel",), vmem_limit_bytes=_vmem_limit(2 * _nbytes((PROJ_COLS, tc), F32))),
    )(*_in_hbm(dwt_r))


def kernel(x, w_in, w_proj_a, w_proj_b, w_out, b_forget, w_ffn_gate, w_ffn_up, w_ffn_down, norm_mix_pre, norm_mix_post, norm_ffn_pre, norm_ffn_post, loss_target, m_w_in, m_w_proj_a, m_w_proj_b, m_w_out, m_b_forget, m_w_ffn_gate, m_w_ffn_up, m_w_ffn_down, m_norm_mix_pre, m_norm_mix_post, m_norm_ffn_pre, m_norm_ffn_post, v_w_in, v_w_proj_a, v_w_proj_b, v_w_out, v_b_forget, v_w_ffn_gate, v_w_ffn_up, v_w_ffn_down, v_norm_mix_pre, v_norm_mix_post, v_norm_ffn_pre, v_norm_ffn_post):
    d = D_MODEL
    names = ("w_in", "w_proj_a", "w_proj_b", "w_out", "w_ffn_gate", "w_ffn_up", "w_ffn_down")
    col_sharded = ("w_in", "w_ffn_gate", "w_ffn_up")

    def row_shards(arrs):
        return {k: (a[0].T if k in col_sharded else a[0]) for k, a in zip(names, arrs)}

    shards = row_shards((w_in, w_proj_a, w_proj_b, w_out, w_ffn_gate, w_ffn_up, w_ffn_down))
    moments_m = row_shards((m_w_in, m_w_proj_a, m_w_proj_b, m_w_out, m_w_ffn_gate, m_w_ffn_up, m_w_ffn_down))
    moments_v = row_shards((v_w_in, v_w_proj_a, v_w_proj_b, v_w_out, v_w_ffn_gate, v_w_ffn_up, v_w_ffn_down))
    pos = jnp.stack([lax.axis_index("c"), 2 * lax.axis_index("x") + lax.axis_index("y")]).astype(jnp.int32)
    x2, target = x[0], loss_target[0]

    me = 4 * lax.axis_index("x") + 2 * lax.axis_index("y") + lax.axis_index("c")
    mid_names, ffn_names = names[1:4], names[4:]
    first_names, later_names = names[:1], names[1:]
    shards16 = {k: shards[k].astype(BF16) for k in names}

    def landing(k):
        return lax.dynamic_update_slice(lax.empty((N_DEV,) + shards[k].shape, BF16), shards16[k][None], (me, 0, 0))

    ag_first = _exchange_start("ag_first_chips_start", _gather_two_route_copies, [shards16[k] for k in first_names],
                               [landing(k) for k in first_names], 4 * len(first_names))
    h = _rowwise(lambda xb, g: xb * _rms_scale(xb) * g, "norm_mix_pre", SEQ, 256, [(x2, d, 0)], [norm_mix_pre],
                 [(d, BF16)], deps=[ag_first.token])[0]
    later_lands = [landing(k) for k in later_names]
    relayed, ag_first_last = _gather_relay(ag_first, "ag_first", [h, shards["w_in"], moments_m["w_in"], moments_v["w_in"],
                                                                 *later_lands, *[shards16[k] for k in later_names]], [])
    ag_later = _exchange_start("ag_later_chips_start", _gather_two_route_copies, [shards16[k] for k in later_names],
                               later_lands, 4 * len(later_names), after=[relayed])
    gathered = dict(zip(first_names, ag_first_last([ag_later.token])))
    wt_r = _proj_weight_t(gathered["w_in"])

    proj = _matmul([(h, *_in_hbm(wt_r))], "nt", F32, "in_proj", 1024, 896, 1024)
    tables = _rope_tables()
    o_dil, lse_dil = _dil_fwd(proj, tables)
    out_a = _dil_combine(o_dil, lse_dil)
    relayed, ag_later_last = _gather_relay(ag_later, "ag_later", [o_dil], [out_a])

    b_pad = jnp.pad(b_forget, ((0, 0), (0, LANES - N_FOX_HEADS)))
    f_rows = _fox_gate(proj, b_pad, deps=[relayed])
    out_b, lse_fox = _fox_fwd(proj, f_rows)

    gathered = dict(zip(later_names, ag_later_last([out_b])))
    wt_pa = gathered["w_proj_a"].transpose(1, 0, 2).reshape(DIL_OUT_WIDTH, d)
    wt_pb = gathered["w_proj_b"].transpose(1, 0, 2).reshape(FOX_WIDTH, d)
    w_o = gathered["w_out"].reshape(d, d)
    wt_g = gathered["w_ffn_gate"].reshape(D_FF, d)
    wt_u = gathered["w_ffn_up"].reshape(D_FF, d)
    w_d = gathered["w_ffn_down"].reshape(D_FF, d)
    merged, mix, x1, h2 = _mix_out(out_a, out_b, proj, x2, wt_pa, wt_pb, w_o, norm_mix_post, norm_ffn_pre)

    gate, up, act = _ffn_up(h2, wt_g, wt_u)
    dy, dff, loss_part, dg_ffn_post = _loss_head(act, w_d, x1, target, norm_ffn_post)

    dgate, dup = _ffn_act_bwd(dff, w_d, gate, up)
    grads_t = {}
    grads_t["w_ffn_down"] = _matmul([(act, dff)], "tn", F32, "grad_w_ffn_down", 1408, 512, 2048, staged=False)
    grads_t["w_ffn_gate"] = _matmul([(dgate, h2)], "tn", F32, "grad_w_ffn_gate", 1408, 512, 2048)
    grads_t["w_ffn_up"] = _matmul([(dup, h2)], "tn", F32, "grad_w_ffn_up", 1408, 512, 2048)
    rs_ffn = _ReduceScatter("ffn", {k: grads_t[k] for k in ffn_names}, pos)
    dx1, dmix, dg_ffn_pre, dg_mix_post = _post_ffn_bwd(dgate, wt_g, dup, wt_u, x1, dy, mix, norm_ffn_pre, norm_mix_post,
                                                       deps=[rs_ffn.token])
    rs_ffn.start_chips([dmix])

    dproj, dya, dyb, d_out_a, d_out_b = _mix_out_bwd(dmix, out_a, out_b, proj, wt_pa, wt_pb, w_o, deps=[rs_ffn.token])
    grads_t["w_out"] = _matmul([(merged, dmix)], "tn", F32, "grad_w_out", 1024, 1024, 1024)
    def column_slots(g):
        return g.reshape(g.shape[0], N_DEV, LANES).transpose(1, 0, 2)

    grads_t["w_proj_a"] = column_slots(_matmul([(out_a, dya)], "tn", F32, "grad_w_proj_a", DIL_OUT_WIDTH, 1024, SEQ))
    grads_t["w_proj_b"] = column_slots(_matmul([(out_b, dyb)], "tn", F32, "grad_w_proj_b", FOX_WIDTH, 1024, SEQ))
    rs_mid = _ReduceScatter("mid", {k: grads_t[k] for k in mid_names}, pos)

    do_dil, c_dil = _dil_combine_bwd(d_out_a, o_dil, lse_dil, deps=[rs_mid.token])
    rs_mid.start_chips([c_dil])
    dproj, d_cum = _fox_bwd(proj, d_out_b, lse_fox, f_rows, dproj)
    d_cum_rows = jnp.pad(d_cum[:, :2].reshape(N_FOX_HEADS, SEQ), ((0, F_ROWS - N_FOX_HEADS), (0, 0)))
    dproj, db_part = _fox_gate_bwd(d_cum_rows, proj, b_pad, dproj)
    dproj = _dil_bwd(proj, tables, do_dil, lse_dil, c_dil, dproj, deps=[rs_mid.token])

    dwt_r = _matmul([(dproj, h)], "tn", F32, "grad_w_in", 896, 1024, 2048)
    rs_in = _ReduceScatter("in", {"w_in": _proj_weight_grad_slots(dwt_r)}, pos)
    def finish(rs, after):
        return {k: _adam(shards[k], mine, recv, moments_m[k], moments_v[k], "adam_" + k)
                for k, (mine, recv) in rs.finish(after).items()}

    done = finish(rs_ffn, [rs_in.token])
    rs_in.start_chips([done[k][0] for k in ffn_names])
    grad_x, dg_mix_pre = _input_bwd(dproj, wt_r, x2, dx1, norm_mix_pre, deps=[rs_in.token])
    done.update(finish(rs_mid, [grad_x]))

    small_all = _all_gather([dg_mix_pre, dg_mix_post, dg_ffn_pre, dg_ffn_post, db_part, loss_part],
                            "small_grads_all_gather", deps=[done[k][0] for k in mid_names])
    small, loss = _adam_small(small_all[:5], [norm_mix_pre, norm_mix_post, norm_ffn_pre, norm_ffn_post, b_forget],
                              [m_norm_mix_pre, m_norm_mix_post, m_norm_ffn_pre, m_norm_ffn_post, m_b_forget],
                              [v_norm_mix_pre, v_norm_mix_post, v_norm_ffn_pre, v_norm_ffn_post, v_b_forget],
                              small_all[5])

    done.update(finish(rs_in, [small[0][0]]))

    def leaves(i):
        def nat(k):
            a = done[k][i]
            return (a.T if k in col_sharded else a)[None]

        return [nat("w_in"), nat("w_proj_a"), nat("w_proj_b"), nat("w_out"), small[4][i],
                nat("w_ffn_gate"), nat("w_ffn_up"), nat("w_ffn_down"), *[small[r][i] for r in range(4)]]

    return (loss[0, 0], grad_x[None], *leaves(0), *leaves(1), *leaves(2), *leaves(3))
```

```python
import functools
import math

import jax
import jax.numpy as jnp
import numpy as np
from jax import lax
from jax.experimental import pallas as pl
from jax.experimental.pallas import tpu as pltpu

F32 = jnp.float32
BF16 = jnp.bfloat16
MESH = pl.DeviceIdType.MESH

D_MODEL = 1024
SEQ = 2048
HEAD_DIM = 64
BLOCK = 128
N_BLOCKS = SEQ // BLOCK
DILATIONS = (1, 4, 16)
N_FOX_HEADS = 8
DIL_WIDTH = 768
DIL_OUT_WIDTH = 256
FOX_WIDTH = 512
D_FF = 2816
ROPE_THETA = 500000.0
ROPE_DIM = HEAD_DIM // 4
ROPE_HALF = ROPE_DIM // 2
EPS = 1e-6
NEG_INF = -1e30
QK_SCALE = 1.0 / math.sqrt(HEAD_DIM)
IN_COLS = 5896
N_DEV = 8
IN_SHARD = IN_COLS // N_DEV

ADAM_LR = 0.001
ADAM_B1 = 0.9
ADAM_B2 = 0.999
ADAM_EPS = 1e-08
ADAM_WD = 0.01
ADAM_STEP = 10

V7X_VMEM_BYTES = 64 * 2**20
LANES = 128
SUBLANES = 8

PROJ_COLS = 6272
COL_GA, COL_GB = 0, 1024
COL_QA, COL_KA, COL_VA = 2304, 3072, 3840
COL_QB, COL_KB, COL_VB = 4608, 5120, 5632
COL_F = 6144
F_ROWS = 16


def _vmem_limit(block_bytes):
    want = 2 * block_bytes + 16 * 2**20
    return int(min(max(want, 32 * 2**20), V7X_VMEM_BYTES - 8 * 2**20))


def _nbytes(shape, dtype):
    return math.prod(shape) * jnp.dtype(dtype).itemsize


def _in_hbm(*arrays):
    return [pltpu.with_memory_space_constraint(a, pltpu.HBM) for a in arrays]


def _dot(a, b, dims):
    return lax.dot_general(a, b, (dims, ((), ())), preferred_element_type=F32)


def _dot_nn(a, b):
    return _dot(a, b, ((1,), (0,)))


def _dot_nt(a, b):
    return _dot(a, b, ((1,), (1,)))


def _dot_tn(a, b):
    return _dot(a, b, ((0,), (0,)))


def _sigmoid(z):
    return 1.0 / (1.0 + jnp.exp(-z))


def _split3(x):
    hi = x.astype(BF16)
    r1 = x - hi.astype(F32)
    mid = r1.astype(BF16)
    lo = (r1 - mid.astype(F32)).astype(BF16)
    return hi, mid, lo


def _dot3_nn(x, ones_matrix):
    hi, mid, lo = _split3(x)
    return (_dot_nn(hi, ones_matrix) + _dot_nn(mid, ones_matrix)) + _dot_nn(lo, ones_matrix)


def _rowwise(fn, name, n_rows, tm, row_ins, bcast_ins, row_outs, acc_outs=(), deps=()):
    n_in = len(row_ins) + len(bcast_ins)
    n_ro = len(row_outs)

    def body(*refs):
        res = fn(*[r[...] for r in refs[:n_in]])
        if not isinstance(res, (tuple, list)):
            res = (res,)
        outs = refs[n_in + len(deps):]
        for r, o in zip(res[:n_ro], outs[:n_ro]):
            o[...] = r.astype(o.dtype)
        first = pl.program_id(0) == 0
        for r, o in zip(res[n_ro:], outs[n_ro:]):
            _accumulate(o, r, first)

    in_specs = [pl.BlockSpec((tm, w), lambda i, cb=cb: (i, cb)) for _, w, cb in row_ins]
    in_specs += [pl.BlockSpec(a.shape, lambda i: (0, 0)) for a in bcast_ins]
    in_specs += [pl.BlockSpec(memory_space=pl.ANY)] * len(deps)
    out_specs = [pl.BlockSpec((tm, w), lambda i: (i, 0)) for w, _ in row_outs]
    out_specs += [pl.BlockSpec((1, w), lambda i: (0, 0)) for w in acc_outs]
    out_shape = [pltpu.HBM((n_rows, w), dt) for w, dt in row_outs]
    out_shape += [pltpu.HBM((1, w), F32) for w in acc_outs]
    blk = sum(_nbytes((tm, w), a.dtype) for a, w, _ in row_ins) + sum(_nbytes((tm, w), dt) for w, dt in row_outs)
    return pl.pallas_call(
        body, name=name, grid=(n_rows // tm,), in_specs=in_specs, out_specs=out_specs, out_shape=out_shape,
        compiler_params=pltpu.CompilerParams(
            dimension_semantics=("arbitrary" if acc_outs else "parallel",), vmem_limit_bytes=_vmem_limit(3 * blk)),
    )(*_in_hbm(*[a for a, _, _ in row_ins], *bcast_ins), *deps)


def _accumulate(o_ref, part, first):
    @pl.when(first)
    def _():
        o_ref[...] = part

    @pl.when(jnp.logical_not(first))
    def _():
        o_ref[...] += part


_MM_DIMS = {"nn": ((1,), (0,)), "nt": ((1,), (1,)), "tn": ((0,), (0,))}


def _matmul(pairs, mode, out_dtype, name, tm, tn, tk, deps=(), staged=True):
    a0, b0 = pairs[0]
    if mode == "tn":
        kk, m = a0.shape
    else:
        m, kk = a0.shape
    n = b0.shape[0] if mode == "nt" else b0.shape[1]
    assert m % tm == 0 and n % tn == 0 and kk % tk == 0, (name, m, n, kk)
    nk = kk // tk
    n_pairs = len(pairs)
    dims = _MM_DIMS[mode]
    n_in = 2 * n_pairs + len(deps)

    def body(*refs):
        o_ref = refs[n_in]
        part = None
        for p in range(n_pairs):
            d = _dot(refs[2 * p][...].astype(BF16), refs[2 * p + 1][...].astype(BF16), dims)
            part = d if part is None else part + d
        if nk == 1:
            o_ref[...] = part.astype(o_ref.dtype)
            return
        acc = refs[n_in + 1]
        k = pl.program_id(2)

        @pl.when(k == 0)
        def _():
            acc[...] = part

        @pl.when(k > 0)
        def _():
            acc[...] += part

        @pl.when(k == nk - 1)
        def _():
            o_ref[...] = acc[...].astype(o_ref.dtype)

    if mode == "tn":
        a_spec = pl.BlockSpec((tk, tm), lambda i, j, k: (k, i))
    else:
        a_spec = pl.BlockSpec((tm, tk), lambda i, j, k: (i, k))
    if mode == "nt":
        b_spec = pl.BlockSpec((tn, tk), lambda i, j, k: (j, k))
    else:
        b_spec = pl.BlockSpec((tk, tn), lambda i, j, k: (k, j))
    blk = sum(_nbytes((tm, tk), a.dtype) + _nbytes((tk, tn), b.dtype) for a, b in pairs) + 2 * _nbytes((tm, tn), F32)
    flat = [a for pair in pairs for a in pair]
    return pl.pallas_call(
        body, name=name, grid=(m // tm, n // tn, nk),
        in_specs=[a_spec, b_spec] * n_pairs + [pl.BlockSpec(memory_space=pl.ANY)] * len(deps),
        out_specs=pl.BlockSpec((tm, tn), lambda i, j, k: (i, j)),
        out_shape=pltpu.HBM((m, n), out_dtype),
        scratch_shapes=[] if nk == 1 else [pltpu.VMEM((tm, tn), F32)],
        compiler_params=pltpu.CompilerParams(
            dimension_semantics=("parallel", "parallel", "arbitrary"), vmem_limit_bytes=_vmem_limit(blk)),
    )(*(flat if staged else _in_hbm(*flat)), *deps)


def _matmul_rowwise(pairs, fn, name, tm, row_ins, bcast_ins, row_outs, acc_outs=(), deps=()):
    m = pairs[0][0].shape[0]
    n_mm, n_in = 2 * len(pairs), len(row_ins) + len(bcast_ins)
    n_ro = len(row_outs)

    def body(*refs):
        prod = None
        for p in range(len(pairs)):
            part = _dot_nn(refs[2 * p][...].astype(BF16), refs[2 * p + 1][...].astype(BF16))
            prod = part if prod is None else prod + part
        res = fn(prod, *[r[...] for r in refs[n_mm:n_mm + n_in]])
        outs = refs[n_mm + n_in + len(deps):]
        for r, o in zip(res[:n_ro], outs[:n_ro]):
            o[...] = r.astype(o.dtype)
        first = pl.program_id(0) == 0
        for r, o in zip(res[n_ro:], outs[n_ro:]):
            _accumulate(o, r, first)

    in_specs = []
    for a, b in pairs:
        in_specs += [pl.BlockSpec((tm, a.shape[1]), lambda i: (i, 0)),
                     pl.BlockSpec(b.shape, lambda i: (0, 0), pipeline_mode=pl.Buffered(1))]
    in_specs += [pl.BlockSpec((tm, w), lambda i, cb=cb: (i, cb)) for _, w, cb in row_ins]
    in_specs += [pl.BlockSpec(a.shape, lambda i: (0, 0)) for a in bcast_ins]
    in_specs += [_ANY] * len(deps)
    out_specs = [pl.BlockSpec((tm, w), lambda i: (i, 0)) for w, _ in row_outs]
    out_specs += [pl.BlockSpec((1, w), lambda i: (0, 0)) for w in acc_outs]
    out_shape = [pltpu.HBM((m, w), dt) for w, dt in row_outs]
    out_shape += [pltpu.HBM((1, w), F32) for w in acc_outs]
    blk = sum(_nbytes((tm, a.shape[1]), a.dtype) + _nbytes(b.shape, b.dtype) // 2 for a, b in pairs)
    blk += sum(_nbytes((tm, w), a.dtype) for a, w, _ in row_ins) + sum(_nbytes((tm, w), dt) for w, dt in row_outs)
    return pl.pallas_call(
        body, name=name, grid=(m // tm,), in_specs=in_specs, out_specs=out_specs, out_shape=out_shape,
        compiler_params=pltpu.CompilerParams(dimension_semantics=("arbitrary",), vmem_limit_bytes=_vmem_limit(blk)),
    )(*[a for pair in pairs for a in pair], *[a for a, _, _ in row_ins], *bcast_ins, *deps)


def _rms_scale(x):
    return lax.rsqrt(jnp.mean(x * x, axis=-1, keepdims=True) + EPS)


def _rms_bwd(xin, dyn, g):
    r = _rms_scale(xin)
    u = dyn * g
    dx = r * u - xin * (r * r * r) * jnp.mean(u * xin, axis=-1, keepdims=True)
    dg = jnp.sum(dyn * xin * r, axis=0, keepdims=True)
    return dx, dg


def _mesh_pos():
    return lax.axis_index("x"), lax.axis_index("y"), lax.axis_index("c")


def _all_gather(xs, name, deps=()):
    n = len(xs)

    def body(*refs):
        x_refs, out_refs = refs[:n], refs[n + len(deps):2 * n + len(deps)]
        send_sems, recv_sems, local_sems = refs[2 * n + len(deps):]
        mx, my, mc = _mesh_pos()
        me, sib = (mx, my, mc), (mx, my, 1 - mc)
        chips = [(1 - mx, my), (mx, 1 - my), (1 - mx, 1 - my)]

        def slot(a, dev):
            px, py, pc = dev
            return out_refs[a].at[4 * px + 2 * py + pc]

        def copy(k, a, block, to, src=None):
            return pltpu.make_async_remote_copy(
                src_ref=slot(a, block) if src is None else src, dst_ref=slot(a, block),
                send_sem=send_sems.at[a * 7 + k], recv_sem=recv_sems.at[a * 7 + k],
                device_id=to, device_id_type=MESH)

        mine = [pltpu.make_async_copy(x_refs[a], slot(a, me), local_sems.at[a]) for a in range(n)]
        for cp in mine:
            cp.start()
        first = []
        for a in range(n):
            first.append(copy(0, a, me, sib, x_refs[a]))
            first += [copy(1 + j, a, me, (*chip, mc), x_refs[a]) for j, chip in enumerate(chips)]
        for cp in first:
            cp.start()
        passed = []
        for a in range(n):
            for j, chip in enumerate(chips):
                copy(1 + j, a, (*chip, mc), me).wait_recv()
                fwd = copy(4 + j, a, (*chip, mc), sib)
                fwd.start()
                passed.append(fwd)
        for a in range(n):
            copy(0, a, sib, me).wait_recv()
            for j, chip in enumerate(chips):
                copy(4 + j, a, (*chip, 1 - mc), me).wait_recv()
        for cp in first + passed:
            cp.wait_send()
        for cp in mine:
            cp.wait()

    hbm = pl.BlockSpec(memory_space=pl.ANY)
    return pl.pallas_call(
        body, name=name,
        out_shape=[pltpu.HBM((N_DEV,) + x.shape, x.dtype) for x in xs],
        in_specs=[hbm] * (n + len(deps)), out_specs=[hbm] * n,
        scratch_shapes=[pltpu.SemaphoreType.DMA((7 * n,)), pltpu.SemaphoreType.DMA((7 * n,)),
                        pltpu.SemaphoreType.DMA((n,))],
    )(*xs, *deps)


_HBM = pl.BlockSpec(memory_space=pltpu.HBM)
_SEM = pl.BlockSpec(memory_space=pltpu.SEMAPHORE)
_ANY = pl.BlockSpec(memory_space=pl.ANY)
_DATAFLOW = pltpu.SideEffectType.DATAFLOW_SIDE_EFFECTING


def _flip_peer(flip):
    mx, my, mc = _mesh_pos()
    return (1 - mx if flip & 2 else mx, 1 - my if flip & 1 else my, mc)


def _remote(src, dst, send_sems, recv_sems, k, peer):
    return pltpu.make_async_remote_copy(src_ref=src, dst_ref=dst, send_sem=send_sems.at[k], recv_sem=recv_sems.at[k],
                                        device_id=peer, device_id_type=MESH)


def _scatter_sibling_copies(srcs, lands, send_sems, recv_sems):
    mx, my, mc = _mesh_pos()
    return [_remote(srcs[a].at[k, 1 - mc], lands[a].at[k], send_sems, recv_sems, 4 * a + k, (mx, my, 1 - mc))
            for a in range(len(srcs)) for k in range(4)]


def _scatter_chips_copies(srcs, lands, send_sems, recv_sems):
    mx, my, _ = _mesh_pos()
    k0 = 2 * mx + my
    return [_remote(srcs[a].at[jnp.bitwise_xor(k0, flip)], lands[a].at[flip - 1], send_sems, recv_sems,
                    3 * a + flip - 1, _flip_peer(flip))
            for a in range(len(srcs)) for flip in (1, 2, 3)]


class _Exchange:
    def __init__(self, copies, n_src, send_sems, recv_sems, thru, token):
        self.copies, self.n_src, self.send_sems, self.recv_sems, self.thru, self.token = (
            copies, n_src, send_sems, recv_sems, thru, token)


def _exchange_start(name, copies, srcs, lands, n_copies, after=()):
    bufs = list(srcs) + list(lands)
    nb, ns = len(bufs), len(srcs)

    def body(*refs):
        send_sems, recv_sems = refs[nb + len(after)], refs[nb + len(after) + 1]
        for cp in copies(refs[:ns], refs[ns:nb], send_sems, recv_sems):
            cp.start()
        refs[-1][...] = jnp.zeros_like(refs[-1])

    out = pl.pallas_call(
        body, name=name,
        out_shape=(pltpu.SemaphoreType.DMA((n_copies,)), pltpu.SemaphoreType.DMA((n_copies,)),
                   *[pltpu.HBM(b.shape, b.dtype) for b in bufs], pltpu.HBM((SUBLANES, LANES), F32)),
        in_specs=[_HBM] * nb + [_ANY] * len(after),
        out_specs=(_SEM, _SEM, *[_HBM] * nb, pl.BlockSpec(memory_space=pltpu.VMEM)),
        input_output_aliases={i: 2 + i for i in range(nb)},
        compiler_params=pltpu.CompilerParams(has_side_effects=_DATAFLOW),
    )(*[pltpu.with_memory_space_constraint(b, pltpu.HBM) for b in bufs], *after)
    return _Exchange(copies, ns, out[0], out[1], list(out[2:2 + nb]), out[-1])


def _exchange_wait(name, ex, after):
    nb, ns = len(ex.thru), ex.n_src

    def body(*refs):
        for cp in ex.copies(refs[:ns], refs[ns:nb], refs[nb], refs[nb + 1]):
            cp.wait_send()
            cp.wait_recv()

    out = pl.pallas_call(
        body, name=name, out_shape=tuple(pltpu.HBM(b.shape, b.dtype) for b in ex.thru),
        in_specs=[_HBM] * nb + [_SEM, _SEM] + [_ANY] * len(after), out_specs=tuple([_HBM] * nb),
        input_output_aliases={i: i for i in range(nb)},
        compiler_params=pltpu.CompilerParams(has_side_effects=_DATAFLOW),
    )(*ex.thru, ex.send_sems, ex.recv_sems, *after)
    return list(out[:ns]), list(out[ns:])


def _halves(ref):
    half = ref.shape[1] // 2
    if half % LANES == 0:
        return ref.at[:, pl.ds(0, half)], ref.at[:, pl.ds(half, half)]
    half = ref.shape[0] // 2
    assert half % (2 * SUBLANES) == 0, ref.shape
    return ref.at[pl.ds(0, half)], ref.at[pl.ds(half, half)]


def _gather_two_route_copies(srcs, lands, send_sems, recv_sems):
    mx, my, mc = _mesh_pos()
    me = 4 * mx + 2 * my + mc
    return [_remote(_halves(srcs[a])[h], _halves(lands[a].at[me])[h], send_sems, recv_sems, 4 * a + i, _flip_peer(flip))
            for a in range(len(srcs)) for i, (flip, h) in enumerate(((2, 0), (1, 1), (2, 1), (1, 0)))]


def _to_sibling(land, flip, h, send_sems, recv_sems, k):
    mx, my, mc = _mesh_pos()
    part = _halves(land.at[2 * jnp.bitwise_xor(2 * mx + my, flip) + mc])[h]
    return _remote(part, part, send_sems, recv_sems, k, (mx, my, 1 - mc))


def _second_hop(land, send_sems, recv_sems, k):
    mx, my, mc = _mesh_pos()
    k0 = 2 * mx + my
    from_y = _halves(land.at[2 * jnp.bitwise_xor(k0, 1) + mc])[1]
    from_x = _halves(land.at[2 * jnp.bitwise_xor(k0, 2) + mc])[0]
    return (_remote(from_y, from_y, send_sems, recv_sems, k, _flip_peer(2)),
            _remote(from_x, from_x, send_sems, recv_sems, k + 1, _flip_peer(1)))


def _relay_call(name, body, bufs, sems, n_new, after):
    nb, n_in = len(bufs), len(bufs) + len(sems) + len(after)

    def call_body(*refs):
        body(refs[:nb], refs[nb:nb + len(sems)], refs[n_in], refs[n_in + 1])
        refs[-1][...] = jnp.zeros_like(refs[-1])

    out = pl.pallas_call(
        call_body, name=name,
        out_shape=(pltpu.SemaphoreType.DMA((n_new,)), pltpu.SemaphoreType.DMA((n_new,)),
                   *[pltpu.HBM(b.shape, b.dtype) for b in bufs], pltpu.HBM((SUBLANES, LANES), F32)),
        in_specs=[_HBM] * nb + [_SEM] * len(sems) + [_ANY] * len(after),
        out_specs=(_SEM, _SEM, *[_HBM] * nb, pl.BlockSpec(memory_space=pltpu.VMEM)),
        input_output_aliases={i: 2 + i for i in range(nb)},
        compiler_params=pltpu.CompilerParams(has_side_effects=_DATAFLOW),
    )(*bufs, *sems, *after)
    return out[0], out[1], list(out[2:2 + nb]), out[-1]


def _gather_relay(ex, tag, after_first):
    ns = ex.n_src
    n = len(ex.thru) - ns

    def first(bufs, sems, send, recv):
        lands, first_hop = bufs[ns:], ex.copies(bufs[:ns], bufs[ns:], *sems)
        for a in range(n):
            for h in (0, 1):
                _to_sibling(lands[a], 0, h, send, recv, 10 * a + h).start()
        for a in range(n):
            x_first, y_second, x_second, y_first = first_hop[4 * a:4 * a + 4]
            to_x, to_y = _second_hop(lands[a], send, recv, 10 * a + 8)
            x_first.wait_recv()
            to_y.start()
            _to_sibling(lands[a], 2, 0, send, recv, 10 * a + 4).start()
            y_second.wait_recv()
            to_x.start()
            _to_sibling(lands[a], 1, 1, send, recv, 10 * a + 3).start()
            x_second.wait_recv()
            _to_sibling(lands[a], 2, 1, send, recv, 10 * a + 5).start()
            y_first.wait_recv()
            _to_sibling(lands[a], 1, 0, send, recv, 10 * a + 2).start()
        for cp in first_hop:
            cp.wait_send()

    def second(lands, sems, send, recv):
        for a in range(n):
            to_x, to_y = _second_hop(lands[a], *sems, 10 * a + 8)
            to_y.wait_recv()
            _to_sibling(lands[a], 3, 0, send, recv, 2 * a).start()
            to_x.wait_recv()
            _to_sibling(lands[a], 3, 1, send, recv, 2 * a + 1).start()
            to_x.wait_send()
            to_y.wait_send()

    def last(lands, sems, send, recv):
        for a in range(n):
            for flip in range(4):
                for h in (0, 1):
                    s, r, k = (sems[2], sems[3], 2 * a + h) if flip == 3 else (sems[0], sems[1], 10 * a + 2 * flip + h)
                    cp = _to_sibling(lands[a], flip, h, s, r, k)
                    cp.wait_send()
                    cp.wait_recv()

    send1, recv1, bufs, token = _relay_call(f"{tag}_chips_relay", first, ex.thru, [ex.send_sems, ex.recv_sems], 10 * n,
                                            after_first)

    def run_second(after):
        send2, recv2, lands, token = _relay_call(f"{tag}_diagonal_relay", second, bufs[ns:], [send1, recv1], 2 * n, after)
        return token, lambda after_last: _relay_call(f"{tag}_sibling_wait", last, lands, [send1, recv1, send2, recv2],
                                                     1, after_last)[2]

    return token, run_second


def _col_tile(r, c):
    return next(t for t in (1024, 512, 256, 128) if c % t == 0 and (r * t * 4 <= 2**20 or t == 128))


def _add_sibling(g4, recv, pos, name):
    _, _, r, c = g4.shape
    tc = _col_tile(r, c)

    def body(pos_ref, g_ref, r_ref, o16_ref, mine_ref):
        s = g_ref[0, 0] + r_ref[0]
        o16_ref[0] = s.astype(BF16)

        @pl.when(pl.program_id(1) == pos_ref[1])
        def _():
            mine_ref[...] = s

    slot = pl.BlockSpec((1, r, tc), lambda j, k, pos_ref: (k, 0, j))
    return pl.pallas_call(
        body, name=name,
        out_shape=[pltpu.HBM((4, r, c), BF16), pltpu.HBM((r, c), F32)],
        grid_spec=pltpu.PrefetchScalarGridSpec(
            num_scalar_prefetch=1, grid=(c // tc, 4),
            in_specs=[pl.BlockSpec((1, 1, r, tc), lambda j, k, pos_ref: (k, pos_ref[0], 0, j)), slot],
            out_specs=[slot, pl.BlockSpec((r, tc), lambda j, k, pos_ref: (0, j))]),
        compiler_params=pltpu.CompilerParams(dimension_semantics=("parallel", "arbitrary")),
    )(pos, *_in_hbm(g4, recv))


class _ReduceScatter:
    def __init__(self, tag, grads_t, pos):
        self.tag, self.pos, self.names = tag, pos, list(grads_t)
        g4s = [g.reshape(4, 2, g.size // (N_DEV * g.shape[-1]), g.shape[-1]) for g in grads_t.values()]
        lands = [lax.empty((4,) + g.shape[2:], F32) for g in g4s]
        self.ex = _exchange_start(f"rs_{tag}_sibling_start", _scatter_sibling_copies, g4s, lands, 4 * len(g4s))
        self.token = self.ex.token

    def start_chips(self, after):
        g4s, from_sibling = _exchange_wait(f"rs_{self.tag}_sibling_wait", self.ex, after)
        parts = [_add_sibling(g4, rv, self.pos, f"rs_add_sibling_{k}")
                 for k, g4, rv in zip(self.names, g4s, from_sibling)]
        self.mine = [mine for _, mine in parts]
        p16s = [p16 for p16, _ in parts]
        lands = [lax.empty((3,) + p.shape[1:], BF16) for p in p16s]
        self.ex = _exchange_start(f"rs_{self.tag}_chips_start", _scatter_chips_copies, p16s, lands, 3 * len(p16s))
        self.token = self.ex.token

    def finish(self, after):
        _, from_chips = _exchange_wait(f"rs_{self.tag}_chips_wait", self.ex, after)
        return dict(zip(self.names, zip(self.mine, from_chips)))


def _rope_tables():
    positions = np.arange(SEQ, dtype=np.float32)
    inv_freq = np.power(np.float32(ROPE_THETA), -np.arange(0, ROPE_DIM, 2, dtype=np.float32) / np.float32(ROPE_DIM))
    ang = (positions[:, None] * inv_freq[None, :]).astype(np.float32)
    cos, sin = np.cos(ang).astype(np.float32), np.sin(ang).astype(np.float32)
    ones = np.ones((SEQ, HEAD_DIM - ROPE_DIM), np.float32)
    zeros8 = np.zeros((SEQ, ROPE_HALF), np.float32)
    zeros = np.zeros((SEQ, HEAD_DIM - ROPE_DIM), np.float32)
    c_head = np.concatenate([cos, cos, ones], axis=1)
    s1_head = np.concatenate([-sin, zeros8, zeros], axis=1)
    s2_head = np.concatenate([zeros8, sin, zeros], axis=1)
    return tuple(jnp.asarray(np.concatenate([t, t], axis=1)) for t in (c_head, s1_head, s2_head))


def _rope_apply(x, c, s1, s2):
    w = x.shape[1]
    return x * c + pltpu.roll(x, w - ROPE_HALF, 1) * s1 + pltpu.roll(x, ROPE_HALF, 1) * s2


def _rope_apply_t(dy, c, s1, s2):
    w = dy.shape[1]
    return dy * c + pltpu.roll(dy * s1, ROPE_HALF, 1) + pltpu.roll(dy * s2, w - ROPE_HALF, 1)


def _dil_prev_limit(has_prev):
    return jnp.where(has_prev, 0, BLOCK)


def _dil_valid(limit):
    row = lax.broadcasted_iota(jnp.int32, (BLOCK, 2 * BLOCK), 0)
    col = lax.broadcasted_iota(jnp.int32, (BLOCK, 2 * BLOCK), 1)
    dist = col - row
    return jnp.logical_and(dist >= jnp.where(col < BLOCK, limit, -BLOCK), dist <= BLOCK)


def _upper_half():
    return lax.broadcasted_iota(jnp.int32, (1, LANES), 1) >= HEAD_DIM


def _stack_heads(x):
    upper = _upper_half()
    return jnp.concatenate([jnp.where(upper, 0, x), jnp.where(upper, x, 0)], axis=0)


def _unstack_heads(y):
    n = y.shape[0] // 2
    return jnp.where(_upper_half(), y[n:], y[:n])


def _head_columns(t):
    return jnp.concatenate([t[:, 0:1], t[:, HEAD_DIM:HEAD_DIM + 1]], axis=0)


def _dil_rows(n, d):
    per = N_BLOCKS // d
    r, lb = n // per, n % per

    def rows(b):
        start = b * (BLOCK * d) + r
        return pl.ds(pl.multiple_of(start, BLOCK), BLOCK) if d == 1 else pl.ds(start, BLOCK, stride=d)

    return rows(lb), rows(jnp.maximum(lb - 1, 0)), lb > 0


def _dil_rotate(q_ref, k_ref, c_ref, s1_ref, s2_ref, q_rot, k_rot):
    tabs = (c_ref[...], s1_ref[...], s2_ref[...])
    q_rot[...] = _rope_apply(q_ref[...], *tabs) * QK_SCALE
    k_rot[...] = _rope_apply(k_ref[...], *tabs)


def _dil_specs():
    def col(base):
        return pl.BlockSpec((SEQ, LANES), lambda p: (0, base // LANES + p))

    table = pl.BlockSpec((SEQ, LANES), lambda p: (0, 0))
    return [col(COL_QA), col(COL_KA), col(COL_VA)], [table] * 3


def _store_columns(blocks, dproj_ref, cols, sem):
    copies = [pltpu.make_async_copy(b, dproj_ref.at[:, pl.ds(pl.multiple_of(c * LANES, LANES), LANES)], sem.at[i])
              for i, (b, c) in enumerate(zip(blocks, cols))]
    for cp in copies:
        cp.start()
    for cp in copies:
        cp.wait()


def _dil_window(d, n, k_rot, v_ref):
    rows, prev, has_prev = _dil_rows(n, d)
    kw, vw = k_rot[rows, :].astype(BF16), v_ref[rows, :].astype(BF16)
    if d == N_BLOCKS:
        row = lax.broadcasted_iota(jnp.int32, (BLOCK, BLOCK), 0)
        valid = lax.broadcasted_iota(jnp.int32, (BLOCK, BLOCK), 1) <= row
    else:
        kw = jnp.concatenate([k_rot[prev, :].astype(BF16), kw], axis=0)
        vw = jnp.concatenate([v_ref[prev, :].astype(BF16), vw], axis=0)
        valid = _dil_valid(_dil_prev_limit(has_prev))
    return rows, prev, kw, vw, jnp.concatenate([valid, valid], axis=0)


def _dil_fwd(proj, tables, deps=()):
    def body(q_ref, k_ref, v_ref, c_ref, s1_ref, s2_ref, *rest):
        o_ref, lse_ref, q_rot, k_rot = rest[len(deps):]
        upper = _upper_half()
        _dil_rotate(q_ref, k_ref, c_ref, s1_ref, s2_ref, q_rot, k_rot)

        def blocks_of(d):
            def block(n, carry):
                rows, _, kw, vw, valid = _dil_window(d, n, k_rot, v_ref)
                s = jnp.where(valid, _dot_nt(_stack_heads(q_rot[rows, :].astype(BF16)), kw), NEG_INF)
                m = jnp.max(s, axis=-1, keepdims=True)
                p = jnp.exp(s - m)
                den = jnp.sum(p, axis=-1, keepdims=True)
                o_ref[rows, :] = _unstack_heads(_dot_nn((p * (1.0 / den)).astype(BF16), vw))
                lse = m + jnp.log(den)
                lse_ref[rows, :] = jnp.where(upper, lse[BLOCK:], lse[:BLOCK])
                return carry

            lax.fori_loop(0, N_BLOCKS, block, 0, unroll=4)

        for g, d in enumerate(DILATIONS):
            pl.when(pl.program_id(0) // 2 == g)(functools.partial(blocks_of, d))

    qkv, tabs = _dil_specs()
    out = pl.BlockSpec((SEQ, LANES), lambda p: (0, p))
    return pl.pallas_call(
        body, name="dil_attn_fwd", grid=(DIL_WIDTH // LANES,), in_specs=qkv + tabs + [_ANY] * len(deps),
        out_specs=[out, out],
        out_shape=[pltpu.HBM((SEQ, DIL_WIDTH), F32)] * 2,
        scratch_shapes=[pltpu.VMEM((SEQ, LANES), F32)] * 2,
        compiler_params=pltpu.CompilerParams(dimension_semantics=("parallel",)),
    )(*_in_hbm(proj, proj, proj, *tables), *deps)


def _dil_bwd(proj, tables, do, lse, c, dproj, deps=()):
    def body(q_ref, k_ref, v_ref, c_ref, s1_ref, s2_ref, do_ref, lse_ref, cc_ref, dproj_in, *rest):
        dproj_ref, dq_acc, dk_acc, dv_acc, dq_out, dk_out, dv_out, q_rot, k_rot, sem = rest[len(deps):]
        dk_acc[...] = jnp.zeros_like(dk_acc)
        dv_acc[...] = jnp.zeros_like(dv_acc)
        _dil_rotate(q_ref, k_ref, c_ref, s1_ref, s2_ref, q_rot, k_rot)

        def blocks_of(d):
            def block(n, carry):
                rows, prev, kw, vw, valid = _dil_window(d, n, k_rot, v_ref)
                q2 = _stack_heads(q_rot[rows, :].astype(BF16))
                do2 = _stack_heads(do_ref[rows, :].astype(BF16))
                lse_col, c_col = _head_columns(lse_ref[rows, :]), _head_columns(cc_ref[rows, :])
                p = jnp.where(valid, jnp.exp(_dot_nt(q2, kw) - lse_col), 0.0)
                ds = (p * (_dot_nt(do2, vw) + c_col)).astype(BF16)
                dk, dv = _dot_tn(ds, q2), _dot_tn(p.astype(BF16), do2)
                dq_acc[rows, :] = _unstack_heads(_dot_nn(ds, kw)) * QK_SCALE
                if d == N_BLOCKS:
                    dk_acc[rows, :] += dk
                    dv_acc[rows, :] += dv
                else:
                    dk_acc[prev, :] += dk[:BLOCK]
                    dv_acc[prev, :] += dv[:BLOCK]
                    dk_acc[rows, :] += dk[BLOCK:]
                    dv_acc[rows, :] += dv[BLOCK:]
                return carry

            lax.fori_loop(0, N_BLOCKS, block, 0, unroll=4)

        pair = pl.program_id(0)
        for g, d in enumerate(DILATIONS):
            pl.when(pair // 2 == g)(functools.partial(blocks_of, d))
        tabs = (c_ref[...], s1_ref[...], s2_ref[...])
        dq_out[...] = _rope_apply_t(dq_acc[...], *tabs).astype(BF16)
        dk_out[...] = _rope_apply_t(dk_acc[...], *tabs).astype(BF16)
        dv_out[...] = dv_acc[...].astype(BF16)
        _store_columns((dq_out, dk_out, dv_out), dproj_ref,
                       [base // LANES + pair for base in (COL_QA, COL_KA, COL_VA)], sem)

    qkv, tabs = _dil_specs()
    tok = pl.BlockSpec((SEQ, LANES), lambda p: (0, p))
    return pl.pallas_call(
        body, name="dil_attn_bwd", grid=(DIL_WIDTH // LANES,),
        in_specs=qkv + tabs + [tok, tok, tok, _ANY] + [_ANY] * len(deps), out_specs=_ANY,
        out_shape=pltpu.HBM(dproj.shape, dproj.dtype),
        scratch_shapes=[pltpu.VMEM((SEQ, LANES), F32)] * 3 + [pltpu.VMEM((SEQ, LANES), BF16)] * 3
        + [pltpu.VMEM((SEQ, LANES), F32)] * 2 + [pltpu.SemaphoreType.DMA((3,))],
        input_output_aliases={9: 0},
        compiler_params=pltpu.CompilerParams(dimension_semantics=("arbitrary",)),
    )(*_in_hbm(proj, proj, proj, *tables, do, lse, c, dproj), *deps)


def _group_weights(l0, l1, l2):
    m = jnp.maximum(jnp.maximum(l0, l1), l2)
    e0, e1, e2 = jnp.exp(l0 - m), jnp.exp(l1 - m), jnp.exp(l2 - m)
    tot = e0 + e1 + e2
    return e0 / tot, e1 / tot, e2 / tot


def _dil_combine(o, lse, deps=()):
    def fn(o0, o1, o2, l0, l1, l2):
        w0, w1, w2 = _group_weights(l0, l1, l2)
        return w0 * o0 + w1 * o1 + w2 * o2

    w = DIL_OUT_WIDTH
    return _rowwise(fn, "dil_combine", SEQ, 512, [(o, w, g) for g in range(3)] + [(lse, w, g) for g in range(3)], [],
                    [(w, F32)], deps=deps)[0]


def _dil_combine_bwd(d_out, o, lse, deps=()):
    w = DIL_OUT_WIDTH

    def fn(d, o0, o1, o2, l0, l1, l2):
        row = lax.broadcasted_iota(jnp.int32, (w, w), 0) // HEAD_DIM
        col = lax.broadcasted_iota(jnp.int32, (w, w), 1) // HEAD_DIM
        same_head = jnp.where(row == col, 1.0, 0.0).astype(BF16)
        ws = _group_weights(l0, l1, l2)
        dws = [_dot3_nn(d * og, same_head) for og in (o0, o1, o2)]
        mean = ws[0] * dws[0] + ws[1] * dws[1] + ws[2] * dws[2]
        return jnp.concatenate([wg * d for wg in ws], axis=1), jnp.concatenate([-wg * mean for wg in ws], axis=1)

    return _rowwise(fn, "dil_combine_bwd", SEQ, 256,
                    [(d_out, w, 0)] + [(o, w, g) for g in range(3)] + [(lse, w, g) for g in range(3)], [],
                    [(DIL_WIDTH, F32)] * 2, deps=deps)


def _log1p(e):
    u = 1.0 + e
    return jnp.where(u == 1.0, e, jnp.log(u) * (e / (u - 1.0)))


def _fox_gate(proj, b_pad, deps=()):
    def body(f_ref, b_ref, *rest):
        o_ref = rest[-1]
        z = f_ref[...] + b_ref[...]
        logf = (jnp.minimum(z, 0.0) - _log1p(jnp.exp(-jnp.abs(z)))).T[:F_ROWS]
        row = lax.broadcasted_iota(jnp.int32, (BLOCK, BLOCK), 0)
        col = lax.broadcasted_iota(jnp.int32, (BLOCK, BLOCK), 1)
        before = jnp.where(row <= col, 1.0, 0.0).astype(BF16)
        carry = jnp.zeros((F_ROWS, 1), F32)
        for blk in range(N_BLOCKS):
            run = _dot3_nn(logf[:, blk * BLOCK:(blk + 1) * BLOCK], before) + carry
            o_ref[:, blk * BLOCK:(blk + 1) * BLOCK] = run
            carry = run[:, BLOCK - 1:BLOCK]

    return pl.pallas_call(
        body, name="fox_gate", grid=(1,),
        in_specs=[pl.BlockSpec((SEQ, LANES), lambda i: (0, COL_F // LANES)), pl.BlockSpec((1, LANES), lambda i: (0, 0))]
        + [_ANY] * len(deps),
        out_specs=pl.BlockSpec((F_ROWS, SEQ), lambda i: (0, 0)),
        out_shape=pltpu.HBM((F_ROWS, SEQ), F32),
    )(*_in_hbm(proj, b_pad), *deps)


def _fox_gate_bwd(d_cum, proj, b_pad, dproj):
    def body(d_ref, f_ref, b_ref, dproj_ref, dz_ref, db_ref):
        row = lax.broadcasted_iota(jnp.int32, (BLOCK, BLOCK), 0)
        col = lax.broadcasted_iota(jnp.int32, (BLOCK, BLOCK), 1)
        after = jnp.where(row >= col, 1.0, 0.0).astype(BF16)
        carry = jnp.zeros((F_ROWS, 1), F32)
        parts = [None] * N_BLOCKS
        for blk in reversed(range(N_BLOCKS)):
            run = _dot3_nn(d_ref[:, blk * BLOCK:(blk + 1) * BLOCK], after) + carry
            parts[blk] = run
            carry = run[:, 0:1]
        dlogf = jnp.concatenate(parts, axis=1)
        dlogf = jnp.concatenate([dlogf, jnp.zeros((LANES - F_ROWS, SEQ), F32)], axis=0).T
        dz = dlogf * _sigmoid(-(f_ref[...] + b_ref[...]))
        dz_ref[...] = dz.astype(BF16)
        db_ref[...] = jnp.sum(dz, axis=0, keepdims=True)

    f_cols = pl.BlockSpec((SEQ, LANES), lambda i: (0, COL_F // LANES))
    return pl.pallas_call(
        body, name="fox_gate_bwd", grid=(1,),
        in_specs=[pl.BlockSpec((F_ROWS, SEQ), lambda i: (0, 0)), f_cols, pl.BlockSpec((1, LANES), lambda i: (0, 0)), _ANY],
        out_specs=[f_cols, pl.BlockSpec((1, LANES), lambda i: (0, 0))],
        out_shape=[pltpu.HBM(dproj.shape, dproj.dtype), pltpu.HBM((1, LANES), F32)],
        input_output_aliases={3: 0},
    )(*_in_hbm(d_cum, proj, b_pad, dproj))


FOX_TILE = 256
FOX_TILES = SEQ // FOX_TILE


def _row_to_col(row):
    n = row.shape[1]
    eye = lax.broadcasted_iota(jnp.int32, (n, n), 0) == lax.broadcasted_iota(jnp.int32, (n, n), 1)
    return jnp.sum(jnp.where(eye, row, 0.0), axis=1, keepdims=True)


def _fox_bias(f_row, i):
    t = FOX_TILE
    ext = (i + 1) * t
    bias = _row_to_col(f_row[:, i * t:(i + 1) * t]) - f_row[:, :ext]
    row = lax.broadcasted_iota(jnp.int32, (t, ext), 0) + i * t
    col = lax.broadcasted_iota(jnp.int32, (t, ext), 1)
    return bias, col <= row


def _fox_specs():
    qkv = [pl.BlockSpec((SEQ, LANES), lambda p, base=base: (0, base // LANES + p)) for base in (COL_QB, COL_KB, COL_VB)]
    return qkv, pl.BlockSpec((F_ROWS, SEQ), lambda p: (0, 0))


def _fox_fwd(proj, f_rows):
    t = FOX_TILE

    def body(q_ref, k_ref, v_ref, f_ref, o_ref, lse_ref):
        pair = pl.program_id(0)
        upper = _upper_half()
        k16, v16 = k_ref[...].astype(BF16), v_ref[...].astype(BF16)
        f_row = [f_ref[pl.ds(2 * pair + e, 1), :] for e in range(2)]
        for i in range(FOX_TILES):
            ext = (i + 1) * t
            q_tile = (q_ref[i * t:(i + 1) * t, :] * QK_SCALE).astype(BF16)
            s2 = _dot_nt(_stack_heads(q_tile), k16[:ext])
            pns, lses = [], []
            for e in range(2):
                bias, causal = _fox_bias(f_row[e], i)
                s = jnp.where(causal, s2[e * t:(e + 1) * t] + bias, NEG_INF)
                m = jnp.max(s, axis=-1, keepdims=True)
                p = jnp.exp(s - m)
                den = jnp.sum(p, axis=-1, keepdims=True)
                pns.append((p * (1.0 / den)).astype(BF16))
                lses.append(m + jnp.log(den))
            o_ref[i * t:(i + 1) * t, :] = _unstack_heads(_dot_nn(jnp.concatenate(pns, axis=0), v16[:ext]))
            lse_ref[i * t:(i + 1) * t, :] = jnp.where(upper, lses[1], lses[0])

    qkv, f_spec = _fox_specs()
    tok = pl.BlockSpec((SEQ, LANES), lambda p: (0, p))
    return pl.pallas_call(
        body, name="fox_attn_fwd", grid=(FOX_WIDTH // LANES,),
        in_specs=qkv + [f_spec], out_specs=[tok, tok],
        out_shape=[pltpu.HBM((SEQ, FOX_WIDTH), F32)] * 2,
        compiler_params=pltpu.CompilerParams(
            dimension_semantics=("parallel",), vmem_limit_bytes=_vmem_limit(8 * t * SEQ * 4)),
    )(*_in_hbm(proj, proj, proj, f_rows))


def _fox_bwd(proj, do, lse, f_rows, dproj):
    t = FOX_TILE

    def body(q_ref, k_ref, v_ref, f_ref, do_ref, lse_ref, dproj_in, dproj_ref, df_ref, dk_acc, dv_acc,
             dq_out, dk_out, dv_out, sem):
        pair = pl.program_id(0)
        upper = _upper_half()
        k16, v16 = k_ref[...].astype(BF16), v_ref[...].astype(BF16)
        f_row = [f_ref[pl.ds(2 * pair + e, 1), :] for e in range(2)]
        dk_acc[...] = jnp.zeros_like(dk_acc)
        dv_acc[...] = jnp.zeros_like(dv_acc)
        df_ref[...] = jnp.zeros_like(df_ref)
        for i in range(FOX_TILES):
            ext = (i + 1) * t
            q_tile = (q_ref[i * t:(i + 1) * t, :] * QK_SCALE).astype(BF16)
            do_tile = do_ref[i * t:(i + 1) * t, :]
            lse_t = lse_ref[i * t:(i + 1) * t, :]
            q2, do2 = _stack_heads(q_tile), _stack_heads(do_tile)
            s2, dp2 = _dot_nt(q2, k16[:ext]), _dot_nt(do2, v16[:ext])
            ps, dss = [], []
            for e in range(2):
                bias, causal = _fox_bias(f_row[e], i)
                s = s2[e * t:(e + 1) * t] + bias
                p = jnp.where(causal, jnp.exp(s - lse_t[:, e * HEAD_DIM:e * HEAD_DIM + 1]), 0.0)
                dp = dp2[e * t:(e + 1) * t]
                ds = p * (dp - jnp.sum(p * dp, axis=-1, keepdims=True))
                df_ref[0, e:e + 1, :ext] -= jnp.sum(ds, axis=0, keepdims=True)
                ps.append(p.astype(BF16))
                dss.append(ds.astype(BF16))
            ds2, p2 = jnp.concatenate(dss, axis=0), jnp.concatenate(ps, axis=0)
            dq_out[i * t:(i + 1) * t, :] = (_unstack_heads(_dot_nn(ds2, k16[:ext])) * QK_SCALE).astype(BF16)
            dk_acc[:ext, :] += _dot_tn(ds2, q2)
            dv_acc[:ext, :] += _dot_tn(p2, do2)
        dk_out[...] = dk_acc[...].astype(BF16)
        dv_out[...] = dv_acc[...].astype(BF16)
        _store_columns((dq_out, dk_out, dv_out), dproj_ref, [base // LANES + pair for base in (COL_QB, COL_KB, COL_VB)],
                       sem)

    qkv, f_spec = _fox_specs()
    tok = pl.BlockSpec((SEQ, LANES), lambda p: (0, p))
    return pl.pallas_call(
        body, name="fox_attn_bwd", grid=(FOX_WIDTH // LANES,),
        in_specs=qkv + [f_spec, tok, tok, _ANY],
        out_specs=[_ANY, pl.BlockSpec((1, SUBLANES, SEQ), lambda p: (p, 0, 0))],
        out_shape=[pltpu.HBM(dproj.shape, dproj.dtype),
                   pltpu.HBM((FOX_WIDTH // LANES, SUBLANES, SEQ), F32)],
        scratch_shapes=[pltpu.VMEM((SEQ, LANES), F32)] * 2 + [pltpu.VMEM((SEQ, LANES), BF16)] * 3
        + [pltpu.SemaphoreType.DMA((3,))],
        input_output_aliases={6: 0},
        compiler_params=pltpu.CompilerParams(
            dimension_semantics=("arbitrary",), vmem_limit_bytes=_vmem_limit(10 * t * SEQ * 4)),
    )(*_in_hbm(proj, proj, proj, f_rows, do, lse, dproj))


MIX_TILE = 256


def _mix_out(out_a, out_b, proj, x, wt_pa, wt_pb, w_out, g_post, g_ffn_pre):
    tm = MIX_TILE

    def body(a_ref, b_ref, ga_ref, gb_ref, x_ref, wpa_ref, wpb_ref, wo_ref, g2_ref, g3_ref,
             merged_ref, mix_ref, x1_ref, h2_ref):
        ya = _dot_nn(a_ref[...].astype(BF16), wpa_ref[...])
        yb = _dot_nn(b_ref[...].astype(BF16), wpb_ref[...])
        merged = (_sigmoid(ga_ref[...]) * ya + _sigmoid(gb_ref[...]) * yb).astype(BF16)
        merged_ref[...] = merged
        mix = _dot_nn(merged, wo_ref[...])
        mix_ref[...] = mix
        x1 = x_ref[...] + mix * _rms_scale(mix) * g2_ref[...]
        x1_ref[...] = x1
        h2_ref[...] = (x1 * _rms_scale(x1) * g3_ref[...]).astype(BF16)

    def rows(w, cb=0):
        return pl.BlockSpec((tm, w), lambda i, cb=cb: (i, cb))

    def whole(a):
        return pl.BlockSpec(a.shape, lambda i: (0, 0))

    d = D_MODEL
    blk = _nbytes((tm, d), F32) * 6 + sum(_nbytes(a.shape, BF16) for a in (wt_pa, wt_pb, w_out))
    return pl.pallas_call(
        body, name="mix_out", grid=(SEQ // tm,),
        in_specs=[rows(DIL_OUT_WIDTH), rows(FOX_WIDTH), rows(d, COL_GA // d), rows(d, COL_GB // d), rows(d),
                  whole(wt_pa), whole(wt_pb), whole(w_out), whole(g_post), whole(g_ffn_pre)],
        out_specs=[rows(d)] * 4,
        out_shape=[pltpu.HBM((SEQ, d), dt) for dt in (BF16, F32, F32, BF16)],
        compiler_params=pltpu.CompilerParams(dimension_semantics=("parallel",), vmem_limit_bytes=_vmem_limit(blk)),
    )(*_in_hbm(out_a, out_b, proj, proj, x, wt_pa, wt_pb, w_out, g_post, g_ffn_pre))


def _mix_out_bwd(dmix, out_a, out_b, proj, wt_pa, wt_pb, w_out, deps=()):
    tm = MIX_TILE

    def body(dm_ref, a_ref, b_ref, ga_ref, gb_ref, wpa_ref, wpb_ref, wo_ref, *rest):
        dproj_ref, dya_ref, dyb_ref, da_ref, db_ref = rest[len(deps):]
        dmerged = _dot_nt(dm_ref[...], wo_ref[...])
        ya = _dot_nn(a_ref[...].astype(BF16), wpa_ref[...])
        yb = _dot_nn(b_ref[...].astype(BF16), wpb_ref[...])
        sa, sb = _sigmoid(ga_ref[...]), _sigmoid(gb_ref[...])
        dproj_ref[:, COL_GA:COL_GA + D_MODEL] = (dmerged * ya * (sa * (1.0 - sa))).astype(BF16)
        dproj_ref[:, COL_GB:COL_GB + D_MODEL] = (dmerged * yb * (sb * (1.0 - sb))).astype(BF16)
        dproj_ref[:, COL_GB + D_MODEL:] = jnp.zeros((tm, COL_QA - COL_GB - D_MODEL), BF16)
        dya = (dmerged * sa).astype(BF16)
        dyb = (dmerged * sb).astype(BF16)
        dya_ref[...] = dya
        dyb_ref[...] = dyb
        da_ref[...] = _dot_nt(dya, wpa_ref[...])
        db_ref[...] = _dot_nt(dyb, wpb_ref[...]).astype(BF16)

    def rows(w, cb=0):
        return pl.BlockSpec((tm, w), lambda i, cb=cb: (i, cb))

    def whole(a):
        return pl.BlockSpec(a.shape, lambda i: (0, 0))

    d = D_MODEL
    blk = _nbytes((tm, d), F32) * 8 + sum(_nbytes(a.shape, BF16) for a in (wt_pa, wt_pb, w_out))
    return pl.pallas_call(
        body, name="mix_out_bwd", grid=(SEQ // tm,),
        in_specs=[rows(d), rows(DIL_OUT_WIDTH), rows(FOX_WIDTH), rows(d, COL_GA // d), rows(d, COL_GB // d),
                  whole(wt_pa), whole(wt_pb), whole(w_out)] + [_ANY] * len(deps),
        out_specs=[rows(COL_QA)] + [rows(d)] * 2 + [rows(DIL_OUT_WIDTH), rows(FOX_WIDTH)],
        out_shape=[pltpu.HBM((SEQ, PROJ_COLS), BF16)] + [pltpu.HBM((SEQ, d), BF16)] * 2
        + [pltpu.HBM((SEQ, DIL_OUT_WIDTH), F32), pltpu.HBM((SEQ, FOX_WIDTH), BF16)],
        compiler_params=pltpu.CompilerParams(dimension_semantics=("parallel",), vmem_limit_bytes=_vmem_limit(blk)),
    )(*_in_hbm(dmix, out_a, out_b, proj, proj, wt_pa, wt_pb, w_out), *deps)


FFN_TM, FFN_TN = 2048, 256


def _ffn_up(h2, wt_gate, wt_up):
    tm, tn = FFN_TM, FFN_TN

    def body(h_ref, wg_ref, wu_ref, gate_ref, up_ref, act_ref):
        for rows in (slice(0, tm // 2), slice(tm // 2, tm)):
            gate = _dot_nt(h_ref[rows, :], wg_ref[...])
            up = _dot_nt(h_ref[rows, :], wu_ref[...])
            gate_ref[rows, :] = gate
            up_ref[rows, :] = up
            act_ref[rows, :] = (gate * _sigmoid(gate) * up).astype(BF16)

    tile = pl.BlockSpec((tm, tn), lambda i, j: (i, j))
    w_spec = pl.BlockSpec((tn, D_MODEL), lambda i, j: (j, 0))
    return pl.pallas_call(
        body, name="ffn_up", grid=(SEQ // tm, D_FF // tn),
        in_specs=[pl.BlockSpec((tm, D_MODEL), lambda i, j: (i, 0)), w_spec, w_spec],
        out_specs=[tile, tile, tile],
        out_shape=[pltpu.HBM((SEQ, D_FF), dt) for dt in (F32, F32, BF16)],
        compiler_params=pltpu.CompilerParams(
            dimension_semantics=("parallel", "parallel"), vmem_limit_bytes=_vmem_limit(8 * 2**20)),
    )(h2, wt_gate, wt_up)


def _ffn_act_bwd(dff, w_down, gate, up):
    tm, tn = FFN_TM, FFN_TN

    def body(d_ref, wd_ref, gate_ref, up_ref, dgate_ref, dup_ref):
        for rows in (slice(0, tm // 2), slice(tm // 2, tm)):
            dact = _dot_nt(d_ref[rows, :], wd_ref[...])
            gate = gate_ref[rows, :]
            sg = _sigmoid(gate)
            dgate_ref[rows, :] = (dact * up_ref[rows, :] * (sg * (1.0 + gate * (1.0 - sg)))).astype(BF16)
            dup_ref[rows, :] = (dact * (gate * sg)).astype(BF16)

    tile = pl.BlockSpec((tm, tn), lambda i, j: (i, j))
    return pl.pallas_call(
        body, name="ffn_act_bwd", grid=(SEQ // tm, D_FF // tn),
        in_specs=[pl.BlockSpec((tm, D_MODEL), lambda i, j: (i, 0)), pl.BlockSpec((tn, D_MODEL), lambda i, j: (j, 0)),
                  tile, tile],
        out_specs=[tile, tile],
        out_shape=[pltpu.HBM((SEQ, D_FF), BF16)] * 2,
        compiler_params=pltpu.CompilerParams(
            dimension_semantics=("parallel", "parallel"), vmem_limit_bytes=_vmem_limit(8 * 2**20)),
    )(dff, w_down, gate, up)


EPILOGUE_TM = 512


def _loss_head(act, w_down, x1, target, g_post):
    def fn(ff, x1, tgt, g):
        r = _rms_scale(ff)
        nrm = ff * r
        err = (x1 + nrm * g) - tgt
        loss = 0.5 * jnp.sum(jnp.mean(err * err, axis=-1, keepdims=True), axis=0, keepdims=True)
        dy = err * (1.0 / D_MODEL)
        u = dy * g
        dff = r * u - ff * (r * r * r) * jnp.mean(u * ff, axis=-1, keepdims=True)
        return dy, dff, jnp.broadcast_to(loss, (1, LANES)), jnp.sum(dy * nrm, axis=0, keepdims=True)

    d = D_MODEL
    return _matmul_rowwise([(act, w_down)], fn, "ffn_down_loss", EPILOGUE_TM, [(x1, d, 0), (target, d, 0)], [g_post],
                           [(d, F32), (d, BF16)], [LANES, d])


def _post_ffn_bwd(dgate, wt_gate, dup, wt_up, x1, dy, mix, g_ffn_pre, g_mix_post, deps=()):
    def fn(dh2, x1, dy, mix, g3, g2):
        dx, dg3 = _rms_bwd(x1, dh2, g3)
        dx1 = dy + dx
        dmix, dg2 = _rms_bwd(mix, dx1, g2)
        return dx1, dmix, dg3, dg2

    d = D_MODEL
    return _matmul_rowwise([(dgate, wt_gate), (dup, wt_up)], fn, "ffn_up_bwd", EPILOGUE_TM,
                           [(x1, d, 0), (dy, d, 0), (mix, d, 0)], [g_ffn_pre, g_mix_post],
                           [(d, F32), (d, BF16)], [d, d], deps=deps)


def _input_bwd(dproj, wt_r, x, dx1, g_pre, deps=()):
    def fn(dh, x, dx1, g):
        dx, dg = _rms_bwd(x, dh, g)
        return dx1 + dx, dg

    d = D_MODEL
    return _matmul_rowwise([(dproj, wt_r)], fn, "in_proj_bwd", EPILOGUE_TM, [(x, d, 0), (dx1, d, 0)], [g_pre],
                           [(d, F32)], [d], deps=deps)


def _adam_math(w, g, m, v):
    m = ADAM_B1 * m + (1.0 - ADAM_B1) * g
    v = ADAM_B2 * v + (1.0 - ADAM_B2) * (g * g)
    m_hat = m / (1.0 - ADAM_B1 ** ADAM_STEP)
    v_hat = v / (1.0 - ADAM_B2 ** ADAM_STEP)
    delta = -ADAM_LR * (m_hat / (jnp.sqrt(v_hat) + ADAM_EPS) + ADAM_WD * w)
    return delta, m, v


def _adam(w, mine, recv, m, v, name):
    r, c = w.shape
    tc = _col_tile(r, c)

    def body(w_ref, p_ref, r_ref, m_ref, v_ref, g_ref, d_ref, nm_ref, nv_ref):
        g = ((p_ref[...] + r_ref[0].astype(F32)) + r_ref[1].astype(F32)) + r_ref[2].astype(F32)
        g_ref[...] = g
        d_ref[...], nm_ref[...], nv_ref[...] = _adam_math(w_ref[...], g, m_ref[...], v_ref[...])

    spec = pl.BlockSpec((r, tc), lambda j: (0, j))
    return pl.pallas_call(
        body, name=name, grid=(c // tc,),
        in_specs=[spec, spec, pl.BlockSpec((3, r, tc), lambda j: (0, 0, j)), spec, spec], out_specs=[spec] * 4,
        out_shape=[pltpu.HBM((r, c), F32)] * 4,
        compiler_params=pltpu.CompilerParams(dimension_semantics=("parallel",)),
    )(*_in_hbm(w, mine, recv, m, v))


def _adam_small(gathered, ws, ms, vs, loss_parts):
    n = len(ws)

    def body(*refs):
        outs = refs[4 * n + 1:]
        loss = refs[4 * n][0]
        for dev in range(1, N_DEV):
            loss = loss + refs[4 * n][dev]
        outs[4 * n][...] = loss
        for i in range(n):
            ga_ref, w_ref, m_ref, v_ref = (refs[j * n + i] for j in range(4))
            g = ga_ref[0]
            for dev in range(1, N_DEV):
                g = g + ga_ref[dev]
            g = g[:, :w_ref.shape[1]]
            outs[4 * i][...] = g
            outs[4 * i + 1][...], outs[4 * i + 2][...], outs[4 * i + 3][...] = _adam_math(
                w_ref[...], g, m_ref[...], v_ref[...])

    out_shape = [pltpu.HBM(w.shape, F32) for w in ws for _ in range(4)]
    out_shape.append(pltpu.HBM((1, LANES), F32))
    out = pl.pallas_call(body, name="adam_small", out_shape=out_shape)(*gathered, *ws, *ms, *vs, loss_parts)
    return [out[4 * i:4 * i + 4] for i in range(n)], out[4 * n]


_PROJ_SEGMENTS = ((3848, 5896), (None, COL_QA - 2 * D_MODEL), (0, 3840), (3840, 3848), (None, PROJ_COLS - COL_F - 8))


def _proj_weight_t(gathered):
    pieces, zeros, at = [], [], 0
    for lo, hi in _PROJ_SEGMENTS:
        if lo is None:
            zeros.append((at, hi))
            at += hi
            continue
        for dev in range(lo // IN_SHARD, (hi - 1) // IN_SHARD + 1):
            a, b = max(lo, dev * IN_SHARD), min(hi, (dev + 1) * IN_SHARD)
            pieces.append((dev, a - dev * IN_SHARD, at + a - lo, b - a))
        at += hi - lo
    assert at == PROJ_COLS
    tc = 2 * LANES

    def body(g_ref, o_ref, shards, rows):
        for dev in range(N_DEV):
            shards[dev] = g_ref[dev].astype(F32)
        for dev, src, dst, n in pieces:
            rows[pl.ds(dst, n), :] = shards[dev, pl.ds(src, n), :]
        for dst, n in zeros:
            rows[pl.ds(dst, n), :] = jnp.zeros((n, tc), F32)
        o_ref[...] = rows[...].astype(o_ref.dtype)

    return pl.pallas_call(
        body, name="w_in_rows", grid=(D_MODEL // tc,),
        in_specs=[pl.BlockSpec((N_DEV, IN_SHARD, tc), lambda j: (0, 0, j))],
        out_specs=pl.BlockSpec((PROJ_COLS, tc), lambda j: (0, j)),
        out_shape=pltpu.HBM((PROJ_COLS, D_MODEL), gathered.dtype),
        scratch_shapes=[pltpu.VMEM((N_DEV, IN_SHARD, tc), F32), pltpu.VMEM((PROJ_COLS, tc), F32)],
        compiler_params=pltpu.CompilerParams(
            dimension_semantics=("parallel",), vmem_limit_bytes=_vmem_limit(2 * _nbytes((PROJ_COLS, tc), F32))),
    )(*_in_hbm(gathered))


def _proj_weight_grad_slots(dwt_r):
    starts, at = [], 0
    for lo, hi in _PROJ_SEGMENTS:
        if lo is not None:
            starts.append((lo, hi, at))
        at += hi if lo is None else hi - lo
    pieces = []
    for dev in range(N_DEV):
        lo, end = dev * IN_SHARD, (dev + 1) * IN_SHARD
        for seg_lo, seg_hi, seg_at in sorted(starts):
            a, b = max(lo, seg_lo), min(end, seg_hi)
            if a < b:
                pieces.append((dev, a - lo, seg_at + a - seg_lo, b - a))

    def body(g_ref, o_ref):
        for dev, dst, src, rows in pieces:
            o_ref[dev, pl.ds(dst, rows), :] = g_ref[pl.ds(src, rows), :]

    tc = 2 * LANES
    return pl.pallas_call(
        body, name="grad_w_in_slots", grid=(D_MODEL // tc,),
        in_specs=[pl.BlockSpec((PROJ_COLS, tc), lambda j: (0, j))],
        out_specs=pl.BlockSpec((N_DEV, IN_SHARD, tc), lambda j: (0, 0, j)),
        out_shape=pltpu.HBM((N_DEV, IN_SHARD, D_MODEL), F32),
        compiler_params=pltpu.CompilerParams(
            dimension_semantics=("parallel",), vmem_limit_bytes=_vmem_limit(2 * _nbytes((PROJ_COLS, tc), F32))),
    )(*_in_hbm(dwt_r))


def kernel(x, w_in, w_proj_a, w_proj_b, w_out, b_forget, w_ffn_gate, w_ffn_up, w_ffn_down, norm_mix_pre, norm_mix_post, norm_ffn_pre, norm_ffn_post, loss_target, m_w_in, m_w_proj_a, m_w_proj_b, m_w_out, m_b_forget, m_w_ffn_gate, m_w_ffn_up, m_w_ffn_down, m_norm_mix_pre, m_norm_mix_post, m_norm_ffn_pre, m_norm_ffn_post, v_w_in, v_w_proj_a, v_w_proj_b, v_w_out, v_b_forget, v_w_ffn_gate, v_w_ffn_up, v_w_ffn_down, v_norm_mix_pre, v_norm_mix_post, v_norm_ffn_pre, v_norm_ffn_post):
    d = D_MODEL
    names = ("w_in", "w_proj_a", "w_proj_b", "w_out", "w_ffn_gate", "w_ffn_up", "w_ffn_down")
    col_sharded = ("w_in", "w_ffn_gate", "w_ffn_up")

    def row_shards(arrs):
        return {k: (a[0].T if k in col_sharded else a[0]) for k, a in zip(names, arrs)}

    shards = row_shards((w_in, w_proj_a, w_proj_b, w_out, w_ffn_gate, w_ffn_up, w_ffn_down))
    moments_m = row_shards((m_w_in, m_w_proj_a, m_w_proj_b, m_w_out, m_w_ffn_gate, m_w_ffn_up, m_w_ffn_down))
    moments_v = row_shards((v_w_in, v_w_proj_a, v_w_proj_b, v_w_out, v_w_ffn_gate, v_w_ffn_up, v_w_ffn_down))
    pos = jnp.stack([lax.axis_index("c"), 2 * lax.axis_index("x") + lax.axis_index("y")]).astype(jnp.int32)
    x2, target = x[0], loss_target[0]

    me = 4 * lax.axis_index("x") + 2 * lax.axis_index("y") + lax.axis_index("c")
    mid_names, ffn_names = names[1:4], names[4:]
    first_names, later_names = names[:1], names[1:]
    shards16 = {k: shards[k].astype(BF16) for k in names}

    def landing(k):
        return lax.dynamic_update_slice(lax.empty((N_DEV,) + shards[k].shape, BF16), shards16[k][None], (me, 0, 0))

    ag_first = _exchange_start("ag_first_chips_start", _gather_two_route_copies, [shards16[k] for k in first_names],
                               [landing(k) for k in first_names], 4 * len(first_names))
    h = _rowwise(lambda xb, g: xb * _rms_scale(xb) * g, "norm_mix_pre", SEQ, 256, [(x2, d, 0)], [norm_mix_pre],
                 [(d, BF16)], deps=[ag_first.token])[0]
    later_lands = [landing(k) for k in later_names]
    _, ag_first_diagonal = _gather_relay(ag_first, "ag_first", [h, shards["w_in"], moments_m["w_in"], moments_v["w_in"],
                                                               *later_lands, *[shards16[k] for k in later_names]])
    relayed, ag_first_last = ag_first_diagonal([])
    ag_later = _exchange_start("ag_later_chips_start", _gather_two_route_copies, [shards16[k] for k in later_names],
                               later_lands, 4 * len(later_names), after=[relayed])
    gathered = dict(zip(first_names, ag_first_last([ag_later.token])))
    wt_r = _proj_weight_t(gathered["w_in"])

    proj = _matmul([(h, *_in_hbm(wt_r))], "nt", F32, "in_proj", 1024, 896, 1024)
    tables = _rope_tables()
    relayed, ag_later_diagonal = _gather_relay(ag_later, "ag_later", [proj])
    o_dil, lse_dil = _dil_fwd(proj, tables, deps=[relayed])
    out_a = _dil_combine(o_dil, lse_dil)
    relayed, ag_later_last = ag_later_diagonal([out_a])

    b_pad = jnp.pad(b_forget, ((0, 0), (0, LANES - N_FOX_HEADS)))
    f_rows = _fox_gate(proj, b_pad, deps=[relayed])
    out_b, lse_fox = _fox_fwd(proj, f_rows)

    gathered = dict(zip(later_names, ag_later_last([out_b])))
    wt_pa = gathered["w_proj_a"].transpose(1, 0, 2).reshape(DIL_OUT_WIDTH, d)
    wt_pb = gathered["w_proj_b"].transpose(1, 0, 2).reshape(FOX_WIDTH, d)
    w_o = gathered["w_out"].reshape(d, d)
    wt_g = gathered["w_ffn_gate"].reshape(D_FF, d)
    wt_u = gathered["w_ffn_up"].reshape(D_FF, d)
    w_d = gathered["w_ffn_down"].reshape(D_FF, d)
    merged, mix, x1, h2 = _mix_out(out_a, out_b, proj, x2, wt_pa, wt_pb, w_o, norm_mix_post, norm_ffn_pre)

    gate, up, act = _ffn_up(h2, wt_g, wt_u)
    dy, dff, loss_part, dg_ffn_post = _loss_head(act, w_d, x1, target, norm_ffn_post)

    dgate, dup = _ffn_act_bwd(dff, w_d, gate, up)
    grads_t = {}
    grads_t["w_ffn_down"] = _matmul([(act, dff)], "tn", F32, "grad_w_ffn_down", 1408, 512, 2048, staged=False)
    grads_t["w_ffn_gate"] = _matmul([(dgate, h2)], "tn", F32, "grad_w_ffn_gate", 1408, 512, 2048)
    grads_t["w_ffn_up"] = _matmul([(dup, h2)], "tn", F32, "grad_w_ffn_up", 1408, 512, 2048)
    rs_ffn = _ReduceScatter("ffn", {k: grads_t[k] for k in ffn_names}, pos)
    dx1, dmix, dg_ffn_pre, dg_mix_post = _post_ffn_bwd(dgate, wt_g, dup, wt_u, x1, dy, mix, norm_ffn_pre, norm_mix_post,
                                                       deps=[rs_ffn.token])
    rs_ffn.start_chips([dmix])

    dproj, dya, dyb, d_out_a, d_out_b = _mix_out_bwd(dmix, out_a, out_b, proj, wt_pa, wt_pb, w_o, deps=[rs_ffn.token])
    grads_t["w_out"] = _matmul([(merged, dmix)], "tn", F32, "grad_w_out", 1024, 1024, 1024)
    def column_slots(g):
        return g.reshape(g.shape[0], N_DEV, LANES).transpose(1, 0, 2)

    grads_t["w_proj_a"] = column_slots(_matmul([(out_a, dya)], "tn", F32, "grad_w_proj_a", DIL_OUT_WIDTH, 1024, SEQ))
    grads_t["w_proj_b"] = column_slots(_matmul([(out_b, dyb)], "tn", F32, "grad_w_proj_b", FOX_WIDTH, 1024, SEQ))
    rs_mid = _ReduceScatter("mid", {k: grads_t[k] for k in mid_names}, pos)

    do_dil, c_dil = _dil_combine_bwd(d_out_a, o_dil, lse_dil, deps=[rs_mid.token])
    rs_mid.start_chips([c_dil])
    dproj, d_cum = _fox_bwd(proj, d_out_b, lse_fox, f_rows, dproj)
    d_cum_rows = jnp.pad(d_cum[:, :2].reshape(N_FOX_HEADS, SEQ), ((0, F_ROWS - N_FOX_HEADS), (0, 0)))
    dproj, db_part = _fox_gate_bwd(d_cum_rows, proj, b_pad, dproj)
    dproj = _dil_bwd(proj, tables, do_dil, lse_dil, c_dil, dproj, deps=[rs_mid.token])

    dwt_r = _matmul([(dproj, h)], "tn", F32, "grad_w_in", 896, 1024, 2048)
    rs_in = _ReduceScatter("in", {"w_in": _proj_weight_grad_slots(dwt_r)}, pos)
    def finish(rs, after):
        return {k: _adam(shards[k], mine, recv, moments_m[k], moments_v[k], "adam_" + k)
                for k, (mine, recv) in rs.finish(after).items()}

    done = finish(rs_ffn, [rs_in.token])
    rs_in.start_chips([done[k][0] for k in ffn_names])
    grad_x, dg_mix_pre = _input_bwd(dproj, wt_r, x2, dx1, norm_mix_pre, deps=[rs_in.token])
    done.update(finish(rs_mid, [grad_x]))

    small_all = _all_gather([dg_mix_pre, dg_mix_post, dg_ffn_pre, dg_ffn_post, db_part, loss_part],
                            "small_grads_all_gather", deps=[done[k][0] for k in mid_names])
    small, loss = _adam_small(small_all[:5], [norm_mix_pre, norm_mix_post, norm_ffn_pre, norm_ffn_post, b_forget],
                              [m_norm_mix_pre, m_norm_mix_post, m_norm_ffn_pre, m_norm_ffn_post, m_b_forget],
                              [v_norm_mix_pre, v_norm_mix_post, v_norm_ffn_pre, v_norm_ffn_post, v_b_forget],
                              small_all[5])

    done.update(finish(rs_in, [small[0][0]]))

    def leaves(i):
        def nat(k):
            a = done[k][i]
            return (a.T if k in col_sharded else a)[None]

        return [nat("w_in"), nat("w_proj_a"), nat("w_proj_b"), nat("w_out"), small[4][i],
                nat("w_ffn_gate"), nat("w_ffn_up"), nat("w_ffn_down"), *[small[r][i] for r in range(4)]]

    return (loss[0, 0], grad_x[None], *leaves(0), *leaves(1), *leaves(2), *leaves(3))
```

```python
import functools
import math

import jax
import jax.numpy as jnp
import numpy as np
from jax import lax
from jax.experimental import pallas as pl
from jax.experimental.pallas import tpu as pltpu

F32 = jnp.float32
BF16 = jnp.bfloat16
MESH = pl.DeviceIdType.MESH

D_MODEL = 1024
SEQ = 2048
HEAD_DIM = 64
BLOCK = 128
N_BLOCKS = SEQ // BLOCK
DILATIONS = (1, 4, 16)
N_FOX_HEADS = 8
DIL_WIDTH = 768
DIL_OUT_WIDTH = 256
FOX_WIDTH = 512
D_FF = 2816
ROPE_THETA = 500000.0
ROPE_DIM = HEAD_DIM // 4
ROPE_HALF = ROPE_DIM // 2
EPS = 1e-6
NEG_INF = -1e30
QK_SCALE = 1.0 / math.sqrt(HEAD_DIM)
IN_COLS = 5896
N_DEV = 8
IN_SHARD = IN_COLS // N_DEV

ADAM_LR = 0.001
ADAM_B1 = 0.9
ADAM_B2 = 0.999
ADAM_EPS = 1e-08
ADAM_WD = 0.01
ADAM_STEP = 10

V7X_VMEM_BYTES = 64 * 2**20
LANES = 128
SUBLANES = 8

PROJ_COLS = 6272
COL_GA, COL_GB = 0, 1024
COL_QA, COL_KA, COL_VA = 2304, 3072, 3840
COL_QB, COL_KB, COL_VB = 4608, 5120, 5632
COL_F = 6144
F_ROWS = 16


def _vmem_limit(block_bytes):
    want = 2 * block_bytes + 16 * 2**20
    return int(min(max(want, 32 * 2**20), V7X_VMEM_BYTES - 8 * 2**20))


def _nbytes(shape, dtype):
    return math.prod(shape) * jnp.dtype(dtype).itemsize


def _in_hbm(*arrays):
    return [pltpu.with_memory_space_constraint(a, pltpu.HBM) for a in arrays]


def _dot(a, b, dims):
    return lax.dot_general(a, b, (dims, ((), ())), preferred_element_type=F32)


def _dot_nn(a, b):
    return _dot(a, b, ((1,), (0,)))


def _dot_nt(a, b):
    return _dot(a, b, ((1,), (1,)))


def _dot_tn(a, b):
    return _dot(a, b, ((0,), (0,)))


def _sigmoid(z):
    return 1.0 / (1.0 + jnp.exp(-z))


def _split3(x):
    hi = x.astype(BF16)
    r1 = x - hi.astype(F32)
    mid = r1.astype(BF16)
    lo = (r1 - mid.astype(F32)).astype(BF16)
    return hi, mid, lo


def _dot3_nn(x, ones_matrix):
    hi, mid, lo = _split3(x)
    return (_dot_nn(hi, ones_matrix) + _dot_nn(mid, ones_matrix)) + _dot_nn(lo, ones_matrix)


def _rowwise(fn, name, n_rows, tm, row_ins, bcast_ins, row_outs, acc_outs=(), deps=()):
    n_in = len(row_ins) + len(bcast_ins)
    n_ro = len(row_outs)

    def body(*refs):
        res = fn(*[r[...] for r in refs[:n_in]])
        if not isinstance(res, (tuple, list)):
            res = (res,)
        outs = refs[n_in + len(deps):]
        for r, o in zip(res[:n_ro], outs[:n_ro]):
            o[...] = r.astype(o.dtype)
        first = pl.program_id(0) == 0
        for r, o in zip(res[n_ro:], outs[n_ro:]):
            _accumulate(o, r, first)

    in_specs = [pl.BlockSpec((tm, w), lambda i, cb=cb: (i, cb)) for _, w, cb in row_ins]
    in_specs += [pl.BlockSpec(a.shape, lambda i: (0, 0)) for a in bcast_ins]
    in_specs += [pl.BlockSpec(memory_space=pl.ANY)] * len(deps)
    out_specs = [pl.BlockSpec((tm, w), lambda i: (i, 0)) for w, _ in row_outs]
    out_specs += [pl.BlockSpec((1, w), lambda i: (0, 0)) for w in acc_outs]
    out_shape = [pltpu.HBM((n_rows, w), dt) for w, dt in row_outs]
    out_shape += [pltpu.HBM((1, w), F32) for w in acc_outs]
    blk = sum(_nbytes((tm, w), a.dtype) for a, w, _ in row_ins) + sum(_nbytes((tm, w), dt) for w, dt in row_outs)
    return pl.pallas_call(
        body, name=name, grid=(n_rows // tm,), in_specs=in_specs, out_specs=out_specs, out_shape=out_shape,
        compiler_params=pltpu.CompilerParams(
            dimension_semantics=("arbitrary" if acc_outs else "parallel",), vmem_limit_bytes=_vmem_limit(3 * blk)),
    )(*_in_hbm(*[a for a, _, _ in row_ins], *bcast_ins), *deps)


def _accumulate(o_ref, part, first):
    @pl.when(first)
    def _():
        o_ref[...] = part

    @pl.when(jnp.logical_not(first))
    def _():
        o_ref[...] += part


_MM_DIMS = {"nn": ((1,), (0,)), "nt": ((1,), (1,)), "tn": ((0,), (0,))}


def _matmul(pairs, mode, out_dtype, name, tm, tn, tk, deps=(), staged=True):
    a0, b0 = pairs[0]
    if mode == "tn":
        kk, m = a0.shape
    else:
        m, kk = a0.shape
    n = b0.shape[0] if mode == "nt" else b0.shape[1]
    assert m % tm == 0 and n % tn == 0 and kk % tk == 0, (name, m, n, kk)
    nk = kk // tk
    n_pairs = len(pairs)
    dims = _MM_DIMS[mode]
    n_in = 2 * n_pairs + len(deps)

    def body(*refs):
        o_ref = refs[n_in]
        part = None
        for p in range(n_pairs):
            d = _dot(refs[2 * p][...].astype(BF16), refs[2 * p + 1][...].astype(BF16), dims)
            part = d if part is None else part + d
        if nk == 1:
            o_ref[...] = part.astype(o_ref.dtype)
            return
        acc = refs[n_in + 1]
        k = pl.program_id(2)

        @pl.when(k == 0)
        def _():
            acc[...] = part

        @pl.when(k > 0)
        def _():
            acc[...] += part

        @pl.when(k == nk - 1)
        def _():
            o_ref[...] = acc[...].astype(o_ref.dtype)

    if mode == "tn":
        a_spec = pl.BlockSpec((tk, tm), lambda i, j, k: (k, i))
    else:
        a_spec = pl.BlockSpec((tm, tk), lambda i, j, k: (i, k))
    if mode == "nt":
        b_spec = pl.BlockSpec((tn, tk), lambda i, j, k: (j, k))
    else:
        b_spec = pl.BlockSpec((tk, tn), lambda i, j, k: (k, j))
    blk = sum(_nbytes((tm, tk), a.dtype) + _nbytes((tk, tn), b.dtype) for a, b in pairs) + 2 * _nbytes((tm, tn), F32)
    flat = [a for pair in pairs for a in pair]
    return pl.pallas_call(
        body, name=name, grid=(m // tm, n // tn, nk),
        in_specs=[a_spec, b_spec] * n_pairs + [pl.BlockSpec(memory_space=pl.ANY)] * len(deps),
        out_specs=pl.BlockSpec((tm, tn), lambda i, j, k: (i, j)),
        out_shape=pltpu.HBM((m, n), out_dtype),
        scratch_shapes=[] if nk == 1 else [pltpu.VMEM((tm, tn), F32)],
        compiler_params=pltpu.CompilerParams(
            dimension_semantics=("parallel", "parallel", "arbitrary"), vmem_limit_bytes=_vmem_limit(blk)),
    )(*(flat if staged else _in_hbm(*flat)), *deps)


def _matmul_rowwise(pairs, fn, name, tm, row_ins, bcast_ins, row_outs, acc_outs=(), deps=()):
    m = pairs[0][0].shape[0]
    n_mm, n_in = 2 * len(pairs), len(row_ins) + len(bcast_ins)
    n_ro = len(row_outs)

    def body(*refs):
        prod = None
        for p in range(len(pairs)):
            part = _dot_nn(refs[2 * p][...].astype(BF16), refs[2 * p + 1][...].astype(BF16))
            prod = part if prod is None else prod + part
        res = fn(prod, *[r[...] for r in refs[n_mm:n_mm + n_in]])
        outs = refs[n_mm + n_in + len(deps):]
        for r, o in zip(res[:n_ro], outs[:n_ro]):
            o[...] = r.astype(o.dtype)
        first = pl.program_id(0) == 0
        for r, o in zip(res[n_ro:], outs[n_ro:]):
            _accumulate(o, r, first)

    in_specs = []
    for a, b in pairs:
        in_specs += [pl.BlockSpec((tm, a.shape[1]), lambda i: (i, 0)),
                     pl.BlockSpec(b.shape, lambda i: (0, 0), pipeline_mode=pl.Buffered(1))]
    in_specs += [pl.BlockSpec((tm, w), lambda i, cb=cb: (i, cb)) for _, w, cb in row_ins]
    in_specs += [pl.BlockSpec(a.shape, lambda i: (0, 0)) for a in bcast_ins]
    in_specs += [_ANY] * len(deps)
    out_specs = [pl.BlockSpec((tm, w), lambda i: (i, 0)) for w, _ in row_outs]
    out_specs += [pl.BlockSpec((1, w), lambda i: (0, 0)) for w in acc_outs]
    out_shape = [pltpu.HBM((m, w), dt) for w, dt in row_outs]
    out_shape += [pltpu.HBM((1, w), F32) for w in acc_outs]
    blk = sum(_nbytes((tm, a.shape[1]), a.dtype) + _nbytes(b.shape, b.dtype) // 2 for a, b in pairs)
    blk += sum(_nbytes((tm, w), a.dtype) for a, w, _ in row_ins) + sum(_nbytes((tm, w), dt) for w, dt in row_outs)
    return pl.pallas_call(
        body, name=name, grid=(m // tm,), in_specs=in_specs, out_specs=out_specs, out_shape=out_shape,
        compiler_params=pltpu.CompilerParams(dimension_semantics=("arbitrary",), vmem_limit_bytes=_vmem_limit(blk)),
    )(*[a for pair in pairs for a in pair], *[a for a, _, _ in row_ins], *bcast_ins, *deps)


def _rms_scale(x):
    return lax.rsqrt(jnp.mean(x * x, axis=-1, keepdims=True) + EPS)


def _rms_bwd(xin, dyn, g):
    r = _rms_scale(xin)
    u = dyn * g
    dx = r * u - xin * (r * r * r) * jnp.mean(u * xin, axis=-1, keepdims=True)
    dg = jnp.sum(dyn * xin * r, axis=0, keepdims=True)
    return dx, dg


def _mesh_pos():
    return lax.axis_index("x"), lax.axis_index("y"), lax.axis_index("c")


def _all_gather(xs, name, deps=()):
    n = len(xs)

    def body(*refs):
        x_refs, out_refs = refs[:n], refs[n + len(deps):2 * n + len(deps)]
        send_sems, recv_sems, local_sems = refs[2 * n + len(deps):]
        mx, my, mc = _mesh_pos()
        me, sib = (mx, my, mc), (mx, my, 1 - mc)
        chips = [(1 - mx, my), (mx, 1 - my), (1 - mx, 1 - my)]

        def slot(a, dev):
            px, py, pc = dev
            return out_refs[a].at[4 * px + 2 * py + pc]

        def copy(k, a, block, to, src=None):
            return pltpu.make_async_remote_copy(
                src_ref=slot(a, block) if src is None else src, dst_ref=slot(a, block),
                send_sem=send_sems.at[a * 7 + k], recv_sem=recv_sems.at[a * 7 + k],
                device_id=to, device_id_type=MESH)

        mine = [pltpu.make_async_copy(x_refs[a], slot(a, me), local_sems.at[a]) for a in range(n)]
        for cp in mine:
            cp.start()
        first = []
        for a in range(n):
            first.append(copy(0, a, me, sib, x_refs[a]))
            first += [copy(1 + j, a, me, (*chip, mc), x_refs[a]) for j, chip in enumerate(chips)]
        for cp in first:
            cp.start()
        passed = []
        for a in range(n):
            for j, chip in enumerate(chips):
                copy(1 + j, a, (*chip, mc), me).wait_recv()
                fwd = copy(4 + j, a, (*chip, mc), sib)
                fwd.start()
                passed.append(fwd)
        for a in range(n):
            copy(0, a, sib, me).wait_recv()
            for j, chip in enumerate(chips):
                copy(4 + j, a, (*chip, 1 - mc), me).wait_recv()
        for cp in first + passed:
            cp.wait_send()
        for cp in mine:
            cp.wait()

    hbm = pl.BlockSpec(memory_space=pl.ANY)
    return pl.pallas_call(
        body, name=name,
        out_shape=[pltpu.HBM((N_DEV,) + x.shape, x.dtype) for x in xs],
        in_specs=[hbm] * (n + len(deps)), out_specs=[hbm] * n,
        scratch_shapes=[pltpu.SemaphoreType.DMA((7 * n,)), pltpu.SemaphoreType.DMA((7 * n,)),
                        pltpu.SemaphoreType.DMA((n,))],
    )(*xs, *deps)


_HBM = pl.BlockSpec(memory_space=pltpu.HBM)
_SEM = pl.BlockSpec(memory_space=pltpu.SEMAPHORE)
_ANY = pl.BlockSpec(memory_space=pl.ANY)
_DATAFLOW = pltpu.SideEffectType.DATAFLOW_SIDE_EFFECTING


def _flip_peer(flip):
    mx, my, mc = _mesh_pos()
    return (1 - mx if flip & 2 else mx, 1 - my if flip & 1 else my, mc)


def _remote(src, dst, send_sems, recv_sems, k, peer):
    return pltpu.make_async_remote_copy(src_ref=src, dst_ref=dst, send_sem=send_sems.at[k], recv_sem=recv_sems.at[k],
                                        device_id=peer, device_id_type=MESH)


def _scatter_sibling_copies(srcs, lands, send_sems, recv_sems):
    mx, my, mc = _mesh_pos()
    return [_remote(srcs[a].at[k, 1 - mc], lands[a].at[k], send_sems, recv_sems, 4 * a + k, (mx, my, 1 - mc))
            for a in range(len(srcs)) for k in range(4)]


def _scatter_chips_copies(srcs, lands, send_sems, recv_sems):
    mx, my, _ = _mesh_pos()
    k0 = 2 * mx + my
    return [_remote(srcs[a].at[jnp.bitwise_xor(k0, flip)], lands[a].at[flip - 1], send_sems, recv_sems,
                    3 * a + flip - 1, _flip_peer(flip))
            for a in range(len(srcs)) for flip in (1, 2, 3)]


class _Exchange:
    def __init__(self, copies, n_src, send_sems, recv_sems, thru, token):
        self.copies, self.n_src, self.send_sems, self.recv_sems, self.thru, self.token = (
            copies, n_src, send_sems, recv_sems, thru, token)


def _exchange_start(name, copies, srcs, lands, n_copies, after=()):
    bufs = list(srcs) + list(lands)
    nb, ns = len(bufs), len(srcs)

    def body(*refs):
        send_sems, recv_sems = refs[nb + len(after)], refs[nb + len(after) + 1]
        for cp in copies(refs[:ns], refs[ns:nb], send_sems, recv_sems):
            cp.start()
        refs[-1][...] = jnp.zeros_like(refs[-1])

    out = pl.pallas_call(
        body, name=name,
        out_shape=(pltpu.SemaphoreType.DMA((n_copies,)), pltpu.SemaphoreType.DMA((n_copies,)),
                   *[pltpu.HBM(b.shape, b.dtype) for b in bufs], pltpu.HBM((SUBLANES, LANES), F32)),
        in_specs=[_HBM] * nb + [_ANY] * len(after),
        out_specs=(_SEM, _SEM, *[_HBM] * nb, pl.BlockSpec(memory_space=pltpu.VMEM)),
        input_output_aliases={i: 2 + i for i in range(nb)},
        compiler_params=pltpu.CompilerParams(has_side_effects=_DATAFLOW),
    )(*[pltpu.with_memory_space_constraint(b, pltpu.HBM) for b in bufs], *after)
    return _Exchange(copies, ns, out[0], out[1], list(out[2:2 + nb]), out[-1])


def _exchange_wait(name, ex, after):
    nb, ns = len(ex.thru), ex.n_src

    def body(*refs):
        for cp in ex.copies(refs[:ns], refs[ns:nb], refs[nb], refs[nb + 1]):
            cp.wait_send()
            cp.wait_recv()

    out = pl.pallas_call(
        body, name=name, out_shape=tuple(pltpu.HBM(b.shape, b.dtype) for b in ex.thru),
        in_specs=[_HBM] * nb + [_SEM, _SEM] + [_ANY] * len(after), out_specs=tuple([_HBM] * nb),
        input_output_aliases={i: i for i in range(nb)},
        compiler_params=pltpu.CompilerParams(has_side_effects=_DATAFLOW),
    )(*ex.thru, ex.send_sems, ex.recv_sems, *after)
    return list(out[:ns]), list(out[ns:])


def _halves(ref):
    half = ref.shape[1] // 2
    if half % LANES == 0:
        return ref.at[:, pl.ds(0, half)], ref.at[:, pl.ds(half, half)]
    half = ref.shape[0] // 2
    assert half % (2 * SUBLANES) == 0, ref.shape
    return ref.at[pl.ds(0, half)], ref.at[pl.ds(half, half)]


def _gather_two_route_copies(srcs, lands, send_sems, recv_sems):
    mx, my, mc = _mesh_pos()
    me = 4 * mx + 2 * my + mc
    return [_remote(_halves(srcs[a])[h], _halves(lands[a].at[me])[h], send_sems, recv_sems, 4 * a + i, _flip_peer(flip))
            for a in range(len(srcs)) for i, (flip, h) in enumerate(((2, 0), (1, 1), (2, 1), (1, 0)))]


def _to_sibling(land, flip, h, send_sems, recv_sems, k):
    mx, my, mc = _mesh_pos()
    part = _halves(land.at[2 * jnp.bitwise_xor(2 * mx + my, flip) + mc])[h]
    return _remote(part, part, send_sems, recv_sems, k, (mx, my, 1 - mc))


def _second_hop(land, send_sems, recv_sems, k):
    mx, my, mc = _mesh_pos()
    k0 = 2 * mx + my
    from_y = _halves(land.at[2 * jnp.bitwise_xor(k0, 1) + mc])[1]
    from_x = _halves(land.at[2 * jnp.bitwise_xor(k0, 2) + mc])[0]
    return (_remote(from_y, from_y, send_sems, recv_sems, k, _flip_peer(2)),
            _remote(from_x, from_x, send_sems, recv_sems, k + 1, _flip_peer(1)))


def _relay_call(name, body, bufs, sems, n_new, after):
    nb, n_in = len(bufs), len(bufs) + len(sems) + len(after)

    def call_body(*refs):
        body(refs[:nb], refs[nb:nb + len(sems)], refs[n_in], refs[n_in + 1])
        refs[-1][...] = jnp.zeros_like(refs[-1])

    out = pl.pallas_call(
        call_body, name=name,
        out_shape=(pltpu.SemaphoreType.DMA((n_new,)), pltpu.SemaphoreType.DMA((n_new,)),
                   *[pltpu.HBM(b.shape, b.dtype) for b in bufs], pltpu.HBM((SUBLANES, LANES), F32)),
        in_specs=[_HBM] * nb + [_SEM] * len(sems) + [_ANY] * len(after),
        out_specs=(_SEM, _SEM, *[_HBM] * nb, pl.BlockSpec(memory_space=pltpu.VMEM)),
        input_output_aliases={i: 2 + i for i in range(nb)},
        compiler_params=pltpu.CompilerParams(has_side_effects=_DATAFLOW),
    )(*bufs, *sems, *after)
    return out[0], out[1], list(out[2:2 + nb]), out[-1]


def _gather_relay(ex, tag, after_first):
    ns = ex.n_src
    n = len(ex.thru) - ns

    def first(bufs, sems, send, recv):
        lands, first_hop = bufs[ns:], ex.copies(bufs[:ns], bufs[ns:], *sems)
        for a in range(n):
            for h in (0, 1):
                _to_sibling(lands[a], 0, h, send, recv, 10 * a + h).start()
        for a in range(n):
            x_first, y_second, x_second, y_first = first_hop[4 * a:4 * a + 4]
            to_x, to_y = _second_hop(lands[a], send, recv, 10 * a + 8)
            x_first.wait_recv()
            to_y.start()
            _to_sibling(lands[a], 2, 0, send, recv, 10 * a + 4).start()
            y_second.wait_recv()
            to_x.start()
            _to_sibling(lands[a], 1, 1, send, recv, 10 * a + 3).start()
            x_second.wait_recv()
            _to_sibling(lands[a], 2, 1, send, recv, 10 * a + 5).start()
            y_first.wait_recv()
            _to_sibling(lands[a], 1, 0, send, recv, 10 * a + 2).start()
        for cp in first_hop:
            cp.wait_send()

    def second(lands, sems, send, recv):
        for a in range(n):
            to_x, to_y = _second_hop(lands[a], *sems, 10 * a + 8)
            to_y.wait_recv()
            _to_sibling(lands[a], 3, 0, send, recv, 2 * a).start()
            to_x.wait_recv()
            _to_sibling(lands[a], 3, 1, send, recv, 2 * a + 1).start()
            to_x.wait_send()
            to_y.wait_send()

    def last(lands, sems, send, recv):
        for a in range(n):
            for flip in range(4):
                for h in (0, 1):
                    s, r, k = (sems[2], sems[3], 2 * a + h) if flip == 3 else (sems[0], sems[1], 10 * a + 2 * flip + h)
                    cp = _to_sibling(lands[a], flip, h, s, r, k)
                    cp.wait_send()
                    cp.wait_recv()

    send1, recv1, bufs, token = _relay_call(f"{tag}_chips_relay", first, ex.thru, [ex.send_sems, ex.recv_sems], 10 * n,
                                            after_first)

    def run_second(after):
        send2, recv2, lands, token = _relay_call(f"{tag}_diagonal_relay", second, bufs[ns:], [send1, recv1], 2 * n, after)
        return token, lambda after_last: _relay_call(f"{tag}_sibling_wait", last, lands, [send1, recv1, send2, recv2],
                                                     1, after_last)[2]

    return token, run_second


def _col_tile(r, c, block_bytes=2**20):
    return next(t for t in (1024, 512, 256, 128) if c % t == 0 and (r * t * 4 <= block_bytes or t == 128))


def _add_sibling(g4, recv, pos, name):
    _, _, r, c = g4.shape
    tc = _col_tile(r, c, 2**21)

    def body(pos_ref, g_ref, r_ref, o16_ref, mine_ref):
        s = g_ref[0, 0] + r_ref[0]
        o16_ref[0] = s.astype(BF16)

        @pl.when(pl.program_id(1) == pos_ref[1])
        def _():
            mine_ref[...] = s

    slot = pl.BlockSpec((1, r, tc), lambda j, k, pos_ref: (k, 0, j))
    return pl.pallas_call(
        body, name=name,
        out_shape=[pltpu.HBM((4, r, c), BF16), pltpu.HBM((r, c), F32)],
        grid_spec=pltpu.PrefetchScalarGridSpec(
            num_scalar_prefetch=1, grid=(c // tc, 4),
            in_specs=[pl.BlockSpec((1, 1, r, tc), lambda j, k, pos_ref: (k, pos_ref[0], 0, j)), slot],
            out_specs=[slot, pl.BlockSpec((r, tc), lambda j, k, pos_ref: (0, j))]),
        compiler_params=pltpu.CompilerParams(
            dimension_semantics=("parallel", "arbitrary"), vmem_limit_bytes=_vmem_limit(4 * _nbytes((r, tc), F32))),
    )(pos, *_in_hbm(g4, recv))


class _ReduceScatter:
    def __init__(self, tag, grads_t, pos):
        self.tag, self.pos, self.names = tag, pos, list(grads_t)
        g4s = [g.reshape(4, 2, g.size // (N_DEV * g.shape[-1]), g.shape[-1]) for g in grads_t.values()]
        lands = [lax.empty((4,) + g.shape[2:], F32) for g in g4s]
        self.ex = _exchange_start(f"rs_{tag}_sibling_start", _scatter_sibling_copies, g4s, lands, 4 * len(g4s))
        self.token = self.ex.token

    def start_chips(self, after):
        g4s, from_sibling = _exchange_wait(f"rs_{self.tag}_sibling_wait", self.ex, after)
        parts = [_add_sibling(g4, rv, self.pos, f"rs_add_sibling_{k}")
                 for k, g4, rv in zip(self.names, g4s, from_sibling)]
        self.mine = [mine for _, mine in parts]
        p16s = [p16 for p16, _ in parts]
        lands = [lax.empty((3,) + p.shape[1:], BF16) for p in p16s]
        self.ex = _exchange_start(f"rs_{self.tag}_chips_start", _scatter_chips_copies, p16s, lands, 3 * len(p16s))
        self.token = self.ex.token

    def finish(self, after):
        _, from_chips = _exchange_wait(f"rs_{self.tag}_chips_wait", self.ex, after)
        return dict(zip(self.names, zip(self.mine, from_chips)))


def _rope_tables():
    positions = np.arange(SEQ, dtype=np.float32)
    inv_freq = np.power(np.float32(ROPE_THETA), -np.arange(0, ROPE_DIM, 2, dtype=np.float32) / np.float32(ROPE_DIM))
    ang = (positions[:, None] * inv_freq[None, :]).astype(np.float32)
    cos, sin = np.cos(ang).astype(np.float32), np.sin(ang).astype(np.float32)
    ones = np.ones((SEQ, HEAD_DIM - ROPE_DIM), np.float32)
    zeros8 = np.zeros((SEQ, ROPE_HALF), np.float32)
    zeros = np.zeros((SEQ, HEAD_DIM - ROPE_DIM), np.float32)
    c_head = np.concatenate([cos, cos, ones], axis=1)
    s1_head = np.concatenate([-sin, zeros8, zeros], axis=1)
    s2_head = np.concatenate([zeros8, sin, zeros], axis=1)
    return tuple(jnp.asarray(np.concatenate([t, t], axis=1)) for t in (c_head, s1_head, s2_head))


def _rope_apply(x, c, s1, s2):
    w = x.shape[1]
    return x * c + pltpu.roll(x, w - ROPE_HALF, 1) * s1 + pltpu.roll(x, ROPE_HALF, 1) * s2


def _rope_apply_t(dy, c, s1, s2):
    w = dy.shape[1]
    return dy * c + pltpu.roll(dy * s1, ROPE_HALF, 1) + pltpu.roll(dy * s2, w - ROPE_HALF, 1)


def _dil_prev_limit(has_prev):
    return jnp.where(has_prev, 0, BLOCK)


def _dil_valid(limit):
    row = lax.broadcasted_iota(jnp.int32, (BLOCK, 2 * BLOCK), 0)
    col = lax.broadcasted_iota(jnp.int32, (BLOCK, 2 * BLOCK), 1)
    dist = col - row
    return jnp.logical_and(dist >= jnp.where(col < BLOCK, limit, -BLOCK), dist <= BLOCK)


def _upper_half():
    return lax.broadcasted_iota(jnp.int32, (1, LANES), 1) >= HEAD_DIM


def _stack_heads(x):
    upper = _upper_half()
    return jnp.concatenate([jnp.where(upper, 0, x), jnp.where(upper, x, 0)], axis=0)


def _unstack_heads(y):
    n = y.shape[0] // 2
    return jnp.where(_upper_half(), y[n:], y[:n])


def _head_columns(t):
    return jnp.concatenate([t[:, 0:1], t[:, HEAD_DIM:HEAD_DIM + 1]], axis=0)


def _dil_rows(n, d):
    per = N_BLOCKS // d
    r, lb = n // per, n % per

    def rows(b):
        start = b * (BLOCK * d) + r
        return pl.ds(pl.multiple_of(start, BLOCK), BLOCK) if d == 1 else pl.ds(start, BLOCK, stride=d)

    return rows(lb), rows(jnp.maximum(lb - 1, 0)), lb > 0


def _dil_rotate(q_ref, k_ref, c_ref, s1_ref, s2_ref, q_rot, k_rot):
    tabs = (c_ref[...], s1_ref[...], s2_ref[...])
    q_rot[...] = _rope_apply(q_ref[...], *tabs) * QK_SCALE
    k_rot[...] = _rope_apply(k_ref[...], *tabs)


def _dil_specs():
    def col(base):
        return pl.BlockSpec((SEQ, LANES), lambda p: (0, base // LANES + p))

    table = pl.BlockSpec((SEQ, LANES), lambda p: (0, 0))
    return [col(COL_QA), col(COL_KA), col(COL_VA)], [table] * 3


def _store_columns(blocks, dproj_ref, cols, sem):
    copies = [pltpu.make_async_copy(b, dproj_ref.at[:, pl.ds(pl.multiple_of(c * LANES, LANES), LANES)], sem.at[i])
              for i, (b, c) in enumerate(zip(blocks, cols))]
    for cp in copies:
        cp.start()
    for cp in copies:
        cp.wait()


def _dil_window(d, n, k_rot, v_ref):
    rows, prev, has_prev = _dil_rows(n, d)
    kw, vw = k_rot[rows, :].astype(BF16), v_ref[rows, :].astype(BF16)
    if d == N_BLOCKS:
        row = lax.broadcasted_iota(jnp.int32, (BLOCK, BLOCK), 0)
        valid = lax.broadcasted_iota(jnp.int32, (BLOCK, BLOCK), 1) <= row
    else:
        kw = jnp.concatenate([k_rot[prev, :].astype(BF16), kw], axis=0)
        vw = jnp.concatenate([v_ref[prev, :].astype(BF16), vw], axis=0)
        valid = _dil_valid(_dil_prev_limit(has_prev))
    return rows, prev, kw, vw, jnp.concatenate([valid, valid], axis=0)


def _dil_fwd(proj, tables, deps=()):
    def body(q_ref, k_ref, v_ref, c_ref, s1_ref, s2_ref, *rest):
        o_ref, lse_ref, q_rot, k_rot = rest[len(deps):]
        upper = _upper_half()
        _dil_rotate(q_ref, k_ref, c_ref, s1_ref, s2_ref, q_rot, k_rot)

        def blocks_of(d):
            def block(n, carry):
                rows, _, kw, vw, valid = _dil_window(d, n, k_rot, v_ref)
                s = jnp.where(valid, _dot_nt(_stack_heads(q_rot[rows, :].astype(BF16)), kw), NEG_INF)
                m = jnp.max(s, axis=-1, keepdims=True)
                p = jnp.exp(s - m)
                den = jnp.sum(p, axis=-1, keepdims=True)
                o_ref[rows, :] = _unstack_heads(_dot_nn((p * (1.0 / den)).astype(BF16), vw))
                lse = m + jnp.log(den)
                lse_ref[rows, :] = jnp.where(upper, lse[BLOCK:], lse[:BLOCK])
                return carry

            lax.fori_loop(0, N_BLOCKS, block, 0, unroll=4)

        for g, d in enumerate(DILATIONS):
            pl.when(pl.program_id(0) // 2 == g)(functools.partial(blocks_of, d))

    qkv, tabs = _dil_specs()
    out = pl.BlockSpec((SEQ, LANES), lambda p: (0, p))
    return pl.pallas_call(
        body, name="dil_attn_fwd", grid=(DIL_WIDTH // LANES,), in_specs=qkv + tabs + [_ANY] * len(deps),
        out_specs=[out, out],
        out_shape=[pltpu.HBM((SEQ, DIL_WIDTH), F32)] * 2,
        scratch_shapes=[pltpu.VMEM((SEQ, LANES), F32)] * 2,
        compiler_params=pltpu.CompilerParams(dimension_semantics=("parallel",)),
    )(*_in_hbm(proj, proj, proj, *tables), *deps)


def _dil_bwd(proj, tables, do, lse, c, dproj, deps=()):
    def body(q_ref, k_ref, v_ref, c_ref, s1_ref, s2_ref, do_ref, lse_ref, cc_ref, dproj_in, *rest):
        dproj_ref, dq_acc, dk_acc, dv_acc, dq_out, dk_out, dv_out, q_rot, k_rot, sem = rest[len(deps):]
        dk_acc[...] = jnp.zeros_like(dk_acc)
        dv_acc[...] = jnp.zeros_like(dv_acc)
        _dil_rotate(q_ref, k_ref, c_ref, s1_ref, s2_ref, q_rot, k_rot)

        def blocks_of(d):
            def block(n, carry):
                rows, prev, kw, vw, valid = _dil_window(d, n, k_rot, v_ref)
                q2 = _stack_heads(q_rot[rows, :].astype(BF16))
                do2 = _stack_heads(do_ref[rows, :].astype(BF16))
                lse_col, c_col = _head_columns(lse_ref[rows, :]), _head_columns(cc_ref[rows, :])
                p = jnp.where(valid, jnp.exp(_dot_nt(q2, kw) - lse_col), 0.0)
                ds = (p * (_dot_nt(do2, vw) + c_col)).astype(BF16)
                dk, dv = _dot_tn(ds, q2), _dot_tn(p.astype(BF16), do2)
                dq_acc[rows, :] = _unstack_heads(_dot_nn(ds, kw)) * QK_SCALE
                if d == N_BLOCKS:
                    dk_acc[rows, :] += dk
                    dv_acc[rows, :] += dv
                else:
                    dk_acc[prev, :] += dk[:BLOCK]
                    dv_acc[prev, :] += dv[:BLOCK]
                    dk_acc[rows, :] += dk[BLOCK:]
                    dv_acc[rows, :] += dv[BLOCK:]
                return carry

            lax.fori_loop(0, N_BLOCKS, block, 0, unroll=4)

        pair = pl.program_id(0)
        for g, d in enumerate(DILATIONS):
            pl.when(pair // 2 == g)(functools.partial(blocks_of, d))
        tabs = (c_ref[...], s1_ref[...], s2_ref[...])
        dq_out[...] = _rope_apply_t(dq_acc[...], *tabs).astype(BF16)
        dk_out[...] = _rope_apply_t(dk_acc[...], *tabs).astype(BF16)
        dv_out[...] = dv_acc[...].astype(BF16)
        _store_columns((dq_out, dk_out, dv_out), dproj_ref,
                       [base // LANES + pair for base in (COL_QA, COL_KA, COL_VA)], sem)

    qkv, tabs = _dil_specs()
    tok = pl.BlockSpec((SEQ, LANES), lambda p: (0, p))
    return pl.pallas_call(
        body, name="dil_attn_bwd", grid=(DIL_WIDTH // LANES,),
        in_specs=qkv + tabs + [tok, tok, tok, _ANY] + [_ANY] * len(deps), out_specs=_ANY,
        out_shape=pltpu.HBM(dproj.shape, dproj.dtype),
        scratch_shapes=[pltpu.VMEM((SEQ, LANES), F32)] * 3 + [pltpu.VMEM((SEQ, LANES), BF16)] * 3
        + [pltpu.VMEM((SEQ, LANES), F32)] * 2 + [pltpu.SemaphoreType.DMA((3,))],
        input_output_aliases={9: 0},
        compiler_params=pltpu.CompilerParams(dimension_semantics=("arbitrary",)),
    )(*_in_hbm(proj, proj, proj, *tables, do, lse, c, dproj), *deps)


def _group_weights(l0, l1, l2):
    m = jnp.maximum(jnp.maximum(l0, l1), l2)
    e0, e1, e2 = jnp.exp(l0 - m), jnp.exp(l1 - m), jnp.exp(l2 - m)
    tot = e0 + e1 + e2
    return e0 / tot, e1 / tot, e2 / tot


def _dil_combine(o, lse, deps=()):
    def fn(o0, o1, o2, l0, l1, l2):
        w0, w1, w2 = _group_weights(l0, l1, l2)
        return w0 * o0 + w1 * o1 + w2 * o2

    w = DIL_OUT_WIDTH
    return _rowwise(fn, "dil_combine", SEQ, 512, [(o, w, g) for g in range(3)] + [(lse, w, g) for g in range(3)], [],
                    [(w, F32)], deps=deps)[0]


def _dil_combine_bwd(d_out, o, lse, deps=()):
    w = DIL_OUT_WIDTH

    def fn(d, o0, o1, o2, l0, l1, l2):
        row = lax.broadcasted_iota(jnp.int32, (w, w), 0) // HEAD_DIM
        col = lax.broadcasted_iota(jnp.int32, (w, w), 1) // HEAD_DIM
        same_head = jnp.where(row == col, 1.0, 0.0).astype(BF16)
        ws = _group_weights(l0, l1, l2)
        dws = [_dot3_nn(d * og, same_head) for og in (o0, o1, o2)]
        mean = ws[0] * dws[0] + ws[1] * dws[1] + ws[2] * dws[2]
        return jnp.concatenate([wg * d for wg in ws], axis=1), jnp.concatenate([-wg * mean for wg in ws], axis=1)

    return _rowwise(fn, "dil_combine_bwd", SEQ, 256,
                    [(d_out, w, 0)] + [(o, w, g) for g in range(3)] + [(lse, w, g) for g in range(3)], [],
                    [(DIL_WIDTH, F32)] * 2, deps=deps)


def _log1p(e):
    u = 1.0 + e
    return jnp.where(u == 1.0, e, jnp.log(u) * (e / (u - 1.0)))


def _fox_gate(proj, b_pad, deps=()):
    def body(f_ref, b_ref, *rest):
        o_ref = rest[-1]
        z = f_ref[...] + b_ref[...]
        logf = (jnp.minimum(z, 0.0) - _log1p(jnp.exp(-jnp.abs(z)))).T[:F_ROWS]
        row = lax.broadcasted_iota(jnp.int32, (BLOCK, BLOCK), 0)
        col = lax.broadcasted_iota(jnp.int32, (BLOCK, BLOCK), 1)
        before = jnp.where(row <= col, 1.0, 0.0).astype(BF16)
        carry = jnp.zeros((F_ROWS, 1), F32)
        for blk in range(N_BLOCKS):
            run = _dot3_nn(logf[:, blk * BLOCK:(blk + 1) * BLOCK], before) + carry
            o_ref[:, blk * BLOCK:(blk + 1) * BLOCK] = run
            carry = run[:, BLOCK - 1:BLOCK]

    return pl.pallas_call(
        body, name="fox_gate", grid=(1,),
        in_specs=[pl.BlockSpec((SEQ, LANES), lambda i: (0, COL_F // LANES)), pl.BlockSpec((1, LANES), lambda i: (0, 0))]
        + [_ANY] * len(deps),
        out_specs=pl.BlockSpec((F_ROWS, SEQ), lambda i: (0, 0)),
        out_shape=pltpu.HBM((F_ROWS, SEQ), F32),
    )(*_in_hbm(proj, b_pad), *deps)


def _fox_gate_bwd(d_cum, proj, b_pad, dproj):
    def body(d_ref, f_ref, b_ref, dproj_ref, dz_ref, db_ref):
        row = lax.broadcasted_iota(jnp.int32, (BLOCK, BLOCK), 0)
        col = lax.broadcasted_iota(jnp.int32, (BLOCK, BLOCK), 1)
        after = jnp.where(row >= col, 1.0, 0.0).astype(BF16)
        carry = jnp.zeros((F_ROWS, 1), F32)
        parts = [None] * N_BLOCKS
        for blk in reversed(range(N_BLOCKS)):
            run = _dot3_nn(d_ref[:, blk * BLOCK:(blk + 1) * BLOCK], after) + carry
            parts[blk] = run
            carry = run[:, 0:1]
        dlogf = jnp.concatenate(parts, axis=1)
        dlogf = jnp.concatenate([dlogf, jnp.zeros((LANES - F_ROWS, SEQ), F32)], axis=0).T
        dz = dlogf * _sigmoid(-(f_ref[...] + b_ref[...]))
        dz_ref[...] = dz.astype(BF16)
        db_ref[...] = jnp.sum(dz, axis=0, keepdims=True)

    f_cols = pl.BlockSpec((SEQ, LANES), lambda i: (0, COL_F // LANES))
    return pl.pallas_call(
        body, name="fox_gate_bwd", grid=(1,),
        in_specs=[pl.BlockSpec((F_ROWS, SEQ), lambda i: (0, 0)), f_cols, pl.BlockSpec((1, LANES), lambda i: (0, 0)), _ANY],
        out_specs=[f_cols, pl.BlockSpec((1, LANES), lambda i: (0, 0))],
        out_shape=[pltpu.HBM(dproj.shape, dproj.dtype), pltpu.HBM((1, LANES), F32)],
        input_output_aliases={3: 0},
    )(*_in_hbm(d_cum, proj, b_pad, dproj))


FOX_TILE = 256
FOX_TILES = SEQ // FOX_TILE


def _row_to_col(row):
    n = row.shape[1]
    eye = lax.broadcasted_iota(jnp.int32, (n, n), 0) == lax.broadcasted_iota(jnp.int32, (n, n), 1)
    return jnp.sum(jnp.where(eye, row, 0.0), axis=1, keepdims=True)


def _fox_bias(f_row, i):
    t = FOX_TILE
    ext = (i + 1) * t
    bias = _row_to_col(f_row[:, i * t:(i + 1) * t]) - f_row[:, :ext]
    row = lax.broadcasted_iota(jnp.int32, (t, ext), 0) + i * t
    col = lax.broadcasted_iota(jnp.int32, (t, ext), 1)
    return bias, col <= row


def _fox_specs():
    qkv = [pl.BlockSpec((SEQ, LANES), lambda p, base=base: (0, base // LANES + p)) for base in (COL_QB, COL_KB, COL_VB)]
    return qkv, pl.BlockSpec((F_ROWS, SEQ), lambda p: (0, 0))


def _fox_fwd(proj, f_rows):
    t = FOX_TILE

    def body(q_ref, k_ref, v_ref, f_ref, o_ref, lse_ref):
        pair = pl.program_id(0)
        upper = _upper_half()
        k16, v16 = k_ref[...].astype(BF16), v_ref[...].astype(BF16)
        f_row = [f_ref[pl.ds(2 * pair + e, 1), :] for e in range(2)]
        for i in range(FOX_TILES):
            ext = (i + 1) * t
            q_tile = (q_ref[i * t:(i + 1) * t, :] * QK_SCALE).astype(BF16)
            s2 = _dot_nt(_stack_heads(q_tile), k16[:ext])
            pns, lses = [], []
            for e in range(2):
                bias, causal = _fox_bias(f_row[e], i)
                s = jnp.where(causal, s2[e * t:(e + 1) * t] + bias, NEG_INF)
                m = jnp.max(s, axis=-1, keepdims=True)
                p = jnp.exp(s - m)
                den = jnp.sum(p, axis=-1, keepdims=True)
                pns.append((p * (1.0 / den)).astype(BF16))
                lses.append(m + jnp.log(den))
            o_ref[i * t:(i + 1) * t, :] = _unstack_heads(_dot_nn(jnp.concatenate(pns, axis=0), v16[:ext]))
            lse_ref[i * t:(i + 1) * t, :] = jnp.where(upper, lses[1], lses[0])

    qkv, f_spec = _fox_specs()
    tok = pl.BlockSpec((SEQ, LANES), lambda p: (0, p))
    return pl.pallas_call(
        body, name="fox_attn_fwd", grid=(FOX_WIDTH // LANES,),
        in_specs=qkv + [f_spec], out_specs=[tok, tok],
        out_shape=[pltpu.HBM((SEQ, FOX_WIDTH), F32)] * 2,
        compiler_params=pltpu.CompilerParams(
            dimension_semantics=("parallel",), vmem_limit_bytes=_vmem_limit(8 * t * SEQ * 4)),
    )(*_in_hbm(proj, proj, proj, f_rows))


def _fox_bwd(proj, do, lse, f_rows, dproj):
    t = FOX_TILE

    def body(q_ref, k_ref, v_ref, f_ref, do_ref, lse_ref, dproj_in, dproj_ref, df_ref, dk_acc, dv_acc,
             dq_out, dk_out, dv_out, sem):
        pair = pl.program_id(0)
        upper = _upper_half()
        k16, v16 = k_ref[...].astype(BF16), v_ref[...].astype(BF16)
        f_row = [f_ref[pl.ds(2 * pair + e, 1), :] for e in range(2)]
        dk_acc[...] = jnp.zeros_like(dk_acc)
        dv_acc[...] = jnp.zeros_like(dv_acc)
        df_ref[...] = jnp.zeros_like(df_ref)
        for i in range(FOX_TILES):
            ext = (i + 1) * t
            q_tile = (q_ref[i * t:(i + 1) * t, :] * QK_SCALE).astype(BF16)
            do_tile = do_ref[i * t:(i + 1) * t, :]
            lse_t = lse_ref[i * t:(i + 1) * t, :]
            q2, do2 = _stack_heads(q_tile), _stack_heads(do_tile)
            s2, dp2 = _dot_nt(q2, k16[:ext]), _dot_nt(do2, v16[:ext])
            ps, dss = [], []
            for e in range(2):
                bias, causal = _fox_bias(f_row[e], i)
                s = s2[e * t:(e + 1) * t] + bias
                p = jnp.where(causal, jnp.exp(s - lse_t[:, e * HEAD_DIM:e * HEAD_DIM + 1]), 0.0)
                dp = dp2[e * t:(e + 1) * t]
                ds = p * (dp - jnp.sum(p * dp, axis=-1, keepdims=True))
                df_ref[0, e:e + 1, :ext] -= jnp.sum(ds, axis=0, keepdims=True)
                ps.append(p.astype(BF16))
                dss.append(ds.astype(BF16))
            ds2, p2 = jnp.concatenate(dss, axis=0), jnp.concatenate(ps, axis=0)
            dq_out[i * t:(i + 1) * t, :] = (_unstack_heads(_dot_nn(ds2, k16[:ext])) * QK_SCALE).astype(BF16)
            dk_acc[:ext, :] += _dot_tn(ds2, q2)
            dv_acc[:ext, :] += _dot_tn(p2, do2)
        dk_out[...] = dk_acc[...].astype(BF16)
        dv_out[...] = dv_acc[...].astype(BF16)
        _store_columns((dq_out, dk_out, dv_out), dproj_ref, [base // LANES + pair for base in (COL_QB, COL_KB, COL_VB)],
                       sem)

    qkv, f_spec = _fox_specs()
    tok = pl.BlockSpec((SEQ, LANES), lambda p: (0, p))
    return pl.pallas_call(
        body, name="fox_attn_bwd", grid=(FOX_WIDTH // LANES,),
        in_specs=qkv + [f_spec, tok, tok, _ANY],
        out_specs=[_ANY, pl.BlockSpec((1, SUBLANES, SEQ), lambda p: (p, 0, 0))],
        out_shape=[pltpu.HBM(dproj.shape, dproj.dtype),
                   pltpu.HBM((FOX_WIDTH // LANES, SUBLANES, SEQ), F32)],
        scratch_shapes=[pltpu.VMEM((SEQ, LANES), F32)] * 2 + [pltpu.VMEM((SEQ, LANES), BF16)] * 3
        + [pltpu.SemaphoreType.DMA((3,))],
        input_output_aliases={6: 0},
        compiler_params=pltpu.CompilerParams(
            dimension_semantics=("arbitrary",), vmem_limit_bytes=_vmem_limit(10 * t * SEQ * 4)),
    )(*_in_hbm(proj, proj, proj, f_rows, do, lse, dproj))


MIX_TILE = 256


def _mix_out(out_a, out_b, proj, x, wt_pa, wt_pb, w_out, g_post, g_ffn_pre):
    tm = MIX_TILE

    def body(a_ref, b_ref, ga_ref, gb_ref, x_ref, wpa_ref, wpb_ref, wo_ref, g2_ref, g3_ref,
             merged_ref, mix_ref, x1_ref, h2_ref):
        ya = _dot_nn(a_ref[...].astype(BF16), wpa_ref[...])
        yb = _dot_nn(b_ref[...].astype(BF16), wpb_ref[...])
        merged = (_sigmoid(ga_ref[...]) * ya + _sigmoid(gb_ref[...]) * yb).astype(BF16)
        merged_ref[...] = merged
        mix = _dot_nn(merged, wo_ref[...])
        mix_ref[...] = mix
        x1 = x_ref[...] + mix * _rms_scale(mix) * g2_ref[...]
        x1_ref[...] = x1
        h2_ref[...] = (x1 * _rms_scale(x1) * g3_ref[...]).astype(BF16)

    def rows(w, cb=0):
        return pl.BlockSpec((tm, w), lambda i, cb=cb: (i, cb))

    def whole(a):
        return pl.BlockSpec(a.shape, lambda i: (0, 0))

    d = D_MODEL
    blk = _nbytes((tm, d), F32) * 6 + sum(_nbytes(a.shape, BF16) for a in (wt_pa, wt_pb, w_out))
    return pl.pallas_call(
        body, name="mix_out", grid=(SEQ // tm,),
        in_specs=[rows(DIL_OUT_WIDTH), rows(FOX_WIDTH), rows(d, COL_GA // d), rows(d, COL_GB // d), rows(d),
                  whole(wt_pa), whole(wt_pb), whole(w_out), whole(g_post), whole(g_ffn_pre)],
        out_specs=[rows(d)] * 4,
        out_shape=[pltpu.HBM((SEQ, d), dt) for dt in (BF16, F32, F32, BF16)],
        compiler_params=pltpu.CompilerParams(dimension_semantics=("parallel",), vmem_limit_bytes=_vmem_limit(blk)),
    )(*_in_hbm(out_a, out_b, proj, proj, x, wt_pa, wt_pb, w_out, g_post, g_ffn_pre))


def _mix_out_bwd(dmix, out_a, out_b, proj, wt_pa, wt_pb, w_out, deps=()):
    tm = MIX_TILE

    def body(dm_ref, a_ref, b_ref, ga_ref, gb_ref, wpa_ref, wpb_ref, wo_ref, *rest):
        dproj_ref, dya_ref, dyb_ref, da_ref, db_ref = rest[len(deps):]
        dmerged = _dot_nt(dm_ref[...], wo_ref[...])
        ya = _dot_nn(a_ref[...].astype(BF16), wpa_ref[...])
        yb = _dot_nn(b_ref[...].astype(BF16), wpb_ref[...])
        sa, sb = _sigmoid(ga_ref[...]), _sigmoid(gb_ref[...])
        dproj_ref[:, COL_GA:COL_GA + D_MODEL] = (dmerged * ya * (sa * (1.0 - sa))).astype(BF16)
        dproj_ref[:, COL_GB:COL_GB + D_MODEL] = (dmerged * yb * (sb * (1.0 - sb))).astype(BF16)
        dproj_ref[:, COL_GB + D_MODEL:] = jnp.zeros((tm, COL_QA - COL_GB - D_MODEL), BF16)
        dya = (dmerged * sa).astype(BF16)
        dyb = (dmerged * sb).astype(BF16)
        dya_ref[...] = dya
        dyb_ref[...] = dyb
        da_ref[...] = _dot_nt(dya, wpa_ref[...])
        db_ref[...] = _dot_nt(dyb, wpb_ref[...]).astype(BF16)

    def rows(w, cb=0):
        return pl.BlockSpec((tm, w), lambda i, cb=cb: (i, cb))

    def whole(a):
        return pl.BlockSpec(a.shape, lambda i: (0, 0))

    d = D_MODEL
    blk = _nbytes((tm, d), F32) * 8 + sum(_nbytes(a.shape, BF16) for a in (wt_pa, wt_pb, w_out))
    return pl.pallas_call(
        body, name="mix_out_bwd", grid=(SEQ // tm,),
        in_specs=[rows(d), rows(DIL_OUT_WIDTH), rows(FOX_WIDTH), rows(d, COL_GA // d), rows(d, COL_GB // d),
                  whole(wt_pa), whole(wt_pb), whole(w_out)] + [_ANY] * len(deps),
        out_specs=[rows(COL_QA)] + [rows(d)] * 2 + [rows(DIL_OUT_WIDTH), rows(FOX_WIDTH)],
        out_shape=[pltpu.HBM((SEQ, PROJ_COLS), BF16)] + [pltpu.HBM((SEQ, d), BF16)] * 2
        + [pltpu.HBM((SEQ, DIL_OUT_WIDTH), F32), pltpu.HBM((SEQ, FOX_WIDTH), BF16)],
        compiler_params=pltpu.CompilerParams(dimension_semantics=("parallel",), vmem_limit_bytes=_vmem_limit(blk)),
    )(*_in_hbm(dmix, out_a, out_b, proj, proj, wt_pa, wt_pb, w_out), *deps)


FFN_TM, FFN_TN = 2048, 256


def _ffn_up(h2, wt_gate, wt_up):
    tm, tn = FFN_TM, FFN_TN

    def body(h_ref, wg_ref, wu_ref, gate_ref, up_ref, act_ref):
        for rows in (slice(0, tm // 2), slice(tm // 2, tm)):
            gate = _dot_nt(h_ref[rows, :], wg_ref[...])
            up = _dot_nt(h_ref[rows, :], wu_ref[...])
            gate_ref[rows, :] = gate
            up_ref[rows, :] = up
            act_ref[rows, :] = (gate * _sigmoid(gate) * up).astype(BF16)

    tile = pl.BlockSpec((tm, tn), lambda i, j: (i, j))
    w_spec = pl.BlockSpec((tn, D_MODEL), lambda i, j: (j, 0))
    return pl.pallas_call(
        body, name="ffn_up", grid=(SEQ // tm, D_FF // tn),
        in_specs=[pl.BlockSpec((tm, D_MODEL), lambda i, j: (i, 0)), w_spec, w_spec],
        out_specs=[tile, tile, tile],
        out_shape=[pltpu.HBM((SEQ, D_FF), dt) for dt in (F32, F32, BF16)],
        compiler_params=pltpu.CompilerParams(
            dimension_semantics=("parallel", "parallel"), vmem_limit_bytes=_vmem_limit(8 * 2**20)),
    )(h2, wt_gate, wt_up)


def _ffn_act_bwd(dff, w_down, gate, up):
    tm, tn = FFN_TM, FFN_TN

    def body(d_ref, wd_ref, gate_ref, up_ref, dgate_ref, dup_ref):
        for rows in (slice(0, tm // 2), slice(tm // 2, tm)):
            dact = _dot_nt(d_ref[rows, :], wd_ref[...])
            gate = gate_ref[rows, :]
            sg = _sigmoid(gate)
            dgate_ref[rows, :] = (dact * up_ref[rows, :] * (sg * (1.0 + gate * (1.0 - sg)))).astype(BF16)
            dup_ref[rows, :] = (dact * (gate * sg)).astype(BF16)

    tile = pl.BlockSpec((tm, tn), lambda i, j: (i, j))
    return pl.pallas_call(
        body, name="ffn_act_bwd", grid=(SEQ // tm, D_FF // tn),
        in_specs=[pl.BlockSpec((tm, D_MODEL), lambda i, j: (i, 0)), pl.BlockSpec((tn, D_MODEL), lambda i, j: (j, 0)),
                  tile, tile],
        out_specs=[tile, tile],
        out_shape=[pltpu.HBM((SEQ, D_FF), BF16)] * 2,
        compiler_params=pltpu.CompilerParams(
            dimension_semantics=("parallel", "parallel"), vmem_limit_bytes=_vmem_limit(8 * 2**20)),
    )(dff, w_down, gate, up)


EPILOGUE_TM = 512


def _loss_head(act, w_down, x1, target, g_post):
    def fn(ff, x1, tgt, g):
        r = _rms_scale(ff)
        nrm = ff * r
        err = (x1 + nrm * g) - tgt
        loss = 0.5 * jnp.sum(jnp.mean(err * err, axis=-1, keepdims=True), axis=0, keepdims=True)
        dy = err * (1.0 / D_MODEL)
        u = dy * g
        dff = r * u - ff * (r * r * r) * jnp.mean(u * ff, axis=-1, keepdims=True)
        return dy, dff, jnp.broadcast_to(loss, (1, LANES)), jnp.sum(dy * nrm, axis=0, keepdims=True)

    d = D_MODEL
    return _matmul_rowwise([(act, w_down)], fn, "ffn_down_loss", EPILOGUE_TM, [(x1, d, 0), (target, d, 0)], [g_post],
                           [(d, F32), (d, BF16)], [LANES, d])


def _post_ffn_bwd(dgate, wt_gate, dup, wt_up, x1, dy, mix, g_ffn_pre, g_mix_post, deps=()):
    def fn(dh2, x1, dy, mix, g3, g2):
        dx, dg3 = _rms_bwd(x1, dh2, g3)
        dx1 = dy + dx
        dmix, dg2 = _rms_bwd(mix, dx1, g2)
        return dx1, dmix, dg3, dg2

    d = D_MODEL
    return _matmul_rowwise([(dgate, wt_gate), (dup, wt_up)], fn, "ffn_up_bwd", EPILOGUE_TM,
                           [(x1, d, 0), (dy, d, 0), (mix, d, 0)], [g_ffn_pre, g_mix_post],
                           [(d, F32), (d, BF16)], [d, d], deps=deps)


def _input_bwd(dproj, wt_r, x, dx1, g_pre, deps=()):
    def fn(dh, x, dx1, g):
        dx, dg = _rms_bwd(x, dh, g)
        return dx1 + dx, dg

    d = D_MODEL
    return _matmul_rowwise([(dproj, wt_r)], fn, "in_proj_bwd", EPILOGUE_TM, [(x, d, 0), (dx1, d, 0)], [g_pre],
                           [(d, F32)], [d], deps=deps)


def _adam_math(w, g, m, v):
    m = ADAM_B1 * m + (1.0 - ADAM_B1) * g
    v = ADAM_B2 * v + (1.0 - ADAM_B2) * (g * g)
    m_hat = m / (1.0 - ADAM_B1 ** ADAM_STEP)
    v_hat = v / (1.0 - ADAM_B2 ** ADAM_STEP)
    delta = -ADAM_LR * (m_hat / (jnp.sqrt(v_hat) + ADAM_EPS) + ADAM_WD * w)
    return delta, m, v


def _adam(w, mine, recv, m, v, name):
    r, c = w.shape
    tc = _col_tile(r, c)

    def body(w_ref, p_ref, r_ref, m_ref, v_ref, g_ref, d_ref, nm_ref, nv_ref):
        g = ((p_ref[...] + r_ref[0].astype(F32)) + r_ref[1].astype(F32)) + r_ref[2].astype(F32)
        g_ref[...] = g
        d_ref[...], nm_ref[...], nv_ref[...] = _adam_math(w_ref[...], g, m_ref[...], v_ref[...])

    spec = pl.BlockSpec((r, tc), lambda j: (0, j))
    return pl.pallas_call(
        body, name=name, grid=(c // tc,),
        in_specs=[spec, spec, pl.BlockSpec((3, r, tc), lambda j: (0, 0, j)), spec, spec], out_specs=[spec] * 4,
        out_shape=[pltpu.HBM((r, c), F32)] * 4,
        compiler_params=pltpu.CompilerParams(dimension_semantics=("parallel",)),
    )(*_in_hbm(w, mine, recv, m, v))


def _adam_small(gathered, ws, ms, vs, loss_parts):
    n = len(ws)

    def body(*refs):
        outs = refs[4 * n + 1:]
        loss = refs[4 * n][0]
        for dev in range(1, N_DEV):
            loss = loss + refs[4 * n][dev]
        outs[4 * n][...] = loss
        for i in range(n):
            ga_ref, w_ref, m_ref, v_ref = (refs[j * n + i] for j in range(4))
            g = ga_ref[0]
            for dev in range(1, N_DEV):
                g = g + ga_ref[dev]
            g = g[:, :w_ref.shape[1]]
            outs[4 * i][...] = g
            outs[4 * i + 1][...], outs[4 * i + 2][...], outs[4 * i + 3][...] = _adam_math(
                w_ref[...], g, m_ref[...], v_ref[...])

    out_shape = [pltpu.HBM(w.shape, F32) for w in ws for _ in range(4)]
    out_shape.append(pltpu.HBM((1, LANES), F32))
    out = pl.pallas_call(body, name="adam_small", out_shape=out_shape)(*gathered, *ws, *ms, *vs, loss_parts)
    return [out[4 * i:4 * i + 4] for i in range(n)], out[4 * n]


_PROJ_SEGMENTS = ((3848, 5896), (None, COL_QA - 2 * D_MODEL), (0, 3840), (3840, 3848), (None, PROJ_COLS - COL_F - 8))


def _proj_weight_t(gathered):
    pieces, zeros, at = [], [], 0
    for lo, hi in _PROJ_SEGMENTS:
        if lo is None:
            zeros.append((at, hi))
            at += hi
            continue
        for dev in range(lo // IN_SHARD, (hi - 1) // IN_SHARD + 1):
            a, b = max(lo, dev * IN_SHARD), min(hi, (dev + 1) * IN_SHARD)
            pieces.append((dev, a - dev * IN_SHARD, at + a - lo, b - a))
        at += hi - lo
    assert at == PROJ_COLS
    tc = 2 * LANES

    def body(g_ref, o_ref, shards, rows):
        for dev in range(N_DEV):
            shards[dev] = g_ref[dev].astype(F32)
        for dev, src, dst, n in pieces:
            rows[pl.ds(dst, n), :] = shards[dev, pl.ds(src, n), :]
        for dst, n in zeros:
            rows[pl.ds(dst, n), :] = jnp.zeros((n, tc), F32)
        o_ref[...] = rows[...].astype(o_ref.dtype)

    return pl.pallas_call(
        body, name="w_in_rows", grid=(D_MODEL // tc,),
        in_specs=[pl.BlockSpec((N_DEV, IN_SHARD, tc), lambda j: (0, 0, j))],
        out_specs=pl.BlockSpec((PROJ_COLS, tc), lambda j: (0, j)),
        out_shape=pltpu.HBM((PROJ_COLS, D_MODEL), gathered.dtype),
        scratch_shapes=[pltpu.VMEM((N_DEV, IN_SHARD, tc), F32), pltpu.VMEM((PROJ_COLS, tc), F32)],
        compiler_params=pltpu.CompilerParams(
            dimension_semantics=("parallel",), vmem_limit_bytes=_vmem_limit(2 * _nbytes((PROJ_COLS, tc), F32))),
    )(*_in_hbm(gathered))


def _proj_weight_grad_slots(dwt_r):
    starts, at = [], 0
    for lo, hi in _PROJ_SEGMENTS:
        if lo is not None:
            starts.append((lo, hi, at))
        at += hi if lo is None else hi - lo
    pieces = []
    for dev in range(N_DEV):
        lo, end = dev * IN_SHARD, (dev + 1) * IN_SHARD
        for seg_lo, seg_hi, seg_at in sorted(starts):
            a, b = max(lo, seg_lo), min(end, seg_hi)
            if a < b:
                pieces.append((dev, a - lo, seg_at + a - seg_lo, b - a))

    def body(g_ref, o_ref):
        for dev, dst, src, rows in pieces:
            o_ref[dev, pl.ds(dst, rows), :] = g_ref[pl.ds(src, rows), :]

    tc = 2 * LANES
    return pl.pallas_call(
        body, name="grad_w_in_slots", grid=(D_MODEL // tc,),
        in_specs=[pl.BlockSpec((PROJ_COLS, tc), lambda j: (0, j))],
        out_specs=pl.BlockSpec((N_DEV, IN_SHARD, tc), lambda j: (0, 0, j)),
        out_shape=pltpu.HBM((N_DEV, IN_SHARD, D_MODEL), F32),
        compiler_params=pltpu.CompilerParams(
            dimension_semantics=("parallel",), vmem_limit_bytes=_vmem_limit(2 * _nbytes((PROJ_COLS, tc), F32))),
    )(*_in_hbm(dwt_r))


def kernel(x, w_in, w_proj_a, w_proj_b, w_out, b_forget, w_ffn_gate, w_ffn_up, w_ffn_down, norm_mix_pre, norm_mix_post, norm_ffn_pre, norm_ffn_post, loss_target, m_w_in, m_w_proj_a, m_w_proj_b, m_w_out, m_b_forget, m_w_ffn_gate, m_w_ffn_up, m_w_ffn_down, m_norm_mix_pre, m_norm_mix_post, m_norm_ffn_pre, m_norm_ffn_post, v_w_in, v_w_proj_a, v_w_proj_b, v_w_out, v_b_forget, v_w_ffn_gate, v_w_ffn_up, v_w_ffn_down, v_norm_mix_pre, v_norm_mix_post, v_norm_ffn_pre, v_norm_ffn_post):
    d = D_MODEL
    names = ("w_in", "w_proj_a", "w_proj_b", "w_out", "w_ffn_gate", "w_ffn_up", "w_ffn_down")
    col_sharded = ("w_in", "w_ffn_gate", "w_ffn_up")

    def row_shards(arrs):
        return {k: (a[0].T if k in col_sharded else a[0]) for k, a in zip(names, arrs)}

    shards = row_shards((w_in, w_proj_a, w_proj_b, w_out, w_ffn_gate, w_ffn_up, w_ffn_down))
    moments_m = row_shards((m_w_in, m_w_proj_a, m_w_proj_b, m_w_out, m_w_ffn_gate, m_w_ffn_up, m_w_ffn_down))
    moments_v = row_shards((v_w_in, v_w_proj_a, v_w_proj_b, v_w_out, v_w_ffn_gate, v_w_ffn_up, v_w_ffn_down))
    pos = jnp.stack([lax.axis_index("c"), 2 * lax.axis_index("x") + lax.axis_index("y")]).astype(jnp.int32)
    x2, target = x[0], loss_target[0]

    me = 4 * lax.axis_index("x") + 2 * lax.axis_index("y") + lax.axis_index("c")
    mid_names, ffn_names = names[1:4], names[4:]
    first_names, later_names = names[:1], names[1:]
    shards16 = {k: shards[k].astype(BF16) for k in names}

    def landing(k):
        return lax.dynamic_update_slice(lax.empty((N_DEV,) + shards[k].shape, BF16), shards16[k][None], (me, 0, 0))

    ag_first = _exchange_start("ag_first_chips_start", _gather_two_route_copies, [shards16[k] for k in first_names],
                               [landing(k) for k in first_names], 4 * len(first_names))
    h = _rowwise(lambda xb, g: xb * _rms_scale(xb) * g, "norm_mix_pre", SEQ, 256, [(x2, d, 0)], [norm_mix_pre],
                 [(d, BF16)], deps=[ag_first.token])[0]
    later_lands = [landing(k) for k in later_names]
    _, ag_first_diagonal = _gather_relay(ag_first, "ag_first", [h, shards["w_in"], moments_m["w_in"], moments_v["w_in"],
                                                               *later_lands, *[shards16[k] for k in later_names]])
    relayed, ag_first_last = ag_first_diagonal([])
    ag_later = _exchange_start("ag_later_chips_start", _gather_two_route_copies, [shards16[k] for k in later_names],
                               later_lands, 4 * len(later_names), after=[relayed])
    gathered = dict(zip(first_names, ag_first_last([ag_later.token])))
    wt_r = _proj_weight_t(gathered["w_in"])

    proj = _matmul([(h, *_in_hbm(wt_r))], "nt", F32, "in_proj", 1024, 896, 1024)
    tables = _rope_tables()
    relayed, ag_later_diagonal = _gather_relay(ag_later, "ag_later", [proj])
    o_dil, lse_dil = _dil_fwd(proj, tables, deps=[relayed])
    out_a = _dil_combine(o_dil, lse_dil)
    relayed, ag_later_last = ag_later_diagonal([out_a])

    b_pad = jnp.pad(b_forget, ((0, 0), (0, LANES - N_FOX_HEADS)))
    f_rows = _fox_gate(proj, b_pad, deps=[relayed])
    out_b, lse_fox = _fox_fwd(proj, f_rows)

    gathered = dict(zip(later_names, ag_later_last([out_b])))
    wt_pa = gathered["w_proj_a"].transpose(1, 0, 2).reshape(DIL_OUT_WIDTH, d)
    wt_pb = gathered["w_proj_b"].transpose(1, 0, 2).reshape(FOX_WIDTH, d)
    w_o = gathered["w_out"].reshape(d, d)
    wt_g = gathered["w_ffn_gate"].reshape(D_FF, d)
    wt_u = gathered["w_ffn_up"].reshape(D_FF, d)
    w_d = gathered["w_ffn_down"].reshape(D_FF, d)
    merged, mix, x1, h2 = _mix_out(out_a, out_b, proj, x2, wt_pa, wt_pb, w_o, norm_mix_post, norm_ffn_pre)

    gate, up, act = _ffn_up(h2, wt_g, wt_u)
    dy, dff, loss_part, dg_ffn_post = _loss_head(act, w_d, x1, target, norm_ffn_post)

    dgate, dup = _ffn_act_bwd(dff, w_d, gate, up)
    grads_t = {}
    grads_t["w_ffn_down"] = _matmul([(act, dff)], "tn", F32, "grad_w_ffn_down", 1408, 512, 2048, staged=False)
    grads_t["w_ffn_gate"] = _matmul([(dgate, h2)], "tn", F32, "grad_w_ffn_gate", 1408, 512, 2048)
    grads_t["w_ffn_up"] = _matmul([(dup, h2)], "tn", F32, "grad_w_ffn_up", 1408, 512, 2048)
    rs_ffn = _ReduceScatter("ffn", {k: grads_t[k] for k in ffn_names}, pos)
    dx1, dmix, dg_ffn_pre, dg_mix_post = _post_ffn_bwd(dgate, wt_g, dup, wt_u, x1, dy, mix, norm_ffn_pre, norm_mix_post,
                                                       deps=[rs_ffn.token])
    rs_ffn.start_chips([dmix])

    dproj, dya, dyb, d_out_a, d_out_b = _mix_out_bwd(dmix, out_a, out_b, proj, wt_pa, wt_pb, w_o, deps=[rs_ffn.token])
    grads_t["w_out"] = _matmul([(merged, dmix)], "tn", F32, "grad_w_out", 1024, 1024, 1024)
    def column_slots(g):
        return g.reshape(g.shape[0], N_DEV, LANES).transpose(1, 0, 2)

    grads_t["w_proj_a"] = column_slots(_matmul([(out_a, dya)], "tn", F32, "grad_w_proj_a", DIL_OUT_WIDTH, 1024, SEQ))
    grads_t["w_proj_b"] = column_slots(_matmul([(out_b, dyb)], "tn", F32, "grad_w_proj_b", FOX_WIDTH, 1024, SEQ))
    rs_mid = _ReduceScatter("mid", {k: grads_t[k] for k in mid_names}, pos)

    do_dil, c_dil = _dil_combine_bwd(d_out_a, o_dil, lse_dil, deps=[rs_mid.token])
    rs_mid.start_chips([c_dil])
    dproj, d_cum = _fox_bwd(proj, d_out_b, lse_fox, f_rows, dproj)
    d_cum_rows = jnp.pad(d_cum[:, :2].reshape(N_FOX_HEADS, SEQ), ((0, F_ROWS - N_FOX_HEADS), (0, 0)))
    dproj, db_part = _fox_gate_bwd(d_cum_rows, proj, b_pad, dproj)
    dproj = _dil_bwd(proj, tables, do_dil, lse_dil, c_dil, dproj, deps=[rs_mid.token])

    dwt_r = _matmul([(dproj, h)], "tn", F32, "grad_w_in", 896, 1024, 2048)
    rs_in = _ReduceScatter("in", {"w_in": _proj_weight_grad_slots(dwt_r)}, pos)
    def finish(rs, after):
        return {k: _adam(shards[k], mine, recv, moments_m[k], moments_v[k], "adam_" + k)
                for k, (mine, recv) in rs.finish(after).items()}

    done = finish(rs_ffn, [rs_in.token])
    rs_in.start_chips([done[k][0] for k in ffn_names])
    grad_x, dg_mix_pre = _input_bwd(dproj, wt_r, x2, dx1, norm_mix_pre, deps=[rs_in.token])
    done.update(finish(rs_mid, [grad_x]))

    small_all = _all_gather([dg_mix_pre, dg_mix_post, dg_ffn_pre, dg_ffn_post, db_part, loss_part],
                            "small_grads_all_gather", deps=[done[k][0] for k in mid_names])
    small, loss = _adam_small(small_all[:5], [norm_mix_pre, norm_mix_post, norm_ffn_pre, norm_ffn_post, b_forget],
                              [m_norm_mix_pre, m_norm_mix_post, m_norm_ffn_pre, m_norm_ffn_post, m_b_forget],
                              [v_norm_mix_pre, v_norm_mix_post, v_norm_ffn_pre, v_norm_ffn_post, v_b_forget],
                              small_all[5])

    done.update(finish(rs_in, [small[0][0]]))

    def leaves(i):
        def nat(k):
            a = done[k][i]
            return (a.T if k in col_sharded else a)[None]

        return [nat("w_in"), nat("w_proj_a"), nat("w_proj_b"), nat("w_out"), small[4][i],
                nat("w_ffn_gate"), nat("w_ffn_up"), nat("w_ffn_down"), *[small[r][i] for r in range(4)]]

    return (loss[0, 0], grad_x[None], *leaves(0), *leaves(1), *leaves(2), *leaves(3))
```

```python
import functools
import math

import jax
import jax.numpy as jnp
import numpy as np
from jax import lax
from jax.experimental import pallas as pl
from jax.experimental.pallas import tpu as pltpu

F32 = jnp.float32
BF16 = jnp.bfloat16
MESH = pl.DeviceIdType.MESH

D_MODEL = 1024
SEQ = 2048
HEAD_DIM = 64
BLOCK = 128
N_BLOCKS = SEQ // BLOCK
DILATIONS = (1, 4, 16)
N_FOX_HEADS = 8
DIL_WIDTH = 768
DIL_OUT_WIDTH = 256
FOX_WIDTH = 512
D_FF = 2816
ROPE_THETA = 500000.0
ROPE_DIM = HEAD_DIM // 4
ROPE_HALF = ROPE_DIM // 2
EPS = 1e-6
NEG_INF = -1e30
QK_SCALE = 1.0 / math.sqrt(HEAD_DIM)
IN_COLS = 5896
N_DEV = 8
IN_SHARD = IN_COLS // N_DEV

ADAM_LR = 0.001
ADAM_B1 = 0.9
ADAM_B2 = 0.999
ADAM_EPS = 1e-08
ADAM_WD = 0.01
ADAM_STEP = 10

V7X_VMEM_BYTES = 64 * 2**20
LANES = 128
SUBLANES = 8

PROJ_COLS = 6272
COL_GA, COL_GB = 0, 1024
COL_QA, COL_KA, COL_VA = 2304, 3072, 3840
COL_QB, COL_KB, COL_VB = 4608, 5120, 5632
COL_F = 6144
F_ROWS = 16


def _vmem_limit(block_bytes):
    want = 2 * block_bytes + 16 * 2**20
    return int(min(max(want, 32 * 2**20), V7X_VMEM_BYTES - 8 * 2**20))


def _nbytes(shape, dtype):
    return math.prod(shape) * jnp.dtype(dtype).itemsize


def _in_hbm(*arrays):
    return [pltpu.with_memory_space_constraint(a, pltpu.HBM) for a in arrays]


def _dot(a, b, dims):
    return lax.dot_general(a, b, (dims, ((), ())), preferred_element_type=F32)


def _dot_nn(a, b):
    return _dot(a, b, ((1,), (0,)))


def _dot_nt(a, b):
    return _dot(a, b, ((1,), (1,)))


def _dot_tn(a, b):
    return _dot(a, b, ((0,), (0,)))


def _sigmoid(z):
    return 1.0 / (1.0 + jnp.exp(-z))


def _split3(x):
    hi = x.astype(BF16)
    r1 = x - hi.astype(F32)
    mid = r1.astype(BF16)
    lo = (r1 - mid.astype(F32)).astype(BF16)
    return hi, mid, lo


def _dot3_nn(x, ones_matrix):
    hi, mid, lo = _split3(x)
    return (_dot_nn(hi, ones_matrix) + _dot_nn(mid, ones_matrix)) + _dot_nn(lo, ones_matrix)


def _rowwise(fn, name, n_rows, tm, row_ins, bcast_ins, row_outs, acc_outs=(), deps=()):
    n_in = len(row_ins) + len(bcast_ins)
    n_ro = len(row_outs)

    def body(*refs):
        res = fn(*[r[...] for r in refs[:n_in]])
        if not isinstance(res, (tuple, list)):
            res = (res,)
        outs = refs[n_in + len(deps):]
        for r, o in zip(res[:n_ro], outs[:n_ro]):
            o[...] = r.astype(o.dtype)
        first = pl.program_id(0) == 0
        for r, o in zip(res[n_ro:], outs[n_ro:]):
            _accumulate(o, r, first)

    in_specs = [pl.BlockSpec((tm, w), lambda i, cb=cb: (i, cb)) for _, w, cb in row_ins]
    in_specs += [pl.BlockSpec(a.shape, lambda i: (0, 0)) for a in bcast_ins]
    in_specs += [pl.BlockSpec(memory_space=pl.ANY)] * len(deps)
    out_specs = [pl.BlockSpec((tm, w), lambda i: (i, 0)) for w, _ in row_outs]
    out_specs += [pl.BlockSpec((1, w), lambda i: (0, 0)) for w in acc_outs]
    out_shape = [pltpu.HBM((n_rows, w), dt) for w, dt in row_outs]
    out_shape += [pltpu.HBM((1, w), F32) for w in acc_outs]
    blk = sum(_nbytes((tm, w), a.dtype) for a, w, _ in row_ins) + sum(_nbytes((tm, w), dt) for w, dt in row_outs)
    return pl.pallas_call(
        body, name=name, grid=(n_rows // tm,), in_specs=in_specs, out_specs=out_specs, out_shape=out_shape,
        compiler_params=pltpu.CompilerParams(
            dimension_semantics=("arbitrary" if acc_outs else "parallel",), vmem_limit_bytes=_vmem_limit(3 * blk)),
    )(*_in_hbm(*[a for a, _, _ in row_ins], *bcast_ins), *deps)


def _accumulate(o_ref, part, first):
    @pl.when(first)
    def _():
        o_ref[...] = part

    @pl.when(jnp.logical_not(first))
    def _():
        o_ref[...] += part


_MM_DIMS = {"nn": ((1,), (0,)), "nt": ((1,), (1,)), "tn": ((0,), (0,))}


def _matmul(pairs, mode, out_dtype, name, tm, tn, tk, deps=(), staged=True):
    a0, b0 = pairs[0]
    if mode == "tn":
        kk, m = a0.shape
    else:
        m, kk = a0.shape
    n = b0.shape[0] if mode == "nt" else b0.shape[1]
    assert m % tm == 0 and n % tn == 0 and kk % tk == 0, (name, m, n, kk)
    nk = kk // tk
    n_pairs = len(pairs)
    dims = _MM_DIMS[mode]
    n_in = 2 * n_pairs + len(deps)

    def body(*refs):
        o_ref = refs[n_in]
        part = None
        for p in range(n_pairs):
            d = _dot(refs[2 * p][...].astype(BF16), refs[2 * p + 1][...].astype(BF16), dims)
            part = d if part is None else part + d
        if nk == 1:
            o_ref[...] = part.astype(o_ref.dtype)
            return
        acc = refs[n_in + 1]
        k = pl.program_id(2)

        @pl.when(k == 0)
        def _():
            acc[...] = part

        @pl.when(k > 0)
        def _():
            acc[...] += part

        @pl.when(k == nk - 1)
        def _():
            o_ref[...] = acc[...].astype(o_ref.dtype)

    if mode == "tn":
        a_spec = pl.BlockSpec((tk, tm), lambda i, j, k: (k, i))
    else:
        a_spec = pl.BlockSpec((tm, tk), lambda i, j, k: (i, k))
    if mode == "nt":
        b_spec = pl.BlockSpec((tn, tk), lambda i, j, k: (j, k))
    else:
        b_spec = pl.BlockSpec((tk, tn), lambda i, j, k: (k, j))
    blk = sum(_nbytes((tm, tk), a.dtype) + _nbytes((tk, tn), b.dtype) for a, b in pairs) + 2 * _nbytes((tm, tn), F32)
    flat = [a for pair in pairs for a in pair]
    return pl.pallas_call(
        body, name=name, grid=(m // tm, n // tn, nk),
        in_specs=[a_spec, b_spec] * n_pairs + [pl.BlockSpec(memory_space=pl.ANY)] * len(deps),
        out_specs=pl.BlockSpec((tm, tn), lambda i, j, k: (i, j)),
        out_shape=pltpu.HBM((m, n), out_dtype),
        scratch_shapes=[] if nk == 1 else [pltpu.VMEM((tm, tn), F32)],
        compiler_params=pltpu.CompilerParams(
            dimension_semantics=("parallel", "parallel", "arbitrary"), vmem_limit_bytes=_vmem_limit(blk)),
    )(*(flat if staged else _in_hbm(*flat)), *deps)


def _matmul_rowwise(pairs, fn, name, tm, row_ins, bcast_ins, row_outs, acc_outs=(), deps=()):
    m = pairs[0][0].shape[0]
    n_mm, n_in = 2 * len(pairs), len(row_ins) + len(bcast_ins)
    n_ro = len(row_outs)

    def body(*refs):
        prod = None
        for p in range(len(pairs)):
            part = _dot_nn(refs[2 * p][...].astype(BF16), refs[2 * p + 1][...].astype(BF16))
            prod = part if prod is None else prod + part
        res = fn(prod, *[r[...] for r in refs[n_mm:n_mm + n_in]])
        outs = refs[n_mm + n_in + len(deps):]
        for r, o in zip(res[:n_ro], outs[:n_ro]):
            o[...] = r.astype(o.dtype)
        first = pl.program_id(0) == 0
        for r, o in zip(res[n_ro:], outs[n_ro:]):
            _accumulate(o, r, first)

    in_specs = []
    for a, b in pairs:
        in_specs += [pl.BlockSpec((tm, a.shape[1]), lambda i: (i, 0)),
                     pl.BlockSpec(b.shape, lambda i: (0, 0), pipeline_mode=pl.Buffered(1))]
    in_specs += [pl.BlockSpec((tm, w), lambda i, cb=cb: (i, cb)) for _, w, cb in row_ins]
    in_specs += [pl.BlockSpec(a.shape, lambda i: (0, 0)) for a in bcast_ins]
    in_specs += [_ANY] * len(deps)
    out_specs = [pl.BlockSpec((tm, w), lambda i: (i, 0)) for w, _ in row_outs]
    out_specs += [pl.BlockSpec((1, w), lambda i: (0, 0)) for w in acc_outs]
    out_shape = [pltpu.HBM((m, w), dt) for w, dt in row_outs]
    out_shape += [pltpu.HBM((1, w), F32) for w in acc_outs]
    blk = sum(_nbytes((tm, a.shape[1]), a.dtype) + _nbytes(b.shape, b.dtype) // 2 for a, b in pairs)
    blk += sum(_nbytes((tm, w), a.dtype) for a, w, _ in row_ins) + sum(_nbytes((tm, w), dt) for w, dt in row_outs)
    return pl.pallas_call(
        body, name=name, grid=(m // tm,), in_specs=in_specs, out_specs=out_specs, out_shape=out_shape,
        compiler_params=pltpu.CompilerParams(dimension_semantics=("arbitrary",), vmem_limit_bytes=_vmem_limit(blk)),
    )(*[a for pair in pairs for a in pair], *[a for a, _, _ in row_ins], *bcast_ins, *deps)


def _rms_scale(x):
    return lax.rsqrt(jnp.mean(x * x, axis=-1, keepdims=True) + EPS)


def _rms_bwd(xin, dyn, g):
    r = _rms_scale(xin)
    u = dyn * g
    dx = r * u - xin * (r * r * r) * jnp.mean(u * xin, axis=-1, keepdims=True)
    dg = jnp.sum(dyn * xin * r, axis=0, keepdims=True)
    return dx, dg


def _mesh_pos():
    return lax.axis_index("x"), lax.axis_index("y"), lax.axis_index("c")


def _all_gather(xs, name, deps=()):
    n = len(xs)

    def body(*refs):
        x_refs, out_refs = refs[:n], refs[n + len(deps):2 * n + len(deps)]
        send_sems, recv_sems, local_sems = refs[2 * n + len(deps):]
        mx, my, mc = _mesh_pos()
        me, sib = (mx, my, mc), (mx, my, 1 - mc)
        chips = [(1 - mx, my), (mx, 1 - my), (1 - mx, 1 - my)]

        def slot(a, dev):
            px, py, pc = dev
            return out_refs[a].at[4 * px + 2 * py + pc]

        def copy(k, a, block, to, src=None):
            return pltpu.make_async_remote_copy(
                src_ref=slot(a, block) if src is None else src, dst_ref=slot(a, block),
                send_sem=send_sems.at[a * 7 + k], recv_sem=recv_sems.at[a * 7 + k],
                device_id=to, device_id_type=MESH)

        mine = [pltpu.make_async_copy(x_refs[a], slot(a, me), local_sems.at[a]) for a in range(n)]
        for cp in mine:
            cp.start()
        first = []
        for a in range(n):
            first.append(copy(0, a, me, sib, x_refs[a]))
            first += [copy(1 + j, a, me, (*chip, mc), x_refs[a]) for j, chip in enumerate(chips)]
        for cp in first:
            cp.start()
        passed = []
        for a in range(n):
            for j, chip in enumerate(chips):
                copy(1 + j, a, (*chip, mc), me).wait_recv()
                fwd = copy(4 + j, a, (*chip, mc), sib)
                fwd.start()
                passed.append(fwd)
        for a in range(n):
            copy(0, a, sib, me).wait_recv()
            for j, chip in enumerate(chips):
                copy(4 + j, a, (*chip, 1 - mc), me).wait_recv()
        for cp in first + passed:
            cp.wait_send()
        for cp in mine:
            cp.wait()

    hbm = pl.BlockSpec(memory_space=pl.ANY)
    return pl.pallas_call(
        body, name=name,
        out_shape=[pltpu.HBM((N_DEV,) + x.shape, x.dtype) for x in xs],
        in_specs=[hbm] * (n + len(deps)), out_specs=[hbm] * n,
        scratch_shapes=[pltpu.SemaphoreType.DMA((7 * n,)), pltpu.SemaphoreType.DMA((7 * n,)),
                        pltpu.SemaphoreType.DMA((n,))],
    )(*xs, *deps)


_HBM = pl.BlockSpec(memory_space=pltpu.HBM)
_SEM = pl.BlockSpec(memory_space=pltpu.SEMAPHORE)
_ANY = pl.BlockSpec(memory_space=pl.ANY)
_DATAFLOW = pltpu.SideEffectType.DATAFLOW_SIDE_EFFECTING


def _flip_peer(flip):
    mx, my, mc = _mesh_pos()
    return (1 - mx if flip & 2 else mx, 1 - my if flip & 1 else my, mc)


def _remote(src, dst, send_sems, recv_sems, k, peer):
    return pltpu.make_async_remote_copy(src_ref=src, dst_ref=dst, send_sem=send_sems.at[k], recv_sem=recv_sems.at[k],
                                        device_id=peer, device_id_type=MESH)


def _scatter_sibling_copies(srcs, lands, send_sems, recv_sems):
    mx, my, mc = _mesh_pos()
    return [_remote(srcs[a].at[k, 1 - mc], lands[a].at[k], send_sems, recv_sems, 4 * a + k, (mx, my, 1 - mc))
            for a in range(len(srcs)) for k in range(4)]


def _scatter_chips_copies(srcs, lands, send_sems, recv_sems):
    mx, my, _ = _mesh_pos()
    k0 = 2 * mx + my
    return [_remote(srcs[a].at[jnp.bitwise_xor(k0, flip)], lands[a].at[flip - 1], send_sems, recv_sems,
                    3 * a + flip - 1, _flip_peer(flip))
            for a in range(len(srcs)) for flip in (1, 2, 3)]


class _Exchange:
    def __init__(self, copies, n_src, send_sems, recv_sems, thru, token):
        self.copies, self.n_src, self.send_sems, self.recv_sems, self.thru, self.token = (
            copies, n_src, send_sems, recv_sems, thru, token)


def _exchange_start(name, copies, srcs, lands, n_copies, after=()):
    bufs = list(srcs) + list(lands)
    nb, ns = len(bufs), len(srcs)

    def body(*refs):
        send_sems, recv_sems = refs[nb + len(after)], refs[nb + len(after) + 1]
        for cp in copies(refs[:ns], refs[ns:nb], send_sems, recv_sems):
            cp.start()
        refs[-1][...] = jnp.zeros_like(refs[-1])

    out = pl.pallas_call(
        body, name=name,
        out_shape=(pltpu.SemaphoreType.DMA((n_copies,)), pltpu.SemaphoreType.DMA((n_copies,)),
                   *[pltpu.HBM(b.shape, b.dtype) for b in bufs], pltpu.HBM((SUBLANES, LANES), F32)),
        in_specs=[_HBM] * nb + [_ANY] * len(after),
        out_specs=(_SEM, _SEM, *[_HBM] * nb, pl.BlockSpec(memory_space=pltpu.VMEM)),
        input_output_aliases={i: 2 + i for i in range(nb)},
        compiler_params=pltpu.CompilerParams(has_side_effects=_DATAFLOW),
    )(*[pltpu.with_memory_space_constraint(b, pltpu.HBM) for b in bufs], *after)
    return _Exchange(copies, ns, out[0], out[1], list(out[2:2 + nb]), out[-1])


def _exchange_wait(name, ex, after):
    nb, ns = len(ex.thru), ex.n_src

    def body(*refs):
        for cp in ex.copies(refs[:ns], refs[ns:nb], refs[nb], refs[nb + 1]):
            cp.wait_send()
            cp.wait_recv()

    out = pl.pallas_call(
        body, name=name, out_shape=tuple(pltpu.HBM(b.shape, b.dtype) for b in ex.thru),
        in_specs=[_HBM] * nb + [_SEM, _SEM] + [_ANY] * len(after), out_specs=tuple([_HBM] * nb),
        input_output_aliases={i: i for i in range(nb)},
        compiler_params=pltpu.CompilerParams(has_side_effects=_DATAFLOW),
    )(*ex.thru, ex.send_sems, ex.recv_sems, *after)
    return list(out[:ns]), list(out[ns:])


def _halves(ref):
    half = ref.shape[1] // 2
    if half % LANES == 0:
        return ref.at[:, pl.ds(0, half)], ref.at[:, pl.ds(half, half)]
    half = ref.shape[0] // 2
    assert half % (2 * SUBLANES) == 0, ref.shape
    return ref.at[pl.ds(0, half)], ref.at[pl.ds(half, half)]


def _gather_two_route_copies(srcs, lands, send_sems, recv_sems):
    mx, my, mc = _mesh_pos()
    me = 4 * mx + 2 * my + mc
    return [_remote(_halves(srcs[a])[h], _halves(lands[a].at[me])[h], send_sems, recv_sems, 4 * a + i, _flip_peer(flip))
            for a in range(len(srcs)) for i, (flip, h) in enumerate(((2, 0), (1, 1), (2, 1), (1, 0)))]


def _to_sibling(land, flip, h, send_sems, recv_sems, k):
    mx, my, mc = _mesh_pos()
    part = _halves(land.at[2 * jnp.bitwise_xor(2 * mx + my, flip) + mc])[h]
    return _remote(part, part, send_sems, recv_sems, k, (mx, my, 1 - mc))


def _second_hop(land, send_sems, recv_sems, k):
    mx, my, mc = _mesh_pos()
    k0 = 2 * mx + my
    from_y = _halves(land.at[2 * jnp.bitwise_xor(k0, 1) + mc])[1]
    from_x = _halves(land.at[2 * jnp.bitwise_xor(k0, 2) + mc])[0]
    return (_remote(from_y, from_y, send_sems, recv_sems, k, _flip_peer(2)),
            _remote(from_x, from_x, send_sems, recv_sems, k + 1, _flip_peer(1)))


def _relay_call(name, body, bufs, sems, n_new, after):
    nb, n_in = len(bufs), len(bufs) + len(sems) + len(after)

    def call_body(*refs):
        body(refs[:nb], refs[nb:nb + len(sems)], refs[n_in], refs[n_in + 1])
        refs[-1][...] = jnp.zeros_like(refs[-1])

    out = pl.pallas_call(
        call_body, name=name,
        out_shape=(pltpu.SemaphoreType.DMA((n_new,)), pltpu.SemaphoreType.DMA((n_new,)),
                   *[pltpu.HBM(b.shape, b.dtype) for b in bufs], pltpu.HBM((SUBLANES, LANES), F32)),
        in_specs=[_HBM] * nb + [_SEM] * len(sems) + [_ANY] * len(after),
        out_specs=(_SEM, _SEM, *[_HBM] * nb, pl.BlockSpec(memory_space=pltpu.VMEM)),
        input_output_aliases={i: 2 + i for i in range(nb)},
        compiler_params=pltpu.CompilerParams(has_side_effects=_DATAFLOW),
    )(*bufs, *sems, *after)
    return out[0], out[1], list(out[2:2 + nb]), out[-1]


def _gather_relay(ex, tag, after_first):
    ns = ex.n_src
    n = len(ex.thru) - ns

    def first(bufs, sems, send, recv):
        lands, first_hop = bufs[ns:], ex.copies(bufs[:ns], bufs[ns:], *sems)
        for a in range(n):
            for h in (0, 1):
                _to_sibling(lands[a], 0, h, send, recv, 10 * a + h).start()
        for a in range(n):
            x_first, y_second, x_second, y_first = first_hop[4 * a:4 * a + 4]
            to_x, to_y = _second_hop(lands[a], send, recv, 10 * a + 8)
            x_first.wait_recv()
            to_y.start()
            _to_sibling(lands[a], 2, 0, send, recv, 10 * a + 4).start()
            y_second.wait_recv()
            to_x.start()
            _to_sibling(lands[a], 1, 1, send, recv, 10 * a + 3).start()
            x_second.wait_recv()
            _to_sibling(lands[a], 2, 1, send, recv, 10 * a + 5).start()
            y_first.wait_recv()
            _to_sibling(lands[a], 1, 0, send, recv, 10 * a + 2).start()
        for cp in first_hop:
            cp.wait_send()

    def second(lands, sems, send, recv):
        for a in range(n):
            to_x, to_y = _second_hop(lands[a], *sems, 10 * a + 8)
            to_y.wait_recv()
            _to_sibling(lands[a], 3, 0, send, recv, 2 * a).start()
            to_x.wait_recv()
            _to_sibling(lands[a], 3, 1, send, recv, 2 * a + 1).start()
            to_x.wait_send()
            to_y.wait_send()

    def last(lands, sems, send, recv):
        for a in range(n):
            for flip in range(4):
                for h in (0, 1):
                    s, r, k = (sems[2], sems[3], 2 * a + h) if flip == 3 else (sems[0], sems[1], 10 * a + 2 * flip + h)
                    cp = _to_sibling(lands[a], flip, h, s, r, k)
                    cp.wait_send()
                    cp.wait_recv()

    send1, recv1, bufs, token = _relay_call(f"{tag}_chips_relay", first, ex.thru, [ex.send_sems, ex.recv_sems], 10 * n,
                                            after_first)

    def run_second(after):
        send2, recv2, lands, token = _relay_call(f"{tag}_diagonal_relay", second, bufs[ns:], [send1, recv1], 2 * n, after)
        return token, lambda after_last: _relay_call(f"{tag}_sibling_wait", last, lands, [send1, recv1, send2, recv2],
                                                     1, after_last)[2]

    return token, run_second


def _col_tile(r, c, block_bytes=2**20):
    return next(t for t in (1024, 512, 256, 128) if c % t == 0 and (r * t * 4 <= block_bytes or t == 128))


def _add_sibling(g4, recv, pos, name):
    _, _, r, c = g4.shape
    tc = _col_tile(r, c, 2**21)

    def body(pos_ref, g_ref, r_ref, o16_ref, mine_ref):
        s = g_ref[0, 0] + r_ref[0]
        o16_ref[0] = s.astype(BF16)

        @pl.when(pl.program_id(1) == pos_ref[1])
        def _():
            mine_ref[...] = s

    slot = pl.BlockSpec((1, r, tc), lambda j, k, pos_ref: (k, 0, j))
    return pl.pallas_call(
        body, name=name,
        out_shape=[pltpu.HBM((4, r, c), BF16), pltpu.HBM((r, c), F32)],
        grid_spec=pltpu.PrefetchScalarGridSpec(
            num_scalar_prefetch=1, grid=(c // tc, 4),
            in_specs=[pl.BlockSpec((1, 1, r, tc), lambda j, k, pos_ref: (k, pos_ref[0], 0, j)), slot],
            out_specs=[slot, pl.BlockSpec((r, tc), lambda j, k, pos_ref: (0, j))]),
        compiler_params=pltpu.CompilerParams(
            dimension_semantics=("parallel", "arbitrary"), vmem_limit_bytes=_vmem_limit(4 * _nbytes((r, tc), F32))),
    )(pos, *_in_hbm(g4, recv))


class _ReduceScatter:
    def __init__(self, tag, grads_t, pos):
        self.tag, self.pos, self.names = tag, pos, list(grads_t)
        g4s = [g.reshape(4, 2, g.size // (N_DEV * g.shape[-1]), g.shape[-1]) for g in grads_t.values()]
        lands = [lax.empty((4,) + g.shape[2:], F32) for g in g4s]
        self.ex = _exchange_start(f"rs_{tag}_sibling_start", _scatter_sibling_copies, g4s, lands, 4 * len(g4s))
        self.token = self.ex.token

    def start_chips(self, after):
        g4s, from_sibling = _exchange_wait(f"rs_{self.tag}_sibling_wait", self.ex, after)
        parts = [_add_sibling(g4, rv, self.pos, f"rs_add_sibling_{k}")
                 for k, g4, rv in zip(self.names, g4s, from_sibling)]
        self.mine = [mine for _, mine in parts]
        p16s = [p16 for p16, _ in parts]
        lands = [lax.empty((3,) + p.shape[1:], BF16) for p in p16s]
        self.ex = _exchange_start(f"rs_{self.tag}_chips_start", _scatter_chips_copies, p16s, lands, 3 * len(p16s))
        self.token = self.ex.token

    def finish(self, after):
        _, from_chips = _exchange_wait(f"rs_{self.tag}_chips_wait", self.ex, after)
        return dict(zip(self.names, zip(self.mine, from_chips)))


def _rope_tables():
    positions = np.arange(SEQ, dtype=np.float32)
    inv_freq = np.power(np.float32(ROPE_THETA), -np.arange(0, ROPE_DIM, 2, dtype=np.float32) / np.float32(ROPE_DIM))
    ang = (positions[:, None] * inv_freq[None, :]).astype(np.float32)
    cos, sin = np.cos(ang).astype(np.float32), np.sin(ang).astype(np.float32)
    ones = np.ones((SEQ, HEAD_DIM - ROPE_DIM), np.float32)
    zeros8 = np.zeros((SEQ, ROPE_HALF), np.float32)
    zeros = np.zeros((SEQ, HEAD_DIM - ROPE_DIM), np.float32)
    c_head = np.concatenate([cos, cos, ones], axis=1)
    s1_head = np.concatenate([-sin, zeros8, zeros], axis=1)
    s2_head = np.concatenate([zeros8, sin, zeros], axis=1)
    return tuple(jnp.asarray(np.concatenate([t, t], axis=1)) for t in (c_head, s1_head, s2_head))


def _rope_apply(x, c, s1, s2):
    w = x.shape[1]
    return x * c + pltpu.roll(x, w - ROPE_HALF, 1) * s1 + pltpu.roll(x, ROPE_HALF, 1) * s2


def _rope_apply_t(dy, c, s1, s2):
    w = dy.shape[1]
    return dy * c + pltpu.roll(dy * s1, ROPE_HALF, 1) + pltpu.roll(dy * s2, w - ROPE_HALF, 1)


def _dil_prev_limit(has_prev):
    return jnp.where(has_prev, 0, BLOCK)


def _dil_valid(limit):
    row = lax.broadcasted_iota(jnp.int32, (BLOCK, 2 * BLOCK), 0)
    col = lax.broadcasted_iota(jnp.int32, (BLOCK, 2 * BLOCK), 1)
    dist = col - row
    return jnp.logical_and(dist >= jnp.where(col < BLOCK, limit, -BLOCK), dist <= BLOCK)


def _upper_half():
    return lax.broadcasted_iota(jnp.int32, (1, LANES), 1) >= HEAD_DIM


def _stack_heads(x):
    upper = _upper_half()
    return jnp.concatenate([jnp.where(upper, 0, x), jnp.where(upper, x, 0)], axis=0)


def _unstack_heads(y):
    n = y.shape[0] // 2
    return jnp.where(_upper_half(), y[n:], y[:n])


def _head_columns(t):
    return jnp.concatenate([t[:, 0:1], t[:, HEAD_DIM:HEAD_DIM + 1]], axis=0)


def _dil_rows(n, d):
    per = N_BLOCKS // d
    r, lb = n // per, n % per

    def rows(b):
        start = b * (BLOCK * d) + r
        return pl.ds(pl.multiple_of(start, BLOCK), BLOCK) if d == 1 else pl.ds(start, BLOCK, stride=d)

    return rows(lb), rows(jnp.maximum(lb - 1, 0)), lb > 0


def _dil_rotate(q_ref, k_ref, c_ref, s1_ref, s2_ref, q_rot, k_rot):
    tabs = (c_ref[...], s1_ref[...], s2_ref[...])
    q_rot[...] = _rope_apply(q_ref[...], *tabs) * QK_SCALE
    k_rot[...] = _rope_apply(k_ref[...], *tabs)


def _dil_specs():
    def col(base):
        return pl.BlockSpec((SEQ, LANES), lambda p: (0, base // LANES + p))

    table = pl.BlockSpec((SEQ, LANES), lambda p: (0, 0))
    return [col(COL_QA), col(COL_KA), col(COL_VA)], [table] * 3


def _store_columns(blocks, dproj_ref, cols, sem):
    copies = [pltpu.make_async_copy(b, dproj_ref.at[:, pl.ds(pl.multiple_of(c * LANES, LANES), LANES)], sem.at[i])
              for i, (b, c) in enumerate(zip(blocks, cols))]
    for cp in copies:
        cp.start()
    for cp in copies:
        cp.wait()


def _dil_window(d, n, k_rot, v_ref):
    rows, prev, has_prev = _dil_rows(n, d)
    kw, vw = k_rot[rows, :].astype(BF16), v_ref[rows, :].astype(BF16)
    if d == N_BLOCKS:
        row = lax.broadcasted_iota(jnp.int32, (BLOCK, BLOCK), 0)
        valid = lax.broadcasted_iota(jnp.int32, (BLOCK, BLOCK), 1) <= row
    else:
        kw = jnp.concatenate([k_rot[prev, :].astype(BF16), kw], axis=0)
        vw = jnp.concatenate([v_ref[prev, :].astype(BF16), vw], axis=0)
        valid = _dil_valid(_dil_prev_limit(has_prev))
    return rows, prev, kw, vw, jnp.concatenate([valid, valid], axis=0)


def _dil_fwd(proj, tables, deps=()):
    def body(q_ref, k_ref, v_ref, c_ref, s1_ref, s2_ref, *rest):
        o_ref, lse_ref, q_rot, k_rot = rest[len(deps):]
        upper = _upper_half()
        _dil_rotate(q_ref, k_ref, c_ref, s1_ref, s2_ref, q_rot, k_rot)

        def blocks_of(d):
            def block(n, carry):
                rows, _, kw, vw, valid = _dil_window(d, n, k_rot, v_ref)
                s = jnp.where(valid, _dot_nt(_stack_heads(q_rot[rows, :].astype(BF16)), kw), NEG_INF)
                m = jnp.max(s, axis=-1, keepdims=True)
                p = jnp.exp(s - m)
                den = jnp.sum(p, axis=-1, keepdims=True)
                o_ref[rows, :] = _unstack_heads(_dot_nn((p * (1.0 / den)).astype(BF16), vw))
                lse = m + jnp.log(den)
                lse_ref[rows, :] = jnp.where(upper, lse[BLOCK:], lse[:BLOCK])
                return carry

            lax.fori_loop(0, N_BLOCKS, block, 0, unroll=4)

        for g, d in enumerate(DILATIONS):
            pl.when(pl.program_id(0) // 2 == g)(functools.partial(blocks_of, d))

    qkv, tabs = _dil_specs()
    out = pl.BlockSpec((SEQ, LANES), lambda p: (0, p))
    return pl.pallas_call(
        body, name="dil_attn_fwd", grid=(DIL_WIDTH // LANES,), in_specs=qkv + tabs + [_ANY] * len(deps),
        out_specs=[out, out],
        out_shape=[pltpu.HBM((SEQ, DIL_WIDTH), F32)] * 2,
        scratch_shapes=[pltpu.VMEM((SEQ, LANES), F32)] * 2,
        compiler_params=pltpu.CompilerParams(dimension_semantics=("parallel",)),
    )(*_in_hbm(proj, proj, proj, *tables), *deps)


def _dil_bwd(proj, tables, do, lse, c, dproj, deps=()):
    def body(q_ref, k_ref, v_ref, c_ref, s1_ref, s2_ref, do_ref, lse_ref, cc_ref, dproj_in, *rest):
        dproj_ref, dq_acc, dk_acc, dv_acc, dq_out, dk_out, dv_out, q_rot, k_rot, sem = rest[len(deps):]
        dk_acc[...] = jnp.zeros_like(dk_acc)
        dv_acc[...] = jnp.zeros_like(dv_acc)
        _dil_rotate(q_ref, k_ref, c_ref, s1_ref, s2_ref, q_rot, k_rot)

        def blocks_of(d):
            def block(n, carry):
                rows, prev, kw, vw, valid = _dil_window(d, n, k_rot, v_ref)
                q2 = _stack_heads(q_rot[rows, :].astype(BF16))
                do2 = _stack_heads(do_ref[rows, :].astype(BF16))
                lse_col, c_col = _head_columns(lse_ref[rows, :]), _head_columns(cc_ref[rows, :])
                p = jnp.where(valid, jnp.exp(_dot_nt(q2, kw) - lse_col), 0.0)
                ds = (p * (_dot_nt(do2, vw) + c_col)).astype(BF16)
                dk, dv = _dot_tn(ds, q2), _dot_tn(p.astype(BF16), do2)
                dq_acc[rows, :] = _unstack_heads(_dot_nn(ds, kw)) * QK_SCALE
                if d == N_BLOCKS:
                    dk_acc[rows, :] += dk
                    dv_acc[rows, :] += dv
                else:
                    dk_acc[prev, :] += dk[:BLOCK]
                    dv_acc[prev, :] += dv[:BLOCK]
                    dk_acc[rows, :] += dk[BLOCK:]
                    dv_acc[rows, :] += dv[BLOCK:]
                return carry

            lax.fori_loop(0, N_BLOCKS, block, 0, unroll=4)

        pair = pl.program_id(0)
        for g, d in enumerate(DILATIONS):
            pl.when(pair // 2 == g)(functools.partial(blocks_of, d))
        tabs = (c_ref[...], s1_ref[...], s2_ref[...])
        dq_out[...] = _rope_apply_t(dq_acc[...], *tabs).astype(BF16)
        dk_out[...] = _rope_apply_t(dk_acc[...], *tabs).astype(BF16)
        dv_out[...] = dv_acc[...].astype(BF16)
        _store_columns((dq_out, dk_out, dv_out), dproj_ref,
                       [base // LANES + pair for base in (COL_QA, COL_KA, COL_VA)], sem)

    qkv, tabs = _dil_specs()
    tok = pl.BlockSpec((SEQ, LANES), lambda p: (0, p))
    return pl.pallas_call(
        body, name="dil_attn_bwd", grid=(DIL_WIDTH // LANES,),
        in_specs=qkv + tabs + [tok, tok, tok, _ANY] + [_ANY] * len(deps), out_specs=_ANY,
        out_shape=pltpu.HBM(dproj.shape, dproj.dtype),
        scratch_shapes=[pltpu.VMEM((SEQ, LANES), F32)] * 3 + [pltpu.VMEM((SEQ, LANES), BF16)] * 3
        + [pltpu.VMEM((SEQ, LANES), F32)] * 2 + [pltpu.SemaphoreType.DMA((3,))],
        input_output_aliases={9: 0},
        compiler_params=pltpu.CompilerParams(dimension_semantics=("arbitrary",)),
    )(*_in_hbm(proj, proj, proj, *tables, do, lse, c, dproj), *deps)


def _group_weights(l0, l1, l2):
    m = jnp.maximum(jnp.maximum(l0, l1), l2)
    e0, e1, e2 = jnp.exp(l0 - m), jnp.exp(l1 - m), jnp.exp(l2 - m)
    tot = e0 + e1 + e2
    return e0 / tot, e1 / tot, e2 / tot


def _dil_combine(o, lse, deps=()):
    def fn(o0, o1, o2, l0, l1, l2):
        w0, w1, w2 = _group_weights(l0, l1, l2)
        return w0 * o0 + w1 * o1 + w2 * o2

    w = DIL_OUT_WIDTH
    return _rowwise(fn, "dil_combine", SEQ, 512, [(o, w, g) for g in range(3)] + [(lse, w, g) for g in range(3)], [],
                    [(w, F32)], deps=deps)[0]


def _dil_combine_bwd(d_out, o, lse, deps=()):
    w = DIL_OUT_WIDTH

    def fn(d, o0, o1, o2, l0, l1, l2):
        row = lax.broadcasted_iota(jnp.int32, (w, w), 0) // HEAD_DIM
        col = lax.broadcasted_iota(jnp.int32, (w, w), 1) // HEAD_DIM
        same_head = jnp.where(row == col, 1.0, 0.0).astype(BF16)
        ws = _group_weights(l0, l1, l2)
        dws = [_dot3_nn(d * og, same_head) for og in (o0, o1, o2)]
        mean = ws[0] * dws[0] + ws[1] * dws[1] + ws[2] * dws[2]
        return jnp.concatenate([wg * d for wg in ws], axis=1), jnp.concatenate([-wg * mean for wg in ws], axis=1)

    return _rowwise(fn, "dil_combine_bwd", SEQ, 512,
                    [(d_out, w, 0)] + [(o, w, g) for g in range(3)] + [(lse, w, g) for g in range(3)], [],
                    [(DIL_WIDTH, F32)] * 2, deps=deps)


def _log1p(e):
    u = 1.0 + e
    return jnp.where(u == 1.0, e, jnp.log(u) * (e / (u - 1.0)))


def _fox_gate(proj, b_pad, deps=()):
    def body(f_ref, b_ref, *rest):
        o_ref = rest[-1]
        z = f_ref[...] + b_ref[...]
        logf = (jnp.minimum(z, 0.0) - _log1p(jnp.exp(-jnp.abs(z)))).T[:F_ROWS]
        row = lax.broadcasted_iota(jnp.int32, (BLOCK, BLOCK), 0)
        col = lax.broadcasted_iota(jnp.int32, (BLOCK, BLOCK), 1)
        before = jnp.where(row <= col, 1.0, 0.0).astype(BF16)
        carry = jnp.zeros((F_ROWS, 1), F32)
        for blk in range(N_BLOCKS):
            run = _dot3_nn(logf[:, blk * BLOCK:(blk + 1) * BLOCK], before) + carry
            o_ref[:, blk * BLOCK:(blk + 1) * BLOCK] = run
            carry = run[:, BLOCK - 1:BLOCK]

    return pl.pallas_call(
        body, name="fox_gate", grid=(1,),
        in_specs=[pl.BlockSpec((SEQ, LANES), lambda i: (0, COL_F // LANES)), pl.BlockSpec((1, LANES), lambda i: (0, 0))]
        + [_ANY] * len(deps),
        out_specs=pl.BlockSpec((F_ROWS, SEQ), lambda i: (0, 0)),
        out_shape=pltpu.HBM((F_ROWS, SEQ), F32),
    )(*_in_hbm(proj, b_pad), *deps)


def _fox_gate_bwd(d_cum, proj, b_pad, dproj):
    def body(d_ref, f_ref, b_ref, dproj_ref, dz_ref, db_ref):
        row = lax.broadcasted_iota(jnp.int32, (BLOCK, BLOCK), 0)
        col = lax.broadcasted_iota(jnp.int32, (BLOCK, BLOCK), 1)
        after = jnp.where(row >= col, 1.0, 0.0).astype(BF16)
        carry = jnp.zeros((F_ROWS, 1), F32)
        parts = [None] * N_BLOCKS
        for blk in reversed(range(N_BLOCKS)):
            run = _dot3_nn(d_ref[:, blk * BLOCK:(blk + 1) * BLOCK], after) + carry
            parts[blk] = run
            carry = run[:, 0:1]
        dlogf = jnp.concatenate(parts, axis=1)
        dlogf = jnp.concatenate([dlogf, jnp.zeros((LANES - F_ROWS, SEQ), F32)], axis=0).T
        dz = dlogf * _sigmoid(-(f_ref[...] + b_ref[...]))
        dz_ref[...] = dz.astype(BF16)
        db_ref[...] = jnp.sum(dz, axis=0, keepdims=True)

    f_cols = pl.BlockSpec((SEQ, LANES), lambda i: (0, COL_F // LANES))
    return pl.pallas_call(
        body, name="fox_gate_bwd", grid=(1,),
        in_specs=[pl.BlockSpec((F_ROWS, SEQ), lambda i: (0, 0)), f_cols, pl.BlockSpec((1, LANES), lambda i: (0, 0)), _ANY],
        out_specs=[f_cols, pl.BlockSpec((1, LANES), lambda i: (0, 0))],
        out_shape=[pltpu.HBM(dproj.shape, dproj.dtype), pltpu.HBM((1, LANES), F32)],
        input_output_aliases={3: 0},
    )(*_in_hbm(d_cum, proj, b_pad, dproj))


FOX_TILE = 256
FOX_TILES = SEQ // FOX_TILE


def _row_to_col(row):
    n = row.shape[1]
    eye = lax.broadcasted_iota(jnp.int32, (n, n), 0) == lax.broadcasted_iota(jnp.int32, (n, n), 1)
    return jnp.sum(jnp.where(eye, row, 0.0), axis=1, keepdims=True)


def _fox_bias(f_row, i):
    t = FOX_TILE
    ext = (i + 1) * t
    bias = _row_to_col(f_row[:, i * t:(i + 1) * t]) - f_row[:, :ext]
    row = lax.broadcasted_iota(jnp.int32, (t, ext), 0) + i * t
    col = lax.broadcasted_iota(jnp.int32, (t, ext), 1)
    return bias, col <= row


def _fox_specs():
    qkv = [pl.BlockSpec((SEQ, LANES), lambda p, base=base: (0, base // LANES + p)) for base in (COL_QB, COL_KB, COL_VB)]
    return qkv, pl.BlockSpec((F_ROWS, SEQ), lambda p: (0, 0))


def _fox_fwd(proj, f_rows):
    t = FOX_TILE

    def body(q_ref, k_ref, v_ref, f_ref, o_ref, lse_ref):
        pair = pl.program_id(0)
        upper = _upper_half()
        k16, v16 = k_ref[...].astype(BF16), v_ref[...].astype(BF16)
        f_row = [f_ref[pl.ds(2 * pair + e, 1), :] for e in range(2)]
        for i in range(FOX_TILES):
            ext = (i + 1) * t
            q_tile = (q_ref[i * t:(i + 1) * t, :] * QK_SCALE).astype(BF16)
            s2 = _dot_nt(_stack_heads(q_tile), k16[:ext])
            pns, lses = [], []
            for e in range(2):
                bias, causal = _fox_bias(f_row[e], i)
                s = jnp.where(causal, s2[e * t:(e + 1) * t] + bias, NEG_INF)
                m = jnp.max(s, axis=-1, keepdims=True)
                p = jnp.exp(s - m)
                den = jnp.sum(p, axis=-1, keepdims=True)
                pns.append((p * (1.0 / den)).astype(BF16))
                lses.append(m + jnp.log(den))
            o_ref[i * t:(i + 1) * t, :] = _unstack_heads(_dot_nn(jnp.concatenate(pns, axis=0), v16[:ext]))
            lse_ref[i * t:(i + 1) * t, :] = jnp.where(upper, lses[1], lses[0])

    qkv, f_spec = _fox_specs()
    tok = pl.BlockSpec((SEQ, LANES), lambda p: (0, p))
    return pl.pallas_call(
        body, name="fox_attn_fwd", grid=(FOX_WIDTH // LANES,),
        in_specs=qkv + [f_spec], out_specs=[tok, tok],
        out_shape=[pltpu.HBM((SEQ, FOX_WIDTH), F32)] * 2,
        compiler_params=pltpu.CompilerParams(
            dimension_semantics=("parallel",), vmem_limit_bytes=_vmem_limit(8 * t * SEQ * 4)),
    )(*_in_hbm(proj, proj, proj, f_rows))


def _fox_bwd(proj, do, lse, f_rows, dproj):
    t = FOX_TILE

    def body(q_ref, k_ref, v_ref, f_ref, do_ref, lse_ref, dproj_in, dproj_ref, df_ref, dk_acc, dv_acc,
             dq_out, dk_out, dv_out, sem):
        pair = pl.program_id(0)
        upper = _upper_half()
        k16, v16 = k_ref[...].astype(BF16), v_ref[...].astype(BF16)
        f_row = [f_ref[pl.ds(2 * pair + e, 1), :] for e in range(2)]
        dk_acc[...] = jnp.zeros_like(dk_acc)
        dv_acc[...] = jnp.zeros_like(dv_acc)
        df_ref[...] = jnp.zeros_like(df_ref)
        for i in range(FOX_TILES):
            ext = (i + 1) * t
            q_tile = (q_ref[i * t:(i + 1) * t, :] * QK_SCALE).astype(BF16)
            do_tile = do_ref[i * t:(i + 1) * t, :]
            lse_t = lse_ref[i * t:(i + 1) * t, :]
            q2, do2 = _stack_heads(q_tile), _stack_heads(do_tile)
            s2, dp2 = _dot_nt(q2, k16[:ext]), _dot_nt(do2, v16[:ext])
            ps, dss = [], []
            for e in range(2):
                bias, causal = _fox_bias(f_row[e], i)
                s = s2[e * t:(e + 1) * t] + bias
                p = jnp.where(causal, jnp.exp(s - lse_t[:, e * HEAD_DIM:e * HEAD_DIM + 1]), 0.0)
                dp = dp2[e * t:(e + 1) * t]
                ds = p * (dp - jnp.sum(p * dp, axis=-1, keepdims=True))
                df_ref[0, e:e + 1, :ext] -= jnp.sum(ds, axis=0, keepdims=True)
                ps.append(p.astype(BF16))
                dss.append(ds.astype(BF16))
            ds2, p2 = jnp.concatenate(dss, axis=0), jnp.concatenate(ps, axis=0)
            dq_out[i * t:(i + 1) * t, :] = (_unstack_heads(_dot_nn(ds2, k16[:ext])) * QK_SCALE).astype(BF16)
            dk_acc[:ext, :] += _dot_tn(ds2, q2)
            dv_acc[:ext, :] += _dot_tn(p2, do2)
        dk_out[...] = dk_acc[...].astype(BF16)
        dv_out[...] = dv_acc[...].astype(BF16)
        _store_columns((dq_out, dk_out, dv_out), dproj_ref, [base // LANES + pair for base in (COL_QB, COL_KB, COL_VB)],
                       sem)

    qkv, f_spec = _fox_specs()
    tok = pl.BlockSpec((SEQ, LANES), lambda p: (0, p))
    return pl.pallas_call(
        body, name="fox_attn_bwd", grid=(FOX_WIDTH // LANES,),
        in_specs=qkv + [f_spec, tok, tok, _ANY],
        out_specs=[_ANY, pl.BlockSpec((1, SUBLANES, SEQ), lambda p: (p, 0, 0))],
        out_shape=[pltpu.HBM(dproj.shape, dproj.dtype),
                   pltpu.HBM((FOX_WIDTH // LANES, SUBLANES, SEQ), F32)],
        scratch_shapes=[pltpu.VMEM((SEQ, LANES), F32)] * 2 + [pltpu.VMEM((SEQ, LANES), BF16)] * 3
        + [pltpu.SemaphoreType.DMA((3,))],
        input_output_aliases={6: 0},
        compiler_params=pltpu.CompilerParams(
            dimension_semantics=("arbitrary",), vmem_limit_bytes=_vmem_limit(10 * t * SEQ * 4)),
    )(*_in_hbm(proj, proj, proj, f_rows, do, lse, dproj))


MIX_TILE = 512


def _mix_out(out_a, out_b, proj, x, wt_pa, wt_pb, w_out, g_post, g_ffn_pre):
    tm = MIX_TILE

    def body(a_ref, b_ref, ga_ref, gb_ref, x_ref, wpa_ref, wpb_ref, wo_ref, g2_ref, g3_ref,
             merged_ref, mix_ref, x1_ref, h2_ref):
        ya = _dot_nn(a_ref[...].astype(BF16), wpa_ref[...])
        yb = _dot_nn(b_ref[...].astype(BF16), wpb_ref[...])
        merged = (_sigmoid(ga_ref[...]) * ya + _sigmoid(gb_ref[...]) * yb).astype(BF16)
        merged_ref[...] = merged
        mix = _dot_nn(merged, wo_ref[...])
        mix_ref[...] = mix
        x1 = x_ref[...] + mix * _rms_scale(mix) * g2_ref[...]
        x1_ref[...] = x1
        h2_ref[...] = (x1 * _rms_scale(x1) * g3_ref[...]).astype(BF16)

    def rows(w, cb=0):
        return pl.BlockSpec((tm, w), lambda i, cb=cb: (i, cb))

    def whole(a):
        return pl.BlockSpec(a.shape, lambda i: (0, 0))

    d = D_MODEL
    blk = _nbytes((tm, d), F32) * 6 + sum(_nbytes(a.shape, BF16) for a in (wt_pa, wt_pb, w_out))
    return pl.pallas_call(
        body, name="mix_out", grid=(SEQ // tm,),
        in_specs=[rows(DIL_OUT_WIDTH), rows(FOX_WIDTH), rows(d, COL_GA // d), rows(d, COL_GB // d), rows(d),
                  whole(wt_pa), whole(wt_pb), whole(w_out), whole(g_post), whole(g_ffn_pre)],
        out_specs=[rows(d)] * 4,
        out_shape=[pltpu.HBM((SEQ, d), dt) for dt in (BF16, F32, F32, BF16)],
        compiler_params=pltpu.CompilerParams(dimension_semantics=("parallel",), vmem_limit_bytes=_vmem_limit(blk)),
    )(*_in_hbm(out_a, out_b, proj, proj, x, wt_pa, wt_pb, w_out, g_post, g_ffn_pre))


def _mix_out_bwd(dmix, out_a, out_b, proj, wt_pa, wt_pb, w_out, deps=()):
    tm = MIX_TILE

    def body(dm_ref, a_ref, b_ref, ga_ref, gb_ref, wpa_ref, wpb_ref, wo_ref, *rest):
        dproj_ref, dya_ref, dyb_ref, da_ref, db_ref = rest[len(deps):]
        dmerged = _dot_nt(dm_ref[...], wo_ref[...])
        ya = _dot_nn(a_ref[...].astype(BF16), wpa_ref[...])
        yb = _dot_nn(b_ref[...].astype(BF16), wpb_ref[...])
        sa, sb = _sigmoid(ga_ref[...]), _sigmoid(gb_ref[...])
        dproj_ref[:, COL_GA:COL_GA + D_MODEL] = (dmerged * ya * (sa * (1.0 - sa))).astype(BF16)
        dproj_ref[:, COL_GB:COL_GB + D_MODEL] = (dmerged * yb * (sb * (1.0 - sb))).astype(BF16)
        dproj_ref[:, COL_GB + D_MODEL:] = jnp.zeros((tm, COL_QA - COL_GB - D_MODEL), BF16)
        dya = (dmerged * sa).astype(BF16)
        dyb = (dmerged * sb).astype(BF16)
        dya_ref[...] = dya
        dyb_ref[...] = dyb
        da_ref[...] = _dot_nt(dya, wpa_ref[...])
        db_ref[...] = _dot_nt(dyb, wpb_ref[...]).astype(BF16)

    def rows(w, cb=0):
        return pl.BlockSpec((tm, w), lambda i, cb=cb: (i, cb))

    def whole(a):
        return pl.BlockSpec(a.shape, lambda i: (0, 0))

    d = D_MODEL
    blk = _nbytes((tm, d), F32) * 8 + sum(_nbytes(a.shape, BF16) for a in (wt_pa, wt_pb, w_out))
    return pl.pallas_call(
        body, name="mix_out_bwd", grid=(SEQ // tm,),
        in_specs=[rows(d), rows(DIL_OUT_WIDTH), rows(FOX_WIDTH), rows(d, COL_GA // d), rows(d, COL_GB // d),
                  whole(wt_pa), whole(wt_pb), whole(w_out)] + [_ANY] * len(deps),
        out_specs=[rows(COL_QA)] + [rows(d)] * 2 + [rows(DIL_OUT_WIDTH), rows(FOX_WIDTH)],
        out_shape=[pltpu.HBM((SEQ, PROJ_COLS), BF16)] + [pltpu.HBM((SEQ, d), BF16)] * 2
        + [pltpu.HBM((SEQ, DIL_OUT_WIDTH), F32), pltpu.HBM((SEQ, FOX_WIDTH), BF16)],
        compiler_params=pltpu.CompilerParams(dimension_semantics=("parallel",), vmem_limit_bytes=_vmem_limit(blk)),
    )(*_in_hbm(dmix, out_a, out_b, proj, proj, wt_pa, wt_pb, w_out), *deps)


FFN_TM, FFN_TN = 2048, 256


def _ffn_up(h2, wt_gate, wt_up):
    tm, tn = FFN_TM, FFN_TN

    def body(h_ref, wg_ref, wu_ref, gate_ref, up_ref, act_ref):
        for rows in (slice(0, tm // 2), slice(tm // 2, tm)):
            gate = _dot_nt(h_ref[rows, :], wg_ref[...])
            up = _dot_nt(h_ref[rows, :], wu_ref[...])
            gate_ref[rows, :] = gate
            up_ref[rows, :] = up
            act_ref[rows, :] = (gate * _sigmoid(gate) * up).astype(BF16)

    tile = pl.BlockSpec((tm, tn), lambda i, j: (i, j))
    w_spec = pl.BlockSpec((tn, D_MODEL), lambda i, j: (j, 0))
    return pl.pallas_call(
        body, name="ffn_up", grid=(SEQ // tm, D_FF // tn),
        in_specs=[pl.BlockSpec((tm, D_MODEL), lambda i, j: (i, 0)), w_spec, w_spec],
        out_specs=[tile, tile, tile],
        out_shape=[pltpu.HBM((SEQ, D_FF), dt) for dt in (F32, F32, BF16)],
        compiler_params=pltpu.CompilerParams(
            dimension_semantics=("parallel", "parallel"), vmem_limit_bytes=_vmem_limit(8 * 2**20)),
    )(h2, wt_gate, wt_up)


def _ffn_act_bwd(dff, w_down, gate, up):
    tm, tn = FFN_TM, FFN_TN

    def body(d_ref, wd_ref, gate_ref, up_ref, dgate_ref, dup_ref):
        for rows in (slice(0, tm // 2), slice(tm // 2, tm)):
            dact = _dot_nt(d_ref[rows, :], wd_ref[...])
            gate = gate_ref[rows, :]
            sg = _sigmoid(gate)
            dgate_ref[rows, :] = (dact * up_ref[rows, :] * (sg * (1.0 + gate * (1.0 - sg)))).astype(BF16)
            dup_ref[rows, :] = (dact * (gate * sg)).astype(BF16)

    tile = pl.BlockSpec((tm, tn), lambda i, j: (i, j))
    return pl.pallas_call(
        body, name="ffn_act_bwd", grid=(SEQ // tm, D_FF // tn),
        in_specs=[pl.BlockSpec((tm, D_MODEL), lambda i, j: (i, 0)), pl.BlockSpec((tn, D_MODEL), lambda i, j: (j, 0)),
                  tile, tile],
        out_specs=[tile, tile],
        out_shape=[pltpu.HBM((SEQ, D_FF), BF16)] * 2,
        compiler_params=pltpu.CompilerParams(
            dimension_semantics=("parallel", "parallel"), vmem_limit_bytes=_vmem_limit(8 * 2**20)),
    )(dff, w_down, gate, up)


EPILOGUE_TM = 512


def _loss_head(act, w_down, x1, target, g_post):
    def fn(ff, x1, tgt, g):
        r = _rms_scale(ff)
        nrm = ff * r
        err = (x1 + nrm * g) - tgt
        loss = 0.5 * jnp.sum(jnp.mean(err * err, axis=-1, keepdims=True), axis=0, keepdims=True)
        dy = err * (1.0 / D_MODEL)
        u = dy * g
        dff = r * u - ff * (r * r * r) * jnp.mean(u * ff, axis=-1, keepdims=True)
        return dy, dff, jnp.broadcast_to(loss, (1, LANES)), jnp.sum(dy * nrm, axis=0, keepdims=True)

    d = D_MODEL
    return _matmul_rowwise([(act, w_down)], fn, "ffn_down_loss", EPILOGUE_TM, [(x1, d, 0), (target, d, 0)], [g_post],
                           [(d, F32), (d, BF16)], [LANES, d])


def _post_ffn_bwd(dgate, wt_gate, dup, wt_up, x1, dy, mix, g_ffn_pre, g_mix_post, deps=()):
    def fn(dh2, x1, dy, mix, g3, g2):
        dx, dg3 = _rms_bwd(x1, dh2, g3)
        dx1 = dy + dx
        dmix, dg2 = _rms_bwd(mix, dx1, g2)
        return dx1, dmix, dg3, dg2

    d = D_MODEL
    return _matmul_rowwise([(dgate, wt_gate), (dup, wt_up)], fn, "ffn_up_bwd", EPILOGUE_TM,
                           [(x1, d, 0), (dy, d, 0), (mix, d, 0)], [g_ffn_pre, g_mix_post],
                           [(d, F32), (d, BF16)], [d, d], deps=deps)


def _input_bwd(dproj, wt_r, x, dx1, g_pre, deps=()):
    def fn(dh, x, dx1, g):
        dx, dg = _rms_bwd(x, dh, g)
        return dx1 + dx, dg

    d = D_MODEL
    return _matmul_rowwise([(dproj, wt_r)], fn, "in_proj_bwd", EPILOGUE_TM, [(x, d, 0), (dx1, d, 0)], [g_pre],
                           [(d, F32)], [d], deps=deps)


def _adam_math(w, g, m, v):
    m = ADAM_B1 * m + (1.0 - ADAM_B1) * g
    v = ADAM_B2 * v + (1.0 - ADAM_B2) * (g * g)
    m_hat = m / (1.0 - ADAM_B1 ** ADAM_STEP)
    v_hat = v / (1.0 - ADAM_B2 ** ADAM_STEP)
    delta = -ADAM_LR * (m_hat / (jnp.sqrt(v_hat) + ADAM_EPS) + ADAM_WD * w)
    return delta, m, v


def _adam(w, mine, recv, m, v, name):
    r, c = w.shape
    tc = _col_tile(r, c)

    def body(w_ref, p_ref, r_ref, m_ref, v_ref, g_ref, d_ref, nm_ref, nv_ref):
        g = ((p_ref[...] + r_ref[0].astype(F32)) + r_ref[1].astype(F32)) + r_ref[2].astype(F32)
        g_ref[...] = g
        d_ref[...], nm_ref[...], nv_ref[...] = _adam_math(w_ref[...], g, m_ref[...], v_ref[...])

    spec = pl.BlockSpec((r, tc), lambda j: (0, j))
    return pl.pallas_call(
        body, name=name, grid=(c // tc,),
        in_specs=[spec, spec, pl.BlockSpec((3, r, tc), lambda j: (0, 0, j)), spec, spec], out_specs=[spec] * 4,
        out_shape=[pltpu.HBM((r, c), F32)] * 4,
        compiler_params=pltpu.CompilerParams(dimension_semantics=("parallel",)),
    )(*_in_hbm(w, mine, recv, m, v))


def _adam_small(gathered, ws, ms, vs, loss_parts):
    n = len(ws)

    def body(*refs):
        outs = refs[4 * n + 1:]
        loss = refs[4 * n][0]
        for dev in range(1, N_DEV):
            loss = loss + refs[4 * n][dev]
        outs[4 * n][...] = loss
        for i in range(n):
            ga_ref, w_ref, m_ref, v_ref = (refs[j * n + i] for j in range(4))
            g = ga_ref[0]
            for dev in range(1, N_DEV):
                g = g + ga_ref[dev]
            g = g[:, :w_ref.shape[1]]
            outs[4 * i][...] = g
            outs[4 * i + 1][...], outs[4 * i + 2][...], outs[4 * i + 3][...] = _adam_math(
                w_ref[...], g, m_ref[...], v_ref[...])

    out_shape = [pltpu.HBM(w.shape, F32) for w in ws for _ in range(4)]
    out_shape.append(pltpu.HBM((1, LANES), F32))
    out = pl.pallas_call(body, name="adam_small", out_shape=out_shape)(*gathered, *ws, *ms, *vs, loss_parts)
    return [out[4 * i:4 * i + 4] for i in range(n)], out[4 * n]


_PROJ_SEGMENTS = ((3848, 5896), (None, COL_QA - 2 * D_MODEL), (0, 3840), (3840, 3848), (None, PROJ_COLS - COL_F - 8))


def _proj_weight_t(gathered):
    pieces, zeros, at = [], [], 0
    for lo, hi in _PROJ_SEGMENTS:
        if lo is None:
            zeros.append((at, hi))
            at += hi
            continue
        for dev in range(lo // IN_SHARD, (hi - 1) // IN_SHARD + 1):
            a, b = max(lo, dev * IN_SHARD), min(hi, (dev + 1) * IN_SHARD)
            pieces.append((dev, a - dev * IN_SHARD, at + a - lo, b - a))
        at += hi - lo
    assert at == PROJ_COLS
    tc = 2 * LANES

    def body(g_ref, o_ref, shards, rows):
        for dev in range(N_DEV):
            shards[dev] = g_ref[dev].astype(F32)
        for dev, src, dst, n in pieces:
            rows[pl.ds(dst, n), :] = shards[dev, pl.ds(src, n), :]
        for dst, n in zeros:
            rows[pl.ds(dst, n), :] = jnp.zeros((n, tc), F32)
        o_ref[...] = rows[...].astype(o_ref.dtype)

    return pl.pallas_call(
        body, name="w_in_rows", grid=(D_MODEL // tc,),
        in_specs=[pl.BlockSpec((N_DEV, IN_SHARD, tc), lambda j: (0, 0, j))],
        out_specs=pl.BlockSpec((PROJ_COLS, tc), lambda j: (0, j)),
        out_shape=pltpu.HBM((PROJ_COLS, D_MODEL), gathered.dtype),
        scratch_shapes=[pltpu.VMEM((N_DEV, IN_SHARD, tc), F32), pltpu.VMEM((PROJ_COLS, tc), F32)],
        compiler_params=pltpu.CompilerParams(
            dimension_semantics=("parallel",), vmem_limit_bytes=_vmem_limit(2 * _nbytes((PROJ_COLS, tc), F32))),
    )(*_in_hbm(gathered))


def _proj_weight_grad_slots(dwt_r):
    starts, at = [], 0
    for lo, hi in _PROJ_SEGMENTS:
        if lo is not None:
            starts.append((lo, hi, at))
        at += hi if lo is None else hi - lo
    pieces = []
    for dev in range(N_DEV):
        lo, end = dev * IN_SHARD, (dev + 1) * IN_SHARD
        for seg_lo, seg_hi, seg_at in sorted(starts):
            a, b = max(lo, seg_lo), min(end, seg_hi)
            if a < b:
                pieces.append((dev, a - lo, seg_at + a - seg_lo, b - a))

    def body(g_ref, o_ref):
        for dev, dst, src, rows in pieces:
            o_ref[dev, pl.ds(dst, rows), :] = g_ref[pl.ds(src, rows), :]

    tc = 2 * LANES
    return pl.pallas_call(
        body, name="grad_w_in_slots", grid=(D_MODEL // tc,),
        in_specs=[pl.BlockSpec((PROJ_COLS, tc), lambda j: (0, j))],
        out_specs=pl.BlockSpec((N_DEV, IN_SHARD, tc), lambda j: (0, 0, j)),
        out_shape=pltpu.HBM((N_DEV, IN_SHARD, D_MODEL), F32),
        compiler_params=pltpu.CompilerParams(
            dimension_semantics=("parallel",), vmem_limit_bytes=_vmem_limit(2 * _nbytes((PROJ_COLS, tc), F32))),
    )(*_in_hbm(dwt_r))


def kernel(x, w_in, w_proj_a, w_proj_b, w_out, b_forget, w_ffn_gate, w_ffn_up, w_ffn_down, norm_mix_pre, norm_mix_post, norm_ffn_pre, norm_ffn_post, loss_target, m_w_in, m_w_proj_a, m_w_proj_b, m_w_out, m_b_forget, m_w_ffn_gate, m_w_ffn_up, m_w_ffn_down, m_norm_mix_pre, m_norm_mix_post, m_norm_ffn_pre, m_norm_ffn_post, v_w_in, v_w_proj_a, v_w_proj_b, v_w_out, v_b_forget, v_w_ffn_gate, v_w_ffn_up, v_w_ffn_down, v_norm_mix_pre, v_norm_mix_post, v_norm_ffn_pre, v_norm_ffn_post):
    d = D_MODEL
    names = ("w_in", "w_proj_a", "w_proj_b", "w_out", "w_ffn_gate", "w_ffn_up", "w_ffn_down")
    col_sharded = ("w_in", "w_ffn_gate", "w_ffn_up")

    def row_shards(arrs):
        return {k: (a[0].T if k in col_sharded else a[0]) for k, a in zip(names, arrs)}

    shards = row_shards((w_in, w_proj_a, w_proj_b, w_out, w_ffn_gate, w_ffn_up, w_ffn_down))
    moments_m = row_shards((m_w_in, m_w_proj_a, m_w_proj_b, m_w_out, m_w_ffn_gate, m_w_ffn_up, m_w_ffn_down))
    moments_v = row_shards((v_w_in, v_w_proj_a, v_w_proj_b, v_w_out, v_w_ffn_gate, v_w_ffn_up, v_w_ffn_down))
    pos = jnp.stack([lax.axis_index("c"), 2 * lax.axis_index("x") + lax.axis_index("y")]).astype(jnp.int32)
    x2, target = x[0], loss_target[0]

    me = 4 * lax.axis_index("x") + 2 * lax.axis_index("y") + lax.axis_index("c")
    mid_names, ffn_names = names[1:4], names[4:]
    first_names, later_names = names[:1], names[1:]
    shards16 = {k: shards[k].astype(BF16) for k in names}

    def landing(k):
        return lax.dynamic_update_slice(lax.empty((N_DEV,) + shards[k].shape, BF16), shards16[k][None], (me, 0, 0))

    ag_first = _exchange_start("ag_first_chips_start", _gather_two_route_copies, [shards16[k] for k in first_names],
                               [landing(k) for k in first_names], 4 * len(first_names))
    h = _rowwise(lambda xb, g: xb * _rms_scale(xb) * g, "norm_mix_pre", SEQ, 512, [(x2, d, 0)], [norm_mix_pre],
                 [(d, BF16)], deps=[ag_first.token])[0]
    later_lands = [landing(k) for k in later_names]
    _, ag_first_diagonal = _gather_relay(ag_first, "ag_first", [h, shards["w_in"], moments_m["w_in"], moments_v["w_in"],
                                                               *later_lands, *[shards16[k] for k in later_names]])
    relayed, ag_first_last = ag_first_diagonal([])
    ag_later = _exchange_start("ag_later_chips_start", _gather_two_route_copies, [shards16[k] for k in later_names],
                               later_lands, 4 * len(later_names), after=[relayed])
    gathered = dict(zip(first_names, ag_first_last([ag_later.token])))
    wt_r = _proj_weight_t(gathered["w_in"])

    proj = _matmul([(h, *_in_hbm(wt_r))], "nt", F32, "in_proj", 1024, 896, 1024)
    tables = _rope_tables()
    relayed, ag_later_diagonal = _gather_relay(ag_later, "ag_later", [proj])
    o_dil, lse_dil = _dil_fwd(proj, tables, deps=[relayed])
    out_a = _dil_combine(o_dil, lse_dil)
    relayed, ag_later_last = ag_later_diagonal([out_a])

    b_pad = jnp.pad(b_forget, ((0, 0), (0, LANES - N_FOX_HEADS)))
    f_rows = _fox_gate(proj, b_pad, deps=[relayed])
    out_b, lse_fox = _fox_fwd(proj, f_rows)

    gathered = dict(zip(later_names, ag_later_last([out_b])))
    wt_pa = gathered["w_proj_a"].transpose(1, 0, 2).reshape(DIL_OUT_WIDTH, d)
    wt_pb = gathered["w_proj_b"].transpose(1, 0, 2).reshape(FOX_WIDTH, d)
    w_o = gathered["w_out"].reshape(d, d)
    wt_g = gathered["w_ffn_gate"].reshape(D_FF, d)
    wt_u = gathered["w_ffn_up"].reshape(D_FF, d)
    w_d = gathered["w_ffn_down"].reshape(D_FF, d)
    merged, mix, x1, h2 = _mix_out(out_a, out_b, proj, x2, wt_pa, wt_pb, w_o, norm_mix_post, norm_ffn_pre)

    gate, up, act = _ffn_up(h2, wt_g, wt_u)
    dy, dff, loss_part, dg_ffn_post = _loss_head(act, w_d, x1, target, norm_ffn_post)

    dgate, dup = _ffn_act_bwd(dff, w_d, gate, up)
    grads_t = {}
    grads_t["w_ffn_down"] = _matmul([(act, dff)], "tn", F32, "grad_w_ffn_down", 1408, 512, 2048, staged=False)
    grads_t["w_ffn_gate"] = _matmul([(dgate, h2)], "tn", F32, "grad_w_ffn_gate", 1408, 512, 2048)
    grads_t["w_ffn_up"] = _matmul([(dup, h2)], "tn", F32, "grad_w_ffn_up", 1408, 512, 2048)
    rs_ffn = _ReduceScatter("ffn", {k: grads_t[k] for k in ffn_names}, pos)
    dx1, dmix, dg_ffn_pre, dg_mix_post = _post_ffn_bwd(dgate, wt_g, dup, wt_u, x1, dy, mix, norm_ffn_pre, norm_mix_post,
                                                       deps=[rs_ffn.token])
    rs_ffn.start_chips([dmix])

    dproj, dya, dyb, d_out_a, d_out_b = _mix_out_bwd(dmix, out_a, out_b, proj, wt_pa, wt_pb, w_o, deps=[rs_ffn.token])
    grads_t["w_out"] = _matmul([(merged, dmix)], "tn", F32, "grad_w_out", 1024, 1024, 1024)
    def column_slots(g):
        return g.reshape(g.shape[0], N_DEV, LANES).transpose(1, 0, 2)

    grads_t["w_proj_a"] = column_slots(_matmul([(out_a, dya)], "tn", F32, "grad_w_proj_a", DIL_OUT_WIDTH, 1024, SEQ))
    grads_t["w_proj_b"] = column_slots(_matmul([(out_b, dyb)], "tn", F32, "grad_w_proj_b", FOX_WIDTH, 1024, SEQ))
    rs_mid = _ReduceScatter("mid", {k: grads_t[k] for k in mid_names}, pos)

    do_dil, c_dil = _dil_combine_bwd(d_out_a, o_dil, lse_dil, deps=[rs_mid.token])
    rs_mid.start_chips([c_dil])
    dproj, d_cum = _fox_bwd(proj, d_out_b, lse_fox, f_rows, dproj)
    d_cum_rows = jnp.pad(d_cum[:, :2].reshape(N_FOX_HEADS, SEQ), ((0, F_ROWS - N_FOX_HEADS), (0, 0)))
    dproj, db_part = _fox_gate_bwd(d_cum_rows, proj, b_pad, dproj)
    dproj = _dil_bwd(proj, tables, do_dil, lse_dil, c_dil, dproj, deps=[rs_mid.token])

    dwt_r = _matmul([(dproj, h)], "tn", F32, "grad_w_in", 896, 1024, 2048)
    rs_in = _ReduceScatter("in", {"w_in": _proj_weight_grad_slots(dwt_r)}, pos)
    def finish(rs, after):
        return {k: _adam(shards[k], mine, recv, moments_m[k], moments_v[k], "adam_" + k)
                for k, (mine, recv) in rs.finish(after).items()}

    done = finish(rs_ffn, [rs_in.token])
    rs_in.start_chips([done[k][0] for k in ffn_names])
    grad_x, dg_mix_pre = _input_bwd(dproj, wt_r, x2, dx1, norm_mix_pre, deps=[rs_in.token])
    done.update(finish(rs_mid, [grad_x]))

    small_all = _all_gather([dg_mix_pre, dg_mix_post, dg_ffn_pre, dg_ffn_post, db_part, loss_part],
                            "small_grads_all_gather", deps=[done[k][0] for k in mid_names])
    small, loss = _adam_small(small_all[:5], [norm_mix_pre, norm_mix_post, norm_ffn_pre, norm_ffn_post, b_forget],
                              [m_norm_mix_pre, m_norm_mix_post, m_norm_ffn_pre, m_norm_ffn_post, m_b_forget],
                              [v_norm_mix_pre, v_norm_mix_post, v_norm_ffn_pre, v_norm_ffn_post, v_b_forget],
                              small_all[5])

    done.update(finish(rs_in, [small[0][0]]))

    def leaves(i):
        def nat(k):
            a = done[k][i]
            return (a.T if k in col_sharded else a)[None]

        return [nat("w_in"), nat("w_proj_a"), nat("w_proj_b"), nat("w_out"), small[4][i],
                nat("w_ffn_gate"), nat("w_ffn_up"), nat("w_ffn_down"), *[small[r][i] for r in range(4)]]

    return (loss[0, 0], grad_x[None], *leaves(0), *leaves(1), *leaves(2), *leaves(3))
```

```python
import functools
import math

import jax
import jax.numpy as jnp
import numpy as np
from jax import lax
from jax.experimental import pallas as pl
from jax.experimental.pallas import tpu as pltpu

F32 = jnp.float32
BF16 = jnp.bfloat16
MESH = pl.DeviceIdType.MESH

D_MODEL = 1024
SEQ = 2048
HEAD_DIM = 64
BLOCK = 128
N_BLOCKS = SEQ // BLOCK
DILATIONS = (1, 4, 16)
N_FOX_HEADS = 8
DIL_WIDTH = 768
DIL_OUT_WIDTH = 256
FOX_WIDTH = 512
D_FF = 2816
ROPE_THETA = 500000.0
ROPE_DIM = HEAD_DIM // 4
ROPE_HALF = ROPE_DIM // 2
EPS = 1e-6
NEG_INF = -1e30
QK_SCALE = 1.0 / math.sqrt(HEAD_DIM)
IN_COLS = 5896
N_DEV = 8
IN_SHARD = IN_COLS // N_DEV

ADAM_LR = 0.001
ADAM_B1 = 0.9
ADAM_B2 = 0.999
ADAM_EPS = 1e-08
ADAM_WD = 0.01
ADAM_STEP = 10

V7X_VMEM_BYTES = 64 * 2**20
LANES = 128
SUBLANES = 8

PROJ_COLS = 6272
COL_GA, COL_GB = 0, 1024
COL_QA, COL_KA, COL_VA = 2304, 3072, 3840
COL_QB, COL_KB, COL_VB = 4608, 5120, 5632
COL_F = 6144
F_ROWS = 16


def _vmem_limit(block_bytes):
    want = 2 * block_bytes + 16 * 2**20
    return int(min(max(want, 32 * 2**20), V7X_VMEM_BYTES - 8 * 2**20))


def _nbytes(shape, dtype):
    return math.prod(shape) * jnp.dtype(dtype).itemsize


def _in_hbm(*arrays):
    return [pltpu.with_memory_space_constraint(a, pltpu.HBM) for a in arrays]


def _dot(a, b, dims):
    return lax.dot_general(a, b, (dims, ((), ())), preferred_element_type=F32)


def _dot_nn(a, b):
    return _dot(a, b, ((1,), (0,)))


def _dot_nt(a, b):
    return _dot(a, b, ((1,), (1,)))


def _dot_tn(a, b):
    return _dot(a, b, ((0,), (0,)))


def _sigmoid(z):
    return 1.0 / (1.0 + jnp.exp(-z))


def _split3(x):
    hi = x.astype(BF16)
    r1 = x - hi.astype(F32)
    mid = r1.astype(BF16)
    lo = (r1 - mid.astype(F32)).astype(BF16)
    return hi, mid, lo


def _dot3_nn(x, ones_matrix):
    hi, mid, lo = _split3(x)
    return (_dot_nn(hi, ones_matrix) + _dot_nn(mid, ones_matrix)) + _dot_nn(lo, ones_matrix)


def _rowwise(fn, name, n_rows, tm, row_ins, bcast_ins, row_outs, acc_outs=(), deps=()):
    n_in = len(row_ins) + len(bcast_ins)
    n_ro = len(row_outs)

    def body(*refs):
        res = fn(*[r[...] for r in refs[:n_in]])
        if not isinstance(res, (tuple, list)):
            res = (res,)
        outs = refs[n_in + len(deps):]
        for r, o in zip(res[:n_ro], outs[:n_ro]):
            o[...] = r.astype(o.dtype)
        first = pl.program_id(0) == 0
        for r, o in zip(res[n_ro:], outs[n_ro:]):
            _accumulate(o, r, first)

    in_specs = [pl.BlockSpec((tm, w), lambda i, cb=cb: (i, cb)) for _, w, cb in row_ins]
    in_specs += [pl.BlockSpec(a.shape, lambda i: (0, 0)) for a in bcast_ins]
    in_specs += [pl.BlockSpec(memory_space=pl.ANY)] * len(deps)
    out_specs = [pl.BlockSpec((tm, w), lambda i: (i, 0)) for w, _ in row_outs]
    out_specs += [pl.BlockSpec((1, w), lambda i: (0, 0)) for w in acc_outs]
    out_shape = [pltpu.HBM((n_rows, w), dt) for w, dt in row_outs]
    out_shape += [pltpu.HBM((1, w), F32) for w in acc_outs]
    blk = sum(_nbytes((tm, w), a.dtype) for a, w, _ in row_ins) + sum(_nbytes((tm, w), dt) for w, dt in row_outs)
    return pl.pallas_call(
        body, name=name, grid=(n_rows // tm,), in_specs=in_specs, out_specs=out_specs, out_shape=out_shape,
        compiler_params=pltpu.CompilerParams(
            dimension_semantics=("arbitrary" if acc_outs else "parallel",), vmem_limit_bytes=_vmem_limit(3 * blk)),
    )(*_in_hbm(*[a for a, _, _ in row_ins], *bcast_ins), *deps)


def _accumulate(o_ref, part, first):
    @pl.when(first)
    def _():
        o_ref[...] = part

    @pl.when(jnp.logical_not(first))
    def _():
        o_ref[...] += part


_MM_DIMS = {"nn": ((1,), (0,)), "nt": ((1,), (1,)), "tn": ((0,), (0,))}


def _matmul(pairs, mode, out_dtype, name, tm, tn, tk, deps=(), staged=True):
    a0, b0 = pairs[0]
    if mode == "tn":
        kk, m = a0.shape
    else:
        m, kk = a0.shape
    n = b0.shape[0] if mode == "nt" else b0.shape[1]
    assert m % tm == 0 and n % tn == 0 and kk % tk == 0, (name, m, n, kk)
    nk = kk // tk
    n_pairs = len(pairs)
    dims = _MM_DIMS[mode]
    n_in = 2 * n_pairs + len(deps)

    def body(*refs):
        o_ref = refs[n_in]
        part = None
        for p in range(n_pairs):
            d = _dot(refs[2 * p][...].astype(BF16), refs[2 * p + 1][...].astype(BF16), dims)
            part = d if part is None else part + d
        if nk == 1:
            o_ref[...] = part.astype(o_ref.dtype)
            return
        acc = refs[n_in + 1]
        k = pl.program_id(2)

        @pl.when(k == 0)
        def _():
            acc[...] = part

        @pl.when(k > 0)
        def _():
            acc[...] += part

        @pl.when(k == nk - 1)
        def _():
            o_ref[...] = acc[...].astype(o_ref.dtype)

    if mode == "tn":
        a_spec = pl.BlockSpec((tk, tm), lambda i, j, k: (k, i))
    else:
        a_spec = pl.BlockSpec((tm, tk), lambda i, j, k: (i, k))
    if mode == "nt":
        b_spec = pl.BlockSpec((tn, tk), lambda i, j, k: (j, k))
    else:
        b_spec = pl.BlockSpec((tk, tn), lambda i, j, k: (k, j))
    blk = sum(_nbytes((tm, tk), a.dtype) + _nbytes((tk, tn), b.dtype) for a, b in pairs) + 2 * _nbytes((tm, tn), F32)
    flat = [a for pair in pairs for a in pair]
    return pl.pallas_call(
        body, name=name, grid=(m // tm, n // tn, nk),
        in_specs=[a_spec, b_spec] * n_pairs + [pl.BlockSpec(memory_space=pl.ANY)] * len(deps),
        out_specs=pl.BlockSpec((tm, tn), lambda i, j, k: (i, j)),
        out_shape=pltpu.HBM((m, n), out_dtype),
        scratch_shapes=[] if nk == 1 else [pltpu.VMEM((tm, tn), F32)],
        compiler_params=pltpu.CompilerParams(
            dimension_semantics=("parallel", "parallel", "arbitrary"), vmem_limit_bytes=_vmem_limit(blk)),
    )(*(flat if staged else _in_hbm(*flat)), *deps)


def _matmul_rowwise(pairs, fn, name, tm, row_ins, bcast_ins, row_outs, acc_outs=(), deps=()):
    m = pairs[0][0].shape[0]
    n_mm, n_in = 2 * len(pairs), len(row_ins) + len(bcast_ins)
    n_ro = len(row_outs)

    def body(*refs):
        prod = None
        for p in range(len(pairs)):
            part = _dot_nn(refs[2 * p][...].astype(BF16), refs[2 * p + 1][...].astype(BF16))
            prod = part if prod is None else prod + part
        res = fn(prod, *[r[...] for r in refs[n_mm:n_mm + n_in]])
        outs = refs[n_mm + n_in + len(deps):]
        for r, o in zip(res[:n_ro], outs[:n_ro]):
            o[...] = r.astype(o.dtype)
        first = pl.program_id(0) == 0
        for r, o in zip(res[n_ro:], outs[n_ro:]):
            _accumulate(o, r, first)

    in_specs = []
    for a, b in pairs:
        in_specs += [pl.BlockSpec((tm, a.shape[1]), lambda i: (i, 0)),
                     pl.BlockSpec(b.shape, lambda i: (0, 0), pipeline_mode=pl.Buffered(1))]
    in_specs += [pl.BlockSpec((tm, w), lambda i, cb=cb: (i, cb)) for _, w, cb in row_ins]
    in_specs += [pl.BlockSpec(a.shape, lambda i: (0, 0)) for a in bcast_ins]
    in_specs += [_ANY] * len(deps)
    out_specs = [pl.BlockSpec((tm, w), lambda i: (i, 0)) for w, _ in row_outs]
    out_specs += [pl.BlockSpec((1, w), lambda i: (0, 0)) for w in acc_outs]
    out_shape = [pltpu.HBM((m, w), dt) for w, dt in row_outs]
    out_shape += [pltpu.HBM((1, w), F32) for w in acc_outs]
    blk = sum(_nbytes((tm, a.shape[1]), a.dtype) + _nbytes(b.shape, b.dtype) // 2 for a, b in pairs)
    blk += sum(_nbytes((tm, w), a.dtype) for a, w, _ in row_ins) + sum(_nbytes((tm, w), dt) for w, dt in row_outs)
    return pl.pallas_call(
        body, name=name, grid=(m // tm,), in_specs=in_specs, out_specs=out_specs, out_shape=out_shape,
        compiler_params=pltpu.CompilerParams(dimension_semantics=("arbitrary",), vmem_limit_bytes=_vmem_limit(blk)),
    )(*[a for pair in pairs for a in pair], *[a for a, _, _ in row_ins], *bcast_ins, *deps)


def _rms_scale(x):
    return lax.rsqrt(jnp.mean(x * x, axis=-1, keepdims=True) + EPS)


def _rms_bwd(xin, dyn, g):
    r = _rms_scale(xin)
    u = dyn * g
    dx = r * u - xin * (r * r * r) * jnp.mean(u * xin, axis=-1, keepdims=True)
    dg = jnp.sum(dyn * xin * r, axis=0, keepdims=True)
    return dx, dg


def _mesh_pos():
    return lax.axis_index("x"), lax.axis_index("y"), lax.axis_index("c")


def _all_gather(xs, name, deps=()):
    n = len(xs)

    def body(*refs):
        x_refs, out_refs = refs[:n], refs[n + len(deps):2 * n + len(deps)]
        send_sems, recv_sems, local_sems = refs[2 * n + len(deps):]
        mx, my, mc = _mesh_pos()
        me, sib = (mx, my, mc), (mx, my, 1 - mc)
        chips = [(1 - mx, my), (mx, 1 - my), (1 - mx, 1 - my)]

        def slot(a, dev):
            px, py, pc = dev
            return out_refs[a].at[4 * px + 2 * py + pc]

        def copy(k, a, block, to, src=None):
            return pltpu.make_async_remote_copy(
                src_ref=slot(a, block) if src is None else src, dst_ref=slot(a, block),
                send_sem=send_sems.at[a * 7 + k], recv_sem=recv_sems.at[a * 7 + k],
                device_id=to, device_id_type=MESH)

        mine = [pltpu.make_async_copy(x_refs[a], slot(a, me), local_sems.at[a]) for a in range(n)]
        for cp in mine:
            cp.start()
        first = []
        for a in range(n):
            first.append(copy(0, a, me, sib, x_refs[a]))
            first += [copy(1 + j, a, me, (*chip, mc), x_refs[a]) for j, chip in enumerate(chips)]
        for cp in first:
            cp.start()
        passed = []
        for a in range(n):
            for j, chip in enumerate(chips):
                copy(1 + j, a, (*chip, mc), me).wait_recv()
                fwd = copy(4 + j, a, (*chip, mc), sib)
                fwd.start()
                passed.append(fwd)
        for a in range(n):
            copy(0, a, sib, me).wait_recv()
            for j, chip in enumerate(chips):
                copy(4 + j, a, (*chip, 1 - mc), me).wait_recv()
        for cp in first + passed:
            cp.wait_send()
        for cp in mine:
            cp.wait()

    hbm = pl.BlockSpec(memory_space=pl.ANY)
    return pl.pallas_call(
        body, name=name,
        out_shape=[pltpu.HBM((N_DEV,) + x.shape, x.dtype) for x in xs],
        in_specs=[hbm] * (n + len(deps)), out_specs=[hbm] * n,
        scratch_shapes=[pltpu.SemaphoreType.DMA((7 * n,)), pltpu.SemaphoreType.DMA((7 * n,)),
                        pltpu.SemaphoreType.DMA((n,))],
    )(*xs, *deps)


_HBM = pl.BlockSpec(memory_space=pltpu.HBM)
_SEM = pl.BlockSpec(memory_space=pltpu.SEMAPHORE)
_ANY = pl.BlockSpec(memory_space=pl.ANY)
_DATAFLOW = pltpu.SideEffectType.DATAFLOW_SIDE_EFFECTING


def _flip_peer(flip):
    mx, my, mc = _mesh_pos()
    return (1 - mx if flip & 2 else mx, 1 - my if flip & 1 else my, mc)


def _remote(src, dst, send_sems, recv_sems, k, peer):
    return pltpu.make_async_remote_copy(src_ref=src, dst_ref=dst, send_sem=send_sems.at[k], recv_sem=recv_sems.at[k],
                                        device_id=peer, device_id_type=MESH)


def _scatter_sibling_copies(srcs, lands, send_sems, recv_sems):
    mx, my, mc = _mesh_pos()
    return [_remote(srcs[a].at[k, 1 - mc], lands[a].at[k], send_sems, recv_sems, 4 * a + k, (mx, my, 1 - mc))
            for a in range(len(srcs)) for k in range(4)]


def _scatter_chips_copies(srcs, lands, send_sems, recv_sems):
    mx, my, _ = _mesh_pos()
    k0 = 2 * mx + my
    return [_remote(srcs[a].at[jnp.bitwise_xor(k0, flip)], lands[a].at[flip - 1], send_sems, recv_sems,
                    3 * a + flip - 1, _flip_peer(flip))
            for a in range(len(srcs)) for flip in (1, 2, 3)]


class _Exchange:
    def __init__(self, copies, n_src, send_sems, recv_sems, thru, token):
        self.copies, self.n_src, self.send_sems, self.recv_sems, self.thru, self.token = (
            copies, n_src, send_sems, recv_sems, thru, token)


def _exchange_start(name, copies, srcs, lands, n_copies, after=()):
    bufs = list(srcs) + list(lands)
    nb, ns = len(bufs), len(srcs)

    def body(*refs):
        send_sems, recv_sems = refs[nb + len(after)], refs[nb + len(after) + 1]
        for cp in copies(refs[:ns], refs[ns:nb], send_sems, recv_sems):
            cp.start()
        refs[-1][...] = jnp.zeros_like(refs[-1])

    out = pl.pallas_call(
        body, name=name,
        out_shape=(pltpu.SemaphoreType.DMA((n_copies,)), pltpu.SemaphoreType.DMA((n_copies,)),
                   *[pltpu.HBM(b.shape, b.dtype) for b in bufs], pltpu.HBM((SUBLANES, LANES), F32)),
        in_specs=[_HBM] * nb + [_ANY] * len(after),
        out_specs=(_SEM, _SEM, *[_HBM] * nb, pl.BlockSpec(memory_space=pltpu.VMEM)),
        input_output_aliases={i: 2 + i for i in range(nb)},
        compiler_params=pltpu.CompilerParams(has_side_effects=_DATAFLOW),
    )(*[pltpu.with_memory_space_constraint(b, pltpu.HBM) for b in bufs], *after)
    return _Exchange(copies, ns, out[0], out[1], list(out[2:2 + nb]), out[-1])


def _exchange_wait(name, ex, after):
    nb, ns = len(ex.thru), ex.n_src

    def body(*refs):
        for cp in ex.copies(refs[:ns], refs[ns:nb], refs[nb], refs[nb + 1]):
            cp.wait_send()
            cp.wait_recv()

    out = pl.pallas_call(
        body, name=name, out_shape=tuple(pltpu.HBM(b.shape, b.dtype) for b in ex.thru),
        in_specs=[_HBM] * nb + [_SEM, _SEM] + [_ANY] * len(after), out_specs=tuple([_HBM] * nb),
        input_output_aliases={i: i for i in range(nb)},
        compiler_params=pltpu.CompilerParams(has_side_effects=_DATAFLOW),
    )(*ex.thru, ex.send_sems, ex.recv_sems, *after)
    return list(out[:ns]), list(out[ns:])


def _halves(ref):
    half = ref.shape[1] // 2
    if half % LANES == 0:
        return ref.at[:, pl.ds(0, half)], ref.at[:, pl.ds(half, half)]
    half = ref.shape[0] // 2
    assert half % (2 * SUBLANES) == 0, ref.shape
    return ref.at[pl.ds(0, half)], ref.at[pl.ds(half, half)]


def _gather_two_route_copies(srcs, lands, send_sems, recv_sems):
    mx, my, mc = _mesh_pos()
    me = 4 * mx + 2 * my + mc
    return [_remote(_halves(srcs[a])[h], _halves(lands[a].at[me])[h], send_sems, recv_sems, 4 * a + i, _flip_peer(flip))
            for a in range(len(srcs)) for i, (flip, h) in enumerate(((2, 0), (1, 1), (2, 1), (1, 0)))]


def _to_sibling(land, flip, h, send_sems, recv_sems, k):
    mx, my, mc = _mesh_pos()
    part = _halves(land.at[2 * jnp.bitwise_xor(2 * mx + my, flip) + mc])[h]
    return _remote(part, part, send_sems, recv_sems, k, (mx, my, 1 - mc))


def _second_hop(land, send_sems, recv_sems, k):
    mx, my, mc = _mesh_pos()
    k0 = 2 * mx + my
    from_y = _halves(land.at[2 * jnp.bitwise_xor(k0, 1) + mc])[1]
    from_x = _halves(land.at[2 * jnp.bitwise_xor(k0, 2) + mc])[0]
    return (_remote(from_y, from_y, send_sems, recv_sems, k, _flip_peer(2)),
            _remote(from_x, from_x, send_sems, recv_sems, k + 1, _flip_peer(1)))


def _relay_call(name, body, bufs, sems, n_new, after):
    nb, n_in = len(bufs), len(bufs) + len(sems) + len(after)

    def call_body(*refs):
        body(refs[:nb], refs[nb:nb + len(sems)], refs[n_in], refs[n_in + 1])
        refs[-1][...] = jnp.zeros_like(refs[-1])

    out = pl.pallas_call(
        call_body, name=name,
        out_shape=(pltpu.SemaphoreType.DMA((n_new,)), pltpu.SemaphoreType.DMA((n_new,)),
                   *[pltpu.HBM(b.shape, b.dtype) for b in bufs], pltpu.HBM((SUBLANES, LANES), F32)),
        in_specs=[_HBM] * nb + [_SEM] * len(sems) + [_ANY] * len(after),
        out_specs=(_SEM, _SEM, *[_HBM] * nb, pl.BlockSpec(memory_space=pltpu.VMEM)),
        input_output_aliases={i: 2 + i for i in range(nb)},
        compiler_params=pltpu.CompilerParams(has_side_effects=_DATAFLOW),
    )(*bufs, *sems, *after)
    return out[0], out[1], list(out[2:2 + nb]), out[-1]


def _gather_relay(ex, tag, after_first):
    ns = ex.n_src
    n = len(ex.thru) - ns

    def first(bufs, sems, send, recv):
        lands, first_hop = bufs[ns:], ex.copies(bufs[:ns], bufs[ns:], *sems)
        for a in range(n):
            for h in (0, 1):
                _to_sibling(lands[a], 0, h, send, recv, 10 * a + h).start()
        for a in range(n):
            x_first, y_second, x_second, y_first = first_hop[4 * a:4 * a + 4]
            to_x, to_y = _second_hop(lands[a], send, recv, 10 * a + 8)
            x_first.wait_recv()
            to_y.start()
            _to_sibling(lands[a], 2, 0, send, recv, 10 * a + 4).start()
            y_second.wait_recv()
            to_x.start()
            _to_sibling(lands[a], 1, 1, send, recv, 10 * a + 3).start()
            x_second.wait_recv()
            _to_sibling(lands[a], 2, 1, send, recv, 10 * a + 5).start()
            y_first.wait_recv()
            _to_sibling(lands[a], 1, 0, send, recv, 10 * a + 2).start()
        for cp in first_hop:
            cp.wait_send()

    def second(lands, sems, send, recv):
        for a in range(n):
            to_x, to_y = _second_hop(lands[a], *sems, 10 * a + 8)
            to_y.wait_recv()
            _to_sibling(lands[a], 3, 0, send, recv, 2 * a).start()
            to_x.wait_recv()
            _to_sibling(lands[a], 3, 1, send, recv, 2 * a + 1).start()
            to_x.wait_send()
            to_y.wait_send()

    def last(lands, sems, send, recv):
        for a in range(n):
            for flip in range(4):
                for h in (0, 1):
                    s, r, k = (sems[2], sems[3], 2 * a + h) if flip == 3 else (sems[0], sems[1], 10 * a + 2 * flip + h)
                    cp = _to_sibling(lands[a], flip, h, s, r, k)
                    cp.wait_send()
                    cp.wait_recv()

    send1, recv1, bufs, token = _relay_call(f"{tag}_chips_relay", first, ex.thru, [ex.send_sems, ex.recv_sems], 10 * n,
                                            after_first)

    def run_second(after):
        send2, recv2, lands, token = _relay_call(f"{tag}_diagonal_relay", second, bufs[ns:], [send1, recv1], 2 * n, after)
        return token, lambda after_last: _relay_call(f"{tag}_sibling_wait", last, lands, [send1, recv1, send2, recv2],
                                                     1, after_last)[2]

    return token, run_second


def _col_tile(r, c, block_bytes=2**20):
    return next(t for t in (1024, 512, 256, 128) if c % t == 0 and (r * t * 4 <= block_bytes or t == 128))


def _add_sibling(g4, recv, pos, name):
    _, _, r, c = g4.shape
    tc = _col_tile(r, c, 2**21)

    def body(pos_ref, g_ref, r_ref, o16_ref, mine_ref):
        s = g_ref[0, 0] + r_ref[0]
        o16_ref[0] = s.astype(BF16)

        @pl.when(pl.program_id(1) == pos_ref[1])
        def _():
            mine_ref[...] = s

    slot = pl.BlockSpec((1, r, tc), lambda j, k, pos_ref: (k, 0, j))
    return pl.pallas_call(
        body, name=name,
        out_shape=[pltpu.HBM((4, r, c), BF16), pltpu.HBM((r, c), F32)],
        grid_spec=pltpu.PrefetchScalarGridSpec(
            num_scalar_prefetch=1, grid=(c // tc, 4),
            in_specs=[pl.BlockSpec((1, 1, r, tc), lambda j, k, pos_ref: (k, pos_ref[0], 0, j)), slot],
            out_specs=[slot, pl.BlockSpec((r, tc), lambda j, k, pos_ref: (0, j))]),
        compiler_params=pltpu.CompilerParams(
            dimension_semantics=("parallel", "arbitrary"), vmem_limit_bytes=_vmem_limit(4 * _nbytes((r, tc), F32))),
    )(pos, *_in_hbm(g4, recv))


class _ReduceScatter:
    def __init__(self, tag, grads_t, pos):
        self.tag, self.pos, self.names = tag, pos, list(grads_t)
        g4s = [g.reshape(4, 2, g.size // (N_DEV * g.shape[-1]), g.shape[-1]) for g in grads_t.values()]
        lands = [lax.empty((4,) + g.shape[2:], F32) for g in g4s]
        self.ex = _exchange_start(f"rs_{tag}_sibling_start", _scatter_sibling_copies, g4s, lands, 4 * len(g4s))
        self.token = self.ex.token

    def start_chips(self, after):
        g4s, from_sibling = _exchange_wait(f"rs_{self.tag}_sibling_wait", self.ex, after)
        parts = [_add_sibling(g4, rv, self.pos, f"rs_add_sibling_{k}")
                 for k, g4, rv in zip(self.names, g4s, from_sibling)]
        self.mine = [mine for _, mine in parts]
        p16s = [p16 for p16, _ in parts]
        lands = [lax.empty((3,) + p.shape[1:], BF16) for p in p16s]
        self.ex = _exchange_start(f"rs_{self.tag}_chips_start", _scatter_chips_copies, p16s, lands, 3 * len(p16s))
        self.token = self.ex.token

    def finish(self, after):
        _, from_chips = _exchange_wait(f"rs_{self.tag}_chips_wait", self.ex, after)
        return dict(zip(self.names, zip(self.mine, from_chips)))


def _rope_tables():
    positions = np.arange(SEQ, dtype=np.float32)
    inv_freq = np.power(np.float32(ROPE_THETA), -np.arange(0, ROPE_DIM, 2, dtype=np.float32) / np.float32(ROPE_DIM))
    ang = (positions[:, None] * inv_freq[None, :]).astype(np.float32)
    cos, sin = np.cos(ang).astype(np.float32), np.sin(ang).astype(np.float32)
    ones = np.ones((SEQ, HEAD_DIM - ROPE_DIM), np.float32)
    zeros8 = np.zeros((SEQ, ROPE_HALF), np.float32)
    zeros = np.zeros((SEQ, HEAD_DIM - ROPE_DIM), np.float32)
    c_head = np.concatenate([cos, cos, ones], axis=1)
    s1_head = np.concatenate([-sin, zeros8, zeros], axis=1)
    s2_head = np.concatenate([zeros8, sin, zeros], axis=1)
    return tuple(jnp.asarray(np.concatenate([t, t], axis=1)) for t in (c_head, s1_head, s2_head))


def _rope_apply(x, c, s1, s2):
    w = x.shape[1]
    return x * c + pltpu.roll(x, w - ROPE_HALF, 1) * s1 + pltpu.roll(x, ROPE_HALF, 1) * s2


def _rope_apply_t(dy, c, s1, s2):
    w = dy.shape[1]
    return dy * c + pltpu.roll(dy * s1, ROPE_HALF, 1) + pltpu.roll(dy * s2, w - ROPE_HALF, 1)


def _dil_prev_limit(has_prev):
    return jnp.where(has_prev, 0, BLOCK)


def _dil_valid(limit):
    row = lax.broadcasted_iota(jnp.int32, (BLOCK, 2 * BLOCK), 0)
    col = lax.broadcasted_iota(jnp.int32, (BLOCK, 2 * BLOCK), 1)
    dist = col - row
    return jnp.logical_and(dist >= jnp.where(col < BLOCK, limit, -BLOCK), dist <= BLOCK)


def _upper_half():
    return lax.broadcasted_iota(jnp.int32, (1, LANES), 1) >= HEAD_DIM


def _stack_heads(x):
    upper = _upper_half()
    return jnp.concatenate([jnp.where(upper, 0, x), jnp.where(upper, x, 0)], axis=0)


def _unstack_heads(y):
    n = y.shape[0] // 2
    return jnp.where(_upper_half(), y[n:], y[:n])


def _head_columns(t):
    return jnp.concatenate([t[:, 0:1], t[:, HEAD_DIM:HEAD_DIM + 1]], axis=0)


def _dil_rows(n, d):
    per = N_BLOCKS // d
    r, lb = n // per, n % per

    def rows(b):
        start = b * (BLOCK * d) + r
        return pl.ds(pl.multiple_of(start, BLOCK), BLOCK) if d == 1 else pl.ds(start, BLOCK, stride=d)

    return rows(lb), rows(jnp.maximum(lb - 1, 0)), lb > 0


def _dil_rotate(q_ref, k_ref, c_ref, s1_ref, s2_ref, q_rot, k_rot):
    tabs = (c_ref[...], s1_ref[...], s2_ref[...])
    q_rot[...] = _rope_apply(q_ref[...], *tabs) * QK_SCALE
    k_rot[...] = _rope_apply(k_ref[...], *tabs)


def _dil_specs():
    def col(base):
        return pl.BlockSpec((SEQ, LANES), lambda p: (0, base // LANES + p))

    table = pl.BlockSpec((SEQ, LANES), lambda p: (0, 0))
    return [col(COL_QA), col(COL_KA), col(COL_VA)], [table] * 3


def _store_columns(blocks, dproj_ref, cols, sem):
    copies = [pltpu.make_async_copy(b, dproj_ref.at[:, pl.ds(pl.multiple_of(c * LANES, LANES), LANES)], sem.at[i])
              for i, (b, c) in enumerate(zip(blocks, cols))]
    for cp in copies:
        cp.start()
    for cp in copies:
        cp.wait()


def _dil_window(d, n, k_rot, v_ref):
    rows, prev, has_prev = _dil_rows(n, d)
    kw, vw = k_rot[rows, :].astype(BF16), v_ref[rows, :].astype(BF16)
    if d == N_BLOCKS:
        row = lax.broadcasted_iota(jnp.int32, (BLOCK, BLOCK), 0)
        valid = lax.broadcasted_iota(jnp.int32, (BLOCK, BLOCK), 1) <= row
    else:
        kw = jnp.concatenate([k_rot[prev, :].astype(BF16), kw], axis=0)
        vw = jnp.concatenate([v_ref[prev, :].astype(BF16), vw], axis=0)
        valid = _dil_valid(_dil_prev_limit(has_prev))
    return rows, prev, kw, vw, jnp.concatenate([valid, valid], axis=0)


def _dil_fwd(proj, tables, deps=()):
    def body(q_ref, k_ref, v_ref, c_ref, s1_ref, s2_ref, *rest):
        o_ref, lse_ref, q_rot, k_rot = rest[len(deps):]
        upper = _upper_half()
        _dil_rotate(q_ref, k_ref, c_ref, s1_ref, s2_ref, q_rot, k_rot)

        def blocks_of(d):
            def block(n, carry):
                rows, _, kw, vw, valid = _dil_window(d, n, k_rot, v_ref)
                s = jnp.where(valid, _dot_nt(_stack_heads(q_rot[rows, :].astype(BF16)), kw), NEG_INF)
                m = jnp.max(s, axis=-1, keepdims=True)
                p = jnp.exp(s - m)
                den = jnp.sum(p, axis=-1, keepdims=True)
                o_ref[rows, :] = _unstack_heads(_dot_nn((p * (1.0 / den)).astype(BF16), vw))
                lse = m + jnp.log(den)
                lse_ref[rows, :] = jnp.where(upper, lse[BLOCK:], lse[:BLOCK])
                return carry

            lax.fori_loop(0, N_BLOCKS, block, 0, unroll=4)

        for g, d in enumerate(DILATIONS):
            pl.when(pl.program_id(0) // 2 == g)(functools.partial(blocks_of, d))

    qkv, tabs = _dil_specs()
    out = pl.BlockSpec((SEQ, LANES), lambda p: (0, p))
    return pl.pallas_call(
        body, name="dil_attn_fwd", grid=(DIL_WIDTH // LANES,), in_specs=qkv + tabs + [_ANY] * len(deps),
        out_specs=[out, out],
        out_shape=[pltpu.HBM((SEQ, DIL_WIDTH), F32)] * 2,
        scratch_shapes=[pltpu.VMEM((SEQ, LANES), F32)] * 2,
        compiler_params=pltpu.CompilerParams(dimension_semantics=("parallel",)),
    )(*_in_hbm(proj, proj, proj, *tables), *deps)


def _dil_bwd(proj, tables, do, lse, c, dproj, deps=()):
    def body(q_ref, k_ref, v_ref, c_ref, s1_ref, s2_ref, do_ref, lse_ref, cc_ref, dproj_in, *rest):
        dproj_ref, dq_acc, dk_acc, dv_acc, dq_out, dk_out, dv_out, q_rot, k_rot, sem = rest[len(deps):]
        dk_acc[...] = jnp.zeros_like(dk_acc)
        dv_acc[...] = jnp.zeros_like(dv_acc)
        _dil_rotate(q_ref, k_ref, c_ref, s1_ref, s2_ref, q_rot, k_rot)

        def blocks_of(d):
            def block(n, carry):
                rows, prev, kw, vw, valid = _dil_window(d, n, k_rot, v_ref)
                q2 = _stack_heads(q_rot[rows, :].astype(BF16))
                do2 = _stack_heads(do_ref[rows, :].astype(BF16))
                lse_col, c_col = _head_columns(lse_ref[rows, :]), _head_columns(cc_ref[rows, :])
                p = jnp.where(valid, jnp.exp(_dot_nt(q2, kw) - lse_col), 0.0)
                ds = (p * (_dot_nt(do2, vw) + c_col)).astype(BF16)
                dk, dv = _dot_tn(ds, q2), _dot_tn(p.astype(BF16), do2)
                dq_acc[rows, :] = _unstack_heads(_dot_nn(ds, kw)) * QK_SCALE
                if d == N_BLOCKS:
                    dk_acc[rows, :] += dk
                    dv_acc[rows, :] += dv
                else:
                    dk_acc[prev, :] += dk[:BLOCK]
                    dv_acc[prev, :] += dv[:BLOCK]
                    dk_acc[rows, :] += dk[BLOCK:]
                    dv_acc[rows, :] += dv[BLOCK:]
                return carry

            lax.fori_loop(0, N_BLOCKS, block, 0, unroll=4)

        pair = pl.program_id(0)
        for g, d in enumerate(DILATIONS):
            pl.when(pair // 2 == g)(functools.partial(blocks_of, d))
        tabs = (c_ref[...], s1_ref[...], s2_ref[...])
        dq_out[...] = _rope_apply_t(dq_acc[...], *tabs).astype(BF16)
        dk_out[...] = _rope_apply_t(dk_acc[...], *tabs).astype(BF16)
        dv_out[...] = dv_acc[...].astype(BF16)
        _store_columns((dq_out, dk_out, dv_out), dproj_ref,
                       [base // LANES + pair for base in (COL_QA, COL_KA, COL_VA)], sem)

    qkv, tabs = _dil_specs()
    tok = pl.BlockSpec((SEQ, LANES), lambda p: (0, p))
    return pl.pallas_call(
        body, name="dil_attn_bwd", grid=(DIL_WIDTH // LANES,),
        in_specs=qkv + tabs + [tok, tok, tok, _ANY] + [_ANY] * len(deps), out_specs=_ANY,
        out_shape=pltpu.HBM(dproj.shape, dproj.dtype),
        scratch_shapes=[pltpu.VMEM((SEQ, LANES), F32)] * 3 + [pltpu.VMEM((SEQ, LANES), BF16)] * 3
        + [pltpu.VMEM((SEQ, LANES), F32)] * 2 + [pltpu.SemaphoreType.DMA((3,))],
        input_output_aliases={9: 0},
        compiler_params=pltpu.CompilerParams(dimension_semantics=("arbitrary",)),
    )(*_in_hbm(proj, proj, proj, *tables, do, lse, c, dproj), *deps)


def _group_weights(l0, l1, l2):
    m = jnp.maximum(jnp.maximum(l0, l1), l2)
    e0, e1, e2 = jnp.exp(l0 - m), jnp.exp(l1 - m), jnp.exp(l2 - m)
    tot = e0 + e1 + e2
    return e0 / tot, e1 / tot, e2 / tot


def _dil_combine(o, lse, deps=()):
    def fn(o0, o1, o2, l0, l1, l2):
        w0, w1, w2 = _group_weights(l0, l1, l2)
        return w0 * o0 + w1 * o1 + w2 * o2

    w = DIL_OUT_WIDTH
    return _rowwise(fn, "dil_combine", SEQ, 512, [(o, w, g) for g in range(3)] + [(lse, w, g) for g in range(3)], [],
                    [(w, F32)], deps=deps)[0]


def _dil_combine_bwd(d_out, o, lse, deps=()):
    w = DIL_OUT_WIDTH

    def fn(d, o0, o1, o2, l0, l1, l2):
        row = lax.broadcasted_iota(jnp.int32, (w, w), 0) // HEAD_DIM
        col = lax.broadcasted_iota(jnp.int32, (w, w), 1) // HEAD_DIM
        same_head = jnp.where(row == col, 1.0, 0.0).astype(BF16)
        ws = _group_weights(l0, l1, l2)
        dws = [_dot3_nn(d * og, same_head) for og in (o0, o1, o2)]
        mean = ws[0] * dws[0] + ws[1] * dws[1] + ws[2] * dws[2]
        return jnp.concatenate([wg * d for wg in ws], axis=1), jnp.concatenate([-wg * mean for wg in ws], axis=1)

    return _rowwise(fn, "dil_combine_bwd", SEQ, 512,
                    [(d_out, w, 0)] + [(o, w, g) for g in range(3)] + [(lse, w, g) for g in range(3)], [],
                    [(DIL_WIDTH, F32)] * 2, deps=deps)


def _log1p(e):
    u = 1.0 + e
    return jnp.where(u == 1.0, e, jnp.log(u) * (e / (u - 1.0)))


def _fox_gate(proj, b_pad, deps=()):
    def body(f_ref, b_ref, *rest):
        o_ref = rest[-1]
        z = f_ref[...] + b_ref[...]
        logf = (jnp.minimum(z, 0.0) - _log1p(jnp.exp(-jnp.abs(z)))).T[:F_ROWS]
        row = lax.broadcasted_iota(jnp.int32, (BLOCK, BLOCK), 0)
        col = lax.broadcasted_iota(jnp.int32, (BLOCK, BLOCK), 1)
        before = jnp.where(row <= col, 1.0, 0.0).astype(BF16)
        carry = jnp.zeros((F_ROWS, 1), F32)
        for blk in range(N_BLOCKS):
            run = _dot3_nn(logf[:, blk * BLOCK:(blk + 1) * BLOCK], before) + carry
            o_ref[:, blk * BLOCK:(blk + 1) * BLOCK] = run
            carry = run[:, BLOCK - 1:BLOCK]

    return pl.pallas_call(
        body, name="fox_gate", grid=(1,),
        in_specs=[pl.BlockSpec((SEQ, LANES), lambda i: (0, COL_F // LANES)), pl.BlockSpec((1, LANES), lambda i: (0, 0))]
        + [_ANY] * len(deps),
        out_specs=pl.BlockSpec((F_ROWS, SEQ), lambda i: (0, 0)),
        out_shape=pltpu.HBM((F_ROWS, SEQ), F32),
    )(*_in_hbm(proj, b_pad), *deps)


def _fox_gate_bwd(d_cum, proj, b_pad, dproj):
    def body(d_ref, f_ref, b_ref, dproj_ref, dz_ref, db_ref):
        row = lax.broadcasted_iota(jnp.int32, (BLOCK, BLOCK), 0)
        col = lax.broadcasted_iota(jnp.int32, (BLOCK, BLOCK), 1)
        after = jnp.where(row >= col, 1.0, 0.0).astype(BF16)
        carry = jnp.zeros((F_ROWS, 1), F32)
        parts = [None] * N_BLOCKS
        for blk in reversed(range(N_BLOCKS)):
            run = _dot3_nn(d_ref[:, blk * BLOCK:(blk + 1) * BLOCK], after) + carry
            parts[blk] = run
            carry = run[:, 0:1]
        dlogf = jnp.concatenate(parts, axis=1)
        dlogf = jnp.concatenate([dlogf, jnp.zeros((LANES - F_ROWS, SEQ), F32)], axis=0).T
        dz = dlogf * _sigmoid(-(f_ref[...] + b_ref[...]))
        dz_ref[...] = dz.astype(BF16)
        db_ref[...] = jnp.sum(dz, axis=0, keepdims=True)

    f_cols = pl.BlockSpec((SEQ, LANES), lambda i: (0, COL_F // LANES))
    return pl.pallas_call(
        body, name="fox_gate_bwd", grid=(1,),
        in_specs=[pl.BlockSpec((F_ROWS, SEQ), lambda i: (0, 0)), f_cols, pl.BlockSpec((1, LANES), lambda i: (0, 0)), _ANY],
        out_specs=[f_cols, pl.BlockSpec((1, LANES), lambda i: (0, 0))],
        out_shape=[pltpu.HBM(dproj.shape, dproj.dtype), pltpu.HBM((1, LANES), F32)],
        input_output_aliases={3: 0},
    )(*_in_hbm(d_cum, proj, b_pad, dproj))


FOX_TILE = 256
FOX_TILES = SEQ // FOX_TILE


def _row_to_col(row):
    n = row.shape[1]
    eye = lax.broadcasted_iota(jnp.int32, (n, n), 0) == lax.broadcasted_iota(jnp.int32, (n, n), 1)
    return jnp.sum(jnp.where(eye, row, 0.0), axis=1, keepdims=True)


def _fox_bias(f_row, i):
    t = FOX_TILE
    ext = (i + 1) * t
    bias = _row_to_col(f_row[:, i * t:(i + 1) * t]) - f_row[:, :ext]
    row = lax.broadcasted_iota(jnp.int32, (t, ext), 0) + i * t
    col = lax.broadcasted_iota(jnp.int32, (t, ext), 1)
    return bias, col <= row


def _fox_specs():
    qkv = [pl.BlockSpec((SEQ, LANES), lambda p, base=base: (0, base // LANES + p)) for base in (COL_QB, COL_KB, COL_VB)]
    return qkv, pl.BlockSpec((F_ROWS, SEQ), lambda p: (0, 0))


def _fox_fwd(proj, f_rows):
    t = FOX_TILE

    def body(q_ref, k_ref, v_ref, f_ref, o_ref, lse_ref):
        pair = pl.program_id(0)
        upper = _upper_half()
        k16, v16 = k_ref[...].astype(BF16), v_ref[...].astype(BF16)
        f_row = [f_ref[pl.ds(2 * pair + e, 1), :] for e in range(2)]
        for i in range(FOX_TILES):
            ext = (i + 1) * t
            q_tile = (q_ref[i * t:(i + 1) * t, :] * QK_SCALE).astype(BF16)
            s2 = _dot_nt(_stack_heads(q_tile), k16[:ext])
            pns, lses = [], []
            for e in range(2):
                bias, causal = _fox_bias(f_row[e], i)
                s = jnp.where(causal, s2[e * t:(e + 1) * t] + bias, NEG_INF)
                m = jnp.max(s, axis=-1, keepdims=True)
                p = jnp.exp(s - m)
                den = jnp.sum(p, axis=-1, keepdims=True)
                pns.append((p * (1.0 / den)).astype(BF16))
                lses.append(m + jnp.log(den))
            o_ref[i * t:(i + 1) * t, :] = _unstack_heads(_dot_nn(jnp.concatenate(pns, axis=0), v16[:ext]))
            lse_ref[i * t:(i + 1) * t, :] = jnp.where(upper, lses[1], lses[0])

    qkv, f_spec = _fox_specs()
    tok = pl.BlockSpec((SEQ, LANES), lambda p: (0, p))
    return pl.pallas_call(
        body, name="fox_attn_fwd", grid=(FOX_WIDTH // LANES,),
        in_specs=qkv + [f_spec], out_specs=[tok, tok],
        out_shape=[pltpu.HBM((SEQ, FOX_WIDTH), F32)] * 2,
        compiler_params=pltpu.CompilerParams(
            dimension_semantics=("parallel",), vmem_limit_bytes=_vmem_limit(8 * t * SEQ * 4)),
    )(*_in_hbm(proj, proj, proj, f_rows))


def _fox_bwd(proj, do, lse, f_rows, dproj):
    t = FOX_TILE

    def body(q_ref, k_ref, v_ref, f_ref, do_ref, lse_ref, dproj_in, dproj_ref, df_ref, dk_acc, dv_acc,
             dq_out, dk_out, dv_out, sem):
        pair = pl.program_id(0)
        upper = _upper_half()
        k16, v16 = k_ref[...].astype(BF16), v_ref[...].astype(BF16)
        f_row = [f_ref[pl.ds(2 * pair + e, 1), :] for e in range(2)]
        dk_acc[...] = jnp.zeros_like(dk_acc)
        dv_acc[...] = jnp.zeros_like(dv_acc)
        df_ref[...] = jnp.zeros_like(df_ref)
        for i in range(FOX_TILES):
            ext = (i + 1) * t
            q_tile = (q_ref[i * t:(i + 1) * t, :] * QK_SCALE).astype(BF16)
            do_tile = do_ref[i * t:(i + 1) * t, :]
            lse_t = lse_ref[i * t:(i + 1) * t, :]
            q2, do2 = _stack_heads(q_tile), _stack_heads(do_tile)
            s2, dp2 = _dot_nt(q2, k16[:ext]), _dot_nt(do2, v16[:ext])
            ps, dss = [], []
            for e in range(2):
                bias, causal = _fox_bias(f_row[e], i)
                s = s2[e * t:(e + 1) * t] + bias
                p = jnp.where(causal, jnp.exp(s - lse_t[:, e * HEAD_DIM:e * HEAD_DIM + 1]), 0.0)
                dp = dp2[e * t:(e + 1) * t]
                ds = p * (dp - jnp.sum(p * dp, axis=-1, keepdims=True))
                df_ref[0, e:e + 1, :ext] -= jnp.sum(ds, axis=0, keepdims=True)
                ps.append(p.astype(BF16))
                dss.append(ds.astype(BF16))
            ds2, p2 = jnp.concatenate(dss, axis=0), jnp.concatenate(ps, axis=0)
            dq_out[i * t:(i + 1) * t, :] = (_unstack_heads(_dot_nn(ds2, k16[:ext])) * QK_SCALE).astype(BF16)
            dk_acc[:ext, :] += _dot_tn(ds2, q2)
            dv_acc[:ext, :] += _dot_tn(p2, do2)
        dk_out[...] = dk_acc[...].astype(BF16)
        dv_out[...] = dv_acc[...].astype(BF16)
        _store_columns((dq_out, dk_out, dv_out), dproj_ref, [base // LANES + pair for base in (COL_QB, COL_KB, COL_VB)],
                       sem)

    qkv, f_spec = _fox_specs()
    tok = pl.BlockSpec((SEQ, LANES), lambda p: (0, p))
    return pl.pallas_call(
        body, name="fox_attn_bwd", grid=(FOX_WIDTH // LANES,),
        in_specs=qkv + [f_spec, tok, tok, _ANY],
        out_specs=[_ANY, pl.BlockSpec((1, SUBLANES, SEQ), lambda p: (p, 0, 0))],
        out_shape=[pltpu.HBM(dproj.shape, dproj.dtype),
                   pltpu.HBM((FOX_WIDTH // LANES, SUBLANES, SEQ), F32)],
        scratch_shapes=[pltpu.VMEM((SEQ, LANES), F32)] * 2 + [pltpu.VMEM((SEQ, LANES), BF16)] * 3
        + [pltpu.SemaphoreType.DMA((3,))],
        input_output_aliases={6: 0},
        compiler_params=pltpu.CompilerParams(
            dimension_semantics=("arbitrary",), vmem_limit_bytes=_vmem_limit(10 * t * SEQ * 4)),
    )(*_in_hbm(proj, proj, proj, f_rows, do, lse, dproj))


MIX_TILE = 256


def _mix_out(out_a, out_b, proj, x, wt_pa, wt_pb, w_out, g_post, g_ffn_pre):
    tm = MIX_TILE

    def body(a_ref, b_ref, ga_ref, gb_ref, x_ref, wpa_ref, wpb_ref, wo_ref, g2_ref, g3_ref,
             merged_ref, mix_ref, x1_ref, h2_ref):
        ya = _dot_nn(a_ref[...].astype(BF16), wpa_ref[...])
        yb = _dot_nn(b_ref[...].astype(BF16), wpb_ref[...])
        merged = (_sigmoid(ga_ref[...]) * ya + _sigmoid(gb_ref[...]) * yb).astype(BF16)
        merged_ref[...] = merged
        mix = _dot_nn(merged, wo_ref[...])
        mix_ref[...] = mix
        x1 = x_ref[...] + mix * _rms_scale(mix) * g2_ref[...]
        x1_ref[...] = x1
        h2_ref[...] = (x1 * _rms_scale(x1) * g3_ref[...]).astype(BF16)

    def rows(w, cb=0):
        return pl.BlockSpec((tm, w), lambda i, cb=cb: (i, cb))

    def whole(a):
        return pl.BlockSpec(a.shape, lambda i: (0, 0))

    d = D_MODEL
    blk = _nbytes((tm, d), F32) * 6 + sum(_nbytes(a.shape, BF16) for a in (wt_pa, wt_pb, w_out))
    return pl.pallas_call(
        body, name="mix_out", grid=(SEQ // tm,),
        in_specs=[rows(DIL_OUT_WIDTH), rows(FOX_WIDTH), rows(d, COL_GA // d), rows(d, COL_GB // d), rows(d),
                  whole(wt_pa), whole(wt_pb), whole(w_out), whole(g_post), whole(g_ffn_pre)],
        out_specs=[rows(d)] * 4,
        out_shape=[pltpu.HBM((SEQ, d), dt) for dt in (BF16, F32, F32, BF16)],
        compiler_params=pltpu.CompilerParams(dimension_semantics=("parallel",), vmem_limit_bytes=_vmem_limit(blk)),
    )(*_in_hbm(out_a, out_b, proj, proj, x, wt_pa, wt_pb, w_out, g_post, g_ffn_pre))


def _mix_out_bwd(dmix, out_a, out_b, proj, wt_pa, wt_pb, w_out, deps=()):
    tm = MIX_TILE

    def body(dm_ref, a_ref, b_ref, ga_ref, gb_ref, wpa_ref, wpb_ref, wo_ref, *rest):
        dproj_ref, dya_ref, dyb_ref, da_ref, db_ref = rest[len(deps):]
        dmerged = _dot_nt(dm_ref[...], wo_ref[...])
        ya = _dot_nn(a_ref[...].astype(BF16), wpa_ref[...])
        yb = _dot_nn(b_ref[...].astype(BF16), wpb_ref[...])
        sa, sb = _sigmoid(ga_ref[...]), _sigmoid(gb_ref[...])
        dproj_ref[:, COL_GA:COL_GA + D_MODEL] = (dmerged * ya * (sa * (1.0 - sa))).astype(BF16)
        dproj_ref[:, COL_GB:COL_GB + D_MODEL] = (dmerged * yb * (sb * (1.0 - sb))).astype(BF16)
        dproj_ref[:, COL_GB + D_MODEL:] = jnp.zeros((tm, COL_QA - COL_GB - D_MODEL), BF16)
        dya = (dmerged * sa).astype(BF16)
        dyb = (dmerged * sb).astype(BF16)
        dya_ref[...] = dya
        dyb_ref[...] = dyb
        da_ref[...] = _dot_nt(dya, wpa_ref[...])
        db_ref[...] = _dot_nt(dyb, wpb_ref[...]).astype(BF16)

    def rows(w, cb=0):
        return pl.BlockSpec((tm, w), lambda i, cb=cb: (i, cb))

    def whole(a):
        return pl.BlockSpec(a.shape, lambda i: (0, 0))

    d = D_MODEL
    blk = _nbytes((tm, d), F32) * 8 + sum(_nbytes(a.shape, BF16) for a in (wt_pa, wt_pb, w_out))
    return pl.pallas_call(
        body, name="mix_out_bwd", grid=(SEQ // tm,),
        in_specs=[rows(d), rows(DIL_OUT_WIDTH), rows(FOX_WIDTH), rows(d, COL_GA // d), rows(d, COL_GB // d),
                  whole(wt_pa), whole(wt_pb), whole(w_out)] + [_ANY] * len(deps),
        out_specs=[rows(COL_QA)] + [rows(d)] * 2 + [rows(DIL_OUT_WIDTH), rows(FOX_WIDTH)],
        out_shape=[pltpu.HBM((SEQ, PROJ_COLS), BF16)] + [pltpu.HBM((SEQ, d), BF16)] * 2
        + [pltpu.HBM((SEQ, DIL_OUT_WIDTH), F32), pltpu.HBM((SEQ, FOX_WIDTH), BF16)],
        compiler_params=pltpu.CompilerParams(dimension_semantics=("parallel",), vmem_limit_bytes=_vmem_limit(blk)),
    )(*_in_hbm(dmix, out_a, out_b, proj, proj, wt_pa, wt_pb, w_out), *deps)


FFN_TM, FFN_TN = 2048, 256


def _ffn_up(h2, wt_gate, wt_up):
    tm, tn = FFN_TM, FFN_TN

    def body(h_ref, wg_ref, wu_ref, gate_ref, up_ref, act_ref):
        for rows in (slice(0, tm // 2), slice(tm // 2, tm)):
            gate = _dot_nt(h_ref[rows, :], wg_ref[...])
            up = _dot_nt(h_ref[rows, :], wu_ref[...])
            gate_ref[rows, :] = gate
            up_ref[rows, :] = up
            act_ref[rows, :] = (gate * _sigmoid(gate) * up).astype(BF16)

    tile = pl.BlockSpec((tm, tn), lambda i, j: (i, j))
    w_spec = pl.BlockSpec((tn, D_MODEL), lambda i, j: (j, 0))
    return pl.pallas_call(
        body, name="ffn_up", grid=(SEQ // tm, D_FF // tn),
        in_specs=[pl.BlockSpec((tm, D_MODEL), lambda i, j: (i, 0)), w_spec, w_spec],
        out_specs=[tile, tile, tile],
        out_shape=[pltpu.HBM((SEQ, D_FF), dt) for dt in (F32, F32, BF16)],
        compiler_params=pltpu.CompilerParams(
            dimension_semantics=("parallel", "parallel"), vmem_limit_bytes=_vmem_limit(8 * 2**20)),
    )(h2, wt_gate, wt_up)


def _ffn_act_bwd(dff, w_down, gate, up):
    tm, tn = FFN_TM, FFN_TN

    def body(d_ref, wd_ref, gate_ref, up_ref, dgate_ref, dup_ref):
        for rows in (slice(0, tm // 2), slice(tm // 2, tm)):
            dact = _dot_nt(d_ref[rows, :], wd_ref[...])
            gate = gate_ref[rows, :]
            sg = _sigmoid(gate)
            dgate_ref[rows, :] = (dact * up_ref[rows, :] * (sg * (1.0 + gate * (1.0 - sg)))).astype(BF16)
            dup_ref[rows, :] = (dact * (gate * sg)).astype(BF16)

    tile = pl.BlockSpec((tm, tn), lambda i, j: (i, j))
    return pl.pallas_call(
        body, name="ffn_act_bwd", grid=(SEQ // tm, D_FF // tn),
        in_specs=[pl.BlockSpec((tm, D_MODEL), lambda i, j: (i, 0)), pl.BlockSpec((tn, D_MODEL), lambda i, j: (j, 0)),
                  tile, tile],
        out_specs=[tile, tile],
        out_shape=[pltpu.HBM((SEQ, D_FF), BF16)] * 2,
        compiler_params=pltpu.CompilerParams(
            dimension_semantics=("parallel", "parallel"), vmem_limit_bytes=_vmem_limit(8 * 2**20)),
    )(dff, w_down, gate, up)


EPILOGUE_TM = 512


def _loss_head(act, w_down, x1, target, g_post):
    def fn(ff, x1, tgt, g):
        r = _rms_scale(ff)
        nrm = ff * r
        err = (x1 + nrm * g) - tgt
        loss = 0.5 * jnp.sum(jnp.mean(err * err, axis=-1, keepdims=True), axis=0, keepdims=True)
        dy = err * (1.0 / D_MODEL)
        u = dy * g
        dff = r * u - ff * (r * r * r) * jnp.mean(u * ff, axis=-1, keepdims=True)
        return dy, dff, jnp.broadcast_to(loss, (1, LANES)), jnp.sum(dy * nrm, axis=0, keepdims=True)

    d = D_MODEL
    return _matmul_rowwise([(act, w_down)], fn, "ffn_down_loss", EPILOGUE_TM, [(x1, d, 0), (target, d, 0)], [g_post],
                           [(d, F32), (d, BF16)], [LANES, d])


def _post_ffn_bwd(dgate, wt_gate, dup, wt_up, x1, dy, mix, g_ffn_pre, g_mix_post, deps=()):
    def fn(dh2, x1, dy, mix, g3, g2):
        dx, dg3 = _rms_bwd(x1, dh2, g3)
        dx1 = dy + dx
        dmix, dg2 = _rms_bwd(mix, dx1, g2)
        return dx1, dmix, dg3, dg2

    d = D_MODEL
    return _matmul_rowwise([(dgate, wt_gate), (dup, wt_up)], fn, "ffn_up_bwd", EPILOGUE_TM,
                           [(x1, d, 0), (dy, d, 0), (mix, d, 0)], [g_ffn_pre, g_mix_post],
                           [(d, F32), (d, BF16)], [d, d], deps=deps)


def _input_bwd(dproj, wt_r, x, dx1, g_pre, deps=()):
    def fn(dh, x, dx1, g):
        dx, dg = _rms_bwd(x, dh, g)
        return dx1 + dx, dg

    d = D_MODEL
    return _matmul_rowwise([(dproj, wt_r)], fn, "in_proj_bwd", EPILOGUE_TM, [(x, d, 0), (dx1, d, 0)], [g_pre],
                           [(d, F32)], [d], deps=deps)


def _adam_math(w, g, m, v):
    m = ADAM_B1 * m + (1.0 - ADAM_B1) * g
    v = ADAM_B2 * v + (1.0 - ADAM_B2) * (g * g)
    m_hat = m / (1.0 - ADAM_B1 ** ADAM_STEP)
    v_hat = v / (1.0 - ADAM_B2 ** ADAM_STEP)
    delta = -ADAM_LR * (m_hat / (jnp.sqrt(v_hat) + ADAM_EPS) + ADAM_WD * w)
    return delta, m, v


def _adam(w, mine, recv, m, v, name):
    r, c = w.shape
    tc = _col_tile(r, c)

    def body(w_ref, p_ref, r_ref, m_ref, v_ref, g_ref, d_ref, nm_ref, nv_ref):
        g = ((p_ref[...] + r_ref[0].astype(F32)) + r_ref[1].astype(F32)) + r_ref[2].astype(F32)
        g_ref[...] = g
        d_ref[...], nm_ref[...], nv_ref[...] = _adam_math(w_ref[...], g, m_ref[...], v_ref[...])

    spec = pl.BlockSpec((r, tc), lambda j: (0, j))
    return pl.pallas_call(
        body, name=name, grid=(c // tc,),
        in_specs=[spec, spec, pl.BlockSpec((3, r, tc), lambda j: (0, 0, j)), spec, spec], out_specs=[spec] * 4,
        out_shape=[pltpu.HBM((r, c), F32)] * 4,
        compiler_params=pltpu.CompilerParams(dimension_semantics=("parallel",)),
    )(*_in_hbm(w, mine, recv, m, v))


def _adam_small(gathered, ws, ms, vs, loss_parts):
    n = len(ws)

    def body(*refs):
        outs = refs[4 * n + 1:]
        loss = refs[4 * n][0]
        for dev in range(1, N_DEV):
            loss = loss + refs[4 * n][dev]
        outs[4 * n][...] = loss
        for i in range(n):
            ga_ref, w_ref, m_ref, v_ref = (refs[j * n + i] for j in range(4))
            g = ga_ref[0]
            for dev in range(1, N_DEV):
                g = g + ga_ref[dev]
            g = g[:, :w_ref.shape[1]]
            outs[4 * i][...] = g
            outs[4 * i + 1][...], outs[4 * i + 2][...], outs[4 * i + 3][...] = _adam_math(
                w_ref[...], g, m_ref[...], v_ref[...])

    out_shape = [pltpu.HBM(w.shape, F32) for w in ws for _ in range(4)]
    out_shape.append(pltpu.HBM((1, LANES), F32))
    out = pl.pallas_call(body, name="adam_small", out_shape=out_shape)(*gathered, *ws, *ms, *vs, loss_parts)
    return [out[4 * i:4 * i + 4] for i in range(n)], out[4 * n]


_PROJ_SEGMENTS = ((3848, 5896), (None, COL_QA - 2 * D_MODEL), (0, 3840), (3840, 3848), (None, PROJ_COLS - COL_F - 8))


def _proj_weight_t(gathered):
    pieces, zeros, at = [], [], 0
    for lo, hi in _PROJ_SEGMENTS:
        if lo is None:
            zeros.append((at, hi))
            at += hi
            continue
        for dev in range(lo // IN_SHARD, (hi - 1) // IN_SHARD + 1):
            a, b = max(lo, dev * IN_SHARD), min(hi, (dev + 1) * IN_SHARD)
            pieces.append((dev, a - dev * IN_SHARD, at + a - lo, b - a))
        at += hi - lo
    assert at == PROJ_COLS
    tc = 2 * LANES

    def body(g_ref, o_ref, shards, rows):
        for dev in range(N_DEV):
            shards[dev] = g_ref[dev].astype(F32)
        for dev, src, dst, n in pieces:
            rows[pl.ds(dst, n), :] = shards[dev, pl.ds(src, n), :]
        for dst, n in zeros:
            rows[pl.ds(dst, n), :] = jnp.zeros((n, tc), F32)
        o_ref[...] = rows[...].astype(o_ref.dtype)

    return pl.pallas_call(
        body, name="w_in_rows", grid=(D_MODEL // tc,),
        in_specs=[pl.BlockSpec((N_DEV, IN_SHARD, tc), lambda j: (0, 0, j))],
        out_specs=pl.BlockSpec((PROJ_COLS, tc), lambda j: (0, j)),
        out_shape=pltpu.HBM((PROJ_COLS, D_MODEL), gathered.dtype),
        scratch_shapes=[pltpu.VMEM((N_DEV, IN_SHARD, tc), F32), pltpu.VMEM((PROJ_COLS, tc), F32)],
        compiler_params=pltpu.CompilerParams(
            dimension_semantics=("parallel",), vmem_limit_bytes=_vmem_limit(2 * _nbytes((PROJ_COLS, tc), F32))),
    )(*_in_hbm(gathered))


def _proj_weight_grad_slots(dwt_r):
    starts, at = [], 0
    for lo, hi in _PROJ_SEGMENTS:
        if lo is not None:
            starts.append((lo, hi, at))
        at += hi if lo is None else hi - lo
    pieces = []
    for dev in range(N_DEV):
        lo, end = dev * IN_SHARD, (dev + 1) * IN_SHARD
        for seg_lo, seg_hi, seg_at in sorted(starts):
            a, b = max(lo, seg_lo), min(end, seg_hi)
            if a < b:
                pieces.append((dev, a - lo, seg_at + a - seg_lo, b - a))

    def body(g_ref, o_ref):
        for dev, dst, src, rows in pieces:
            o_ref[dev, pl.ds(dst, rows), :] = g_ref[pl.ds(src, rows), :]

    tc = 2 * LANES
    return pl.pallas_call(
        body, name="grad_w_in_slots", grid=(D_MODEL // tc,),
        in_specs=[pl.BlockSpec((PROJ_COLS, tc), lambda j: (0, j))],
        out_specs=pl.BlockSpec((N_DEV, IN_SHARD, tc), lambda j: (0, 0, j)),
        out_shape=pltpu.HBM((N_DEV, IN_SHARD, D_MODEL), F32),
        compiler_params=pltpu.CompilerParams(
            dimension_semantics=("parallel",), vmem_limit_bytes=_vmem_limit(2 * _nbytes((PROJ_COLS, tc), F32))),
    )(*_in_hbm(dwt_r))


def kernel(x, w_in, w_proj_a, w_proj_b, w_out, b_forget, w_ffn_gate, w_ffn_up, w_ffn_down, norm_mix_pre, norm_mix_post, norm_ffn_pre, norm_ffn_post, loss_target, m_w_in, m_w_proj_a, m_w_proj_b, m_w_out, m_b_forget, m_w_ffn_gate, m_w_ffn_up, m_w_ffn_down, m_norm_mix_pre, m_norm_mix_post, m_norm_ffn_pre, m_norm_ffn_post, v_w_in, v_w_proj_a, v_w_proj_b, v_w_out, v_b_forget, v_w_ffn_gate, v_w_ffn_up, v_w_ffn_down, v_norm_mix_pre, v_norm_mix_post, v_norm_ffn_pre, v_norm_ffn_post):
    d = D_MODEL
    names = ("w_in", "w_proj_a", "w_proj_b", "w_out", "w_ffn_gate", "w_ffn_up", "w_ffn_down")
    col_sharded = ("w_in", "w_ffn_gate", "w_ffn_up")

    def row_shards(arrs):
        return {k: (a[0].T if k in col_sharded else a[0]) for k, a in zip(names, arrs)}

    shards = row_shards((w_in, w_proj_a, w_proj_b, w_out, w_ffn_gate, w_ffn_up, w_ffn_down))
    moments_m = row_shards((m_w_in, m_w_proj_a, m_w_proj_b, m_w_out, m_w_ffn_gate, m_w_ffn_up, m_w_ffn_down))
    moments_v = row_shards((v_w_in, v_w_proj_a, v_w_proj_b, v_w_out, v_w_ffn_gate, v_w_ffn_up, v_w_ffn_down))
    pos = jnp.stack([lax.axis_index("c"), 2 * lax.axis_index("x") + lax.axis_index("y")]).astype(jnp.int32)
    x2, target = x[0], loss_target[0]

    me = 4 * lax.axis_index("x") + 2 * lax.axis_index("y") + lax.axis_index("c")
    mid_names, ffn_names = names[1:4], names[4:]
    first_names, later_names = names[:1], names[1:]
    shards16 = {k: shards[k].astype(BF16) for k in names}

    def landing(k):
        return lax.dynamic_update_slice(lax.empty((N_DEV,) + shards[k].shape, BF16), shards16[k][None], (me, 0, 0))

    ag_first = _exchange_start("ag_first_chips_start", _gather_two_route_copies, [shards16[k] for k in first_names],
                               [landing(k) for k in first_names], 4 * len(first_names))
    h = _rowwise(lambda xb, g: xb * _rms_scale(xb) * g, "norm_mix_pre", SEQ, 512, [(x2, d, 0)], [norm_mix_pre],
                 [(d, BF16)], deps=[ag_first.token])[0]
    later_lands = [landing(k) for k in later_names]
    _, ag_first_diagonal = _gather_relay(ag_first, "ag_first", [h, shards["w_in"], moments_m["w_in"], moments_v["w_in"],
                                                               *later_lands, *[shards16[k] for k in later_names]])
    relayed, ag_first_last = ag_first_diagonal([])
    ag_later = _exchange_start("ag_later_chips_start", _gather_two_route_copies, [shards16[k] for k in later_names],
                               later_lands, 4 * len(later_names), after=[relayed])
    gathered = dict(zip(first_names, ag_first_last([ag_later.token])))
    wt_r = _proj_weight_t(gathered["w_in"])

    proj = _matmul([(h, *_in_hbm(wt_r))], "nt", F32, "in_proj", 1024, 896, 1024)
    tables = _rope_tables()
    relayed, ag_later_diagonal = _gather_relay(ag_later, "ag_later", [proj])
    o_dil, lse_dil = _dil_fwd(proj, tables, deps=[relayed])
    out_a = _dil_combine(o_dil, lse_dil)
    relayed, ag_later_last = ag_later_diagonal([out_a])

    b_pad = jnp.pad(b_forget, ((0, 0), (0, LANES - N_FOX_HEADS)))
    f_rows = _fox_gate(proj, b_pad, deps=[relayed])
    out_b, lse_fox = _fox_fwd(proj, f_rows)

    gathered = dict(zip(later_names, ag_later_last([out_b])))
    wt_pa = gathered["w_proj_a"].transpose(1, 0, 2).reshape(DIL_OUT_WIDTH, d)
    wt_pb = gathered["w_proj_b"].transpose(1, 0, 2).reshape(FOX_WIDTH, d)
    w_o = gathered["w_out"].reshape(d, d)
    wt_g = gathered["w_ffn_gate"].reshape(D_FF, d)
    wt_u = gathered["w_ffn_up"].reshape(D_FF, d)
    w_d = gathered["w_ffn_down"].reshape(D_FF, d)
    merged, mix, x1, h2 = _mix_out(out_a, out_b, proj, x2, wt_pa, wt_pb, w_o, norm_mix_post, norm_ffn_pre)

    gate, up, act = _ffn_up(h2, wt_g, wt_u)
    dy, dff, loss_part, dg_ffn_post = _loss_head(act, w_d, x1, target, norm_ffn_post)

    dgate, dup = _ffn_act_bwd(dff, w_d, gate, up)
    grads_t = {}
    grads_t["w_ffn_down"] = _matmul([(act, dff)], "tn", F32, "grad_w_ffn_down", 1408, 512, 2048, staged=False)
    grads_t["w_ffn_gate"] = _matmul([(dgate, h2)], "tn", F32, "grad_w_ffn_gate", 1408, 512, 2048)
    grads_t["w_ffn_up"] = _matmul([(dup, h2)], "tn", F32, "grad_w_ffn_up", 1408, 512, 2048)
    rs_ffn = _ReduceScatter("ffn", {k: grads_t[k] for k in ffn_names}, pos)
    dx1, dmix, dg_ffn_pre, dg_mix_post = _post_ffn_bwd(dgate, wt_g, dup, wt_u, x1, dy, mix, norm_ffn_pre, norm_mix_post,
                                                       deps=[rs_ffn.token])
    rs_ffn.start_chips([dmix])

    dproj, dya, dyb, d_out_a, d_out_b = _mix_out_bwd(dmix, out_a, out_b, proj, wt_pa, wt_pb, w_o, deps=[rs_ffn.token])
    grads_t["w_out"] = _matmul([(merged, dmix)], "tn", F32, "grad_w_out", 1024, 1024, 1024)
    def column_slots(g):
        return g.reshape(g.shape[0], N_DEV, LANES).transpose(1, 0, 2)

    grads_t["w_proj_a"] = column_slots(_matmul([(out_a, dya)], "tn", F32, "grad_w_proj_a", DIL_OUT_WIDTH, 1024, SEQ))
    grads_t["w_proj_b"] = column_slots(_matmul([(out_b, dyb)], "tn", F32, "grad_w_proj_b", FOX_WIDTH, 1024, SEQ))
    rs_mid = _ReduceScatter("mid", {k: grads_t[k] for k in mid_names}, pos)

    do_dil, c_dil = _dil_combine_bwd(d_out_a, o_dil, lse_dil, deps=[rs_mid.token])
    rs_mid.start_chips([c_dil])
    dproj, d_cum = _fox_bwd(proj, d_out_b, lse_fox, f_rows, dproj)
    d_cum_rows = jnp.pad(d_cum[:, :2].reshape(N_FOX_HEADS, SEQ), ((0, F_ROWS - N_FOX_HEADS), (0, 0)))
    dproj, db_part = _fox_gate_bwd(d_cum_rows, proj, b_pad, dproj)
    dproj = _dil_bwd(proj, tables, do_dil, lse_dil, c_dil, dproj, deps=[rs_mid.token])

    dwt_r = _matmul([(dproj, h)], "tn", F32, "grad_w_in", 896, 1024, 2048)
    rs_in = _ReduceScatter("in", {"w_in": _proj_weight_grad_slots(dwt_r)}, pos)
    def finish(rs, after):
        return {k: _adam(shards[k], mine, recv, moments_m[k], moments_v[k], "adam_" + k)
                for k, (mine, recv) in rs.finish(after).items()}

    done = finish(rs_ffn, [rs_in.token])
    rs_in.start_chips([done[k][0] for k in ffn_names])
    grad_x, dg_mix_pre = _input_bwd(dproj, wt_r, x2, dx1, norm_mix_pre, deps=[rs_in.token])
    done.update(finish(rs_mid, [grad_x]))

    small_all = _all_gather([dg_mix_pre, dg_mix_post, dg_ffn_pre, dg_ffn_post, db_part, loss_part],
                            "small_grads_all_gather", deps=[done[k][0] for k in mid_names])
    small, loss = _adam_small(small_all[:5], [norm_mix_pre, norm_mix_post, norm_ffn_pre, norm_ffn_post, b_forget],
                              [m_norm_mix_pre, m_norm_mix_post, m_norm_ffn_pre, m_norm_ffn_post, m_b_forget],
                              [v_norm_mix_pre, v_norm_mix_post, v_norm_ffn_pre, v_norm_ffn_post, v_b_forget],
                              small_all[5])

    done.update(finish(rs_in, [small[0][0]]))

    def leaves(i):
        def nat(k):
            a = done[k][i]
            return (a.T if k in col_sharded else a)[None]

        return [nat("w_in"), nat("w_proj_a"), nat("w_proj_b"), nat("w_out"), small[4][i],
                nat("w_ffn_gate"), nat("w_ffn_up"), nat("w_ffn_down"), *[small[r][i] for r in range(4)]]

    return (loss[0, 0], grad_x[None], *leaves(0), *leaves(1), *leaves(2), *leaves(3))
```

```python
import functools
import math

import jax
import jax.numpy as jnp
import numpy as np
from jax import lax
from jax.experimental import pallas as pl
from jax.experimental.pallas import tpu as pltpu

F32 = jnp.float32
BF16 = jnp.bfloat16
MESH = pl.DeviceIdType.MESH

D_MODEL = 1024
SEQ = 2048
HEAD_DIM = 64
BLOCK = 128
N_BLOCKS = SEQ // BLOCK
DILATIONS = (1, 4, 16)
N_FOX_HEADS = 8
DIL_WIDTH = 768
DIL_OUT_WIDTH = 256
FOX_WIDTH = 512
D_FF = 2816
ROPE_THETA = 500000.0
ROPE_DIM = HEAD_DIM // 4
ROPE_HALF = ROPE_DIM // 2
EPS = 1e-6
NEG_INF = -1e30
QK_SCALE = 1.0 / math.sqrt(HEAD_DIM)
IN_COLS = 5896
N_DEV = 8
IN_SHARD = IN_COLS // N_DEV

ADAM_LR = 0.001
ADAM_B1 = 0.9
ADAM_B2 = 0.999
ADAM_EPS = 1e-08
ADAM_WD = 0.01
ADAM_STEP = 10

V7X_VMEM_BYTES = 64 * 2**20
LANES = 128
SUBLANES = 8

PROJ_COLS = 6272
COL_GA, COL_GB = 0, 1024
COL_QA, COL_KA, COL_VA = 2304, 3072, 3840
COL_QB, COL_KB, COL_VB = 4608, 5120, 5632
COL_F = 6144
F_ROWS = 16


def _vmem_limit(block_bytes):
    want = 2 * block_bytes + 16 * 2**20
    return int(min(max(want, 32 * 2**20), V7X_VMEM_BYTES - 8 * 2**20))


def _nbytes(shape, dtype):
    return math.prod(shape) * jnp.dtype(dtype).itemsize


def _in_hbm(*arrays):
    return [pltpu.with_memory_space_constraint(a, pltpu.HBM) for a in arrays]


def _dot(a, b, dims):
    return lax.dot_general(a, b, (dims, ((), ())), preferred_element_type=F32)


def _dot_nn(a, b):
    return _dot(a, b, ((1,), (0,)))


def _dot_nt(a, b):
    return _dot(a, b, ((1,), (1,)))


def _dot_tn(a, b):
    return _dot(a, b, ((0,), (0,)))


def _sigmoid(z):
    return 1.0 / (1.0 + jnp.exp(-z))


def _split3(x):
    hi = x.astype(BF16)
    r1 = x - hi.astype(F32)
    mid = r1.astype(BF16)
    lo = (r1 - mid.astype(F32)).astype(BF16)
    return hi, mid, lo


def _dot3_nn(x, ones_matrix):
    hi, mid, lo = _split3(x)
    return (_dot_nn(hi, ones_matrix) + _dot_nn(mid, ones_matrix)) + _dot_nn(lo, ones_matrix)


def _rowwise(fn, name, n_rows, tm, row_ins, bcast_ins, row_outs, acc_outs=(), deps=()):
    n_in = len(row_ins) + len(bcast_ins)
    n_ro = len(row_outs)

    def body(*refs):
        res = fn(*[r[...] for r in refs[:n_in]])
        if not isinstance(res, (tuple, list)):
            res = (res,)
        outs = refs[n_in + len(deps):]
        for r, o in zip(res[:n_ro], outs[:n_ro]):
            o[...] = r.astype(o.dtype)
        first = pl.program_id(0) == 0
        for r, o in zip(res[n_ro:], outs[n_ro:]):
            _accumulate(o, r, first)

    in_specs = [pl.BlockSpec((tm, w), lambda i, cb=cb: (i, cb)) for _, w, cb in row_ins]
    in_specs += [pl.BlockSpec(a.shape, lambda i: (0, 0)) for a in bcast_ins]
    in_specs += [pl.BlockSpec(memory_space=pl.ANY)] * len(deps)
    out_specs = [pl.BlockSpec((tm, w), lambda i: (i, 0)) for w, _ in row_outs]
    out_specs += [pl.BlockSpec((1, w), lambda i: (0, 0)) for w in acc_outs]
    out_shape = [pltpu.HBM((n_rows, w), dt) for w, dt in row_outs]
    out_shape += [pltpu.HBM((1, w), F32) for w in acc_outs]
    blk = sum(_nbytes((tm, w), a.dtype) for a, w, _ in row_ins) + sum(_nbytes((tm, w), dt) for w, dt in row_outs)
    return pl.pallas_call(
        body, name=name, grid=(n_rows // tm,), in_specs=in_specs, out_specs=out_specs, out_shape=out_shape,
        compiler_params=pltpu.CompilerParams(
            dimension_semantics=("arbitrary" if acc_outs else "parallel",), vmem_limit_bytes=_vmem_limit(3 * blk)),
    )(*_in_hbm(*[a for a, _, _ in row_ins], *bcast_ins), *deps)


def _accumulate(o_ref, part, first):
    @pl.when(first)
    def _():
        o_ref[...] = part

    @pl.when(jnp.logical_not(first))
    def _():
        o_ref[...] += part


_MM_DIMS = {"nn": ((1,), (0,)), "nt": ((1,), (1,)), "tn": ((0,), (0,))}


def _matmul(pairs, mode, out_dtype, name, tm, tn, tk, deps=(), staged=True):
    a0, b0 = pairs[0]
    if mode == "tn":
        kk, m = a0.shape
    else:
        m, kk = a0.shape
    n = b0.shape[0] if mode == "nt" else b0.shape[1]
    assert m % tm == 0 and n % tn == 0 and kk % tk == 0, (name, m, n, kk)
    nk = kk // tk
    n_pairs = len(pairs)
    dims = _MM_DIMS[mode]
    n_in = 2 * n_pairs + len(deps)

    def body(*refs):
        o_ref = refs[n_in]
        part = None
        for p in range(n_pairs):
            d = _dot(refs[2 * p][...].astype(BF16), refs[2 * p + 1][...].astype(BF16), dims)
            part = d if part is None else part + d
        if nk == 1:
            o_ref[...] = part.astype(o_ref.dtype)
            return
        acc = refs[n_in + 1]
        k = pl.program_id(2)

        @pl.when(k == 0)
        def _():
            acc[...] = part

        @pl.when(k > 0)
        def _():
            acc[...] += part

        @pl.when(k == nk - 1)
        def _():
            o_ref[...] = acc[...].astype(o_ref.dtype)

    if mode == "tn":
        a_spec = pl.BlockSpec((tk, tm), lambda i, j, k: (k, i))
    else:
        a_spec = pl.BlockSpec((tm, tk), lambda i, j, k: (i, k))
    if mode == "nt":
        b_spec = pl.BlockSpec((tn, tk), lambda i, j, k: (j, k))
    else:
        b_spec = pl.BlockSpec((tk, tn), lambda i, j, k: (k, j))
    blk = sum(_nbytes((tm, tk), a.dtype) + _nbytes((tk, tn), b.dtype) for a, b in pairs) + 2 * _nbytes((tm, tn), F32)
    flat = [a for pair in pairs for a in pair]
    return pl.pallas_call(
        body, name=name, grid=(m // tm, n // tn, nk),
        in_specs=[a_spec, b_spec] * n_pairs + [pl.BlockSpec(memory_space=pl.ANY)] * len(deps),
        out_specs=pl.BlockSpec((tm, tn), lambda i, j, k: (i, j)),
        out_shape=pltpu.HBM((m, n), out_dtype),
        scratch_shapes=[] if nk == 1 else [pltpu.VMEM((tm, tn), F32)],
        compiler_params=pltpu.CompilerParams(
            dimension_semantics=("parallel", "parallel", "arbitrary"), vmem_limit_bytes=_vmem_limit(blk)),
    )(*(flat if staged else _in_hbm(*flat)), *deps)


def _matmul_rowwise(pairs, fn, name, tm, row_ins, bcast_ins, row_outs, acc_outs=(), deps=()):
    m = pairs[0][0].shape[0]
    n_mm, n_in = 2 * len(pairs), len(row_ins) + len(bcast_ins)
    n_ro = len(row_outs)

    def body(*refs):
        prod = None
        for p in range(len(pairs)):
            part = _dot_nn(refs[2 * p][...].astype(BF16), refs[2 * p + 1][...].astype(BF16))
            prod = part if prod is None else prod + part
        res = fn(prod, *[r[...] for r in refs[n_mm:n_mm + n_in]])
        outs = refs[n_mm + n_in + len(deps):]
        for r, o in zip(res[:n_ro], outs[:n_ro]):
            o[...] = r.astype(o.dtype)
        first = pl.program_id(0) == 0
        for r, o in zip(res[n_ro:], outs[n_ro:]):
            _accumulate(o, r, first)

    in_specs = []
    for a, b in pairs:
        in_specs += [pl.BlockSpec((tm, a.shape[1]), lambda i: (i, 0)),
                     pl.BlockSpec(b.shape, lambda i: (0, 0), pipeline_mode=pl.Buffered(1))]
    in_specs += [pl.BlockSpec((tm, w), lambda i, cb=cb: (i, cb)) for _, w, cb in row_ins]
    in_specs += [pl.BlockSpec(a.shape, lambda i: (0, 0)) for a in bcast_ins]
    in_specs += [_ANY] * len(deps)
    out_specs = [pl.BlockSpec((tm, w), lambda i: (i, 0)) for w, _ in row_outs]
    out_specs += [pl.BlockSpec((1, w), lambda i: (0, 0)) for w in acc_outs]
    out_shape = [pltpu.HBM((m, w), dt) for w, dt in row_outs]
    out_shape += [pltpu.HBM((1, w), F32) for w in acc_outs]
    blk = sum(_nbytes((tm, a.shape[1]), a.dtype) + _nbytes(b.shape, b.dtype) // 2 for a, b in pairs)
    blk += sum(_nbytes((tm, w), a.dtype) for a, w, _ in row_ins) + sum(_nbytes((tm, w), dt) for w, dt in row_outs)
    return pl.pallas_call(
        body, name=name, grid=(m // tm,), in_specs=in_specs, out_specs=out_specs, out_shape=out_shape,
        compiler_params=pltpu.CompilerParams(dimension_semantics=("arbitrary",), vmem_limit_bytes=_vmem_limit(blk)),
    )(*[a for pair in pairs for a in pair], *[a for a, _, _ in row_ins], *bcast_ins, *deps)


def _rms_scale(x):
    return lax.rsqrt(jnp.mean(x * x, axis=-1, keepdims=True) + EPS)


def _rms_bwd(xin, dyn, g):
    r = _rms_scale(xin)
    u = dyn * g
    dx = r * u - xin * (r * r * r) * jnp.mean(u * xin, axis=-1, keepdims=True)
    dg = jnp.sum(dyn * xin * r, axis=0, keepdims=True)
    return dx, dg


def _mesh_pos():
    return lax.axis_index("x"), lax.axis_index("y"), lax.axis_index("c")


def _all_gather(xs, name, deps=()):
    n = len(xs)

    def body(*refs):
        x_refs, out_refs = refs[:n], refs[n + len(deps):2 * n + len(deps)]
        send_sems, recv_sems, local_sems = refs[2 * n + len(deps):]
        mx, my, mc = _mesh_pos()
        me, sib = (mx, my, mc), (mx, my, 1 - mc)
        chips = [(1 - mx, my), (mx, 1 - my), (1 - mx, 1 - my)]

        def slot(a, dev):
            px, py, pc = dev
            return out_refs[a].at[4 * px + 2 * py + pc]

        def copy(k, a, block, to, src=None):
            return pltpu.make_async_remote_copy(
                src_ref=slot(a, block) if src is None else src, dst_ref=slot(a, block),
                send_sem=send_sems.at[a * 7 + k], recv_sem=recv_sems.at[a * 7 + k],
                device_id=to, device_id_type=MESH)

        mine = [pltpu.make_async_copy(x_refs[a], slot(a, me), local_sems.at[a]) for a in range(n)]
        for cp in mine:
            cp.start()
        first = []
        for a in range(n):
            first.append(copy(0, a, me, sib, x_refs[a]))
            first += [copy(1 + j, a, me, (*chip, mc), x_refs[a]) for j, chip in enumerate(chips)]
        for cp in first:
            cp.start()
        passed = []
        for a in range(n):
            for j, chip in enumerate(chips):
                copy(1 + j, a, (*chip, mc), me).wait_recv()
                fwd = copy(4 + j, a, (*chip, mc), sib)
                fwd.start()
                passed.append(fwd)
        for a in range(n):
            copy(0, a, sib, me).wait_recv()
            for j, chip in enumerate(chips):
                copy(4 + j, a, (*chip, 1 - mc), me).wait_recv()
        for cp in first + passed:
            cp.wait_send()
        for cp in mine:
            cp.wait()

    hbm = pl.BlockSpec(memory_space=pl.ANY)
    return pl.pallas_call(
        body, name=name,
        out_shape=[pltpu.HBM((N_DEV,) + x.shape, x.dtype) for x in xs],
        in_specs=[hbm] * (n + len(deps)), out_specs=[hbm] * n,
        scratch_shapes=[pltpu.SemaphoreType.DMA((7 * n,)), pltpu.SemaphoreType.DMA((7 * n,)),
                        pltpu.SemaphoreType.DMA((n,))],
    )(*xs, *deps)


_HBM = pl.BlockSpec(memory_space=pltpu.HBM)
_SEM = pl.BlockSpec(memory_space=pltpu.SEMAPHORE)
_ANY = pl.BlockSpec(memory_space=pl.ANY)
_DATAFLOW = pltpu.SideEffectType.DATAFLOW_SIDE_EFFECTING


def _flip_peer(flip):
    mx, my, mc = _mesh_pos()
    return (1 - mx if flip & 2 else mx, 1 - my if flip & 1 else my, mc)


def _remote(src, dst, send_sems, recv_sems, k, peer):
    return pltpu.make_async_remote_copy(src_ref=src, dst_ref=dst, send_sem=send_sems.at[k], recv_sem=recv_sems.at[k],
                                        device_id=peer, device_id_type=MESH)


def _scatter_sibling_copies(srcs, lands, send_sems, recv_sems):
    mx, my, mc = _mesh_pos()
    return [_remote(srcs[a].at[k, 1 - mc], lands[a].at[k], send_sems, recv_sems, 4 * a + k, (mx, my, 1 - mc))
            for a in range(len(srcs)) for k in range(4)]


def _scatter_chips_copies(srcs, lands, send_sems, recv_sems):
    mx, my, _ = _mesh_pos()
    k0 = 2 * mx + my
    return [_remote(srcs[a].at[jnp.bitwise_xor(k0, flip)], lands[a].at[flip - 1], send_sems, recv_sems,
                    3 * a + flip - 1, _flip_peer(flip))
            for a in range(len(srcs)) for flip in (1, 2, 3)]


class _Exchange:
    def __init__(self, copies, n_src, send_sems, recv_sems, thru, token):
        self.copies, self.n_src, self.send_sems, self.recv_sems, self.thru, self.token = (
            copies, n_src, send_sems, recv_sems, thru, token)


def _exchange_start(name, copies, srcs, lands, n_copies, after=()):
    bufs = list(srcs) + list(lands)
    nb, ns = len(bufs), len(srcs)

    def body(*refs):
        send_sems, recv_sems = refs[nb + len(after)], refs[nb + len(after) + 1]
        for cp in copies(refs[:ns], refs[ns:nb], send_sems, recv_sems):
            cp.start()
        refs[-1][...] = jnp.zeros_like(refs[-1])

    out = pl.pallas_call(
        body, name=name,
        out_shape=(pltpu.SemaphoreType.DMA((n_copies,)), pltpu.SemaphoreType.DMA((n_copies,)),
                   *[pltpu.HBM(b.shape, b.dtype) for b in bufs], pltpu.HBM((SUBLANES, LANES), F32)),
        in_specs=[_HBM] * nb + [_ANY] * len(after),
        out_specs=(_SEM, _SEM, *[_HBM] * nb, pl.BlockSpec(memory_space=pltpu.VMEM)),
        input_output_aliases={i: 2 + i for i in range(nb)},
        compiler_params=pltpu.CompilerParams(has_side_effects=_DATAFLOW),
    )(*[pltpu.with_memory_space_constraint(b, pltpu.HBM) for b in bufs], *after)
    return _Exchange(copies, ns, out[0], out[1], list(out[2:2 + nb]), out[-1])


def _exchange_wait(name, ex, after):
    nb, ns = len(ex.thru), ex.n_src

    def body(*refs):
        for cp in ex.copies(refs[:ns], refs[ns:nb], refs[nb], refs[nb + 1]):
            cp.wait_send()
            cp.wait_recv()

    out = pl.pallas_call(
        body, name=name, out_shape=tuple(pltpu.HBM(b.shape, b.dtype) for b in ex.thru),
        in_specs=[_HBM] * nb + [_SEM, _SEM] + [_ANY] * len(after), out_specs=tuple([_HBM] * nb),
        input_output_aliases={i: i for i in range(nb)},
        compiler_params=pltpu.CompilerParams(has_side_effects=_DATAFLOW),
    )(*ex.thru, ex.send_sems, ex.recv_sems, *after)
    return list(out[:ns]), list(out[ns:])


def _halves(ref):
    half = ref.shape[1] // 2
    if half % LANES == 0:
        return ref.at[:, pl.ds(0, half)], ref.at[:, pl.ds(half, half)]
    half = ref.shape[0] // 2
    assert half % (2 * SUBLANES) == 0, ref.shape
    return ref.at[pl.ds(0, half)], ref.at[pl.ds(half, half)]


def _gather_two_route_copies(srcs, lands, send_sems, recv_sems):
    mx, my, mc = _mesh_pos()
    me = 4 * mx + 2 * my + mc
    return [_remote(_halves(srcs[a])[h], _halves(lands[a].at[me])[h], send_sems, recv_sems, 4 * a + i, _flip_peer(flip))
            for a in range(len(srcs)) for i, (flip, h) in enumerate(((2, 0), (1, 1), (2, 1), (1, 0)))]


def _to_sibling(land, flip, h, send_sems, recv_sems, k):
    mx, my, mc = _mesh_pos()
    part = _halves(land.at[2 * jnp.bitwise_xor(2 * mx + my, flip) + mc])[h]
    return _remote(part, part, send_sems, recv_sems, k, (mx, my, 1 - mc))


def _second_hop(land, send_sems, recv_sems, k):
    mx, my, mc = _mesh_pos()
    k0 = 2 * mx + my
    from_y = _halves(land.at[2 * jnp.bitwise_xor(k0, 1) + mc])[1]
    from_x = _halves(land.at[2 * jnp.bitwise_xor(k0, 2) + mc])[0]
    return (_remote(from_y, from_y, send_sems, recv_sems, k, _flip_peer(2)),
            _remote(from_x, from_x, send_sems, recv_sems, k + 1, _flip_peer(1)))


def _relay_call(name, body, bufs, sems, n_new, after):
    nb, n_in = len(bufs), len(bufs) + len(sems) + len(after)

    def call_body(*refs):
        body(refs[:nb], refs[nb:nb + len(sems)], refs[n_in], refs[n_in + 1])
        refs[-1][...] = jnp.zeros_like(refs[-1])

    out = pl.pallas_call(
        call_body, name=name,
        out_shape=(pltpu.SemaphoreType.DMA((n_new,)), pltpu.SemaphoreType.DMA((n_new,)),
                   *[pltpu.HBM(b.shape, b.dtype) for b in bufs], pltpu.HBM((SUBLANES, LANES), F32)),
        in_specs=[_HBM] * nb + [_SEM] * len(sems) + [_ANY] * len(after),
        out_specs=(_SEM, _SEM, *[_HBM] * nb, pl.BlockSpec(memory_space=pltpu.VMEM)),
        input_output_aliases={i: 2 + i for i in range(nb)},
        compiler_params=pltpu.CompilerParams(has_side_effects=_DATAFLOW),
    )(*bufs, *sems, *after)
    return out[0], out[1], list(out[2:2 + nb]), out[-1]


def _gather_relay(ex, tag, after_first):
    ns = ex.n_src
    n = len(ex.thru) - ns

    def first(bufs, sems, send, recv):
        lands, first_hop = bufs[ns:], ex.copies(bufs[:ns], bufs[ns:], *sems)
        for a in range(n):
            for h in (0, 1):
                _to_sibling(lands[a], 0, h, send, recv, 10 * a + h).start()
        for a in range(n):
            x_first, y_second, x_second, y_first = first_hop[4 * a:4 * a + 4]
            to_x, to_y = _second_hop(lands[a], send, recv, 10 * a + 8)
            x_first.wait_recv()
            to_y.start()
            _to_sibling(lands[a], 2, 0, send, recv, 10 * a + 4).start()
            y_second.wait_recv()
            to_x.start()
            _to_sibling(lands[a], 1, 1, send, recv, 10 * a + 3).start()
            x_second.wait_recv()
            _to_sibling(lands[a], 2, 1, send, recv, 10 * a + 5).start()
            y_first.wait_recv()
            _to_sibling(lands[a], 1, 0, send, recv, 10 * a + 2).start()
        for cp in first_hop:
            cp.wait_send()

    def second(lands, sems, send, recv):
        for a in range(n):
            to_x, to_y = _second_hop(lands[a], *sems, 10 * a + 8)
            to_y.wait_recv()
            _to_sibling(lands[a], 3, 0, send, recv, 2 * a).start()
            to_x.wait_recv()
            _to_sibling(lands[a], 3, 1, send, recv, 2 * a + 1).start()
            to_x.wait_send()
            to_y.wait_send()

    def last(lands, sems, send, recv):
        for a in range(n):
            for flip in range(4):
                for h in (0, 1):
                    s, r, k = (sems[2], sems[3], 2 * a + h) if flip == 3 else (sems[0], sems[1], 10 * a + 2 * flip + h)
                    cp = _to_sibling(lands[a], flip, h, s, r, k)
                    cp.wait_send()
                    cp.wait_recv()

    send1, recv1, bufs, token = _relay_call(f"{tag}_chips_relay", first, ex.thru, [ex.send_sems, ex.recv_sems], 10 * n,
                                            after_first)

    def run_second(after):
        send2, recv2, lands, token = _relay_call(f"{tag}_diagonal_relay", second, bufs[ns:], [send1, recv1], 2 * n, after)
        return token, lambda after_last: _relay_call(f"{tag}_sibling_wait", last, lands, [send1, recv1, send2, recv2],
                                                     1, after_last)[2]

    return token, run_second


def _col_tile(r, c, block_bytes=2**20):
    return next(t for t in (1024, 512, 256, 128) if c % t == 0 and (r * t * 4 <= block_bytes or t == 128))


def _add_sibling(g4, recv, pos, name):
    _, _, r, c = g4.shape
    tc = _col_tile(r, c, 2**22)

    def body(pos_ref, g_ref, r_ref, o16_ref, mine_ref):
        s = g_ref[0, 0] + r_ref[0]
        o16_ref[0] = s.astype(BF16)

        @pl.when(pl.program_id(1) == pos_ref[1])
        def _():
            mine_ref[...] = s

    slot = pl.BlockSpec((1, r, tc), lambda j, k, pos_ref: (k, 0, j))
    return pl.pallas_call(
        body, name=name,
        out_shape=[pltpu.HBM((4, r, c), BF16), pltpu.HBM((r, c), F32)],
        grid_spec=pltpu.PrefetchScalarGridSpec(
            num_scalar_prefetch=1, grid=(c // tc, 4),
            in_specs=[pl.BlockSpec((1, 1, r, tc), lambda j, k, pos_ref: (k, pos_ref[0], 0, j)), slot],
            out_specs=[slot, pl.BlockSpec((r, tc), lambda j, k, pos_ref: (0, j))]),
        compiler_params=pltpu.CompilerParams(
            dimension_semantics=("parallel", "arbitrary"), vmem_limit_bytes=_vmem_limit(4 * _nbytes((r, tc), F32))),
    )(pos, *_in_hbm(g4, recv))


class _ReduceScatter:
    def __init__(self, tag, grads_t, pos):
        self.tag, self.pos, self.names = tag, pos, list(grads_t)
        g4s = [g.reshape(4, 2, g.size // (N_DEV * g.shape[-1]), g.shape[-1]) for g in grads_t.values()]
        lands = [lax.empty((4,) + g.shape[2:], F32) for g in g4s]
        self.ex = _exchange_start(f"rs_{tag}_sibling_start", _scatter_sibling_copies, g4s, lands, 4 * len(g4s))
        self.token = self.ex.token

    def start_chips(self, after):
        g4s, from_sibling = _exchange_wait(f"rs_{self.tag}_sibling_wait", self.ex, after)
        parts = [_add_sibling(g4, rv, self.pos, f"rs_add_sibling_{k}")
                 for k, g4, rv in zip(self.names, g4s, from_sibling)]
        self.mine = [mine for _, mine in parts]
        p16s = [p16 for p16, _ in parts]
        lands = [lax.empty((3,) + p.shape[1:], BF16) for p in p16s]
        self.ex = _exchange_start(f"rs_{self.tag}_chips_start", _scatter_chips_copies, p16s, lands, 3 * len(p16s))
        self.token = self.ex.token

    def finish(self, after):
        _, from_chips = _exchange_wait(f"rs_{self.tag}_chips_wait", self.ex, after)
        return dict(zip(self.names, zip(self.mine, from_chips)))


def _rope_tables():
    positions = np.arange(SEQ, dtype=np.float32)
    inv_freq = np.power(np.float32(ROPE_THETA), -np.arange(0, ROPE_DIM, 2, dtype=np.float32) / np.float32(ROPE_DIM))
    ang = (positions[:, None] * inv_freq[None, :]).astype(np.float32)
    cos, sin = np.cos(ang).astype(np.float32), np.sin(ang).astype(np.float32)
    ones = np.ones((SEQ, HEAD_DIM - ROPE_DIM), np.float32)
    zeros8 = np.zeros((SEQ, ROPE_HALF), np.float32)
    zeros = np.zeros((SEQ, HEAD_DIM - ROPE_DIM), np.float32)
    c_head = np.concatenate([cos, cos, ones], axis=1)
    s1_head = np.concatenate([-sin, zeros8, zeros], axis=1)
    s2_head = np.concatenate([zeros8, sin, zeros], axis=1)
    return tuple(jnp.asarray(np.concatenate([t, t], axis=1)) for t in (c_head, s1_head, s2_head))


def _rope_apply(x, c, s1, s2):
    w = x.shape[1]
    return x * c + pltpu.roll(x, w - ROPE_HALF, 1) * s1 + pltpu.roll(x, ROPE_HALF, 1) * s2


def _rope_apply_t(dy, c, s1, s2):
    w = dy.shape[1]
    return dy * c + pltpu.roll(dy * s1, ROPE_HALF, 1) + pltpu.roll(dy * s2, w - ROPE_HALF, 1)


def _dil_prev_limit(has_prev):
    return jnp.where(has_prev, 0, BLOCK)


def _dil_valid(limit):
    row = lax.broadcasted_iota(jnp.int32, (BLOCK, 2 * BLOCK), 0)
    col = lax.broadcasted_iota(jnp.int32, (BLOCK, 2 * BLOCK), 1)
    dist = col - row
    return jnp.logical_and(dist >= jnp.where(col < BLOCK, limit, -BLOCK), dist <= BLOCK)


def _upper_half():
    return lax.broadcasted_iota(jnp.int32, (1, LANES), 1) >= HEAD_DIM


def _stack_heads(x):
    upper = _upper_half()
    return jnp.concatenate([jnp.where(upper, 0, x), jnp.where(upper, x, 0)], axis=0)


def _unstack_heads(y):
    n = y.shape[0] // 2
    return jnp.where(_upper_half(), y[n:], y[:n])


def _head_columns(t):
    return jnp.concatenate([t[:, 0:1], t[:, HEAD_DIM:HEAD_DIM + 1]], axis=0)


def _dil_rows(n, d):
    per = N_BLOCKS // d
    r, lb = n // per, n % per

    def rows(b):
        start = b * (BLOCK * d) + r
        return pl.ds(pl.multiple_of(start, BLOCK), BLOCK) if d == 1 else pl.ds(start, BLOCK, stride=d)

    return rows(lb), rows(jnp.maximum(lb - 1, 0)), lb > 0


def _dil_rotate(q_ref, k_ref, c_ref, s1_ref, s2_ref, q_rot, k_rot):
    tabs = (c_ref[...], s1_ref[...], s2_ref[...])
    q_rot[...] = _rope_apply(q_ref[...], *tabs) * QK_SCALE
    k_rot[...] = _rope_apply(k_ref[...], *tabs)


def _dil_specs():
    def col(base):
        return pl.BlockSpec((SEQ, LANES), lambda p: (0, base // LANES + p))

    table = pl.BlockSpec((SEQ, LANES), lambda p: (0, 0))
    return [col(COL_QA), col(COL_KA), col(COL_VA)], [table] * 3


def _store_columns(blocks, dproj_ref, cols, sem):
    copies = [pltpu.make_async_copy(b, dproj_ref.at[:, pl.ds(pl.multiple_of(c * LANES, LANES), LANES)], sem.at[i])
              for i, (b, c) in enumerate(zip(blocks, cols))]
    for cp in copies:
        cp.start()
    for cp in copies:
        cp.wait()


def _dil_window(d, n, k_rot, v_ref):
    rows, prev, has_prev = _dil_rows(n, d)
    kw, vw = k_rot[rows, :].astype(BF16), v_ref[rows, :].astype(BF16)
    if d == N_BLOCKS:
        row = lax.broadcasted_iota(jnp.int32, (BLOCK, BLOCK), 0)
        valid = lax.broadcasted_iota(jnp.int32, (BLOCK, BLOCK), 1) <= row
    else:
        kw = jnp.concatenate([k_rot[prev, :].astype(BF16), kw], axis=0)
        vw = jnp.concatenate([v_ref[prev, :].astype(BF16), vw], axis=0)
        valid = _dil_valid(_dil_prev_limit(has_prev))
    return rows, prev, kw, vw, jnp.concatenate([valid, valid], axis=0)


def _dil_fwd(proj, tables, deps=()):
    def body(q_ref, k_ref, v_ref, c_ref, s1_ref, s2_ref, *rest):
        o_ref, lse_ref, q_rot, k_rot = rest[len(deps):]
        upper = _upper_half()
        _dil_rotate(q_ref, k_ref, c_ref, s1_ref, s2_ref, q_rot, k_rot)

        def blocks_of(d):
            def block(n, carry):
                rows, _, kw, vw, valid = _dil_window(d, n, k_rot, v_ref)
                s = jnp.where(valid, _dot_nt(_stack_heads(q_rot[rows, :].astype(BF16)), kw), NEG_INF)
                m = jnp.max(s, axis=-1, keepdims=True)
                p = jnp.exp(s - m)
                den = jnp.sum(p, axis=-1, keepdims=True)
                o_ref[rows, :] = _unstack_heads(_dot_nn((p * (1.0 / den)).astype(BF16), vw))
                lse = m + jnp.log(den)
                lse_ref[rows, :] = jnp.where(upper, lse[BLOCK:], lse[:BLOCK])
                return carry

            lax.fori_loop(0, N_BLOCKS, block, 0, unroll=4)

        for g, d in enumerate(DILATIONS):
            pl.when(pl.program_id(0) // 2 == g)(functools.partial(blocks_of, d))

    qkv, tabs = _dil_specs()
    out = pl.BlockSpec((SEQ, LANES), lambda p: (0, p))
    return pl.pallas_call(
        body, name="dil_attn_fwd", grid=(DIL_WIDTH // LANES,), in_specs=qkv + tabs + [_ANY] * len(deps),
        out_specs=[out, out],
        out_shape=[pltpu.HBM((SEQ, DIL_WIDTH), F32)] * 2,
        scratch_shapes=[pltpu.VMEM((SEQ, LANES), F32)] * 2,
        compiler_params=pltpu.CompilerParams(dimension_semantics=("parallel",)),
    )(*_in_hbm(proj, proj, proj, *tables), *deps)


def _dil_bwd(proj, tables, do, lse, c, dproj, deps=()):
    def body(q_ref, k_ref, v_ref, c_ref, s1_ref, s2_ref, do_ref, lse_ref, cc_ref, dproj_in, *rest):
        dproj_ref, dq_acc, dk_acc, dv_acc, dq_out, dk_out, dv_out, q_rot, k_rot, sem = rest[len(deps):]
        dk_acc[...] = jnp.zeros_like(dk_acc)
        dv_acc[...] = jnp.zeros_like(dv_acc)
        _dil_rotate(q_ref, k_ref, c_ref, s1_ref, s2_ref, q_rot, k_rot)

        def blocks_of(d):
            def block(n, carry):
                rows, prev, kw, vw, valid = _dil_window(d, n, k_rot, v_ref)
                q2 = _stack_heads(q_rot[rows, :].astype(BF16))
                do2 = _stack_heads(do_ref[rows, :].astype(BF16))
                lse_col, c_col = _head_columns(lse_ref[rows, :]), _head_columns(cc_ref[rows, :])
                p = jnp.where(valid, jnp.exp(_dot_nt(q2, kw) - lse_col), 0.0)
                ds = (p * (_dot_nt(do2, vw) + c_col)).astype(BF16)
                dk, dv = _dot_tn(ds, q2), _dot_tn(p.astype(BF16), do2)
                dq_acc[rows, :] = _unstack_heads(_dot_nn(ds, kw)) * QK_SCALE
                if d == N_BLOCKS:
                    dk_acc[rows, :] += dk
                    dv_acc[rows, :] += dv
                else:
                    dk_acc[prev, :] += dk[:BLOCK]
                    dv_acc[prev, :] += dv[:BLOCK]
                    dk_acc[rows, :] += dk[BLOCK:]
                    dv_acc[rows, :] += dv[BLOCK:]
                return carry

            lax.fori_loop(0, N_BLOCKS, block, 0, unroll=4)

        pair = pl.program_id(0)
        for g, d in enumerate(DILATIONS):
            pl.when(pair // 2 == g)(functools.partial(blocks_of, d))
        tabs = (c_ref[...], s1_ref[...], s2_ref[...])
        dq_out[...] = _rope_apply_t(dq_acc[...], *tabs).astype(BF16)
        dk_out[...] = _rope_apply_t(dk_acc[...], *tabs).astype(BF16)
        dv_out[...] = dv_acc[...].astype(BF16)
        _store_columns((dq_out, dk_out, dv_out), dproj_ref,
                       [base // LANES + pair for base in (COL_QA, COL_KA, COL_VA)], sem)

    qkv, tabs = _dil_specs()
    tok = pl.BlockSpec((SEQ, LANES), lambda p: (0, p))
    return pl.pallas_call(
        body, name="dil_attn_bwd", grid=(DIL_WIDTH // LANES,),
        in_specs=qkv + tabs + [tok, tok, tok, _ANY] + [_ANY] * len(deps), out_specs=_ANY,
        out_shape=pltpu.HBM(dproj.shape, dproj.dtype),
        scratch_shapes=[pltpu.VMEM((SEQ, LANES), F32)] * 3 + [pltpu.VMEM((SEQ, LANES), BF16)] * 3
        + [pltpu.VMEM((SEQ, LANES), F32)] * 2 + [pltpu.SemaphoreType.DMA((3,))],
        input_output_aliases={9: 0},
        compiler_params=pltpu.CompilerParams(dimension_semantics=("arbitrary",)),
    )(*_in_hbm(proj, proj, proj, *tables, do, lse, c, dproj), *deps)


def _group_weights(l0, l1, l2):
    m = jnp.maximum(jnp.maximum(l0, l1), l2)
    e0, e1, e2 = jnp.exp(l0 - m), jnp.exp(l1 - m), jnp.exp(l2 - m)
    tot = e0 + e1 + e2
    return e0 / tot, e1 / tot, e2 / tot


def _dil_combine(o, lse, deps=()):
    def fn(o0, o1, o2, l0, l1, l2):
        w0, w1, w2 = _group_weights(l0, l1, l2)
        return w0 * o0 + w1 * o1 + w2 * o2

    w = DIL_OUT_WIDTH
    return _rowwise(fn, "dil_combine", SEQ, 512, [(o, w, g) for g in range(3)] + [(lse, w, g) for g in range(3)], [],
                    [(w, F32)], deps=deps)[0]


def _dil_combine_bwd(d_out, o, lse, deps=()):
    w = DIL_OUT_WIDTH

    def fn(d, o0, o1, o2, l0, l1, l2):
        row = lax.broadcasted_iota(jnp.int32, (w, w), 0) // HEAD_DIM
        col = lax.broadcasted_iota(jnp.int32, (w, w), 1) // HEAD_DIM
        same_head = jnp.where(row == col, 1.0, 0.0).astype(BF16)
        ws = _group_weights(l0, l1, l2)
        dws = [_dot3_nn(d * og, same_head) for og in (o0, o1, o2)]
        mean = ws[0] * dws[0] + ws[1] * dws[1] + ws[2] * dws[2]
        return jnp.concatenate([wg * d for wg in ws], axis=1), jnp.concatenate([-wg * mean for wg in ws], axis=1)

    return _rowwise(fn, "dil_combine_bwd", SEQ, 512,
                    [(d_out, w, 0)] + [(o, w, g) for g in range(3)] + [(lse, w, g) for g in range(3)], [],
                    [(DIL_WIDTH, F32)] * 2, deps=deps)


def _log1p(e):
    u = 1.0 + e
    return jnp.where(u == 1.0, e, jnp.log(u) * (e / (u - 1.0)))


def _fox_gate(proj, b_pad, deps=()):
    def body(f_ref, b_ref, *rest):
        o_ref = rest[-1]
        z = f_ref[...] + b_ref[...]
        logf = (jnp.minimum(z, 0.0) - _log1p(jnp.exp(-jnp.abs(z)))).T[:F_ROWS]
        row = lax.broadcasted_iota(jnp.int32, (BLOCK, BLOCK), 0)
        col = lax.broadcasted_iota(jnp.int32, (BLOCK, BLOCK), 1)
        before = jnp.where(row <= col, 1.0, 0.0).astype(BF16)
        carry = jnp.zeros((F_ROWS, 1), F32)
        for blk in range(N_BLOCKS):
            run = _dot3_nn(logf[:, blk * BLOCK:(blk + 1) * BLOCK], before) + carry
            o_ref[:, blk * BLOCK:(blk + 1) * BLOCK] = run
            carry = run[:, BLOCK - 1:BLOCK]

    return pl.pallas_call(
        body, name="fox_gate", grid=(1,),
        in_specs=[pl.BlockSpec((SEQ, LANES), lambda i: (0, COL_F // LANES)), pl.BlockSpec((1, LANES), lambda i: (0, 0))]
        + [_ANY] * len(deps),
        out_specs=pl.BlockSpec((F_ROWS, SEQ), lambda i: (0, 0)),
        out_shape=pltpu.HBM((F_ROWS, SEQ), F32),
    )(*_in_hbm(proj, b_pad), *deps)


def _fox_gate_bwd(d_cum, proj, b_pad, dproj):
    def body(d_ref, f_ref, b_ref, dproj_ref, dz_ref, db_ref):
        row = lax.broadcasted_iota(jnp.int32, (BLOCK, BLOCK), 0)
        col = lax.broadcasted_iota(jnp.int32, (BLOCK, BLOCK), 1)
        after = jnp.where(row >= col, 1.0, 0.0).astype(BF16)
        carry = jnp.zeros((F_ROWS, 1), F32)
        parts = [None] * N_BLOCKS
        for blk in reversed(range(N_BLOCKS)):
            run = _dot3_nn(d_ref[:, blk * BLOCK:(blk + 1) * BLOCK], after) + carry
            parts[blk] = run
            carry = run[:, 0:1]
        dlogf = jnp.concatenate(parts, axis=1)
        dlogf = jnp.concatenate([dlogf, jnp.zeros((LANES - F_ROWS, SEQ), F32)], axis=0).T
        dz = dlogf * _sigmoid(-(f_ref[...] + b_ref[...]))
        dz_ref[...] = dz.astype(BF16)
        db_ref[...] = jnp.sum(dz, axis=0, keepdims=True)

    f_cols = pl.BlockSpec((SEQ, LANES), lambda i: (0, COL_F // LANES))
    return pl.pallas_call(
        body, name="fox_gate_bwd", grid=(1,),
        in_specs=[pl.BlockSpec((F_ROWS, SEQ), lambda i: (0, 0)), f_cols, pl.BlockSpec((1, LANES), lambda i: (0, 0)), _ANY],
        out_specs=[f_cols, pl.BlockSpec((1, LANES), lambda i: (0, 0))],
        out_shape=[pltpu.HBM(dproj.shape, dproj.dtype), pltpu.HBM((1, LANES), F32)],
        input_output_aliases={3: 0},
    )(*_in_hbm(d_cum, proj, b_pad, dproj))


FOX_TILE = 256
FOX_TILES = SEQ // FOX_TILE


def _row_to_col(row):
    n = row.shape[1]
    eye = lax.broadcasted_iota(jnp.int32, (n, n), 0) == lax.broadcasted_iota(jnp.int32, (n, n), 1)
    return jnp.sum(jnp.where(eye, row, 0.0), axis=1, keepdims=True)


def _fox_bias(f_row, i):
    t = FOX_TILE
    ext = (i + 1) * t
    bias = _row_to_col(f_row[:, i * t:(i + 1) * t]) - f_row[:, :ext]
    row = lax.broadcasted_iota(jnp.int32, (t, ext), 0) + i * t
    col = lax.broadcasted_iota(jnp.int32, (t, ext), 1)
    return bias, col <= row


def _fox_specs():
    qkv = [pl.BlockSpec((SEQ, LANES), lambda p, base=base: (0, base // LANES + p)) for base in (COL_QB, COL_KB, COL_VB)]
    return qkv, pl.BlockSpec((F_ROWS, SEQ), lambda p: (0, 0))


def _fox_fwd(proj, f_rows):
    t = FOX_TILE

    def body(q_ref, k_ref, v_ref, f_ref, o_ref, lse_ref):
        pair = pl.program_id(0)
        upper = _upper_half()
        k16, v16 = k_ref[...].astype(BF16), v_ref[...].astype(BF16)
        f_row = [f_ref[pl.ds(2 * pair + e, 1), :] for e in range(2)]
        for i in range(FOX_TILES):
            ext = (i + 1) * t
            q_tile = (q_ref[i * t:(i + 1) * t, :] * QK_SCALE).astype(BF16)
            s2 = _dot_nt(_stack_heads(q_tile), k16[:ext])
            pns, lses = [], []
            for e in range(2):
                bias, causal = _fox_bias(f_row[e], i)
                s = jnp.where(causal, s2[e * t:(e + 1) * t] + bias, NEG_INF)
                m = jnp.max(s, axis=-1, keepdims=True)
                p = jnp.exp(s - m)
                den = jnp.sum(p, axis=-1, keepdims=True)
                pns.append((p * (1.0 / den)).astype(BF16))
                lses.append(m + jnp.log(den))
            o_ref[i * t:(i + 1) * t, :] = _unstack_heads(_dot_nn(jnp.concatenate(pns, axis=0), v16[:ext]))
            lse_ref[i * t:(i + 1) * t, :] = jnp.where(upper, lses[1], lses[0])

    qkv, f_spec = _fox_specs()
    tok = pl.BlockSpec((SEQ, LANES), lambda p: (0, p))
    return pl.pallas_call(
        body, name="fox_attn_fwd", grid=(FOX_WIDTH // LANES,),
        in_specs=qkv + [f_spec], out_specs=[tok, tok],
        out_shape=[pltpu.HBM((SEQ, FOX_WIDTH), F32)] * 2,
        compiler_params=pltpu.CompilerParams(
            dimension_semantics=("parallel",), vmem_limit_bytes=_vmem_limit(8 * t * SEQ * 4)),
    )(*_in_hbm(proj, proj, proj, f_rows))


def _fox_bwd(proj, do, lse, f_rows, dproj):
    t = FOX_TILE

    def body(q_ref, k_ref, v_ref, f_ref, do_ref, lse_ref, dproj_in, dproj_ref, df_ref, dk_acc, dv_acc,
             dq_out, dk_out, dv_out, sem):
        pair = pl.program_id(0)
        upper = _upper_half()
        k16, v16 = k_ref[...].astype(BF16), v_ref[...].astype(BF16)
        f_row = [f_ref[pl.ds(2 * pair + e, 1), :] for e in range(2)]
        dk_acc[...] = jnp.zeros_like(dk_acc)
        dv_acc[...] = jnp.zeros_like(dv_acc)
        df_ref[...] = jnp.zeros_like(df_ref)
        for i in range(FOX_TILES):
            ext = (i + 1) * t
            q_tile = (q_ref[i * t:(i + 1) * t, :] * QK_SCALE).astype(BF16)
            do_tile = do_ref[i * t:(i + 1) * t, :]
            lse_t = lse_ref[i * t:(i + 1) * t, :]
            q2, do2 = _stack_heads(q_tile), _stack_heads(do_tile)
            s2, dp2 = _dot_nt(q2, k16[:ext]), _dot_nt(do2, v16[:ext])
            ps, dss = [], []
            for e in range(2):
                bias, causal = _fox_bias(f_row[e], i)
                s = s2[e * t:(e + 1) * t] + bias
                p = jnp.where(causal, jnp.exp(s - lse_t[:, e * HEAD_DIM:e * HEAD_DIM + 1]), 0.0)
                dp = dp2[e * t:(e + 1) * t]
                ds = p * (dp - jnp.sum(p * dp, axis=-1, keepdims=True))
                df_ref[0, e:e + 1, :ext] -= jnp.sum(ds, axis=0, keepdims=True)
                ps.append(p.astype(BF16))
                dss.append(ds.astype(BF16))
            ds2, p2 = jnp.concatenate(dss, axis=0), jnp.concatenate(ps, axis=0)
            dq_out[i * t:(i + 1) * t, :] = (_unstack_heads(_dot_nn(ds2, k16[:ext])) * QK_SCALE).astype(BF16)
            dk_acc[:ext, :] += _dot_tn(ds2, q2)
            dv_acc[:ext, :] += _dot_tn(p2, do2)
        dk_out[...] = dk_acc[...].astype(BF16)
        dv_out[...] = dv_acc[...].astype(BF16)
        _store_columns((dq_out, dk_out, dv_out), dproj_ref, [base // LANES + pair for base in (COL_QB, COL_KB, COL_VB)],
                       sem)

    qkv, f_spec = _fox_specs()
    tok = pl.BlockSpec((SEQ, LANES), lambda p: (0, p))
    return pl.pallas_call(
        body, name="fox_attn_bwd", grid=(FOX_WIDTH // LANES,),
        in_specs=qkv + [f_spec, tok, tok, _ANY],
        out_specs=[_ANY, pl.BlockSpec((1, SUBLANES, SEQ), lambda p: (p, 0, 0))],
        out_shape=[pltpu.HBM(dproj.shape, dproj.dtype),
                   pltpu.HBM((FOX_WIDTH // LANES, SUBLANES, SEQ), F32)],
        scratch_shapes=[pltpu.VMEM((SEQ, LANES), F32)] * 2 + [pltpu.VMEM((SEQ, LANES), BF16)] * 3
        + [pltpu.SemaphoreType.DMA((3,))],
        input_output_aliases={6: 0},
        compiler_params=pltpu.CompilerParams(
            dimension_semantics=("arbitrary",), vmem_limit_bytes=_vmem_limit(10 * t * SEQ * 4)),
    )(*_in_hbm(proj, proj, proj, f_rows, do, lse, dproj))


MIX_TILE = 512


def _mix_out(out_a, out_b, proj, x, wt_pa, wt_pb, w_out, g_post, g_ffn_pre):
    tm = MIX_TILE

    def body(a_ref, b_ref, ga_ref, gb_ref, x_ref, wpa_ref, wpb_ref, wo_ref, g2_ref, g3_ref,
             merged_ref, mix_ref, x1_ref, h2_ref):
        ya = _dot_nn(a_ref[...].astype(BF16), wpa_ref[...])
        yb = _dot_nn(b_ref[...].astype(BF16), wpb_ref[...])
        merged = (_sigmoid(ga_ref[...]) * ya + _sigmoid(gb_ref[...]) * yb).astype(BF16)
        merged_ref[...] = merged
        mix = _dot_nn(merged, wo_ref[...])
        mix_ref[...] = mix
        x1 = x_ref[...] + mix * _rms_scale(mix) * g2_ref[...]
        x1_ref[...] = x1
        h2_ref[...] = (x1 * _rms_scale(x1) * g3_ref[...]).astype(BF16)

    def rows(w, cb=0):
        return pl.BlockSpec((tm, w), lambda i, cb=cb: (i, cb))

    def whole(a):
        return pl.BlockSpec(a.shape, lambda i: (0, 0))

    d = D_MODEL
    blk = _nbytes((tm, d), F32) * 6 + sum(_nbytes(a.shape, BF16) for a in (wt_pa, wt_pb, w_out))
    return pl.pallas_call(
        body, name="mix_out", grid=(SEQ // tm,),
        in_specs=[rows(DIL_OUT_WIDTH), rows(FOX_WIDTH), rows(d, COL_GA // d), rows(d, COL_GB // d), rows(d),
                  whole(wt_pa), whole(wt_pb), whole(w_out), whole(g_post), whole(g_ffn_pre)],
        out_specs=[rows(d)] * 4,
        out_shape=[pltpu.HBM((SEQ, d), dt) for dt in (BF16, F32, F32, BF16)],
        compiler_params=pltpu.CompilerParams(dimension_semantics=("parallel",), vmem_limit_bytes=_vmem_limit(blk)),
    )(*_in_hbm(out_a, out_b, proj, proj, x, wt_pa, wt_pb, w_out, g_post, g_ffn_pre))


def _mix_out_bwd(dmix, out_a, out_b, proj, wt_pa, wt_pb, w_out, deps=()):
    tm = MIX_TILE

    def body(dm_ref, a_ref, b_ref, ga_ref, gb_ref, wpa_ref, wpb_ref, wo_ref, *rest):
        dproj_ref, dya_ref, dyb_ref, da_ref, db_ref = rest[len(deps):]
        dmerged = _dot_nt(dm_ref[...], wo_ref[...])
        ya = _dot_nn(a_ref[...].astype(BF16), wpa_ref[...])
        yb = _dot_nn(b_ref[...].astype(BF16), wpb_ref[...])
        sa, sb = _sigmoid(ga_ref[...]), _sigmoid(gb_ref[...])
        dproj_ref[:, COL_GA:COL_GA + D_MODEL] = (dmerged * ya * (sa * (1.0 - sa))).astype(BF16)
        dproj_ref[:, COL_GB:COL_GB + D_MODEL] = (dmerged * yb * (sb * (1.0 - sb))).astype(BF16)
        dproj_ref[:, COL_GB + D_MODEL:] = jnp.zeros((tm, COL_QA - COL_GB - D_MODEL), BF16)
        dya = (dmerged * sa).astype(BF16)
        dyb = (dmerged * sb).astype(BF16)
        dya_ref[...] = dya
        dyb_ref[...] = dyb
        da_ref[...] = _dot_nt(dya, wpa_ref[...])
        db_ref[...] = _dot_nt(dyb, wpb_ref[...]).astype(BF16)

    def rows(w, cb=0):
        return pl.BlockSpec((tm, w), lambda i, cb=cb: (i, cb))

    def whole(a):
        return pl.BlockSpec(a.shape, lambda i: (0, 0))

    d = D_MODEL
    blk = _nbytes((tm, d), F32) * 8 + sum(_nbytes(a.shape, BF16) for a in (wt_pa, wt_pb, w_out))
    return pl.pallas_call(
        body, name="mix_out_bwd", grid=(SEQ // tm,),
        in_specs=[rows(d), rows(DIL_OUT_WIDTH), rows(FOX_WIDTH), rows(d, COL_GA // d), rows(d, COL_GB // d),
                  whole(wt_pa), whole(wt_pb), whole(w_out)] + [_ANY] * len(deps),
        out_specs=[rows(COL_QA)] + [rows(d)] * 2 + [rows(DIL_OUT_WIDTH), rows(FOX_WIDTH)],
        out_shape=[pltpu.HBM((SEQ, PROJ_COLS), BF16)] + [pltpu.HBM((SEQ, d), BF16)] * 2
        + [pltpu.HBM((SEQ, DIL_OUT_WIDTH), F32), pltpu.HBM((SEQ, FOX_WIDTH), BF16)],
        compiler_params=pltpu.CompilerParams(dimension_semantics=("parallel",), vmem_limit_bytes=_vmem_limit(blk)),
    )(*_in_hbm(dmix, out_a, out_b, proj, proj, wt_pa, wt_pb, w_out), *deps)


FFN_TM, FFN_TN = 2048, 256


def _ffn_up(h2, wt_gate, wt_up):
    tm, tn = FFN_TM, FFN_TN

    def body(h_ref, wg_ref, wu_ref, gate_ref, up_ref, act_ref):
        for rows in (slice(0, tm // 2), slice(tm // 2, tm)):
            gate = _dot_nt(h_ref[rows, :], wg_ref[...])
            up = _dot_nt(h_ref[rows, :], wu_ref[...])
            gate_ref[rows, :] = gate
            up_ref[rows, :] = up
            act_ref[rows, :] = (gate * _sigmoid(gate) * up).astype(BF16)

    tile = pl.BlockSpec((tm, tn), lambda i, j: (i, j))
    w_spec = pl.BlockSpec((tn, D_MODEL), lambda i, j: (j, 0))
    return pl.pallas_call(
        body, name="ffn_up", grid=(SEQ // tm, D_FF // tn),
        in_specs=[pl.BlockSpec((tm, D_MODEL), lambda i, j: (i, 0)), w_spec, w_spec],
        out_specs=[tile, tile, tile],
        out_shape=[pltpu.HBM((SEQ, D_FF), dt) for dt in (F32, F32, BF16)],
        compiler_params=pltpu.CompilerParams(
            dimension_semantics=("parallel", "parallel"), vmem_limit_bytes=_vmem_limit(8 * 2**20)),
    )(h2, wt_gate, wt_up)


def _ffn_act_bwd(dff, w_down, gate, up):
    tm, tn = FFN_TM, FFN_TN

    def body(d_ref, wd_ref, gate_ref, up_ref, dgate_ref, dup_ref):
        for rows in (slice(0, tm // 2), slice(tm // 2, tm)):
            dact = _dot_nt(d_ref[rows, :], wd_ref[...])
            gate = gate_ref[rows, :]
            sg = _sigmoid(gate)
            dgate_ref[rows, :] = (dact * up_ref[rows, :] * (sg * (1.0 + gate * (1.0 - sg)))).astype(BF16)
            dup_ref[rows, :] = (dact * (gate * sg)).astype(BF16)

    tile = pl.BlockSpec((tm, tn), lambda i, j: (i, j))
    return pl.pallas_call(
        body, name="ffn_act_bwd", grid=(SEQ // tm, D_FF // tn),
        in_specs=[pl.BlockSpec((tm, D_MODEL), lambda i, j: (i, 0)), pl.BlockSpec((tn, D_MODEL), lambda i, j: (j, 0)),
                  tile, tile],
        out_specs=[tile, tile],
        out_shape=[pltpu.HBM((SEQ, D_FF), BF16)] * 2,
        compiler_params=pltpu.CompilerParams(
            dimension_semantics=("parallel", "parallel"), vmem_limit_bytes=_vmem_limit(8 * 2**20)),
    )(dff, w_down, gate, up)


EPILOGUE_TM = 512


def _loss_head(act, w_down, x1, target, g_post):
    def fn(ff, x1, tgt, g):
        r = _rms_scale(ff)
        nrm = ff * r
        err = (x1 + nrm * g) - tgt
        loss = 0.5 * jnp.sum(jnp.mean(err * err, axis=-1, keepdims=True), axis=0, keepdims=True)
        dy = err * (1.0 / D_MODEL)
        u = dy * g
        dff = r * u - ff * (r * r * r) * jnp.mean(u * ff, axis=-1, keepdims=True)
        return dy, dff, jnp.broadcast_to(loss, (1, LANES)), jnp.sum(dy * nrm, axis=0, keepdims=True)

    d = D_MODEL
    return _matmul_rowwise([(act, w_down)], fn, "ffn_down_loss", EPILOGUE_TM, [(x1, d, 0), (target, d, 0)], [g_post],
                           [(d, F32), (d, BF16)], [LANES, d])


def _post_ffn_bwd(dgate, wt_gate, dup, wt_up, x1, dy, mix, g_ffn_pre, g_mix_post, deps=()):
    def fn(dh2, x1, dy, mix, g3, g2):
        dx, dg3 = _rms_bwd(x1, dh2, g3)
        dx1 = dy + dx
        dmix, dg2 = _rms_bwd(mix, dx1, g2)
        return dx1, dmix, dg3, dg2

    d = D_MODEL
    return _matmul_rowwise([(dgate, wt_gate), (dup, wt_up)], fn, "ffn_up_bwd", EPILOGUE_TM,
                           [(x1, d, 0), (dy, d, 0), (mix, d, 0)], [g_ffn_pre, g_mix_post],
                           [(d, F32), (d, BF16)], [d, d], deps=deps)


def _input_bwd(dproj, wt_r, x, dx1, g_pre, deps=()):
    def fn(dh, x, dx1, g):
        dx, dg = _rms_bwd(x, dh, g)
        return dx1 + dx, dg

    d = D_MODEL
    return _matmul_rowwise([(dproj, wt_r)], fn, "in_proj_bwd", EPILOGUE_TM, [(x, d, 0), (dx1, d, 0)], [g_pre],
                           [(d, F32)], [d], deps=deps)


def _adam_math(w, g, m, v):
    m = ADAM_B1 * m + (1.0 - ADAM_B1) * g
    v = ADAM_B2 * v + (1.0 - ADAM_B2) * (g * g)
    m_hat = m / (1.0 - ADAM_B1 ** ADAM_STEP)
    v_hat = v / (1.0 - ADAM_B2 ** ADAM_STEP)
    delta = -ADAM_LR * (m_hat / (jnp.sqrt(v_hat) + ADAM_EPS) + ADAM_WD * w)
    return delta, m, v


def _adam(w, mine, recv, m, v, name):
    r, c = w.shape
    tc = _col_tile(r, c)

    def body(w_ref, p_ref, r_ref, m_ref, v_ref, g_ref, d_ref, nm_ref, nv_ref):
        g = ((p_ref[...] + r_ref[0].astype(F32)) + r_ref[1].astype(F32)) + r_ref[2].astype(F32)
        g_ref[...] = g
        d_ref[...], nm_ref[...], nv_ref[...] = _adam_math(w_ref[...], g, m_ref[...], v_ref[...])

    spec = pl.BlockSpec((r, tc), lambda j: (0, j))
    return pl.pallas_call(
        body, name=name, grid=(c // tc,),
        in_specs=[spec, spec, pl.BlockSpec((3, r, tc), lambda j: (0, 0, j)), spec, spec], out_specs=[spec] * 4,
        out_shape=[pltpu.HBM((r, c), F32)] * 4,
        compiler_params=pltpu.CompilerParams(dimension_semantics=("parallel",)),
    )(*_in_hbm(w, mine, recv, m, v))


def _adam_small(gathered, ws, ms, vs, loss_parts):
    n = len(ws)

    def body(*refs):
        outs = refs[4 * n + 1:]
        loss = refs[4 * n][0]
        for dev in range(1, N_DEV):
            loss = loss + refs[4 * n][dev]
        outs[4 * n][...] = loss
        for i in range(n):
            ga_ref, w_ref, m_ref, v_ref = (refs[j * n + i] for j in range(4))
            g = ga_ref[0]
            for dev in range(1, N_DEV):
                g = g + ga_ref[dev]
            g = g[:, :w_ref.shape[1]]
            outs[4 * i][...] = g
            outs[4 * i + 1][...], outs[4 * i + 2][...], outs[4 * i + 3][...] = _adam_math(
                w_ref[...], g, m_ref[...], v_ref[...])

    out_shape = [pltpu.HBM(w.shape, F32) for w in ws for _ in range(4)]
    out_shape.append(pltpu.HBM((1, LANES), F32))
    out = pl.pallas_call(body, name="adam_small", out_shape=out_shape)(*gathered, *ws, *ms, *vs, loss_parts)
    return [out[4 * i:4 * i + 4] for i in range(n)], out[4 * n]


_PROJ_SEGMENTS = ((3848, 5896), (None, COL_QA - 2 * D_MODEL), (0, 3840), (3840, 3848), (None, PROJ_COLS - COL_F - 8))


def _proj_weight_t(gathered):
    pieces, zeros, at = [], [], 0
    for lo, hi in _PROJ_SEGMENTS:
        if lo is None:
            zeros.append((at, hi))
            at += hi
            continue
        for dev in range(lo // IN_SHARD, (hi - 1) // IN_SHARD + 1):
            a, b = max(lo, dev * IN_SHARD), min(hi, (dev + 1) * IN_SHARD)
            pieces.append((dev, a - dev * IN_SHARD, at + a - lo, b - a))
        at += hi - lo
    assert at == PROJ_COLS
    tc = 2 * LANES

    def body(g_ref, o_ref, shards, rows):
        for dev in range(N_DEV):
            shards[dev] = g_ref[dev].astype(F32)
        for dev, src, dst, n in pieces:
            rows[pl.ds(dst, n), :] = shards[dev, pl.ds(src, n), :]
        for dst, n in zeros:
            rows[pl.ds(dst, n), :] = jnp.zeros((n, tc), F32)
        o_ref[...] = rows[...].astype(o_ref.dtype)

    return pl.pallas_call(
        body, name="w_in_rows", grid=(D_MODEL // tc,),
        in_specs=[pl.BlockSpec((N_DEV, IN_SHARD, tc), lambda j: (0, 0, j))],
        out_specs=pl.BlockSpec((PROJ_COLS, tc), lambda j: (0, j)),
        out_shape=pltpu.HBM((PROJ_COLS, D_MODEL), gathered.dtype),
        scratch_shapes=[pltpu.VMEM((N_DEV, IN_SHARD, tc), F32), pltpu.VMEM((PROJ_COLS, tc), F32)],
        compiler_params=pltpu.CompilerParams(
            dimension_semantics=("parallel",), vmem_limit_bytes=_vmem_limit(2 * _nbytes((PROJ_COLS, tc), F32))),
    )(*_in_hbm(gathered))


def _proj_weight_grad_slots(dwt_r):
    starts, at = [], 0
    for lo, hi in _PROJ_SEGMENTS:
        if lo is not None:
            starts.append((lo, hi, at))
        at += hi if lo is None else hi - lo
    pieces = []
    for dev in range(N_DEV):
        lo, end = dev * IN_SHARD, (dev + 1) * IN_SHARD
        for seg_lo, seg_hi, seg_at in sorted(starts):
            a, b = max(lo, seg_lo), min(end, seg_hi)
            if a < b:
                pieces.append((dev, a - lo, seg_at + a - seg_lo, b - a))

    def body(g_ref, o_ref):
        for dev, dst, src, rows in pieces:
            o_ref[dev, pl.ds(dst, rows), :] = g_ref[pl.ds(src, rows), :]

    tc = 2 * LANES
    return pl.pallas_call(
        body, name="grad_w_in_slots", grid=(D_MODEL // tc,),
        in_specs=[pl.BlockSpec((PROJ_COLS, tc), lambda j: (0, j))],
        out_specs=pl.BlockSpec((N_DEV, IN_SHARD, tc), lambda j: (0, 0, j)),
        out_shape=pltpu.HBM((N_DEV, IN_SHARD, D_MODEL), F32),
        compiler_params=pltpu.CompilerParams(
            dimension_semantics=("parallel",), vmem_limit_bytes=_vmem_limit(2 * _nbytes((PROJ_COLS, tc), F32))),
    )(*_in_hbm(dwt_r))


def kernel(x, w_in, w_proj_a, w_proj_b, w_out, b_forget, w_ffn_gate, w_ffn_up, w_ffn_down, norm_mix_pre, norm_mix_post, norm_ffn_pre, norm_ffn_post, loss_target, m_w_in, m_w_proj_a, m_w_proj_b, m_w_out, m_b_forget, m_w_ffn_gate, m_w_ffn_up, m_w_ffn_down, m_norm_mix_pre, m_norm_mix_post, m_norm_ffn_pre, m_norm_ffn_post, v_w_in, v_w_proj_a, v_w_proj_b, v_w_out, v_b_forget, v_w_ffn_gate, v_w_ffn_up, v_w_ffn_down, v_norm_mix_pre, v_norm_mix_post, v_norm_ffn_pre, v_norm_ffn_post):
    d = D_MODEL
    names = ("w_in", "w_proj_a", "w_proj_b", "w_out", "w_ffn_gate", "w_ffn_up", "w_ffn_down")
    col_sharded = ("w_in", "w_ffn_gate", "w_ffn_up")

    def row_shards(arrs):
        return {k: (a[0].T if k in col_sharded else a[0]) for k, a in zip(names, arrs)}

    shards = row_shards((w_in, w_proj_a, w_proj_b, w_out, w_ffn_gate, w_ffn_up, w_ffn_down))
    moments_m = row_shards((m_w_in, m_w_proj_a, m_w_proj_b, m_w_out, m_w_ffn_gate, m_w_ffn_up, m_w_ffn_down))
    moments_v = row_shards((v_w_in, v_w_proj_a, v_w_proj_b, v_w_out, v_w_ffn_gate, v_w_ffn_up, v_w_ffn_down))
    pos = jnp.stack([lax.axis_index("c"), 2 * lax.axis_index("x") + lax.axis_index("y")]).astype(jnp.int32)
    x2, target = x[0], loss_target[0]

    me = 4 * lax.axis_index("x") + 2 * lax.axis_index("y") + lax.axis_index("c")
    mid_names, ffn_names = names[1:4], names[4:]
    first_names, later_names = names[:1], names[1:]
    shards16 = {k: shards[k].astype(BF16) for k in names}

    def landing(k):
        return lax.dynamic_update_slice(lax.empty((N_DEV,) + shards[k].shape, BF16), shards16[k][None], (me, 0, 0))

    ag_first = _exchange_start("ag_first_chips_start", _gather_two_route_copies, [shards16[k] for k in first_names],
                               [landing(k) for k in first_names], 4 * len(first_names))
    h = _rowwise(lambda xb, g: xb * _rms_scale(xb) * g, "norm_mix_pre", SEQ, 512, [(x2, d, 0)], [norm_mix_pre],
                 [(d, BF16)], deps=[ag_first.token])[0]
    later_lands = [landing(k) for k in later_names]
    _, ag_first_diagonal = _gather_relay(ag_first, "ag_first", [h, shards["w_in"], moments_m["w_in"], moments_v["w_in"],
                                                               *later_lands, *[shards16[k] for k in later_names]])
    relayed, ag_first_last = ag_first_diagonal([])
    ag_later = _exchange_start("ag_later_chips_start", _gather_two_route_copies, [shards16[k] for k in later_names],
                               later_lands, 4 * len(later_names), after=[relayed])
    gathered = dict(zip(first_names, ag_first_last([ag_later.token])))
    wt_r = _proj_weight_t(gathered["w_in"])

    proj = _matmul([(h, *_in_hbm(wt_r))], "nt", F32, "in_proj", 1024, 896, 1024)
    tables = _rope_tables()
    relayed, ag_later_diagonal = _gather_relay(ag_later, "ag_later", [proj])
    o_dil, lse_dil = _dil_fwd(proj, tables, deps=[relayed])
    out_a = _dil_combine(o_dil, lse_dil)
    relayed, ag_later_last = ag_later_diagonal([out_a])

    b_pad = jnp.pad(b_forget, ((0, 0), (0, LANES - N_FOX_HEADS)))
    f_rows = _fox_gate(proj, b_pad, deps=[relayed])
    out_b, lse_fox = _fox_fwd(proj, f_rows)

    gathered = dict(zip(later_names, ag_later_last([out_b])))
    wt_pa = gathered["w_proj_a"].transpose(1, 0, 2).reshape(DIL_OUT_WIDTH, d)
    wt_pb = gathered["w_proj_b"].transpose(1, 0, 2).reshape(FOX_WIDTH, d)
    w_o = gathered["w_out"].reshape(d, d)
    wt_g = gathered["w_ffn_gate"].reshape(D_FF, d)
    wt_u = gathered["w_ffn_up"].reshape(D_FF, d)
    w_d = gathered["w_ffn_down"].reshape(D_FF, d)
    merged, mix, x1, h2 = _mix_out(out_a, out_b, proj, x2, wt_pa, wt_pb, w_o, norm_mix_post, norm_ffn_pre)

    gate, up, act = _ffn_up(h2, wt_g, wt_u)
    dy, dff, loss_part, dg_ffn_post = _loss_head(act, w_d, x1, target, norm_ffn_post)

    dgate, dup = _ffn_act_bwd(dff, w_d, gate, up)
    grads_t = {}
    grads_t["w_ffn_down"] = _matmul([(act, dff)], "tn", F32, "grad_w_ffn_down", 1408, 512, 2048, staged=False)
    grads_t["w_ffn_gate"] = _matmul([(dgate, h2)], "tn", F32, "grad_w_ffn_gate", 1408, 512, 2048)
    grads_t["w_ffn_up"] = _matmul([(dup, h2)], "tn", F32, "grad_w_ffn_up", 1408, 512, 2048)
    rs_ffn = _ReduceScatter("ffn", {k: grads_t[k] for k in ffn_names}, pos)
    dx1, dmix, dg_ffn_pre, dg_mix_post = _post_ffn_bwd(dgate, wt_g, dup, wt_u, x1, dy, mix, norm_ffn_pre, norm_mix_post,
                                                       deps=[rs_ffn.token])
    rs_ffn.start_chips([dmix])

    dproj, dya, dyb, d_out_a, d_out_b = _mix_out_bwd(dmix, out_a, out_b, proj, wt_pa, wt_pb, w_o, deps=[rs_ffn.token])
    grads_t["w_out"] = _matmul([(merged, dmix)], "tn", F32, "grad_w_out", 1024, 1024, 1024)
    def column_slots(g):
        return g.reshape(g.shape[0], N_DEV, LANES).transpose(1, 0, 2)

    grads_t["w_proj_a"] = column_slots(_matmul([(out_a, dya)], "tn", F32, "grad_w_proj_a", DIL_OUT_WIDTH, 1024, SEQ))
    grads_t["w_proj_b"] = column_slots(_matmul([(out_b, dyb)], "tn", F32, "grad_w_proj_b", FOX_WIDTH, 1024, SEQ))
    rs_mid = _ReduceScatter("mid", {k: grads_t[k] for k in mid_names}, pos)

    do_dil, c_dil = _dil_combine_bwd(d_out_a, o_dil, lse_dil, deps=[rs_mid.token])
    rs_mid.start_chips([c_dil])
    dproj, d_cum = _fox_bwd(proj, d_out_b, lse_fox, f_rows, dproj)
    d_cum_rows = jnp.pad(d_cum[:, :2].reshape(N_FOX_HEADS, SEQ), ((0, F_ROWS - N_FOX_HEADS), (0, 0)))
    dproj, db_part = _fox_gate_bwd(d_cum_rows, proj, b_pad, dproj)
    dproj = _dil_bwd(proj, tables, do_dil, lse_dil, c_dil, dproj, deps=[rs_mid.token])

    dwt_r = _matmul([(dproj, h)], "tn", F32, "grad_w_in", 896, 1024, 2048)
    rs_in = _ReduceScatter("in", {"w_in": _proj_weight_grad_slots(dwt_r)}, pos)
    def finish(rs, after):
        return {k: _adam(shards[k], mine, recv, moments_m[k], moments_v[k], "adam_" + k)
                for k, (mine, recv) in rs.finish(after).items()}

    done = finish(rs_ffn, [rs_in.token])
    rs_in.start_chips([done[k][0] for k in ffn_names])
    grad_x, dg_mix_pre = _input_bwd(dproj, wt_r, x2, dx1, norm_mix_pre, deps=[rs_in.token])
    done.update(finish(rs_mid, [grad_x]))

    small_all = _all_gather([dg_mix_pre, dg_mix_post, dg_ffn_pre, dg_ffn_post, db_part, loss_part],
                            "small_grads_all_gather", deps=[done[k][0] for k in mid_names])
    small, loss = _adam_small(small_all[:5], [norm_mix_pre, norm_mix_post, norm_ffn_pre, norm_ffn_post, b_forget],
                              [m_norm_mix_pre, m_norm_mix_post, m_norm_ffn_pre, m_norm_ffn_post, m_b_forget],
                              [v_norm_mix_pre, v_norm_mix_post, v_norm_ffn_pre, v_norm_ffn_post, v_b_forget],
                              small_all[5])

    done.update(finish(rs_in, [small[0][0]]))

    def leaves(i):
        def nat(k):
            a = done[k][i]
            return (a.T if k in col_sharded else a)[None]

        return [nat("w_in"), nat("w_proj_a"), nat("w_proj_b"), nat("w_out"), small[4][i],
                nat("w_ffn_gate"), nat("w_ffn_up"), nat("w_ffn_down"), *[small[r][i] for r in range(4)]]

    return (loss[0, 0], grad_x[None], *leaves(0), *leaves(1), *leaves(2), *leaves(3))
```

```python
import functools
import math

import jax
import jax.numpy as jnp
import numpy as np
from jax import lax
from jax.experimental import pallas as pl
from jax.experimental.pallas import tpu as pltpu

F32 = jnp.float32
BF16 = jnp.bfloat16
MESH = pl.DeviceIdType.MESH

D_MODEL = 1024
SEQ = 2048
HEAD_DIM = 64
BLOCK = 128
N_BLOCKS = SEQ // BLOCK
DILATIONS = (1, 4, 16)
N_FOX_HEADS = 8
DIL_WIDTH = 768
DIL_OUT_WIDTH = 256
FOX_WIDTH = 512
D_FF = 2816
ROPE_THETA = 500000.0
ROPE_DIM = HEAD_DIM // 4
ROPE_HALF = ROPE_DIM // 2
EPS = 1e-6
NEG_INF = -1e30
QK_SCALE = 1.0 / math.sqrt(HEAD_DIM)
IN_COLS = 5896
N_DEV = 8
IN_SHARD = IN_COLS // N_DEV

ADAM_LR = 0.001
ADAM_B1 = 0.9
ADAM_B2 = 0.999
ADAM_EPS = 1e-08
ADAM_WD = 0.01
ADAM_STEP = 10

V7X_VMEM_BYTES = 64 * 2**20
LANES = 128
SUBLANES = 8

PROJ_COLS = 6272
COL_GA, COL_GB = 0, 1024
COL_QA, COL_KA, COL_VA = 2304, 3072, 3840
COL_QB, COL_KB, COL_VB = 4608, 5120, 5632
COL_F = 6144
F_ROWS = 16


def _vmem_limit(block_bytes):
    want = 2 * block_bytes + 16 * 2**20
    return int(min(max(want, 32 * 2**20), V7X_VMEM_BYTES - 8 * 2**20))


def _nbytes(shape, dtype):
    return math.prod(shape) * jnp.dtype(dtype).itemsize


def _in_hbm(*arrays):
    return [pltpu.with_memory_space_constraint(a, pltpu.HBM) for a in arrays]


def _dot(a, b, dims):
    return lax.dot_general(a, b, (dims, ((), ())), preferred_element_type=F32)


def _dot_nn(a, b):
    return _dot(a, b, ((1,), (0,)))


def _dot_nt(a, b):
    return _dot(a, b, ((1,), (1,)))


def _dot_tn(a, b):
    return _dot(a, b, ((0,), (0,)))


def _sigmoid(z):
    return 1.0 / (1.0 + jnp.exp(-z))


def _split3(x):
    hi = x.astype(BF16)
    r1 = x - hi.astype(F32)
    mid = r1.astype(BF16)
    lo = (r1 - mid.astype(F32)).astype(BF16)
    return hi, mid, lo


def _dot3_nn(x, ones_matrix):
    hi, mid, lo = _split3(x)
    return (_dot_nn(hi, ones_matrix) + _dot_nn(mid, ones_matrix)) + _dot_nn(lo, ones_matrix)


def _rowwise(fn, name, n_rows, tm, row_ins, bcast_ins, row_outs, acc_outs=(), deps=()):
    n_in = len(row_ins) + len(bcast_ins)
    n_ro = len(row_outs)

    def body(*refs):
        res = fn(*[r[...] for r in refs[:n_in]])
        if not isinstance(res, (tuple, list)):
            res = (res,)
        outs = refs[n_in + len(deps):]
        for r, o in zip(res[:n_ro], outs[:n_ro]):
            o[...] = r.astype(o.dtype)
        first = pl.program_id(0) == 0
        for r, o in zip(res[n_ro:], outs[n_ro:]):
            _accumulate(o, r, first)

    in_specs = [pl.BlockSpec((tm, w), lambda i, cb=cb: (i, cb)) for _, w, cb in row_ins]
    in_specs += [pl.BlockSpec(a.shape, lambda i: (0, 0)) for a in bcast_ins]
    in_specs += [pl.BlockSpec(memory_space=pl.ANY)] * len(deps)
    out_specs = [pl.BlockSpec((tm, w), lambda i: (i, 0)) for w, _ in row_outs]
    out_specs += [pl.BlockSpec((1, w), lambda i: (0, 0)) for w in acc_outs]
    out_shape = [pltpu.HBM((n_rows, w), dt) for w, dt in row_outs]
    out_shape += [pltpu.HBM((1, w), F32) for w in acc_outs]
    blk = sum(_nbytes((tm, w), a.dtype) for a, w, _ in row_ins) + sum(_nbytes((tm, w), dt) for w, dt in row_outs)
    return pl.pallas_call(
        body, name=name, grid=(n_rows // tm,), in_specs=in_specs, out_specs=out_specs, out_shape=out_shape,
        compiler_params=pltpu.CompilerParams(
            dimension_semantics=("arbitrary" if acc_outs else "parallel",), vmem_limit_bytes=_vmem_limit(3 * blk)),
    )(*_in_hbm(*[a for a, _, _ in row_ins], *bcast_ins), *deps)


def _accumulate(o_ref, part, first):
    @pl.when(first)
    def _():
        o_ref[...] = part

    @pl.when(jnp.logical_not(first))
    def _():
        o_ref[...] += part


_MM_DIMS = {"nn": ((1,), (0,)), "nt": ((1,), (1,)), "tn": ((0,), (0,))}


def _matmul(pairs, mode, out_dtype, name, tm, tn, tk, deps=(), staged=True):
    a0, b0 = pairs[0]
    if mode == "tn":
        kk, m = a0.shape
    else:
        m, kk = a0.shape
    n = b0.shape[0] if mode == "nt" else b0.shape[1]
    assert m % tm == 0 and n % tn == 0 and kk % tk == 0, (name, m, n, kk)
    nk = kk // tk
    n_pairs = len(pairs)
    dims = _MM_DIMS[mode]
    n_in = 2 * n_pairs + len(deps)

    def body(*refs):
        o_ref = refs[n_in]
        part = None
        for p in range(n_pairs):
            d = _dot(refs[2 * p][...].astype(BF16), refs[2 * p + 1][...].astype(BF16), dims)
            part = d if part is None else part + d
        if nk == 1:
            o_ref[...] = part.astype(o_ref.dtype)
            return
        acc = refs[n_in + 1]
        k = pl.program_id(2)

        @pl.when(k == 0)
        def _():
            acc[...] = part

        @pl.when(k > 0)
        def _():
            acc[...] += part

        @pl.when(k == nk - 1)
        def _():
            o_ref[...] = acc[...].astype(o_ref.dtype)

    if mode == "tn":
        a_spec = pl.BlockSpec((tk, tm), lambda i, j, k: (k, i))
    else:
        a_spec = pl.BlockSpec((tm, tk), lambda i, j, k: (i, k))
    if mode == "nt":
        b_spec = pl.BlockSpec((tn, tk), lambda i, j, k: (j, k))
    else:
        b_spec = pl.BlockSpec((tk, tn), lambda i, j, k: (k, j))
    blk = sum(_nbytes((tm, tk), a.dtype) + _nbytes((tk, tn), b.dtype) for a, b in pairs) + 2 * _nbytes((tm, tn), F32)
    flat = [a for pair in pairs for a in pair]
    return pl.pallas_call(
        body, name=name, grid=(m // tm, n // tn, nk),
        in_specs=[a_spec, b_spec] * n_pairs + [pl.BlockSpec(memory_space=pl.ANY)] * len(deps),
        out_specs=pl.BlockSpec((tm, tn), lambda i, j, k: (i, j)),
        out_shape=pltpu.HBM((m, n), out_dtype),
        scratch_shapes=[] if nk == 1 else [pltpu.VMEM((tm, tn), F32)],
        compiler_params=pltpu.CompilerParams(
            dimension_semantics=("parallel", "parallel", "arbitrary"), vmem_limit_bytes=_vmem_limit(blk)),
    )(*(flat if staged else _in_hbm(*flat)), *deps)


def _matmul_rowwise(pairs, fn, name, tm, row_ins, bcast_ins, row_outs, acc_outs=(), deps=()):
    m = pairs[0][0].shape[0]
    n_mm, n_in = 2 * len(pairs), len(row_ins) + len(bcast_ins)
    n_ro = len(row_outs)

    def body(*refs):
        prod = None
        for p in range(len(pairs)):
            part = _dot_nn(refs[2 * p][...].astype(BF16), refs[2 * p + 1][...].astype(BF16))
            prod = part if prod is None else prod + part
        res = fn(prod, *[r[...] for r in refs[n_mm:n_mm + n_in]])
        outs = refs[n_mm + n_in + len(deps):]
        for r, o in zip(res[:n_ro], outs[:n_ro]):
            o[...] = r.astype(o.dtype)
        first = pl.program_id(0) == 0
        for r, o in zip(res[n_ro:], outs[n_ro:]):
            _accumulate(o, r, first)

    in_specs = []
    for a, b in pairs:
        in_specs += [pl.BlockSpec((tm, a.shape[1]), lambda i: (i, 0)),
                     pl.BlockSpec(b.shape, lambda i: (0, 0), pipeline_mode=pl.Buffered(1))]
    in_specs += [pl.BlockSpec((tm, w), lambda i, cb=cb: (i, cb)) for _, w, cb in row_ins]
    in_specs += [pl.BlockSpec(a.shape, lambda i: (0, 0)) for a in bcast_ins]
    in_specs += [_ANY] * len(deps)
    out_specs = [pl.BlockSpec((tm, w), lambda i: (i, 0)) for w, _ in row_outs]
    out_specs += [pl.BlockSpec((1, w), lambda i: (0, 0)) for w in acc_outs]
    out_shape = [pltpu.HBM((m, w), dt) for w, dt in row_outs]
    out_shape += [pltpu.HBM((1, w), F32) for w in acc_outs]
    blk = sum(_nbytes((tm, a.shape[1]), a.dtype) + _nbytes(b.shape, b.dtype) // 2 for a, b in pairs)
    blk += sum(_nbytes((tm, w), a.dtype) for a, w, _ in row_ins) + sum(_nbytes((tm, w), dt) for w, dt in row_outs)
    return pl.pallas_call(
        body, name=name, grid=(m // tm,), in_specs=in_specs, out_specs=out_specs, out_shape=out_shape,
        compiler_params=pltpu.CompilerParams(dimension_semantics=("arbitrary",), vmem_limit_bytes=_vmem_limit(blk)),
    )(*[a for pair in pairs for a in pair], *[a for a, _, _ in row_ins], *bcast_ins, *deps)


def _rms_scale(x):
    return lax.rsqrt(jnp.mean(x * x, axis=-1, keepdims=True) + EPS)


def _rms_bwd(xin, dyn, g):
    r = _rms_scale(xin)
    u = dyn * g
    dx = r * u - xin * (r * r * r) * jnp.mean(u * xin, axis=-1, keepdims=True)
    dg = jnp.sum(dyn * xin * r, axis=0, keepdims=True)
    return dx, dg


def _mesh_pos():
    return lax.axis_index("x"), lax.axis_index("y"), lax.axis_index("c")


def _all_gather(xs, name, deps=()):
    n = len(xs)

    def body(*refs):
        x_refs, out_refs = refs[:n], refs[n + len(deps):2 * n + len(deps)]
        send_sems, recv_sems, local_sems = refs[2 * n + len(deps):]
        mx, my, mc = _mesh_pos()
        me, sib = (mx, my, mc), (mx, my, 1 - mc)
        chips = [(1 - mx, my), (mx, 1 - my), (1 - mx, 1 - my)]

        def slot(a, dev):
            px, py, pc = dev
            return out_refs[a].at[4 * px + 2 * py + pc]

        def copy(k, a, block, to, src=None):
            return pltpu.make_async_remote_copy(
                src_ref=slot(a, block) if src is None else src, dst_ref=slot(a, block),
                send_sem=send_sems.at[a * 7 + k], recv_sem=recv_sems.at[a * 7 + k],
                device_id=to, device_id_type=MESH)

        mine = [pltpu.make_async_copy(x_refs[a], slot(a, me), local_sems.at[a]) for a in range(n)]
        for cp in mine:
            cp.start()
        first = []
        for a in range(n):
            first.append(copy(0, a, me, sib, x_refs[a]))
            first += [copy(1 + j, a, me, (*chip, mc), x_refs[a]) for j, chip in enumerate(chips)]
        for cp in first:
            cp.start()
        passed = []
        for a in range(n):
            for j, chip in enumerate(chips):
                copy(1 + j, a, (*chip, mc), me).wait_recv()
                fwd = copy(4 + j, a, (*chip, mc), sib)
                fwd.start()
                passed.append(fwd)
        for a in range(n):
            copy(0, a, sib, me).wait_recv()
            for j, chip in enumerate(chips):
                copy(4 + j, a, (*chip, 1 - mc), me).wait_recv()
        for cp in first + passed:
            cp.wait_send()
        for cp in mine:
            cp.wait()

    hbm = pl.BlockSpec(memory_space=pl.ANY)
    return pl.pallas_call(
        body, name=name,
        out_shape=[pltpu.HBM((N_DEV,) + x.shape, x.dtype) for x in xs],
        in_specs=[hbm] * (n + len(deps)), out_specs=[hbm] * n,
        scratch_shapes=[pltpu.SemaphoreType.DMA((7 * n,)), pltpu.SemaphoreType.DMA((7 * n,)),
                        pltpu.SemaphoreType.DMA((n,))],
    )(*xs, *deps)


_HBM = pl.BlockSpec(memory_space=pltpu.HBM)
_SEM = pl.BlockSpec(memory_space=pltpu.SEMAPHORE)
_ANY = pl.BlockSpec(memory_space=pl.ANY)
_DATAFLOW = pltpu.SideEffectType.DATAFLOW_SIDE_EFFECTING


def _flip_peer(flip):
    mx, my, mc = _mesh_pos()
    return (1 - mx if flip & 2 else mx, 1 - my if flip & 1 else my, mc)


def _remote(src, dst, send_sems, recv_sems, k, peer):
    return pltpu.make_async_remote_copy(src_ref=src, dst_ref=dst, send_sem=send_sems.at[k], recv_sem=recv_sems.at[k],
                                        device_id=peer, device_id_type=MESH)


def _scatter_sibling_copies(srcs, lands, send_sems, recv_sems):
    mx, my, mc = _mesh_pos()
    return [_remote(srcs[a].at[k, 1 - mc], lands[a].at[k], send_sems, recv_sems, 4 * a + k, (mx, my, 1 - mc))
            for a in range(len(srcs)) for k in range(4)]


def _scatter_chips_copies(srcs, lands, send_sems, recv_sems):
    mx, my, _ = _mesh_pos()
    k0 = 2 * mx + my
    return [_remote(srcs[a].at[jnp.bitwise_xor(k0, flip)], lands[a].at[flip - 1], send_sems, recv_sems,
                    3 * a + flip - 1, _flip_peer(flip))
            for a in range(len(srcs)) for flip in (1, 2, 3)]


class _Exchange:
    def __init__(self, copies, n_src, send_sems, recv_sems, thru, token):
        self.copies, self.n_src, self.send_sems, self.recv_sems, self.thru, self.token = (
            copies, n_src, send_sems, recv_sems, thru, token)


def _exchange_start(name, copies, srcs, lands, n_copies, after=()):
    bufs = list(srcs) + list(lands)
    nb, ns = len(bufs), len(srcs)

    def body(*refs):
        send_sems, recv_sems = refs[nb + len(after)], refs[nb + len(after) + 1]
        for cp in copies(refs[:ns], refs[ns:nb], send_sems, recv_sems):
            cp.start()
        refs[-1][...] = jnp.zeros_like(refs[-1])

    out = pl.pallas_call(
        body, name=name,
        out_shape=(pltpu.SemaphoreType.DMA((n_copies,)), pltpu.SemaphoreType.DMA((n_copies,)),
                   *[pltpu.HBM(b.shape, b.dtype) for b in bufs], pltpu.HBM((SUBLANES, LANES), F32)),
        in_specs=[_HBM] * nb + [_ANY] * len(after),
        out_specs=(_SEM, _SEM, *[_HBM] * nb, pl.BlockSpec(memory_space=pltpu.VMEM)),
        input_output_aliases={i: 2 + i for i in range(nb)},
        compiler_params=pltpu.CompilerParams(has_side_effects=_DATAFLOW),
    )(*[pltpu.with_memory_space_constraint(b, pltpu.HBM) for b in bufs], *after)
    return _Exchange(copies, ns, out[0], out[1], list(out[2:2 + nb]), out[-1])


def _exchange_wait(name, ex, after):
    nb, ns = len(ex.thru), ex.n_src

    def body(*refs):
        for cp in ex.copies(refs[:ns], refs[ns:nb], refs[nb], refs[nb + 1]):
            cp.wait_send()
            cp.wait_recv()

    out = pl.pallas_call(
        body, name=name, out_shape=tuple(pltpu.HBM(b.shape, b.dtype) for b in ex.thru),
        in_specs=[_HBM] * nb + [_SEM, _SEM] + [_ANY] * len(after), out_specs=tuple([_HBM] * nb),
        input_output_aliases={i: i for i in range(nb)},
        compiler_params=pltpu.CompilerParams(has_side_effects=_DATAFLOW),
    )(*ex.thru, ex.send_sems, ex.recv_sems, *after)
    return list(out[:ns]), list(out[ns:])


def _halves(ref):
    half = ref.shape[1] // 2
    if half % LANES == 0:
        return ref.at[:, pl.ds(0, half)], ref.at[:, pl.ds(half, half)]
    half = ref.shape[0] // 2
    assert half % (2 * SUBLANES) == 0, ref.shape
    return ref.at[pl.ds(0, half)], ref.at[pl.ds(half, half)]


def _gather_two_route_copies(srcs, lands, send_sems, recv_sems):
    mx, my, mc = _mesh_pos()
    me = 4 * mx + 2 * my + mc
    return [_remote(_halves(srcs[a])[h], _halves(lands[a].at[me])[h], send_sems, recv_sems, 4 * a + i, _flip_peer(flip))
            for a in range(len(srcs)) for i, (flip, h) in enumerate(((2, 0), (1, 1), (2, 1), (1, 0)))]


def _to_sibling(land, flip, h, send_sems, recv_sems, k):
    mx, my, mc = _mesh_pos()
    part = _halves(land.at[2 * jnp.bitwise_xor(2 * mx + my, flip) + mc])[h]
    return _remote(part, part, send_sems, recv_sems, k, (mx, my, 1 - mc))


def _second_hop(land, send_sems, recv_sems, k):
    mx, my, mc = _mesh_pos()
    k0 = 2 * mx + my
    from_y = _halves(land.at[2 * jnp.bitwise_xor(k0, 1) + mc])[1]
    from_x = _halves(land.at[2 * jnp.bitwise_xor(k0, 2) + mc])[0]
    return (_remote(from_y, from_y, send_sems, recv_sems, k, _flip_peer(2)),
            _remote(from_x, from_x, send_sems, recv_sems, k + 1, _flip_peer(1)))


def _relay_call(name, body, bufs, sems, n_new, after):
    nb, n_in = len(bufs), len(bufs) + len(sems) + len(after)

    def call_body(*refs):
        body(refs[:nb], refs[nb:nb + len(sems)], refs[n_in], refs[n_in + 1])
        refs[-1][...] = jnp.zeros_like(refs[-1])

    out = pl.pallas_call(
        call_body, name=name,
        out_shape=(pltpu.SemaphoreType.DMA((n_new,)), pltpu.SemaphoreType.DMA((n_new,)),
                   *[pltpu.HBM(b.shape, b.dtype) for b in bufs], pltpu.HBM((SUBLANES, LANES), F32)),
        in_specs=[_HBM] * nb + [_SEM] * len(sems) + [_ANY] * len(after),
        out_specs=(_SEM, _SEM, *[_HBM] * nb, pl.BlockSpec(memory_space=pltpu.VMEM)),
        input_output_aliases={i: 2 + i for i in range(nb)},
        compiler_params=pltpu.CompilerParams(has_side_effects=_DATAFLOW),
    )(*bufs, *sems, *after)
    return out[0], out[1], list(out[2:2 + nb]), out[-1]


def _gather_relay(ex, tag, after_first):
    ns = ex.n_src
    n = len(ex.thru) - ns

    def first(bufs, sems, send, recv):
        lands, first_hop = bufs[ns:], ex.copies(bufs[:ns], bufs[ns:], *sems)
        for a in range(n):
            for h in (0, 1):
                _to_sibling(lands[a], 0, h, send, recv, 10 * a + h).start()
        for a in range(n):
            x_first, y_second, x_second, y_first = first_hop[4 * a:4 * a + 4]
            to_x, to_y = _second_hop(lands[a], send, recv, 10 * a + 8)
            x_first.wait_recv()
            to_y.start()
            _to_sibling(lands[a], 2, 0, send, recv, 10 * a + 4).start()
            y_second.wait_recv()
            to_x.start()
            _to_sibling(lands[a], 1, 1, send, recv, 10 * a + 3).start()
            x_second.wait_recv()
            _to_sibling(lands[a], 2, 1, send, recv, 10 * a + 5).start()
            y_first.wait_recv()
            _to_sibling(lands[a], 1, 0, send, recv, 10 * a + 2).start()
        for cp in first_hop:
            cp.wait_send()

    def second(lands, sems, send, recv):
        for a in range(n):
            to_x, to_y = _second_hop(lands[a], *sems, 10 * a + 8)
            to_y.wait_recv()
            _to_sibling(lands[a], 3, 0, send, recv, 2 * a).start()
            to_x.wait_recv()
            _to_sibling(lands[a], 3, 1, send, recv, 2 * a + 1).start()
            to_x.wait_send()
            to_y.wait_send()

    def last(lands, sems, send, recv):
        for a in range(n):
            for flip in range(4):
                for h in (0, 1):
                    s, r, k = (sems[2], sems[3], 2 * a + h) if flip == 3 else (sems[0], sems[1], 10 * a + 2 * flip + h)
                    cp = _to_sibling(lands[a], flip, h, s, r, k)
                    cp.wait_send()
                    cp.wait_recv()

    send1, recv1, bufs, token = _relay_call(f"{tag}_chips_relay", first, ex.thru, [ex.send_sems, ex.recv_sems], 10 * n,
                                            after_first)

    def run_second(after):
        send2, recv2, lands, token = _relay_call(f"{tag}_diagonal_relay", second, bufs[ns:], [send1, recv1], 2 * n, after)
        return token, lambda after_last: _relay_call(f"{tag}_sibling_wait", last, lands, [send1, recv1, send2, recv2],
                                                     1, after_last)[2]

    return token, run_second


def _col_tile(r, c, block_bytes=2**20):
    return next(t for t in (1024, 512, 256, 128) if c % t == 0 and (r * t * 4 <= block_bytes or t == 128))


def _add_sibling(g4, recv, pos, name):
    _, _, r, c = g4.shape
    tc = _col_tile(r, c, 2**22)

    def body(pos_ref, g_ref, r_ref, o16_ref, mine_ref):
        s = g_ref[0, 0] + r_ref[0]
        o16_ref[0] = s.astype(BF16)

        @pl.when(pl.program_id(1) == pos_ref[1])
        def _():
            mine_ref[...] = s

    slot = pl.BlockSpec((1, r, tc), lambda j, k, pos_ref: (k, 0, j))
    return pl.pallas_call(
        body, name=name,
        out_shape=[pltpu.HBM((4, r, c), BF16), pltpu.HBM((r, c), F32)],
        grid_spec=pltpu.PrefetchScalarGridSpec(
            num_scalar_prefetch=1, grid=(c // tc, 4),
            in_specs=[pl.BlockSpec((1, 1, r, tc), lambda j, k, pos_ref: (k, pos_ref[0], 0, j)), slot],
            out_specs=[slot, pl.BlockSpec((r, tc), lambda j, k, pos_ref: (0, j))]),
        compiler_params=pltpu.CompilerParams(
            dimension_semantics=("parallel", "arbitrary"), vmem_limit_bytes=_vmem_limit(4 * _nbytes((r, tc), F32))),
    )(pos, *_in_hbm(g4, recv))


class _ReduceScatter:
    def __init__(self, tag, grads_t, pos):
        self.tag, self.pos, self.names = tag, pos, list(grads_t)
        g4s = [g.reshape(4, 2, g.size // (N_DEV * g.shape[-1]), g.shape[-1]) for g in grads_t.values()]
        lands = [lax.empty((4,) + g.shape[2:], F32) for g in g4s]
        self.ex = _exchange_start(f"rs_{tag}_sibling_start", _scatter_sibling_copies, g4s, lands, 4 * len(g4s))
        self.token = self.ex.token

    def start_chips(self, after):
        g4s, from_sibling = _exchange_wait(f"rs_{self.tag}_sibling_wait", self.ex, after)
        parts = [_add_sibling(g4, rv, self.pos, f"rs_add_sibling_{k}")
                 for k, g4, rv in zip(self.names, g4s, from_sibling)]
        self.mine = [mine for _, mine in parts]
        p16s = [p16 for p16, _ in parts]
        lands = [lax.empty((3,) + p.shape[1:], BF16) for p in p16s]
        self.ex = _exchange_start(f"rs_{self.tag}_chips_start", _scatter_chips_copies, p16s, lands, 3 * len(p16s))
        self.token = self.ex.token

    def finish(self, after):
        _, from_chips = _exchange_wait(f"rs_{self.tag}_chips_wait", self.ex, after)
        return dict(zip(self.names, zip(self.mine, from_chips)))


def _rope_tables():
    positions = np.arange(SEQ, dtype=np.float32)
    inv_freq = np.power(np.float32(ROPE_THETA), -np.arange(0, ROPE_DIM, 2, dtype=np.float32) / np.float32(ROPE_DIM))
    ang = (positions[:, None] * inv_freq[None, :]).astype(np.float32)
    cos, sin = np.cos(ang).astype(np.float32), np.sin(ang).astype(np.float32)
    ones = np.ones((SEQ, HEAD_DIM - ROPE_DIM), np.float32)
    zeros8 = np.zeros((SEQ, ROPE_HALF), np.float32)
    zeros = np.zeros((SEQ, HEAD_DIM - ROPE_DIM), np.float32)
    c_head = np.concatenate([cos, cos, ones], axis=1)
    s1_head = np.concatenate([-sin, zeros8, zeros], axis=1)
    s2_head = np.concatenate([zeros8, sin, zeros], axis=1)
    return tuple(jnp.asarray(np.concatenate([t, t], axis=1)) for t in (c_head, s1_head, s2_head))


def _rope_apply(x, c, s1, s2):
    w = x.shape[1]
    return x * c + pltpu.roll(x, w - ROPE_HALF, 1) * s1 + pltpu.roll(x, ROPE_HALF, 1) * s2


def _rope_apply_t(dy, c, s1, s2):
    w = dy.shape[1]
    return dy * c + pltpu.roll(dy * s1, ROPE_HALF, 1) + pltpu.roll(dy * s2, w - ROPE_HALF, 1)


def _dil_prev_limit(has_prev):
    return jnp.where(has_prev, 0, BLOCK)


def _dil_valid(limit):
    row = lax.broadcasted_iota(jnp.int32, (BLOCK, 2 * BLOCK), 0)
    col = lax.broadcasted_iota(jnp.int32, (BLOCK, 2 * BLOCK), 1)
    dist = col - row
    return jnp.logical_and(dist >= jnp.where(col < BLOCK, limit, -BLOCK), dist <= BLOCK)


def _upper_half():
    return lax.broadcasted_iota(jnp.int32, (1, LANES), 1) >= HEAD_DIM


def _stack_heads(x):
    upper = _upper_half()
    return jnp.concatenate([jnp.where(upper, 0, x), jnp.where(upper, x, 0)], axis=0)


def _unstack_heads(y):
    n = y.shape[0] // 2
    return jnp.where(_upper_half(), y[n:], y[:n])


def _head_columns(t):
    return jnp.concatenate([t[:, 0:1], t[:, HEAD_DIM:HEAD_DIM + 1]], axis=0)


def _dil_rows(n, d):
    per = N_BLOCKS // d
    r, lb = n // per, n % per

    def rows(b):
        start = b * (BLOCK * d) + r
        return pl.ds(pl.multiple_of(start, BLOCK), BLOCK) if d == 1 else pl.ds(start, BLOCK, stride=d)

    return rows(lb), rows(jnp.maximum(lb - 1, 0)), lb > 0


def _dil_rotate(q_ref, k_ref, c_ref, s1_ref, s2_ref, q_rot, k_rot):
    tabs = (c_ref[...], s1_ref[...], s2_ref[...])
    q_rot[...] = _rope_apply(q_ref[...], *tabs) * QK_SCALE
    k_rot[...] = _rope_apply(k_ref[...], *tabs)


def _dil_specs():
    def col(base):
        return pl.BlockSpec((SEQ, LANES), lambda p: (0, base // LANES + p))

    table = pl.BlockSpec((SEQ, LANES), lambda p: (0, 0))
    return [col(COL_QA), col(COL_KA), col(COL_VA)], [table] * 3


def _store_columns(blocks, dproj_ref, cols, sem):
    copies = [pltpu.make_async_copy(b, dproj_ref.at[:, pl.ds(pl.multiple_of(c * LANES, LANES), LANES)], sem.at[i])
              for i, (b, c) in enumerate(zip(blocks, cols))]
    for cp in copies:
        cp.start()
    for cp in copies:
        cp.wait()


def _dil_window(d, n, k_rot, v_ref):
    rows, prev, has_prev = _dil_rows(n, d)
    kw, vw = k_rot[rows, :].astype(BF16), v_ref[rows, :].astype(BF16)
    if d == N_BLOCKS:
        row = lax.broadcasted_iota(jnp.int32, (BLOCK, BLOCK), 0)
        valid = lax.broadcasted_iota(jnp.int32, (BLOCK, BLOCK), 1) <= row
    else:
        kw = jnp.concatenate([k_rot[prev, :].astype(BF16), kw], axis=0)
        vw = jnp.concatenate([v_ref[prev, :].astype(BF16), vw], axis=0)
        valid = _dil_valid(_dil_prev_limit(has_prev))
    return rows, prev, kw, vw, jnp.concatenate([valid, valid], axis=0)


def _dil_fwd(proj, tables, deps=()):
    def body(q_ref, k_ref, v_ref, c_ref, s1_ref, s2_ref, *rest):
        o_ref, lse_ref, qr_ref, kr_ref, q_rot, k_rot = rest[len(deps):]
        upper = _upper_half()
        _dil_rotate(q_ref, k_ref, c_ref, s1_ref, s2_ref, q_rot, k_rot)
        qr_ref[...] = q_rot[...].astype(BF16)
        kr_ref[...] = k_rot[...].astype(BF16)

        def blocks_of(d):
            def block(n, carry):
                rows, _, kw, vw, valid = _dil_window(d, n, k_rot, v_ref)
                s = jnp.where(valid, _dot_nt(_stack_heads(q_rot[rows, :].astype(BF16)), kw), NEG_INF)
                m = jnp.max(s, axis=-1, keepdims=True)
                p = jnp.exp(s - m)
                den = jnp.sum(p, axis=-1, keepdims=True)
                o_ref[rows, :] = _unstack_heads(_dot_nn((p * (1.0 / den)).astype(BF16), vw))
                lse = m + jnp.log(den)
                lse_ref[rows, :] = jnp.where(upper, lse[BLOCK:], lse[:BLOCK])
                return carry

            lax.fori_loop(0, N_BLOCKS, block, 0, unroll=4)

        for g, d in enumerate(DILATIONS):
            pl.when(pl.program_id(0) // 2 == g)(functools.partial(blocks_of, d))

    qkv, tabs = _dil_specs()
    out = pl.BlockSpec((SEQ, LANES), lambda p: (0, p))
    return pl.pallas_call(
        body, name="dil_attn_fwd", grid=(DIL_WIDTH // LANES,), in_specs=qkv + tabs + [_ANY] * len(deps),
        out_specs=[out] * 4,
        out_shape=[pltpu.HBM((SEQ, DIL_WIDTH), F32)] * 2 + [pltpu.HBM((SEQ, DIL_WIDTH), BF16)] * 2,
        scratch_shapes=[pltpu.VMEM((SEQ, LANES), F32)] * 2,
        compiler_params=pltpu.CompilerParams(dimension_semantics=("parallel",)),
    )(*_in_hbm(proj, proj, proj, *tables), *deps)


def _dil_bwd(proj, q_rotated, k_rotated, tables, do, lse, c, dproj, deps=()):
    def body(q_ref, k_ref, v_ref, c_ref, s1_ref, s2_ref, do_ref, lse_ref, cc_ref, dproj_in, *rest):
        dproj_ref, dq_acc, dk_acc, dv_acc, dq_out, dk_out, dv_out, q_rot, k_rot, sem = rest[len(deps):]
        dk_acc[...] = jnp.zeros_like(dk_acc)
        dv_acc[...] = jnp.zeros_like(dv_acc)
        q_rot[...] = q_ref[...].astype(F32)
        k_rot[...] = k_ref[...].astype(F32)

        def blocks_of(d):
            def block(n, carry):
                rows, prev, kw, vw, valid = _dil_window(d, n, k_rot, v_ref)
                q2 = _stack_heads(q_rot[rows, :].astype(BF16))
                do2 = _stack_heads(do_ref[rows, :].astype(BF16))
                lse_col, c_col = _head_columns(lse_ref[rows, :]), _head_columns(cc_ref[rows, :])
                p = jnp.where(valid, jnp.exp(_dot_nt(q2, kw) - lse_col), 0.0)
                ds = (p * (_dot_nt(do2, vw) + c_col)).astype(BF16)
                dk, dv = _dot_tn(ds, q2), _dot_tn(p.astype(BF16), do2)
                dq_acc[rows, :] = _unstack_heads(_dot_nn(ds, kw)) * QK_SCALE
                if d == N_BLOCKS:
                    dk_acc[rows, :] += dk
                    dv_acc[rows, :] += dv
                else:
                    dk_acc[prev, :] += dk[:BLOCK]
                    dv_acc[prev, :] += dv[:BLOCK]
                    dk_acc[rows, :] += dk[BLOCK:]
                    dv_acc[rows, :] += dv[BLOCK:]
                return carry

            lax.fori_loop(0, N_BLOCKS, block, 0, unroll=4)

        pair = pl.program_id(0)
        for g, d in enumerate(DILATIONS):
            pl.when(pair // 2 == g)(functools.partial(blocks_of, d))
        tabs = (c_ref[...], s1_ref[...], s2_ref[...])
        dq_out[...] = _rope_apply_t(dq_acc[...], *tabs).astype(BF16)
        dk_out[...] = _rope_apply_t(dk_acc[...], *tabs).astype(BF16)
        dv_out[...] = dv_acc[...].astype(BF16)
        _store_columns((dq_out, dk_out, dv_out), dproj_ref,
                       [base // LANES + pair for base in (COL_QA, COL_KA, COL_VA)], sem)

    qkv, tabs = _dil_specs()
    tok = pl.BlockSpec((SEQ, LANES), lambda p: (0, p))
    return pl.pallas_call(
        body, name="dil_attn_bwd", grid=(DIL_WIDTH // LANES,),
        in_specs=[tok, tok, qkv[2]] + tabs + [tok, tok, tok, _ANY] + [_ANY] * len(deps), out_specs=_ANY,
        out_shape=pltpu.HBM(dproj.shape, dproj.dtype),
        scratch_shapes=[pltpu.VMEM((SEQ, LANES), F32)] * 3 + [pltpu.VMEM((SEQ, LANES), BF16)] * 3
        + [pltpu.VMEM((SEQ, LANES), F32)] * 2 + [pltpu.SemaphoreType.DMA((3,))],
        input_output_aliases={9: 0},
        compiler_params=pltpu.CompilerParams(dimension_semantics=("arbitrary",)),
    )(*_in_hbm(q_rotated, k_rotated, proj, *tables, do, lse, c, dproj), *deps)


def _group_weights(l0, l1, l2):
    m = jnp.maximum(jnp.maximum(l0, l1), l2)
    e0, e1, e2 = jnp.exp(l0 - m), jnp.exp(l1 - m), jnp.exp(l2 - m)
    tot = e0 + e1 + e2
    return e0 / tot, e1 / tot, e2 / tot


def _dil_combine(o, lse, deps=()):
    def fn(o0, o1, o2, l0, l1, l2):
        w0, w1, w2 = _group_weights(l0, l1, l2)
        return w0 * o0 + w1 * o1 + w2 * o2

    w = DIL_OUT_WIDTH
    return _rowwise(fn, "dil_combine", SEQ, 512, [(o, w, g) for g in range(3)] + [(lse, w, g) for g in range(3)], [],
                    [(w, F32)], deps=deps)[0]


def _dil_combine_bwd(d_out, o, lse, deps=()):
    w = DIL_OUT_WIDTH

    def fn(d, o0, o1, o2, l0, l1, l2):
        row = lax.broadcasted_iota(jnp.int32, (w, w), 0) // HEAD_DIM
        col = lax.broadcasted_iota(jnp.int32, (w, w), 1) // HEAD_DIM
        same_head = jnp.where(row == col, 1.0, 0.0).astype(BF16)
        ws = _group_weights(l0, l1, l2)
        dws = [_dot3_nn(d * og, same_head) for og in (o0, o1, o2)]
        mean = ws[0] * dws[0] + ws[1] * dws[1] + ws[2] * dws[2]
        return jnp.concatenate([wg * d for wg in ws], axis=1), jnp.concatenate([-wg * mean for wg in ws], axis=1)

    return _rowwise(fn, "dil_combine_bwd", SEQ, 512,
                    [(d_out, w, 0)] + [(o, w, g) for g in range(3)] + [(lse, w, g) for g in range(3)], [],
                    [(DIL_WIDTH, F32)] * 2, deps=deps)


def _log1p(e):
    u = 1.0 + e
    return jnp.where(u == 1.0, e, jnp.log(u) * (e / (u - 1.0)))


def _fox_gate(proj, b_pad, deps=()):
    def body(f_ref, b_ref, *rest):
        o_ref = rest[-1]
        z = f_ref[...] + b_ref[...]
        logf = (jnp.minimum(z, 0.0) - _log1p(jnp.exp(-jnp.abs(z)))).T[:F_ROWS]
        row = lax.broadcasted_iota(jnp.int32, (BLOCK, BLOCK), 0)
        col = lax.broadcasted_iota(jnp.int32, (BLOCK, BLOCK), 1)
        before = jnp.where(row <= col, 1.0, 0.0).astype(BF16)
        carry = jnp.zeros((F_ROWS, 1), F32)
        for blk in range(N_BLOCKS):
            run = _dot3_nn(logf[:, blk * BLOCK:(blk + 1) * BLOCK], before) + carry
            o_ref[:, blk * BLOCK:(blk + 1) * BLOCK] = run
            carry = run[:, BLOCK - 1:BLOCK]

    return pl.pallas_call(
        body, name="fox_gate", grid=(1,),
        in_specs=[pl.BlockSpec((SEQ, LANES), lambda i: (0, COL_F // LANES)), pl.BlockSpec((1, LANES), lambda i: (0, 0))]
        + [_ANY] * len(deps),
        out_specs=pl.BlockSpec((F_ROWS, SEQ), lambda i: (0, 0)),
        out_shape=pltpu.HBM((F_ROWS, SEQ), F32),
    )(*_in_hbm(proj, b_pad), *deps)


def _fox_gate_bwd(d_cum, proj, b_pad, dproj):
    def body(d_ref, f_ref, b_ref, dproj_ref, dz_ref, db_ref):
        row = lax.broadcasted_iota(jnp.int32, (BLOCK, BLOCK), 0)
        col = lax.broadcasted_iota(jnp.int32, (BLOCK, BLOCK), 1)
        after = jnp.where(row >= col, 1.0, 0.0).astype(BF16)
        carry = jnp.zeros((F_ROWS, 1), F32)
        parts = [None] * N_BLOCKS
        for blk in reversed(range(N_BLOCKS)):
            run = _dot3_nn(d_ref[:, blk * BLOCK:(blk + 1) * BLOCK], after) + carry
            parts[blk] = run
            carry = run[:, 0:1]
        dlogf = jnp.concatenate(parts, axis=1)
        dlogf = jnp.concatenate([dlogf, jnp.zeros((LANES - F_ROWS, SEQ), F32)], axis=0).T
        dz = dlogf * _sigmoid(-(f_ref[...] + b_ref[...]))
        dz_ref[...] = dz.astype(BF16)
        db_ref[...] = jnp.sum(dz, axis=0, keepdims=True)

    f_cols = pl.BlockSpec((SEQ, LANES), lambda i: (0, COL_F // LANES))
    return pl.pallas_call(
        body, name="fox_gate_bwd", grid=(1,),
        in_specs=[pl.BlockSpec((F_ROWS, SEQ), lambda i: (0, 0)), f_cols, pl.BlockSpec((1, LANES), lambda i: (0, 0)), _ANY],
        out_specs=[f_cols, pl.BlockSpec((1, LANES), lambda i: (0, 0))],
        out_shape=[pltpu.HBM(dproj.shape, dproj.dtype), pltpu.HBM((1, LANES), F32)],
        input_output_aliases={3: 0},
    )(*_in_hbm(d_cum, proj, b_pad, dproj))


FOX_TILE = 256
FOX_TILES = SEQ // FOX_TILE


def _row_to_col(row):
    n = row.shape[1]
    eye = lax.broadcasted_iota(jnp.int32, (n, n), 0) == lax.broadcasted_iota(jnp.int32, (n, n), 1)
    return jnp.sum(jnp.where(eye, row, 0.0), axis=1, keepdims=True)


def _fox_bias(f_row, i):
    t = FOX_TILE
    ext = (i + 1) * t
    bias = _row_to_col(f_row[:, i * t:(i + 1) * t]) - f_row[:, :ext]
    row = lax.broadcasted_iota(jnp.int32, (t, ext), 0) + i * t
    col = lax.broadcasted_iota(jnp.int32, (t, ext), 1)
    return bias, col <= row


def _fox_specs():
    qkv = [pl.BlockSpec((SEQ, LANES), lambda p, base=base: (0, base // LANES + p)) for base in (COL_QB, COL_KB, COL_VB)]
    return qkv, pl.BlockSpec((F_ROWS, SEQ), lambda p: (0, 0))


def _fox_fwd(proj, f_rows):
    t = FOX_TILE

    def body(q_ref, k_ref, v_ref, f_ref, o_ref, lse_ref):
        pair = pl.program_id(0)
        upper = _upper_half()
        k16, v16 = k_ref[...].astype(BF16), v_ref[...].astype(BF16)
        f_row = [f_ref[pl.ds(2 * pair + e, 1), :] for e in range(2)]
        for i in range(FOX_TILES):
            ext = (i + 1) * t
            q_tile = (q_ref[i * t:(i + 1) * t, :] * QK_SCALE).astype(BF16)
            s2 = _dot_nt(_stack_heads(q_tile), k16[:ext])
            pns, lses = [], []
            for e in range(2):
                bias, causal = _fox_bias(f_row[e], i)
                s = jnp.where(causal, s2[e * t:(e + 1) * t] + bias, NEG_INF)
                m = jnp.max(s, axis=-1, keepdims=True)
                p = jnp.exp(s - m)
                den = jnp.sum(p, axis=-1, keepdims=True)
                pns.append((p * (1.0 / den)).astype(BF16))
                lses.append(m + jnp.log(den))
            o_ref[i * t:(i + 1) * t, :] = _unstack_heads(_dot_nn(jnp.concatenate(pns, axis=0), v16[:ext]))
            lse_ref[i * t:(i + 1) * t, :] = jnp.where(upper, lses[1], lses[0])

    qkv, f_spec = _fox_specs()
    tok = pl.BlockSpec((SEQ, LANES), lambda p: (0, p))
    return pl.pallas_call(
        body, name="fox_attn_fwd", grid=(FOX_WIDTH // LANES,),
        in_specs=qkv + [f_spec], out_specs=[tok, tok],
        out_shape=[pltpu.HBM((SEQ, FOX_WIDTH), F32)] * 2,
        compiler_params=pltpu.CompilerParams(
            dimension_semantics=("parallel",), vmem_limit_bytes=_vmem_limit(8 * t * SEQ * 4)),
    )(*_in_hbm(proj, proj, proj, f_rows))


def _fox_bwd(proj, do, lse, f_rows, dproj):
    t = FOX_TILE

    def body(q_ref, k_ref, v_ref, f_ref, do_ref, lse_ref, dproj_in, dproj_ref, df_ref, dk_acc, dv_acc,
             dq_out, dk_out, dv_out, sem):
        pair = pl.program_id(0)
        upper = _upper_half()
        k16, v16 = k_ref[...].astype(BF16), v_ref[...].astype(BF16)
        f_row = [f_ref[pl.ds(2 * pair + e, 1), :] for e in range(2)]
        dk_acc[...] = jnp.zeros_like(dk_acc)
        dv_acc[...] = jnp.zeros_like(dv_acc)
        df_ref[...] = jnp.zeros_like(df_ref)
        for i in range(FOX_TILES):
            ext = (i + 1) * t
            q_tile = (q_ref[i * t:(i + 1) * t, :] * QK_SCALE).astype(BF16)
            do_tile = do_ref[i * t:(i + 1) * t, :]
            lse_t = lse_ref[i * t:(i + 1) * t, :]
            q2, do2 = _stack_heads(q_tile), _stack_heads(do_tile)
            s2, dp2 = _dot_nt(q2, k16[:ext]), _dot_nt(do2, v16[:ext])
            ps, dss = [], []
            for e in range(2):
                bias, causal = _fox_bias(f_row[e], i)
                s = s2[e * t:(e + 1) * t] + bias
                p = jnp.where(causal, jnp.exp(s - lse_t[:, e * HEAD_DIM:e * HEAD_DIM + 1]), 0.0)
                dp = dp2[e * t:(e + 1) * t]
                ds = p * (dp - jnp.sum(p * dp, axis=-1, keepdims=True))
                df_ref[0, e:e + 1, :ext] -= jnp.sum(ds, axis=0, keepdims=True)
                ps.append(p.astype(BF16))
                dss.append(ds.astype(BF16))
            ds2, p2 = jnp.concatenate(dss, axis=0), jnp.concatenate(ps, axis=0)
            dq_out[i * t:(i + 1) * t, :] = (_unstack_heads(_dot_nn(ds2, k16[:ext])) * QK_SCALE).astype(BF16)
            dk_acc[:ext, :] += _dot_tn(ds2, q2)
            dv_acc[:ext, :] += _dot_tn(p2, do2)
        dk_out[...] = dk_acc[...].astype(BF16)
        dv_out[...] = dv_acc[...].astype(BF16)
        _store_columns((dq_out, dk_out, dv_out), dproj_ref, [base // LANES + pair for base in (COL_QB, COL_KB, COL_VB)],
                       sem)

    qkv, f_spec = _fox_specs()
    tok = pl.BlockSpec((SEQ, LANES), lambda p: (0, p))
    return pl.pallas_call(
        body, name="fox_attn_bwd", grid=(FOX_WIDTH // LANES,),
        in_specs=qkv + [f_spec, tok, tok, _ANY],
        out_specs=[_ANY, pl.BlockSpec((1, SUBLANES, SEQ), lambda p: (p, 0, 0))],
        out_shape=[pltpu.HBM(dproj.shape, dproj.dtype),
                   pltpu.HBM((FOX_WIDTH // LANES, SUBLANES, SEQ), F32)],
        scratch_shapes=[pltpu.VMEM((SEQ, LANES), F32)] * 2 + [pltpu.VMEM((SEQ, LANES), BF16)] * 3
        + [pltpu.SemaphoreType.DMA((3,))],
        input_output_aliases={6: 0},
        compiler_params=pltpu.CompilerParams(
            dimension_semantics=("arbitrary",), vmem_limit_bytes=_vmem_limit(10 * t * SEQ * 4)),
    )(*_in_hbm(proj, proj, proj, f_rows, do, lse, dproj))


MIX_TILE = 512


def _mix_out(out_a, out_b, proj, x, wt_pa, wt_pb, w_out, g_post, g_ffn_pre):
    tm = MIX_TILE

    def body(a_ref, b_ref, ga_ref, gb_ref, x_ref, wpa_ref, wpb_ref, wo_ref, g2_ref, g3_ref,
             merged_ref, mix_ref, x1_ref, h2_ref):
        ya = _dot_nn(a_ref[...].astype(BF16), wpa_ref[...])
        yb = _dot_nn(b_ref[...].astype(BF16), wpb_ref[...])
        merged = (_sigmoid(ga_ref[...]) * ya + _sigmoid(gb_ref[...]) * yb).astype(BF16)
        merged_ref[...] = merged
        mix = _dot_nn(merged, wo_ref[...])
        mix_ref[...] = mix
        x1 = x_ref[...] + mix * _rms_scale(mix) * g2_ref[...]
        x1_ref[...] = x1
        h2_ref[...] = (x1 * _rms_scale(x1) * g3_ref[...]).astype(BF16)

    def rows(w, cb=0):
        return pl.BlockSpec((tm, w), lambda i, cb=cb: (i, cb))

    def whole(a):
        return pl.BlockSpec(a.shape, lambda i: (0, 0))

    d = D_MODEL
    blk = _nbytes((tm, d), F32) * 6 + sum(_nbytes(a.shape, BF16) for a in (wt_pa, wt_pb, w_out))
    return pl.pallas_call(
        body, name="mix_out", grid=(SEQ // tm,),
        in_specs=[rows(DIL_OUT_WIDTH), rows(FOX_WIDTH), rows(d, COL_GA // d), rows(d, COL_GB // d), rows(d),
                  whole(wt_pa), whole(wt_pb), whole(w_out), whole(g_post), whole(g_ffn_pre)],
        out_specs=[rows(d)] * 4,
        out_shape=[pltpu.HBM((SEQ, d), dt) for dt in (BF16, F32, F32, BF16)],
        compiler_params=pltpu.CompilerParams(dimension_semantics=("parallel",), vmem_limit_bytes=_vmem_limit(blk)),
    )(*_in_hbm(out_a, out_b, proj, proj, x, wt_pa, wt_pb, w_out, g_post, g_ffn_pre))


def _mix_out_bwd(dmix, out_a, out_b, proj, wt_pa, wt_pb, w_out, deps=()):
    tm = MIX_TILE

    def body(dm_ref, a_ref, b_ref, ga_ref, gb_ref, wpa_ref, wpb_ref, wo_ref, *rest):
        dproj_ref, dya_ref, dyb_ref, da_ref, db_ref = rest[len(deps):]
        dmerged = _dot_nt(dm_ref[...], wo_ref[...])
        ya = _dot_nn(a_ref[...].astype(BF16), wpa_ref[...])
        yb = _dot_nn(b_ref[...].astype(BF16), wpb_ref[...])
        sa, sb = _sigmoid(ga_ref[...]), _sigmoid(gb_ref[...])
        dproj_ref[:, COL_GA:COL_GA + D_MODEL] = (dmerged * ya * (sa * (1.0 - sa))).astype(BF16)
        dproj_ref[:, COL_GB:COL_GB + D_MODEL] = (dmerged * yb * (sb * (1.0 - sb))).astype(BF16)
        dproj_ref[:, COL_GB + D_MODEL:] = jnp.zeros((tm, COL_QA - COL_GB - D_MODEL), BF16)
        dya = (dmerged * sa).astype(BF16)
        dyb = (dmerged * sb).astype(BF16)
        dya_ref[...] = dya
        dyb_ref[...] = dyb
        da_ref[...] = _dot_nt(dya, wpa_ref[...])
        db_ref[...] = _dot_nt(dyb, wpb_ref[...]).astype(BF16)

    def rows(w, cb=0):
        return pl.BlockSpec((tm, w), lambda i, cb=cb: (i, cb))

    def whole(a):
        return pl.BlockSpec(a.shape, lambda i: (0, 0))

    d = D_MODEL
    blk = _nbytes((tm, d), F32) * 8 + sum(_nbytes(a.shape, BF16) for a in (wt_pa, wt_pb, w_out))
    return pl.pallas_call(
        body, name="mix_out_bwd", grid=(SEQ // tm,),
        in_specs=[rows(d), rows(DIL_OUT_WIDTH), rows(FOX_WIDTH), rows(d, COL_GA // d), rows(d, COL_GB // d),
                  whole(wt_pa), whole(wt_pb), whole(w_out)] + [_ANY] * len(deps),
        out_specs=[rows(COL_QA)] + [rows(d)] * 2 + [rows(DIL_OUT_WIDTH), rows(FOX_WIDTH)],
        out_shape=[pltpu.HBM((SEQ, PROJ_COLS), BF16)] + [pltpu.HBM((SEQ, d), BF16)] * 2
        + [pltpu.HBM((SEQ, DIL_OUT_WIDTH), F32), pltpu.HBM((SEQ, FOX_WIDTH), BF16)],
        compiler_params=pltpu.CompilerParams(dimension_semantics=("parallel",), vmem_limit_bytes=_vmem_limit(blk)),
    )(*_in_hbm(dmix, out_a, out_b, proj, proj, wt_pa, wt_pb, w_out), *deps)


FFN_TM, FFN_TN = 2048, 256


def _ffn_up(h2, wt_gate, wt_up):
    tm, tn = FFN_TM, FFN_TN

    def body(h_ref, wg_ref, wu_ref, gate_ref, up_ref, act_ref):
        for rows in (slice(0, tm // 2), slice(tm // 2, tm)):
            gate = _dot_nt(h_ref[rows, :], wg_ref[...])
            up = _dot_nt(h_ref[rows, :], wu_ref[...])
            gate_ref[rows, :] = gate
            up_ref[rows, :] = up
            act_ref[rows, :] = (gate * _sigmoid(gate) * up).astype(BF16)

    tile = pl.BlockSpec((tm, tn), lambda i, j: (i, j))
    w_spec = pl.BlockSpec((tn, D_MODEL), lambda i, j: (j, 0))
    return pl.pallas_call(
        body, name="ffn_up", grid=(SEQ // tm, D_FF // tn),
        in_specs=[pl.BlockSpec((tm, D_MODEL), lambda i, j: (i, 0)), w_spec, w_spec],
        out_specs=[tile, tile, tile],
        out_shape=[pltpu.HBM((SEQ, D_FF), dt) for dt in (F32, F32, BF16)],
        compiler_params=pltpu.CompilerParams(
            dimension_semantics=("parallel", "parallel"), vmem_limit_bytes=_vmem_limit(8 * 2**20)),
    )(h2, wt_gate, wt_up)


def _ffn_act_bwd(dff, w_down, gate, up):
    tm, tn = FFN_TM, FFN_TN

    def body(d_ref, wd_ref, gate_ref, up_ref, dgate_ref, dup_ref):
        for rows in (slice(0, tm // 2), slice(tm // 2, tm)):
            dact = _dot_nt(d_ref[rows, :], wd_ref[...])
            gate = gate_ref[rows, :]
            sg = _sigmoid(gate)
            dgate_ref[rows, :] = (dact * up_ref[rows, :] * (sg * (1.0 + gate * (1.0 - sg)))).astype(BF16)
            dup_ref[rows, :] = (dact * (gate * sg)).astype(BF16)

    tile = pl.BlockSpec((tm, tn), lambda i, j: (i, j))
    return pl.pallas_call(
        body, name="ffn_act_bwd", grid=(SEQ // tm, D_FF // tn),
        in_specs=[pl.BlockSpec((tm, D_MODEL), lambda i, j: (i, 0)), pl.BlockSpec((tn, D_MODEL), lambda i, j: (j, 0)),
                  tile, tile],
        out_specs=[tile, tile],
        out_shape=[pltpu.HBM((SEQ, D_FF), BF16)] * 2,
        compiler_params=pltpu.CompilerParams(
            dimension_semantics=("parallel", "parallel"), vmem_limit_bytes=_vmem_limit(8 * 2**20)),
    )(dff, w_down, gate, up)


EPILOGUE_TM = 512


def _loss_head(act, w_down, x1, target, g_post):
    def fn(ff, x1, tgt, g):
        r = _rms_scale(ff)
        nrm = ff * r
        err = (x1 + nrm * g) - tgt
        loss = 0.5 * jnp.sum(jnp.mean(err * err, axis=-1, keepdims=True), axis=0, keepdims=True)
        dy = err * (1.0 / D_MODEL)
        u = dy * g
        dff = r * u - ff * (r * r * r) * jnp.mean(u * ff, axis=-1, keepdims=True)
        return dy, dff, jnp.broadcast_to(loss, (1, LANES)), jnp.sum(dy * nrm, axis=0, keepdims=True)

    d = D_MODEL
    return _matmul_rowwise([(act, w_down)], fn, "ffn_down_loss", EPILOGUE_TM, [(x1, d, 0), (target, d, 0)], [g_post],
                           [(d, F32), (d, BF16)], [LANES, d])


def _post_ffn_bwd(dgate, wt_gate, dup, wt_up, x1, dy, mix, g_ffn_pre, g_mix_post, deps=()):
    def fn(dh2, x1, dy, mix, g3, g2):
        dx, dg3 = _rms_bwd(x1, dh2, g3)
        dx1 = dy + dx
        dmix, dg2 = _rms_bwd(mix, dx1, g2)
        return dx1, dmix, dg3, dg2

    d = D_MODEL
    return _matmul_rowwise([(dgate, wt_gate), (dup, wt_up)], fn, "ffn_up_bwd", EPILOGUE_TM,
                           [(x1, d, 0), (dy, d, 0), (mix, d, 0)], [g_ffn_pre, g_mix_post],
                           [(d, F32), (d, BF16)], [d, d], deps=deps)


def _input_bwd(dproj, wt_r, x, dx1, g_pre, deps=()):
    def fn(dh, x, dx1, g):
        dx, dg = _rms_bwd(x, dh, g)
        return dx1 + dx, dg

    d = D_MODEL
    return _matmul_rowwise([(dproj, wt_r)], fn, "in_proj_bwd", EPILOGUE_TM, [(x, d, 0), (dx1, d, 0)], [g_pre],
                           [(d, F32)], [d], deps=deps)


def _adam_math(w, g, m, v):
    m = ADAM_B1 * m + (1.0 - ADAM_B1) * g
    v = ADAM_B2 * v + (1.0 - ADAM_B2) * (g * g)
    m_hat = m / (1.0 - ADAM_B1 ** ADAM_STEP)
    v_hat = v / (1.0 - ADAM_B2 ** ADAM_STEP)
    delta = -ADAM_LR * (m_hat / (jnp.sqrt(v_hat) + ADAM_EPS) + ADAM_WD * w)
    return delta, m, v


def _adam(w, mine, recv, m, v, name):
    r, c = w.shape
    tc = _col_tile(r, c)

    def body(w_ref, p_ref, r_ref, m_ref, v_ref, g_ref, d_ref, nm_ref, nv_ref):
        g = ((p_ref[...] + r_ref[0].astype(F32)) + r_ref[1].astype(F32)) + r_ref[2].astype(F32)
        g_ref[...] = g
        d_ref[...], nm_ref[...], nv_ref[...] = _adam_math(w_ref[...], g, m_ref[...], v_ref[...])

    spec = pl.BlockSpec((r, tc), lambda j: (0, j))
    return pl.pallas_call(
        body, name=name, grid=(c // tc,),
        in_specs=[spec, spec, pl.BlockSpec((3, r, tc), lambda j: (0, 0, j)), spec, spec], out_specs=[spec] * 4,
        out_shape=[pltpu.HBM((r, c), F32)] * 4,
        compiler_params=pltpu.CompilerParams(dimension_semantics=("parallel",)),
    )(*_in_hbm(w, mine, recv, m, v))


def _adam_small(gathered, ws, ms, vs, loss_parts):
    n = len(ws)

    def body(*refs):
        outs = refs[4 * n + 1:]
        loss = refs[4 * n][0]
        for dev in range(1, N_DEV):
            loss = loss + refs[4 * n][dev]
        outs[4 * n][...] = loss
        for i in range(n):
            ga_ref, w_ref, m_ref, v_ref = (refs[j * n + i] for j in range(4))
            g = ga_ref[0]
            for dev in range(1, N_DEV):
                g = g + ga_ref[dev]
            g = g[:, :w_ref.shape[1]]
            outs[4 * i][...] = g
            outs[4 * i + 1][...], outs[4 * i + 2][...], outs[4 * i + 3][...] = _adam_math(
                w_ref[...], g, m_ref[...], v_ref[...])

    out_shape = [pltpu.HBM(w.shape, F32) for w in ws for _ in range(4)]
    out_shape.append(pltpu.HBM((1, LANES), F32))
    out = pl.pallas_call(body, name="adam_small", out_shape=out_shape)(*gathered, *ws, *ms, *vs, loss_parts)
    return [out[4 * i:4 * i + 4] for i in range(n)], out[4 * n]


_PROJ_SEGMENTS = ((3848, 5896), (None, COL_QA - 2 * D_MODEL), (0, 3840), (3840, 3848), (None, PROJ_COLS - COL_F - 8))


def _proj_weight_t(gathered):
    pieces, zeros, at = [], [], 0
    for lo, hi in _PROJ_SEGMENTS:
        if lo is None:
            zeros.append((at, hi))
            at += hi
            continue
        for dev in range(lo // IN_SHARD, (hi - 1) // IN_SHARD + 1):
            a, b = max(lo, dev * IN_SHARD), min(hi, (dev + 1) * IN_SHARD)
            pieces.append((dev, a - dev * IN_SHARD, at + a - lo, b - a))
        at += hi - lo
    assert at == PROJ_COLS
    tc = 2 * LANES

    def body(g_ref, o_ref, shards, rows):
        for dev in range(N_DEV):
            shards[dev] = g_ref[dev].astype(F32)
        for dev, src, dst, n in pieces:
            rows[pl.ds(dst, n), :] = shards[dev, pl.ds(src, n), :]
        for dst, n in zeros:
            rows[pl.ds(dst, n), :] = jnp.zeros((n, tc), F32)
        o_ref[...] = rows[...].astype(o_ref.dtype)

    return pl.pallas_call(
        body, name="w_in_rows", grid=(D_MODEL // tc,),
        in_specs=[pl.BlockSpec((N_DEV, IN_SHARD, tc), lambda j: (0, 0, j))],
        out_specs=pl.BlockSpec((PROJ_COLS, tc), lambda j: (0, j)),
        out_shape=pltpu.HBM((PROJ_COLS, D_MODEL), gathered.dtype),
        scratch_shapes=[pltpu.VMEM((N_DEV, IN_SHARD, tc), F32), pltpu.VMEM((PROJ_COLS, tc), F32)],
        compiler_params=pltpu.CompilerParams(
            dimension_semantics=("parallel",), vmem_limit_bytes=_vmem_limit(2 * _nbytes((PROJ_COLS, tc), F32))),
    )(*_in_hbm(gathered))


def _proj_weight_grad_slots(dwt_r):
    starts, at = [], 0
    for lo, hi in _PROJ_SEGMENTS:
        if lo is not None:
            starts.append((lo, hi, at))
        at += hi if lo is None else hi - lo
    pieces = []
    for dev in range(N_DEV):
        lo, end = dev * IN_SHARD, (dev + 1) * IN_SHARD
        for seg_lo, seg_hi, seg_at in sorted(starts):
            a, b = max(lo, seg_lo), min(end, seg_hi)
            if a < b:
                pieces.append((dev, a - lo, seg_at + a - seg_lo, b - a))

    def body(g_ref, o_ref):
        for dev, dst, src, rows in pieces:
            o_ref[dev, pl.ds(dst, rows), :] = g_ref[pl.ds(src, rows), :]

    tc = 2 * LANES
    return pl.pallas_call(
        body, name="grad_w_in_slots", grid=(D_MODEL // tc,),
        in_specs=[pl.BlockSpec((PROJ_COLS, tc), lambda j: (0, j))],
        out_specs=pl.BlockSpec((N_DEV, IN_SHARD, tc), lambda j: (0, 0, j)),
        out_shape=pltpu.HBM((N_DEV, IN_SHARD, D_MODEL), F32),
        compiler_params=pltpu.CompilerParams(
            dimension_semantics=("parallel",), vmem_limit_bytes=_vmem_limit(2 * _nbytes((PROJ_COLS, tc), F32))),
    )(*_in_hbm(dwt_r))


def kernel(x, w_in, w_proj_a, w_proj_b, w_out, b_forget, w_ffn_gate, w_ffn_up, w_ffn_down, norm_mix_pre, norm_mix_post, norm_ffn_pre, norm_ffn_post, loss_target, m_w_in, m_w_proj_a, m_w_proj_b, m_w_out, m_b_forget, m_w_ffn_gate, m_w_ffn_up, m_w_ffn_down, m_norm_mix_pre, m_norm_mix_post, m_norm_ffn_pre, m_norm_ffn_post, v_w_in, v_w_proj_a, v_w_proj_b, v_w_out, v_b_forget, v_w_ffn_gate, v_w_ffn_up, v_w_ffn_down, v_norm_mix_pre, v_norm_mix_post, v_norm_ffn_pre, v_norm_ffn_post):
    d = D_MODEL
    names = ("w_in", "w_proj_a", "w_proj_b", "w_out", "w_ffn_gate", "w_ffn_up", "w_ffn_down")
    col_sharded = ("w_in", "w_ffn_gate", "w_ffn_up")

    def row_shards(arrs):
        return {k: (a[0].T if k in col_sharded else a[0]) for k, a in zip(names, arrs)}

    shards = row_shards((w_in, w_proj_a, w_proj_b, w_out, w_ffn_gate, w_ffn_up, w_ffn_down))
    moments_m = row_shards((m_w_in, m_w_proj_a, m_w_proj_b, m_w_out, m_w_ffn_gate, m_w_ffn_up, m_w_ffn_down))
    moments_v = row_shards((v_w_in, v_w_proj_a, v_w_proj_b, v_w_out, v_w_ffn_gate, v_w_ffn_up, v_w_ffn_down))
    pos = jnp.stack([lax.axis_index("c"), 2 * lax.axis_index("x") + lax.axis_index("y")]).astype(jnp.int32)
    x2, target = x[0], loss_target[0]

    me = 4 * lax.axis_index("x") + 2 * lax.axis_index("y") + lax.axis_index("c")
    mid_names, ffn_names = names[1:4], names[4:]
    first_names, later_names = names[:1], names[1:]
    shards16 = {k: shards[k].astype(BF16) for k in names}

    def landing(k):
        return lax.dynamic_update_slice(lax.empty((N_DEV,) + shards[k].shape, BF16), shards16[k][None], (me, 0, 0))

    ag_first = _exchange_start("ag_first_chips_start", _gather_two_route_copies, [shards16[k] for k in first_names],
                               [landing(k) for k in first_names], 4 * len(first_names))
    h = _rowwise(lambda xb, g: xb * _rms_scale(xb) * g, "norm_mix_pre", SEQ, 512, [(x2, d, 0)], [norm_mix_pre],
                 [(d, BF16)], deps=[ag_first.token])[0]
    later_lands = [landing(k) for k in later_names]
    _, ag_first_diagonal = _gather_relay(ag_first, "ag_first", [h, shards["w_in"], moments_m["w_in"], moments_v["w_in"],
                                                               *later_lands, *[shards16[k] for k in later_names]])
    relayed, ag_first_last = ag_first_diagonal([])
    ag_later = _exchange_start("ag_later_chips_start", _gather_two_route_copies, [shards16[k] for k in later_names],
                               later_lands, 4 * len(later_names), after=[relayed])
    gathered = dict(zip(first_names, ag_first_last([ag_later.token])))
    wt_r = _proj_weight_t(gathered["w_in"])

    proj = _matmul([(h, *_in_hbm(wt_r))], "nt", F32, "in_proj", 1024, 896, 1024)
    tables = _rope_tables()
    relayed, ag_later_diagonal = _gather_relay(ag_later, "ag_later", [proj])
    o_dil, lse_dil, q_dil, k_dil = _dil_fwd(proj, tables, deps=[relayed])
    out_a = _dil_combine(o_dil, lse_dil)
    relayed, ag_later_last = ag_later_diagonal([out_a])

    b_pad = jnp.pad(b_forget, ((0, 0), (0, LANES - N_FOX_HEADS)))
    f_rows = _fox_gate(proj, b_pad, deps=[relayed])
    out_b, lse_fox = _fox_fwd(proj, f_rows)

    gathered = dict(zip(later_names, ag_later_last([out_b])))
    wt_pa = gathered["w_proj_a"].transpose(1, 0, 2).reshape(DIL_OUT_WIDTH, d)
    wt_pb = gathered["w_proj_b"].transpose(1, 0, 2).reshape(FOX_WIDTH, d)
    w_o = gathered["w_out"].reshape(d, d)
    wt_g = gathered["w_ffn_gate"].reshape(D_FF, d)
    wt_u = gathered["w_ffn_up"].reshape(D_FF, d)
    w_d = gathered["w_ffn_down"].reshape(D_FF, d)
    merged, mix, x1, h2 = _mix_out(out_a, out_b, proj, x2, wt_pa, wt_pb, w_o, norm_mix_post, norm_ffn_pre)

    gate, up, act = _ffn_up(h2, wt_g, wt_u)
    dy, dff, loss_part, dg_ffn_post = _loss_head(act, w_d, x1, target, norm_ffn_post)

    dgate, dup = _ffn_act_bwd(dff, w_d, gate, up)
    grads_t = {}
    grads_t["w_ffn_down"] = _matmul([(act, dff)], "tn", F32, "grad_w_ffn_down", 1408, 512, 2048, staged=False)
    grads_t["w_ffn_gate"] = _matmul([(dgate, h2)], "tn", F32, "grad_w_ffn_gate", 1408, 512, 2048)
    grads_t["w_ffn_up"] = _matmul([(dup, h2)], "tn", F32, "grad_w_ffn_up", 1408, 512, 2048)
    rs_ffn = _ReduceScatter("ffn", {k: grads_t[k] for k in ffn_names}, pos)
    dx1, dmix, dg_ffn_pre, dg_mix_post = _post_ffn_bwd(dgate, wt_g, dup, wt_u, x1, dy, mix, norm_ffn_pre, norm_mix_post,
                                                       deps=[rs_ffn.token])
    rs_ffn.start_chips([dmix])

    dproj, dya, dyb, d_out_a, d_out_b = _mix_out_bwd(dmix, out_a, out_b, proj, wt_pa, wt_pb, w_o, deps=[rs_ffn.token])
    grads_t["w_out"] = _matmul([(merged, dmix)], "tn", F32, "grad_w_out", 1024, 1024, 1024)
    def column_slots(g):
        return g.reshape(g.shape[0], N_DEV, LANES).transpose(1, 0, 2)

    grads_t["w_proj_a"] = column_slots(_matmul([(out_a, dya)], "tn", F32, "grad_w_proj_a", DIL_OUT_WIDTH, 1024, SEQ))
    grads_t["w_proj_b"] = column_slots(_matmul([(out_b, dyb)], "tn", F32, "grad_w_proj_b", FOX_WIDTH, 1024, SEQ))
    rs_mid = _ReduceScatter("mid", {k: grads_t[k] for k in mid_names}, pos)

    do_dil, c_dil = _dil_combine_bwd(d_out_a, o_dil, lse_dil, deps=[rs_mid.token])
    rs_mid.start_chips([c_dil])
    dproj, d_cum = _fox_bwd(proj, d_out_b, lse_fox, f_rows, dproj)
    d_cum_rows = jnp.pad(d_cum[:, :2].reshape(N_FOX_HEADS, SEQ), ((0, F_ROWS - N_FOX_HEADS), (0, 0)))
    dproj, db_part = _fox_gate_bwd(d_cum_rows, proj, b_pad, dproj)
    dproj = _dil_bwd(proj, q_dil, k_dil, tables, do_dil, lse_dil, c_dil, dproj, deps=[rs_mid.token])

    dwt_r = _matmul([(dproj, h)], "tn", F32, "grad_w_in", 896, 1024, 2048)
    rs_in = _ReduceScatter("in", {"w_in": _proj_weight_grad_slots(dwt_r)}, pos)
    def finish(rs, after):
        return {k: _adam(shards[k], mine, recv, moments_m[k], moments_v[k], "adam_" + k)
                for k, (mine, recv) in rs.finish(after).items()}

    done = finish(rs_ffn, [rs_in.token])
    rs_in.start_chips([done[k][0] for k in ffn_names])
    grad_x, dg_mix_pre = _input_bwd(dproj, wt_r, x2, dx1, norm_mix_pre, deps=[rs_in.token])
    done.update(finish(rs_mid, [grad_x]))

    small_all = _all_gather([dg_mix_pre, dg_mix_post, dg_ffn_pre, dg_ffn_post, db_part, loss_part],
                            "small_grads_all_gather", deps=[done[k][0] for k in mid_names])
    small, loss = _adam_small(small_all[:5], [norm_mix_pre, norm_mix_post, norm_ffn_pre, norm_ffn_post, b_forget],
                              [m_norm_mix_pre, m_norm_mix_post, m_norm_ffn_pre, m_norm_ffn_post, m_b_forget],
                              [v_norm_mix_pre, v_norm_mix_post, v_norm_ffn_pre, v_norm_ffn_post, v_b_forget],
                              small_all[5])

    done.update(finish(rs_in, [small[0][0]]))

    def leaves(i):
        def nat(k):
            a = done[k][i]
            return (a.T if k in col_sharded else a)[None]

        return [nat("w_in"), nat("w_proj_a"), nat("w_proj_b"), nat("w_out"), small[4][i],
                nat("w_ffn_gate"), nat("w_ffn_up"), nat("w_ffn_down"), *[small[r][i] for r in range(4)]]

    return (loss[0, 0], grad_x[None], *leaves(0), *leaves(1), *leaves(2), *leaves(3))
```

```python
import functools
import math

import jax
import jax.numpy as jnp
import numpy as np
from jax import lax
from jax.experimental import pallas as pl
from jax.experimental.pallas import tpu as pltpu

F32 = jnp.float32
BF16 = jnp.bfloat16
MESH = pl.DeviceIdType.MESH

D_MODEL = 1024
SEQ = 2048
HEAD_DIM = 64
BLOCK = 128
N_BLOCKS = SEQ // BLOCK
DILATIONS = (1, 4, 16)
N_FOX_HEADS = 8
DIL_WIDTH = 768
DIL_OUT_WIDTH = 256
FOX_WIDTH = 512
D_FF = 2816
ROPE_THETA = 500000.0
ROPE_DIM = HEAD_DIM // 4
ROPE_HALF = ROPE_DIM // 2
EPS = 1e-6
NEG_INF = -1e30
QK_SCALE = 1.0 / math.sqrt(HEAD_DIM)
IN_COLS = 5896
N_DEV = 8
IN_SHARD = IN_COLS // N_DEV

ADAM_LR = 0.001
ADAM_B1 = 0.9
ADAM_B2 = 0.999
ADAM_EPS = 1e-08
ADAM_WD = 0.01
ADAM_STEP = 10

V7X_VMEM_BYTES = 64 * 2**20
LANES = 128
SUBLANES = 8

PROJ_COLS = 6272
COL_GA, COL_GB = 0, 1024
COL_QA, COL_KA, COL_VA = 2304, 3072, 3840
COL_QB, COL_KB, COL_VB = 4608, 5120, 5632
COL_F = 6144
F_ROWS = 16


def _vmem_limit(block_bytes):
    want = 2 * block_bytes + 16 * 2**20
    return int(min(max(want, 32 * 2**20), V7X_VMEM_BYTES - 8 * 2**20))


def _nbytes(shape, dtype):
    return math.prod(shape) * jnp.dtype(dtype).itemsize


def _in_hbm(*arrays):
    return [pltpu.with_memory_space_constraint(a, pltpu.HBM) for a in arrays]


def _dot(a, b, dims):
    return lax.dot_general(a, b, (dims, ((), ())), preferred_element_type=F32)


def _dot_nn(a, b):
    return _dot(a, b, ((1,), (0,)))


def _dot_nt(a, b):
    return _dot(a, b, ((1,), (1,)))


def _dot_tn(a, b):
    return _dot(a, b, ((0,), (0,)))


def _sigmoid(z):
    return 1.0 / (1.0 + jnp.exp(-z))


def _split3(x):
    hi = x.astype(BF16)
    r1 = x - hi.astype(F32)
    mid = r1.astype(BF16)
    lo = (r1 - mid.astype(F32)).astype(BF16)
    return hi, mid, lo


def _dot3_nn(x, ones_matrix):
    hi, mid, lo = _split3(x)
    return (_dot_nn(hi, ones_matrix) + _dot_nn(mid, ones_matrix)) + _dot_nn(lo, ones_matrix)


def _rowwise(fn, name, n_rows, tm, row_ins, bcast_ins, row_outs, acc_outs=(), deps=()):
    n_in = len(row_ins) + len(bcast_ins)
    n_ro = len(row_outs)

    def body(*refs):
        res = fn(*[r[...] for r in refs[:n_in]])
        if not isinstance(res, (tuple, list)):
            res = (res,)
        outs = refs[n_in + len(deps):]
        for r, o in zip(res[:n_ro], outs[:n_ro]):
            o[...] = r.astype(o.dtype)
        first = pl.program_id(0) == 0
        for r, o in zip(res[n_ro:], outs[n_ro:]):
            _accumulate(o, r, first)

    in_specs = [pl.BlockSpec((tm, w), lambda i, cb=cb: (i, cb)) for _, w, cb in row_ins]
    in_specs += [pl.BlockSpec(a.shape, lambda i: (0, 0)) for a in bcast_ins]
    in_specs += [pl.BlockSpec(memory_space=pl.ANY)] * len(deps)
    out_specs = [pl.BlockSpec((tm, w), lambda i: (i, 0)) for w, _ in row_outs]
    out_specs += [pl.BlockSpec((1, w), lambda i: (0, 0)) for w in acc_outs]
    out_shape = [pltpu.HBM((n_rows, w), dt) for w, dt in row_outs]
    out_shape += [pltpu.HBM((1, w), F32) for w in acc_outs]
    blk = sum(_nbytes((tm, w), a.dtype) for a, w, _ in row_ins) + sum(_nbytes((tm, w), dt) for w, dt in row_outs)
    return pl.pallas_call(
        body, name=name, grid=(n_rows // tm,), in_specs=in_specs, out_specs=out_specs, out_shape=out_shape,
        compiler_params=pltpu.CompilerParams(
            dimension_semantics=("arbitrary" if acc_outs else "parallel",), vmem_limit_bytes=_vmem_limit(3 * blk)),
    )(*_in_hbm(*[a for a, _, _ in row_ins], *bcast_ins), *deps)


def _accumulate(o_ref, part, first):
    @pl.when(first)
    def _():
        o_ref[...] = part

    @pl.when(jnp.logical_not(first))
    def _():
        o_ref[...] += part


_MM_DIMS = {"nn": ((1,), (0,)), "nt": ((1,), (1,)), "tn": ((0,), (0,))}


def _matmul(pairs, mode, out_dtype, name, tm, tn, tk, deps=(), staged=True):
    a0, b0 = pairs[0]
    if mode == "tn":
        kk, m = a0.shape
    else:
        m, kk = a0.shape
    n = b0.shape[0] if mode == "nt" else b0.shape[1]
    assert m % tm == 0 and n % tn == 0 and kk % tk == 0, (name, m, n, kk)
    nk = kk // tk
    n_pairs = len(pairs)
    dims = _MM_DIMS[mode]
    n_in = 2 * n_pairs + len(deps)

    def body(*refs):
        o_ref = refs[n_in]
        part = None
        for p in range(n_pairs):
            d = _dot(refs[2 * p][...].astype(BF16), refs[2 * p + 1][...].astype(BF16), dims)
            part = d if part is None else part + d
        if nk == 1:
            o_ref[...] = part.astype(o_ref.dtype)
            return
        acc = refs[n_in + 1]
        k = pl.program_id(2)

        @pl.when(k == 0)
        def _():
            acc[...] = part

        @pl.when(k > 0)
        def _():
            acc[...] += part

        @pl.when(k == nk - 1)
        def _():
            o_ref[...] = acc[...].astype(o_ref.dtype)

    if mode == "tn":
        a_spec = pl.BlockSpec((tk, tm), lambda i, j, k: (k, i))
    else:
        a_spec = pl.BlockSpec((tm, tk), lambda i, j, k: (i, k))
    if mode == "nt":
        b_spec = pl.BlockSpec((tn, tk), lambda i, j, k: (j, k))
    else:
        b_spec = pl.BlockSpec((tk, tn), lambda i, j, k: (k, j))
    blk = sum(_nbytes((tm, tk), a.dtype) + _nbytes((tk, tn), b.dtype) for a, b in pairs) + 2 * _nbytes((tm, tn), F32)
    flat = [a for pair in pairs for a in pair]
    return pl.pallas_call(
        body, name=name, grid=(m // tm, n // tn, nk),
        in_specs=[a_spec, b_spec] * n_pairs + [pl.BlockSpec(memory_space=pl.ANY)] * len(deps),
        out_specs=pl.BlockSpec((tm, tn), lambda i, j, k: (i, j)),
        out_shape=pltpu.HBM((m, n), out_dtype),
        scratch_shapes=[] if nk == 1 else [pltpu.VMEM((tm, tn), F32)],
        compiler_params=pltpu.CompilerParams(
            dimension_semantics=("parallel", "parallel", "arbitrary"), vmem_limit_bytes=_vmem_limit(blk)),
    )(*(flat if staged else _in_hbm(*flat)), *deps)


def _matmul_rowwise(pairs, fn, name, tm, row_ins, bcast_ins, row_outs, acc_outs=(), deps=()):
    m = pairs[0][0].shape[0]
    n_mm, n_in = 2 * len(pairs), len(row_ins) + len(bcast_ins)
    n_ro = len(row_outs)
    n_chunks = [next(n for n in (8, 7, 4, 2, 1) if b.shape[0] % (n * LANES) == 0) for _, b in pairs]

    def body(*refs):
        outs = refs[n_mm + n_in + len(deps):]
        resident, sem = outs[n_ro + len(acc_outs):-1], outs[-1]
        first = pl.program_id(0) == 0
        chunks = []
        for p, (_, b) in enumerate(pairs):
            ck = b.shape[0] // n_chunks[p]
            for c in range(n_chunks[p]):
                rows = pl.ds(c * ck, ck)
                chunks.append((p, (c * ck, ck), pltpu.make_async_copy(refs[2 * p + 1].at[rows], resident[p].at[rows],
                                                                      sem.at[len(chunks)])))

        @pl.when(first)
        def _():
            for _, _, cp in chunks:
                cp.start()

        prod = None
        for p, (k0, ck), cp in chunks:
            pl.when(first)(cp.wait)
            part = _dot_nn(refs[2 * p][:, k0:k0 + ck].astype(BF16), resident[p][k0:k0 + ck, :].astype(BF16))
            prod = part if prod is None else prod + part
        res = fn(prod, *[r[...] for r in refs[n_mm:n_mm + n_in]])
        for r, o in zip(res[:n_ro], outs[:n_ro]):
            o[...] = r.astype(o.dtype)
        for r, o in zip(res[n_ro:], outs[n_ro:]):
            _accumulate(o, r, first)

    in_specs = []
    for a, b in pairs:
        in_specs += [pl.BlockSpec((tm, a.shape[1]), lambda i: (i, 0)), _ANY]
    in_specs += [pl.BlockSpec((tm, w), lambda i, cb=cb: (i, cb)) for _, w, cb in row_ins]
    in_specs += [pl.BlockSpec(a.shape, lambda i: (0, 0)) for a in bcast_ins]
    in_specs += [_ANY] * len(deps)
    out_specs = [pl.BlockSpec((tm, w), lambda i: (i, 0)) for w, _ in row_outs]
    out_specs += [pl.BlockSpec((1, w), lambda i: (0, 0)) for w in acc_outs]
    out_shape = [pltpu.HBM((m, w), dt) for w, dt in row_outs]
    out_shape += [pltpu.HBM((1, w), F32) for w in acc_outs]
    blk = sum(_nbytes((tm, a.shape[1]), a.dtype) + _nbytes(b.shape, b.dtype) // 2 for a, b in pairs)
    blk += sum(_nbytes((tm, w), a.dtype) for a, w, _ in row_ins) + sum(_nbytes((tm, w), dt) for w, dt in row_outs)
    return pl.pallas_call(
        body, name=name, grid=(m // tm,), in_specs=in_specs, out_specs=out_specs, out_shape=out_shape,
        scratch_shapes=[pltpu.VMEM(b.shape, b.dtype) for _, b in pairs] + [pltpu.SemaphoreType.DMA((sum(n_chunks),))],
        compiler_params=pltpu.CompilerParams(dimension_semantics=("arbitrary",), vmem_limit_bytes=_vmem_limit(blk)),
    )(*[a for pair in pairs for a in pair], *[a for a, _, _ in row_ins], *bcast_ins, *deps)


def _rms_scale(x):
    return lax.rsqrt(jnp.mean(x * x, axis=-1, keepdims=True) + EPS)


def _rms_bwd(xin, dyn, g):
    r = _rms_scale(xin)
    u = dyn * g
    dx = r * u - xin * (r * r * r) * jnp.mean(u * xin, axis=-1, keepdims=True)
    dg = jnp.sum(dyn * xin * r, axis=0, keepdims=True)
    return dx, dg


def _mesh_pos():
    return lax.axis_index("x"), lax.axis_index("y"), lax.axis_index("c")


def _all_gather(xs, name, deps=()):
    n = len(xs)

    def body(*refs):
        x_refs, out_refs = refs[:n], refs[n + len(deps):2 * n + len(deps)]
        send_sems, recv_sems, local_sems = refs[2 * n + len(deps):]
        mx, my, mc = _mesh_pos()
        me, sib = (mx, my, mc), (mx, my, 1 - mc)
        chips = [(1 - mx, my), (mx, 1 - my), (1 - mx, 1 - my)]

        def slot(a, dev):
            px, py, pc = dev
            return out_refs[a].at[4 * px + 2 * py + pc]

        def copy(k, a, block, to, src=None):
            return pltpu.make_async_remote_copy(
                src_ref=slot(a, block) if src is None else src, dst_ref=slot(a, block),
                send_sem=send_sems.at[a * 7 + k], recv_sem=recv_sems.at[a * 7 + k],
                device_id=to, device_id_type=MESH)

        mine = [pltpu.make_async_copy(x_refs[a], slot(a, me), local_sems.at[a]) for a in range(n)]
        for cp in mine:
            cp.start()
        first = []
        for a in range(n):
            first.append(copy(0, a, me, sib, x_refs[a]))
            first += [copy(1 + j, a, me, (*chip, mc), x_refs[a]) for j, chip in enumerate(chips)]
        for cp in first:
            cp.start()
        passed = []
        for a in range(n):
            for j, chip in enumerate(chips):
                copy(1 + j, a, (*chip, mc), me).wait_recv()
                fwd = copy(4 + j, a, (*chip, mc), sib)
                fwd.start()
                passed.append(fwd)
        for a in range(n):
            copy(0, a, sib, me).wait_recv()
            for j, chip in enumerate(chips):
                copy(4 + j, a, (*chip, 1 - mc), me).wait_recv()
        for cp in first + passed:
            cp.wait_send()
        for cp in mine:
            cp.wait()

    hbm = pl.BlockSpec(memory_space=pl.ANY)
    return pl.pallas_call(
        body, name=name,
        out_shape=[pltpu.HBM((N_DEV,) + x.shape, x.dtype) for x in xs],
        in_specs=[hbm] * (n + len(deps)), out_specs=[hbm] * n,
        scratch_shapes=[pltpu.SemaphoreType.DMA((7 * n,)), pltpu.SemaphoreType.DMA((7 * n,)),
                        pltpu.SemaphoreType.DMA((n,))],
    )(*xs, *deps)


_HBM = pl.BlockSpec(memory_space=pltpu.HBM)
_SEM = pl.BlockSpec(memory_space=pltpu.SEMAPHORE)
_ANY = pl.BlockSpec(memory_space=pl.ANY)
_DATAFLOW = pltpu.SideEffectType.DATAFLOW_SIDE_EFFECTING


def _flip_peer(flip):
    mx, my, mc = _mesh_pos()
    return (1 - mx if flip & 2 else mx, 1 - my if flip & 1 else my, mc)


def _remote(src, dst, send_sems, recv_sems, k, peer):
    return pltpu.make_async_remote_copy(src_ref=src, dst_ref=dst, send_sem=send_sems.at[k], recv_sem=recv_sems.at[k],
                                        device_id=peer, device_id_type=MESH)


def _scatter_sibling_copies(srcs, lands, send_sems, recv_sems):
    mx, my, mc = _mesh_pos()
    return [_remote(srcs[a].at[k, 1 - mc], lands[a].at[k], send_sems, recv_sems, 4 * a + k, (mx, my, 1 - mc))
            for a in range(len(srcs)) for k in range(4)]


def _scatter_chips_copies(srcs, lands, send_sems, recv_sems):
    mx, my, _ = _mesh_pos()
    k0 = 2 * mx + my
    return [_remote(srcs[a].at[jnp.bitwise_xor(k0, flip)], lands[a].at[flip - 1], send_sems, recv_sems,
                    3 * a + flip - 1, _flip_peer(flip))
            for a in range(len(srcs)) for flip in (1, 2, 3)]


class _Exchange:
    def __init__(self, copies, n_src, send_sems, recv_sems, thru, token):
        self.copies, self.n_src, self.send_sems, self.recv_sems, self.thru, self.token = (
            copies, n_src, send_sems, recv_sems, thru, token)


def _exchange_start(name, copies, srcs, lands, n_copies, after=()):
    bufs = list(srcs) + list(lands)
    nb, ns = len(bufs), len(srcs)

    def body(*refs):
        send_sems, recv_sems = refs[nb + len(after)], refs[nb + len(after) + 1]
        for cp in copies(refs[:ns], refs[ns:nb], send_sems, recv_sems):
            cp.start()
        refs[-1][...] = jnp.zeros_like(refs[-1])

    out = pl.pallas_call(
        body, name=name,
        out_shape=(pltpu.SemaphoreType.DMA((n_copies,)), pltpu.SemaphoreType.DMA((n_copies,)),
                   *[pltpu.HBM(b.shape, b.dtype) for b in bufs], pltpu.HBM((SUBLANES, LANES), F32)),
        in_specs=[_HBM] * nb + [_ANY] * len(after),
        out_specs=(_SEM, _SEM, *[_HBM] * nb, pl.BlockSpec(memory_space=pltpu.VMEM)),
        input_output_aliases={i: 2 + i for i in range(nb)},
        compiler_params=pltpu.CompilerParams(has_side_effects=_DATAFLOW),
    )(*[pltpu.with_memory_space_constraint(b, pltpu.HBM) for b in bufs], *after)
    return _Exchange(copies, ns, out[0], out[1], list(out[2:2 + nb]), out[-1])


def _exchange_wait(name, ex, after):
    nb, ns = len(ex.thru), ex.n_src

    def body(*refs):
        for cp in ex.copies(refs[:ns], refs[ns:nb], refs[nb], refs[nb + 1]):
            cp.wait_send()
            cp.wait_recv()

    out = pl.pallas_call(
        body, name=name, out_shape=tuple(pltpu.HBM(b.shape, b.dtype) for b in ex.thru),
        in_specs=[_HBM] * nb + [_SEM, _SEM] + [_ANY] * len(after), out_specs=tuple([_HBM] * nb),
        input_output_aliases={i: i for i in range(nb)},
        compiler_params=pltpu.CompilerParams(has_side_effects=_DATAFLOW),
    )(*ex.thru, ex.send_sems, ex.recv_sems, *after)
    return list(out[:ns]), list(out[ns:])


def _halves(ref):
    half = ref.shape[1] // 2
    if half % LANES == 0:
        return ref.at[:, pl.ds(0, half)], ref.at[:, pl.ds(half, half)]
    half = ref.shape[0] // 2
    assert half % (2 * SUBLANES) == 0, ref.shape
    return ref.at[pl.ds(0, half)], ref.at[pl.ds(half, half)]


def _gather_two_route_copies(srcs, lands, send_sems, recv_sems):
    mx, my, mc = _mesh_pos()
    me = 4 * mx + 2 * my + mc
    return [_remote(_halves(srcs[a])[h], _halves(lands[a].at[me])[h], send_sems, recv_sems, 4 * a + i, _flip_peer(flip))
            for a in range(len(srcs)) for i, (flip, h) in enumerate(((2, 0), (1, 1), (2, 1), (1, 0)))]


def _to_sibling(land, flip, h, send_sems, recv_sems, k):
    mx, my, mc = _mesh_pos()
    part = _halves(land.at[2 * jnp.bitwise_xor(2 * mx + my, flip) + mc])[h]
    return _remote(part, part, send_sems, recv_sems, k, (mx, my, 1 - mc))


def _second_hop(land, send_sems, recv_sems, k):
    mx, my, mc = _mesh_pos()
    k0 = 2 * mx + my
    from_y = _halves(land.at[2 * jnp.bitwise_xor(k0, 1) + mc])[1]
    from_x = _halves(land.at[2 * jnp.bitwise_xor(k0, 2) + mc])[0]
    return (_remote(from_y, from_y, send_sems, recv_sems, k, _flip_peer(2)),
            _remote(from_x, from_x, send_sems, recv_sems, k + 1, _flip_peer(1)))


def _relay_call(name, body, bufs, sems, n_new, after):
    nb, n_in = len(bufs), len(bufs) + len(sems) + len(after)

    def call_body(*refs):
        body(refs[:nb], refs[nb:nb + len(sems)], refs[n_in], refs[n_in + 1])
        refs[-1][...] = jnp.zeros_like(refs[-1])

    out = pl.pallas_call(
        call_body, name=name,
        out_shape=(pltpu.SemaphoreType.DMA((n_new,)), pltpu.SemaphoreType.DMA((n_new,)),
                   *[pltpu.HBM(b.shape, b.dtype) for b in bufs], pltpu.HBM((SUBLANES, LANES), F32)),
        in_specs=[_HBM] * nb + [_SEM] * len(sems) + [_ANY] * len(after),
        out_specs=(_SEM, _SEM, *[_HBM] * nb, pl.BlockSpec(memory_space=pltpu.VMEM)),
        input_output_aliases={i: 2 + i for i in range(nb)},
        compiler_params=pltpu.CompilerParams(has_side_effects=_DATAFLOW),
    )(*bufs, *sems, *after)
    return out[0], out[1], list(out[2:2 + nb]), out[-1]


def _gather_relay(ex, tag, after_first):
    ns = ex.n_src
    n = len(ex.thru) - ns

    def first(bufs, sems, send, recv):
        lands, first_hop = bufs[ns:], ex.copies(bufs[:ns], bufs[ns:], *sems)
        for a in range(n):
            for h in (0, 1):
                _to_sibling(lands[a], 0, h, send, recv, 10 * a + h).start()
        for a in range(n):
            x_first, y_second, x_second, y_first = first_hop[4 * a:4 * a + 4]
            to_x, to_y = _second_hop(lands[a], send, recv, 10 * a + 8)
            x_first.wait_recv()
            to_y.start()
            _to_sibling(lands[a], 2, 0, send, recv, 10 * a + 4).start()
            y_second.wait_recv()
            to_x.start()
            _to_sibling(lands[a], 1, 1, send, recv, 10 * a + 3).start()
            x_second.wait_recv()
            _to_sibling(lands[a], 2, 1, send, recv, 10 * a + 5).start()
            y_first.wait_recv()
            _to_sibling(lands[a], 1, 0, send, recv, 10 * a + 2).start()
        for cp in first_hop:
            cp.wait_send()

    def second(lands, sems, send, recv):
        for a in range(n):
            to_x, to_y = _second_hop(lands[a], *sems, 10 * a + 8)
            to_y.wait_recv()
            _to_sibling(lands[a], 3, 0, send, recv, 2 * a).start()
            to_x.wait_recv()
            _to_sibling(lands[a], 3, 1, send, recv, 2 * a + 1).start()
            to_x.wait_send()
            to_y.wait_send()

    def last(lands, sems, send, recv):
        for a in range(n):
            for flip in range(4):
                for h in (0, 1):
                    s, r, k = (sems[2], sems[3], 2 * a + h) if flip == 3 else (sems[0], sems[1], 10 * a + 2 * flip + h)
                    cp = _to_sibling(lands[a], flip, h, s, r, k)
                    cp.wait_send()
                    cp.wait_recv()

    send1, recv1, bufs, token = _relay_call(f"{tag}_chips_relay", first, ex.thru, [ex.send_sems, ex.recv_sems], 10 * n,
                                            after_first)

    def run_second(after):
        send2, recv2, lands, token = _relay_call(f"{tag}_diagonal_relay", second, bufs[ns:], [send1, recv1], 2 * n, after)
        return token, lambda after_last: _relay_call(f"{tag}_sibling_wait", last, lands, [send1, recv1, send2, recv2],
                                                     1, after_last)[2]

    return token, run_second


def _col_tile(r, c, block_bytes=2**20):
    return next(t for t in (1024, 512, 256, 128) if c % t == 0 and (r * t * 4 <= block_bytes or t == 128))


def _add_sibling(g4, recv, pos, name):
    _, _, r, c = g4.shape
    tc = _col_tile(r, c, 2**22)

    def body(pos_ref, g_ref, r_ref, o16_ref, mine_ref):
        s = g_ref[0, 0] + r_ref[0]
        o16_ref[0] = s.astype(BF16)

        @pl.when(pl.program_id(1) == pos_ref[1])
        def _():
            mine_ref[...] = s

    slot = pl.BlockSpec((1, r, tc), lambda j, k, pos_ref: (k, 0, j))
    return pl.pallas_call(
        body, name=name,
        out_shape=[pltpu.HBM((4, r, c), BF16), pltpu.HBM((r, c), F32)],
        grid_spec=pltpu.PrefetchScalarGridSpec(
            num_scalar_prefetch=1, grid=(c // tc, 4),
            in_specs=[pl.BlockSpec((1, 1, r, tc), lambda j, k, pos_ref: (k, pos_ref[0], 0, j)), slot],
            out_specs=[slot, pl.BlockSpec((r, tc), lambda j, k, pos_ref: (0, j))]),
        compiler_params=pltpu.CompilerParams(
            dimension_semantics=("parallel", "arbitrary"), vmem_limit_bytes=_vmem_limit(4 * _nbytes((r, tc), F32))),
    )(pos, *_in_hbm(g4, recv))


class _ReduceScatter:
    def __init__(self, tag, grads_t, pos):
        self.tag, self.pos, self.names = tag, pos, list(grads_t)
        g4s = [g.reshape(4, 2, g.size // (N_DEV * g.shape[-1]), g.shape[-1]) for g in grads_t.values()]
        lands = [lax.empty((4,) + g.shape[2:], F32) for g in g4s]
        self.ex = _exchange_start(f"rs_{tag}_sibling_start", _scatter_sibling_copies, g4s, lands, 4 * len(g4s))
        self.token = self.ex.token

    def start_chips(self, after):
        g4s, from_sibling = _exchange_wait(f"rs_{self.tag}_sibling_wait", self.ex, after)
        parts = [_add_sibling(g4, rv, self.pos, f"rs_add_sibling_{k}")
                 for k, g4, rv in zip(self.names, g4s, from_sibling)]
        self.mine = [mine for _, mine in parts]
        p16s = [p16 for p16, _ in parts]
        lands = [lax.empty((3,) + p.shape[1:], BF16) for p in p16s]
        self.ex = _exchange_start(f"rs_{self.tag}_chips_start", _scatter_chips_copies, p16s, lands, 3 * len(p16s))
        self.token = self.ex.token

    def finish(self, after):
        _, from_chips = _exchange_wait(f"rs_{self.tag}_chips_wait", self.ex, after)
        return dict(zip(self.names, zip(self.mine, from_chips)))


def _rope_tables():
    positions = np.arange(SEQ, dtype=np.float32)
    inv_freq = np.power(np.float32(ROPE_THETA), -np.arange(0, ROPE_DIM, 2, dtype=np.float32) / np.float32(ROPE_DIM))
    ang = (positions[:, None] * inv_freq[None, :]).astype(np.float32)
    cos, sin = np.cos(ang).astype(np.float32), np.sin(ang).astype(np.float32)
    ones = np.ones((SEQ, HEAD_DIM - ROPE_DIM), np.float32)
    zeros8 = np.zeros((SEQ, ROPE_HALF), np.float32)
    zeros = np.zeros((SEQ, HEAD_DIM - ROPE_DIM), np.float32)
    c_head = np.concatenate([cos, cos, ones], axis=1)
    s1_head = np.concatenate([-sin, zeros8, zeros], axis=1)
    s2_head = np.concatenate([zeros8, sin, zeros], axis=1)
    return tuple(jnp.asarray(np.concatenate([t, t], axis=1)) for t in (c_head, s1_head, s2_head))


def _rope_apply(x, c, s1, s2):
    w = x.shape[1]
    return x * c + pltpu.roll(x, w - ROPE_HALF, 1) * s1 + pltpu.roll(x, ROPE_HALF, 1) * s2


def _rope_apply_t(dy, c, s1, s2):
    w = dy.shape[1]
    return dy * c + pltpu.roll(dy * s1, ROPE_HALF, 1) + pltpu.roll(dy * s2, w - ROPE_HALF, 1)


def _dil_prev_limit(has_prev):
    return jnp.where(has_prev, 0, BLOCK)


def _dil_valid(limit):
    row = lax.broadcasted_iota(jnp.int32, (BLOCK, 2 * BLOCK), 0)
    col = lax.broadcasted_iota(jnp.int32, (BLOCK, 2 * BLOCK), 1)
    dist = col - row
    return jnp.logical_and(dist >= jnp.where(col < BLOCK, limit, -BLOCK), dist <= BLOCK)


def _upper_half():
    return lax.broadcasted_iota(jnp.int32, (1, LANES), 1) >= HEAD_DIM


def _stack_heads(x):
    upper = _upper_half()
    return jnp.concatenate([jnp.where(upper, 0, x), jnp.where(upper, x, 0)], axis=0)


def _unstack_heads(y):
    n = y.shape[0] // 2
    return jnp.where(_upper_half(), y[n:], y[:n])


def _head_columns(t):
    return jnp.concatenate([t[:, 0:1], t[:, HEAD_DIM:HEAD_DIM + 1]], axis=0)


def _dil_rows(n, d):
    per = N_BLOCKS // d
    r, lb = n // per, n % per

    def rows(b):
        start = b * (BLOCK * d) + r
        return pl.ds(pl.multiple_of(start, BLOCK), BLOCK) if d == 1 else pl.ds(start, BLOCK, stride=d)

    return rows(lb), rows(jnp.maximum(lb - 1, 0)), lb > 0


def _dil_rotate(q_ref, k_ref, c_ref, s1_ref, s2_ref, q_rot, k_rot):
    tabs = (c_ref[...], s1_ref[...], s2_ref[...])
    q_rot[...] = _rope_apply(q_ref[...], *tabs) * QK_SCALE
    k_rot[...] = _rope_apply(k_ref[...], *tabs)


def _dil_specs():
    def col(base):
        return pl.BlockSpec((SEQ, LANES), lambda p: (0, base // LANES + p))

    table = pl.BlockSpec((SEQ, LANES), lambda p: (0, 0))
    return [col(COL_QA), col(COL_KA), col(COL_VA)], [table] * 3


def _store_columns(blocks, dproj_ref, cols, sem):
    copies = [pltpu.make_async_copy(b, dproj_ref.at[:, pl.ds(pl.multiple_of(c * LANES, LANES), LANES)], sem.at[i])
              for i, (b, c) in enumerate(zip(blocks, cols))]
    for cp in copies:
        cp.start()
    for cp in copies:
        cp.wait()


def _dil_window(d, n, k_rot, v_ref):
    rows, prev, has_prev = _dil_rows(n, d)
    kw, vw = k_rot[rows, :].astype(BF16), v_ref[rows, :].astype(BF16)
    if d == N_BLOCKS:
        row = lax.broadcasted_iota(jnp.int32, (BLOCK, BLOCK), 0)
        valid = lax.broadcasted_iota(jnp.int32, (BLOCK, BLOCK), 1) <= row
    else:
        kw = jnp.concatenate([k_rot[prev, :].astype(BF16), kw], axis=0)
        vw = jnp.concatenate([v_ref[prev, :].astype(BF16), vw], axis=0)
        valid = _dil_valid(_dil_prev_limit(has_prev))
    return rows, prev, kw, vw, jnp.concatenate([valid, valid], axis=0)


def _dil_fwd(proj, tables, deps=()):
    def body(q_ref, k_ref, v_ref, c_ref, s1_ref, s2_ref, *rest):
        o_ref, lse_ref, q_rot, k_rot = rest[len(deps):]
        upper = _upper_half()
        _dil_rotate(q_ref, k_ref, c_ref, s1_ref, s2_ref, q_rot, k_rot)

        def blocks_of(d):
            def block(n, carry):
                rows, _, kw, vw, valid = _dil_window(d, n, k_rot, v_ref)
                s = jnp.where(valid, _dot_nt(_stack_heads(q_rot[rows, :].astype(BF16)), kw), NEG_INF)
                m = jnp.max(s, axis=-1, keepdims=True)
                p = jnp.exp(s - m)
                den = jnp.sum(p, axis=-1, keepdims=True)
                o_ref[rows, :] = _unstack_heads(_dot_nn((p * (1.0 / den)).astype(BF16), vw))
                lse = m + jnp.log(den)
                lse_ref[rows, :] = jnp.where(upper, lse[BLOCK:], lse[:BLOCK])
                return carry

            lax.fori_loop(0, N_BLOCKS, block, 0, unroll=4)

        for g, d in enumerate(DILATIONS):
            pl.when(pl.program_id(0) // 2 == g)(functools.partial(blocks_of, d))

    qkv, tabs = _dil_specs()
    out = pl.BlockSpec((SEQ, LANES), lambda p: (0, p))
    return pl.pallas_call(
        body, name="dil_attn_fwd", grid=(DIL_WIDTH // LANES,), in_specs=qkv + tabs + [_ANY] * len(deps),
        out_specs=[out, out],
        out_shape=[pltpu.HBM((SEQ, DIL_WIDTH), F32)] * 2,
        scratch_shapes=[pltpu.VMEM((SEQ, LANES), F32)] * 2,
        compiler_params=pltpu.CompilerParams(dimension_semantics=("parallel",)),
    )(*_in_hbm(proj, proj, proj, *tables), *deps)


def _dil_bwd(proj, tables, do, lse, c, dproj, deps=()):
    def body(q_ref, k_ref, v_ref, c_ref, s1_ref, s2_ref, do_ref, lse_ref, cc_ref, dproj_in, *rest):
        dproj_ref, dq_acc, dk_acc, dv_acc, dq_out, dk_out, dv_out, q_rot, k_rot, sem = rest[len(deps):]
        dk_acc[...] = jnp.zeros_like(dk_acc)
        dv_acc[...] = jnp.zeros_like(dv_acc)
        _dil_rotate(q_ref, k_ref, c_ref, s1_ref, s2_ref, q_rot, k_rot)

        def blocks_of(d):
            def block(n, carry):
                rows, prev, kw, vw, valid = _dil_window(d, n, k_rot, v_ref)
                q2 = _stack_heads(q_rot[rows, :].astype(BF16))
                do2 = _stack_heads(do_ref[rows, :].astype(BF16))
                lse_col, c_col = _head_columns(lse_ref[rows, :]), _head_columns(cc_ref[rows, :])
                p = jnp.where(valid, jnp.exp(_dot_nt(q2, kw) - lse_col), 0.0)
                ds = (p * (_dot_nt(do2, vw) + c_col)).astype(BF16)
                dk, dv = _dot_tn(ds, q2), _dot_tn(p.astype(BF16), do2)
                dq_acc[rows, :] = _unstack_heads(_dot_nn(ds, kw)) * QK_SCALE
                if d == N_BLOCKS:
                    dk_acc[rows, :] += dk
                    dv_acc[rows, :] += dv
                else:
                    dk_acc[prev, :] += dk[:BLOCK]
                    dv_acc[prev, :] += dv[:BLOCK]
                    dk_acc[rows, :] += dk[BLOCK:]
                    dv_acc[rows, :] += dv[BLOCK:]
                return carry

            lax.fori_loop(0, N_BLOCKS, block, 0, unroll=4)

        pair = pl.program_id(0)
        for g, d in enumerate(DILATIONS):
            pl.when(pair // 2 == g)(functools.partial(blocks_of, d))
        tabs = (c_ref[...], s1_ref[...], s2_ref[...])
        dq_out[...] = _rope_apply_t(dq_acc[...], *tabs).astype(BF16)
        dk_out[...] = _rope_apply_t(dk_acc[...], *tabs).astype(BF16)
        dv_out[...] = dv_acc[...].astype(BF16)
        _store_columns((dq_out, dk_out, dv_out), dproj_ref,
                       [base // LANES + pair for base in (COL_QA, COL_KA, COL_VA)], sem)

    qkv, tabs = _dil_specs()
    tok = pl.BlockSpec((SEQ, LANES), lambda p: (0, p))
    return pl.pallas_call(
        body, name="dil_attn_bwd", grid=(DIL_WIDTH // LANES,),
        in_specs=qkv + tabs + [tok, tok, tok, _ANY] + [_ANY] * len(deps), out_specs=_ANY,
        out_shape=pltpu.HBM(dproj.shape, dproj.dtype),
        scratch_shapes=[pltpu.VMEM((SEQ, LANES), F32)] * 3 + [pltpu.VMEM((SEQ, LANES), BF16)] * 3
        + [pltpu.VMEM((SEQ, LANES), F32)] * 2 + [pltpu.SemaphoreType.DMA((3,))],
        input_output_aliases={9: 0},
        compiler_params=pltpu.CompilerParams(dimension_semantics=("arbitrary",)),
    )(*_in_hbm(proj, proj, proj, *tables, do, lse, c, dproj), *deps)


def _group_weights(l0, l1, l2):
    m = jnp.maximum(jnp.maximum(l0, l1), l2)
    e0, e1, e2 = jnp.exp(l0 - m), jnp.exp(l1 - m), jnp.exp(l2 - m)
    tot = e0 + e1 + e2
    return e0 / tot, e1 / tot, e2 / tot


def _dil_combine(o, lse, deps=()):
    def fn(o0, o1, o2, l0, l1, l2):
        w0, w1, w2 = _group_weights(l0, l1, l2)
        return w0 * o0 + w1 * o1 + w2 * o2

    w = DIL_OUT_WIDTH
    return _rowwise(fn, "dil_combine", SEQ, 512, [(o, w, g) for g in range(3)] + [(lse, w, g) for g in range(3)], [],
                    [(w, F32)], deps=deps)[0]


def _dil_combine_bwd(d_out, o, lse, deps=()):
    w = DIL_OUT_WIDTH

    def fn(d, o0, o1, o2, l0, l1, l2):
        row = lax.broadcasted_iota(jnp.int32, (w, w), 0) // HEAD_DIM
        col = lax.broadcasted_iota(jnp.int32, (w, w), 1) // HEAD_DIM
        same_head = jnp.where(row == col, 1.0, 0.0).astype(BF16)
        ws = _group_weights(l0, l1, l2)
        dws = [_dot3_nn(d * og, same_head) for og in (o0, o1, o2)]
        mean = ws[0] * dws[0] + ws[1] * dws[1] + ws[2] * dws[2]
        return jnp.concatenate([wg * d for wg in ws], axis=1), jnp.concatenate([-wg * mean for wg in ws], axis=1)

    return _rowwise(fn, "dil_combine_bwd", SEQ, 512,
                    [(d_out, w, 0)] + [(o, w, g) for g in range(3)] + [(lse, w, g) for g in range(3)], [],
                    [(DIL_WIDTH, F32)] * 2, deps=deps)


def _log1p(e):
    u = 1.0 + e
    return jnp.where(u == 1.0, e, jnp.log(u) * (e / (u - 1.0)))


def _fox_gate(proj, b_pad, deps=()):
    def body(f_ref, b_ref, *rest):
        o_ref = rest[-1]
        z = f_ref[...] + b_ref[...]
        logf = (jnp.minimum(z, 0.0) - _log1p(jnp.exp(-jnp.abs(z)))).T[:F_ROWS]
        row = lax.broadcasted_iota(jnp.int32, (BLOCK, BLOCK), 0)
        col = lax.broadcasted_iota(jnp.int32, (BLOCK, BLOCK), 1)
        before = jnp.where(row <= col, 1.0, 0.0).astype(BF16)
        carry = jnp.zeros((F_ROWS, 1), F32)
        for blk in range(N_BLOCKS):
            run = _dot3_nn(logf[:, blk * BLOCK:(blk + 1) * BLOCK], before) + carry
            o_ref[:, blk * BLOCK:(blk + 1) * BLOCK] = run
            carry = run[:, BLOCK - 1:BLOCK]

    return pl.pallas_call(
        body, name="fox_gate", grid=(1,),
        in_specs=[pl.BlockSpec((SEQ, LANES), lambda i: (0, COL_F // LANES)), pl.BlockSpec((1, LANES), lambda i: (0, 0))]
        + [_ANY] * len(deps),
        out_specs=pl.BlockSpec((F_ROWS, SEQ), lambda i: (0, 0)),
        out_shape=pltpu.HBM((F_ROWS, SEQ), F32),
    )(*_in_hbm(proj, b_pad), *deps)


def _fox_gate_bwd(d_cum, proj, b_pad, dproj):
    def body(d_ref, f_ref, b_ref, dproj_ref, dz_ref, db_ref):
        row = lax.broadcasted_iota(jnp.int32, (BLOCK, BLOCK), 0)
        col = lax.broadcasted_iota(jnp.int32, (BLOCK, BLOCK), 1)
        after = jnp.where(row >= col, 1.0, 0.0).astype(BF16)
        carry = jnp.zeros((F_ROWS, 1), F32)
        parts = [None] * N_BLOCKS
        for blk in reversed(range(N_BLOCKS)):
            run = _dot3_nn(d_ref[:, blk * BLOCK:(blk + 1) * BLOCK], after) + carry
            parts[blk] = run
            carry = run[:, 0:1]
        dlogf = jnp.concatenate(parts, axis=1)
        dlogf = jnp.concatenate([dlogf, jnp.zeros((LANES - F_ROWS, SEQ), F32)], axis=0).T
        dz = dlogf * _sigmoid(-(f_ref[...] + b_ref[...]))
        dz_ref[...] = dz.astype(BF16)
        db_ref[...] = jnp.sum(dz, axis=0, keepdims=True)

    f_cols = pl.BlockSpec((SEQ, LANES), lambda i: (0, COL_F // LANES))
    return pl.pallas_call(
        body, name="fox_gate_bwd", grid=(1,),
        in_specs=[pl.BlockSpec((F_ROWS, SEQ), lambda i: (0, 0)), f_cols, pl.BlockSpec((1, LANES), lambda i: (0, 0)), _ANY],
        out_specs=[f_cols, pl.BlockSpec((1, LANES), lambda i: (0, 0))],
        out_shape=[pltpu.HBM(dproj.shape, dproj.dtype), pltpu.HBM((1, LANES), F32)],
        input_output_aliases={3: 0},
    )(*_in_hbm(d_cum, proj, b_pad, dproj))


FOX_TILE = 256
FOX_TILES = SEQ // FOX_TILE


def _row_to_col(row):
    n = row.shape[1]
    eye = lax.broadcasted_iota(jnp.int32, (n, n), 0) == lax.broadcasted_iota(jnp.int32, (n, n), 1)
    return jnp.sum(jnp.where(eye, row, 0.0), axis=1, keepdims=True)


def _fox_bias(f_row, i):
    t = FOX_TILE
    ext = (i + 1) * t
    bias = _row_to_col(f_row[:, i * t:(i + 1) * t]) - f_row[:, :ext]
    row = lax.broadcasted_iota(jnp.int32, (t, ext), 0) + i * t
    col = lax.broadcasted_iota(jnp.int32, (t, ext), 1)
    return bias, col <= row


def _fox_specs():
    qkv = [pl.BlockSpec((SEQ, LANES), lambda p, base=base: (0, base // LANES + p)) for base in (COL_QB, COL_KB, COL_VB)]
    return qkv, pl.BlockSpec((F_ROWS, SEQ), lambda p: (0, 0))


def _fox_fwd(proj, f_rows):
    t = FOX_TILE

    def body(q_ref, k_ref, v_ref, f_ref, o_ref, lse_ref):
        pair = pl.program_id(0)
        upper = _upper_half()
        k16, v16 = k_ref[...].astype(BF16), v_ref[...].astype(BF16)
        f_row = [f_ref[pl.ds(2 * pair + e, 1), :] for e in range(2)]
        for i in range(FOX_TILES):
            ext = (i + 1) * t
            q_tile = (q_ref[i * t:(i + 1) * t, :] * QK_SCALE).astype(BF16)
            s2 = _dot_nt(_stack_heads(q_tile), k16[:ext])
            pns, lses = [], []
            for e in range(2):
                bias, causal = _fox_bias(f_row[e], i)
                s = jnp.where(causal, s2[e * t:(e + 1) * t] + bias, NEG_INF)
                m = jnp.max(s, axis=-1, keepdims=True)
                p = jnp.exp(s - m)
                den = jnp.sum(p, axis=-1, keepdims=True)
                pns.append((p * (1.0 / den)).astype(BF16))
                lses.append(m + jnp.log(den))
            o_ref[i * t:(i + 1) * t, :] = _unstack_heads(_dot_nn(jnp.concatenate(pns, axis=0), v16[:ext]))
            lse_ref[i * t:(i + 1) * t, :] = jnp.where(upper, lses[1], lses[0])

    qkv, f_spec = _fox_specs()
    tok = pl.BlockSpec((SEQ, LANES), lambda p: (0, p))
    return pl.pallas_call(
        body, name="fox_attn_fwd", grid=(FOX_WIDTH // LANES,),
        in_specs=qkv + [f_spec], out_specs=[tok, tok],
        out_shape=[pltpu.HBM((SEQ, FOX_WIDTH), F32)] * 2,
        compiler_params=pltpu.CompilerParams(
            dimension_semantics=("parallel",), vmem_limit_bytes=_vmem_limit(8 * t * SEQ * 4)),
    )(*_in_hbm(proj, proj, proj, f_rows))


def _fox_bwd(proj, do, lse, f_rows, dproj):
    t = FOX_TILE

    def body(q_ref, k_ref, v_ref, f_ref, do_ref, lse_ref, dproj_in, dproj_ref, df_ref, dk_acc, dv_acc,
             dq_out, dk_out, dv_out, sem):
        pair = pl.program_id(0)
        upper = _upper_half()
        k16, v16 = k_ref[...].astype(BF16), v_ref[...].astype(BF16)
        f_row = [f_ref[pl.ds(2 * pair + e, 1), :] for e in range(2)]
        dk_acc[...] = jnp.zeros_like(dk_acc)
        dv_acc[...] = jnp.zeros_like(dv_acc)
        df_ref[...] = jnp.zeros_like(df_ref)
        for i in range(FOX_TILES):
            ext = (i + 1) * t
            q_tile = (q_ref[i * t:(i + 1) * t, :] * QK_SCALE).astype(BF16)
            do_tile = do_ref[i * t:(i + 1) * t, :]
            lse_t = lse_ref[i * t:(i + 1) * t, :]
            q2, do2 = _stack_heads(q_tile), _stack_heads(do_tile)
            s2, dp2 = _dot_nt(q2, k16[:ext]), _dot_nt(do2, v16[:ext])
            ps, dss = [], []
            for e in range(2):
                bias, causal = _fox_bias(f_row[e], i)
                s = s2[e * t:(e + 1) * t] + bias
                p = jnp.where(causal, jnp.exp(s - lse_t[:, e * HEAD_DIM:e * HEAD_DIM + 1]), 0.0)
                dp = dp2[e * t:(e + 1) * t]
                ds = p * (dp - jnp.sum(p * dp, axis=-1, keepdims=True))
                df_ref[0, e:e + 1, :ext] -= jnp.sum(ds, axis=0, keepdims=True)
                ps.append(p.astype(BF16))
                dss.append(ds.astype(BF16))
            ds2, p2 = jnp.concatenate(dss, axis=0), jnp.concatenate(ps, axis=0)
            dq_out[i * t:(i + 1) * t, :] = (_unstack_heads(_dot_nn(ds2, k16[:ext])) * QK_SCALE).astype(BF16)
            dk_acc[:ext, :] += _dot_tn(ds2, q2)
            dv_acc[:ext, :] += _dot_tn(p2, do2)
        dk_out[...] = dk_acc[...].astype(BF16)
        dv_out[...] = dv_acc[...].astype(BF16)
        _store_columns((dq_out, dk_out, dv_out), dproj_ref, [base // LANES + pair for base in (COL_QB, COL_KB, COL_VB)],
                       sem)

    qkv, f_spec = _fox_specs()
    tok = pl.BlockSpec((SEQ, LANES), lambda p: (0, p))
    return pl.pallas_call(
        body, name="fox_attn_bwd", grid=(FOX_WIDTH // LANES,),
        in_specs=qkv + [f_spec, tok, tok, _ANY],
        out_specs=[_ANY, pl.BlockSpec((1, SUBLANES, SEQ), lambda p: (p, 0, 0))],
        out_shape=[pltpu.HBM(dproj.shape, dproj.dtype),
                   pltpu.HBM((FOX_WIDTH // LANES, SUBLANES, SEQ), F32)],
        scratch_shapes=[pltpu.VMEM((SEQ, LANES), F32)] * 2 + [pltpu.VMEM((SEQ, LANES), BF16)] * 3
        + [pltpu.SemaphoreType.DMA((3,))],
        input_output_aliases={6: 0},
        compiler_params=pltpu.CompilerParams(
            dimension_semantics=("arbitrary",), vmem_limit_bytes=_vmem_limit(10 * t * SEQ * 4)),
    )(*_in_hbm(proj, proj, proj, f_rows, do, lse, dproj))


MIX_TILE = 512


def _mix_out(out_a, out_b, proj, x, wt_pa, wt_pb, w_out, g_post, g_ffn_pre):
    tm = MIX_TILE

    def body(a_ref, b_ref, ga_ref, gb_ref, x_ref, wpa_ref, wpb_ref, wo_ref, g2_ref, g3_ref,
             merged_ref, mix_ref, x1_ref, h2_ref):
        ya = _dot_nn(a_ref[...].astype(BF16), wpa_ref[...])
        yb = _dot_nn(b_ref[...].astype(BF16), wpb_ref[...])
        merged = (_sigmoid(ga_ref[...]) * ya + _sigmoid(gb_ref[...]) * yb).astype(BF16)
        merged_ref[...] = merged
        mix = _dot_nn(merged, wo_ref[...])
        mix_ref[...] = mix
        x1 = x_ref[...] + mix * _rms_scale(mix) * g2_ref[...]
        x1_ref[...] = x1
        h2_ref[...] = (x1 * _rms_scale(x1) * g3_ref[...]).astype(BF16)

    def rows(w, cb=0):
        return pl.BlockSpec((tm, w), lambda i, cb=cb: (i, cb))

    def whole(a):
        return pl.BlockSpec(a.shape, lambda i: (0, 0))

    d = D_MODEL
    blk = _nbytes((tm, d), F32) * 6 + sum(_nbytes(a.shape, BF16) for a in (wt_pa, wt_pb, w_out))
    return pl.pallas_call(
        body, name="mix_out", grid=(SEQ // tm,),
        in_specs=[rows(DIL_OUT_WIDTH), rows(FOX_WIDTH), rows(d, COL_GA // d), rows(d, COL_GB // d), rows(d),
                  whole(wt_pa), whole(wt_pb), whole(w_out), whole(g_post), whole(g_ffn_pre)],
        out_specs=[rows(d)] * 4,
        out_shape=[pltpu.HBM((SEQ, d), dt) for dt in (BF16, F32, F32, BF16)],
        compiler_params=pltpu.CompilerParams(dimension_semantics=("parallel",), vmem_limit_bytes=_vmem_limit(blk)),
    )(*_in_hbm(out_a, out_b, proj, proj, x, wt_pa, wt_pb, w_out, g_post, g_ffn_pre))


def _mix_out_bwd(dmix, out_a, out_b, proj, wt_pa, wt_pb, w_out, deps=()):
    tm = MIX_TILE

    def body(dm_ref, a_ref, b_ref, ga_ref, gb_ref, wpa_ref, wpb_ref, wo_ref, *rest):
        dproj_ref, dya_ref, dyb_ref, da_ref, db_ref = rest[len(deps):]
        dmerged = _dot_nt(dm_ref[...], wo_ref[...])
        ya = _dot_nn(a_ref[...].astype(BF16), wpa_ref[...])
        yb = _dot_nn(b_ref[...].astype(BF16), wpb_ref[...])
        sa, sb = _sigmoid(ga_ref[...]), _sigmoid(gb_ref[...])
        dproj_ref[:, COL_GA:COL_GA + D_MODEL] = (dmerged * ya * (sa * (1.0 - sa))).astype(BF16)
        dproj_ref[:, COL_GB:COL_GB + D_MODEL] = (dmerged * yb * (sb * (1.0 - sb))).astype(BF16)
        dproj_ref[:, COL_GB + D_MODEL:] = jnp.zeros((tm, COL_QA - COL_GB - D_MODEL), BF16)
        dya = (dmerged * sa).astype(BF16)
        dyb = (dmerged * sb).astype(BF16)
        dya_ref[...] = dya
        dyb_ref[...] = dyb
        da_ref[...] = _dot_nt(dya, wpa_ref[...])
        db_ref[...] = _dot_nt(dyb, wpb_ref[...]).astype(BF16)

    def rows(w, cb=0):
        return pl.BlockSpec((tm, w), lambda i, cb=cb: (i, cb))

    def whole(a):
        return pl.BlockSpec(a.shape, lambda i: (0, 0))

    d = D_MODEL
    blk = _nbytes((tm, d), F32) * 8 + sum(_nbytes(a.shape, BF16) for a in (wt_pa, wt_pb, w_out))
    return pl.pallas_call(
        body, name="mix_out_bwd", grid=(SEQ // tm,),
        in_specs=[rows(d), rows(DIL_OUT_WIDTH), rows(FOX_WIDTH), rows(d, COL_GA // d), rows(d, COL_GB // d),
                  whole(wt_pa), whole(wt_pb), whole(w_out)] + [_ANY] * len(deps),
        out_specs=[rows(COL_QA)] + [rows(d)] * 2 + [rows(DIL_OUT_WIDTH), rows(FOX_WIDTH)],
        out_shape=[pltpu.HBM((SEQ, PROJ_COLS), BF16)] + [pltpu.HBM((SEQ, d), BF16)] * 2
        + [pltpu.HBM((SEQ, DIL_OUT_WIDTH), F32), pltpu.HBM((SEQ, FOX_WIDTH), BF16)],
        compiler_params=pltpu.CompilerParams(dimension_semantics=("parallel",), vmem_limit_bytes=_vmem_limit(blk)),
    )(*_in_hbm(dmix, out_a, out_b, proj, proj, wt_pa, wt_pb, w_out), *deps)


FFN_TM, FFN_TN = 2048, 256


def _ffn_up(h2, wt_gate, wt_up):
    tm, tn = FFN_TM, FFN_TN

    def body(h_ref, wg_ref, wu_ref, gate_ref, up_ref, act_ref):
        for rows in (slice(0, tm // 2), slice(tm // 2, tm)):
            gate = _dot_nt(h_ref[rows, :], wg_ref[...])
            up = _dot_nt(h_ref[rows, :], wu_ref[...])
            gate_ref[rows, :] = gate
            up_ref[rows, :] = up
            act_ref[rows, :] = (gate * _sigmoid(gate) * up).astype(BF16)

    tile = pl.BlockSpec((tm, tn), lambda i, j: (i, j))
    w_spec = pl.BlockSpec((tn, D_MODEL), lambda i, j: (j, 0))
    return pl.pallas_call(
        body, name="ffn_up", grid=(SEQ // tm, D_FF // tn),
        in_specs=[pl.BlockSpec((tm, D_MODEL), lambda i, j: (i, 0)), w_spec, w_spec],
        out_specs=[tile, tile, tile],
        out_shape=[pltpu.HBM((SEQ, D_FF), dt) for dt in (F32, F32, BF16)],
        compiler_params=pltpu.CompilerParams(
            dimension_semantics=("parallel", "parallel"), vmem_limit_bytes=_vmem_limit(8 * 2**20)),
    )(h2, wt_gate, wt_up)


def _ffn_act_bwd(dff, w_down, gate, up):
    tm, tn = FFN_TM, FFN_TN

    def body(d_ref, wd_ref, gate_ref, up_ref, dgate_ref, dup_ref):
        for rows in (slice(0, tm // 2), slice(tm // 2, tm)):
            dact = _dot_nt(d_ref[rows, :], wd_ref[...])
            gate = gate_ref[rows, :]
            sg = _sigmoid(gate)
            dgate_ref[rows, :] = (dact * up_ref[rows, :] * (sg * (1.0 + gate * (1.0 - sg)))).astype(BF16)
            dup_ref[rows, :] = (dact * (gate * sg)).astype(BF16)

    tile = pl.BlockSpec((tm, tn), lambda i, j: (i, j))
    return pl.pallas_call(
        body, name="ffn_act_bwd", grid=(SEQ // tm, D_FF // tn),
        in_specs=[pl.BlockSpec((tm, D_MODEL), lambda i, j: (i, 0)), pl.BlockSpec((tn, D_MODEL), lambda i, j: (j, 0)),
                  tile, tile],
        out_specs=[tile, tile],
        out_shape=[pltpu.HBM((SEQ, D_FF), BF16)] * 2,
        compiler_params=pltpu.CompilerParams(
            dimension_semantics=("parallel", "parallel"), vmem_limit_bytes=_vmem_limit(8 * 2**20)),
    )(dff, w_down, gate, up)


EPILOGUE_TM = 512


def _loss_head(act, w_down, x1, target, g_post):
    def fn(ff, x1, tgt, g):
        r = _rms_scale(ff)
        nrm = ff * r
        err = (x1 + nrm * g) - tgt
        loss = 0.5 * jnp.sum(jnp.mean(err * err, axis=-1, keepdims=True), axis=0, keepdims=True)
        dy = err * (1.0 / D_MODEL)
        u = dy * g
        dff = r * u - ff * (r * r * r) * jnp.mean(u * ff, axis=-1, keepdims=True)
        return dy, dff, jnp.broadcast_to(loss, (1, LANES)), jnp.sum(dy * nrm, axis=0, keepdims=True)

    d = D_MODEL
    return _matmul_rowwise([(act, w_down)], fn, "ffn_down_loss", EPILOGUE_TM, [(x1, d, 0), (target, d, 0)], [g_post],
                           [(d, F32), (d, BF16)], [LANES, d])


def _post_ffn_bwd(dgate, wt_gate, dup, wt_up, x1, dy, mix, g_ffn_pre, g_mix_post, deps=()):
    def fn(dh2, x1, dy, mix, g3, g2):
        dx, dg3 = _rms_bwd(x1, dh2, g3)
        dx1 = dy + dx
        dmix, dg2 = _rms_bwd(mix, dx1, g2)
        return dx1, dmix, dg3, dg2

    d = D_MODEL
    return _matmul_rowwise([(dgate, wt_gate), (dup, wt_up)], fn, "ffn_up_bwd", EPILOGUE_TM,
                           [(x1, d, 0), (dy, d, 0), (mix, d, 0)], [g_ffn_pre, g_mix_post],
                           [(d, F32), (d, BF16)], [d, d], deps=deps)


def _input_bwd(dproj, wt_r, x, dx1, g_pre, deps=()):
    def fn(dh, x, dx1, g):
        dx, dg = _rms_bwd(x, dh, g)
        return dx1 + dx, dg

    d = D_MODEL
    return _matmul_rowwise([(dproj, wt_r)], fn, "in_proj_bwd", EPILOGUE_TM, [(x, d, 0), (dx1, d, 0)], [g_pre],
                           [(d, F32)], [d], deps=deps)


def _adam_math(w, g, m, v):
    m = ADAM_B1 * m + (1.0 - ADAM_B1) * g
    v = ADAM_B2 * v + (1.0 - ADAM_B2) * (g * g)
    m_hat = m / (1.0 - ADAM_B1 ** ADAM_STEP)
    v_hat = v / (1.0 - ADAM_B2 ** ADAM_STEP)
    delta = -ADAM_LR * (m_hat / (jnp.sqrt(v_hat) + ADAM_EPS) + ADAM_WD * w)
    return delta, m, v


def _adam(w, mine, recv, m, v, name):
    r, c = w.shape
    tc = _col_tile(r, c)

    def body(w_ref, p_ref, r_ref, m_ref, v_ref, g_ref, d_ref, nm_ref, nv_ref):
        g = ((p_ref[...] + r_ref[0].astype(F32)) + r_ref[1].astype(F32)) + r_ref[2].astype(F32)
        g_ref[...] = g
        d_ref[...], nm_ref[...], nv_ref[...] = _adam_math(w_ref[...], g, m_ref[...], v_ref[...])

    spec = pl.BlockSpec((r, tc), lambda j: (0, j))
    return pl.pallas_call(
        body, name=name, grid=(c // tc,),
        in_specs=[spec, spec, pl.BlockSpec((3, r, tc), lambda j: (0, 0, j)), spec, spec], out_specs=[spec] * 4,
        out_shape=[pltpu.HBM((r, c), F32)] * 4,
        compiler_params=pltpu.CompilerParams(dimension_semantics=("parallel",)),
    )(*_in_hbm(w, mine, recv, m, v))


def _adam_small(gathered, ws, ms, vs, loss_parts):
    n = len(ws)

    def body(*refs):
        outs = refs[4 * n + 1:]
        loss = refs[4 * n][0]
        for dev in range(1, N_DEV):
            loss = loss + refs[4 * n][dev]
        outs[4 * n][...] = loss
        for i in range(n):
            ga_ref, w_ref, m_ref, v_ref = (refs[j * n + i] for j in range(4))
            g = ga_ref[0]
            for dev in range(1, N_DEV):
                g = g + ga_ref[dev]
            g = g[:, :w_ref.shape[1]]
            outs[4 * i][...] = g
            outs[4 * i + 1][...], outs[4 * i + 2][...], outs[4 * i + 3][...] = _adam_math(
                w_ref[...], g, m_ref[...], v_ref[...])

    out_shape = [pltpu.HBM(w.shape, F32) for w in ws for _ in range(4)]
    out_shape.append(pltpu.HBM((1, LANES), F32))
    out = pl.pallas_call(body, name="adam_small", out_shape=out_shape)(*gathered, *ws, *ms, *vs, loss_parts)
    return [out[4 * i:4 * i + 4] for i in range(n)], out[4 * n]


_PROJ_SEGMENTS = ((3848, 5896), (None, COL_QA - 2 * D_MODEL), (0, 3840), (3840, 3848), (None, PROJ_COLS - COL_F - 8))


def _proj_weight_t(gathered):
    pieces, zeros, at = [], [], 0
    for lo, hi in _PROJ_SEGMENTS:
        if lo is None:
            zeros.append((at, hi))
            at += hi
            continue
        for dev in range(lo // IN_SHARD, (hi - 1) // IN_SHARD + 1):
            a, b = max(lo, dev * IN_SHARD), min(hi, (dev + 1) * IN_SHARD)
            pieces.append((dev, a - dev * IN_SHARD, at + a - lo, b - a))
        at += hi - lo
    assert at == PROJ_COLS
    tc = 2 * LANES

    def body(g_ref, o_ref, shards, rows):
        for dev in range(N_DEV):
            shards[dev] = g_ref[dev].astype(F32)
        for dev, src, dst, n in pieces:
            rows[pl.ds(dst, n), :] = shards[dev, pl.ds(src, n), :]
        for dst, n in zeros:
            rows[pl.ds(dst, n), :] = jnp.zeros((n, tc), F32)
        o_ref[...] = rows[...].astype(o_ref.dtype)

    return pl.pallas_call(
        body, name="w_in_rows", grid=(D_MODEL // tc,),
        in_specs=[pl.BlockSpec((N_DEV, IN_SHARD, tc), lambda j: (0, 0, j))],
        out_specs=pl.BlockSpec((PROJ_COLS, tc), lambda j: (0, j)),
        out_shape=pltpu.HBM((PROJ_COLS, D_MODEL), gathered.dtype),
        scratch_shapes=[pltpu.VMEM((N_DEV, IN_SHARD, tc), F32), pltpu.VMEM((PROJ_COLS, tc), F32)],
        compiler_params=pltpu.CompilerParams(
            dimension_semantics=("parallel",), vmem_limit_bytes=_vmem_limit(2 * _nbytes((PROJ_COLS, tc), F32))),
    )(*_in_hbm(gathered))


def _proj_weight_grad_slots(dwt_r):
    starts, at = [], 0
    for lo, hi in _PROJ_SEGMENTS:
        if lo is not None:
            starts.append((lo, hi, at))
        at += hi if lo is None else hi - lo
    pieces = []
    for dev in range(N_DEV):
        lo, end = dev * IN_SHARD, (dev + 1) * IN_SHARD
        for seg_lo, seg_hi, seg_at in sorted(starts):
            a, b = max(lo, seg_lo), min(end, seg_hi)
            if a < b:
                pieces.append((dev, a - lo, seg_at + a - seg_lo, b - a))

    def body(g_ref, o_ref):
        for dev, dst, src, rows in pieces:
            o_ref[dev, pl.ds(dst, rows), :] = g_ref[pl.ds(src, rows), :]

    tc = 2 * LANES
    return pl.pallas_call(
        body, name="grad_w_in_slots", grid=(D_MODEL // tc,),
        in_specs=[pl.BlockSpec((PROJ_COLS, tc), lambda j: (0, j))],
        out_specs=pl.BlockSpec((N_DEV, IN_SHARD, tc), lambda j: (0, 0, j)),
        out_shape=pltpu.HBM((N_DEV, IN_SHARD, D_MODEL), F32),
        compiler_params=pltpu.CompilerParams(
            dimension_semantics=("parallel",), vmem_limit_bytes=_vmem_limit(2 * _nbytes((PROJ_COLS, tc), F32))),
    )(*_in_hbm(dwt_r))


def kernel(x, w_in, w_proj_a, w_proj_b, w_out, b_forget, w_ffn_gate, w_ffn_up, w_ffn_down, norm_mix_pre, norm_mix_post, norm_ffn_pre, norm_ffn_post, loss_target, m_w_in, m_w_proj_a, m_w_proj_b, m_w_out, m_b_forget, m_w_ffn_gate, m_w_ffn_up, m_w_ffn_down, m_norm_mix_pre, m_norm_mix_post, m_norm_ffn_pre, m_norm_ffn_post, v_w_in, v_w_proj_a, v_w_proj_b, v_w_out, v_b_forget, v_w_ffn_gate, v_w_ffn_up, v_w_ffn_down, v_norm_mix_pre, v_norm_mix_post, v_norm_ffn_pre, v_norm_ffn_post):
    d = D_MODEL
    names = ("w_in", "w_proj_a", "w_proj_b", "w_out", "w_ffn_gate", "w_ffn_up", "w_ffn_down")
    col_sharded = ("w_in", "w_ffn_gate", "w_ffn_up")

    def row_shards(arrs):
        return {k: (a[0].T if k in col_sharded else a[0]) for k, a in zip(names, arrs)}

    shards = row_shards((w_in, w_proj_a, w_proj_b, w_out, w_ffn_gate, w_ffn_up, w_ffn_down))
    moments_m = row_shards((m_w_in, m_w_proj_a, m_w_proj_b, m_w_out, m_w_ffn_gate, m_w_ffn_up, m_w_ffn_down))
    moments_v = row_shards((v_w_in, v_w_proj_a, v_w_proj_b, v_w_out, v_w_ffn_gate, v_w_ffn_up, v_w_ffn_down))
    pos = jnp.stack([lax.axis_index("c"), 2 * lax.axis_index("x") + lax.axis_index("y")]).astype(jnp.int32)
    x2, target = x[0], loss_target[0]

    me = 4 * lax.axis_index("x") + 2 * lax.axis_index("y") + lax.axis_index("c")
    mid_names, ffn_names = names[1:4], names[4:]
    first_names, later_names = names[:1], names[1:]
    shards16 = {k: shards[k].astype(BF16) for k in names}

    def landing(k):
        return lax.dynamic_update_slice(lax.empty((N_DEV,) + shards[k].shape, BF16), shards16[k][None], (me, 0, 0))

    ag_first = _exchange_start("ag_first_chips_start", _gather_two_route_copies, [shards16[k] for k in first_names],
                               [landing(k) for k in first_names], 4 * len(first_names))
    h = _rowwise(lambda xb, g: xb * _rms_scale(xb) * g, "norm_mix_pre", SEQ, 512, [(x2, d, 0)], [norm_mix_pre],
                 [(d, BF16)], deps=[ag_first.token])[0]
    later_lands = [landing(k) for k in later_names]
    _, ag_first_diagonal = _gather_relay(ag_first, "ag_first", [h, shards["w_in"], moments_m["w_in"], moments_v["w_in"],
                                                               *later_lands, *[shards16[k] for k in later_names]])
    relayed, ag_first_last = ag_first_diagonal([])
    ag_later = _exchange_start("ag_later_chips_start", _gather_two_route_copies, [shards16[k] for k in later_names],
                               later_lands, 4 * len(later_names), after=[relayed])
    gathered = dict(zip(first_names, ag_first_last([ag_later.token])))
    wt_r = _proj_weight_t(gathered["w_in"])

    proj = _matmul([(h, *_in_hbm(wt_r))], "nt", F32, "in_proj", 1024, 896, 1024)
    tables = _rope_tables()
    relayed, ag_later_diagonal = _gather_relay(ag_later, "ag_later", [proj])
    o_dil, lse_dil = _dil_fwd(proj, tables, deps=[relayed])
    out_a = _dil_combine(o_dil, lse_dil)
    relayed, ag_later_last = ag_later_diagonal([out_a])

    b_pad = jnp.pad(b_forget, ((0, 0), (0, LANES - N_FOX_HEADS)))
    f_rows = _fox_gate(proj, b_pad, deps=[relayed])
    out_b, lse_fox = _fox_fwd(proj, f_rows)

    gathered = dict(zip(later_names, ag_later_last([out_b])))
    wt_pa = gathered["w_proj_a"].transpose(1, 0, 2).reshape(DIL_OUT_WIDTH, d)
    wt_pb = gathered["w_proj_b"].transpose(1, 0, 2).reshape(FOX_WIDTH, d)
    w_o = gathered["w_out"].reshape(d, d)
    wt_g = gathered["w_ffn_gate"].reshape(D_FF, d)
    wt_u = gathered["w_ffn_up"].reshape(D_FF, d)
    w_d = gathered["w_ffn_down"].reshape(D_FF, d)
    merged, mix, x1, h2 = _mix_out(out_a, out_b, proj, x2, wt_pa, wt_pb, w_o, norm_mix_post, norm_ffn_pre)

    gate, up, act = _ffn_up(h2, wt_g, wt_u)
    dy, dff, loss_part, dg_ffn_post = _loss_head(act, w_d, x1, target, norm_ffn_post)

    dgate, dup = _ffn_act_bwd(dff, w_d, gate, up)
    grads_t = {}
    grads_t["w_ffn_down"] = _matmul([(act, dff)], "tn", F32, "grad_w_ffn_down", 1408, 512, 2048, staged=False)
    grads_t["w_ffn_gate"] = _matmul([(dgate, h2)], "tn", F32, "grad_w_ffn_gate", 1408, 512, 2048)
    grads_t["w_ffn_up"] = _matmul([(dup, h2)], "tn", F32, "grad_w_ffn_up", 1408, 512, 2048)
    rs_ffn = _ReduceScatter("ffn", {k: grads_t[k] for k in ffn_names}, pos)
    dx1, dmix, dg_ffn_pre, dg_mix_post = _post_ffn_bwd(dgate, wt_g, dup, wt_u, x1, dy, mix, norm_ffn_pre, norm_mix_post,
                                                       deps=[rs_ffn.token])
    rs_ffn.start_chips([dmix])

    dproj, dya, dyb, d_out_a, d_out_b = _mix_out_bwd(dmix, out_a, out_b, proj, wt_pa, wt_pb, w_o, deps=[rs_ffn.token])
    grads_t["w_out"] = _matmul([(merged, dmix)], "tn", F32, "grad_w_out", 1024, 1024, 1024)
    def column_slots(g):
        return g.reshape(g.shape[0], N_DEV, LANES).transpose(1, 0, 2)

    grads_t["w_proj_a"] = column_slots(_matmul([(out_a, dya)], "tn", F32, "grad_w_proj_a", DIL_OUT_WIDTH, 1024, SEQ))
    grads_t["w_proj_b"] = column_slots(_matmul([(out_b, dyb)], "tn", F32, "grad_w_proj_b", FOX_WIDTH, 1024, SEQ))
    rs_mid = _ReduceScatter("mid", {k: grads_t[k] for k in mid_names}, pos)

    do_dil, c_dil = _dil_combine_bwd(d_out_a, o_dil, lse_dil, deps=[rs_mid.token])
    rs_mid.start_chips([c_dil])
    dproj, d_cum = _fox_bwd(proj, d_out_b, lse_fox, f_rows, dproj)
    d_cum_rows = jnp.pad(d_cum[:, :2].reshape(N_FOX_HEADS, SEQ), ((0, F_ROWS - N_FOX_HEADS), (0, 0)))
    dproj, db_part = _fox_gate_bwd(d_cum_rows, proj, b_pad, dproj)
    dproj = _dil_bwd(proj, tables, do_dil, lse_dil, c_dil, dproj, deps=[rs_mid.token])

    dwt_r = _matmul([(dproj, h)], "tn", F32, "grad_w_in", 896, 1024, 2048)
    rs_in = _ReduceScatter("in", {"w_in": _proj_weight_grad_slots(dwt_r)}, pos)
    def finish(rs, after):
        return {k: _adam(shards[k], mine, recv, moments_m[k], moments_v[k], "adam_" + k)
                for k, (mine, recv) in rs.finish(after).items()}

    done = finish(rs_ffn, [rs_in.token])
    rs_in.start_chips([done[k][0] for k in ffn_names])
    grad_x, dg_mix_pre = _input_bwd(dproj, wt_r, x2, dx1, norm_mix_pre, deps=[rs_in.token])
    done.update(finish(rs_mid, [grad_x]))

    small_all = _all_gather([dg_mix_pre, dg_mix_post, dg_ffn_pre, dg_ffn_post, db_part, loss_part],
                            "small_grads_all_gather", deps=[done[k][0] for k in mid_names])
    small, loss = _adam_small(small_all[:5], [norm_mix_pre, norm_mix_post, norm_ffn_pre, norm_ffn_post, b_forget],
                              [m_norm_mix_pre, m_norm_mix_post, m_norm_ffn_pre, m_norm_ffn_post, m_b_forget],
                              [v_norm_mix_pre, v_norm_mix_post, v_norm_ffn_pre, v_norm_ffn_post, v_b_forget],
                              small_all[5])

    done.update(finish(rs_in, [small[0][0]]))

    def leaves(i):
        def nat(k):
            a = done[k][i]
            return (a.T if k in col_sharded else a)[None]

        return [nat("w_in"), nat("w_proj_a"), nat("w_proj_b"), nat("w_out"), small[4][i],
                nat("w_ffn_gate"), nat("w_ffn_up"), nat("w_ffn_down"), *[small[r][i] for r in range(4)]]

    return (loss[0, 0], grad_x[None], *leaves(0), *leaves(1), *leaves(2), *leaves(3))
```

```python
import functools
import math

import jax
import jax.numpy as jnp
import numpy as np
from jax import lax
from jax.experimental import pallas as pl
from jax.experimental.pallas import tpu as pltpu

F32 = jnp.float32
BF16 = jnp.bfloat16
MESH = pl.DeviceIdType.MESH

D_MODEL = 1024
SEQ = 2048
HEAD_DIM = 64
BLOCK = 128
N_BLOCKS = SEQ // BLOCK
DILATIONS = (1, 4, 16)
N_FOX_HEADS = 8
DIL_WIDTH = 768
DIL_OUT_WIDTH = 256
FOX_WIDTH = 512
D_FF = 2816
ROPE_THETA = 500000.0
ROPE_DIM = HEAD_DIM // 4
ROPE_HALF = ROPE_DIM // 2
EPS = 1e-6
NEG_INF = -1e30
QK_SCALE = 1.0 / math.sqrt(HEAD_DIM)
IN_COLS = 5896
N_DEV = 8
IN_SHARD = IN_COLS // N_DEV

ADAM_LR = 0.001
ADAM_B1 = 0.9
ADAM_B2 = 0.999
ADAM_EPS = 1e-08
ADAM_WD = 0.01
ADAM_STEP = 10

V7X_VMEM_BYTES = 64 * 2**20
LANES = 128
SUBLANES = 8

PROJ_COLS = 6272
COL_GA, COL_GB = 0, 1024
COL_QA, COL_KA, COL_VA = 2304, 3072, 3840
COL_QB, COL_KB, COL_VB = 4608, 5120, 5632
COL_F = 6144
F_ROWS = 16


def _vmem_limit(block_bytes):
    want = 2 * block_bytes + 16 * 2**20
    return int(min(max(want, 32 * 2**20), V7X_VMEM_BYTES - 8 * 2**20))


def _nbytes(shape, dtype):
    return math.prod(shape) * jnp.dtype(dtype).itemsize


def _in_hbm(*arrays):
    return [pltpu.with_memory_space_constraint(a, pltpu.HBM) for a in arrays]


def _dot(a, b, dims):
    return lax.dot_general(a, b, (dims, ((), ())), preferred_element_type=F32)


def _dot_nn(a, b):
    return _dot(a, b, ((1,), (0,)))


def _dot_nt(a, b):
    return _dot(a, b, ((1,), (1,)))


def _dot_tn(a, b):
    return _dot(a, b, ((0,), (0,)))


def _sigmoid(z):
    return 1.0 / (1.0 + jnp.exp(-z))


def _split3(x):
    hi = x.astype(BF16)
    r1 = x - hi.astype(F32)
    mid = r1.astype(BF16)
    lo = (r1 - mid.astype(F32)).astype(BF16)
    return hi, mid, lo


def _dot3_nn(x, ones_matrix):
    hi, mid, lo = _split3(x)
    return (_dot_nn(hi, ones_matrix) + _dot_nn(mid, ones_matrix)) + _dot_nn(lo, ones_matrix)


def _rowwise(fn, name, n_rows, tm, row_ins, bcast_ins, row_outs, acc_outs=(), deps=()):
    n_in = len(row_ins) + len(bcast_ins)
    n_ro = len(row_outs)

    def body(*refs):
        res = fn(*[r[...] for r in refs[:n_in]])
        if not isinstance(res, (tuple, list)):
            res = (res,)
        outs = refs[n_in + len(deps):]
        for r, o in zip(res[:n_ro], outs[:n_ro]):
            o[...] = r.astype(o.dtype)
        first = pl.program_id(0) == 0
        for r, o in zip(res[n_ro:], outs[n_ro:]):
            _accumulate(o, r, first)

    in_specs = [pl.BlockSpec((tm, w), lambda i, cb=cb: (i, cb)) for _, w, cb in row_ins]
    in_specs += [pl.BlockSpec(a.shape, lambda i: (0, 0)) for a in bcast_ins]
    in_specs += [pl.BlockSpec(memory_space=pl.ANY)] * len(deps)
    out_specs = [pl.BlockSpec((tm, w), lambda i: (i, 0)) for w, _ in row_outs]
    out_specs += [pl.BlockSpec((1, w), lambda i: (0, 0)) for w in acc_outs]
    out_shape = [pltpu.HBM((n_rows, w), dt) for w, dt in row_outs]
    out_shape += [pltpu.HBM((1, w), F32) for w in acc_outs]
    blk = sum(_nbytes((tm, w), a.dtype) for a, w, _ in row_ins) + sum(_nbytes((tm, w), dt) for w, dt in row_outs)
    return pl.pallas_call(
        body, name=name, grid=(n_rows // tm,), in_specs=in_specs, out_specs=out_specs, out_shape=out_shape,
        compiler_params=pltpu.CompilerParams(
            dimension_semantics=("arbitrary" if acc_outs else "parallel",), vmem_limit_bytes=_vmem_limit(3 * blk)),
    )(*_in_hbm(*[a for a, _, _ in row_ins], *bcast_ins), *deps)


def _accumulate(o_ref, part, first):
    @pl.when(first)
    def _():
        o_ref[...] = part

    @pl.when(jnp.logical_not(first))
    def _():
        o_ref[...] += part


_MM_DIMS = {"nn": ((1,), (0,)), "nt": ((1,), (1,)), "tn": ((0,), (0,))}


def _matmul(pairs, mode, out_dtype, name, tm, tn, tk, deps=(), staged=True):
    a0, b0 = pairs[0]
    if mode == "tn":
        kk, m = a0.shape
    else:
        m, kk = a0.shape
    n = b0.shape[0] if mode == "nt" else b0.shape[1]
    assert m % tm == 0 and n % tn == 0 and kk % tk == 0, (name, m, n, kk)
    nk = kk // tk
    n_pairs = len(pairs)
    dims = _MM_DIMS[mode]
    n_in = 2 * n_pairs + len(deps)

    def body(*refs):
        o_ref = refs[n_in]
        part = None
        for p in range(n_pairs):
            d = _dot(refs[2 * p][...].astype(BF16), refs[2 * p + 1][...].astype(BF16), dims)
            part = d if part is None else part + d
        if nk == 1:
            o_ref[...] = part.astype(o_ref.dtype)
            return
        acc = refs[n_in + 1]
        k = pl.program_id(2)

        @pl.when(k == 0)
        def _():
            acc[...] = part

        @pl.when(k > 0)
        def _():
            acc[...] += part

        @pl.when(k == nk - 1)
        def _():
            o_ref[...] = acc[...].astype(o_ref.dtype)

    if mode == "tn":
        a_spec = pl.BlockSpec((tk, tm), lambda i, j, k: (k, i))
    else:
        a_spec = pl.BlockSpec((tm, tk), lambda i, j, k: (i, k))
    if mode == "nt":
        b_spec = pl.BlockSpec((tn, tk), lambda i, j, k: (j, k))
    else:
        b_spec = pl.BlockSpec((tk, tn), lambda i, j, k: (k, j))
    blk = sum(_nbytes((tm, tk), a.dtype) + _nbytes((tk, tn), b.dtype) for a, b in pairs) + 2 * _nbytes((tm, tn), F32)
    flat = [a for pair in pairs for a in pair]
    return pl.pallas_call(
        body, name=name, grid=(m // tm, n // tn, nk),
        in_specs=[a_spec, b_spec] * n_pairs + [pl.BlockSpec(memory_space=pl.ANY)] * len(deps),
        out_specs=pl.BlockSpec((tm, tn), lambda i, j, k: (i, j)),
        out_shape=pltpu.HBM((m, n), out_dtype),
        scratch_shapes=[] if nk == 1 else [pltpu.VMEM((tm, tn), F32)],
        compiler_params=pltpu.CompilerParams(
            dimension_semantics=("parallel", "parallel", "arbitrary"), vmem_limit_bytes=_vmem_limit(blk)),
    )(*(flat if staged else _in_hbm(*flat)), *deps)


def _matmul_rowwise(pairs, fn, name, tm, row_ins, bcast_ins, row_outs, acc_outs=(), deps=()):
    m = pairs[0][0].shape[0]
    n_mm, n_in = 2 * len(pairs), len(row_ins) + len(bcast_ins)
    n_ro = len(row_outs)

    def body(*refs):
        prod = None
        for p in range(len(pairs)):
            part = _dot_nn(refs[2 * p][...].astype(BF16), refs[2 * p + 1][...].astype(BF16))
            prod = part if prod is None else prod + part
        res = fn(prod, *[r[...] for r in refs[n_mm:n_mm + n_in]])
        outs = refs[n_mm + n_in + len(deps):]
        for r, o in zip(res[:n_ro], outs[:n_ro]):
            o[...] = r.astype(o.dtype)
        first = pl.program_id(0) == 0
        for r, o in zip(res[n_ro:], outs[n_ro:]):
            _accumulate(o, r, first)

    in_specs = []
    for a, b in pairs:
        in_specs += [pl.BlockSpec((tm, a.shape[1]), lambda i: (i, 0)),
                     pl.BlockSpec(b.shape, lambda i: (0, 0), pipeline_mode=pl.Buffered(1))]
    in_specs += [pl.BlockSpec((tm, w), lambda i, cb=cb: (i, cb)) for _, w, cb in row_ins]
    in_specs += [pl.BlockSpec(a.shape, lambda i: (0, 0)) for a in bcast_ins]
    in_specs += [_ANY] * len(deps)
    out_specs = [pl.BlockSpec((tm, w), lambda i: (i, 0)) for w, _ in row_outs]
    out_specs += [pl.BlockSpec((1, w), lambda i: (0, 0)) for w in acc_outs]
    out_shape = [pltpu.HBM((m, w), dt) for w, dt in row_outs]
    out_shape += [pltpu.HBM((1, w), F32) for w in acc_outs]
    blk = sum(_nbytes((tm, a.shape[1]), a.dtype) + _nbytes(b.shape, b.dtype) // 2 for a, b in pairs)
    blk += sum(_nbytes((tm, w), a.dtype) for a, w, _ in row_ins) + sum(_nbytes((tm, w), dt) for w, dt in row_outs)
    return pl.pallas_call(
        body, name=name, grid=(m // tm,), in_specs=in_specs, out_specs=out_specs, out_shape=out_shape,
        compiler_params=pltpu.CompilerParams(dimension_semantics=("arbitrary",), vmem_limit_bytes=_vmem_limit(blk)),
    )(*[a for pair in pairs for a in pair], *[a for a, _, _ in row_ins], *bcast_ins, *deps)


def _rms_scale(x):
    return lax.rsqrt(jnp.mean(x * x, axis=-1, keepdims=True) + EPS)


def _rms_bwd(xin, dyn, g):
    r = _rms_scale(xin)
    u = dyn * g
    dx = r * u - xin * (r * r * r) * jnp.mean(u * xin, axis=-1, keepdims=True)
    dg = jnp.sum(dyn * xin * r, axis=0, keepdims=True)
    return dx, dg


def _mesh_pos():
    return lax.axis_index("x"), lax.axis_index("y"), lax.axis_index("c")


def _all_gather(xs, name, deps=()):
    n = len(xs)

    def body(*refs):
        x_refs, out_refs = refs[:n], refs[n + len(deps):2 * n + len(deps)]
        send_sems, recv_sems, local_sems = refs[2 * n + len(deps):]
        mx, my, mc = _mesh_pos()
        me, sib = (mx, my, mc), (mx, my, 1 - mc)
        chips = [(1 - mx, my), (mx, 1 - my), (1 - mx, 1 - my)]

        def slot(a, dev):
            px, py, pc = dev
            return out_refs[a].at[4 * px + 2 * py + pc]

        def copy(k, a, block, to, src=None):
            return pltpu.make_async_remote_copy(
                src_ref=slot(a, block) if src is None else src, dst_ref=slot(a, block),
                send_sem=send_sems.at[a * 7 + k], recv_sem=recv_sems.at[a * 7 + k],
                device_id=to, device_id_type=MESH)

        mine = [pltpu.make_async_copy(x_refs[a], slot(a, me), local_sems.at[a]) for a in range(n)]
        for cp in mine:
            cp.start()
        first = []
        for a in range(n):
            first.append(copy(0, a, me, sib, x_refs[a]))
            first += [copy(1 + j, a, me, (*chip, mc), x_refs[a]) for j, chip in enumerate(chips)]
        for cp in first:
            cp.start()
        passed = []
        for a in range(n):
            for j, chip in enumerate(chips):
                copy(1 + j, a, (*chip, mc), me).wait_recv()
                fwd = copy(4 + j, a, (*chip, mc), sib)
                fwd.start()
                passed.append(fwd)
        for a in range(n):
            copy(0, a, sib, me).wait_recv()
            for j, chip in enumerate(chips):
                copy(4 + j, a, (*chip, 1 - mc), me).wait_recv()
        for cp in first + passed:
            cp.wait_send()
        for cp in mine:
            cp.wait()

    hbm = pl.BlockSpec(memory_space=pl.ANY)
    return pl.pallas_call(
        body, name=name,
        out_shape=[pltpu.HBM((N_DEV,) + x.shape, x.dtype) for x in xs],
        in_specs=[hbm] * (n + len(deps)), out_specs=[hbm] * n,
        scratch_shapes=[pltpu.SemaphoreType.DMA((7 * n,)), pltpu.SemaphoreType.DMA((7 * n,)),
                        pltpu.SemaphoreType.DMA((n,))],
    )(*xs, *deps)


_HBM = pl.BlockSpec(memory_space=pltpu.HBM)
_SEM = pl.BlockSpec(memory_space=pltpu.SEMAPHORE)
_ANY = pl.BlockSpec(memory_space=pl.ANY)
_DATAFLOW = pltpu.SideEffectType.DATAFLOW_SIDE_EFFECTING


def _flip_peer(flip):
    mx, my, mc = _mesh_pos()
    return (1 - mx if flip & 2 else mx, 1 - my if flip & 1 else my, mc)


def _remote(src, dst, send_sems, recv_sems, k, peer):
    return pltpu.make_async_remote_copy(src_ref=src, dst_ref=dst, send_sem=send_sems.at[k], recv_sem=recv_sems.at[k],
                                        device_id=peer, device_id_type=MESH)


def _scatter_sibling_copies(srcs, lands, send_sems, recv_sems):
    mx, my, mc = _mesh_pos()
    return [_remote(srcs[a].at[k, 1 - mc], lands[a].at[k], send_sems, recv_sems, 4 * a + k, (mx, my, 1 - mc))
            for a in range(len(srcs)) for k in range(4)]


def _scatter_chips_copies(srcs, lands, send_sems, recv_sems):
    mx, my, _ = _mesh_pos()
    k0 = 2 * mx + my
    return [_remote(srcs[a].at[jnp.bitwise_xor(k0, flip)], lands[a].at[flip - 1], send_sems, recv_sems,
                    3 * a + flip - 1, _flip_peer(flip))
            for a in range(len(srcs)) for flip in (1, 2, 3)]


class _Exchange:
    def __init__(self, copies, n_src, send_sems, recv_sems, thru, token):
        self.copies, self.n_src, self.send_sems, self.recv_sems, self.thru, self.token = (
            copies, n_src, send_sems, recv_sems, thru, token)


def _exchange_start(name, copies, srcs, lands, n_copies, after=()):
    bufs = list(srcs) + list(lands)
    nb, ns = len(bufs), len(srcs)

    def body(*refs):
        send_sems, recv_sems = refs[nb + len(after)], refs[nb + len(after) + 1]
        for cp in copies(refs[:ns], refs[ns:nb], send_sems, recv_sems):
            cp.start()
        refs[-1][...] = jnp.zeros_like(refs[-1])

    out = pl.pallas_call(
        body, name=name,
        out_shape=(pltpu.SemaphoreType.DMA((n_copies,)), pltpu.SemaphoreType.DMA((n_copies,)),
                   *[pltpu.HBM(b.shape, b.dtype) for b in bufs], pltpu.HBM((SUBLANES, LANES), F32)),
        in_specs=[_HBM] * nb + [_ANY] * len(after),
        out_specs=(_SEM, _SEM, *[_HBM] * nb, pl.BlockSpec(memory_space=pltpu.VMEM)),
        input_output_aliases={i: 2 + i for i in range(nb)},
        compiler_params=pltpu.CompilerParams(has_side_effects=_DATAFLOW),
    )(*[pltpu.with_memory_space_constraint(b, pltpu.HBM) for b in bufs], *after)
    return _Exchange(copies, ns, out[0], out[1], list(out[2:2 + nb]), out[-1])


def _exchange_wait(name, ex, after):
    nb, ns = len(ex.thru), ex.n_src

    def body(*refs):
        for cp in ex.copies(refs[:ns], refs[ns:nb], refs[nb], refs[nb + 1]):
            cp.wait_send()
            cp.wait_recv()

    out = pl.pallas_call(
        body, name=name, out_shape=tuple(pltpu.HBM(b.shape, b.dtype) for b in ex.thru),
        in_specs=[_HBM] * nb + [_SEM, _SEM] + [_ANY] * len(after), out_specs=tuple([_HBM] * nb),
        input_output_aliases={i: i for i in range(nb)},
        compiler_params=pltpu.CompilerParams(has_side_effects=_DATAFLOW),
    )(*ex.thru, ex.send_sems, ex.recv_sems, *after)
    return list(out[:ns]), list(out[ns:])


def _halves(ref):
    half = ref.shape[1] // 2
    if half % LANES == 0:
        return ref.at[:, pl.ds(0, half)], ref.at[:, pl.ds(half, half)]
    half = ref.shape[0] // 2
    assert half % (2 * SUBLANES) == 0, ref.shape
    return ref.at[pl.ds(0, half)], ref.at[pl.ds(half, half)]


def _gather_two_route_copies(srcs, lands, send_sems, recv_sems):
    mx, my, mc = _mesh_pos()
    me = 4 * mx + 2 * my + mc
    return [_remote(_halves(srcs[a])[h], _halves(lands[a].at[me])[h], send_sems, recv_sems, 4 * a + i, _flip_peer(flip))
            for a in range(len(srcs)) for i, (flip, h) in enumerate(((2, 0), (1, 1), (2, 1), (1, 0)))]


def _to_sibling(land, flip, h, send_sems, recv_sems, k):
    mx, my, mc = _mesh_pos()
    part = _halves(land.at[2 * jnp.bitwise_xor(2 * mx + my, flip) + mc])[h]
    return _remote(part, part, send_sems, recv_sems, k, (mx, my, 1 - mc))


def _second_hop(land, send_sems, recv_sems, k):
    mx, my, mc = _mesh_pos()
    k0 = 2 * mx + my
    from_y = _halves(land.at[2 * jnp.bitwise_xor(k0, 1) + mc])[1]
    from_x = _halves(land.at[2 * jnp.bitwise_xor(k0, 2) + mc])[0]
    return (_remote(from_y, from_y, send_sems, recv_sems, k, _flip_peer(2)),
            _remote(from_x, from_x, send_sems, recv_sems, k + 1, _flip_peer(1)))


def _relay_call(name, body, bufs, sems, n_new, after):
    nb, n_in = len(bufs), len(bufs) + len(sems) + len(after)

    def call_body(*refs):
        body(refs[:nb], refs[nb:nb + len(sems)], refs[n_in], refs[n_in + 1])
        refs[-1][...] = jnp.zeros_like(refs[-1])

    out = pl.pallas_call(
        call_body, name=name,
        out_shape=(pltpu.SemaphoreType.DMA((n_new,)), pltpu.SemaphoreType.DMA((n_new,)),
                   *[pltpu.HBM(b.shape, b.dtype) for b in bufs], pltpu.HBM((SUBLANES, LANES), F32)),
        in_specs=[_HBM] * nb + [_SEM] * len(sems) + [_ANY] * len(after),
        out_specs=(_SEM, _SEM, *[_HBM] * nb, pl.BlockSpec(memory_space=pltpu.VMEM)),
        input_output_aliases={i: 2 + i for i in range(nb)},
        compiler_params=pltpu.CompilerParams(has_side_effects=_DATAFLOW),
    )(*bufs, *sems, *after)
    return out[0], out[1], list(out[2:2 + nb]), out[-1]


def _gather_relay(ex, tag, after_first):
    ns = ex.n_src
    n = len(ex.thru) - ns

    def first(bufs, sems, send, recv):
        lands, first_hop = bufs[ns:], ex.copies(bufs[:ns], bufs[ns:], *sems)
        for a in range(n):
            for h in (0, 1):
                _to_sibling(lands[a], 0, h, send, recv, 10 * a + h).start()
        for a in range(n):
            x_first, y_second, x_second, y_first = first_hop[4 * a:4 * a + 4]
            to_x, to_y = _second_hop(lands[a], send, recv, 10 * a + 8)
            x_first.wait_recv()
            to_y.start()
            _to_sibling(lands[a], 2, 0, send, recv, 10 * a + 4).start()
            y_second.wait_recv()
            to_x.start()
            _to_sibling(lands[a], 1, 1, send, recv, 10 * a + 3).start()
            x_second.wait_recv()
            _to_sibling(lands[a], 2, 1, send, recv, 10 * a + 5).start()
            y_first.wait_recv()
            _to_sibling(lands[a], 1, 0, send, recv, 10 * a + 2).start()
        for cp in first_hop:
            cp.wait_send()

    def second(lands, sems, send, recv):
        for a in range(n):
            to_x, to_y = _second_hop(lands[a], *sems, 10 * a + 8)
            to_y.wait_recv()
            _to_sibling(lands[a], 3, 0, send, recv, 2 * a).start()
            to_x.wait_recv()
            _to_sibling(lands[a], 3, 1, send, recv, 2 * a + 1).start()
            to_x.wait_send()
            to_y.wait_send()

    def last(lands, sems, send, recv):
        for a in range(n):
            for flip in range(4):
                for h in (0, 1):
                    s, r, k = (sems[2], sems[3], 2 * a + h) if flip == 3 else (sems[0], sems[1], 10 * a + 2 * flip + h)
                    cp = _to_sibling(lands[a], flip, h, s, r, k)
                    cp.wait_send()
                    cp.wait_recv()

    send1, recv1, bufs, token = _relay_call(f"{tag}_chips_relay", first, ex.thru, [ex.send_sems, ex.recv_sems], 10 * n,
                                            after_first)

    def run_second(after):
        send2, recv2, lands, token = _relay_call(f"{tag}_diagonal_relay", second, bufs[ns:], [send1, recv1], 2 * n, after)
        return token, lambda after_last: _relay_call(f"{tag}_sibling_wait", last, lands, [send1, recv1, send2, recv2],
                                                     1, after_last)[2]

    return token, run_second


def _col_tile(r, c, block_bytes=2**20):
    return next(t for t in (1024, 512, 256, 128) if c % t == 0 and (r * t * 4 <= block_bytes or t == 128))


def _add_sibling(g4, recv, pos, name):
    _, _, r, c = g4.shape
    tc = _col_tile(r, c, 2**22)

    def body(pos_ref, g_ref, r_ref, o16_ref, mine_ref):
        s = g_ref[0, 0] + r_ref[0]
        o16_ref[0] = s.astype(BF16)

        @pl.when(pl.program_id(1) == pos_ref[1])
        def _():
            mine_ref[...] = s

    slot = pl.BlockSpec((1, r, tc), lambda j, k, pos_ref: (k, 0, j))
    return pl.pallas_call(
        body, name=name,
        out_shape=[pltpu.HBM((4, r, c), BF16), pltpu.HBM((r, c), F32)],
        grid_spec=pltpu.PrefetchScalarGridSpec(
            num_scalar_prefetch=1, grid=(c // tc, 4),
            in_specs=[pl.BlockSpec((1, 1, r, tc), lambda j, k, pos_ref: (k, pos_ref[0], 0, j)), slot],
            out_specs=[slot, pl.BlockSpec((r, tc), lambda j, k, pos_ref: (0, j))]),
        compiler_params=pltpu.CompilerParams(
            dimension_semantics=("parallel", "arbitrary"), vmem_limit_bytes=_vmem_limit(4 * _nbytes((r, tc), F32))),
    )(pos, *_in_hbm(g4, recv))


class _ReduceScatter:
    def __init__(self, tag, grads_t, pos):
        self.tag, self.pos, self.names = tag, pos, list(grads_t)
        g4s = [g.reshape(4, 2, g.size // (N_DEV * g.shape[-1]), g.shape[-1]) for g in grads_t.values()]
        lands = [lax.empty((4,) + g.shape[2:], F32) for g in g4s]
        self.ex = _exchange_start(f"rs_{tag}_sibling_start", _scatter_sibling_copies, g4s, lands, 4 * len(g4s))
        self.token = self.ex.token

    def start_chips(self, after):
        g4s, from_sibling = _exchange_wait(f"rs_{self.tag}_sibling_wait", self.ex, after)
        parts = [_add_sibling(g4, rv, self.pos, f"rs_add_sibling_{k}")
                 for k, g4, rv in zip(self.names, g4s, from_sibling)]
        self.mine = [mine for _, mine in parts]
        p16s = [p16 for p16, _ in parts]
        lands = [lax.empty((3,) + p.shape[1:], BF16) for p in p16s]
        self.ex = _exchange_start(f"rs_{self.tag}_chips_start", _scatter_chips_copies, p16s, lands, 3 * len(p16s))
        self.token = self.ex.token

    def finish(self, after):
        _, from_chips = _exchange_wait(f"rs_{self.tag}_chips_wait", self.ex, after)
        return dict(zip(self.names, zip(self.mine, from_chips)))


def _rope_tables():
    positions = np.arange(SEQ, dtype=np.float32)
    inv_freq = np.power(np.float32(ROPE_THETA), -np.arange(0, ROPE_DIM, 2, dtype=np.float32) / np.float32(ROPE_DIM))
    ang = (positions[:, None] * inv_freq[None, :]).astype(np.float32)
    cos, sin = np.cos(ang).astype(np.float32), np.sin(ang).astype(np.float32)
    ones = np.ones((SEQ, HEAD_DIM - ROPE_DIM), np.float32)
    zeros8 = np.zeros((SEQ, ROPE_HALF), np.float32)
    zeros = np.zeros((SEQ, HEAD_DIM - ROPE_DIM), np.float32)
    c_head = np.concatenate([cos, cos, ones], axis=1)
    s1_head = np.concatenate([-sin, zeros8, zeros], axis=1)
    s2_head = np.concatenate([zeros8, sin, zeros], axis=1)
    return tuple(jnp.asarray(np.concatenate([t, t], axis=1)) for t in (c_head, s1_head, s2_head))


def _rope_apply(x, c, s1, s2):
    w = x.shape[1]
    return x * c + pltpu.roll(x, w - ROPE_HALF, 1) * s1 + pltpu.roll(x, ROPE_HALF, 1) * s2


def _rope_apply_t(dy, c, s1, s2):
    w = dy.shape[1]
    return dy * c + pltpu.roll(dy * s1, ROPE_HALF, 1) + pltpu.roll(dy * s2, w - ROPE_HALF, 1)


def _dil_prev_limit(has_prev):
    return jnp.where(has_prev, 0, BLOCK)


def _dil_valid(limit):
    row = lax.broadcasted_iota(jnp.int32, (BLOCK, 2 * BLOCK), 0)
    col = lax.broadcasted_iota(jnp.int32, (BLOCK, 2 * BLOCK), 1)
    dist = col - row
    return jnp.logical_and(dist >= jnp.where(col < BLOCK, limit, -BLOCK), dist <= BLOCK)


def _upper_half():
    return lax.broadcasted_iota(jnp.int32, (1, LANES), 1) >= HEAD_DIM


def _stack_heads(x):
    upper = _upper_half()
    return jnp.concatenate([jnp.where(upper, 0, x), jnp.where(upper, x, 0)], axis=0)


def _unstack_heads(y):
    n = y.shape[0] // 2
    return jnp.where(_upper_half(), y[n:], y[:n])


def _head_columns(t):
    return jnp.concatenate([t[:, 0:1], t[:, HEAD_DIM:HEAD_DIM + 1]], axis=0)


def _dil_rows(n, d):
    per = N_BLOCKS // d
    r, lb = n // per, n % per

    def rows(b):
        start = b * (BLOCK * d) + r
        return pl.ds(pl.multiple_of(start, BLOCK), BLOCK) if d == 1 else pl.ds(start, BLOCK, stride=d)

    return rows(lb), rows(jnp.maximum(lb - 1, 0)), lb > 0


def _dil_rotate(q_ref, k_ref, c_ref, s1_ref, s2_ref, q_rot, k_rot):
    tabs = (c_ref[...], s1_ref[...], s2_ref[...])
    q_rot[...] = _rope_apply(q_ref[...], *tabs) * QK_SCALE
    k_rot[...] = _rope_apply(k_ref[...], *tabs)


def _dil_specs():
    def col(base):
        return pl.BlockSpec((SEQ, LANES), lambda p: (0, base // LANES + p))

    table = pl.BlockSpec((SEQ, LANES), lambda p: (0, 0))
    return [col(COL_QA), col(COL_KA), col(COL_VA)], [table] * 3


def _store_columns(blocks, dproj_ref, cols, sem):
    copies = [pltpu.make_async_copy(b, dproj_ref.at[:, pl.ds(pl.multiple_of(c * LANES, LANES), LANES)], sem.at[i])
              for i, (b, c) in enumerate(zip(blocks, cols))]
    for cp in copies:
        cp.start()
    for cp in copies:
        cp.wait()


def _dil_window(d, n, k_rot, v_ref):
    rows, prev, has_prev = _dil_rows(n, d)
    kw, vw = k_rot[rows, :].astype(BF16), v_ref[rows, :].astype(BF16)
    if d == N_BLOCKS:
        row = lax.broadcasted_iota(jnp.int32, (BLOCK, BLOCK), 0)
        valid = lax.broadcasted_iota(jnp.int32, (BLOCK, BLOCK), 1) <= row
    else:
        kw = jnp.concatenate([k_rot[prev, :].astype(BF16), kw], axis=0)
        vw = jnp.concatenate([v_ref[prev, :].astype(BF16), vw], axis=0)
        valid = _dil_valid(_dil_prev_limit(has_prev))
    return rows, prev, kw, vw, jnp.concatenate([valid, valid], axis=0)


def _dil_fwd(proj, tables, deps=()):
    def body(q_ref, k_ref, v_ref, c_ref, s1_ref, s2_ref, *rest):
        o_ref, lse_ref, q_rot, k_rot = rest[len(deps):]
        upper = _upper_half()
        _dil_rotate(q_ref, k_ref, c_ref, s1_ref, s2_ref, q_rot, k_rot)

        def blocks_of(d):
            def block(n, carry):
                rows, _, kw, vw, valid = _dil_window(d, n, k_rot, v_ref)
                s = jnp.where(valid, _dot_nt(_stack_heads(q_rot[rows, :].astype(BF16)), kw), NEG_INF)
                m = jnp.max(s, axis=-1, keepdims=True)
                p = jnp.exp(s - m)
                den = jnp.sum(p, axis=-1, keepdims=True)
                o_ref[rows, :] = _unstack_heads(_dot_nn((p * (1.0 / den)).astype(BF16), vw))
                lse = m + jnp.log(den)
                lse_ref[rows, :] = jnp.where(upper, lse[BLOCK:], lse[:BLOCK])
                return carry

            lax.fori_loop(0, N_BLOCKS, block, 0, unroll=4)

        for g, d in enumerate(DILATIONS):
            pl.when(pl.program_id(0) // 2 == g)(functools.partial(blocks_of, d))

    qkv, tabs = _dil_specs()
    out = pl.BlockSpec((SEQ, LANES), lambda p: (0, p))
    return pl.pallas_call(
        body, name="dil_attn_fwd", grid=(DIL_WIDTH // LANES,), in_specs=qkv + tabs + [_ANY] * len(deps),
        out_specs=[out, out],
        out_shape=[pltpu.HBM((SEQ, DIL_WIDTH), F32)] * 2,
        scratch_shapes=[pltpu.VMEM((SEQ, LANES), F32)] * 2,
        compiler_params=pltpu.CompilerParams(dimension_semantics=("parallel",)),
    )(*_in_hbm(proj, proj, proj, *tables), *deps)


def _dil_bwd(proj, tables, do, lse, c, dproj, deps=()):
    def body(q_ref, k_ref, v_ref, c_ref, s1_ref, s2_ref, do_ref, lse_ref, cc_ref, dproj_in, *rest):
        dproj_ref, dq_acc, dk_acc, dv_acc, dq_out, dk_out, dv_out, q_rot, k_rot, sem = rest[len(deps):]
        dk_acc[...] = jnp.zeros_like(dk_acc)
        dv_acc[...] = jnp.zeros_like(dv_acc)
        _dil_rotate(q_ref, k_ref, c_ref, s1_ref, s2_ref, q_rot, k_rot)

        def blocks_of(d):
            def block(n, carry):
                rows, prev, kw, vw, valid = _dil_window(d, n, k_rot, v_ref)
                q2 = _stack_heads(q_rot[rows, :].astype(BF16))
                do2 = _stack_heads(do_ref[rows, :].astype(BF16))
                lse_col, c_col = _head_columns(lse_ref[rows, :]), _head_columns(cc_ref[rows, :])
                p = jnp.where(valid, jnp.exp(_dot_nt(q2, kw) - lse_col), 0.0)
                ds = (p * (_dot_nt(do2, vw) + c_col)).astype(BF16)
                dk, dv = _dot_tn(ds, q2), _dot_tn(p.astype(BF16), do2)
                dq_acc[rows, :] = _unstack_heads(_dot_nn(ds, kw)) * QK_SCALE
                if d == N_BLOCKS:
                    dk_acc[rows, :] += dk
                    dv_acc[rows, :] += dv
                else:
                    dk_acc[prev, :] += dk[:BLOCK]
                    dv_acc[prev, :] += dv[:BLOCK]
                    dk_acc[rows, :] += dk[BLOCK:]
                    dv_acc[rows, :] += dv[BLOCK:]
                return carry

            lax.fori_loop(0, N_BLOCKS, block, 0, unroll=4)

        pair = pl.program_id(0)
        for g, d in enumerate(DILATIONS):
            pl.when(pair // 2 == g)(functools.partial(blocks_of, d))
        tabs = (c_ref[...], s1_ref[...], s2_ref[...])
        dq_out[...] = _rope_apply_t(dq_acc[...], *tabs).astype(BF16)
        dk_out[...] = _rope_apply_t(dk_acc[...], *tabs).astype(BF16)
        dv_out[...] = dv_acc[...].astype(BF16)
        _store_columns((dq_out, dk_out, dv_out), dproj_ref,
                       [base // LANES + pair for base in (COL_QA, COL_KA, COL_VA)], sem)

    qkv, tabs = _dil_specs()
    tok = pl.BlockSpec((SEQ, LANES), lambda p: (0, p))
    return pl.pallas_call(
        body, name="dil_attn_bwd", grid=(DIL_WIDTH // LANES,),
        in_specs=qkv + tabs + [tok, tok, tok, _ANY] + [_ANY] * len(deps), out_specs=_ANY,
        out_shape=pltpu.HBM(dproj.shape, dproj.dtype),
        scratch_shapes=[pltpu.VMEM((SEQ, LANES), F32)] * 3 + [pltpu.VMEM((SEQ, LANES), BF16)] * 3
        + [pltpu.VMEM((SEQ, LANES), F32)] * 2 + [pltpu.SemaphoreType.DMA((3,))],
        input_output_aliases={9: 0},
        compiler_params=pltpu.CompilerParams(dimension_semantics=("arbitrary",)),
    )(*_in_hbm(proj, proj, proj, *tables, do, lse, c, dproj), *deps)


def _group_weights(l0, l1, l2):
    m = jnp.maximum(jnp.maximum(l0, l1), l2)
    e0, e1, e2 = jnp.exp(l0 - m), jnp.exp(l1 - m), jnp.exp(l2 - m)
    tot = e0 + e1 + e2
    return e0 / tot, e1 / tot, e2 / tot


def _dil_combine(o, lse, deps=()):
    def fn(o0, o1, o2, l0, l1, l2):
        w0, w1, w2 = _group_weights(l0, l1, l2)
        return w0 * o0 + w1 * o1 + w2 * o2

    w = DIL_OUT_WIDTH
    return _rowwise(fn, "dil_combine", SEQ, 512, [(o, w, g) for g in range(3)] + [(lse, w, g) for g in range(3)], [],
                    [(w, F32)], deps=deps)[0]


def _dil_combine_bwd(d_out, o, lse, deps=()):
    w = DIL_OUT_WIDTH

    def fn(d, o0, o1, o2, l0, l1, l2):
        row = lax.broadcasted_iota(jnp.int32, (w, w), 0) // HEAD_DIM
        col = lax.broadcasted_iota(jnp.int32, (w, w), 1) // HEAD_DIM
        same_head = jnp.where(row == col, 1.0, 0.0).astype(BF16)
        ws = _group_weights(l0, l1, l2)
        dws = [_dot3_nn(d * og, same_head) for og in (o0, o1, o2)]
        mean = ws[0] * dws[0] + ws[1] * dws[1] + ws[2] * dws[2]
        return jnp.concatenate([wg * d for wg in ws], axis=1), jnp.concatenate([-wg * mean for wg in ws], axis=1)

    return _rowwise(fn, "dil_combine_bwd", SEQ, 512,
                    [(d_out, w, 0)] + [(o, w, g) for g in range(3)] + [(lse, w, g) for g in range(3)], [],
                    [(DIL_WIDTH, F32)] * 2, deps=deps)


def _log1p(e):
    u = 1.0 + e
    return jnp.where(u == 1.0, e, jnp.log(u) * (e / (u - 1.0)))


def _fox_gate(proj, b_pad, deps=()):
    def body(f_ref, b_ref, *rest):
        o_ref = rest[-1]
        z = f_ref[...] + b_ref[...]
        logf = (jnp.minimum(z, 0.0) - _log1p(jnp.exp(-jnp.abs(z)))).T[:F_ROWS]
        row = lax.broadcasted_iota(jnp.int32, (BLOCK, BLOCK), 0)
        col = lax.broadcasted_iota(jnp.int32, (BLOCK, BLOCK), 1)
        before = jnp.where(row <= col, 1.0, 0.0).astype(BF16)
        carry = jnp.zeros((F_ROWS, 1), F32)
        for blk in range(N_BLOCKS):
            run = _dot3_nn(logf[:, blk * BLOCK:(blk + 1) * BLOCK], before) + carry
            o_ref[:, blk * BLOCK:(blk + 1) * BLOCK] = run
            carry = run[:, BLOCK - 1:BLOCK]

    return pl.pallas_call(
        body, name="fox_gate", grid=(1,),
        in_specs=[pl.BlockSpec((SEQ, LANES), lambda i: (0, COL_F // LANES)), pl.BlockSpec((1, LANES), lambda i: (0, 0))]
        + [_ANY] * len(deps),
        out_specs=pl.BlockSpec((F_ROWS, SEQ), lambda i: (0, 0)),
        out_shape=pltpu.HBM((F_ROWS, SEQ), F32),
    )(*_in_hbm(proj, b_pad), *deps)


def _fox_gate_bwd(d_cum, proj, b_pad, dproj):
    def body(d_ref, f_ref, b_ref, dproj_ref, dz_ref, db_ref):
        row = lax.broadcasted_iota(jnp.int32, (BLOCK, BLOCK), 0)
        col = lax.broadcasted_iota(jnp.int32, (BLOCK, BLOCK), 1)
        after = jnp.where(row >= col, 1.0, 0.0).astype(BF16)
        carry = jnp.zeros((F_ROWS, 1), F32)
        parts = [None] * N_BLOCKS
        for blk in reversed(range(N_BLOCKS)):
            run = _dot3_nn(d_ref[:, blk * BLOCK:(blk + 1) * BLOCK], after) + carry
            parts[blk] = run
            carry = run[:, 0:1]
        dlogf = jnp.concatenate(parts, axis=1)
        dlogf = jnp.concatenate([dlogf, jnp.zeros((LANES - F_ROWS, SEQ), F32)], axis=0).T
        dz = dlogf * _sigmoid(-(f_ref[...] + b_ref[...]))
        dz_ref[...] = dz.astype(BF16)
        db_ref[...] = jnp.sum(dz, axis=0, keepdims=True)

    f_cols = pl.BlockSpec((SEQ, LANES), lambda i: (0, COL_F // LANES))
    return pl.pallas_call(
        body, name="fox_gate_bwd", grid=(1,),
        in_specs=[pl.BlockSpec((F_ROWS, SEQ), lambda i: (0, 0)), f_cols, pl.BlockSpec((1, LANES), lambda i: (0, 0)), _ANY],
        out_specs=[f_cols, pl.BlockSpec((1, LANES), lambda i: (0, 0))],
        out_shape=[pltpu.HBM(dproj.shape, dproj.dtype), pltpu.HBM((1, LANES), F32)],
        input_output_aliases={3: 0},
    )(*_in_hbm(d_cum, proj, b_pad, dproj))


FOX_TILE = 256
FOX_TILES = SEQ // FOX_TILE


def _row_to_col(row):
    n = row.shape[1]
    eye = lax.broadcasted_iota(jnp.int32, (n, n), 0) == lax.broadcasted_iota(jnp.int32, (n, n), 1)
    return jnp.sum(jnp.where(eye, row, 0.0), axis=1, keepdims=True)


def _fox_bias(f_row, i):
    t = FOX_TILE
    ext = (i + 1) * t
    bias = _row_to_col(f_row[:, i * t:(i + 1) * t]) - f_row[:, :ext]
    row = lax.broadcasted_iota(jnp.int32, (t, ext), 0) + i * t
    col = lax.broadcasted_iota(jnp.int32, (t, ext), 1)
    return bias, col <= row


def _fox_specs():
    qkv = [pl.BlockSpec((SEQ, LANES), lambda p, base=base: (0, base // LANES + p)) for base in (COL_QB, COL_KB, COL_VB)]
    return qkv, pl.BlockSpec((F_ROWS, SEQ), lambda p: (0, 0))


def _fox_fwd(proj, f_rows):
    t = FOX_TILE

    def body(q_ref, k_ref, v_ref, f_ref, o_ref, lse_ref):
        pair = pl.program_id(0)
        upper = _upper_half()
        k16, v16 = k_ref[...].astype(BF16), v_ref[...].astype(BF16)
        f_row = [f_ref[pl.ds(2 * pair + e, 1), :] for e in range(2)]
        for i in range(FOX_TILES):
            ext = (i + 1) * t
            q_tile = (q_ref[i * t:(i + 1) * t, :] * QK_SCALE).astype(BF16)
            s2 = _dot_nt(_stack_heads(q_tile), k16[:ext])
            pns, lses = [], []
            for e in range(2):
                bias, causal = _fox_bias(f_row[e], i)
                s = jnp.where(causal, s2[e * t:(e + 1) * t] + bias, NEG_INF)
                m = jnp.max(s, axis=-1, keepdims=True)
                p = jnp.exp(s - m)
                den = jnp.sum(p, axis=-1, keepdims=True)
                pns.append((p * (1.0 / den)).astype(BF16))
                lses.append(m + jnp.log(den))
            o_ref[i * t:(i + 1) * t, :] = _unstack_heads(_dot_nn(jnp.concatenate(pns, axis=0), v16[:ext]))
            lse_ref[i * t:(i + 1) * t, :] = jnp.where(upper, lses[1], lses[0])

    qkv, f_spec = _fox_specs()
    tok = pl.BlockSpec((SEQ, LANES), lambda p: (0, p))
    return pl.pallas_call(
        body, name="fox_attn_fwd", grid=(FOX_WIDTH // LANES,),
        in_specs=qkv + [f_spec], out_specs=[tok, tok],
        out_shape=[pltpu.HBM((SEQ, FOX_WIDTH), F32)] * 2,
        compiler_params=pltpu.CompilerParams(
            dimension_semantics=("parallel",), vmem_limit_bytes=_vmem_limit(8 * t * SEQ * 4)),
    )(*_in_hbm(proj, proj, proj, f_rows))


def _fox_bwd(proj, do, lse, f_rows, dproj):
    t = FOX_TILE

    def body(q_ref, k_ref, v_ref, f_ref, do_ref, lse_ref, dproj_in, dproj_ref, df_ref, dk_acc, dv_acc,
             dq_out, dk_out, dv_out, sem):
        pair = pl.program_id(0)
        upper = _upper_half()
        k16, v16 = k_ref[...].astype(BF16), v_ref[...].astype(BF16)
        f_row = [f_ref[pl.ds(2 * pair + e, 1), :] for e in range(2)]
        dk_acc[...] = jnp.zeros_like(dk_acc)
        dv_acc[...] = jnp.zeros_like(dv_acc)
        df_ref[...] = jnp.zeros_like(df_ref)
        for i in range(FOX_TILES):
            ext = (i + 1) * t
            q_tile = (q_ref[i * t:(i + 1) * t, :] * QK_SCALE).astype(BF16)
            do_tile = do_ref[i * t:(i + 1) * t, :]
            lse_t = lse_ref[i * t:(i + 1) * t, :]
            q2, do2 = _stack_heads(q_tile), _stack_heads(do_tile)
            s2, dp2 = _dot_nt(q2, k16[:ext]), _dot_nt(do2, v16[:ext])
            ps, dss = [], []
            for e in range(2):
                bias, causal = _fox_bias(f_row[e], i)
                s = s2[e * t:(e + 1) * t] + bias
                p = jnp.where(causal, jnp.exp(s - lse_t[:, e * HEAD_DIM:e * HEAD_DIM + 1]), 0.0)
                dp = dp2[e * t:(e + 1) * t]
                ds = p * (dp - jnp.sum(p * dp, axis=-1, keepdims=True))
                df_ref[0, e:e + 1, :ext] -= jnp.sum(ds, axis=0, keepdims=True)
                ps.append(p.astype(BF16))
                dss.append(ds.astype(BF16))
            ds2, p2 = jnp.concatenate(dss, axis=0), jnp.concatenate(ps, axis=0)
            dq_out[i * t:(i + 1) * t, :] = (_unstack_heads(_dot_nn(ds2, k16[:ext])) * QK_SCALE).astype(BF16)
            dk_acc[:ext, :] += _dot_tn(ds2, q2)
            dv_acc[:ext, :] += _dot_tn(p2, do2)
        dk_out[...] = dk_acc[...].astype(BF16)
        dv_out[...] = dv_acc[...].astype(BF16)
        _store_columns((dq_out, dk_out, dv_out), dproj_ref, [base // LANES + pair for base in (COL_QB, COL_KB, COL_VB)],
                       sem)

    qkv, f_spec = _fox_specs()
    tok = pl.BlockSpec((SEQ, LANES), lambda p: (0, p))
    return pl.pallas_call(
        body, name="fox_attn_bwd", grid=(FOX_WIDTH // LANES,),
        in_specs=qkv + [f_spec, tok, tok, _ANY],
        out_specs=[_ANY, pl.BlockSpec((1, SUBLANES, SEQ), lambda p: (p, 0, 0))],
        out_shape=[pltpu.HBM(dproj.shape, dproj.dtype),
                   pltpu.HBM((FOX_WIDTH // LANES, SUBLANES, SEQ), F32)],
        scratch_shapes=[pltpu.VMEM((SEQ, LANES), F32)] * 2 + [pltpu.VMEM((SEQ, LANES), BF16)] * 3
        + [pltpu.SemaphoreType.DMA((3,))],
        input_output_aliases={6: 0},
        compiler_params=pltpu.CompilerParams(
            dimension_semantics=("arbitrary",), vmem_limit_bytes=_vmem_limit(10 * t * SEQ * 4)),
    )(*_in_hbm(proj, proj, proj, f_rows, do, lse, dproj))


MIX_TILE = 256


def _mix_out(out_a, out_b, proj, x, wt_pa, wt_pb, w_out, g_post, g_ffn_pre):
    tm = MIX_TILE

    def body(a_ref, b_ref, ga_ref, gb_ref, x_ref, wpa_ref, wpb_ref, wo_ref, g2_ref, g3_ref,
             merged_ref, mix_ref, x1_ref, h2_ref):
        ya = _dot_nn(a_ref[...].astype(BF16), wpa_ref[...])
        yb = _dot_nn(b_ref[...].astype(BF16), wpb_ref[...])
        merged = (_sigmoid(ga_ref[...]) * ya + _sigmoid(gb_ref[...]) * yb).astype(BF16)
        merged_ref[...] = merged
        mix = _dot_nn(merged, wo_ref[...])
        mix_ref[...] = mix
        x1 = x_ref[...] + mix * _rms_scale(mix) * g2_ref[...]
        x1_ref[...] = x1
        h2_ref[...] = (x1 * _rms_scale(x1) * g3_ref[...]).astype(BF16)

    def rows(w, cb=0, buffers=2):
        return pl.BlockSpec((tm, w), lambda i, cb=cb: (i, cb), pipeline_mode=pl.Buffered(buffers))

    def whole(a):
        return pl.BlockSpec(a.shape, lambda i: (0, 0))

    d = D_MODEL
    blk = _nbytes((tm, d), F32) * 6 + sum(_nbytes(a.shape, BF16) for a in (wt_pa, wt_pb, w_out))
    def pipelined(*refs):
        pltpu.emit_pipeline(
            body, grid=(SEQ // tm,),
            in_specs=[rows(DIL_OUT_WIDTH, 0, 3), rows(FOX_WIDTH, 0, 3), rows(d, COL_GA // d, 3), rows(d, COL_GB // d, 3),
                      rows(d, 0, 3), whole(wt_pa), whole(wt_pb), whole(w_out), whole(g_post), whole(g_ffn_pre)],
            out_specs=[rows(d)] * 4,
        )(*refs)

    return pl.pallas_call(
        pipelined, name="mix_out", in_specs=[_ANY] * 10, out_specs=[_ANY] * 4,
        out_shape=[pltpu.HBM((SEQ, d), dt) for dt in (BF16, F32, F32, BF16)],
        compiler_params=pltpu.CompilerParams(vmem_limit_bytes=_vmem_limit(2 * blk)),
    )(*_in_hbm(out_a, out_b, proj, proj, x, wt_pa, wt_pb, w_out, g_post, g_ffn_pre))


def _mix_out_bwd(dmix, out_a, out_b, proj, wt_pa, wt_pb, w_out, deps=()):
    tm = MIX_TILE

    def body(dm_ref, a_ref, b_ref, ga_ref, gb_ref, wpa_ref, wpb_ref, wo_ref, *rest):
        dproj_ref, dya_ref, dyb_ref, da_ref, db_ref = rest[len(deps):]
        dmerged = _dot_nt(dm_ref[...], wo_ref[...])
        ya = _dot_nn(a_ref[...].astype(BF16), wpa_ref[...])
        yb = _dot_nn(b_ref[...].astype(BF16), wpb_ref[...])
        sa, sb = _sigmoid(ga_ref[...]), _sigmoid(gb_ref[...])
        dproj_ref[:, COL_GA:COL_GA + D_MODEL] = (dmerged * ya * (sa * (1.0 - sa))).astype(BF16)
        dproj_ref[:, COL_GB:COL_GB + D_MODEL] = (dmerged * yb * (sb * (1.0 - sb))).astype(BF16)
        dproj_ref[:, COL_GB + D_MODEL:] = jnp.zeros((tm, COL_QA - COL_GB - D_MODEL), BF16)
        dya = (dmerged * sa).astype(BF16)
        dyb = (dmerged * sb).astype(BF16)
        dya_ref[...] = dya
        dyb_ref[...] = dyb
        da_ref[...] = _dot_nt(dya, wpa_ref[...])
        db_ref[...] = _dot_nt(dyb, wpb_ref[...]).astype(BF16)

    def rows(w, cb=0, buffers=2):
        return pl.BlockSpec((tm, w), lambda i, cb=cb: (i, cb), pipeline_mode=pl.Buffered(buffers))

    def whole(a):
        return pl.BlockSpec(a.shape, lambda i: (0, 0))

    d = D_MODEL
    blk = _nbytes((tm, d), F32) * 8 + sum(_nbytes(a.shape, BF16) for a in (wt_pa, wt_pb, w_out))
    return pl.pallas_call(
        body, name="mix_out_bwd", grid=(SEQ // tm,),
        in_specs=[rows(d), rows(DIL_OUT_WIDTH), rows(FOX_WIDTH), rows(d, COL_GA // d), rows(d, COL_GB // d),
                  whole(wt_pa), whole(wt_pb), whole(w_out)] + [_ANY] * len(deps),
        out_specs=[rows(COL_QA)] + [rows(d)] * 2 + [rows(DIL_OUT_WIDTH), rows(FOX_WIDTH)],
        out_shape=[pltpu.HBM((SEQ, PROJ_COLS), BF16)] + [pltpu.HBM((SEQ, d), BF16)] * 2
        + [pltpu.HBM((SEQ, DIL_OUT_WIDTH), F32), pltpu.HBM((SEQ, FOX_WIDTH), BF16)],
        compiler_params=pltpu.CompilerParams(dimension_semantics=("parallel",), vmem_limit_bytes=_vmem_limit(blk)),
    )(*_in_hbm(dmix, out_a, out_b, proj, proj, wt_pa, wt_pb, w_out), *deps)


FFN_TM, FFN_TN = 2048, 256


def _ffn_up(h2, wt_gate, wt_up):
    tm, tn = FFN_TM, FFN_TN

    def body(h_ref, wg_ref, wu_ref, gate_ref, up_ref, act_ref):
        for rows in (slice(0, tm // 2), slice(tm // 2, tm)):
            gate = _dot_nt(h_ref[rows, :], wg_ref[...])
            up = _dot_nt(h_ref[rows, :], wu_ref[...])
            gate_ref[rows, :] = gate
            up_ref[rows, :] = up
            act_ref[rows, :] = (gate * _sigmoid(gate) * up).astype(BF16)

    tile = pl.BlockSpec((tm, tn), lambda i, j: (i, j))
    w_spec = pl.BlockSpec((tn, D_MODEL), lambda i, j: (j, 0))
    return pl.pallas_call(
        body, name="ffn_up", grid=(SEQ // tm, D_FF // tn),
        in_specs=[pl.BlockSpec((tm, D_MODEL), lambda i, j: (i, 0)), w_spec, w_spec],
        out_specs=[tile, tile, tile],
        out_shape=[pltpu.HBM((SEQ, D_FF), dt) for dt in (F32, F32, BF16)],
        compiler_params=pltpu.CompilerParams(
            dimension_semantics=("parallel", "parallel"), vmem_limit_bytes=_vmem_limit(8 * 2**20)),
    )(h2, wt_gate, wt_up)


def _ffn_act_bwd(dff, w_down, gate, up):
    tm, tn = FFN_TM, FFN_TN

    def body(d_ref, wd_ref, gate_ref, up_ref, dgate_ref, dup_ref):
        for rows in (slice(0, tm // 2), slice(tm // 2, tm)):
            dact = _dot_nt(d_ref[rows, :], wd_ref[...])
            gate = gate_ref[rows, :]
            sg = _sigmoid(gate)
            dgate_ref[rows, :] = (dact * up_ref[rows, :] * (sg * (1.0 + gate * (1.0 - sg)))).astype(BF16)
            dup_ref[rows, :] = (dact * (gate * sg)).astype(BF16)

    tile = pl.BlockSpec((tm, tn), lambda i, j: (i, j))
    return pl.pallas_call(
        body, name="ffn_act_bwd", grid=(SEQ // tm, D_FF // tn),
        in_specs=[pl.BlockSpec((tm, D_MODEL), lambda i, j: (i, 0)), pl.BlockSpec((tn, D_MODEL), lambda i, j: (j, 0)),
                  tile, tile],
        out_specs=[tile, tile],
        out_shape=[pltpu.HBM((SEQ, D_FF), BF16)] * 2,
        compiler_params=pltpu.CompilerParams(
            dimension_semantics=("parallel", "parallel"), vmem_limit_bytes=_vmem_limit(8 * 2**20)),
    )(dff, w_down, gate, up)


EPILOGUE_TM = 512


def _loss_head(act, w_down, x1, target, g_post):
    def fn(ff, x1, tgt, g):
        r = _rms_scale(ff)
        nrm = ff * r
        err = (x1 + nrm * g) - tgt
        loss = 0.5 * jnp.sum(jnp.mean(err * err, axis=-1, keepdims=True), axis=0, keepdims=True)
        dy = err * (1.0 / D_MODEL)
        u = dy * g
        dff = r * u - ff * (r * r * r) * jnp.mean(u * ff, axis=-1, keepdims=True)
        return dy, dff, jnp.broadcast_to(loss, (1, LANES)), jnp.sum(dy * nrm, axis=0, keepdims=True)

    d = D_MODEL
    return _matmul_rowwise([(act, w_down)], fn, "ffn_down_loss", EPILOGUE_TM, [(x1, d, 0), (target, d, 0)], [g_post],
                           [(d, F32), (d, BF16)], [LANES, d])


def _post_ffn_bwd(dgate, wt_gate, dup, wt_up, x1, dy, mix, g_ffn_pre, g_mix_post, deps=()):
    def fn(dh2, x1, dy, mix, g3, g2):
        dx, dg3 = _rms_bwd(x1, dh2, g3)
        dx1 = dy + dx
        dmix, dg2 = _rms_bwd(mix, dx1, g2)
        return dx1, dmix, dg3, dg2

    d = D_MODEL
    return _matmul_rowwise([(dgate, wt_gate), (dup, wt_up)], fn, "ffn_up_bwd", EPILOGUE_TM,
                           [(x1, d, 0), (dy, d, 0), (mix, d, 0)], [g_ffn_pre, g_mix_post],
                           [(d, F32), (d, BF16)], [d, d], deps=deps)


def _input_bwd(dproj, wt_r, x, dx1, g_pre, deps=()):
    def fn(dh, x, dx1, g):
        dx, dg = _rms_bwd(x, dh, g)
        return dx1 + dx, dg

    d = D_MODEL
    return _matmul_rowwise([(dproj, wt_r)], fn, "in_proj_bwd", EPILOGUE_TM, [(x, d, 0), (dx1, d, 0)], [g_pre],
                           [(d, F32)], [d], deps=deps)


def _adam_math(w, g, m, v):
    m = ADAM_B1 * m + (1.0 - ADAM_B1) * g
    v = ADAM_B2 * v + (1.0 - ADAM_B2) * (g * g)
    m_hat = m / (1.0 - ADAM_B1 ** ADAM_STEP)
    v_hat = v / (1.0 - ADAM_B2 ** ADAM_STEP)
    delta = -ADAM_LR * (m_hat / (jnp.sqrt(v_hat) + ADAM_EPS) + ADAM_WD * w)
    return delta, m, v


def _adam(w, mine, recv, m, v, name):
    r, c = w.shape
    tc = _col_tile(r, c)

    def body(w_ref, p_ref, r_ref, m_ref, v_ref, g_ref, d_ref, nm_ref, nv_ref):
        g = ((p_ref[...] + r_ref[0].astype(F32)) + r_ref[1].astype(F32)) + r_ref[2].astype(F32)
        g_ref[...] = g
        d_ref[...], nm_ref[...], nv_ref[...] = _adam_math(w_ref[...], g, m_ref[...], v_ref[...])

    spec = pl.BlockSpec((r, tc), lambda j: (0, j))
    return pl.pallas_call(
        body, name=name, grid=(c // tc,),
        in_specs=[spec, spec, pl.BlockSpec((3, r, tc), lambda j: (0, 0, j)), spec, spec], out_specs=[spec] * 4,
        out_shape=[pltpu.HBM((r, c), F32)] * 4,
        compiler_params=pltpu.CompilerParams(dimension_semantics=("parallel",)),
    )(*_in_hbm(w, mine, recv, m, v))


def _adam_small(gathered, ws, ms, vs, loss_parts):
    n = len(ws)

    def body(*refs):
        outs = refs[4 * n + 1:]
        loss = refs[4 * n][0]
        for dev in range(1, N_DEV):
            loss = loss + refs[4 * n][dev]
        outs[4 * n][...] = loss
        for i in range(n):
            ga_ref, w_ref, m_ref, v_ref = (refs[j * n + i] for j in range(4))
            g = ga_ref[0]
            for dev in range(1, N_DEV):
                g = g + ga_ref[dev]
            g = g[:, :w_ref.shape[1]]
            outs[4 * i][...] = g
            outs[4 * i + 1][...], outs[4 * i + 2][...], outs[4 * i + 3][...] = _adam_math(
                w_ref[...], g, m_ref[...], v_ref[...])

    out_shape = [pltpu.HBM(w.shape, F32) for w in ws for _ in range(4)]
    out_shape.append(pltpu.HBM((1, LANES), F32))
    out = pl.pallas_call(body, name="adam_small", out_shape=out_shape)(*gathered, *ws, *ms, *vs, loss_parts)
    return [out[4 * i:4 * i + 4] for i in range(n)], out[4 * n]


_PROJ_SEGMENTS = ((3848, 5896), (None, COL_QA - 2 * D_MODEL), (0, 3840), (3840, 3848), (None, PROJ_COLS - COL_F - 8))


def _proj_weight_t(gathered):
    pieces, zeros, at = [], [], 0
    for lo, hi in _PROJ_SEGMENTS:
        if lo is None:
            zeros.append((at, hi))
            at += hi
            continue
        for dev in range(lo // IN_SHARD, (hi - 1) // IN_SHARD + 1):
            a, b = max(lo, dev * IN_SHARD), min(hi, (dev + 1) * IN_SHARD)
            pieces.append((dev, a - dev * IN_SHARD, at + a - lo, b - a))
        at += hi - lo
    assert at == PROJ_COLS
    tc = 2 * LANES

    def body(g_ref, o_ref, shards, rows):
        for dev in range(N_DEV):
            shards[dev] = g_ref[dev].astype(F32)
        for dev, src, dst, n in pieces:
            rows[pl.ds(dst, n), :] = shards[dev, pl.ds(src, n), :]
        for dst, n in zeros:
            rows[pl.ds(dst, n), :] = jnp.zeros((n, tc), F32)
        o_ref[...] = rows[...].astype(o_ref.dtype)

    return pl.pallas_call(
        body, name="w_in_rows", grid=(D_MODEL // tc,),
        in_specs=[pl.BlockSpec((N_DEV, IN_SHARD, tc), lambda j: (0, 0, j))],
        out_specs=pl.BlockSpec((PROJ_COLS, tc), lambda j: (0, j)),
        out_shape=pltpu.HBM((PROJ_COLS, D_MODEL), gathered.dtype),
        scratch_shapes=[pltpu.VMEM((N_DEV, IN_SHARD, tc), F32), pltpu.VMEM((PROJ_COLS, tc), F32)],
        compiler_params=pltpu.CompilerParams(
            dimension_semantics=("parallel",), vmem_limit_bytes=_vmem_limit(2 * _nbytes((PROJ_COLS, tc), F32))),
    )(*_in_hbm(gathered))


def _proj_weight_grad_slots(dwt_r):
    starts, at = [], 0
    for lo, hi in _PROJ_SEGMENTS:
        if lo is not None:
            starts.append((lo, hi, at))
        at += hi if lo is None else hi - lo
    pieces = []
    for dev in range(N_DEV):
        lo, end = dev * IN_SHARD, (dev + 1) * IN_SHARD
        for seg_lo, seg_hi, seg_at in sorted(starts):
            a, b = max(lo, seg_lo), min(end, seg_hi)
            if a < b:
                pieces.append((dev, a - lo, seg_at + a - seg_lo, b - a))

    def body(g_ref, o_ref):
        for dev, dst, src, rows in pieces:
            o_ref[dev, pl.ds(dst, rows), :] = g_ref[pl.ds(src, rows), :]

    tc = 2 * LANES
    return pl.pallas_call(
        body, name="grad_w_in_slots", grid=(D_MODEL // tc,),
        in_specs=[pl.BlockSpec((PROJ_COLS, tc), lambda j: (0, j))],
        out_specs=pl.BlockSpec((N_DEV, IN_SHARD, tc), lambda j: (0, 0, j)),
        out_shape=pltpu.HBM((N_DEV, IN_SHARD, D_MODEL), F32),
        compiler_params=pltpu.CompilerParams(
            dimension_semantics=("parallel",), vmem_limit_bytes=_vmem_limit(2 * _nbytes((PROJ_COLS, tc), F32))),
    )(*_in_hbm(dwt_r))


def kernel(x, w_in, w_proj_a, w_proj_b, w_out, b_forget, w_ffn_gate, w_ffn_up, w_ffn_down, norm_mix_pre, norm_mix_post, norm_ffn_pre, norm_ffn_post, loss_target, m_w_in, m_w_proj_a, m_w_proj_b, m_w_out, m_b_forget, m_w_ffn_gate, m_w_ffn_up, m_w_ffn_down, m_norm_mix_pre, m_norm_mix_post, m_norm_ffn_pre, m_norm_ffn_post, v_w_in, v_w_proj_a, v_w_proj_b, v_w_out, v_b_forget, v_w_ffn_gate, v_w_ffn_up, v_w_ffn_down, v_norm_mix_pre, v_norm_mix_post, v_norm_ffn_pre, v_norm_ffn_post):
    d = D_MODEL
    names = ("w_in", "w_proj_a", "w_proj_b", "w_out", "w_ffn_gate", "w_ffn_up", "w_ffn_down")
    col_sharded = ("w_in", "w_ffn_gate", "w_ffn_up")

    def row_shards(arrs):
        return {k: (a[0].T if k in col_sharded else a[0]) for k, a in zip(names, arrs)}

    shards = row_shards((w_in, w_proj_a, w_proj_b, w_out, w_ffn_gate, w_ffn_up, w_ffn_down))
    moments_m = row_shards((m_w_in, m_w_proj_a, m_w_proj_b, m_w_out, m_w_ffn_gate, m_w_ffn_up, m_w_ffn_down))
    moments_v = row_shards((v_w_in, v_w_proj_a, v_w_proj_b, v_w_out, v_w_ffn_gate, v_w_ffn_up, v_w_ffn_down))
    pos = jnp.stack([lax.axis_index("c"), 2 * lax.axis_index("x") + lax.axis_index("y")]).astype(jnp.int32)
    x2, target = x[0], loss_target[0]

    me = 4 * lax.axis_index("x") + 2 * lax.axis_index("y") + lax.axis_index("c")
    mid_names, ffn_names = names[1:4], names[4:]
    first_names, later_names = names[:1], names[1:]
    shards16 = {k: shards[k].astype(BF16) for k in names}

    def landing(k):
        return lax.dynamic_update_slice(lax.empty((N_DEV,) + shards[k].shape, BF16), shards16[k][None], (me, 0, 0))

    ag_first = _exchange_start("ag_first_chips_start", _gather_two_route_copies, [shards16[k] for k in first_names],
                               [landing(k) for k in first_names], 4 * len(first_names))
    h = _rowwise(lambda xb, g: xb * _rms_scale(xb) * g, "norm_mix_pre", SEQ, 512, [(x2, d, 0)], [norm_mix_pre],
                 [(d, BF16)], deps=[ag_first.token])[0]
    later_lands = [landing(k) for k in later_names]
    _, ag_first_diagonal = _gather_relay(ag_first, "ag_first", [h, shards["w_in"], moments_m["w_in"], moments_v["w_in"],
                                                               *later_lands, *[shards16[k] for k in later_names]])
    relayed, ag_first_last = ag_first_diagonal([])
    ag_later = _exchange_start("ag_later_chips_start", _gather_two_route_copies, [shards16[k] for k in later_names],
                               later_lands, 4 * len(later_names), after=[relayed])
    gathered = dict(zip(first_names, ag_first_last([ag_later.token])))
    wt_r = _proj_weight_t(gathered["w_in"])

    proj = _matmul([(h, *_in_hbm(wt_r))], "nt", F32, "in_proj", 1024, 896, 1024)
    tables = _rope_tables()
    relayed, ag_later_diagonal = _gather_relay(ag_later, "ag_later", [proj])
    o_dil, lse_dil = _dil_fwd(proj, tables, deps=[relayed])
    out_a = _dil_combine(o_dil, lse_dil)
    relayed, ag_later_last = ag_later_diagonal([out_a])

    b_pad = jnp.pad(b_forget, ((0, 0), (0, LANES - N_FOX_HEADS)))
    f_rows = _fox_gate(proj, b_pad, deps=[relayed])
    out_b, lse_fox = _fox_fwd(proj, f_rows)

    gathered = dict(zip(later_names, ag_later_last([out_b])))
    wt_pa = gathered["w_proj_a"].transpose(1, 0, 2).reshape(DIL_OUT_WIDTH, d)
    wt_pb = gathered["w_proj_b"].transpose(1, 0, 2).reshape(FOX_WIDTH, d)
    w_o = gathered["w_out"].reshape(d, d)
    wt_g = gathered["w_ffn_gate"].reshape(D_FF, d)
    wt_u = gathered["w_ffn_up"].reshape(D_FF, d)
    w_d = gathered["w_ffn_down"].reshape(D_FF, d)
    merged, mix, x1, h2 = _mix_out(out_a, out_b, proj, x2, wt_pa, wt_pb, w_o, norm_mix_post, norm_ffn_pre)

    gate, up, act = _ffn_up(h2, wt_g, wt_u)
    dy, dff, loss_part, dg_ffn_post = _loss_head(act, w_d, x1, target, norm_ffn_post)

    dgate, dup = _ffn_act_bwd(dff, w_d, gate, up)
    grads_t = {}
    grads_t["w_ffn_down"] = _matmul([(act, dff)], "tn", F32, "grad_w_ffn_down", 1408, 512, 2048, staged=False)
    grads_t["w_ffn_gate"] = _matmul([(dgate, h2)], "tn", F32, "grad_w_ffn_gate", 1408, 512, 2048)
    grads_t["w_ffn_up"] = _matmul([(dup, h2)], "tn", F32, "grad_w_ffn_up", 1408, 512, 2048)
    rs_ffn = _ReduceScatter("ffn", {k: grads_t[k] for k in ffn_names}, pos)
    dx1, dmix, dg_ffn_pre, dg_mix_post = _post_ffn_bwd(dgate, wt_g, dup, wt_u, x1, dy, mix, norm_ffn_pre, norm_mix_post,
                                                       deps=[rs_ffn.token])
    rs_ffn.start_chips([dmix])

    dproj, dya, dyb, d_out_a, d_out_b = _mix_out_bwd(dmix, out_a, out_b, proj, wt_pa, wt_pb, w_o, deps=[rs_ffn.token])
    grads_t["w_out"] = _matmul([(merged, dmix)], "tn", F32, "grad_w_out", 1024, 1024, 1024)
    def column_slots(g):
        return g.reshape(g.shape[0], N_DEV, LANES).transpose(1, 0, 2)

    grads_t["w_proj_a"] = column_slots(_matmul([(out_a, dya)], "tn", F32, "grad_w_proj_a", DIL_OUT_WIDTH, 1024, SEQ))
    grads_t["w_proj_b"] = column_slots(_matmul([(out_b, dyb)], "tn", F32, "grad_w_proj_b", FOX_WIDTH, 1024, SEQ))
    rs_mid = _ReduceScatter("mid", {k: grads_t[k] for k in mid_names}, pos)

    do_dil, c_dil = _dil_combine_bwd(d_out_a, o_dil, lse_dil, deps=[rs_mid.token])
    rs_mid.start_chips([c_dil])
    dproj, d_cum = _fox_bwd(proj, d_out_b, lse_fox, f_rows, dproj)
    d_cum_rows = jnp.pad(d_cum[:, :2].reshape(N_FOX_HEADS, SEQ), ((0, F_ROWS - N_FOX_HEADS), (0, 0)))
    dproj, db_part = _fox_gate_bwd(d_cum_rows, proj, b_pad, dproj)
    dproj = _dil_bwd(proj, tables, do_dil, lse_dil, c_dil, dproj, deps=[rs_mid.token])

    dwt_r = _matmul([(dproj, h)], "tn", F32, "grad_w_in", 896, 1024, 2048)
    rs_in = _ReduceScatter("in", {"w_in": _proj_weight_grad_slots(dwt_r)}, pos)
    def finish(rs, after):
        return {k: _adam(shards[k], mine, recv, moments_m[k], moments_v[k], "adam_" + k)
                for k, (mine, recv) in rs.finish(after).items()}

    done = finish(rs_ffn, [rs_in.token])
    rs_in.start_chips([done[k][0] for k in ffn_names])
    grad_x, dg_mix_pre = _input_bwd(dproj, wt_r, x2, dx1, norm_mix_pre, deps=[rs_in.token])
    done.update(finish(rs_mid, [grad_x]))

    small_all = _all_gather([dg_mix_pre, dg_mix_post, dg_ffn_pre, dg_ffn_post, db_part, loss_part],
                            "small_grads_all_gather", deps=[done[k][0] for k in mid_names])
    small, loss = _adam_small(small_all[:5], [norm_mix_pre, norm_mix_post, norm_ffn_pre, norm_ffn_post, b_forget],
                              [m_norm_mix_pre, m_norm_mix_post, m_norm_ffn_pre, m_norm_ffn_post, m_b_forget],
                              [v_norm_mix_pre, v_norm_mix_post, v_norm_ffn_pre, v_norm_ffn_post, v_b_forget],
                              small_all[5])

    done.update(finish(rs_in, [small[0][0]]))

    def leaves(i):
        def nat(k):
            a = done[k][i]
            return (a.T if k in col_sharded else a)[None]

        return [nat("w_in"), nat("w_proj_a"), nat("w_proj_b"), nat("w_out"), small[4][i],
                nat("w_ffn_gate"), nat("w_ffn_up"), nat("w_ffn_down"), *[small[r][i] for r in range(4)]]

    return (loss[0, 0], grad_x[None], *leaves(0), *leaves(1), *leaves(2), *leaves(3))
```

```python
import functools
import math

import jax
import jax.numpy as jnp
import numpy as np
from jax import lax
from jax.experimental import pallas as pl
from jax.experimental.pallas import tpu as pltpu

F32 = jnp.float32
BF16 = jnp.bfloat16
MESH = pl.DeviceIdType.MESH

D_MODEL = 1024
SEQ = 2048
HEAD_DIM = 64
BLOCK = 128
N_BLOCKS = SEQ // BLOCK
DILATIONS = (1, 4, 16)
N_FOX_HEADS = 8
DIL_WIDTH = 768
DIL_OUT_WIDTH = 256
FOX_WIDTH = 512
D_FF = 2816
ROPE_THETA = 500000.0
ROPE_DIM = HEAD_DIM // 4
ROPE_HALF = ROPE_DIM // 2
EPS = 1e-6
NEG_INF = -1e30
QK_SCALE = 1.0 / math.sqrt(HEAD_DIM)
IN_COLS = 5896
N_DEV = 8
IN_SHARD = IN_COLS // N_DEV

ADAM_LR = 0.001
ADAM_B1 = 0.9
ADAM_B2 = 0.999
ADAM_EPS = 1e-08
ADAM_WD = 0.01
ADAM_STEP = 10

V7X_VMEM_BYTES = 64 * 2**20
LANES = 128
SUBLANES = 8

PROJ_COLS = 6272
COL_GA, COL_GB = 0, 1024
COL_QA, COL_KA, COL_VA = 2304, 3072, 3840
COL_QB, COL_KB, COL_VB = 4608, 5120, 5632
COL_F = 6144
F_ROWS = 16


def _vmem_limit(block_bytes):
    want = 2 * block_bytes + 16 * 2**20
    return int(min(max(want, 32 * 2**20), V7X_VMEM_BYTES - 8 * 2**20))


def _nbytes(shape, dtype):
    return math.prod(shape) * jnp.dtype(dtype).itemsize


def _in_hbm(*arrays):
    return [pltpu.with_memory_space_constraint(a, pltpu.HBM) for a in arrays]


def _dot(a, b, dims):
    return lax.dot_general(a, b, (dims, ((), ())), preferred_element_type=F32)


def _dot_nn(a, b):
    return _dot(a, b, ((1,), (0,)))


def _dot_nt(a, b):
    return _dot(a, b, ((1,), (1,)))


def _dot_tn(a, b):
    return _dot(a, b, ((0,), (0,)))


def _sigmoid(z):
    return 1.0 / (1.0 + jnp.exp(-z))


def _split3(x):
    hi = x.astype(BF16)
    r1 = x - hi.astype(F32)
    mid = r1.astype(BF16)
    lo = (r1 - mid.astype(F32)).astype(BF16)
    return hi, mid, lo


def _dot3_nn(x, ones_matrix):
    hi, mid, lo = _split3(x)
    return (_dot_nn(hi, ones_matrix) + _dot_nn(mid, ones_matrix)) + _dot_nn(lo, ones_matrix)


def _rowwise(fn, name, n_rows, tm, row_ins, bcast_ins, row_outs, acc_outs=(), deps=()):
    n_in = len(row_ins) + len(bcast_ins)
    n_ro = len(row_outs)

    def body(*refs):
        res = fn(*[r[...] for r in refs[:n_in]])
        if not isinstance(res, (tuple, list)):
            res = (res,)
        outs = refs[n_in + len(deps):]
        for r, o in zip(res[:n_ro], outs[:n_ro]):
            o[...] = r.astype(o.dtype)
        first = pl.program_id(0) == 0
        for r, o in zip(res[n_ro:], outs[n_ro:]):
            _accumulate(o, r, first)

    in_specs = [pl.BlockSpec((tm, w), lambda i, cb=cb: (i, cb)) for _, w, cb in row_ins]
    in_specs += [pl.BlockSpec(a.shape, lambda i: (0, 0)) for a in bcast_ins]
    in_specs += [pl.BlockSpec(memory_space=pl.ANY)] * len(deps)
    out_specs = [pl.BlockSpec((tm, w), lambda i: (i, 0)) for w, _ in row_outs]
    out_specs += [pl.BlockSpec((1, w), lambda i: (0, 0)) for w in acc_outs]
    out_shape = [pltpu.HBM((n_rows, w), dt) for w, dt in row_outs]
    out_shape += [pltpu.HBM((1, w), F32) for w in acc_outs]
    blk = sum(_nbytes((tm, w), a.dtype) for a, w, _ in row_ins) + sum(_nbytes((tm, w), dt) for w, dt in row_outs)
    return pl.pallas_call(
        body, name=name, grid=(n_rows // tm,), in_specs=in_specs, out_specs=out_specs, out_shape=out_shape,
        compiler_params=pltpu.CompilerParams(
            dimension_semantics=("arbitrary" if acc_outs else "parallel",), vmem_limit_bytes=_vmem_limit(3 * blk)),
    )(*_in_hbm(*[a for a, _, _ in row_ins], *bcast_ins), *deps)


def _accumulate(o_ref, part, first):
    @pl.when(first)
    def _():
        o_ref[...] = part

    @pl.when(jnp.logical_not(first))
    def _():
        o_ref[...] += part


_MM_DIMS = {"nn": ((1,), (0,)), "nt": ((1,), (1,)), "tn": ((0,), (0,))}


def _matmul(pairs, mode, out_dtype, name, tm, tn, tk, deps=(), staged=True, column_slots=False):
    a0, b0 = pairs[0]
    if mode == "tn":
        kk, m = a0.shape
    else:
        m, kk = a0.shape
    n = b0.shape[0] if mode == "nt" else b0.shape[1]
    assert m % tm == 0 and n % tn == 0 and kk % tk == 0, (name, m, n, kk)
    nk = kk // tk
    n_pairs = len(pairs)
    dims = _MM_DIMS[mode]
    n_in = 2 * n_pairs + len(deps)

    def body(*refs):
        o_ref = refs[n_in]
        part = None
        for p in range(n_pairs):
            d = _dot(refs[2 * p][...].astype(BF16), refs[2 * p + 1][...].astype(BF16), dims)
            part = d if part is None else part + d
        if nk == 1:
            o_ref[...] = part.astype(o_ref.dtype)
            return
        acc = refs[n_in + 1]
        k = pl.program_id(2)

        @pl.when(k == 0)
        def _():
            acc[...] = part

        @pl.when(k > 0)
        def _():
            acc[...] += part

        @pl.when(k == nk - 1)
        def _():
            o_ref[...] = acc[...].astype(o_ref.dtype)

    if mode == "tn":
        a_spec = pl.BlockSpec((tk, tm), lambda i, j, k: (k, i))
    else:
        a_spec = pl.BlockSpec((tm, tk), lambda i, j, k: (i, k))
    if mode == "nt":
        b_spec = pl.BlockSpec((tn, tk), lambda i, j, k: (j, k))
    else:
        b_spec = pl.BlockSpec((tk, tn), lambda i, j, k: (k, j))
    blk = sum(_nbytes((tm, tk), a.dtype) + _nbytes((tk, tn), b.dtype) for a, b in pairs) + 2 * _nbytes((tm, tn), F32)
    flat = [a for pair in pairs for a in pair]
    return pl.pallas_call(
        body, name=name, grid=(m // tm, n // tn, nk),
        in_specs=[a_spec, b_spec] * n_pairs + [pl.BlockSpec(memory_space=pl.ANY)] * len(deps),
        out_specs=(pl.BlockSpec((None, tm, tn), lambda i, j, k: (j, i, 0)) if column_slots
                   else pl.BlockSpec((tm, tn), lambda i, j, k: (i, j))),
        out_shape=pltpu.HBM((n // tn, m, tn) if column_slots else (m, n), out_dtype),
        scratch_shapes=[] if nk == 1 else [pltpu.VMEM((tm, tn), F32)],
        compiler_params=pltpu.CompilerParams(
            dimension_semantics=("parallel", "parallel", "arbitrary"), vmem_limit_bytes=_vmem_limit(blk)),
    )(*(flat if staged else _in_hbm(*flat)), *deps)


def _matmul_rowwise(pairs, fn, name, tm, row_ins, bcast_ins, row_outs, acc_outs=(), deps=()):
    m = pairs[0][0].shape[0]
    n_mm, n_in = 2 * len(pairs), len(row_ins) + len(bcast_ins)
    n_ro = len(row_outs)

    def body(*refs):
        prod = None
        for p in range(len(pairs)):
            part = _dot_nn(refs[2 * p][...].astype(BF16), refs[2 * p + 1][...].astype(BF16))
            prod = part if prod is None else prod + part
        res = fn(prod, *[r[...] for r in refs[n_mm:n_mm + n_in]])
        outs = refs[n_mm + n_in + len(deps):]
        for r, o in zip(res[:n_ro], outs[:n_ro]):
            o[...] = r.astype(o.dtype)
        first = pl.program_id(0) == 0
        for r, o in zip(res[n_ro:], outs[n_ro:]):
            _accumulate(o, r, first)

    in_specs = []
    for a, b in pairs:
        in_specs += [pl.BlockSpec((tm, a.shape[1]), lambda i: (i, 0)),
                     pl.BlockSpec(b.shape, lambda i: (0, 0), pipeline_mode=pl.Buffered(1))]
    in_specs += [pl.BlockSpec((tm, w), lambda i, cb=cb: (i, cb)) for _, w, cb in row_ins]
    in_specs += [pl.BlockSpec(a.shape, lambda i: (0, 0)) for a in bcast_ins]
    in_specs += [_ANY] * len(deps)
    out_specs = [pl.BlockSpec((tm, w), lambda i: (i, 0)) for w, _ in row_outs]
    out_specs += [pl.BlockSpec((1, w), lambda i: (0, 0)) for w in acc_outs]
    out_shape = [pltpu.HBM((m, w), dt) for w, dt in row_outs]
    out_shape += [pltpu.HBM((1, w), F32) for w in acc_outs]
    blk = sum(_nbytes((tm, a.shape[1]), a.dtype) + _nbytes(b.shape, b.dtype) // 2 for a, b in pairs)
    blk += sum(_nbytes((tm, w), a.dtype) for a, w, _ in row_ins) + sum(_nbytes((tm, w), dt) for w, dt in row_outs)
    return pl.pallas_call(
        body, name=name, grid=(m // tm,), in_specs=in_specs, out_specs=out_specs, out_shape=out_shape,
        compiler_params=pltpu.CompilerParams(dimension_semantics=("arbitrary",), vmem_limit_bytes=_vmem_limit(blk)),
    )(*[a for pair in pairs for a in pair], *[a for a, _, _ in row_ins], *bcast_ins, *deps)


def _rms_scale(x):
    return lax.rsqrt(jnp.mean(x * x, axis=-1, keepdims=True) + EPS)


def _rms_bwd(xin, dyn, g):
    r = _rms_scale(xin)
    u = dyn * g
    dx = r * u - xin * (r * r * r) * jnp.mean(u * xin, axis=-1, keepdims=True)
    dg = jnp.sum(dyn * xin * r, axis=0, keepdims=True)
    return dx, dg


def _mesh_pos():
    return lax.axis_index("x"), lax.axis_index("y"), lax.axis_index("c")


def _all_gather(xs, name, deps=()):
    n = len(xs)

    def body(*refs):
        x_refs, out_refs = refs[:n], refs[n + len(deps):2 * n + len(deps)]
        send_sems, recv_sems, local_sems = refs[2 * n + len(deps):]
        mx, my, mc = _mesh_pos()
        me, sib = (mx, my, mc), (mx, my, 1 - mc)
        chips = [(1 - mx, my), (mx, 1 - my), (1 - mx, 1 - my)]

        def slot(a, dev):
            px, py, pc = dev
            return out_refs[a].at[4 * px + 2 * py + pc]

        def copy(k, a, block, to, src=None):
            return pltpu.make_async_remote_copy(
                src_ref=slot(a, block) if src is None else src, dst_ref=slot(a, block),
                send_sem=send_sems.at[a * 7 + k], recv_sem=recv_sems.at[a * 7 + k],
                device_id=to, device_id_type=MESH)

        mine = [pltpu.make_async_copy(x_refs[a], slot(a, me), local_sems.at[a]) for a in range(n)]
        for cp in mine:
            cp.start()
        first = []
        for a in range(n):
            first.append(copy(0, a, me, sib, x_refs[a]))
            first += [copy(1 + j, a, me, (*chip, mc), x_refs[a]) for j, chip in enumerate(chips)]
        for cp in first:
            cp.start()
        passed = []
        for a in range(n):
            for j, chip in enumerate(chips):
                copy(1 + j, a, (*chip, mc), me).wait_recv()
                fwd = copy(4 + j, a, (*chip, mc), sib)
                fwd.start()
                passed.append(fwd)
        for a in range(n):
            copy(0, a, sib, me).wait_recv()
            for j, chip in enumerate(chips):
                copy(4 + j, a, (*chip, 1 - mc), me).wait_recv()
        for cp in first + passed:
            cp.wait_send()
        for cp in mine:
            cp.wait()

    hbm = pl.BlockSpec(memory_space=pl.ANY)
    return pl.pallas_call(
        body, name=name,
        out_shape=[pltpu.HBM((N_DEV,) + x.shape, x.dtype) for x in xs],
        in_specs=[hbm] * (n + len(deps)), out_specs=[hbm] * n,
        scratch_shapes=[pltpu.SemaphoreType.DMA((7 * n,)), pltpu.SemaphoreType.DMA((7 * n,)),
                        pltpu.SemaphoreType.DMA((n,))],
    )(*xs, *deps)


_HBM = pl.BlockSpec(memory_space=pltpu.HBM)
_SEM = pl.BlockSpec(memory_space=pltpu.SEMAPHORE)
_ANY = pl.BlockSpec(memory_space=pl.ANY)
_DATAFLOW = pltpu.SideEffectType.DATAFLOW_SIDE_EFFECTING


def _flip_peer(flip):
    mx, my, mc = _mesh_pos()
    return (1 - mx if flip & 2 else mx, 1 - my if flip & 1 else my, mc)


def _remote(src, dst, send_sems, recv_sems, k, peer):
    return pltpu.make_async_remote_copy(src_ref=src, dst_ref=dst, send_sem=send_sems.at[k], recv_sem=recv_sems.at[k],
                                        device_id=peer, device_id_type=MESH)


def _scatter_sibling_copies(srcs, lands, send_sems, recv_sems):
    mx, my, mc = _mesh_pos()
    return [_remote(srcs[a].at[k, 1 - mc], lands[a].at[k], send_sems, recv_sems, 4 * a + k, (mx, my, 1 - mc))
            for a in range(len(srcs)) for k in range(4)]


def _scatter_chips_copies(srcs, lands, send_sems, recv_sems):
    mx, my, _ = _mesh_pos()
    k0 = 2 * mx + my
    return [_remote(srcs[a].at[jnp.bitwise_xor(k0, flip)], lands[a].at[flip - 1], send_sems, recv_sems,
                    3 * a + flip - 1, _flip_peer(flip))
            for a in range(len(srcs)) for flip in (1, 2, 3)]


class _Exchange:
    def __init__(self, copies, n_src, send_sems, recv_sems, thru, token):
        self.copies, self.n_src, self.send_sems, self.recv_sems, self.thru, self.token = (
            copies, n_src, send_sems, recv_sems, thru, token)


def _exchange_start(name, copies, srcs, lands, n_copies, after=()):
    bufs = list(srcs) + list(lands)
    nb, ns = len(bufs), len(srcs)

    def body(*refs):
        send_sems, recv_sems = refs[nb + len(after)], refs[nb + len(after) + 1]
        for cp in copies(refs[:ns], refs[ns:nb], send_sems, recv_sems):
            cp.start()
        refs[-1][...] = jnp.zeros_like(refs[-1])

    out = pl.pallas_call(
        body, name=name,
        out_shape=(pltpu.SemaphoreType.DMA((n_copies,)), pltpu.SemaphoreType.DMA((n_copies,)),
                   *[pltpu.HBM(b.shape, b.dtype) for b in bufs], pltpu.HBM((SUBLANES, LANES), F32)),
        in_specs=[_HBM] * nb + [_ANY] * len(after),
        out_specs=(_SEM, _SEM, *[_HBM] * nb, pl.BlockSpec(memory_space=pltpu.VMEM)),
        input_output_aliases={i: 2 + i for i in range(nb)},
        compiler_params=pltpu.CompilerParams(has_side_effects=_DATAFLOW),
    )(*[pltpu.with_memory_space_constraint(b, pltpu.HBM) for b in bufs], *after)
    return _Exchange(copies, ns, out[0], out[1], list(out[2:2 + nb]), out[-1])


def _exchange_wait(name, ex, after):
    nb, ns = len(ex.thru), ex.n_src

    def body(*refs):
        for cp in ex.copies(refs[:ns], refs[ns:nb], refs[nb], refs[nb + 1]):
            cp.wait_send()
            cp.wait_recv()

    out = pl.pallas_call(
        body, name=name, out_shape=tuple(pltpu.HBM(b.shape, b.dtype) for b in ex.thru),
        in_specs=[_HBM] * nb + [_SEM, _SEM] + [_ANY] * len(after), out_specs=tuple([_HBM] * nb),
        input_output_aliases={i: i for i in range(nb)},
        compiler_params=pltpu.CompilerParams(has_side_effects=_DATAFLOW),
    )(*ex.thru, ex.send_sems, ex.recv_sems, *after)
    return list(out[:ns]), list(out[ns:])


def _halves(ref):
    half = ref.shape[1] // 2
    if half % LANES == 0:
        return ref.at[:, pl.ds(0, half)], ref.at[:, pl.ds(half, half)]
    half = ref.shape[0] // 2
    assert half % (2 * SUBLANES) == 0, ref.shape
    return ref.at[pl.ds(0, half)], ref.at[pl.ds(half, half)]


def _gather_two_route_copies(srcs, lands, send_sems, recv_sems):
    mx, my, mc = _mesh_pos()
    me = 4 * mx + 2 * my + mc
    return [_remote(_halves(srcs[a])[h], _halves(lands[a].at[me])[h], send_sems, recv_sems, 4 * a + i, _flip_peer(flip))
            for a in range(len(srcs)) for i, (flip, h) in enumerate(((2, 0), (1, 1), (2, 1), (1, 0)))]


def _to_sibling(land, flip, h, send_sems, recv_sems, k):
    mx, my, mc = _mesh_pos()
    part = _halves(land.at[2 * jnp.bitwise_xor(2 * mx + my, flip) + mc])[h]
    return _remote(part, part, send_sems, recv_sems, k, (mx, my, 1 - mc))


def _second_hop(land, send_sems, recv_sems, k):
    mx, my, mc = _mesh_pos()
    k0 = 2 * mx + my
    from_y = _halves(land.at[2 * jnp.bitwise_xor(k0, 1) + mc])[1]
    from_x = _halves(land.at[2 * jnp.bitwise_xor(k0, 2) + mc])[0]
    return (_remote(from_y, from_y, send_sems, recv_sems, k, _flip_peer(2)),
            _remote(from_x, from_x, send_sems, recv_sems, k + 1, _flip_peer(1)))


def _relay_call(name, body, bufs, sems, n_new, after):
    nb, n_in = len(bufs), len(bufs) + len(sems) + len(after)

    def call_body(*refs):
        body(refs[:nb], refs[nb:nb + len(sems)], refs[n_in], refs[n_in + 1])
        refs[-1][...] = jnp.zeros_like(refs[-1])

    out = pl.pallas_call(
        call_body, name=name,
        out_shape=(pltpu.SemaphoreType.DMA((n_new,)), pltpu.SemaphoreType.DMA((n_new,)),
                   *[pltpu.HBM(b.shape, b.dtype) for b in bufs], pltpu.HBM((SUBLANES, LANES), F32)),
        in_specs=[_HBM] * nb + [_SEM] * len(sems) + [_ANY] * len(after),
        out_specs=(_SEM, _SEM, *[_HBM] * nb, pl.BlockSpec(memory_space=pltpu.VMEM)),
        input_output_aliases={i: 2 + i for i in range(nb)},
        compiler_params=pltpu.CompilerParams(has_side_effects=_DATAFLOW),
    )(*bufs, *sems, *after)
    return out[0], out[1], list(out[2:2 + nb]), out[-1]


def _gather_relay(ex, tag, after_first):
    ns = ex.n_src
    n = len(ex.thru) - ns

    def first(bufs, sems, send, recv):
        lands, first_hop = bufs[ns:], ex.copies(bufs[:ns], bufs[ns:], *sems)
        for a in range(n):
            for h in (0, 1):
                _to_sibling(lands[a], 0, h, send, recv, 10 * a + h).start()
        for a in range(n):
            x_first, y_second, x_second, y_first = first_hop[4 * a:4 * a + 4]
            to_x, to_y = _second_hop(lands[a], send, recv, 10 * a + 8)
            x_first.wait_recv()
            to_y.start()
            _to_sibling(lands[a], 2, 0, send, recv, 10 * a + 4).start()
            y_second.wait_recv()
            to_x.start()
            _to_sibling(lands[a], 1, 1, send, recv, 10 * a + 3).start()
            x_second.wait_recv()
            _to_sibling(lands[a], 2, 1, send, recv, 10 * a + 5).start()
            y_first.wait_recv()
            _to_sibling(lands[a], 1, 0, send, recv, 10 * a + 2).start()
        for cp in first_hop:
            cp.wait_send()

    def second(lands, sems, send, recv):
        for a in range(n):
            to_x, to_y = _second_hop(lands[a], *sems, 10 * a + 8)
            to_y.wait_recv()
            _to_sibling(lands[a], 3, 0, send, recv, 2 * a).start()
            to_x.wait_recv()
            _to_sibling(lands[a], 3, 1, send, recv, 2 * a + 1).start()
            to_x.wait_send()
            to_y.wait_send()

    def last(lands, sems, send, recv):
        for a in range(n):
            for flip in range(4):
                for h in (0, 1):
                    s, r, k = (sems[2], sems[3], 2 * a + h) if flip == 3 else (sems[0], sems[1], 10 * a + 2 * flip + h)
                    cp = _to_sibling(lands[a], flip, h, s, r, k)
                    cp.wait_send()
                    cp.wait_recv()

    send1, recv1, bufs, token = _relay_call(f"{tag}_chips_relay", first, ex.thru, [ex.send_sems, ex.recv_sems], 10 * n,
                                            after_first)

    def run_second(after):
        send2, recv2, lands, token = _relay_call(f"{tag}_diagonal_relay", second, bufs[ns:], [send1, recv1], 2 * n, after)
        return token, lambda after_last: _relay_call(f"{tag}_sibling_wait", last, lands, [send1, recv1, send2, recv2],
                                                     1, after_last)[2]

    return token, run_second


def _col_tile(r, c, block_bytes=2**20):
    return next(t for t in (1024, 512, 256, 128) if c % t == 0 and (r * t * 4 <= block_bytes or t == 128))


def _add_sibling(g4, recv, pos, name):
    _, _, r, c = g4.shape
    tc = _col_tile(r, c, 2**22)

    def body(pos_ref, g_ref, r_ref, o16_ref, mine_ref):
        s = g_ref[0, 0] + r_ref[0]
        o16_ref[0] = s.astype(BF16)

        @pl.when(pl.program_id(1) == pos_ref[1])
        def _():
            mine_ref[...] = s

    slot = pl.BlockSpec((1, r, tc), lambda j, k, pos_ref: (k, 0, j))
    return pl.pallas_call(
        body, name=name,
        out_shape=[pltpu.HBM((4, r, c), BF16), pltpu.HBM((r, c), F32)],
        grid_spec=pltpu.PrefetchScalarGridSpec(
            num_scalar_prefetch=1, grid=(c // tc, 4),
            in_specs=[pl.BlockSpec((1, 1, r, tc), lambda j, k, pos_ref: (k, pos_ref[0], 0, j)), slot],
            out_specs=[slot, pl.BlockSpec((r, tc), lambda j, k, pos_ref: (0, j))]),
        compiler_params=pltpu.CompilerParams(
            dimension_semantics=("parallel", "arbitrary"), vmem_limit_bytes=_vmem_limit(4 * _nbytes((r, tc), F32))),
    )(pos, *_in_hbm(g4, recv))


class _ReduceScatter:
    def __init__(self, tag, grads_t, pos):
        self.tag, self.pos, self.names = tag, pos, list(grads_t)
        g4s = [g.reshape(4, 2, g.size // (N_DEV * g.shape[-1]), g.shape[-1]) for g in grads_t.values()]
        lands = [lax.empty((4,) + g.shape[2:], F32) for g in g4s]
        self.ex = _exchange_start(f"rs_{tag}_sibling_start", _scatter_sibling_copies, g4s, lands, 4 * len(g4s))
        self.token = self.ex.token

    def start_chips(self, after):
        g4s, from_sibling = _exchange_wait(f"rs_{self.tag}_sibling_wait", self.ex, after)
        parts = [_add_sibling(g4, rv, self.pos, f"rs_add_sibling_{k}")
                 for k, g4, rv in zip(self.names, g4s, from_sibling)]
        self.mine = [mine for _, mine in parts]
        p16s = [p16 for p16, _ in parts]
        lands = [lax.empty((3,) + p.shape[1:], BF16) for p in p16s]
        self.ex = _exchange_start(f"rs_{self.tag}_chips_start", _scatter_chips_copies, p16s, lands, 3 * len(p16s))
        self.token = self.ex.token

    def finish(self, after):
        _, from_chips = _exchange_wait(f"rs_{self.tag}_chips_wait", self.ex, after)
        return dict(zip(self.names, zip(self.mine, from_chips)))


def _rope_tables():
    positions = np.arange(SEQ, dtype=np.float32)
    inv_freq = np.power(np.float32(ROPE_THETA), -np.arange(0, ROPE_DIM, 2, dtype=np.float32) / np.float32(ROPE_DIM))
    ang = (positions[:, None] * inv_freq[None, :]).astype(np.float32)
    cos, sin = np.cos(ang).astype(np.float32), np.sin(ang).astype(np.float32)
    ones = np.ones((SEQ, HEAD_DIM - ROPE_DIM), np.float32)
    zeros8 = np.zeros((SEQ, ROPE_HALF), np.float32)
    zeros = np.zeros((SEQ, HEAD_DIM - ROPE_DIM), np.float32)
    c_head = np.concatenate([cos, cos, ones], axis=1)
    s1_head = np.concatenate([-sin, zeros8, zeros], axis=1)
    s2_head = np.concatenate([zeros8, sin, zeros], axis=1)
    return tuple(jnp.asarray(np.concatenate([t, t], axis=1)) for t in (c_head, s1_head, s2_head))


def _rope_apply(x, c, s1, s2):
    w = x.shape[1]
    return x * c + pltpu.roll(x, w - ROPE_HALF, 1) * s1 + pltpu.roll(x, ROPE_HALF, 1) * s2


def _rope_apply_t(dy, c, s1, s2):
    w = dy.shape[1]
    return dy * c + pltpu.roll(dy * s1, ROPE_HALF, 1) + pltpu.roll(dy * s2, w - ROPE_HALF, 1)


def _dil_prev_limit(has_prev):
    return jnp.where(has_prev, 0, BLOCK)


def _dil_valid(limit):
    row = lax.broadcasted_iota(jnp.int32, (BLOCK, 2 * BLOCK), 0)
    col = lax.broadcasted_iota(jnp.int32, (BLOCK, 2 * BLOCK), 1)
    dist = col - row
    return jnp.logical_and(dist >= jnp.where(col < BLOCK, limit, -BLOCK), dist <= BLOCK)


def _upper_half():
    return lax.broadcasted_iota(jnp.int32, (1, LANES), 1) >= HEAD_DIM


def _stack_heads(x):
    upper = _upper_half()
    return jnp.concatenate([jnp.where(upper, 0, x), jnp.where(upper, x, 0)], axis=0)


def _unstack_heads(y):
    n = y.shape[0] // 2
    return jnp.where(_upper_half(), y[n:], y[:n])


def _head_columns(t):
    return jnp.concatenate([t[:, 0:1], t[:, HEAD_DIM:HEAD_DIM + 1]], axis=0)


def _dil_rows(n, d):
    per = N_BLOCKS // d
    r, lb = n // per, n % per

    def rows(b):
        start = b * (BLOCK * d) + r
        return pl.ds(pl.multiple_of(start, BLOCK), BLOCK) if d == 1 else pl.ds(start, BLOCK, stride=d)

    return rows(lb), rows(jnp.maximum(lb - 1, 0)), lb > 0


def _dil_rotate(q_ref, k_ref, c_ref, s1_ref, s2_ref, q_rot, k_rot):
    tabs = (c_ref[...], s1_ref[...], s2_ref[...])
    q_rot[...] = _rope_apply(q_ref[...], *tabs) * QK_SCALE
    k_rot[...] = _rope_apply(k_ref[...], *tabs)


def _dil_specs():
    def col(base):
        return pl.BlockSpec((SEQ, LANES), lambda p: (0, base // LANES + p))

    table = pl.BlockSpec((SEQ, LANES), lambda p: (0, 0))
    return [col(COL_QA), col(COL_KA), col(COL_VA)], [table] * 3


def _store_columns(blocks, dproj_ref, cols, sem):
    copies = [pltpu.make_async_copy(b, dproj_ref.at[:, pl.ds(pl.multiple_of(c * LANES, LANES), LANES)], sem.at[i])
              for i, (b, c) in enumerate(zip(blocks, cols))]
    for cp in copies:
        cp.start()
    for cp in copies:
        cp.wait()


def _dil_window(d, n, k_rot, v_ref):
    rows, prev, has_prev = _dil_rows(n, d)
    kw, vw = k_rot[rows, :].astype(BF16), v_ref[rows, :].astype(BF16)
    if d == N_BLOCKS:
        row = lax.broadcasted_iota(jnp.int32, (BLOCK, BLOCK), 0)
        valid = lax.broadcasted_iota(jnp.int32, (BLOCK, BLOCK), 1) <= row
    else:
        kw = jnp.concatenate([k_rot[prev, :].astype(BF16), kw], axis=0)
        vw = jnp.concatenate([v_ref[prev, :].astype(BF16), vw], axis=0)
        valid = _dil_valid(_dil_prev_limit(has_prev))
    return rows, prev, kw, vw, jnp.concatenate([valid, valid], axis=0)


def _dil_fwd(proj, tables, deps=()):
    def body(q_ref, k_ref, v_ref, c_ref, s1_ref, s2_ref, *rest):
        o_ref, lse_ref, q_rot, k_rot = rest[len(deps):]
        upper = _upper_half()
        _dil_rotate(q_ref, k_ref, c_ref, s1_ref, s2_ref, q_rot, k_rot)

        def blocks_of(d):
            def block(n, carry):
                rows, _, kw, vw, valid = _dil_window(d, n, k_rot, v_ref)
                s = jnp.where(valid, _dot_nt(_stack_heads(q_rot[rows, :].astype(BF16)), kw), NEG_INF)
                m = jnp.max(s, axis=-1, keepdims=True)
                p = jnp.exp(s - m)
                den = jnp.sum(p, axis=-1, keepdims=True)
                o_ref[rows, :] = _unstack_heads(_dot_nn((p * (1.0 / den)).astype(BF16), vw))
                lse = m + jnp.log(den)
                lse_ref[rows, :] = jnp.where(upper, lse[BLOCK:], lse[:BLOCK])
                return carry

            lax.fori_loop(0, N_BLOCKS, block, 0, unroll=4)

        for g, d in enumerate(DILATIONS):
            pl.when(pl.program_id(0) // 2 == g)(functools.partial(blocks_of, d))

    qkv, tabs = _dil_specs()
    out = pl.BlockSpec((SEQ, LANES), lambda p: (0, p))
    return pl.pallas_call(
        body, name="dil_attn_fwd", grid=(DIL_WIDTH // LANES,), in_specs=qkv + tabs + [_ANY] * len(deps),
        out_specs=[out, out],
        out_shape=[pltpu.HBM((SEQ, DIL_WIDTH), F32)] * 2,
        scratch_shapes=[pltpu.VMEM((SEQ, LANES), F32)] * 2,
        compiler_params=pltpu.CompilerParams(dimension_semantics=("parallel",)),
    )(*_in_hbm(proj, proj, proj, *tables), *deps)


def _dil_bwd(proj, tables, do, lse, c, dproj, deps=()):
    def body(q_ref, k_ref, v_ref, c_ref, s1_ref, s2_ref, do_ref, lse_ref, cc_ref, dproj_in, *rest):
        dproj_ref, dq_acc, dk_acc, dv_acc, dq_out, dk_out, dv_out, q_rot, k_rot, sem = rest[len(deps):]
        dk_acc[...] = jnp.zeros_like(dk_acc)
        dv_acc[...] = jnp.zeros_like(dv_acc)
        _dil_rotate(q_ref, k_ref, c_ref, s1_ref, s2_ref, q_rot, k_rot)

        def blocks_of(d):
            def block(n, carry):
                rows, prev, kw, vw, valid = _dil_window(d, n, k_rot, v_ref)
                q2 = _stack_heads(q_rot[rows, :].astype(BF16))
                do2 = _stack_heads(do_ref[rows, :].astype(BF16))
                lse_col, c_col = _head_columns(lse_ref[rows, :]), _head_columns(cc_ref[rows, :])
                p = jnp.where(valid, jnp.exp(_dot_nt(q2, kw) - lse_col), 0.0)
                ds = (p * (_dot_nt(do2, vw) + c_col)).astype(BF16)
                dk, dv = _dot_tn(ds, q2), _dot_tn(p.astype(BF16), do2)
                dq_acc[rows, :] = _unstack_heads(_dot_nn(ds, kw)) * QK_SCALE
                if d == N_BLOCKS:
                    dk_acc[rows, :] += dk
                    dv_acc[rows, :] += dv
                else:
                    dk_acc[prev, :] += dk[:BLOCK]
                    dv_acc[prev, :] += dv[:BLOCK]
                    dk_acc[rows, :] += dk[BLOCK:]
                    dv_acc[rows, :] += dv[BLOCK:]
                return carry

            lax.fori_loop(0, N_BLOCKS, block, 0, unroll=4)

        pair = pl.program_id(0)
        for g, d in enumerate(DILATIONS):
            pl.when(pair // 2 == g)(functools.partial(blocks_of, d))
        tabs = (c_ref[...], s1_ref[...], s2_ref[...])
        dq_out[...] = _rope_apply_t(dq_acc[...], *tabs).astype(BF16)
        dk_out[...] = _rope_apply_t(dk_acc[...], *tabs).astype(BF16)
        dv_out[...] = dv_acc[...].astype(BF16)
        _store_columns((dq_out, dk_out, dv_out), dproj_ref,
                       [base // LANES + pair for base in (COL_QA, COL_KA, COL_VA)], sem)

    qkv, tabs = _dil_specs()
    tok = pl.BlockSpec((SEQ, LANES), lambda p: (0, p))
    return pl.pallas_call(
        body, name="dil_attn_bwd", grid=(DIL_WIDTH // LANES,),
        in_specs=qkv + tabs + [tok, tok, tok, _ANY] + [_ANY] * len(deps), out_specs=_ANY,
        out_shape=pltpu.HBM(dproj.shape, dproj.dtype),
        scratch_shapes=[pltpu.VMEM((SEQ, LANES), F32)] * 3 + [pltpu.VMEM((SEQ, LANES), BF16)] * 3
        + [pltpu.VMEM((SEQ, LANES), F32)] * 2 + [pltpu.SemaphoreType.DMA((3,))],
        input_output_aliases={9: 0},
        compiler_params=pltpu.CompilerParams(dimension_semantics=("arbitrary",)),
    )(*_in_hbm(proj, proj, proj, *tables, do, lse, c, dproj), *deps)


def _group_weights(l0, l1, l2):
    m = jnp.maximum(jnp.maximum(l0, l1), l2)
    e0, e1, e2 = jnp.exp(l0 - m), jnp.exp(l1 - m), jnp.exp(l2 - m)
    tot = e0 + e1 + e2
    return e0 / tot, e1 / tot, e2 / tot


def _dil_combine(o, lse, deps=()):
    def fn(o0, o1, o2, l0, l1, l2):
        w0, w1, w2 = _group_weights(l0, l1, l2)
        return w0 * o0 + w1 * o1 + w2 * o2

    w = DIL_OUT_WIDTH
    return _rowwise(fn, "dil_combine", SEQ, 512, [(o, w, g) for g in range(3)] + [(lse, w, g) for g in range(3)], [],
                    [(w, F32)], deps=deps)[0]


def _dil_combine_bwd(d_out, o, lse, deps=()):
    w = DIL_OUT_WIDTH

    def fn(d, o0, o1, o2, l0, l1, l2):
        row = lax.broadcasted_iota(jnp.int32, (w, w), 0) // HEAD_DIM
        col = lax.broadcasted_iota(jnp.int32, (w, w), 1) // HEAD_DIM
        same_head = jnp.where(row == col, 1.0, 0.0).astype(BF16)
        ws = _group_weights(l0, l1, l2)
        dws = [_dot3_nn(d * og, same_head) for og in (o0, o1, o2)]
        mean = ws[0] * dws[0] + ws[1] * dws[1] + ws[2] * dws[2]
        return jnp.concatenate([wg * d for wg in ws], axis=1), jnp.concatenate([-wg * mean for wg in ws], axis=1)

    return _rowwise(fn, "dil_combine_bwd", SEQ, 512,
                    [(d_out, w, 0)] + [(o, w, g) for g in range(3)] + [(lse, w, g) for g in range(3)], [],
                    [(DIL_WIDTH, F32)] * 2, deps=deps)


def _log1p(e):
    u = 1.0 + e
    return jnp.where(u == 1.0, e, jnp.log(u) * (e / (u - 1.0)))


def _fox_gate(proj, b_pad, deps=()):
    def body(f_ref, b_ref, *rest):
        o_ref = rest[-1]
        z = f_ref[...] + b_ref[...]
        logf = (jnp.minimum(z, 0.0) - _log1p(jnp.exp(-jnp.abs(z)))).T[:F_ROWS]
        row = lax.broadcasted_iota(jnp.int32, (BLOCK, BLOCK), 0)
        col = lax.broadcasted_iota(jnp.int32, (BLOCK, BLOCK), 1)
        before = jnp.where(row <= col, 1.0, 0.0).astype(BF16)
        carry = jnp.zeros((F_ROWS, 1), F32)
        for blk in range(N_BLOCKS):
            run = _dot3_nn(logf[:, blk * BLOCK:(blk + 1) * BLOCK], before) + carry
            o_ref[:, blk * BLOCK:(blk + 1) * BLOCK] = run
            carry = run[:, BLOCK - 1:BLOCK]

    return pl.pallas_call(
        body, name="fox_gate", grid=(1,),
        in_specs=[pl.BlockSpec((SEQ, LANES), lambda i: (0, COL_F // LANES)), pl.BlockSpec((1, LANES), lambda i: (0, 0))]
        + [_ANY] * len(deps),
        out_specs=pl.BlockSpec((F_ROWS, SEQ), lambda i: (0, 0)),
        out_shape=pltpu.HBM((F_ROWS, SEQ), F32),
    )(*_in_hbm(proj, b_pad), *deps)


def _fox_gate_bwd(d_cum, proj, b_pad, dproj):
    def body(d_ref, f_ref, b_ref, dproj_ref, dz_ref, db_ref):
        row = lax.broadcasted_iota(jnp.int32, (BLOCK, BLOCK), 0)
        col = lax.broadcasted_iota(jnp.int32, (BLOCK, BLOCK), 1)
        after = jnp.where(row >= col, 1.0, 0.0).astype(BF16)
        carry = jnp.zeros((F_ROWS, 1), F32)
        parts = [None] * N_BLOCKS
        for blk in reversed(range(N_BLOCKS)):
            run = _dot3_nn(d_ref[:, blk * BLOCK:(blk + 1) * BLOCK], after) + carry
            parts[blk] = run
            carry = run[:, 0:1]
        dlogf = jnp.concatenate(parts, axis=1)
        dlogf = jnp.concatenate([dlogf, jnp.zeros((LANES - F_ROWS, SEQ), F32)], axis=0).T
        dz = dlogf * _sigmoid(-(f_ref[...] + b_ref[...]))
        dz_ref[...] = dz.astype(BF16)
        db_ref[...] = jnp.sum(dz, axis=0, keepdims=True)

    f_cols = pl.BlockSpec((SEQ, LANES), lambda i: (0, COL_F // LANES))
    return pl.pallas_call(
        body, name="fox_gate_bwd", grid=(1,),
        in_specs=[pl.BlockSpec((F_ROWS, SEQ), lambda i: (0, 0)), f_cols, pl.BlockSpec((1, LANES), lambda i: (0, 0)), _ANY],
        out_specs=[f_cols, pl.BlockSpec((1, LANES), lambda i: (0, 0))],
        out_shape=[pltpu.HBM(dproj.shape, dproj.dtype), pltpu.HBM((1, LANES), F32)],
        input_output_aliases={3: 0},
    )(*_in_hbm(d_cum, proj, b_pad, dproj))


FOX_TILE = 256
FOX_TILES = SEQ // FOX_TILE


def _row_to_col(row):
    n = row.shape[1]
    eye = lax.broadcasted_iota(jnp.int32, (n, n), 0) == lax.broadcasted_iota(jnp.int32, (n, n), 1)
    return jnp.sum(jnp.where(eye, row, 0.0), axis=1, keepdims=True)


def _fox_bias(f_row, i):
    t = FOX_TILE
    ext = (i + 1) * t
    bias = _row_to_col(f_row[:, i * t:(i + 1) * t]) - f_row[:, :ext]
    row = lax.broadcasted_iota(jnp.int32, (t, ext), 0) + i * t
    col = lax.broadcasted_iota(jnp.int32, (t, ext), 1)
    return bias, col <= row


def _fox_specs():
    qkv = [pl.BlockSpec((SEQ, LANES), lambda p, base=base: (0, base // LANES + p)) for base in (COL_QB, COL_KB, COL_VB)]
    return qkv, pl.BlockSpec((F_ROWS, SEQ), lambda p: (0, 0))


def _fox_fwd(proj, f_rows):
    t = FOX_TILE

    def body(q_ref, k_ref, v_ref, f_ref, o_ref, lse_ref):
        pair = pl.program_id(0)
        upper = _upper_half()
        k16, v16 = k_ref[...].astype(BF16), v_ref[...].astype(BF16)
        f_row = [f_ref[pl.ds(2 * pair + e, 1), :] for e in range(2)]
        for i in range(FOX_TILES):
            ext = (i + 1) * t
            q_tile = (q_ref[i * t:(i + 1) * t, :] * QK_SCALE).astype(BF16)
            s2 = _dot_nt(_stack_heads(q_tile), k16[:ext])
            pns, lses = [], []
            for e in range(2):
                bias, causal = _fox_bias(f_row[e], i)
                s = jnp.where(causal, s2[e * t:(e + 1) * t] + bias, NEG_INF)
                m = jnp.max(s, axis=-1, keepdims=True)
                p = jnp.exp(s - m)
                den = jnp.sum(p, axis=-1, keepdims=True)
                pns.append((p * (1.0 / den)).astype(BF16))
                lses.append(m + jnp.log(den))
            o_ref[i * t:(i + 1) * t, :] = _unstack_heads(_dot_nn(jnp.concatenate(pns, axis=0), v16[:ext]))
            lse_ref[i * t:(i + 1) * t, :] = jnp.where(upper, lses[1], lses[0])

    qkv, f_spec = _fox_specs()
    tok = pl.BlockSpec((SEQ, LANES), lambda p: (0, p))
    return pl.pallas_call(
        body, name="fox_attn_fwd", grid=(FOX_WIDTH // LANES,),
        in_specs=qkv + [f_spec], out_specs=[tok, tok],
        out_shape=[pltpu.HBM((SEQ, FOX_WIDTH), F32)] * 2,
        compiler_params=pltpu.CompilerParams(
            dimension_semantics=("parallel",), vmem_limit_bytes=_vmem_limit(8 * t * SEQ * 4)),
    )(*_in_hbm(proj, proj, proj, f_rows))


def _fox_bwd(proj, do, lse, f_rows, dproj):
    t = FOX_TILE

    def body(q_ref, k_ref, v_ref, f_ref, do_ref, lse_ref, dproj_in, dproj_ref, df_ref, dk_acc, dv_acc,
             dq_out, dk_out, dv_out, sem):
        pair = pl.program_id(0)
        upper = _upper_half()
        k16, v16 = k_ref[...].astype(BF16), v_ref[...].astype(BF16)
        f_row = [f_ref[pl.ds(2 * pair + e, 1), :] for e in range(2)]
        dk_acc[...] = jnp.zeros_like(dk_acc)
        dv_acc[...] = jnp.zeros_like(dv_acc)
        df_ref[...] = jnp.zeros_like(df_ref)
        for i in range(FOX_TILES):
            ext = (i + 1) * t
            q_tile = (q_ref[i * t:(i + 1) * t, :] * QK_SCALE).astype(BF16)
            do_tile = do_ref[i * t:(i + 1) * t, :]
            lse_t = lse_ref[i * t:(i + 1) * t, :]
            q2, do2 = _stack_heads(q_tile), _stack_heads(do_tile)
            s2, dp2 = _dot_nt(q2, k16[:ext]), _dot_nt(do2, v16[:ext])
            ps, dss = [], []
            for e in range(2):
                bias, causal = _fox_bias(f_row[e], i)
                s = s2[e * t:(e + 1) * t] + bias
                p = jnp.where(causal, jnp.exp(s - lse_t[:, e * HEAD_DIM:e * HEAD_DIM + 1]), 0.0)
                dp = dp2[e * t:(e + 1) * t]
                ds = p * (dp - jnp.sum(p * dp, axis=-1, keepdims=True))
                df_ref[0, e:e + 1, :ext] -= jnp.sum(ds, axis=0, keepdims=True)
                ps.append(p.astype(BF16))
                dss.append(ds.astype(BF16))
            ds2, p2 = jnp.concatenate(dss, axis=0), jnp.concatenate(ps, axis=0)
            dq_out[i * t:(i + 1) * t, :] = (_unstack_heads(_dot_nn(ds2, k16[:ext])) * QK_SCALE).astype(BF16)
            dk_acc[:ext, :] += _dot_tn(ds2, q2)
            dv_acc[:ext, :] += _dot_tn(p2, do2)
        dk_out[...] = dk_acc[...].astype(BF16)
        dv_out[...] = dv_acc[...].astype(BF16)
        _store_columns((dq_out, dk_out, dv_out), dproj_ref, [base // LANES + pair for base in (COL_QB, COL_KB, COL_VB)],
                       sem)

    qkv, f_spec = _fox_specs()
    tok = pl.BlockSpec((SEQ, LANES), lambda p: (0, p))
    return pl.pallas_call(
        body, name="fox_attn_bwd", grid=(FOX_WIDTH // LANES,),
        in_specs=qkv + [f_spec, tok, tok, _ANY],
        out_specs=[_ANY, pl.BlockSpec((1, SUBLANES, SEQ), lambda p: (p, 0, 0))],
        out_shape=[pltpu.HBM(dproj.shape, dproj.dtype),
                   pltpu.HBM((FOX_WIDTH // LANES, SUBLANES, SEQ), F32)],
        scratch_shapes=[pltpu.VMEM((SEQ, LANES), F32)] * 2 + [pltpu.VMEM((SEQ, LANES), BF16)] * 3
        + [pltpu.SemaphoreType.DMA((3,))],
        input_output_aliases={6: 0},
        compiler_params=pltpu.CompilerParams(
            dimension_semantics=("arbitrary",), vmem_limit_bytes=_vmem_limit(10 * t * SEQ * 4)),
    )(*_in_hbm(proj, proj, proj, f_rows, do, lse, dproj))


MIX_TILE = 512


def _slot_columns(w_ref):
    return jnp.concatenate([w_ref[j] for j in range(N_DEV)], axis=1)


def _mix_out(out_a, out_b, proj, x, wt_pa, wt_pb, w_out, g_post, g_ffn_pre):
    tm = MIX_TILE

    def body(a_ref, b_ref, ga_ref, gb_ref, x_ref, wpa_ref, wpb_ref, wo_ref, g2_ref, g3_ref,
             merged_ref, mix_ref, x1_ref, h2_ref):
        ya = _dot_nn(a_ref[...].astype(BF16), _slot_columns(wpa_ref))
        yb = _dot_nn(b_ref[...].astype(BF16), _slot_columns(wpb_ref))
        merged = (_sigmoid(ga_ref[...]) * ya + _sigmoid(gb_ref[...]) * yb).astype(BF16)
        merged_ref[...] = merged
        mix = _dot_nn(merged, wo_ref[...])
        mix_ref[...] = mix
        x1 = x_ref[...] + mix * _rms_scale(mix) * g2_ref[...]
        x1_ref[...] = x1
        h2_ref[...] = (x1 * _rms_scale(x1) * g3_ref[...]).astype(BF16)

    def rows(w, cb=0):
        return pl.BlockSpec((tm, w), lambda i, cb=cb: (i, cb))

    def whole(a):
        return pl.BlockSpec(a.shape, lambda i: (0,) * a.ndim)

    d = D_MODEL
    blk = _nbytes((tm, d), F32) * 6 + sum(_nbytes(a.shape, BF16) for a in (wt_pa, wt_pb, w_out))
    return pl.pallas_call(
        body, name="mix_out", grid=(SEQ // tm,),
        in_specs=[rows(DIL_OUT_WIDTH), rows(FOX_WIDTH), rows(d, COL_GA // d), rows(d, COL_GB // d), rows(d),
                  whole(wt_pa), whole(wt_pb), whole(w_out), whole(g_post), whole(g_ffn_pre)],
        out_specs=[rows(d)] * 4,
        out_shape=[pltpu.HBM((SEQ, d), dt) for dt in (BF16, F32, F32, BF16)],
        compiler_params=pltpu.CompilerParams(dimension_semantics=("parallel",), vmem_limit_bytes=_vmem_limit(blk)),
    )(*_in_hbm(out_a, out_b, proj, proj, x, wt_pa, wt_pb, w_out, g_post, g_ffn_pre))


def _mix_out_bwd(dmix, out_a, out_b, proj, wt_pa, wt_pb, w_out, deps=()):
    tm = MIX_TILE

    def body(dm_ref, a_ref, b_ref, ga_ref, gb_ref, wpa_ref, wpb_ref, wo_ref, *rest):
        dproj_ref, dya_ref, dyb_ref, da_ref, db_ref = rest[len(deps):]
        dmerged = _dot_nt(dm_ref[...], wo_ref[...])
        wpa, wpb = _slot_columns(wpa_ref), _slot_columns(wpb_ref)
        ya = _dot_nn(a_ref[...].astype(BF16), wpa)
        yb = _dot_nn(b_ref[...].astype(BF16), wpb)
        sa, sb = _sigmoid(ga_ref[...]), _sigmoid(gb_ref[...])
        dproj_ref[:, COL_GA:COL_GA + D_MODEL] = (dmerged * ya * (sa * (1.0 - sa))).astype(BF16)
        dproj_ref[:, COL_GB:COL_GB + D_MODEL] = (dmerged * yb * (sb * (1.0 - sb))).astype(BF16)
        dproj_ref[:, COL_GB + D_MODEL:] = jnp.zeros((tm, COL_QA - COL_GB - D_MODEL), BF16)
        dya = (dmerged * sa).astype(BF16)
        dyb = (dmerged * sb).astype(BF16)
        dya_ref[...] = dya
        dyb_ref[...] = dyb
        da_ref[...] = _dot_nt(dya, wpa)
        db_ref[...] = _dot_nt(dyb, wpb).astype(BF16)

    def rows(w, cb=0):
        return pl.BlockSpec((tm, w), lambda i, cb=cb: (i, cb))

    def whole(a):
        return pl.BlockSpec(a.shape, lambda i: (0,) * a.ndim)

    d = D_MODEL
    blk = _nbytes((tm, d), F32) * 8 + sum(_nbytes(a.shape, BF16) for a in (wt_pa, wt_pb, w_out))
    return pl.pallas_call(
        body, name="mix_out_bwd", grid=(SEQ // tm,),
        in_specs=[rows(d), rows(DIL_OUT_WIDTH), rows(FOX_WIDTH), rows(d, COL_GA // d), rows(d, COL_GB // d),
                  whole(wt_pa), whole(wt_pb), whole(w_out)] + [_ANY] * len(deps),
        out_specs=[rows(COL_QA)] + [rows(d)] * 2 + [rows(DIL_OUT_WIDTH), rows(FOX_WIDTH)],
        out_shape=[pltpu.HBM((SEQ, PROJ_COLS), BF16)] + [pltpu.HBM((SEQ, d), BF16)] * 2
        + [pltpu.HBM((SEQ, DIL_OUT_WIDTH), F32), pltpu.HBM((SEQ, FOX_WIDTH), BF16)],
        compiler_params=pltpu.CompilerParams(dimension_semantics=("parallel",), vmem_limit_bytes=_vmem_limit(blk)),
    )(*_in_hbm(dmix, out_a, out_b, proj, proj, wt_pa, wt_pb, w_out), *deps)


FFN_TM, FFN_TN = 2048, 256


def _ffn_up(h2, wt_gate, wt_up):
    tm, tn = FFN_TM, FFN_TN

    def body(h_ref, wg_ref, wu_ref, gate_ref, up_ref, act_ref):
        for rows in (slice(0, tm // 2), slice(tm // 2, tm)):
            gate = _dot_nt(h_ref[rows, :], wg_ref[...])
            up = _dot_nt(h_ref[rows, :], wu_ref[...])
            gate_ref[rows, :] = gate
            up_ref[rows, :] = up
            act_ref[rows, :] = (gate * _sigmoid(gate) * up).astype(BF16)

    tile = pl.BlockSpec((tm, tn), lambda i, j: (i, j))
    w_spec = pl.BlockSpec((tn, D_MODEL), lambda i, j: (j, 0))
    return pl.pallas_call(
        body, name="ffn_up", grid=(SEQ // tm, D_FF // tn),
        in_specs=[pl.BlockSpec((tm, D_MODEL), lambda i, j: (i, 0)), w_spec, w_spec],
        out_specs=[tile, tile, tile],
        out_shape=[pltpu.HBM((SEQ, D_FF), dt) for dt in (F32, F32, BF16)],
        compiler_params=pltpu.CompilerParams(
            dimension_semantics=("parallel", "parallel"), vmem_limit_bytes=_vmem_limit(8 * 2**20)),
    )(h2, wt_gate, wt_up)


def _ffn_act_bwd(dff, w_down, gate, up):
    tm, tn = FFN_TM, FFN_TN

    def body(d_ref, wd_ref, gate_ref, up_ref, dgate_ref, dup_ref):
        for rows in (slice(0, tm // 2), slice(tm // 2, tm)):
            dact = _dot_nt(d_ref[rows, :], wd_ref[...])
            gate = gate_ref[rows, :]
            sg = _sigmoid(gate)
            dgate_ref[rows, :] = (dact * up_ref[rows, :] * (sg * (1.0 + gate * (1.0 - sg)))).astype(BF16)
            dup_ref[rows, :] = (dact * (gate * sg)).astype(BF16)

    tile = pl.BlockSpec((tm, tn), lambda i, j: (i, j))
    return pl.pallas_call(
        body, name="ffn_act_bwd", grid=(SEQ // tm, D_FF // tn),
        in_specs=[pl.BlockSpec((tm, D_MODEL), lambda i, j: (i, 0)), pl.BlockSpec((tn, D_MODEL), lambda i, j: (j, 0)),
                  tile, tile],
        out_specs=[tile, tile],
        out_shape=[pltpu.HBM((SEQ, D_FF), BF16)] * 2,
        compiler_params=pltpu.CompilerParams(
            dimension_semantics=("parallel", "parallel"), vmem_limit_bytes=_vmem_limit(8 * 2**20)),
    )(dff, w_down, gate, up)


EPILOGUE_TM = 512


def _loss_head(act, w_down, x1, target, g_post):
    def fn(ff, x1, tgt, g):
        r = _rms_scale(ff)
        nrm = ff * r
        err = (x1 + nrm * g) - tgt
        loss = 0.5 * jnp.sum(jnp.mean(err * err, axis=-1, keepdims=True), axis=0, keepdims=True)
        dy = err * (1.0 / D_MODEL)
        u = dy * g
        dff = r * u - ff * (r * r * r) * jnp.mean(u * ff, axis=-1, keepdims=True)
        return dy, dff, jnp.broadcast_to(loss, (1, LANES)), jnp.sum(dy * nrm, axis=0, keepdims=True)

    d = D_MODEL
    return _matmul_rowwise([(act, w_down)], fn, "ffn_down_loss", EPILOGUE_TM, [(x1, d, 0), (target, d, 0)], [g_post],
                           [(d, F32), (d, BF16)], [LANES, d])


def _post_ffn_bwd(dgate, wt_gate, dup, wt_up, x1, dy, mix, g_ffn_pre, g_mix_post, deps=()):
    def fn(dh2, x1, dy, mix, g3, g2):
        dx, dg3 = _rms_bwd(x1, dh2, g3)
        dx1 = dy + dx
        dmix, dg2 = _rms_bwd(mix, dx1, g2)
        return dx1, dmix, dg3, dg2

    d = D_MODEL
    return _matmul_rowwise([(dgate, wt_gate), (dup, wt_up)], fn, "ffn_up_bwd", EPILOGUE_TM,
                           [(x1, d, 0), (dy, d, 0), (mix, d, 0)], [g_ffn_pre, g_mix_post],
                           [(d, F32), (d, BF16)], [d, d], deps=deps)


def _input_bwd(dproj, wt_r, x, dx1, g_pre, deps=()):
    def fn(dh, x, dx1, g):
        dx, dg = _rms_bwd(x, dh, g)
        return dx1 + dx, dg

    d = D_MODEL
    return _matmul_rowwise([(dproj, wt_r)], fn, "in_proj_bwd", EPILOGUE_TM, [(x, d, 0), (dx1, d, 0)], [g_pre],
                           [(d, F32)], [d], deps=deps)


def _adam_math(w, g, m, v):
    m = ADAM_B1 * m + (1.0 - ADAM_B1) * g
    v = ADAM_B2 * v + (1.0 - ADAM_B2) * (g * g)
    m_hat = m / (1.0 - ADAM_B1 ** ADAM_STEP)
    v_hat = v / (1.0 - ADAM_B2 ** ADAM_STEP)
    delta = -ADAM_LR * (m_hat / (jnp.sqrt(v_hat) + ADAM_EPS) + ADAM_WD * w)
    return delta, m, v


def _adam(w, mine, recv, m, v, name):
    r, c = w.shape
    tc = _col_tile(r, c)

    def body(w_ref, p_ref, r_ref, m_ref, v_ref, g_ref, d_ref, nm_ref, nv_ref):
        g = ((p_ref[...] + r_ref[0].astype(F32)) + r_ref[1].astype(F32)) + r_ref[2].astype(F32)
        g_ref[...] = g
        d_ref[...], nm_ref[...], nv_ref[...] = _adam_math(w_ref[...], g, m_ref[...], v_ref[...])

    spec = pl.BlockSpec((r, tc), lambda j: (0, j))
    return pl.pallas_call(
        body, name=name, grid=(c // tc,),
        in_specs=[spec, spec, pl.BlockSpec((3, r, tc), lambda j: (0, 0, j)), spec, spec], out_specs=[spec] * 4,
        out_shape=[pltpu.HBM((r, c), F32)] * 4,
        compiler_params=pltpu.CompilerParams(dimension_semantics=("parallel",)),
    )(*_in_hbm(w, mine, recv, m, v))


def _adam_small(gathered, ws, ms, vs, loss_parts):
    n = len(ws)

    def body(*refs):
        outs = refs[4 * n + 1:]
        loss = refs[4 * n][0]
        for dev in range(1, N_DEV):
            loss = loss + refs[4 * n][dev]
        outs[4 * n][...] = loss
        for i in range(n):
            ga_ref, w_ref, m_ref, v_ref = (refs[j * n + i] for j in range(4))
            g = ga_ref[0]
            for dev in range(1, N_DEV):
                g = g + ga_ref[dev]
            g = g[:, :w_ref.shape[1]]
            outs[4 * i][...] = g
            outs[4 * i + 1][...], outs[4 * i + 2][...], outs[4 * i + 3][...] = _adam_math(
                w_ref[...], g, m_ref[...], v_ref[...])

    out_shape = [pltpu.HBM(w.shape, F32) for w in ws for _ in range(4)]
    out_shape.append(pltpu.HBM((1, LANES), F32))
    out = pl.pallas_call(body, name="adam_small", out_shape=out_shape)(*gathered, *ws, *ms, *vs, loss_parts)
    return [out[4 * i:4 * i + 4] for i in range(n)], out[4 * n]


_PROJ_SEGMENTS = ((3848, 5896), (None, COL_QA - 2 * D_MODEL), (0, 3840), (3840, 3848), (None, PROJ_COLS - COL_F - 8))


def _proj_weight_t(gathered):
    pieces, zeros, at = [], [], 0
    for lo, hi in _PROJ_SEGMENTS:
        if lo is None:
            zeros.append((at, hi))
            at += hi
            continue
        for dev in range(lo // IN_SHARD, (hi - 1) // IN_SHARD + 1):
            a, b = max(lo, dev * IN_SHARD), min(hi, (dev + 1) * IN_SHARD)
            pieces.append((dev, a - dev * IN_SHARD, at + a - lo, b - a))
        at += hi - lo
    assert at == PROJ_COLS
    tc = 2 * LANES

    def body(g_ref, o_ref, shards, rows):
        for dev in range(N_DEV):
            shards[dev] = g_ref[dev].astype(F32)
        for dev, src, dst, n in pieces:
            rows[pl.ds(dst, n), :] = shards[dev, pl.ds(src, n), :]
        for dst, n in zeros:
            rows[pl.ds(dst, n), :] = jnp.zeros((n, tc), F32)
        o_ref[...] = rows[...].astype(o_ref.dtype)

    return pl.pallas_call(
        body, name="w_in_rows", grid=(D_MODEL // tc,),
        in_specs=[pl.BlockSpec((N_DEV, IN_SHARD, tc), lambda j: (0, 0, j))],
        out_specs=pl.BlockSpec((PROJ_COLS, tc), lambda j: (0, j)),
        out_shape=pltpu.HBM((PROJ_COLS, D_MODEL), gathered.dtype),
        scratch_shapes=[pltpu.VMEM((N_DEV, IN_SHARD, tc), F32), pltpu.VMEM((PROJ_COLS, tc), F32)],
        compiler_params=pltpu.CompilerParams(
            dimension_semantics=("parallel",), vmem_limit_bytes=_vmem_limit(2 * _nbytes((PROJ_COLS, tc), F32))),
    )(*_in_hbm(gathered))


def _proj_weight_grad_slots(dwt_r):
    starts, at = [], 0
    for lo, hi in _PROJ_SEGMENTS:
        if lo is not None:
            starts.append((lo, hi, at))
        at += hi if lo is None else hi - lo
    pieces = []
    for dev in range(N_DEV):
        lo, end = dev * IN_SHARD, (dev + 1) * IN_SHARD
        for seg_lo, seg_hi, seg_at in sorted(starts):
            a, b = max(lo, seg_lo), min(end, seg_hi)
            if a < b:
                pieces.append((dev, a - lo, seg_at + a - seg_lo, b - a))

    def body(g_ref, o_ref):
        for dev, dst, src, rows in pieces:
            o_ref[dev, pl.ds(dst, rows), :] = g_ref[pl.ds(src, rows), :]

    tc = 2 * LANES
    return pl.pallas_call(
        body, name="grad_w_in_slots", grid=(D_MODEL // tc,),
        in_specs=[pl.BlockSpec((PROJ_COLS, tc), lambda j: (0, j))],
        out_specs=pl.BlockSpec((N_DEV, IN_SHARD, tc), lambda j: (0, 0, j)),
        out_shape=pltpu.HBM((N_DEV, IN_SHARD, D_MODEL), F32),
        compiler_params=pltpu.CompilerParams(
            dimension_semantics=("parallel",), vmem_limit_bytes=_vmem_limit(2 * _nbytes((PROJ_COLS, tc), F32))),
    )(*_in_hbm(dwt_r))


def kernel(x, w_in, w_proj_a, w_proj_b, w_out, b_forget, w_ffn_gate, w_ffn_up, w_ffn_down, norm_mix_pre, norm_mix_post, norm_ffn_pre, norm_ffn_post, loss_target, m_w_in, m_w_proj_a, m_w_proj_b, m_w_out, m_b_forget, m_w_ffn_gate, m_w_ffn_up, m_w_ffn_down, m_norm_mix_pre, m_norm_mix_post, m_norm_ffn_pre, m_norm_ffn_post, v_w_in, v_w_proj_a, v_w_proj_b, v_w_out, v_b_forget, v_w_ffn_gate, v_w_ffn_up, v_w_ffn_down, v_norm_mix_pre, v_norm_mix_post, v_norm_ffn_pre, v_norm_ffn_post):
    d = D_MODEL
    names = ("w_in", "w_proj_a", "w_proj_b", "w_out", "w_ffn_gate", "w_ffn_up", "w_ffn_down")
    col_sharded = ("w_in", "w_ffn_gate", "w_ffn_up")

    def row_shards(arrs):
        return {k: (a[0].T if k in col_sharded else a[0]) for k, a in zip(names, arrs)}

    shards = row_shards((w_in, w_proj_a, w_proj_b, w_out, w_ffn_gate, w_ffn_up, w_ffn_down))
    moments_m = row_shards((m_w_in, m_w_proj_a, m_w_proj_b, m_w_out, m_w_ffn_gate, m_w_ffn_up, m_w_ffn_down))
    moments_v = row_shards((v_w_in, v_w_proj_a, v_w_proj_b, v_w_out, v_w_ffn_gate, v_w_ffn_up, v_w_ffn_down))
    pos = jnp.stack([lax.axis_index("c"), 2 * lax.axis_index("x") + lax.axis_index("y")]).astype(jnp.int32)
    x2, target = x[0], loss_target[0]

    me = 4 * lax.axis_index("x") + 2 * lax.axis_index("y") + lax.axis_index("c")
    mid_names, ffn_names = names[1:4], names[4:]
    first_names, later_names = names[:1], names[1:]
    shards16 = {k: shards[k].astype(BF16) for k in names}

    def landing(k):
        return lax.dynamic_update_slice(lax.empty((N_DEV,) + shards[k].shape, BF16), shards16[k][None], (me, 0, 0))

    ag_first = _exchange_start("ag_first_chips_start", _gather_two_route_copies, [shards16[k] for k in first_names],
                               [landing(k) for k in first_names], 4 * len(first_names))
    h = _rowwise(lambda xb, g: xb * _rms_scale(xb) * g, "norm_mix_pre", SEQ, 512, [(x2, d, 0)], [norm_mix_pre],
                 [(d, BF16)], deps=[ag_first.token])[0]
    later_lands = [landing(k) for k in later_names]
    _, ag_first_diagonal = _gather_relay(ag_first, "ag_first", [h, shards["w_in"], moments_m["w_in"], moments_v["w_in"],
                                                               *later_lands, *[shards16[k] for k in later_names]])
    relayed, ag_first_last = ag_first_diagonal([])
    ag_later = _exchange_start("ag_later_chips_start", _gather_two_route_copies, [shards16[k] for k in later_names],
                               later_lands, 4 * len(later_names), after=[relayed])
    gathered = dict(zip(first_names, ag_first_last([ag_later.token])))
    wt_r = _proj_weight_t(gathered["w_in"])

    proj = _matmul([(h, *_in_hbm(wt_r))], "nt", F32, "in_proj", 1024, 896, 1024)
    tables = _rope_tables()
    relayed, ag_later_diagonal = _gather_relay(ag_later, "ag_later", [proj])
    o_dil, lse_dil = _dil_fwd(proj, tables, deps=[relayed])
    out_a = _dil_combine(o_dil, lse_dil)
    relayed, ag_later_last = ag_later_diagonal([out_a])

    b_pad = jnp.pad(b_forget, ((0, 0), (0, LANES - N_FOX_HEADS)))
    f_rows = _fox_gate(proj, b_pad, deps=[relayed])
    out_b, lse_fox = _fox_fwd(proj, f_rows)

    gathered = dict(zip(later_names, ag_later_last([out_b])))
    wt_pa, wt_pb = gathered["w_proj_a"], gathered["w_proj_b"]
    w_o = gathered["w_out"].reshape(d, d)
    wt_g = gathered["w_ffn_gate"].reshape(D_FF, d)
    wt_u = gathered["w_ffn_up"].reshape(D_FF, d)
    w_d = gathered["w_ffn_down"].reshape(D_FF, d)
    merged, mix, x1, h2 = _mix_out(out_a, out_b, proj, x2, wt_pa, wt_pb, w_o, norm_mix_post, norm_ffn_pre)

    gate, up, act = _ffn_up(h2, wt_g, wt_u)
    dy, dff, loss_part, dg_ffn_post = _loss_head(act, w_d, x1, target, norm_ffn_post)

    dgate, dup = _ffn_act_bwd(dff, w_d, gate, up)
    grads_t = {}
    grads_t["w_ffn_down"] = _matmul([(act, dff)], "tn", F32, "grad_w_ffn_down", 1408, 512, 2048, staged=False)
    grads_t["w_ffn_gate"] = _matmul([(dgate, h2)], "tn", F32, "grad_w_ffn_gate", 1408, 512, 2048)
    grads_t["w_ffn_up"] = _matmul([(dup, h2)], "tn", F32, "grad_w_ffn_up", 1408, 512, 2048)
    rs_ffn = _ReduceScatter("ffn", {k: grads_t[k] for k in ffn_names}, pos)
    dx1, dmix, dg_ffn_pre, dg_mix_post = _post_ffn_bwd(dgate, wt_g, dup, wt_u, x1, dy, mix, norm_ffn_pre, norm_mix_post,
                                                       deps=[rs_ffn.token])
    rs_ffn.start_chips([dmix])

    dproj, dya, dyb, d_out_a, d_out_b = _mix_out_bwd(dmix, out_a, out_b, proj, wt_pa, wt_pb, w_o, deps=[rs_ffn.token])
    grads_t["w_out"] = _matmul([(merged, dmix)], "tn", F32, "grad_w_out", 1024, 1024, 1024)
    grads_t["w_proj_a"] = _matmul([(out_a, dya)], "tn", F32, "grad_w_proj_a", DIL_OUT_WIDTH, LANES, SEQ,
                                  column_slots=True)
    grads_t["w_proj_b"] = _matmul([(out_b, dyb)], "tn", F32, "grad_w_proj_b", FOX_WIDTH, LANES, SEQ, column_slots=True)
    rs_mid = _ReduceScatter("mid", {k: grads_t[k] for k in mid_names}, pos)

    do_dil, c_dil = _dil_combine_bwd(d_out_a, o_dil, lse_dil, deps=[rs_mid.token])
    rs_mid.start_chips([c_dil])
    dproj, d_cum = _fox_bwd(proj, d_out_b, lse_fox, f_rows, dproj)
    d_cum_rows = jnp.pad(d_cum[:, :2].reshape(N_FOX_HEADS, SEQ), ((0, F_ROWS - N_FOX_HEADS), (0, 0)))
    dproj, db_part = _fox_gate_bwd(d_cum_rows, proj, b_pad, dproj)
    dproj = _dil_bwd(proj, tables, do_dil, lse_dil, c_dil, dproj, deps=[rs_mid.token])

    dwt_r = _matmul([(dproj, h)], "tn", F32, "grad_w_in", 896, 1024, 2048)
    rs_in = _ReduceScatter("in", {"w_in": _proj_weight_grad_slots(dwt_r)}, pos)
    def finish(rs, after):
        return {k: _adam(shards[k], mine, recv, moments_m[k], moments_v[k], "adam_" + k)
                for k, (mine, recv) in rs.finish(after).items()}

    done = finish(rs_ffn, [rs_in.token])
    rs_in.start_chips([done[k][0] for k in ffn_names])
    grad_x, dg_mix_pre = _input_bwd(dproj, wt_r, x2, dx1, norm_mix_pre, deps=[rs_in.token])
    done.update(finish(rs_mid, [grad_x]))

    small_all = _all_gather([dg_mix_pre, dg_mix_post, dg_ffn_pre, dg_ffn_post, db_part, loss_part],
                            "small_grads_all_gather", deps=[done[k][0] for k in mid_names])
    small, loss = _adam_small(small_all[:5], [norm_mix_pre, norm_mix_post, norm_ffn_pre, norm_ffn_post, b_forget],
                              [m_norm_mix_pre, m_norm_mix_post, m_norm_ffn_pre, m_norm_ffn_post, m_b_forget],
                              [v_norm_mix_pre, v_norm_mix_post, v_norm_ffn_pre, v_norm_ffn_post, v_b_forget],
                              small_all[5])

    done.update(finish(rs_in, [small[0][0]]))

    def leaves(i):
        def nat(k):
            a = done[k][i]
            return (a.T if k in col_sharded else a)[None]

        return [nat("w_in"), nat("w_proj_a"), nat("w_proj_b"), nat("w_out"), small[4][i],
                nat("w_ffn_gate"), nat("w_ffn_up"), nat("w_ffn_down"), *[small[r][i] for r in range(4)]]

    return (loss[0, 0], grad_x[None], *leaves(0), *leaves(1), *leaves(2), *leaves(3))
```

```python
import functools
import math

import jax
import jax.numpy as jnp
import numpy as np
from jax import lax
from jax.experimental import pallas as pl
from jax.experimental.pallas import tpu as pltpu

F32 = jnp.float32
BF16 = jnp.bfloat16
MESH = pl.DeviceIdType.MESH

D_MODEL = 1024
SEQ = 2048
HEAD_DIM = 64
BLOCK = 128
N_BLOCKS = SEQ // BLOCK
DILATIONS = (1, 4, 16)
N_FOX_HEADS = 8
DIL_WIDTH = 768
DIL_OUT_WIDTH = 256
FOX_WIDTH = 512
D_FF = 2816
ROPE_THETA = 500000.0
ROPE_DIM = HEAD_DIM // 4
ROPE_HALF = ROPE_DIM // 2
EPS = 1e-6
NEG_INF = -1e30
QK_SCALE = 1.0 / math.sqrt(HEAD_DIM)
IN_COLS = 5896
N_DEV = 8
IN_SHARD = IN_COLS // N_DEV

ADAM_LR = 0.001
ADAM_B1 = 0.9
ADAM_B2 = 0.999
ADAM_EPS = 1e-08
ADAM_WD = 0.01
ADAM_STEP = 10

V7X_VMEM_BYTES = 64 * 2**20
LANES = 128
SUBLANES = 8

PROJ_COLS = 6272
COL_GA, COL_GB = 0, 1024
COL_QA, COL_KA, COL_VA = 2304, 3072, 3840
COL_QB, COL_KB, COL_VB = 4608, 5120, 5632
COL_F = 6144
F_ROWS = 16


def _vmem_limit(block_bytes):
    want = 2 * block_bytes + 16 * 2**20
    return int(min(max(want, 32 * 2**20), V7X_VMEM_BYTES - 8 * 2**20))


def _nbytes(shape, dtype):
    return math.prod(shape) * jnp.dtype(dtype).itemsize


def _in_hbm(*arrays):
    return [pltpu.with_memory_space_constraint(a, pltpu.HBM) for a in arrays]


def _dot(a, b, dims):
    return lax.dot_general(a, b, (dims, ((), ())), preferred_element_type=F32)


def _dot_nn(a, b):
    return _dot(a, b, ((1,), (0,)))


def _dot_nt(a, b):
    return _dot(a, b, ((1,), (1,)))


def _dot_tn(a, b):
    return _dot(a, b, ((0,), (0,)))


def _sigmoid(z):
    return 1.0 / (1.0 + jnp.exp(-z))


def _split3(x):
    hi = x.astype(BF16)
    r1 = x - hi.astype(F32)
    mid = r1.astype(BF16)
    lo = (r1 - mid.astype(F32)).astype(BF16)
    return hi, mid, lo


def _dot3_nn(x, ones_matrix):
    hi, mid, lo = _split3(x)
    return (_dot_nn(hi, ones_matrix) + _dot_nn(mid, ones_matrix)) + _dot_nn(lo, ones_matrix)


def _rowwise(fn, name, n_rows, tm, row_ins, bcast_ins, row_outs, acc_outs=(), deps=()):
    n_in = len(row_ins) + len(bcast_ins)
    n_ro = len(row_outs)

    def body(*refs):
        res = fn(*[r[...] for r in refs[:n_in]])
        if not isinstance(res, (tuple, list)):
            res = (res,)
        outs = refs[n_in + len(deps):]
        for r, o in zip(res[:n_ro], outs[:n_ro]):
            o[...] = r.astype(o.dtype)
        first = pl.program_id(0) == 0
        for r, o in zip(res[n_ro:], outs[n_ro:]):
            _accumulate(o, r, first)

    in_specs = [pl.BlockSpec((tm, w), lambda i, cb=cb: (i, cb)) for _, w, cb in row_ins]
    in_specs += [pl.BlockSpec(a.shape, lambda i: (0, 0)) for a in bcast_ins]
    in_specs += [pl.BlockSpec(memory_space=pl.ANY)] * len(deps)
    out_specs = [pl.BlockSpec((tm, w), lambda i: (i, 0)) for w, _ in row_outs]
    out_specs += [pl.BlockSpec((1, w), lambda i: (0, 0)) for w in acc_outs]
    out_shape = [pltpu.HBM((n_rows, w), dt) for w, dt in row_outs]
    out_shape += [pltpu.HBM((1, w), F32) for w in acc_outs]
    blk = sum(_nbytes((tm, w), a.dtype) for a, w, _ in row_ins) + sum(_nbytes((tm, w), dt) for w, dt in row_outs)
    return pl.pallas_call(
        body, name=name, grid=(n_rows // tm,), in_specs=in_specs, out_specs=out_specs, out_shape=out_shape,
        compiler_params=pltpu.CompilerParams(
            dimension_semantics=("arbitrary" if acc_outs else "parallel",), vmem_limit_bytes=_vmem_limit(3 * blk)),
    )(*_in_hbm(*[a for a, _, _ in row_ins], *bcast_ins), *deps)


def _accumulate(o_ref, part, first):
    @pl.when(first)
    def _():
        o_ref[...] = part

    @pl.when(jnp.logical_not(first))
    def _():
        o_ref[...] += part


_MM_DIMS = {"nn": ((1,), (0,)), "nt": ((1,), (1,)), "tn": ((0,), (0,))}


def _matmul(pairs, mode, out_dtype, name, tm, tn, tk, deps=(), staged=True, column_slots=False):
    a0, b0 = pairs[0]
    if mode == "tn":
        kk, m = a0.shape
    else:
        m, kk = a0.shape
    n = b0.shape[0] if mode == "nt" else b0.shape[1]
    assert m % tm == 0 and n % tn == 0 and kk % tk == 0, (name, m, n, kk)
    nk = kk // tk
    n_pairs = len(pairs)
    dims = _MM_DIMS[mode]
    n_in = 2 * n_pairs + len(deps)

    def body(*refs):
        o_ref = refs[n_in]
        part = None
        for p in range(n_pairs):
            d = _dot(refs[2 * p][...].astype(BF16), refs[2 * p + 1][...].astype(BF16), dims)
            part = d if part is None else part + d
        if nk == 1:
            o_ref[...] = part.astype(o_ref.dtype)
            return
        acc = refs[n_in + 1]
        k = pl.program_id(2)

        @pl.when(k == 0)
        def _():
            acc[...] = part

        @pl.when(k > 0)
        def _():
            acc[...] += part

        @pl.when(k == nk - 1)
        def _():
            o_ref[...] = acc[...].astype(o_ref.dtype)

    if mode == "tn":
        a_spec = pl.BlockSpec((tk, tm), lambda i, j, k: (k, i))
    else:
        a_spec = pl.BlockSpec((tm, tk), lambda i, j, k: (i, k))
    if mode == "nt":
        b_spec = pl.BlockSpec((tn, tk), lambda i, j, k: (j, k))
    else:
        b_spec = pl.BlockSpec((tk, tn), lambda i, j, k: (k, j))
    blk = sum(_nbytes((tm, tk), a.dtype) + _nbytes((tk, tn), b.dtype) for a, b in pairs) + 2 * _nbytes((tm, tn), F32)
    flat = [a for pair in pairs for a in pair]
    return pl.pallas_call(
        body, name=name, grid=(m // tm, n // tn, nk),
        in_specs=[a_spec, b_spec] * n_pairs + [pl.BlockSpec(memory_space=pl.ANY)] * len(deps),
        out_specs=(pl.BlockSpec((None, tm, tn), lambda i, j, k: (j, i, 0)) if column_slots
                   else pl.BlockSpec((tm, tn), lambda i, j, k: (i, j))),
        out_shape=pltpu.HBM((n // tn, m, tn) if column_slots else (m, n), out_dtype),
        scratch_shapes=[] if nk == 1 else [pltpu.VMEM((tm, tn), F32)],
        compiler_params=pltpu.CompilerParams(
            dimension_semantics=("parallel", "parallel", "arbitrary"), vmem_limit_bytes=_vmem_limit(blk)),
    )(*(flat if staged else _in_hbm(*flat)), *deps)


def _matmul_rowwise(pairs, fn, name, tm, row_ins, bcast_ins, row_outs, acc_outs=(), deps=()):
    m = pairs[0][0].shape[0]
    n_mm, n_in = 2 * len(pairs), len(row_ins) + len(bcast_ins)
    n_ro = len(row_outs)

    def body(*refs):
        prod = None
        for p in range(len(pairs)):
            part = _dot_nn(refs[2 * p][...].astype(BF16), refs[2 * p + 1][...].astype(BF16))
            prod = part if prod is None else prod + part
        res = fn(prod, *[r[...] for r in refs[n_mm:n_mm + n_in]])
        outs = refs[n_mm + n_in + len(deps):]
        for r, o in zip(res[:n_ro], outs[:n_ro]):
            o[...] = r.astype(o.dtype)
        first = pl.program_id(0) == 0
        for r, o in zip(res[n_ro:], outs[n_ro:]):
            _accumulate(o, r, first)

    in_specs = []
    for a, b in pairs:
        in_specs += [pl.BlockSpec((tm, a.shape[1]), lambda i: (i, 0)),
                     pl.BlockSpec(b.shape, lambda i: (0, 0), pipeline_mode=pl.Buffered(1))]
    in_specs += [pl.BlockSpec((tm, w), lambda i, cb=cb: (i, cb)) for _, w, cb in row_ins]
    in_specs += [pl.BlockSpec(a.shape, lambda i: (0, 0)) for a in bcast_ins]
    in_specs += [_ANY] * len(deps)
    out_specs = [pl.BlockSpec((tm, w), lambda i: (i, 0)) for w, _ in row_outs]
    out_specs += [pl.BlockSpec((1, w), lambda i: (0, 0)) for w in acc_outs]
    out_shape = [pltpu.HBM((m, w), dt) for w, dt in row_outs]
    out_shape += [pltpu.HBM((1, w), F32) for w in acc_outs]
    blk = sum(_nbytes((tm, a.shape[1]), a.dtype) + _nbytes(b.shape, b.dtype) // 2 for a, b in pairs)
    blk += sum(_nbytes((tm, w), a.dtype) for a, w, _ in row_ins) + sum(_nbytes((tm, w), dt) for w, dt in row_outs)
    return pl.pallas_call(
        body, name=name, grid=(m // tm,), in_specs=in_specs, out_specs=out_specs, out_shape=out_shape,
        compiler_params=pltpu.CompilerParams(dimension_semantics=("arbitrary",), vmem_limit_bytes=_vmem_limit(blk)),
    )(*[a for pair in pairs for a in pair], *[a for a, _, _ in row_ins], *bcast_ins, *deps)


def _rms_scale(x):
    return lax.rsqrt(jnp.mean(x * x, axis=-1, keepdims=True) + EPS)


def _rms_bwd(xin, dyn, g):
    r = _rms_scale(xin)
    u = dyn * g
    dx = r * u - xin * (r * r * r) * jnp.mean(u * xin, axis=-1, keepdims=True)
    dg = jnp.sum(dyn * xin * r, axis=0, keepdims=True)
    return dx, dg


def _mesh_pos():
    return lax.axis_index("x"), lax.axis_index("y"), lax.axis_index("c")


def _all_gather(xs, name, deps=()):
    n = len(xs)

    def body(*refs):
        x_refs, out_refs = refs[:n], refs[n + len(deps):2 * n + len(deps)]
        send_sems, recv_sems, local_sems = refs[2 * n + len(deps):]
        mx, my, mc = _mesh_pos()
        me, sib = (mx, my, mc), (mx, my, 1 - mc)
        chips = [(1 - mx, my), (mx, 1 - my), (1 - mx, 1 - my)]

        def slot(a, dev):
            px, py, pc = dev
            return out_refs[a].at[4 * px + 2 * py + pc]

        def copy(k, a, block, to, src=None):
            return pltpu.make_async_remote_copy(
                src_ref=slot(a, block) if src is None else src, dst_ref=slot(a, block),
                send_sem=send_sems.at[a * 7 + k], recv_sem=recv_sems.at[a * 7 + k],
                device_id=to, device_id_type=MESH)

        mine = [pltpu.make_async_copy(x_refs[a], slot(a, me), local_sems.at[a]) for a in range(n)]
        for cp in mine:
            cp.start()
        first = []
        for a in range(n):
            first.append(copy(0, a, me, sib, x_refs[a]))
            first += [copy(1 + j, a, me, (*chip, mc), x_refs[a]) for j, chip in enumerate(chips)]
        for cp in first:
            cp.start()
        passed = []
        for a in range(n):
            for j, chip in enumerate(chips):
                copy(1 + j, a, (*chip, mc), me).wait_recv()
                fwd = copy(4 + j, a, (*chip, mc), sib)
                fwd.start()
                passed.append(fwd)
        for a in range(n):
            copy(0, a, sib, me).wait_recv()
            for j, chip in enumerate(chips):
                copy(4 + j, a, (*chip, 1 - mc), me).wait_recv()
        for cp in first + passed:
            cp.wait_send()
        for cp in mine:
            cp.wait()

    hbm = pl.BlockSpec(memory_space=pl.ANY)
    return pl.pallas_call(
        body, name=name,
        out_shape=[pltpu.HBM((N_DEV,) + x.shape, x.dtype) for x in xs],
        in_specs=[hbm] * (n + len(deps)), out_specs=[hbm] * n,
        scratch_shapes=[pltpu.SemaphoreType.DMA((7 * n,)), pltpu.SemaphoreType.DMA((7 * n,)),
                        pltpu.SemaphoreType.DMA((n,))],
    )(*xs, *deps)


_HBM = pl.BlockSpec(memory_space=pltpu.HBM)
_SEM = pl.BlockSpec(memory_space=pltpu.SEMAPHORE)
_ANY = pl.BlockSpec(memory_space=pl.ANY)
_DATAFLOW = pltpu.SideEffectType.DATAFLOW_SIDE_EFFECTING


def _flip_peer(flip):
    mx, my, mc = _mesh_pos()
    return (1 - mx if flip & 2 else mx, 1 - my if flip & 1 else my, mc)


def _remote(src, dst, send_sems, recv_sems, k, peer):
    return pltpu.make_async_remote_copy(src_ref=src, dst_ref=dst, send_sem=send_sems.at[k], recv_sem=recv_sems.at[k],
                                        device_id=peer, device_id_type=MESH)


def _scatter_sibling_copies(srcs, lands, send_sems, recv_sems):
    mx, my, mc = _mesh_pos()
    return [_remote(srcs[a].at[k, 1 - mc], lands[a].at[k], send_sems, recv_sems, 4 * a + k, (mx, my, 1 - mc))
            for a in range(len(srcs)) for k in range(4)]


def _scatter_chips_copies(srcs, lands, send_sems, recv_sems):
    mx, my, _ = _mesh_pos()
    k0 = 2 * mx + my
    return [_remote(srcs[a].at[jnp.bitwise_xor(k0, flip)], lands[a].at[flip - 1], send_sems, recv_sems,
                    3 * a + flip - 1, _flip_peer(flip))
            for a in range(len(srcs)) for flip in (1, 2, 3)]


class _Exchange:
    def __init__(self, copies, n_src, send_sems, recv_sems, thru, token):
        self.copies, self.n_src, self.send_sems, self.recv_sems, self.thru, self.token = (
            copies, n_src, send_sems, recv_sems, thru, token)


def _exchange_start(name, copies, srcs, lands, n_copies, after=()):
    bufs = list(srcs) + list(lands)
    nb, ns = len(bufs), len(srcs)

    def body(*refs):
        send_sems, recv_sems = refs[nb + len(after)], refs[nb + len(after) + 1]
        for cp in copies(refs[:ns], refs[ns:nb], send_sems, recv_sems):
            cp.start()
        refs[-1][...] = jnp.zeros_like(refs[-1])

    out = pl.pallas_call(
        body, name=name,
        out_shape=(pltpu.SemaphoreType.DMA((n_copies,)), pltpu.SemaphoreType.DMA((n_copies,)),
                   *[pltpu.HBM(b.shape, b.dtype) for b in bufs], pltpu.HBM((SUBLANES, LANES), F32)),
        in_specs=[_HBM] * nb + [_ANY] * len(after),
        out_specs=(_SEM, _SEM, *[_HBM] * nb, pl.BlockSpec(memory_space=pltpu.VMEM)),
        input_output_aliases={i: 2 + i for i in range(nb)},
        compiler_params=pltpu.CompilerParams(has_side_effects=_DATAFLOW),
    )(*[pltpu.with_memory_space_constraint(b, pltpu.HBM) for b in bufs], *after)
    return _Exchange(copies, ns, out[0], out[1], list(out[2:2 + nb]), out[-1])


def _exchange_wait(name, ex, after):
    nb, ns = len(ex.thru), ex.n_src

    def body(*refs):
        for cp in ex.copies(refs[:ns], refs[ns:nb], refs[nb], refs[nb + 1]):
            cp.wait_send()
            cp.wait_recv()

    out = pl.pallas_call(
        body, name=name, out_shape=tuple(pltpu.HBM(b.shape, b.dtype) for b in ex.thru),
        in_specs=[_HBM] * nb + [_SEM, _SEM] + [_ANY] * len(after), out_specs=tuple([_HBM] * nb),
        input_output_aliases={i: i for i in range(nb)},
        compiler_params=pltpu.CompilerParams(has_side_effects=_DATAFLOW),
    )(*ex.thru, ex.send_sems, ex.recv_sems, *after)
    return list(out[:ns]), list(out[ns:])


def _halves(ref):
    half = ref.shape[1] // 2
    if half % LANES == 0:
        return ref.at[:, pl.ds(0, half)], ref.at[:, pl.ds(half, half)]
    half = ref.shape[0] // 2
    assert half % (2 * SUBLANES) == 0, ref.shape
    return ref.at[pl.ds(0, half)], ref.at[pl.ds(half, half)]


def _gather_two_route_copies(srcs, lands, send_sems, recv_sems):
    mx, my, mc = _mesh_pos()
    me = 4 * mx + 2 * my + mc
    return [_remote(_halves(srcs[a])[h], _halves(lands[a].at[me])[h], send_sems, recv_sems, 4 * a + i, _flip_peer(flip))
            for a in range(len(srcs)) for i, (flip, h) in enumerate(((2, 0), (1, 1), (2, 1), (1, 0)))]


def _to_sibling(land, flip, h, send_sems, recv_sems, k):
    mx, my, mc = _mesh_pos()
    part = _halves(land.at[2 * jnp.bitwise_xor(2 * mx + my, flip) + mc])[h]
    return _remote(part, part, send_sems, recv_sems, k, (mx, my, 1 - mc))


def _second_hop(land, send_sems, recv_sems, k):
    mx, my, mc = _mesh_pos()
    k0 = 2 * mx + my
    from_y = _halves(land.at[2 * jnp.bitwise_xor(k0, 1) + mc])[1]
    from_x = _halves(land.at[2 * jnp.bitwise_xor(k0, 2) + mc])[0]
    return (_remote(from_y, from_y, send_sems, recv_sems, k, _flip_peer(2)),
            _remote(from_x, from_x, send_sems, recv_sems, k + 1, _flip_peer(1)))


def _relay_call(name, body, bufs, sems, n_new, after):
    nb, n_in = len(bufs), len(bufs) + len(sems) + len(after)

    def call_body(*refs):
        body(refs[:nb], refs[nb:nb + len(sems)], refs[n_in], refs[n_in + 1])
        refs[-1][...] = jnp.zeros_like(refs[-1])

    out = pl.pallas_call(
        call_body, name=name,
        out_shape=(pltpu.SemaphoreType.DMA((n_new,)), pltpu.SemaphoreType.DMA((n_new,)),
                   *[pltpu.HBM(b.shape, b.dtype) for b in bufs], pltpu.HBM((SUBLANES, LANES), F32)),
        in_specs=[_HBM] * nb + [_SEM] * len(sems) + [_ANY] * len(after),
        out_specs=(_SEM, _SEM, *[_HBM] * nb, pl.BlockSpec(memory_space=pltpu.VMEM)),
        input_output_aliases={i: 2 + i for i in range(nb)},
        compiler_params=pltpu.CompilerParams(has_side_effects=_DATAFLOW),
    )(*bufs, *sems, *after)
    return out[0], out[1], list(out[2:2 + nb]), out[-1]


def _gather_relay(ex, tag, after_first):
    ns = ex.n_src
    n = len(ex.thru) - ns

    def first(bufs, sems, send, recv):
        lands, first_hop = bufs[ns:], ex.copies(bufs[:ns], bufs[ns:], *sems)
        for a in range(n):
            for h in (0, 1):
                _to_sibling(lands[a], 0, h, send, recv, 10 * a + h).start()
        for a in range(n):
            x_first, y_second, x_second, y_first = first_hop[4 * a:4 * a + 4]
            to_x, to_y = _second_hop(lands[a], send, recv, 10 * a + 8)
            x_first.wait_recv()
            to_y.start()
            _to_sibling(lands[a], 2, 0, send, recv, 10 * a + 4).start()
            y_second.wait_recv()
            to_x.start()
            _to_sibling(lands[a], 1, 1, send, recv, 10 * a + 3).start()
            x_second.wait_recv()
            _to_sibling(lands[a], 2, 1, send, recv, 10 * a + 5).start()
            y_first.wait_recv()
            _to_sibling(lands[a], 1, 0, send, recv, 10 * a + 2).start()
        for cp in first_hop:
            cp.wait_send()

    def second(lands, sems, send, recv):
        for a in range(n):
            to_x, to_y = _second_hop(lands[a], *sems, 10 * a + 8)
            to_y.wait_recv()
            _to_sibling(lands[a], 3, 0, send, recv, 2 * a).start()
            to_x.wait_recv()
            _to_sibling(lands[a], 3, 1, send, recv, 2 * a + 1).start()
            to_x.wait_send()
            to_y.wait_send()

    def last(lands, sems, send, recv):
        for a in range(n):
            for flip in range(4):
                for h in (0, 1):
                    s, r, k = (sems[2], sems[3], 2 * a + h) if flip == 3 else (sems[0], sems[1], 10 * a + 2 * flip + h)
                    cp = _to_sibling(lands[a], flip, h, s, r, k)
                    cp.wait_send()
                    cp.wait_recv()

    send1, recv1, bufs, token = _relay_call(f"{tag}_chips_relay", first, ex.thru, [ex.send_sems, ex.recv_sems], 10 * n,
                                            after_first)

    def run_second(after):
        send2, recv2, lands, token = _relay_call(f"{tag}_diagonal_relay", second, bufs[ns:], [send1, recv1], 2 * n, after)
        return token, lambda after_last: _relay_call(f"{tag}_sibling_wait", last, lands, [send1, recv1, send2, recv2],
                                                     1, after_last)[2]

    return token, run_second


def _col_tile(r, c, block_bytes=2**20):
    return next(t for t in (1024, 512, 256, 128) if c % t == 0 and (r * t * 4 <= block_bytes or t == 128))


def _add_sibling(g4, recv, pos, name):
    _, _, r, c = g4.shape
    tc = _col_tile(r, c, 2**22)

    def body(pos_ref, g_ref, r_ref, o16_ref, mine_ref):
        s = g_ref[0, 0] + r_ref[0]
        o16_ref[0] = s.astype(BF16)

        @pl.when(pl.program_id(1) == pos_ref[1])
        def _():
            mine_ref[...] = s

    slot = pl.BlockSpec((1, r, tc), lambda j, k, pos_ref: (k, 0, j))
    return pl.pallas_call(
        body, name=name,
        out_shape=[pltpu.HBM((4, r, c), BF16), pltpu.HBM((r, c), F32)],
        grid_spec=pltpu.PrefetchScalarGridSpec(
            num_scalar_prefetch=1, grid=(c // tc, 4),
            in_specs=[pl.BlockSpec((1, 1, r, tc), lambda j, k, pos_ref: (k, pos_ref[0], 0, j)), slot],
            out_specs=[slot, pl.BlockSpec((r, tc), lambda j, k, pos_ref: (0, j))]),
        compiler_params=pltpu.CompilerParams(
            dimension_semantics=("parallel", "arbitrary"), vmem_limit_bytes=_vmem_limit(4 * _nbytes((r, tc), F32))),
    )(pos, *_in_hbm(g4, recv))


class _ReduceScatter:
    def __init__(self, tag, grads_t, pos):
        self.tag, self.pos, self.names = tag, pos, list(grads_t)
        g4s = [g.reshape(4, 2, g.size // (N_DEV * g.shape[-1]), g.shape[-1]) for g in grads_t.values()]
        lands = [lax.empty((4,) + g.shape[2:], F32) for g in g4s]
        self.ex = _exchange_start(f"rs_{tag}_sibling_start", _scatter_sibling_copies, g4s, lands, 4 * len(g4s))
        self.token = self.ex.token

    def start_chips(self, after):
        g4s, from_sibling = _exchange_wait(f"rs_{self.tag}_sibling_wait", self.ex, after)
        parts = [_add_sibling(g4, rv, self.pos, f"rs_add_sibling_{k}")
                 for k, g4, rv in zip(self.names, g4s, from_sibling)]
        self.mine = [mine for _, mine in parts]
        p16s = [p16 for p16, _ in parts]
        lands = [lax.empty((3,) + p.shape[1:], BF16) for p in p16s]
        self.ex = _exchange_start(f"rs_{self.tag}_chips_start", _scatter_chips_copies, p16s, lands, 3 * len(p16s))
        self.token = self.ex.token

    def finish(self, after):
        _, from_chips = _exchange_wait(f"rs_{self.tag}_chips_wait", self.ex, after)
        return dict(zip(self.names, zip(self.mine, from_chips)))


def _rope_tables():
    positions = np.arange(SEQ, dtype=np.float32)
    inv_freq = np.power(np.float32(ROPE_THETA), -np.arange(0, ROPE_DIM, 2, dtype=np.float32) / np.float32(ROPE_DIM))
    ang = (positions[:, None] * inv_freq[None, :]).astype(np.float32)
    cos, sin = np.cos(ang).astype(np.float32), np.sin(ang).astype(np.float32)
    ones = np.ones((SEQ, HEAD_DIM - ROPE_DIM), np.float32)
    zeros8 = np.zeros((SEQ, ROPE_HALF), np.float32)
    zeros = np.zeros((SEQ, HEAD_DIM - ROPE_DIM), np.float32)
    c_head = np.concatenate([cos, cos, ones], axis=1)
    s1_head = np.concatenate([-sin, zeros8, zeros], axis=1)
    s2_head = np.concatenate([zeros8, sin, zeros], axis=1)
    return tuple(jnp.asarray(np.concatenate([t, t], axis=1)) for t in (c_head, s1_head, s2_head))


def _rope_apply(x, c, s1, s2):
    w = x.shape[1]
    return x * c + pltpu.roll(x, w - ROPE_HALF, 1) * s1 + pltpu.roll(x, ROPE_HALF, 1) * s2


def _rope_apply_t(dy, c, s1, s2):
    w = dy.shape[1]
    return dy * c + pltpu.roll(dy * s1, ROPE_HALF, 1) + pltpu.roll(dy * s2, w - ROPE_HALF, 1)


def _dil_prev_limit(has_prev):
    return jnp.where(has_prev, 0, BLOCK)


def _dil_valid(limit):
    row = lax.broadcasted_iota(jnp.int32, (BLOCK, 2 * BLOCK), 0)
    col = lax.broadcasted_iota(jnp.int32, (BLOCK, 2 * BLOCK), 1)
    dist = col - row
    return jnp.logical_and(dist >= jnp.where(col < BLOCK, limit, -BLOCK), dist <= BLOCK)


def _upper_half():
    return lax.broadcasted_iota(jnp.int32, (1, LANES), 1) >= HEAD_DIM


def _stack_heads(x):
    upper = _upper_half()
    return jnp.concatenate([jnp.where(upper, 0, x), jnp.where(upper, x, 0)], axis=0)


def _unstack_heads(y):
    n = y.shape[0] // 2
    return jnp.where(_upper_half(), y[n:], y[:n])


def _head_columns(t):
    return jnp.concatenate([t[:, 0:1], t[:, HEAD_DIM:HEAD_DIM + 1]], axis=0)


def _dil_rows(n, d):
    per = N_BLOCKS // d
    r, lb = n // per, n % per

    def rows(b):
        start = b * (BLOCK * d) + r
        return pl.ds(pl.multiple_of(start, BLOCK), BLOCK) if d == 1 else pl.ds(start, BLOCK, stride=d)

    return rows(lb), rows(jnp.maximum(lb - 1, 0)), lb > 0


def _dil_rotate(q_ref, k_ref, c_ref, s1_ref, s2_ref, q_rot, k_rot):
    tabs = (c_ref[...], s1_ref[...], s2_ref[...])
    q_rot[...] = _rope_apply(q_ref[...], *tabs) * QK_SCALE
    k_rot[...] = _rope_apply(k_ref[...], *tabs)


def _dil_specs():
    def col(base):
        return pl.BlockSpec((SEQ, LANES), lambda p: (0, base // LANES + p))

    table = pl.BlockSpec((SEQ, LANES), lambda p: (0, 0))
    return [col(COL_QA), col(COL_KA), col(COL_VA)], [table] * 3


def _store_columns(blocks, dproj_ref, cols, sem):
    copies = [pltpu.make_async_copy(b, dproj_ref.at[:, pl.ds(pl.multiple_of(c * LANES, LANES), LANES)], sem.at[i])
              for i, (b, c) in enumerate(zip(blocks, cols))]
    for cp in copies:
        cp.start()
    for cp in copies:
        cp.wait()


def _dil_window(d, n, k_rot, v_ref):
    rows, prev, has_prev = _dil_rows(n, d)
    kw, vw = k_rot[rows, :].astype(BF16), v_ref[rows, :].astype(BF16)
    if d == N_BLOCKS:
        row = lax.broadcasted_iota(jnp.int32, (BLOCK, BLOCK), 0)
        valid = lax.broadcasted_iota(jnp.int32, (BLOCK, BLOCK), 1) <= row
    else:
        kw = jnp.concatenate([k_rot[prev, :].astype(BF16), kw], axis=0)
        vw = jnp.concatenate([v_ref[prev, :].astype(BF16), vw], axis=0)
        valid = _dil_valid(_dil_prev_limit(has_prev))
    return rows, prev, kw, vw, jnp.concatenate([valid, valid], axis=0)


def _dil_fwd(proj, tables, deps=()):
    def body(q_ref, k_ref, v_ref, c_ref, s1_ref, s2_ref, *rest):
        o_ref, lse_ref, qr_ref, kr_ref, q_rot, k_rot = rest[len(deps):]
        upper = _upper_half()
        _dil_rotate(q_ref, k_ref, c_ref, s1_ref, s2_ref, q_rot, k_rot)
        qr_ref[...] = q_rot[...].astype(BF16)
        kr_ref[...] = k_rot[...].astype(BF16)

        def blocks_of(d):
            def block(n, carry):
                rows, _, kw, vw, valid = _dil_window(d, n, k_rot, v_ref)
                s = jnp.where(valid, _dot_nt(_stack_heads(q_rot[rows, :].astype(BF16)), kw), NEG_INF)
                m = jnp.max(s, axis=-1, keepdims=True)
                p = jnp.exp(s - m)
                den = jnp.sum(p, axis=-1, keepdims=True)
                o_ref[rows, :] = _unstack_heads(_dot_nn((p * (1.0 / den)).astype(BF16), vw))
                lse = m + jnp.log(den)
                lse_ref[rows, :] = jnp.where(upper, lse[BLOCK:], lse[:BLOCK])
                return carry

            lax.fori_loop(0, N_BLOCKS, block, 0, unroll=4)

        for g, d in enumerate(DILATIONS):
            pl.when(pl.program_id(0) // 2 == g)(functools.partial(blocks_of, d))

    qkv, tabs = _dil_specs()
    out = pl.BlockSpec((SEQ, LANES), lambda p: (0, p))
    return pl.pallas_call(
        body, name="dil_attn_fwd", grid=(DIL_WIDTH // LANES,), in_specs=qkv + tabs + [_ANY] * len(deps),
        out_specs=[out] * 4,
        out_shape=[pltpu.HBM((SEQ, DIL_WIDTH), F32)] * 2 + [pltpu.HBM((SEQ, DIL_WIDTH), BF16)] * 2,
        scratch_shapes=[pltpu.VMEM((SEQ, LANES), F32)] * 2,
        compiler_params=pltpu.CompilerParams(dimension_semantics=("parallel",)),
    )(*_in_hbm(proj, proj, proj, *tables), *deps)


def _dil_bwd(proj, q_rotated, k_rotated, tables, do, lse, c, dproj, deps=()):
    def body(q_ref, k_ref, v_ref, c_ref, s1_ref, s2_ref, do_ref, lse_ref, cc_ref, dproj_in, *rest):
        dproj_ref, dq_acc, dk_acc, dv_acc, dq_out, dk_out, dv_out, q_rot, k_rot, sem = rest[len(deps):]
        dk_acc[...] = jnp.zeros_like(dk_acc)
        dv_acc[...] = jnp.zeros_like(dv_acc)
        q_rot[...] = q_ref[...].astype(F32)
        k_rot[...] = k_ref[...].astype(F32)

        def blocks_of(d):
            def block(n, carry):
                rows, prev, kw, vw, valid = _dil_window(d, n, k_rot, v_ref)
                q2 = _stack_heads(q_rot[rows, :].astype(BF16))
                do2 = _stack_heads(do_ref[rows, :].astype(BF16))
                lse_col, c_col = _head_columns(lse_ref[rows, :]), _head_columns(cc_ref[rows, :])
                p = jnp.where(valid, jnp.exp(_dot_nt(q2, kw) - lse_col), 0.0)
                ds = (p * (_dot_nt(do2, vw) + c_col)).astype(BF16)
                dk, dv = _dot_tn(ds, q2), _dot_tn(p.astype(BF16), do2)
                dq_acc[rows, :] = _unstack_heads(_dot_nn(ds, kw)) * QK_SCALE
                if d == N_BLOCKS:
                    dk_acc[rows, :] += dk
                    dv_acc[rows, :] += dv
                else:
                    dk_acc[prev, :] += dk[:BLOCK]
                    dv_acc[prev, :] += dv[:BLOCK]
                    dk_acc[rows, :] += dk[BLOCK:]
                    dv_acc[rows, :] += dv[BLOCK:]
                return carry

            lax.fori_loop(0, N_BLOCKS, block, 0, unroll=4)

        pair = pl.program_id(0)
        for g, d in enumerate(DILATIONS):
            pl.when(pair // 2 == g)(functools.partial(blocks_of, d))
        tabs = (c_ref[...], s1_ref[...], s2_ref[...])
        dq_out[...] = _rope_apply_t(dq_acc[...], *tabs).astype(BF16)
        dk_out[...] = _rope_apply_t(dk_acc[...], *tabs).astype(BF16)
        dv_out[...] = dv_acc[...].astype(BF16)
        _store_columns((dq_out, dk_out, dv_out), dproj_ref,
                       [base // LANES + pair for base in (COL_QA, COL_KA, COL_VA)], sem)

    qkv, tabs = _dil_specs()
    tok = pl.BlockSpec((SEQ, LANES), lambda p: (0, p))
    return pl.pallas_call(
        body, name="dil_attn_bwd", grid=(DIL_WIDTH // LANES,),
        in_specs=[tok, tok, qkv[2]] + tabs + [tok, tok, tok, _ANY] + [_ANY] * len(deps), out_specs=_ANY,
        out_shape=pltpu.HBM(dproj.shape, dproj.dtype),
        scratch_shapes=[pltpu.VMEM((SEQ, LANES), F32)] * 3 + [pltpu.VMEM((SEQ, LANES), BF16)] * 3
        + [pltpu.VMEM((SEQ, LANES), F32)] * 2 + [pltpu.SemaphoreType.DMA((3,))],
        input_output_aliases={9: 0},
        compiler_params=pltpu.CompilerParams(dimension_semantics=("arbitrary",)),
    )(*_in_hbm(q_rotated, k_rotated, proj, *tables, do, lse, c, dproj), *deps)


def _group_weights(l0, l1, l2):
    m = jnp.maximum(jnp.maximum(l0, l1), l2)
    e0, e1, e2 = jnp.exp(l0 - m), jnp.exp(l1 - m), jnp.exp(l2 - m)
    tot = e0 + e1 + e2
    return e0 / tot, e1 / tot, e2 / tot


def _dil_combine(o, lse, deps=()):
    def fn(o0, o1, o2, l0, l1, l2):
        w0, w1, w2 = _group_weights(l0, l1, l2)
        return w0 * o0 + w1 * o1 + w2 * o2

    w = DIL_OUT_WIDTH
    return _rowwise(fn, "dil_combine", SEQ, 512, [(o, w, g) for g in range(3)] + [(lse, w, g) for g in range(3)], [],
                    [(w, F32)], deps=deps)[0]


def _dil_combine_bwd(d_out, o, lse, deps=()):
    w = DIL_OUT_WIDTH

    def fn(d, o0, o1, o2, l0, l1, l2):
        row = lax.broadcasted_iota(jnp.int32, (w, w), 0) // HEAD_DIM
        col = lax.broadcasted_iota(jnp.int32, (w, w), 1) // HEAD_DIM
        same_head = jnp.where(row == col, 1.0, 0.0).astype(BF16)
        ws = _group_weights(l0, l1, l2)
        dws = [_dot3_nn(d * og, same_head) for og in (o0, o1, o2)]
        mean = ws[0] * dws[0] + ws[1] * dws[1] + ws[2] * dws[2]
        return jnp.concatenate([wg * d for wg in ws], axis=1), jnp.concatenate([-wg * mean for wg in ws], axis=1)

    return _rowwise(fn, "dil_combine_bwd", SEQ, 512,
                    [(d_out, w, 0)] + [(o, w, g) for g in range(3)] + [(lse, w, g) for g in range(3)], [],
                    [(DIL_WIDTH, F32)] * 2, deps=deps)


def _log1p(e):
    u = 1.0 + e
    return jnp.where(u == 1.0, e, jnp.log(u) * (e / (u - 1.0)))


def _fox_gate(proj, b_pad, deps=()):
    def body(f_ref, b_ref, *rest):
        o_ref = rest[-1]
        z = f_ref[...] + b_ref[...]
        logf = (jnp.minimum(z, 0.0) - _log1p(jnp.exp(-jnp.abs(z)))).T[:F_ROWS]
        row = lax.broadcasted_iota(jnp.int32, (BLOCK, BLOCK), 0)
        col = lax.broadcasted_iota(jnp.int32, (BLOCK, BLOCK), 1)
        before = jnp.where(row <= col, 1.0, 0.0).astype(BF16)
        carry = jnp.zeros((F_ROWS, 1), F32)
        for blk in range(N_BLOCKS):
            run = _dot3_nn(logf[:, blk * BLOCK:(blk + 1) * BLOCK], before) + carry
            o_ref[:, blk * BLOCK:(blk + 1) * BLOCK] = run
            carry = run[:, BLOCK - 1:BLOCK]

    return pl.pallas_call(
        body, name="fox_gate", grid=(1,),
        in_specs=[pl.BlockSpec((SEQ, LANES), lambda i: (0, COL_F // LANES)), pl.BlockSpec((1, LANES), lambda i: (0, 0))]
        + [_ANY] * len(deps),
        out_specs=pl.BlockSpec((F_ROWS, SEQ), lambda i: (0, 0)),
        out_shape=pltpu.HBM((F_ROWS, SEQ), F32),
    )(*_in_hbm(proj, b_pad), *deps)


def _fox_gate_bwd(d_cum, proj, b_pad, dproj):
    def body(d_ref, f_ref, b_ref, dproj_ref, dz_ref, db_ref):
        row = lax.broadcasted_iota(jnp.int32, (BLOCK, BLOCK), 0)
        col = lax.broadcasted_iota(jnp.int32, (BLOCK, BLOCK), 1)
        after = jnp.where(row >= col, 1.0, 0.0).astype(BF16)
        carry = jnp.zeros((F_ROWS, 1), F32)
        parts = [None] * N_BLOCKS
        for blk in reversed(range(N_BLOCKS)):
            run = _dot3_nn(d_ref[:, blk * BLOCK:(blk + 1) * BLOCK], after) + carry
            parts[blk] = run
            carry = run[:, 0:1]
        dlogf = jnp.concatenate(parts, axis=1)
        dlogf = jnp.concatenate([dlogf, jnp.zeros((LANES - F_ROWS, SEQ), F32)], axis=0).T
        dz = dlogf * _sigmoid(-(f_ref[...] + b_ref[...]))
        dz_ref[...] = dz.astype(BF16)
        db_ref[...] = jnp.sum(dz, axis=0, keepdims=True)

    f_cols = pl.BlockSpec((SEQ, LANES), lambda i: (0, COL_F // LANES))
    return pl.pallas_call(
        body, name="fox_gate_bwd", grid=(1,),
        in_specs=[pl.BlockSpec((F_ROWS, SEQ), lambda i: (0, 0)), f_cols, pl.BlockSpec((1, LANES), lambda i: (0, 0)), _ANY],
        out_specs=[f_cols, pl.BlockSpec((1, LANES), lambda i: (0, 0))],
        out_shape=[pltpu.HBM(dproj.shape, dproj.dtype), pltpu.HBM((1, LANES), F32)],
        input_output_aliases={3: 0},
    )(*_in_hbm(d_cum, proj, b_pad, dproj))


FOX_TILE = 256
FOX_TILES = SEQ // FOX_TILE


def _row_to_col(row):
    n = row.shape[1]
    eye = lax.broadcasted_iota(jnp.int32, (n, n), 0) == lax.broadcasted_iota(jnp.int32, (n, n), 1)
    return jnp.sum(jnp.where(eye, row, 0.0), axis=1, keepdims=True)


def _fox_bias(f_row, i):
    t = FOX_TILE
    ext = (i + 1) * t
    bias = _row_to_col(f_row[:, i * t:(i + 1) * t]) - f_row[:, :ext]
    row = lax.broadcasted_iota(jnp.int32, (t, ext), 0) + i * t
    col = lax.broadcasted_iota(jnp.int32, (t, ext), 1)
    return bias, col <= row


def _fox_specs():
    qkv = [pl.BlockSpec((SEQ, LANES), lambda p, base=base: (0, base // LANES + p)) for base in (COL_QB, COL_KB, COL_VB)]
    return qkv, pl.BlockSpec((F_ROWS, SEQ), lambda p: (0, 0))


def _fox_fwd(proj, f_rows):
    t = FOX_TILE

    def body(q_ref, k_ref, v_ref, f_ref, o_ref, lse_ref):
        pair = pl.program_id(0)
        upper = _upper_half()
        k16, v16 = k_ref[...].astype(BF16), v_ref[...].astype(BF16)
        f_row = [f_ref[pl.ds(2 * pair + e, 1), :] for e in range(2)]
        for i in range(FOX_TILES):
            ext = (i + 1) * t
            q_tile = (q_ref[i * t:(i + 1) * t, :] * QK_SCALE).astype(BF16)
            s2 = _dot_nt(_stack_heads(q_tile), k16[:ext])
            pns, lses = [], []
            for e in range(2):
                bias, causal = _fox_bias(f_row[e], i)
                s = jnp.where(causal, s2[e * t:(e + 1) * t] + bias, NEG_INF)
                m = jnp.max(s, axis=-1, keepdims=True)
                p = jnp.exp(s - m)
                den = jnp.sum(p, axis=-1, keepdims=True)
                pns.append((p * (1.0 / den)).astype(BF16))
                lses.append(m + jnp.log(den))
            o_ref[i * t:(i + 1) * t, :] = _unstack_heads(_dot_nn(jnp.concatenate(pns, axis=0), v16[:ext]))
            lse_ref[i * t:(i + 1) * t, :] = jnp.where(upper, lses[1], lses[0])

    qkv, f_spec = _fox_specs()
    tok = pl.BlockSpec((SEQ, LANES), lambda p: (0, p))
    return pl.pallas_call(
        body, name="fox_attn_fwd", grid=(FOX_WIDTH // LANES,),
        in_specs=qkv + [f_spec], out_specs=[tok, tok],
        out_shape=[pltpu.HBM((SEQ, FOX_WIDTH), F32)] * 2,
        compiler_params=pltpu.CompilerParams(
            dimension_semantics=("parallel",), vmem_limit_bytes=_vmem_limit(8 * t * SEQ * 4)),
    )(*_in_hbm(proj, proj, proj, f_rows))


def _fox_bwd(proj, do, lse, f_rows, dproj):
    t = FOX_TILE

    def body(q_ref, k_ref, v_ref, f_ref, do_ref, lse_ref, dproj_in, dproj_ref, df_ref, dk_acc, dv_acc,
             dq_out, dk_out, dv_out, sem):
        pair = pl.program_id(0)
        upper = _upper_half()
        k16, v16 = k_ref[...].astype(BF16), v_ref[...].astype(BF16)
        f_row = [f_ref[pl.ds(2 * pair + e, 1), :] for e in range(2)]
        dk_acc[...] = jnp.zeros_like(dk_acc)
        dv_acc[...] = jnp.zeros_like(dv_acc)
        df_ref[...] = jnp.zeros_like(df_ref)
        for i in range(FOX_TILES):
            ext = (i + 1) * t
            q_tile = (q_ref[i * t:(i + 1) * t, :] * QK_SCALE).astype(BF16)
            do_tile = do_ref[i * t:(i + 1) * t, :]
            lse_t = lse_ref[i * t:(i + 1) * t, :]
            q2, do2 = _stack_heads(q_tile), _stack_heads(do_tile)
            s2, dp2 = _dot_nt(q2, k16[:ext]), _dot_nt(do2, v16[:ext])
            ps, dss = [], []
            for e in range(2):
                bias, causal = _fox_bias(f_row[e], i)
                s = s2[e * t:(e + 1) * t] + bias
                p = jnp.where(causal, jnp.exp(s - lse_t[:, e * HEAD_DIM:e * HEAD_DIM + 1]), 0.0)
                dp = dp2[e * t:(e + 1) * t]
                ds = p * (dp - jnp.sum(p * dp, axis=-1, keepdims=True))
                df_ref[0, e:e + 1, :ext] -= jnp.sum(ds, axis=0, keepdims=True)
                ps.append(p.astype(BF16))
                dss.append(ds.astype(BF16))
            ds2, p2 = jnp.concatenate(dss, axis=0), jnp.concatenate(ps, axis=0)
            dq_out[i * t:(i + 1) * t, :] = (_unstack_heads(_dot_nn(ds2, k16[:ext])) * QK_SCALE).astype(BF16)
            dk_acc[:ext, :] += _dot_tn(ds2, q2)
            dv_acc[:ext, :] += _dot_tn(p2, do2)
        dk_out[...] = dk_acc[...].astype(BF16)
        dv_out[...] = dv_acc[...].astype(BF16)
        _store_columns((dq_out, dk_out, dv_out), dproj_ref, [base // LANES + pair for base in (COL_QB, COL_KB, COL_VB)],
                       sem)

    qkv, f_spec = _fox_specs()
    tok = pl.BlockSpec((SEQ, LANES), lambda p: (0, p))
    return pl.pallas_call(
        body, name="fox_attn_bwd", grid=(FOX_WIDTH // LANES,),
        in_specs=qkv + [f_spec, tok, tok, _ANY],
        out_specs=[_ANY, pl.BlockSpec((1, SUBLANES, SEQ), lambda p: (p, 0, 0))],
        out_shape=[pltpu.HBM(dproj.shape, dproj.dtype),
                   pltpu.HBM((FOX_WIDTH // LANES, SUBLANES, SEQ), F32)],
        scratch_shapes=[pltpu.VMEM((SEQ, LANES), F32)] * 2 + [pltpu.VMEM((SEQ, LANES), BF16)] * 3
        + [pltpu.SemaphoreType.DMA((3,))],
        input_output_aliases={6: 0},
        compiler_params=pltpu.CompilerParams(
            dimension_semantics=("arbitrary",), vmem_limit_bytes=_vmem_limit(10 * t * SEQ * 4)),
    )(*_in_hbm(proj, proj, proj, f_rows, do, lse, dproj))


MIX_TILE = 512


def _slot_columns(w_ref):
    return jnp.concatenate([w_ref[j] for j in range(N_DEV)], axis=1)


def _mix_out(out_a, out_b, proj, x, wt_pa, wt_pb, w_out, g_post, g_ffn_pre):
    tm = MIX_TILE

    def body(a_ref, b_ref, ga_ref, gb_ref, x_ref, wpa_ref, wpb_ref, wo_ref, g2_ref, g3_ref,
             merged_ref, mix_ref, x1_ref, h2_ref):
        ya = _dot_nn(a_ref[...].astype(BF16), _slot_columns(wpa_ref))
        yb = _dot_nn(b_ref[...].astype(BF16), _slot_columns(wpb_ref))
        merged = (_sigmoid(ga_ref[...]) * ya + _sigmoid(gb_ref[...]) * yb).astype(BF16)
        merged_ref[...] = merged
        mix = _dot_nn(merged, wo_ref[...])
        mix_ref[...] = mix
        x1 = x_ref[...] + mix * _rms_scale(mix) * g2_ref[...]
        x1_ref[...] = x1
        h2_ref[...] = (x1 * _rms_scale(x1) * g3_ref[...]).astype(BF16)

    def rows(w, cb=0):
        return pl.BlockSpec((tm, w), lambda i, cb=cb: (i, cb))

    def whole(a):
        return pl.BlockSpec(a.shape, lambda i: (0,) * a.ndim)

    d = D_MODEL
    blk = _nbytes((tm, d), F32) * 6 + sum(_nbytes(a.shape, BF16) for a in (wt_pa, wt_pb, w_out))
    return pl.pallas_call(
        body, name="mix_out", grid=(SEQ // tm,),
        in_specs=[rows(DIL_OUT_WIDTH), rows(FOX_WIDTH), rows(d, COL_GA // d), rows(d, COL_GB // d), rows(d),
                  whole(wt_pa), whole(wt_pb), whole(w_out), whole(g_post), whole(g_ffn_pre)],
        out_specs=[rows(d)] * 4,
        out_shape=[pltpu.HBM((SEQ, d), dt) for dt in (BF16, F32, F32, BF16)],
        compiler_params=pltpu.CompilerParams(dimension_semantics=("parallel",), vmem_limit_bytes=_vmem_limit(blk)),
    )(*_in_hbm(out_a, out_b, proj, proj, x, wt_pa, wt_pb, w_out, g_post, g_ffn_pre))


def _mix_out_bwd(dmix, out_a, out_b, proj, wt_pa, wt_pb, w_out, deps=()):
    tm = MIX_TILE

    def body(dm_ref, a_ref, b_ref, ga_ref, gb_ref, wpa_ref, wpb_ref, wo_ref, *rest):
        dproj_ref, dya_ref, dyb_ref, da_ref, db_ref = rest[len(deps):]
        dmerged = _dot_nt(dm_ref[...], wo_ref[...])
        wpa, wpb = _slot_columns(wpa_ref), _slot_columns(wpb_ref)
        ya = _dot_nn(a_ref[...].astype(BF16), wpa)
        yb = _dot_nn(b_ref[...].astype(BF16), wpb)
        sa, sb = _sigmoid(ga_ref[...]), _sigmoid(gb_ref[...])
        dproj_ref[:, COL_GA:COL_GA + D_MODEL] = (dmerged * ya * (sa * (1.0 - sa))).astype(BF16)
        dproj_ref[:, COL_GB:COL_GB + D_MODEL] = (dmerged * yb * (sb * (1.0 - sb))).astype(BF16)
        dproj_ref[:, COL_GB + D_MODEL:] = jnp.zeros((tm, COL_QA - COL_GB - D_MODEL), BF16)
        dya = (dmerged * sa).astype(BF16)
        dyb = (dmerged * sb).astype(BF16)
        dya_ref[...] = dya
        dyb_ref[...] = dyb
        da_ref[...] = _dot_nt(dya, wpa)
        db_ref[...] = _dot_nt(dyb, wpb).astype(BF16)

    def rows(w, cb=0):
        return pl.BlockSpec((tm, w), lambda i, cb=cb: (i, cb))

    def whole(a):
        return pl.BlockSpec(a.shape, lambda i: (0,) * a.ndim)

    d = D_MODEL
    blk = _nbytes((tm, d), F32) * 8 + sum(_nbytes(a.shape, BF16) for a in (wt_pa, wt_pb, w_out))
    return pl.pallas_call(
        body, name="mix_out_bwd", grid=(SEQ // tm,),
        in_specs=[rows(d), rows(DIL_OUT_WIDTH), rows(FOX_WIDTH), rows(d, COL_GA // d), rows(d, COL_GB // d),
                  whole(wt_pa), whole(wt_pb), whole(w_out)] + [_ANY] * len(deps),
        out_specs=[rows(COL_QA)] + [rows(d)] * 2 + [rows(DIL_OUT_WIDTH), rows(FOX_WIDTH)],
        out_shape=[pltpu.HBM((SEQ, PROJ_COLS), BF16)] + [pltpu.HBM((SEQ, d), BF16)] * 2
        + [pltpu.HBM((SEQ, DIL_OUT_WIDTH), F32), pltpu.HBM((SEQ, FOX_WIDTH), BF16)],
        compiler_params=pltpu.CompilerParams(dimension_semantics=("parallel",), vmem_limit_bytes=_vmem_limit(blk)),
    )(*_in_hbm(dmix, out_a, out_b, proj, proj, wt_pa, wt_pb, w_out), *deps)


FFN_TM, FFN_TN = 2048, 256


def _ffn_up(h2, wt_gate, wt_up):
    tm, tn = FFN_TM, FFN_TN

    def body(h_ref, wg_ref, wu_ref, gate_ref, up_ref, act_ref):
        for rows in (slice(0, tm // 2), slice(tm // 2, tm)):
            gate = _dot_nt(h_ref[rows, :], wg_ref[...])
            up = _dot_nt(h_ref[rows, :], wu_ref[...])
            gate_ref[rows, :] = gate
            up_ref[rows, :] = up
            act_ref[rows, :] = (gate * _sigmoid(gate) * up).astype(BF16)

    tile = pl.BlockSpec((tm, tn), lambda i, j: (i, j))
    w_spec = pl.BlockSpec((tn, D_MODEL), lambda i, j: (j, 0))
    return pl.pallas_call(
        body, name="ffn_up", grid=(SEQ // tm, D_FF // tn),
        in_specs=[pl.BlockSpec((tm, D_MODEL), lambda i, j: (i, 0)), w_spec, w_spec],
        out_specs=[tile, tile, tile],
        out_shape=[pltpu.HBM((SEQ, D_FF), dt) for dt in (F32, F32, BF16)],
        compiler_params=pltpu.CompilerParams(
            dimension_semantics=("parallel", "parallel"), vmem_limit_bytes=_vmem_limit(8 * 2**20)),
    )(h2, wt_gate, wt_up)


def _ffn_act_bwd(dff, w_down, gate, up):
    tm, tn = FFN_TM, FFN_TN

    def body(d_ref, wd_ref, gate_ref, up_ref, dgate_ref, dup_ref):
        for rows in (slice(0, tm // 2), slice(tm // 2, tm)):
            dact = _dot_nt(d_ref[rows, :], wd_ref[...])
            gate = gate_ref[rows, :]
            sg = _sigmoid(gate)
            dgate_ref[rows, :] = (dact * up_ref[rows, :] * (sg * (1.0 + gate * (1.0 - sg)))).astype(BF16)
            dup_ref[rows, :] = (dact * (gate * sg)).astype(BF16)

    tile = pl.BlockSpec((tm, tn), lambda i, j: (i, j))
    return pl.pallas_call(
        body, name="ffn_act_bwd", grid=(SEQ // tm, D_FF // tn),
        in_specs=[pl.BlockSpec((tm, D_MODEL), lambda i, j: (i, 0)), pl.BlockSpec((tn, D_MODEL), lambda i, j: (j, 0)),
                  tile, tile],
        out_specs=[tile, tile],
        out_shape=[pltpu.HBM((SEQ, D_FF), BF16)] * 2,
        compiler_params=pltpu.CompilerParams(
            dimension_semantics=("parallel", "parallel"), vmem_limit_bytes=_vmem_limit(8 * 2**20)),
    )(dff, w_down, gate, up)


EPILOGUE_TM = 512


def _loss_head(act, w_down, x1, target, g_post):
    def fn(ff, x1, tgt, g):
        r = _rms_scale(ff)
        nrm = ff * r
        err = (x1 + nrm * g) - tgt
        loss = 0.5 * jnp.sum(jnp.mean(err * err, axis=-1, keepdims=True), axis=0, keepdims=True)
        dy = err * (1.0 / D_MODEL)
        u = dy * g
        dff = r * u - ff * (r * r * r) * jnp.mean(u * ff, axis=-1, keepdims=True)
        return dy, dff, jnp.broadcast_to(loss, (1, LANES)), jnp.sum(dy * nrm, axis=0, keepdims=True)

    d = D_MODEL
    return _matmul_rowwise([(act, w_down)], fn, "ffn_down_loss", EPILOGUE_TM, [(x1, d, 0), (target, d, 0)], [g_post],
                           [(d, F32), (d, BF16)], [LANES, d])


def _post_ffn_bwd(dgate, wt_gate, dup, wt_up, x1, dy, mix, g_ffn_pre, g_mix_post, deps=()):
    def fn(dh2, x1, dy, mix, g3, g2):
        dx, dg3 = _rms_bwd(x1, dh2, g3)
        dx1 = dy + dx
        dmix, dg2 = _rms_bwd(mix, dx1, g2)
        return dx1, dmix, dg3, dg2

    d = D_MODEL
    return _matmul_rowwise([(dgate, wt_gate), (dup, wt_up)], fn, "ffn_up_bwd", EPILOGUE_TM,
                           [(x1, d, 0), (dy, d, 0), (mix, d, 0)], [g_ffn_pre, g_mix_post],
                           [(d, F32), (d, BF16)], [d, d], deps=deps)


def _input_bwd(dproj, wt_r, x, dx1, g_pre, deps=()):
    def fn(dh, x, dx1, g):
        dx, dg = _rms_bwd(x, dh, g)
        return dx1 + dx, dg

    d = D_MODEL
    return _matmul_rowwise([(dproj, wt_r)], fn, "in_proj_bwd", EPILOGUE_TM, [(x, d, 0), (dx1, d, 0)], [g_pre],
                           [(d, F32)], [d], deps=deps)


def _adam_math(w, g, m, v):
    m = ADAM_B1 * m + (1.0 - ADAM_B1) * g
    v = ADAM_B2 * v + (1.0 - ADAM_B2) * (g * g)
    m_hat = m / (1.0 - ADAM_B1 ** ADAM_STEP)
    v_hat = v / (1.0 - ADAM_B2 ** ADAM_STEP)
    delta = -ADAM_LR * (m_hat / (jnp.sqrt(v_hat) + ADAM_EPS) + ADAM_WD * w)
    return delta, m, v


def _adam(w, mine, recv, m, v, name):
    r, c = w.shape
    tc = _col_tile(r, c)

    def body(w_ref, p_ref, r_ref, m_ref, v_ref, g_ref, d_ref, nm_ref, nv_ref):
        g = ((p_ref[...] + r_ref[0].astype(F32)) + r_ref[1].astype(F32)) + r_ref[2].astype(F32)
        g_ref[...] = g
        d_ref[...], nm_ref[...], nv_ref[...] = _adam_math(w_ref[...], g, m_ref[...], v_ref[...])

    spec = pl.BlockSpec((r, tc), lambda j: (0, j))
    return pl.pallas_call(
        body, name=name, grid=(c // tc,),
        in_specs=[spec, spec, pl.BlockSpec((3, r, tc), lambda j: (0, 0, j)), spec, spec], out_specs=[spec] * 4,
        out_shape=[pltpu.HBM((r, c), F32)] * 4,
        compiler_params=pltpu.CompilerParams(dimension_semantics=("parallel",)),
    )(*_in_hbm(w, mine, recv, m, v))


def _adam_small(gathered, ws, ms, vs, loss_parts):
    n = len(ws)

    def body(*refs):
        outs = refs[4 * n + 1:]
        loss = refs[4 * n][0]
        for dev in range(1, N_DEV):
            loss = loss + refs[4 * n][dev]
        outs[4 * n][...] = loss
        for i in range(n):
            ga_ref, w_ref, m_ref, v_ref = (refs[j * n + i] for j in range(4))
            g = ga_ref[0]
            for dev in range(1, N_DEV):
                g = g + ga_ref[dev]
            g = g[:, :w_ref.shape[1]]
            outs[4 * i][...] = g
            outs[4 * i + 1][...], outs[4 * i + 2][...], outs[4 * i + 3][...] = _adam_math(
                w_ref[...], g, m_ref[...], v_ref[...])

    out_shape = [pltpu.HBM(w.shape, F32) for w in ws for _ in range(4)]
    out_shape.append(pltpu.HBM((1, LANES), F32))
    out = pl.pallas_call(body, name="adam_small", out_shape=out_shape)(*gathered, *ws, *ms, *vs, loss_parts)
    return [out[4 * i:4 * i + 4] for i in range(n)], out[4 * n]


_PROJ_SEGMENTS = ((3848, 5896), (None, COL_QA - 2 * D_MODEL), (0, 3840), (3840, 3848), (None, PROJ_COLS - COL_F - 8))


def _proj_weight_t(gathered):
    pieces, zeros, at = [], [], 0
    for lo, hi in _PROJ_SEGMENTS:
        if lo is None:
            zeros.append((at, hi))
            at += hi
            continue
        for dev in range(lo // IN_SHARD, (hi - 1) // IN_SHARD + 1):
            a, b = max(lo, dev * IN_SHARD), min(hi, (dev + 1) * IN_SHARD)
            pieces.append((dev, a - dev * IN_SHARD, at + a - lo, b - a))
        at += hi - lo
    assert at == PROJ_COLS
    tc = 2 * LANES

    def body(g_ref, o_ref, shards, rows):
        for dev in range(N_DEV):
            shards[dev] = g_ref[dev].astype(F32)
        for dev, src, dst, n in pieces:
            rows[pl.ds(dst, n), :] = shards[dev, pl.ds(src, n), :]
        for dst, n in zeros:
            rows[pl.ds(dst, n), :] = jnp.zeros((n, tc), F32)
        o_ref[...] = rows[...].astype(o_ref.dtype)

    return pl.pallas_call(
        body, name="w_in_rows", grid=(D_MODEL // tc,),
        in_specs=[pl.BlockSpec((N_DEV, IN_SHARD, tc), lambda j: (0, 0, j))],
        out_specs=pl.BlockSpec((PROJ_COLS, tc), lambda j: (0, j)),
        out_shape=pltpu.HBM((PROJ_COLS, D_MODEL), gathered.dtype),
        scratch_shapes=[pltpu.VMEM((N_DEV, IN_SHARD, tc), F32), pltpu.VMEM((PROJ_COLS, tc), F32)],
        compiler_params=pltpu.CompilerParams(
            dimension_semantics=("parallel",), vmem_limit_bytes=_vmem_limit(2 * _nbytes((PROJ_COLS, tc), F32))),
    )(*_in_hbm(gathered))


def _proj_weight_grad_slots(dwt_r):
    starts, at = [], 0
    for lo, hi in _PROJ_SEGMENTS:
        if lo is not None:
            starts.append((lo, hi, at))
        at += hi if lo is None else hi - lo
    pieces = []
    for dev in range(N_DEV):
        lo, end = dev * IN_SHARD, (dev + 1) * IN_SHARD
        for seg_lo, seg_hi, seg_at in sorted(starts):
            a, b = max(lo, seg_lo), min(end, seg_hi)
            if a < b:
                pieces.append((dev, a - lo, seg_at + a - seg_lo, b - a))

    def body(g_ref, o_ref):
        for dev, dst, src, rows in pieces:
            o_ref[dev, pl.ds(dst, rows), :] = g_ref[pl.ds(src, rows), :]

    tc = 2 * LANES
    return pl.pallas_call(
        body, name="grad_w_in_slots", grid=(D_MODEL // tc,),
        in_specs=[pl.BlockSpec((PROJ_COLS, tc), lambda j: (0, j))],
        out_specs=pl.BlockSpec((N_DEV, IN_SHARD, tc), lambda j: (0, 0, j)),
        out_shape=pltpu.HBM((N_DEV, IN_SHARD, D_MODEL), F32),
        compiler_params=pltpu.CompilerParams(
            dimension_semantics=("parallel",), vmem_limit_bytes=_vmem_limit(2 * _nbytes((PROJ_COLS, tc), F32))),
    )(*_in_hbm(dwt_r))


def kernel(x, w_in, w_proj_a, w_proj_b, w_out, b_forget, w_ffn_gate, w_ffn_up, w_ffn_down, norm_mix_pre, norm_mix_post, norm_ffn_pre, norm_ffn_post, loss_target, m_w_in, m_w_proj_a, m_w_proj_b, m_w_out, m_b_forget, m_w_ffn_gate, m_w_ffn_up, m_w_ffn_down, m_norm_mix_pre, m_norm_mix_post, m_norm_ffn_pre, m_norm_ffn_post, v_w_in, v_w_proj_a, v_w_proj_b, v_w_out, v_b_forget, v_w_ffn_gate, v_w_ffn_up, v_w_ffn_down, v_norm_mix_pre, v_norm_mix_post, v_norm_ffn_pre, v_norm_ffn_post):
    d = D_MODEL
    names = ("w_in", "w_proj_a", "w_proj_b", "w_out", "w_ffn_gate", "w_ffn_up", "w_ffn_down")
    col_sharded = ("w_in", "w_ffn_gate", "w_ffn_up")

    def row_shards(arrs):
        return {k: (a[0].T if k in col_sharded else a[0]) for k, a in zip(names, arrs)}

    shards = row_shards((w_in, w_proj_a, w_proj_b, w_out, w_ffn_gate, w_ffn_up, w_ffn_down))
    moments_m = row_shards((m_w_in, m_w_proj_a, m_w_proj_b, m_w_out, m_w_ffn_gate, m_w_ffn_up, m_w_ffn_down))
    moments_v = row_shards((v_w_in, v_w_proj_a, v_w_proj_b, v_w_out, v_w_ffn_gate, v_w_ffn_up, v_w_ffn_down))
    pos = jnp.stack([lax.axis_index("c"), 2 * lax.axis_index("x") + lax.axis_index("y")]).astype(jnp.int32)
    x2, target = x[0], loss_target[0]

    me = 4 * lax.axis_index("x") + 2 * lax.axis_index("y") + lax.axis_index("c")
    mid_names, ffn_names = names[1:4], names[4:]
    first_names, later_names = names[:1], names[1:]
    shards16 = {k: shards[k].astype(BF16) for k in names}

    def landing(k):
        return lax.dynamic_update_slice(lax.empty((N_DEV,) + shards[k].shape, BF16), shards16[k][None], (me, 0, 0))

    ag_first = _exchange_start("ag_first_chips_start", _gather_two_route_copies, [shards16[k] for k in first_names],
                               [landing(k) for k in first_names], 4 * len(first_names))
    h = _rowwise(lambda xb, g: xb * _rms_scale(xb) * g, "norm_mix_pre", SEQ, 512, [(x2, d, 0)], [norm_mix_pre],
                 [(d, BF16)], deps=[ag_first.token])[0]
    later_lands = [landing(k) for k in later_names]
    _, ag_first_diagonal = _gather_relay(ag_first, "ag_first", [h, shards["w_in"], moments_m["w_in"], moments_v["w_in"],
                                                               *later_lands, *[shards16[k] for k in later_names]])
    relayed, ag_first_last = ag_first_diagonal([])
    ag_later = _exchange_start("ag_later_chips_start", _gather_two_route_copies, [shards16[k] for k in later_names],
                               later_lands, 4 * len(later_names), after=[relayed])
    gathered = dict(zip(first_names, ag_first_last([ag_later.token])))
    wt_r = _proj_weight_t(gathered["w_in"])

    proj = _matmul([(h, *_in_hbm(wt_r))], "nt", F32, "in_proj", 1024, 896, 1024)
    tables = _rope_tables()
    relayed, ag_later_diagonal = _gather_relay(ag_later, "ag_later", [proj])
    o_dil, lse_dil, q_dil, k_dil = _dil_fwd(proj, tables, deps=[relayed])
    out_a = _dil_combine(o_dil, lse_dil)
    relayed, ag_later_last = ag_later_diagonal([out_a])

    b_pad = jnp.pad(b_forget, ((0, 0), (0, LANES - N_FOX_HEADS)))
    f_rows = _fox_gate(proj, b_pad, deps=[relayed])
    out_b, lse_fox = _fox_fwd(proj, f_rows)

    gathered = dict(zip(later_names, ag_later_last([out_b])))
    wt_pa, wt_pb = gathered["w_proj_a"], gathered["w_proj_b"]
    w_o = gathered["w_out"].reshape(d, d)
    wt_g = gathered["w_ffn_gate"].reshape(D_FF, d)
    wt_u = gathered["w_ffn_up"].reshape(D_FF, d)
    w_d = gathered["w_ffn_down"].reshape(D_FF, d)
    merged, mix, x1, h2 = _mix_out(out_a, out_b, proj, x2, wt_pa, wt_pb, w_o, norm_mix_post, norm_ffn_pre)

    gate, up, act = _ffn_up(h2, wt_g, wt_u)
    dy, dff, loss_part, dg_ffn_post = _loss_head(act, w_d, x1, target, norm_ffn_post)

    dgate, dup = _ffn_act_bwd(dff, w_d, gate, up)
    grads_t = {}
    grads_t["w_ffn_down"] = _matmul([(act, dff)], "tn", F32, "grad_w_ffn_down", 1408, 512, 2048, staged=False)
    grads_t["w_ffn_gate"] = _matmul([(dgate, h2)], "tn", F32, "grad_w_ffn_gate", 1408, 512, 2048)
    grads_t["w_ffn_up"] = _matmul([(dup, h2)], "tn", F32, "grad_w_ffn_up", 1408, 512, 2048)
    rs_ffn = _ReduceScatter("ffn", {k: grads_t[k] for k in ffn_names}, pos)
    dx1, dmix, dg_ffn_pre, dg_mix_post = _post_ffn_bwd(dgate, wt_g, dup, wt_u, x1, dy, mix, norm_ffn_pre, norm_mix_post,
                                                       deps=[rs_ffn.token])
    rs_ffn.start_chips([dmix])

    dproj, dya, dyb, d_out_a, d_out_b = _mix_out_bwd(dmix, out_a, out_b, proj, wt_pa, wt_pb, w_o, deps=[rs_ffn.token])
    grads_t["w_out"] = _matmul([(merged, dmix)], "tn", F32, "grad_w_out", 1024, 1024, 1024)
    grads_t["w_proj_a"] = _matmul([(out_a, dya)], "tn", F32, "grad_w_proj_a", DIL_OUT_WIDTH, LANES, SEQ,
                                  column_slots=True)
    grads_t["w_proj_b"] = _matmul([(out_b, dyb)], "tn", F32, "grad_w_proj_b", FOX_WIDTH, LANES, SEQ, column_slots=True)
    rs_mid = _ReduceScatter("mid", {k: grads_t[k] for k in mid_names}, pos)

    do_dil, c_dil = _dil_combine_bwd(d_out_a, o_dil, lse_dil, deps=[rs_mid.token])
    rs_mid.start_chips([c_dil])
    dproj, d_cum = _fox_bwd(proj, d_out_b, lse_fox, f_rows, dproj)
    d_cum_rows = jnp.pad(d_cum[:, :2].reshape(N_FOX_HEADS, SEQ), ((0, F_ROWS - N_FOX_HEADS), (0, 0)))
    dproj, db_part = _fox_gate_bwd(d_cum_rows, proj, b_pad, dproj)
    dproj = _dil_bwd(proj, q_dil, k_dil, tables, do_dil, lse_dil, c_dil, dproj, deps=[rs_mid.token])

    dwt_r = _matmul([(dproj, h)], "tn", F32, "grad_w_in", 896, 1024, 2048)
    rs_in = _ReduceScatter("in", {"w_in": _proj_weight_grad_slots(dwt_r)}, pos)
    def finish(rs, after):
        return {k: _adam(shards[k], mine, recv, moments_m[k], moments_v[k], "adam_" + k)
                for k, (mine, recv) in rs.finish(after).items()}

    done = finish(rs_ffn, [rs_in.token])
    rs_in.start_chips([done[k][0] for k in ffn_names])
    grad_x, dg_mix_pre = _input_bwd(dproj, wt_r, x2, dx1, norm_mix_pre, deps=[rs_in.token])
    done.update(finish(rs_mid, [grad_x]))

    small_all = _all_gather([dg_mix_pre, dg_mix_post, dg_ffn_pre, dg_ffn_post, db_part, loss_part],
                            "small_grads_all_gather", deps=[done[k][0] for k in mid_names])
    small, loss = _adam_small(small_all[:5], [norm_mix_pre, norm_mix_post, norm_ffn_pre, norm_ffn_post, b_forget],
                              [m_norm_mix_pre, m_norm_mix_post, m_norm_ffn_pre, m_norm_ffn_post, m_b_forget],
                              [v_norm_mix_pre, v_norm_mix_post, v_norm_ffn_pre, v_norm_ffn_post, v_b_forget],
                              small_all[5])

    done.update(finish(rs_in, [small[0][0]]))

    def leaves(i):
        def nat(k):
            a = done[k][i]
            return (a.T if k in col_sharded else a)[None]

        return [nat("w_in"), nat("w_proj_a"), nat("w_proj_b"), nat("w_out"), small[4][i],
                nat("w_ffn_gate"), nat("w_ffn_up"), nat("w_ffn_down"), *[small[r][i] for r in range(4)]]

    return (loss[0, 0], grad_x[None], *leaves(0), *leaves(1), *leaves(2), *leaves(3))
```
